```python
import math
import jax, jax.numpy as jnp
from jax import lax
import numpy as np

D_MODEL = 1024
BATCH = 8
SEQ = 2048
DEPTH = 1

SC_WIDTH = D_MODEL
SC_GROUPS = 16
SC_KERNEL = 3
SSM_EXPAND = 2
SSM_INNER = SSM_EXPAND * D_MODEL
SSM_HEADDIM = 64
SSM_HEADS = SSM_INNER // SSM_HEADDIM
SSM_GROUPS = 8
SSM_STATE = 128
SSM_CONV = 4
SSM_CHUNK = 128
SSM_CONV_DIM = SSM_INNER + 2 * SSM_GROUPS * SSM_STATE
D_FF = 4 * D_MODEL
EPS = 1e-6

COL_SC = 3 * SC_WIDTH
COL_SSM = SSM_INNER + SSM_CONV_DIM + SSM_HEADS
COL_GATE = 2 * D_MODEL
D_IN_PROJ = COL_SC + COL_SSM + COL_GATE

kernel_name = "hybrid_shortconv_ssd_gated_merge"


def rmsnorm(x, w):
    xf = x.astype(jnp.float32)
    xf = xf * lax.rsqrt(jnp.mean(xf * xf, axis=-1, keepdims=True) + EPS)
    return xf.astype(x.dtype) * w


def causal_depthwise_conv(u, w):
    K = w.shape[0]
    L = u.shape[1]
    up = jnp.pad(u, ((0, 0), (K - 1, 0), (0, 0)))
    y = up[:, 0:L] * w[0]
    for k in range(1, K):
        y = y + up[:, k:k + L] * w[k]
    return y


def ssd_chunked(xh, dt, A, Bg, Cg):
    b, l, h, p = xh.shape
    g, n = Bg.shape[2], Bg.shape[3]
    r = h // g
    q = SSM_CHUNK
    c = l // q
    x = xh.astype(jnp.float32).reshape(b, c, q, g, r, p)
    dt = dt.astype(jnp.float32).reshape(b, c, q, g, r)
    B = Bg.astype(jnp.float32).reshape(b, c, q, g, n)
    C = Cg.astype(jnp.float32).reshape(b, c, q, g, n)
    dA = dt * A.astype(jnp.float32).reshape(g, r)
    dA_cs = jnp.cumsum(dA, axis=2)
    xdt = x * dt[..., None]
    seg = dA_cs[:, :, :, None] - dA_cs[:, :, None, :]
    mask = jnp.tril(jnp.ones((q, q), dtype=bool))[:, :, None, None]
    Lmat = jnp.exp(jnp.where(mask, seg, -jnp.inf))
    CB = jnp.einsum('bcign,bcjgn->bcijg', C, B)
    W = CB[..., None] * Lmat
    y_diag = jnp.einsum('bcijgr,bcjgrp->bcigrp', W, xdt)
    decay = jnp.exp(dA_cs[:, :, -1:] - dA_cs)
    states = jnp.einsum('bcjgn,bcjgr,bcjgrp->bcgrpn', B, decay, xdt)
    chunk_decay = jnp.exp(dA_cs[:, :, -1])

    def step(carry, inp):
        s_c, d_c = inp
        new = carry * d_c[..., None, None] + s_c
        return new, carry

    init = jnp.zeros((b, g, r, p, n), jnp.float32)
    _, prev = lax.scan(step, init, (jnp.moveaxis(states, 1, 0), jnp.moveaxis(chunk_decay, 1, 0)))
    prev = jnp.moveaxis(prev, 0, 1)
    y_off = jnp.einsum('bcign,bcgrpn,bcigr->bcigrp', C, prev, jnp.exp(dA_cs))
    return (y_diag + y_off).reshape(b, l, h, p)


def _fwd_setup_inputs(seed: int = 0) -> dict:
    key = jax.random.key(seed)
    ks = jax.random.split(key, 20)
    f32 = jnp.float32
    nrm = lambda k, shape, fan: jax.random.normal(k, shape, f32) * (fan ** -0.5)
    dt0 = jnp.exp(jax.random.uniform(ks[9], (SSM_HEADS,), f32, math.log(1e-3), math.log(1e-1)))
    return {
        "x": jax.random.normal(ks[0], (BATCH, SEQ, D_MODEL), f32),
        "norm_mix": 1.0 + 0.02 * jax.random.normal(ks[1], (D_MODEL,), f32),
        "w_in": nrm(ks[2], (D_MODEL, D_IN_PROJ), D_MODEL),
        "b_gate": 0.02 * jax.random.normal(ks[3], (COL_GATE,), f32),
        "sc_conv_w": nrm(ks[4], (SC_KERNEL, SC_WIDTH), SC_KERNEL),
        "ssm_conv_w": nrm(ks[5], (SSM_CONV, SSM_CONV_DIM), SSM_CONV),
        "ssm_conv_b": 0.02 * jax.random.normal(ks[6], (SSM_CONV_DIM,), f32),
        "dt_bias": dt0 + jnp.log(-jnp.expm1(-dt0)),
        "A_log": jnp.log(jax.random.uniform(ks[7], (SSM_HEADS,), f32, 1.0, 16.0)),
        "D_skip": 1.0 + 0.02 * jax.random.normal(ks[8], (SSM_HEADS,), f32),
        "ssm_norm_w": 1.0 + 0.02 * jax.random.normal(ks[10], (SSM_INNER,), f32),
        "w_branch_sc": nrm(ks[11], (SC_WIDTH, D_MODEL), SC_WIDTH),
        "w_branch_ssm": nrm(ks[12], (SSM_INNER, D_MODEL), SSM_INNER),
        "w_out": nrm(ks[13], (D_MODEL, D_MODEL), D_MODEL),
        "norm_mlp": 1.0 + 0.02 * jax.random.normal(ks[14], (D_MODEL,), f32),
        "w_mlp1": nrm(ks[15], (D_MODEL, D_FF), D_MODEL),
        "w_mlp2": nrm(ks[16], (D_FF, D_MODEL), D_FF),
        "norm_final": 1.0 + 0.02 * jax.random.normal(ks[17], (D_MODEL,), f32),
    }


def _fwd_reference(x, norm_mix, w_in, b_gate, sc_conv_w, ssm_conv_w, ssm_conv_b, dt_bias, A_log,
              D_skip, ssm_norm_w, w_branch_sc, w_branch_ssm, w_out, norm_mlp, w_mlp1, w_mlp2,
              norm_final):
    b, l, _ = x.shape
    for _layer in range(DEPTH):
        h = rmsnorm(x, norm_mix)
        proj = h @ w_in
        sc_part = proj[..., :COL_SC]
        ssm_part = proj[..., COL_SC:COL_SC + COL_SSM]
        gate_part = proj[..., COL_SC + COL_SSM:] + b_gate

        B_sc = sc_part[..., :SC_WIDTH]
        C_sc = sc_part[..., SC_WIDTH:2 * SC_WIDTH]
        x_sc = sc_part[..., 2 * SC_WIDTH:]
        y_a = B_sc * causal_depthwise_conv(C_sc * x_sc, sc_conv_w)
        br_a = y_a @ w_branch_sc

        z = ssm_part[..., :SSM_INNER]
        xBC = ssm_part[..., SSM_INNER:SSM_INNER + SSM_CONV_DIM]
        dt_raw = ssm_part[..., SSM_INNER + SSM_CONV_DIM:]
        xBC = jax.nn.silu(causal_depthwise_conv(xBC, ssm_conv_w) + ssm_conv_b)
        xs = xBC[..., :SSM_INNER].reshape(b, l, SSM_HEADS, SSM_HEADDIM)
        Bg = xBC[..., SSM_INNER:SSM_INNER + SSM_GROUPS * SSM_STATE].reshape(b, l, SSM_GROUPS, SSM_STATE)
        Cg = xBC[..., SSM_INNER + SSM_GROUPS * SSM_STATE:].reshape(b, l, SSM_GROUPS, SSM_STATE)
        dt = jax.nn.softplus(dt_raw.astype(jnp.float32) + dt_bias.astype(jnp.float32))
        A = -jnp.exp(A_log.astype(jnp.float32))
        y = ssd_chunked(xs, dt, A, Bg, Cg) + D_skip.astype(jnp.float32)[:, None] * xs.astype(jnp.float32)
        y = y.reshape(b, l, SSM_INNER)
        yz = (y * jax.nn.silu(z.astype(jnp.float32))).reshape(b, l, SSM_GROUPS, SSM_INNER // SSM_GROUPS)
        yz = yz * lax.rsqrt(jnp.mean(yz * yz, axis=-1, keepdims=True) + EPS)
        y_b = yz.reshape(b, l, SSM_INNER).astype(x.dtype) * ssm_norm_w
        br_b = y_b @ w_branch_ssm

        g = jax.nn.sigmoid(gate_part)
        merged = g[..., :D_MODEL] * br_a + g[..., D_MODEL:] * br_b
        x = x + merged @ w_out

        h2 = rmsnorm(x, norm_mlp)
        x = x + jnp.square(jax.nn.relu(h2 @ w_mlp1)) @ w_mlp2
    return rmsnorm(x, norm_final)


import jax as _jax
import jax.numpy as _jnp

TWIN_FORMAT = 'train_step'
FWD_PARAMS = ['x', 'norm_mix', 'w_in', 'b_gate', 'sc_conv_w', 'ssm_conv_w', 'ssm_conv_b', 'dt_bias', 'A_log', 'D_skip', 'ssm_norm_w', 'w_branch_sc', 'w_branch_ssm', 'w_out', 'norm_mlp', 'w_mlp1', 'w_mlp2', 'norm_final']
TWIN_WEIGHTS = ['norm_mix', 'w_in', 'b_gate', 'sc_conv_w', 'ssm_conv_w', 'ssm_conv_b', 'dt_bias', 'A_log', 'D_skip', 'ssm_norm_w', 'w_branch_sc', 'w_branch_ssm', 'w_out', 'norm_mlp', 'w_mlp1', 'w_mlp2', 'norm_final']
TWIN_DIFF_INPUT = 'x'
TWIN_INPUTS = ['x', 'norm_mix', 'w_in', 'b_gate', 'sc_conv_w', 'ssm_conv_w', 'ssm_conv_b', 'dt_bias', 'A_log', 'D_skip', 'ssm_norm_w', 'w_branch_sc', 'w_branch_ssm', 'w_out', 'norm_mlp', 'w_mlp1', 'w_mlp2', 'norm_final', 'loss_target', 'm_norm_mix', 'm_w_in', 'm_b_gate', 'm_sc_conv_w', 'm_ssm_conv_w', 'm_ssm_conv_b', 'm_dt_bias', 'm_A_log', 'm_D_skip', 'm_ssm_norm_w', 'm_w_branch_sc', 'm_w_branch_ssm', 'm_w_out', 'm_norm_mlp', 'm_w_mlp1', 'm_w_mlp2', 'm_norm_final', 'v_norm_mix', 'v_w_in', 'v_b_gate', 'v_sc_conv_w', 'v_ssm_conv_w', 'v_ssm_conv_b', 'v_dt_bias', 'v_A_log', 'v_D_skip', 'v_ssm_norm_w', 'v_w_branch_sc', 'v_w_branch_ssm', 'v_w_out', 'v_norm_mlp', 'v_w_mlp1', 'v_w_mlp2', 'v_norm_final']
TWIN_OUTPUTS = ['loss', 'grad_x', 'grad_norm_mix', 'grad_w_in', 'grad_b_gate', 'grad_sc_conv_w', 'grad_ssm_conv_w', 'grad_ssm_conv_b', 'grad_dt_bias', 'grad_A_log', 'grad_D_skip', 'grad_ssm_norm_w', 'grad_w_branch_sc', 'grad_w_branch_ssm', 'grad_w_out', 'grad_norm_mlp', 'grad_w_mlp1', 'grad_w_mlp2', 'grad_norm_final', 'delta_norm_mix', 'delta_w_in', 'delta_b_gate', 'delta_sc_conv_w', 'delta_ssm_conv_w', 'delta_ssm_conv_b', 'delta_dt_bias', 'delta_A_log', 'delta_D_skip', 'delta_ssm_norm_w', 'delta_w_branch_sc', 'delta_w_branch_ssm', 'delta_w_out', 'delta_norm_mlp', 'delta_w_mlp1', 'delta_w_mlp2', 'delta_norm_final', 'new_m_norm_mix', 'new_m_w_in', 'new_m_b_gate', 'new_m_sc_conv_w', 'new_m_ssm_conv_w', 'new_m_ssm_conv_b', 'new_m_dt_bias', 'new_m_A_log', 'new_m_D_skip', 'new_m_ssm_norm_w', 'new_m_w_branch_sc', 'new_m_w_branch_ssm', 'new_m_w_out', 'new_m_norm_mlp', 'new_m_w_mlp1', 'new_m_w_mlp2', 'new_m_norm_final', 'new_v_norm_mix', 'new_v_w_in', 'new_v_b_gate', 'new_v_sc_conv_w', 'new_v_ssm_conv_w', 'new_v_ssm_conv_b', 'new_v_dt_bias', 'new_v_A_log', 'new_v_D_skip', 'new_v_ssm_norm_w', 'new_v_w_branch_sc', 'new_v_w_branch_ssm', 'new_v_w_out', 'new_v_norm_mlp', 'new_v_w_mlp1', 'new_v_w_mlp2', 'new_v_norm_final']
TWIN_LEAF_KINDS = {'loss': 'loss', 'grad_x': 'grad_x', 'grad_norm_mix': 'grad_w', 'grad_w_in': 'grad_w', 'grad_b_gate': 'grad_w', 'grad_sc_conv_w': 'grad_w', 'grad_ssm_conv_w': 'grad_w', 'grad_ssm_conv_b': 'grad_w', 'grad_dt_bias': 'grad_w', 'grad_A_log': 'grad_w', 'grad_D_skip': 'grad_w', 'grad_ssm_norm_w': 'grad_w', 'grad_w_branch_sc': 'grad_w', 'grad_w_branch_ssm': 'grad_w', 'grad_w_out': 'grad_w', 'grad_norm_mlp': 'grad_w', 'grad_w_mlp1': 'grad_w', 'grad_w_mlp2': 'grad_w', 'grad_norm_final': 'grad_w', 'delta_norm_mix': 'delta_w', 'delta_w_in': 'delta_w', 'delta_b_gate': 'delta_w', 'delta_sc_conv_w': 'delta_w', 'delta_ssm_conv_w': 'delta_w', 'delta_ssm_conv_b': 'delta_w', 'delta_dt_bias': 'delta_w', 'delta_A_log': 'delta_w', 'delta_D_skip': 'delta_w', 'delta_ssm_norm_w': 'delta_w', 'delta_w_branch_sc': 'delta_w', 'delta_w_branch_ssm': 'delta_w', 'delta_w_out': 'delta_w', 'delta_norm_mlp': 'delta_w', 'delta_w_mlp1': 'delta_w', 'delta_w_mlp2': 'delta_w', 'delta_norm_final': 'delta_w', 'new_m_norm_mix': 'new_m', 'new_m_w_in': 'new_m', 'new_m_b_gate': 'new_m', 'new_m_sc_conv_w': 'new_m', 'new_m_ssm_conv_w': 'new_m', 'new_m_ssm_conv_b': 'new_m', 'new_m_dt_bias': 'new_m', 'new_m_A_log': 'new_m', 'new_m_D_skip': 'new_m', 'new_m_ssm_norm_w': 'new_m', 'new_m_w_branch_sc': 'new_m', 'new_m_w_branch_ssm': 'new_m', 'new_m_w_out': 'new_m', 'new_m_norm_mlp': 'new_m', 'new_m_w_mlp1': 'new_m', 'new_m_w_mlp2': 'new_m', 'new_m_norm_final': 'new_m', 'new_v_norm_mix': 'new_v', 'new_v_w_in': 'new_v', 'new_v_b_gate': 'new_v', 'new_v_sc_conv_w': 'new_v', 'new_v_ssm_conv_w': 'new_v', 'new_v_ssm_conv_b': 'new_v', 'new_v_dt_bias': 'new_v', 'new_v_A_log': 'new_v', 'new_v_D_skip': 'new_v', 'new_v_ssm_norm_w': 'new_v', 'new_v_w_branch_sc': 'new_v', 'new_v_w_branch_ssm': 'new_v', 'new_v_w_out': 'new_v', 'new_v_norm_mlp': 'new_v', 'new_v_w_mlp1': 'new_v', 'new_v_w_mlp2': 'new_v', 'new_v_norm_final': 'new_v'}


def _forward(args):
    return _fwd_reference(*[args[k] for k in FWD_PARAMS])


def _output_shape():
    out = _jax.eval_shape(lambda: _forward(_fwd_setup_inputs(0)))
    return out.shape, out.dtype

N_MICROBATCH = 1
ADAM_LR = 0.001
ADAM_B1 = 0.9
ADAM_B2 = 0.999
ADAM_EPS = 1e-08
ADAM_WD = 0.01
ADAM_STEP = 10
PER_EXAMPLE_BATCH_AXIS = {'x': 0, 'loss_target': 0}
SHARED_INPUTS = []
_WEIGHT_DTYPES = {'norm_mix': _jnp.float32, 'w_in': _jnp.float32, 'b_gate': _jnp.float32, 'sc_conv_w': _jnp.float32, 'ssm_conv_w': _jnp.float32, 'ssm_conv_b': _jnp.float32, 'dt_bias': _jnp.float32, 'A_log': _jnp.float32, 'D_skip': _jnp.float32, 'ssm_norm_w': _jnp.float32, 'w_branch_sc': _jnp.float32, 'w_branch_ssm': _jnp.float32, 'w_out': _jnp.float32, 'norm_mlp': _jnp.float32, 'w_mlp1': _jnp.float32, 'w_mlp2': _jnp.float32, 'norm_final': _jnp.float32}
MOMENT_SCALE = {'norm_mix': 1.379198e-01, 'w_in': 4.125096e-02, 'b_gate': 2.350433e-02, 'sc_conv_w': 5.879504e-02, 'ssm_conv_w': 3.118424e-02, 'ssm_conv_b': 4.572778e-02, 'dt_bias': 1.094738e-01, 'A_log': 2.255218e-01, 'D_skip': 2.154611e-01, 'ssm_norm_w': 4.275816e-02, 'w_branch_sc': 5.800095e-02, 'w_branch_ssm': 5.846757e-02, 'w_out': 8.246393e-02, 'norm_mlp': 1.045239e-01, 'w_mlp1': 5.042800e-02, 'w_mlp2': 9.472441e-02, 'norm_final': 1.613928e+01}


def _to_microbatches(a, axis):
    t = _jnp.moveaxis(a, axis, 0)
    t = t.reshape((N_MICROBATCH, t.shape[0] // N_MICROBATCH) + t.shape[1:])
    return _jnp.moveaxis(t, 1, axis + 1)


def setup_inputs(seed: int = 0) -> dict:
    inp = _fwd_setup_inputs(seed)
    key = _jax.random.fold_in(_jax.random.key(seed), 7919)
    shape, _ = _output_shape()
    out = dict(inp)
    out["loss_target"] = _jax.random.normal(_jax.random.fold_in(key, 0), shape, _jnp.float32)
    for i, name in enumerate(TWIN_WEIGHTS):
        w = inp[name].astype(_jnp.float32)
        if MOMENT_SCALE is None:
            s = _jnp.sqrt(_jnp.mean(_jnp.square(w)) + 1e-30)
        else:
            s = MOMENT_SCALE[name]
        km, kv = _jax.random.split(_jax.random.fold_in(key, i + 1))
        out[name] = w
        out["m_" + name] = s * _jax.random.normal(km, w.shape, _jnp.float32)
        out["v_" + name] = (s * s) * _jax.random.uniform(kv, w.shape, _jnp.float32, 0.5, 1.5)
    if N_MICROBATCH > 1:
        for name, axis in PER_EXAMPLE_BATCH_AXIS.items():
            out[name] = _to_microbatches(out[name], axis)
    return {'x': out['x'], 'norm_mix': out['norm_mix'], 'w_in': out['w_in'], 'b_gate': out['b_gate'], 'sc_conv_w': out['sc_conv_w'], 'ssm_conv_w': out['ssm_conv_w'], 'ssm_conv_b': out['ssm_conv_b'], 'dt_bias': out['dt_bias'], 'A_log': out['A_log'], 'D_skip': out['D_skip'], 'ssm_norm_w': out['ssm_norm_w'], 'w_branch_sc': out['w_branch_sc'], 'w_branch_ssm': out['w_branch_ssm'], 'w_out': out['w_out'], 'norm_mlp': out['norm_mlp'], 'w_mlp1': out['w_mlp1'], 'w_mlp2': out['w_mlp2'], 'norm_final': out['norm_final'], 'loss_target': out['loss_target'], 'm_norm_mix': out['m_norm_mix'], 'm_w_in': out['m_w_in'], 'm_b_gate': out['m_b_gate'], 'm_sc_conv_w': out['m_sc_conv_w'], 'm_ssm_conv_w': out['m_ssm_conv_w'], 'm_ssm_conv_b': out['m_ssm_conv_b'], 'm_dt_bias': out['m_dt_bias'], 'm_A_log': out['m_A_log'], 'm_D_skip': out['m_D_skip'], 'm_ssm_norm_w': out['m_ssm_norm_w'], 'm_w_branch_sc': out['m_w_branch_sc'], 'm_w_branch_ssm': out['m_w_branch_ssm'], 'm_w_out': out['m_w_out'], 'm_norm_mlp': out['m_norm_mlp'], 'm_w_mlp1': out['m_w_mlp1'], 'm_w_mlp2': out['m_w_mlp2'], 'm_norm_final': out['m_norm_final'], 'v_norm_mix': out['v_norm_mix'], 'v_w_in': out['v_w_in'], 'v_b_gate': out['v_b_gate'], 'v_sc_conv_w': out['v_sc_conv_w'], 'v_ssm_conv_w': out['v_ssm_conv_w'], 'v_ssm_conv_b': out['v_ssm_conv_b'], 'v_dt_bias': out['v_dt_bias'], 'v_A_log': out['v_A_log'], 'v_D_skip': out['v_D_skip'], 'v_ssm_norm_w': out['v_ssm_norm_w'], 'v_w_branch_sc': out['v_w_branch_sc'], 'v_w_branch_ssm': out['v_w_branch_ssm'], 'v_w_out': out['v_w_out'], 'v_norm_mlp': out['v_norm_mlp'], 'v_w_mlp1': out['v_w_mlp1'], 'v_w_mlp2': out['v_w_mlp2'], 'v_norm_final': out['v_norm_final']}


def _loss(weights, diff, rest, loss_target):
    with _jax.named_scope("forward"):
        args = {**rest, TWIN_DIFF_INPUT: diff, **{k: w.astype(_WEIGHT_DTYPES[k]) for k, w in weights.items()}}
        y = _forward(args)
    with _jax.named_scope("loss_head"):
        err = _jnp.square(y.astype(_jnp.float32) - loss_target)
        return 0.5 * _jnp.sum(_jnp.mean(err, axis=-1)) if err.ndim else 0.5 * err


def _adamw(w, g, m, v):
    m = ADAM_B1 * m + (1.0 - ADAM_B1) * g
    v = ADAM_B2 * v + (1.0 - ADAM_B2) * _jnp.square(g)
    m_hat = m / (1.0 - ADAM_B1 ** ADAM_STEP)
    v_hat = v / (1.0 - ADAM_B2 ** ADAM_STEP)
    delta = -ADAM_LR * (m_hat / (_jnp.sqrt(v_hat) + ADAM_EPS) + ADAM_WD * w)
    return delta, m, v


def reference(x, norm_mix, w_in, b_gate, sc_conv_w, ssm_conv_w, ssm_conv_b, dt_bias, A_log, D_skip, ssm_norm_w, w_branch_sc, w_branch_ssm, w_out, norm_mlp, w_mlp1, w_mlp2, norm_final, loss_target, m_norm_mix, m_w_in, m_b_gate, m_sc_conv_w, m_ssm_conv_w, m_ssm_conv_b, m_dt_bias, m_A_log, m_D_skip, m_ssm_norm_w, m_w_branch_sc, m_w_branch_ssm, m_w_out, m_norm_mlp, m_w_mlp1, m_w_mlp2, m_norm_final, v_norm_mix, v_w_in, v_b_gate, v_sc_conv_w, v_ssm_conv_w, v_ssm_conv_b, v_dt_bias, v_A_log, v_D_skip, v_ssm_norm_w, v_w_branch_sc, v_w_branch_ssm, v_w_out, v_norm_mlp, v_w_mlp1, v_w_mlp2, v_norm_final):
    given = dict(x=x, norm_mix=norm_mix, w_in=w_in, b_gate=b_gate, sc_conv_w=sc_conv_w, ssm_conv_w=ssm_conv_w, ssm_conv_b=ssm_conv_b, dt_bias=dt_bias, A_log=A_log, D_skip=D_skip, ssm_norm_w=ssm_norm_w, w_branch_sc=w_branch_sc, w_branch_ssm=w_branch_ssm, w_out=w_out, norm_mlp=norm_mlp, w_mlp1=w_mlp1, w_mlp2=w_mlp2, norm_final=norm_final, loss_target=loss_target, m_norm_mix=m_norm_mix, m_w_in=m_w_in, m_b_gate=m_b_gate, m_sc_conv_w=m_sc_conv_w, m_ssm_conv_w=m_ssm_conv_w, m_ssm_conv_b=m_ssm_conv_b, m_dt_bias=m_dt_bias, m_A_log=m_A_log, m_D_skip=m_D_skip, m_ssm_norm_w=m_ssm_norm_w, m_w_branch_sc=m_w_branch_sc, m_w_branch_ssm=m_w_branch_ssm, m_w_out=m_w_out, m_norm_mlp=m_norm_mlp, m_w_mlp1=m_w_mlp1, m_w_mlp2=m_w_mlp2, m_norm_final=m_norm_final, v_norm_mix=v_norm_mix, v_w_in=v_w_in, v_b_gate=v_b_gate, v_sc_conv_w=v_sc_conv_w, v_ssm_conv_w=v_ssm_conv_w, v_ssm_conv_b=v_ssm_conv_b, v_dt_bias=v_dt_bias, v_A_log=v_A_log, v_D_skip=v_D_skip, v_ssm_norm_w=v_ssm_norm_w, v_w_branch_sc=v_w_branch_sc, v_w_branch_ssm=v_w_branch_ssm, v_w_out=v_w_out, v_norm_mlp=v_norm_mlp, v_w_mlp1=v_w_mlp1, v_w_mlp2=v_w_mlp2, v_norm_final=v_norm_final)
    weights = {n: given[n] for n in TWIN_WEIGHTS}
    shared = {n: given[n] for n in SHARED_INPUTS}
    per_example = {n: given[n] for n in ['x']}
    grad_fn = _jax.value_and_grad(_loss, argnums=(0, 1))

    def one_microbatch(ex, loss_target):
        ex = dict(ex)
        diff = ex.pop(TWIN_DIFF_INPUT)
        return grad_fn(weights, diff, {**shared, **ex}, loss_target)

    if N_MICROBATCH == 1:
        loss, (grad_w, grad_x) = one_microbatch(per_example, given["loss_target"])
    else:
        def body(carry, xs):
            loss_sum, grad_sum = carry
            l_k, (gw_k, gx_k) = one_microbatch(xs[0], xs[1])
            with _jax.named_scope("update"):
                return (loss_sum + l_k, _jax.tree.map(_jnp.add, grad_sum, gw_k)), gx_k

        init = (_jnp.zeros((), _jnp.float32), _jax.tree.map(_jnp.zeros_like, weights))
        (loss, grad_w), grad_x = _jax.lax.scan(body, init, (per_example, given["loss_target"]))
    with _jax.named_scope("update"):
        delta_w, new_m, new_v = {}, {}, {}
        for n in TWIN_WEIGHTS:
            delta_w[n], new_m[n], new_v[n] = _adamw(weights[n], grad_w[n], given["m_" + n], given["v_" + n])
    return (loss, grad_x, *[grad_w[n] for n in TWIN_WEIGHTS], *[delta_w[n] for n in TWIN_WEIGHTS],
            *[new_m[n] for n in TWIN_WEIGHTS], *[new_v[n] for n in TWIN_WEIGHTS])
```

```python
import functools

import jax
import jax.numpy as jnp
from jax import lax
from jax.experimental import pallas as pl
from jax.experimental.pallas import tpu as pltpu

F32 = jnp.float32
BF16 = jnp.bfloat16

EPS = 1e-6
N_DEV = 8
HEADDIM = 64
NSTATE = 128
CHUNK = 128
NGROUPS = 8
GROUP_W = 256
SC_K = 3
SSM_K = 4
LANES = 128

ADAM_LR = 0.001
ADAM_B1 = 0.9
ADAM_B2 = 0.999
ADAM_EPS = 1e-08
ADAM_WD = 0.01
ADAM_STEP = 10

NN = (((1,), (0,)), ((), ()))
NT = (((1,), (1,)), ((), ()))
TN = (((0,), (0,)), ((), ()))
_DIMS = {"nn": NN, "nt": NT, "tn": TN}

ANY = pl.BlockSpec(memory_space=pl.ANY)
MESH = pl.DeviceIdType.MESH


def _sds(shape, dtype):
    return jax.ShapeDtypeStruct(tuple(shape), dtype)


def _dot(a, b, dims=NN):
    return lax.dot_general(a, b, dims, preferred_element_type=F32)


def _params(*sem):
    return pltpu.CompilerParams(dimension_semantics=tuple(sem))


def _mm(a, b, *, mode, name, extras=(), epi=None, out_dtypes=(F32,), tm=512, tn=512):
    if mode == "nn":
        (M, K), N = a.shape, b.shape[1]
    elif mode == "nt":
        (M, K), N = a.shape, b.shape[0]
    else:
        (K, M), N = a.shape, b.shape[1]
    tm, tn = min(tm, M), min(tn, N)
    assert M % tm == 0 and N % tn == 0
    a_spec = pl.BlockSpec((K, tm), lambda i, j: (0, i)) if mode == "tn" else pl.BlockSpec((tm, K), lambda i, j: (i, 0))
    b_spec = pl.BlockSpec((tn, K), lambda i, j: (j, 0)) if mode == "nt" else pl.BlockSpec((K, tn), lambda i, j: (0, j))
    mn_spec = pl.BlockSpec((tm, tn), lambda i, j: (i, j))
    n_ex = len(extras)
    dims = _DIMS[mode]

    def body(a_ref, b_ref, *rest):
        acc = _dot(a_ref[...], b_ref[...], dims)
        res = (acc,) if epi is None else epi(acc, *[r[...] for r in rest[:n_ex]])
        for o_ref, r in zip(rest[n_ex:], res):
            o_ref[...] = r.astype(o_ref.dtype)

    outs = pl.pallas_call(
        body, grid=(M // tm, N // tn), in_specs=[a_spec, b_spec] + [mn_spec] * n_ex,
        out_specs=[mn_spec] * len(out_dtypes), out_shape=[_sds((M, N), d) for d in out_dtypes],
        name=name, compiler_params=_params("parallel", "parallel"),
    )(a, b, *extras)
    return outs[0] if len(outs) == 1 else outs


def _epi_add(acc, r):
    return (acc + r,)


def _epi_add2(acc, r):
    s = acc + r
    return (s, s)


def _epi_relu2(acc):
    p = jnp.maximum(acc, 0.0)
    return (acc, p * p)


def _epi_relu2_bwd(acc, a):
    return (acc * (2.0 * jnp.maximum(a, 0.0)),)


def _row(tr, n):
    return pl.BlockSpec((tr, n), lambda i: (i, 0))


def _vec(n):
    return pl.BlockSpec((1, n), lambda i: (0, 0))


def _rms_fwd(x, w, name):
    T, D = x.shape
    tr = min(256, T)

    def body(x_ref, w_ref, o_ref):
        xv = x_ref[...]
        r = lax.rsqrt(jnp.mean(xv * xv, axis=-1, keepdims=True) + EPS)
        o_ref[...] = (xv * r * w_ref[...]).astype(BF16)

    return pl.pallas_call(body, grid=(T // tr,), in_specs=[_row(tr, D), _vec(D)], out_specs=_row(tr, D),
                          out_shape=_sds((T, D), BF16), name=name, compiler_params=_params("parallel"))(x, w)


def _rms_bwd(x, w, dh, dres, name):
    T, D = x.shape
    tr = min(256, T)

    def body(x_ref, w_ref, dh_ref, dres_ref, dx_ref, dxb_ref, dw_ref):
        @pl.when(pl.program_id(0) == 0)
        def _():
            dw_ref[...] = jnp.zeros_like(dw_ref)

        xv = x_ref[...]
        r = lax.rsqrt(jnp.mean(xv * xv, axis=-1, keepdims=True) + EPS)
        xh = xv * r
        dh_v = dh_ref[...]
        dw_ref[...] += jnp.sum(dh_v * xh, axis=0, keepdims=True)
        dxh = dh_v * w_ref[...]
        dx = r * (dxh - xh * jnp.mean(dxh * xh, axis=-1, keepdims=True)) + dres_ref[...]
        dx_ref[...] = dx
        dxb_ref[...] = dx.astype(BF16)

    return pl.pallas_call(
        body, grid=(T // tr,), in_specs=[_row(tr, D), _vec(D), _row(tr, D), _row(tr, D)],
        out_specs=[_row(tr, D), _row(tr, D), _vec(D)],
        out_shape=[_sds((T, D), F32), _sds((T, D), BF16), _sds((1, D), F32)],
        name=name, compiler_params=_params("arbitrary"))(x, w, dh, dres)


def _final(x2, w, tgt, name):
    T, D = x2.shape
    tr = min(256, T)

    def body(x_ref, w_ref, t_ref, dx_ref, dxb_ref, dw_ref, loss_ref):
        @pl.when(pl.program_id(0) == 0)
        def _():
            dw_ref[...] = jnp.zeros_like(dw_ref)
            loss_ref[...] = jnp.zeros_like(loss_ref)

        xv = x_ref[...]
        wv = w_ref[...]
        r = lax.rsqrt(jnp.mean(xv * xv, axis=-1, keepdims=True) + EPS)
        xh = xv * r
        err = xh * wv - t_ref[...]
        part = jnp.sum(jnp.sum(err * err, axis=1, keepdims=True), axis=0, keepdims=True) * (0.5 / D)
        loss_ref[...] += jnp.broadcast_to(part, loss_ref.shape)
        dy = err * (1.0 / D)
        dw_ref[...] += jnp.sum(dy * xh, axis=0, keepdims=True)
        dxh = dy * wv
        dx = r * (dxh - xh * jnp.mean(dxh * xh, axis=-1, keepdims=True))
        dx_ref[...] = dx
        dxb_ref[...] = dx.astype(BF16)

    return pl.pallas_call(
        body, grid=(T // tr,), in_specs=[_row(tr, D), _vec(D), _row(tr, D)],
        out_specs=[_row(tr, D), _row(tr, D), _vec(D), _vec(LANES)],
        out_shape=[_sds((T, D), F32), _sds((T, D), BF16), _sds((1, D), F32), _sds((1, LANES), F32)],
        name=name, compiler_params=_params("arbitrary"))(x2, w, tgt)


def _silu_parts(z):
    s = jax.nn.sigmoid(z)
    return z * s, s * (1.0 + z * (1.0 - s))


def _gnorm_fwd(y, z, w, name):
    T, N = y.shape
    tr = min(256, T)

    def body(y_ref, z_ref, w_ref, o_ref):
        for g in range(N // GROUP_W):
            sl = slice(g * GROUP_W, (g + 1) * GROUP_W)
            silu, _ = _silu_parts(z_ref[:, sl])
            yz = y_ref[:, sl] * silu
            r = lax.rsqrt(jnp.mean(yz * yz, axis=-1, keepdims=True) + EPS)
            o_ref[:, sl] = (yz * r * w_ref[:, sl]).astype(BF16)

    return pl.pallas_call(body, grid=(T // tr,), in_specs=[_row(tr, N), _row(tr, N), _vec(N)], out_specs=_row(tr, N),
                          out_shape=_sds((T, N), BF16), name=name, compiler_params=_params("parallel"))(y, z, w)


def _gnorm_bwd(y, z, w, dyb, name):
    T, N = y.shape
    tr = min(256, T)

    def body(y_ref, z_ref, w_ref, d_ref, dy_ref, dz_ref, dw_ref):
        @pl.when(pl.program_id(0) == 0)
        def _():
            dw_ref[...] = jnp.zeros_like(dw_ref)

        for g in range(N // GROUP_W):
            sl = slice(g * GROUP_W, (g + 1) * GROUP_W)
            yv = y_ref[:, sl]
            silu, dsilu = _silu_parts(z_ref[:, sl])
            yz = yv * silu
            r = lax.rsqrt(jnp.mean(yz * yz, axis=-1, keepdims=True) + EPS)
            yzh = yz * r
            d = d_ref[:, sl]
            dw_ref[:, sl] += jnp.sum(d * yzh, axis=0, keepdims=True)
            dyzh = d * w_ref[:, sl]
            dyz = r * (dyzh - yzh * jnp.mean(dyzh * yzh, axis=-1, keepdims=True))
            dy_ref[:, sl] = dyz * silu
            dz_ref[:, sl] = (dyz * yv * dsilu).astype(BF16)

    return pl.pallas_call(
        body, grid=(T // tr,), in_specs=[_row(tr, N), _row(tr, N), _vec(N), _row(tr, N)],
        out_specs=[_row(tr, N), _row(tr, N), _vec(N)],
        out_shape=[_sds((T, N), F32), _sds((T, N), BF16), _sds((1, N), F32)],
        name=name, compiler_params=_params("arbitrary"))(y, z, w, dyb)


def _merge_fwd(gate_raw, b_gate, br_a, br_b, name):
    T, D = br_a.shape
    tr = min(256, T)

    def body(g_ref, bg_ref, a_ref, b_ref, o_ref):
        g = jax.nn.sigmoid(g_ref[...] + bg_ref[...])
        o_ref[...] = (g[:, :D] * a_ref[...] + g[:, D:] * b_ref[...]).astype(BF16)

    return pl.pallas_call(body, grid=(T // tr,), in_specs=[_row(tr, 2 * D), _vec(2 * D), _row(tr, D), _row(tr, D)],
                          out_specs=_row(tr, D), out_shape=_sds((T, D), BF16), name=name,
                          compiler_params=_params("parallel"))(gate_raw, b_gate, br_a, br_b)


def _merge_bwd(dmerged, gate_raw, b_gate, br_a, br_b, name):
    T, D = br_a.shape
    tr = min(256, T)

    def body(d_ref, g_ref, bg_ref, a_ref, b_ref, da_ref, db_ref, dg_ref, dbg_ref):
        @pl.when(pl.program_id(0) == 0)
        def _():
            dbg_ref[...] = jnp.zeros_like(dbg_ref)

        g = jax.nn.sigmoid(g_ref[...] + bg_ref[...])
        d = d_ref[...]
        da_ref[...] = (d * g[:, :D]).astype(BF16)
        db_ref[...] = (d * g[:, D:]).astype(BF16)
        dg = jnp.concatenate([d * a_ref[...], d * b_ref[...]], axis=1) * g * (1.0 - g)
        dg_ref[...] = dg.astype(BF16)
        dbg_ref[...] += jnp.sum(dg, axis=0, keepdims=True)

    return pl.pallas_call(
        body, grid=(T // tr,), in_specs=[_row(tr, D), _row(tr, 2 * D), _vec(2 * D), _row(tr, D), _row(tr, D)],
        out_specs=[_row(tr, D), _row(tr, D), _row(tr, 2 * D), _vec(2 * D)],
        out_shape=[_sds((T, D), BF16), _sds((T, D), BF16), _sds((T, 2 * D), BF16), _sds((1, 2 * D), F32)],
        name=name, compiler_params=_params("arbitrary"))(dmerged, gate_raw, b_gate, br_a, br_b)


def _shift_down(u, s):
    if s == 0:
        return u
    row = lax.broadcasted_iota(jnp.int32, u.shape, 0)
    return jnp.where(row >= s, pltpu.roll(u, s, 0), 0.0)


def _shift_up(u, s):
    if s == 0:
        return u
    n = u.shape[0]
    row = lax.broadcasted_iota(jnp.int32, u.shape, 0)
    return jnp.where(row < n - s, pltpu.roll(u, n - s, 0), 0.0)


def _conv(u, w_ref, K):
    acc = u * w_ref[K - 1:K, :]
    for k in range(K - 1):
        acc = acc + _shift_down(u, K - 1 - k) * w_ref[k:k + 1, :]
    return acc


def _conv_bwd(u, dc, w_ref, dw_ref, K):
    du = dc * w_ref[K - 1:K, :]
    dw_ref[K - 1:K, :] = jnp.sum(dc * u, axis=0, keepdims=True)
    for k in range(K - 1):
        s = K - 1 - k
        dw_ref[k:k + 1, :] = jnp.sum(dc * _shift_down(u, s), axis=0, keepdims=True)
        du = du + _shift_up(dc, s) * w_ref[k:k + 1, :]
    return du


CB_W = 256


def _col(T, j0=0):
    return pl.BlockSpec((T, CB_W), lambda j: (0, j + j0))


def _sc_fwd(psc, w, name):
    T, D = psc.shape[0], psc.shape[1] // 3
    nb = D // CB_W

    def body(b_ref, c_ref, x_ref, w_ref, o_ref):
        o_ref[...] = (b_ref[...] * _conv(c_ref[...] * x_ref[...], w_ref, SC_K)).astype(BF16)

    return pl.pallas_call(
        body, grid=(nb,), in_specs=[_col(T), _col(T, nb), _col(T, 2 * nb), pl.BlockSpec((SC_K, CB_W), lambda j: (0, j))],
        out_specs=_col(T), out_shape=_sds((T, D), BF16), name=name, compiler_params=_params("parallel"))(psc, psc, psc, w)


def _sc_bwd(psc, w, dya, name):
    T, D = psc.shape[0], psc.shape[1] // 3
    nb = D // CB_W

    def body(b_ref, c_ref, x_ref, w_ref, d_ref, db_ref, dc_ref, dx_ref, dw_ref):
        cv, xv, d = c_ref[...], x_ref[...], d_ref[...]
        u = cv * xv
        db_ref[...] = (d * _conv(u, w_ref, SC_K)).astype(BF16)
        du = _conv_bwd(u, d * b_ref[...], w_ref, dw_ref, SC_K)
        dc_ref[...] = (du * xv).astype(BF16)
        dx_ref[...] = (du * cv).astype(BF16)

    wspec = pl.BlockSpec((SC_K, CB_W), lambda j: (0, j))
    return pl.pallas_call(
        body, grid=(nb,), in_specs=[_col(T), _col(T, nb), _col(T, 2 * nb), wspec, _col(T)],
        out_specs=[_col(T), _col(T), _col(T), wspec],
        out_shape=[_sds((T, D), BF16)] * 3 + [_sds((SC_K, D), F32)],
        name=name, compiler_params=_params("parallel"))(psc, psc, psc, w, dya)


def _ssm_conv_fwd(u, w, b, name):
    T, N = u.shape

    def body(u_ref, w_ref, b_ref, o_ref):
        c = _conv(u_ref[...], w_ref, SSM_K) + b_ref[...]
        o_ref[...] = c * jax.nn.sigmoid(c)

    return pl.pallas_call(
        body, grid=(N // CB_W,), in_specs=[_col(T), pl.BlockSpec((SSM_K, CB_W), lambda j: (0, j)), pl.BlockSpec((1, CB_W), lambda j: (0, j))],
        out_specs=_col(T), out_shape=_sds((T, N), F32), name=name, compiler_params=_params("parallel"))(u, w, b)


def _ssm_conv_bwd(u, w, b, dxs, dB, dC, name):
    T, N = u.shape
    n_x, n_b = dxs.shape[1] // CB_W, dB.shape[1] // CB_W

    def body(u_ref, w_ref, b_ref, dx_ref, db_ref, dc_ref, du_ref, dw_ref, dbias_ref):
        j = pl.program_id(0)
        uv = u_ref[...]
        c = _conv(uv, w_ref, SSM_K) + b_ref[...]
        _, dsilu = _silu_parts(c)
        d = jnp.where(j < n_x, dx_ref[...], jnp.where(j < n_x + n_b, db_ref[...], dc_ref[...])) * dsilu
        dbias_ref[...] = jnp.sum(d, axis=0, keepdims=True)
        du_ref[...] = _conv_bwd(uv, d, w_ref, dw_ref, SSM_K).astype(BF16)

    wspec = pl.BlockSpec((SSM_K, CB_W), lambda j: (0, j))
    bspec = pl.BlockSpec((1, CB_W), lambda j: (0, j))
    return pl.pallas_call(
        body, grid=(N // CB_W,),
        in_specs=[_col(T), wspec, bspec,
                  pl.BlockSpec((T, CB_W), lambda j: (0, jnp.minimum(j, n_x - 1))),
                  pl.BlockSpec((T, CB_W), lambda j: (0, jnp.clip(j - n_x, 0, n_b - 1))),
                  pl.BlockSpec((T, CB_W), lambda j: (0, jnp.clip(j - n_x - n_b, 0, n_b - 1)))],
        out_specs=[_col(T), wspec, bspec],
        out_shape=[_sds((T, N), BF16), _sds((SSM_K, N), F32), _sds((1, N), F32)],
        name=name, compiler_params=_params("parallel"))(u, w, b, dxs, dB, dC)


def _split3(v):
    hi = v.astype(BF16)
    r = v - hi.astype(F32)
    mid = r.astype(BF16)
    lo = (r - mid.astype(F32)).astype(BF16)
    return hi, mid, lo


def _head_expand(n_lanes):
    h = lax.broadcasted_iota(jnp.int32, (LANES, n_lanes), 0)
    l = lax.broadcasted_iota(jnp.int32, (LANES, n_lanes), 1)
    return (jnp.right_shift(l, HEADDIM.bit_length() - 1) == h).astype(BF16)


def _softplus(v):
    return jnp.maximum(v, 0.0) + jnp.log1p(jnp.exp(-jnp.abs(v)))


def _ssd_prep(dt_raw, dt_bias, a_log, n_inner, name):
    T = dt_raw.shape[0]

    def body(r_ref, b_ref, al_ref, dt_ref, cs_ref):
        dt = _softplus(r_ref[...] + b_ref[...])
        a = dt * (-jnp.exp(al_ref[...]))
        i = lax.broadcasted_iota(jnp.int32, (CHUNK, CHUNK), 0)
        j = lax.broadcasted_iota(jnp.int32, (CHUNK, CHUNK), 1)
        tri = (j <= i).astype(BF16)
        cs = sum(_dot(tri, p) for p in _split3(a))
        ex = _head_expand(n_inner)
        dt_ref[...] = sum(_dot(p, ex) for p in _split3(dt))
        cs_ref[...] = sum(_dot(p, ex) for p in _split3(cs))

    blk = pl.BlockSpec((CHUNK, LANES), lambda c: (c, 0))
    out = pl.BlockSpec((CHUNK, n_inner), lambda c: (c, 0))
    return pl.pallas_call(body, grid=(T // CHUNK,), in_specs=[blk, _vec(LANES), _vec(LANES)], out_specs=[out, out],
                          out_shape=[_sds((T, n_inner), F32)] * 2, name=name, compiler_params=_params("parallel"))(dt_raw, dt_bias, a_log)


def _pair_terms(cs_p):
    lane = lax.broadcasted_iota(jnp.int32, (CHUNK, CHUNK), 1)
    sub = lax.broadcasted_iota(jnp.int32, (CHUNK, CHUNK), 0)
    csT = cs_p.T
    Ls = []
    for k in range(2):
        col = jnp.sum(jnp.where(lane == k * HEADDIM, cs_p, 0.0), axis=1, keepdims=True)
        rowv = csT[k * HEADDIM:k * HEADDIM + 1, :]
        Ls.append(jnp.exp(jnp.where(sub >= lane, col - rowv, -jnp.inf)))
    return Ls, jnp.exp(csT[:, CHUNK - 1:CHUNK])


def _block_diag(xp):
    lane = lax.broadcasted_iota(jnp.int32, xp.shape, 1)
    return jnp.concatenate([jnp.where(lane < HEADDIM, xp, 0.0), jnp.where(lane >= HEADDIM, xp, 0.0)], axis=0)


def _ssd_specs(T, n_inner):
    nc = T // CHUNK
    xo, bo, co = 0, n_inner // LANES, n_inner // LANES + NGROUPS
    g_blk = lambda f: pl.BlockSpec((CHUNK, GROUP_W), lambda c, g: (f(c), g))
    return nc, g_blk, (lambda f: pl.BlockSpec((CHUNK, NSTATE), lambda c, g: (f(c), bo + g))), (lambda f: pl.BlockSpec((CHUNK, NSTATE), lambda c, g: (f(c), co + g)))


def _ssd_fwd(xbc, dt_e, cs_e, d_e, name):
    T = xbc.shape[0]
    n_inner = dt_e.shape[1]
    nc, g_blk, b_blk, c_blk = _ssd_specs(T, n_inner)
    ident = lambda c: c

    def body(xs_ref, b_ref, c_ref, dt_ref, cs_ref, d_ref, y_ref, p_ref, st):
        c, g = pl.program_id(0), pl.program_id(1)

        @pl.when(c == 0)
        def _():
            st[g] = jnp.zeros((GROUP_W, NSTATE), F32)

        P = st[g]
        p_ref[0, 0] = P
        xs, dt, cs = xs_ref[...], dt_ref[...], cs_ref[...]
        Bb, Cb = b_ref[...].astype(BF16), c_ref[...].astype(BF16)
        CBm = _dot(Cb, Bb, NT)
        X = xs * dt
        decay = jnp.exp(cs[CHUNK - 1:CHUNK, :] - cs)
        y_off = _dot(Cb, P.astype(BF16), NT) * jnp.exp(cs)
        ys, ecl = [], []
        for pr in range(2):
            sl = slice(pr * LANES, (pr + 1) * LANES)
            Ls, e_last = _pair_terms(cs[:, sl])
            ecl.append(e_last)
            Mcat = jnp.concatenate([(CBm * L).astype(BF16) for L in Ls], axis=1)
            ys.append(_dot(Mcat, _block_diag(X[:, sl]).astype(BF16)))
        y_ref[...] = jnp.concatenate(ys, axis=1) + y_off + xs * d_ref[...]
        S = _dot((X * decay).astype(BF16), Bb, TN)
        st[g] = P * jnp.concatenate(ecl, axis=0) + S

    p_blk = pl.BlockSpec((1, 1, GROUP_W, NSTATE), lambda c, g: (c, g, 0, 0))
    return pl.pallas_call(
        body, grid=(nc, NGROUPS),
        in_specs=[g_blk(ident), b_blk(ident), c_blk(ident), g_blk(ident), g_blk(ident), pl.BlockSpec((1, GROUP_W), lambda c, g: (0, g))],
        out_specs=[g_blk(ident), p_blk],
        out_shape=[_sds((T, n_inner), F32), _sds((nc, NGROUPS, GROUP_W, NSTATE), F32)],
        scratch_shapes=[pltpu.VMEM((NGROUPS, GROUP_W, NSTATE), F32)],
        name=name, compiler_params=_params("arbitrary", "arbitrary"))(xbc, xbc, xbc, dt_e, cs_e, d_e)


def _ssd_bwd(xbc, dt_e, cs_e, d_e, states, dy, name):
    T = xbc.shape[0]
    n_inner = dt_e.shape[1]
    nc, g_blk, b_blk, c_blk = _ssd_specs(T, n_inner)
    rev = lambda c: nc - 1 - c

    def body(xs_ref, b_ref, c_ref, dt_ref, cs_ref, d_ref, p_ref, pn_ref, dy_ref,
             dxs_ref, db_ref, dc_ref, ddt_ref, dcs_ref, dd_ref, dst):
        cc, g = pl.program_id(0), pl.program_id(1)

        @pl.when(cc == 0)
        def _():
            dst[g] = jnp.zeros((GROUP_W, NSTATE), F32)

        dS = dst[g]
        dSb = dS.astype(BF16)
        P, Pn = p_ref[0, 0], pn_ref[0, 0]
        Pb = P.astype(BF16)
        xs, dt, cs, dY = xs_ref[...], dt_ref[...], cs_ref[...], dy_ref[...]
        Bb, Cb = b_ref[...].astype(BF16), c_ref[...].astype(BF16)
        X = xs * dt
        ecs = jnp.exp(cs)
        decay = jnp.exp(cs[CHUNK - 1:CHUNK, :] - cs)
        CBm = _dot(Cb, Bb, NT)
        dYe = dY * ecs
        dYeb = dYe.astype(BF16)
        dP_off = _dot(dYeb, Cb, TN)
        dC = _dot(dYeb, Pb)
        dcs = dYe * _dot(Cb, Pb, NT)
        Xd = X * decay
        dB = _dot(Xd.astype(BF16), dSb)
        E = _dot(Bb, dSb, NT)
        dX = E * decay
        dcs = dcs - E * Xd
        Z = dS * Pn
        zh = Z.astype(BF16)
        zl = (Z - zh.astype(F32)).astype(BF16)
        ones = jnp.ones((8, NSTATE), BF16)
        R = _dot(ones, zh, NT) + _dot(ones, zl, NT)
        sub_g = lax.broadcasted_iota(jnp.int32, (CHUNK, GROUP_W), 0)
        dcs = dcs + jnp.where(sub_g == CHUNK - 1, R[0:1, :], 0.0)
        lane = lax.broadcasted_iota(jnp.int32, (CHUNK, CHUNK), 1)
        sub = lax.broadcasted_iota(jnp.int32, (CHUNK, CHUNK), 0)
        dCB = jnp.zeros((CHUNK, CHUNK), F32)
        dXs, dcss, ecl = [], [], []
        for pr in range(2):
            sl = slice(pr * LANES, (pr + 1) * LANES)
            Ls, e_last = _pair_terms(cs[:, sl])
            ecl.append(e_last)
            dYpb = dY[:, sl].astype(BF16)
            dMcat = _dot(dYpb, _block_diag(X[:, sl]).astype(BF16), NT)
            Mcat = jnp.concatenate([(CBm * L).astype(BF16) for L in Ls], axis=1)
            dXt = _dot(Mcat, dYpb, TN)
            dXs.append(jnp.where(lane < HEADDIM, dXt[:CHUNK], dXt[CHUNK:]))
            colacc = jnp.zeros((CHUNK, CHUNK), F32)
            rowacc = jnp.zeros((CHUNK, CHUNK), F32)
            for k in range(2):
                dG = dMcat[:, k * CHUNK:(k + 1) * CHUNK] * Ls[k]
                dCB = dCB + dG
                Q = dG * CBm
                colacc = colacc + jnp.where(lane == k * HEADDIM, jnp.sum(Q, axis=1, keepdims=True), 0.0)
                rowacc = rowacc + jnp.where(sub == k * HEADDIM, jnp.sum(Q, axis=0, keepdims=True), 0.0)
            dcss.append(colacc - rowacc.T)
        dX = dX + jnp.concatenate(dXs, axis=1)
        dcs = dcs + jnp.concatenate(dcss, axis=1)
        dCBb = dCB.astype(BF16)
        dc_ref[...] = dC + _dot(dCBb, Bb)
        db_ref[...] = dB + _dot(dCBb, Cb, TN)
        dxs_ref[...] = dX * dt + dY * d_ref[...]
        ddt_ref[...] = dX * xs
        dcs_ref[...] = dcs
        dd_ref[0] = jnp.sum(dY * xs, axis=0, keepdims=True)
        dst[g] = dS * jnp.concatenate(ecl, axis=0) + dP_off

    p_blk = pl.BlockSpec((1, 1, GROUP_W, NSTATE), lambda c, g: (nc - 1 - c, g, 0, 0))
    pn_blk = pl.BlockSpec((1, 1, GROUP_W, NSTATE), lambda c, g: (jnp.minimum(nc - c, nc - 1), g, 0, 0))
    st_blk = pl.BlockSpec((CHUNK, NSTATE), lambda c, g: (nc - 1 - c, g))
    return pl.pallas_call(
        body, grid=(nc, NGROUPS),
        in_specs=[g_blk(rev), b_blk(rev), c_blk(rev), g_blk(rev), g_blk(rev), pl.BlockSpec((1, GROUP_W), lambda c, g: (0, g)),
                  p_blk, pn_blk, g_blk(rev)],
        out_specs=[g_blk(rev), st_blk, st_blk, g_blk(rev), g_blk(rev), pl.BlockSpec((1, 1, GROUP_W), lambda c, g: (nc - 1 - c, 0, g))],
        out_shape=[_sds((T, n_inner), F32), _sds((T, NGROUPS * NSTATE), F32), _sds((T, NGROUPS * NSTATE), F32),
                   _sds((T, n_inner), F32), _sds((T, n_inner), F32), _sds((nc, 1, n_inner), F32)],
        scratch_shapes=[pltpu.VMEM((NGROUPS, GROUP_W, NSTATE), F32)],
        name=name, compiler_params=_params("arbitrary", "arbitrary"))(xbc, xbc, xbc, dt_e, cs_e, d_e, states, states, dy)


def _ssd_post(ddt_e, dcs_e, dd_p, dt_raw, dt_bias, a_log, n_heads, name):
    T, n_inner = ddt_e.shape

    def body(ddt_ref, dcs_ref, dd_ref, r_ref, b_ref, al_ref, draw_ref, dbias_ref, dal_ref, ddsk_ref):
        @pl.when(pl.program_id(0) == 0)
        def _():
            dbias_ref[...] = jnp.zeros_like(dbias_ref)
            dal_ref[...] = jnp.zeros_like(dal_ref)
            ddsk_ref[...] = jnp.zeros_like(ddsk_ref)

        ex = _head_expand(n_inner)
        red = lambda v: sum(_dot(p, ex, NT) for p in _split3(v))
        raw = r_ref[...] + b_ref[...]
        dt = _softplus(raw)
        A = -jnp.exp(al_ref[...])
        i = lax.broadcasted_iota(jnp.int32, (CHUNK, CHUNK), 0)
        j = lax.broadcasted_iota(jnp.int32, (CHUNK, CHUNK), 1)
        upper = (j >= i).astype(BF16)
        da = sum(_dot(upper, p) for p in _split3(red(dcs_ref[...])))
        ddt = red(ddt_ref[...]) + da * A
        lane = lax.broadcasted_iota(jnp.int32, (CHUNK, LANES), 1)
        draw = jnp.where(lane < n_heads, ddt * jax.nn.sigmoid(raw), 0.0)
        draw_ref[...] = draw.astype(BF16)
        dbias_ref[...] += jnp.sum(draw, axis=0, keepdims=True)
        dal_ref[...] += jnp.sum(da * dt, axis=0, keepdims=True) * A
        ddsk_ref[...] += red(jnp.broadcast_to(dd_ref[0], (8, n_inner)))[0:1, :]

    wide = pl.BlockSpec((CHUNK, n_inner), lambda c: (c, 0))
    blk = pl.BlockSpec((CHUNK, LANES), lambda c: (c, 0))
    return pl.pallas_call(
        body, grid=(T // CHUNK,),
        in_specs=[wide, wide, pl.BlockSpec((1, 1, n_inner), lambda c: (c, 0, 0)), blk, _vec(LANES), _vec(LANES)],
        out_specs=[blk, _vec(LANES), _vec(LANES), _vec(LANES)],
        out_shape=[_sds((T, LANES), BF16)] + [_sds((1, LANES), F32)] * 3,
        name=name, compiler_params=_params("arbitrary"))(ddt_e, dcs_e, dd_p, dt_raw, dt_bias, a_log)


def _row2(v):
    return v.reshape(1, -1).astype(F32)


def _pad_lanes(v):
    return jnp.pad(_row2(v), ((0, 0), (0, LANES - v.shape[-1])))


def _local_step(x, tgt, W, small):
    T, D = x.shape
    n_inner = 2 * D
    n_heads = n_inner // HEADDIM
    norm_mix, norm_mlp, norm_final = _row2(small["norm_mix"]), _row2(small["norm_mlp"]), _row2(small["norm_final"])
    b_gate, ssm_b, ssm_norm_w = _row2(small["b_gate"]), _row2(small["ssm_conv_b"]), _row2(small["ssm_norm_w"])
    dt_bias, a_log = _pad_lanes(small["dt_bias"]), _pad_lanes(small["A_log"])
    d_e = jnp.repeat(small["D_skip"].astype(F32), HEADDIM).reshape(1, n_inner)
    sc_w, ssm_w = small["sc_conv_w"], small["ssm_conv_w"]

    hb = _rms_fwd(x, norm_mix, "rms_mix")
    p_sc = _mm(hb, W["sc"], mode="nn", name="proj_sc")
    p_z = _mm(hb, W["z"], mode="nn", name="proj_z")
    p_xbc = _mm(hb, W["xbc"], mode="nn", name="proj_xbc")
    p_dt = _mm(hb, W["dt"], mode="nn", name="proj_dt")
    p_gate = _mm(hb, W["gate"], mode="nn", name="proj_gate")
    ya = _sc_fwd(p_sc, sc_w, "sc_fwd")
    br_a = _mm(ya, W["bsc"], mode="nn", name="branch_sc")
    xbc = _ssm_conv_fwd(p_xbc, ssm_w, ssm_b, "ssm_conv_fwd")
    dt_e, cs_e = _ssd_prep(p_dt, dt_bias, a_log, n_inner, "ssd_prep")
    y, states = _ssd_fwd(xbc, dt_e, cs_e, d_e, "ssd_fwd")
    yb = _gnorm_fwd(y, p_z, ssm_norm_w, "gnorm_fwd")
    br_b = _mm(yb, W["bssm"], mode="nn", name="branch_ssm")
    merged = _merge_fwd(p_gate, b_gate, br_a, br_b, "merge_fwd")
    x1 = _mm(merged, W["out"], mode="nn", name="out_proj", extras=(x,), epi=_epi_add)
    h2 = _rms_fwd(x1, norm_mlp, "rms_mlp")
    a_pre, r_act = _mm(h2, W["w1"], mode="nn", name="mlp_up", epi=_epi_relu2, out_dtypes=(F32, BF16))
    x2 = _mm(r_act, W["w2"], mode="nn", name="mlp_down", extras=(x1,), epi=_epi_add)
    dx2, dx2b, g_norm_final, loss_row = _final(x2, norm_final, tgt, "final")

    da = _mm(dx2b, W["w2"], mode="nt", name="mlp_down_dx", extras=(a_pre,), epi=_epi_relu2_bwd, out_dtypes=(BF16,))
    g_w2 = _mm(r_act, dx2b, mode="tn", name="mlp_down_dw", out_dtypes=(BF16,))
    g_w1 = _mm(h2, da, mode="tn", name="mlp_up_dw", out_dtypes=(BF16,))
    dh2 = _mm(da, W["w1"], mode="nt", name="mlp_up_dx")
    dx1, dx1b, g_norm_mlp = _rms_bwd(x1, norm_mlp, dh2, dx2, "rms_mlp_bwd")
    dmerged = _mm(dx1b, W["out"], mode="nt", name="out_proj_dx")
    g_wout = _mm(merged, dx1b, mode="tn", name="out_proj_dw", out_dtypes=(BF16,))
    dbr_a, dbr_b, d_gate, g_b_gate = _merge_bwd(dmerged, p_gate, b_gate, br_a, br_b, "merge_bwd")
    dyb = _mm(dbr_b, W["bssm"], mode="nt", name="branch_ssm_dx")
    g_wbssm = _mm(yb, dbr_b, mode="tn", name="branch_ssm_dw", out_dtypes=(BF16,))
    dya = _mm(dbr_a, W["bsc"], mode="nt", name="branch_sc_dx")
    g_wbsc = _mm(ya, dbr_a, mode="tn", name="branch_sc_dw", out_dtypes=(BF16,))
    dy, d_z, g_ssm_norm_w = _gnorm_bwd(y, p_z, ssm_norm_w, dyb, "gnorm_bwd")
    dxs, dB, dC, ddt_e, dcs_e, dd_p = _ssd_bwd(xbc, dt_e, cs_e, d_e, states, dy, "ssd_bwd")
    d_dt, g_dt_bias, g_a_log, g_d_skip = _ssd_post(ddt_e, dcs_e, dd_p, p_dt, dt_bias, a_log, n_heads, "ssd_post")
    d_xbc, g_ssm_w, g_ssm_b = _ssm_conv_bwd(p_xbc, ssm_w, ssm_b, dxs, dB, dC, "ssm_conv_bwd")
    d_scB, d_scC, d_scX, g_sc_w = _sc_bwd(p_sc, sc_w, dya, "sc_bwd")
    d_sc = jnp.concatenate([d_scB, d_scC, d_scX], axis=1)
    pieces = [("sc", d_sc), ("z", d_z), ("xbc", d_xbc), ("dt", d_dt), ("gate", d_gate)]
    g_win = {k: _mm(hb, d, mode="tn", name="proj_dw_" + k, out_dtypes=(BF16,)) for k, d in pieces}
    dh = None
    for k, d in pieces:
        if dh is None:
            dh = _mm(d, W[k], mode="nt", name="proj_dx_" + k)
        else:
            dh = _mm(d, W[k], mode="nt", name="proj_dx_" + k, extras=(dh,), epi=_epi_add)
    grad_x, _, g_norm_mix = _rms_bwd(x, norm_mix, dh, dx1, "rms_mix_bwd")

    g_big = dict(win=g_win, bsc=g_wbsc, bssm=g_wbssm, out=g_wout, w1=g_w1, w2=g_w2)
    g_small = dict(norm_mix=g_norm_mix, b_gate=g_b_gate, sc_conv_w=g_sc_w, ssm_conv_w=g_ssm_w, ssm_conv_b=g_ssm_b,
                   dt_bias=g_dt_bias, A_log=g_a_log, D_skip=g_d_skip, ssm_norm_w=g_ssm_norm_w, norm_mlp=g_norm_mlp,
                   norm_final=g_norm_final, loss=loss_row)
    return grad_x, g_big, g_small


def _exchange(arrs, scatter, name):
    n = len(arrs)

    def body(*refs):
        ins, outs = refs[:n], refs[n:2 * n]
        send_sems, recv_sems, local_sems = refs[2 * n:]
        x, y, c = lax.axis_index("x"), lax.axis_index("y"), lax.axis_index("c")
        me = 4 * x + 2 * y + c
        peers = []
        for k in range(1, N_DEV):
            px = 1 - x if k & 4 else x
            py = 1 - y if k & 2 else y
            pc = 1 - c if k & 1 else c
            peers.append(((px, py, pc), 4 * px + 2 * py + pc))

        def src_of(a, s):
            return ins[a].at[s] if scatter[a] else ins[a]

        def remote(a, k, slot):
            peer, pid = peers[k]
            return pltpu.make_async_remote_copy(src_ref=src_of(a, pid), dst_ref=outs[a].at[slot], send_sem=send_sems.at[a, k],
                                                recv_sem=recv_sems.at[a, k], device_id=peer, device_id_type=MESH)

        own = [pltpu.make_async_copy(src_of(a, me), outs[a].at[me], local_sems.at[a]) for a in range(n)]
        sends = [remote(a, k, me) for a in range(n) for k in range(N_DEV - 1)]
        for cp in own + sends:
            cp.start()
        for a in range(n):
            for k in range(N_DEV - 1):
                remote(a, k, peers[k][1]).wait_recv()
        for cp in sends:
            cp.wait_send()
        for cp in own:
            cp.wait()

    out_shape = [_sds(a.shape if sc else (N_DEV,) + a.shape, a.dtype) for a, sc in zip(arrs, scatter)]
    return pl.pallas_call(
        body, in_specs=[ANY] * n, out_specs=[ANY] * n, out_shape=out_shape,
        scratch_shapes=[pltpu.SemaphoreType.DMA((n, N_DEV - 1)), pltpu.SemaphoreType.DMA((n, N_DEV - 1)), pltpu.SemaphoreType.DMA((n,))],
        name=name,
    )(*arrs)


def _adam(w, m, v, gparts, name):
    R, C = w.shape
    n = gparts.shape[0]
    tr = R if R <= 256 else 128
    assert R % tr == 0
    c1 = 1.0 / (1.0 - ADAM_B1 ** ADAM_STEP)
    c2 = 1.0 / (1.0 - ADAM_B2 ** ADAM_STEP)

    def body(w_ref, m_ref, v_ref, g_ref, go_ref, d_ref, mo_ref, vo_ref):
        g = g_ref[0].astype(F32)
        for s in range(1, n):
            g = g + g_ref[s].astype(F32)
        mn = ADAM_B1 * m_ref[...] + (1.0 - ADAM_B1) * g
        vn = ADAM_B2 * v_ref[...] + (1.0 - ADAM_B2) * (g * g)
        go_ref[...] = g
        mo_ref[...] = mn
        vo_ref[...] = vn
        d_ref[...] = -ADAM_LR * ((mn * c1) / (jnp.sqrt(vn * c2) + ADAM_EPS) + ADAM_WD * w_ref[...])

    blk = pl.BlockSpec((tr, C), lambda i: (i, 0))
    return pl.pallas_call(
        body, grid=(R // tr,), in_specs=[blk, blk, blk, pl.BlockSpec((n, tr, C), lambda i: (0, i, 0))],
        out_specs=[blk] * 4, out_shape=[_sds((R, C), F32)] * 4, name=name, compiler_params=_params("parallel"))(w, m, v, gparts)


_SMALL_ORDER = ("norm_mix", "b_gate", "sc_conv_w", "ssm_conv_w", "ssm_conv_b", "dt_bias", "A_log", "D_skip", "ssm_norm_w",
                "norm_mlp", "norm_final", "loss")
_REPLICATED = ("norm_mix", "b_gate", "ssm_conv_b", "dt_bias", "A_log", "D_skip", "ssm_norm_w", "norm_mlp", "norm_final")


def _cols_to_slots(g, n):
    R = g.shape[0]
    return jnp.transpose(g.reshape(R, n, g.shape[1] // n), (1, 0, 2))


def _slots_to_cols(g):
    n, R, C = g.shape
    return jnp.transpose(g, (1, 0, 2)).reshape(R, n * C)


def kernel(x, norm_mix, w_in, b_gate, sc_conv_w, ssm_conv_w, ssm_conv_b, dt_bias, A_log, D_skip, ssm_norm_w, w_branch_sc, w_branch_ssm, w_out, norm_mlp, w_mlp1, w_mlp2, norm_final, loss_target, m_norm_mix, m_w_in, m_b_gate, m_sc_conv_w, m_ssm_conv_w, m_ssm_conv_b, m_dt_bias, m_A_log, m_D_skip, m_ssm_norm_w, m_w_branch_sc, m_w_branch_ssm, m_w_out, m_norm_mlp, m_w_mlp1, m_w_mlp2, m_norm_final, v_norm_mix, v_w_in, v_b_gate, v_sc_conv_w, v_ssm_conv_w, v_ssm_conv_b, v_dt_bias, v_A_log, v_D_skip, v_ssm_norm_w, v_w_branch_sc, v_w_branch_ssm, v_w_out, v_norm_mlp, v_w_mlp1, v_w_mlp2, v_norm_final):
    T, D = x.shape[1], x.shape[2]
    n_inner = 2 * D
    n_heads = n_inner // HEADDIM
    n_xbc = n_inner + 2 * NGROUPS * NSTATE
    me = 4 * lax.axis_index("x") + 2 * lax.axis_index("y") + lax.axis_index("c")

    gathered = _exchange([w_in.astype(BF16), w_branch_sc.astype(BF16), w_branch_ssm.astype(BF16), w_out.astype(BF16),
                          w_mlp1.astype(BF16), w_mlp2.astype(BF16), sc_conv_w, ssm_conv_w], [False] * 8, "gather_weights")
    win_full = _slots_to_cols(gathered[0])
    o_z, o_xbc, o_dt, o_gate = 3 * D, 3 * D + n_inner, 3 * D + n_inner + n_xbc, 3 * D + n_inner + n_xbc + n_heads
    W = dict(
        sc=win_full[:, :o_z], z=win_full[:, o_z:o_xbc], xbc=win_full[:, o_xbc:o_dt],
        dt=jnp.pad(win_full[:, o_dt:o_gate], ((0, 0), (0, LANES - n_heads))), gate=win_full[:, o_gate:],
        bsc=gathered[1].reshape(-1, D), bssm=gathered[2].reshape(-1, D), out=gathered[3].reshape(-1, D),
        w1=_slots_to_cols(gathered[4]), w2=gathered[5].reshape(-1, D))
    small = dict(norm_mix=norm_mix, b_gate=b_gate, sc_conv_w=_slots_to_cols(gathered[6]), ssm_conv_w=_slots_to_cols(gathered[7]),
                 ssm_conv_b=ssm_conv_b, dt_bias=dt_bias, A_log=A_log, D_skip=D_skip, ssm_norm_w=ssm_norm_w, norm_mlp=norm_mlp,
                 norm_final=norm_final)

    grad_x, g_big, g_small = _local_step(x.reshape(T, D), loss_target.reshape(T, D), W, small)

    gw = g_big["win"]
    gwin_full = jnp.concatenate([gw["sc"], gw["z"], gw["xbc"], gw["dt"][:, :n_heads], gw["gate"]], axis=1)
    chunks = lambda g: g.reshape((N_DEV, g.shape[0] // N_DEV) + g.shape[1:])
    small_flat = jnp.concatenate([g_small[k].reshape(-1) for k in _SMALL_ORDER])
    n_small = small_flat.shape[0]
    rows = -(-n_small // (8 * LANES)) * 8
    small_pack = jnp.pad(small_flat, (0, rows * LANES - n_small)).reshape(rows, LANES)
    parts = _exchange([_cols_to_slots(gwin_full, N_DEV), chunks(g_big["bsc"]), chunks(g_big["bssm"]), chunks(g_big["out"]),
                       _cols_to_slots(g_big["w1"], N_DEV), chunks(g_big["w2"]),
                       small_pack], [True] * 6 + [False], "scatter_grads")

    res = {}
    big = [("w_in", w_in, m_w_in, v_w_in), ("w_branch_sc", w_branch_sc, m_w_branch_sc, v_w_branch_sc),
           ("w_branch_ssm", w_branch_ssm, m_w_branch_ssm, v_w_branch_ssm), ("w_out", w_out, m_w_out, v_w_out),
           ("w_mlp1", w_mlp1, m_w_mlp1, v_w_mlp1), ("w_mlp2", w_mlp2, m_w_mlp2, v_w_mlp2)]
    for i, (k, w, m, v) in enumerate(big):
        res[k] = _adam(w, m, v, parts[i], "adam_" + k)

    sizes = {k: g_small[k].size for k in _SMALL_ORDER}
    offs, o = {}, 0
    for k in _SMALL_ORDER:
        offs[k] = o
        o += sizes[k]
    rep_w = dict(norm_mix=norm_mix, b_gate=b_gate, ssm_conv_b=ssm_conv_b, dt_bias=dt_bias, A_log=A_log, D_skip=D_skip,
                 ssm_norm_w=ssm_norm_w, norm_mlp=norm_mlp, norm_final=norm_final)
    rep_m = dict(norm_mix=m_norm_mix, b_gate=m_b_gate, ssm_conv_b=m_ssm_conv_b, dt_bias=m_dt_bias, A_log=m_A_log, D_skip=m_D_skip,
                 ssm_norm_w=m_ssm_norm_w, norm_mlp=m_norm_mlp, norm_final=m_norm_final)
    rep_v = dict(norm_mix=v_norm_mix, b_gate=v_b_gate, ssm_conv_b=v_ssm_conv_b, dt_bias=v_dt_bias, A_log=v_A_log, D_skip=v_D_skip,
                 ssm_norm_w=v_ssm_norm_w, norm_mlp=v_norm_mlp, norm_final=v_norm_final)

    def pack(d):
        flat = jnp.zeros((rows * LANES,), F32)
        for k in _REPLICATED:
            flat = lax.dynamic_update_slice(flat, d[k].astype(F32).reshape(-1), (offs[k],))
        return flat.reshape(rows, LANES)

    sm = _adam(pack(rep_w), pack(rep_m), pack(rep_v), parts[6], "adam_small")
    sm = [s.reshape(-1) for s in sm]
    for k in _REPLICATED:
        n_k = rep_w[k].shape[0]
        res[k] = tuple(s[offs[k]:offs[k] + n_k] for s in sm)
    loss = sm[0][offs["loss"]]
    for k, w, m, v, K, full in (("sc_conv_w", sc_conv_w, m_sc_conv_w, v_sc_conv_w, SC_K, D),
                                ("ssm_conv_w", ssm_conv_w, m_ssm_conv_w, v_ssm_conv_w, SSM_K, n_xbc)):
        g_full = sm[0][offs[k]:offs[k] + K * full].reshape(K, full)
        cw = full // N_DEV
        g_mine = lax.dynamic_slice_in_dim(g_full, me * cw, cw, axis=1)
        res[k] = _adam(w, m, v, g_mine[None], "adam_" + k)

    order = ("norm_mix", "w_in", "b_gate", "sc_conv_w", "ssm_conv_w", "ssm_conv_b", "dt_bias", "A_log", "D_skip", "ssm_norm_w",
             "w_branch_sc", "w_branch_ssm", "w_out", "norm_mlp", "w_mlp1", "w_mlp2", "norm_final")
    outs = [loss, grad_x.reshape(1, T, D)]
    for j in range(4):
        outs += [res[k][j] for k in order]
    return tuple(outs)
```

```python
import functools

import jax
import jax.numpy as jnp
from jax import lax
from jax.experimental import pallas as pl
from jax.experimental.pallas import tpu as pltpu

F32 = jnp.float32
BF16 = jnp.bfloat16

EPS = 1e-6
N_DEV = 8
HEADDIM = 64
NSTATE = 128
CHUNK = 128
NGROUPS = 8
GROUP_W = 256
SC_K = 3
SSM_K = 4
LANES = 128

ADAM_LR = 0.001
ADAM_B1 = 0.9
ADAM_B2 = 0.999
ADAM_EPS = 1e-08
ADAM_WD = 0.01
ADAM_STEP = 10

NN = (((1,), (0,)), ((), ()))
NT = (((1,), (1,)), ((), ()))
TN = (((0,), (0,)), ((), ()))
_DIMS = {"nn": NN, "nt": NT, "tn": TN}

ANY = pl.BlockSpec(memory_space=pl.ANY)
MESH = pl.DeviceIdType.MESH


def _sds(shape, dtype):
    return jax.ShapeDtypeStruct(tuple(shape), dtype)


def _dot(a, b, dims=NN):
    return lax.dot_general(a, b, dims, preferred_element_type=F32)


def _params(*sem):
    return pltpu.CompilerParams(dimension_semantics=tuple(sem))


def _call(body, *, grid, in_specs, out_specs, out_shape, args, name, sem, scratch=(), comm=None):
    if comm is None:
        outs = pl.pallas_call(body, grid=grid, in_specs=list(in_specs), out_specs=list(out_specs), out_shape=list(out_shape),
                              scratch_shapes=list(scratch), name=name, compiler_params=_params(*sem))(*args)
        return list(outs), None
    n, n_in, n_out, n_scr = comm.n, len(in_specs), len(out_shape), len(scratch)

    def wrapped(*refs):
        ins, c_in = refs[:n_in], refs[n_in:n_in + n]
        outs, c_out = refs[n_in + n:n_in + n + n_out], refs[n_in + n + n_out:n_in + 2 * n + n_out]
        rest = refs[n_in + 2 * n + n_out:]
        scr, sems = rest[:n_scr], rest[n_scr:]
        first, last = None, None
        for d, g in enumerate(grid):
            f, l = pl.program_id(d) == 0, pl.program_id(d) == g - 1
            first, last = (f, l) if first is None else (first & f, last & l)

        @pl.when(first)
        def _():
            comm.start(c_in, c_out, sems)

        body(*ins, *outs, *scr)

        @pl.when(last)
        def _():
            comm.finish(c_in, c_out, sems)

    outs = pl.pallas_call(
        wrapped, grid=grid, in_specs=list(in_specs) + [ANY] * n, out_specs=list(out_specs) + [ANY] * n,
        out_shape=list(out_shape) + comm.out_shape, scratch_shapes=list(scratch) + comm.scratch,
        input_output_aliases={n_in + i: n_out + o for i, o in comm.aliases.items()},
        name=name, compiler_params=_params(*["arbitrary"] * len(grid)))(*args, *comm.arrs)
    return list(outs[:n_out]), list(outs[n_out:])


def _mm(a, b, *, mode, name, extras=(), epi=None, out_dtypes=(F32,), tm=512, tn=512, comm=None):
    a_list = list(a) if isinstance(a, (list, tuple)) else [a]
    b_list = list(b) if isinstance(b, (list, tuple)) else [b]
    if mode == "nn":
        M, N = a_list[0].shape[0], b_list[0].shape[1]
    elif mode == "nt":
        M, N = a_list[0].shape[0], b_list[0].shape[0]
    else:
        M, N = a_list[0].shape[1], b_list[0].shape[1]
    tm, tn = min(tm, M), min(tn, N)
    assert M % tm == 0 and N % tn == 0
    a_specs, b_specs = [], []
    for av, bv in zip(a_list, b_list):
        K = av.shape[0] if mode == "tn" else av.shape[1]
        a_specs.append(pl.BlockSpec((K, tm), lambda i, j: (0, i)) if mode == "tn" else pl.BlockSpec((tm, K), lambda i, j: (i, 0)))
        b_specs.append(pl.BlockSpec((tn, K), lambda i, j: (j, 0)) if mode == "nt" else pl.BlockSpec((K, tn), lambda i, j: (0, j)))
    mn_spec = pl.BlockSpec((tm, tn), lambda i, j: (i, j))
    n_p, n_ex = len(a_list), len(extras)
    dims = _DIMS[mode]

    def body(*refs):
        acc = _dot(refs[0][...], refs[n_p][...], dims)
        for p in range(1, n_p):
            acc = acc + _dot(refs[p][...], refs[n_p + p][...], dims)
        rest = refs[2 * n_p:]
        res = (acc,) if epi is None else epi(acc, *[r[...] for r in rest[:n_ex]])
        for o_ref, r in zip(rest[n_ex:], res):
            o_ref[...] = r.astype(o_ref.dtype)

    outs, carried = _call(
        body, grid=(M // tm, N // tn), in_specs=a_specs + b_specs + [mn_spec] * n_ex,
        out_specs=[mn_spec] * len(out_dtypes), out_shape=[_sds((M, N), d) for d in out_dtypes],
        args=a_list + b_list + list(extras), name=name, sem=("parallel", "parallel"), comm=comm)
    res = outs[0] if len(outs) == 1 else outs
    return res if comm is None else (res, carried)


def _epi_add(acc, r):
    return (acc + r,)


def _epi_add2(acc, r):
    s = acc + r
    return (s, s)


def _epi_relu2(acc):
    p = jnp.maximum(acc, 0.0)
    return (acc, p * p)


def _epi_relu2_bwd(acc, a):
    return (acc * (2.0 * jnp.maximum(a, 0.0)),)


def _row(tr, n):
    return pl.BlockSpec((tr, n), lambda i: (i, 0))


def _vec(n):
    return pl.BlockSpec((1, n), lambda i: (0, 0))


def _rms_fwd(x, w, name):
    T, D = x.shape
    tr = min(256, T)

    def body(x_ref, w_ref, o_ref):
        xv = x_ref[...]
        r = lax.rsqrt(jnp.mean(xv * xv, axis=-1, keepdims=True) + EPS)
        o_ref[...] = (xv * r * w_ref[...]).astype(BF16)

    return pl.pallas_call(body, grid=(T // tr,), in_specs=[_row(tr, D), _vec(D)], out_specs=_row(tr, D),
                          out_shape=_sds((T, D), BF16), name=name, compiler_params=_params("parallel"))(x, w)


def _rms_bwd(x, w, dh, dres, name):
    T, D = x.shape
    tr = min(256, T)

    def body(x_ref, w_ref, dh_ref, dres_ref, dx_ref, dxb_ref, dw_ref):
        @pl.when(pl.program_id(0) == 0)
        def _():
            dw_ref[...] = jnp.zeros_like(dw_ref)

        xv = x_ref[...]
        r = lax.rsqrt(jnp.mean(xv * xv, axis=-1, keepdims=True) + EPS)
        xh = xv * r
        dh_v = dh_ref[...]
        dw_ref[...] += jnp.sum(dh_v * xh, axis=0, keepdims=True)
        dxh = dh_v * w_ref[...]
        dx = r * (dxh - xh * jnp.mean(dxh * xh, axis=-1, keepdims=True)) + dres_ref[...]
        dx_ref[...] = dx
        dxb_ref[...] = dx.astype(BF16)

    return pl.pallas_call(
        body, grid=(T // tr,), in_specs=[_row(tr, D), _vec(D), _row(tr, D), _row(tr, D)],
        out_specs=[_row(tr, D), _row(tr, D), _vec(D)],
        out_shape=[_sds((T, D), F32), _sds((T, D), BF16), _sds((1, D), F32)],
        name=name, compiler_params=_params("arbitrary"))(x, w, dh, dres)


def _final(x2, w, tgt, name):
    T, D = x2.shape
    tr = min(256, T)

    def body(x_ref, w_ref, t_ref, dx_ref, dxb_ref, dw_ref, loss_ref):
        @pl.when(pl.program_id(0) == 0)
        def _():
            dw_ref[...] = jnp.zeros_like(dw_ref)
            loss_ref[...] = jnp.zeros_like(loss_ref)

        xv = x_ref[...]
        wv = w_ref[...]
        r = lax.rsqrt(jnp.mean(xv * xv, axis=-1, keepdims=True) + EPS)
        xh = xv * r
        err = xh * wv - t_ref[...]
        part = jnp.sum(jnp.sum(err * err, axis=1, keepdims=True), axis=0, keepdims=True) * (0.5 / D)
        loss_ref[...] += jnp.broadcast_to(part, loss_ref.shape)
        dy = err * (1.0 / D)
        dw_ref[...] += jnp.sum(dy * xh, axis=0, keepdims=True)
        dxh = dy * wv
        dx = r * (dxh - xh * jnp.mean(dxh * xh, axis=-1, keepdims=True))
        dx_ref[...] = dx
        dxb_ref[...] = dx.astype(BF16)

    return pl.pallas_call(
        body, grid=(T // tr,), in_specs=[_row(tr, D), _vec(D), _row(tr, D)],
        out_specs=[_row(tr, D), _row(tr, D), _vec(D), _vec(LANES)],
        out_shape=[_sds((T, D), F32), _sds((T, D), BF16), _sds((1, D), F32), _sds((1, LANES), F32)],
        name=name, compiler_params=_params("arbitrary"))(x2, w, tgt)


def _silu_parts(z):
    s = jax.nn.sigmoid(z)
    return z * s, s * (1.0 + z * (1.0 - s))


def _gnorm_fwd(y, z, w, name):
    T, N = y.shape
    tr = min(256, T)

    def body(y_ref, z_ref, w_ref, o_ref):
        for g in range(N // GROUP_W):
            sl = slice(g * GROUP_W, (g + 1) * GROUP_W)
            silu, _ = _silu_parts(z_ref[:, sl])
            yz = y_ref[:, sl] * silu
            r = lax.rsqrt(jnp.mean(yz * yz, axis=-1, keepdims=True) + EPS)
            o_ref[:, sl] = (yz * r * w_ref[:, sl]).astype(BF16)

    return pl.pallas_call(body, grid=(T // tr,), in_specs=[_row(tr, N), _row(tr, N), _vec(N)], out_specs=_row(tr, N),
                          out_shape=_sds((T, N), BF16), name=name, compiler_params=_params("parallel"))(y, z, w)


def _gnorm_bwd(y, z, w, dyb, name):
    T, N = y.shape
    tr = min(256, T)

    def body(y_ref, z_ref, w_ref, d_ref, dy_ref, dz_ref, dw_ref):
        @pl.when(pl.program_id(0) == 0)
        def _():
            dw_ref[...] = jnp.zeros_like(dw_ref)

        for g in range(N // GROUP_W):
            sl = slice(g * GROUP_W, (g + 1) * GROUP_W)
            yv = y_ref[:, sl]
            silu, dsilu = _silu_parts(z_ref[:, sl])
            yz = yv * silu
            r = lax.rsqrt(jnp.mean(yz * yz, axis=-1, keepdims=True) + EPS)
            yzh = yz * r
            d = d_ref[:, sl]
            dw_ref[:, sl] += jnp.sum(d * yzh, axis=0, keepdims=True)
            dyzh = d * w_ref[:, sl]
            dyz = r * (dyzh - yzh * jnp.mean(dyzh * yzh, axis=-1, keepdims=True))
            dy_ref[:, sl] = dyz * silu
            dz_ref[:, sl] = (dyz * yv * dsilu).astype(BF16)

    return pl.pallas_call(
        body, grid=(T // tr,), in_specs=[_row(tr, N), _row(tr, N), _vec(N), _row(tr, N)],
        out_specs=[_row(tr, N), _row(tr, N), _vec(N)],
        out_shape=[_sds((T, N), F32), _sds((T, N), BF16), _sds((1, N), F32)],
        name=name, compiler_params=_params("arbitrary"))(y, z, w, dyb)


def _merge_fwd(gate_raw, b_gate, br_a, br_b, name):
    T, D = br_a.shape
    tr = min(256, T)

    def body(g_ref, bg_ref, a_ref, b_ref, o_ref):
        g = jax.nn.sigmoid(g_ref[...] + bg_ref[...])
        o_ref[...] = (g[:, :D] * a_ref[...] + g[:, D:] * b_ref[...]).astype(BF16)

    return pl.pallas_call(body, grid=(T // tr,), in_specs=[_row(tr, 2 * D), _vec(2 * D), _row(tr, D), _row(tr, D)],
                          out_specs=_row(tr, D), out_shape=_sds((T, D), BF16), name=name,
                          compiler_params=_params("parallel"))(gate_raw, b_gate, br_a, br_b)


def _merge_bwd(dmerged, gate_raw, b_gate, br_a, br_b, name):
    T, D = br_a.shape
    tr = min(256, T)

    def body(d_ref, g_ref, bg_ref, a_ref, b_ref, da_ref, db_ref, dg_ref, dbg_ref):
        @pl.when(pl.program_id(0) == 0)
        def _():
            dbg_ref[...] = jnp.zeros_like(dbg_ref)

        g = jax.nn.sigmoid(g_ref[...] + bg_ref[...])
        d = d_ref[...]
        da_ref[...] = (d * g[:, :D]).astype(BF16)
        db_ref[...] = (d * g[:, D:]).astype(BF16)
        dg = jnp.concatenate([d * a_ref[...], d * b_ref[...]], axis=1) * g * (1.0 - g)
        dg_ref[...] = dg.astype(BF16)
        dbg_ref[...] += jnp.sum(dg, axis=0, keepdims=True)

    return pl.pallas_call(
        body, grid=(T // tr,), in_specs=[_row(tr, D), _row(tr, 2 * D), _vec(2 * D), _row(tr, D), _row(tr, D)],
        out_specs=[_row(tr, D), _row(tr, D), _row(tr, 2 * D), _vec(2 * D)],
        out_shape=[_sds((T, D), BF16), _sds((T, D), BF16), _sds((T, 2 * D), BF16), _sds((1, 2 * D), F32)],
        name=name, compiler_params=_params("arbitrary"))(dmerged, gate_raw, b_gate, br_a, br_b)


def _shift_down(u, s):
    if s == 0:
        return u
    row = lax.broadcasted_iota(jnp.int32, u.shape, 0)
    return jnp.where(row >= s, pltpu.roll(u, s, 0), 0.0)


def _shift_up(u, s):
    if s == 0:
        return u
    n = u.shape[0]
    row = lax.broadcasted_iota(jnp.int32, u.shape, 0)
    return jnp.where(row < n - s, pltpu.roll(u, n - s, 0), 0.0)


def _conv(u, w_ref, K):
    acc = u * w_ref[K - 1:K, :]
    for k in range(K - 1):
        acc = acc + _shift_down(u, K - 1 - k) * w_ref[k:k + 1, :]
    return acc


def _conv_bwd(u, dc, w_ref, dw_ref, K):
    du = dc * w_ref[K - 1:K, :]
    dw_ref[K - 1:K, :] = jnp.sum(dc * u, axis=0, keepdims=True)
    for k in range(K - 1):
        s = K - 1 - k
        dw_ref[k:k + 1, :] = jnp.sum(dc * _shift_down(u, s), axis=0, keepdims=True)
        du = du + _shift_up(dc, s) * w_ref[k:k + 1, :]
    return du


CB_W = 256


def _col(T, j0=0):
    return pl.BlockSpec((T, CB_W), lambda j: (0, j + j0))


def _sc_fwd(psc, w, name):
    T, D = psc.shape[0], psc.shape[1] // 3
    nb = D // CB_W

    def body(b_ref, c_ref, x_ref, w_ref, o_ref):
        o_ref[...] = (b_ref[...] * _conv(c_ref[...] * x_ref[...], w_ref, SC_K)).astype(BF16)

    return pl.pallas_call(
        body, grid=(nb,), in_specs=[_col(T), _col(T, nb), _col(T, 2 * nb), pl.BlockSpec((SC_K, CB_W), lambda j: (0, j))],
        out_specs=_col(T), out_shape=_sds((T, D), BF16), name=name, compiler_params=_params("parallel"))(psc, psc, psc, w)


def _sc_bwd(psc, w, dya, name):
    T, D = psc.shape[0], psc.shape[1] // 3
    nb = D // CB_W

    def body(b_ref, c_ref, x_ref, w_ref, d_ref, db_ref, dc_ref, dx_ref, dw_ref):
        cv, xv, d = c_ref[...], x_ref[...], d_ref[...]
        u = cv * xv
        db_ref[...] = (d * _conv(u, w_ref, SC_K)).astype(BF16)
        du = _conv_bwd(u, d * b_ref[...], w_ref, dw_ref, SC_K)
        dc_ref[...] = (du * xv).astype(BF16)
        dx_ref[...] = (du * cv).astype(BF16)

    wspec = pl.BlockSpec((SC_K, CB_W), lambda j: (0, j))
    return pl.pallas_call(
        body, grid=(nb,), in_specs=[_col(T), _col(T, nb), _col(T, 2 * nb), wspec, _col(T)],
        out_specs=[_col(T), _col(T), _col(T), wspec],
        out_shape=[_sds((T, D), BF16)] * 3 + [_sds((SC_K, D), F32)],
        name=name, compiler_params=_params("parallel"))(psc, psc, psc, w, dya)


def _ssm_conv_fwd(u, w, b, name):
    T, N = u.shape

    def body(u_ref, w_ref, b_ref, o_ref):
        c = _conv(u_ref[...], w_ref, SSM_K) + b_ref[...]
        o_ref[...] = c * jax.nn.sigmoid(c)

    return pl.pallas_call(
        body, grid=(N // CB_W,), in_specs=[_col(T), pl.BlockSpec((SSM_K, CB_W), lambda j: (0, j)), pl.BlockSpec((1, CB_W), lambda j: (0, j))],
        out_specs=_col(T), out_shape=_sds((T, N), F32), name=name, compiler_params=_params("parallel"))(u, w, b)


def _ssm_conv_bwd(u, w, b, dxs, dB, dC, name, comm=None):
    T, N = u.shape
    n_x, n_b = dxs.shape[1] // CB_W, dB.shape[1] // CB_W

    def body(u_ref, w_ref, b_ref, dx_ref, db_ref, dc_ref, du_ref, dw_ref, dbias_ref):
        j = pl.program_id(0)
        uv = u_ref[...]
        c = _conv(uv, w_ref, SSM_K) + b_ref[...]
        _, dsilu = _silu_parts(c)
        d = jnp.where(j < n_x, dx_ref[...], jnp.where(j < n_x + n_b, db_ref[...], dc_ref[...])) * dsilu
        dbias_ref[...] = jnp.sum(d, axis=0, keepdims=True)
        du_ref[...] = _conv_bwd(uv, d, w_ref, dw_ref, SSM_K).astype(BF16)

    wspec = pl.BlockSpec((SSM_K, CB_W), lambda j: (0, j))
    bspec = pl.BlockSpec((1, CB_W), lambda j: (0, j))
    outs, carried = _call(
        body, grid=(N // CB_W,),
        in_specs=[_col(T), wspec, bspec,
                  pl.BlockSpec((T, CB_W), lambda j: (0, jnp.minimum(j, n_x - 1))),
                  pl.BlockSpec((T, CB_W), lambda j: (0, jnp.clip(j - n_x, 0, n_b - 1))),
                  pl.BlockSpec((T, CB_W), lambda j: (0, jnp.clip(j - n_x - n_b, 0, n_b - 1)))],
        out_specs=[_col(T), wspec, bspec],
        out_shape=[_sds((T, N), BF16), _sds((SSM_K, N), F32), _sds((1, N), F32)],
        args=[u, w, b, dxs, dB, dC], name=name, sem=("parallel",), comm=comm)
    return outs if comm is None else (outs, carried)


def _split3(v):
    hi = v.astype(BF16)
    r = v - hi.astype(F32)
    mid = r.astype(BF16)
    lo = (r - mid.astype(F32)).astype(BF16)
    return hi, mid, lo


def _head_expand(n_lanes):
    h = lax.broadcasted_iota(jnp.int32, (LANES, n_lanes), 0)
    l = lax.broadcasted_iota(jnp.int32, (LANES, n_lanes), 1)
    return (jnp.right_shift(l, HEADDIM.bit_length() - 1) == h).astype(BF16)


def _softplus(v):
    return jnp.maximum(v, 0.0) + jnp.log1p(jnp.exp(-jnp.abs(v)))


def _ssd_prep(dt_raw, dt_bias, a_log, n_inner, name):
    T = dt_raw.shape[0]

    def body(r_ref, b_ref, al_ref, dt_ref, cs_ref):
        dt = _softplus(r_ref[...] + b_ref[...])
        a = dt * (-jnp.exp(al_ref[...]))
        i = lax.broadcasted_iota(jnp.int32, (CHUNK, CHUNK), 0)
        j = lax.broadcasted_iota(jnp.int32, (CHUNK, CHUNK), 1)
        tri = (j <= i).astype(BF16)
        cs = sum(_dot(tri, p) for p in _split3(a))
        ex = _head_expand(n_inner)
        dt_ref[...] = sum(_dot(p, ex) for p in _split3(dt))
        cs_ref[...] = sum(_dot(p, ex) for p in _split3(cs))

    blk = pl.BlockSpec((CHUNK, LANES), lambda c: (c, 0))
    out = pl.BlockSpec((CHUNK, n_inner), lambda c: (c, 0))
    return pl.pallas_call(body, grid=(T // CHUNK,), in_specs=[blk, _vec(LANES), _vec(LANES)], out_specs=[out, out],
                          out_shape=[_sds((T, n_inner), F32)] * 2, name=name, compiler_params=_params("parallel"))(dt_raw, dt_bias, a_log)


def _pair_terms(cs_p):
    lane = lax.broadcasted_iota(jnp.int32, (CHUNK, CHUNK), 1)
    sub = lax.broadcasted_iota(jnp.int32, (CHUNK, CHUNK), 0)
    csT = cs_p.T
    Ls = []
    for k in range(2):
        col = jnp.sum(jnp.where(lane == k * HEADDIM, cs_p, 0.0), axis=1, keepdims=True)
        rowv = csT[k * HEADDIM:k * HEADDIM + 1, :]
        Ls.append(jnp.exp(jnp.where(sub >= lane, col - rowv, -jnp.inf)))
    return Ls, jnp.exp(csT[:, CHUNK - 1:CHUNK])


def _block_diag(xp):
    lane = lax.broadcasted_iota(jnp.int32, xp.shape, 1)
    return jnp.concatenate([jnp.where(lane < HEADDIM, xp, 0.0), jnp.where(lane >= HEADDIM, xp, 0.0)], axis=0)


def _ssd_specs(T, n_inner):
    nc = T // CHUNK
    xo, bo, co = 0, n_inner // LANES, n_inner // LANES + NGROUPS
    g_blk = lambda f: pl.BlockSpec((CHUNK, GROUP_W), lambda c, g: (f(c), g))
    return nc, g_blk, (lambda f: pl.BlockSpec((CHUNK, NSTATE), lambda c, g: (f(c), bo + g))), (lambda f: pl.BlockSpec((CHUNK, NSTATE), lambda c, g: (f(c), co + g)))


def _ssd_fwd(xbc, dt_e, cs_e, d_e, name, comm=None):
    T = xbc.shape[0]
    n_inner = dt_e.shape[1]
    nc, g_blk, b_blk, c_blk = _ssd_specs(T, n_inner)
    ident = lambda c: c

    def body(xs_ref, b_ref, c_ref, dt_ref, cs_ref, d_ref, y_ref, p_ref, st):
        c, g = pl.program_id(0), pl.program_id(1)

        @pl.when(c == 0)
        def _():
            st[g] = jnp.zeros((GROUP_W, NSTATE), F32)

        P = st[g]
        p_ref[0, 0] = P
        xs, dt, cs = xs_ref[...], dt_ref[...], cs_ref[...]
        Bb, Cb = b_ref[...].astype(BF16), c_ref[...].astype(BF16)
        CBm = _dot(Cb, Bb, NT)
        X = xs * dt
        decay = jnp.exp(cs[CHUNK - 1:CHUNK, :] - cs)
        y_off = _dot(Cb, P.astype(BF16), NT) * jnp.exp(cs)
        ys, ecl = [], []
        for pr in range(2):
            sl = slice(pr * LANES, (pr + 1) * LANES)
            Ls, e_last = _pair_terms(cs[:, sl])
            ecl.append(e_last)
            Mcat = jnp.concatenate([(CBm * L).astype(BF16) for L in Ls], axis=1)
            ys.append(_dot(Mcat, _block_diag(X[:, sl]).astype(BF16)))
        y_ref[...] = jnp.concatenate(ys, axis=1) + y_off + xs * d_ref[...]
        S = _dot((X * decay).astype(BF16), Bb, TN)
        st[g] = P * jnp.concatenate(ecl, axis=0) + S

    p_blk = pl.BlockSpec((1, 1, GROUP_W, NSTATE), lambda c, g: (c, g, 0, 0))
    outs, carried = _call(
        body, grid=(nc, NGROUPS),
        in_specs=[g_blk(ident), b_blk(ident), c_blk(ident), g_blk(ident), g_blk(ident), pl.BlockSpec((1, GROUP_W), lambda c, g: (0, g))],
        out_specs=[g_blk(ident), p_blk],
        out_shape=[_sds((T, n_inner), F32), _sds((nc, NGROUPS, GROUP_W, NSTATE), F32)],
        scratch=[pltpu.VMEM((NGROUPS, GROUP_W, NSTATE), F32)],
        args=[xbc, xbc, xbc, dt_e, cs_e, d_e], name=name, sem=("arbitrary", "arbitrary"), comm=comm)
    return outs if comm is None else (outs, carried)


def _ssd_bwd(xbc, dt_e, cs_e, d_e, states, dy, name, comm=None):
    T = xbc.shape[0]
    n_inner = dt_e.shape[1]
    nc, g_blk, b_blk, c_blk = _ssd_specs(T, n_inner)
    rev = lambda c: nc - 1 - c

    def body(xs_ref, b_ref, c_ref, dt_ref, cs_ref, d_ref, p_ref, pn_ref, dy_ref,
             dxs_ref, db_ref, dc_ref, ddt_ref, dcs_ref, dd_ref, dst):
        cc, g = pl.program_id(0), pl.program_id(1)

        @pl.when(cc == 0)
        def _():
            dst[g] = jnp.zeros((GROUP_W, NSTATE), F32)

        dS = dst[g]
        dSb = dS.astype(BF16)
        P, Pn = p_ref[0, 0], pn_ref[0, 0]
        Pb = P.astype(BF16)
        xs, dt, cs, dY = xs_ref[...], dt_ref[...], cs_ref[...], dy_ref[...]
        Bb, Cb = b_ref[...].astype(BF16), c_ref[...].astype(BF16)
        X = xs * dt
        ecs = jnp.exp(cs)
        decay = jnp.exp(cs[CHUNK - 1:CHUNK, :] - cs)
        CBm = _dot(Cb, Bb, NT)
        dYe = dY * ecs
        dYeb = dYe.astype(BF16)
        dP_off = _dot(dYeb, Cb, TN)
        dC = _dot(dYeb, Pb)
        dcs = dYe * _dot(Cb, Pb, NT)
        Xd = X * decay
        dB = _dot(Xd.astype(BF16), dSb)
        E = _dot(Bb, dSb, NT)
        dX = E * decay
        dcs = dcs - E * Xd
        Z = dS * Pn
        zh = Z.astype(BF16)
        zl = (Z - zh.astype(F32)).astype(BF16)
        ones = jnp.ones((8, NSTATE), BF16)
        R = _dot(ones, zh, NT) + _dot(ones, zl, NT)
        sub_g = lax.broadcasted_iota(jnp.int32, (CHUNK, GROUP_W), 0)
        dcs = dcs + jnp.where(sub_g == CHUNK - 1, R[0:1, :], 0.0)
        lane = lax.broadcasted_iota(jnp.int32, (CHUNK, CHUNK), 1)
        sub = lax.broadcasted_iota(jnp.int32, (CHUNK, CHUNK), 0)
        dCB = jnp.zeros((CHUNK, CHUNK), F32)
        dXs, dcss, ecl = [], [], []
        for pr in range(2):
            sl = slice(pr * LANES, (pr + 1) * LANES)
            Ls, e_last = _pair_terms(cs[:, sl])
            ecl.append(e_last)
            dYpb = dY[:, sl].astype(BF16)
            dMcat = _dot(dYpb, _block_diag(X[:, sl]).astype(BF16), NT)
            Mcat = jnp.concatenate([(CBm * L).astype(BF16) for L in Ls], axis=1)
            dXt = _dot(Mcat, dYpb, TN)
            dXs.append(jnp.where(lane < HEADDIM, dXt[:CHUNK], dXt[CHUNK:]))
            colacc = jnp.zeros((CHUNK, CHUNK), F32)
            rowacc = jnp.zeros((CHUNK, CHUNK), F32)
            for k in range(2):
                dG = dMcat[:, k * CHUNK:(k + 1) * CHUNK] * Ls[k]
                dCB = dCB + dG
                Q = dG * CBm
                colacc = colacc + jnp.where(lane == k * HEADDIM, jnp.sum(Q, axis=1, keepdims=True), 0.0)
                rowacc = rowacc + jnp.where(sub == k * HEADDIM, jnp.sum(Q, axis=0, keepdims=True), 0.0)
            dcss.append(colacc - rowacc.T)
        dX = dX + jnp.concatenate(dXs, axis=1)
        dcs = dcs + jnp.concatenate(dcss, axis=1)
        dCBb = dCB.astype(BF16)
        dc_ref[...] = dC + _dot(dCBb, Bb)
        db_ref[...] = dB + _dot(dCBb, Cb, TN)
        dxs_ref[...] = dX * dt + dY * d_ref[...]
        ddt_ref[...] = dX * xs
        dcs_ref[...] = dcs
        dd_ref[0] = jnp.sum(dY * xs, axis=0, keepdims=True)
        dst[g] = dS * jnp.concatenate(ecl, axis=0) + dP_off

    p_blk = pl.BlockSpec((1, 1, GROUP_W, NSTATE), lambda c, g: (nc - 1 - c, g, 0, 0))
    pn_blk = pl.BlockSpec((1, 1, GROUP_W, NSTATE), lambda c, g: (jnp.minimum(nc - c, nc - 1), g, 0, 0))
    st_blk = pl.BlockSpec((CHUNK, NSTATE), lambda c, g: (nc - 1 - c, g))
    outs, carried = _call(
        body, grid=(nc, NGROUPS),
        in_specs=[g_blk(rev), b_blk(rev), c_blk(rev), g_blk(rev), g_blk(rev), pl.BlockSpec((1, GROUP_W), lambda c, g: (0, g)),
                  p_blk, pn_blk, g_blk(rev)],
        out_specs=[g_blk(rev), st_blk, st_blk, g_blk(rev), g_blk(rev), pl.BlockSpec((1, 1, GROUP_W), lambda c, g: (nc - 1 - c, 0, g))],
        out_shape=[_sds((T, n_inner), F32), _sds((T, NGROUPS * NSTATE), F32), _sds((T, NGROUPS * NSTATE), F32),
                   _sds((T, n_inner), F32), _sds((T, n_inner), F32), _sds((nc, 1, n_inner), F32)],
        scratch=[pltpu.VMEM((NGROUPS, GROUP_W, NSTATE), F32)],
        args=[xbc, xbc, xbc, dt_e, cs_e, d_e, states, states, dy], name=name, sem=("arbitrary", "arbitrary"), comm=comm)
    return outs if comm is None else (outs, carried)


def _ssd_post(ddt_e, dcs_e, dd_p, dt_raw, dt_bias, a_log, n_heads, name):
    T, n_inner = ddt_e.shape

    def body(ddt_ref, dcs_ref, dd_ref, r_ref, b_ref, al_ref, draw_ref, dbias_ref, dal_ref, ddsk_ref):
        @pl.when(pl.program_id(0) == 0)
        def _():
            dbias_ref[...] = jnp.zeros_like(dbias_ref)
            dal_ref[...] = jnp.zeros_like(dal_ref)
            ddsk_ref[...] = jnp.zeros_like(ddsk_ref)

        ex = _head_expand(n_inner)
        red = lambda v: sum(_dot(p, ex, NT) for p in _split3(v))
        raw = r_ref[...] + b_ref[...]
        dt = _softplus(raw)
        A = -jnp.exp(al_ref[...])
        i = lax.broadcasted_iota(jnp.int32, (CHUNK, CHUNK), 0)
        j = lax.broadcasted_iota(jnp.int32, (CHUNK, CHUNK), 1)
        upper = (j >= i).astype(BF16)
        da = sum(_dot(upper, p) for p in _split3(red(dcs_ref[...])))
        ddt = red(ddt_ref[...]) + da * A
        lane = lax.broadcasted_iota(jnp.int32, (CHUNK, LANES), 1)
        draw = jnp.where(lane < n_heads, ddt * jax.nn.sigmoid(raw), 0.0)
        draw_ref[...] = draw.astype(BF16)
        dbias_ref[...] += jnp.sum(draw, axis=0, keepdims=True)
        dal_ref[...] += jnp.sum(da * dt, axis=0, keepdims=True) * A
        ddsk_ref[...] += red(jnp.broadcast_to(dd_ref[0], (8, n_inner)))[0:1, :]

    wide = pl.BlockSpec((CHUNK, n_inner), lambda c: (c, 0))
    blk = pl.BlockSpec((CHUNK, LANES), lambda c: (c, 0))
    return pl.pallas_call(
        body, grid=(T // CHUNK,),
        in_specs=[wide, wide, pl.BlockSpec((1, 1, n_inner), lambda c: (c, 0, 0)), blk, _vec(LANES), _vec(LANES)],
        out_specs=[blk, _vec(LANES), _vec(LANES), _vec(LANES)],
        out_shape=[_sds((T, LANES), BF16)] + [_sds((1, LANES), F32)] * 3,
        name=name, compiler_params=_params("arbitrary"))(ddt_e, dcs_e, dd_p, dt_raw, dt_bias, a_log)


def _row2(v):
    return v.reshape(1, -1).astype(F32)


def _pad_lanes(v):
    return jnp.pad(_row2(v), ((0, 0), (0, LANES - v.shape[-1])))


class _NoExchange:
    def __init__(self, W):
        self.W, self.grads = W, {}

    def weight(self, k):
        return self.W[k]

    def carry(self, name):
        return None

    def carried(self, name, outs):
        pass

    def grad(self, k, g):
        self.grads[k] = g


def _local_step(x, tgt, S, small):
    T, D = x.shape

    def mm(a, b, *, name, **kw):
        comm = S.carry(name)
        if comm is None:
            return _mm(a, b, name=name, **kw)
        res, outs = _mm(a, b, name=name, comm=comm, **kw)
        S.carried(name, outs)
        return res

    def carrying(fn, *args, name):
        comm = S.carry(name)
        if comm is None:
            return fn(*args, name)
        res, outs = fn(*args, name, comm=comm)
        S.carried(name, outs)
        return res

    n_inner = 2 * D
    n_heads = n_inner // HEADDIM
    norm_mix, norm_mlp, norm_final = _row2(small["norm_mix"]), _row2(small["norm_mlp"]), _row2(small["norm_final"])
    b_gate, ssm_b, ssm_norm_w = _row2(small["b_gate"]), _row2(small["ssm_conv_b"]), _row2(small["ssm_norm_w"])
    dt_bias, a_log = _pad_lanes(small["dt_bias"]), _pad_lanes(small["A_log"])
    d_e = jnp.repeat(small["D_skip"].astype(F32), HEADDIM).reshape(1, n_inner)
    sc_w, ssm_w = small["sc_conv_w"], small["ssm_conv_w"]

    hb = _rms_fwd(x, norm_mix, "rms_mix")
    p_xbc = mm(hb, S.weight("xbc"), mode="nn", name="proj_xbc")
    p_dt = mm(hb, S.weight("dt"), mode="nn", name="proj_dt")
    p_z = mm(hb, S.weight("z"), mode="nn", name="proj_z")
    p_sc = mm(hb, S.weight("sc"), mode="nn", name="proj_sc")
    p_gate = mm(hb, S.weight("gate"), mode="nn", name="proj_gate")
    xbc = _ssm_conv_fwd(p_xbc, ssm_w, ssm_b, "ssm_conv_fwd")
    dt_e, cs_e = _ssd_prep(p_dt, dt_bias, a_log, n_inner, "ssd_prep")
    y, states = carrying(_ssd_fwd, xbc, dt_e, cs_e, d_e, name="ssd_fwd")
    yb = _gnorm_fwd(y, p_z, ssm_norm_w, "gnorm_fwd")
    ya = _sc_fwd(p_sc, sc_w, "sc_fwd")
    br_a = mm(ya, S.weight("bsc"), mode="nn", name="branch_sc")
    br_b = mm(yb, S.weight("bssm"), mode="nn", name="branch_ssm")
    merged = _merge_fwd(p_gate, b_gate, br_a, br_b, "merge_fwd")
    x1 = mm(merged, S.weight("out"), mode="nn", name="out_proj", extras=(x,), epi=_epi_add)
    h2 = _rms_fwd(x1, norm_mlp, "rms_mlp")
    a_pre, r_act = mm(h2, S.weight("w1"), mode="nn", name="mlp_up", epi=_epi_relu2, out_dtypes=(F32, BF16))
    x2 = mm(r_act, S.weight("w2"), mode="nn", name="mlp_down", extras=(x1,), epi=_epi_add)
    dx2, dx2b, g_norm_final, loss_row = _final(x2, norm_final, tgt, "final")

    S.grad("w2", mm(r_act, dx2b, mode="tn", name="mlp_down_dw", out_dtypes=(BF16,)))
    da = mm(dx2b, S.weight("w2"), mode="nt", name="mlp_down_dx", extras=(a_pre,), epi=_epi_relu2_bwd, out_dtypes=(BF16,))
    S.grad("w1", mm(h2, da, mode="tn", name="mlp_up_dw", out_dtypes=(BF16,)))
    dh2 = mm(da, S.weight("w1"), mode="nt", name="mlp_up_dx")
    dx1, dx1b, g_norm_mlp = _rms_bwd(x1, norm_mlp, dh2, dx2, "rms_mlp_bwd")
    S.grad("out", mm(merged, dx1b, mode="tn", name="out_proj_dw", out_dtypes=(BF16,)))
    dmerged = mm(dx1b, S.weight("out"), mode="nt", name="out_proj_dx")
    dbr_a, dbr_b, d_gate, g_b_gate = _merge_bwd(dmerged, p_gate, b_gate, br_a, br_b, "merge_bwd")
    S.grad("bssm", mm(yb, dbr_b, mode="tn", name="branch_ssm_dw", out_dtypes=(BF16,)))
    S.grad("bsc", mm(ya, dbr_a, mode="tn", name="branch_sc_dw", out_dtypes=(BF16,)))
    dyb = mm(dbr_b, S.weight("bssm"), mode="nt", name="branch_ssm_dx")
    dya = mm(dbr_a, S.weight("bsc"), mode="nt", name="branch_sc_dx")
    dy, d_z, g_ssm_norm_w = _gnorm_bwd(y, p_z, ssm_norm_w, dyb, "gnorm_bwd")
    dxs, dB, dC, ddt_e, dcs_e, dd_p = carrying(_ssd_bwd, xbc, dt_e, cs_e, d_e, states, dy, name="ssd_bwd")
    d_dt, g_dt_bias, g_a_log, g_d_skip = _ssd_post(ddt_e, dcs_e, dd_p, p_dt, dt_bias, a_log, n_heads, "ssd_post")
    d_xbc, g_ssm_w, g_ssm_b = carrying(_ssm_conv_bwd, p_xbc, ssm_w, ssm_b, dxs, dB, dC, name="ssm_conv_bwd")
    d_scB, d_scC, d_scX, g_sc_w = _sc_bwd(p_sc, sc_w, dya, "sc_bwd")
    d_sc = jnp.concatenate([d_scB, d_scC, d_scX], axis=1)
    pieces = [("sc", d_sc), ("z", d_z), ("xbc", d_xbc), ("dt", d_dt), ("gate", d_gate)]
    S.grad("win", {k: mm(hb, d, mode="tn", name="proj_dw_" + k, out_dtypes=(BF16,)) for k, d in pieces})
    dh = mm([d for _, d in pieces], [S.weight(k) for k, _ in pieces], mode="nt", name="proj_dx", tm=256, tn=256)
    grad_x, _, g_norm_mix = _rms_bwd(x, norm_mix, dh, dx1, "rms_mix_bwd")

    g_small = dict(norm_mix=g_norm_mix, b_gate=g_b_gate, sc_conv_w=g_sc_w, ssm_conv_w=g_ssm_w, ssm_conv_b=g_ssm_b,
                   dt_bias=g_dt_bias, A_log=g_a_log, D_skip=g_d_skip, ssm_norm_w=g_ssm_norm_w, norm_mlp=g_norm_mlp,
                   norm_final=g_norm_final, loss=loss_row)
    return grad_x, g_small


class _Place:
    def __init__(self, k=0):
        x, y, c = lax.axis_index("x"), lax.axis_index("y"), lax.axis_index("c")
        self.x = 1 - x if k & 4 else x
        self.y = 1 - y if k & 2 else y
        self.c = 1 - c if k & 1 else c
        self.chip = 2 * self.x + self.y
        self.id = 2 * self.chip + self.c


ICI_PEERS = (2, 4, 6)
SIBLING = (1,)
ALL_PEERS = (1, 2, 3, 4, 5, 6, 7)


class _Comm:
    def __init__(self, arrs, out_shape, ks, src, dst, own=None, aliases=None):
        self.arrs, self.out_shape, self.ks = list(arrs), list(out_shape), tuple(ks)
        self.n = len(self.arrs)
        self.src, self.dst, self.own = src, dst, own
        self.aliases = aliases or {}
        dma = pltpu.SemaphoreType.DMA
        self.scratch = [dma((self.n, len(self.ks))), dma((self.n, len(self.ks))), dma((self.n,))]

    def _copies(self, ins, outs, sems):
        send_sems, recv_sems, local_sems = sems
        me = _Place()
        owns, sends, recvs = [], [], []
        for a in range(self.n):
            if self.own is not None:
                s, d = self.own(a, ins[a], outs[a], me)
                owns.append(pltpu.make_async_copy(s, d, local_sems.at[a]))
            for i, k in enumerate(self.ks):
                peer = _Place(k)
                for sender, lst in ((me, sends), (peer, recvs)):
                    lst.append(pltpu.make_async_remote_copy(
                        src_ref=self.src(a, ins[a], me, peer), dst_ref=self.dst(a, outs[a], sender),
                        send_sem=send_sems.at[a, i], recv_sem=recv_sems.at[a, i],
                        device_id=(peer.x, peer.y, peer.c), device_id_type=MESH))
        return owns, sends, recvs

    def start(self, ins, outs, sems):
        owns, sends, _ = self._copies(ins, outs, sems)
        for cp in owns + sends:
            cp.start()

    def finish(self, ins, outs, sems):
        owns, sends, recvs = self._copies(ins, outs, sems)
        for cp in recvs:
            cp.wait_recv()
        for cp in sends:
            cp.wait_send()
        for cp in owns:
            cp.wait()


def _run_comm(comm, name):
    n = comm.n

    def body(*refs):
        ins, outs, sems = refs[:n], refs[n:2 * n], refs[2 * n:]
        comm.start(ins, outs, sems)
        comm.finish(ins, outs, sems)

    return list(pl.pallas_call(body, in_specs=[ANY] * n, out_specs=[ANY] * n, out_shape=comm.out_shape, scratch_shapes=comm.scratch,
                               input_output_aliases=dict(comm.aliases), name=name)(*comm.arrs))


def _gather_ici(shards):
    return _Comm(shards, [_sds((4, 2) + s.shape, s.dtype) for s in shards], ICI_PEERS,
                 src=lambda a, i, me, p: i, dst=lambda a, o, s: o.at[s.chip, s.c], own=lambda a, i, o, me: (i, o.at[me.chip, me.c]))


def _gather_sibling(bufs):
    return _Comm(bufs, [_sds(b.shape, b.dtype) for b in bufs], SIBLING,
                 src=lambda a, i, me, p: i.at[:, me.c], dst=lambda a, o, s: o.at[:, s.c], aliases={a: a for a in range(len(bufs))})


def _scatter_sibling(parts):
    return _Comm(parts, [_sds((2, 4) + p.shape[2:], p.dtype) for p in parts], SIBLING,
                 src=lambda a, i, me, p: i.at[:, p.c], dst=lambda a, o, s: o.at[s.c], own=lambda a, i, o, me: (i.at[:, me.c], o.at[me.c]))


def _scatter_ici(parts):
    return _Comm(parts, [_sds(p.shape, p.dtype) for p in parts], ICI_PEERS,
                 src=lambda a, i, me, p: i.at[p.chip], dst=lambda a, o, s: o.at[s.chip], own=lambda a, i, o, me: (i.at[me.chip], o.at[me.chip]))


def _gather_all(arrs):
    return _Comm(arrs, [_sds((N_DEV,) + a.shape, a.dtype) for a in arrs], ALL_PEERS,
                 src=lambda a, i, me, p: i, dst=lambda a, o, s: o.at[s.id], own=lambda a, i, o, me: (i, o.at[me.id]))


def _add_halves(buf, name):
    _, n, R, C = buf.shape
    tr = R if R <= 256 else 256
    assert R % tr == 0

    def body(b_ref, o_ref):
        o_ref[0] = (b_ref[0, 0].astype(F32) + b_ref[1, 0].astype(F32)).astype(o_ref.dtype)

    return pl.pallas_call(body, grid=(n, R // tr), in_specs=[pl.BlockSpec((2, 1, tr, C), lambda q, i: (0, q, i, 0))],
                          out_specs=pl.BlockSpec((1, tr, C), lambda q, i: (q, i, 0)), out_shape=_sds((n, R, C), buf.dtype),
                          name=name, compiler_params=_params("parallel", "parallel"))(buf)


def _adam(w, m, v, gparts, name):
    R, C = w.shape
    n = gparts.shape[0]
    tr = R if R <= 256 else 128
    assert R % tr == 0
    c1 = 1.0 / (1.0 - ADAM_B1 ** ADAM_STEP)
    c2 = 1.0 / (1.0 - ADAM_B2 ** ADAM_STEP)

    def body(w_ref, m_ref, v_ref, g_ref, go_ref, d_ref, mo_ref, vo_ref):
        g = g_ref[0].astype(F32)
        for s in range(1, n):
            g = g + g_ref[s].astype(F32)
        mn = ADAM_B1 * m_ref[...] + (1.0 - ADAM_B1) * g
        vn = ADAM_B2 * v_ref[...] + (1.0 - ADAM_B2) * (g * g)
        go_ref[...] = g
        mo_ref[...] = mn
        vo_ref[...] = vn
        d_ref[...] = -ADAM_LR * ((mn * c1) / (jnp.sqrt(vn * c2) + ADAM_EPS) + ADAM_WD * w_ref[...])

    blk = pl.BlockSpec((tr, C), lambda i: (i, 0))
    return pl.pallas_call(
        body, grid=(R // tr,), in_specs=[blk, blk, blk, pl.BlockSpec((n, tr, C), lambda i: (0, i, 0))],
        out_specs=[blk] * 4, out_shape=[_sds((R, C), F32)] * 4, name=name, compiler_params=_params("parallel"))(w, m, v, gparts)


_SMALL_ORDER = ("norm_mix", "b_gate", "sc_conv_w", "ssm_conv_w", "ssm_conv_b", "dt_bias", "A_log", "D_skip", "ssm_norm_w",
                "norm_mlp", "norm_final", "loss")
_REPLICATED = ("norm_mix", "b_gate", "ssm_conv_b", "dt_bias", "A_log", "D_skip", "ssm_norm_w", "norm_mlp", "norm_final")


def _cols_to_slots(g, n):
    R = g.shape[0]
    return jnp.transpose(g.reshape(R, n, g.shape[1] // n), (1, 0, 2))


def _slots_to_cols(g):
    n, R, C = g.shape
    return jnp.transpose(g, (1, 0, 2)).reshape(R, n * C)


def kernel(x, norm_mix, w_in, b_gate, sc_conv_w, ssm_conv_w, ssm_conv_b, dt_bias, A_log, D_skip, ssm_norm_w, w_branch_sc, w_branch_ssm, w_out, norm_mlp, w_mlp1, w_mlp2, norm_final, loss_target, m_norm_mix, m_w_in, m_b_gate, m_sc_conv_w, m_ssm_conv_w, m_ssm_conv_b, m_dt_bias, m_A_log, m_D_skip, m_ssm_norm_w, m_w_branch_sc, m_w_branch_ssm, m_w_out, m_norm_mlp, m_w_mlp1, m_w_mlp2, m_norm_final, v_norm_mix, v_w_in, v_b_gate, v_sc_conv_w, v_ssm_conv_w, v_ssm_conv_b, v_dt_bias, v_A_log, v_D_skip, v_ssm_norm_w, v_w_branch_sc, v_w_branch_ssm, v_w_out, v_norm_mlp, v_w_mlp1, v_w_mlp2, v_norm_final):
    T, D = x.shape[1], x.shape[2]
    n_inner = 2 * D
    n_heads = n_inner // HEADDIM
    n_xbc = n_inner + 2 * NGROUPS * NSTATE
    me = 4 * lax.axis_index("x") + 2 * lax.axis_index("y") + lax.axis_index("c")

    o_z, o_xbc, o_dt, o_gate = 3 * D, 3 * D + n_inner, 3 * D + n_inner + n_xbc, 3 * D + n_inner + n_xbc + n_heads
    by_owner = lambda b: b.reshape((N_DEV,) + b.shape[2:])
    to_owner = lambda g: g.reshape((4, 2) + g.shape[1:])
    rows_of = lambda g: to_owner(g.reshape((N_DEV, g.shape[0] // N_DEV) + g.shape[1:]))
    cols_of = lambda g: to_owner(_cols_to_slots(g, N_DEV))

    class Schedule(_NoExchange):
        late = dict(proj_xbc=("bsc", "bssm", "out"), ssd_fwd=("w1", "w2"))
        shards = dict(bsc=w_branch_sc, bssm=w_branch_ssm, out=w_out, w1=w_mlp1, w2=w_mlp2)
        grad_groups = (("w2", "w1"), ("out", "bssm", "bsc"), ("win",))
        grad_carrier = dict(ssd_bwd=("w2", "w1"), ssm_conv_bwd=("out", "bssm", "bsc"), proj_dx=("win",))

        def __init__(self):
            bufs = _run_comm(_gather_ici([w_in.astype(BF16), sc_conv_w, ssm_conv_w]), "gather_in_ici")
            bufs = _run_comm(_gather_sibling(bufs), "gather_in_sibling")
            win_full = _slots_to_cols(by_owner(bufs[0]))
            self.W = dict(sc=win_full[:, :o_z], z=win_full[:, o_z:o_xbc], xbc=win_full[:, o_xbc:o_dt],
                          dt=jnp.pad(win_full[:, o_dt:o_gate], ((0, 0), (0, LANES - n_heads))), gate=win_full[:, o_gate:])
            self.taps = dict(sc_conv_w=_slots_to_cols(by_owner(bufs[1])), ssm_conv_w=_slots_to_cols(by_owner(bufs[2])))
            self.staged, self.grads, self.halves, self.summed = {}, {}, {}, {}

        def carry(self, name):
            if name in self.late:
                return _gather_ici([self.shards[k].astype(BF16) for k in self.late[name]])
            if name in self.grad_carrier:
                return _scatter_ici([self.halves[k] for k in self.grad_carrier[name]])
            return None

        def carried(self, name, outs):
            if name in self.late:
                self.staged[self.late[name]] = outs
            else:
                self.summed.update(zip(self.grad_carrier[name], outs))

        def weight(self, k):
            if k not in self.W:
                group = next(g for g in self.staged if k in g)
                bufs = _run_comm(_gather_sibling(self.staged.pop(group)), "gather_sibling_" + group[0])
                for kk, b in zip(group, bufs):
                    full = by_owner(b)
                    self.W[kk] = _slots_to_cols(full) if kk == "w1" else full.reshape(-1, D)
            return self.W[k]

        def grad(self, k, g):
            if k == "win":
                g = cols_of(jnp.concatenate([g["sc"], g["z"], g["xbc"], g["dt"][:, :n_heads], g["gate"]], axis=1))
            else:
                g = cols_of(g) if k == "w1" else rows_of(g)
            self.grads[k] = g
            group = next(gr for gr in self.grad_groups if k in gr)
            if all(kk in self.grads for kk in group):
                bufs = _run_comm(_scatter_sibling([self.grads[kk] for kk in group]), "scatter_sibling_" + group[0])
                for kk, b in zip(group, bufs):
                    self.halves[kk] = _add_halves(b, "add_halves_" + kk)

    S = Schedule()
    small = dict(norm_mix=norm_mix, b_gate=b_gate, ssm_conv_b=ssm_conv_b, dt_bias=dt_bias, A_log=A_log, D_skip=D_skip,
                 ssm_norm_w=ssm_norm_w, norm_mlp=norm_mlp, norm_final=norm_final, **S.taps)
    grad_x, g_small = _local_step(x.reshape(T, D), loss_target.reshape(T, D), S, small)

    small_flat = jnp.concatenate([g_small[k].reshape(-1) for k in _SMALL_ORDER])
    n_small = small_flat.shape[0]
    rows = -(-n_small // (8 * LANES)) * 8
    small_pack = jnp.pad(small_flat, (0, rows * LANES - n_small)).reshape(rows, LANES)
    small_parts = _run_comm(_gather_all([small_pack]), "gather_small")[0]

    res = {}
    big = [("w_in", "win", w_in, m_w_in, v_w_in), ("w_branch_sc", "bsc", w_branch_sc, m_w_branch_sc, v_w_branch_sc),
           ("w_branch_ssm", "bssm", w_branch_ssm, m_w_branch_ssm, v_w_branch_ssm), ("w_out", "out", w_out, m_w_out, v_w_out),
           ("w_mlp1", "w1", w_mlp1, m_w_mlp1, v_w_mlp1), ("w_mlp2", "w2", w_mlp2, m_w_mlp2, v_w_mlp2)]
    for k, gk, w, m, v in big:
        res[k] = _adam(w, m, v, S.summed[gk], "adam_" + k)

    sizes = {k: g_small[k].size for k in _SMALL_ORDER}
    offs, o = {}, 0
    for k in _SMALL_ORDER:
        offs[k] = o
        o += sizes[k]
    rep_w = dict(norm_mix=norm_mix, b_gate=b_gate, ssm_conv_b=ssm_conv_b, dt_bias=dt_bias, A_log=A_log, D_skip=D_skip,
                 ssm_norm_w=ssm_norm_w, norm_mlp=norm_mlp, norm_final=norm_final)
    rep_m = dict(norm_mix=m_norm_mix, b_gate=m_b_gate, ssm_conv_b=m_ssm_conv_b, dt_bias=m_dt_bias, A_log=m_A_log, D_skip=m_D_skip,
                 ssm_norm_w=m_ssm_norm_w, norm_mlp=m_norm_mlp, norm_final=m_norm_final)
    rep_v = dict(norm_mix=v_norm_mix, b_gate=v_b_gate, ssm_conv_b=v_ssm_conv_b, dt_bias=v_dt_bias, A_log=v_A_log, D_skip=v_D_skip,
                 ssm_norm_w=v_ssm_norm_w, norm_mlp=v_norm_mlp, norm_final=v_norm_final)

    def pack(d):
        flat = jnp.zeros((rows * LANES,), F32)
        for k in _REPLICATED:
            flat = lax.dynamic_update_slice(flat, d[k].astype(F32).reshape(-1), (offs[k],))
        return flat.reshape(rows, LANES)

    sm = _adam(pack(rep_w), pack(rep_m), pack(rep_v), small_parts, "adam_small")
    sm = [s.reshape(-1) for s in sm]
    for k in _REPLICATED:
        n_k = rep_w[k].shape[0]
        res[k] = tuple(s[offs[k]:offs[k] + n_k] for s in sm)
    loss = sm[0][offs["loss"]]
    for k, w, m, v, K, full in (("sc_conv_w", sc_conv_w, m_sc_conv_w, v_sc_conv_w, SC_K, D),
                                ("ssm_conv_w", ssm_conv_w, m_ssm_conv_w, v_ssm_conv_w, SSM_K, n_xbc)):
        g_full = sm[0][offs[k]:offs[k] + K * full].reshape(K, full)
        cw = full // N_DEV
        g_mine = lax.dynamic_slice_in_dim(g_full, me * cw, cw, axis=1)
        res[k] = _adam(w, m, v, g_mine[None], "adam_" + k)

    order = ("norm_mix", "w_in", "b_gate", "sc_conv_w", "ssm_conv_w", "ssm_conv_b", "dt_bias", "A_log", "D_skip", "ssm_norm_w",
             "w_branch_sc", "w_branch_ssm", "w_out", "norm_mlp", "w_mlp1", "w_mlp2", "norm_final")
    outs = [loss, grad_x.reshape(1, T, D)]
    for j in range(4):
        outs += [res[k][j] for k in order]
    return tuple(outs)
```

```python
import functools

import jax
import jax.numpy as jnp
from jax import lax
from jax.experimental import pallas as pl
from jax.experimental.pallas import tpu as pltpu

F32 = jnp.float32
BF16 = jnp.bfloat16

EPS = 1e-6
N_DEV = 8
HEADDIM = 64
NSTATE = 128
CHUNK = 128
NGROUPS = 8
GROUP_W = 256
SC_K = 3
SSM_K = 4
LANES = 128

ADAM_LR = 0.001
ADAM_B1 = 0.9
ADAM_B2 = 0.999
ADAM_EPS = 1e-08
ADAM_WD = 0.01
ADAM_STEP = 10

NN = (((1,), (0,)), ((), ()))
NT = (((1,), (1,)), ((), ()))
TN = (((0,), (0,)), ((), ()))
_DIMS = {"nn": NN, "nt": NT, "tn": TN}

ANY = pl.BlockSpec(memory_space=pl.ANY)
MESH = pl.DeviceIdType.MESH


def _sds(shape, dtype):
    return jax.ShapeDtypeStruct(tuple(shape), dtype)


def _dot(a, b, dims=NN):
    return lax.dot_general(a, b, dims, preferred_element_type=F32)


def _dot3(a, b, dims=NN):
    return lax.dot_general(a, b, dims, preferred_element_type=F32, precision=lax.Precision.HIGH)


def _params(*sem):
    return pltpu.CompilerParams(dimension_semantics=tuple(sem))


def _call(body, *, grid, in_specs, out_specs, out_shape, args, name, sem, scratch=(), comm=None):
    if comm is None:
        outs = pl.pallas_call(body, grid=grid, in_specs=list(in_specs), out_specs=list(out_specs), out_shape=list(out_shape),
                              scratch_shapes=list(scratch), name=name, compiler_params=_params(*sem))(*args)
        return list(outs), None
    n, n_in, n_out, n_scr = comm.n, len(in_specs), len(out_shape), len(scratch)

    def wrapped(*refs):
        ins, c_in = refs[:n_in], refs[n_in:n_in + n]
        outs, c_out = refs[n_in + n:n_in + n + n_out], refs[n_in + n + n_out:n_in + 2 * n + n_out]
        rest = refs[n_in + 2 * n + n_out:]
        scr, sems = rest[:n_scr], rest[n_scr:]
        first, last = None, None
        for d, g in enumerate(grid):
            f, l = pl.program_id(d) == 0, pl.program_id(d) == g - 1
            first, last = (f, l) if first is None else (first & f, last & l)

        @pl.when(first)
        def _():
            comm.start(c_in, c_out, sems)

        body(*ins, *outs, *scr)

        @pl.when(last)
        def _():
            comm.finish(c_in, c_out, sems)

    outs = pl.pallas_call(
        wrapped, grid=grid, in_specs=list(in_specs) + [ANY] * n, out_specs=list(out_specs) + [ANY] * n,
        out_shape=list(out_shape) + comm.out_shape, scratch_shapes=list(scratch) + comm.scratch,
        input_output_aliases={n_in + i: n_out + o for i, o in comm.aliases.items()},
        name=name, compiler_params=_params(*["arbitrary"] * len(grid)))(*args, *comm.arrs)
    return list(outs[:n_out]), list(outs[n_out:])


def _mm(a, b, *, mode, name, extras=(), epi=None, out_dtypes=(F32,), tm=512, tn=512, comm=None):
    a_list = list(a) if isinstance(a, (list, tuple)) else [a]
    b_list = list(b) if isinstance(b, (list, tuple)) else [b]
    if mode == "nn":
        M, N = a_list[0].shape[0], b_list[0].shape[1]
    elif mode == "nt":
        M, N = a_list[0].shape[0], b_list[0].shape[0]
    else:
        M, N = a_list[0].shape[1], b_list[0].shape[1]
    tm, tn = min(tm, M), min(tn, N)
    assert M % tm == 0 and N % tn == 0
    a_specs, b_specs = [], []
    for av, bv in zip(a_list, b_list):
        K = av.shape[0] if mode == "tn" else av.shape[1]
        a_specs.append(pl.BlockSpec((K, tm), lambda i, j: (0, i)) if mode == "tn" else pl.BlockSpec((tm, K), lambda i, j: (i, 0)))
        b_specs.append(pl.BlockSpec((tn, K), lambda i, j: (j, 0)) if mode == "nt" else pl.BlockSpec((K, tn), lambda i, j: (0, j)))
    mn_spec = pl.BlockSpec((tm, tn), lambda i, j: (i, j))
    n_p, n_ex = len(a_list), len(extras)
    dims = _DIMS[mode]

    def body(*refs):
        acc = _dot(refs[0][...], refs[n_p][...], dims)
        for p in range(1, n_p):
            acc = acc + _dot(refs[p][...], refs[n_p + p][...], dims)
        rest = refs[2 * n_p:]
        res = (acc,) if epi is None else epi(acc, *[r[...] for r in rest[:n_ex]])
        for o_ref, r in zip(rest[n_ex:], res):
            o_ref[...] = r.astype(o_ref.dtype)

    outs, carried = _call(
        body, grid=(M // tm, N // tn), in_specs=a_specs + b_specs + [mn_spec] * n_ex,
        out_specs=[mn_spec] * len(out_dtypes), out_shape=[_sds((M, N), d) for d in out_dtypes],
        args=a_list + b_list + list(extras), name=name, sem=("parallel", "parallel"), comm=comm)
    res = outs[0] if len(outs) == 1 else outs
    return res if comm is None else (res, carried)


def _epi_add(acc, r):
    return (acc + r,)


def _epi_add2(acc, r):
    s = acc + r
    return (s, s)


def _epi_relu2(acc):
    p = jnp.maximum(acc, 0.0)
    return (acc, p * p)


def _epi_relu2_bwd(acc, a):
    return (acc * (2.0 * jnp.maximum(a, 0.0)),)


def _row(tr, n):
    return pl.BlockSpec((tr, n), lambda i: (i, 0))


def _vec(n):
    return pl.BlockSpec((1, n), lambda i: (0, 0))


def _rms_fwd(x, w, name):
    T, D = x.shape
    tr = min(256, T)

    def body(x_ref, w_ref, o_ref):
        xv = x_ref[...]
        r = lax.rsqrt(jnp.mean(xv * xv, axis=-1, keepdims=True) + EPS)
        o_ref[...] = (xv * r * w_ref[...]).astype(BF16)

    return pl.pallas_call(body, grid=(T // tr,), in_specs=[_row(tr, D), _vec(D)], out_specs=_row(tr, D),
                          out_shape=_sds((T, D), BF16), name=name, compiler_params=_params("parallel"))(x, w)


def _rms_bwd(x, w, dh, dres, name):
    T, D = x.shape
    tr = min(256, T)

    def body(x_ref, w_ref, dh_ref, dres_ref, dx_ref, dxb_ref, dw_ref):
        @pl.when(pl.program_id(0) == 0)
        def _():
            dw_ref[...] = jnp.zeros_like(dw_ref)

        xv = x_ref[...]
        r = lax.rsqrt(jnp.mean(xv * xv, axis=-1, keepdims=True) + EPS)
        xh = xv * r
        dh_v = dh_ref[...]
        dw_ref[...] += jnp.sum(dh_v * xh, axis=0, keepdims=True)
        dxh = dh_v * w_ref[...]
        dx = r * (dxh - xh * jnp.mean(dxh * xh, axis=-1, keepdims=True)) + dres_ref[...]
        dx_ref[...] = dx
        dxb_ref[...] = dx.astype(BF16)

    return pl.pallas_call(
        body, grid=(T // tr,), in_specs=[_row(tr, D), _vec(D), _row(tr, D), _row(tr, D)],
        out_specs=[_row(tr, D), _row(tr, D), _vec(D)],
        out_shape=[_sds((T, D), F32), _sds((T, D), BF16), _sds((1, D), F32)],
        name=name, compiler_params=_params("arbitrary"))(x, w, dh, dres)


def _final(x2, w, tgt, name):
    T, D = x2.shape
    tr = min(256, T)

    def body(x_ref, w_ref, t_ref, dx_ref, dxb_ref, dw_ref, loss_ref):
        @pl.when(pl.program_id(0) == 0)
        def _():
            dw_ref[...] = jnp.zeros_like(dw_ref)
            loss_ref[...] = jnp.zeros_like(loss_ref)

        xv = x_ref[...]
        wv = w_ref[...]
        r = lax.rsqrt(jnp.mean(xv * xv, axis=-1, keepdims=True) + EPS)
        xh = xv * r
        err = xh * wv - t_ref[...]
        part = jnp.sum(jnp.sum(err * err, axis=1, keepdims=True), axis=0, keepdims=True) * (0.5 / D)
        loss_ref[...] += jnp.broadcast_to(part, loss_ref.shape)
        dy = err * (1.0 / D)
        dw_ref[...] += jnp.sum(dy * xh, axis=0, keepdims=True)
        dxh = dy * wv
        dx = r * (dxh - xh * jnp.mean(dxh * xh, axis=-1, keepdims=True))
        dx_ref[...] = dx
        dxb_ref[...] = dx.astype(BF16)

    return pl.pallas_call(
        body, grid=(T // tr,), in_specs=[_row(tr, D), _vec(D), _row(tr, D)],
        out_specs=[_row(tr, D), _row(tr, D), _vec(D), _vec(LANES)],
        out_shape=[_sds((T, D), F32), _sds((T, D), BF16), _sds((1, D), F32), _sds((1, LANES), F32)],
        name=name, compiler_params=_params("arbitrary"))(x2, w, tgt)


def _silu_parts(z):
    s = jax.nn.sigmoid(z)
    return z * s, s * (1.0 + z * (1.0 - s))


def _gnorm_fwd(y, z, w, name):
    T, N = y.shape
    tr = min(256, T)

    def body(y_ref, z_ref, w_ref, o_ref):
        for g in range(N // GROUP_W):
            sl = slice(g * GROUP_W, (g + 1) * GROUP_W)
            silu, _ = _silu_parts(z_ref[:, sl])
            yz = y_ref[:, sl] * silu
            r = lax.rsqrt(jnp.mean(yz * yz, axis=-1, keepdims=True) + EPS)
            o_ref[:, sl] = (yz * r * w_ref[:, sl]).astype(BF16)

    return pl.pallas_call(body, grid=(T // tr,), in_specs=[_row(tr, N), _row(tr, N), _vec(N)], out_specs=_row(tr, N),
                          out_shape=_sds((T, N), BF16), name=name, compiler_params=_params("parallel"))(y, z, w)


def _gnorm_bwd(y, z, w, dyb, name):
    T, N = y.shape
    tr = min(256, T)

    def body(y_ref, z_ref, w_ref, d_ref, dy_ref, dz_ref, dw_ref):
        @pl.when(pl.program_id(0) == 0)
        def _():
            dw_ref[...] = jnp.zeros_like(dw_ref)

        for g in range(N // GROUP_W):
            sl = slice(g * GROUP_W, (g + 1) * GROUP_W)
            yv = y_ref[:, sl]
            silu, dsilu = _silu_parts(z_ref[:, sl])
            yz = yv * silu
            r = lax.rsqrt(jnp.mean(yz * yz, axis=-1, keepdims=True) + EPS)
            yzh = yz * r
            d = d_ref[:, sl]
            dw_ref[:, sl] += jnp.sum(d * yzh, axis=0, keepdims=True)
            dyzh = d * w_ref[:, sl]
            dyz = r * (dyzh - yzh * jnp.mean(dyzh * yzh, axis=-1, keepdims=True))
            dy_ref[:, sl] = dyz * silu
            dz_ref[:, sl] = (dyz * yv * dsilu).astype(BF16)

    return pl.pallas_call(
        body, grid=(T // tr,), in_specs=[_row(tr, N), _row(tr, N), _vec(N), _row(tr, N)],
        out_specs=[_row(tr, N), _row(tr, N), _vec(N)],
        out_shape=[_sds((T, N), F32), _sds((T, N), BF16), _sds((1, N), F32)],
        name=name, compiler_params=_params("arbitrary"))(y, z, w, dyb)


def _merge_fwd(gate_raw, b_gate, br_a, br_b, name):
    T, D = br_a.shape
    tr = min(256, T)

    def body(g_ref, bg_ref, a_ref, b_ref, o_ref):
        g = jax.nn.sigmoid(g_ref[...] + bg_ref[...])
        o_ref[...] = (g[:, :D] * a_ref[...] + g[:, D:] * b_ref[...]).astype(BF16)

    return pl.pallas_call(body, grid=(T // tr,), in_specs=[_row(tr, 2 * D), _vec(2 * D), _row(tr, D), _row(tr, D)],
                          out_specs=_row(tr, D), out_shape=_sds((T, D), BF16), name=name,
                          compiler_params=_params("parallel"))(gate_raw, b_gate, br_a, br_b)


def _merge_bwd(dmerged, gate_raw, b_gate, br_a, br_b, name):
    T, D = br_a.shape
    tr = min(256, T)

    def body(d_ref, g_ref, bg_ref, a_ref, b_ref, da_ref, db_ref, dg_ref, dbg_ref):
        @pl.when(pl.program_id(0) == 0)
        def _():
            dbg_ref[...] = jnp.zeros_like(dbg_ref)

        g = jax.nn.sigmoid(g_ref[...] + bg_ref[...])
        d = d_ref[...]
        da_ref[...] = (d * g[:, :D]).astype(BF16)
        db_ref[...] = (d * g[:, D:]).astype(BF16)
        dg = jnp.concatenate([d * a_ref[...], d * b_ref[...]], axis=1) * g * (1.0 - g)
        dg_ref[...] = dg.astype(BF16)
        dbg_ref[...] += jnp.sum(dg, axis=0, keepdims=True)

    return pl.pallas_call(
        body, grid=(T // tr,), in_specs=[_row(tr, D), _row(tr, 2 * D), _vec(2 * D), _row(tr, D), _row(tr, D)],
        out_specs=[_row(tr, D), _row(tr, D), _row(tr, 2 * D), _vec(2 * D)],
        out_shape=[_sds((T, D), BF16), _sds((T, D), BF16), _sds((T, 2 * D), BF16), _sds((1, 2 * D), F32)],
        name=name, compiler_params=_params("arbitrary"))(dmerged, gate_raw, b_gate, br_a, br_b)


def _shift_down(u, s):
    if s == 0:
        return u
    row = lax.broadcasted_iota(jnp.int32, u.shape, 0)
    return jnp.where(row >= s, pltpu.roll(u, s, 0), 0.0)


def _shift_up(u, s):
    if s == 0:
        return u
    n = u.shape[0]
    row = lax.broadcasted_iota(jnp.int32, u.shape, 0)
    return jnp.where(row < n - s, pltpu.roll(u, n - s, 0), 0.0)


def _conv(u, w_ref, K):
    acc = u * w_ref[K - 1:K, :]
    for k in range(K - 1):
        acc = acc + _shift_down(u, K - 1 - k) * w_ref[k:k + 1, :]
    return acc


def _conv_bwd(u, dc, w_ref, dw_ref, K):
    du = dc * w_ref[K - 1:K, :]
    dw_ref[K - 1:K, :] = jnp.sum(dc * u, axis=0, keepdims=True)
    for k in range(K - 1):
        s = K - 1 - k
        dw_ref[k:k + 1, :] = jnp.sum(dc * _shift_down(u, s), axis=0, keepdims=True)
        du = du + _shift_up(dc, s) * w_ref[k:k + 1, :]
    return du


CB_W = 256


def _col(T, j0=0):
    return pl.BlockSpec((T, CB_W), lambda j: (0, j + j0))


def _sc_fwd(psc, w, name):
    T, D = psc.shape[0], psc.shape[1] // 3
    nb = D // CB_W

    def body(b_ref, c_ref, x_ref, w_ref, o_ref):
        o_ref[...] = (b_ref[...] * _conv(c_ref[...] * x_ref[...], w_ref, SC_K)).astype(BF16)

    return pl.pallas_call(
        body, grid=(nb,), in_specs=[_col(T), _col(T, nb), _col(T, 2 * nb), pl.BlockSpec((SC_K, CB_W), lambda j: (0, j))],
        out_specs=_col(T), out_shape=_sds((T, D), BF16), name=name, compiler_params=_params("parallel"))(psc, psc, psc, w)


def _sc_bwd(psc, w, dya, name):
    T, D = psc.shape[0], psc.shape[1] // 3
    nb = D // CB_W

    def body(b_ref, c_ref, x_ref, w_ref, d_ref, db_ref, dc_ref, dx_ref, dw_ref):
        cv, xv, d = c_ref[...], x_ref[...], d_ref[...]
        u = cv * xv
        db_ref[...] = (d * _conv(u, w_ref, SC_K)).astype(BF16)
        du = _conv_bwd(u, d * b_ref[...], w_ref, dw_ref, SC_K)
        dc_ref[...] = (du * xv).astype(BF16)
        dx_ref[...] = (du * cv).astype(BF16)

    wspec = pl.BlockSpec((SC_K, CB_W), lambda j: (0, j))
    return pl.pallas_call(
        body, grid=(nb,), in_specs=[_col(T), _col(T, nb), _col(T, 2 * nb), wspec, _col(T)],
        out_specs=[_col(T), _col(T), _col(T), wspec],
        out_shape=[_sds((T, D), BF16)] * 3 + [_sds((SC_K, D), F32)],
        name=name, compiler_params=_params("parallel"))(psc, psc, psc, w, dya)


def _ssm_conv_fwd(u, w, b, name):
    T, N = u.shape

    def body(u_ref, w_ref, b_ref, o_ref):
        c = _conv(u_ref[...], w_ref, SSM_K) + b_ref[...]
        o_ref[...] = c * jax.nn.sigmoid(c)

    return pl.pallas_call(
        body, grid=(N // CB_W,), in_specs=[_col(T), pl.BlockSpec((SSM_K, CB_W), lambda j: (0, j)), pl.BlockSpec((1, CB_W), lambda j: (0, j))],
        out_specs=_col(T), out_shape=_sds((T, N), F32), name=name, compiler_params=_params("parallel"))(u, w, b)


def _ssm_conv_bwd(u, w, b, dxs, dB, dC, name, comm=None):
    T, N = u.shape
    n_x, n_b = dxs.shape[1] // CB_W, dB.shape[1] // CB_W

    def body(u_ref, w_ref, b_ref, dx_ref, db_ref, dc_ref, du_ref, dw_ref, dbias_ref):
        j = pl.program_id(0)
        uv = u_ref[...]
        c = _conv(uv, w_ref, SSM_K) + b_ref[...]
        _, dsilu = _silu_parts(c)
        d = jnp.where(j < n_x, dx_ref[...], jnp.where(j < n_x + n_b, db_ref[...], dc_ref[...])) * dsilu
        dbias_ref[...] = jnp.sum(d, axis=0, keepdims=True)
        du_ref[...] = _conv_bwd(uv, d, w_ref, dw_ref, SSM_K).astype(BF16)

    wspec = pl.BlockSpec((SSM_K, CB_W), lambda j: (0, j))
    bspec = pl.BlockSpec((1, CB_W), lambda j: (0, j))
    outs, carried = _call(
        body, grid=(N // CB_W,),
        in_specs=[_col(T), wspec, bspec,
                  pl.BlockSpec((T, CB_W), lambda j: (0, jnp.minimum(j, n_x - 1))),
                  pl.BlockSpec((T, CB_W), lambda j: (0, jnp.clip(j - n_x, 0, n_b - 1))),
                  pl.BlockSpec((T, CB_W), lambda j: (0, jnp.clip(j - n_x - n_b, 0, n_b - 1)))],
        out_specs=[_col(T), wspec, bspec],
        out_shape=[_sds((T, N), BF16), _sds((SSM_K, N), F32), _sds((1, N), F32)],
        args=[u, w, b, dxs, dB, dC], name=name, sem=("parallel",), comm=comm)
    return outs if comm is None else (outs, carried)


def _split3(v):
    hi = v.astype(BF16)
    r = v - hi.astype(F32)
    mid = r.astype(BF16)
    lo = (r - mid.astype(F32)).astype(BF16)
    return hi, mid, lo


def _head_expand(n_lanes):
    h = lax.broadcasted_iota(jnp.int32, (LANES, n_lanes), 0)
    l = lax.broadcasted_iota(jnp.int32, (LANES, n_lanes), 1)
    return (jnp.right_shift(l, HEADDIM.bit_length() - 1) == h).astype(BF16)


def _softplus(v):
    return jnp.maximum(v, 0.0) + jnp.log1p(jnp.exp(-jnp.abs(v)))


def _ssd_prep(dt_raw, dt_bias, a_log, n_inner, name):
    T = dt_raw.shape[0]

    def body(r_ref, b_ref, al_ref, dt_ref, cs_ref):
        dt = _softplus(r_ref[...] + b_ref[...])
        a = dt * (-jnp.exp(al_ref[...]))
        i = lax.broadcasted_iota(jnp.int32, (CHUNK, CHUNK), 0)
        j = lax.broadcasted_iota(jnp.int32, (CHUNK, CHUNK), 1)
        tri = (j <= i).astype(BF16)
        cs = sum(_dot(tri, p) for p in _split3(a))
        ex = _head_expand(n_inner)
        dt_ref[...] = sum(_dot(p, ex) for p in _split3(dt))
        cs_ref[...] = sum(_dot(p, ex) for p in _split3(cs))

    blk = pl.BlockSpec((CHUNK, LANES), lambda c: (c, 0))
    out = pl.BlockSpec((CHUNK, n_inner), lambda c: (c, 0))
    return pl.pallas_call(body, grid=(T // CHUNK,), in_specs=[blk, _vec(LANES), _vec(LANES)], out_specs=[out, out],
                          out_shape=[_sds((T, n_inner), F32)] * 2, name=name, compiler_params=_params("parallel"))(dt_raw, dt_bias, a_log)


def _pair_terms(cs_p):
    lane = lax.broadcasted_iota(jnp.int32, (CHUNK, CHUNK), 1)
    sub = lax.broadcasted_iota(jnp.int32, (CHUNK, CHUNK), 0)
    csT = cs_p.T
    Ls = []
    for k in range(2):
        col = jnp.sum(jnp.where(lane == k * HEADDIM, cs_p, 0.0), axis=1, keepdims=True)
        rowv = csT[k * HEADDIM:k * HEADDIM + 1, :]
        Ls.append(jnp.exp(jnp.where(sub >= lane, col - rowv, -jnp.inf)))
    return Ls, jnp.exp(csT[:, CHUNK - 1:CHUNK])


def _block_diag(xp):
    lane = lax.broadcasted_iota(jnp.int32, xp.shape, 1)
    return jnp.concatenate([jnp.where(lane < HEADDIM, xp, 0.0), jnp.where(lane >= HEADDIM, xp, 0.0)], axis=0)


def _ssd_specs(T, n_inner):
    nc = T // CHUNK
    xo, bo, co = 0, n_inner // LANES, n_inner // LANES + NGROUPS
    g_blk = lambda f: pl.BlockSpec((CHUNK, GROUP_W), lambda c, g: (f(c), g))
    return nc, g_blk, (lambda f: pl.BlockSpec((CHUNK, NSTATE), lambda c, g: (f(c), bo + g))), (lambda f: pl.BlockSpec((CHUNK, NSTATE), lambda c, g: (f(c), co + g)))


def _ssd_fwd(xbc, dt_e, cs_e, d_e, name, comm=None):
    T = xbc.shape[0]
    n_inner = dt_e.shape[1]
    nc, g_blk, b_blk, c_blk = _ssd_specs(T, n_inner)
    ident = lambda c: c

    def body(xs_ref, b_ref, c_ref, dt_ref, cs_ref, d_ref, y_ref, p_ref, st):
        c, g = pl.program_id(0), pl.program_id(1)

        @pl.when(c == 0)
        def _():
            st[g] = jnp.zeros((GROUP_W, NSTATE), F32)

        P = st[g]
        p_ref[0, 0] = P
        xs, dt, cs = xs_ref[...], dt_ref[...], cs_ref[...]
        Bf, Cf = b_ref[...], c_ref[...]
        CBm = _dot3(Cf, Bf, NT)
        X = xs * dt
        decay = jnp.exp(cs[CHUNK - 1:CHUNK, :] - cs)
        y_off = _dot3(Cf, P, NT) * jnp.exp(cs)
        ys, ecl = [], []
        for pr in range(2):
            sl = slice(pr * LANES, (pr + 1) * LANES)
            Ls, e_last = _pair_terms(cs[:, sl])
            ecl.append(e_last)
            Mcat = jnp.concatenate([CBm * L for L in Ls], axis=1)
            ys.append(_dot3(Mcat, _block_diag(X[:, sl])))
        y_ref[...] = jnp.concatenate(ys, axis=1) + y_off + xs * d_ref[...]
        S = _dot3(X * decay, Bf, TN)
        st[g] = P * jnp.concatenate(ecl, axis=0) + S

    p_blk = pl.BlockSpec((1, 1, GROUP_W, NSTATE), lambda c, g: (c, g, 0, 0))
    outs, carried = _call(
        body, grid=(nc, NGROUPS),
        in_specs=[g_blk(ident), b_blk(ident), c_blk(ident), g_blk(ident), g_blk(ident), pl.BlockSpec((1, GROUP_W), lambda c, g: (0, g))],
        out_specs=[g_blk(ident), p_blk],
        out_shape=[_sds((T, n_inner), F32), _sds((nc, NGROUPS, GROUP_W, NSTATE), F32)],
        scratch=[pltpu.VMEM((NGROUPS, GROUP_W, NSTATE), F32)],
        args=[xbc, xbc, xbc, dt_e, cs_e, d_e], name=name, sem=("arbitrary", "arbitrary"), comm=comm)
    return outs if comm is None else (outs, carried)


def _ssd_bwd(xbc, dt_e, cs_e, d_e, states, dy, name, comm=None):
    T = xbc.shape[0]
    n_inner = dt_e.shape[1]
    nc, g_blk, b_blk, c_blk = _ssd_specs(T, n_inner)
    rev = lambda c: nc - 1 - c

    def body(xs_ref, b_ref, c_ref, dt_ref, cs_ref, d_ref, p_ref, pn_ref, dy_ref,
             dxs_ref, db_ref, dc_ref, ddt_ref, dcs_ref, dd_ref, dst):
        cc, g = pl.program_id(0), pl.program_id(1)

        @pl.when(cc == 0)
        def _():
            dst[g] = jnp.zeros((GROUP_W, NSTATE), F32)

        dS = dst[g]
        P, Pn = p_ref[0, 0], pn_ref[0, 0]
        xs, dt, cs, dY = xs_ref[...], dt_ref[...], cs_ref[...], dy_ref[...]
        Bf, Cf = b_ref[...], c_ref[...]
        Bb, Cb = Bf.astype(BF16), Cf.astype(BF16)
        X = xs * dt
        ecs = jnp.exp(cs)
        decay = jnp.exp(cs[CHUNK - 1:CHUNK, :] - cs)
        CBm = _dot3(Cf, Bf, NT)
        dYe = dY * ecs
        dP_off = _dot3(dYe, Cf, TN)
        dC = _dot(dYe.astype(BF16), P.astype(BF16))
        dcs = dYe * _dot3(Cf, P, NT)
        Xd = X * decay
        dB = _dot(Xd.astype(BF16), dS.astype(BF16))
        E = _dot3(Bf, dS, NT)
        dX = E * decay
        dcs = dcs - E * Xd
        R = _dot3(jnp.ones((8, NSTATE), F32), dS * Pn, NT)
        sub_g = lax.broadcasted_iota(jnp.int32, (CHUNK, GROUP_W), 0)
        dcs = dcs + jnp.where(sub_g == CHUNK - 1, R[0:1, :], 0.0)
        lane = lax.broadcasted_iota(jnp.int32, (CHUNK, CHUNK), 1)
        sub = lax.broadcasted_iota(jnp.int32, (CHUNK, CHUNK), 0)
        dCB = jnp.zeros((CHUNK, CHUNK), F32)
        dXs, dcss, ecl = [], [], []
        for pr in range(2):
            sl = slice(pr * LANES, (pr + 1) * LANES)
            Ls, e_last = _pair_terms(cs[:, sl])
            ecl.append(e_last)
            dYp = dY[:, sl]
            dMcat = _dot3(dYp, _block_diag(X[:, sl]), NT)
            Mcat = jnp.concatenate([CBm * L for L in Ls], axis=1)
            dXt = _dot3(Mcat, dYp, TN)
            dXs.append(jnp.where(lane < HEADDIM, dXt[:CHUNK], dXt[CHUNK:]))
            colacc = jnp.zeros((CHUNK, CHUNK), F32)
            rowacc = jnp.zeros((CHUNK, CHUNK), F32)
            for k in range(2):
                dG = dMcat[:, k * CHUNK:(k + 1) * CHUNK] * Ls[k]
                dCB = dCB + dG
                Q = dG * CBm
                colacc = colacc + jnp.where(lane == k * HEADDIM, jnp.sum(Q, axis=1, keepdims=True), 0.0)
                rowacc = rowacc + jnp.where(sub == k * HEADDIM, jnp.sum(Q, axis=0, keepdims=True), 0.0)
            dcss.append(colacc - rowacc.T)
        dX = dX + jnp.concatenate(dXs, axis=1)
        dcs = dcs + jnp.concatenate(dcss, axis=1)
        dCBb = dCB.astype(BF16)
        dc_ref[...] = dC + _dot(dCBb, Bb)
        db_ref[...] = dB + _dot(dCBb, Cb, TN)
        dxs_ref[...] = dX * dt + dY * d_ref[...]
        ddt_ref[...] = dX * xs
        dcs_ref[...] = dcs
        dd_ref[0] = jnp.sum(dY * xs, axis=0, keepdims=True)
        dst[g] = dS * jnp.concatenate(ecl, axis=0) + dP_off

    p_blk = pl.BlockSpec((1, 1, GROUP_W, NSTATE), lambda c, g: (nc - 1 - c, g, 0, 0))
    pn_blk = pl.BlockSpec((1, 1, GROUP_W, NSTATE), lambda c, g: (jnp.minimum(nc - c, nc - 1), g, 0, 0))
    st_blk = pl.BlockSpec((CHUNK, NSTATE), lambda c, g: (nc - 1 - c, g))
    outs, carried = _call(
        body, grid=(nc, NGROUPS),
        in_specs=[g_blk(rev), b_blk(rev), c_blk(rev), g_blk(rev), g_blk(rev), pl.BlockSpec((1, GROUP_W), lambda c, g: (0, g)),
                  p_blk, pn_blk, g_blk(rev)],
        out_specs=[g_blk(rev), st_blk, st_blk, g_blk(rev), g_blk(rev), pl.BlockSpec((1, 1, GROUP_W), lambda c, g: (nc - 1 - c, 0, g))],
        out_shape=[_sds((T, n_inner), F32), _sds((T, NGROUPS * NSTATE), F32), _sds((T, NGROUPS * NSTATE), F32),
                   _sds((T, n_inner), F32), _sds((T, n_inner), F32), _sds((nc, 1, n_inner), F32)],
        scratch=[pltpu.VMEM((NGROUPS, GROUP_W, NSTATE), F32)],
        args=[xbc, xbc, xbc, dt_e, cs_e, d_e, states, states, dy], name=name, sem=("arbitrary", "arbitrary"), comm=comm)
    return outs if comm is None else (outs, carried)


def _ssd_post(ddt_e, dcs_e, dd_p, dt_raw, dt_bias, a_log, n_heads, name):
    T, n_inner = ddt_e.shape

    def body(ddt_ref, dcs_ref, dd_ref, r_ref, b_ref, al_ref, draw_ref, dbias_ref, dal_ref, ddsk_ref):
        @pl.when(pl.program_id(0) == 0)
        def _():
            dbias_ref[...] = jnp.zeros_like(dbias_ref)
            dal_ref[...] = jnp.zeros_like(dal_ref)
            ddsk_ref[...] = jnp.zeros_like(ddsk_ref)

        ex = _head_expand(n_inner)
        red = lambda v: sum(_dot(p, ex, NT) for p in _split3(v))
        raw = r_ref[...] + b_ref[...]
        dt = _softplus(raw)
        A = -jnp.exp(al_ref[...])
        i = lax.broadcasted_iota(jnp.int32, (CHUNK, CHUNK), 0)
        j = lax.broadcasted_iota(jnp.int32, (CHUNK, CHUNK), 1)
        upper = (j >= i).astype(BF16)
        da = sum(_dot(upper, p) for p in _split3(red(dcs_ref[...])))
        ddt = red(ddt_ref[...]) + da * A
        lane = lax.broadcasted_iota(jnp.int32, (CHUNK, LANES), 1)
        draw = jnp.where(lane < n_heads, ddt * jax.nn.sigmoid(raw), 0.0)
        draw_ref[...] = draw.astype(BF16)
        dbias_ref[...] += jnp.sum(draw, axis=0, keepdims=True)
        dal_ref[...] += jnp.sum(da * dt, axis=0, keepdims=True) * A
        ddsk_ref[...] += red(jnp.broadcast_to(dd_ref[0], (8, n_inner)))[0:1, :]

    wide = pl.BlockSpec((CHUNK, n_inner), lambda c: (c, 0))
    blk = pl.BlockSpec((CHUNK, LANES), lambda c: (c, 0))
    return pl.pallas_call(
        body, grid=(T // CHUNK,),
        in_specs=[wide, wide, pl.BlockSpec((1, 1, n_inner), lambda c: (c, 0, 0)), blk, _vec(LANES), _vec(LANES)],
        out_specs=[blk, _vec(LANES), _vec(LANES), _vec(LANES)],
        out_shape=[_sds((T, LANES), BF16)] + [_sds((1, LANES), F32)] * 3,
        name=name, compiler_params=_params("arbitrary"))(ddt_e, dcs_e, dd_p, dt_raw, dt_bias, a_log)


def _row2(v):
    return v.reshape(1, -1).astype(F32)


def _pad_lanes(v):
    return jnp.pad(_row2(v), ((0, 0), (0, LANES - v.shape[-1])))


class _NoExchange:
    def __init__(self, W):
        self.W, self.grads = W, {}

    def weight(self, k):
        return self.W[k]

    def carry(self, name):
        return None

    def carried(self, name, outs):
        pass

    def grad(self, k, g):
        self.grads[k] = g


def _local_step(x, tgt, S, small):
    T, D = x.shape

    def mm(a, b, *, name, **kw):
        comm = S.carry(name)
        if comm is None:
            return _mm(a, b, name=name, **kw)
        res, outs = _mm(a, b, name=name, comm=comm, **kw)
        S.carried(name, outs)
        return res

    def carrying(fn, *args, name):
        comm = S.carry(name)
        if comm is None:
            return fn(*args, name)
        res, outs = fn(*args, name, comm=comm)
        S.carried(name, outs)
        return res

    n_inner = 2 * D
    n_heads = n_inner // HEADDIM
    norm_mix, norm_mlp, norm_final = _row2(small["norm_mix"]), _row2(small["norm_mlp"]), _row2(small["norm_final"])
    b_gate, ssm_b, ssm_norm_w = _row2(small["b_gate"]), _row2(small["ssm_conv_b"]), _row2(small["ssm_norm_w"])
    dt_bias, a_log = _pad_lanes(small["dt_bias"]), _pad_lanes(small["A_log"])
    d_e = jnp.repeat(small["D_skip"].astype(F32), HEADDIM).reshape(1, n_inner)
    sc_w, ssm_w = small["sc_conv_w"], small["ssm_conv_w"]

    hb = _rms_fwd(x, norm_mix, "rms_mix")
    p_xbc = mm(hb, S.weight("xbc"), mode="nn", name="proj_xbc")
    p_dt = mm(hb, S.weight("dt"), mode="nn", name="proj_dt")
    p_z = mm(hb, S.weight("z"), mode="nn", name="proj_z")
    p_sc = mm(hb, S.weight("sc"), mode="nn", name="proj_sc")
    p_gate = mm(hb, S.weight("gate"), mode="nn", name="proj_gate")
    xbc = _ssm_conv_fwd(p_xbc, ssm_w, ssm_b, "ssm_conv_fwd")
    dt_e, cs_e = _ssd_prep(p_dt, dt_bias, a_log, n_inner, "ssd_prep")
    y, states = carrying(_ssd_fwd, xbc, dt_e, cs_e, d_e, name="ssd_fwd")
    yb = _gnorm_fwd(y, p_z, ssm_norm_w, "gnorm_fwd")
    ya = _sc_fwd(p_sc, sc_w, "sc_fwd")
    br_a = mm(ya, S.weight("bsc"), mode="nn", name="branch_sc")
    br_b = mm(yb, S.weight("bssm"), mode="nn", name="branch_ssm")
    merged = _merge_fwd(p_gate, b_gate, br_a, br_b, "merge_fwd")
    x1 = mm(merged, S.weight("out"), mode="nn", name="out_proj", extras=(x,), epi=_epi_add)
    h2 = _rms_fwd(x1, norm_mlp, "rms_mlp")
    a_pre, r_act = mm(h2, S.weight("w1"), mode="nn", name="mlp_up", epi=_epi_relu2, out_dtypes=(F32, BF16))
    x2 = mm(r_act, S.weight("w2"), mode="nn", name="mlp_down", extras=(x1,), epi=_epi_add)
    dx2, dx2b, g_norm_final, loss_row = _final(x2, norm_final, tgt, "final")

    S.grad("w2", mm(r_act, dx2b, mode="tn", name="mlp_down_dw", out_dtypes=(BF16,)))
    da = mm(dx2b, S.weight("w2"), mode="nt", name="mlp_down_dx", extras=(a_pre,), epi=_epi_relu2_bwd, out_dtypes=(BF16,))
    S.grad("w1", mm(h2, da, mode="tn", name="mlp_up_dw", out_dtypes=(BF16,)))
    dh2 = mm(da, S.weight("w1"), mode="nt", name="mlp_up_dx")
    dx1, dx1b, g_norm_mlp = _rms_bwd(x1, norm_mlp, dh2, dx2, "rms_mlp_bwd")
    S.grad("out", mm(merged, dx1b, mode="tn", name="out_proj_dw", out_dtypes=(BF16,)))
    dmerged = mm(dx1b, S.weight("out"), mode="nt", name="out_proj_dx")
    dbr_a, dbr_b, d_gate, g_b_gate = _merge_bwd(dmerged, p_gate, b_gate, br_a, br_b, "merge_bwd")
    S.grad("bssm", mm(yb, dbr_b, mode="tn", name="branch_ssm_dw", out_dtypes=(BF16,)))
    S.grad("bsc", mm(ya, dbr_a, mode="tn", name="branch_sc_dw", out_dtypes=(BF16,)))
    dyb = mm(dbr_b, S.weight("bssm"), mode="nt", name="branch_ssm_dx")
    dya = mm(dbr_a, S.weight("bsc"), mode="nt", name="branch_sc_dx")
    dy, d_z, g_ssm_norm_w = _gnorm_bwd(y, p_z, ssm_norm_w, dyb, "gnorm_bwd")
    dxs, dB, dC, ddt_e, dcs_e, dd_p = carrying(_ssd_bwd, xbc, dt_e, cs_e, d_e, states, dy, name="ssd_bwd")
    d_dt, g_dt_bias, g_a_log, g_d_skip = _ssd_post(ddt_e, dcs_e, dd_p, p_dt, dt_bias, a_log, n_heads, "ssd_post")
    d_xbc, g_ssm_w, g_ssm_b = carrying(_ssm_conv_bwd, p_xbc, ssm_w, ssm_b, dxs, dB, dC, name="ssm_conv_bwd")
    d_scB, d_scC, d_scX, g_sc_w = _sc_bwd(p_sc, sc_w, dya, "sc_bwd")
    d_sc = jnp.concatenate([d_scB, d_scC, d_scX], axis=1)
    pieces = [("sc", d_sc), ("z", d_z), ("xbc", d_xbc), ("dt", d_dt), ("gate", d_gate)]
    S.grad("win", {k: mm(hb, d, mode="tn", name="proj_dw_" + k, out_dtypes=(BF16,)) for k, d in pieces})
    dh = mm([d for _, d in pieces], [S.weight(k) for k, _ in pieces], mode="nt", name="proj_dx", tm=256, tn=256)
    grad_x, _, g_norm_mix = _rms_bwd(x, norm_mix, dh, dx1, "rms_mix_bwd")

    g_small = dict(norm_mix=g_norm_mix, b_gate=g_b_gate, sc_conv_w=g_sc_w, ssm_conv_w=g_ssm_w, ssm_conv_b=g_ssm_b,
                   dt_bias=g_dt_bias, A_log=g_a_log, D_skip=g_d_skip, ssm_norm_w=g_ssm_norm_w, norm_mlp=g_norm_mlp,
                   norm_final=g_norm_final, loss=loss_row)
    return grad_x, g_small


class _Place:
    def __init__(self, k=0):
        x, y, c = lax.axis_index("x"), lax.axis_index("y"), lax.axis_index("c")
        self.x = 1 - x if k & 4 else x
        self.y = 1 - y if k & 2 else y
        self.c = 1 - c if k & 1 else c
        self.chip = 2 * self.x + self.y
        self.id = 2 * self.chip + self.c


ICI_PEERS = (2, 4, 6)
SIBLING = (1,)
ALL_PEERS = (1, 2, 3, 4, 5, 6, 7)


class _Comm:
    def __init__(self, arrs, out_shape, ks, src, dst, own=None, aliases=None):
        self.arrs, self.out_shape, self.ks = list(arrs), list(out_shape), tuple(ks)
        self.n = len(self.arrs)
        self.src, self.dst, self.own = src, dst, own
        self.aliases = aliases or {}
        dma = pltpu.SemaphoreType.DMA
        self.scratch = [dma((self.n, len(self.ks))), dma((self.n, len(self.ks))), dma((self.n,))]

    def _copies(self, ins, outs, sems, with_recvs):
        send_sems, recv_sems, local_sems = sems
        me = _Place()
        owns, sends, recvs = [], [], []
        for a in range(self.n):
            if self.own is not None:
                s, d = self.own(a, ins[a], outs[a], me)
                owns.append(pltpu.make_async_copy(s, d, local_sems.at[a]))
            for i, k in enumerate(self.ks):
                peer = _Place(k)
                for sender, lst in ((me, sends), (peer, recvs)) if with_recvs else ((me, sends),):
                    lst.append(pltpu.make_async_remote_copy(
                        src_ref=self.src(a, ins[a], me, peer), dst_ref=self.dst(a, outs[a], sender),
                        send_sem=send_sems.at[a, i], recv_sem=recv_sems.at[a, i],
                        device_id=(peer.x, peer.y, peer.c), device_id_type=MESH))
        return owns, sends, recvs

    def start(self, ins, outs, sems):
        owns, sends, _ = self._copies(ins, outs, sems, False)
        for cp in owns + sends:
            cp.start()

    def finish(self, ins, outs, sems):
        owns, sends, recvs = self._copies(ins, outs, sems, True)
        for cp in recvs:
            cp.wait_recv()
        for cp in sends:
            cp.wait_send()
        for cp in owns:
            cp.wait()


def _run_comm(comm, name):
    n = comm.n

    def body(*refs):
        ins, outs, sems = refs[:n], refs[n:2 * n], refs[2 * n:]
        comm.start(ins, outs, sems)
        comm.finish(ins, outs, sems)

    return list(pl.pallas_call(body, in_specs=[ANY] * n, out_specs=[ANY] * n, out_shape=comm.out_shape, scratch_shapes=comm.scratch,
                               input_output_aliases=dict(comm.aliases), name=name)(*comm.arrs))


def _gather_ici(shards):
    return _Comm(shards, [_sds((4, 2) + s.shape, s.dtype) for s in shards], ICI_PEERS,
                 src=lambda a, i, me, p: i, dst=lambda a, o, s: o.at[s.chip, s.c], own=lambda a, i, o, me: (i, o.at[me.chip, me.c]))


def _gather_sibling(bufs):
    return _Comm(bufs, [_sds(b.shape, b.dtype) for b in bufs], SIBLING,
                 src=lambda a, i, me, p: i.at[:, me.c], dst=lambda a, o, s: o.at[:, s.c], aliases={a: a for a in range(len(bufs))})


def _scatter_sibling(parts):
    return _Comm(parts, [_sds((4,) + p.shape[2:], p.dtype) for p in parts], SIBLING,
                 src=lambda a, i, me, p: i.at[:, p.c], dst=lambda a, o, s: o)


def _scatter_ici(parts):
    return _Comm(parts, [_sds(p.shape, p.dtype) for p in parts], ICI_PEERS,
                 src=lambda a, i, me, p: i.at[p.chip], dst=lambda a, o, s: o.at[s.chip], own=lambda a, i, o, me: (i.at[me.chip], o.at[me.chip]))


def _gather_all(arrs):
    return _Comm(arrs, [_sds((N_DEV,) + a.shape, a.dtype) for a in arrs], ALL_PEERS,
                 src=lambda a, i, me, p: i, dst=lambda a, o, s: o.at[s.id], own=lambda a, i, o, me: (i, o.at[me.id]))


def _add_halves(parts, got, name):
    n, _, R, C = parts.shape
    tr = R if R <= 256 else 256
    assert R % tr == 0
    core = lax.axis_index("c").astype(jnp.int32).reshape(1)

    def body(c_ref, p_ref, g_ref, o_ref):
        o_ref[0] = (p_ref[0, 0].astype(F32) + g_ref[0].astype(F32)).astype(o_ref.dtype)

    spec = pltpu.PrefetchScalarGridSpec(
        num_scalar_prefetch=1, grid=(n, R // tr),
        in_specs=[pl.BlockSpec((1, 1, tr, C), lambda q, i, c_ref: (q, c_ref[0], i, 0)), pl.BlockSpec((1, tr, C), lambda q, i, c_ref: (q, i, 0))],
        out_specs=pl.BlockSpec((1, tr, C), lambda q, i, c_ref: (q, i, 0)))
    return pl.pallas_call(body, grid_spec=spec, out_shape=_sds((n, R, C), parts.dtype), name=name,
                          compiler_params=_params("parallel", "parallel"))(core, parts, got)


def _adam(w, m, v, gparts, name):
    R, C = w.shape
    n = gparts.shape[0]
    tr = R if R <= 256 else 128
    assert R % tr == 0
    c1 = 1.0 / (1.0 - ADAM_B1 ** ADAM_STEP)
    c2 = 1.0 / (1.0 - ADAM_B2 ** ADAM_STEP)

    def body(w_ref, m_ref, v_ref, g_ref, go_ref, d_ref, mo_ref, vo_ref):
        g = g_ref[0].astype(F32)
        for s in range(1, n):
            g = g + g_ref[s].astype(F32)
        mn = ADAM_B1 * m_ref[...] + (1.0 - ADAM_B1) * g
        vn = ADAM_B2 * v_ref[...] + (1.0 - ADAM_B2) * (g * g)
        go_ref[...] = g
        mo_ref[...] = mn
        vo_ref[...] = vn
        d_ref[...] = -ADAM_LR * ((mn * c1) / (jnp.sqrt(vn * c2) + ADAM_EPS) + ADAM_WD * w_ref[...])

    blk = pl.BlockSpec((tr, C), lambda i: (i, 0))
    return pl.pallas_call(
        body, grid=(R // tr,), in_specs=[blk, blk, blk, pl.BlockSpec((n, tr, C), lambda i: (0, i, 0))],
        out_specs=[blk] * 4, out_shape=[_sds((R, C), F32)] * 4, name=name, compiler_params=_params("parallel"))(w, m, v, gparts)


_SMALL_ORDER = ("norm_mix", "b_gate", "sc_conv_w", "ssm_conv_w", "ssm_conv_b", "dt_bias", "A_log", "D_skip", "ssm_norm_w",
                "norm_mlp", "norm_final", "loss")
_REPLICATED = ("norm_mix", "b_gate", "ssm_conv_b", "dt_bias", "A_log", "D_skip", "ssm_norm_w", "norm_mlp", "norm_final")


def _cols_to_slots(g, n):
    R = g.shape[0]
    return jnp.transpose(g.reshape(R, n, g.shape[1] // n), (1, 0, 2))


def _slots_to_cols(g):
    n, R, C = g.shape
    return jnp.transpose(g, (1, 0, 2)).reshape(R, n * C)


def kernel(x, norm_mix, w_in, b_gate, sc_conv_w, ssm_conv_w, ssm_conv_b, dt_bias, A_log, D_skip, ssm_norm_w, w_branch_sc, w_branch_ssm, w_out, norm_mlp, w_mlp1, w_mlp2, norm_final, loss_target, m_norm_mix, m_w_in, m_b_gate, m_sc_conv_w, m_ssm_conv_w, m_ssm_conv_b, m_dt_bias, m_A_log, m_D_skip, m_ssm_norm_w, m_w_branch_sc, m_w_branch_ssm, m_w_out, m_norm_mlp, m_w_mlp1, m_w_mlp2, m_norm_final, v_norm_mix, v_w_in, v_b_gate, v_sc_conv_w, v_ssm_conv_w, v_ssm_conv_b, v_dt_bias, v_A_log, v_D_skip, v_ssm_norm_w, v_w_branch_sc, v_w_branch_ssm, v_w_out, v_norm_mlp, v_w_mlp1, v_w_mlp2, v_norm_final):
    T, D = x.shape[1], x.shape[2]
    n_inner = 2 * D
    n_heads = n_inner // HEADDIM
    n_xbc = n_inner + 2 * NGROUPS * NSTATE
    me = 4 * lax.axis_index("x") + 2 * lax.axis_index("y") + lax.axis_index("c")

    o_z, o_xbc, o_dt, o_gate = 3 * D, 3 * D + n_inner, 3 * D + n_inner + n_xbc, 3 * D + n_inner + n_xbc + n_heads
    by_owner = lambda b: b.reshape((N_DEV,) + b.shape[2:])
    to_owner = lambda g: g.reshape((4, 2) + g.shape[1:])
    rows_of = lambda g: to_owner(g.reshape((N_DEV, g.shape[0] // N_DEV) + g.shape[1:]))
    cols_of = lambda g: to_owner(_cols_to_slots(g, N_DEV))

    class Schedule(_NoExchange):
        late = dict(proj_xbc=("bsc", "bssm", "out"), ssd_fwd=("w1", "w2"))
        shards = dict(bsc=w_branch_sc, bssm=w_branch_ssm, out=w_out, w1=w_mlp1, w2=w_mlp2)
        grad_groups = (("w2", "w1"), ("out", "bssm", "bsc"), ("win",))
        grad_carrier = dict(ssd_bwd=("w2", "w1"), ssm_conv_bwd=("out", "bssm", "bsc"), proj_dx=("win",))

        def __init__(self):
            bufs = _run_comm(_gather_ici([w_in.astype(BF16), sc_conv_w, ssm_conv_w]), "gather_in_ici")
            bufs = _run_comm(_gather_sibling(bufs), "gather_in_sibling")
            win_full = _slots_to_cols(by_owner(bufs[0]))
            self.W = dict(sc=win_full[:, :o_z], z=win_full[:, o_z:o_xbc], xbc=win_full[:, o_xbc:o_dt],
                          dt=jnp.pad(win_full[:, o_dt:o_gate], ((0, 0), (0, LANES - n_heads))), gate=win_full[:, o_gate:])
            self.taps = dict(sc_conv_w=_slots_to_cols(by_owner(bufs[1])), ssm_conv_w=_slots_to_cols(by_owner(bufs[2])))
            self.staged, self.grads, self.halves, self.summed = {}, {}, {}, {}

        def carry(self, name):
            if name in self.late:
                return _gather_ici([self.shards[k].astype(BF16) for k in self.late[name]])
            if name in self.grad_carrier:
                return _scatter_ici([self.halves[k] for k in self.grad_carrier[name]])
            return None

        def carried(self, name, outs):
            if name in self.late:
                self.staged[self.late[name]] = outs
            else:
                self.summed.update(zip(self.grad_carrier[name], outs))

        def weight(self, k):
            if k not in self.W:
                group = next(g for g in self.staged if k in g)
                bufs = _run_comm(_gather_sibling(self.staged.pop(group)), "gather_sibling_" + group[0])
                for kk, b in zip(group, bufs):
                    full = by_owner(b)
                    self.W[kk] = _slots_to_cols(full) if kk == "w1" else full.reshape(-1, D)
            return self.W[k]

        def grad(self, k, g):
            if k == "win":
                g = cols_of(jnp.concatenate([g["sc"], g["z"], g["xbc"], g["dt"][:, :n_heads], g["gate"]], axis=1))
            else:
                g = cols_of(g) if k == "w1" else rows_of(g)
            self.grads[k] = g
            group = next(gr for gr in self.grad_groups if k in gr)
            if all(kk in self.grads for kk in group):
                bufs = _run_comm(_scatter_sibling([self.grads[kk] for kk in group]), "scatter_sibling_" + group[0])
                for kk, b in zip(group, bufs):
                    self.halves[kk] = _add_halves(self.grads[kk], b, "add_halves_" + kk)

    S = Schedule()
    small = dict(norm_mix=norm_mix, b_gate=b_gate, ssm_conv_b=ssm_conv_b, dt_bias=dt_bias, A_log=A_log, D_skip=D_skip,
                 ssm_norm_w=ssm_norm_w, norm_mlp=norm_mlp, norm_final=norm_final, **S.taps)
    grad_x, g_small = _local_step(x.reshape(T, D), loss_target.reshape(T, D), S, small)

    small_flat = jnp.concatenate([g_small[k].reshape(-1) for k in _SMALL_ORDER])
    n_small = small_flat.shape[0]
    rows = -(-n_small // (8 * LANES)) * 8
    small_pack = jnp.pad(small_flat, (0, rows * LANES - n_small)).reshape(rows, LANES)
    small_parts = _run_comm(_gather_all([small_pack]), "gather_small")[0]

    res = {}
    big = [("w_in", "win", w_in, m_w_in, v_w_in), ("w_branch_sc", "bsc", w_branch_sc, m_w_branch_sc, v_w_branch_sc),
           ("w_branch_ssm", "bssm", w_branch_ssm, m_w_branch_ssm, v_w_branch_ssm), ("w_out", "out", w_out, m_w_out, v_w_out),
           ("w_mlp1", "w1", w_mlp1, m_w_mlp1, v_w_mlp1), ("w_mlp2", "w2", w_mlp2, m_w_mlp2, v_w_mlp2)]
    for k, gk, w, m, v in big:
        res[k] = _adam(w, m, v, S.summed[gk], "adam_" + k)

    sizes = {k: g_small[k].size for k in _SMALL_ORDER}
    offs, o = {}, 0
    for k in _SMALL_ORDER:
        offs[k] = o
        o += sizes[k]
    rep_w = dict(norm_mix=norm_mix, b_gate=b_gate, ssm_conv_b=ssm_conv_b, dt_bias=dt_bias, A_log=A_log, D_skip=D_skip,
                 ssm_norm_w=ssm_norm_w, norm_mlp=norm_mlp, norm_final=norm_final)
    rep_m = dict(norm_mix=m_norm_mix, b_gate=m_b_gate, ssm_conv_b=m_ssm_conv_b, dt_bias=m_dt_bias, A_log=m_A_log, D_skip=m_D_skip,
                 ssm_norm_w=m_ssm_norm_w, norm_mlp=m_norm_mlp, norm_final=m_norm_final)
    rep_v = dict(norm_mix=v_norm_mix, b_gate=v_b_gate, ssm_conv_b=v_ssm_conv_b, dt_bias=v_dt_bias, A_log=v_A_log, D_skip=v_D_skip,
                 ssm_norm_w=v_ssm_norm_w, norm_mlp=v_norm_mlp, norm_final=v_norm_final)

    def pack(d):
        flat = jnp.zeros((rows * LANES,), F32)
        for k in _REPLICATED:
            flat = lax.dynamic_update_slice(flat, d[k].astype(F32).reshape(-1), (offs[k],))
        return flat.reshape(rows, LANES)

    sm = _adam(pack(rep_w), pack(rep_m), pack(rep_v), small_parts, "adam_small")
    sm = [s.reshape(-1) for s in sm]
    for k in _REPLICATED:
        n_k = rep_w[k].shape[0]
        res[k] = tuple(s[offs[k]:offs[k] + n_k] for s in sm)
    loss = sm[0][offs["loss"]]
    for k, w, m, v, K, full in (("sc_conv_w", sc_conv_w, m_sc_conv_w, v_sc_conv_w, SC_K, D),
                                ("ssm_conv_w", ssm_conv_w, m_ssm_conv_w, v_ssm_conv_w, SSM_K, n_xbc)):
        g_full = sm[0][offs[k]:offs[k] + K * full].reshape(K, full)
        cw = full // N_DEV
        g_mine = lax.dynamic_slice_in_dim(g_full, me * cw, cw, axis=1)
        res[k] = _adam(w, m, v, g_mine[None], "adam_" + k)

    order = ("norm_mix", "w_in", "b_gate", "sc_conv_w", "ssm_conv_w", "ssm_conv_b", "dt_bias", "A_log", "D_skip", "ssm_norm_w",
             "w_branch_sc", "w_branch_ssm", "w_out", "norm_mlp", "w_mlp1", "w_mlp2", "norm_final")
    outs = [loss, grad_x.reshape(1, T, D)]
    for j in range(4):
        outs += [res[k][j] for k in order]
    return tuple(outs)
```

```python
import functools

import jax
import jax.numpy as jnp
from jax import lax
from jax.experimental import pallas as pl
from jax.experimental.pallas import tpu as pltpu

F32 = jnp.float32
BF16 = jnp.bfloat16

EPS = 1e-6
N_DEV = 8
HEADDIM = 64
NSTATE = 128
CHUNK = 128
NGROUPS = 8
GROUP_W = 256
SC_K = 3
SSM_K = 4
LANES = 128

ADAM_LR = 0.001
ADAM_B1 = 0.9
ADAM_B2 = 0.999
ADAM_EPS = 1e-08
ADAM_WD = 0.01
ADAM_STEP = 10

NN = (((1,), (0,)), ((), ()))
NT = (((1,), (1,)), ((), ()))
TN = (((0,), (0,)), ((), ()))
_DIMS = {"nn": NN, "nt": NT, "tn": TN}

ANY = pl.BlockSpec(memory_space=pl.ANY)
MESH = pl.DeviceIdType.MESH


def _sds(shape, dtype):
    return jax.ShapeDtypeStruct(tuple(shape), dtype)


def _dot(a, b, dims=NN):
    return lax.dot_general(a, b, dims, preferred_element_type=F32)


def _dot3(a, b, dims=NN):
    return lax.dot_general(a, b, dims, preferred_element_type=F32, precision=lax.Precision.HIGH)


def _params(*sem):
    return pltpu.CompilerParams(dimension_semantics=tuple(sem))


def _call(body, *, grid, in_specs, out_specs, out_shape, args, name, sem, scratch=(), comm=None):
    if comm is None:
        outs = pl.pallas_call(body, grid=grid, in_specs=list(in_specs), out_specs=list(out_specs), out_shape=list(out_shape),
                              scratch_shapes=list(scratch), name=name, compiler_params=_params(*sem))(*args)
        return list(outs), None
    n, n_in, n_out, n_scr = comm.n, len(in_specs), len(out_shape), len(scratch)

    def wrapped(*refs):
        ins, c_in = refs[:n_in], refs[n_in:n_in + n]
        outs, c_out = refs[n_in + n:n_in + n + n_out], refs[n_in + n + n_out:n_in + 2 * n + n_out]
        rest = refs[n_in + 2 * n + n_out:]
        scr, sems = rest[:n_scr], rest[n_scr:]
        first, last = None, None
        for d, g in enumerate(grid):
            f, l = pl.program_id(d) == 0, pl.program_id(d) == g - 1
            first, last = (f, l) if first is None else (first & f, last & l)

        @pl.when(first)
        def _():
            comm.start(c_in, c_out, sems)

        body(*ins, *outs, *scr)

        @pl.when(last)
        def _():
            comm.finish(c_in, c_out, sems)

    outs = pl.pallas_call(
        wrapped, grid=grid, in_specs=list(in_specs) + [ANY] * n, out_specs=list(out_specs) + [ANY] * n,
        out_shape=list(out_shape) + comm.out_shape, scratch_shapes=list(scratch) + comm.scratch,
        input_output_aliases={n_in + i: n_out + o for i, o in comm.aliases.items()},
        name=name, compiler_params=_params(*["arbitrary"] * len(grid)))(*args, *comm.arrs)
    return list(outs[:n_out]), list(outs[n_out:])


MM_VMEM_BUDGET = 44 * 2 ** 20


def _mm_tiles(M, N, k_bytes, mn_bytes):
    best = None
    for tm in (2048, 1024, 512, 256, 128):
        for tn in (1024, 512, 256, 128):
            if M % tm or N % tn:
                continue
            need = 2 * ((tm + tn) * k_bytes + tm * tn * mn_bytes) + 4 * tm * tn * 4
            if need <= MM_VMEM_BUDGET and (best is None or (tm * tn, tm) > (best[0] * best[1], best[0])):
                best = (tm, tn)
    assert best is not None, (M, N, k_bytes, mn_bytes)
    return best


def _mm(a, b, *, mode, name, extras=(), epi=None, out_dtypes=(F32,), comm=None):
    a_list = list(a) if isinstance(a, (list, tuple)) else [a]
    b_list = list(b) if isinstance(b, (list, tuple)) else [b]
    if mode == "nn":
        M, N = a_list[0].shape[0], b_list[0].shape[1]
    elif mode == "nt":
        M, N = a_list[0].shape[0], b_list[0].shape[0]
    else:
        M, N = a_list[0].shape[1], b_list[0].shape[1]
    k_bytes = sum((av.shape[0] if mode == "tn" else av.shape[1]) * av.dtype.itemsize for av in a_list)
    mn_bytes = sum(e.dtype.itemsize for e in extras) + sum(jnp.dtype(d).itemsize for d in out_dtypes)
    tm, tn = _mm_tiles(min(M, 2048), min(N, 1024), k_bytes, mn_bytes) if M % 128 == 0 and N % 128 == 0 else (M, N)
    assert M % tm == 0 and N % tn == 0
    a_specs, b_specs = [], []
    for av, bv in zip(a_list, b_list):
        K = av.shape[0] if mode == "tn" else av.shape[1]
        a_specs.append(pl.BlockSpec((K, tm), lambda i, j: (0, i)) if mode == "tn" else pl.BlockSpec((tm, K), lambda i, j: (i, 0)))
        b_specs.append(pl.BlockSpec((tn, K), lambda i, j: (j, 0)) if mode == "nt" else pl.BlockSpec((K, tn), lambda i, j: (0, j)))
    mn_spec = pl.BlockSpec((tm, tn), lambda i, j: (i, j))
    n_p, n_ex = len(a_list), len(extras)
    dims = _DIMS[mode]

    def body(*refs):
        acc = _dot(refs[0][...], refs[n_p][...], dims)
        for p in range(1, n_p):
            acc = acc + _dot(refs[p][...], refs[n_p + p][...], dims)
        rest = refs[2 * n_p:]
        res = (acc,) if epi is None else epi(acc, *[r[...] for r in rest[:n_ex]])
        for o_ref, r in zip(rest[n_ex:], res):
            o_ref[...] = r.astype(o_ref.dtype)

    outs, carried = _call(
        body, grid=(M // tm, N // tn), in_specs=a_specs + b_specs + [mn_spec] * n_ex,
        out_specs=[mn_spec] * len(out_dtypes), out_shape=[_sds((M, N), d) for d in out_dtypes],
        args=a_list + b_list + list(extras), name=name, sem=("parallel", "parallel"), comm=comm)
    res = outs[0] if len(outs) == 1 else outs
    return res if comm is None else (res, carried)


def _epi_add(acc, r):
    return (acc + r,)


def _epi_add2(acc, r):
    s = acc + r
    return (s, s)


def _epi_relu2(acc):
    p = jnp.maximum(acc, 0.0)
    return (p * p,)


def _epi_relu2_bwd(acc, r):
    return (acc * (2.0 * jnp.sqrt(r.astype(F32))),)


def _row(tr, n):
    return pl.BlockSpec((tr, n), lambda i: (i, 0))


def _vec(n):
    return pl.BlockSpec((1, n), lambda i: (0, 0))


def _rms_fwd(x, w, name):
    T, D = x.shape
    tr = min(256, T)

    def body(x_ref, w_ref, o_ref):
        xv = x_ref[...]
        r = lax.rsqrt(jnp.mean(xv * xv, axis=-1, keepdims=True) + EPS)
        o_ref[...] = (xv * r * w_ref[...]).astype(BF16)

    return pl.pallas_call(body, grid=(T // tr,), in_specs=[_row(tr, D), _vec(D)], out_specs=_row(tr, D),
                          out_shape=_sds((T, D), BF16), name=name, compiler_params=_params("parallel"))(x, w)


def _rms_bwd(x, w, dh, dres, name):
    T, D = x.shape
    tr = min(256, T)

    def body(x_ref, w_ref, dh_ref, dres_ref, dx_ref, dxb_ref, dw_ref):
        @pl.when(pl.program_id(0) == 0)
        def _():
            dw_ref[...] = jnp.zeros_like(dw_ref)

        xv = x_ref[...]
        r = lax.rsqrt(jnp.mean(xv * xv, axis=-1, keepdims=True) + EPS)
        xh = xv * r
        dh_v = dh_ref[...]
        dw_ref[...] += jnp.sum(dh_v * xh, axis=0, keepdims=True)
        dxh = dh_v * w_ref[...]
        dx = r * (dxh - xh * jnp.mean(dxh * xh, axis=-1, keepdims=True)) + dres_ref[...]
        dx_ref[...] = dx
        dxb_ref[...] = dx.astype(BF16)

    return pl.pallas_call(
        body, grid=(T // tr,), in_specs=[_row(tr, D), _vec(D), _row(tr, D), _row(tr, D)],
        out_specs=[_row(tr, D), _row(tr, D), _vec(D)],
        out_shape=[_sds((T, D), F32), _sds((T, D), BF16), _sds((1, D), F32)],
        name=name, compiler_params=_params("arbitrary"))(x, w, dh, dres)


def _final(x2, w, tgt, name):
    T, D = x2.shape
    tr = min(256, T)

    def body(x_ref, w_ref, t_ref, dx_ref, dxb_ref, dw_ref, loss_ref):
        @pl.when(pl.program_id(0) == 0)
        def _():
            dw_ref[...] = jnp.zeros_like(dw_ref)
            loss_ref[...] = jnp.zeros_like(loss_ref)

        xv = x_ref[...]
        wv = w_ref[...]
        r = lax.rsqrt(jnp.mean(xv * xv, axis=-1, keepdims=True) + EPS)
        xh = xv * r
        err = xh * wv - t_ref[...]
        part = jnp.sum(jnp.sum(err * err, axis=1, keepdims=True), axis=0, keepdims=True) * (0.5 / D)
        loss_ref[...] += jnp.broadcast_to(part, loss_ref.shape)
        dy = err * (1.0 / D)
        dw_ref[...] += jnp.sum(dy * xh, axis=0, keepdims=True)
        dxh = dy * wv
        dx = r * (dxh - xh * jnp.mean(dxh * xh, axis=-1, keepdims=True))
        dx_ref[...] = dx
        dxb_ref[...] = dx.astype(BF16)

    return pl.pallas_call(
        body, grid=(T // tr,), in_specs=[_row(tr, D), _vec(D), _row(tr, D)],
        out_specs=[_row(tr, D), _row(tr, D), _vec(D), _vec(LANES)],
        out_shape=[_sds((T, D), F32), _sds((T, D), BF16), _sds((1, D), F32), _sds((1, LANES), F32)],
        name=name, compiler_params=_params("arbitrary"))(x2, w, tgt)


def _silu_parts(z):
    s = jax.nn.sigmoid(z)
    return z * s, s * (1.0 + z * (1.0 - s))


def _gnorm_fwd(y, z, w, name):
    T, N = y.shape
    tr = min(256, T)

    def body(y_ref, z_ref, w_ref, o_ref):
        for g in range(N // GROUP_W):
            sl = slice(g * GROUP_W, (g + 1) * GROUP_W)
            silu, _ = _silu_parts(z_ref[:, sl])
            yz = y_ref[:, sl] * silu
            r = lax.rsqrt(jnp.mean(yz * yz, axis=-1, keepdims=True) + EPS)
            o_ref[:, sl] = (yz * r * w_ref[:, sl]).astype(BF16)

    return pl.pallas_call(body, grid=(T // tr,), in_specs=[_row(tr, N), _row(tr, N), _vec(N)], out_specs=_row(tr, N),
                          out_shape=_sds((T, N), BF16), name=name, compiler_params=_params("parallel"))(y, z, w)


def _gnorm_bwd(y, z, w, dyb, name):
    T, N = y.shape
    tr = min(256, T)

    def body(y_ref, z_ref, w_ref, d_ref, dy_ref, dz_ref, dw_ref):
        @pl.when(pl.program_id(0) == 0)
        def _():
            dw_ref[...] = jnp.zeros_like(dw_ref)

        for g in range(N // GROUP_W):
            sl = slice(g * GROUP_W, (g + 1) * GROUP_W)
            yv = y_ref[:, sl]
            silu, dsilu = _silu_parts(z_ref[:, sl])
            yz = yv * silu
            r = lax.rsqrt(jnp.mean(yz * yz, axis=-1, keepdims=True) + EPS)
            yzh = yz * r
            d = d_ref[:, sl]
            dw_ref[:, sl] += jnp.sum(d * yzh, axis=0, keepdims=True)
            dyzh = d * w_ref[:, sl]
            dyz = r * (dyzh - yzh * jnp.mean(dyzh * yzh, axis=-1, keepdims=True))
            dy_ref[:, sl] = dyz * silu
            dz_ref[:, sl] = (dyz * yv * dsilu).astype(BF16)

    return pl.pallas_call(
        body, grid=(T // tr,), in_specs=[_row(tr, N), _row(tr, N), _vec(N), _row(tr, N)],
        out_specs=[_row(tr, N), _row(tr, N), _vec(N)],
        out_shape=[_sds((T, N), F32), _sds((T, N), BF16), _sds((1, N), F32)],
        name=name, compiler_params=_params("arbitrary"))(y, z, w, dyb)


def _merge_fwd(gate_raw, b_gate, br_a, br_b, name):
    T, D = br_a.shape
    tr = min(256, T)

    def body(g_ref, bg_ref, a_ref, b_ref, o_ref):
        g = jax.nn.sigmoid(g_ref[...] + bg_ref[...])
        o_ref[...] = (g[:, :D] * a_ref[...] + g[:, D:] * b_ref[...]).astype(BF16)

    return pl.pallas_call(body, grid=(T // tr,), in_specs=[_row(tr, 2 * D), _vec(2 * D), _row(tr, D), _row(tr, D)],
                          out_specs=_row(tr, D), out_shape=_sds((T, D), BF16), name=name,
                          compiler_params=_params("parallel"))(gate_raw, b_gate, br_a, br_b)


def _merge_bwd(dmerged, gate_raw, b_gate, br_a, br_b, name):
    T, D = br_a.shape
    tr = min(256, T)

    def body(d_ref, g_ref, bg_ref, a_ref, b_ref, da_ref, db_ref, dg_ref, dbg_ref):
        @pl.when(pl.program_id(0) == 0)
        def _():
            dbg_ref[...] = jnp.zeros_like(dbg_ref)

        g = jax.nn.sigmoid(g_ref[...] + bg_ref[...])
        d = d_ref[...]
        da_ref[...] = (d * g[:, :D]).astype(BF16)
        db_ref[...] = (d * g[:, D:]).astype(BF16)
        dg = jnp.concatenate([d * a_ref[...], d * b_ref[...]], axis=1) * g * (1.0 - g)
        dg_ref[...] = dg.astype(BF16)
        dbg_ref[...] += jnp.sum(dg, axis=0, keepdims=True)

    return pl.pallas_call(
        body, grid=(T // tr,), in_specs=[_row(tr, D), _row(tr, 2 * D), _vec(2 * D), _row(tr, D), _row(tr, D)],
        out_specs=[_row(tr, D), _row(tr, D), _row(tr, 2 * D), _vec(2 * D)],
        out_shape=[_sds((T, D), BF16), _sds((T, D), BF16), _sds((T, 2 * D), BF16), _sds((1, 2 * D), F32)],
        name=name, compiler_params=_params("arbitrary"))(dmerged, gate_raw, b_gate, br_a, br_b)


def _shift_down(u, s):
    if s == 0:
        return u
    row = lax.broadcasted_iota(jnp.int32, u.shape, 0)
    return jnp.where(row >= s, pltpu.roll(u, s, 0), 0.0)


def _shift_up(u, s):
    if s == 0:
        return u
    n = u.shape[0]
    row = lax.broadcasted_iota(jnp.int32, u.shape, 0)
    return jnp.where(row < n - s, pltpu.roll(u, n - s, 0), 0.0)


def _conv(u, w_ref, K):
    acc = u * w_ref[K - 1:K, :]
    for k in range(K - 1):
        acc = acc + _shift_down(u, K - 1 - k) * w_ref[k:k + 1, :]
    return acc


def _conv_bwd(u, dc, w_ref, dw_ref, K):
    du = dc * w_ref[K - 1:K, :]
    dw_ref[K - 1:K, :] = jnp.sum(dc * u, axis=0, keepdims=True)
    for k in range(K - 1):
        s = K - 1 - k
        dw_ref[k:k + 1, :] = jnp.sum(dc * _shift_down(u, s), axis=0, keepdims=True)
        du = du + _shift_up(dc, s) * w_ref[k:k + 1, :]
    return du


CB_W = 256


def _col(T, j0=0):
    return pl.BlockSpec((T, CB_W), lambda j: (0, j + j0))


def _sc_fwd(psc, w, name):
    T, D = psc.shape[0], psc.shape[1] // 3
    nb = D // CB_W

    def body(b_ref, c_ref, x_ref, w_ref, o_ref):
        o_ref[...] = (b_ref[...] * _conv(c_ref[...] * x_ref[...], w_ref, SC_K)).astype(BF16)

    return pl.pallas_call(
        body, grid=(nb,), in_specs=[_col(T), _col(T, nb), _col(T, 2 * nb), pl.BlockSpec((SC_K, CB_W), lambda j: (0, j))],
        out_specs=_col(T), out_shape=_sds((T, D), BF16), name=name, compiler_params=_params("parallel"))(psc, psc, psc, w)


def _sc_bwd(psc, w, dya, name):
    T, D = psc.shape[0], psc.shape[1] // 3
    nb = D // CB_W

    def body(b_ref, c_ref, x_ref, w_ref, d_ref, db_ref, dc_ref, dx_ref, dw_ref):
        cv, xv, d = c_ref[...], x_ref[...], d_ref[...]
        u = cv * xv
        db_ref[...] = (d * _conv(u, w_ref, SC_K)).astype(BF16)
        du = _conv_bwd(u, d * b_ref[...], w_ref, dw_ref, SC_K)
        dc_ref[...] = (du * xv).astype(BF16)
        dx_ref[...] = (du * cv).astype(BF16)

    wspec = pl.BlockSpec((SC_K, CB_W), lambda j: (0, j))
    return pl.pallas_call(
        body, grid=(nb,), in_specs=[_col(T), _col(T, nb), _col(T, 2 * nb), wspec, _col(T)],
        out_specs=[_col(T), _col(T), _col(T), wspec],
        out_shape=[_sds((T, D), BF16)] * 3 + [_sds((SC_K, D), F32)],
        name=name, compiler_params=_params("parallel"))(psc, psc, psc, w, dya)


def _ssm_conv_fwd(u, w, b, name):
    T, N = u.shape

    def body(u_ref, w_ref, b_ref, o_ref):
        c = _conv(u_ref[...], w_ref, SSM_K) + b_ref[...]
        o_ref[...] = c * jax.nn.sigmoid(c)

    return pl.pallas_call(
        body, grid=(N // CB_W,), in_specs=[_col(T), pl.BlockSpec((SSM_K, CB_W), lambda j: (0, j)), pl.BlockSpec((1, CB_W), lambda j: (0, j))],
        out_specs=_col(T), out_shape=_sds((T, N), F32), name=name, compiler_params=_params("parallel"))(u, w, b)


def _ssm_conv_bwd(u, w, b, dxs, dB, dC, name, comm=None):
    T, N = u.shape
    n_x, n_b = dxs.shape[1] // CB_W, dB.shape[1] // CB_W

    def body(u_ref, w_ref, b_ref, dx_ref, db_ref, dc_ref, du_ref, dw_ref, dbias_ref):
        j = pl.program_id(0)
        uv = u_ref[...]
        c = _conv(uv, w_ref, SSM_K) + b_ref[...]
        _, dsilu = _silu_parts(c)
        d = jnp.where(j < n_x, dx_ref[...], jnp.where(j < n_x + n_b, db_ref[...], dc_ref[...])) * dsilu
        dbias_ref[...] = jnp.sum(d, axis=0, keepdims=True)
        du_ref[...] = _conv_bwd(uv, d, w_ref, dw_ref, SSM_K).astype(BF16)

    wspec = pl.BlockSpec((SSM_K, CB_W), lambda j: (0, j))
    bspec = pl.BlockSpec((1, CB_W), lambda j: (0, j))
    outs, carried = _call(
        body, grid=(N // CB_W,),
        in_specs=[_col(T), wspec, bspec,
                  pl.BlockSpec((T, CB_W), lambda j: (0, jnp.minimum(j, n_x - 1))),
                  pl.BlockSpec((T, CB_W), lambda j: (0, jnp.clip(j - n_x, 0, n_b - 1))),
                  pl.BlockSpec((T, CB_W), lambda j: (0, jnp.clip(j - n_x - n_b, 0, n_b - 1)))],
        out_specs=[_col(T), wspec, bspec],
        out_shape=[_sds((T, N), BF16), _sds((SSM_K, N), F32), _sds((1, N), F32)],
        args=[u, w, b, dxs, dB, dC], name=name, sem=("parallel",), comm=comm)
    return outs if comm is None else (outs, carried)


def _split3(v):
    hi = v.astype(BF16)
    r = v - hi.astype(F32)
    mid = r.astype(BF16)
    lo = (r - mid.astype(F32)).astype(BF16)
    return hi, mid, lo


def _head_expand(n_lanes):
    h = lax.broadcasted_iota(jnp.int32, (LANES, n_lanes), 0)
    l = lax.broadcasted_iota(jnp.int32, (LANES, n_lanes), 1)
    return (jnp.right_shift(l, HEADDIM.bit_length() - 1) == h).astype(BF16)


def _softplus(v):
    return jnp.maximum(v, 0.0) + jnp.log1p(jnp.exp(-jnp.abs(v)))


def _ssd_prep(dt_raw, dt_bias, a_log, n_inner, name):
    T = dt_raw.shape[0]

    def body(r_ref, b_ref, al_ref, dt_ref, cs_ref):
        dt = _softplus(r_ref[...] + b_ref[...])
        a = dt * (-jnp.exp(al_ref[...]))
        i = lax.broadcasted_iota(jnp.int32, (CHUNK, CHUNK), 0)
        j = lax.broadcasted_iota(jnp.int32, (CHUNK, CHUNK), 1)
        tri = (j <= i).astype(BF16)
        cs = sum(_dot(tri, p) for p in _split3(a))
        ex = _head_expand(n_inner)
        dt_ref[...] = sum(_dot(p, ex) for p in _split3(dt))
        cs_ref[...] = sum(_dot(p, ex) for p in _split3(cs))

    blk = pl.BlockSpec((CHUNK, LANES), lambda c: (c, 0))
    out = pl.BlockSpec((CHUNK, n_inner), lambda c: (c, 0))
    return pl.pallas_call(body, grid=(T // CHUNK,), in_specs=[blk, _vec(LANES), _vec(LANES)], out_specs=[out, out],
                          out_shape=[_sds((T, n_inner), F32)] * 2, name=name, compiler_params=_params("parallel"))(dt_raw, dt_bias, a_log)


def _pair_terms(cs_p):
    lane = lax.broadcasted_iota(jnp.int32, (CHUNK, CHUNK), 1)
    sub = lax.broadcasted_iota(jnp.int32, (CHUNK, CHUNK), 0)
    csT = cs_p.T
    Ls = []
    for k in range(2):
        col = jnp.sum(jnp.where(lane == k * HEADDIM, cs_p, 0.0), axis=1, keepdims=True)
        rowv = csT[k * HEADDIM:k * HEADDIM + 1, :]
        Ls.append(jnp.exp(jnp.where(sub >= lane, col - rowv, -jnp.inf)))
    return Ls, jnp.exp(csT[:, CHUNK - 1:CHUNK])


def _block_diag(xp):
    lane = lax.broadcasted_iota(jnp.int32, xp.shape, 1)
    return jnp.concatenate([jnp.where(lane < HEADDIM, xp, 0.0), jnp.where(lane >= HEADDIM, xp, 0.0)], axis=0)


def _ssd_specs(T, n_inner):
    nc = T // CHUNK
    xo, bo, co = 0, n_inner // LANES, n_inner // LANES + NGROUPS
    g_blk = lambda f: pl.BlockSpec((CHUNK, GROUP_W), lambda c, g: (f(c), g))
    return nc, g_blk, (lambda f: pl.BlockSpec((CHUNK, NSTATE), lambda c, g: (f(c), bo + g))), (lambda f: pl.BlockSpec((CHUNK, NSTATE), lambda c, g: (f(c), co + g)))


def _ssd_fwd(xbc, dt_e, cs_e, d_e, name, comm=None):
    T = xbc.shape[0]
    n_inner = dt_e.shape[1]
    nc, g_blk, b_blk, c_blk = _ssd_specs(T, n_inner)
    ident = lambda c: c

    def body(xs_ref, b_ref, c_ref, dt_ref, cs_ref, d_ref, y_ref, p_ref, st):
        c, g = pl.program_id(0), pl.program_id(1)

        @pl.when(c == 0)
        def _():
            st[g] = jnp.zeros((GROUP_W, NSTATE), F32)

        P = st[g]
        p_ref[0, 0] = P
        xs, dt, cs = xs_ref[...], dt_ref[...], cs_ref[...]
        Bf, Cf = b_ref[...], c_ref[...]
        CBm = _dot3(Cf, Bf, NT)
        X = xs * dt
        decay = jnp.exp(cs[CHUNK - 1:CHUNK, :] - cs)
        y_off = _dot3(Cf, P, NT) * jnp.exp(cs)
        ys, ecl = [], []
        for pr in range(2):
            sl = slice(pr * LANES, (pr + 1) * LANES)
            Ls, e_last = _pair_terms(cs[:, sl])
            ecl.append(e_last)
            Mcat = jnp.concatenate([CBm * L for L in Ls], axis=1)
            ys.append(_dot3(Mcat, _block_diag(X[:, sl])))
        y_ref[...] = jnp.concatenate(ys, axis=1) + y_off + xs * d_ref[...]
        S = _dot3(X * decay, Bf, TN)
        st[g] = P * jnp.concatenate(ecl, axis=0) + S

    p_blk = pl.BlockSpec((1, 1, GROUP_W, NSTATE), lambda c, g: (c, g, 0, 0))
    outs, carried = _call(
        body, grid=(nc, NGROUPS),
        in_specs=[g_blk(ident), b_blk(ident), c_blk(ident), g_blk(ident), g_blk(ident), pl.BlockSpec((1, GROUP_W), lambda c, g: (0, g))],
        out_specs=[g_blk(ident), p_blk],
        out_shape=[_sds((T, n_inner), F32), _sds((nc, NGROUPS, GROUP_W, NSTATE), F32)],
        scratch=[pltpu.VMEM((NGROUPS, GROUP_W, NSTATE), F32)],
        args=[xbc, xbc, xbc, dt_e, cs_e, d_e], name=name, sem=("arbitrary", "arbitrary"), comm=comm)
    return outs if comm is None else (outs, carried)


def _ssd_bwd(xbc, dt_e, cs_e, d_e, states, dy, name, comm=None):
    T = xbc.shape[0]
    n_inner = dt_e.shape[1]
    nc, g_blk, b_blk, c_blk = _ssd_specs(T, n_inner)
    rev = lambda c: nc - 1 - c

    def body(xs_ref, b_ref, c_ref, dt_ref, cs_ref, d_ref, p_ref, pn_ref, dy_ref,
             dxs_ref, db_ref, dc_ref, ddt_ref, dcs_ref, dd_ref, dst):
        cc, g = pl.program_id(0), pl.program_id(1)

        @pl.when(cc == 0)
        def _():
            dst[g] = jnp.zeros((GROUP_W, NSTATE), F32)

        dS = dst[g]
        P, Pn = p_ref[0, 0], pn_ref[0, 0]
        xs, dt, cs, dY = xs_ref[...], dt_ref[...], cs_ref[...], dy_ref[...]
        Bf, Cf = b_ref[...], c_ref[...]
        Bb, Cb = Bf.astype(BF16), Cf.astype(BF16)
        X = xs * dt
        ecs = jnp.exp(cs)
        decay = jnp.exp(cs[CHUNK - 1:CHUNK, :] - cs)
        CBm = _dot3(Cf, Bf, NT)
        dYe = dY * ecs
        dP_off = _dot3(dYe, Cf, TN)
        dC = _dot(dYe.astype(BF16), P.astype(BF16))
        dcs = dYe * _dot3(Cf, P, NT)
        Xd = X * decay
        dB = _dot(Xd.astype(BF16), dS.astype(BF16))
        E = _dot3(Bf, dS, NT)
        dX = E * decay
        dcs = dcs - E * Xd
        R = _dot3(jnp.ones((8, NSTATE), F32), dS * Pn, NT)
        sub_g = lax.broadcasted_iota(jnp.int32, (CHUNK, GROUP_W), 0)
        dcs = dcs + jnp.where(sub_g == CHUNK - 1, R[0:1, :], 0.0)
        lane = lax.broadcasted_iota(jnp.int32, (CHUNK, CHUNK), 1)
        sub = lax.broadcasted_iota(jnp.int32, (CHUNK, CHUNK), 0)
        dCB = jnp.zeros((CHUNK, CHUNK), F32)
        dXs, dcss, ecl = [], [], []
        for pr in range(2):
            sl = slice(pr * LANES, (pr + 1) * LANES)
            Ls, e_last = _pair_terms(cs[:, sl])
            ecl.append(e_last)
            dYp = dY[:, sl]
            dMcat = _dot3(dYp, _block_diag(X[:, sl]), NT)
            Mcat = jnp.concatenate([CBm * L for L in Ls], axis=1)
            dXt = _dot3(Mcat, dYp, TN)
            dXs.append(jnp.where(lane < HEADDIM, dXt[:CHUNK], dXt[CHUNK:]))
            colacc = jnp.zeros((CHUNK, CHUNK), F32)
            rowacc = jnp.zeros((CHUNK, CHUNK), F32)
            for k in range(2):
                dG = dMcat[:, k * CHUNK:(k + 1) * CHUNK] * Ls[k]
                dCB = dCB + dG
                Q = dG * CBm
                colacc = colacc + jnp.where(lane == k * HEADDIM, jnp.sum(Q, axis=1, keepdims=True), 0.0)
                rowacc = rowacc + jnp.where(sub == k * HEADDIM, jnp.sum(Q, axis=0, keepdims=True), 0.0)
            dcss.append(colacc - rowacc.T)
        dX = dX + jnp.concatenate(dXs, axis=1)
        dcs = dcs + jnp.concatenate(dcss, axis=1)
        dCBb = dCB.astype(BF16)
        dc_ref[...] = dC + _dot(dCBb, Bb)
        db_ref[...] = dB + _dot(dCBb, Cb, TN)
        dxs_ref[...] = dX * dt + dY * d_ref[...]
        ddt_ref[...] = dX * xs
        dcs_ref[...] = dcs
        dd_ref[0] = jnp.sum(dY * xs, axis=0, keepdims=True)
        dst[g] = dS * jnp.concatenate(ecl, axis=0) + dP_off

    p_blk = pl.BlockSpec((1, 1, GROUP_W, NSTATE), lambda c, g: (nc - 1 - c, g, 0, 0))
    pn_blk = pl.BlockSpec((1, 1, GROUP_W, NSTATE), lambda c, g: (jnp.minimum(nc - c, nc - 1), g, 0, 0))
    st_blk = pl.BlockSpec((CHUNK, NSTATE), lambda c, g: (nc - 1 - c, g))
    outs, carried = _call(
        body, grid=(nc, NGROUPS),
        in_specs=[g_blk(rev), b_blk(rev), c_blk(rev), g_blk(rev), g_blk(rev), pl.BlockSpec((1, GROUP_W), lambda c, g: (0, g)),
                  p_blk, pn_blk, g_blk(rev)],
        out_specs=[g_blk(rev), st_blk, st_blk, g_blk(rev), g_blk(rev), pl.BlockSpec((1, 1, GROUP_W), lambda c, g: (nc - 1 - c, 0, g))],
        out_shape=[_sds((T, n_inner), F32), _sds((T, NGROUPS * NSTATE), F32), _sds((T, NGROUPS * NSTATE), F32),
                   _sds((T, n_inner), F32), _sds((T, n_inner), F32), _sds((nc, 1, n_inner), F32)],
        scratch=[pltpu.VMEM((NGROUPS, GROUP_W, NSTATE), F32)],
        args=[xbc, xbc, xbc, dt_e, cs_e, d_e, states, states, dy], name=name, sem=("arbitrary", "arbitrary"), comm=comm)
    return outs if comm is None else (outs, carried)


def _ssd_post(ddt_e, dcs_e, dd_p, dt_raw, dt_bias, a_log, n_heads, name):
    T, n_inner = ddt_e.shape

    def body(ddt_ref, dcs_ref, dd_ref, r_ref, b_ref, al_ref, draw_ref, dbias_ref, dal_ref, ddsk_ref):
        @pl.when(pl.program_id(0) == 0)
        def _():
            dbias_ref[...] = jnp.zeros_like(dbias_ref)
            dal_ref[...] = jnp.zeros_like(dal_ref)
            ddsk_ref[...] = jnp.zeros_like(ddsk_ref)

        ex = _head_expand(n_inner)
        red = lambda v: sum(_dot(p, ex, NT) for p in _split3(v))
        raw = r_ref[...] + b_ref[...]
        dt = _softplus(raw)
        A = -jnp.exp(al_ref[...])
        i = lax.broadcasted_iota(jnp.int32, (CHUNK, CHUNK), 0)
        j = lax.broadcasted_iota(jnp.int32, (CHUNK, CHUNK), 1)
        upper = (j >= i).astype(BF16)
        da = sum(_dot(upper, p) for p in _split3(red(dcs_ref[...])))
        ddt = red(ddt_ref[...]) + da * A
        lane = lax.broadcasted_iota(jnp.int32, (CHUNK, LANES), 1)
        draw = jnp.where(lane < n_heads, ddt * jax.nn.sigmoid(raw), 0.0)
        draw_ref[...] = draw.astype(BF16)
        dbias_ref[...] += jnp.sum(draw, axis=0, keepdims=True)
        dal_ref[...] += jnp.sum(da * dt, axis=0, keepdims=True) * A
        ddsk_ref[...] += red(jnp.broadcast_to(dd_ref[0], (8, n_inner)))[0:1, :]

    wide = pl.BlockSpec((CHUNK, n_inner), lambda c: (c, 0))
    blk = pl.BlockSpec((CHUNK, LANES), lambda c: (c, 0))
    return pl.pallas_call(
        body, grid=(T // CHUNK,),
        in_specs=[wide, wide, pl.BlockSpec((1, 1, n_inner), lambda c: (c, 0, 0)), blk, _vec(LANES), _vec(LANES)],
        out_specs=[blk, _vec(LANES), _vec(LANES), _vec(LANES)],
        out_shape=[_sds((T, LANES), BF16)] + [_sds((1, LANES), F32)] * 3,
        name=name, compiler_params=_params("arbitrary"))(ddt_e, dcs_e, dd_p, dt_raw, dt_bias, a_log)


def _row2(v):
    return v.reshape(1, -1).astype(F32)


def _pad_lanes(v):
    return jnp.pad(_row2(v), ((0, 0), (0, LANES - v.shape[-1])))


class _NoExchange:
    def __init__(self, W):
        self.W, self.grads = W, {}

    def weight(self, k):
        return self.W[k]

    def carry(self, name):
        return None

    def carried(self, name, outs):
        pass

    def grad(self, k, g):
        self.grads[k] = g


def _local_step(x, tgt, S, small):
    T, D = x.shape

    def mm(a, b, *, name, **kw):
        comm = S.carry(name)
        if comm is None:
            return _mm(a, b, name=name, **kw)
        res, outs = _mm(a, b, name=name, comm=comm, **kw)
        S.carried(name, outs)
        return res

    def carrying(fn, *args, name):
        comm = S.carry(name)
        if comm is None:
            return fn(*args, name)
        res, outs = fn(*args, name, comm=comm)
        S.carried(name, outs)
        return res

    n_inner = 2 * D
    n_heads = n_inner // HEADDIM
    norm_mix, norm_mlp, norm_final = _row2(small["norm_mix"]), _row2(small["norm_mlp"]), _row2(small["norm_final"])
    b_gate, ssm_b, ssm_norm_w = _row2(small["b_gate"]), _row2(small["ssm_conv_b"]), _row2(small["ssm_norm_w"])
    dt_bias, a_log = _pad_lanes(small["dt_bias"]), _pad_lanes(small["A_log"])
    d_e = jnp.repeat(small["D_skip"].astype(F32), HEADDIM).reshape(1, n_inner)
    sc_w, ssm_w = small["sc_conv_w"], small["ssm_conv_w"]

    hb = _rms_fwd(x, norm_mix, "rms_mix")
    p_xbc = mm(hb, S.weight("xbc"), mode="nn", name="proj_xbc")
    p_dt = mm(hb, S.weight("dt"), mode="nn", name="proj_dt")
    p_z = mm(hb, S.weight("z"), mode="nn", name="proj_z")
    p_sc = mm(hb, S.weight("sc"), mode="nn", name="proj_sc")
    p_gate = mm(hb, S.weight("gate"), mode="nn", name="proj_gate")
    xbc = _ssm_conv_fwd(p_xbc, ssm_w, ssm_b, "ssm_conv_fwd")
    dt_e, cs_e = _ssd_prep(p_dt, dt_bias, a_log, n_inner, "ssd_prep")
    y, states = carrying(_ssd_fwd, xbc, dt_e, cs_e, d_e, name="ssd_fwd")
    yb = _gnorm_fwd(y, p_z, ssm_norm_w, "gnorm_fwd")
    ya = _sc_fwd(p_sc, sc_w, "sc_fwd")
    br_a = mm(ya, S.weight("bsc"), mode="nn", name="branch_sc")
    br_b = mm(yb, S.weight("bssm"), mode="nn", name="branch_ssm")
    merged = _merge_fwd(p_gate, b_gate, br_a, br_b, "merge_fwd")
    x1 = mm(merged, S.weight("out"), mode="nn", name="out_proj", extras=(x,), epi=_epi_add)
    h2 = _rms_fwd(x1, norm_mlp, "rms_mlp")
    r_act = mm(h2, S.weight("w1"), mode="nn", name="mlp_up", epi=_epi_relu2, out_dtypes=(BF16,))
    x2 = mm(r_act, S.weight("w2"), mode="nn", name="mlp_down", extras=(x1,), epi=_epi_add)
    dx2, dx2b, g_norm_final, loss_row = _final(x2, norm_final, tgt, "final")

    S.grad("w2", mm(r_act, dx2b, mode="tn", name="mlp_down_dw", out_dtypes=(BF16,)))
    da = mm(dx2b, S.weight("w2"), mode="nt", name="mlp_down_dx", extras=(r_act,), epi=_epi_relu2_bwd, out_dtypes=(BF16,))
    S.grad("w1", mm(h2, da, mode="tn", name="mlp_up_dw", out_dtypes=(BF16,)))
    dh2 = mm(da, S.weight("w1"), mode="nt", name="mlp_up_dx")
    dx1, dx1b, g_norm_mlp = _rms_bwd(x1, norm_mlp, dh2, dx2, "rms_mlp_bwd")
    S.grad("out", mm(merged, dx1b, mode="tn", name="out_proj_dw", out_dtypes=(BF16,)))
    dmerged = mm(dx1b, S.weight("out"), mode="nt", name="out_proj_dx")
    dbr_a, dbr_b, d_gate, g_b_gate = _merge_bwd(dmerged, p_gate, b_gate, br_a, br_b, "merge_bwd")
    S.grad("bssm", mm(yb, dbr_b, mode="tn", name="branch_ssm_dw", out_dtypes=(BF16,)))
    S.grad("bsc", mm(ya, dbr_a, mode="tn", name="branch_sc_dw", out_dtypes=(BF16,)))
    dyb = mm(dbr_b, S.weight("bssm"), mode="nt", name="branch_ssm_dx")
    dya = mm(dbr_a, S.weight("bsc"), mode="nt", name="branch_sc_dx")
    dy, d_z, g_ssm_norm_w = _gnorm_bwd(y, p_z, ssm_norm_w, dyb, "gnorm_bwd")
    dxs, dB, dC, ddt_e, dcs_e, dd_p = carrying(_ssd_bwd, xbc, dt_e, cs_e, d_e, states, dy, name="ssd_bwd")
    d_dt, g_dt_bias, g_a_log, g_d_skip = _ssd_post(ddt_e, dcs_e, dd_p, p_dt, dt_bias, a_log, n_heads, "ssd_post")
    d_xbc, g_ssm_w, g_ssm_b = carrying(_ssm_conv_bwd, p_xbc, ssm_w, ssm_b, dxs, dB, dC, name="ssm_conv_bwd")
    d_scB, d_scC, d_scX, g_sc_w = _sc_bwd(p_sc, sc_w, dya, "sc_bwd")
    d_sc = jnp.concatenate([d_scB, d_scC, d_scX], axis=1)
    pieces = [("sc", d_sc), ("z", d_z), ("xbc", d_xbc), ("dt", d_dt), ("gate", d_gate)]
    S.grad("win", {k: mm(hb, d, mode="tn", name="proj_dw_" + k, out_dtypes=(BF16,)) for k, d in pieces})
    dh = mm([d for _, d in pieces], [S.weight(k) for k, _ in pieces], mode="nt", name="proj_dx")
    grad_x, _, g_norm_mix = _rms_bwd(x, norm_mix, dh, dx1, "rms_mix_bwd")

    g_small = dict(norm_mix=g_norm_mix, b_gate=g_b_gate, sc_conv_w=g_sc_w, ssm_conv_w=g_ssm_w, ssm_conv_b=g_ssm_b,
                   dt_bias=g_dt_bias, A_log=g_a_log, D_skip=g_d_skip, ssm_norm_w=g_ssm_norm_w, norm_mlp=g_norm_mlp,
                   norm_final=g_norm_final, loss=loss_row)
    return grad_x, g_small


class _Place:
    def __init__(self, k=0):
        x, y, c = lax.axis_index("x"), lax.axis_index("y"), lax.axis_index("c")
        self.x = 1 - x if k & 4 else x
        self.y = 1 - y if k & 2 else y
        self.c = 1 - c if k & 1 else c
        self.chip = 2 * self.x + self.y
        self.id = 2 * self.chip + self.c


ICI_PEERS = (2, 4, 6)
SIBLING = (1,)
ALL_PEERS = (1, 2, 3, 4, 5, 6, 7)


class _Comm:
    def __init__(self, arrs, out_shape, ks, src, dst, own=None, aliases=None):
        self.arrs, self.out_shape, self.ks = list(arrs), list(out_shape), tuple(ks)
        self.n = len(self.arrs)
        self.src, self.dst, self.own = src, dst, own
        self.aliases = aliases or {}
        dma = pltpu.SemaphoreType.DMA
        self.scratch = [dma((self.n, len(self.ks))), dma((self.n, len(self.ks))), dma((self.n,))]

    def _copies(self, ins, outs, sems, with_recvs):
        send_sems, recv_sems, local_sems = sems
        me = _Place()
        owns, sends, recvs = [], [], []
        for a in range(self.n):
            if self.own is not None:
                s, d = self.own(a, ins[a], outs[a], me)
                owns.append(pltpu.make_async_copy(s, d, local_sems.at[a]))
            for i, k in enumerate(self.ks):
                peer = _Place(k)
                for sender, lst in ((me, sends), (peer, recvs)) if with_recvs else ((me, sends),):
                    lst.append(pltpu.make_async_remote_copy(
                        src_ref=self.src(a, ins[a], me, peer), dst_ref=self.dst(a, outs[a], sender),
                        send_sem=send_sems.at[a, i], recv_sem=recv_sems.at[a, i],
                        device_id=(peer.x, peer.y, peer.c), device_id_type=MESH))
        return owns, sends, recvs

    def start(self, ins, outs, sems):
        owns, sends, _ = self._copies(ins, outs, sems, False)
        for cp in owns + sends:
            cp.start()

    def finish(self, ins, outs, sems):
        owns, sends, recvs = self._copies(ins, outs, sems, True)
        for cp in recvs:
            cp.wait_recv()
        for cp in sends:
            cp.wait_send()
        for cp in owns:
            cp.wait()


def _run_comm(comm, name):
    n = comm.n

    def body(*refs):
        ins, outs, sems = refs[:n], refs[n:2 * n], refs[2 * n:]
        comm.start(ins, outs, sems)
        comm.finish(ins, outs, sems)

    return list(pl.pallas_call(body, in_specs=[ANY] * n, out_specs=[ANY] * n, out_shape=comm.out_shape, scratch_shapes=comm.scratch,
                               input_output_aliases=dict(comm.aliases), name=name)(*comm.arrs))


def _gather_ici(shards):
    return _Comm(shards, [_sds((4, 2) + s.shape, s.dtype) for s in shards], ICI_PEERS,
                 src=lambda a, i, me, p: i, dst=lambda a, o, s: o.at[s.chip, s.c], own=lambda a, i, o, me: (i, o.at[me.chip, me.c]))


def _gather_sibling(bufs):
    return _Comm(bufs, [_sds(b.shape, b.dtype) for b in bufs], SIBLING,
                 src=lambda a, i, me, p: i.at[:, me.c], dst=lambda a, o, s: o.at[:, s.c], aliases={a: a for a in range(len(bufs))})


def _scatter_sibling(parts):
    return _Comm(parts, [_sds((4,) + p.shape[2:], p.dtype) for p in parts], SIBLING,
                 src=lambda a, i, me, p: i.at[:, p.c], dst=lambda a, o, s: o)


def _scatter_ici(parts):
    return _Comm(parts, [_sds(p.shape, p.dtype) for p in parts], ICI_PEERS,
                 src=lambda a, i, me, p: i.at[p.chip], dst=lambda a, o, s: o.at[s.chip], own=lambda a, i, o, me: (i.at[me.chip], o.at[me.chip]))


def _gather_all(arrs):
    return _Comm(arrs, [_sds((N_DEV,) + a.shape, a.dtype) for a in arrs], ALL_PEERS,
                 src=lambda a, i, me, p: i, dst=lambda a, o, s: o.at[s.id], own=lambda a, i, o, me: (i, o.at[me.id]))


def _add_halves(parts, got, name):
    n, _, R, C = parts.shape
    tr = R if R <= 256 else 256
    assert R % tr == 0
    core = lax.axis_index("c").astype(jnp.int32).reshape(1)

    def body(c_ref, p_ref, g_ref, o_ref):
        o_ref[0] = (p_ref[0, 0].astype(F32) + g_ref[0].astype(F32)).astype(o_ref.dtype)

    spec = pltpu.PrefetchScalarGridSpec(
        num_scalar_prefetch=1, grid=(n, R // tr),
        in_specs=[pl.BlockSpec((1, 1, tr, C), lambda q, i, c_ref: (q, c_ref[0], i, 0)), pl.BlockSpec((1, tr, C), lambda q, i, c_ref: (q, i, 0))],
        out_specs=pl.BlockSpec((1, tr, C), lambda q, i, c_ref: (q, i, 0)))
    return pl.pallas_call(body, grid_spec=spec, out_shape=_sds((n, R, C), parts.dtype), name=name,
                          compiler_params=_params("parallel", "parallel"))(core, parts, got)


def _adam(w, m, v, gparts, name):
    R, C = w.shape
    n = gparts.shape[0]
    tr = R if R <= 256 else 128
    assert R % tr == 0
    c1 = 1.0 / (1.0 - ADAM_B1 ** ADAM_STEP)
    c2 = 1.0 / (1.0 - ADAM_B2 ** ADAM_STEP)

    def body(w_ref, m_ref, v_ref, g_ref, go_ref, d_ref, mo_ref, vo_ref):
        g = g_ref[0].astype(F32)
        for s in range(1, n):
            g = g + g_ref[s].astype(F32)
        mn = ADAM_B1 * m_ref[...] + (1.0 - ADAM_B1) * g
        vn = ADAM_B2 * v_ref[...] + (1.0 - ADAM_B2) * (g * g)
        go_ref[...] = g
        mo_ref[...] = mn
        vo_ref[...] = vn
        d_ref[...] = -ADAM_LR * ((mn * c1) / (jnp.sqrt(vn * c2) + ADAM_EPS) + ADAM_WD * w_ref[...])

    blk = pl.BlockSpec((tr, C), lambda i: (i, 0))
    return pl.pallas_call(
        body, grid=(R // tr,), in_specs=[blk, blk, blk, pl.BlockSpec((n, tr, C), lambda i: (0, i, 0))],
        out_specs=[blk] * 4, out_shape=[_sds((R, C), F32)] * 4, name=name, compiler_params=_params("parallel"))(w, m, v, gparts)


_SMALL_ORDER = ("norm_mix", "b_gate", "sc_conv_w", "ssm_conv_w", "ssm_conv_b", "dt_bias", "A_log", "D_skip", "ssm_norm_w",
                "norm_mlp", "norm_final", "loss")
_REPLICATED = ("norm_mix", "b_gate", "ssm_conv_b", "dt_bias", "A_log", "D_skip", "ssm_norm_w", "norm_mlp", "norm_final")


def _cols_to_slots(g, n):
    R = g.shape[0]
    return jnp.transpose(g.reshape(R, n, g.shape[1] // n), (1, 0, 2))


def _slots_to_cols(g):
    n, R, C = g.shape
    return jnp.transpose(g, (1, 0, 2)).reshape(R, n * C)


def kernel(x, norm_mix, w_in, b_gate, sc_conv_w, ssm_conv_w, ssm_conv_b, dt_bias, A_log, D_skip, ssm_norm_w, w_branch_sc, w_branch_ssm, w_out, norm_mlp, w_mlp1, w_mlp2, norm_final, loss_target, m_norm_mix, m_w_in, m_b_gate, m_sc_conv_w, m_ssm_conv_w, m_ssm_conv_b, m_dt_bias, m_A_log, m_D_skip, m_ssm_norm_w, m_w_branch_sc, m_w_branch_ssm, m_w_out, m_norm_mlp, m_w_mlp1, m_w_mlp2, m_norm_final, v_norm_mix, v_w_in, v_b_gate, v_sc_conv_w, v_ssm_conv_w, v_ssm_conv_b, v_dt_bias, v_A_log, v_D_skip, v_ssm_norm_w, v_w_branch_sc, v_w_branch_ssm, v_w_out, v_norm_mlp, v_w_mlp1, v_w_mlp2, v_norm_final):
    T, D = x.shape[1], x.shape[2]
    n_inner = 2 * D
    n_heads = n_inner // HEADDIM
    n_xbc = n_inner + 2 * NGROUPS * NSTATE
    me = 4 * lax.axis_index("x") + 2 * lax.axis_index("y") + lax.axis_index("c")

    o_z, o_xbc, o_dt, o_gate = 3 * D, 3 * D + n_inner, 3 * D + n_inner + n_xbc, 3 * D + n_inner + n_xbc + n_heads
    by_owner = lambda b: b.reshape((N_DEV,) + b.shape[2:])
    to_owner = lambda g: g.reshape((4, 2) + g.shape[1:])
    rows_of = lambda g: to_owner(g.reshape((N_DEV, g.shape[0] // N_DEV) + g.shape[1:]))
    cols_of = lambda g: to_owner(_cols_to_slots(g, N_DEV))

    class Schedule(_NoExchange):
        late = dict(proj_xbc=("bsc", "bssm", "out"), ssd_fwd=("w1", "w2"))
        shards = dict(bsc=w_branch_sc, bssm=w_branch_ssm, out=w_out, w1=w_mlp1, w2=w_mlp2)
        grad_groups = (("w2", "w1"), ("out", "bssm", "bsc"), ("win",))
        grad_carrier = dict(ssd_bwd=("w2", "w1"), ssm_conv_bwd=("out", "bssm", "bsc"), proj_dx=("win",))

        def __init__(self):
            bufs = _run_comm(_gather_ici([w_in.astype(BF16), sc_conv_w, ssm_conv_w]), "gather_in_ici")
            bufs = _run_comm(_gather_sibling(bufs), "gather_in_sibling")
            win_full = _slots_to_cols(by_owner(bufs[0]))
            self.W = dict(sc=win_full[:, :o_z], z=win_full[:, o_z:o_xbc], xbc=win_full[:, o_xbc:o_dt],
                          dt=jnp.pad(win_full[:, o_dt:o_gate], ((0, 0), (0, LANES - n_heads))), gate=win_full[:, o_gate:])
            self.taps = dict(sc_conv_w=_slots_to_cols(by_owner(bufs[1])), ssm_conv_w=_slots_to_cols(by_owner(bufs[2])))
            self.staged, self.grads, self.halves, self.summed = {}, {}, {}, {}

        def carry(self, name):
            if name in self.late:
                return _gather_ici([self.shards[k].astype(BF16) for k in self.late[name]])
            if name in self.grad_carrier:
                return _scatter_ici([self.halves[k] for k in self.grad_carrier[name]])
            return None

        def carried(self, name, outs):
            if name in self.late:
                self.staged[self.late[name]] = outs
            else:
                self.summed.update(zip(self.grad_carrier[name], outs))

        def weight(self, k):
            if k not in self.W:
                group = next(g for g in self.staged if k in g)
                bufs = _run_comm(_gather_sibling(self.staged.pop(group)), "gather_sibling_" + group[0])
                for kk, b in zip(group, bufs):
                    full = by_owner(b)
                    self.W[kk] = _slots_to_cols(full) if kk == "w1" else full.reshape(-1, D)
            return self.W[k]

        def grad(self, k, g):
            if k == "win":
                g = cols_of(jnp.concatenate([g["sc"], g["z"], g["xbc"], g["dt"][:, :n_heads], g["gate"]], axis=1))
            else:
                g = cols_of(g) if k == "w1" else rows_of(g)
            self.grads[k] = g
            group = next(gr for gr in self.grad_groups if k in gr)
            if all(kk in self.grads for kk in group):
                bufs = _run_comm(_scatter_sibling([self.grads[kk] for kk in group]), "scatter_sibling_" + group[0])
                for kk, b in zip(group, bufs):
                    self.halves[kk] = _add_halves(self.grads[kk], b, "add_halves_" + kk)

    S = Schedule()
    small = dict(norm_mix=norm_mix, b_gate=b_gate, ssm_conv_b=ssm_conv_b, dt_bias=dt_bias, A_log=A_log, D_skip=D_skip,
                 ssm_norm_w=ssm_norm_w, norm_mlp=norm_mlp, norm_final=norm_final, **S.taps)
    grad_x, g_small = _local_step(x.reshape(T, D), loss_target.reshape(T, D), S, small)

    small_flat = jnp.concatenate([g_small[k].reshape(-1) for k in _SMALL_ORDER])
    n_small = small_flat.shape[0]
    rows = -(-n_small // (8 * LANES)) * 8
    small_pack = jnp.pad(small_flat, (0, rows * LANES - n_small)).reshape(rows, LANES)
    small_parts = _run_comm(_gather_all([small_pack]), "gather_small")[0]

    res = {}
    big = [("w_in", "win", w_in, m_w_in, v_w_in), ("w_branch_sc", "bsc", w_branch_sc, m_w_branch_sc, v_w_branch_sc),
           ("w_branch_ssm", "bssm", w_branch_ssm, m_w_branch_ssm, v_w_branch_ssm), ("w_out", "out", w_out, m_w_out, v_w_out),
           ("w_mlp1", "w1", w_mlp1, m_w_mlp1, v_w_mlp1), ("w_mlp2", "w2", w_mlp2, m_w_mlp2, v_w_mlp2)]
    for k, gk, w, m, v in big:
        res[k] = _adam(w, m, v, S.summed[gk], "adam_" + k)

    sizes = {k: g_small[k].size for k in _SMALL_ORDER}
    offs, o = {}, 0
    for k in _SMALL_ORDER:
        offs[k] = o
        o += sizes[k]
    rep_w = dict(norm_mix=norm_mix, b_gate=b_gate, ssm_conv_b=ssm_conv_b, dt_bias=dt_bias, A_log=A_log, D_skip=D_skip,
                 ssm_norm_w=ssm_norm_w, norm_mlp=norm_mlp, norm_final=norm_final)
    rep_m = dict(norm_mix=m_norm_mix, b_gate=m_b_gate, ssm_conv_b=m_ssm_conv_b, dt_bias=m_dt_bias, A_log=m_A_log, D_skip=m_D_skip,
                 ssm_norm_w=m_ssm_norm_w, norm_mlp=m_norm_mlp, norm_final=m_norm_final)
    rep_v = dict(norm_mix=v_norm_mix, b_gate=v_b_gate, ssm_conv_b=v_ssm_conv_b, dt_bias=v_dt_bias, A_log=v_A_log, D_skip=v_D_skip,
                 ssm_norm_w=v_ssm_norm_w, norm_mlp=v_norm_mlp, norm_final=v_norm_final)

    def pack(d):
        segs = [jnp.pad(d[k].astype(F32).reshape(-1), (0, sizes[k] - d[k].size)) if k in d else jnp.zeros((sizes[k],), F32)
                for k in _SMALL_ORDER]
        return jnp.pad(jnp.concatenate(segs), (0, rows * LANES - n_small)).reshape(rows, LANES)

    sm = _adam(pack(rep_w), pack(rep_m), pack(rep_v), small_parts, "adam_small")
    sm = [s.reshape(-1) for s in sm]
    for k in _REPLICATED:
        n_k = rep_w[k].shape[0]
        res[k] = tuple(s[offs[k]:offs[k] + n_k] for s in sm)
    loss = sm[0][offs["loss"]]
    for k, w, m, v, K, full in (("sc_conv_w", sc_conv_w, m_sc_conv_w, v_sc_conv_w, SC_K, D),
                                ("ssm_conv_w", ssm_conv_w, m_ssm_conv_w, v_ssm_conv_w, SSM_K, n_xbc)):
        g_full = sm[0][offs[k]:offs[k] + K * full].reshape(K, full)
        cw = full // N_DEV
        g_mine = lax.dynamic_slice_in_dim(g_full, me * cw, cw, axis=1)
        res[k] = _adam(w, m, v, g_mine[None], "adam_" + k)

    order = ("norm_mix", "w_in", "b_gate", "sc_conv_w", "ssm_conv_w", "ssm_conv_b", "dt_bias", "A_log", "D_skip", "ssm_norm_w",
             "w_branch_sc", "w_branch_ssm", "w_out", "norm_mlp", "w_mlp1", "w_mlp2", "norm_final")
    outs = [loss, grad_x.reshape(1, T, D)]
    for j in range(4):
        outs += [res[k][j] for k in order]
    return tuple(outs)
```

```python
import functools

import jax
import jax.numpy as jnp
from jax import lax
from jax.experimental import pallas as pl
from jax.experimental.pallas import tpu as pltpu

F32 = jnp.float32
BF16 = jnp.bfloat16

EPS = 1e-6
N_DEV = 8
HEADDIM = 64
NSTATE = 128
CHUNK = 128
NGROUPS = 8
GROUP_W = 256
SC_K = 3
SSM_K = 4
LANES = 128

ADAM_LR = 0.001
ADAM_B1 = 0.9
ADAM_B2 = 0.999
ADAM_EPS = 1e-08
ADAM_WD = 0.01
ADAM_STEP = 10

NN = (((1,), (0,)), ((), ()))
NT = (((1,), (1,)), ((), ()))
TN = (((0,), (0,)), ((), ()))
_DIMS = {"nn": NN, "nt": NT, "tn": TN}

ANY = pl.BlockSpec(memory_space=pl.ANY)
MESH = pl.DeviceIdType.MESH


def _sds(shape, dtype):
    return jax.ShapeDtypeStruct(tuple(shape), dtype)


def _dot(a, b, dims=NN):
    return lax.dot_general(a, b, dims, preferred_element_type=F32)


def _dot3(a, b, dims=NN):
    return lax.dot_general(a, b, dims, preferred_element_type=F32, precision=lax.Precision.HIGH)


def _params(*sem):
    return pltpu.CompilerParams(dimension_semantics=tuple(sem))


def _call(body, *, grid, in_specs, out_specs, out_shape, args, name, sem, scratch=(), comm=None):
    if comm is None:
        outs = pl.pallas_call(body, grid=grid, in_specs=list(in_specs), out_specs=list(out_specs), out_shape=list(out_shape),
                              scratch_shapes=list(scratch), name=name, compiler_params=_params(*sem))(*args)
        return list(outs), None
    n, n_in, n_out, n_scr = comm.n, len(in_specs), len(out_shape), len(scratch)

    def wrapped(*refs):
        ins, c_in = refs[:n_in], refs[n_in:n_in + n]
        outs, c_out = refs[n_in + n:n_in + n + n_out], refs[n_in + n + n_out:n_in + 2 * n + n_out]
        rest = refs[n_in + 2 * n + n_out:]
        scr, sems = rest[:n_scr], rest[n_scr:]
        first, last = None, None
        for d, g in enumerate(grid):
            f, l = pl.program_id(d) == 0, pl.program_id(d) == g - 1
            first, last = (f, l) if first is None else (first & f, last & l)

        @pl.when(first)
        def _():
            comm.start(c_in, c_out, sems)

        body(*ins, *outs, *scr)

        @pl.when(last)
        def _():
            comm.finish(c_in, c_out, sems)

    outs = pl.pallas_call(
        wrapped, grid=grid, in_specs=list(in_specs) + [ANY] * n, out_specs=list(out_specs) + [ANY] * n,
        out_shape=list(out_shape) + comm.out_shape, scratch_shapes=list(scratch) + comm.scratch,
        input_output_aliases={n_in + i: n_out + o for i, o in comm.aliases.items()},
        name=name, compiler_params=_params(*["arbitrary"] * len(grid)))(*args, *comm.arrs)
    return list(outs[:n_out]), list(outs[n_out:])


MM_VMEM_BUDGET = 44 * 2 ** 20


def _mm_tiles(M, N, k_bytes, mn_bytes):
    best = None
    for tm in (2048, 1024, 512, 256, 128):
        for tn in (1024, 512, 256, 128):
            if M % tm or N % tn:
                continue
            need = 2 * ((tm + tn) * k_bytes + tm * tn * mn_bytes) + 4 * tm * tn * 4
            if need <= MM_VMEM_BUDGET and (best is None or (tm * tn, tm) > (best[0] * best[1], best[0])):
                best = (tm, tn)
    assert best is not None, (M, N, k_bytes, mn_bytes)
    return best


def _mm(a, b, *, mode, name, extras=(), epi=None, out_dtypes=(F32,), comm=None):
    a_list = list(a) if isinstance(a, (list, tuple)) else [a]
    b_list = list(b) if isinstance(b, (list, tuple)) else [b]
    if mode == "nn":
        M, N = a_list[0].shape[0], b_list[0].shape[1]
    elif mode == "nt":
        M, N = a_list[0].shape[0], b_list[0].shape[0]
    else:
        M, N = a_list[0].shape[1], b_list[0].shape[1]
    k_bytes = sum((av.shape[0] if mode == "tn" else av.shape[1]) * av.dtype.itemsize for av in a_list)
    mn_bytes = sum(e.dtype.itemsize for e in extras) + sum(jnp.dtype(d).itemsize for d in out_dtypes)
    tm, tn = _mm_tiles(min(M, 2048), min(N, 1024), k_bytes, mn_bytes) if M % 128 == 0 and N % 128 == 0 else (M, N)
    assert M % tm == 0 and N % tn == 0
    a_specs, b_specs = [], []
    for av, bv in zip(a_list, b_list):
        K = av.shape[0] if mode == "tn" else av.shape[1]
        a_specs.append(pl.BlockSpec((K, tm), lambda i, j: (0, i)) if mode == "tn" else pl.BlockSpec((tm, K), lambda i, j: (i, 0)))
        b_specs.append(pl.BlockSpec((tn, K), lambda i, j: (j, 0)) if mode == "nt" else pl.BlockSpec((K, tn), lambda i, j: (0, j)))
    mn_spec = pl.BlockSpec((tm, tn), lambda i, j: (i, j))
    n_p, n_ex = len(a_list), len(extras)
    dims = _DIMS[mode]

    def body(*refs):
        acc = _dot(refs[0][...], refs[n_p][...], dims)
        for p in range(1, n_p):
            acc = acc + _dot(refs[p][...], refs[n_p + p][...], dims)
        rest = refs[2 * n_p:]
        res = (acc,) if epi is None else epi(acc, *[r[...] for r in rest[:n_ex]])
        for o_ref, r in zip(rest[n_ex:], res):
            o_ref[...] = r.astype(o_ref.dtype)

    outs, carried = _call(
        body, grid=(M // tm, N // tn), in_specs=a_specs + b_specs + [mn_spec] * n_ex,
        out_specs=[mn_spec] * len(out_dtypes), out_shape=[_sds((M, N), d) for d in out_dtypes],
        args=a_list + b_list + list(extras), name=name, sem=("parallel", "parallel"), comm=comm)
    res = outs[0] if len(outs) == 1 else outs
    return res if comm is None else (res, carried)


def _epi_add(acc, r):
    return (acc + r,)


def _epi_add2(acc, r):
    s = acc + r
    return (s, s)


def _epi_relu2(acc):
    p = jnp.maximum(acc, 0.0)
    return (p * p,)


def _epi_relu2_bwd(acc, r):
    return (acc * (2.0 * jnp.sqrt(r.astype(F32))),)


def _row(tr, n):
    return pl.BlockSpec((tr, n), lambda i: (i, 0))


def _vec(n):
    return pl.BlockSpec((1, n), lambda i: (0, 0))


def _rms_fwd(x, w, name):
    T, D = x.shape
    tr = min(256, T)

    def body(x_ref, w_ref, o_ref):
        xv = x_ref[...]
        r = lax.rsqrt(jnp.mean(xv * xv, axis=-1, keepdims=True) + EPS)
        o_ref[...] = (xv * r * w_ref[...]).astype(BF16)

    return pl.pallas_call(body, grid=(T // tr,), in_specs=[_row(tr, D), _vec(D)], out_specs=_row(tr, D),
                          out_shape=_sds((T, D), BF16), name=name, compiler_params=_params("parallel"))(x, w)


def _rms_bwd(x, w, dh, dres, name):
    T, D = x.shape
    tr = min(256, T)

    def body(x_ref, w_ref, dh_ref, dres_ref, dx_ref, dxb_ref, dw_ref):
        @pl.when(pl.program_id(0) == 0)
        def _():
            dw_ref[...] = jnp.zeros_like(dw_ref)

        xv = x_ref[...]
        r = lax.rsqrt(jnp.mean(xv * xv, axis=-1, keepdims=True) + EPS)
        xh = xv * r
        dh_v = dh_ref[...]
        dw_ref[...] += jnp.sum(dh_v * xh, axis=0, keepdims=True)
        dxh = dh_v * w_ref[...]
        dx = r * (dxh - xh * jnp.mean(dxh * xh, axis=-1, keepdims=True)) + dres_ref[...]
        dx_ref[...] = dx
        dxb_ref[...] = dx.astype(BF16)

    return pl.pallas_call(
        body, grid=(T // tr,), in_specs=[_row(tr, D), _vec(D), _row(tr, D), _row(tr, D)],
        out_specs=[_row(tr, D), _row(tr, D), _vec(D)],
        out_shape=[_sds((T, D), F32), _sds((T, D), BF16), _sds((1, D), F32)],
        name=name, compiler_params=_params("arbitrary"))(x, w, dh, dres)


def _final(x2, w, tgt, name):
    T, D = x2.shape
    tr = min(256, T)

    def body(x_ref, w_ref, t_ref, dx_ref, dxb_ref, dw_ref, loss_ref):
        @pl.when(pl.program_id(0) == 0)
        def _():
            dw_ref[...] = jnp.zeros_like(dw_ref)
            loss_ref[...] = jnp.zeros_like(loss_ref)

        xv = x_ref[...]
        wv = w_ref[...]
        r = lax.rsqrt(jnp.mean(xv * xv, axis=-1, keepdims=True) + EPS)
        xh = xv * r
        err = xh * wv - t_ref[...]
        part = jnp.sum(jnp.sum(err * err, axis=1, keepdims=True), axis=0, keepdims=True) * (0.5 / D)
        loss_ref[...] += jnp.broadcast_to(part, loss_ref.shape)
        dy = err * (1.0 / D)
        dw_ref[...] += jnp.sum(dy * xh, axis=0, keepdims=True)
        dxh = dy * wv
        dx = r * (dxh - xh * jnp.mean(dxh * xh, axis=-1, keepdims=True))
        dx_ref[...] = dx
        dxb_ref[...] = dx.astype(BF16)

    return pl.pallas_call(
        body, grid=(T // tr,), in_specs=[_row(tr, D), _vec(D), _row(tr, D)],
        out_specs=[_row(tr, D), _row(tr, D), _vec(D), _vec(LANES)],
        out_shape=[_sds((T, D), F32), _sds((T, D), BF16), _sds((1, D), F32), _sds((1, LANES), F32)],
        name=name, compiler_params=_params("arbitrary"))(x2, w, tgt)


def _silu_parts(z):
    s = jax.nn.sigmoid(z)
    return z * s, s * (1.0 + z * (1.0 - s))


def _gnorm_fwd(y, z, w, name):
    T, N = y.shape
    tr = min(256, T)

    def body(y_ref, z_ref, w_ref, o_ref):
        for g in range(N // GROUP_W):
            sl = slice(g * GROUP_W, (g + 1) * GROUP_W)
            silu, _ = _silu_parts(z_ref[:, sl])
            yz = y_ref[:, sl] * silu
            r = lax.rsqrt(jnp.mean(yz * yz, axis=-1, keepdims=True) + EPS)
            o_ref[:, sl] = (yz * r * w_ref[:, sl]).astype(BF16)

    return pl.pallas_call(body, grid=(T // tr,), in_specs=[_row(tr, N), _row(tr, N), _vec(N)], out_specs=_row(tr, N),
                          out_shape=_sds((T, N), BF16), name=name, compiler_params=_params("parallel"))(y, z, w)


def _gnorm_bwd(y, z, w, dyb, name):
    T, N = y.shape
    tr = min(256, T)

    def body(y_ref, z_ref, w_ref, d_ref, dy_ref, dz_ref, dw_ref):
        @pl.when(pl.program_id(0) == 0)
        def _():
            dw_ref[...] = jnp.zeros_like(dw_ref)

        for g in range(N // GROUP_W):
            sl = slice(g * GROUP_W, (g + 1) * GROUP_W)
            yv = y_ref[:, sl]
            silu, dsilu = _silu_parts(z_ref[:, sl])
            yz = yv * silu
            r = lax.rsqrt(jnp.mean(yz * yz, axis=-1, keepdims=True) + EPS)
            yzh = yz * r
            d = d_ref[:, sl]
            dw_ref[:, sl] += jnp.sum(d * yzh, axis=0, keepdims=True)
            dyzh = d * w_ref[:, sl]
            dyz = r * (dyzh - yzh * jnp.mean(dyzh * yzh, axis=-1, keepdims=True))
            dy_ref[:, sl] = dyz * silu
            dz_ref[:, sl] = (dyz * yv * dsilu).astype(BF16)

    return pl.pallas_call(
        body, grid=(T // tr,), in_specs=[_row(tr, N), _row(tr, N), _vec(N), _row(tr, N)],
        out_specs=[_row(tr, N), _row(tr, N), _vec(N)],
        out_shape=[_sds((T, N), F32), _sds((T, N), BF16), _sds((1, N), F32)],
        name=name, compiler_params=_params("arbitrary"))(y, z, w, dyb)


def _merge_fwd(gate_raw, b_gate, br_a, br_b, name):
    T, D = br_a.shape
    tr = min(256, T)

    def body(g_ref, bg_ref, a_ref, b_ref, o_ref):
        g = jax.nn.sigmoid(g_ref[...] + bg_ref[...])
        o_ref[...] = (g[:, :D] * a_ref[...] + g[:, D:] * b_ref[...]).astype(BF16)

    return pl.pallas_call(body, grid=(T // tr,), in_specs=[_row(tr, 2 * D), _vec(2 * D), _row(tr, D), _row(tr, D)],
                          out_specs=_row(tr, D), out_shape=_sds((T, D), BF16), name=name,
                          compiler_params=_params("parallel"))(gate_raw, b_gate, br_a, br_b)


def _merge_bwd(dmerged, gate_raw, b_gate, br_a, br_b, name):
    T, D = br_a.shape
    tr = min(256, T)

    def body(d_ref, g_ref, bg_ref, a_ref, b_ref, da_ref, db_ref, dg_ref, dbg_ref):
        @pl.when(pl.program_id(0) == 0)
        def _():
            dbg_ref[...] = jnp.zeros_like(dbg_ref)

        g = jax.nn.sigmoid(g_ref[...] + bg_ref[...])
        d = d_ref[...]
        da_ref[...] = (d * g[:, :D]).astype(BF16)
        db_ref[...] = (d * g[:, D:]).astype(BF16)
        dg = jnp.concatenate([d * a_ref[...], d * b_ref[...]], axis=1) * g * (1.0 - g)
        dg_ref[...] = dg.astype(BF16)
        dbg_ref[...] += jnp.sum(dg, axis=0, keepdims=True)

    return pl.pallas_call(
        body, grid=(T // tr,), in_specs=[_row(tr, D), _row(tr, 2 * D), _vec(2 * D), _row(tr, D), _row(tr, D)],
        out_specs=[_row(tr, D), _row(tr, D), _row(tr, 2 * D), _vec(2 * D)],
        out_shape=[_sds((T, D), BF16), _sds((T, D), BF16), _sds((T, 2 * D), BF16), _sds((1, 2 * D), F32)],
        name=name, compiler_params=_params("arbitrary"))(dmerged, gate_raw, b_gate, br_a, br_b)


def _shift_down(u, s):
    if s == 0:
        return u
    row = lax.broadcasted_iota(jnp.int32, u.shape, 0)
    return jnp.where(row >= s, pltpu.roll(u, s, 0), 0.0)


def _shift_up(u, s):
    if s == 0:
        return u
    n = u.shape[0]
    row = lax.broadcasted_iota(jnp.int32, u.shape, 0)
    return jnp.where(row < n - s, pltpu.roll(u, n - s, 0), 0.0)


def _conv(u, w_ref, K):
    acc = u * w_ref[K - 1:K, :]
    for k in range(K - 1):
        acc = acc + _shift_down(u, K - 1 - k) * w_ref[k:k + 1, :]
    return acc


def _conv_bwd(u, dc, w_ref, dw_ref, K):
    du = dc * w_ref[K - 1:K, :]
    dw_ref[K - 1:K, :] = jnp.sum(dc * u, axis=0, keepdims=True)
    for k in range(K - 1):
        s = K - 1 - k
        dw_ref[k:k + 1, :] = jnp.sum(dc * _shift_down(u, s), axis=0, keepdims=True)
        du = du + _shift_up(dc, s) * w_ref[k:k + 1, :]
    return du


CB_W = 256


def _col(T, j0=0):
    return pl.BlockSpec((T, CB_W), lambda j: (0, j + j0))


def _sc_fwd(psc, w, name):
    T, D = psc.shape[0], psc.shape[1] // 3
    nb = D // CB_W

    def body(b_ref, c_ref, x_ref, w_ref, o_ref):
        o_ref[...] = (b_ref[...] * _conv(c_ref[...] * x_ref[...], w_ref, SC_K)).astype(BF16)

    return pl.pallas_call(
        body, grid=(nb,), in_specs=[_col(T), _col(T, nb), _col(T, 2 * nb), pl.BlockSpec((SC_K, CB_W), lambda j: (0, j))],
        out_specs=_col(T), out_shape=_sds((T, D), BF16), name=name, compiler_params=_params("parallel"))(psc, psc, psc, w)


def _sc_bwd(psc, w, dya, name):
    T, D = psc.shape[0], psc.shape[1] // 3
    nb = D // CB_W

    def body(b_ref, c_ref, x_ref, w_ref, d_ref, db_ref, dc_ref, dx_ref, dw_ref):
        cv, xv, d = c_ref[...], x_ref[...], d_ref[...]
        u = cv * xv
        db_ref[...] = (d * _conv(u, w_ref, SC_K)).astype(BF16)
        du = _conv_bwd(u, d * b_ref[...], w_ref, dw_ref, SC_K)
        dc_ref[...] = (du * xv).astype(BF16)
        dx_ref[...] = (du * cv).astype(BF16)

    wspec = pl.BlockSpec((SC_K, CB_W), lambda j: (0, j))
    return pl.pallas_call(
        body, grid=(nb,), in_specs=[_col(T), _col(T, nb), _col(T, 2 * nb), wspec, _col(T)],
        out_specs=[_col(T), _col(T), _col(T), wspec],
        out_shape=[_sds((T, D), BF16)] * 3 + [_sds((SC_K, D), F32)],
        name=name, compiler_params=_params("parallel"))(psc, psc, psc, w, dya)


def _ssm_conv_fwd(u, w, b, name):
    T, N = u.shape

    def body(u_ref, w_ref, b_ref, o_ref):
        c = _conv(u_ref[...], w_ref, SSM_K) + b_ref[...]
        o_ref[...] = c * jax.nn.sigmoid(c)

    return pl.pallas_call(
        body, grid=(N // CB_W,), in_specs=[_col(T), pl.BlockSpec((SSM_K, CB_W), lambda j: (0, j)), pl.BlockSpec((1, CB_W), lambda j: (0, j))],
        out_specs=_col(T), out_shape=_sds((T, N), F32), name=name, compiler_params=_params("parallel"))(u, w, b)


def _ssm_conv_bwd(u, w, b, dxs, dB, dC, name, comm=None):
    T, N = u.shape
    n_x, n_b = dxs.shape[1] // CB_W, dB.shape[1] // CB_W

    def body(u_ref, w_ref, b_ref, dx_ref, db_ref, dc_ref, du_ref, dw_ref, dbias_ref):
        j = pl.program_id(0)
        uv = u_ref[...]
        c = _conv(uv, w_ref, SSM_K) + b_ref[...]
        _, dsilu = _silu_parts(c)
        d = jnp.where(j < n_x, dx_ref[...], jnp.where(j < n_x + n_b, db_ref[...], dc_ref[...])) * dsilu
        dbias_ref[...] = jnp.sum(d, axis=0, keepdims=True)
        du_ref[...] = _conv_bwd(uv, d, w_ref, dw_ref, SSM_K).astype(BF16)

    wspec = pl.BlockSpec((SSM_K, CB_W), lambda j: (0, j))
    bspec = pl.BlockSpec((1, CB_W), lambda j: (0, j))
    outs, carried = _call(
        body, grid=(N // CB_W,),
        in_specs=[_col(T), wspec, bspec,
                  pl.BlockSpec((T, CB_W), lambda j: (0, jnp.minimum(j, n_x - 1))),
                  pl.BlockSpec((T, CB_W), lambda j: (0, jnp.clip(j - n_x, 0, n_b - 1))),
                  pl.BlockSpec((T, CB_W), lambda j: (0, jnp.clip(j - n_x - n_b, 0, n_b - 1)))],
        out_specs=[_col(T), wspec, bspec],
        out_shape=[_sds((T, N), BF16), _sds((SSM_K, N), F32), _sds((1, N), F32)],
        args=[u, w, b, dxs, dB, dC], name=name, sem=("parallel",), comm=comm)
    return outs if comm is None else (outs, carried)


def _split3(v):
    hi = v.astype(BF16)
    r = v - hi.astype(F32)
    mid = r.astype(BF16)
    lo = (r - mid.astype(F32)).astype(BF16)
    return hi, mid, lo


def _head_expand(n_lanes):
    h = lax.broadcasted_iota(jnp.int32, (LANES, n_lanes), 0)
    l = lax.broadcasted_iota(jnp.int32, (LANES, n_lanes), 1)
    return (jnp.right_shift(l, HEADDIM.bit_length() - 1) == h).astype(BF16)


def _softplus(v):
    return jnp.maximum(v, 0.0) + jnp.log1p(jnp.exp(-jnp.abs(v)))


def _ssd_prep(dt_raw, dt_bias, a_log, n_inner, name):
    T = dt_raw.shape[0]

    def body(r_ref, b_ref, al_ref, dt_ref, cs_ref):
        dt = _softplus(r_ref[...] + b_ref[...])
        a = dt * (-jnp.exp(al_ref[...]))
        i = lax.broadcasted_iota(jnp.int32, (CHUNK, CHUNK), 0)
        j = lax.broadcasted_iota(jnp.int32, (CHUNK, CHUNK), 1)
        tri = (j <= i).astype(BF16)
        cs = sum(_dot(tri, p) for p in _split3(a))
        ex = _head_expand(n_inner)
        dt_ref[...] = sum(_dot(p, ex) for p in _split3(dt))
        cs_ref[...] = sum(_dot(p, ex) for p in _split3(cs))

    blk = pl.BlockSpec((CHUNK, LANES), lambda c: (c, 0))
    out = pl.BlockSpec((CHUNK, n_inner), lambda c: (c, 0))
    return pl.pallas_call(body, grid=(T // CHUNK,), in_specs=[blk, _vec(LANES), _vec(LANES)], out_specs=[out, out],
                          out_shape=[_sds((T, n_inner), F32)] * 2, name=name, compiler_params=_params("parallel"))(dt_raw, dt_bias, a_log)


def _pair_terms(cs_p):
    lane = lax.broadcasted_iota(jnp.int32, (CHUNK, CHUNK), 1)
    sub = lax.broadcasted_iota(jnp.int32, (CHUNK, CHUNK), 0)
    csT = cs_p.T
    Ls = []
    for k in range(2):
        col = jnp.sum(jnp.where(lane == k * HEADDIM, cs_p, 0.0), axis=1, keepdims=True)
        rowv = csT[k * HEADDIM:k * HEADDIM + 1, :]
        Ls.append(jnp.exp(jnp.where(sub >= lane, col - rowv, -jnp.inf)))
    return Ls, jnp.exp(csT[:, CHUNK - 1:CHUNK])


def _block_diag(xp):
    lane = lax.broadcasted_iota(jnp.int32, xp.shape, 1)
    return jnp.concatenate([jnp.where(lane < HEADDIM, xp, 0.0), jnp.where(lane >= HEADDIM, xp, 0.0)], axis=0)


SSD_GROUPS_PER_STEP = 8


def _ssd_specs(T, n_inner):
    nc, gs = T // CHUNK, SSD_GROUPS_PER_STEP
    bo, co = n_inner // (gs * NSTATE), (n_inner + NGROUPS * NSTATE) // (gs * NSTATE)
    assert NGROUPS % gs == 0 and n_inner % (gs * NSTATE) == 0 and (NGROUPS * NSTATE) % (gs * NSTATE) == 0
    g_blk = lambda f: pl.BlockSpec((CHUNK, gs * GROUP_W), lambda c, s: (f(c), s))
    b_blk = lambda f: pl.BlockSpec((CHUNK, gs * NSTATE), lambda c, s: (f(c), bo + s))
    c_blk = lambda f: pl.BlockSpec((CHUNK, gs * NSTATE), lambda c, s: (f(c), co + s))
    return nc, g_blk, b_blk, c_blk


def _ssd_fwd(xbc, dt_e, cs_e, d_e, name, comm=None):
    T = xbc.shape[0]
    n_inner = dt_e.shape[1]
    nc, g_blk, b_blk, c_blk = _ssd_specs(T, n_inner)
    ident = lambda c: c

    gs = SSD_GROUPS_PER_STEP

    def body(xs_ref, b_ref, c_ref, dt_ref, cs_ref, d_ref, y_ref, p_ref, st):
        c, s = pl.program_id(0), pl.program_id(1)

        @pl.when(c == 0)
        def _():
            for gi in range(gs):
                st[s * gs + gi] = jnp.zeros((GROUP_W, NSTATE), F32)

        for gi in range(gs):
            g = s * gs + gi
            gw, gn = slice(gi * GROUP_W, (gi + 1) * GROUP_W), slice(gi * NSTATE, (gi + 1) * NSTATE)
            P = st[g]
            p_ref[0, gi] = P
            xs, dt, cs = xs_ref[:, gw], dt_ref[:, gw], cs_ref[:, gw]
            Bf, Cf = b_ref[:, gn], c_ref[:, gn]
            CBm = _dot3(Cf, Bf, NT)
            X = xs * dt
            decay = jnp.exp(cs[CHUNK - 1:CHUNK, :] - cs)
            y_off = _dot3(Cf, P, NT) * jnp.exp(cs)
            ys, ecl = [], []
            for pr in range(2):
                sl = slice(pr * LANES, (pr + 1) * LANES)
                Ls, e_last = _pair_terms(cs[:, sl])
                ecl.append(e_last)
                Mcat = jnp.concatenate([CBm * L for L in Ls], axis=1)
                ys.append(_dot3(Mcat, _block_diag(X[:, sl])))
            y_ref[:, gw] = jnp.concatenate(ys, axis=1) + y_off + xs * d_ref[:, gw]
            S = _dot3(X * decay, Bf, TN)
            st[g] = P * jnp.concatenate(ecl, axis=0) + S

    p_blk = pl.BlockSpec((1, gs, GROUP_W, NSTATE), lambda c, s: (c, s, 0, 0))
    outs, carried = _call(
        body, grid=(nc, NGROUPS // gs),
        in_specs=[g_blk(ident), b_blk(ident), c_blk(ident), g_blk(ident), g_blk(ident), pl.BlockSpec((1, gs * GROUP_W), lambda c, s: (0, s))],
        out_specs=[g_blk(ident), p_blk],
        out_shape=[_sds((T, n_inner), F32), _sds((nc, NGROUPS, GROUP_W, NSTATE), F32)],
        scratch=[pltpu.VMEM((NGROUPS, GROUP_W, NSTATE), F32)],
        args=[xbc, xbc, xbc, dt_e, cs_e, d_e], name=name, sem=("arbitrary", "arbitrary"), comm=comm)
    return outs if comm is None else (outs, carried)


def _ssd_bwd(xbc, dt_e, cs_e, d_e, states, dy, name, comm=None):
    T = xbc.shape[0]
    n_inner = dt_e.shape[1]
    nc, g_blk, b_blk, c_blk = _ssd_specs(T, n_inner)
    rev = lambda c: nc - 1 - c

    gs = SSD_GROUPS_PER_STEP

    def body(xs_ref, b_ref, c_ref, dt_ref, cs_ref, d_ref, p_ref, pn_ref, dy_ref,
             dxs_ref, db_ref, dc_ref, ddt_ref, dcs_ref, dd_ref, dst):
        cc, s = pl.program_id(0), pl.program_id(1)

        @pl.when(cc == 0)
        def _():
            for gi in range(gs):
                dst[s * gs + gi] = jnp.zeros((GROUP_W, NSTATE), F32)

        for gi in range(gs):
            one_group(s * gs + gi, gi, xs_ref, b_ref, c_ref, dt_ref, cs_ref, d_ref, p_ref, pn_ref, dy_ref,
                      dxs_ref, db_ref, dc_ref, ddt_ref, dcs_ref, dd_ref, dst)

    def one_group(g, gi, xs_ref, b_ref, c_ref, dt_ref, cs_ref, d_ref, p_ref, pn_ref, dy_ref,
                  dxs_ref, db_ref, dc_ref, ddt_ref, dcs_ref, dd_ref, dst):
        gw, gn = slice(gi * GROUP_W, (gi + 1) * GROUP_W), slice(gi * NSTATE, (gi + 1) * NSTATE)
        dS = dst[g]
        P, Pn = p_ref[0, gi], pn_ref[0, gi]
        xs, dt, cs, dY = xs_ref[:, gw], dt_ref[:, gw], cs_ref[:, gw], dy_ref[:, gw]
        Bf, Cf = b_ref[:, gn], c_ref[:, gn]
        Bb, Cb = Bf.astype(BF16), Cf.astype(BF16)
        X = xs * dt
        ecs = jnp.exp(cs)
        decay = jnp.exp(cs[CHUNK - 1:CHUNK, :] - cs)
        CBm = _dot3(Cf, Bf, NT)
        dYe = dY * ecs
        dP_off = _dot3(dYe, Cf, TN)
        dC = _dot(dYe.astype(BF16), P.astype(BF16))
        dcs = dYe * _dot3(Cf, P, NT)
        Xd = X * decay
        dB = _dot(Xd.astype(BF16), dS.astype(BF16))
        E = _dot3(Bf, dS, NT)
        dX = E * decay
        dcs = dcs - E * Xd
        R = _dot3(jnp.ones((8, NSTATE), F32), dS * Pn, NT)
        sub_g = lax.broadcasted_iota(jnp.int32, (CHUNK, GROUP_W), 0)
        dcs = dcs + jnp.where(sub_g == CHUNK - 1, R[0:1, :], 0.0)
        lane = lax.broadcasted_iota(jnp.int32, (CHUNK, CHUNK), 1)
        sub = lax.broadcasted_iota(jnp.int32, (CHUNK, CHUNK), 0)
        dCB = jnp.zeros((CHUNK, CHUNK), F32)
        dXs, dcss, ecl = [], [], []
        for pr in range(2):
            sl = slice(pr * LANES, (pr + 1) * LANES)
            Ls, e_last = _pair_terms(cs[:, sl])
            ecl.append(e_last)
            dYp = dY[:, sl]
            dMcat = _dot3(dYp, _block_diag(X[:, sl]), NT)
            Mcat = jnp.concatenate([CBm * L for L in Ls], axis=1)
            dXt = _dot3(Mcat, dYp, TN)
            dXs.append(jnp.where(lane < HEADDIM, dXt[:CHUNK], dXt[CHUNK:]))
            colacc = jnp.zeros((CHUNK, CHUNK), F32)
            rowacc = jnp.zeros((CHUNK, CHUNK), F32)
            for k in range(2):
                dG = dMcat[:, k * CHUNK:(k + 1) * CHUNK] * Ls[k]
                dCB = dCB + dG
                Q = dG * CBm
                colacc = colacc + jnp.where(lane == k * HEADDIM, jnp.sum(Q, axis=1, keepdims=True), 0.0)
                rowacc = rowacc + jnp.where(sub == k * HEADDIM, jnp.sum(Q, axis=0, keepdims=True), 0.0)
            dcss.append(colacc - rowacc.T)
        dX = dX + jnp.concatenate(dXs, axis=1)
        dcs = dcs + jnp.concatenate(dcss, axis=1)
        dCBb = dCB.astype(BF16)
        dc_ref[:, gn] = dC + _dot(dCBb, Bb)
        db_ref[:, gn] = dB + _dot(dCBb, Cb, TN)
        dxs_ref[:, gw] = dX * dt + dY * d_ref[:, gw]
        ddt_ref[:, gw] = dX * xs
        dcs_ref[:, gw] = dcs
        dd_ref[0, :, gw] = jnp.sum(dY * xs, axis=0, keepdims=True)
        dst[g] = dS * jnp.concatenate(ecl, axis=0) + dP_off

    p_blk = pl.BlockSpec((1, gs, GROUP_W, NSTATE), lambda c, s: (nc - 1 - c, s, 0, 0))
    pn_blk = pl.BlockSpec((1, gs, GROUP_W, NSTATE), lambda c, s: (jnp.minimum(nc - c, nc - 1), s, 0, 0))
    st_blk = pl.BlockSpec((CHUNK, gs * NSTATE), lambda c, s: (nc - 1 - c, s))
    outs, carried = _call(
        body, grid=(nc, NGROUPS // gs),
        in_specs=[g_blk(rev), b_blk(rev), c_blk(rev), g_blk(rev), g_blk(rev), pl.BlockSpec((1, gs * GROUP_W), lambda c, s: (0, s)),
                  p_blk, pn_blk, g_blk(rev)],
        out_specs=[g_blk(rev), st_blk, st_blk, g_blk(rev), g_blk(rev), pl.BlockSpec((1, 1, gs * GROUP_W), lambda c, s: (nc - 1 - c, 0, s))],
        out_shape=[_sds((T, n_inner), F32), _sds((T, NGROUPS * NSTATE), F32), _sds((T, NGROUPS * NSTATE), F32),
                   _sds((T, n_inner), F32), _sds((T, n_inner), F32), _sds((nc, 1, n_inner), F32)],
        scratch=[pltpu.VMEM((NGROUPS, GROUP_W, NSTATE), F32)],
        args=[xbc, xbc, xbc, dt_e, cs_e, d_e, states, states, dy], name=name, sem=("arbitrary", "arbitrary"), comm=comm)
    return outs if comm is None else (outs, carried)


def _ssd_post(ddt_e, dcs_e, dd_p, dt_raw, dt_bias, a_log, n_heads, name):
    T, n_inner = ddt_e.shape

    def body(ddt_ref, dcs_ref, dd_ref, r_ref, b_ref, al_ref, draw_ref, dbias_ref, dal_ref, ddsk_ref):
        @pl.when(pl.program_id(0) == 0)
        def _():
            dbias_ref[...] = jnp.zeros_like(dbias_ref)
            dal_ref[...] = jnp.zeros_like(dal_ref)
            ddsk_ref[...] = jnp.zeros_like(ddsk_ref)

        ex = _head_expand(n_inner)
        red = lambda v: sum(_dot(p, ex, NT) for p in _split3(v))
        raw = r_ref[...] + b_ref[...]
        dt = _softplus(raw)
        A = -jnp.exp(al_ref[...])
        i = lax.broadcasted_iota(jnp.int32, (CHUNK, CHUNK), 0)
        j = lax.broadcasted_iota(jnp.int32, (CHUNK, CHUNK), 1)
        upper = (j >= i).astype(BF16)
        da = sum(_dot(upper, p) for p in _split3(red(dcs_ref[...])))
        ddt = red(ddt_ref[...]) + da * A
        lane = lax.broadcasted_iota(jnp.int32, (CHUNK, LANES), 1)
        draw = jnp.where(lane < n_heads, ddt * jax.nn.sigmoid(raw), 0.0)
        draw_ref[...] = draw.astype(BF16)
        dbias_ref[...] += jnp.sum(draw, axis=0, keepdims=True)
        dal_ref[...] += jnp.sum(da * dt, axis=0, keepdims=True) * A
        ddsk_ref[...] += red(jnp.broadcast_to(dd_ref[0], (8, n_inner)))[0:1, :]

    wide = pl.BlockSpec((CHUNK, n_inner), lambda c: (c, 0))
    blk = pl.BlockSpec((CHUNK, LANES), lambda c: (c, 0))
    return pl.pallas_call(
        body, grid=(T // CHUNK,),
        in_specs=[wide, wide, pl.BlockSpec((1, 1, n_inner), lambda c: (c, 0, 0)), blk, _vec(LANES), _vec(LANES)],
        out_specs=[blk, _vec(LANES), _vec(LANES), _vec(LANES)],
        out_shape=[_sds((T, LANES), BF16)] + [_sds((1, LANES), F32)] * 3,
        name=name, compiler_params=_params("arbitrary"))(ddt_e, dcs_e, dd_p, dt_raw, dt_bias, a_log)


def _row2(v):
    return v.reshape(1, -1).astype(F32)


def _pad_lanes(v):
    return jnp.pad(_row2(v), ((0, 0), (0, LANES - v.shape[-1])))


class _NoExchange:
    def __init__(self, W):
        self.W, self.grads = W, {}

    def weight(self, k):
        return self.W[k]

    def carry(self, name):
        return None

    def carried(self, name, outs):
        pass

    def grad(self, k, g):
        self.grads[k] = g


def _local_step(x, tgt, S, small):
    T, D = x.shape

    def mm(a, b, *, name, **kw):
        comm = S.carry(name)
        if comm is None:
            return _mm(a, b, name=name, **kw)
        res, outs = _mm(a, b, name=name, comm=comm, **kw)
        S.carried(name, outs)
        return res

    def carrying(fn, *args, name):
        comm = S.carry(name)
        if comm is None:
            return fn(*args, name)
        res, outs = fn(*args, name, comm=comm)
        S.carried(name, outs)
        return res

    n_inner = 2 * D
    n_heads = n_inner // HEADDIM
    norm_mix, norm_mlp, norm_final = _row2(small["norm_mix"]), _row2(small["norm_mlp"]), _row2(small["norm_final"])
    b_gate, ssm_b, ssm_norm_w = _row2(small["b_gate"]), _row2(small["ssm_conv_b"]), _row2(small["ssm_norm_w"])
    dt_bias, a_log = _pad_lanes(small["dt_bias"]), _pad_lanes(small["A_log"])
    d_e = jnp.repeat(small["D_skip"].astype(F32), HEADDIM).reshape(1, n_inner)
    sc_w, ssm_w = small["sc_conv_w"], small["ssm_conv_w"]

    hb = _rms_fwd(x, norm_mix, "rms_mix")
    p_xbc = mm(hb, S.weight("xbc"), mode="nn", name="proj_xbc")
    p_dt = mm(hb, S.weight("dt"), mode="nn", name="proj_dt")
    p_z = mm(hb, S.weight("z"), mode="nn", name="proj_z")
    p_sc = mm(hb, S.weight("sc"), mode="nn", name="proj_sc")
    p_gate = mm(hb, S.weight("gate"), mode="nn", name="proj_gate")
    xbc = _ssm_conv_fwd(p_xbc, ssm_w, ssm_b, "ssm_conv_fwd")
    dt_e, cs_e = _ssd_prep(p_dt, dt_bias, a_log, n_inner, "ssd_prep")
    y, states = carrying(_ssd_fwd, xbc, dt_e, cs_e, d_e, name="ssd_fwd")
    yb = _gnorm_fwd(y, p_z, ssm_norm_w, "gnorm_fwd")
    ya = _sc_fwd(p_sc, sc_w, "sc_fwd")
    br_a = mm(ya, S.weight("bsc"), mode="nn", name="branch_sc")
    br_b = mm(yb, S.weight("bssm"), mode="nn", name="branch_ssm")
    merged = _merge_fwd(p_gate, b_gate, br_a, br_b, "merge_fwd")
    x1 = mm(merged, S.weight("out"), mode="nn", name="out_proj", extras=(x,), epi=_epi_add)
    h2 = _rms_fwd(x1, norm_mlp, "rms_mlp")
    r_act = mm(h2, S.weight("w1"), mode="nn", name="mlp_up", epi=_epi_relu2, out_dtypes=(BF16,))
    x2 = mm(r_act, S.weight("w2"), mode="nn", name="mlp_down", extras=(x1,), epi=_epi_add)
    dx2, dx2b, g_norm_final, loss_row = _final(x2, norm_final, tgt, "final")

    S.grad("w2", mm(r_act, dx2b, mode="tn", name="mlp_down_dw", out_dtypes=(BF16,)))
    da = mm(dx2b, S.weight("w2"), mode="nt", name="mlp_down_dx", extras=(r_act,), epi=_epi_relu2_bwd, out_dtypes=(BF16,))
    S.grad("w1", mm(h2, da, mode="tn", name="mlp_up_dw", out_dtypes=(BF16,)))
    dh2 = mm(da, S.weight("w1"), mode="nt", name="mlp_up_dx")
    dx1, dx1b, g_norm_mlp = _rms_bwd(x1, norm_mlp, dh2, dx2, "rms_mlp_bwd")
    S.grad("out", mm(merged, dx1b, mode="tn", name="out_proj_dw", out_dtypes=(BF16,)))
    dmerged = mm(dx1b, S.weight("out"), mode="nt", name="out_proj_dx")
    dbr_a, dbr_b, d_gate, g_b_gate = _merge_bwd(dmerged, p_gate, b_gate, br_a, br_b, "merge_bwd")
    S.grad("bssm", mm(yb, dbr_b, mode="tn", name="branch_ssm_dw", out_dtypes=(BF16,)))
    S.grad("bsc", mm(ya, dbr_a, mode="tn", name="branch_sc_dw", out_dtypes=(BF16,)))
    dyb = mm(dbr_b, S.weight("bssm"), mode="nt", name="branch_ssm_dx")
    dya = mm(dbr_a, S.weight("bsc"), mode="nt", name="branch_sc_dx")
    dy, d_z, g_ssm_norm_w = _gnorm_bwd(y, p_z, ssm_norm_w, dyb, "gnorm_bwd")
    dxs, dB, dC, ddt_e, dcs_e, dd_p = carrying(_ssd_bwd, xbc, dt_e, cs_e, d_e, states, dy, name="ssd_bwd")
    d_dt, g_dt_bias, g_a_log, g_d_skip = _ssd_post(ddt_e, dcs_e, dd_p, p_dt, dt_bias, a_log, n_heads, "ssd_post")
    d_xbc, g_ssm_w, g_ssm_b = carrying(_ssm_conv_bwd, p_xbc, ssm_w, ssm_b, dxs, dB, dC, name="ssm_conv_bwd")
    d_scB, d_scC, d_scX, g_sc_w = _sc_bwd(p_sc, sc_w, dya, "sc_bwd")
    d_sc = jnp.concatenate([d_scB, d_scC, d_scX], axis=1)
    pieces = [("sc", d_sc), ("z", d_z), ("xbc", d_xbc), ("dt", d_dt), ("gate", d_gate)]
    S.grad("win", {k: mm(hb, d, mode="tn", name="proj_dw_" + k, out_dtypes=(BF16,)) for k, d in pieces})
    dh = mm([d for _, d in pieces], [S.weight(k) for k, _ in pieces], mode="nt", name="proj_dx")
    grad_x, _, g_norm_mix = _rms_bwd(x, norm_mix, dh, dx1, "rms_mix_bwd")

    g_small = dict(norm_mix=g_norm_mix, b_gate=g_b_gate, sc_conv_w=g_sc_w, ssm_conv_w=g_ssm_w, ssm_conv_b=g_ssm_b,
                   dt_bias=g_dt_bias, A_log=g_a_log, D_skip=g_d_skip, ssm_norm_w=g_ssm_norm_w, norm_mlp=g_norm_mlp,
                   norm_final=g_norm_final, loss=loss_row)
    return grad_x, g_small


class _Place:
    def __init__(self, k=0):
        x, y, c = lax.axis_index("x"), lax.axis_index("y"), lax.axis_index("c")
        self.x = 1 - x if k & 4 else x
        self.y = 1 - y if k & 2 else y
        self.c = 1 - c if k & 1 else c
        self.chip = 2 * self.x + self.y
        self.id = 2 * self.chip + self.c


ICI_PEERS = (2, 4, 6)
SIBLING = (1,)
ALL_PEERS = (1, 2, 3, 4, 5, 6, 7)


class _Comm:
    def __init__(self, arrs, out_shape, ks, src, dst, own=None, aliases=None):
        self.arrs, self.out_shape, self.ks = list(arrs), list(out_shape), tuple(ks)
        self.n = len(self.arrs)
        self.src, self.dst, self.own = src, dst, own
        self.aliases = aliases or {}
        dma = pltpu.SemaphoreType.DMA
        self.scratch = [dma((self.n, len(self.ks))), dma((self.n, len(self.ks))), dma((self.n,))]

    def _copies(self, ins, outs, sems, with_recvs):
        send_sems, recv_sems, local_sems = sems
        me = _Place()
        owns, sends, recvs = [], [], []
        for a in range(self.n):
            if self.own is not None:
                s, d = self.own(a, ins[a], outs[a], me)
                owns.append(pltpu.make_async_copy(s, d, local_sems.at[a]))
            for i, k in enumerate(self.ks):
                peer = _Place(k)
                for sender, lst in ((me, sends), (peer, recvs)) if with_recvs else ((me, sends),):
                    lst.append(pltpu.make_async_remote_copy(
                        src_ref=self.src(a, ins[a], me, peer), dst_ref=self.dst(a, outs[a], sender),
                        send_sem=send_sems.at[a, i], recv_sem=recv_sems.at[a, i],
                        device_id=(peer.x, peer.y, peer.c), device_id_type=MESH))
        return owns, sends, recvs

    def start(self, ins, outs, sems):
        owns, sends, _ = self._copies(ins, outs, sems, False)
        for cp in owns + sends:
            cp.start()

    def finish(self, ins, outs, sems):
        owns, sends, recvs = self._copies(ins, outs, sems, True)
        for cp in recvs:
            cp.wait_recv()
        for cp in sends:
            cp.wait_send()
        for cp in owns:
            cp.wait()


def _run_comm(comm, name):
    n = comm.n

    def body(*refs):
        ins, outs, sems = refs[:n], refs[n:2 * n], refs[2 * n:]
        comm.start(ins, outs, sems)
        comm.finish(ins, outs, sems)

    return list(pl.pallas_call(body, in_specs=[ANY] * n, out_specs=[ANY] * n, out_shape=comm.out_shape, scratch_shapes=comm.scratch,
                               input_output_aliases=dict(comm.aliases), name=name)(*comm.arrs))


def _gather_ici(shards):
    return _Comm(shards, [_sds((4, 2) + s.shape, s.dtype) for s in shards], ICI_PEERS,
                 src=lambda a, i, me, p: i, dst=lambda a, o, s: o.at[s.chip, s.c], own=lambda a, i, o, me: (i, o.at[me.chip, me.c]))


def _gather_sibling(bufs):
    return _Comm(bufs, [_sds(b.shape, b.dtype) for b in bufs], SIBLING,
                 src=lambda a, i, me, p: i.at[:, me.c], dst=lambda a, o, s: o.at[:, s.c], aliases={a: a for a in range(len(bufs))})


def _scatter_sibling(parts):
    return _Comm(parts, [_sds((4,) + p.shape[2:], p.dtype) for p in parts], SIBLING,
                 src=lambda a, i, me, p: i.at[:, p.c], dst=lambda a, o, s: o)


def _scatter_ici(parts):
    return _Comm(parts, [_sds(p.shape, p.dtype) for p in parts], ICI_PEERS,
                 src=lambda a, i, me, p: i.at[p.chip], dst=lambda a, o, s: o.at[s.chip], own=lambda a, i, o, me: (i.at[me.chip], o.at[me.chip]))


def _gather_all(arrs):
    return _Comm(arrs, [_sds((N_DEV,) + a.shape, a.dtype) for a in arrs], ALL_PEERS,
                 src=lambda a, i, me, p: i, dst=lambda a, o, s: o.at[s.id], own=lambda a, i, o, me: (i, o.at[me.id]))


def _add_halves(parts, got, name):
    n, _, R, C = parts.shape
    tr = R if R <= 256 else 256
    assert R % tr == 0
    core = lax.axis_index("c").astype(jnp.int32).reshape(1)

    def body(c_ref, p_ref, g_ref, o_ref):
        o_ref[0] = (p_ref[0, 0].astype(F32) + g_ref[0].astype(F32)).astype(o_ref.dtype)

    spec = pltpu.PrefetchScalarGridSpec(
        num_scalar_prefetch=1, grid=(n, R // tr),
        in_specs=[pl.BlockSpec((1, 1, tr, C), lambda q, i, c_ref: (q, c_ref[0], i, 0)), pl.BlockSpec((1, tr, C), lambda q, i, c_ref: (q, i, 0))],
        out_specs=pl.BlockSpec((1, tr, C), lambda q, i, c_ref: (q, i, 0)))
    return pl.pallas_call(body, grid_spec=spec, out_shape=_sds((n, R, C), parts.dtype), name=name,
                          compiler_params=_params("parallel", "parallel"))(core, parts, got)


def _adam(w, m, v, gparts, name):
    R, C = w.shape
    n = gparts.shape[0]
    tr = R if R <= 256 else 128
    assert R % tr == 0
    c1 = 1.0 / (1.0 - ADAM_B1 ** ADAM_STEP)
    c2 = 1.0 / (1.0 - ADAM_B2 ** ADAM_STEP)

    def body(w_ref, m_ref, v_ref, g_ref, go_ref, d_ref, mo_ref, vo_ref):
        g = g_ref[0].astype(F32)
        for s in range(1, n):
            g = g + g_ref[s].astype(F32)
        mn = ADAM_B1 * m_ref[...] + (1.0 - ADAM_B1) * g
        vn = ADAM_B2 * v_ref[...] + (1.0 - ADAM_B2) * (g * g)
        go_ref[...] = g
        mo_ref[...] = mn
        vo_ref[...] = vn
        d_ref[...] = -ADAM_LR * ((mn * c1) / (jnp.sqrt(vn * c2) + ADAM_EPS) + ADAM_WD * w_ref[...])

    blk = pl.BlockSpec((tr, C), lambda i: (i, 0))
    return pl.pallas_call(
        body, grid=(R // tr,), in_specs=[blk, blk, blk, pl.BlockSpec((n, tr, C), lambda i: (0, i, 0))],
        out_specs=[blk] * 4, out_shape=[_sds((R, C), F32)] * 4, name=name, compiler_params=_params("parallel"))(w, m, v, gparts)


_SMALL_ORDER = ("norm_mix", "b_gate", "sc_conv_w", "ssm_conv_w", "ssm_conv_b", "dt_bias", "A_log", "D_skip", "ssm_norm_w",
                "norm_mlp", "norm_final", "loss")
_REPLICATED = ("norm_mix", "b_gate", "ssm_conv_b", "dt_bias", "A_log", "D_skip", "ssm_norm_w", "norm_mlp", "norm_final")


def _cols_to_slots(g, n):
    R = g.shape[0]
    return jnp.transpose(g.reshape(R, n, g.shape[1] // n), (1, 0, 2))


def _slots_to_cols(g):
    n, R, C = g.shape
    return jnp.transpose(g, (1, 0, 2)).reshape(R, n * C)


def kernel(x, norm_mix, w_in, b_gate, sc_conv_w, ssm_conv_w, ssm_conv_b, dt_bias, A_log, D_skip, ssm_norm_w, w_branch_sc, w_branch_ssm, w_out, norm_mlp, w_mlp1, w_mlp2, norm_final, loss_target, m_norm_mix, m_w_in, m_b_gate, m_sc_conv_w, m_ssm_conv_w, m_ssm_conv_b, m_dt_bias, m_A_log, m_D_skip, m_ssm_norm_w, m_w_branch_sc, m_w_branch_ssm, m_w_out, m_norm_mlp, m_w_mlp1, m_w_mlp2, m_norm_final, v_norm_mix, v_w_in, v_b_gate, v_sc_conv_w, v_ssm_conv_w, v_ssm_conv_b, v_dt_bias, v_A_log, v_D_skip, v_ssm_norm_w, v_w_branch_sc, v_w_branch_ssm, v_w_out, v_norm_mlp, v_w_mlp1, v_w_mlp2, v_norm_final):
    T, D = x.shape[1], x.shape[2]
    n_inner = 2 * D
    n_heads = n_inner // HEADDIM
    n_xbc = n_inner + 2 * NGROUPS * NSTATE
    me = 4 * lax.axis_index("x") + 2 * lax.axis_index("y") + lax.axis_index("c")

    o_z, o_xbc, o_dt, o_gate = 3 * D, 3 * D + n_inner, 3 * D + n_inner + n_xbc, 3 * D + n_inner + n_xbc + n_heads
    by_owner = lambda b: b.reshape((N_DEV,) + b.shape[2:])
    to_owner = lambda g: g.reshape((4, 2) + g.shape[1:])
    rows_of = lambda g: to_owner(g.reshape((N_DEV, g.shape[0] // N_DEV) + g.shape[1:]))
    cols_of = lambda g: to_owner(_cols_to_slots(g, N_DEV))

    class Schedule(_NoExchange):
        late = dict(proj_xbc=("bsc", "bssm", "out"), ssd_fwd=("w1", "w2"))
        shards = dict(bsc=w_branch_sc, bssm=w_branch_ssm, out=w_out, w1=w_mlp1, w2=w_mlp2)
        grad_groups = (("w2", "w1"), ("out", "bssm", "bsc"), ("win",))
        grad_carrier = dict(ssd_bwd=("w2", "w1"), ssm_conv_bwd=("out", "bssm", "bsc"), proj_dx=("win",))

        def __init__(self):
            bufs = _run_comm(_gather_ici([w_in.astype(BF16), sc_conv_w, ssm_conv_w]), "gather_in_ici")
            bufs = _run_comm(_gather_sibling(bufs), "gather_in_sibling")
            win_full = _slots_to_cols(by_owner(bufs[0]))
            self.W = dict(sc=win_full[:, :o_z], z=win_full[:, o_z:o_xbc], xbc=win_full[:, o_xbc:o_dt],
                          dt=jnp.pad(win_full[:, o_dt:o_gate], ((0, 0), (0, LANES - n_heads))), gate=win_full[:, o_gate:])
            self.taps = dict(sc_conv_w=_slots_to_cols(by_owner(bufs[1])), ssm_conv_w=_slots_to_cols(by_owner(bufs[2])))
            self.staged, self.grads, self.halves, self.summed = {}, {}, {}, {}

        def carry(self, name):
            if name in self.late:
                return _gather_ici([self.shards[k].astype(BF16) for k in self.late[name]])
            if name in self.grad_carrier:
                return _scatter_ici([self.halves[k] for k in self.grad_carrier[name]])
            return None

        def carried(self, name, outs):
            if name in self.late:
                self.staged[self.late[name]] = outs
            else:
                self.summed.update(zip(self.grad_carrier[name], outs))

        def weight(self, k):
            if k not in self.W:
                group = next(g for g in self.staged if k in g)
                bufs = _run_comm(_gather_sibling(self.staged.pop(group)), "gather_sibling_" + group[0])
                for kk, b in zip(group, bufs):
                    full = by_owner(b)
                    self.W[kk] = _slots_to_cols(full) if kk == "w1" else full.reshape(-1, D)
            return self.W[k]

        def grad(self, k, g):
            if k == "win":
                g = cols_of(jnp.concatenate([g["sc"], g["z"], g["xbc"], g["dt"][:, :n_heads], g["gate"]], axis=1))
            else:
                g = cols_of(g) if k == "w1" else rows_of(g)
            self.grads[k] = g
            group = next(gr for gr in self.grad_groups if k in gr)
            if all(kk in self.grads for kk in group):
                bufs = _run_comm(_scatter_sibling([self.grads[kk] for kk in group]), "scatter_sibling_" + group[0])
                for kk, b in zip(group, bufs):
                    self.halves[kk] = _add_halves(self.grads[kk], b, "add_halves_" + kk)

    S = Schedule()
    small = dict(norm_mix=norm_mix, b_gate=b_gate, ssm_conv_b=ssm_conv_b, dt_bias=dt_bias, A_log=A_log, D_skip=D_skip,
                 ssm_norm_w=ssm_norm_w, norm_mlp=norm_mlp, norm_final=norm_final, **S.taps)
    grad_x, g_small = _local_step(x.reshape(T, D), loss_target.reshape(T, D), S, small)

    small_flat = jnp.concatenate([g_small[k].reshape(-1) for k in _SMALL_ORDER])
    n_small = small_flat.shape[0]
    rows = -(-n_small // (8 * LANES)) * 8
    small_pack = jnp.pad(small_flat, (0, rows * LANES - n_small)).reshape(rows, LANES)
    small_parts = _run_comm(_gather_all([small_pack]), "gather_small")[0]

    res = {}
    big = [("w_in", "win", w_in, m_w_in, v_w_in), ("w_branch_sc", "bsc", w_branch_sc, m_w_branch_sc, v_w_branch_sc),
           ("w_branch_ssm", "bssm", w_branch_ssm, m_w_branch_ssm, v_w_branch_ssm), ("w_out", "out", w_out, m_w_out, v_w_out),
           ("w_mlp1", "w1", w_mlp1, m_w_mlp1, v_w_mlp1), ("w_mlp2", "w2", w_mlp2, m_w_mlp2, v_w_mlp2)]
    for k, gk, w, m, v in big:
        res[k] = _adam(w, m, v, S.summed[gk], "adam_" + k)

    sizes = {k: g_small[k].size for k in _SMALL_ORDER}
    offs, o = {}, 0
    for k in _SMALL_ORDER:
        offs[k] = o
        o += sizes[k]
    rep_w = dict(norm_mix=norm_mix, b_gate=b_gate, ssm_conv_b=ssm_conv_b, dt_bias=dt_bias, A_log=A_log, D_skip=D_skip,
                 ssm_norm_w=ssm_norm_w, norm_mlp=norm_mlp, norm_final=norm_final)
    rep_m = dict(norm_mix=m_norm_mix, b_gate=m_b_gate, ssm_conv_b=m_ssm_conv_b, dt_bias=m_dt_bias, A_log=m_A_log, D_skip=m_D_skip,
                 ssm_norm_w=m_ssm_norm_w, norm_mlp=m_norm_mlp, norm_final=m_norm_final)
    rep_v = dict(norm_mix=v_norm_mix, b_gate=v_b_gate, ssm_conv_b=v_ssm_conv_b, dt_bias=v_dt_bias, A_log=v_A_log, D_skip=v_D_skip,
                 ssm_norm_w=v_ssm_norm_w, norm_mlp=v_norm_mlp, norm_final=v_norm_final)

    def pack(d):
        segs = [jnp.pad(d[k].astype(F32).reshape(-1), (0, sizes[k] - d[k].size)) if k in d else jnp.zeros((sizes[k],), F32)
                for k in _SMALL_ORDER]
        return jnp.pad(jnp.concatenate(segs), (0, rows * LANES - n_small)).reshape(rows, LANES)

    sm = _adam(pack(rep_w), pack(rep_m), pack(rep_v), small_parts, "adam_small")
    sm = [s.reshape(-1) for s in sm]
    for k in _REPLICATED:
        n_k = rep_w[k].shape[0]
        res[k] = tuple(s[offs[k]:offs[k] + n_k] for s in sm)
    loss = sm[0][offs["loss"]]
    for k, w, m, v, K, full in (("sc_conv_w", sc_conv_w, m_sc_conv_w, v_sc_conv_w, SC_K, D),
                                ("ssm_conv_w", ssm_conv_w, m_ssm_conv_w, v_ssm_conv_w, SSM_K, n_xbc)):
        g_full = sm[0][offs[k]:offs[k] + K * full].reshape(K, full)
        cw = full // N_DEV
        g_mine = lax.dynamic_slice_in_dim(g_full, me * cw, cw, axis=1)
        res[k] = _adam(w, m, v, g_mine[None], "adam_" + k)

    order = ("norm_mix", "w_in", "b_gate", "sc_conv_w", "ssm_conv_w", "ssm_conv_b", "dt_bias", "A_log", "D_skip", "ssm_norm_w",
             "w_branch_sc", "w_branch_ssm", "w_out", "norm_mlp", "w_mlp1", "w_mlp2", "norm_final")
    outs = [loss, grad_x.reshape(1, T, D)]
    for j in range(4):
        outs += [res[k][j] for k in order]
    return tuple(outs)
```

```python
import functools

import jax
import jax.numpy as jnp
from jax import lax
from jax.experimental import pallas as pl
from jax.experimental.pallas import tpu as pltpu

F32 = jnp.float32
BF16 = jnp.bfloat16

EPS = 1e-6
N_DEV = 8
HEADDIM = 64
NSTATE = 128
CHUNK = 128
NGROUPS = 8
GROUP_W = 256
SC_K = 3
SSM_K = 4
LANES = 128

ADAM_LR = 0.001
ADAM_B1 = 0.9
ADAM_B2 = 0.999
ADAM_EPS = 1e-08
ADAM_WD = 0.01
ADAM_STEP = 10

NN = (((1,), (0,)), ((), ()))
NT = (((1,), (1,)), ((), ()))
TN = (((0,), (0,)), ((), ()))
_DIMS = {"nn": NN, "nt": NT, "tn": TN}

ANY = pl.BlockSpec(memory_space=pl.ANY)
MESH = pl.DeviceIdType.MESH


def _sds(shape, dtype):
    return jax.ShapeDtypeStruct(tuple(shape), dtype)


def _dot(a, b, dims=NN):
    return lax.dot_general(a, b, dims, preferred_element_type=F32)


def _dot3(a, b, dims=NN):
    return lax.dot_general(a, b, dims, preferred_element_type=F32, precision=lax.Precision.HIGH)


def _params(*sem):
    return pltpu.CompilerParams(dimension_semantics=tuple(sem))


def _call(body, *, grid, in_specs, out_specs, out_shape, args, name, sem, scratch=(), comm=None):
    if comm is None:
        outs = pl.pallas_call(body, grid=grid, in_specs=list(in_specs), out_specs=list(out_specs), out_shape=list(out_shape),
                              scratch_shapes=list(scratch), name=name, compiler_params=_params(*sem))(*args)
        return list(outs), None
    n, n_in, n_out, n_scr = comm.n, len(in_specs), len(out_shape), len(scratch)

    def wrapped(*refs):
        ins, c_in = refs[:n_in], refs[n_in:n_in + n]
        outs, c_out = refs[n_in + n:n_in + n + n_out], refs[n_in + n + n_out:n_in + 2 * n + n_out]
        rest = refs[n_in + 2 * n + n_out:]
        scr, sems = rest[:n_scr], rest[n_scr:]
        first, last = None, None
        for d, g in enumerate(grid):
            f, l = pl.program_id(d) == 0, pl.program_id(d) == g - 1
            first, last = (f, l) if first is None else (first & f, last & l)

        @pl.when(first)
        def _():
            comm.start(c_in, c_out, sems)

        body(*ins, *outs, *scr)

        @pl.when(last)
        def _():
            comm.finish(c_in, c_out, sems)

    outs = pl.pallas_call(
        wrapped, grid=grid, in_specs=list(in_specs) + [ANY] * n, out_specs=list(out_specs) + [ANY] * n,
        out_shape=list(out_shape) + comm.out_shape, scratch_shapes=list(scratch) + comm.scratch,
        input_output_aliases={n_in + i: n_out + o for i, o in comm.aliases.items()},
        name=name, compiler_params=_params(*["arbitrary"] * len(grid)))(*args, *comm.arrs)
    return list(outs[:n_out]), list(outs[n_out:])


MM_VMEM_BUDGET = 44 * 2 ** 20


def _mm_tiles(M, N, k_bytes, mn_bytes):
    best = None
    for tm in (2048, 1024, 512, 256, 128):
        for tn in (1024, 512, 256, 128):
            if M % tm or N % tn:
                continue
            need = 2 * ((tm + tn) * k_bytes + tm * tn * mn_bytes) + 4 * tm * tn * 4
            if need <= MM_VMEM_BUDGET and (best is None or (tm * tn, tm) > (best[0] * best[1], best[0])):
                best = (tm, tn)
    assert best is not None, (M, N, k_bytes, mn_bytes)
    return best


def _mm(a, b, *, mode, name, extras=(), epi=None, out_dtypes=(F32,), comm=None):
    a_list = list(a) if isinstance(a, (list, tuple)) else [a]
    b_list = list(b) if isinstance(b, (list, tuple)) else [b]
    if mode == "nn":
        M, N = a_list[0].shape[0], b_list[0].shape[1]
    elif mode == "nt":
        M, N = a_list[0].shape[0], b_list[0].shape[0]
    else:
        M, N = a_list[0].shape[1], b_list[0].shape[1]
    k_bytes = sum((av.shape[0] if mode == "tn" else av.shape[1]) * av.dtype.itemsize for av in a_list)
    mn_bytes = sum(e.dtype.itemsize for e in extras) + sum(jnp.dtype(d).itemsize for d in out_dtypes)
    tm, tn = _mm_tiles(min(M, 2048), min(N, 1024), k_bytes, mn_bytes) if M % 128 == 0 and N % 128 == 0 else (M, N)
    assert M % tm == 0 and N % tn == 0
    a_specs, b_specs = [], []
    for av, bv in zip(a_list, b_list):
        K = av.shape[0] if mode == "tn" else av.shape[1]
        a_specs.append(pl.BlockSpec((K, tm), lambda i, j: (0, i)) if mode == "tn" else pl.BlockSpec((tm, K), lambda i, j: (i, 0)))
        b_specs.append(pl.BlockSpec((tn, K), lambda i, j: (j, 0)) if mode == "nt" else pl.BlockSpec((K, tn), lambda i, j: (0, j)))
    mn_spec = pl.BlockSpec((tm, tn), lambda i, j: (i, j))
    n_p, n_ex = len(a_list), len(extras)
    dims = _DIMS[mode]

    def body(*refs):
        acc = _dot(refs[0][...], refs[n_p][...], dims)
        for p in range(1, n_p):
            acc = acc + _dot(refs[p][...], refs[n_p + p][...], dims)
        rest = refs[2 * n_p:]
        res = (acc,) if epi is None else epi(acc, *[r[...] for r in rest[:n_ex]])
        for o_ref, r in zip(rest[n_ex:], res):
            o_ref[...] = r.astype(o_ref.dtype)

    outs, carried = _call(
        body, grid=(M // tm, N // tn), in_specs=a_specs + b_specs + [mn_spec] * n_ex,
        out_specs=[mn_spec] * len(out_dtypes), out_shape=[_sds((M, N), d) for d in out_dtypes],
        args=a_list + b_list + list(extras), name=name, sem=("parallel", "parallel"), comm=comm)
    res = outs[0] if len(outs) == 1 else outs
    return res if comm is None else (res, carried)


def _epi_add(acc, r):
    return (acc + r,)


def _epi_add2(acc, r):
    s = acc + r
    return (s, s)


def _epi_relu2(acc):
    p = jnp.maximum(acc, 0.0)
    return (p * p,)


def _epi_relu2_bwd(acc, r):
    return (acc * (2.0 * jnp.sqrt(r.astype(F32))),)


def _row(tr, n):
    return pl.BlockSpec((tr, n), lambda i: (i, 0))


def _vec(n):
    return pl.BlockSpec((1, n), lambda i: (0, 0))


def _rms_fwd(x, w, name):
    T, D = x.shape
    tr = min(256, T)

    def body(x_ref, w_ref, o_ref):
        xv = x_ref[...]
        r = lax.rsqrt(jnp.mean(xv * xv, axis=-1, keepdims=True) + EPS)
        o_ref[...] = (xv * r * w_ref[...]).astype(BF16)

    return pl.pallas_call(body, grid=(T // tr,), in_specs=[_row(tr, D), _vec(D)], out_specs=_row(tr, D),
                          out_shape=_sds((T, D), BF16), name=name, compiler_params=_params("parallel"))(x, w)


def _rms_bwd(x, w, dh, dres, name):
    T, D = x.shape
    tr = min(256, T)

    def body(x_ref, w_ref, dh_ref, dres_ref, dx_ref, dxb_ref, dw_ref):
        @pl.when(pl.program_id(0) == 0)
        def _():
            dw_ref[...] = jnp.zeros_like(dw_ref)

        xv = x_ref[...]
        r = lax.rsqrt(jnp.mean(xv * xv, axis=-1, keepdims=True) + EPS)
        xh = xv * r
        dh_v = dh_ref[...]
        dw_ref[...] += jnp.sum(dh_v * xh, axis=0, keepdims=True)
        dxh = dh_v * w_ref[...]
        dx = r * (dxh - xh * jnp.mean(dxh * xh, axis=-1, keepdims=True)) + dres_ref[...]
        dx_ref[...] = dx
        dxb_ref[...] = dx.astype(BF16)

    return pl.pallas_call(
        body, grid=(T // tr,), in_specs=[_row(tr, D), _vec(D), _row(tr, D), _row(tr, D)],
        out_specs=[_row(tr, D), _row(tr, D), _vec(D)],
        out_shape=[_sds((T, D), F32), _sds((T, D), BF16), _sds((1, D), F32)],
        name=name, compiler_params=_params("arbitrary"))(x, w, dh, dres)


def _final(x2, w, tgt, name):
    T, D = x2.shape
    tr = min(256, T)

    def body(x_ref, w_ref, t_ref, dx_ref, dxb_ref, dw_ref, loss_ref):
        @pl.when(pl.program_id(0) == 0)
        def _():
            dw_ref[...] = jnp.zeros_like(dw_ref)
            loss_ref[...] = jnp.zeros_like(loss_ref)

        xv = x_ref[...]
        wv = w_ref[...]
        r = lax.rsqrt(jnp.mean(xv * xv, axis=-1, keepdims=True) + EPS)
        xh = xv * r
        err = xh * wv - t_ref[...]
        part = jnp.sum(jnp.sum(err * err, axis=1, keepdims=True), axis=0, keepdims=True) * (0.5 / D)
        loss_ref[...] += jnp.broadcast_to(part, loss_ref.shape)
        dy = err * (1.0 / D)
        dw_ref[...] += jnp.sum(dy * xh, axis=0, keepdims=True)
        dxh = dy * wv
        dx = r * (dxh - xh * jnp.mean(dxh * xh, axis=-1, keepdims=True))
        dx_ref[...] = dx
        dxb_ref[...] = dx.astype(BF16)

    return pl.pallas_call(
        body, grid=(T // tr,), in_specs=[_row(tr, D), _vec(D), _row(tr, D)],
        out_specs=[_row(tr, D), _row(tr, D), _vec(D), _vec(LANES)],
        out_shape=[_sds((T, D), F32), _sds((T, D), BF16), _sds((1, D), F32), _sds((1, LANES), F32)],
        name=name, compiler_params=_params("arbitrary"))(x2, w, tgt)


def _silu_parts(z):
    s = jax.nn.sigmoid(z)
    return z * s, s * (1.0 + z * (1.0 - s))


def _gnorm_fwd(y, z, w, name):
    T, N = y.shape
    tr = min(256, T)

    def body(y_ref, z_ref, w_ref, o_ref):
        for g in range(N // GROUP_W):
            sl = slice(g * GROUP_W, (g + 1) * GROUP_W)
            silu, _ = _silu_parts(z_ref[:, sl])
            yz = y_ref[:, sl] * silu
            r = lax.rsqrt(jnp.mean(yz * yz, axis=-1, keepdims=True) + EPS)
            o_ref[:, sl] = (yz * r * w_ref[:, sl]).astype(BF16)

    return pl.pallas_call(body, grid=(T // tr,), in_specs=[_row(tr, N), _row(tr, N), _vec(N)], out_specs=_row(tr, N),
                          out_shape=_sds((T, N), BF16), name=name, compiler_params=_params("parallel"))(y, z, w)


def _gnorm_bwd(y, z, w, dyb, name):
    T, N = y.shape
    tr = min(256, T)

    def body(y_ref, z_ref, w_ref, d_ref, dy_ref, dz_ref, dw_ref):
        @pl.when(pl.program_id(0) == 0)
        def _():
            dw_ref[...] = jnp.zeros_like(dw_ref)

        for g in range(N // GROUP_W):
            sl = slice(g * GROUP_W, (g + 1) * GROUP_W)
            yv = y_ref[:, sl]
            silu, dsilu = _silu_parts(z_ref[:, sl])
            yz = yv * silu
            r = lax.rsqrt(jnp.mean(yz * yz, axis=-1, keepdims=True) + EPS)
            yzh = yz * r
            d = d_ref[:, sl]
            dw_ref[:, sl] += jnp.sum(d * yzh, axis=0, keepdims=True)
            dyzh = d * w_ref[:, sl]
            dyz = r * (dyzh - yzh * jnp.mean(dyzh * yzh, axis=-1, keepdims=True))
            dy_ref[:, sl] = dyz * silu
            dz_ref[:, sl] = (dyz * yv * dsilu).astype(BF16)

    return pl.pallas_call(
        body, grid=(T // tr,), in_specs=[_row(tr, N), _row(tr, N), _vec(N), _row(tr, N)],
        out_specs=[_row(tr, N), _row(tr, N), _vec(N)],
        out_shape=[_sds((T, N), F32), _sds((T, N), BF16), _sds((1, N), F32)],
        name=name, compiler_params=_params("arbitrary"))(y, z, w, dyb)


def _merge_fwd(gate_raw, b_gate, br_a, br_b, name):
    T, D = br_a.shape
    tr = min(256, T)

    def body(g_ref, bg_ref, a_ref, b_ref, o_ref):
        g = jax.nn.sigmoid(g_ref[...] + bg_ref[...])
        o_ref[...] = (g[:, :D] * a_ref[...] + g[:, D:] * b_ref[...]).astype(BF16)

    return pl.pallas_call(body, grid=(T // tr,), in_specs=[_row(tr, 2 * D), _vec(2 * D), _row(tr, D), _row(tr, D)],
                          out_specs=_row(tr, D), out_shape=_sds((T, D), BF16), name=name,
                          compiler_params=_params("parallel"))(gate_raw, b_gate, br_a, br_b)


def _merge_bwd(dmerged, gate_raw, b_gate, br_a, br_b, name):
    T, D = br_a.shape
    tr = min(256, T)

    def body(d_ref, g_ref, bg_ref, a_ref, b_ref, da_ref, db_ref, dg_ref, dbg_ref):
        @pl.when(pl.program_id(0) == 0)
        def _():
            dbg_ref[...] = jnp.zeros_like(dbg_ref)

        g = jax.nn.sigmoid(g_ref[...] + bg_ref[...])
        d = d_ref[...]
        da_ref[...] = (d * g[:, :D]).astype(BF16)
        db_ref[...] = (d * g[:, D:]).astype(BF16)
        dg = jnp.concatenate([d * a_ref[...], d * b_ref[...]], axis=1) * g * (1.0 - g)
        dg_ref[...] = dg.astype(BF16)
        dbg_ref[...] += jnp.sum(dg, axis=0, keepdims=True)

    return pl.pallas_call(
        body, grid=(T // tr,), in_specs=[_row(tr, D), _row(tr, 2 * D), _vec(2 * D), _row(tr, D), _row(tr, D)],
        out_specs=[_row(tr, D), _row(tr, D), _row(tr, 2 * D), _vec(2 * D)],
        out_shape=[_sds((T, D), BF16), _sds((T, D), BF16), _sds((T, 2 * D), BF16), _sds((1, 2 * D), F32)],
        name=name, compiler_params=_params("arbitrary"))(dmerged, gate_raw, b_gate, br_a, br_b)


def _shift_down(u, s):
    if s == 0:
        return u
    row = lax.broadcasted_iota(jnp.int32, u.shape, 0)
    return jnp.where(row >= s, pltpu.roll(u, s, 0), 0.0)


def _shift_up(u, s):
    if s == 0:
        return u
    n = u.shape[0]
    row = lax.broadcasted_iota(jnp.int32, u.shape, 0)
    return jnp.where(row < n - s, pltpu.roll(u, n - s, 0), 0.0)


def _conv(u, w_ref, K):
    acc = u * w_ref[K - 1:K, :]
    for k in range(K - 1):
        acc = acc + _shift_down(u, K - 1 - k) * w_ref[k:k + 1, :]
    return acc


def _conv_bwd(u, dc, w_ref, dw_ref, K):
    du = dc * w_ref[K - 1:K, :]
    dw_ref[K - 1:K, :] = jnp.sum(dc * u, axis=0, keepdims=True)
    for k in range(K - 1):
        s = K - 1 - k
        dw_ref[k:k + 1, :] = jnp.sum(dc * _shift_down(u, s), axis=0, keepdims=True)
        du = du + _shift_up(dc, s) * w_ref[k:k + 1, :]
    return du


CB_W = 256


def _col(T, j0=0):
    return pl.BlockSpec((T, CB_W), lambda j: (0, j + j0))


def _sc_fwd(psc, w, name):
    T, D = psc.shape[0], psc.shape[1] // 3
    nb = D // CB_W

    def body(b_ref, c_ref, x_ref, w_ref, o_ref):
        o_ref[...] = (b_ref[...] * _conv(c_ref[...] * x_ref[...], w_ref, SC_K)).astype(BF16)

    return pl.pallas_call(
        body, grid=(nb,), in_specs=[_col(T), _col(T, nb), _col(T, 2 * nb), pl.BlockSpec((SC_K, CB_W), lambda j: (0, j))],
        out_specs=_col(T), out_shape=_sds((T, D), BF16), name=name, compiler_params=_params("parallel"))(psc, psc, psc, w)


def _sc_bwd(psc, w, dya, name):
    T, D = psc.shape[0], psc.shape[1] // 3
    nb = D // CB_W

    def body(b_ref, c_ref, x_ref, w_ref, d_ref, db_ref, dc_ref, dx_ref, dw_ref):
        cv, xv, d = c_ref[...], x_ref[...], d_ref[...]
        u = cv * xv
        db_ref[...] = (d * _conv(u, w_ref, SC_K)).astype(BF16)
        du = _conv_bwd(u, d * b_ref[...], w_ref, dw_ref, SC_K)
        dc_ref[...] = (du * xv).astype(BF16)
        dx_ref[...] = (du * cv).astype(BF16)

    wspec = pl.BlockSpec((SC_K, CB_W), lambda j: (0, j))
    return pl.pallas_call(
        body, grid=(nb,), in_specs=[_col(T), _col(T, nb), _col(T, 2 * nb), wspec, _col(T)],
        out_specs=[_col(T), _col(T), _col(T), wspec],
        out_shape=[_sds((T, D), BF16)] * 3 + [_sds((SC_K, D), F32)],
        name=name, compiler_params=_params("parallel"))(psc, psc, psc, w, dya)


def _ssm_conv_fwd(u, w, b, name, comm=None):
    T, N = u.shape

    def body(u_ref, w_ref, b_ref, o_ref):
        c = _conv(u_ref[...], w_ref, SSM_K) + b_ref[...]
        o_ref[...] = c * jax.nn.sigmoid(c)

    outs, carried = _call(
        body, grid=(N // CB_W,), in_specs=[_col(T), pl.BlockSpec((SSM_K, CB_W), lambda j: (0, j)), pl.BlockSpec((1, CB_W), lambda j: (0, j))],
        out_specs=[_col(T)], out_shape=[_sds((T, N), F32)], args=[u, w, b], name=name, sem=("parallel",), comm=comm)
    return outs[0] if comm is None else (outs[0], carried)


def _ssm_conv_bwd(u, w, b, dxs, dB, dC, name, comm=None):
    T, N = u.shape
    n_x, n_b = dxs.shape[1] // CB_W, dB.shape[1] // CB_W

    def body(u_ref, w_ref, b_ref, dx_ref, db_ref, dc_ref, du_ref, dw_ref, dbias_ref):
        j = pl.program_id(0)
        uv = u_ref[...]
        c = _conv(uv, w_ref, SSM_K) + b_ref[...]
        _, dsilu = _silu_parts(c)
        d = jnp.where(j < n_x, dx_ref[...], jnp.where(j < n_x + n_b, db_ref[...], dc_ref[...])) * dsilu
        dbias_ref[...] = jnp.sum(d, axis=0, keepdims=True)
        du_ref[...] = _conv_bwd(uv, d, w_ref, dw_ref, SSM_K).astype(BF16)

    wspec = pl.BlockSpec((SSM_K, CB_W), lambda j: (0, j))
    bspec = pl.BlockSpec((1, CB_W), lambda j: (0, j))
    outs, carried = _call(
        body, grid=(N // CB_W,),
        in_specs=[_col(T), wspec, bspec,
                  pl.BlockSpec((T, CB_W), lambda j: (0, jnp.minimum(j, n_x - 1))),
                  pl.BlockSpec((T, CB_W), lambda j: (0, jnp.clip(j - n_x, 0, n_b - 1))),
                  pl.BlockSpec((T, CB_W), lambda j: (0, jnp.clip(j - n_x - n_b, 0, n_b - 1)))],
        out_specs=[_col(T), wspec, bspec],
        out_shape=[_sds((T, N), BF16), _sds((SSM_K, N), F32), _sds((1, N), F32)],
        args=[u, w, b, dxs, dB, dC], name=name, sem=("parallel",), comm=comm)
    return outs if comm is None else (outs, carried)


def _split3(v):
    hi = v.astype(BF16)
    r = v - hi.astype(F32)
    mid = r.astype(BF16)
    lo = (r - mid.astype(F32)).astype(BF16)
    return hi, mid, lo


def _head_expand(n_lanes):
    h = lax.broadcasted_iota(jnp.int32, (LANES, n_lanes), 0)
    l = lax.broadcasted_iota(jnp.int32, (LANES, n_lanes), 1)
    return (jnp.right_shift(l, HEADDIM.bit_length() - 1) == h).astype(BF16)


def _softplus(v):
    return jnp.maximum(v, 0.0) + jnp.log1p(jnp.exp(-jnp.abs(v)))


def _ssd_prep(dt_raw, dt_bias, a_log, n_inner, name):
    T = dt_raw.shape[0]

    def body(r_ref, b_ref, al_ref, dt_ref, cs_ref):
        dt = _softplus(r_ref[...] + b_ref[...])
        a = dt * (-jnp.exp(al_ref[...]))
        i = lax.broadcasted_iota(jnp.int32, (CHUNK, CHUNK), 0)
        j = lax.broadcasted_iota(jnp.int32, (CHUNK, CHUNK), 1)
        tri = (j <= i).astype(BF16)
        cs = sum(_dot(tri, p) for p in _split3(a))
        ex = _head_expand(n_inner)
        dt_ref[...] = sum(_dot(p, ex) for p in _split3(dt))
        cs_ref[...] = sum(_dot(p, ex) for p in _split3(cs))

    blk = pl.BlockSpec((CHUNK, LANES), lambda c: (c, 0))
    out = pl.BlockSpec((CHUNK, n_inner), lambda c: (c, 0))
    return pl.pallas_call(body, grid=(T // CHUNK,), in_specs=[blk, _vec(LANES), _vec(LANES)], out_specs=[out, out],
                          out_shape=[_sds((T, n_inner), F32)] * 2, name=name, compiler_params=_params("parallel"))(dt_raw, dt_bias, a_log)


def _pair_terms(cs_p):
    lane = lax.broadcasted_iota(jnp.int32, (CHUNK, CHUNK), 1)
    sub = lax.broadcasted_iota(jnp.int32, (CHUNK, CHUNK), 0)
    csT = cs_p.T
    Ls = []
    for k in range(2):
        col = jnp.sum(jnp.where(lane == k * HEADDIM, cs_p, 0.0), axis=1, keepdims=True)
        rowv = csT[k * HEADDIM:k * HEADDIM + 1, :]
        Ls.append(jnp.exp(jnp.where(sub >= lane, col - rowv, -jnp.inf)))
    return Ls, jnp.exp(csT[:, CHUNK - 1:CHUNK])


def _block_diag(xp):
    lane = lax.broadcasted_iota(jnp.int32, xp.shape, 1)
    return jnp.concatenate([jnp.where(lane < HEADDIM, xp, 0.0), jnp.where(lane >= HEADDIM, xp, 0.0)], axis=0)


SSD_GROUPS_PER_STEP = 8


def _ssd_specs(T, n_inner):
    nc, gs = T // CHUNK, SSD_GROUPS_PER_STEP
    bo, co = n_inner // (gs * NSTATE), (n_inner + NGROUPS * NSTATE) // (gs * NSTATE)
    assert NGROUPS % gs == 0 and n_inner % (gs * NSTATE) == 0 and (NGROUPS * NSTATE) % (gs * NSTATE) == 0
    g_blk = lambda f: pl.BlockSpec((CHUNK, gs * GROUP_W), lambda c, s: (f(c), s))
    b_blk = lambda f: pl.BlockSpec((CHUNK, gs * NSTATE), lambda c, s: (f(c), bo + s))
    c_blk = lambda f: pl.BlockSpec((CHUNK, gs * NSTATE), lambda c, s: (f(c), co + s))
    return nc, g_blk, b_blk, c_blk


def _ssd_fwd(xbc, dt_e, cs_e, d_e, name, comm=None):
    T = xbc.shape[0]
    n_inner = dt_e.shape[1]
    nc, g_blk, b_blk, c_blk = _ssd_specs(T, n_inner)
    ident = lambda c: c

    gs = SSD_GROUPS_PER_STEP

    def body(xs_ref, b_ref, c_ref, dt_ref, cs_ref, d_ref, y_ref, p_ref, st):
        c, s = pl.program_id(0), pl.program_id(1)

        @pl.when(c == 0)
        def _():
            for gi in range(gs):
                st[s * gs + gi] = jnp.zeros((GROUP_W, NSTATE), F32)

        for gi in range(gs):
            g = s * gs + gi
            gw, gn = slice(gi * GROUP_W, (gi + 1) * GROUP_W), slice(gi * NSTATE, (gi + 1) * NSTATE)
            P = st[g]
            p_ref[0, gi] = P
            xs, dt, cs = xs_ref[:, gw], dt_ref[:, gw], cs_ref[:, gw]
            Bf, Cf = b_ref[:, gn], c_ref[:, gn]
            CBm = _dot3(Cf, Bf, NT)
            X = xs * dt
            decay = jnp.exp(cs[CHUNK - 1:CHUNK, :] - cs)
            y_off = _dot3(Cf, P, NT) * jnp.exp(cs)
            ys, ecl = [], []
            for pr in range(2):
                sl = slice(pr * LANES, (pr + 1) * LANES)
                Ls, e_last = _pair_terms(cs[:, sl])
                ecl.append(e_last)
                Mcat = jnp.concatenate([CBm * L for L in Ls], axis=1)
                ys.append(_dot3(Mcat, _block_diag(X[:, sl])))
            y_ref[:, gw] = jnp.concatenate(ys, axis=1) + y_off + xs * d_ref[:, gw]
            S = _dot3(X * decay, Bf, TN)
            st[g] = P * jnp.concatenate(ecl, axis=0) + S

    p_blk = pl.BlockSpec((1, gs, GROUP_W, NSTATE), lambda c, s: (c, s, 0, 0))
    outs, carried = _call(
        body, grid=(nc, NGROUPS // gs),
        in_specs=[g_blk(ident), b_blk(ident), c_blk(ident), g_blk(ident), g_blk(ident), pl.BlockSpec((1, gs * GROUP_W), lambda c, s: (0, s))],
        out_specs=[g_blk(ident), p_blk],
        out_shape=[_sds((T, n_inner), F32), _sds((nc, NGROUPS, GROUP_W, NSTATE), F32)],
        scratch=[pltpu.VMEM((NGROUPS, GROUP_W, NSTATE), F32)],
        args=[xbc, xbc, xbc, dt_e, cs_e, d_e], name=name, sem=("arbitrary", "arbitrary"), comm=comm)
    return outs if comm is None else (outs, carried)


def _ssd_bwd(xbc, dt_e, cs_e, d_e, states, dy, name, comm=None):
    T = xbc.shape[0]
    n_inner = dt_e.shape[1]
    nc, g_blk, b_blk, c_blk = _ssd_specs(T, n_inner)
    rev = lambda c: nc - 1 - c

    gs = SSD_GROUPS_PER_STEP

    def body(xs_ref, b_ref, c_ref, dt_ref, cs_ref, d_ref, p_ref, pn_ref, dy_ref,
             dxs_ref, db_ref, dc_ref, ddt_ref, dcs_ref, dd_ref, dst):
        cc, s = pl.program_id(0), pl.program_id(1)

        @pl.when(cc == 0)
        def _():
            for gi in range(gs):
                dst[s * gs + gi] = jnp.zeros((GROUP_W, NSTATE), F32)

        for gi in range(gs):
            one_group(s * gs + gi, gi, xs_ref, b_ref, c_ref, dt_ref, cs_ref, d_ref, p_ref, pn_ref, dy_ref,
                      dxs_ref, db_ref, dc_ref, ddt_ref, dcs_ref, dd_ref, dst)

    def one_group(g, gi, xs_ref, b_ref, c_ref, dt_ref, cs_ref, d_ref, p_ref, pn_ref, dy_ref,
                  dxs_ref, db_ref, dc_ref, ddt_ref, dcs_ref, dd_ref, dst):
        gw, gn = slice(gi * GROUP_W, (gi + 1) * GROUP_W), slice(gi * NSTATE, (gi + 1) * NSTATE)
        dS = dst[g]
        P, Pn = p_ref[0, gi], pn_ref[0, gi]
        xs, dt, cs, dY = xs_ref[:, gw], dt_ref[:, gw], cs_ref[:, gw], dy_ref[:, gw]
        Bf, Cf = b_ref[:, gn], c_ref[:, gn]
        Bb, Cb = Bf.astype(BF16), Cf.astype(BF16)
        X = xs * dt
        ecs = jnp.exp(cs)
        decay = jnp.exp(cs[CHUNK - 1:CHUNK, :] - cs)
        CBm = _dot3(Cf, Bf, NT)
        dYe = dY * ecs
        dP_off = _dot3(dYe, Cf, TN)
        dC = _dot(dYe.astype(BF16), P.astype(BF16))
        dcs = dYe * _dot3(Cf, P, NT)
        Xd = X * decay
        dB = _dot(Xd.astype(BF16), dS.astype(BF16))
        E = _dot3(Bf, dS, NT)
        dX = E * decay
        dcs = dcs - E * Xd
        R = _dot3(jnp.ones((8, NSTATE), F32), dS * Pn, NT)
        sub_g = lax.broadcasted_iota(jnp.int32, (CHUNK, GROUP_W), 0)
        dcs = dcs + jnp.where(sub_g == CHUNK - 1, R[0:1, :], 0.0)
        lane = lax.broadcasted_iota(jnp.int32, (CHUNK, CHUNK), 1)
        sub = lax.broadcasted_iota(jnp.int32, (CHUNK, CHUNK), 0)
        dCB = jnp.zeros((CHUNK, CHUNK), F32)
        dXs, dcss, ecl = [], [], []
        for pr in range(2):
            sl = slice(pr * LANES, (pr + 1) * LANES)
            Ls, e_last = _pair_terms(cs[:, sl])
            ecl.append(e_last)
            dYp = dY[:, sl]
            dMcat = _dot3(dYp, _block_diag(X[:, sl]), NT)
            Mcat = jnp.concatenate([CBm * L for L in Ls], axis=1)
            dXt = _dot3(Mcat, dYp, TN)
            dXs.append(jnp.where(lane < HEADDIM, dXt[:CHUNK], dXt[CHUNK:]))
            colacc = jnp.zeros((CHUNK, CHUNK), F32)
            rowacc = jnp.zeros((CHUNK, CHUNK), F32)
            for k in range(2):
                dG = dMcat[:, k * CHUNK:(k + 1) * CHUNK] * Ls[k]
                dCB = dCB + dG
                Q = dG * CBm
                colacc = colacc + jnp.where(lane == k * HEADDIM, jnp.sum(Q, axis=1, keepdims=True), 0.0)
                rowacc = rowacc + jnp.where(sub == k * HEADDIM, jnp.sum(Q, axis=0, keepdims=True), 0.0)
            dcss.append(colacc - rowacc.T)
        dX = dX + jnp.concatenate(dXs, axis=1)
        dcs = dcs + jnp.concatenate(dcss, axis=1)
        dCBb = dCB.astype(BF16)
        dc_ref[:, gn] = dC + _dot(dCBb, Bb)
        db_ref[:, gn] = dB + _dot(dCBb, Cb, TN)
        dxs_ref[:, gw] = dX * dt + dY * d_ref[:, gw]
        ddt_ref[:, gw] = dX * xs
        dcs_ref[:, gw] = dcs
        dd_ref[0, :, gw] = jnp.sum(dY * xs, axis=0, keepdims=True)
        dst[g] = dS * jnp.concatenate(ecl, axis=0) + dP_off

    p_blk = pl.BlockSpec((1, gs, GROUP_W, NSTATE), lambda c, s: (nc - 1 - c, s, 0, 0))
    pn_blk = pl.BlockSpec((1, gs, GROUP_W, NSTATE), lambda c, s: (jnp.minimum(nc - c, nc - 1), s, 0, 0))
    st_blk = pl.BlockSpec((CHUNK, gs * NSTATE), lambda c, s: (nc - 1 - c, s))
    outs, carried = _call(
        body, grid=(nc, NGROUPS // gs),
        in_specs=[g_blk(rev), b_blk(rev), c_blk(rev), g_blk(rev), g_blk(rev), pl.BlockSpec((1, gs * GROUP_W), lambda c, s: (0, s)),
                  p_blk, pn_blk, g_blk(rev)],
        out_specs=[g_blk(rev), st_blk, st_blk, g_blk(rev), g_blk(rev), pl.BlockSpec((1, 1, gs * GROUP_W), lambda c, s: (nc - 1 - c, 0, s))],
        out_shape=[_sds((T, n_inner), F32), _sds((T, NGROUPS * NSTATE), F32), _sds((T, NGROUPS * NSTATE), F32),
                   _sds((T, n_inner), F32), _sds((T, n_inner), F32), _sds((nc, 1, n_inner), F32)],
        scratch=[pltpu.VMEM((NGROUPS, GROUP_W, NSTATE), F32)],
        args=[xbc, xbc, xbc, dt_e, cs_e, d_e, states, states, dy], name=name, sem=("arbitrary", "arbitrary"), comm=comm)
    return outs if comm is None else (outs, carried)


def _ssd_post(ddt_e, dcs_e, dd_p, dt_raw, dt_bias, a_log, n_heads, name):
    T, n_inner = ddt_e.shape

    def body(ddt_ref, dcs_ref, dd_ref, r_ref, b_ref, al_ref, draw_ref, dbias_ref, dal_ref, ddsk_ref):
        @pl.when(pl.program_id(0) == 0)
        def _():
            dbias_ref[...] = jnp.zeros_like(dbias_ref)
            dal_ref[...] = jnp.zeros_like(dal_ref)
            ddsk_ref[...] = jnp.zeros_like(ddsk_ref)

        ex = _head_expand(n_inner)
        red = lambda v: sum(_dot(p, ex, NT) for p in _split3(v))
        raw = r_ref[...] + b_ref[...]
        dt = _softplus(raw)
        A = -jnp.exp(al_ref[...])
        i = lax.broadcasted_iota(jnp.int32, (CHUNK, CHUNK), 0)
        j = lax.broadcasted_iota(jnp.int32, (CHUNK, CHUNK), 1)
        upper = (j >= i).astype(BF16)
        da = sum(_dot(upper, p) for p in _split3(red(dcs_ref[...])))
        ddt = red(ddt_ref[...]) + da * A
        lane = lax.broadcasted_iota(jnp.int32, (CHUNK, LANES), 1)
        draw = jnp.where(lane < n_heads, ddt * jax.nn.sigmoid(raw), 0.0)
        draw_ref[...] = draw.astype(BF16)
        dbias_ref[...] += jnp.sum(draw, axis=0, keepdims=True)
        dal_ref[...] += jnp.sum(da * dt, axis=0, keepdims=True) * A
        ddsk_ref[...] += red(jnp.broadcast_to(dd_ref[0], (8, n_inner)))[0:1, :]

    wide = pl.BlockSpec((CHUNK, n_inner), lambda c: (c, 0))
    blk = pl.BlockSpec((CHUNK, LANES), lambda c: (c, 0))
    return pl.pallas_call(
        body, grid=(T // CHUNK,),
        in_specs=[wide, wide, pl.BlockSpec((1, 1, n_inner), lambda c: (c, 0, 0)), blk, _vec(LANES), _vec(LANES)],
        out_specs=[blk, _vec(LANES), _vec(LANES), _vec(LANES)],
        out_shape=[_sds((T, LANES), BF16)] + [_sds((1, LANES), F32)] * 3,
        name=name, compiler_params=_params("arbitrary"))(ddt_e, dcs_e, dd_p, dt_raw, dt_bias, a_log)


def _row2(v):
    return v.reshape(1, -1).astype(F32)


def _pad_lanes(v):
    return jnp.pad(_row2(v), ((0, 0), (0, LANES - v.shape[-1])))


class _NoExchange:
    def __init__(self, W):
        self.W, self.grads = W, {}

    def weight(self, k):
        return self.W[k]

    def carry(self, name):
        return None

    def carried(self, name, outs):
        pass

    def grad(self, k, g):
        self.grads[k] = g


def _local_step(x, tgt, S, small):
    T, D = x.shape

    def mm(a, b, *, name, **kw):
        comm = S.carry(name)
        if comm is None:
            return _mm(a, b, name=name, **kw)
        res, outs = _mm(a, b, name=name, comm=comm, **kw)
        S.carried(name, outs)
        return res

    def carrying(fn, *args, name):
        comm = S.carry(name)
        if comm is None:
            return fn(*args, name)
        res, outs = fn(*args, name, comm=comm)
        S.carried(name, outs)
        return res

    n_inner = 2 * D
    n_heads = n_inner // HEADDIM
    norm_mix, norm_mlp, norm_final = _row2(small["norm_mix"]), _row2(small["norm_mlp"]), _row2(small["norm_final"])
    b_gate, ssm_b, ssm_norm_w = _row2(small["b_gate"]), _row2(small["ssm_conv_b"]), _row2(small["ssm_norm_w"])
    dt_bias, a_log = _pad_lanes(small["dt_bias"]), _pad_lanes(small["A_log"])
    d_e = jnp.repeat(small["D_skip"].astype(F32), HEADDIM).reshape(1, n_inner)
    sc_w, ssm_w = small["sc_conv_w"], small["ssm_conv_w"]

    hb = _rms_fwd(x, norm_mix, "rms_mix")
    p_xbc = mm(hb, S.weight("xbc"), mode="nn", name="proj_xbc")
    p_dt = mm(hb, S.weight("dt"), mode="nn", name="proj_dt")
    p_z = mm(hb, S.weight("z"), mode="nn", name="proj_z")
    p_sc = mm(hb, S.weight("sc"), mode="nn", name="proj_sc")
    p_gate = mm(hb, S.weight("gate"), mode="nn", name="proj_gate")
    xbc = carrying(_ssm_conv_fwd, p_xbc, ssm_w, ssm_b, name="ssm_conv_fwd")
    dt_e, cs_e = _ssd_prep(p_dt, dt_bias, a_log, n_inner, "ssd_prep")
    y, states = carrying(_ssd_fwd, xbc, dt_e, cs_e, d_e, name="ssd_fwd")
    yb = _gnorm_fwd(y, p_z, ssm_norm_w, "gnorm_fwd")
    ya = _sc_fwd(p_sc, sc_w, "sc_fwd")
    br_a = mm(ya, S.weight("bsc"), mode="nn", name="branch_sc")
    br_b = mm(yb, S.weight("bssm"), mode="nn", name="branch_ssm")
    merged = _merge_fwd(p_gate, b_gate, br_a, br_b, "merge_fwd")
    x1 = mm(merged, S.weight("out"), mode="nn", name="out_proj", extras=(x,), epi=_epi_add)
    h2 = _rms_fwd(x1, norm_mlp, "rms_mlp")
    r_act = mm(h2, S.weight("w1"), mode="nn", name="mlp_up", epi=_epi_relu2, out_dtypes=(BF16,))
    x2 = mm(r_act, S.weight("w2"), mode="nn", name="mlp_down", extras=(x1,), epi=_epi_add)
    dx2, dx2b, g_norm_final, loss_row = _final(x2, norm_final, tgt, "final")

    S.grad("w2", mm(r_act, dx2b, mode="tn", name="mlp_down_dw", out_dtypes=(BF16,)))
    da = mm(dx2b, S.weight("w2"), mode="nt", name="mlp_down_dx", extras=(r_act,), epi=_epi_relu2_bwd, out_dtypes=(BF16,))
    S.grad("w1", mm(h2, da, mode="tn", name="mlp_up_dw", out_dtypes=(BF16,)))
    dh2 = mm(da, S.weight("w1"), mode="nt", name="mlp_up_dx")
    dx1, dx1b, g_norm_mlp = _rms_bwd(x1, norm_mlp, dh2, dx2, "rms_mlp_bwd")
    S.grad("out", mm(merged, dx1b, mode="tn", name="out_proj_dw", out_dtypes=(BF16,)))
    dmerged = mm(dx1b, S.weight("out"), mode="nt", name="out_proj_dx")
    dbr_a, dbr_b, d_gate, g_b_gate = _merge_bwd(dmerged, p_gate, b_gate, br_a, br_b, "merge_bwd")
    S.grad("bssm", mm(yb, dbr_b, mode="tn", name="branch_ssm_dw", out_dtypes=(BF16,)))
    S.grad("bsc", mm(ya, dbr_a, mode="tn", name="branch_sc_dw", out_dtypes=(BF16,)))
    dyb = mm(dbr_b, S.weight("bssm"), mode="nt", name="branch_ssm_dx")
    dya = mm(dbr_a, S.weight("bsc"), mode="nt", name="branch_sc_dx")
    dy, d_z, g_ssm_norm_w = _gnorm_bwd(y, p_z, ssm_norm_w, dyb, "gnorm_bwd")
    dxs, dB, dC, ddt_e, dcs_e, dd_p = carrying(_ssd_bwd, xbc, dt_e, cs_e, d_e, states, dy, name="ssd_bwd")
    d_dt, g_dt_bias, g_a_log, g_d_skip = _ssd_post(ddt_e, dcs_e, dd_p, p_dt, dt_bias, a_log, n_heads, "ssd_post")
    d_xbc, g_ssm_w, g_ssm_b = carrying(_ssm_conv_bwd, p_xbc, ssm_w, ssm_b, dxs, dB, dC, name="ssm_conv_bwd")
    d_scB, d_scC, d_scX, g_sc_w = _sc_bwd(p_sc, sc_w, dya, "sc_bwd")
    d_sc = jnp.concatenate([d_scB, d_scC, d_scX], axis=1)
    pieces = [("sc", d_sc), ("z", d_z), ("xbc", d_xbc), ("dt", d_dt), ("gate", d_gate)]
    S.grad("win", {k: mm(hb, d, mode="tn", name="proj_dw_" + k, out_dtypes=(BF16,)) for k, d in pieces})
    dh = mm([d for _, d in pieces], [S.weight(k) for k, _ in pieces], mode="nt", name="proj_dx")
    grad_x, _, g_norm_mix = _rms_bwd(x, norm_mix, dh, dx1, "rms_mix_bwd")

    g_small = dict(norm_mix=g_norm_mix, b_gate=g_b_gate, sc_conv_w=g_sc_w, ssm_conv_w=g_ssm_w, ssm_conv_b=g_ssm_b,
                   dt_bias=g_dt_bias, A_log=g_a_log, D_skip=g_d_skip, ssm_norm_w=g_ssm_norm_w, norm_mlp=g_norm_mlp,
                   norm_final=g_norm_final, loss=loss_row)
    return grad_x, g_small


class _Place:
    def __init__(self, k=0):
        x, y, c = lax.axis_index("x"), lax.axis_index("y"), lax.axis_index("c")
        self.x = 1 - x if k & 4 else x
        self.y = 1 - y if k & 2 else y
        self.c = 1 - c if k & 1 else c
        self.chip = 2 * self.x + self.y
        self.id = 2 * self.chip + self.c


ICI_PEERS = (2, 4, 6)
SIBLING = (1,)
ALL_PEERS = (1, 2, 3, 4, 5, 6, 7)


class _Comm:
    def __init__(self, arrs, out_shape, ks, src, dst, own=None, aliases=None):
        self.arrs, self.out_shape, self.ks = list(arrs), list(out_shape), tuple(ks)
        self.n = len(self.arrs)
        self.src, self.dst, self.own = src, dst, own
        self.aliases = aliases or {}
        dma = pltpu.SemaphoreType.DMA
        self.scratch = [dma((self.n, len(self.ks))), dma((self.n, len(self.ks))), dma((self.n,))]

    def _copies(self, ins, outs, sems, with_recvs):
        send_sems, recv_sems, local_sems = sems
        me = _Place()
        owns, sends, recvs = [], [], []
        for a in range(self.n):
            if self.own is not None:
                s, d = self.own(a, ins[a], outs[a], me)
                owns.append(pltpu.make_async_copy(s, d, local_sems.at[a]))
            for i, k in enumerate(self.ks):
                peer = _Place(k)
                for sender, lst in ((me, sends), (peer, recvs)) if with_recvs else ((me, sends),):
                    lst.append(pltpu.make_async_remote_copy(
                        src_ref=self.src(a, ins[a], me, peer), dst_ref=self.dst(a, outs[a], sender),
                        send_sem=send_sems.at[a, i], recv_sem=recv_sems.at[a, i],
                        device_id=(peer.x, peer.y, peer.c), device_id_type=MESH))
        return owns, sends, recvs

    def start(self, ins, outs, sems):
        owns, sends, _ = self._copies(ins, outs, sems, False)
        for cp in owns + sends:
            cp.start()

    def finish(self, ins, outs, sems):
        owns, sends, recvs = self._copies(ins, outs, sems, True)
        for cp in recvs:
            cp.wait_recv()
        for cp in sends:
            cp.wait_send()
        for cp in owns:
            cp.wait()


def _run_comm(comm, name):
    n = comm.n

    def body(*refs):
        ins, outs, sems = refs[:n], refs[n:2 * n], refs[2 * n:]
        comm.start(ins, outs, sems)
        comm.finish(ins, outs, sems)

    return list(pl.pallas_call(body, in_specs=[ANY] * n, out_specs=[ANY] * n, out_shape=comm.out_shape, scratch_shapes=comm.scratch,
                               input_output_aliases=dict(comm.aliases), name=name)(*comm.arrs))


def _gather_ici(shards):
    return _Comm(shards, [_sds((4, 2) + s.shape, s.dtype) for s in shards], ICI_PEERS,
                 src=lambda a, i, me, p: i, dst=lambda a, o, s: o.at[s.chip, s.c], own=lambda a, i, o, me: (i, o.at[me.chip, me.c]))


def _gather_sibling(bufs):
    return _Comm(bufs, [_sds(b.shape, b.dtype) for b in bufs], SIBLING,
                 src=lambda a, i, me, p: i.at[:, me.c], dst=lambda a, o, s: o.at[:, s.c], aliases={a: a for a in range(len(bufs))})


def _scatter_sibling(parts):
    return _Comm(parts, [_sds((4,) + p.shape[2:], p.dtype) for p in parts], SIBLING,
                 src=lambda a, i, me, p: i.at[:, p.c], dst=lambda a, o, s: o)


def _scatter_ici(parts):
    return _Comm(parts, [_sds(p.shape, p.dtype) for p in parts], ICI_PEERS,
                 src=lambda a, i, me, p: i.at[p.chip], dst=lambda a, o, s: o.at[s.chip], own=lambda a, i, o, me: (i.at[me.chip], o.at[me.chip]))


def _gather_all(arrs):
    return _Comm(arrs, [_sds((N_DEV,) + a.shape, a.dtype) for a in arrs], ALL_PEERS,
                 src=lambda a, i, me, p: i, dst=lambda a, o, s: o.at[s.id], own=lambda a, i, o, me: (i, o.at[me.id]))


def _add_halves(parts, got, name):
    n, _, R, C = parts.shape
    tr = R if R <= 256 else 256
    assert R % tr == 0
    core = lax.axis_index("c").astype(jnp.int32).reshape(1)

    def body(c_ref, p_ref, g_ref, o_ref):
        o_ref[0] = (p_ref[0, 0].astype(F32) + g_ref[0].astype(F32)).astype(o_ref.dtype)

    spec = pltpu.PrefetchScalarGridSpec(
        num_scalar_prefetch=1, grid=(n, R // tr),
        in_specs=[pl.BlockSpec((1, 1, tr, C), lambda q, i, c_ref: (q, c_ref[0], i, 0)), pl.BlockSpec((1, tr, C), lambda q, i, c_ref: (q, i, 0))],
        out_specs=pl.BlockSpec((1, tr, C), lambda q, i, c_ref: (q, i, 0)))
    return pl.pallas_call(body, grid_spec=spec, out_shape=_sds((n, R, C), parts.dtype), name=name,
                          compiler_params=_params("parallel", "parallel"))(core, parts, got)


def _adam(w, m, v, gparts, name):
    R, C = w.shape
    n = gparts.shape[0]
    tr = R if R <= 256 else 128
    assert R % tr == 0
    c1 = 1.0 / (1.0 - ADAM_B1 ** ADAM_STEP)
    c2 = 1.0 / (1.0 - ADAM_B2 ** ADAM_STEP)

    def body(w_ref, m_ref, v_ref, g_ref, go_ref, d_ref, mo_ref, vo_ref):
        g = g_ref[0].astype(F32)
        for s in range(1, n):
            g = g + g_ref[s].astype(F32)
        mn = ADAM_B1 * m_ref[...] + (1.0 - ADAM_B1) * g
        vn = ADAM_B2 * v_ref[...] + (1.0 - ADAM_B2) * (g * g)
        go_ref[...] = g
        mo_ref[...] = mn
        vo_ref[...] = vn
        d_ref[...] = -ADAM_LR * ((mn * c1) / (jnp.sqrt(vn * c2) + ADAM_EPS) + ADAM_WD * w_ref[...])

    blk = pl.BlockSpec((tr, C), lambda i: (i, 0))
    return pl.pallas_call(
        body, grid=(R // tr,), in_specs=[blk, blk, blk, pl.BlockSpec((n, tr, C), lambda i: (0, i, 0))],
        out_specs=[blk] * 4, out_shape=[_sds((R, C), F32)] * 4, name=name, compiler_params=_params("parallel"))(w, m, v, gparts)


_SMALL_ORDER = ("norm_mix", "b_gate", "sc_conv_w", "ssm_conv_w", "ssm_conv_b", "dt_bias", "A_log", "D_skip", "ssm_norm_w",
                "norm_mlp", "norm_final", "loss")
_REPLICATED = ("norm_mix", "b_gate", "ssm_conv_b", "dt_bias", "A_log", "D_skip", "ssm_norm_w", "norm_mlp", "norm_final")


def _cols_to_slots(g, n):
    R = g.shape[0]
    return jnp.transpose(g.reshape(R, n, g.shape[1] // n), (1, 0, 2))


def _slots_to_cols(g):
    n, R, C = g.shape
    return jnp.transpose(g, (1, 0, 2)).reshape(R, n * C)


def kernel(x, norm_mix, w_in, b_gate, sc_conv_w, ssm_conv_w, ssm_conv_b, dt_bias, A_log, D_skip, ssm_norm_w, w_branch_sc, w_branch_ssm, w_out, norm_mlp, w_mlp1, w_mlp2, norm_final, loss_target, m_norm_mix, m_w_in, m_b_gate, m_sc_conv_w, m_ssm_conv_w, m_ssm_conv_b, m_dt_bias, m_A_log, m_D_skip, m_ssm_norm_w, m_w_branch_sc, m_w_branch_ssm, m_w_out, m_norm_mlp, m_w_mlp1, m_w_mlp2, m_norm_final, v_norm_mix, v_w_in, v_b_gate, v_sc_conv_w, v_ssm_conv_w, v_ssm_conv_b, v_dt_bias, v_A_log, v_D_skip, v_ssm_norm_w, v_w_branch_sc, v_w_branch_ssm, v_w_out, v_norm_mlp, v_w_mlp1, v_w_mlp2, v_norm_final):
    T, D = x.shape[1], x.shape[2]
    n_inner = 2 * D
    n_heads = n_inner // HEADDIM
    n_xbc = n_inner + 2 * NGROUPS * NSTATE
    me = 4 * lax.axis_index("x") + 2 * lax.axis_index("y") + lax.axis_index("c")

    o_z, o_xbc, o_dt, o_gate = 3 * D, 3 * D + n_inner, 3 * D + n_inner + n_xbc, 3 * D + n_inner + n_xbc + n_heads
    by_owner = lambda b: b.reshape((N_DEV,) + b.shape[2:])
    to_owner = lambda g: g.reshape((4, 2) + g.shape[1:])
    rows_of = lambda g: to_owner(g.reshape((N_DEV, g.shape[0] // N_DEV) + g.shape[1:]))
    cols_of = lambda g: to_owner(_cols_to_slots(g, N_DEV))

    class Schedule(_NoExchange):
        late = dict(proj_xbc=("bssm",), proj_sc=("bsc", "out"), ssm_conv_fwd=("w2",), ssd_fwd=("w1",))
        sibling_groups = (("bsc", "bssm", "out"), ("w1", "w2"))
        shards = dict(bsc=w_branch_sc, bssm=w_branch_ssm, out=w_out, w1=w_mlp1, w2=w_mlp2)
        grad_groups = (("w2", "w1"), ("out", "bssm", "bsc"), ("win",))
        grad_carrier = dict(ssd_bwd=("out", "bssm", "bsc"), ssm_conv_bwd=("w2", "w1"), proj_dx=("win",))

        def __init__(self):
            bufs = _run_comm(_gather_ici([w_in.astype(BF16), sc_conv_w, ssm_conv_w]), "gather_in_ici")
            bufs = _run_comm(_gather_sibling(bufs), "gather_in_sibling")
            win_full = _slots_to_cols(by_owner(bufs[0]))
            self.W = dict(sc=win_full[:, :o_z], z=win_full[:, o_z:o_xbc], xbc=win_full[:, o_xbc:o_dt],
                          dt=jnp.pad(win_full[:, o_dt:o_gate], ((0, 0), (0, LANES - n_heads))), gate=win_full[:, o_gate:])
            self.taps = dict(sc_conv_w=_slots_to_cols(by_owner(bufs[1])), ssm_conv_w=_slots_to_cols(by_owner(bufs[2])))
            self.staged, self.grads, self.halves, self.summed = {}, {}, {}, {}

        def carry(self, name):
            if name in self.late:
                return _gather_ici([self.shards[k].astype(BF16) for k in self.late[name]])
            if name in self.grad_carrier:
                return _scatter_ici([self.halves[k] for k in self.grad_carrier[name]])
            return None

        def carried(self, name, outs):
            if name in self.late:
                self.staged.update(zip(self.late[name], outs))
            else:
                self.summed.update(zip(self.grad_carrier[name], outs))

        def weight(self, k):
            if k not in self.W:
                group = next(g for g in self.sibling_groups if k in g)
                bufs = _run_comm(_gather_sibling([self.staged.pop(kk) for kk in group]), "gather_sibling_" + group[0])
                for kk, b in zip(group, bufs):
                    full = by_owner(b)
                    self.W[kk] = _slots_to_cols(full) if kk == "w1" else full.reshape(-1, D)
            return self.W[k]

        def grad(self, k, g):
            if k == "win":
                g = cols_of(jnp.concatenate([g["sc"], g["z"], g["xbc"], g["dt"][:, :n_heads], g["gate"]], axis=1))
            else:
                g = cols_of(g) if k == "w1" else rows_of(g)
            self.grads[k] = g
            group = next(gr for gr in self.grad_groups if k in gr)
            if all(kk in self.grads for kk in group):
                bufs = _run_comm(_scatter_sibling([self.grads[kk] for kk in group]), "scatter_sibling_" + group[0])
                for kk, b in zip(group, bufs):
                    self.halves[kk] = _add_halves(self.grads[kk], b, "add_halves_" + kk)

    S = Schedule()
    small = dict(norm_mix=norm_mix, b_gate=b_gate, ssm_conv_b=ssm_conv_b, dt_bias=dt_bias, A_log=A_log, D_skip=D_skip,
                 ssm_norm_w=ssm_norm_w, norm_mlp=norm_mlp, norm_final=norm_final, **S.taps)
    grad_x, g_small = _local_step(x.reshape(T, D), loss_target.reshape(T, D), S, small)

    small_flat = jnp.concatenate([g_small[k].reshape(-1) for k in _SMALL_ORDER])
    n_small = small_flat.shape[0]
    rows = -(-n_small // (8 * LANES)) * 8
    small_pack = jnp.pad(small_flat, (0, rows * LANES - n_small)).reshape(rows, LANES)
    small_parts = _run_comm(_gather_all([small_pack]), "gather_small")[0]

    res = {}
    big = [("w_in", "win", w_in, m_w_in, v_w_in), ("w_branch_sc", "bsc", w_branch_sc, m_w_branch_sc, v_w_branch_sc),
           ("w_branch_ssm", "bssm", w_branch_ssm, m_w_branch_ssm, v_w_branch_ssm), ("w_out", "out", w_out, m_w_out, v_w_out),
           ("w_mlp1", "w1", w_mlp1, m_w_mlp1, v_w_mlp1), ("w_mlp2", "w2", w_mlp2, m_w_mlp2, v_w_mlp2)]
    for k, gk, w, m, v in big:
        res[k] = _adam(w, m, v, S.summed[gk], "adam_" + k)

    sizes = {k: g_small[k].size for k in _SMALL_ORDER}
    offs, o = {}, 0
    for k in _SMALL_ORDER:
        offs[k] = o
        o += sizes[k]
    rep_w = dict(norm_mix=norm_mix, b_gate=b_gate, ssm_conv_b=ssm_conv_b, dt_bias=dt_bias, A_log=A_log, D_skip=D_skip,
                 ssm_norm_w=ssm_norm_w, norm_mlp=norm_mlp, norm_final=norm_final)
    rep_m = dict(norm_mix=m_norm_mix, b_gate=m_b_gate, ssm_conv_b=m_ssm_conv_b, dt_bias=m_dt_bias, A_log=m_A_log, D_skip=m_D_skip,
                 ssm_norm_w=m_ssm_norm_w, norm_mlp=m_norm_mlp, norm_final=m_norm_final)
    rep_v = dict(norm_mix=v_norm_mix, b_gate=v_b_gate, ssm_conv_b=v_ssm_conv_b, dt_bias=v_dt_bias, A_log=v_A_log, D_skip=v_D_skip,
                 ssm_norm_w=v_ssm_norm_w, norm_mlp=v_norm_mlp, norm_final=v_norm_final)

    def pack(d):
        segs = [jnp.pad(d[k].astype(F32).reshape(-1), (0, sizes[k] - d[k].size)) if k in d else jnp.zeros((sizes[k],), F32)
                for k in _SMALL_ORDER]
        return jnp.pad(jnp.concatenate(segs), (0, rows * LANES - n_small)).reshape(rows, LANES)

    sm = _adam(pack(rep_w), pack(rep_m), pack(rep_v), small_parts, "adam_small")
    sm = [s.reshape(-1) for s in sm]
    for k in _REPLICATED:
        n_k = rep_w[k].shape[0]
        res[k] = tuple(s[offs[k]:offs[k] + n_k] for s in sm)
    loss = sm[0][offs["loss"]]
    for k, w, m, v, K, full in (("sc_conv_w", sc_conv_w, m_sc_conv_w, v_sc_conv_w, SC_K, D),
                                ("ssm_conv_w", ssm_conv_w, m_ssm_conv_w, v_ssm_conv_w, SSM_K, n_xbc)):
        g_full = sm[0][offs[k]:offs[k] + K * full].reshape(K, full)
        cw = full // N_DEV
        g_mine = lax.dynamic_slice_in_dim(g_full, me * cw, cw, axis=1)
        res[k] = _adam(w, m, v, g_mine[None], "adam_" + k)

    order = ("norm_mix", "w_in", "b_gate", "sc_conv_w", "ssm_conv_w", "ssm_conv_b", "dt_bias", "A_log", "D_skip", "ssm_norm_w",
             "w_branch_sc", "w_branch_ssm", "w_out", "norm_mlp", "w_mlp1", "w_mlp2", "norm_final")
    outs = [loss, grad_x.reshape(1, T, D)]
    for j in range(4):
        outs += [res[k][j] for k in order]
    return tuple(outs)
```

```python
import functools

import jax
import jax.numpy as jnp
from jax import lax
from jax.experimental import pallas as pl
from jax.experimental.pallas import tpu as pltpu

F32 = jnp.float32
BF16 = jnp.bfloat16

EPS = 1e-6
N_DEV = 8
HEADDIM = 64
NSTATE = 128
CHUNK = 128
NGROUPS = 8
GROUP_W = 256
SC_K = 3
SSM_K = 4
LANES = 128

ADAM_LR = 0.001
ADAM_B1 = 0.9
ADAM_B2 = 0.999
ADAM_EPS = 1e-08
ADAM_WD = 0.01
ADAM_STEP = 10

NN = (((1,), (0,)), ((), ()))
NT = (((1,), (1,)), ((), ()))
TN = (((0,), (0,)), ((), ()))
_DIMS = {"nn": NN, "nt": NT, "tn": TN}

ANY = pl.BlockSpec(memory_space=pl.ANY)
MESH = pl.DeviceIdType.MESH


def _sds(shape, dtype):
    return jax.ShapeDtypeStruct(tuple(shape), dtype)


def _dot(a, b, dims=NN):
    return lax.dot_general(a, b, dims, preferred_element_type=F32)


def _dot3(a, b, dims=NN):
    return lax.dot_general(a, b, dims, preferred_element_type=F32, precision=lax.Precision.HIGH)


def _params(*sem):
    return pltpu.CompilerParams(dimension_semantics=tuple(sem))


def _call(body, *, grid, in_specs, out_specs, out_shape, args, name, sem, scratch=(), comm=None):
    if comm is None:
        outs = pl.pallas_call(body, grid=grid, in_specs=list(in_specs), out_specs=list(out_specs), out_shape=list(out_shape),
                              scratch_shapes=list(scratch), name=name, compiler_params=_params(*sem))(*args)
        return list(outs), None
    n, n_in, n_out, n_scr = comm.n, len(in_specs), len(out_shape), len(scratch)

    def wrapped(*refs):
        ins, c_in = refs[:n_in], refs[n_in:n_in + n]
        outs, c_out = refs[n_in + n:n_in + n + n_out], refs[n_in + n + n_out:n_in + 2 * n + n_out]
        rest = refs[n_in + 2 * n + n_out:]
        scr, sems = rest[:n_scr], rest[n_scr:]
        first, last = None, None
        for d, g in enumerate(grid):
            f, l = pl.program_id(d) == 0, pl.program_id(d) == g - 1
            first, last = (f, l) if first is None else (first & f, last & l)

        @pl.when(first)
        def _():
            comm.start(c_in, c_out, sems)

        body(*ins, *outs, *scr)

        @pl.when(last)
        def _():
            comm.finish(c_in, c_out, sems)

    outs = pl.pallas_call(
        wrapped, grid=grid, in_specs=list(in_specs) + [ANY] * n, out_specs=list(out_specs) + [ANY] * n,
        out_shape=list(out_shape) + comm.out_shape, scratch_shapes=list(scratch) + comm.scratch,
        input_output_aliases={n_in + i: n_out + o for i, o in comm.aliases.items()},
        name=name, compiler_params=_params(*["arbitrary"] * len(grid)))(*args, *comm.arrs)
    return list(outs[:n_out]), list(outs[n_out:])


MM_VMEM_BUDGET = 44 * 2 ** 20


def _mm_tiles(M, N, k_bytes, mn_bytes):
    best = None
    for tm in (2048, 1024, 512, 256, 128):
        for tn in (1024, 512, 256, 128):
            if M % tm or N % tn:
                continue
            need = 2 * ((tm + tn) * k_bytes + tm * tn * mn_bytes) + 4 * tm * tn * 4
            if need <= MM_VMEM_BUDGET and (best is None or (tm * tn, tm) > (best[0] * best[1], best[0])):
                best = (tm, tn)
    assert best is not None, (M, N, k_bytes, mn_bytes)
    return best


def _mm(a, b, *, mode, name, extras=(), epi=None, out_dtypes=(F32,), comm=None):
    a_list = list(a) if isinstance(a, (list, tuple)) else [a]
    b_list = list(b) if isinstance(b, (list, tuple)) else [b]
    if mode == "nn":
        M, N = a_list[0].shape[0], b_list[0].shape[1]
    elif mode == "nt":
        M, N = a_list[0].shape[0], b_list[0].shape[0]
    else:
        M, N = a_list[0].shape[1], b_list[0].shape[1]
    k_bytes = sum((av.shape[0] if mode == "tn" else av.shape[1]) * av.dtype.itemsize for av in a_list)
    mn_bytes = sum(e.dtype.itemsize for e in extras) + sum(jnp.dtype(d).itemsize for d in out_dtypes)
    tm, tn = _mm_tiles(min(M, 2048), min(N, 1024), k_bytes, mn_bytes) if M % 128 == 0 and N % 128 == 0 else (M, N)
    assert M % tm == 0 and N % tn == 0
    a_specs, b_specs = [], []
    for av, bv in zip(a_list, b_list):
        K = av.shape[0] if mode == "tn" else av.shape[1]
        a_specs.append(pl.BlockSpec((K, tm), lambda i, j: (0, i)) if mode == "tn" else pl.BlockSpec((tm, K), lambda i, j: (i, 0)))
        b_specs.append(pl.BlockSpec((tn, K), lambda i, j: (j, 0)) if mode == "nt" else pl.BlockSpec((K, tn), lambda i, j: (0, j)))
    mn_spec = pl.BlockSpec((tm, tn), lambda i, j: (i, j))
    n_p, n_ex = len(a_list), len(extras)
    dims = _DIMS[mode]

    def body(*refs):
        acc = _dot(refs[0][...], refs[n_p][...], dims)
        for p in range(1, n_p):
            acc = acc + _dot(refs[p][...], refs[n_p + p][...], dims)
        rest = refs[2 * n_p:]
        res = (acc,) if epi is None else epi(acc, *[r[...] for r in rest[:n_ex]])
        for o_ref, r in zip(rest[n_ex:], res):
            o_ref[...] = r.astype(o_ref.dtype)

    outs, carried = _call(
        body, grid=(M // tm, N // tn), in_specs=a_specs + b_specs + [mn_spec] * n_ex,
        out_specs=[mn_spec] * len(out_dtypes), out_shape=[_sds((M, N), d) for d in out_dtypes],
        args=a_list + b_list + list(extras), name=name, sem=("parallel", "parallel"), comm=comm)
    res = outs[0] if len(outs) == 1 else outs
    return res if comm is None else (res, carried)


def _epi_add(acc, r):
    return (acc + r,)


def _epi_add2(acc, r):
    s = acc + r
    return (s, s)


def _epi_relu2(acc):
    p = jnp.maximum(acc, 0.0)
    return (p * p,)


def _epi_relu2_bwd(acc, r):
    return (acc * (2.0 * jnp.sqrt(r.astype(F32))),)


def _row(tr, n):
    return pl.BlockSpec((tr, n), lambda i: (i, 0))


def _vec(n):
    return pl.BlockSpec((1, n), lambda i: (0, 0))


def _rms_fwd(x, w, name):
    T, D = x.shape
    tr = min(256, T)

    def body(x_ref, w_ref, o_ref):
        xv = x_ref[...]
        r = lax.rsqrt(jnp.mean(xv * xv, axis=-1, keepdims=True) + EPS)
        o_ref[...] = (xv * r * w_ref[...]).astype(BF16)

    return pl.pallas_call(body, grid=(T // tr,), in_specs=[_row(tr, D), _vec(D)], out_specs=_row(tr, D),
                          out_shape=_sds((T, D), BF16), name=name, compiler_params=_params("parallel"))(x, w)


def _rms_bwd(x, w, dh, dres, name):
    T, D = x.shape
    tr = min(256, T)

    def body(x_ref, w_ref, dh_ref, dres_ref, dx_ref, dxb_ref, dw_ref):
        @pl.when(pl.program_id(0) == 0)
        def _():
            dw_ref[...] = jnp.zeros_like(dw_ref)

        xv = x_ref[...]
        r = lax.rsqrt(jnp.mean(xv * xv, axis=-1, keepdims=True) + EPS)
        xh = xv * r
        dh_v = dh_ref[...]
        dw_ref[...] += jnp.sum(dh_v * xh, axis=0, keepdims=True)
        dxh = dh_v * w_ref[...]
        dx = r * (dxh - xh * jnp.mean(dxh * xh, axis=-1, keepdims=True)) + dres_ref[...]
        dx_ref[...] = dx
        dxb_ref[...] = dx.astype(BF16)

    return pl.pallas_call(
        body, grid=(T // tr,), in_specs=[_row(tr, D), _vec(D), _row(tr, D), _row(tr, D)],
        out_specs=[_row(tr, D), _row(tr, D), _vec(D)],
        out_shape=[_sds((T, D), F32), _sds((T, D), BF16), _sds((1, D), F32)],
        name=name, compiler_params=_params("arbitrary"))(x, w, dh, dres)


def _final(x2, w, tgt, name):
    T, D = x2.shape
    tr = min(256, T)

    def body(x_ref, w_ref, t_ref, dx_ref, dxb_ref, dw_ref, loss_ref):
        @pl.when(pl.program_id(0) == 0)
        def _():
            dw_ref[...] = jnp.zeros_like(dw_ref)
            loss_ref[...] = jnp.zeros_like(loss_ref)

        xv = x_ref[...]
        wv = w_ref[...]
        r = lax.rsqrt(jnp.mean(xv * xv, axis=-1, keepdims=True) + EPS)
        xh = xv * r
        err = xh * wv - t_ref[...]
        part = jnp.sum(jnp.sum(err * err, axis=1, keepdims=True), axis=0, keepdims=True) * (0.5 / D)
        loss_ref[...] += jnp.broadcast_to(part, loss_ref.shape)
        dy = err * (1.0 / D)
        dw_ref[...] += jnp.sum(dy * xh, axis=0, keepdims=True)
        dxh = dy * wv
        dx = r * (dxh - xh * jnp.mean(dxh * xh, axis=-1, keepdims=True))
        dx_ref[...] = dx
        dxb_ref[...] = dx.astype(BF16)

    return pl.pallas_call(
        body, grid=(T // tr,), in_specs=[_row(tr, D), _vec(D), _row(tr, D)],
        out_specs=[_row(tr, D), _row(tr, D), _vec(D), _vec(LANES)],
        out_shape=[_sds((T, D), F32), _sds((T, D), BF16), _sds((1, D), F32), _sds((1, LANES), F32)],
        name=name, compiler_params=_params("arbitrary"))(x2, w, tgt)


def _silu_parts(z):
    s = jax.nn.sigmoid(z)
    return z * s, s * (1.0 + z * (1.0 - s))


def _gnorm_fwd(y, z, w, name, comm=None):
    T, N = y.shape
    tr = min(256, T)

    def body(y_ref, z_ref, w_ref, o_ref):
        for g in range(N // GROUP_W):
            sl = slice(g * GROUP_W, (g + 1) * GROUP_W)
            silu, _ = _silu_parts(z_ref[:, sl])
            yz = y_ref[:, sl] * silu
            r = lax.rsqrt(jnp.mean(yz * yz, axis=-1, keepdims=True) + EPS)
            o_ref[:, sl] = (yz * r * w_ref[:, sl]).astype(BF16)

    outs, carried = _call(body, grid=(T // tr,), in_specs=[_row(tr, N), _row(tr, N), _vec(N)], out_specs=[_row(tr, N)],
                          out_shape=[_sds((T, N), BF16)], args=[y, z, w], name=name, sem=("parallel",), comm=comm)
    return outs[0] if comm is None else (outs[0], carried)


def _gnorm_bwd(y, z, w, dyb, name):
    T, N = y.shape
    tr = min(256, T)

    def body(y_ref, z_ref, w_ref, d_ref, dy_ref, dz_ref, dw_ref):
        @pl.when(pl.program_id(0) == 0)
        def _():
            dw_ref[...] = jnp.zeros_like(dw_ref)

        for g in range(N // GROUP_W):
            sl = slice(g * GROUP_W, (g + 1) * GROUP_W)
            yv = y_ref[:, sl]
            silu, dsilu = _silu_parts(z_ref[:, sl])
            yz = yv * silu
            r = lax.rsqrt(jnp.mean(yz * yz, axis=-1, keepdims=True) + EPS)
            yzh = yz * r
            d = d_ref[:, sl]
            dw_ref[:, sl] += jnp.sum(d * yzh, axis=0, keepdims=True)
            dyzh = d * w_ref[:, sl]
            dyz = r * (dyzh - yzh * jnp.mean(dyzh * yzh, axis=-1, keepdims=True))
            dy_ref[:, sl] = dyz * silu
            dz_ref[:, sl] = (dyz * yv * dsilu).astype(BF16)

    return pl.pallas_call(
        body, grid=(T // tr,), in_specs=[_row(tr, N), _row(tr, N), _vec(N), _row(tr, N)],
        out_specs=[_row(tr, N), _row(tr, N), _vec(N)],
        out_shape=[_sds((T, N), F32), _sds((T, N), BF16), _sds((1, N), F32)],
        name=name, compiler_params=_params("arbitrary"))(y, z, w, dyb)


def _merge_fwd(gate_raw, b_gate, br_a, br_b, name):
    T, D = br_a.shape
    tr = min(256, T)

    def body(g_ref, bg_ref, a_ref, b_ref, o_ref):
        g = jax.nn.sigmoid(g_ref[...] + bg_ref[...])
        o_ref[...] = (g[:, :D] * a_ref[...] + g[:, D:] * b_ref[...]).astype(BF16)

    return pl.pallas_call(body, grid=(T // tr,), in_specs=[_row(tr, 2 * D), _vec(2 * D), _row(tr, D), _row(tr, D)],
                          out_specs=_row(tr, D), out_shape=_sds((T, D), BF16), name=name,
                          compiler_params=_params("parallel"))(gate_raw, b_gate, br_a, br_b)


def _merge_bwd(dmerged, gate_raw, b_gate, br_a, br_b, name):
    T, D = br_a.shape
    tr = min(256, T)

    def body(d_ref, g_ref, bg_ref, a_ref, b_ref, da_ref, db_ref, dg_ref, dbg_ref):
        @pl.when(pl.program_id(0) == 0)
        def _():
            dbg_ref[...] = jnp.zeros_like(dbg_ref)

        g = jax.nn.sigmoid(g_ref[...] + bg_ref[...])
        d = d_ref[...]
        da_ref[...] = (d * g[:, :D]).astype(BF16)
        db_ref[...] = (d * g[:, D:]).astype(BF16)
        dg = jnp.concatenate([d * a_ref[...], d * b_ref[...]], axis=1) * g * (1.0 - g)
        dg_ref[...] = dg.astype(BF16)
        dbg_ref[...] += jnp.sum(dg, axis=0, keepdims=True)

    return pl.pallas_call(
        body, grid=(T // tr,), in_specs=[_row(tr, D), _row(tr, 2 * D), _vec(2 * D), _row(tr, D), _row(tr, D)],
        out_specs=[_row(tr, D), _row(tr, D), _row(tr, 2 * D), _vec(2 * D)],
        out_shape=[_sds((T, D), BF16), _sds((T, D), BF16), _sds((T, 2 * D), BF16), _sds((1, 2 * D), F32)],
        name=name, compiler_params=_params("arbitrary"))(dmerged, gate_raw, b_gate, br_a, br_b)


def _shift_down(u, s):
    if s == 0:
        return u
    row = lax.broadcasted_iota(jnp.int32, u.shape, 0)
    return jnp.where(row >= s, pltpu.roll(u, s, 0), 0.0)


def _shift_up(u, s):
    if s == 0:
        return u
    n = u.shape[0]
    row = lax.broadcasted_iota(jnp.int32, u.shape, 0)
    return jnp.where(row < n - s, pltpu.roll(u, n - s, 0), 0.0)


def _conv(u, w_ref, K):
    acc = u * w_ref[K - 1:K, :]
    for k in range(K - 1):
        acc = acc + _shift_down(u, K - 1 - k) * w_ref[k:k + 1, :]
    return acc


def _conv_bwd(u, dc, w_ref, dw_ref, K):
    du = dc * w_ref[K - 1:K, :]
    dw_ref[K - 1:K, :] = jnp.sum(dc * u, axis=0, keepdims=True)
    for k in range(K - 1):
        s = K - 1 - k
        dw_ref[k:k + 1, :] = jnp.sum(dc * _shift_down(u, s), axis=0, keepdims=True)
        du = du + _shift_up(dc, s) * w_ref[k:k + 1, :]
    return du


CB_W = 256


def _col(T, j0=0):
    return pl.BlockSpec((T, CB_W), lambda j: (0, j + j0))


def _sc_fwd(psc, w, name):
    T, D = psc.shape[0], psc.shape[1] // 3
    nb = D // CB_W

    def body(b_ref, c_ref, x_ref, w_ref, o_ref):
        o_ref[...] = (b_ref[...] * _conv(c_ref[...] * x_ref[...], w_ref, SC_K)).astype(BF16)

    return pl.pallas_call(
        body, grid=(nb,), in_specs=[_col(T), _col(T, nb), _col(T, 2 * nb), pl.BlockSpec((SC_K, CB_W), lambda j: (0, j))],
        out_specs=_col(T), out_shape=_sds((T, D), BF16), name=name, compiler_params=_params("parallel"))(psc, psc, psc, w)


def _sc_bwd(psc, w, dya, name):
    T, D = psc.shape[0], psc.shape[1] // 3
    nb = D // CB_W

    def body(b_ref, c_ref, x_ref, w_ref, d_ref, db_ref, dc_ref, dx_ref, dw_ref):
        cv, xv, d = c_ref[...], x_ref[...], d_ref[...]
        u = cv * xv
        db_ref[...] = (d * _conv(u, w_ref, SC_K)).astype(BF16)
        du = _conv_bwd(u, d * b_ref[...], w_ref, dw_ref, SC_K)
        dc_ref[...] = (du * xv).astype(BF16)
        dx_ref[...] = (du * cv).astype(BF16)

    wspec = pl.BlockSpec((SC_K, CB_W), lambda j: (0, j))
    return pl.pallas_call(
        body, grid=(nb,), in_specs=[_col(T), _col(T, nb), _col(T, 2 * nb), wspec, _col(T)],
        out_specs=[_col(T), _col(T), _col(T), wspec],
        out_shape=[_sds((T, D), BF16)] * 3 + [_sds((SC_K, D), F32)],
        name=name, compiler_params=_params("parallel"))(psc, psc, psc, w, dya)


def _ssm_conv_fwd(u, w, b, name, comm=None):
    T, N = u.shape

    def body(u_ref, w_ref, b_ref, o_ref):
        c = _conv(u_ref[...], w_ref, SSM_K) + b_ref[...]
        o_ref[...] = c * jax.nn.sigmoid(c)

    outs, carried = _call(
        body, grid=(N // CB_W,), in_specs=[_col(T), pl.BlockSpec((SSM_K, CB_W), lambda j: (0, j)), pl.BlockSpec((1, CB_W), lambda j: (0, j))],
        out_specs=[_col(T)], out_shape=[_sds((T, N), F32)], args=[u, w, b], name=name, sem=("parallel",), comm=comm)
    return outs[0] if comm is None else (outs[0], carried)


def _ssm_conv_bwd(u, w, b, dxs, dB, dC, name, comm=None):
    T, N = u.shape
    n_x, n_b = dxs.shape[1] // CB_W, dB.shape[1] // CB_W

    def body(u_ref, w_ref, b_ref, dx_ref, db_ref, dc_ref, du_ref, dw_ref, dbias_ref):
        j = pl.program_id(0)
        uv = u_ref[...]
        c = _conv(uv, w_ref, SSM_K) + b_ref[...]
        _, dsilu = _silu_parts(c)
        d = jnp.where(j < n_x, dx_ref[...], jnp.where(j < n_x + n_b, db_ref[...], dc_ref[...])) * dsilu
        dbias_ref[...] = jnp.sum(d, axis=0, keepdims=True)
        du_ref[...] = _conv_bwd(uv, d, w_ref, dw_ref, SSM_K).astype(BF16)

    wspec = pl.BlockSpec((SSM_K, CB_W), lambda j: (0, j))
    bspec = pl.BlockSpec((1, CB_W), lambda j: (0, j))
    outs, carried = _call(
        body, grid=(N // CB_W,),
        in_specs=[_col(T), wspec, bspec,
                  pl.BlockSpec((T, CB_W), lambda j: (0, jnp.minimum(j, n_x - 1))),
                  pl.BlockSpec((T, CB_W), lambda j: (0, jnp.clip(j - n_x, 0, n_b - 1))),
                  pl.BlockSpec((T, CB_W), lambda j: (0, jnp.clip(j - n_x - n_b, 0, n_b - 1)))],
        out_specs=[_col(T), wspec, bspec],
        out_shape=[_sds((T, N), BF16), _sds((SSM_K, N), F32), _sds((1, N), F32)],
        args=[u, w, b, dxs, dB, dC], name=name, sem=("parallel",), comm=comm)
    return outs if comm is None else (outs, carried)


def _split3(v):
    hi = v.astype(BF16)
    r = v - hi.astype(F32)
    mid = r.astype(BF16)
    lo = (r - mid.astype(F32)).astype(BF16)
    return hi, mid, lo


def _head_expand(n_lanes):
    h = lax.broadcasted_iota(jnp.int32, (LANES, n_lanes), 0)
    l = lax.broadcasted_iota(jnp.int32, (LANES, n_lanes), 1)
    return (jnp.right_shift(l, HEADDIM.bit_length() - 1) == h).astype(BF16)


def _softplus(v):
    return jnp.maximum(v, 0.0) + jnp.log1p(jnp.exp(-jnp.abs(v)))


def _ssd_prep(dt_raw, dt_bias, a_log, n_inner, name):
    T = dt_raw.shape[0]

    def body(r_ref, b_ref, al_ref, dt_ref, cs_ref):
        dt = _softplus(r_ref[...] + b_ref[...])
        a = dt * (-jnp.exp(al_ref[...]))
        i = lax.broadcasted_iota(jnp.int32, (CHUNK, CHUNK), 0)
        j = lax.broadcasted_iota(jnp.int32, (CHUNK, CHUNK), 1)
        tri = (j <= i).astype(BF16)
        cs = sum(_dot(tri, p) for p in _split3(a))
        ex = _head_expand(n_inner)
        dt_ref[...] = sum(_dot(p, ex) for p in _split3(dt))
        cs_ref[...] = sum(_dot(p, ex) for p in _split3(cs))

    blk = pl.BlockSpec((CHUNK, LANES), lambda c: (c, 0))
    out = pl.BlockSpec((CHUNK, n_inner), lambda c: (c, 0))
    return pl.pallas_call(body, grid=(T // CHUNK,), in_specs=[blk, _vec(LANES), _vec(LANES)], out_specs=[out, out],
                          out_shape=[_sds((T, n_inner), F32)] * 2, name=name, compiler_params=_params("parallel"))(dt_raw, dt_bias, a_log)


def _pair_terms(cs_p):
    lane = lax.broadcasted_iota(jnp.int32, (CHUNK, CHUNK), 1)
    sub = lax.broadcasted_iota(jnp.int32, (CHUNK, CHUNK), 0)
    csT = cs_p.T
    Ls = []
    for k in range(2):
        col = jnp.sum(jnp.where(lane == k * HEADDIM, cs_p, 0.0), axis=1, keepdims=True)
        rowv = csT[k * HEADDIM:k * HEADDIM + 1, :]
        Ls.append(jnp.exp(jnp.where(sub >= lane, col - rowv, -jnp.inf)))
    return Ls, jnp.exp(csT[:, CHUNK - 1:CHUNK])


def _block_diag(xp):
    lane = lax.broadcasted_iota(jnp.int32, xp.shape, 1)
    return jnp.concatenate([jnp.where(lane < HEADDIM, xp, 0.0), jnp.where(lane >= HEADDIM, xp, 0.0)], axis=0)


SSD_GROUPS_PER_STEP = 8


def _ssd_specs(T, n_inner):
    nc, gs = T // CHUNK, SSD_GROUPS_PER_STEP
    bo, co = n_inner // (gs * NSTATE), (n_inner + NGROUPS * NSTATE) // (gs * NSTATE)
    assert NGROUPS % gs == 0 and n_inner % (gs * NSTATE) == 0 and (NGROUPS * NSTATE) % (gs * NSTATE) == 0
    g_blk = lambda f: pl.BlockSpec((CHUNK, gs * GROUP_W), lambda c, s: (f(c), s))
    b_blk = lambda f: pl.BlockSpec((CHUNK, gs * NSTATE), lambda c, s: (f(c), bo + s))
    c_blk = lambda f: pl.BlockSpec((CHUNK, gs * NSTATE), lambda c, s: (f(c), co + s))
    return nc, g_blk, b_blk, c_blk


def _ssd_fwd(xbc, dt_e, cs_e, d_e, name, comm=None):
    T = xbc.shape[0]
    n_inner = dt_e.shape[1]
    nc, g_blk, b_blk, c_blk = _ssd_specs(T, n_inner)
    ident = lambda c: c

    gs = SSD_GROUPS_PER_STEP

    def body(xs_ref, b_ref, c_ref, dt_ref, cs_ref, d_ref, y_ref, p_ref, st):
        c, s = pl.program_id(0), pl.program_id(1)

        @pl.when(c == 0)
        def _():
            for gi in range(gs):
                st[s * gs + gi] = jnp.zeros((GROUP_W, NSTATE), F32)

        for gi in range(gs):
            g = s * gs + gi
            gw, gn = slice(gi * GROUP_W, (gi + 1) * GROUP_W), slice(gi * NSTATE, (gi + 1) * NSTATE)
            P = st[g]
            p_ref[0, gi] = P
            xs, dt, cs = xs_ref[:, gw], dt_ref[:, gw], cs_ref[:, gw]
            Bf, Cf = b_ref[:, gn], c_ref[:, gn]
            CBm = _dot3(Cf, Bf, NT)
            X = xs * dt
            decay = jnp.exp(cs[CHUNK - 1:CHUNK, :] - cs)
            y_off = _dot3(Cf, P, NT) * jnp.exp(cs)
            ys, ecl = [], []
            for pr in range(2):
                sl = slice(pr * LANES, (pr + 1) * LANES)
                Ls, e_last = _pair_terms(cs[:, sl])
                ecl.append(e_last)
                Mcat = jnp.concatenate([CBm * L for L in Ls], axis=1)
                ys.append(_dot3(Mcat, _block_diag(X[:, sl])))
            y_ref[:, gw] = jnp.concatenate(ys, axis=1) + y_off + xs * d_ref[:, gw]
            S = _dot3(X * decay, Bf, TN)
            st[g] = P * jnp.concatenate(ecl, axis=0) + S

    p_blk = pl.BlockSpec((1, gs, GROUP_W, NSTATE), lambda c, s: (c, s, 0, 0))
    outs, carried = _call(
        body, grid=(nc, NGROUPS // gs),
        in_specs=[g_blk(ident), b_blk(ident), c_blk(ident), g_blk(ident), g_blk(ident), pl.BlockSpec((1, gs * GROUP_W), lambda c, s: (0, s))],
        out_specs=[g_blk(ident), p_blk],
        out_shape=[_sds((T, n_inner), F32), _sds((nc, NGROUPS, GROUP_W, NSTATE), F32)],
        scratch=[pltpu.VMEM((NGROUPS, GROUP_W, NSTATE), F32)],
        args=[xbc, xbc, xbc, dt_e, cs_e, d_e], name=name, sem=("arbitrary", "arbitrary"), comm=comm)
    return outs if comm is None else (outs, carried)


def _ssd_bwd(xbc, dt_e, cs_e, d_e, states, dy, name, comm=None):
    T = xbc.shape[0]
    n_inner = dt_e.shape[1]
    nc, g_blk, b_blk, c_blk = _ssd_specs(T, n_inner)
    rev = lambda c: nc - 1 - c

    gs = SSD_GROUPS_PER_STEP

    def body(xs_ref, b_ref, c_ref, dt_ref, cs_ref, d_ref, p_ref, pn_ref, dy_ref,
             dxs_ref, db_ref, dc_ref, ddt_ref, dcs_ref, dd_ref, dst):
        cc, s = pl.program_id(0), pl.program_id(1)

        @pl.when(cc == 0)
        def _():
            for gi in range(gs):
                dst[s * gs + gi] = jnp.zeros((GROUP_W, NSTATE), F32)

        for gi in range(gs):
            one_group(s * gs + gi, gi, xs_ref, b_ref, c_ref, dt_ref, cs_ref, d_ref, p_ref, pn_ref, dy_ref,
                      dxs_ref, db_ref, dc_ref, ddt_ref, dcs_ref, dd_ref, dst)

    def one_group(g, gi, xs_ref, b_ref, c_ref, dt_ref, cs_ref, d_ref, p_ref, pn_ref, dy_ref,
                  dxs_ref, db_ref, dc_ref, ddt_ref, dcs_ref, dd_ref, dst):
        gw, gn = slice(gi * GROUP_W, (gi + 1) * GROUP_W), slice(gi * NSTATE, (gi + 1) * NSTATE)
        dS = dst[g]
        P, Pn = p_ref[0, gi], pn_ref[0, gi]
        xs, dt, cs, dY = xs_ref[:, gw], dt_ref[:, gw], cs_ref[:, gw], dy_ref[:, gw]
        Bf, Cf = b_ref[:, gn], c_ref[:, gn]
        Bb, Cb = Bf.astype(BF16), Cf.astype(BF16)
        X = xs * dt
        ecs = jnp.exp(cs)
        decay = jnp.exp(cs[CHUNK - 1:CHUNK, :] - cs)
        CBm = _dot3(Cf, Bf, NT)
        dYe = dY * ecs
        dP_off = _dot3(dYe, Cf, TN)
        dC = _dot(dYe.astype(BF16), P.astype(BF16))
        dcs = dYe * _dot3(Cf, P, NT)
        Xd = X * decay
        dB = _dot(Xd.astype(BF16), dS.astype(BF16))
        E = _dot3(Bf, dS, NT)
        dX = E * decay
        dcs = dcs - E * Xd
        R = _dot3(jnp.ones((8, NSTATE), F32), dS * Pn, NT)
        sub_g = lax.broadcasted_iota(jnp.int32, (CHUNK, GROUP_W), 0)
        dcs = dcs + jnp.where(sub_g == CHUNK - 1, R[0:1, :], 0.0)
        lane = lax.broadcasted_iota(jnp.int32, (CHUNK, CHUNK), 1)
        sub = lax.broadcasted_iota(jnp.int32, (CHUNK, CHUNK), 0)
        dCB = jnp.zeros((CHUNK, CHUNK), F32)
        dXs, dcss, ecl = [], [], []
        for pr in range(2):
            sl = slice(pr * LANES, (pr + 1) * LANES)
            Ls, e_last = _pair_terms(cs[:, sl])
            ecl.append(e_last)
            dYp = dY[:, sl]
            dMcat = _dot3(dYp, _block_diag(X[:, sl]), NT)
            Mcat = jnp.concatenate([CBm * L for L in Ls], axis=1)
            dXt = _dot3(Mcat, dYp, TN)
            dXs.append(jnp.where(lane < HEADDIM, dXt[:CHUNK], dXt[CHUNK:]))
            colacc = jnp.zeros((CHUNK, CHUNK), F32)
            rowacc = jnp.zeros((CHUNK, CHUNK), F32)
            for k in range(2):
                dG = dMcat[:, k * CHUNK:(k + 1) * CHUNK] * Ls[k]
                dCB = dCB + dG
                Q = dG * CBm
                colacc = colacc + jnp.where(lane == k * HEADDIM, jnp.sum(Q, axis=1, keepdims=True), 0.0)
                rowacc = rowacc + jnp.where(sub == k * HEADDIM, jnp.sum(Q, axis=0, keepdims=True), 0.0)
            dcss.append(colacc - rowacc.T)
        dX = dX + jnp.concatenate(dXs, axis=1)
        dcs = dcs + jnp.concatenate(dcss, axis=1)
        dCBb = dCB.astype(BF16)
        dc_ref[:, gn] = dC + _dot(dCBb, Bb)
        db_ref[:, gn] = dB + _dot(dCBb, Cb, TN)
        dxs_ref[:, gw] = dX * dt + dY * d_ref[:, gw]
        ddt_ref[:, gw] = dX * xs
        dcs_ref[:, gw] = dcs
        dd_ref[0, :, gw] = jnp.sum(dY * xs, axis=0, keepdims=True)
        dst[g] = dS * jnp.concatenate(ecl, axis=0) + dP_off

    p_blk = pl.BlockSpec((1, gs, GROUP_W, NSTATE), lambda c, s: (nc - 1 - c, s, 0, 0))
    pn_blk = pl.BlockSpec((1, gs, GROUP_W, NSTATE), lambda c, s: (jnp.minimum(nc - c, nc - 1), s, 0, 0))
    st_blk = pl.BlockSpec((CHUNK, gs * NSTATE), lambda c, s: (nc - 1 - c, s))
    outs, carried = _call(
        body, grid=(nc, NGROUPS // gs),
        in_specs=[g_blk(rev), b_blk(rev), c_blk(rev), g_blk(rev), g_blk(rev), pl.BlockSpec((1, gs * GROUP_W), lambda c, s: (0, s)),
                  p_blk, pn_blk, g_blk(rev)],
        out_specs=[g_blk(rev), st_blk, st_blk, g_blk(rev), g_blk(rev), pl.BlockSpec((1, 1, gs * GROUP_W), lambda c, s: (nc - 1 - c, 0, s))],
        out_shape=[_sds((T, n_inner), F32), _sds((T, NGROUPS * NSTATE), F32), _sds((T, NGROUPS * NSTATE), F32),
                   _sds((T, n_inner), F32), _sds((T, n_inner), F32), _sds((nc, 1, n_inner), F32)],
        scratch=[pltpu.VMEM((NGROUPS, GROUP_W, NSTATE), F32)],
        args=[xbc, xbc, xbc, dt_e, cs_e, d_e, states, states, dy], name=name, sem=("arbitrary", "arbitrary"), comm=comm)
    return outs if comm is None else (outs, carried)


def _ssd_post(ddt_e, dcs_e, dd_p, dt_raw, dt_bias, a_log, n_heads, name):
    T, n_inner = ddt_e.shape

    def body(ddt_ref, dcs_ref, dd_ref, r_ref, b_ref, al_ref, draw_ref, dbias_ref, dal_ref, ddsk_ref):
        @pl.when(pl.program_id(0) == 0)
        def _():
            dbias_ref[...] = jnp.zeros_like(dbias_ref)
            dal_ref[...] = jnp.zeros_like(dal_ref)
            ddsk_ref[...] = jnp.zeros_like(ddsk_ref)

        ex = _head_expand(n_inner)
        red = lambda v: sum(_dot(p, ex, NT) for p in _split3(v))
        raw = r_ref[...] + b_ref[...]
        dt = _softplus(raw)
        A = -jnp.exp(al_ref[...])
        i = lax.broadcasted_iota(jnp.int32, (CHUNK, CHUNK), 0)
        j = lax.broadcasted_iota(jnp.int32, (CHUNK, CHUNK), 1)
        upper = (j >= i).astype(BF16)
        da = sum(_dot(upper, p) for p in _split3(red(dcs_ref[...])))
        ddt = red(ddt_ref[...]) + da * A
        lane = lax.broadcasted_iota(jnp.int32, (CHUNK, LANES), 1)
        draw = jnp.where(lane < n_heads, ddt * jax.nn.sigmoid(raw), 0.0)
        draw_ref[...] = draw.astype(BF16)
        dbias_ref[...] += jnp.sum(draw, axis=0, keepdims=True)
        dal_ref[...] += jnp.sum(da * dt, axis=0, keepdims=True) * A
        ddsk_ref[...] += red(jnp.broadcast_to(dd_ref[0], (8, n_inner)))[0:1, :]

    wide = pl.BlockSpec((CHUNK, n_inner), lambda c: (c, 0))
    blk = pl.BlockSpec((CHUNK, LANES), lambda c: (c, 0))
    return pl.pallas_call(
        body, grid=(T // CHUNK,),
        in_specs=[wide, wide, pl.BlockSpec((1, 1, n_inner), lambda c: (c, 0, 0)), blk, _vec(LANES), _vec(LANES)],
        out_specs=[blk, _vec(LANES), _vec(LANES), _vec(LANES)],
        out_shape=[_sds((T, LANES), BF16)] + [_sds((1, LANES), F32)] * 3,
        name=name, compiler_params=_params("arbitrary"))(ddt_e, dcs_e, dd_p, dt_raw, dt_bias, a_log)


def _row2(v):
    return v.reshape(1, -1).astype(F32)


def _pad_lanes(v):
    return jnp.pad(_row2(v), ((0, 0), (0, LANES - v.shape[-1])))


class _NoExchange:
    def __init__(self, W):
        self.W, self.grads = W, {}

    def weight(self, k):
        return self.W[k]

    def carry(self, name):
        return None

    def carried(self, name, outs):
        pass

    def grad(self, k, g):
        self.grads[k] = g


def _local_step(x, tgt, S, small):
    T, D = x.shape

    def mm(a, b, *, name, **kw):
        comm = S.carry(name)
        if comm is None:
            return _mm(a, b, name=name, **kw)
        res, outs = _mm(a, b, name=name, comm=comm, **kw)
        S.carried(name, outs)
        return res

    def carrying(fn, *args, name):
        comm = S.carry(name)
        if comm is None:
            return fn(*args, name)
        res, outs = fn(*args, name, comm=comm)
        S.carried(name, outs)
        return res

    n_inner = 2 * D
    n_heads = n_inner // HEADDIM
    norm_mix, norm_mlp, norm_final = _row2(small["norm_mix"]), _row2(small["norm_mlp"]), _row2(small["norm_final"])
    b_gate, ssm_b, ssm_norm_w = _row2(small["b_gate"]), _row2(small["ssm_conv_b"]), _row2(small["ssm_norm_w"])
    dt_bias, a_log = _pad_lanes(small["dt_bias"]), _pad_lanes(small["A_log"])
    d_e = jnp.repeat(small["D_skip"].astype(F32), HEADDIM).reshape(1, n_inner)
    sc_w, ssm_w = small["sc_conv_w"], small["ssm_conv_w"]

    hb = _rms_fwd(x, norm_mix, "rms_mix")
    p_xbc = mm(hb, S.weight("xbc"), mode="nn", name="proj_xbc")
    p_dt = mm(hb, S.weight("dt"), mode="nn", name="proj_dt")
    p_z = mm(hb, S.weight("z"), mode="nn", name="proj_z")
    p_sc = mm(hb, S.weight("sc"), mode="nn", name="proj_sc")
    p_gate = mm(hb, S.weight("gate"), mode="nn", name="proj_gate")
    xbc = carrying(_ssm_conv_fwd, p_xbc, ssm_w, ssm_b, name="ssm_conv_fwd")
    dt_e, cs_e = _ssd_prep(p_dt, dt_bias, a_log, n_inner, "ssd_prep")
    y, states = carrying(_ssd_fwd, xbc, dt_e, cs_e, d_e, name="ssd_fwd")
    yb = carrying(_gnorm_fwd, y, p_z, ssm_norm_w, name="gnorm_fwd")
    ya = _sc_fwd(p_sc, sc_w, "sc_fwd")
    br_a = mm(ya, S.weight("bsc"), mode="nn", name="branch_sc")
    br_b = mm(yb, S.weight("bssm"), mode="nn", name="branch_ssm")
    merged = _merge_fwd(p_gate, b_gate, br_a, br_b, "merge_fwd")
    x1 = mm(merged, S.weight("out"), mode="nn", name="out_proj", extras=(x,), epi=_epi_add)
    h2 = _rms_fwd(x1, norm_mlp, "rms_mlp")
    r_act = mm(h2, S.weight("w1"), mode="nn", name="mlp_up", epi=_epi_relu2, out_dtypes=(BF16,))
    x2 = mm(r_act, S.weight("w2"), mode="nn", name="mlp_down", extras=(x1,), epi=_epi_add)
    dx2, dx2b, g_norm_final, loss_row = _final(x2, norm_final, tgt, "final")

    S.grad("w2", mm(r_act, dx2b, mode="tn", name="mlp_down_dw", out_dtypes=(BF16,)))
    da = mm(dx2b, S.weight("w2"), mode="nt", name="mlp_down_dx", extras=(r_act,), epi=_epi_relu2_bwd, out_dtypes=(BF16,))
    S.grad("w1", mm(h2, da, mode="tn", name="mlp_up_dw", out_dtypes=(BF16,)))
    dh2 = mm(da, S.weight("w1"), mode="nt", name="mlp_up_dx")
    dx1, dx1b, g_norm_mlp = _rms_bwd(x1, norm_mlp, dh2, dx2, "rms_mlp_bwd")
    S.grad("out", mm(merged, dx1b, mode="tn", name="out_proj_dw", out_dtypes=(BF16,)))
    dmerged = mm(dx1b, S.weight("out"), mode="nt", name="out_proj_dx")
    dbr_a, dbr_b, d_gate, g_b_gate = _merge_bwd(dmerged, p_gate, b_gate, br_a, br_b, "merge_bwd")
    S.grad("bssm", mm(yb, dbr_b, mode="tn", name="branch_ssm_dw", out_dtypes=(BF16,)))
    S.grad("bsc", mm(ya, dbr_a, mode="tn", name="branch_sc_dw", out_dtypes=(BF16,)))
    dyb = mm(dbr_b, S.weight("bssm"), mode="nt", name="branch_ssm_dx")
    dya = mm(dbr_a, S.weight("bsc"), mode="nt", name="branch_sc_dx")
    dy, d_z, g_ssm_norm_w = _gnorm_bwd(y, p_z, ssm_norm_w, dyb, "gnorm_bwd")
    dxs, dB, dC, ddt_e, dcs_e, dd_p = carrying(_ssd_bwd, xbc, dt_e, cs_e, d_e, states, dy, name="ssd_bwd")
    d_dt, g_dt_bias, g_a_log, g_d_skip = _ssd_post(ddt_e, dcs_e, dd_p, p_dt, dt_bias, a_log, n_heads, "ssd_post")
    d_xbc, g_ssm_w, g_ssm_b = carrying(_ssm_conv_bwd, p_xbc, ssm_w, ssm_b, dxs, dB, dC, name="ssm_conv_bwd")
    d_scB, d_scC, d_scX, g_sc_w = _sc_bwd(p_sc, sc_w, dya, "sc_bwd")
    d_sc = jnp.concatenate([d_scB, d_scC, d_scX], axis=1)
    pieces = [("sc", d_sc), ("z", d_z), ("xbc", d_xbc), ("dt", d_dt), ("gate", d_gate)]
    S.grad("win", {k: mm(hb, d, mode="tn", name="proj_dw_" + k, out_dtypes=(BF16,)) for k, d in pieces})
    dh = mm([d for _, d in pieces], [S.weight(k) for k, _ in pieces], mode="nt", name="proj_dx")
    grad_x, _, g_norm_mix = _rms_bwd(x, norm_mix, dh, dx1, "rms_mix_bwd")

    g_small = dict(norm_mix=g_norm_mix, b_gate=g_b_gate, sc_conv_w=g_sc_w, ssm_conv_w=g_ssm_w, ssm_conv_b=g_ssm_b,
                   dt_bias=g_dt_bias, A_log=g_a_log, D_skip=g_d_skip, ssm_norm_w=g_ssm_norm_w, norm_mlp=g_norm_mlp,
                   norm_final=g_norm_final, loss=loss_row)
    return grad_x, g_small


class _Place:
    def __init__(self, k=0):
        x, y, c = lax.axis_index("x"), lax.axis_index("y"), lax.axis_index("c")
        self.x = 1 - x if k & 4 else x
        self.y = 1 - y if k & 2 else y
        self.c = 1 - c if k & 1 else c
        self.chip = 2 * self.x + self.y
        self.id = 2 * self.chip + self.c


ICI_PEERS = (2, 4, 6)
SIBLING = (1,)
ALL_PEERS = (1, 2, 3, 4, 5, 6, 7)


class _Comm:
    def __init__(self, arrs, out_shape, ks, src, dst, own=None, aliases=None):
        self.arrs, self.out_shape, self.ks = list(arrs), list(out_shape), tuple(ks)
        self.n = len(self.arrs)
        self.src, self.dst, self.own = src, dst, own
        self.aliases = aliases or {}
        dma = pltpu.SemaphoreType.DMA
        self.scratch = [dma((self.n, len(self.ks))), dma((self.n, len(self.ks))), dma((self.n,))]

    def _copies(self, ins, outs, sems, with_recvs):
        send_sems, recv_sems, local_sems = sems
        me = _Place()
        owns, sends, recvs = [], [], []
        for a in range(self.n):
            if self.own is not None:
                s, d = self.own(a, ins[a], outs[a], me)
                owns.append(pltpu.make_async_copy(s, d, local_sems.at[a]))
            for i, k in enumerate(self.ks):
                peer = _Place(k)
                for sender, lst in ((me, sends), (peer, recvs)) if with_recvs else ((me, sends),):
                    lst.append(pltpu.make_async_remote_copy(
                        src_ref=self.src(a, ins[a], me, peer), dst_ref=self.dst(a, outs[a], sender),
                        send_sem=send_sems.at[a, i], recv_sem=recv_sems.at[a, i],
                        device_id=(peer.x, peer.y, peer.c), device_id_type=MESH))
        return owns, sends, recvs

    def start(self, ins, outs, sems):
        owns, sends, _ = self._copies(ins, outs, sems, False)
        for cp in owns + sends:
            cp.start()

    def finish(self, ins, outs, sems):
        owns, sends, recvs = self._copies(ins, outs, sems, True)
        for cp in recvs:
            cp.wait_recv()
        for cp in sends:
            cp.wait_send()
        for cp in owns:
            cp.wait()


def _run_comm(comm, name):
    n = comm.n

    def body(*refs):
        ins, outs, sems = refs[:n], refs[n:2 * n], refs[2 * n:]
        comm.start(ins, outs, sems)
        comm.finish(ins, outs, sems)

    return list(pl.pallas_call(body, in_specs=[ANY] * n, out_specs=[ANY] * n, out_shape=comm.out_shape, scratch_shapes=comm.scratch,
                               input_output_aliases=dict(comm.aliases), name=name)(*comm.arrs))


def _gather_ici(shards):
    return _Comm(shards, [_sds((4, 2) + s.shape, s.dtype) for s in shards], ICI_PEERS,
                 src=lambda a, i, me, p: i, dst=lambda a, o, s: o.at[s.chip, s.c], own=lambda a, i, o, me: (i, o.at[me.chip, me.c]))


def _gather_sibling(bufs):
    return _Comm(bufs, [_sds(b.shape, b.dtype) for b in bufs], SIBLING,
                 src=lambda a, i, me, p: i.at[:, me.c], dst=lambda a, o, s: o.at[:, s.c], aliases={a: a for a in range(len(bufs))})


def _scatter_sibling(parts):
    return _Comm(parts, [_sds((4,) + p.shape[2:], p.dtype) for p in parts], SIBLING,
                 src=lambda a, i, me, p: i.at[:, p.c], dst=lambda a, o, s: o)


def _scatter_ici(parts):
    return _Comm(parts, [_sds(p.shape, p.dtype) for p in parts], ICI_PEERS,
                 src=lambda a, i, me, p: i.at[p.chip], dst=lambda a, o, s: o.at[s.chip], own=lambda a, i, o, me: (i.at[me.chip], o.at[me.chip]))


def _gather_all(arrs):
    return _Comm(arrs, [_sds((N_DEV,) + a.shape, a.dtype) for a in arrs], ALL_PEERS,
                 src=lambda a, i, me, p: i, dst=lambda a, o, s: o.at[s.id], own=lambda a, i, o, me: (i, o.at[me.id]))


def _add_halves(parts, got, name):
    n, _, R, C = parts.shape
    tr = R if R <= 256 else 256
    assert R % tr == 0
    core = lax.axis_index("c").astype(jnp.int32).reshape(1)

    def body(c_ref, p_ref, g_ref, o_ref):
        o_ref[0] = (p_ref[0, 0].astype(F32) + g_ref[0].astype(F32)).astype(o_ref.dtype)

    spec = pltpu.PrefetchScalarGridSpec(
        num_scalar_prefetch=1, grid=(n, R // tr),
        in_specs=[pl.BlockSpec((1, 1, tr, C), lambda q, i, c_ref: (q, c_ref[0], i, 0)), pl.BlockSpec((1, tr, C), lambda q, i, c_ref: (q, i, 0))],
        out_specs=pl.BlockSpec((1, tr, C), lambda q, i, c_ref: (q, i, 0)))
    return pl.pallas_call(body, grid_spec=spec, out_shape=_sds((n, R, C), parts.dtype), name=name,
                          compiler_params=_params("parallel", "parallel"))(core, parts, got)


def _adam(w, m, v, gparts, name):
    R, C = w.shape
    n = gparts.shape[0]
    tr = R if R <= 256 else 128
    assert R % tr == 0
    c1 = 1.0 / (1.0 - ADAM_B1 ** ADAM_STEP)
    c2 = 1.0 / (1.0 - ADAM_B2 ** ADAM_STEP)

    def body(w_ref, m_ref, v_ref, g_ref, go_ref, d_ref, mo_ref, vo_ref):
        g = g_ref[0].astype(F32)
        for s in range(1, n):
            g = g + g_ref[s].astype(F32)
        mn = ADAM_B1 * m_ref[...] + (1.0 - ADAM_B1) * g
        vn = ADAM_B2 * v_ref[...] + (1.0 - ADAM_B2) * (g * g)
        go_ref[...] = g
        mo_ref[...] = mn
        vo_ref[...] = vn
        d_ref[...] = -ADAM_LR * ((mn * c1) / (jnp.sqrt(vn * c2) + ADAM_EPS) + ADAM_WD * w_ref[...])

    blk = pl.BlockSpec((tr, C), lambda i: (i, 0))
    return pl.pallas_call(
        body, grid=(R // tr,), in_specs=[blk, blk, blk, pl.BlockSpec((n, tr, C), lambda i: (0, i, 0))],
        out_specs=[blk] * 4, out_shape=[_sds((R, C), F32)] * 4, name=name, compiler_params=_params("parallel"))(w, m, v, gparts)


_SMALL_ORDER = ("norm_mix", "b_gate", "sc_conv_w", "ssm_conv_w", "ssm_conv_b", "dt_bias", "A_log", "D_skip", "ssm_norm_w",
                "norm_mlp", "norm_final", "loss")
_REPLICATED = ("norm_mix", "b_gate", "ssm_conv_b", "dt_bias", "A_log", "D_skip", "ssm_norm_w", "norm_mlp", "norm_final")


def _cols_to_slots(g, n):
    R = g.shape[0]
    return jnp.transpose(g.reshape(R, n, g.shape[1] // n), (1, 0, 2))


def _slots_to_cols(g):
    n, R, C = g.shape
    return jnp.transpose(g, (1, 0, 2)).reshape(R, n * C)


def kernel(x, norm_mix, w_in, b_gate, sc_conv_w, ssm_conv_w, ssm_conv_b, dt_bias, A_log, D_skip, ssm_norm_w, w_branch_sc, w_branch_ssm, w_out, norm_mlp, w_mlp1, w_mlp2, norm_final, loss_target, m_norm_mix, m_w_in, m_b_gate, m_sc_conv_w, m_ssm_conv_w, m_ssm_conv_b, m_dt_bias, m_A_log, m_D_skip, m_ssm_norm_w, m_w_branch_sc, m_w_branch_ssm, m_w_out, m_norm_mlp, m_w_mlp1, m_w_mlp2, m_norm_final, v_norm_mix, v_w_in, v_b_gate, v_sc_conv_w, v_ssm_conv_w, v_ssm_conv_b, v_dt_bias, v_A_log, v_D_skip, v_ssm_norm_w, v_w_branch_sc, v_w_branch_ssm, v_w_out, v_norm_mlp, v_w_mlp1, v_w_mlp2, v_norm_final):
    T, D = x.shape[1], x.shape[2]
    n_inner = 2 * D
    n_heads = n_inner // HEADDIM
    n_xbc = n_inner + 2 * NGROUPS * NSTATE
    me = 4 * lax.axis_index("x") + 2 * lax.axis_index("y") + lax.axis_index("c")

    o_z, o_xbc, o_dt, o_gate = 3 * D, 3 * D + n_inner, 3 * D + n_inner + n_xbc, 3 * D + n_inner + n_xbc + n_heads
    by_owner = lambda b: b.reshape((N_DEV,) + b.shape[2:])
    to_owner = lambda g: g.reshape((4, 2) + g.shape[1:])
    rows_of = lambda g: to_owner(g.reshape((N_DEV, g.shape[0] // N_DEV) + g.shape[1:]))
    cols_of = lambda g: to_owner(_cols_to_slots(g, N_DEV))

    class Schedule(_NoExchange):
        gather_ici = dict(proj_xbc=("bssm",), proj_sc=("bsc", "out"), ssm_conv_fwd=("w2",), ssd_fwd=("w1",))
        gather_sib = dict(gnorm_fwd=("bsc", "bssm", "out"), branch_ssm=("w1", "w2"))
        scatter_sib = dict(mlp_up_dx=("w2", "w1"), branch_ssm_dx=("out", "bssm", "bsc"))
        scatter_ici = dict(ssd_bwd=("out", "bssm", "bsc"), ssm_conv_bwd=("w2", "w1"), proj_dx=("win",))
        shards = dict(bsc=w_branch_sc, bssm=w_branch_ssm, out=w_out, w1=w_mlp1, w2=w_mlp2)

        def __init__(self):
            bufs = _run_comm(_gather_ici([w_in.astype(BF16), sc_conv_w, ssm_conv_w]), "gather_in_ici")
            bufs = _run_comm(_gather_sibling(bufs), "gather_in_sibling")
            win_full = _slots_to_cols(by_owner(bufs[0]))
            self.W = dict(sc=win_full[:, :o_z], z=win_full[:, o_z:o_xbc], xbc=win_full[:, o_xbc:o_dt],
                          dt=jnp.pad(win_full[:, o_dt:o_gate], ((0, 0), (0, LANES - n_heads))), gate=win_full[:, o_gate:])
            self.taps = dict(sc_conv_w=_slots_to_cols(by_owner(bufs[1])), ssm_conv_w=_slots_to_cols(by_owner(bufs[2])))
            self.staged, self.grads, self.halves, self.summed = {}, {}, {}, {}

        def carry(self, name):
            if name in self.gather_ici:
                return _gather_ici([self.shards[k].astype(BF16) for k in self.gather_ici[name]])
            if name in self.gather_sib:
                return _gather_sibling([self.staged.pop(k) for k in self.gather_sib[name]])
            if name in self.scatter_sib:
                return _scatter_sibling([self.grads[k] for k in self.scatter_sib[name]])
            if name in self.scatter_ici:
                return _scatter_ici([self.halves[k] for k in self.scatter_ici[name]])
            return None

        def carried(self, name, outs):
            if name in self.gather_ici:
                self.staged.update(zip(self.gather_ici[name], outs))
            elif name in self.gather_sib:
                for k, b in zip(self.gather_sib[name], outs):
                    full = by_owner(b)
                    self.W[k] = _slots_to_cols(full) if k == "w1" else full.reshape(-1, D)
            elif name in self.scatter_sib:
                for k, b in zip(self.scatter_sib[name], outs):
                    self.halves[k] = _add_halves(self.grads[k], b, "add_halves_" + k)
            else:
                self.summed.update(zip(self.scatter_ici[name], outs))

        def grad(self, k, g):
            if k == "win":
                g = cols_of(jnp.concatenate([g["sc"], g["z"], g["xbc"], g["dt"][:, :n_heads], g["gate"]], axis=1))
                got = _run_comm(_scatter_sibling([g]), "scatter_sibling_win")[0]
                self.halves[k] = _add_halves(g, got, "add_halves_win")
            else:
                self.grads[k] = cols_of(g) if k == "w1" else rows_of(g)

    S = Schedule()
    small = dict(norm_mix=norm_mix, b_gate=b_gate, ssm_conv_b=ssm_conv_b, dt_bias=dt_bias, A_log=A_log, D_skip=D_skip,
                 ssm_norm_w=ssm_norm_w, norm_mlp=norm_mlp, norm_final=norm_final, **S.taps)
    grad_x, g_small = _local_step(x.reshape(T, D), loss_target.reshape(T, D), S, small)

    small_flat = jnp.concatenate([g_small[k].reshape(-1) for k in _SMALL_ORDER])
    n_small = small_flat.shape[0]
    rows = -(-n_small // (8 * LANES)) * 8
    small_pack = jnp.pad(small_flat, (0, rows * LANES - n_small)).reshape(rows, LANES)
    small_parts = _run_comm(_gather_all([small_pack]), "gather_small")[0]

    res = {}
    big = [("w_in", "win", w_in, m_w_in, v_w_in), ("w_branch_sc", "bsc", w_branch_sc, m_w_branch_sc, v_w_branch_sc),
           ("w_branch_ssm", "bssm", w_branch_ssm, m_w_branch_ssm, v_w_branch_ssm), ("w_out", "out", w_out, m_w_out, v_w_out),
           ("w_mlp1", "w1", w_mlp1, m_w_mlp1, v_w_mlp1), ("w_mlp2", "w2", w_mlp2, m_w_mlp2, v_w_mlp2)]
    for k, gk, w, m, v in big:
        res[k] = _adam(w, m, v, S.summed[gk], "adam_" + k)

    sizes = {k: g_small[k].size for k in _SMALL_ORDER}
    offs, o = {}, 0
    for k in _SMALL_ORDER:
        offs[k] = o
        o += sizes[k]
    rep_w = dict(norm_mix=norm_mix, b_gate=b_gate, ssm_conv_b=ssm_conv_b, dt_bias=dt_bias, A_log=A_log, D_skip=D_skip,
                 ssm_norm_w=ssm_norm_w, norm_mlp=norm_mlp, norm_final=norm_final)
    rep_m = dict(norm_mix=m_norm_mix, b_gate=m_b_gate, ssm_conv_b=m_ssm_conv_b, dt_bias=m_dt_bias, A_log=m_A_log, D_skip=m_D_skip,
                 ssm_norm_w=m_ssm_norm_w, norm_mlp=m_norm_mlp, norm_final=m_norm_final)
    rep_v = dict(norm_mix=v_norm_mix, b_gate=v_b_gate, ssm_conv_b=v_ssm_conv_b, dt_bias=v_dt_bias, A_log=v_A_log, D_skip=v_D_skip,
                 ssm_norm_w=v_ssm_norm_w, norm_mlp=v_norm_mlp, norm_final=v_norm_final)

    def pack(d):
        segs = [jnp.pad(d[k].astype(F32).reshape(-1), (0, sizes[k] - d[k].size)) if k in d else jnp.zeros((sizes[k],), F32)
                for k in _SMALL_ORDER]
        return jnp.pad(jnp.concatenate(segs), (0, rows * LANES - n_small)).reshape(rows, LANES)

    sm = _adam(pack(rep_w), pack(rep_m), pack(rep_v), small_parts, "adam_small")
    sm = [s.reshape(-1) for s in sm]
    for k in _REPLICATED:
        n_k = rep_w[k].shape[0]
        res[k] = tuple(s[offs[k]:offs[k] + n_k] for s in sm)
    loss = sm[0][offs["loss"]]
    for k, w, m, v, K, full in (("sc_conv_w", sc_conv_w, m_sc_conv_w, v_sc_conv_w, SC_K, D),
                                ("ssm_conv_w", ssm_conv_w, m_ssm_conv_w, v_ssm_conv_w, SSM_K, n_xbc)):
        g_full = sm[0][offs[k]:offs[k] + K * full].reshape(K, full)
        cw = full // N_DEV
        g_mine = lax.dynamic_slice_in_dim(g_full, me * cw, cw, axis=1)
        res[k] = _adam(w, m, v, g_mine[None], "adam_" + k)

    order = ("norm_mix", "w_in", "b_gate", "sc_conv_w", "ssm_conv_w", "ssm_conv_b", "dt_bias", "A_log", "D_skip", "ssm_norm_w",
             "w_branch_sc", "w_branch_ssm", "w_out", "norm_mlp", "w_mlp1", "w_mlp2", "norm_final")
    outs = [loss, grad_x.reshape(1, T, D)]
    for j in range(4):
        outs += [res[k][j] for k in order]
    return tuple(outs)
```

```python
import functools

import jax
import jax.numpy as jnp
from jax import lax
from jax.experimental import pallas as pl
from jax.experimental.pallas import tpu as pltpu

F32 = jnp.float32
BF16 = jnp.bfloat16

EPS = 1e-6
N_DEV = 8
HEADDIM = 64
NSTATE = 128
CHUNK = 128
NGROUPS = 8
GROUP_W = 256
SC_K = 3
SSM_K = 4
LANES = 128

ADAM_LR = 0.001
ADAM_B1 = 0.9
ADAM_B2 = 0.999
ADAM_EPS = 1e-08
ADAM_WD = 0.01
ADAM_STEP = 10

NN = (((1,), (0,)), ((), ()))
NT = (((1,), (1,)), ((), ()))
TN = (((0,), (0,)), ((), ()))
_DIMS = {"nn": NN, "nt": NT, "tn": TN}

ANY = pl.BlockSpec(memory_space=pl.ANY)
MESH = pl.DeviceIdType.MESH


def _sds(shape, dtype):
    return jax.ShapeDtypeStruct(tuple(shape), dtype)


def _dot(a, b, dims=NN):
    return lax.dot_general(a, b, dims, preferred_element_type=F32)


def _dot3(a, b, dims=NN):
    return lax.dot_general(a, b, dims, preferred_element_type=F32, precision=lax.Precision.HIGH)


def _params(*sem):
    return pltpu.CompilerParams(dimension_semantics=tuple(sem))


def _call(body, *, grid, in_specs, out_specs, out_shape, args, name, sem, scratch=(), comm=None):
    if comm is None:
        outs = pl.pallas_call(body, grid=grid, in_specs=list(in_specs), out_specs=list(out_specs), out_shape=list(out_shape),
                              scratch_shapes=list(scratch), name=name, compiler_params=_params(*sem))(*args)
        return list(outs), None
    n, n_in, n_out, n_scr = comm.n, len(in_specs), len(out_shape), len(scratch)

    def wrapped(*refs):
        ins, c_in = refs[:n_in], refs[n_in:n_in + n]
        outs, c_out = refs[n_in + n:n_in + n + n_out], refs[n_in + n + n_out:n_in + 2 * n + n_out]
        rest = refs[n_in + 2 * n + n_out:]
        scr, sems = rest[:n_scr], rest[n_scr:]
        first, last = None, None
        for d, g in enumerate(grid):
            f, l = pl.program_id(d) == 0, pl.program_id(d) == g - 1
            first, last = (f, l) if first is None else (first & f, last & l)

        @pl.when(first)
        def _():
            comm.start(c_in, c_out, sems)

        body(*ins, *outs, *scr)

        @pl.when(last)
        def _():
            comm.finish(c_in, c_out, sems)

    outs = pl.pallas_call(
        wrapped, grid=grid, in_specs=list(in_specs) + [ANY] * n, out_specs=list(out_specs) + [ANY] * n,
        out_shape=list(out_shape) + comm.out_shape, scratch_shapes=list(scratch) + comm.scratch,
        input_output_aliases={n_in + i: n_out + o for i, o in comm.aliases.items()},
        name=name, compiler_params=_params(*["arbitrary"] * len(grid)))(*args, *comm.arrs)
    return list(outs[:n_out]), list(outs[n_out:])


MM_VMEM_BUDGET = 44 * 2 ** 20


def _mm_tiles(M, N, k_bytes, mn_bytes):
    best = None
    for tm in (2048, 1024, 512, 256, 128):
        for tn in (1024, 512, 256, 128):
            if M % tm or N % tn:
                continue
            need = 2 * ((tm + tn) * k_bytes + tm * tn * mn_bytes) + 4 * tm * tn * 4
            if need <= MM_VMEM_BUDGET and (best is None or (tm * tn, tm) > (best[0] * best[1], best[0])):
                best = (tm, tn)
    assert best is not None, (M, N, k_bytes, mn_bytes)
    return best


def _mm(a, b, *, mode, name, extras=(), epi=None, out_dtypes=(F32,), comm=None):
    a_list = list(a) if isinstance(a, (list, tuple)) else [a]
    b_list = list(b) if isinstance(b, (list, tuple)) else [b]
    if mode == "nn":
        M, N = a_list[0].shape[0], b_list[0].shape[1]
    elif mode == "nt":
        M, N = a_list[0].shape[0], b_list[0].shape[0]
    else:
        M, N = a_list[0].shape[1], b_list[0].shape[1]
    k_bytes = sum((av.shape[0] if mode == "tn" else av.shape[1]) * av.dtype.itemsize for av in a_list)
    mn_bytes = sum(e.dtype.itemsize for e in extras) + sum(jnp.dtype(d).itemsize for d in out_dtypes)
    tm, tn = _mm_tiles(min(M, 2048), min(N, 1024), k_bytes, mn_bytes) if M % 128 == 0 and N % 128 == 0 else (M, N)
    assert M % tm == 0 and N % tn == 0
    a_specs, b_specs = [], []
    for av, bv in zip(a_list, b_list):
        K = av.shape[0] if mode == "tn" else av.shape[1]
        a_specs.append(pl.BlockSpec((K, tm), lambda i, j: (0, i)) if mode == "tn" else pl.BlockSpec((tm, K), lambda i, j: (i, 0)))
        b_specs.append(pl.BlockSpec((tn, K), lambda i, j: (j, 0)) if mode == "nt" else pl.BlockSpec((K, tn), lambda i, j: (0, j)))
    mn_spec = pl.BlockSpec((tm, tn), lambda i, j: (i, j))
    n_p, n_ex = len(a_list), len(extras)
    dims = _DIMS[mode]

    def body(*refs):
        acc = _dot(refs[0][...], refs[n_p][...], dims)
        for p in range(1, n_p):
            acc = acc + _dot(refs[p][...], refs[n_p + p][...], dims)
        rest = refs[2 * n_p:]
        res = (acc,) if epi is None else epi(acc, *[r[...] for r in rest[:n_ex]])
        for o_ref, r in zip(rest[n_ex:], res):
            o_ref[...] = r.astype(o_ref.dtype)

    outs, carried = _call(
        body, grid=(M // tm, N // tn), in_specs=a_specs + b_specs + [mn_spec] * n_ex,
        out_specs=[mn_spec] * len(out_dtypes), out_shape=[_sds((M, N), d) for d in out_dtypes],
        args=a_list + b_list + list(extras), name=name, sem=("parallel", "parallel"), comm=comm)
    res = outs[0] if len(outs) == 1 else outs
    return res if comm is None else (res, carried)


def _epi_add(acc, r):
    return (acc + r,)


def _epi_add2(acc, r):
    s = acc + r
    return (s, s)


def _epi_relu2(acc):
    p = jnp.maximum(acc, 0.0)
    return (p * p,)


def _epi_relu2_bwd(acc, r):
    return (acc * (2.0 * jnp.sqrt(r.astype(F32))),)


def _row(tr, n):
    return pl.BlockSpec((tr, n), lambda i: (i, 0))


def _vec(n):
    return pl.BlockSpec((1, n), lambda i: (0, 0))


def _rms_fwd(x, w, name):
    T, D = x.shape
    tr = min(256, T)

    def body(x_ref, w_ref, o_ref):
        xv = x_ref[...]
        r = lax.rsqrt(jnp.mean(xv * xv, axis=-1, keepdims=True) + EPS)
        o_ref[...] = (xv * r * w_ref[...]).astype(BF16)

    return pl.pallas_call(body, grid=(T // tr,), in_specs=[_row(tr, D), _vec(D)], out_specs=_row(tr, D),
                          out_shape=_sds((T, D), BF16), name=name, compiler_params=_params("parallel"))(x, w)


def _rms_bwd(x, w, dh, dres, name):
    T, D = x.shape
    tr = min(256, T)

    def body(x_ref, w_ref, dh_ref, dres_ref, dx_ref, dxb_ref, dw_ref):
        @pl.when(pl.program_id(0) == 0)
        def _():
            dw_ref[...] = jnp.zeros_like(dw_ref)

        xv = x_ref[...]
        r = lax.rsqrt(jnp.mean(xv * xv, axis=-1, keepdims=True) + EPS)
        xh = xv * r
        dh_v = dh_ref[...]
        dw_ref[...] += jnp.sum(dh_v * xh, axis=0, keepdims=True)
        dxh = dh_v * w_ref[...]
        dx = r * (dxh - xh * jnp.mean(dxh * xh, axis=-1, keepdims=True)) + dres_ref[...]
        dx_ref[...] = dx
        dxb_ref[...] = dx.astype(BF16)

    return pl.pallas_call(
        body, grid=(T // tr,), in_specs=[_row(tr, D), _vec(D), _row(tr, D), _row(tr, D)],
        out_specs=[_row(tr, D), _row(tr, D), _vec(D)],
        out_shape=[_sds((T, D), F32), _sds((T, D), BF16), _sds((1, D), F32)],
        name=name, compiler_params=_params("arbitrary"))(x, w, dh, dres)


def _final(x2, w, tgt, name):
    T, D = x2.shape
    tr = min(256, T)

    def body(x_ref, w_ref, t_ref, dx_ref, dxb_ref, dw_ref, loss_ref):
        @pl.when(pl.program_id(0) == 0)
        def _():
            dw_ref[...] = jnp.zeros_like(dw_ref)
            loss_ref[...] = jnp.zeros_like(loss_ref)

        xv = x_ref[...]
        wv = w_ref[...]
        r = lax.rsqrt(jnp.mean(xv * xv, axis=-1, keepdims=True) + EPS)
        xh = xv * r
        err = xh * wv - t_ref[...]
        part = jnp.sum(jnp.sum(err * err, axis=1, keepdims=True), axis=0, keepdims=True) * (0.5 / D)
        loss_ref[...] += jnp.broadcast_to(part, loss_ref.shape)
        dy = err * (1.0 / D)
        dw_ref[...] += jnp.sum(dy * xh, axis=0, keepdims=True)
        dxh = dy * wv
        dx = r * (dxh - xh * jnp.mean(dxh * xh, axis=-1, keepdims=True))
        dx_ref[...] = dx
        dxb_ref[...] = dx.astype(BF16)

    return pl.pallas_call(
        body, grid=(T // tr,), in_specs=[_row(tr, D), _vec(D), _row(tr, D)],
        out_specs=[_row(tr, D), _row(tr, D), _vec(D), _vec(LANES)],
        out_shape=[_sds((T, D), F32), _sds((T, D), BF16), _sds((1, D), F32), _sds((1, LANES), F32)],
        name=name, compiler_params=_params("arbitrary"))(x2, w, tgt)


def _silu_parts(z):
    s = jax.nn.sigmoid(z)
    return z * s, s * (1.0 + z * (1.0 - s))


def _gnorm_fwd(y, z, w, name, comm=None):
    T, N = y.shape
    tr = min(256, T)

    def body(y_ref, z_ref, w_ref, o_ref):
        for g in range(N // GROUP_W):
            sl = slice(g * GROUP_W, (g + 1) * GROUP_W)
            silu, _ = _silu_parts(z_ref[:, sl])
            yz = y_ref[:, sl] * silu
            r = lax.rsqrt(jnp.mean(yz * yz, axis=-1, keepdims=True) + EPS)
            o_ref[:, sl] = (yz * r * w_ref[:, sl]).astype(BF16)

    outs, carried = _call(body, grid=(T // tr,), in_specs=[_row(tr, N), _row(tr, N), _vec(N)], out_specs=[_row(tr, N)],
                          out_shape=[_sds((T, N), BF16)], args=[y, z, w], name=name, sem=("parallel",), comm=comm)
    return outs[0] if comm is None else (outs[0], carried)


def _gnorm_bwd(y, z, w, dyb, name):
    T, N = y.shape
    tr = min(256, T)

    def body(y_ref, z_ref, w_ref, d_ref, dy_ref, dz_ref, dw_ref):
        @pl.when(pl.program_id(0) == 0)
        def _():
            dw_ref[...] = jnp.zeros_like(dw_ref)

        for g in range(N // GROUP_W):
            sl = slice(g * GROUP_W, (g + 1) * GROUP_W)
            yv = y_ref[:, sl]
            silu, dsilu = _silu_parts(z_ref[:, sl])
            yz = yv * silu
            r = lax.rsqrt(jnp.mean(yz * yz, axis=-1, keepdims=True) + EPS)
            yzh = yz * r
            d = d_ref[:, sl]
            dw_ref[:, sl] += jnp.sum(d * yzh, axis=0, keepdims=True)
            dyzh = d * w_ref[:, sl]
            dyz = r * (dyzh - yzh * jnp.mean(dyzh * yzh, axis=-1, keepdims=True))
            dy_ref[:, sl] = dyz * silu
            dz_ref[:, sl] = (dyz * yv * dsilu).astype(BF16)

    return pl.pallas_call(
        body, grid=(T // tr,), in_specs=[_row(tr, N), _row(tr, N), _vec(N), _row(tr, N)],
        out_specs=[_row(tr, N), _row(tr, N), _vec(N)],
        out_shape=[_sds((T, N), F32), _sds((T, N), BF16), _sds((1, N), F32)],
        name=name, compiler_params=_params("arbitrary"))(y, z, w, dyb)


def _merge_fwd(gate_raw, b_gate, br_a, br_b, name):
    T, D = br_a.shape
    tr = min(256, T)

    def body(g_ref, bg_ref, a_ref, b_ref, o_ref):
        g = jax.nn.sigmoid(g_ref[...] + bg_ref[...])
        o_ref[...] = (g[:, :D] * a_ref[...] + g[:, D:] * b_ref[...]).astype(BF16)

    return pl.pallas_call(body, grid=(T // tr,), in_specs=[_row(tr, 2 * D), _vec(2 * D), _row(tr, D), _row(tr, D)],
                          out_specs=_row(tr, D), out_shape=_sds((T, D), BF16), name=name,
                          compiler_params=_params("parallel"))(gate_raw, b_gate, br_a, br_b)


def _merge_bwd(dmerged, gate_raw, b_gate, br_a, br_b, name):
    T, D = br_a.shape
    tr = min(256, T)

    def body(d_ref, g_ref, bg_ref, a_ref, b_ref, da_ref, db_ref, dg_ref, dbg_ref):
        @pl.when(pl.program_id(0) == 0)
        def _():
            dbg_ref[...] = jnp.zeros_like(dbg_ref)

        g = jax.nn.sigmoid(g_ref[...] + bg_ref[...])
        d = d_ref[...]
        da_ref[...] = (d * g[:, :D]).astype(BF16)
        db_ref[...] = (d * g[:, D:]).astype(BF16)
        dg = jnp.concatenate([d * a_ref[...], d * b_ref[...]], axis=1) * g * (1.0 - g)
        dg_ref[...] = dg.astype(BF16)
        dbg_ref[...] += jnp.sum(dg, axis=0, keepdims=True)

    return pl.pallas_call(
        body, grid=(T // tr,), in_specs=[_row(tr, D), _row(tr, 2 * D), _vec(2 * D), _row(tr, D), _row(tr, D)],
        out_specs=[_row(tr, D), _row(tr, D), _row(tr, 2 * D), _vec(2 * D)],
        out_shape=[_sds((T, D), BF16), _sds((T, D), BF16), _sds((T, 2 * D), BF16), _sds((1, 2 * D), F32)],
        name=name, compiler_params=_params("arbitrary"))(dmerged, gate_raw, b_gate, br_a, br_b)


def _shift_down(u, s):
    if s == 0:
        return u
    row = lax.broadcasted_iota(jnp.int32, u.shape, 0)
    return jnp.where(row >= s, pltpu.roll(u, s, 0), 0.0)


def _shift_up(u, s):
    if s == 0:
        return u
    n = u.shape[0]
    row = lax.broadcasted_iota(jnp.int32, u.shape, 0)
    return jnp.where(row < n - s, pltpu.roll(u, n - s, 0), 0.0)


def _conv(u, w_ref, K):
    acc = u * w_ref[K - 1:K, :]
    for k in range(K - 1):
        acc = acc + _shift_down(u, K - 1 - k) * w_ref[k:k + 1, :]
    return acc


def _conv_bwd(u, dc, w_ref, dw_ref, K):
    du = dc * w_ref[K - 1:K, :]
    dw_ref[K - 1:K, :] = jnp.sum(dc * u, axis=0, keepdims=True)
    for k in range(K - 1):
        s = K - 1 - k
        dw_ref[k:k + 1, :] = jnp.sum(dc * _shift_down(u, s), axis=0, keepdims=True)
        du = du + _shift_up(dc, s) * w_ref[k:k + 1, :]
    return du


CB_W = 256


def _col(T, j0=0):
    return pl.BlockSpec((T, CB_W), lambda j: (0, j + j0))


def _sc_fwd(psc, w, name):
    T, D = psc.shape[0], psc.shape[1] // 3
    nb = D // CB_W

    def body(b_ref, c_ref, x_ref, w_ref, o_ref):
        o_ref[...] = (b_ref[...] * _conv(c_ref[...] * x_ref[...], w_ref, SC_K)).astype(BF16)

    return pl.pallas_call(
        body, grid=(nb,), in_specs=[_col(T), _col(T, nb), _col(T, 2 * nb), pl.BlockSpec((SC_K, CB_W), lambda j: (0, j))],
        out_specs=_col(T), out_shape=_sds((T, D), BF16), name=name, compiler_params=_params("parallel"))(psc, psc, psc, w)


def _sc_bwd(psc, w, dya, name):
    T, D = psc.shape[0], psc.shape[1] // 3
    nb = D // CB_W

    def body(b_ref, c_ref, x_ref, w_ref, d_ref, db_ref, dc_ref, dx_ref, dw_ref):
        cv, xv, d = c_ref[...], x_ref[...], d_ref[...]
        u = cv * xv
        db_ref[...] = (d * _conv(u, w_ref, SC_K)).astype(BF16)
        du = _conv_bwd(u, d * b_ref[...], w_ref, dw_ref, SC_K)
        dc_ref[...] = (du * xv).astype(BF16)
        dx_ref[...] = (du * cv).astype(BF16)

    wspec = pl.BlockSpec((SC_K, CB_W), lambda j: (0, j))
    return pl.pallas_call(
        body, grid=(nb,), in_specs=[_col(T), _col(T, nb), _col(T, 2 * nb), wspec, _col(T)],
        out_specs=[_col(T), _col(T), _col(T), wspec],
        out_shape=[_sds((T, D), BF16)] * 3 + [_sds((SC_K, D), F32)],
        name=name, compiler_params=_params("parallel"))(psc, psc, psc, w, dya)


def _ssm_conv_fwd(u, w, b, name, comm=None):
    T, N = u.shape

    def body(u_ref, w_ref, b_ref, o_ref):
        c = _conv(u_ref[...], w_ref, SSM_K) + b_ref[...]
        o_ref[...] = c * jax.nn.sigmoid(c)

    outs, carried = _call(
        body, grid=(N // CB_W,), in_specs=[_col(T), pl.BlockSpec((SSM_K, CB_W), lambda j: (0, j)), pl.BlockSpec((1, CB_W), lambda j: (0, j))],
        out_specs=[_col(T)], out_shape=[_sds((T, N), F32)], args=[u, w, b], name=name, sem=("parallel",), comm=comm)
    return outs[0] if comm is None else (outs[0], carried)


def _ssm_conv_bwd(u, w, b, dxs, dB, dC, name, comm=None):
    T, N = u.shape
    n_x, n_b = dxs.shape[1] // CB_W, dB.shape[1] // CB_W

    def body(u_ref, w_ref, b_ref, dx_ref, db_ref, dc_ref, du_ref, dw_ref, dbias_ref):
        j = pl.program_id(0)
        uv = u_ref[...]
        c = _conv(uv, w_ref, SSM_K) + b_ref[...]
        _, dsilu = _silu_parts(c)
        d = jnp.where(j < n_x, dx_ref[...], jnp.where(j < n_x + n_b, db_ref[...], dc_ref[...])) * dsilu
        dbias_ref[...] = jnp.sum(d, axis=0, keepdims=True)
        du_ref[...] = _conv_bwd(uv, d, w_ref, dw_ref, SSM_K).astype(BF16)

    wspec = pl.BlockSpec((SSM_K, CB_W), lambda j: (0, j))
    bspec = pl.BlockSpec((1, CB_W), lambda j: (0, j))
    outs, carried = _call(
        body, grid=(N // CB_W,),
        in_specs=[_col(T), wspec, bspec,
                  pl.BlockSpec((T, CB_W), lambda j: (0, jnp.minimum(j, n_x - 1))),
                  pl.BlockSpec((T, CB_W), lambda j: (0, jnp.clip(j - n_x, 0, n_b - 1))),
                  pl.BlockSpec((T, CB_W), lambda j: (0, jnp.clip(j - n_x - n_b, 0, n_b - 1)))],
        out_specs=[_col(T), wspec, bspec],
        out_shape=[_sds((T, N), BF16), _sds((SSM_K, N), F32), _sds((1, N), F32)],
        args=[u, w, b, dxs, dB, dC], name=name, sem=("parallel",), comm=comm)
    return outs if comm is None else (outs, carried)


def _split3(v):
    hi = v.astype(BF16)
    r = v - hi.astype(F32)
    mid = r.astype(BF16)
    lo = (r - mid.astype(F32)).astype(BF16)
    return hi, mid, lo


def _head_expand(n_lanes):
    h = lax.broadcasted_iota(jnp.int32, (LANES, n_lanes), 0)
    l = lax.broadcasted_iota(jnp.int32, (LANES, n_lanes), 1)
    return (jnp.right_shift(l, HEADDIM.bit_length() - 1) == h).astype(BF16)


def _softplus(v):
    return jnp.maximum(v, 0.0) + jnp.log1p(jnp.exp(-jnp.abs(v)))


def _ssd_prep(dt_raw, dt_bias, a_log, n_inner, name):
    T = dt_raw.shape[0]

    def body(r_ref, b_ref, al_ref, dt_ref, cs_ref):
        dt = _softplus(r_ref[...] + b_ref[...])
        a = dt * (-jnp.exp(al_ref[...]))
        i = lax.broadcasted_iota(jnp.int32, (CHUNK, CHUNK), 0)
        j = lax.broadcasted_iota(jnp.int32, (CHUNK, CHUNK), 1)
        tri = (j <= i).astype(BF16)
        cs = sum(_dot(tri, p) for p in _split3(a))
        ex = _head_expand(n_inner)
        dt_ref[...] = sum(_dot(p, ex) for p in _split3(dt))
        cs_ref[...] = sum(_dot(p, ex) for p in _split3(cs))

    blk = pl.BlockSpec((CHUNK, LANES), lambda c: (c, 0))
    out = pl.BlockSpec((CHUNK, n_inner), lambda c: (c, 0))
    return pl.pallas_call(body, grid=(T // CHUNK,), in_specs=[blk, _vec(LANES), _vec(LANES)], out_specs=[out, out],
                          out_shape=[_sds((T, n_inner), F32)] * 2, name=name, compiler_params=_params("parallel"))(dt_raw, dt_bias, a_log)


def _pair_terms(cs_p):
    lane = lax.broadcasted_iota(jnp.int32, (CHUNK, CHUNK), 1)
    sub = lax.broadcasted_iota(jnp.int32, (CHUNK, CHUNK), 0)
    csT = cs_p.T
    Ls = []
    for k in range(2):
        col = jnp.sum(jnp.where(lane == k * HEADDIM, cs_p, 0.0), axis=1, keepdims=True)
        rowv = csT[k * HEADDIM:k * HEADDIM + 1, :]
        Ls.append(jnp.exp(jnp.where(sub >= lane, col - rowv, -jnp.inf)))
    return Ls, jnp.exp(csT[:, CHUNK - 1:CHUNK])


def _block_diag(xp):
    lane = lax.broadcasted_iota(jnp.int32, xp.shape, 1)
    return jnp.concatenate([jnp.where(lane < HEADDIM, xp, 0.0), jnp.where(lane >= HEADDIM, xp, 0.0)], axis=0)


SSD_GROUPS_PER_STEP = 8


def _ssd_specs(T, n_inner):
    nc, gs = T // CHUNK, SSD_GROUPS_PER_STEP
    bo, co = n_inner // (gs * NSTATE), (n_inner + NGROUPS * NSTATE) // (gs * NSTATE)
    assert NGROUPS % gs == 0 and n_inner % (gs * NSTATE) == 0 and (NGROUPS * NSTATE) % (gs * NSTATE) == 0
    g_blk = lambda f: pl.BlockSpec((CHUNK, gs * GROUP_W), lambda c, s: (f(c), s))
    b_blk = lambda f: pl.BlockSpec((CHUNK, gs * NSTATE), lambda c, s: (f(c), bo + s))
    c_blk = lambda f: pl.BlockSpec((CHUNK, gs * NSTATE), lambda c, s: (f(c), co + s))
    return nc, g_blk, b_blk, c_blk


def _ssd_fwd(xbc, dt_e, cs_e, d_e, name, comm=None):
    T = xbc.shape[0]
    n_inner = dt_e.shape[1]
    nc, g_blk, b_blk, c_blk = _ssd_specs(T, n_inner)
    ident = lambda c: c

    gs = SSD_GROUPS_PER_STEP

    def body(xs_ref, b_ref, c_ref, dt_ref, cs_ref, d_ref, y_ref, p_ref, st):
        c, s = pl.program_id(0), pl.program_id(1)

        @pl.when(c == 0)
        def _():
            for gi in range(gs):
                st[s * gs + gi] = jnp.zeros((GROUP_W, NSTATE), F32)

        for gi in range(gs):
            g = s * gs + gi
            gw, gn = slice(gi * GROUP_W, (gi + 1) * GROUP_W), slice(gi * NSTATE, (gi + 1) * NSTATE)
            P = st[g]
            p_ref[0, gi] = P
            xs, dt, cs = xs_ref[:, gw], dt_ref[:, gw], cs_ref[:, gw]
            Bf, Cf = b_ref[:, gn], c_ref[:, gn]
            CBm = _dot3(Cf, Bf, NT)
            X = xs * dt
            decay = jnp.exp(cs[CHUNK - 1:CHUNK, :] - cs)
            y_off = _dot3(Cf, P, NT) * jnp.exp(cs)
            ys, ecl = [], []
            for pr in range(2):
                sl = slice(pr * LANES, (pr + 1) * LANES)
                Ls, e_last = _pair_terms(cs[:, sl])
                ecl.append(e_last)
                Mcat = jnp.concatenate([CBm * L for L in Ls], axis=1)
                ys.append(_dot3(Mcat, _block_diag(X[:, sl])))
            y_ref[:, gw] = jnp.concatenate(ys, axis=1) + y_off + xs * d_ref[:, gw]
            S = _dot3(X * decay, Bf, TN)
            st[g] = P * jnp.concatenate(ecl, axis=0) + S

    p_blk = pl.BlockSpec((1, gs, GROUP_W, NSTATE), lambda c, s: (c, s, 0, 0))
    outs, carried = _call(
        body, grid=(nc, NGROUPS // gs),
        in_specs=[g_blk(ident), b_blk(ident), c_blk(ident), g_blk(ident), g_blk(ident), pl.BlockSpec((1, gs * GROUP_W), lambda c, s: (0, s))],
        out_specs=[g_blk(ident), p_blk],
        out_shape=[_sds((T, n_inner), F32), _sds((nc, NGROUPS, GROUP_W, NSTATE), F32)],
        scratch=[pltpu.VMEM((NGROUPS, GROUP_W, NSTATE), F32)],
        args=[xbc, xbc, xbc, dt_e, cs_e, d_e], name=name, sem=("arbitrary", "arbitrary"), comm=comm)
    return outs if comm is None else (outs, carried)


def _ssd_bwd(xbc, dt_e, cs_e, d_e, states, dy, name, comm=None):
    T = xbc.shape[0]
    n_inner = dt_e.shape[1]
    nc, g_blk, b_blk, c_blk = _ssd_specs(T, n_inner)
    rev = lambda c: nc - 1 - c

    gs = SSD_GROUPS_PER_STEP

    def body(xs_ref, b_ref, c_ref, dt_ref, cs_ref, d_ref, p_ref, pn_ref, dy_ref,
             dxs_ref, db_ref, dc_ref, ddt_ref, dcs_ref, dd_ref, dst):
        cc, s = pl.program_id(0), pl.program_id(1)

        @pl.when(cc == 0)
        def _():
            for gi in range(gs):
                dst[s * gs + gi] = jnp.zeros((GROUP_W, NSTATE), F32)

        for gi in range(gs):
            one_group(s * gs + gi, gi, xs_ref, b_ref, c_ref, dt_ref, cs_ref, d_ref, p_ref, pn_ref, dy_ref,
                      dxs_ref, db_ref, dc_ref, ddt_ref, dcs_ref, dd_ref, dst)

    def one_group(g, gi, xs_ref, b_ref, c_ref, dt_ref, cs_ref, d_ref, p_ref, pn_ref, dy_ref,
                  dxs_ref, db_ref, dc_ref, ddt_ref, dcs_ref, dd_ref, dst):
        gw, gn = slice(gi * GROUP_W, (gi + 1) * GROUP_W), slice(gi * NSTATE, (gi + 1) * NSTATE)
        dS = dst[g]
        P, Pn = p_ref[0, gi], pn_ref[0, gi]
        xs, dt, cs, dY = xs_ref[:, gw], dt_ref[:, gw], cs_ref[:, gw], dy_ref[:, gw]
        Bf, Cf = b_ref[:, gn], c_ref[:, gn]
        Bb, Cb = Bf.astype(BF16), Cf.astype(BF16)
        X = xs * dt
        ecs = jnp.exp(cs)
        decay = jnp.exp(cs[CHUNK - 1:CHUNK, :] - cs)
        CBm = _dot3(Cf, Bf, NT)
        dYe = dY * ecs
        dP_off = _dot3(dYe, Cf, TN)
        dC = _dot(dYe.astype(BF16), P.astype(BF16))
        dcs = dYe * _dot3(Cf, P, NT)
        Xd = X * decay
        dB = _dot(Xd.astype(BF16), dS.astype(BF16))
        E = _dot3(Bf, dS, NT)
        dX = E * decay
        dcs = dcs - E * Xd
        R = _dot3(jnp.ones((8, NSTATE), F32), dS * Pn, NT)
        sub_g = lax.broadcasted_iota(jnp.int32, (CHUNK, GROUP_W), 0)
        dcs = dcs + jnp.where(sub_g == CHUNK - 1, R[0:1, :], 0.0)
        lane = lax.broadcasted_iota(jnp.int32, (CHUNK, CHUNK), 1)
        sub = lax.broadcasted_iota(jnp.int32, (CHUNK, CHUNK), 0)
        dCB = jnp.zeros((CHUNK, CHUNK), F32)
        dXs, dcss, ecl = [], [], []
        for pr in range(2):
            sl = slice(pr * LANES, (pr + 1) * LANES)
            Ls, e_last = _pair_terms(cs[:, sl])
            ecl.append(e_last)
            dYp = dY[:, sl]
            dMcat = _dot3(dYp, _block_diag(X[:, sl]), NT)
            Mcat = jnp.concatenate([CBm * L for L in Ls], axis=1)
            dXt = _dot3(Mcat, dYp, TN)
            dXs.append(jnp.where(lane < HEADDIM, dXt[:CHUNK], dXt[CHUNK:]))
            colacc = jnp.zeros((CHUNK, CHUNK), F32)
            rowacc = jnp.zeros((CHUNK, CHUNK), F32)
            for k in range(2):
                dG = dMcat[:, k * CHUNK:(k + 1) * CHUNK] * Ls[k]
                dCB = dCB + dG
                Q = dG * CBm
                colacc = colacc + jnp.where(lane == k * HEADDIM, jnp.sum(Q, axis=1, keepdims=True), 0.0)
                rowacc = rowacc + jnp.where(sub == k * HEADDIM, jnp.sum(Q, axis=0, keepdims=True), 0.0)
            dcss.append(colacc - rowacc.T)
        dX = dX + jnp.concatenate(dXs, axis=1)
        dcs = dcs + jnp.concatenate(dcss, axis=1)
        dCBb = dCB.astype(BF16)
        dc_ref[:, gn] = dC + _dot(dCBb, Bb)
        db_ref[:, gn] = dB + _dot(dCBb, Cb, TN)
        dxs_ref[:, gw] = dX * dt + dY * d_ref[:, gw]
        ddt_ref[:, gw] = dX * xs
        dcs_ref[:, gw] = dcs
        dd_ref[0, :, gw] = jnp.sum(dY * xs, axis=0, keepdims=True)
        dst[g] = dS * jnp.concatenate(ecl, axis=0) + dP_off

    p_blk = pl.BlockSpec((1, gs, GROUP_W, NSTATE), lambda c, s: (nc - 1 - c, s, 0, 0))
    pn_blk = pl.BlockSpec((1, gs, GROUP_W, NSTATE), lambda c, s: (jnp.minimum(nc - c, nc - 1), s, 0, 0))
    st_blk = pl.BlockSpec((CHUNK, gs * NSTATE), lambda c, s: (nc - 1 - c, s))
    outs, carried = _call(
        body, grid=(nc, NGROUPS // gs),
        in_specs=[g_blk(rev), b_blk(rev), c_blk(rev), g_blk(rev), g_blk(rev), pl.BlockSpec((1, gs * GROUP_W), lambda c, s: (0, s)),
                  p_blk, pn_blk, g_blk(rev)],
        out_specs=[g_blk(rev), st_blk, st_blk, g_blk(rev), g_blk(rev), pl.BlockSpec((1, 1, gs * GROUP_W), lambda c, s: (nc - 1 - c, 0, s))],
        out_shape=[_sds((T, n_inner), F32), _sds((T, NGROUPS * NSTATE), F32), _sds((T, NGROUPS * NSTATE), F32),
                   _sds((T, n_inner), F32), _sds((T, n_inner), F32), _sds((nc, 1, n_inner), F32)],
        scratch=[pltpu.VMEM((NGROUPS, GROUP_W, NSTATE), F32)],
        args=[xbc, xbc, xbc, dt_e, cs_e, d_e, states, states, dy], name=name, sem=("arbitrary", "arbitrary"), comm=comm)
    return outs if comm is None else (outs, carried)


def _ssd_post(ddt_e, dcs_e, dd_p, dt_raw, dt_bias, a_log, n_heads, name):
    T, n_inner = ddt_e.shape

    def body(ddt_ref, dcs_ref, dd_ref, r_ref, b_ref, al_ref, draw_ref, dbias_ref, dal_ref, ddsk_ref):
        @pl.when(pl.program_id(0) == 0)
        def _():
            dbias_ref[...] = jnp.zeros_like(dbias_ref)
            dal_ref[...] = jnp.zeros_like(dal_ref)
            ddsk_ref[...] = jnp.zeros_like(ddsk_ref)

        ex = _head_expand(n_inner)
        red = lambda v: sum(_dot(p, ex, NT) for p in _split3(v))
        raw = r_ref[...] + b_ref[...]
        dt = _softplus(raw)
        A = -jnp.exp(al_ref[...])
        i = lax.broadcasted_iota(jnp.int32, (CHUNK, CHUNK), 0)
        j = lax.broadcasted_iota(jnp.int32, (CHUNK, CHUNK), 1)
        upper = (j >= i).astype(BF16)
        da = sum(_dot(upper, p) for p in _split3(red(dcs_ref[...])))
        ddt = red(ddt_ref[...]) + da * A
        lane = lax.broadcasted_iota(jnp.int32, (CHUNK, LANES), 1)
        draw = jnp.where(lane < n_heads, ddt * jax.nn.sigmoid(raw), 0.0)
        draw_ref[...] = draw.astype(BF16)
        dbias_ref[...] += jnp.sum(draw, axis=0, keepdims=True)
        dal_ref[...] += jnp.sum(da * dt, axis=0, keepdims=True) * A
        ddsk_ref[...] += red(jnp.broadcast_to(dd_ref[0], (8, n_inner)))[0:1, :]

    wide = pl.BlockSpec((CHUNK, n_inner), lambda c: (c, 0))
    blk = pl.BlockSpec((CHUNK, LANES), lambda c: (c, 0))
    return pl.pallas_call(
        body, grid=(T // CHUNK,),
        in_specs=[wide, wide, pl.BlockSpec((1, 1, n_inner), lambda c: (c, 0, 0)), blk, _vec(LANES), _vec(LANES)],
        out_specs=[blk, _vec(LANES), _vec(LANES), _vec(LANES)],
        out_shape=[_sds((T, LANES), BF16)] + [_sds((1, LANES), F32)] * 3,
        name=name, compiler_params=_params("arbitrary"))(ddt_e, dcs_e, dd_p, dt_raw, dt_bias, a_log)


def _row2(v):
    return v.reshape(1, -1).astype(F32)


def _pad_lanes(v):
    return jnp.pad(_row2(v), ((0, 0), (0, LANES - v.shape[-1])))


class _NoExchange:
    def __init__(self, W):
        self.W, self.grads = W, {}

    def weight(self, k):
        return self.W[k]

    def carry(self, name):
        return None

    def carried(self, name, outs):
        pass

    def grad(self, k, g):
        self.grads[k] = g

    def after_in_grad(self, pieces):
        return pieces


def _local_step(x, tgt, S, small):
    T, D = x.shape

    def mm(a, b, *, name, **kw):
        comm = S.carry(name)
        if comm is None:
            return _mm(a, b, name=name, **kw)
        res, outs = _mm(a, b, name=name, comm=comm, **kw)
        S.carried(name, outs)
        return res

    def carrying(fn, *args, name):
        comm = S.carry(name)
        if comm is None:
            return fn(*args, name)
        res, outs = fn(*args, name, comm=comm)
        S.carried(name, outs)
        return res

    n_inner = 2 * D
    n_heads = n_inner // HEADDIM
    norm_mix, norm_mlp, norm_final = _row2(small["norm_mix"]), _row2(small["norm_mlp"]), _row2(small["norm_final"])
    b_gate, ssm_b, ssm_norm_w = _row2(small["b_gate"]), _row2(small["ssm_conv_b"]), _row2(small["ssm_norm_w"])
    dt_bias, a_log = _pad_lanes(small["dt_bias"]), _pad_lanes(small["A_log"])
    d_e = jnp.repeat(small["D_skip"].astype(F32), HEADDIM).reshape(1, n_inner)
    sc_w, ssm_w = small["sc_conv_w"], small["ssm_conv_w"]

    hb = _rms_fwd(x, norm_mix, "rms_mix")
    p_xbc = mm(hb, S.weight("xbc"), mode="nn", name="proj_xbc")
    p_dt = mm(hb, S.weight("dt"), mode="nn", name="proj_dt")
    p_z = mm(hb, S.weight("z"), mode="nn", name="proj_z")
    p_sc = mm(hb, S.weight("sc"), mode="nn", name="proj_sc")
    p_gate = mm(hb, S.weight("gate"), mode="nn", name="proj_gate")
    xbc = carrying(_ssm_conv_fwd, p_xbc, ssm_w, ssm_b, name="ssm_conv_fwd")
    dt_e, cs_e = _ssd_prep(p_dt, dt_bias, a_log, n_inner, "ssd_prep")
    y, states = carrying(_ssd_fwd, xbc, dt_e, cs_e, d_e, name="ssd_fwd")
    yb = carrying(_gnorm_fwd, y, p_z, ssm_norm_w, name="gnorm_fwd")
    ya = _sc_fwd(p_sc, sc_w, "sc_fwd")
    br_a = mm(ya, S.weight("bsc"), mode="nn", name="branch_sc")
    br_b = mm(yb, S.weight("bssm"), mode="nn", name="branch_ssm")
    merged = _merge_fwd(p_gate, b_gate, br_a, br_b, "merge_fwd")
    x1 = mm(merged, S.weight("out"), mode="nn", name="out_proj", extras=(x,), epi=_epi_add)
    h2 = _rms_fwd(x1, norm_mlp, "rms_mlp")
    r_act = mm(h2, S.weight("w1"), mode="nn", name="mlp_up", epi=_epi_relu2, out_dtypes=(BF16,))
    x2 = mm(r_act, S.weight("w2"), mode="nn", name="mlp_down", extras=(x1,), epi=_epi_add)
    dx2, dx2b, g_norm_final, loss_row = _final(x2, norm_final, tgt, "final")

    S.grad("w2", mm(r_act, dx2b, mode="tn", name="mlp_down_dw", out_dtypes=(BF16,)))
    da = mm(dx2b, S.weight("w2"), mode="nt", name="mlp_down_dx", extras=(r_act,), epi=_epi_relu2_bwd, out_dtypes=(BF16,))
    S.grad("w1", mm(h2, da, mode="tn", name="mlp_up_dw", out_dtypes=(BF16,)))
    dh2 = mm(da, S.weight("w1"), mode="nt", name="mlp_up_dx")
    dx1, dx1b, g_norm_mlp = _rms_bwd(x1, norm_mlp, dh2, dx2, "rms_mlp_bwd")
    S.grad("out", mm(merged, dx1b, mode="tn", name="out_proj_dw", out_dtypes=(BF16,)))
    dmerged = mm(dx1b, S.weight("out"), mode="nt", name="out_proj_dx")
    dbr_a, dbr_b, d_gate, g_b_gate = _merge_bwd(dmerged, p_gate, b_gate, br_a, br_b, "merge_bwd")
    S.grad("bssm", mm(yb, dbr_b, mode="tn", name="branch_ssm_dw", out_dtypes=(BF16,)))
    S.grad("bsc", mm(ya, dbr_a, mode="tn", name="branch_sc_dw", out_dtypes=(BF16,)))
    dyb = mm(dbr_b, S.weight("bssm"), mode="nt", name="branch_ssm_dx")
    dya = mm(dbr_a, S.weight("bsc"), mode="nt", name="branch_sc_dx")
    dy, d_z, g_ssm_norm_w = _gnorm_bwd(y, p_z, ssm_norm_w, dyb, "gnorm_bwd")
    dxs, dB, dC, ddt_e, dcs_e, dd_p = carrying(_ssd_bwd, xbc, dt_e, cs_e, d_e, states, dy, name="ssd_bwd")
    d_dt, g_dt_bias, g_a_log, g_d_skip = _ssd_post(ddt_e, dcs_e, dd_p, p_dt, dt_bias, a_log, n_heads, "ssd_post")
    d_xbc, g_ssm_w, g_ssm_b = carrying(_ssm_conv_bwd, p_xbc, ssm_w, ssm_b, dxs, dB, dC, name="ssm_conv_bwd")
    d_scB, d_scC, d_scX, g_sc_w = _sc_bwd(p_sc, sc_w, dya, "sc_bwd")
    d_sc = jnp.concatenate([d_scB, d_scC, d_scX], axis=1)
    pieces = [("sc", d_sc), ("z", d_z), ("xbc", d_xbc), ("dt", d_dt), ("gate", d_gate)]
    S.grad("win", {k: mm(hb, d, mode="tn", name="proj_dw_" + k, out_dtypes=(BF16,)) for k, d in pieces})
    pieces = S.after_in_grad(pieces)
    dh = mm([d for _, d in pieces], [S.weight(k) for k, _ in pieces], mode="nt", name="proj_dx")
    grad_x, _, g_norm_mix = _rms_bwd(x, norm_mix, dh, dx1, "rms_mix_bwd")

    g_small = dict(norm_mix=g_norm_mix, b_gate=g_b_gate, sc_conv_w=g_sc_w, ssm_conv_w=g_ssm_w, ssm_conv_b=g_ssm_b,
                   dt_bias=g_dt_bias, A_log=g_a_log, D_skip=g_d_skip, ssm_norm_w=g_ssm_norm_w, norm_mlp=g_norm_mlp,
                   norm_final=g_norm_final, loss=loss_row)
    return grad_x, g_small


class _Place:
    def __init__(self, k=0):
        x, y, c = lax.axis_index("x"), lax.axis_index("y"), lax.axis_index("c")
        self.x = 1 - x if k & 4 else x
        self.y = 1 - y if k & 2 else y
        self.c = 1 - c if k & 1 else c
        self.chip = 2 * self.x + self.y
        self.id = 2 * self.chip + self.c


ICI_PEERS = (2, 4, 6)
SIBLING = (1,)
ALL_PEERS = (1, 2, 3, 4, 5, 6, 7)


class _Comm:
    def __init__(self, arrs, out_shape, ks, src, dst, own=None, aliases=None):
        self.arrs, self.out_shape, self.ks = list(arrs), list(out_shape), tuple(ks)
        self.n = len(self.arrs)
        self.src, self.dst, self.own = src, dst, own
        self.aliases = aliases or {}
        dma = pltpu.SemaphoreType.DMA
        self.scratch = [dma((self.n, len(self.ks))), dma((self.n, len(self.ks))), dma((self.n,))]

    def _copies(self, ins, outs, sems, with_recvs):
        send_sems, recv_sems, local_sems = sems
        me = _Place()
        owns, sends, recvs = [], [], []
        for a in range(self.n):
            if self.own is not None:
                s, d = self.own(a, ins[a], outs[a], me)
                owns.append(pltpu.make_async_copy(s, d, local_sems.at[a]))
            for i, k in enumerate(self.ks):
                peer = _Place(k)
                for sender, lst in ((me, sends), (peer, recvs)) if with_recvs else ((me, sends),):
                    lst.append(pltpu.make_async_remote_copy(
                        src_ref=self.src(a, ins[a], me, peer), dst_ref=self.dst(a, outs[a], sender),
                        send_sem=send_sems.at[a, i], recv_sem=recv_sems.at[a, i],
                        device_id=(peer.x, peer.y, peer.c), device_id_type=MESH))
        return owns, sends, recvs

    def start(self, ins, outs, sems):
        owns, sends, _ = self._copies(ins, outs, sems, False)
        for cp in owns + sends:
            cp.start()

    def finish(self, ins, outs, sems):
        owns, sends, recvs = self._copies(ins, outs, sems, True)
        for cp in recvs:
            cp.wait_recv()
        for cp in sends:
            cp.wait_send()
        for cp in owns:
            cp.wait()


def _run_comm(comm, name):
    n = comm.n

    def body(*refs):
        ins, outs, sems = refs[:n], refs[n:2 * n], refs[2 * n:]
        comm.start(ins, outs, sems)
        comm.finish(ins, outs, sems)

    return list(pl.pallas_call(body, in_specs=[ANY] * n, out_specs=[ANY] * n, out_shape=comm.out_shape, scratch_shapes=comm.scratch,
                               input_output_aliases=dict(comm.aliases), name=name)(*comm.arrs))


def _gather_ici(shards):
    return _Comm(shards, [_sds((4, 2) + s.shape, s.dtype) for s in shards], ICI_PEERS,
                 src=lambda a, i, me, p: i, dst=lambda a, o, s: o.at[s.chip, s.c], own=lambda a, i, o, me: (i, o.at[me.chip, me.c]))


def _gather_sibling(bufs):
    return _Comm(bufs, [_sds(b.shape, b.dtype) for b in bufs], SIBLING,
                 src=lambda a, i, me, p: i.at[:, me.c], dst=lambda a, o, s: o.at[:, s.c], aliases={a: a for a in range(len(bufs))})


def _scatter_sibling(parts):
    return _Comm(parts, [_sds((4,) + p.shape[2:], p.dtype) for p in parts], SIBLING,
                 src=lambda a, i, me, p: i.at[:, p.c], dst=lambda a, o, s: o)


def _scatter_ici(parts):
    return _Comm(parts, [_sds(p.shape, p.dtype) for p in parts], ICI_PEERS,
                 src=lambda a, i, me, p: i.at[p.chip], dst=lambda a, o, s: o.at[s.chip], own=lambda a, i, o, me: (i.at[me.chip], o.at[me.chip]))


HBM_SPEC = pl.BlockSpec(memory_space=pltpu.HBM)
SEM_SPEC = pl.BlockSpec(memory_space=pltpu.SEMAPHORE)
DATAFLOW = pltpu.SideEffectType.DATAFLOW_SIDE_EFFECTING


def _own_part(parts, name):
    n, R, C = parts.shape
    tr = R if R <= 256 else 256
    chip = (2 * lax.axis_index("x") + lax.axis_index("y")).astype(jnp.int32).reshape(1)

    def body(q_ref, p_ref, o_ref):
        o_ref[...] = p_ref[...]

    blk = pl.BlockSpec((1, tr, C), lambda i, q_ref: (q_ref[0], i, 0))
    spec = pltpu.PrefetchScalarGridSpec(num_scalar_prefetch=1, grid=(R // tr,), in_specs=[blk], out_specs=blk)
    return pl.pallas_call(body, grid_spec=spec, out_shape=_sds((n, R, C), parts.dtype), name=name,
                          compiler_params=_params("parallel"))(chip, parts)


def _scatter_ici_start(parts, land, name):
    def body(src_ref, land_ref, send_sems, recv_sems, src_thru, land_thru, token):
        me = _Place()
        for i, k in enumerate(ICI_PEERS):
            peer = _Place(k)
            pltpu.make_async_remote_copy(src_ref=src_ref.at[peer.chip], dst_ref=land_ref.at[me.chip], send_sem=send_sems.at[i],
                                         recv_sem=recv_sems.at[i], device_id=(peer.x, peer.y, peer.c), device_id_type=MESH).start()
        token[...] = jnp.zeros_like(token)

    dma = pltpu.SemaphoreType.DMA((len(ICI_PEERS),))
    return pl.pallas_call(
        body, name=name,
        out_shape=(dma, dma, pltpu.HBM(parts.shape, parts.dtype), pltpu.HBM(land.shape, land.dtype), _sds((8, LANES), F32)),
        in_specs=(HBM_SPEC, HBM_SPEC), out_specs=(SEM_SPEC, SEM_SPEC, HBM_SPEC, HBM_SPEC, pl.BlockSpec(memory_space=pltpu.VMEM)),
        input_output_aliases={0: 2, 1: 3}, compiler_params=pltpu.CompilerParams(has_side_effects=DATAFLOW),
    )(pltpu.with_memory_space_constraint(parts, pltpu.HBM), pltpu.with_memory_space_constraint(land, pltpu.HBM))


def _scatter_ici_wait(send_sems, recv_sems, src_thru, land_thru, after, name):
    n_after = len(after)

    def body(src_ref, land_ref, send_sems, recv_sems, *rest):
        for i, k in enumerate(ICI_PEERS):
            peer = _Place(k)
            cp = pltpu.make_async_remote_copy(src_ref=src_ref.at[peer.chip], dst_ref=land_ref.at[peer.chip], send_sem=send_sems.at[i],
                                              recv_sem=recv_sems.at[i], device_id=(peer.x, peer.y, peer.c), device_id_type=MESH)
            cp.wait_send()
            cp.wait_recv()

    return pl.pallas_call(
        body, name=name, out_shape=(pltpu.HBM(src_thru.shape, src_thru.dtype), pltpu.HBM(land_thru.shape, land_thru.dtype)),
        in_specs=(HBM_SPEC, HBM_SPEC, SEM_SPEC, SEM_SPEC) + (ANY,) * n_after, out_specs=(HBM_SPEC, HBM_SPEC),
        input_output_aliases={0: 0, 1: 1}, compiler_params=pltpu.CompilerParams(has_side_effects=DATAFLOW),
    )(src_thru, land_thru, send_sems, recv_sems, *after)[1]


def _gather_all(arrs):
    return _Comm(arrs, [_sds((N_DEV,) + a.shape, a.dtype) for a in arrs], ALL_PEERS,
                 src=lambda a, i, me, p: i, dst=lambda a, o, s: o.at[s.id], own=lambda a, i, o, me: (i, o.at[me.id]))


def _add_halves(parts, got, name):
    n, _, R, C = parts.shape
    tr = R if R <= 256 else 256
    assert R % tr == 0
    core = lax.axis_index("c").astype(jnp.int32).reshape(1)

    def body(c_ref, p_ref, g_ref, o_ref):
        o_ref[0] = (p_ref[0, 0].astype(F32) + g_ref[0].astype(F32)).astype(o_ref.dtype)

    spec = pltpu.PrefetchScalarGridSpec(
        num_scalar_prefetch=1, grid=(n, R // tr),
        in_specs=[pl.BlockSpec((1, 1, tr, C), lambda q, i, c_ref: (q, c_ref[0], i, 0)), pl.BlockSpec((1, tr, C), lambda q, i, c_ref: (q, i, 0))],
        out_specs=pl.BlockSpec((1, tr, C), lambda q, i, c_ref: (q, i, 0)))
    return pl.pallas_call(body, grid_spec=spec, out_shape=_sds((n, R, C), parts.dtype), name=name,
                          compiler_params=_params("parallel", "parallel"))(core, parts, got)


def _adam(w, m, v, gparts, name):
    R, C = w.shape
    n = gparts.shape[0]
    tr = R if R <= 256 else 128
    assert R % tr == 0
    c1 = 1.0 / (1.0 - ADAM_B1 ** ADAM_STEP)
    c2 = 1.0 / (1.0 - ADAM_B2 ** ADAM_STEP)

    def body(w_ref, m_ref, v_ref, g_ref, go_ref, d_ref, mo_ref, vo_ref):
        g = g_ref[0].astype(F32)
        for s in range(1, n):
            g = g + g_ref[s].astype(F32)
        mn = ADAM_B1 * m_ref[...] + (1.0 - ADAM_B1) * g
        vn = ADAM_B2 * v_ref[...] + (1.0 - ADAM_B2) * (g * g)
        go_ref[...] = g
        mo_ref[...] = mn
        vo_ref[...] = vn
        d_ref[...] = -ADAM_LR * ((mn * c1) / (jnp.sqrt(vn * c2) + ADAM_EPS) + ADAM_WD * w_ref[...])

    blk = pl.BlockSpec((tr, C), lambda i: (i, 0))
    return pl.pallas_call(
        body, grid=(R // tr,), in_specs=[blk, blk, blk, pl.BlockSpec((n, tr, C), lambda i: (0, i, 0))],
        out_specs=[blk] * 4, out_shape=[_sds((R, C), F32)] * 4, name=name, compiler_params=_params("parallel"))(w, m, v, gparts)


_SMALL_ORDER = ("norm_mix", "b_gate", "sc_conv_w", "ssm_conv_w", "ssm_conv_b", "dt_bias", "A_log", "D_skip", "ssm_norm_w",
                "norm_mlp", "norm_final", "loss")
_REPLICATED = ("norm_mix", "b_gate", "ssm_conv_b", "dt_bias", "A_log", "D_skip", "ssm_norm_w", "norm_mlp", "norm_final")


def _cols_to_slots(g, n):
    R = g.shape[0]
    return jnp.transpose(g.reshape(R, n, g.shape[1] // n), (1, 0, 2))


def _slots_to_cols(g):
    n, R, C = g.shape
    return jnp.transpose(g, (1, 0, 2)).reshape(R, n * C)


def kernel(x, norm_mix, w_in, b_gate, sc_conv_w, ssm_conv_w, ssm_conv_b, dt_bias, A_log, D_skip, ssm_norm_w, w_branch_sc, w_branch_ssm, w_out, norm_mlp, w_mlp1, w_mlp2, norm_final, loss_target, m_norm_mix, m_w_in, m_b_gate, m_sc_conv_w, m_ssm_conv_w, m_ssm_conv_b, m_dt_bias, m_A_log, m_D_skip, m_ssm_norm_w, m_w_branch_sc, m_w_branch_ssm, m_w_out, m_norm_mlp, m_w_mlp1, m_w_mlp2, m_norm_final, v_norm_mix, v_w_in, v_b_gate, v_sc_conv_w, v_ssm_conv_w, v_ssm_conv_b, v_dt_bias, v_A_log, v_D_skip, v_ssm_norm_w, v_w_branch_sc, v_w_branch_ssm, v_w_out, v_norm_mlp, v_w_mlp1, v_w_mlp2, v_norm_final):
    T, D = x.shape[1], x.shape[2]
    n_inner = 2 * D
    n_heads = n_inner // HEADDIM
    n_xbc = n_inner + 2 * NGROUPS * NSTATE
    me = 4 * lax.axis_index("x") + 2 * lax.axis_index("y") + lax.axis_index("c")

    o_z, o_xbc, o_dt, o_gate = 3 * D, 3 * D + n_inner, 3 * D + n_inner + n_xbc, 3 * D + n_inner + n_xbc + n_heads
    by_owner = lambda b: b.reshape((N_DEV,) + b.shape[2:])
    to_owner = lambda g: g.reshape((4, 2) + g.shape[1:])
    rows_of = lambda g: to_owner(g.reshape((N_DEV, g.shape[0] // N_DEV) + g.shape[1:]))
    cols_of = lambda g: to_owner(_cols_to_slots(g, N_DEV))

    class Schedule(_NoExchange):
        gather_ici = dict(proj_xbc=("bssm",), proj_sc=("bsc", "out"), ssm_conv_fwd=("w2",), ssd_fwd=("w1",))
        gather_sib = dict(gnorm_fwd=("bsc", "bssm", "out"), branch_ssm=("w1", "w2"))
        scatter_sib = dict(mlp_up_dx=("w2", "w1"), branch_ssm_dx=("out", "bssm", "bsc"))
        scatter_ici = dict(ssd_bwd=("out", "bssm", "bsc"), ssm_conv_bwd=("w2", "w1"))
        shards = dict(bsc=w_branch_sc, bssm=w_branch_ssm, out=w_out, w1=w_mlp1, w2=w_mlp2)

        def __init__(self):
            bufs = _run_comm(_gather_ici([w_in.astype(BF16), sc_conv_w, ssm_conv_w]), "gather_in_ici")
            bufs = _run_comm(_gather_sibling(bufs), "gather_in_sibling")
            win_full = _slots_to_cols(by_owner(bufs[0]))
            self.W = dict(sc=win_full[:, :o_z], z=win_full[:, o_z:o_xbc], xbc=win_full[:, o_xbc:o_dt],
                          dt=jnp.pad(win_full[:, o_dt:o_gate], ((0, 0), (0, LANES - n_heads))), gate=win_full[:, o_gate:])
            self.taps = dict(sc_conv_w=_slots_to_cols(by_owner(bufs[1])), ssm_conv_w=_slots_to_cols(by_owner(bufs[2])))
            self.staged, self.grads, self.halves, self.summed = {}, {}, {}, {}

        def carry(self, name):
            if name in self.gather_ici:
                return _gather_ici([self.shards[k].astype(BF16) for k in self.gather_ici[name]])
            if name in self.gather_sib:
                return _gather_sibling([self.staged.pop(k) for k in self.gather_sib[name]])
            if name in self.scatter_sib:
                return _scatter_sibling([self.grads[k] for k in self.scatter_sib[name]])
            if name in self.scatter_ici:
                return _scatter_ici([self.halves[k] for k in self.scatter_ici[name]])
            return None

        def carried(self, name, outs):
            if name in self.gather_ici:
                self.staged.update(zip(self.gather_ici[name], outs))
            elif name in self.gather_sib:
                for k, b in zip(self.gather_sib[name], outs):
                    full = by_owner(b)
                    self.W[k] = _slots_to_cols(full) if k == "w1" else full.reshape(-1, D)
            elif name in self.scatter_sib:
                for k, b in zip(self.scatter_sib[name], outs):
                    self.halves[k] = _add_halves(self.grads[k], b, "add_halves_" + k)
            else:
                self.summed.update(zip(self.scatter_ici[name], outs))

        def grad(self, k, g):
            if k == "win":
                g = cols_of(jnp.concatenate([g["sc"], g["z"], g["xbc"], g["dt"][:, :n_heads], g["gate"]], axis=1))
                got = _run_comm(_scatter_sibling([g]), "scatter_sibling_win")[0]
                half = _add_halves(g, got, "add_halves_win")
                self.in_flight = _scatter_ici_start(half, _own_part(half, "own_part_win"), "scatter_win_start")
            else:
                self.grads[k] = cols_of(g) if k == "w1" else rows_of(g)

        def after_in_grad(self, pieces):
            token = self.in_flight[4][0, 0]
            return [(k, d + token.astype(d.dtype) if k == "dt" else d) for k, d in pieces]

        def finish_in_grad(self, after):
            send_sems, recv_sems, src_thru, land_thru, _ = self.in_flight
            self.summed["win"] = _scatter_ici_wait(send_sems, recv_sems, src_thru, land_thru, after, "scatter_win_wait")

    S = Schedule()
    small = dict(norm_mix=norm_mix, b_gate=b_gate, ssm_conv_b=ssm_conv_b, dt_bias=dt_bias, A_log=A_log, D_skip=D_skip,
                 ssm_norm_w=ssm_norm_w, norm_mlp=norm_mlp, norm_final=norm_final, **S.taps)
    grad_x, g_small = _local_step(x.reshape(T, D), loss_target.reshape(T, D), S, small)

    small_flat = jnp.concatenate([g_small[k].reshape(-1) for k in _SMALL_ORDER])
    n_small = small_flat.shape[0]
    rows = -(-n_small // (8 * LANES)) * 8
    small_pack = jnp.pad(small_flat, (0, rows * LANES - n_small)).reshape(rows, LANES)
    small_parts = _run_comm(_gather_all([small_pack]), "gather_small")[0]

    res = {}
    big = [("w_in", "win", w_in, m_w_in, v_w_in), ("w_branch_sc", "bsc", w_branch_sc, m_w_branch_sc, v_w_branch_sc),
           ("w_branch_ssm", "bssm", w_branch_ssm, m_w_branch_ssm, v_w_branch_ssm), ("w_out", "out", w_out, m_w_out, v_w_out),
           ("w_mlp1", "w1", w_mlp1, m_w_mlp1, v_w_mlp1), ("w_mlp2", "w2", w_mlp2, m_w_mlp2, v_w_mlp2)]
    for k, gk, w, m, v in big[1:]:
        res[k] = _adam(w, m, v, S.summed[gk], "adam_" + k)

    sizes = {k: g_small[k].size for k in _SMALL_ORDER}
    offs, o = {}, 0
    for k in _SMALL_ORDER:
        offs[k] = o
        o += sizes[k]
    rep_w = dict(norm_mix=norm_mix, b_gate=b_gate, ssm_conv_b=ssm_conv_b, dt_bias=dt_bias, A_log=A_log, D_skip=D_skip,
                 ssm_norm_w=ssm_norm_w, norm_mlp=norm_mlp, norm_final=norm_final)
    rep_m = dict(norm_mix=m_norm_mix, b_gate=m_b_gate, ssm_conv_b=m_ssm_conv_b, dt_bias=m_dt_bias, A_log=m_A_log, D_skip=m_D_skip,
                 ssm_norm_w=m_ssm_norm_w, norm_mlp=m_norm_mlp, norm_final=m_norm_final)
    rep_v = dict(norm_mix=v_norm_mix, b_gate=v_b_gate, ssm_conv_b=v_ssm_conv_b, dt_bias=v_dt_bias, A_log=v_A_log, D_skip=v_D_skip,
                 ssm_norm_w=v_ssm_norm_w, norm_mlp=v_norm_mlp, norm_final=v_norm_final)

    def pack(d):
        segs = [jnp.pad(d[k].astype(F32).reshape(-1), (0, sizes[k] - d[k].size)) if k in d else jnp.zeros((sizes[k],), F32)
                for k in _SMALL_ORDER]
        return jnp.pad(jnp.concatenate(segs), (0, rows * LANES - n_small)).reshape(rows, LANES)

    sm = _adam(pack(rep_w), pack(rep_m), pack(rep_v), small_parts, "adam_small")
    sm = [s.reshape(-1) for s in sm]
    for k in _REPLICATED:
        n_k = rep_w[k].shape[0]
        res[k] = tuple(s[offs[k]:offs[k] + n_k] for s in sm)
    loss = sm[0][offs["loss"]]
    for k, w, m, v, K, full in (("sc_conv_w", sc_conv_w, m_sc_conv_w, v_sc_conv_w, SC_K, D),
                                ("ssm_conv_w", ssm_conv_w, m_ssm_conv_w, v_ssm_conv_w, SSM_K, n_xbc)):
        g_full = sm[0][offs[k]:offs[k] + K * full].reshape(K, full)
        cw = full // N_DEV
        g_mine = lax.dynamic_slice_in_dim(g_full, me * cw, cw, axis=1)
        res[k] = _adam(w, m, v, g_mine[None], "adam_" + k)

    S.finish_in_grad([grad_x, sm[1]] + [res[k][1] for k, *_ in big[1:]])
    k, gk, w, m, v = big[0]
    res[k] = _adam(w, m, v, S.summed[gk], "adam_" + k)

    order = ("norm_mix", "w_in", "b_gate", "sc_conv_w", "ssm_conv_w", "ssm_conv_b", "dt_bias", "A_log", "D_skip", "ssm_norm_w",
             "w_branch_sc", "w_branch_ssm", "w_out", "norm_mlp", "w_mlp1", "w_mlp2", "norm_final")
    outs = [loss, grad_x.reshape(1, T, D)]
    for j in range(4):
        outs += [res[k][j] for k in order]
    return tuple(outs)
```

```python
import functools

import jax
import jax.numpy as jnp
from jax import lax
from jax.experimental import pallas as pl
from jax.experimental.pallas import tpu as pltpu

F32 = jnp.float32
BF16 = jnp.bfloat16

EPS = 1e-6
N_DEV = 8
HEADDIM = 64
NSTATE = 128
CHUNK = 128
NGROUPS = 8
GROUP_W = 256
SC_K = 3
SSM_K = 4
LANES = 128

ADAM_LR = 0.001
ADAM_B1 = 0.9
ADAM_B2 = 0.999
ADAM_EPS = 1e-08
ADAM_WD = 0.01
ADAM_STEP = 10

NN = (((1,), (0,)), ((), ()))
NT = (((1,), (1,)), ((), ()))
TN = (((0,), (0,)), ((), ()))
_DIMS = {"nn": NN, "nt": NT, "tn": TN}

ANY = pl.BlockSpec(memory_space=pl.ANY)
MESH = pl.DeviceIdType.MESH


def _sds(shape, dtype):
    return jax.ShapeDtypeStruct(tuple(shape), dtype)


def _dot(a, b, dims=NN):
    return lax.dot_general(a, b, dims, preferred_element_type=F32)


def _dot3(a, b, dims=NN):
    return lax.dot_general(a, b, dims, preferred_element_type=F32, precision=lax.Precision.HIGH)


def _params(*sem):
    return pltpu.CompilerParams(dimension_semantics=tuple(sem))


def _call(body, *, grid, in_specs, out_specs, out_shape, args, name, sem, scratch=(), comm=None):
    if comm is None:
        outs = pl.pallas_call(body, grid=grid, in_specs=list(in_specs), out_specs=list(out_specs), out_shape=list(out_shape),
                              scratch_shapes=list(scratch), name=name, compiler_params=_params(*sem))(*args)
        return list(outs), None
    n, n_in, n_out, n_scr = comm.n, len(in_specs), len(out_shape), len(scratch)

    def wrapped(*refs):
        ins, c_in = refs[:n_in], refs[n_in:n_in + n]
        outs, c_out = refs[n_in + n:n_in + n + n_out], refs[n_in + n + n_out:n_in + 2 * n + n_out]
        rest = refs[n_in + 2 * n + n_out:]
        scr, sems = rest[:n_scr], rest[n_scr:]
        first, last = None, None
        for d, g in enumerate(grid):
            f, l = pl.program_id(d) == 0, pl.program_id(d) == g - 1
            first, last = (f, l) if first is None else (first & f, last & l)

        @pl.when(first)
        def _():
            comm.start(c_in, c_out, sems)

        body(*ins, *outs, *scr)

        @pl.when(last)
        def _():
            comm.finish(c_in, c_out, sems)

    outs = pl.pallas_call(
        wrapped, grid=grid, in_specs=list(in_specs) + [ANY] * n, out_specs=list(out_specs) + [ANY] * n,
        out_shape=list(out_shape) + comm.out_shape, scratch_shapes=list(scratch) + comm.scratch,
        input_output_aliases={n_in + i: n_out + o for i, o in comm.aliases.items()},
        name=name, compiler_params=_params(*["arbitrary"] * len(grid)))(*args, *comm.arrs)
    return list(outs[:n_out]), list(outs[n_out:])


MM_VMEM_BUDGET = 44 * 2 ** 20


def _mm_tiles(M, N, k_bytes, mn_bytes):
    best = None
    for tm in (2048, 1024, 512, 256, 128):
        for tn in (1024, 512, 256, 128):
            if M % tm or N % tn:
                continue
            need = 2 * ((tm + tn) * k_bytes + tm * tn * mn_bytes) + 4 * tm * tn * 4
            if need <= MM_VMEM_BUDGET and (best is None or (tm * tn, tm) > (best[0] * best[1], best[0])):
                best = (tm, tn)
    assert best is not None, (M, N, k_bytes, mn_bytes)
    return best


def _mm(a, b, *, mode, name, extras=(), epi=None, out_dtypes=(F32,), comm=None):
    a_list = list(a) if isinstance(a, (list, tuple)) else [a]
    b_list = list(b) if isinstance(b, (list, tuple)) else [b]
    if mode == "nn":
        M, N = a_list[0].shape[0], b_list[0].shape[1]
    elif mode == "nt":
        M, N = a_list[0].shape[0], b_list[0].shape[0]
    else:
        M, N = a_list[0].shape[1], b_list[0].shape[1]
    k_bytes = sum((av.shape[0] if mode == "tn" else av.shape[1]) * av.dtype.itemsize for av in a_list)
    mn_bytes = sum(e.dtype.itemsize for e in extras) + sum(jnp.dtype(d).itemsize for d in out_dtypes)
    tm, tn = _mm_tiles(min(M, 2048), min(N, 1024), k_bytes, mn_bytes) if M % 128 == 0 and N % 128 == 0 else (M, N)
    assert M % tm == 0 and N % tn == 0
    a_specs, b_specs = [], []
    for av, bv in zip(a_list, b_list):
        K = av.shape[0] if mode == "tn" else av.shape[1]
        a_specs.append(pl.BlockSpec((K, tm), lambda i, j: (0, i)) if mode == "tn" else pl.BlockSpec((tm, K), lambda i, j: (i, 0)))
        b_specs.append(pl.BlockSpec((tn, K), lambda i, j: (j, 0)) if mode == "nt" else pl.BlockSpec((K, tn), lambda i, j: (0, j)))
    mn_spec = pl.BlockSpec((tm, tn), lambda i, j: (i, j))
    n_p, n_ex = len(a_list), len(extras)
    dims = _DIMS[mode]

    def body(*refs):
        acc = _dot(refs[0][...], refs[n_p][...], dims)
        for p in range(1, n_p):
            acc = acc + _dot(refs[p][...], refs[n_p + p][...], dims)
        rest = refs[2 * n_p:]
        res = (acc,) if epi is None else epi(acc, *[r[...] for r in rest[:n_ex]])
        for o_ref, r in zip(rest[n_ex:], res):
            o_ref[...] = r.astype(o_ref.dtype)

    outs, carried = _call(
        body, grid=(M // tm, N // tn), in_specs=a_specs + b_specs + [mn_spec] * n_ex,
        out_specs=[mn_spec] * len(out_dtypes), out_shape=[_sds((M, N), d) for d in out_dtypes],
        args=a_list + b_list + list(extras), name=name, sem=("parallel", "parallel"), comm=comm)
    res = outs[0] if len(outs) == 1 else outs
    return res if comm is None else (res, carried)


def _epi_add(acc, r):
    return (acc + r,)


def _epi_add2(acc, r):
    s = acc + r
    return (s, s)


def _epi_relu2(acc):
    p = jnp.maximum(acc, 0.0)
    return (p * p,)


def _epi_relu2_bwd(acc, r):
    return (acc * (2.0 * jnp.sqrt(r.astype(F32))),)


def _row(tr, n):
    return pl.BlockSpec((tr, n), lambda i: (i, 0))


def _vec(n):
    return pl.BlockSpec((1, n), lambda i: (0, 0))


def _rms_fwd(x, w, name):
    T, D = x.shape
    tr = min(256, T)

    def body(x_ref, w_ref, o_ref):
        xv = x_ref[...]
        r = lax.rsqrt(jnp.mean(xv * xv, axis=-1, keepdims=True) + EPS)
        o_ref[...] = (xv * r * w_ref[...]).astype(BF16)

    return pl.pallas_call(body, grid=(T // tr,), in_specs=[_row(tr, D), _vec(D)], out_specs=_row(tr, D),
                          out_shape=_sds((T, D), BF16), name=name, compiler_params=_params("parallel"))(x, w)


def _rms_bwd(x, w, dh, dres, name):
    T, D = x.shape
    tr = min(256, T)

    def body(x_ref, w_ref, dh_ref, dres_ref, dx_ref, dxb_ref, dw_ref):
        @pl.when(pl.program_id(0) == 0)
        def _():
            dw_ref[...] = jnp.zeros_like(dw_ref)

        xv = x_ref[...]
        r = lax.rsqrt(jnp.mean(xv * xv, axis=-1, keepdims=True) + EPS)
        xh = xv * r
        dh_v = dh_ref[...]
        dw_ref[...] += jnp.sum(dh_v * xh, axis=0, keepdims=True)
        dxh = dh_v * w_ref[...]
        dx = r * (dxh - xh * jnp.mean(dxh * xh, axis=-1, keepdims=True)) + dres_ref[...]
        dx_ref[...] = dx
        dxb_ref[...] = dx.astype(BF16)

    return pl.pallas_call(
        body, grid=(T // tr,), in_specs=[_row(tr, D), _vec(D), _row(tr, D), _row(tr, D)],
        out_specs=[_row(tr, D), _row(tr, D), _vec(D)],
        out_shape=[_sds((T, D), F32), _sds((T, D), BF16), _sds((1, D), F32)],
        name=name, compiler_params=_params("arbitrary"))(x, w, dh, dres)


def _final(x2, w, tgt, name):
    T, D = x2.shape
    tr = min(256, T)

    def body(x_ref, w_ref, t_ref, dx_ref, dxb_ref, dw_ref, loss_ref):
        @pl.when(pl.program_id(0) == 0)
        def _():
            dw_ref[...] = jnp.zeros_like(dw_ref)
            loss_ref[...] = jnp.zeros_like(loss_ref)

        xv = x_ref[...]
        wv = w_ref[...]
        r = lax.rsqrt(jnp.mean(xv * xv, axis=-1, keepdims=True) + EPS)
        xh = xv * r
        err = xh * wv - t_ref[...]
        part = jnp.sum(jnp.sum(err * err, axis=1, keepdims=True), axis=0, keepdims=True) * (0.5 / D)
        loss_ref[...] += jnp.broadcast_to(part, loss_ref.shape)
        dy = err * (1.0 / D)
        dw_ref[...] += jnp.sum(dy * xh, axis=0, keepdims=True)
        dxh = dy * wv
        dx = r * (dxh - xh * jnp.mean(dxh * xh, axis=-1, keepdims=True))
        dx_ref[...] = dx
        dxb_ref[...] = dx.astype(BF16)

    return pl.pallas_call(
        body, grid=(T // tr,), in_specs=[_row(tr, D), _vec(D), _row(tr, D)],
        out_specs=[_row(tr, D), _row(tr, D), _vec(D), _vec(LANES)],
        out_shape=[_sds((T, D), F32), _sds((T, D), BF16), _sds((1, D), F32), _sds((1, LANES), F32)],
        name=name, compiler_params=_params("arbitrary"))(x2, w, tgt)


def _silu_parts(z):
    s = jax.nn.sigmoid(z)
    return z * s, s * (1.0 + z * (1.0 - s))


def _gnorm_fwd(y, z, w, name, comm=None):
    T, N = y.shape
    tr = min(256, T)

    def body(y_ref, z_ref, w_ref, o_ref):
        for g in range(N // GROUP_W):
            sl = slice(g * GROUP_W, (g + 1) * GROUP_W)
            silu, _ = _silu_parts(z_ref[:, sl])
            yz = y_ref[:, sl] * silu
            r = lax.rsqrt(jnp.mean(yz * yz, axis=-1, keepdims=True) + EPS)
            o_ref[:, sl] = (yz * r * w_ref[:, sl]).astype(BF16)

    outs, carried = _call(body, grid=(T // tr,), in_specs=[_row(tr, N), _row(tr, N), _vec(N)], out_specs=[_row(tr, N)],
                          out_shape=[_sds((T, N), BF16)], args=[y, z, w], name=name, sem=("parallel",), comm=comm)
    return outs[0] if comm is None else (outs[0], carried)


def _gnorm_bwd(y, z, w, dyb, name):
    T, N = y.shape
    tr = min(256, T)

    def body(y_ref, z_ref, w_ref, d_ref, dy_ref, dz_ref, dw_ref):
        @pl.when(pl.program_id(0) == 0)
        def _():
            dw_ref[...] = jnp.zeros_like(dw_ref)

        for g in range(N // GROUP_W):
            sl = slice(g * GROUP_W, (g + 1) * GROUP_W)
            yv = y_ref[:, sl]
            silu, dsilu = _silu_parts(z_ref[:, sl])
            yz = yv * silu
            r = lax.rsqrt(jnp.mean(yz * yz, axis=-1, keepdims=True) + EPS)
            yzh = yz * r
            d = d_ref[:, sl]
            dw_ref[:, sl] += jnp.sum(d * yzh, axis=0, keepdims=True)
            dyzh = d * w_ref[:, sl]
            dyz = r * (dyzh - yzh * jnp.mean(dyzh * yzh, axis=-1, keepdims=True))
            dy_ref[:, sl] = dyz * silu
            dz_ref[:, sl] = (dyz * yv * dsilu).astype(BF16)

    return pl.pallas_call(
        body, grid=(T // tr,), in_specs=[_row(tr, N), _row(tr, N), _vec(N), _row(tr, N)],
        out_specs=[_row(tr, N), _row(tr, N), _vec(N)],
        out_shape=[_sds((T, N), F32), _sds((T, N), BF16), _sds((1, N), F32)],
        name=name, compiler_params=_params("arbitrary"))(y, z, w, dyb)


def _merge_fwd(gate_raw, b_gate, br_a, br_b, name):
    T, D = br_a.shape
    tr = min(256, T)

    def body(g_ref, bg_ref, a_ref, b_ref, o_ref):
        g = jax.nn.sigmoid(g_ref[...] + bg_ref[...])
        o_ref[...] = (g[:, :D] * a_ref[...] + g[:, D:] * b_ref[...]).astype(BF16)

    return pl.pallas_call(body, grid=(T // tr,), in_specs=[_row(tr, 2 * D), _vec(2 * D), _row(tr, D), _row(tr, D)],
                          out_specs=_row(tr, D), out_shape=_sds((T, D), BF16), name=name,
                          compiler_params=_params("parallel"))(gate_raw, b_gate, br_a, br_b)


def _merge_bwd(dmerged, gate_raw, b_gate, br_a, br_b, name):
    T, D = br_a.shape
    tr = min(256, T)

    def body(d_ref, g_ref, bg_ref, a_ref, b_ref, da_ref, db_ref, dg_ref, dbg_ref):
        @pl.when(pl.program_id(0) == 0)
        def _():
            dbg_ref[...] = jnp.zeros_like(dbg_ref)

        g = jax.nn.sigmoid(g_ref[...] + bg_ref[...])
        d = d_ref[...]
        da_ref[...] = (d * g[:, :D]).astype(BF16)
        db_ref[...] = (d * g[:, D:]).astype(BF16)
        dg = jnp.concatenate([d * a_ref[...], d * b_ref[...]], axis=1) * g * (1.0 - g)
        dg_ref[...] = dg.astype(BF16)
        dbg_ref[...] += jnp.sum(dg, axis=0, keepdims=True)

    return pl.pallas_call(
        body, grid=(T // tr,), in_specs=[_row(tr, D), _row(tr, 2 * D), _vec(2 * D), _row(tr, D), _row(tr, D)],
        out_specs=[_row(tr, D), _row(tr, D), _row(tr, 2 * D), _vec(2 * D)],
        out_shape=[_sds((T, D), BF16), _sds((T, D), BF16), _sds((T, 2 * D), BF16), _sds((1, 2 * D), F32)],
        name=name, compiler_params=_params("arbitrary"))(dmerged, gate_raw, b_gate, br_a, br_b)


def _shift_down(u, s):
    if s == 0:
        return u
    row = lax.broadcasted_iota(jnp.int32, u.shape, 0)
    return jnp.where(row >= s, pltpu.roll(u, s, 0), 0.0)


def _shift_up(u, s):
    if s == 0:
        return u
    n = u.shape[0]
    row = lax.broadcasted_iota(jnp.int32, u.shape, 0)
    return jnp.where(row < n - s, pltpu.roll(u, n - s, 0), 0.0)


def _conv(u, w_ref, K):
    acc = u * w_ref[K - 1:K, :]
    for k in range(K - 1):
        acc = acc + _shift_down(u, K - 1 - k) * w_ref[k:k + 1, :]
    return acc


def _conv_bwd(u, dc, w_ref, dw_ref, K):
    du = dc * w_ref[K - 1:K, :]
    dw_ref[K - 1:K, :] = jnp.sum(dc * u, axis=0, keepdims=True)
    for k in range(K - 1):
        s = K - 1 - k
        dw_ref[k:k + 1, :] = jnp.sum(dc * _shift_down(u, s), axis=0, keepdims=True)
        du = du + _shift_up(dc, s) * w_ref[k:k + 1, :]
    return du


CB_W = 256


def _col(T, j0=0):
    return pl.BlockSpec((T, CB_W), lambda j: (0, j + j0))


def _sc_fwd(psc, w, name):
    T, D = psc.shape[0], psc.shape[1] // 3
    nb = D // CB_W

    def body(b_ref, c_ref, x_ref, w_ref, o_ref):
        o_ref[...] = (b_ref[...] * _conv(c_ref[...] * x_ref[...], w_ref, SC_K)).astype(BF16)

    return pl.pallas_call(
        body, grid=(nb,), in_specs=[_col(T), _col(T, nb), _col(T, 2 * nb), pl.BlockSpec((SC_K, CB_W), lambda j: (0, j))],
        out_specs=_col(T), out_shape=_sds((T, D), BF16), name=name, compiler_params=_params("parallel"))(psc, psc, psc, w)


def _sc_bwd(psc, w, dya, name):
    T, D = psc.shape[0], psc.shape[1] // 3
    nb = D // CB_W

    def body(b_ref, c_ref, x_ref, w_ref, d_ref, db_ref, dc_ref, dx_ref, dw_ref):
        cv, xv, d = c_ref[...], x_ref[...], d_ref[...]
        u = cv * xv
        db_ref[...] = (d * _conv(u, w_ref, SC_K)).astype(BF16)
        du = _conv_bwd(u, d * b_ref[...], w_ref, dw_ref, SC_K)
        dc_ref[...] = (du * xv).astype(BF16)
        dx_ref[...] = (du * cv).astype(BF16)

    wspec = pl.BlockSpec((SC_K, CB_W), lambda j: (0, j))
    return pl.pallas_call(
        body, grid=(nb,), in_specs=[_col(T), _col(T, nb), _col(T, 2 * nb), wspec, _col(T)],
        out_specs=[_col(T), _col(T), _col(T), wspec],
        out_shape=[_sds((T, D), BF16)] * 3 + [_sds((SC_K, D), F32)],
        name=name, compiler_params=_params("parallel"))(psc, psc, psc, w, dya)


def _ssm_conv_fwd(u, w, b, name, comm=None):
    T, N = u.shape

    def body(u_ref, w_ref, b_ref, o_ref):
        c = _conv(u_ref[...], w_ref, SSM_K) + b_ref[...]
        o_ref[...] = c * jax.nn.sigmoid(c)

    outs, carried = _call(
        body, grid=(N // CB_W,), in_specs=[_col(T), pl.BlockSpec((SSM_K, CB_W), lambda j: (0, j)), pl.BlockSpec((1, CB_W), lambda j: (0, j))],
        out_specs=[_col(T)], out_shape=[_sds((T, N), F32)], args=[u, w, b], name=name, sem=("parallel",), comm=comm)
    return outs[0] if comm is None else (outs[0], carried)


def _ssm_conv_bwd(u, w, b, dxs, dB, dC, name, comm=None):
    T, N = u.shape
    n_x, n_b = dxs.shape[1] // CB_W, dB.shape[1] // CB_W

    def body(u_ref, w_ref, b_ref, dx_ref, db_ref, dc_ref, du_ref, dw_ref, dbias_ref):
        j = pl.program_id(0)
        uv = u_ref[...]
        c = _conv(uv, w_ref, SSM_K) + b_ref[...]
        _, dsilu = _silu_parts(c)
        d = jnp.where(j < n_x, dx_ref[...], jnp.where(j < n_x + n_b, db_ref[...], dc_ref[...])) * dsilu
        dbias_ref[...] = jnp.sum(d, axis=0, keepdims=True)
        du_ref[...] = _conv_bwd(uv, d, w_ref, dw_ref, SSM_K).astype(BF16)

    wspec = pl.BlockSpec((SSM_K, CB_W), lambda j: (0, j))
    bspec = pl.BlockSpec((1, CB_W), lambda j: (0, j))
    outs, carried = _call(
        body, grid=(N // CB_W,),
        in_specs=[_col(T), wspec, bspec,
                  pl.BlockSpec((T, CB_W), lambda j: (0, jnp.minimum(j, n_x - 1))),
                  pl.BlockSpec((T, CB_W), lambda j: (0, jnp.clip(j - n_x, 0, n_b - 1))),
                  pl.BlockSpec((T, CB_W), lambda j: (0, jnp.clip(j - n_x - n_b, 0, n_b - 1)))],
        out_specs=[_col(T), wspec, bspec],
        out_shape=[_sds((T, N), BF16), _sds((SSM_K, N), F32), _sds((1, N), F32)],
        args=[u, w, b, dxs, dB, dC], name=name, sem=("parallel",), comm=comm)
    return outs if comm is None else (outs, carried)


def _split3(v):
    hi = v.astype(BF16)
    r = v - hi.astype(F32)
    mid = r.astype(BF16)
    lo = (r - mid.astype(F32)).astype(BF16)
    return hi, mid, lo


def _head_expand(n_lanes):
    h = lax.broadcasted_iota(jnp.int32, (LANES, n_lanes), 0)
    l = lax.broadcasted_iota(jnp.int32, (LANES, n_lanes), 1)
    return (jnp.right_shift(l, HEADDIM.bit_length() - 1) == h).astype(BF16)


def _softplus(v):
    return jnp.maximum(v, 0.0) + jnp.log1p(jnp.exp(-jnp.abs(v)))


def _ssd_prep(dt_raw, dt_bias, a_log, n_inner, name):
    T = dt_raw.shape[0]

    def body(r_ref, b_ref, al_ref, dt_ref, cs_ref):
        dt = _softplus(r_ref[...] + b_ref[...])
        a = dt * (-jnp.exp(al_ref[...]))
        i = lax.broadcasted_iota(jnp.int32, (CHUNK, CHUNK), 0)
        j = lax.broadcasted_iota(jnp.int32, (CHUNK, CHUNK), 1)
        tri = (j <= i).astype(BF16)
        cs = sum(_dot(tri, p) for p in _split3(a))
        ex = _head_expand(n_inner)
        dt_ref[...] = sum(_dot(p, ex) for p in _split3(dt))
        cs_ref[...] = sum(_dot(p, ex) for p in _split3(cs))

    blk = pl.BlockSpec((CHUNK, LANES), lambda c: (c, 0))
    out = pl.BlockSpec((CHUNK, n_inner), lambda c: (c, 0))
    return pl.pallas_call(body, grid=(T // CHUNK,), in_specs=[blk, _vec(LANES), _vec(LANES)], out_specs=[out, out],
                          out_shape=[_sds((T, n_inner), F32)] * 2, name=name, compiler_params=_params("parallel"))(dt_raw, dt_bias, a_log)


def _pair_terms(cs_p):
    lane = lax.broadcasted_iota(jnp.int32, (CHUNK, CHUNK), 1)
    sub = lax.broadcasted_iota(jnp.int32, (CHUNK, CHUNK), 0)
    csT = cs_p.T
    Ls = []
    for k in range(2):
        col = jnp.sum(jnp.where(lane == k * HEADDIM, cs_p, 0.0), axis=1, keepdims=True)
        rowv = csT[k * HEADDIM:k * HEADDIM + 1, :]
        Ls.append(jnp.exp(jnp.where(sub >= lane, col - rowv, -jnp.inf)))
    return Ls, jnp.exp(csT[:, CHUNK - 1:CHUNK])


def _block_diag(xp):
    lane = lax.broadcasted_iota(jnp.int32, xp.shape, 1)
    return jnp.concatenate([jnp.where(lane < HEADDIM, xp, 0.0), jnp.where(lane >= HEADDIM, xp, 0.0)], axis=0)


SSD_GROUPS_PER_STEP = 8


def _ssd_specs(T, n_inner):
    nc, gs = T // CHUNK, SSD_GROUPS_PER_STEP
    bo, co = n_inner // (gs * NSTATE), (n_inner + NGROUPS * NSTATE) // (gs * NSTATE)
    assert NGROUPS % gs == 0 and n_inner % (gs * NSTATE) == 0 and (NGROUPS * NSTATE) % (gs * NSTATE) == 0
    g_blk = lambda f: pl.BlockSpec((CHUNK, gs * GROUP_W), lambda c, s: (f(c), s))
    b_blk = lambda f: pl.BlockSpec((CHUNK, gs * NSTATE), lambda c, s: (f(c), bo + s))
    c_blk = lambda f: pl.BlockSpec((CHUNK, gs * NSTATE), lambda c, s: (f(c), co + s))
    return nc, g_blk, b_blk, c_blk


def _ssd_fwd(xbc, dt_e, cs_e, d_e, name, comm=None):
    T = xbc.shape[0]
    n_inner = dt_e.shape[1]
    nc, g_blk, b_blk, c_blk = _ssd_specs(T, n_inner)
    ident = lambda c: c

    gs = SSD_GROUPS_PER_STEP

    def body(xs_ref, b_ref, c_ref, dt_ref, cs_ref, d_ref, y_ref, p_ref, st):
        c, s = pl.program_id(0), pl.program_id(1)

        @pl.when(c == 0)
        def _():
            for gi in range(gs):
                st[s * gs + gi] = jnp.zeros((GROUP_W, NSTATE), F32)

        for gi in range(gs):
            g = s * gs + gi
            gw, gn = slice(gi * GROUP_W, (gi + 1) * GROUP_W), slice(gi * NSTATE, (gi + 1) * NSTATE)
            P = st[g]
            p_ref[0, gi] = P
            xs, dt, cs = xs_ref[:, gw], dt_ref[:, gw], cs_ref[:, gw]
            Bf, Cf = b_ref[:, gn], c_ref[:, gn]
            CBm = _dot3(Cf, Bf, NT)
            X = xs * dt
            decay = jnp.exp(cs[CHUNK - 1:CHUNK, :] - cs)
            y_off = _dot3(Cf, P, NT) * jnp.exp(cs)
            ys, ecl = [], []
            for pr in range(2):
                sl = slice(pr * LANES, (pr + 1) * LANES)
                Ls, e_last = _pair_terms(cs[:, sl])
                ecl.append(e_last)
                Mcat = jnp.concatenate([CBm * L for L in Ls], axis=1)
                ys.append(_dot3(Mcat, _block_diag(X[:, sl])))
            y_ref[:, gw] = jnp.concatenate(ys, axis=1) + y_off + xs * d_ref[:, gw]
            S = _dot3(X * decay, Bf, TN)
            st[g] = P * jnp.concatenate(ecl, axis=0) + S

    p_blk = pl.BlockSpec((1, gs, GROUP_W, NSTATE), lambda c, s: (c, s, 0, 0))
    outs, carried = _call(
        body, grid=(nc, NGROUPS // gs),
        in_specs=[g_blk(ident), b_blk(ident), c_blk(ident), g_blk(ident), g_blk(ident), pl.BlockSpec((1, gs * GROUP_W), lambda c, s: (0, s))],
        out_specs=[g_blk(ident), p_blk],
        out_shape=[_sds((T, n_inner), F32), _sds((nc, NGROUPS, GROUP_W, NSTATE), F32)],
        scratch=[pltpu.VMEM((NGROUPS, GROUP_W, NSTATE), F32)],
        args=[xbc, xbc, xbc, dt_e, cs_e, d_e], name=name, sem=("arbitrary", "arbitrary"), comm=comm)
    return outs if comm is None else (outs, carried)


def _ssd_bwd(xbc, dt_e, cs_e, d_e, states, dy, name, comm=None):
    T = xbc.shape[0]
    n_inner = dt_e.shape[1]
    nc, g_blk, b_blk, c_blk = _ssd_specs(T, n_inner)
    rev = lambda c: nc - 1 - c

    gs = SSD_GROUPS_PER_STEP

    def body(xs_ref, b_ref, c_ref, dt_ref, cs_ref, d_ref, p_ref, pn_ref, dy_ref,
             dxs_ref, db_ref, dc_ref, ddt_ref, dcs_ref, dd_ref, dst):
        cc, s = pl.program_id(0), pl.program_id(1)

        @pl.when(cc == 0)
        def _():
            for gi in range(gs):
                dst[s * gs + gi] = jnp.zeros((GROUP_W, NSTATE), F32)

        for gi in range(gs):
            one_group(s * gs + gi, gi, xs_ref, b_ref, c_ref, dt_ref, cs_ref, d_ref, p_ref, pn_ref, dy_ref,
                      dxs_ref, db_ref, dc_ref, ddt_ref, dcs_ref, dd_ref, dst)

    def one_group(g, gi, xs_ref, b_ref, c_ref, dt_ref, cs_ref, d_ref, p_ref, pn_ref, dy_ref,
                  dxs_ref, db_ref, dc_ref, ddt_ref, dcs_ref, dd_ref, dst):
        gw, gn = slice(gi * GROUP_W, (gi + 1) * GROUP_W), slice(gi * NSTATE, (gi + 1) * NSTATE)
        dS = dst[g]
        P, Pn = p_ref[0, gi], pn_ref[0, gi]
        xs, dt, cs, dY = xs_ref[:, gw], dt_ref[:, gw], cs_ref[:, gw], dy_ref[:, gw]
        Bf, Cf = b_ref[:, gn], c_ref[:, gn]
        Bb, Cb = Bf.astype(BF16), Cf.astype(BF16)
        X = xs * dt
        ecs = jnp.exp(cs)
        decay = jnp.exp(cs[CHUNK - 1:CHUNK, :] - cs)
        CBm = _dot3(Cf, Bf, NT)
        dYe = dY * ecs
        dP_off = _dot3(dYe, Cf, TN)
        dC = _dot(dYe.astype(BF16), P.astype(BF16))
        dcs = dYe * _dot3(Cf, P, NT)
        Xd = X * decay
        dB = _dot(Xd.astype(BF16), dS.astype(BF16))
        E = _dot3(Bf, dS, NT)
        dX = E * decay
        dcs = dcs - E * Xd
        R = _dot3(jnp.ones((8, NSTATE), F32), dS * Pn, NT)
        sub_g = lax.broadcasted_iota(jnp.int32, (CHUNK, GROUP_W), 0)
        dcs = dcs + jnp.where(sub_g == CHUNK - 1, R[0:1, :], 0.0)
        lane = lax.broadcasted_iota(jnp.int32, (CHUNK, CHUNK), 1)
        sub = lax.broadcasted_iota(jnp.int32, (CHUNK, CHUNK), 0)
        dCB = jnp.zeros((CHUNK, CHUNK), F32)
        dXs, dcss, ecl = [], [], []
        for pr in range(2):
            sl = slice(pr * LANES, (pr + 1) * LANES)
            Ls, e_last = _pair_terms(cs[:, sl])
            ecl.append(e_last)
            dYp = dY[:, sl]
            dMcat = _dot3(dYp, _block_diag(X[:, sl]), NT)
            Mcat = jnp.concatenate([CBm * L for L in Ls], axis=1)
            dXt = _dot3(Mcat, dYp, TN)
            dXs.append(jnp.where(lane < HEADDIM, dXt[:CHUNK], dXt[CHUNK:]))
            colacc = jnp.zeros((CHUNK, CHUNK), F32)
            rowacc = jnp.zeros((CHUNK, CHUNK), F32)
            for k in range(2):
                dG = dMcat[:, k * CHUNK:(k + 1) * CHUNK] * Ls[k]
                dCB = dCB + dG
                Q = dG * CBm
                colacc = colacc + jnp.where(lane == k * HEADDIM, jnp.sum(Q, axis=1, keepdims=True), 0.0)
                rowacc = rowacc + jnp.where(sub == k * HEADDIM, jnp.sum(Q, axis=0, keepdims=True), 0.0)
            dcss.append(colacc - rowacc.T)
        dX = dX + jnp.concatenate(dXs, axis=1)
        dcs = dcs + jnp.concatenate(dcss, axis=1)
        dCBb = dCB.astype(BF16)
        dc_ref[:, gn] = dC + _dot(dCBb, Bb)
        db_ref[:, gn] = dB + _dot(dCBb, Cb, TN)
        dxs_ref[:, gw] = dX * dt + dY * d_ref[:, gw]
        ddt_ref[:, gw] = dX * xs
        dcs_ref[:, gw] = dcs
        dd_ref[0, :, gw] = jnp.sum(dY * xs, axis=0, keepdims=True)
        dst[g] = dS * jnp.concatenate(ecl, axis=0) + dP_off

    p_blk = pl.BlockSpec((1, gs, GROUP_W, NSTATE), lambda c, s: (nc - 1 - c, s, 0, 0))
    pn_blk = pl.BlockSpec((1, gs, GROUP_W, NSTATE), lambda c, s: (jnp.minimum(nc - c, nc - 1), s, 0, 0))
    st_blk = pl.BlockSpec((CHUNK, gs * NSTATE), lambda c, s: (nc - 1 - c, s))
    outs, carried = _call(
        body, grid=(nc, NGROUPS // gs),
        in_specs=[g_blk(rev), b_blk(rev), c_blk(rev), g_blk(rev), g_blk(rev), pl.BlockSpec((1, gs * GROUP_W), lambda c, s: (0, s)),
                  p_blk, pn_blk, g_blk(rev)],
        out_specs=[g_blk(rev), st_blk, st_blk, g_blk(rev), g_blk(rev), pl.BlockSpec((1, 1, gs * GROUP_W), lambda c, s: (nc - 1 - c, 0, s))],
        out_shape=[_sds((T, n_inner), F32), _sds((T, NGROUPS * NSTATE), F32), _sds((T, NGROUPS * NSTATE), F32),
                   _sds((T, n_inner), F32), _sds((T, n_inner), F32), _sds((nc, 1, n_inner), F32)],
        scratch=[pltpu.VMEM((NGROUPS, GROUP_W, NSTATE), F32)],
        args=[xbc, xbc, xbc, dt_e, cs_e, d_e, states, states, dy], name=name, sem=("arbitrary", "arbitrary"), comm=comm)
    return outs if comm is None else (outs, carried)


def _ssd_post(ddt_e, dcs_e, dd_p, dt_raw, dt_bias, a_log, n_heads, name):
    T, n_inner = ddt_e.shape

    def body(ddt_ref, dcs_ref, dd_ref, r_ref, b_ref, al_ref, draw_ref, dbias_ref, dal_ref, ddsk_ref):
        @pl.when(pl.program_id(0) == 0)
        def _():
            dbias_ref[...] = jnp.zeros_like(dbias_ref)
            dal_ref[...] = jnp.zeros_like(dal_ref)
            ddsk_ref[...] = jnp.zeros_like(ddsk_ref)

        ex = _head_expand(n_inner)
        red = lambda v: sum(_dot(p, ex, NT) for p in _split3(v))
        raw = r_ref[...] + b_ref[...]
        dt = _softplus(raw)
        A = -jnp.exp(al_ref[...])
        i = lax.broadcasted_iota(jnp.int32, (CHUNK, CHUNK), 0)
        j = lax.broadcasted_iota(jnp.int32, (CHUNK, CHUNK), 1)
        upper = (j >= i).astype(BF16)
        da = sum(_dot(upper, p) for p in _split3(red(dcs_ref[...])))
        ddt = red(ddt_ref[...]) + da * A
        lane = lax.broadcasted_iota(jnp.int32, (CHUNK, LANES), 1)
        draw = jnp.where(lane < n_heads, ddt * jax.nn.sigmoid(raw), 0.0)
        draw_ref[...] = draw.astype(BF16)
        dbias_ref[...] += jnp.sum(draw, axis=0, keepdims=True)
        dal_ref[...] += jnp.sum(da * dt, axis=0, keepdims=True) * A
        ddsk_ref[...] += red(jnp.broadcast_to(dd_ref[0], (8, n_inner)))[0:1, :]

    wide = pl.BlockSpec((CHUNK, n_inner), lambda c: (c, 0))
    blk = pl.BlockSpec((CHUNK, LANES), lambda c: (c, 0))
    return pl.pallas_call(
        body, grid=(T // CHUNK,),
        in_specs=[wide, wide, pl.BlockSpec((1, 1, n_inner), lambda c: (c, 0, 0)), blk, _vec(LANES), _vec(LANES)],
        out_specs=[blk, _vec(LANES), _vec(LANES), _vec(LANES)],
        out_shape=[_sds((T, LANES), BF16)] + [_sds((1, LANES), F32)] * 3,
        name=name, compiler_params=_params("arbitrary"))(ddt_e, dcs_e, dd_p, dt_raw, dt_bias, a_log)


def _row2(v):
    return v.reshape(1, -1).astype(F32)


def _pad_lanes(v):
    return jnp.pad(_row2(v), ((0, 0), (0, LANES - v.shape[-1])))


class _NoExchange:
    def __init__(self, W):
        self.W, self.grads = W, {}

    def weight(self, k):
        return self.W[k]

    def carry(self, name):
        return None

    def carried(self, name, outs):
        pass

    def grad(self, k, g):
        self.grads[k] = g

    def after_in_grad(self, pieces):
        return pieces


def _local_step(x, tgt, S, small):
    T, D = x.shape

    def mm(a, b, *, name, **kw):
        comm = S.carry(name)
        if comm is None:
            return _mm(a, b, name=name, **kw)
        res, outs = _mm(a, b, name=name, comm=comm, **kw)
        S.carried(name, outs)
        return res

    def carrying(fn, *args, name):
        comm = S.carry(name)
        if comm is None:
            return fn(*args, name)
        res, outs = fn(*args, name, comm=comm)
        S.carried(name, outs)
        return res

    n_inner = 2 * D
    n_heads = n_inner // HEADDIM
    norm_mix, norm_mlp, norm_final = _row2(small["norm_mix"]), _row2(small["norm_mlp"]), _row2(small["norm_final"])
    b_gate, ssm_b, ssm_norm_w = _row2(small["b_gate"]), _row2(small["ssm_conv_b"]), _row2(small["ssm_norm_w"])
    dt_bias, a_log = _pad_lanes(small["dt_bias"]), _pad_lanes(small["A_log"])
    d_e = jnp.repeat(small["D_skip"].astype(F32), HEADDIM).reshape(1, n_inner)
    sc_w, ssm_w = small["sc_conv_w"], small["ssm_conv_w"]

    hb = _rms_fwd(x, norm_mix, "rms_mix")
    p_xbc = mm(hb, S.weight("xbc"), mode="nn", name="proj_xbc")
    p_dt = mm(hb, S.weight("dt"), mode="nn", name="proj_dt")
    p_z = mm(hb, S.weight("z"), mode="nn", name="proj_z")
    p_sc = mm(hb, S.weight("sc"), mode="nn", name="proj_sc")
    p_gate = mm(hb, S.weight("gate"), mode="nn", name="proj_gate")
    xbc = carrying(_ssm_conv_fwd, p_xbc, ssm_w, ssm_b, name="ssm_conv_fwd")
    dt_e, cs_e = _ssd_prep(p_dt, dt_bias, a_log, n_inner, "ssd_prep")
    y, states = carrying(_ssd_fwd, xbc, dt_e, cs_e, d_e, name="ssd_fwd")
    yb = carrying(_gnorm_fwd, y, p_z, ssm_norm_w, name="gnorm_fwd")
    ya = _sc_fwd(p_sc, sc_w, "sc_fwd")
    br_a = mm(ya, S.weight("bsc"), mode="nn", name="branch_sc")
    br_b = mm(yb, S.weight("bssm"), mode="nn", name="branch_ssm")
    merged = _merge_fwd(p_gate, b_gate, br_a, br_b, "merge_fwd")
    x1 = mm(merged, S.weight("out"), mode="nn", name="out_proj", extras=(x,), epi=_epi_add)
    h2 = _rms_fwd(x1, norm_mlp, "rms_mlp")
    r_act = mm(h2, S.weight("w1"), mode="nn", name="mlp_up", epi=_epi_relu2, out_dtypes=(BF16,))
    x2 = mm(r_act, S.weight("w2"), mode="nn", name="mlp_down", extras=(x1,), epi=_epi_add)
    dx2, dx2b, g_norm_final, loss_row = _final(x2, norm_final, tgt, "final")

    S.grad("w2", mm(r_act, dx2b, mode="tn", name="mlp_down_dw", out_dtypes=(BF16,)))
    da = mm(dx2b, S.weight("w2"), mode="nt", name="mlp_down_dx", extras=(r_act,), epi=_epi_relu2_bwd, out_dtypes=(BF16,))
    S.grad("w1", mm(h2, da, mode="tn", name="mlp_up_dw", out_dtypes=(BF16,)))
    dh2 = mm(da, S.weight("w1"), mode="nt", name="mlp_up_dx")
    dx1, dx1b, g_norm_mlp = _rms_bwd(x1, norm_mlp, dh2, dx2, "rms_mlp_bwd")
    S.grad("out", mm(merged, dx1b, mode="tn", name="out_proj_dw", out_dtypes=(BF16,)))
    dmerged = mm(dx1b, S.weight("out"), mode="nt", name="out_proj_dx")
    dbr_a, dbr_b, d_gate, g_b_gate = _merge_bwd(dmerged, p_gate, b_gate, br_a, br_b, "merge_bwd")
    S.grad("bssm", mm(yb, dbr_b, mode="tn", name="branch_ssm_dw", out_dtypes=(BF16,)))
    S.grad("bsc", mm(ya, dbr_a, mode="tn", name="branch_sc_dw", out_dtypes=(BF16,)))
    dyb = mm(dbr_b, S.weight("bssm"), mode="nt", name="branch_ssm_dx")
    dya = mm(dbr_a, S.weight("bsc"), mode="nt", name="branch_sc_dx")
    dy, d_z, g_ssm_norm_w = _gnorm_bwd(y, p_z, ssm_norm_w, dyb, "gnorm_bwd")
    dxs, dB, dC, ddt_e, dcs_e, dd_p = carrying(_ssd_bwd, xbc, dt_e, cs_e, d_e, states, dy, name="ssd_bwd")
    d_dt, g_dt_bias, g_a_log, g_d_skip = _ssd_post(ddt_e, dcs_e, dd_p, p_dt, dt_bias, a_log, n_heads, "ssd_post")
    d_xbc, g_ssm_w, g_ssm_b = carrying(_ssm_conv_bwd, p_xbc, ssm_w, ssm_b, dxs, dB, dC, name="ssm_conv_bwd")
    d_scB, d_scC, d_scX, g_sc_w = _sc_bwd(p_sc, sc_w, dya, "sc_bwd")
    d_sc = jnp.concatenate([d_scB, d_scC, d_scX], axis=1)
    pieces = [("sc", d_sc), ("z", d_z), ("xbc", d_xbc), ("dt", d_dt), ("gate", d_gate)]
    S.grad("win", {k: mm(hb, d, mode="tn", name="proj_dw_" + k, out_dtypes=(BF16,)) for k, d in pieces})
    pieces = S.after_in_grad(pieces)
    dh = mm([d for _, d in pieces], [S.weight(k) for k, _ in pieces], mode="nt", name="proj_dx")
    grad_x, _, g_norm_mix = _rms_bwd(x, norm_mix, dh, dx1, "rms_mix_bwd")

    g_small = dict(norm_mix=g_norm_mix, b_gate=g_b_gate, sc_conv_w=g_sc_w, ssm_conv_w=g_ssm_w, ssm_conv_b=g_ssm_b,
                   dt_bias=g_dt_bias, A_log=g_a_log, D_skip=g_d_skip, ssm_norm_w=g_ssm_norm_w, norm_mlp=g_norm_mlp,
                   norm_final=g_norm_final, loss=loss_row)
    return grad_x, g_small


class _Place:
    def __init__(self, k=0):
        x, y, c = lax.axis_index("x"), lax.axis_index("y"), lax.axis_index("c")
        self.x = 1 - x if k & 4 else x
        self.y = 1 - y if k & 2 else y
        self.c = 1 - c if k & 1 else c
        self.chip = 2 * self.x + self.y
        self.id = 2 * self.chip + self.c


ICI_PEERS = (2, 4, 6)
SIBLING = (1,)
ALL_PEERS = (1, 2, 3, 4, 5, 6, 7)


class _Comm:
    def __init__(self, arrs, out_shape, ks, src, dst, own=None, aliases=None):
        self.arrs, self.out_shape, self.ks = list(arrs), list(out_shape), tuple(ks)
        self.n = len(self.arrs)
        self.src, self.dst, self.own = src, dst, own
        self.aliases = aliases or {}
        dma = pltpu.SemaphoreType.DMA
        self.scratch = [dma((self.n, len(self.ks))), dma((self.n, len(self.ks))), dma((self.n,))]

    def _copies(self, ins, outs, sems, with_recvs):
        send_sems, recv_sems, local_sems = sems
        me = _Place()
        owns, sends, recvs = [], [], []
        for a in range(self.n):
            if self.own is not None:
                s, d = self.own(a, ins[a], outs[a], me)
                owns.append(pltpu.make_async_copy(s, d, local_sems.at[a]))
            for i, k in enumerate(self.ks):
                peer = _Place(k)
                for sender, lst in ((me, sends), (peer, recvs)) if with_recvs else ((me, sends),):
                    lst.append(pltpu.make_async_remote_copy(
                        src_ref=self.src(a, ins[a], me, peer), dst_ref=self.dst(a, outs[a], sender),
                        send_sem=send_sems.at[a, i], recv_sem=recv_sems.at[a, i],
                        device_id=(peer.x, peer.y, peer.c), device_id_type=MESH))
        return owns, sends, recvs

    def start(self, ins, outs, sems):
        owns, sends, _ = self._copies(ins, outs, sems, False)
        for cp in owns + sends:
            cp.start()

    def finish(self, ins, outs, sems):
        owns, sends, recvs = self._copies(ins, outs, sems, True)
        for cp in recvs:
            cp.wait_recv()
        for cp in sends:
            cp.wait_send()
        for cp in owns:
            cp.wait()


def _run_comm(comm, name, after=()):
    n, n_after = comm.n, len(after)

    def body(*refs):
        ins, outs, sems = refs[:n], refs[n + n_after:2 * n + n_after], refs[2 * n + n_after:]
        comm.start(ins, outs, sems)
        comm.finish(ins, outs, sems)

    return list(pl.pallas_call(body, in_specs=[ANY] * (n + n_after), out_specs=[ANY] * n, out_shape=comm.out_shape,
                               scratch_shapes=comm.scratch, input_output_aliases=dict(comm.aliases), name=name)(*comm.arrs, *after))


def _gather_ici(shards):
    return _Comm(shards, [_sds((4, 2) + s.shape, s.dtype) for s in shards], ICI_PEERS,
                 src=lambda a, i, me, p: i, dst=lambda a, o, s: o.at[s.chip, s.c], own=lambda a, i, o, me: (i, o.at[me.chip, me.c]))


def _gather_sibling(bufs):
    return _Comm(bufs, [_sds(b.shape, b.dtype) for b in bufs], SIBLING,
                 src=lambda a, i, me, p: i.at[:, me.c], dst=lambda a, o, s: o.at[:, s.c], aliases={a: a for a in range(len(bufs))})


def _scatter_sibling(parts):
    return _Comm(parts, [_sds((4,) + p.shape[2:], p.dtype) for p in parts], SIBLING,
                 src=lambda a, i, me, p: i.at[:, p.c], dst=lambda a, o, s: o)


def _scatter_ici(parts):
    return _Comm(parts, [_sds(p.shape, p.dtype) for p in parts], ICI_PEERS,
                 src=lambda a, i, me, p: i.at[p.chip], dst=lambda a, o, s: o.at[s.chip], own=lambda a, i, o, me: (i.at[me.chip], o.at[me.chip]))


HBM_SPEC = pl.BlockSpec(memory_space=pltpu.HBM)
SEM_SPEC = pl.BlockSpec(memory_space=pltpu.SEMAPHORE)
DATAFLOW = pltpu.SideEffectType.DATAFLOW_SIDE_EFFECTING


def _own_part(parts, name):
    n, R, C = parts.shape
    tr = R if R <= 256 else 256
    chip = (2 * lax.axis_index("x") + lax.axis_index("y")).astype(jnp.int32).reshape(1)

    def body(q_ref, p_ref, o_ref):
        o_ref[...] = p_ref[...]

    blk = pl.BlockSpec((1, tr, C), lambda i, q_ref: (q_ref[0], i, 0))
    spec = pltpu.PrefetchScalarGridSpec(num_scalar_prefetch=1, grid=(R // tr,), in_specs=[blk], out_specs=blk)
    return pl.pallas_call(body, grid_spec=spec, out_shape=_sds((n, R, C), parts.dtype), name=name,
                          compiler_params=_params("parallel"))(chip, parts)


def _scatter_ici_start(parts, land, name):
    def body(src_ref, land_ref, send_sems, recv_sems, src_thru, land_thru, token):
        me = _Place()
        for i, k in enumerate(ICI_PEERS):
            peer = _Place(k)
            pltpu.make_async_remote_copy(src_ref=src_ref.at[peer.chip], dst_ref=land_ref.at[me.chip], send_sem=send_sems.at[i],
                                         recv_sem=recv_sems.at[i], device_id=(peer.x, peer.y, peer.c), device_id_type=MESH).start()
        token[...] = jnp.zeros_like(token)

    dma = pltpu.SemaphoreType.DMA((len(ICI_PEERS),))
    return pl.pallas_call(
        body, name=name,
        out_shape=(dma, dma, pltpu.HBM(parts.shape, parts.dtype), pltpu.HBM(land.shape, land.dtype), _sds((8, LANES), F32)),
        in_specs=(HBM_SPEC, HBM_SPEC), out_specs=(SEM_SPEC, SEM_SPEC, HBM_SPEC, HBM_SPEC, pl.BlockSpec(memory_space=pltpu.VMEM)),
        input_output_aliases={0: 2, 1: 3}, compiler_params=pltpu.CompilerParams(has_side_effects=DATAFLOW),
    )(pltpu.with_memory_space_constraint(parts, pltpu.HBM), pltpu.with_memory_space_constraint(land, pltpu.HBM))


def _scatter_ici_wait(send_sems, recv_sems, src_thru, land_thru, after, name):
    n_after = len(after)

    def body(src_ref, land_ref, send_sems, recv_sems, *rest):
        for i, k in enumerate(ICI_PEERS):
            peer = _Place(k)
            cp = pltpu.make_async_remote_copy(src_ref=src_ref.at[peer.chip], dst_ref=land_ref.at[peer.chip], send_sem=send_sems.at[i],
                                              recv_sem=recv_sems.at[i], device_id=(peer.x, peer.y, peer.c), device_id_type=MESH)
            cp.wait_send()
            cp.wait_recv()

    return pl.pallas_call(
        body, name=name, out_shape=(pltpu.HBM(src_thru.shape, src_thru.dtype), pltpu.HBM(land_thru.shape, land_thru.dtype)),
        in_specs=(HBM_SPEC, HBM_SPEC, SEM_SPEC, SEM_SPEC) + (ANY,) * n_after, out_specs=(HBM_SPEC, HBM_SPEC),
        input_output_aliases={0: 0, 1: 1}, compiler_params=pltpu.CompilerParams(has_side_effects=DATAFLOW),
    )(src_thru, land_thru, send_sems, recv_sems, *after)[1]


def _gather_all(arrs):
    return _Comm(arrs, [_sds((N_DEV,) + a.shape, a.dtype) for a in arrs], ALL_PEERS,
                 src=lambda a, i, me, p: i, dst=lambda a, o, s: o.at[s.id], own=lambda a, i, o, me: (i, o.at[me.id]))


def _add_halves(parts, got, name):
    n, _, R, C = parts.shape
    tr = R if R <= 256 else 256
    assert R % tr == 0
    core = lax.axis_index("c").astype(jnp.int32).reshape(1)

    def body(c_ref, p_ref, g_ref, o_ref):
        o_ref[0] = (p_ref[0, 0].astype(F32) + g_ref[0].astype(F32)).astype(o_ref.dtype)

    spec = pltpu.PrefetchScalarGridSpec(
        num_scalar_prefetch=1, grid=(n, R // tr),
        in_specs=[pl.BlockSpec((1, 1, tr, C), lambda q, i, c_ref: (q, c_ref[0], i, 0)), pl.BlockSpec((1, tr, C), lambda q, i, c_ref: (q, i, 0))],
        out_specs=pl.BlockSpec((1, tr, C), lambda q, i, c_ref: (q, i, 0)))
    return pl.pallas_call(body, grid_spec=spec, out_shape=_sds((n, R, C), parts.dtype), name=name,
                          compiler_params=_params("parallel", "parallel"))(core, parts, got)


def _adam(w, m, v, gparts, name):
    R, C = w.shape
    n = gparts.shape[0]
    tr = R if R <= 256 else 128
    assert R % tr == 0
    c1 = 1.0 / (1.0 - ADAM_B1 ** ADAM_STEP)
    c2 = 1.0 / (1.0 - ADAM_B2 ** ADAM_STEP)

    def body(w_ref, m_ref, v_ref, g_ref, go_ref, d_ref, mo_ref, vo_ref):
        g = g_ref[0].astype(F32)
        for s in range(1, n):
            g = g + g_ref[s].astype(F32)
        mn = ADAM_B1 * m_ref[...] + (1.0 - ADAM_B1) * g
        vn = ADAM_B2 * v_ref[...] + (1.0 - ADAM_B2) * (g * g)
        go_ref[...] = g
        mo_ref[...] = mn
        vo_ref[...] = vn
        d_ref[...] = -ADAM_LR * ((mn * c1) / (jnp.sqrt(vn * c2) + ADAM_EPS) + ADAM_WD * w_ref[...])

    blk = pl.BlockSpec((tr, C), lambda i: (i, 0))
    return pl.pallas_call(
        body, grid=(R // tr,), in_specs=[blk, blk, blk, pl.BlockSpec((n, tr, C), lambda i: (0, i, 0))],
        out_specs=[blk] * 4, out_shape=[_sds((R, C), F32)] * 4, name=name, compiler_params=_params("parallel"))(w, m, v, gparts)


_SMALL_ORDER = ("norm_mix", "b_gate", "sc_conv_w", "ssm_conv_w", "ssm_conv_b", "dt_bias", "A_log", "D_skip", "ssm_norm_w",
                "norm_mlp", "norm_final", "loss")
_REPLICATED = ("norm_mix", "b_gate", "ssm_conv_b", "dt_bias", "A_log", "D_skip", "ssm_norm_w", "norm_mlp", "norm_final")


def _cols_to_slots(g, n):
    R = g.shape[0]
    return jnp.transpose(g.reshape(R, n, g.shape[1] // n), (1, 0, 2))


def _slots_to_cols(g):
    n, R, C = g.shape
    return jnp.transpose(g, (1, 0, 2)).reshape(R, n * C)


def kernel(x, norm_mix, w_in, b_gate, sc_conv_w, ssm_conv_w, ssm_conv_b, dt_bias, A_log, D_skip, ssm_norm_w, w_branch_sc, w_branch_ssm, w_out, norm_mlp, w_mlp1, w_mlp2, norm_final, loss_target, m_norm_mix, m_w_in, m_b_gate, m_sc_conv_w, m_ssm_conv_w, m_ssm_conv_b, m_dt_bias, m_A_log, m_D_skip, m_ssm_norm_w, m_w_branch_sc, m_w_branch_ssm, m_w_out, m_norm_mlp, m_w_mlp1, m_w_mlp2, m_norm_final, v_norm_mix, v_w_in, v_b_gate, v_sc_conv_w, v_ssm_conv_w, v_ssm_conv_b, v_dt_bias, v_A_log, v_D_skip, v_ssm_norm_w, v_w_branch_sc, v_w_branch_ssm, v_w_out, v_norm_mlp, v_w_mlp1, v_w_mlp2, v_norm_final):
    T, D = x.shape[1], x.shape[2]
    n_inner = 2 * D
    n_heads = n_inner // HEADDIM
    n_xbc = n_inner + 2 * NGROUPS * NSTATE
    me = 4 * lax.axis_index("x") + 2 * lax.axis_index("y") + lax.axis_index("c")

    o_z, o_xbc, o_dt, o_gate = 3 * D, 3 * D + n_inner, 3 * D + n_inner + n_xbc, 3 * D + n_inner + n_xbc + n_heads
    by_owner = lambda b: b.reshape((N_DEV,) + b.shape[2:])
    to_owner = lambda g: g.reshape((4, 2) + g.shape[1:])
    rows_of = lambda g: to_owner(g.reshape((N_DEV, g.shape[0] // N_DEV) + g.shape[1:]))
    cols_of = lambda g: to_owner(_cols_to_slots(g, N_DEV))

    class Schedule(_NoExchange):
        gather_ici = dict(proj_xbc=("bssm",), proj_sc=("bsc", "out"), ssm_conv_fwd=("w2",), ssd_fwd=("w1",))
        gather_sib = dict(gnorm_fwd=("bsc", "bssm", "out"), branch_ssm=("w1", "w2"))
        scatter_sib = dict(mlp_up_dx=("w2", "w1"), branch_ssm_dx=("out", "bssm", "bsc"))
        scatter_ici = dict(ssd_bwd=("out", "bssm", "bsc"), ssm_conv_bwd=("w2", "w1"))
        shards = dict(bsc=w_branch_sc, bssm=w_branch_ssm, out=w_out, w1=w_mlp1, w2=w_mlp2)

        def __init__(self):
            bufs = _run_comm(_gather_ici([w_in.astype(BF16), sc_conv_w, ssm_conv_w]), "gather_in_ici")
            bufs = _run_comm(_gather_sibling(bufs), "gather_in_sibling")
            win_full = _slots_to_cols(by_owner(bufs[0]))
            self.W = dict(sc=win_full[:, :o_z], z=win_full[:, o_z:o_xbc], xbc=win_full[:, o_xbc:o_dt],
                          dt=jnp.pad(win_full[:, o_dt:o_gate], ((0, 0), (0, LANES - n_heads))), gate=win_full[:, o_gate:])
            self.taps = dict(sc_conv_w=_slots_to_cols(by_owner(bufs[1])), ssm_conv_w=_slots_to_cols(by_owner(bufs[2])))
            self.staged, self.grads, self.halves, self.summed = {}, {}, {}, {}

        def carry(self, name):
            if name in self.gather_ici:
                return _gather_ici([self.shards[k].astype(BF16) for k in self.gather_ici[name]])
            if name in self.gather_sib:
                return _gather_sibling([self.staged.pop(k) for k in self.gather_sib[name]])
            if name in self.scatter_sib:
                return _scatter_sibling([self.grads[k] for k in self.scatter_sib[name]])
            if name in self.scatter_ici:
                return _scatter_ici([self.halves[k] for k in self.scatter_ici[name]])
            return None

        def carried(self, name, outs):
            if name in self.gather_ici:
                self.staged.update(zip(self.gather_ici[name], outs))
            elif name in self.gather_sib:
                for k, b in zip(self.gather_sib[name], outs):
                    full = by_owner(b)
                    self.W[k] = _slots_to_cols(full) if k == "w1" else full.reshape(-1, D)
            elif name in self.scatter_sib:
                for k, b in zip(self.scatter_sib[name], outs):
                    self.halves[k] = _add_halves(self.grads[k], b, "add_halves_" + k)
            else:
                self.summed.update(zip(self.scatter_ici[name], outs))

        def grad(self, k, g):
            if k == "win":
                g = cols_of(jnp.concatenate([g["sc"], g["z"], g["xbc"], g["dt"][:, :n_heads], g["gate"]], axis=1))
                got = _run_comm(_scatter_sibling([g]), "scatter_sibling_win")[0]
                half = _add_halves(g, got, "add_halves_win")
                self.in_flight = _scatter_ici_start(half, _own_part(half, "own_part_win"), "scatter_win_start")
            else:
                self.grads[k] = cols_of(g) if k == "w1" else rows_of(g)

        def after_in_grad(self, pieces):
            token = self.in_flight[4][0, 0]
            return [(k, d + token.astype(d.dtype) if k == "dt" else d) for k, d in pieces]

        def finish_in_grad(self, after):
            send_sems, recv_sems, src_thru, land_thru, _ = self.in_flight
            self.summed["win"] = _scatter_ici_wait(send_sems, recv_sems, src_thru, land_thru, after, "scatter_win_wait")

    S = Schedule()
    small = dict(norm_mix=norm_mix, b_gate=b_gate, ssm_conv_b=ssm_conv_b, dt_bias=dt_bias, A_log=A_log, D_skip=D_skip,
                 ssm_norm_w=ssm_norm_w, norm_mlp=norm_mlp, norm_final=norm_final, **S.taps)
    grad_x, g_small = _local_step(x.reshape(T, D), loss_target.reshape(T, D), S, small)

    small_flat = jnp.concatenate([g_small[k].reshape(-1) for k in _SMALL_ORDER])
    n_small = small_flat.shape[0]
    rows = -(-n_small // (8 * LANES)) * 8
    small_pack = jnp.pad(small_flat, (0, rows * LANES - n_small)).reshape(rows, LANES)

    res = {}
    big = [("w_in", "win", w_in, m_w_in, v_w_in), ("w_branch_sc", "bsc", w_branch_sc, m_w_branch_sc, v_w_branch_sc),
           ("w_branch_ssm", "bssm", w_branch_ssm, m_w_branch_ssm, v_w_branch_ssm), ("w_out", "out", w_out, m_w_out, v_w_out),
           ("w_mlp1", "w1", w_mlp1, m_w_mlp1, v_w_mlp1), ("w_mlp2", "w2", w_mlp2, m_w_mlp2, v_w_mlp2)]
    for k, gk, w, m, v in big[1:]:
        res[k] = _adam(w, m, v, S.summed[gk], "adam_" + k)
    S.finish_in_grad([grad_x] + [res[k][1] for k, *_ in big[1:]])
    k, gk, w, m, v = big[0]
    res[k] = _adam(w, m, v, S.summed[gk], "adam_" + k)
    small_parts = _run_comm(_gather_all([small_pack]), "gather_small", after=[res[k][1]])[0]

    sizes = {k: g_small[k].size for k in _SMALL_ORDER}
    offs, o = {}, 0
    for k in _SMALL_ORDER:
        offs[k] = o
        o += sizes[k]
    rep_w = dict(norm_mix=norm_mix, b_gate=b_gate, ssm_conv_b=ssm_conv_b, dt_bias=dt_bias, A_log=A_log, D_skip=D_skip,
                 ssm_norm_w=ssm_norm_w, norm_mlp=norm_mlp, norm_final=norm_final)
    rep_m = dict(norm_mix=m_norm_mix, b_gate=m_b_gate, ssm_conv_b=m_ssm_conv_b, dt_bias=m_dt_bias, A_log=m_A_log, D_skip=m_D_skip,
                 ssm_norm_w=m_ssm_norm_w, norm_mlp=m_norm_mlp, norm_final=m_norm_final)
    rep_v = dict(norm_mix=v_norm_mix, b_gate=v_b_gate, ssm_conv_b=v_ssm_conv_b, dt_bias=v_dt_bias, A_log=v_A_log, D_skip=v_D_skip,
                 ssm_norm_w=v_ssm_norm_w, norm_mlp=v_norm_mlp, norm_final=v_norm_final)

    def pack(d):
        segs = [jnp.pad(d[k].astype(F32).reshape(-1), (0, sizes[k] - d[k].size)) if k in d else jnp.zeros((sizes[k],), F32)
                for k in _SMALL_ORDER]
        return jnp.pad(jnp.concatenate(segs), (0, rows * LANES - n_small)).reshape(rows, LANES)

    sm = _adam(pack(rep_w), pack(rep_m), pack(rep_v), small_parts, "adam_small")
    sm = [s.reshape(-1) for s in sm]
    for k in _REPLICATED:
        n_k = rep_w[k].shape[0]
        res[k] = tuple(s[offs[k]:offs[k] + n_k] for s in sm)
    loss = sm[0][offs["loss"]]
    for k, w, m, v, K, full in (("sc_conv_w", sc_conv_w, m_sc_conv_w, v_sc_conv_w, SC_K, D),
                                ("ssm_conv_w", ssm_conv_w, m_ssm_conv_w, v_ssm_conv_w, SSM_K, n_xbc)):
        g_full = sm[0][offs[k]:offs[k] + K * full].reshape(K, full)
        cw = full // N_DEV
        g_mine = lax.dynamic_slice_in_dim(g_full, me * cw, cw, axis=1)
        res[k] = _adam(w, m, v, g_mine[None], "adam_" + k)

    order = ("norm_mix", "w_in", "b_gate", "sc_conv_w", "ssm_conv_w", "ssm_conv_b", "dt_bias", "A_log", "D_skip", "ssm_norm_w",
             "w_branch_sc", "w_branch_ssm", "w_out", "norm_mlp", "w_mlp1", "w_mlp2", "norm_final")
    outs = [loss, grad_x.reshape(1, T, D)]
    for j in range(4):
        outs += [res[k][j] for k in order]
    return tuple(outs)
```

```python
import functools

import jax
import jax.numpy as jnp
from jax import lax
from jax.experimental import pallas as pl
from jax.experimental.pallas import tpu as pltpu

F32 = jnp.float32
BF16 = jnp.bfloat16

EPS = 1e-6
N_DEV = 8
HEADDIM = 64
NSTATE = 128
CHUNK = 128
NGROUPS = 8
GROUP_W = 256
SC_K = 3
SSM_K = 4
LANES = 128

ADAM_LR = 0.001
ADAM_B1 = 0.9
ADAM_B2 = 0.999
ADAM_EPS = 1e-08
ADAM_WD = 0.01
ADAM_STEP = 10

NN = (((1,), (0,)), ((), ()))
NT = (((1,), (1,)), ((), ()))
TN = (((0,), (0,)), ((), ()))
_DIMS = {"nn": NN, "nt": NT, "tn": TN}

ANY = pl.BlockSpec(memory_space=pl.ANY)
MESH = pl.DeviceIdType.MESH


def _sds(shape, dtype):
    return jax.ShapeDtypeStruct(tuple(shape), dtype)


def _dot(a, b, dims=NN):
    return lax.dot_general(a, b, dims, preferred_element_type=F32)


def _dot3(a, b, dims=NN):
    return lax.dot_general(a, b, dims, preferred_element_type=F32, precision=lax.Precision.HIGH)


def _params(*sem):
    return pltpu.CompilerParams(dimension_semantics=tuple(sem))


def _call(body, *, grid, in_specs, out_specs, out_shape, args, name, sem, scratch=(), comm=None):
    if comm is None:
        outs = pl.pallas_call(body, grid=grid, in_specs=list(in_specs), out_specs=list(out_specs), out_shape=list(out_shape),
                              scratch_shapes=list(scratch), name=name, compiler_params=_params(*sem))(*args)
        return list(outs), None
    n, n_in, n_out, n_scr = comm.n, len(in_specs), len(out_shape), len(scratch)

    def wrapped(*refs):
        ins, c_in = refs[:n_in], refs[n_in:n_in + n]
        outs, c_out = refs[n_in + n:n_in + n + n_out], refs[n_in + n + n_out:n_in + 2 * n + n_out]
        rest = refs[n_in + 2 * n + n_out:]
        scr, sems = rest[:n_scr], rest[n_scr:]
        first, last = None, None
        for d, g in enumerate(grid):
            f, l = pl.program_id(d) == 0, pl.program_id(d) == g - 1
            first, last = (f, l) if first is None else (first & f, last & l)

        @pl.when(first)
        def _():
            comm.start(c_in, c_out, sems)

        body(*ins, *outs, *scr)

        @pl.when(last)
        def _():
            comm.finish(c_in, c_out, sems)

    outs = pl.pallas_call(
        wrapped, grid=grid, in_specs=list(in_specs) + [ANY] * n, out_specs=list(out_specs) + [ANY] * n,
        out_shape=list(out_shape) + comm.out_shape, scratch_shapes=list(scratch) + comm.scratch,
        input_output_aliases={n_in + i: n_out + o for i, o in comm.aliases.items()},
        name=name, compiler_params=_params(*["arbitrary"] * len(grid)))(*args, *comm.arrs)
    return list(outs[:n_out]), list(outs[n_out:])


MM_VMEM_BUDGET = 44 * 2 ** 20


def _mm_tiles(M, N, k_bytes, mn_bytes):
    best = None
    for tm in (2048, 1024, 512, 256, 128):
        for tn in (1024, 512, 256, 128):
            if M % tm or N % tn:
                continue
            need = 2 * ((tm + tn) * k_bytes + tm * tn * mn_bytes) + 4 * tm * tn * 4
            if need <= MM_VMEM_BUDGET and (best is None or (tm * tn, tm) > (best[0] * best[1], best[0])):
                best = (tm, tn)
    assert best is not None, (M, N, k_bytes, mn_bytes)
    return best


def _mm(a, b, *, mode, name, extras=(), epi=None, out_dtypes=(F32,), comm=None):
    a_list = list(a) if isinstance(a, (list, tuple)) else [a]
    b_list = list(b) if isinstance(b, (list, tuple)) else [b]
    if mode == "nn":
        M, N = a_list[0].shape[0], b_list[0].shape[1]
    elif mode == "nt":
        M, N = a_list[0].shape[0], b_list[0].shape[0]
    else:
        M, N = a_list[0].shape[1], b_list[0].shape[1]
    k_bytes = sum((av.shape[0] if mode == "tn" else av.shape[1]) * av.dtype.itemsize for av in a_list)
    mn_bytes = sum(e.dtype.itemsize for e in extras) + sum(jnp.dtype(d).itemsize for d in out_dtypes)
    tm, tn = _mm_tiles(min(M, 2048), min(N, 1024), k_bytes, mn_bytes) if M % 128 == 0 and N % 128 == 0 else (M, N)
    assert M % tm == 0 and N % tn == 0
    a_specs, b_specs = [], []
    for av, bv in zip(a_list, b_list):
        K = av.shape[0] if mode == "tn" else av.shape[1]
        a_specs.append(pl.BlockSpec((K, tm), lambda i, j: (0, i)) if mode == "tn" else pl.BlockSpec((tm, K), lambda i, j: (i, 0)))
        b_specs.append(pl.BlockSpec((tn, K), lambda i, j: (j, 0)) if mode == "nt" else pl.BlockSpec((K, tn), lambda i, j: (0, j)))
    mn_spec = pl.BlockSpec((tm, tn), lambda i, j: (i, j))
    n_p, n_ex = len(a_list), len(extras)
    dims = _DIMS[mode]

    def body(*refs):
        acc = _dot(refs[0][...], refs[n_p][...], dims)
        for p in range(1, n_p):
            acc = acc + _dot(refs[p][...], refs[n_p + p][...], dims)
        rest = refs[2 * n_p:]
        res = (acc,) if epi is None else epi(acc, *[r[...] for r in rest[:n_ex]])
        for o_ref, r in zip(rest[n_ex:], res):
            o_ref[...] = r.astype(o_ref.dtype)

    outs, carried = _call(
        body, grid=(M // tm, N // tn), in_specs=a_specs + b_specs + [mn_spec] * n_ex,
        out_specs=[mn_spec] * len(out_dtypes), out_shape=[_sds((M, N), d) for d in out_dtypes],
        args=a_list + b_list + list(extras), name=name, sem=("parallel", "parallel"), comm=comm)
    res = outs[0] if len(outs) == 1 else outs
    return res if comm is None else (res, carried)


def _epi_add(acc, r):
    return (acc + r,)


def _epi_add2(acc, r):
    s = acc + r
    return (s, s)


def _epi_relu2(acc):
    p = jnp.maximum(acc, 0.0)
    return (p * p,)


def _epi_relu2_bwd(acc, r):
    return (acc * (2.0 * jnp.sqrt(r.astype(F32))),)


def _row(tr, n):
    return pl.BlockSpec((tr, n), lambda i: (i, 0))


def _vec(n):
    return pl.BlockSpec((1, n), lambda i: (0, 0))


def _rms_fwd(x, w, name):
    T, D = x.shape
    tr = min(256, T)

    def body(x_ref, w_ref, o_ref):
        xv = x_ref[...]
        r = lax.rsqrt(jnp.mean(xv * xv, axis=-1, keepdims=True) + EPS)
        o_ref[...] = (xv * r * w_ref[...]).astype(BF16)

    return pl.pallas_call(body, grid=(T // tr,), in_specs=[_row(tr, D), _vec(D)], out_specs=_row(tr, D),
                          out_shape=_sds((T, D), BF16), name=name, compiler_params=_params("parallel"))(x, w)


def _rms_bwd(x, w, dh, dres, name):
    T, D = x.shape
    tr = min(256, T)

    def body(x_ref, w_ref, dh_ref, dres_ref, dx_ref, dxb_ref, dw_ref):
        @pl.when(pl.program_id(0) == 0)
        def _():
            dw_ref[...] = jnp.zeros_like(dw_ref)

        xv = x_ref[...]
        r = lax.rsqrt(jnp.mean(xv * xv, axis=-1, keepdims=True) + EPS)
        xh = xv * r
        dh_v = dh_ref[...]
        dw_ref[...] += jnp.sum(dh_v * xh, axis=0, keepdims=True)
        dxh = dh_v * w_ref[...]
        dx = r * (dxh - xh * jnp.mean(dxh * xh, axis=-1, keepdims=True)) + dres_ref[...]
        dx_ref[...] = dx
        dxb_ref[...] = dx.astype(BF16)

    return pl.pallas_call(
        body, grid=(T // tr,), in_specs=[_row(tr, D), _vec(D), _row(tr, D), _row(tr, D)],
        out_specs=[_row(tr, D), _row(tr, D), _vec(D)],
        out_shape=[_sds((T, D), F32), _sds((T, D), BF16), _sds((1, D), F32)],
        name=name, compiler_params=_params("arbitrary"))(x, w, dh, dres)


def _final(x2, w, tgt, name):
    T, D = x2.shape
    tr = min(256, T)

    def body(x_ref, w_ref, t_ref, dx_ref, dxb_ref, dw_ref, loss_ref):
        @pl.when(pl.program_id(0) == 0)
        def _():
            dw_ref[...] = jnp.zeros_like(dw_ref)
            loss_ref[...] = jnp.zeros_like(loss_ref)

        xv = x_ref[...]
        wv = w_ref[...]
        r = lax.rsqrt(jnp.mean(xv * xv, axis=-1, keepdims=True) + EPS)
        xh = xv * r
        err = xh * wv - t_ref[...]
        part = jnp.sum(jnp.sum(err * err, axis=1, keepdims=True), axis=0, keepdims=True) * (0.5 / D)
        loss_ref[...] += jnp.broadcast_to(part, loss_ref.shape)
        dy = err * (1.0 / D)
        dw_ref[...] += jnp.sum(dy * xh, axis=0, keepdims=True)
        dxh = dy * wv
        dx = r * (dxh - xh * jnp.mean(dxh * xh, axis=-1, keepdims=True))
        dx_ref[...] = dx
        dxb_ref[...] = dx.astype(BF16)

    return pl.pallas_call(
        body, grid=(T // tr,), in_specs=[_row(tr, D), _vec(D), _row(tr, D)],
        out_specs=[_row(tr, D), _row(tr, D), _vec(D), _vec(LANES)],
        out_shape=[_sds((T, D), F32), _sds((T, D), BF16), _sds((1, D), F32), _sds((1, LANES), F32)],
        name=name, compiler_params=_params("arbitrary"))(x2, w, tgt)


def _silu_parts(z):
    s = jax.nn.sigmoid(z)
    return z * s, s * (1.0 + z * (1.0 - s))


def _gnorm_fwd(y, z, w, name, comm=None):
    T, N = y.shape
    tr = min(256, T)

    def body(y_ref, z_ref, w_ref, o_ref):
        for g in range(N // GROUP_W):
            sl = slice(g * GROUP_W, (g + 1) * GROUP_W)
            silu, _ = _silu_parts(z_ref[:, sl])
            yz = y_ref[:, sl] * silu
            r = lax.rsqrt(jnp.mean(yz * yz, axis=-1, keepdims=True) + EPS)
            o_ref[:, sl] = (yz * r * w_ref[:, sl]).astype(BF16)

    outs, carried = _call(body, grid=(T // tr,), in_specs=[_row(tr, N), _row(tr, N), _vec(N)], out_specs=[_row(tr, N)],
                          out_shape=[_sds((T, N), BF16)], args=[y, z, w], name=name, sem=("parallel",), comm=comm)
    return outs[0] if comm is None else (outs[0], carried)


def _gnorm_bwd(y, z, w, dyb, name):
    T, N = y.shape
    tr = min(256, T)

    def body(y_ref, z_ref, w_ref, d_ref, dy_ref, dz_ref, dw_ref):
        @pl.when(pl.program_id(0) == 0)
        def _():
            dw_ref[...] = jnp.zeros_like(dw_ref)

        for g in range(N // GROUP_W):
            sl = slice(g * GROUP_W, (g + 1) * GROUP_W)
            yv = y_ref[:, sl]
            silu, dsilu = _silu_parts(z_ref[:, sl])
            yz = yv * silu
            r = lax.rsqrt(jnp.mean(yz * yz, axis=-1, keepdims=True) + EPS)
            yzh = yz * r
            d = d_ref[:, sl]
            dw_ref[:, sl] += jnp.sum(d * yzh, axis=0, keepdims=True)
            dyzh = d * w_ref[:, sl]
            dyz = r * (dyzh - yzh * jnp.mean(dyzh * yzh, axis=-1, keepdims=True))
            dy_ref[:, sl] = dyz * silu
            dz_ref[:, sl] = (dyz * yv * dsilu).astype(BF16)

    return pl.pallas_call(
        body, grid=(T // tr,), in_specs=[_row(tr, N), _row(tr, N), _vec(N), _row(tr, N)],
        out_specs=[_row(tr, N), _row(tr, N), _vec(N)],
        out_shape=[_sds((T, N), F32), _sds((T, N), BF16), _sds((1, N), F32)],
        name=name, compiler_params=_params("arbitrary"))(y, z, w, dyb)


def _merge_fwd(gate_raw, b_gate, br_a, br_b, name):
    T, D = br_a.shape
    tr = min(256, T)

    def body(g_ref, bg_ref, a_ref, b_ref, o_ref):
        g = jax.nn.sigmoid(g_ref[...] + bg_ref[...])
        o_ref[...] = (g[:, :D] * a_ref[...] + g[:, D:] * b_ref[...]).astype(BF16)

    return pl.pallas_call(body, grid=(T // tr,), in_specs=[_row(tr, 2 * D), _vec(2 * D), _row(tr, D), _row(tr, D)],
                          out_specs=_row(tr, D), out_shape=_sds((T, D), BF16), name=name,
                          compiler_params=_params("parallel"))(gate_raw, b_gate, br_a, br_b)


def _merge_bwd(dmerged, gate_raw, b_gate, br_a, br_b, name):
    T, D = br_a.shape
    tr = min(256, T)

    def body(d_ref, g_ref, bg_ref, a_ref, b_ref, da_ref, db_ref, dg_ref, dbg_ref):
        @pl.when(pl.program_id(0) == 0)
        def _():
            dbg_ref[...] = jnp.zeros_like(dbg_ref)

        g = jax.nn.sigmoid(g_ref[...] + bg_ref[...])
        d = d_ref[...]
        da_ref[...] = (d * g[:, :D]).astype(BF16)
        db_ref[...] = (d * g[:, D:]).astype(BF16)
        dg = jnp.concatenate([d * a_ref[...], d * b_ref[...]], axis=1) * g * (1.0 - g)
        dg_ref[...] = dg.astype(BF16)
        dbg_ref[...] += jnp.sum(dg, axis=0, keepdims=True)

    return pl.pallas_call(
        body, grid=(T // tr,), in_specs=[_row(tr, D), _row(tr, 2 * D), _vec(2 * D), _row(tr, D), _row(tr, D)],
        out_specs=[_row(tr, D), _row(tr, D), _row(tr, 2 * D), _vec(2 * D)],
        out_shape=[_sds((T, D), BF16), _sds((T, D), BF16), _sds((T, 2 * D), BF16), _sds((1, 2 * D), F32)],
        name=name, compiler_params=_params("arbitrary"))(dmerged, gate_raw, b_gate, br_a, br_b)


CB_W = 256
CONV_ROWS = 32
CONV_PAD = 8


def _rows_down(load, r0, s):
    if s == 0:
        return load(r0, r0 + CONV_ROWS)
    if r0 == 0:
        row = lax.broadcasted_iota(jnp.int32, (CONV_ROWS, CB_W), 0)
        return jnp.where(row >= s, pltpu.roll(load(0, CONV_ROWS), s, 0), 0.0)
    return load(r0 - s, r0 - s + CONV_ROWS)


def _conv_tile(load, taps, r0):
    K = len(taps)
    us = [_rows_down(load, r0, K - 1 - k) for k in range(K)]
    acc = us[K - 1] * taps[K - 1]
    for k in range(K - 1):
        acc = acc + us[k] * taps[k]
    return acc, us


def _conv_back_tile(scr, taps, r0):
    K = len(taps)
    du = scr[r0:r0 + CONV_ROWS, :] * taps[K - 1]
    for k in range(K - 1):
        s = K - 1 - k
        du = du + scr[r0 + s:r0 + s + CONV_ROWS, :] * taps[k]
    return du


def _fold8(v):
    return jnp.sum(v.reshape(CONV_ROWS // 8, 8, v.shape[1]), axis=0)


def _col(T, j0=0):
    return pl.BlockSpec((T, CB_W), lambda j: (0, j + j0))


def _sc_fwd(psc, w, name):
    T, D = psc.shape[0], psc.shape[1] // 3
    nb = D // CB_W

    def body(b_ref, c_ref, x_ref, w_ref, o_ref):
        taps = [w_ref[k:k + 1, :] for k in range(SC_K)]
        load = lambda a, b: c_ref[a:b, :] * x_ref[a:b, :]
        for r0 in range(0, T, CONV_ROWS):
            cu, _ = _conv_tile(load, taps, r0)
            o_ref[r0:r0 + CONV_ROWS, :] = (b_ref[r0:r0 + CONV_ROWS, :] * cu).astype(BF16)

    return pl.pallas_call(
        body, grid=(nb,), in_specs=[_col(T), _col(T, nb), _col(T, 2 * nb), pl.BlockSpec((SC_K, CB_W), lambda j: (0, j))],
        out_specs=_col(T), out_shape=_sds((T, D), BF16), name=name, compiler_params=_params("parallel"))(psc, psc, psc, w)


def _sc_bwd(psc, w, dya, name):
    T, D = psc.shape[0], psc.shape[1] // 3
    nb = D // CB_W

    def body(b_ref, c_ref, x_ref, w_ref, d_ref, db_ref, dc_ref, dx_ref, dw_ref, scr):
        taps = [w_ref[k:k + 1, :] for k in range(SC_K)]
        load = lambda a, b: c_ref[a:b, :] * x_ref[a:b, :]
        scr[T:T + CONV_PAD, :] = jnp.zeros((CONV_PAD, CB_W), F32)
        dw8 = [jnp.zeros((8, CB_W), F32)] * SC_K
        for r0 in range(0, T, CONV_ROWS):
            rows = slice(r0, r0 + CONV_ROWS)
            cu, us = _conv_tile(load, taps, r0)
            d = d_ref[rows, :]
            db_ref[rows, :] = (d * cu).astype(BF16)
            dcu = d * b_ref[rows, :]
            scr[rows, :] = dcu
            dw8 = [acc + _fold8(dcu * u) for acc, u in zip(dw8, us)]
        for k in range(SC_K):
            dw_ref[k:k + 1, :] = jnp.sum(dw8[k], axis=0, keepdims=True)
        for r0 in range(0, T, CONV_ROWS):
            rows = slice(r0, r0 + CONV_ROWS)
            du = _conv_back_tile(scr, taps, r0)
            dc_ref[rows, :] = (du * x_ref[rows, :]).astype(BF16)
            dx_ref[rows, :] = (du * c_ref[rows, :]).astype(BF16)

    wspec = pl.BlockSpec((SC_K, CB_W), lambda j: (0, j))
    return pl.pallas_call(
        body, grid=(nb,), in_specs=[_col(T), _col(T, nb), _col(T, 2 * nb), wspec, _col(T)],
        out_specs=[_col(T), _col(T), _col(T), wspec],
        out_shape=[_sds((T, D), BF16)] * 3 + [_sds((SC_K, D), F32)],
        scratch_shapes=[pltpu.VMEM((T + CONV_PAD, CB_W), F32)],
        name=name, compiler_params=_params("parallel"))(psc, psc, psc, w, dya)


def _ssm_conv_fwd(u, w, b, name, comm=None):
    T, N = u.shape

    def body(u_ref, w_ref, b_ref, o_ref):
        taps = [w_ref[k:k + 1, :] for k in range(SSM_K)]
        bias = b_ref[...]
        for r0 in range(0, T, CONV_ROWS):
            c, _ = _conv_tile(lambda a, b: u_ref[a:b, :], taps, r0)
            c = c + bias
            o_ref[r0:r0 + CONV_ROWS, :] = c * jax.nn.sigmoid(c)

    outs, carried = _call(
        body, grid=(N // CB_W,), in_specs=[_col(T), pl.BlockSpec((SSM_K, CB_W), lambda j: (0, j)), pl.BlockSpec((1, CB_W), lambda j: (0, j))],
        out_specs=[_col(T)], out_shape=[_sds((T, N), F32)], args=[u, w, b], name=name, sem=("parallel",), comm=comm)
    return outs[0] if comm is None else (outs[0], carried)


def _ssm_conv_bwd(u, w, b, dxs, dB, dC, name, comm=None):
    T, N = u.shape
    n_x, n_b = dxs.shape[1] // CB_W, dB.shape[1] // CB_W

    def body(u_ref, w_ref, b_ref, dx_ref, db_ref, dc_ref, du_ref, dw_ref, dbias_ref, scr):
        j = pl.program_id(0)
        taps = [w_ref[k:k + 1, :] for k in range(SSM_K)]
        bias = b_ref[...]
        scr[T:T + CONV_PAD, :] = jnp.zeros((CONV_PAD, CB_W), F32)
        dw8 = [jnp.zeros((8, CB_W), F32)] * SSM_K
        db8 = jnp.zeros((8, CB_W), F32)
        for r0 in range(0, T, CONV_ROWS):
            rows = slice(r0, r0 + CONV_ROWS)
            c, us = _conv_tile(lambda a, b: u_ref[a:b, :], taps, r0)
            _, dsilu = _silu_parts(c + bias)
            d = jnp.where(j < n_x, dx_ref[rows, :], jnp.where(j < n_x + n_b, db_ref[rows, :], dc_ref[rows, :])) * dsilu
            scr[rows, :] = d
            db8 = db8 + _fold8(d)
            dw8 = [acc + _fold8(d * u) for acc, u in zip(dw8, us)]
        dbias_ref[...] = jnp.sum(db8, axis=0, keepdims=True)
        for k in range(SSM_K):
            dw_ref[k:k + 1, :] = jnp.sum(dw8[k], axis=0, keepdims=True)
        for r0 in range(0, T, CONV_ROWS):
            du_ref[r0:r0 + CONV_ROWS, :] = _conv_back_tile(scr, taps, r0).astype(BF16)

    wspec = pl.BlockSpec((SSM_K, CB_W), lambda j: (0, j))
    bspec = pl.BlockSpec((1, CB_W), lambda j: (0, j))
    outs, carried = _call(
        body, grid=(N // CB_W,),
        in_specs=[_col(T), wspec, bspec,
                  pl.BlockSpec((T, CB_W), lambda j: (0, jnp.minimum(j, n_x - 1))),
                  pl.BlockSpec((T, CB_W), lambda j: (0, jnp.clip(j - n_x, 0, n_b - 1))),
                  pl.BlockSpec((T, CB_W), lambda j: (0, jnp.clip(j - n_x - n_b, 0, n_b - 1)))],
        out_specs=[_col(T), wspec, bspec],
        out_shape=[_sds((T, N), BF16), _sds((SSM_K, N), F32), _sds((1, N), F32)],
        scratch=[pltpu.VMEM((T + CONV_PAD, CB_W), F32)],
        args=[u, w, b, dxs, dB, dC], name=name, sem=("parallel",), comm=comm)
    return outs if comm is None else (outs, carried)


def _split3(v):
    hi = v.astype(BF16)
    r = v - hi.astype(F32)
    mid = r.astype(BF16)
    lo = (r - mid.astype(F32)).astype(BF16)
    return hi, mid, lo


def _head_expand(n_lanes):
    h = lax.broadcasted_iota(jnp.int32, (LANES, n_lanes), 0)
    l = lax.broadcasted_iota(jnp.int32, (LANES, n_lanes), 1)
    return (jnp.right_shift(l, HEADDIM.bit_length() - 1) == h).astype(BF16)


def _softplus(v):
    return jnp.maximum(v, 0.0) + jnp.log1p(jnp.exp(-jnp.abs(v)))


def _ssd_prep(dt_raw, dt_bias, a_log, n_inner, name):
    T = dt_raw.shape[0]

    def body(r_ref, b_ref, al_ref, dt_ref, cs_ref):
        dt = _softplus(r_ref[...] + b_ref[...])
        a = dt * (-jnp.exp(al_ref[...]))
        i = lax.broadcasted_iota(jnp.int32, (CHUNK, CHUNK), 0)
        j = lax.broadcasted_iota(jnp.int32, (CHUNK, CHUNK), 1)
        tri = (j <= i).astype(BF16)
        cs = sum(_dot(tri, p) for p in _split3(a))
        ex = _head_expand(n_inner)
        dt_ref[...] = sum(_dot(p, ex) for p in _split3(dt))
        cs_ref[...] = sum(_dot(p, ex) for p in _split3(cs))

    blk = pl.BlockSpec((CHUNK, LANES), lambda c: (c, 0))
    out = pl.BlockSpec((CHUNK, n_inner), lambda c: (c, 0))
    return pl.pallas_call(body, grid=(T // CHUNK,), in_specs=[blk, _vec(LANES), _vec(LANES)], out_specs=[out, out],
                          out_shape=[_sds((T, n_inner), F32)] * 2, name=name, compiler_params=_params("parallel"))(dt_raw, dt_bias, a_log)


def _pair_terms(cs_p):
    lane = lax.broadcasted_iota(jnp.int32, (CHUNK, CHUNK), 1)
    sub = lax.broadcasted_iota(jnp.int32, (CHUNK, CHUNK), 0)
    csT = cs_p.T
    Ls = []
    for k in range(2):
        col = jnp.sum(jnp.where(lane == k * HEADDIM, cs_p, 0.0), axis=1, keepdims=True)
        rowv = csT[k * HEADDIM:k * HEADDIM + 1, :]
        Ls.append(jnp.exp(jnp.where(sub >= lane, col - rowv, -jnp.inf)))
    return Ls, jnp.exp(csT[:, CHUNK - 1:CHUNK])


def _block_diag(xp):
    lane = lax.broadcasted_iota(jnp.int32, xp.shape, 1)
    return jnp.concatenate([jnp.where(lane < HEADDIM, xp, 0.0), jnp.where(lane >= HEADDIM, xp, 0.0)], axis=0)


SSD_GROUPS_PER_STEP = 8


def _ssd_specs(T, n_inner):
    nc, gs = T // CHUNK, SSD_GROUPS_PER_STEP
    bo, co = n_inner // (gs * NSTATE), (n_inner + NGROUPS * NSTATE) // (gs * NSTATE)
    assert NGROUPS % gs == 0 and n_inner % (gs * NSTATE) == 0 and (NGROUPS * NSTATE) % (gs * NSTATE) == 0
    g_blk = lambda f: pl.BlockSpec((CHUNK, gs * GROUP_W), lambda c, s: (f(c), s))
    b_blk = lambda f: pl.BlockSpec((CHUNK, gs * NSTATE), lambda c, s: (f(c), bo + s))
    c_blk = lambda f: pl.BlockSpec((CHUNK, gs * NSTATE), lambda c, s: (f(c), co + s))
    return nc, g_blk, b_blk, c_blk


def _ssd_fwd(xbc, dt_e, cs_e, d_e, name, comm=None):
    T = xbc.shape[0]
    n_inner = dt_e.shape[1]
    nc, g_blk, b_blk, c_blk = _ssd_specs(T, n_inner)
    ident = lambda c: c

    gs = SSD_GROUPS_PER_STEP

    def body(xs_ref, b_ref, c_ref, dt_ref, cs_ref, d_ref, y_ref, p_ref, st):
        c, s = pl.program_id(0), pl.program_id(1)

        @pl.when(c == 0)
        def _():
            for gi in range(gs):
                st[s * gs + gi] = jnp.zeros((GROUP_W, NSTATE), F32)

        for gi in range(gs):
            g = s * gs + gi
            gw, gn = slice(gi * GROUP_W, (gi + 1) * GROUP_W), slice(gi * NSTATE, (gi + 1) * NSTATE)
            P = st[g]
            p_ref[0, gi] = P
            xs, dt, cs = xs_ref[:, gw], dt_ref[:, gw], cs_ref[:, gw]
            Bf, Cf = b_ref[:, gn], c_ref[:, gn]
            CBm = _dot3(Cf, Bf, NT)
            X = xs * dt
            decay = jnp.exp(cs[CHUNK - 1:CHUNK, :] - cs)
            y_off = _dot3(Cf, P, NT) * jnp.exp(cs)
            ys, ecl = [], []
            for pr in range(2):
                sl = slice(pr * LANES, (pr + 1) * LANES)
                Ls, e_last = _pair_terms(cs[:, sl])
                ecl.append(e_last)
                Mcat = jnp.concatenate([CBm * L for L in Ls], axis=1)
                ys.append(_dot3(Mcat, _block_diag(X[:, sl])))
            y_ref[:, gw] = jnp.concatenate(ys, axis=1) + y_off + xs * d_ref[:, gw]
            S = _dot3(X * decay, Bf, TN)
            st[g] = P * jnp.concatenate(ecl, axis=0) + S

    p_blk = pl.BlockSpec((1, gs, GROUP_W, NSTATE), lambda c, s: (c, s, 0, 0))
    outs, carried = _call(
        body, grid=(nc, NGROUPS // gs),
        in_specs=[g_blk(ident), b_blk(ident), c_blk(ident), g_blk(ident), g_blk(ident), pl.BlockSpec((1, gs * GROUP_W), lambda c, s: (0, s))],
        out_specs=[g_blk(ident), p_blk],
        out_shape=[_sds((T, n_inner), F32), _sds((nc, NGROUPS, GROUP_W, NSTATE), F32)],
        scratch=[pltpu.VMEM((NGROUPS, GROUP_W, NSTATE), F32)],
        args=[xbc, xbc, xbc, dt_e, cs_e, d_e], name=name, sem=("arbitrary", "arbitrary"), comm=comm)
    return outs if comm is None else (outs, carried)


def _ssd_bwd(xbc, dt_e, cs_e, d_e, states, dy, name, comm=None):
    T = xbc.shape[0]
    n_inner = dt_e.shape[1]
    nc, g_blk, b_blk, c_blk = _ssd_specs(T, n_inner)
    rev = lambda c: nc - 1 - c

    gs = SSD_GROUPS_PER_STEP

    def body(xs_ref, b_ref, c_ref, dt_ref, cs_ref, d_ref, p_ref, pn_ref, dy_ref,
             dxs_ref, db_ref, dc_ref, ddt_ref, dcs_ref, dd_ref, dst):
        cc, s = pl.program_id(0), pl.program_id(1)

        @pl.when(cc == 0)
        def _():
            for gi in range(gs):
                dst[s * gs + gi] = jnp.zeros((GROUP_W, NSTATE), F32)

        for gi in range(gs):
            one_group(s * gs + gi, gi, xs_ref, b_ref, c_ref, dt_ref, cs_ref, d_ref, p_ref, pn_ref, dy_ref,
                      dxs_ref, db_ref, dc_ref, ddt_ref, dcs_ref, dd_ref, dst)

    def one_group(g, gi, xs_ref, b_ref, c_ref, dt_ref, cs_ref, d_ref, p_ref, pn_ref, dy_ref,
                  dxs_ref, db_ref, dc_ref, ddt_ref, dcs_ref, dd_ref, dst):
        gw, gn = slice(gi * GROUP_W, (gi + 1) * GROUP_W), slice(gi * NSTATE, (gi + 1) * NSTATE)
        dS = dst[g]
        P, Pn = p_ref[0, gi], pn_ref[0, gi]
        xs, dt, cs, dY = xs_ref[:, gw], dt_ref[:, gw], cs_ref[:, gw], dy_ref[:, gw]
        Bf, Cf = b_ref[:, gn], c_ref[:, gn]
        Bb, Cb = Bf.astype(BF16), Cf.astype(BF16)
        X = xs * dt
        ecs = jnp.exp(cs)
        decay = jnp.exp(cs[CHUNK - 1:CHUNK, :] - cs)
        CBm = _dot3(Cf, Bf, NT)
        dYe = dY * ecs
        dP_off = _dot3(dYe, Cf, TN)
        dC = _dot(dYe.astype(BF16), P.astype(BF16))
        dcs = dYe * _dot3(Cf, P, NT)
        Xd = X * decay
        dB = _dot(Xd.astype(BF16), dS.astype(BF16))
        E = _dot3(Bf, dS, NT)
        dX = E * decay
        dcs = dcs - E * Xd
        R = _dot3(jnp.ones((8, NSTATE), F32), dS * Pn, NT)
        sub_g = lax.broadcasted_iota(jnp.int32, (CHUNK, GROUP_W), 0)
        dcs = dcs + jnp.where(sub_g == CHUNK - 1, R[0:1, :], 0.0)
        lane = lax.broadcasted_iota(jnp.int32, (CHUNK, CHUNK), 1)
        sub = lax.broadcasted_iota(jnp.int32, (CHUNK, CHUNK), 0)
        dCB = jnp.zeros((CHUNK, CHUNK), F32)
        dXs, dcss, ecl = [], [], []
        for pr in range(2):
            sl = slice(pr * LANES, (pr + 1) * LANES)
            Ls, e_last = _pair_terms(cs[:, sl])
            ecl.append(e_last)
            dYp = dY[:, sl]
            dMcat = _dot3(dYp, _block_diag(X[:, sl]), NT)
            Mcat = jnp.concatenate([CBm * L for L in Ls], axis=1)
            dXt = _dot3(Mcat, dYp, TN)
            dXs.append(jnp.where(lane < HEADDIM, dXt[:CHUNK], dXt[CHUNK:]))
            colacc = jnp.zeros((CHUNK, CHUNK), F32)
            rowacc = jnp.zeros((CHUNK, CHUNK), F32)
            for k in range(2):
                dG = dMcat[:, k * CHUNK:(k + 1) * CHUNK] * Ls[k]
                dCB = dCB + dG
                Q = dG * CBm
                colacc = colacc + jnp.where(lane == k * HEADDIM, jnp.sum(Q, axis=1, keepdims=True), 0.0)
                rowacc = rowacc + jnp.where(sub == k * HEADDIM, jnp.sum(Q, axis=0, keepdims=True), 0.0)
            dcss.append(colacc - rowacc.T)
        dX = dX + jnp.concatenate(dXs, axis=1)
        dcs = dcs + jnp.concatenate(dcss, axis=1)
        dCBb = dCB.astype(BF16)
        dc_ref[:, gn] = dC + _dot(dCBb, Bb)
        db_ref[:, gn] = dB + _dot(dCBb, Cb, TN)
        dxs_ref[:, gw] = dX * dt + dY * d_ref[:, gw]
        ddt_ref[:, gw] = dX * xs
        dcs_ref[:, gw] = dcs
        dd_ref[0, :, gw] = jnp.sum(dY * xs, axis=0, keepdims=True)
        dst[g] = dS * jnp.concatenate(ecl, axis=0) + dP_off

    p_blk = pl.BlockSpec((1, gs, GROUP_W, NSTATE), lambda c, s: (nc - 1 - c, s, 0, 0))
    pn_blk = pl.BlockSpec((1, gs, GROUP_W, NSTATE), lambda c, s: (jnp.minimum(nc - c, nc - 1), s, 0, 0))
    st_blk = pl.BlockSpec((CHUNK, gs * NSTATE), lambda c, s: (nc - 1 - c, s))
    outs, carried = _call(
        body, grid=(nc, NGROUPS // gs),
        in_specs=[g_blk(rev), b_blk(rev), c_blk(rev), g_blk(rev), g_blk(rev), pl.BlockSpec((1, gs * GROUP_W), lambda c, s: (0, s)),
                  p_blk, pn_blk, g_blk(rev)],
        out_specs=[g_blk(rev), st_blk, st_blk, g_blk(rev), g_blk(rev), pl.BlockSpec((1, 1, gs * GROUP_W), lambda c, s: (nc - 1 - c, 0, s))],
        out_shape=[_sds((T, n_inner), F32), _sds((T, NGROUPS * NSTATE), F32), _sds((T, NGROUPS * NSTATE), F32),
                   _sds((T, n_inner), F32), _sds((T, n_inner), F32), _sds((nc, 1, n_inner), F32)],
        scratch=[pltpu.VMEM((NGROUPS, GROUP_W, NSTATE), F32)],
        args=[xbc, xbc, xbc, dt_e, cs_e, d_e, states, states, dy], name=name, sem=("arbitrary", "arbitrary"), comm=comm)
    return outs if comm is None else (outs, carried)


def _ssd_post(ddt_e, dcs_e, dd_p, dt_raw, dt_bias, a_log, n_heads, name):
    T, n_inner = ddt_e.shape

    def body(ddt_ref, dcs_ref, dd_ref, r_ref, b_ref, al_ref, draw_ref, dbias_ref, dal_ref, ddsk_ref):
        @pl.when(pl.program_id(0) == 0)
        def _():
            dbias_ref[...] = jnp.zeros_like(dbias_ref)
            dal_ref[...] = jnp.zeros_like(dal_ref)
            ddsk_ref[...] = jnp.zeros_like(ddsk_ref)

        ex = _head_expand(n_inner)
        red = lambda v: sum(_dot(p, ex, NT) for p in _split3(v))
        raw = r_ref[...] + b_ref[...]
        dt = _softplus(raw)
        A = -jnp.exp(al_ref[...])
        i = lax.broadcasted_iota(jnp.int32, (CHUNK, CHUNK), 0)
        j = lax.broadcasted_iota(jnp.int32, (CHUNK, CHUNK), 1)
        upper = (j >= i).astype(BF16)
        da = sum(_dot(upper, p) for p in _split3(red(dcs_ref[...])))
        ddt = red(ddt_ref[...]) + da * A
        lane = lax.broadcasted_iota(jnp.int32, (CHUNK, LANES), 1)
        draw = jnp.where(lane < n_heads, ddt * jax.nn.sigmoid(raw), 0.0)
        draw_ref[...] = draw.astype(BF16)
        dbias_ref[...] += jnp.sum(draw, axis=0, keepdims=True)
        dal_ref[...] += jnp.sum(da * dt, axis=0, keepdims=True) * A
        ddsk_ref[...] += red(jnp.broadcast_to(dd_ref[0], (8, n_inner)))[0:1, :]

    wide = pl.BlockSpec((CHUNK, n_inner), lambda c: (c, 0))
    blk = pl.BlockSpec((CHUNK, LANES), lambda c: (c, 0))
    return pl.pallas_call(
        body, grid=(T // CHUNK,),
        in_specs=[wide, wide, pl.BlockSpec((1, 1, n_inner), lambda c: (c, 0, 0)), blk, _vec(LANES), _vec(LANES)],
        out_specs=[blk, _vec(LANES), _vec(LANES), _vec(LANES)],
        out_shape=[_sds((T, LANES), BF16)] + [_sds((1, LANES), F32)] * 3,
        name=name, compiler_params=_params("arbitrary"))(ddt_e, dcs_e, dd_p, dt_raw, dt_bias, a_log)


def _row2(v):
    return v.reshape(1, -1).astype(F32)


def _pad_lanes(v):
    return jnp.pad(_row2(v), ((0, 0), (0, LANES - v.shape[-1])))


class _NoExchange:
    def __init__(self, W):
        self.W, self.grads = W, {}

    def weight(self, k):
        return self.W[k]

    def carry(self, name):
        return None

    def carried(self, name, outs):
        pass

    def grad(self, k, g):
        self.grads[k] = g

    def after_in_grad(self, pieces):
        return pieces


def _local_step(x, tgt, S, small):
    T, D = x.shape

    def mm(a, b, *, name, **kw):
        comm = S.carry(name)
        if comm is None:
            return _mm(a, b, name=name, **kw)
        res, outs = _mm(a, b, name=name, comm=comm, **kw)
        S.carried(name, outs)
        return res

    def carrying(fn, *args, name):
        comm = S.carry(name)
        if comm is None:
            return fn(*args, name)
        res, outs = fn(*args, name, comm=comm)
        S.carried(name, outs)
        return res

    n_inner = 2 * D
    n_heads = n_inner // HEADDIM
    norm_mix, norm_mlp, norm_final = _row2(small["norm_mix"]), _row2(small["norm_mlp"]), _row2(small["norm_final"])
    b_gate, ssm_b, ssm_norm_w = _row2(small["b_gate"]), _row2(small["ssm_conv_b"]), _row2(small["ssm_norm_w"])
    dt_bias, a_log = _pad_lanes(small["dt_bias"]), _pad_lanes(small["A_log"])
    d_e = jnp.repeat(small["D_skip"].astype(F32), HEADDIM).reshape(1, n_inner)
    sc_w, ssm_w = small["sc_conv_w"], small["ssm_conv_w"]

    hb = _rms_fwd(x, norm_mix, "rms_mix")
    p_xbc = mm(hb, S.weight("xbc"), mode="nn", name="proj_xbc")
    p_dt = mm(hb, S.weight("dt"), mode="nn", name="proj_dt")
    p_z = mm(hb, S.weight("z"), mode="nn", name="proj_z")
    p_sc = mm(hb, S.weight("sc"), mode="nn", name="proj_sc")
    p_gate = mm(hb, S.weight("gate"), mode="nn", name="proj_gate")
    xbc = carrying(_ssm_conv_fwd, p_xbc, ssm_w, ssm_b, name="ssm_conv_fwd")
    dt_e, cs_e = _ssd_prep(p_dt, dt_bias, a_log, n_inner, "ssd_prep")
    y, states = carrying(_ssd_fwd, xbc, dt_e, cs_e, d_e, name="ssd_fwd")
    yb = carrying(_gnorm_fwd, y, p_z, ssm_norm_w, name="gnorm_fwd")
    ya = _sc_fwd(p_sc, sc_w, "sc_fwd")
    br_a = mm(ya, S.weight("bsc"), mode="nn", name="branch_sc")
    br_b = mm(yb, S.weight("bssm"), mode="nn", name="branch_ssm")
    merged = _merge_fwd(p_gate, b_gate, br_a, br_b, "merge_fwd")
    x1 = mm(merged, S.weight("out"), mode="nn", name="out_proj", extras=(x,), epi=_epi_add)
    h2 = _rms_fwd(x1, norm_mlp, "rms_mlp")
    r_act = mm(h2, S.weight("w1"), mode="nn", name="mlp_up", epi=_epi_relu2, out_dtypes=(BF16,))
    x2 = mm(r_act, S.weight("w2"), mode="nn", name="mlp_down", extras=(x1,), epi=_epi_add)
    dx2, dx2b, g_norm_final, loss_row = _final(x2, norm_final, tgt, "final")

    S.grad("w2", mm(r_act, dx2b, mode="tn", name="mlp_down_dw", out_dtypes=(BF16,)))
    da = mm(dx2b, S.weight("w2"), mode="nt", name="mlp_down_dx", extras=(r_act,), epi=_epi_relu2_bwd, out_dtypes=(BF16,))
    S.grad("w1", mm(h2, da, mode="tn", name="mlp_up_dw", out_dtypes=(BF16,)))
    dh2 = mm(da, S.weight("w1"), mode="nt", name="mlp_up_dx")
    dx1, dx1b, g_norm_mlp = _rms_bwd(x1, norm_mlp, dh2, dx2, "rms_mlp_bwd")
    S.grad("out", mm(merged, dx1b, mode="tn", name="out_proj_dw", out_dtypes=(BF16,)))
    dmerged = mm(dx1b, S.weight("out"), mode="nt", name="out_proj_dx")
    dbr_a, dbr_b, d_gate, g_b_gate = _merge_bwd(dmerged, p_gate, b_gate, br_a, br_b, "merge_bwd")
    S.grad("bssm", mm(yb, dbr_b, mode="tn", name="branch_ssm_dw", out_dtypes=(BF16,)))
    S.grad("bsc", mm(ya, dbr_a, mode="tn", name="branch_sc_dw", out_dtypes=(BF16,)))
    dyb = mm(dbr_b, S.weight("bssm"), mode="nt", name="branch_ssm_dx")
    dya = mm(dbr_a, S.weight("bsc"), mode="nt", name="branch_sc_dx")
    dy, d_z, g_ssm_norm_w = _gnorm_bwd(y, p_z, ssm_norm_w, dyb, "gnorm_bwd")
    dxs, dB, dC, ddt_e, dcs_e, dd_p = carrying(_ssd_bwd, xbc, dt_e, cs_e, d_e, states, dy, name="ssd_bwd")
    d_dt, g_dt_bias, g_a_log, g_d_skip = _ssd_post(ddt_e, dcs_e, dd_p, p_dt, dt_bias, a_log, n_heads, "ssd_post")
    d_xbc, g_ssm_w, g_ssm_b = carrying(_ssm_conv_bwd, p_xbc, ssm_w, ssm_b, dxs, dB, dC, name="ssm_conv_bwd")
    d_scB, d_scC, d_scX, g_sc_w = _sc_bwd(p_sc, sc_w, dya, "sc_bwd")
    d_sc = jnp.concatenate([d_scB, d_scC, d_scX], axis=1)
    pieces = [("sc", d_sc), ("z", d_z), ("xbc", d_xbc), ("dt", d_dt), ("gate", d_gate)]
    S.grad("win", {k: mm(hb, d, mode="tn", name="proj_dw_" + k, out_dtypes=(BF16,)) for k, d in pieces})
    pieces = S.after_in_grad(pieces)
    dh = mm([d for _, d in pieces], [S.weight(k) for k, _ in pieces], mode="nt", name="proj_dx")
    grad_x, _, g_norm_mix = _rms_bwd(x, norm_mix, dh, dx1, "rms_mix_bwd")

    g_small = dict(norm_mix=g_norm_mix, b_gate=g_b_gate, sc_conv_w=g_sc_w, ssm_conv_w=g_ssm_w, ssm_conv_b=g_ssm_b,
                   dt_bias=g_dt_bias, A_log=g_a_log, D_skip=g_d_skip, ssm_norm_w=g_ssm_norm_w, norm_mlp=g_norm_mlp,
                   norm_final=g_norm_final, loss=loss_row)
    return grad_x, g_small


class _Place:
    def __init__(self, k=0):
        x, y, c = lax.axis_index("x"), lax.axis_index("y"), lax.axis_index("c")
        self.x = 1 - x if k & 4 else x
        self.y = 1 - y if k & 2 else y
        self.c = 1 - c if k & 1 else c
        self.chip = 2 * self.x + self.y
        self.id = 2 * self.chip + self.c


ICI_PEERS = (2, 4, 6)
SIBLING = (1,)
ALL_PEERS = (1, 2, 3, 4, 5, 6, 7)


class _Comm:
    def __init__(self, arrs, out_shape, ks, src, dst, own=None, aliases=None):
        self.arrs, self.out_shape, self.ks = list(arrs), list(out_shape), tuple(ks)
        self.n = len(self.arrs)
        self.src, self.dst, self.own = src, dst, own
        self.aliases = aliases or {}
        dma = pltpu.SemaphoreType.DMA
        self.scratch = [dma((self.n, len(self.ks))), dma((self.n, len(self.ks))), dma((self.n,))]

    def _copies(self, ins, outs, sems, with_recvs):
        send_sems, recv_sems, local_sems = sems
        me = _Place()
        owns, sends, recvs = [], [], []
        for a in range(self.n):
            if self.own is not None:
                s, d = self.own(a, ins[a], outs[a], me)
                owns.append(pltpu.make_async_copy(s, d, local_sems.at[a]))
            for i, k in enumerate(self.ks):
                peer = _Place(k)
                for sender, lst in ((me, sends), (peer, recvs)) if with_recvs else ((me, sends),):
                    lst.append(pltpu.make_async_remote_copy(
                        src_ref=self.src(a, ins[a], me, peer), dst_ref=self.dst(a, outs[a], sender),
                        send_sem=send_sems.at[a, i], recv_sem=recv_sems.at[a, i],
                        device_id=(peer.x, peer.y, peer.c), device_id_type=MESH))
        return owns, sends, recvs

    def start(self, ins, outs, sems):
        owns, sends, _ = self._copies(ins, outs, sems, False)
        for cp in owns + sends:
            cp.start()

    def finish(self, ins, outs, sems):
        owns, sends, recvs = self._copies(ins, outs, sems, True)
        for cp in recvs:
            cp.wait_recv()
        for cp in sends:
            cp.wait_send()
        for cp in owns:
            cp.wait()


def _run_comm(comm, name, after=()):
    n, n_after = comm.n, len(after)

    def body(*refs):
        ins, outs, sems = refs[:n], refs[n + n_after:2 * n + n_after], refs[2 * n + n_after:]
        comm.start(ins, outs, sems)
        comm.finish(ins, outs, sems)

    return list(pl.pallas_call(body, in_specs=[ANY] * (n + n_after), out_specs=[ANY] * n, out_shape=comm.out_shape,
                               scratch_shapes=comm.scratch, input_output_aliases=dict(comm.aliases), name=name)(*comm.arrs, *after))


def _gather_ici(shards):
    return _Comm(shards, [_sds((4, 2) + s.shape, s.dtype) for s in shards], ICI_PEERS,
                 src=lambda a, i, me, p: i, dst=lambda a, o, s: o.at[s.chip, s.c], own=lambda a, i, o, me: (i, o.at[me.chip, me.c]))


def _gather_sibling(bufs):
    return _Comm(bufs, [_sds(b.shape, b.dtype) for b in bufs], SIBLING,
                 src=lambda a, i, me, p: i.at[:, me.c], dst=lambda a, o, s: o.at[:, s.c], aliases={a: a for a in range(len(bufs))})


def _scatter_sibling(parts):
    return _Comm(parts, [_sds((4,) + p.shape[2:], p.dtype) for p in parts], SIBLING,
                 src=lambda a, i, me, p: i.at[:, p.c], dst=lambda a, o, s: o)


def _scatter_ici(parts):
    return _Comm(parts, [_sds(p.shape, p.dtype) for p in parts], ICI_PEERS,
                 src=lambda a, i, me, p: i.at[p.chip], dst=lambda a, o, s: o.at[s.chip], own=lambda a, i, o, me: (i.at[me.chip], o.at[me.chip]))


HBM_SPEC = pl.BlockSpec(memory_space=pltpu.HBM)
SEM_SPEC = pl.BlockSpec(memory_space=pltpu.SEMAPHORE)
DATAFLOW = pltpu.SideEffectType.DATAFLOW_SIDE_EFFECTING


def _own_part(parts, name):
    n, R, C = parts.shape
    tr = R if R <= 256 else 256
    chip = (2 * lax.axis_index("x") + lax.axis_index("y")).astype(jnp.int32).reshape(1)

    def body(q_ref, p_ref, o_ref):
        o_ref[...] = p_ref[...]

    blk = pl.BlockSpec((1, tr, C), lambda i, q_ref: (q_ref[0], i, 0))
    spec = pltpu.PrefetchScalarGridSpec(num_scalar_prefetch=1, grid=(R // tr,), in_specs=[blk], out_specs=blk)
    return pl.pallas_call(body, grid_spec=spec, out_shape=_sds((n, R, C), parts.dtype), name=name,
                          compiler_params=_params("parallel"))(chip, parts)


def _scatter_ici_start(parts, land, name):
    def body(src_ref, land_ref, send_sems, recv_sems, src_thru, land_thru, token):
        me = _Place()
        for i, k in enumerate(ICI_PEERS):
            peer = _Place(k)
            pltpu.make_async_remote_copy(src_ref=src_ref.at[peer.chip], dst_ref=land_ref.at[me.chip], send_sem=send_sems.at[i],
                                         recv_sem=recv_sems.at[i], device_id=(peer.x, peer.y, peer.c), device_id_type=MESH).start()
        token[...] = jnp.zeros_like(token)

    dma = pltpu.SemaphoreType.DMA((len(ICI_PEERS),))
    return pl.pallas_call(
        body, name=name,
        out_shape=(dma, dma, pltpu.HBM(parts.shape, parts.dtype), pltpu.HBM(land.shape, land.dtype), _sds((8, LANES), F32)),
        in_specs=(HBM_SPEC, HBM_SPEC), out_specs=(SEM_SPEC, SEM_SPEC, HBM_SPEC, HBM_SPEC, pl.BlockSpec(memory_space=pltpu.VMEM)),
        input_output_aliases={0: 2, 1: 3}, compiler_params=pltpu.CompilerParams(has_side_effects=DATAFLOW),
    )(pltpu.with_memory_space_constraint(parts, pltpu.HBM), pltpu.with_memory_space_constraint(land, pltpu.HBM))


def _scatter_ici_wait(send_sems, recv_sems, src_thru, land_thru, after, name):
    n_after = len(after)

    def body(src_ref, land_ref, send_sems, recv_sems, *rest):
        for i, k in enumerate(ICI_PEERS):
            peer = _Place(k)
            cp = pltpu.make_async_remote_copy(src_ref=src_ref.at[peer.chip], dst_ref=land_ref.at[peer.chip], send_sem=send_sems.at[i],
                                              recv_sem=recv_sems.at[i], device_id=(peer.x, peer.y, peer.c), device_id_type=MESH)
            cp.wait_send()
            cp.wait_recv()

    return pl.pallas_call(
        body, name=name, out_shape=(pltpu.HBM(src_thru.shape, src_thru.dtype), pltpu.HBM(land_thru.shape, land_thru.dtype)),
        in_specs=(HBM_SPEC, HBM_SPEC, SEM_SPEC, SEM_SPEC) + (ANY,) * n_after, out_specs=(HBM_SPEC, HBM_SPEC),
        input_output_aliases={0: 0, 1: 1}, compiler_params=pltpu.CompilerParams(has_side_effects=DATAFLOW),
    )(src_thru, land_thru, send_sems, recv_sems, *after)[1]


def _gather_all(arrs):
    return _Comm(arrs, [_sds((N_DEV,) + a.shape, a.dtype) for a in arrs], ALL_PEERS,
                 src=lambda a, i, me, p: i, dst=lambda a, o, s: o.at[s.id], own=lambda a, i, o, me: (i, o.at[me.id]))


def _add_halves(parts, got, name):
    n, _, R, C = parts.shape
    tr = R if R <= 256 else 256
    assert R % tr == 0
    core = lax.axis_index("c").astype(jnp.int32).reshape(1)

    def body(c_ref, p_ref, g_ref, o_ref):
        o_ref[0] = (p_ref[0, 0].astype(F32) + g_ref[0].astype(F32)).astype(o_ref.dtype)

    spec = pltpu.PrefetchScalarGridSpec(
        num_scalar_prefetch=1, grid=(n, R // tr),
        in_specs=[pl.BlockSpec((1, 1, tr, C), lambda q, i, c_ref: (q, c_ref[0], i, 0)), pl.BlockSpec((1, tr, C), lambda q, i, c_ref: (q, i, 0))],
        out_specs=pl.BlockSpec((1, tr, C), lambda q, i, c_ref: (q, i, 0)))
    return pl.pallas_call(body, grid_spec=spec, out_shape=_sds((n, R, C), parts.dtype), name=name,
                          compiler_params=_params("parallel", "parallel"))(core, parts, got)


def _adam(w, m, v, gparts, name):
    R, C = w.shape
    n = gparts.shape[0]
    tr = R if R <= 256 else 128
    assert R % tr == 0
    c1 = 1.0 / (1.0 - ADAM_B1 ** ADAM_STEP)
    c2 = 1.0 / (1.0 - ADAM_B2 ** ADAM_STEP)

    def body(w_ref, m_ref, v_ref, g_ref, go_ref, d_ref, mo_ref, vo_ref):
        g = g_ref[0].astype(F32)
        for s in range(1, n):
            g = g + g_ref[s].astype(F32)
        mn = ADAM_B1 * m_ref[...] + (1.0 - ADAM_B1) * g
        vn = ADAM_B2 * v_ref[...] + (1.0 - ADAM_B2) * (g * g)
        go_ref[...] = g
        mo_ref[...] = mn
        vo_ref[...] = vn
        d_ref[...] = -ADAM_LR * ((mn * c1) / (jnp.sqrt(vn * c2) + ADAM_EPS) + ADAM_WD * w_ref[...])

    blk = pl.BlockSpec((tr, C), lambda i: (i, 0))
    return pl.pallas_call(
        body, grid=(R // tr,), in_specs=[blk, blk, blk, pl.BlockSpec((n, tr, C), lambda i: (0, i, 0))],
        out_specs=[blk] * 4, out_shape=[_sds((R, C), F32)] * 4, name=name, compiler_params=_params("parallel"))(w, m, v, gparts)


_SMALL_ORDER = ("norm_mix", "b_gate", "sc_conv_w", "ssm_conv_w", "ssm_conv_b", "dt_bias", "A_log", "D_skip", "ssm_norm_w",
                "norm_mlp", "norm_final", "loss")
_REPLICATED = ("norm_mix", "b_gate", "ssm_conv_b", "dt_bias", "A_log", "D_skip", "ssm_norm_w", "norm_mlp", "norm_final")


def _cols_to_slots(g, n):
    R = g.shape[0]
    return jnp.transpose(g.reshape(R, n, g.shape[1] // n), (1, 0, 2))


def _slots_to_cols(g):
    n, R, C = g.shape
    return jnp.transpose(g, (1, 0, 2)).reshape(R, n * C)


def kernel(x, norm_mix, w_in, b_gate, sc_conv_w, ssm_conv_w, ssm_conv_b, dt_bias, A_log, D_skip, ssm_norm_w, w_branch_sc, w_branch_ssm, w_out, norm_mlp, w_mlp1, w_mlp2, norm_final, loss_target, m_norm_mix, m_w_in, m_b_gate, m_sc_conv_w, m_ssm_conv_w, m_ssm_conv_b, m_dt_bias, m_A_log, m_D_skip, m_ssm_norm_w, m_w_branch_sc, m_w_branch_ssm, m_w_out, m_norm_mlp, m_w_mlp1, m_w_mlp2, m_norm_final, v_norm_mix, v_w_in, v_b_gate, v_sc_conv_w, v_ssm_conv_w, v_ssm_conv_b, v_dt_bias, v_A_log, v_D_skip, v_ssm_norm_w, v_w_branch_sc, v_w_branch_ssm, v_w_out, v_norm_mlp, v_w_mlp1, v_w_mlp2, v_norm_final):
    T, D = x.shape[1], x.shape[2]
    n_inner = 2 * D
    n_heads = n_inner // HEADDIM
    n_xbc = n_inner + 2 * NGROUPS * NSTATE
    me = 4 * lax.axis_index("x") + 2 * lax.axis_index("y") + lax.axis_index("c")

    o_z, o_xbc, o_dt, o_gate = 3 * D, 3 * D + n_inner, 3 * D + n_inner + n_xbc, 3 * D + n_inner + n_xbc + n_heads
    by_owner = lambda b: b.reshape((N_DEV,) + b.shape[2:])
    to_owner = lambda g: g.reshape((4, 2) + g.shape[1:])
    rows_of = lambda g: to_owner(g.reshape((N_DEV, g.shape[0] // N_DEV) + g.shape[1:]))
    cols_of = lambda g: to_owner(_cols_to_slots(g, N_DEV))

    class Schedule(_NoExchange):
        gather_ici = dict(proj_xbc=("bssm",), proj_sc=("bsc", "out"), ssm_conv_fwd=("w2",), ssd_fwd=("w1",))
        gather_sib = dict(gnorm_fwd=("bsc", "bssm", "out"), branch_ssm=("w1", "w2"))
        scatter_sib = dict(mlp_up_dx=("w2", "w1"), branch_ssm_dx=("out", "bssm", "bsc"))
        scatter_ici = dict(ssd_bwd=("out", "bssm", "bsc"), ssm_conv_bwd=("w2", "w1"))
        shards = dict(bsc=w_branch_sc, bssm=w_branch_ssm, out=w_out, w1=w_mlp1, w2=w_mlp2)

        def __init__(self):
            bufs = _run_comm(_gather_ici([w_in.astype(BF16), sc_conv_w, ssm_conv_w]), "gather_in_ici")
            bufs = _run_comm(_gather_sibling(bufs), "gather_in_sibling")
            win_full = _slots_to_cols(by_owner(bufs[0]))
            self.W = dict(sc=win_full[:, :o_z], z=win_full[:, o_z:o_xbc], xbc=win_full[:, o_xbc:o_dt],
                          dt=jnp.pad(win_full[:, o_dt:o_gate], ((0, 0), (0, LANES - n_heads))), gate=win_full[:, o_gate:])
            self.taps = dict(sc_conv_w=_slots_to_cols(by_owner(bufs[1])), ssm_conv_w=_slots_to_cols(by_owner(bufs[2])))
            self.staged, self.grads, self.halves, self.summed = {}, {}, {}, {}

        def carry(self, name):
            if name in self.gather_ici:
                return _gather_ici([self.shards[k].astype(BF16) for k in self.gather_ici[name]])
            if name in self.gather_sib:
                return _gather_sibling([self.staged.pop(k) for k in self.gather_sib[name]])
            if name in self.scatter_sib:
                return _scatter_sibling([self.grads[k] for k in self.scatter_sib[name]])
            if name in self.scatter_ici:
                return _scatter_ici([self.halves[k] for k in self.scatter_ici[name]])
            return None

        def carried(self, name, outs):
            if name in self.gather_ici:
                self.staged.update(zip(self.gather_ici[name], outs))
            elif name in self.gather_sib:
                for k, b in zip(self.gather_sib[name], outs):
                    full = by_owner(b)
                    self.W[k] = _slots_to_cols(full) if k == "w1" else full.reshape(-1, D)
            elif name in self.scatter_sib:
                for k, b in zip(self.scatter_sib[name], outs):
                    self.halves[k] = _add_halves(self.grads[k], b, "add_halves_" + k)
            else:
                self.summed.update(zip(self.scatter_ici[name], outs))

        def grad(self, k, g):
            if k == "win":
                g = cols_of(jnp.concatenate([g["sc"], g["z"], g["xbc"], g["dt"][:, :n_heads], g["gate"]], axis=1))
                got = _run_comm(_scatter_sibling([g]), "scatter_sibling_win")[0]
                half = _add_halves(g, got, "add_halves_win")
                self.in_flight = _scatter_ici_start(half, _own_part(half, "own_part_win"), "scatter_win_start")
            else:
                self.grads[k] = cols_of(g) if k == "w1" else rows_of(g)

        def after_in_grad(self, pieces):
            token = self.in_flight[4][0, 0]
            return [(k, d + token.astype(d.dtype) if k == "dt" else d) for k, d in pieces]

        def finish_in_grad(self, after):
            send_sems, recv_sems, src_thru, land_thru, _ = self.in_flight
            self.summed["win"] = _scatter_ici_wait(send_sems, recv_sems, src_thru, land_thru, after, "scatter_win_wait")

    S = Schedule()
    small = dict(norm_mix=norm_mix, b_gate=b_gate, ssm_conv_b=ssm_conv_b, dt_bias=dt_bias, A_log=A_log, D_skip=D_skip,
                 ssm_norm_w=ssm_norm_w, norm_mlp=norm_mlp, norm_final=norm_final, **S.taps)
    grad_x, g_small = _local_step(x.reshape(T, D), loss_target.reshape(T, D), S, small)

    small_flat = jnp.concatenate([g_small[k].reshape(-1) for k in _SMALL_ORDER])
    n_small = small_flat.shape[0]
    rows = -(-n_small // (8 * LANES)) * 8
    small_pack = jnp.pad(small_flat, (0, rows * LANES - n_small)).reshape(rows, LANES)

    res = {}
    big = [("w_in", "win", w_in, m_w_in, v_w_in), ("w_branch_sc", "bsc", w_branch_sc, m_w_branch_sc, v_w_branch_sc),
           ("w_branch_ssm", "bssm", w_branch_ssm, m_w_branch_ssm, v_w_branch_ssm), ("w_out", "out", w_out, m_w_out, v_w_out),
           ("w_mlp1", "w1", w_mlp1, m_w_mlp1, v_w_mlp1), ("w_mlp2", "w2", w_mlp2, m_w_mlp2, v_w_mlp2)]
    for k, gk, w, m, v in big[1:]:
        res[k] = _adam(w, m, v, S.summed[gk], "adam_" + k)
    S.finish_in_grad([grad_x] + [res[k][1] for k, *_ in big[1:]])
    k, gk, w, m, v = big[0]
    res[k] = _adam(w, m, v, S.summed[gk], "adam_" + k)
    small_parts = _run_comm(_gather_all([small_pack]), "gather_small", after=[res[k][1]])[0]

    sizes = {k: g_small[k].size for k in _SMALL_ORDER}
    offs, o = {}, 0
    for k in _SMALL_ORDER:
        offs[k] = o
        o += sizes[k]
    rep_w = dict(norm_mix=norm_mix, b_gate=b_gate, ssm_conv_b=ssm_conv_b, dt_bias=dt_bias, A_log=A_log, D_skip=D_skip,
                 ssm_norm_w=ssm_norm_w, norm_mlp=norm_mlp, norm_final=norm_final)
    rep_m = dict(norm_mix=m_norm_mix, b_gate=m_b_gate, ssm_conv_b=m_ssm_conv_b, dt_bias=m_dt_bias, A_log=m_A_log, D_skip=m_D_skip,
                 ssm_norm_w=m_ssm_norm_w, norm_mlp=m_norm_mlp, norm_final=m_norm_final)
    rep_v = dict(norm_mix=v_norm_mix, b_gate=v_b_gate, ssm_conv_b=v_ssm_conv_b, dt_bias=v_dt_bias, A_log=v_A_log, D_skip=v_D_skip,
                 ssm_norm_w=v_ssm_norm_w, norm_mlp=v_norm_mlp, norm_final=v_norm_final)

    def pack(d):
        segs = [jnp.pad(d[k].astype(F32).reshape(-1), (0, sizes[k] - d[k].size)) if k in d else jnp.zeros((sizes[k],), F32)
                for k in _SMALL_ORDER]
        return jnp.pad(jnp.concatenate(segs), (0, rows * LANES - n_small)).reshape(rows, LANES)

    sm = _adam(pack(rep_w), pack(rep_m), pack(rep_v), small_parts, "adam_small")
    sm = [s.reshape(-1) for s in sm]
    for k in _REPLICATED:
        n_k = rep_w[k].shape[0]
        res[k] = tuple(s[offs[k]:offs[k] + n_k] for s in sm)
    loss = sm[0][offs["loss"]]
    for k, w, m, v, K, full in (("sc_conv_w", sc_conv_w, m_sc_conv_w, v_sc_conv_w, SC_K, D),
                                ("ssm_conv_w", ssm_conv_w, m_ssm_conv_w, v_ssm_conv_w, SSM_K, n_xbc)):
        g_full = sm[0][offs[k]:offs[k] + K * full].reshape(K, full)
        cw = full // N_DEV
        g_mine = lax.dynamic_slice_in_dim(g_full, me * cw, cw, axis=1)
        res[k] = _adam(w, m, v, g_mine[None], "adam_" + k)

    order = ("norm_mix", "w_in", "b_gate", "sc_conv_w", "ssm_conv_w", "ssm_conv_b", "dt_bias", "A_log", "D_skip", "ssm_norm_w",
             "w_branch_sc", "w_branch_ssm", "w_out", "norm_mlp", "w_mlp1", "w_mlp2", "norm_final")
    outs = [loss, grad_x.reshape(1, T, D)]
    for j in range(4):
        outs += [res[k][j] for k in order]
    return tuple(outs)
```

```python
import functools

import jax
import jax.numpy as jnp
from jax import lax
from jax.experimental import pallas as pl
from jax.experimental.pallas import tpu as pltpu

F32 = jnp.float32
BF16 = jnp.bfloat16

EPS = 1e-6
N_DEV = 8
HEADDIM = 64
NSTATE = 128
CHUNK = 128
NGROUPS = 8
GROUP_W = 256
SC_K = 3
SSM_K = 4
LANES = 128

ADAM_LR = 0.001
ADAM_B1 = 0.9
ADAM_B2 = 0.999
ADAM_EPS = 1e-08
ADAM_WD = 0.01
ADAM_STEP = 10

NN = (((1,), (0,)), ((), ()))
NT = (((1,), (1,)), ((), ()))
TN = (((0,), (0,)), ((), ()))
_DIMS = {"nn": NN, "nt": NT, "tn": TN}

ANY = pl.BlockSpec(memory_space=pl.ANY)
MESH = pl.DeviceIdType.MESH


def _sds(shape, dtype):
    return jax.ShapeDtypeStruct(tuple(shape), dtype)


def _dot(a, b, dims=NN):
    return lax.dot_general(a, b, dims, preferred_element_type=F32)


def _dot3(a, b, dims=NN):
    return lax.dot_general(a, b, dims, preferred_element_type=F32, precision=lax.Precision.HIGH)


def _params(*sem):
    return pltpu.CompilerParams(dimension_semantics=tuple(sem))


def _call(body, *, grid, in_specs, out_specs, out_shape, args, name, sem, scratch=(), comm=None):
    if comm is None:
        outs = pl.pallas_call(body, grid=grid, in_specs=list(in_specs), out_specs=list(out_specs), out_shape=list(out_shape),
                              scratch_shapes=list(scratch), name=name, compiler_params=_params(*sem))(*args)
        return list(outs), None
    n, n_in, n_out, n_scr = comm.n, len(in_specs), len(out_shape), len(scratch)

    def wrapped(*refs):
        ins, c_in = refs[:n_in], refs[n_in:n_in + n]
        outs, c_out = refs[n_in + n:n_in + n + n_out], refs[n_in + n + n_out:n_in + 2 * n + n_out]
        rest = refs[n_in + 2 * n + n_out:]
        scr, sems = rest[:n_scr], rest[n_scr:]
        first, last = None, None
        for d, g in enumerate(grid):
            f, l = pl.program_id(d) == 0, pl.program_id(d) == g - 1
            first, last = (f, l) if first is None else (first & f, last & l)

        @pl.when(first)
        def _():
            comm.start(c_in, c_out, sems)

        body(*ins, *outs, *scr)

        @pl.when(last)
        def _():
            comm.finish(c_in, c_out, sems)

    outs = pl.pallas_call(
        wrapped, grid=grid, in_specs=list(in_specs) + [ANY] * n, out_specs=list(out_specs) + [ANY] * n,
        out_shape=list(out_shape) + comm.out_shape, scratch_shapes=list(scratch) + comm.scratch,
        input_output_aliases={n_in + i: n_out + o for i, o in comm.aliases.items()},
        name=name, compiler_params=_params(*["arbitrary"] * len(grid)))(*args, *comm.arrs)
    return list(outs[:n_out]), list(outs[n_out:])


MM_VMEM_BUDGET = 44 * 2 ** 20


def _mm_tiles(M, N, k_bytes, mn_bytes):
    best = None
    for tm in (2048, 1024, 512, 256, 128):
        for tn in (1024, 512, 256, 128):
            if M % tm or N % tn:
                continue
            need = 2 * ((tm + tn) * k_bytes + tm * tn * mn_bytes) + 4 * tm * tn * 4
            if need <= MM_VMEM_BUDGET and (best is None or (tm * tn, tm) > (best[0] * best[1], best[0])):
                best = (tm, tn)
    assert best is not None, (M, N, k_bytes, mn_bytes)
    return best


def _mm(a, b, *, mode, name, extras=(), epi=None, out_dtypes=(F32,), comm=None):
    a_list = list(a) if isinstance(a, (list, tuple)) else [a]
    b_list = list(b) if isinstance(b, (list, tuple)) else [b]
    if mode == "nn":
        M, N = a_list[0].shape[0], b_list[0].shape[1]
    elif mode == "nt":
        M, N = a_list[0].shape[0], b_list[0].shape[0]
    else:
        M, N = a_list[0].shape[1], b_list[0].shape[1]
    k_bytes = sum((av.shape[0] if mode == "tn" else av.shape[1]) * av.dtype.itemsize for av in a_list)
    mn_bytes = sum(e.dtype.itemsize for e in extras) + sum(jnp.dtype(d).itemsize for d in out_dtypes)
    tm, tn = _mm_tiles(min(M, 2048), min(N, 1024), k_bytes, mn_bytes) if M % 128 == 0 and N % 128 == 0 else (M, N)
    assert M % tm == 0 and N % tn == 0
    a_specs, b_specs = [], []
    for av, bv in zip(a_list, b_list):
        K = av.shape[0] if mode == "tn" else av.shape[1]
        a_specs.append(pl.BlockSpec((K, tm), lambda i, j: (0, i)) if mode == "tn" else pl.BlockSpec((tm, K), lambda i, j: (i, 0)))
        b_specs.append(pl.BlockSpec((tn, K), lambda i, j: (j, 0)) if mode == "nt" else pl.BlockSpec((K, tn), lambda i, j: (0, j)))
    mn_spec = pl.BlockSpec((tm, tn), lambda i, j: (i, j))
    n_p, n_ex = len(a_list), len(extras)
    dims = _DIMS[mode]

    def body(*refs):
        acc = _dot(refs[0][...], refs[n_p][...], dims)
        for p in range(1, n_p):
            acc = acc + _dot(refs[p][...], refs[n_p + p][...], dims)
        rest = refs[2 * n_p:]
        res = (acc,) if epi is None else epi(acc, *[r[...] for r in rest[:n_ex]])
        for o_ref, r in zip(rest[n_ex:], res):
            o_ref[...] = r.astype(o_ref.dtype)

    outs, carried = _call(
        body, grid=(M // tm, N // tn), in_specs=a_specs + b_specs + [mn_spec] * n_ex,
        out_specs=[mn_spec] * len(out_dtypes), out_shape=[_sds((M, N), d) for d in out_dtypes],
        args=a_list + b_list + list(extras), name=name, sem=("parallel", "parallel"), comm=comm)
    res = outs[0] if len(outs) == 1 else outs
    return res if comm is None else (res, carried)


def _epi_add(acc, r):
    return (acc + r,)


def _epi_add2(acc, r):
    s = acc + r
    return (s, s)


def _epi_relu2(acc):
    p = jnp.maximum(acc, 0.0)
    return (p * p,)


def _epi_relu2_bwd(acc, r):
    return (acc * (2.0 * jnp.sqrt(r.astype(F32))),)


def _row(tr, n):
    return pl.BlockSpec((tr, n), lambda i: (i, 0))


def _vec(n):
    return pl.BlockSpec((1, n), lambda i: (0, 0))


def _rms_fwd(x, w, name):
    T, D = x.shape
    tr = min(256, T)

    def body(x_ref, w_ref, o_ref):
        xv = x_ref[...]
        r = lax.rsqrt(jnp.mean(xv * xv, axis=-1, keepdims=True) + EPS)
        o_ref[...] = (xv * r * w_ref[...]).astype(BF16)

    return pl.pallas_call(body, grid=(T // tr,), in_specs=[_row(tr, D), _vec(D)], out_specs=_row(tr, D),
                          out_shape=_sds((T, D), BF16), name=name, compiler_params=_params("parallel"))(x, w)


def _rms_bwd(x, w, dh, dres, name):
    T, D = x.shape
    tr = min(256, T)

    def body(x_ref, w_ref, dh_ref, dres_ref, dx_ref, dxb_ref, dw_ref):
        @pl.when(pl.program_id(0) == 0)
        def _():
            dw_ref[...] = jnp.zeros_like(dw_ref)

        xv = x_ref[...]
        r = lax.rsqrt(jnp.mean(xv * xv, axis=-1, keepdims=True) + EPS)
        xh = xv * r
        dh_v = dh_ref[...]
        dw_ref[...] += jnp.sum(dh_v * xh, axis=0, keepdims=True)
        dxh = dh_v * w_ref[...]
        dx = r * (dxh - xh * jnp.mean(dxh * xh, axis=-1, keepdims=True)) + dres_ref[...]
        dx_ref[...] = dx
        dxb_ref[...] = dx.astype(BF16)

    return pl.pallas_call(
        body, grid=(T // tr,), in_specs=[_row(tr, D), _vec(D), _row(tr, D), _row(tr, D)],
        out_specs=[_row(tr, D), _row(tr, D), _vec(D)],
        out_shape=[_sds((T, D), F32), _sds((T, D), BF16), _sds((1, D), F32)],
        name=name, compiler_params=_params("arbitrary"))(x, w, dh, dres)


def _final(x2, w, tgt, name):
    T, D = x2.shape
    tr = min(256, T)

    def body(x_ref, w_ref, t_ref, dx_ref, dxb_ref, dw_ref, loss_ref):
        @pl.when(pl.program_id(0) == 0)
        def _():
            dw_ref[...] = jnp.zeros_like(dw_ref)
            loss_ref[...] = jnp.zeros_like(loss_ref)

        xv = x_ref[...]
        wv = w_ref[...]
        r = lax.rsqrt(jnp.mean(xv * xv, axis=-1, keepdims=True) + EPS)
        xh = xv * r
        err = xh * wv - t_ref[...]
        part = jnp.sum(jnp.sum(err * err, axis=1, keepdims=True), axis=0, keepdims=True) * (0.5 / D)
        loss_ref[...] += jnp.broadcast_to(part, loss_ref.shape)
        dy = err * (1.0 / D)
        dw_ref[...] += jnp.sum(dy * xh, axis=0, keepdims=True)
        dxh = dy * wv
        dx = r * (dxh - xh * jnp.mean(dxh * xh, axis=-1, keepdims=True))
        dx_ref[...] = dx
        dxb_ref[...] = dx.astype(BF16)

    return pl.pallas_call(
        body, grid=(T // tr,), in_specs=[_row(tr, D), _vec(D), _row(tr, D)],
        out_specs=[_row(tr, D), _row(tr, D), _vec(D), _vec(LANES)],
        out_shape=[_sds((T, D), F32), _sds((T, D), BF16), _sds((1, D), F32), _sds((1, LANES), F32)],
        name=name, compiler_params=_params("arbitrary"))(x2, w, tgt)


def _silu_parts(z):
    s = jax.nn.sigmoid(z)
    return z * s, s * (1.0 + z * (1.0 - s))


def _gnorm_fwd(y, z, w, name, comm=None):
    T, N = y.shape
    tr = min(256, T)

    def body(y_ref, z_ref, w_ref, o_ref):
        for g in range(N // GROUP_W):
            sl = slice(g * GROUP_W, (g + 1) * GROUP_W)
            silu, _ = _silu_parts(z_ref[:, sl])
            yz = y_ref[:, sl] * silu
            r = lax.rsqrt(jnp.mean(yz * yz, axis=-1, keepdims=True) + EPS)
            o_ref[:, sl] = (yz * r * w_ref[:, sl]).astype(BF16)

    outs, carried = _call(body, grid=(T // tr,), in_specs=[_row(tr, N), _row(tr, N), _vec(N)], out_specs=[_row(tr, N)],
                          out_shape=[_sds((T, N), BF16)], args=[y, z, w], name=name, sem=("parallel",), comm=comm)
    return outs[0] if comm is None else (outs[0], carried)


def _gnorm_bwd(y, z, w, dyb, name):
    T, N = y.shape
    tr = min(256, T)

    def body(y_ref, z_ref, w_ref, d_ref, dy_ref, dz_ref, dw_ref):
        @pl.when(pl.program_id(0) == 0)
        def _():
            dw_ref[...] = jnp.zeros_like(dw_ref)

        for g in range(N // GROUP_W):
            sl = slice(g * GROUP_W, (g + 1) * GROUP_W)
            yv = y_ref[:, sl]
            silu, dsilu = _silu_parts(z_ref[:, sl])
            yz = yv * silu
            r = lax.rsqrt(jnp.mean(yz * yz, axis=-1, keepdims=True) + EPS)
            yzh = yz * r
            d = d_ref[:, sl]
            dw_ref[:, sl] += jnp.sum(d * yzh, axis=0, keepdims=True)
            dyzh = d * w_ref[:, sl]
            dyz = r * (dyzh - yzh * jnp.mean(dyzh * yzh, axis=-1, keepdims=True))
            dy_ref[:, sl] = dyz * silu
            dz_ref[:, sl] = (dyz * yv * dsilu).astype(BF16)

    return pl.pallas_call(
        body, grid=(T // tr,), in_specs=[_row(tr, N), _row(tr, N), _vec(N), _row(tr, N)],
        out_specs=[_row(tr, N), _row(tr, N), _vec(N)],
        out_shape=[_sds((T, N), F32), _sds((T, N), BF16), _sds((1, N), F32)],
        name=name, compiler_params=_params("arbitrary"))(y, z, w, dyb)


def _merge_fwd(gate_raw, b_gate, br_a, br_b, name):
    T, D = br_a.shape
    tr = min(256, T)

    def body(g_ref, bg_ref, a_ref, b_ref, o_ref):
        g = jax.nn.sigmoid(g_ref[...] + bg_ref[...])
        o_ref[...] = (g[:, :D] * a_ref[...] + g[:, D:] * b_ref[...]).astype(BF16)

    return pl.pallas_call(body, grid=(T // tr,), in_specs=[_row(tr, 2 * D), _vec(2 * D), _row(tr, D), _row(tr, D)],
                          out_specs=_row(tr, D), out_shape=_sds((T, D), BF16), name=name,
                          compiler_params=_params("parallel"))(gate_raw, b_gate, br_a, br_b)


def _merge_bwd(dmerged, gate_raw, b_gate, br_a, br_b, name):
    T, D = br_a.shape
    tr = min(256, T)

    def body(d_ref, g_ref, bg_ref, a_ref, b_ref, da_ref, db_ref, dg_ref, dbg_ref):
        @pl.when(pl.program_id(0) == 0)
        def _():
            dbg_ref[...] = jnp.zeros_like(dbg_ref)

        g = jax.nn.sigmoid(g_ref[...] + bg_ref[...])
        d = d_ref[...]
        da_ref[...] = (d * g[:, :D]).astype(BF16)
        db_ref[...] = (d * g[:, D:]).astype(BF16)
        dg = jnp.concatenate([d * a_ref[...], d * b_ref[...]], axis=1) * g * (1.0 - g)
        dg_ref[...] = dg.astype(BF16)
        dbg_ref[...] += jnp.sum(dg, axis=0, keepdims=True)

    return pl.pallas_call(
        body, grid=(T // tr,), in_specs=[_row(tr, D), _row(tr, 2 * D), _vec(2 * D), _row(tr, D), _row(tr, D)],
        out_specs=[_row(tr, D), _row(tr, D), _row(tr, 2 * D), _vec(2 * D)],
        out_shape=[_sds((T, D), BF16), _sds((T, D), BF16), _sds((T, 2 * D), BF16), _sds((1, 2 * D), F32)],
        name=name, compiler_params=_params("arbitrary"))(dmerged, gate_raw, b_gate, br_a, br_b)


CB_W = 256
CONV_ROWS = 32
CONV_PAD = 8


def _rows_down(load, r0, s):
    if s == 0:
        return load(r0, r0 + CONV_ROWS)
    if r0 == 0:
        row = lax.broadcasted_iota(jnp.int32, (CONV_ROWS, CB_W), 0)
        return jnp.where(row >= s, pltpu.roll(load(0, CONV_ROWS), s, 0), 0.0)
    return load(r0 - s, r0 - s + CONV_ROWS)


def _conv_tile(load, taps, r0):
    K = len(taps)
    us = [_rows_down(load, r0, K - 1 - k) for k in range(K)]
    acc = us[K - 1] * taps[K - 1]
    for k in range(K - 1):
        acc = acc + us[k] * taps[k]
    return acc, us


def _conv_back_tile(scr, taps, r0):
    K = len(taps)
    du = scr[r0:r0 + CONV_ROWS, :] * taps[K - 1]
    for k in range(K - 1):
        s = K - 1 - k
        du = du + scr[r0 + s:r0 + s + CONV_ROWS, :] * taps[k]
    return du


def _fold8(v):
    return jnp.sum(v.reshape(CONV_ROWS // 8, 8, v.shape[1]), axis=0)


def _col(T, j0=0):
    return pl.BlockSpec((T, CB_W), lambda j: (0, j + j0))


def _sc_fwd(psc, w, name):
    T, D = psc.shape[0], psc.shape[1] // 3
    nb = D // CB_W

    def body(b_ref, c_ref, x_ref, w_ref, o_ref):
        taps = [w_ref[k:k + 1, :] for k in range(SC_K)]
        load = lambda a, b: c_ref[a:b, :] * x_ref[a:b, :]
        for r0 in range(0, T, CONV_ROWS):
            cu, _ = _conv_tile(load, taps, r0)
            o_ref[r0:r0 + CONV_ROWS, :] = (b_ref[r0:r0 + CONV_ROWS, :] * cu).astype(BF16)

    return pl.pallas_call(
        body, grid=(nb,), in_specs=[_col(T), _col(T, nb), _col(T, 2 * nb), pl.BlockSpec((SC_K, CB_W), lambda j: (0, j))],
        out_specs=_col(T), out_shape=_sds((T, D), BF16), name=name, compiler_params=_params("parallel"))(psc, psc, psc, w)


def _sc_bwd(psc, w, dya, name):
    T, D = psc.shape[0], psc.shape[1] // 3
    nb = D // CB_W

    def body(b_ref, c_ref, x_ref, w_ref, d_ref, db_ref, dc_ref, dx_ref, dw_ref, scr):
        taps = [w_ref[k:k + 1, :] for k in range(SC_K)]
        load = lambda a, b: c_ref[a:b, :] * x_ref[a:b, :]
        scr[T:T + CONV_PAD, :] = jnp.zeros((CONV_PAD, CB_W), F32)
        dw8 = [jnp.zeros((8, CB_W), F32)] * SC_K
        for r0 in range(0, T, CONV_ROWS):
            rows = slice(r0, r0 + CONV_ROWS)
            cu, us = _conv_tile(load, taps, r0)
            d = d_ref[rows, :]
            db_ref[rows, :] = (d * cu).astype(BF16)
            dcu = d * b_ref[rows, :]
            scr[rows, :] = dcu
            dw8 = [acc + _fold8(dcu * u) for acc, u in zip(dw8, us)]
        for k in range(SC_K):
            dw_ref[k:k + 1, :] = jnp.sum(dw8[k], axis=0, keepdims=True)
        for r0 in range(0, T, CONV_ROWS):
            rows = slice(r0, r0 + CONV_ROWS)
            du = _conv_back_tile(scr, taps, r0)
            dc_ref[rows, :] = (du * x_ref[rows, :]).astype(BF16)
            dx_ref[rows, :] = (du * c_ref[rows, :]).astype(BF16)

    wspec = pl.BlockSpec((SC_K, CB_W), lambda j: (0, j))
    return pl.pallas_call(
        body, grid=(nb,), in_specs=[_col(T), _col(T, nb), _col(T, 2 * nb), wspec, _col(T)],
        out_specs=[_col(T), _col(T), _col(T), wspec],
        out_shape=[_sds((T, D), BF16)] * 3 + [_sds((SC_K, D), F32)],
        scratch_shapes=[pltpu.VMEM((T + CONV_PAD, CB_W), F32)],
        name=name, compiler_params=_params("parallel"))(psc, psc, psc, w, dya)


def _ssm_conv_fwd(u, w, b, name, comm=None):
    T, N = u.shape

    def body(u_ref, w_ref, b_ref, o_ref):
        taps = [w_ref[k:k + 1, :] for k in range(SSM_K)]
        bias = b_ref[...]
        for r0 in range(0, T, CONV_ROWS):
            c, _ = _conv_tile(lambda a, b: u_ref[a:b, :], taps, r0)
            c = c + bias
            o_ref[r0:r0 + CONV_ROWS, :] = c * jax.nn.sigmoid(c)

    outs, carried = _call(
        body, grid=(N // CB_W,), in_specs=[_col(T), pl.BlockSpec((SSM_K, CB_W), lambda j: (0, j)), pl.BlockSpec((1, CB_W), lambda j: (0, j))],
        out_specs=[_col(T)], out_shape=[_sds((T, N), F32)], args=[u, w, b], name=name, sem=("parallel",), comm=comm)
    return outs[0] if comm is None else (outs[0], carried)


def _ssm_conv_bwd(u, w, b, dxs, dB, dC, name, comm=None):
    T, N = u.shape
    n_x, n_b = dxs.shape[1] // CB_W, dB.shape[1] // CB_W

    def body(u_ref, w_ref, b_ref, dx_ref, db_ref, dc_ref, du_ref, dw_ref, dbias_ref, scr):
        j = pl.program_id(0)
        taps = [w_ref[k:k + 1, :] for k in range(SSM_K)]
        bias = b_ref[...]
        scr[T:T + CONV_PAD, :] = jnp.zeros((CONV_PAD, CB_W), F32)
        dw8 = [jnp.zeros((8, CB_W), F32)] * SSM_K
        db8 = jnp.zeros((8, CB_W), F32)
        for r0 in range(0, T, CONV_ROWS):
            rows = slice(r0, r0 + CONV_ROWS)
            c, us = _conv_tile(lambda a, b: u_ref[a:b, :], taps, r0)
            _, dsilu = _silu_parts(c + bias)
            d = jnp.where(j < n_x, dx_ref[rows, :], jnp.where(j < n_x + n_b, db_ref[rows, :], dc_ref[rows, :])) * dsilu
            scr[rows, :] = d
            db8 = db8 + _fold8(d)
            dw8 = [acc + _fold8(d * u) for acc, u in zip(dw8, us)]
        dbias_ref[...] = jnp.sum(db8, axis=0, keepdims=True)
        for k in range(SSM_K):
            dw_ref[k:k + 1, :] = jnp.sum(dw8[k], axis=0, keepdims=True)
        for r0 in range(0, T, CONV_ROWS):
            du_ref[r0:r0 + CONV_ROWS, :] = _conv_back_tile(scr, taps, r0).astype(BF16)

    wspec = pl.BlockSpec((SSM_K, CB_W), lambda j: (0, j))
    bspec = pl.BlockSpec((1, CB_W), lambda j: (0, j))
    outs, carried = _call(
        body, grid=(N // CB_W,),
        in_specs=[_col(T), wspec, bspec,
                  pl.BlockSpec((T, CB_W), lambda j: (0, jnp.minimum(j, n_x - 1))),
                  pl.BlockSpec((T, CB_W), lambda j: (0, jnp.clip(j - n_x, 0, n_b - 1))),
                  pl.BlockSpec((T, CB_W), lambda j: (0, jnp.clip(j - n_x - n_b, 0, n_b - 1)))],
        out_specs=[_col(T), wspec, bspec],
        out_shape=[_sds((T, N), BF16), _sds((SSM_K, N), F32), _sds((1, N), F32)],
        scratch=[pltpu.VMEM((T + CONV_PAD, CB_W), F32)],
        args=[u, w, b, dxs, dB, dC], name=name, sem=("parallel",), comm=comm)
    return outs if comm is None else (outs, carried)


def _split3(v):
    hi = v.astype(BF16)
    r = v - hi.astype(F32)
    mid = r.astype(BF16)
    lo = (r - mid.astype(F32)).astype(BF16)
    return hi, mid, lo


def _head_expand(n_lanes):
    h = lax.broadcasted_iota(jnp.int32, (LANES, n_lanes), 0)
    l = lax.broadcasted_iota(jnp.int32, (LANES, n_lanes), 1)
    return (jnp.right_shift(l, HEADDIM.bit_length() - 1) == h).astype(BF16)


def _softplus(v):
    return jnp.maximum(v, 0.0) + jnp.log1p(jnp.exp(-jnp.abs(v)))


def _ssd_prep(dt_raw, dt_bias, a_log, n_inner, name):
    T = dt_raw.shape[0]

    def body(r_ref, b_ref, al_ref, dt_ref, cs_ref):
        dt = _softplus(r_ref[...] + b_ref[...])
        a = dt * (-jnp.exp(al_ref[...]))
        i = lax.broadcasted_iota(jnp.int32, (CHUNK, CHUNK), 0)
        j = lax.broadcasted_iota(jnp.int32, (CHUNK, CHUNK), 1)
        tri = (j <= i).astype(BF16)
        cs = sum(_dot(tri, p) for p in _split3(a))
        ex = _head_expand(n_inner)
        dt_ref[...] = sum(_dot(p, ex) for p in _split3(dt))
        cs_ref[...] = sum(_dot(p, ex) for p in _split3(cs))

    blk = pl.BlockSpec((CHUNK, LANES), lambda c: (c, 0))
    out = pl.BlockSpec((CHUNK, n_inner), lambda c: (c, 0))
    return pl.pallas_call(body, grid=(T // CHUNK,), in_specs=[blk, _vec(LANES), _vec(LANES)], out_specs=[out, out],
                          out_shape=[_sds((T, n_inner), F32)] * 2, name=name, compiler_params=_params("parallel"))(dt_raw, dt_bias, a_log)


def _pair_terms(cs_p):
    lane = lax.broadcasted_iota(jnp.int32, (CHUNK, CHUNK), 1)
    sub = lax.broadcasted_iota(jnp.int32, (CHUNK, CHUNK), 0)
    csT = cs_p.T
    Ls = []
    for k in range(2):
        col = jnp.sum(jnp.where(lane == k * HEADDIM, cs_p, 0.0), axis=1, keepdims=True)
        rowv = csT[k * HEADDIM:k * HEADDIM + 1, :]
        Ls.append(jnp.exp(jnp.where(sub >= lane, col - rowv, -jnp.inf)))
    return Ls, jnp.exp(csT[:, CHUNK - 1:CHUNK])


def _block_diag(xp):
    lane = lax.broadcasted_iota(jnp.int32, xp.shape, 1)
    return jnp.concatenate([jnp.where(lane < HEADDIM, xp, 0.0), jnp.where(lane >= HEADDIM, xp, 0.0)], axis=0)


SSD_GROUPS_PER_STEP = 8


def _ssd_specs(T, n_inner):
    nc, gs = T // CHUNK, SSD_GROUPS_PER_STEP
    bo, co = n_inner // (gs * NSTATE), (n_inner + NGROUPS * NSTATE) // (gs * NSTATE)
    assert NGROUPS % gs == 0 and n_inner % (gs * NSTATE) == 0 and (NGROUPS * NSTATE) % (gs * NSTATE) == 0
    g_blk = lambda f: pl.BlockSpec((CHUNK, gs * GROUP_W), lambda c, s: (f(c), s))
    b_blk = lambda f: pl.BlockSpec((CHUNK, gs * NSTATE), lambda c, s: (f(c), bo + s))
    c_blk = lambda f: pl.BlockSpec((CHUNK, gs * NSTATE), lambda c, s: (f(c), co + s))
    return nc, g_blk, b_blk, c_blk


def _ssd_fwd(xbc, dt_e, cs_e, d_e, name, comm=None):
    T = xbc.shape[0]
    n_inner = dt_e.shape[1]
    nc, g_blk, b_blk, c_blk = _ssd_specs(T, n_inner)
    ident = lambda c: c

    gs = SSD_GROUPS_PER_STEP

    def body(xs_ref, b_ref, c_ref, dt_ref, cs_ref, d_ref, y_ref, p_ref, st):
        c, s = pl.program_id(0), pl.program_id(1)

        @pl.when(c == 0)
        def _():
            for gi in range(gs):
                st[s * gs + gi] = jnp.zeros((GROUP_W, NSTATE), F32)

        for gi in range(gs):
            g = s * gs + gi
            gw, gn = slice(gi * GROUP_W, (gi + 1) * GROUP_W), slice(gi * NSTATE, (gi + 1) * NSTATE)
            P = st[g]
            p_ref[0, gi] = P
            xs, dt, cs = xs_ref[:, gw], dt_ref[:, gw], cs_ref[:, gw]
            Bf, Cf = b_ref[:, gn], c_ref[:, gn]
            CBm = _dot3(Cf, Bf, NT)
            X = xs * dt
            decay = jnp.exp(cs[CHUNK - 1:CHUNK, :] - cs)
            y_off = _dot3(Cf, P, NT) * jnp.exp(cs)
            ys, ecl = [], []
            for pr in range(2):
                sl = slice(pr * LANES, (pr + 1) * LANES)
                Ls, e_last = _pair_terms(cs[:, sl])
                ecl.append(e_last)
                Mcat = jnp.concatenate([CBm * L for L in Ls], axis=1)
                ys.append(_dot3(Mcat, _block_diag(X[:, sl])))
            y_ref[:, gw] = jnp.concatenate(ys, axis=1) + y_off + xs * d_ref[:, gw]
            S = _dot3(X * decay, Bf, TN)
            st[g] = P * jnp.concatenate(ecl, axis=0) + S

    p_blk = pl.BlockSpec((1, gs, GROUP_W, NSTATE), lambda c, s: (c, s, 0, 0))
    outs, carried = _call(
        body, grid=(nc, NGROUPS // gs),
        in_specs=[g_blk(ident), b_blk(ident), c_blk(ident), g_blk(ident), g_blk(ident), pl.BlockSpec((1, gs * GROUP_W), lambda c, s: (0, s))],
        out_specs=[g_blk(ident), p_blk],
        out_shape=[_sds((T, n_inner), F32), _sds((nc, NGROUPS, GROUP_W, NSTATE), F32)],
        scratch=[pltpu.VMEM((NGROUPS, GROUP_W, NSTATE), F32)],
        args=[xbc, xbc, xbc, dt_e, cs_e, d_e], name=name, sem=("arbitrary", "arbitrary"), comm=comm)
    return outs if comm is None else (outs, carried)


def _ssd_bwd(xbc, dt_e, cs_e, d_e, states, dy, name, comm=None):
    T = xbc.shape[0]
    n_inner = dt_e.shape[1]
    nc, g_blk, b_blk, c_blk = _ssd_specs(T, n_inner)
    rev = lambda c: nc - 1 - c

    gs = SSD_GROUPS_PER_STEP

    def body(xs_ref, b_ref, c_ref, dt_ref, cs_ref, d_ref, p_ref, pn_ref, dy_ref,
             dxs_ref, db_ref, dc_ref, ddt_ref, dcs_ref, dd_ref, dst):
        cc, s = pl.program_id(0), pl.program_id(1)

        @pl.when(cc == 0)
        def _():
            for gi in range(gs):
                dst[s * gs + gi] = jnp.zeros((GROUP_W, NSTATE), F32)

        for gi in range(gs):
            one_group(s * gs + gi, gi, xs_ref, b_ref, c_ref, dt_ref, cs_ref, d_ref, p_ref, pn_ref, dy_ref,
                      dxs_ref, db_ref, dc_ref, ddt_ref, dcs_ref, dd_ref, dst)

    def one_group(g, gi, xs_ref, b_ref, c_ref, dt_ref, cs_ref, d_ref, p_ref, pn_ref, dy_ref,
                  dxs_ref, db_ref, dc_ref, ddt_ref, dcs_ref, dd_ref, dst):
        gw, gn = slice(gi * GROUP_W, (gi + 1) * GROUP_W), slice(gi * NSTATE, (gi + 1) * NSTATE)
        dS = dst[g]
        P, Pn = p_ref[0, gi], pn_ref[0, gi]
        xs, dt, cs, dY = xs_ref[:, gw], dt_ref[:, gw], cs_ref[:, gw], dy_ref[:, gw]
        Bf, Cf = b_ref[:, gn], c_ref[:, gn]
        Bb, Cb = Bf.astype(BF16), Cf.astype(BF16)
        X = xs * dt
        ecs = jnp.exp(cs)
        decay = jnp.exp(cs[CHUNK - 1:CHUNK, :] - cs)
        CBm = _dot3(Cf, Bf, NT)
        dYe = dY * ecs
        dP_off = _dot3(dYe, Cf, TN)
        dC = _dot(dYe.astype(BF16), P.astype(BF16))
        dcs = dYe * _dot3(Cf, P, NT)
        Xd = X * decay
        dB = _dot(Xd.astype(BF16), dS.astype(BF16))
        E = _dot3(Bf, dS, NT)
        dX = E * decay
        dcs = dcs - E * Xd
        R = _dot3(jnp.ones((8, NSTATE), F32), dS * Pn, NT)
        sub_g = lax.broadcasted_iota(jnp.int32, (CHUNK, GROUP_W), 0)
        dcs = dcs + jnp.where(sub_g == CHUNK - 1, R[0:1, :], 0.0)
        lane = lax.broadcasted_iota(jnp.int32, (CHUNK, CHUNK), 1)
        sub = lax.broadcasted_iota(jnp.int32, (CHUNK, CHUNK), 0)
        dCB = jnp.zeros((CHUNK, CHUNK), F32)
        dXs, dcss, ecl = [], [], []
        for pr in range(2):
            sl = slice(pr * LANES, (pr + 1) * LANES)
            Ls, e_last = _pair_terms(cs[:, sl])
            ecl.append(e_last)
            dYp = dY[:, sl]
            dMcat = _dot3(dYp, _block_diag(X[:, sl]), NT)
            Mcat = jnp.concatenate([CBm * L for L in Ls], axis=1)
            dXt = _dot3(Mcat, dYp, TN)
            dXs.append(jnp.where(lane < HEADDIM, dXt[:CHUNK], dXt[CHUNK:]))
            colacc = jnp.zeros((CHUNK, CHUNK), F32)
            rowacc = jnp.zeros((CHUNK, CHUNK), F32)
            for k in range(2):
                dG = dMcat[:, k * CHUNK:(k + 1) * CHUNK] * Ls[k]
                dCB = dCB + dG
                Q = dG * CBm
                colacc = colacc + jnp.where(lane == k * HEADDIM, jnp.sum(Q, axis=1, keepdims=True), 0.0)
                rowacc = rowacc + jnp.where(sub == k * HEADDIM, jnp.sum(Q, axis=0, keepdims=True), 0.0)
            dcss.append(colacc - rowacc.T)
        dX = dX + jnp.concatenate(dXs, axis=1)
        dcs = dcs + jnp.concatenate(dcss, axis=1)
        dCBb = dCB.astype(BF16)
        dc_ref[:, gn] = dC + _dot(dCBb, Bb)
        db_ref[:, gn] = dB + _dot(dCBb, Cb, TN)
        dxs_ref[:, gw] = dX * dt + dY * d_ref[:, gw]
        ddt_ref[:, gw] = dX * xs
        dcs_ref[:, gw] = dcs
        dd_ref[0, :, gw] = jnp.sum(dY * xs, axis=0, keepdims=True)
        dst[g] = dS * jnp.concatenate(ecl, axis=0) + dP_off

    p_blk = pl.BlockSpec((1, gs, GROUP_W, NSTATE), lambda c, s: (nc - 1 - c, s, 0, 0))
    pn_blk = pl.BlockSpec((1, gs, GROUP_W, NSTATE), lambda c, s: (jnp.minimum(nc - c, nc - 1), s, 0, 0))
    st_blk = pl.BlockSpec((CHUNK, gs * NSTATE), lambda c, s: (nc - 1 - c, s))
    outs, carried = _call(
        body, grid=(nc, NGROUPS // gs),
        in_specs=[g_blk(rev), b_blk(rev), c_blk(rev), g_blk(rev), g_blk(rev), pl.BlockSpec((1, gs * GROUP_W), lambda c, s: (0, s)),
                  p_blk, pn_blk, g_blk(rev)],
        out_specs=[g_blk(rev), st_blk, st_blk, g_blk(rev), g_blk(rev), pl.BlockSpec((1, 1, gs * GROUP_W), lambda c, s: (nc - 1 - c, 0, s))],
        out_shape=[_sds((T, n_inner), F32), _sds((T, NGROUPS * NSTATE), F32), _sds((T, NGROUPS * NSTATE), F32),
                   _sds((T, n_inner), F32), _sds((T, n_inner), F32), _sds((nc, 1, n_inner), F32)],
        scratch=[pltpu.VMEM((NGROUPS, GROUP_W, NSTATE), F32)],
        args=[xbc, xbc, xbc, dt_e, cs_e, d_e, states, states, dy], name=name, sem=("arbitrary", "arbitrary"), comm=comm)
    return outs if comm is None else (outs, carried)


def _ssd_post(ddt_e, dcs_e, dd_p, dt_raw, dt_bias, a_log, n_heads, name):
    T, n_inner = ddt_e.shape

    def body(ddt_ref, dcs_ref, dd_ref, r_ref, b_ref, al_ref, draw_ref, dbias_ref, dal_ref, ddsk_ref):
        @pl.when(pl.program_id(0) == 0)
        def _():
            dbias_ref[...] = jnp.zeros_like(dbias_ref)
            dal_ref[...] = jnp.zeros_like(dal_ref)
            ddsk_ref[...] = jnp.zeros_like(ddsk_ref)

        ex = _head_expand(n_inner)
        red = lambda v: sum(_dot(p, ex, NT) for p in _split3(v))
        raw = r_ref[...] + b_ref[...]
        dt = _softplus(raw)
        A = -jnp.exp(al_ref[...])
        i = lax.broadcasted_iota(jnp.int32, (CHUNK, CHUNK), 0)
        j = lax.broadcasted_iota(jnp.int32, (CHUNK, CHUNK), 1)
        upper = (j >= i).astype(BF16)
        da = sum(_dot(upper, p) for p in _split3(red(dcs_ref[...])))
        ddt = red(ddt_ref[...]) + da * A
        lane = lax.broadcasted_iota(jnp.int32, (CHUNK, LANES), 1)
        draw = jnp.where(lane < n_heads, ddt * jax.nn.sigmoid(raw), 0.0)
        draw_ref[...] = draw.astype(BF16)
        dbias_ref[...] += jnp.sum(draw, axis=0, keepdims=True)
        dal_ref[...] += jnp.sum(da * dt, axis=0, keepdims=True) * A
        ddsk_ref[...] += red(jnp.broadcast_to(dd_ref[0], (8, n_inner)))[0:1, :]

    wide = pl.BlockSpec((CHUNK, n_inner), lambda c: (c, 0))
    blk = pl.BlockSpec((CHUNK, LANES), lambda c: (c, 0))
    return pl.pallas_call(
        body, grid=(T // CHUNK,),
        in_specs=[wide, wide, pl.BlockSpec((1, 1, n_inner), lambda c: (c, 0, 0)), blk, _vec(LANES), _vec(LANES)],
        out_specs=[blk, _vec(LANES), _vec(LANES), _vec(LANES)],
        out_shape=[_sds((T, LANES), BF16)] + [_sds((1, LANES), F32)] * 3,
        name=name, compiler_params=_params("arbitrary"))(ddt_e, dcs_e, dd_p, dt_raw, dt_bias, a_log)


def _row2(v):
    return v.reshape(1, -1).astype(F32)


def _pad_lanes(v):
    return jnp.pad(_row2(v), ((0, 0), (0, LANES - v.shape[-1])))


class _NoExchange:
    def __init__(self, W):
        self.W, self.grads = W, {}

    def weight(self, k):
        return self.W[k]

    def carry(self, name):
        return None

    def carried(self, name, outs):
        pass

    def grad(self, k, g):
        self.grads[k] = g

    def tok(self):
        return jnp.zeros((), F32)

    def point(self, name, value):
        pass


def _local_step(x, tgt, S, small):
    T, D = x.shape

    def mm(a, b, *, name, **kw):
        comm = S.carry(name)
        if comm is None:
            return _mm(a, b, name=name, **kw)
        res, outs = _mm(a, b, name=name, comm=comm, **kw)
        S.carried(name, outs)
        return res

    def carrying(fn, *args, name):
        comm = S.carry(name)
        if comm is None:
            return fn(*args, name)
        res, outs = fn(*args, name, comm=comm)
        S.carried(name, outs)
        return res

    n_inner = 2 * D
    n_heads = n_inner // HEADDIM
    norm_mix, norm_mlp, norm_final = _row2(small["norm_mix"]), _row2(small["norm_mlp"]), _row2(small["norm_final"])
    b_gate, ssm_b, ssm_norm_w = _row2(small["b_gate"]), _row2(small["ssm_conv_b"]), _row2(small["ssm_norm_w"])
    dt_bias, a_log = _pad_lanes(small["dt_bias"]), _pad_lanes(small["A_log"])
    d_e = jnp.repeat(small["D_skip"].astype(F32), HEADDIM).reshape(1, n_inner)
    sc_w, ssm_w = small["sc_conv_w"], small["ssm_conv_w"]

    hb = _rms_fwd(x, norm_mix + S.tok(), "rms_mix")
    p_xbc = mm(hb, S.weight("xbc"), mode="nn", name="proj_xbc")
    p_dt = mm(hb, S.weight("dt"), mode="nn", name="proj_dt")
    p_z = mm(hb, S.weight("z"), mode="nn", name="proj_z")
    p_sc = mm(hb, S.weight("sc"), mode="nn", name="proj_sc")
    p_gate = mm(hb, S.weight("gate"), mode="nn", name="proj_gate")
    xbc = carrying(_ssm_conv_fwd, p_xbc, ssm_w, ssm_b, name="ssm_conv_fwd")
    dt_e, cs_e = _ssd_prep(p_dt, dt_bias, a_log, n_inner, "ssd_prep")
    y, states = carrying(_ssd_fwd, xbc, dt_e, cs_e, d_e, name="ssd_fwd")
    S.point("ssd_fwd_done", y)
    yb = carrying(_gnorm_fwd, y, p_z, ssm_norm_w, name="gnorm_fwd")
    ya = _sc_fwd(p_sc, sc_w, "sc_fwd")
    br_a = mm(ya, S.weight("bsc"), mode="nn", name="branch_sc")
    br_b = mm(yb, S.weight("bssm"), mode="nn", name="branch_ssm")
    merged = _merge_fwd(p_gate, b_gate, br_a, br_b, "merge_fwd")
    x1 = mm(merged, S.weight("out"), mode="nn", name="out_proj", extras=(x,), epi=_epi_add)
    h2 = _rms_fwd(x1, norm_mlp, "rms_mlp")
    r_act = mm(h2, S.weight("w1"), mode="nn", name="mlp_up", epi=_epi_relu2, out_dtypes=(BF16,))
    x2 = mm(r_act, S.weight("w2"), mode="nn", name="mlp_down", extras=(x1,), epi=_epi_add)
    dx2, dx2b, g_norm_final, loss_row = _final(x2, norm_final, tgt, "final")

    S.grad("w2", mm(r_act, dx2b, mode="tn", name="mlp_down_dw", out_dtypes=(BF16,)))
    da = mm(dx2b, S.weight("w2"), mode="nt", name="mlp_down_dx", extras=(r_act,), epi=_epi_relu2_bwd, out_dtypes=(BF16,))
    S.grad("w1", mm(h2, da, mode="tn", name="mlp_up_dw", out_dtypes=(BF16,)))
    dh2 = mm(da, S.weight("w1"), mode="nt", name="mlp_up_dx")
    dx1, dx1b, g_norm_mlp = _rms_bwd(x1, norm_mlp + S.tok(), dh2, dx2, "rms_mlp_bwd")
    S.grad("out", mm(merged, dx1b, mode="tn", name="out_proj_dw", out_dtypes=(BF16,)))
    dmerged = mm(dx1b, S.weight("out"), mode="nt", name="out_proj_dx")
    dbr_a, dbr_b, d_gate, g_b_gate = _merge_bwd(dmerged, p_gate, b_gate, br_a, br_b, "merge_bwd")
    S.grad("bssm", mm(yb, dbr_b, mode="tn", name="branch_ssm_dw", out_dtypes=(BF16,)))
    S.grad("bsc", mm(ya, dbr_a, mode="tn", name="branch_sc_dw", out_dtypes=(BF16,)))
    dyb = mm(dbr_b, S.weight("bssm"), mode="nt", name="branch_ssm_dx")
    dya = mm(dbr_a, S.weight("bsc"), mode="nt", name="branch_sc_dx")
    dy, d_z, g_ssm_norm_w = _gnorm_bwd(y, p_z, ssm_norm_w + S.tok(), dyb, "gnorm_bwd")
    dxs, dB, dC, ddt_e, dcs_e, dd_p = carrying(_ssd_bwd, xbc, dt_e, cs_e, d_e, states, dy, name="ssd_bwd")
    d_dt, g_dt_bias, g_a_log, g_d_skip = _ssd_post(ddt_e, dcs_e, dd_p, p_dt, dt_bias, a_log, n_heads, "ssd_post")
    d_xbc, g_ssm_w, g_ssm_b = carrying(_ssm_conv_bwd, p_xbc, ssm_w, ssm_b, dxs, dB, dC, name="ssm_conv_bwd")
    d_scB, d_scC, d_scX, g_sc_w = _sc_bwd(p_sc, sc_w, dya, "sc_bwd")
    d_sc = jnp.concatenate([d_scB, d_scC, d_scX], axis=1)
    pieces = [("sc", d_sc), ("z", d_z), ("xbc", d_xbc), ("dt", d_dt), ("gate", d_gate)]
    S.grad("win", {k: mm(hb, d, mode="tn", name="proj_dw_" + k, out_dtypes=(BF16,)) for k, d in pieces})
    pieces = [(k, d + S.tok().astype(d.dtype) if k == "dt" else d) for k, d in pieces]
    dh = mm([d for _, d in pieces], [S.weight(k) for k, _ in pieces], mode="nt", name="proj_dx")
    grad_x, _, g_norm_mix = _rms_bwd(x, norm_mix, dh, dx1, "rms_mix_bwd")

    g_small = dict(norm_mix=g_norm_mix, b_gate=g_b_gate, sc_conv_w=g_sc_w, ssm_conv_w=g_ssm_w, ssm_conv_b=g_ssm_b,
                   dt_bias=g_dt_bias, A_log=g_a_log, D_skip=g_d_skip, ssm_norm_w=g_ssm_norm_w, norm_mlp=g_norm_mlp,
                   norm_final=g_norm_final, loss=loss_row)
    return grad_x, g_small


class _Place:
    def __init__(self, k=0):
        x, y, c = lax.axis_index("x"), lax.axis_index("y"), lax.axis_index("c")
        self.x = 1 - x if k & 4 else x
        self.y = 1 - y if k & 2 else y
        self.c = 1 - c if k & 1 else c
        self.chip = 2 * self.x + self.y
        self.id = 2 * self.chip + self.c


ICI_PEERS = (2, 4, 6)
SIBLING = (1,)
ALL_PEERS = (1, 2, 3, 4, 5, 6, 7)


class _Comm:
    def __init__(self, arrs, out_shape, ks, src, dst, own=None, aliases=None):
        self.arrs, self.out_shape, self.ks = list(arrs), list(out_shape), tuple(ks)
        self.n = len(self.arrs)
        self.src, self.dst, self.own = src, dst, own
        self.aliases = aliases or {}
        dma = pltpu.SemaphoreType.DMA
        self.scratch = [dma((self.n, len(self.ks))), dma((self.n, len(self.ks))), dma((self.n,))]

    def _copies(self, ins, outs, sems, with_recvs):
        send_sems, recv_sems, local_sems = sems
        me = _Place()
        owns, sends, recvs = [], [], []
        for a in range(self.n):
            if self.own is not None:
                s, d = self.own(a, ins[a], outs[a], me)
                owns.append(pltpu.make_async_copy(s, d, local_sems.at[a]))
            for i, k in enumerate(self.ks):
                peer = _Place(k)
                for sender, lst in ((me, sends), (peer, recvs)) if with_recvs else ((me, sends),):
                    lst.append(pltpu.make_async_remote_copy(
                        src_ref=self.src(a, ins[a], me, peer), dst_ref=self.dst(a, outs[a], sender),
                        send_sem=send_sems.at[a, i], recv_sem=recv_sems.at[a, i],
                        device_id=(peer.x, peer.y, peer.c), device_id_type=MESH))
        return owns, sends, recvs

    def start(self, ins, outs, sems):
        owns, sends, _ = self._copies(ins, outs, sems, False)
        for cp in owns + sends:
            cp.start()

    def finish(self, ins, outs, sems):
        owns, sends, recvs = self._copies(ins, outs, sems, True)
        for cp in recvs:
            cp.wait_recv()
        for cp in sends:
            cp.wait_send()
        for cp in owns:
            cp.wait()


def _run_comm(comm, name, after=()):
    n, n_after = comm.n, len(after)

    def body(*refs):
        ins, outs, sems = refs[:n], refs[n + n_after:2 * n + n_after], refs[2 * n + n_after:]
        comm.start(ins, outs, sems)
        comm.finish(ins, outs, sems)

    return list(pl.pallas_call(body, in_specs=[ANY] * (n + n_after), out_specs=[ANY] * n, out_shape=comm.out_shape,
                               scratch_shapes=comm.scratch, input_output_aliases=dict(comm.aliases), name=name)(*comm.arrs, *after))


def _gather_ici(shards):
    return _Comm(shards, [_sds((4, 2) + s.shape, s.dtype) for s in shards], ICI_PEERS,
                 src=lambda a, i, me, p: i, dst=lambda a, o, s: o.at[s.chip, s.c], own=lambda a, i, o, me: (i, o.at[me.chip, me.c]))


def _gather_sibling(bufs):
    return _Comm(bufs, [_sds(b.shape, b.dtype) for b in bufs], SIBLING,
                 src=lambda a, i, me, p: i.at[:, me.c], dst=lambda a, o, s: o.at[:, s.c], aliases={a: a for a in range(len(bufs))})


def _scatter_sibling(parts):
    return _Comm(parts, [_sds((4,) + p.shape[2:], p.dtype) for p in parts], SIBLING,
                 src=lambda a, i, me, p: i.at[:, p.c], dst=lambda a, o, s: o)


def _scatter_ici(parts):
    return _Comm(parts, [_sds(p.shape, p.dtype) for p in parts], ICI_PEERS,
                 src=lambda a, i, me, p: i.at[p.chip], dst=lambda a, o, s: o.at[s.chip], own=lambda a, i, o, me: (i.at[me.chip], o.at[me.chip]))


HBM_SPEC = pl.BlockSpec(memory_space=pltpu.HBM)
SEM_SPEC = pl.BlockSpec(memory_space=pltpu.SEMAPHORE)
DATAFLOW = pltpu.SideEffectType.DATAFLOW_SIDE_EFFECTING


def _own_part(parts, name):
    n, R, C = parts.shape
    tr = R if R <= 256 else 256
    chip = (2 * lax.axis_index("x") + lax.axis_index("y")).astype(jnp.int32).reshape(1)

    def body(q_ref, p_ref, o_ref):
        o_ref[...] = p_ref[...]

    blk = pl.BlockSpec((1, tr, C), lambda i, q_ref: (q_ref[0], i, 0))
    spec = pltpu.PrefetchScalarGridSpec(num_scalar_prefetch=1, grid=(R // tr,), in_specs=[blk], out_specs=blk)
    return pl.pallas_call(body, grid_spec=spec, out_shape=_sds((n, R, C), parts.dtype), name=name,
                          compiler_params=_params("parallel"))(chip, parts)


def _ici_copy(gather, a, srcs, lands, send_sems, recv_sems, i, me, peer, sender):
    src = lands[a].at[me.chip, me.c] if gather else srcs[a].at[peer.chip]
    dst = lands[a].at[sender.chip, sender.c] if gather else lands[a].at[sender.chip]
    j = a * len(ICI_PEERS) + i
    return pltpu.make_async_remote_copy(src_ref=src, dst_ref=dst, send_sem=send_sems.at[j], recv_sem=recv_sems.at[j],
                                        device_id=(peer.x, peer.y, peer.c), device_id_type=MESH)


def _ici_start(srcs, lands, gather, name):
    n, n_s = len(lands), len(srcs)
    bufs = list(srcs) + list(lands)

    def body(*refs):
        src_refs, land_refs = refs[:n_s], refs[n_s:n_s + n]
        send_sems, recv_sems = refs[n_s + n], refs[n_s + n + 1]
        token = refs[-1]
        me = _Place()
        for a in range(n):
            for i, k in enumerate(ICI_PEERS):
                _ici_copy(gather, a, src_refs, land_refs, send_sems, recv_sems, i, me, _Place(k), me).start()
        token[...] = jnp.zeros_like(token)

    dma = pltpu.SemaphoreType.DMA((n * len(ICI_PEERS),))
    outs = pl.pallas_call(
        body, name=name, out_shape=(dma, dma, *[pltpu.HBM(v.shape, v.dtype) for v in bufs], _sds((8, LANES), F32)),
        in_specs=(HBM_SPEC,) * len(bufs),
        out_specs=(SEM_SPEC, SEM_SPEC) + (HBM_SPEC,) * len(bufs) + (pl.BlockSpec(memory_space=pltpu.VMEM),),
        input_output_aliases={j: 2 + j for j in range(len(bufs))}, compiler_params=pltpu.CompilerParams(has_side_effects=DATAFLOW),
    )(*[pltpu.with_memory_space_constraint(v, pltpu.HBM) for v in bufs])
    return outs[0], outs[1], list(outs[2:2 + n_s]), list(outs[2 + n_s:2 + n_s + n]), outs[-1]


def _ici_wait(flight, after, gather, name):
    send_sems, recv_sems, srcs, lands, _ = flight
    n, n_s = len(lands), len(srcs)
    bufs = srcs + lands

    def body(*refs):
        src_refs, land_refs = refs[:n_s], refs[n_s:n_s + n]
        s_sems, r_sems = refs[n_s + n], refs[n_s + n + 1]
        me = _Place()
        for a in range(n):
            for i, k in enumerate(ICI_PEERS):
                peer = _Place(k)
                cp = _ici_copy(gather, a, src_refs, land_refs, s_sems, r_sems, i, me, peer, peer)
                cp.wait_send()
                cp.wait_recv()

    outs = pl.pallas_call(
        body, name=name, out_shape=tuple(pltpu.HBM(v.shape, v.dtype) for v in bufs),
        in_specs=(HBM_SPEC,) * len(bufs) + (SEM_SPEC, SEM_SPEC) + (ANY,) * len(after), out_specs=(HBM_SPEC,) * len(bufs),
        input_output_aliases={j: j for j in range(len(bufs))}, compiler_params=pltpu.CompilerParams(has_side_effects=DATAFLOW),
    )(*bufs, send_sems, recv_sems, *after)
    return list(outs[n_s:])


def _own_shard(shard, name):
    R, C = shard.shape
    tr = R if R <= 256 else 256
    place = jnp.stack([2 * lax.axis_index("x") + lax.axis_index("y"), lax.axis_index("c")]).astype(jnp.int32)

    def body(q_ref, s_ref, o_ref):
        o_ref[0, 0] = s_ref[...].astype(o_ref.dtype)

    spec = pltpu.PrefetchScalarGridSpec(
        num_scalar_prefetch=1, grid=(R // tr,), in_specs=[pl.BlockSpec((tr, C), lambda i, q_ref: (i, 0))],
        out_specs=pl.BlockSpec((1, 1, tr, C), lambda i, q_ref: (q_ref[0], q_ref[1], i, 0)))
    return pl.pallas_call(body, grid_spec=spec, out_shape=_sds((4, 2, R, C), BF16), name=name,
                          compiler_params=_params("parallel"))(place, shard)


def _gather_all(arrs):
    return _Comm(arrs, [_sds((N_DEV,) + a.shape, a.dtype) for a in arrs], ALL_PEERS,
                 src=lambda a, i, me, p: i, dst=lambda a, o, s: o.at[s.id], own=lambda a, i, o, me: (i, o.at[me.id]))


def _add_halves(parts, got, name):
    n, _, R, C = parts.shape
    tr = R if R <= 256 else 256
    assert R % tr == 0
    core = lax.axis_index("c").astype(jnp.int32).reshape(1)

    def body(c_ref, p_ref, g_ref, o_ref):
        o_ref[0] = (p_ref[0, 0].astype(F32) + g_ref[0].astype(F32)).astype(o_ref.dtype)

    spec = pltpu.PrefetchScalarGridSpec(
        num_scalar_prefetch=1, grid=(n, R // tr),
        in_specs=[pl.BlockSpec((1, 1, tr, C), lambda q, i, c_ref: (q, c_ref[0], i, 0)), pl.BlockSpec((1, tr, C), lambda q, i, c_ref: (q, i, 0))],
        out_specs=pl.BlockSpec((1, tr, C), lambda q, i, c_ref: (q, i, 0)))
    return pl.pallas_call(body, grid_spec=spec, out_shape=_sds((n, R, C), parts.dtype), name=name,
                          compiler_params=_params("parallel", "parallel"))(core, parts, got)


def _adam(w, m, v, gparts, name):
    R, C = w.shape
    n = gparts.shape[0]
    tr = R if R <= 256 else 128
    assert R % tr == 0
    c1 = 1.0 / (1.0 - ADAM_B1 ** ADAM_STEP)
    c2 = 1.0 / (1.0 - ADAM_B2 ** ADAM_STEP)

    def body(w_ref, m_ref, v_ref, g_ref, go_ref, d_ref, mo_ref, vo_ref):
        g = g_ref[0].astype(F32)
        for s in range(1, n):
            g = g + g_ref[s].astype(F32)
        mn = ADAM_B1 * m_ref[...] + (1.0 - ADAM_B1) * g
        vn = ADAM_B2 * v_ref[...] + (1.0 - ADAM_B2) * (g * g)
        go_ref[...] = g
        mo_ref[...] = mn
        vo_ref[...] = vn
        d_ref[...] = -ADAM_LR * ((mn * c1) / (jnp.sqrt(vn * c2) + ADAM_EPS) + ADAM_WD * w_ref[...])

    blk = pl.BlockSpec((tr, C), lambda i: (i, 0))
    return pl.pallas_call(
        body, grid=(R // tr,), in_specs=[blk, blk, blk, pl.BlockSpec((n, tr, C), lambda i: (0, i, 0))],
        out_specs=[blk] * 4, out_shape=[_sds((R, C), F32)] * 4, name=name, compiler_params=_params("parallel"))(w, m, v, gparts)


_SMALL_ORDER = ("norm_mix", "b_gate", "sc_conv_w", "ssm_conv_w", "ssm_conv_b", "dt_bias", "A_log", "D_skip", "ssm_norm_w",
                "norm_mlp", "norm_final", "loss")
_REPLICATED = ("norm_mix", "b_gate", "ssm_conv_b", "dt_bias", "A_log", "D_skip", "ssm_norm_w", "norm_mlp", "norm_final")


def _cols_to_slots(g, n):
    R = g.shape[0]
    return jnp.transpose(g.reshape(R, n, g.shape[1] // n), (1, 0, 2))


def _slots_to_cols(g):
    n, R, C = g.shape
    return jnp.transpose(g, (1, 0, 2)).reshape(R, n * C)


def kernel(x, norm_mix, w_in, b_gate, sc_conv_w, ssm_conv_w, ssm_conv_b, dt_bias, A_log, D_skip, ssm_norm_w, w_branch_sc, w_branch_ssm, w_out, norm_mlp, w_mlp1, w_mlp2, norm_final, loss_target, m_norm_mix, m_w_in, m_b_gate, m_sc_conv_w, m_ssm_conv_w, m_ssm_conv_b, m_dt_bias, m_A_log, m_D_skip, m_ssm_norm_w, m_w_branch_sc, m_w_branch_ssm, m_w_out, m_norm_mlp, m_w_mlp1, m_w_mlp2, m_norm_final, v_norm_mix, v_w_in, v_b_gate, v_sc_conv_w, v_ssm_conv_w, v_ssm_conv_b, v_dt_bias, v_A_log, v_D_skip, v_ssm_norm_w, v_w_branch_sc, v_w_branch_ssm, v_w_out, v_norm_mlp, v_w_mlp1, v_w_mlp2, v_norm_final):
    T, D = x.shape[1], x.shape[2]
    n_inner = 2 * D
    n_heads = n_inner // HEADDIM
    n_xbc = n_inner + 2 * NGROUPS * NSTATE
    me = 4 * lax.axis_index("x") + 2 * lax.axis_index("y") + lax.axis_index("c")

    o_z, o_xbc, o_dt, o_gate = 3 * D, 3 * D + n_inner, 3 * D + n_inner + n_xbc, 3 * D + n_inner + n_xbc + n_heads
    by_owner = lambda b: b.reshape((N_DEV,) + b.shape[2:])
    to_owner = lambda g: g.reshape((4, 2) + g.shape[1:])
    rows_of = lambda g: to_owner(g.reshape((N_DEV, g.shape[0] // N_DEV) + g.shape[1:]))
    cols_of = lambda g: to_owner(_cols_to_slots(g, N_DEV))

    class Schedule(_NoExchange):
        late = ("bssm", "bsc", "out", "w1", "w2")
        gather_sib = dict(gnorm_fwd=("bsc", "bssm", "out"), branch_ssm=("w1", "w2"))
        scatter_sib = dict(mlp_up_dx=("w2", "w1"), branch_ssm_dx=("out", "bssm", "bsc"))
        shards = dict(bsc=w_branch_sc, bssm=w_branch_ssm, out=w_out, w1=w_mlp1, w2=w_mlp2)

        def __init__(self):
            bufs = _run_comm(_gather_ici([w_in.astype(BF16), sc_conv_w, ssm_conv_w]), "gather_in_ici")
            bufs = _run_comm(_gather_sibling(bufs), "gather_in_sibling")
            win_full = _slots_to_cols(by_owner(bufs[0]))
            self.W = dict(sc=win_full[:, :o_z], z=win_full[:, o_z:o_xbc], xbc=win_full[:, o_xbc:o_dt],
                          dt=jnp.pad(win_full[:, o_dt:o_gate], ((0, 0), (0, LANES - n_heads))), gate=win_full[:, o_gate:])
            self.taps = dict(sc_conv_w=_slots_to_cols(by_owner(bufs[1])), ssm_conv_w=_slots_to_cols(by_owner(bufs[2])))
            self.staged, self.grads, self.summed, self.scatters = {}, {}, {}, []
            lands = [_own_shard(self.shards[k], "own_shard_" + k) for k in self.late]
            self.gather_flight = _ici_start([], lands, True, "gather_late_start")
            self.token = self.gather_flight[4][0, 0]

        def tok(self):
            return self.token

        def point(self, name, value):
            if name == "ssd_fwd_done":
                lands = _ici_wait(self.gather_flight, [value], True, "gather_late_wait")
                self.staged.update(zip(self.late, lands))

        def carry(self, name):
            if name in self.gather_sib:
                return _gather_sibling([self.staged.pop(k) for k in self.gather_sib[name]])
            if name in self.scatter_sib:
                return _scatter_sibling([self.grads[k] for k in self.scatter_sib[name]])
            return None

        def start_scatter(self, keys, halves):
            lands = [_own_part(h, "own_part_" + k) for k, h in zip(keys, halves)]
            flight = _ici_start(halves, lands, False, "scatter_%s_start" % keys[0])
            self.scatters.append((keys, flight))
            self.token = flight[4][0, 0]

        def carried(self, name, outs):
            if name in self.gather_sib:
                for k, b in zip(self.gather_sib[name], outs):
                    full = by_owner(b)
                    self.W[k] = _slots_to_cols(full) if k == "w1" else full.reshape(-1, D)
            else:
                keys = self.scatter_sib[name]
                self.start_scatter(keys, [_add_halves(self.grads[k], b, "add_halves_" + k) for k, b in zip(keys, outs)])

        def grad(self, k, g):
            if k == "win":
                g = cols_of(jnp.concatenate([g["sc"], g["z"], g["xbc"], g["dt"][:, :n_heads], g["gate"]], axis=1))
                got = _run_comm(_scatter_sibling([g]), "scatter_sibling_win")[0]
                self.start_scatter(("win",), [_add_halves(g, got, "add_halves_win")])
            else:
                self.grads[k] = cols_of(g) if k == "w1" else rows_of(g)

        def finish_scatter(self, after):
            keys, flight = self.scatters.pop(0)
            return dict(zip(keys, _ici_wait(flight, after, False, "scatter_%s_wait" % keys[0])))

    S = Schedule()
    small = dict(norm_mix=norm_mix, b_gate=b_gate, ssm_conv_b=ssm_conv_b, dt_bias=dt_bias, A_log=A_log, D_skip=D_skip,
                 ssm_norm_w=ssm_norm_w, norm_mlp=norm_mlp, norm_final=norm_final, **S.taps)
    grad_x, g_small = _local_step(x.reshape(T, D), loss_target.reshape(T, D), S, small)

    small_flat = jnp.concatenate([g_small[k].reshape(-1) for k in _SMALL_ORDER])
    n_small = small_flat.shape[0]
    rows = -(-n_small // (8 * LANES)) * 8
    small_pack = jnp.pad(small_flat, (0, rows * LANES - n_small)).reshape(rows, LANES)

    res = {}
    big = [("w_in", "win", w_in, m_w_in, v_w_in), ("w_branch_sc", "bsc", w_branch_sc, m_w_branch_sc, v_w_branch_sc),
           ("w_branch_ssm", "bssm", w_branch_ssm, m_w_branch_ssm, v_w_branch_ssm), ("w_out", "out", w_out, m_w_out, v_w_out),
           ("w_mlp1", "w1", w_mlp1, m_w_mlp1, v_w_mlp1), ("w_mlp2", "w2", w_mlp2, m_w_mlp2, v_w_mlp2)]
    by_grad = {gk: (k, w, m, v) for k, gk, w, m, v in big}
    after = [grad_x]
    while S.scatters:
        for gk, parts in S.finish_scatter(after).items():
            k, w, m, v = by_grad[gk]
            res[k] = _adam(w, m, v, parts, "adam_" + k)
            after = [res[k][1]]
    small_parts = _run_comm(_gather_all([small_pack]), "gather_small", after=[res[k][1]])[0]

    sizes = {k: g_small[k].size for k in _SMALL_ORDER}
    offs, o = {}, 0
    for k in _SMALL_ORDER:
        offs[k] = o
        o += sizes[k]
    rep_w = dict(norm_mix=norm_mix, b_gate=b_gate, ssm_conv_b=ssm_conv_b, dt_bias=dt_bias, A_log=A_log, D_skip=D_skip,
                 ssm_norm_w=ssm_norm_w, norm_mlp=norm_mlp, norm_final=norm_final)
    rep_m = dict(norm_mix=m_norm_mix, b_gate=m_b_gate, ssm_conv_b=m_ssm_conv_b, dt_bias=m_dt_bias, A_log=m_A_log, D_skip=m_D_skip,
                 ssm_norm_w=m_ssm_norm_w, norm_mlp=m_norm_mlp, norm_final=m_norm_final)
    rep_v = dict(norm_mix=v_norm_mix, b_gate=v_b_gate, ssm_conv_b=v_ssm_conv_b, dt_bias=v_dt_bias, A_log=v_A_log, D_skip=v_D_skip,
                 ssm_norm_w=v_ssm_norm_w, norm_mlp=v_norm_mlp, norm_final=v_norm_final)

    def pack(d):
        segs = [jnp.pad(d[k].astype(F32).reshape(-1), (0, sizes[k] - d[k].size)) if k in d else jnp.zeros((sizes[k],), F32)
                for k in _SMALL_ORDER]
        return jnp.pad(jnp.concatenate(segs), (0, rows * LANES - n_small)).reshape(rows, LANES)

    sm = _adam(pack(rep_w), pack(rep_m), pack(rep_v), small_parts, "adam_small")
    sm = [s.reshape(-1) for s in sm]
    for k in _REPLICATED:
        n_k = rep_w[k].shape[0]
        res[k] = tuple(s[offs[k]:offs[k] + n_k] for s in sm)
    loss = sm[0][offs["loss"]]
    for k, w, m, v, K, full in (("sc_conv_w", sc_conv_w, m_sc_conv_w, v_sc_conv_w, SC_K, D),
                                ("ssm_conv_w", ssm_conv_w, m_ssm_conv_w, v_ssm_conv_w, SSM_K, n_xbc)):
        g_full = sm[0][offs[k]:offs[k] + K * full].reshape(K, full)
        cw = full // N_DEV
        g_mine = lax.dynamic_slice_in_dim(g_full, me * cw, cw, axis=1)
        res[k] = _adam(w, m, v, g_mine[None], "adam_" + k)

    order = ("norm_mix", "w_in", "b_gate", "sc_conv_w", "ssm_conv_w", "ssm_conv_b", "dt_bias", "A_log", "D_skip", "ssm_norm_w",
             "w_branch_sc", "w_branch_ssm", "w_out", "norm_mlp", "w_mlp1", "w_mlp2", "norm_final")
    outs = [loss, grad_x.reshape(1, T, D)]
    for j in range(4):
        outs += [res[k][j] for k in order]
    return tuple(outs)
```

```python
import functools

import jax
import jax.numpy as jnp
from jax import lax
from jax.experimental import pallas as pl
from jax.experimental.pallas import tpu as pltpu

F32 = jnp.float32
BF16 = jnp.bfloat16

EPS = 1e-6
N_DEV = 8
HEADDIM = 64
NSTATE = 128
CHUNK = 128
NGROUPS = 8
GROUP_W = 256
SC_K = 3
SSM_K = 4
LANES = 128

ADAM_LR = 0.001
ADAM_B1 = 0.9
ADAM_B2 = 0.999
ADAM_EPS = 1e-08
ADAM_WD = 0.01
ADAM_STEP = 10

NN = (((1,), (0,)), ((), ()))
NT = (((1,), (1,)), ((), ()))
TN = (((0,), (0,)), ((), ()))
_DIMS = {"nn": NN, "nt": NT, "tn": TN}

ANY = pl.BlockSpec(memory_space=pl.ANY)
MESH = pl.DeviceIdType.MESH


def _sds(shape, dtype):
    return jax.ShapeDtypeStruct(tuple(shape), dtype)


def _dot(a, b, dims=NN):
    return lax.dot_general(a, b, dims, preferred_element_type=F32)


def _dot3(a, b, dims=NN):
    return lax.dot_general(a, b, dims, preferred_element_type=F32, precision=lax.Precision.HIGH)


def _params(*sem):
    return pltpu.CompilerParams(dimension_semantics=tuple(sem))


def _call(body, *, grid, in_specs, out_specs, out_shape, args, name, sem, scratch=(), comm=None):
    if comm is None:
        outs = pl.pallas_call(body, grid=grid, in_specs=list(in_specs), out_specs=list(out_specs), out_shape=list(out_shape),
                              scratch_shapes=list(scratch), name=name, compiler_params=_params(*sem))(*args)
        return list(outs), None
    n, n_in, n_out, n_scr = comm.n, len(in_specs), len(out_shape), len(scratch)

    def wrapped(*refs):
        ins, c_in = refs[:n_in], refs[n_in:n_in + n]
        outs, c_out = refs[n_in + n:n_in + n + n_out], refs[n_in + n + n_out:n_in + 2 * n + n_out]
        rest = refs[n_in + 2 * n + n_out:]
        scr, sems = rest[:n_scr], rest[n_scr:]
        first, last = None, None
        for d, g in enumerate(grid):
            f, l = pl.program_id(d) == 0, pl.program_id(d) == g - 1
            first, last = (f, l) if first is None else (first & f, last & l)

        @pl.when(first)
        def _():
            comm.start(c_in, c_out, sems)

        body(*ins, *outs, *scr)

        @pl.when(last)
        def _():
            comm.finish(c_in, c_out, sems)

    outs = pl.pallas_call(
        wrapped, grid=grid, in_specs=list(in_specs) + [ANY] * n, out_specs=list(out_specs) + [ANY] * n,
        out_shape=list(out_shape) + comm.out_shape, scratch_shapes=list(scratch) + comm.scratch,
        input_output_aliases={n_in + i: n_out + o for i, o in comm.aliases.items()},
        name=name, compiler_params=_params(*["arbitrary"] * len(grid)))(*args, *comm.arrs)
    return list(outs[:n_out]), list(outs[n_out:])


MM_VMEM_BUDGET = 44 * 2 ** 20


def _mm_tiles(M, N, k_bytes, mn_bytes):
    best = None
    for tm in (2048, 1024, 512, 256, 128):
        for tn in (1024, 512, 256, 128):
            if M % tm or N % tn:
                continue
            need = 2 * ((tm + tn) * k_bytes + tm * tn * mn_bytes) + 4 * tm * tn * 4
            if need <= MM_VMEM_BUDGET and (best is None or (tm * tn, tm) > (best[0] * best[1], best[0])):
                best = (tm, tn)
    assert best is not None, (M, N, k_bytes, mn_bytes)
    return best


def _mm(a, b, *, mode, name, extras=(), epi=None, out_dtypes=(F32,), comm=None):
    a_list = list(a) if isinstance(a, (list, tuple)) else [a]
    b_list = list(b) if isinstance(b, (list, tuple)) else [b]
    if mode == "nn":
        M, N = a_list[0].shape[0], b_list[0].shape[1]
    elif mode == "nt":
        M, N = a_list[0].shape[0], b_list[0].shape[0]
    else:
        M, N = a_list[0].shape[1], b_list[0].shape[1]
    k_bytes = sum((av.shape[0] if mode == "tn" else av.shape[1]) * av.dtype.itemsize for av in a_list)
    mn_bytes = sum(e.dtype.itemsize for e in extras) + sum(jnp.dtype(d).itemsize for d in out_dtypes)
    tm, tn = _mm_tiles(min(M, 2048), min(N, 1024), k_bytes, mn_bytes) if M % 128 == 0 and N % 128 == 0 else (M, N)
    assert M % tm == 0 and N % tn == 0
    a_specs, b_specs = [], []
    for av, bv in zip(a_list, b_list):
        K = av.shape[0] if mode == "tn" else av.shape[1]
        a_specs.append(pl.BlockSpec((K, tm), lambda i, j: (0, i)) if mode == "tn" else pl.BlockSpec((tm, K), lambda i, j: (i, 0)))
        b_specs.append(pl.BlockSpec((tn, K), lambda i, j: (j, 0)) if mode == "nt" else pl.BlockSpec((K, tn), lambda i, j: (0, j)))
    mn_spec = pl.BlockSpec((tm, tn), lambda i, j: (i, j))
    n_p, n_ex = len(a_list), len(extras)
    dims = _DIMS[mode]

    def body(*refs):
        acc = _dot(refs[0][...], refs[n_p][...], dims)
        for p in range(1, n_p):
            acc = acc + _dot(refs[p][...], refs[n_p + p][...], dims)
        rest = refs[2 * n_p:]
        res = (acc,) if epi is None else epi(acc, *[r[...] for r in rest[:n_ex]])
        for o_ref, r in zip(rest[n_ex:], res):
            o_ref[...] = r.astype(o_ref.dtype)

    outs, carried = _call(
        body, grid=(M // tm, N // tn), in_specs=a_specs + b_specs + [mn_spec] * n_ex,
        out_specs=[mn_spec] * len(out_dtypes), out_shape=[_sds((M, N), d) for d in out_dtypes],
        args=a_list + b_list + list(extras), name=name, sem=("parallel", "parallel"), comm=comm)
    res = outs[0] if len(outs) == 1 else outs
    return res if comm is None else (res, carried)


def _epi_add(acc, r):
    return (acc + r,)


def _epi_add2(acc, r):
    s = acc + r
    return (s, s)


def _epi_relu2(acc):
    p = jnp.maximum(acc, 0.0)
    return (p * p,)


def _epi_relu2_bwd(acc, r):
    return (acc * (2.0 * jnp.sqrt(r.astype(F32))),)


def _row(tr, n):
    return pl.BlockSpec((tr, n), lambda i: (i, 0))


def _vec(n):
    return pl.BlockSpec((1, n), lambda i: (0, 0))


def _rms_fwd(x, w, name):
    T, D = x.shape
    tr = min(256, T)

    def body(x_ref, w_ref, o_ref):
        xv = x_ref[...]
        r = lax.rsqrt(jnp.mean(xv * xv, axis=-1, keepdims=True) + EPS)
        o_ref[...] = (xv * r * w_ref[...]).astype(BF16)

    return pl.pallas_call(body, grid=(T // tr,), in_specs=[_row(tr, D), _vec(D)], out_specs=_row(tr, D),
                          out_shape=_sds((T, D), BF16), name=name, compiler_params=_params("parallel"))(x, w)


def _rms_bwd(x, w, dh, dres, name):
    T, D = x.shape
    tr = min(256, T)

    def body(x_ref, w_ref, dh_ref, dres_ref, dx_ref, dxb_ref, dw_ref):
        @pl.when(pl.program_id(0) == 0)
        def _():
            dw_ref[...] = jnp.zeros_like(dw_ref)

        xv = x_ref[...]
        r = lax.rsqrt(jnp.mean(xv * xv, axis=-1, keepdims=True) + EPS)
        xh = xv * r
        dh_v = dh_ref[...]
        dw_ref[...] += jnp.sum(dh_v * xh, axis=0, keepdims=True)
        dxh = dh_v * w_ref[...]
        dx = r * (dxh - xh * jnp.mean(dxh * xh, axis=-1, keepdims=True)) + dres_ref[...]
        dx_ref[...] = dx
        dxb_ref[...] = dx.astype(BF16)

    return pl.pallas_call(
        body, grid=(T // tr,), in_specs=[_row(tr, D), _vec(D), _row(tr, D), _row(tr, D)],
        out_specs=[_row(tr, D), _row(tr, D), _vec(D)],
        out_shape=[_sds((T, D), F32), _sds((T, D), BF16), _sds((1, D), F32)],
        name=name, compiler_params=_params("arbitrary"))(x, w, dh, dres)


def _final(x2, w, tgt, name):
    T, D = x2.shape
    tr = min(256, T)

    def body(x_ref, w_ref, t_ref, dx_ref, dxb_ref, dw_ref, loss_ref):
        @pl.when(pl.program_id(0) == 0)
        def _():
            dw_ref[...] = jnp.zeros_like(dw_ref)
            loss_ref[...] = jnp.zeros_like(loss_ref)

        xv = x_ref[...]
        wv = w_ref[...]
        r = lax.rsqrt(jnp.mean(xv * xv, axis=-1, keepdims=True) + EPS)
        xh = xv * r
        err = xh * wv - t_ref[...]
        part = jnp.sum(jnp.sum(err * err, axis=1, keepdims=True), axis=0, keepdims=True) * (0.5 / D)
        loss_ref[...] += jnp.broadcast_to(part, loss_ref.shape)
        dy = err * (1.0 / D)
        dw_ref[...] += jnp.sum(dy * xh, axis=0, keepdims=True)
        dxh = dy * wv
        dx = r * (dxh - xh * jnp.mean(dxh * xh, axis=-1, keepdims=True))
        dx_ref[...] = dx
        dxb_ref[...] = dx.astype(BF16)

    return pl.pallas_call(
        body, grid=(T // tr,), in_specs=[_row(tr, D), _vec(D), _row(tr, D)],
        out_specs=[_row(tr, D), _row(tr, D), _vec(D), _vec(LANES)],
        out_shape=[_sds((T, D), F32), _sds((T, D), BF16), _sds((1, D), F32), _sds((1, LANES), F32)],
        name=name, compiler_params=_params("arbitrary"))(x2, w, tgt)


def _silu_parts(z):
    s = jax.nn.sigmoid(z)
    return z * s, s * (1.0 + z * (1.0 - s))


def _gnorm_fwd(y, z, w, name, comm=None):
    T, N = y.shape
    tr = min(256, T)

    def body(y_ref, z_ref, w_ref, o_ref):
        for g in range(N // GROUP_W):
            sl = slice(g * GROUP_W, (g + 1) * GROUP_W)
            silu, _ = _silu_parts(z_ref[:, sl])
            yz = y_ref[:, sl] * silu
            r = lax.rsqrt(jnp.mean(yz * yz, axis=-1, keepdims=True) + EPS)
            o_ref[:, sl] = (yz * r * w_ref[:, sl]).astype(BF16)

    outs, carried = _call(body, grid=(T // tr,), in_specs=[_row(tr, N), _row(tr, N), _vec(N)], out_specs=[_row(tr, N)],
                          out_shape=[_sds((T, N), BF16)], args=[y, z, w], name=name, sem=("parallel",), comm=comm)
    return outs[0] if comm is None else (outs[0], carried)


def _gnorm_bwd(y, z, w, dyb, name):
    T, N = y.shape
    tr = min(256, T)

    def body(y_ref, z_ref, w_ref, d_ref, dy_ref, dz_ref, dw_ref):
        @pl.when(pl.program_id(0) == 0)
        def _():
            dw_ref[...] = jnp.zeros_like(dw_ref)

        for g in range(N // GROUP_W):
            sl = slice(g * GROUP_W, (g + 1) * GROUP_W)
            yv = y_ref[:, sl]
            silu, dsilu = _silu_parts(z_ref[:, sl])
            yz = yv * silu
            r = lax.rsqrt(jnp.mean(yz * yz, axis=-1, keepdims=True) + EPS)
            yzh = yz * r
            d = d_ref[:, sl]
            dw_ref[:, sl] += jnp.sum(d * yzh, axis=0, keepdims=True)
            dyzh = d * w_ref[:, sl]
            dyz = r * (dyzh - yzh * jnp.mean(dyzh * yzh, axis=-1, keepdims=True))
            dy_ref[:, sl] = dyz * silu
            dz_ref[:, sl] = (dyz * yv * dsilu).astype(BF16)

    return pl.pallas_call(
        body, grid=(T // tr,), in_specs=[_row(tr, N), _row(tr, N), _vec(N), _row(tr, N)],
        out_specs=[_row(tr, N), _row(tr, N), _vec(N)],
        out_shape=[_sds((T, N), F32), _sds((T, N), BF16), _sds((1, N), F32)],
        name=name, compiler_params=_params("arbitrary"))(y, z, w, dyb)


def _merge_fwd(gate_raw, b_gate, br_a, br_b, name):
    T, D = br_a.shape
    tr = min(256, T)

    def body(g_ref, bg_ref, a_ref, b_ref, o_ref):
        g = jax.nn.sigmoid(g_ref[...] + bg_ref[...])
        o_ref[...] = (g[:, :D] * a_ref[...] + g[:, D:] * b_ref[...]).astype(BF16)

    return pl.pallas_call(body, grid=(T // tr,), in_specs=[_row(tr, 2 * D), _vec(2 * D), _row(tr, D), _row(tr, D)],
                          out_specs=_row(tr, D), out_shape=_sds((T, D), BF16), name=name,
                          compiler_params=_params("parallel"))(gate_raw, b_gate, br_a, br_b)


def _merge_bwd(dmerged, gate_raw, b_gate, br_a, br_b, name):
    T, D = br_a.shape
    tr = min(256, T)

    def body(d_ref, g_ref, bg_ref, a_ref, b_ref, da_ref, db_ref, dg_ref, dbg_ref):
        @pl.when(pl.program_id(0) == 0)
        def _():
            dbg_ref[...] = jnp.zeros_like(dbg_ref)

        g = jax.nn.sigmoid(g_ref[...] + bg_ref[...])
        d = d_ref[...]
        da_ref[...] = (d * g[:, :D]).astype(BF16)
        db_ref[...] = (d * g[:, D:]).astype(BF16)
        dg = jnp.concatenate([d * a_ref[...], d * b_ref[...]], axis=1) * g * (1.0 - g)
        dg_ref[...] = dg.astype(BF16)
        dbg_ref[...] += jnp.sum(dg, axis=0, keepdims=True)

    return pl.pallas_call(
        body, grid=(T // tr,), in_specs=[_row(tr, D), _row(tr, 2 * D), _vec(2 * D), _row(tr, D), _row(tr, D)],
        out_specs=[_row(tr, D), _row(tr, D), _row(tr, 2 * D), _vec(2 * D)],
        out_shape=[_sds((T, D), BF16), _sds((T, D), BF16), _sds((T, 2 * D), BF16), _sds((1, 2 * D), F32)],
        name=name, compiler_params=_params("arbitrary"))(dmerged, gate_raw, b_gate, br_a, br_b)


CB_W = 256
CONV_ROWS = 32
CONV_PAD = 8


def _rows_down(load, r0, s):
    if s == 0:
        return load(r0, r0 + CONV_ROWS)
    if r0 == 0:
        row = lax.broadcasted_iota(jnp.int32, (CONV_ROWS, CB_W), 0)
        return jnp.where(row >= s, pltpu.roll(load(0, CONV_ROWS), s, 0), 0.0)
    return load(r0 - s, r0 - s + CONV_ROWS)


def _conv_tile(load, taps, r0):
    K = len(taps)
    us = [_rows_down(load, r0, K - 1 - k) for k in range(K)]
    acc = us[K - 1] * taps[K - 1]
    for k in range(K - 1):
        acc = acc + us[k] * taps[k]
    return acc, us


def _conv_back_tile(scr, taps, r0):
    K = len(taps)
    du = scr[r0:r0 + CONV_ROWS, :] * taps[K - 1]
    for k in range(K - 1):
        s = K - 1 - k
        du = du + scr[r0 + s:r0 + s + CONV_ROWS, :] * taps[k]
    return du


def _fold8(v):
    return jnp.sum(v.reshape(CONV_ROWS // 8, 8, v.shape[1]), axis=0)


def _col(T, j0=0):
    return pl.BlockSpec((T, CB_W), lambda j: (0, j + j0))


def _sc_fwd(psc, w, name):
    T, D = psc.shape[0], psc.shape[1] // 3
    nb = D // CB_W

    def body(b_ref, c_ref, x_ref, w_ref, o_ref):
        taps = [w_ref[k:k + 1, :] for k in range(SC_K)]
        load = lambda a, b: c_ref[a:b, :] * x_ref[a:b, :]
        for r0 in range(0, T, CONV_ROWS):
            cu, _ = _conv_tile(load, taps, r0)
            o_ref[r0:r0 + CONV_ROWS, :] = (b_ref[r0:r0 + CONV_ROWS, :] * cu).astype(BF16)

    return pl.pallas_call(
        body, grid=(nb,), in_specs=[_col(T), _col(T, nb), _col(T, 2 * nb), pl.BlockSpec((SC_K, CB_W), lambda j: (0, j))],
        out_specs=_col(T), out_shape=_sds((T, D), BF16), name=name, compiler_params=_params("parallel"))(psc, psc, psc, w)


def _sc_bwd(psc, w, dya, name):
    T, D = psc.shape[0], psc.shape[1] // 3
    nb = D // CB_W

    def body(b_ref, c_ref, x_ref, w_ref, d_ref, db_ref, dc_ref, dx_ref, dw_ref, scr):
        taps = [w_ref[k:k + 1, :] for k in range(SC_K)]
        load = lambda a, b: c_ref[a:b, :] * x_ref[a:b, :]
        scr[T:T + CONV_PAD, :] = jnp.zeros((CONV_PAD, CB_W), F32)
        dw8 = [jnp.zeros((8, CB_W), F32)] * SC_K
        for r0 in range(0, T, CONV_ROWS):
            rows = slice(r0, r0 + CONV_ROWS)
            cu, us = _conv_tile(load, taps, r0)
            d = d_ref[rows, :]
            db_ref[rows, :] = (d * cu).astype(BF16)
            dcu = d * b_ref[rows, :]
            scr[rows, :] = dcu
            dw8 = [acc + _fold8(dcu * u) for acc, u in zip(dw8, us)]
        for k in range(SC_K):
            dw_ref[k:k + 1, :] = jnp.sum(dw8[k], axis=0, keepdims=True)
        for r0 in range(0, T, CONV_ROWS):
            rows = slice(r0, r0 + CONV_ROWS)
            du = _conv_back_tile(scr, taps, r0)
            dc_ref[rows, :] = (du * x_ref[rows, :]).astype(BF16)
            dx_ref[rows, :] = (du * c_ref[rows, :]).astype(BF16)

    wspec = pl.BlockSpec((SC_K, CB_W), lambda j: (0, j))
    return pl.pallas_call(
        body, grid=(nb,), in_specs=[_col(T), _col(T, nb), _col(T, 2 * nb), wspec, _col(T)],
        out_specs=[_col(T), _col(T), _col(T), wspec],
        out_shape=[_sds((T, D), BF16)] * 3 + [_sds((SC_K, D), F32)],
        scratch_shapes=[pltpu.VMEM((T + CONV_PAD, CB_W), F32)],
        name=name, compiler_params=_params("parallel"))(psc, psc, psc, w, dya)


def _ssm_conv_fwd(u, w, b, name, comm=None):
    T, N = u.shape

    def body(u_ref, w_ref, b_ref, o_ref):
        taps = [w_ref[k:k + 1, :] for k in range(SSM_K)]
        bias = b_ref[...]
        for r0 in range(0, T, CONV_ROWS):
            c, _ = _conv_tile(lambda a, b: u_ref[a:b, :], taps, r0)
            c = c + bias
            o_ref[r0:r0 + CONV_ROWS, :] = c * jax.nn.sigmoid(c)

    outs, carried = _call(
        body, grid=(N // CB_W,), in_specs=[_col(T), pl.BlockSpec((SSM_K, CB_W), lambda j: (0, j)), pl.BlockSpec((1, CB_W), lambda j: (0, j))],
        out_specs=[_col(T)], out_shape=[_sds((T, N), F32)], args=[u, w, b], name=name, sem=("parallel",), comm=comm)
    return outs[0] if comm is None else (outs[0], carried)


def _ssm_conv_bwd(u, w, b, dxs, dB, dC, name, comm=None):
    T, N = u.shape
    n_x, n_b = dxs.shape[1] // CB_W, dB.shape[1] // CB_W

    def body(u_ref, w_ref, b_ref, dx_ref, db_ref, dc_ref, du_ref, dw_ref, dbias_ref, scr):
        j = pl.program_id(0)
        taps = [w_ref[k:k + 1, :] for k in range(SSM_K)]
        bias = b_ref[...]
        scr[T:T + CONV_PAD, :] = jnp.zeros((CONV_PAD, CB_W), F32)
        dw8 = [jnp.zeros((8, CB_W), F32)] * SSM_K
        db8 = jnp.zeros((8, CB_W), F32)
        for r0 in range(0, T, CONV_ROWS):
            rows = slice(r0, r0 + CONV_ROWS)
            c, us = _conv_tile(lambda a, b: u_ref[a:b, :], taps, r0)
            _, dsilu = _silu_parts(c + bias)
            d = jnp.where(j < n_x, dx_ref[rows, :], jnp.where(j < n_x + n_b, db_ref[rows, :], dc_ref[rows, :])) * dsilu
            scr[rows, :] = d
            db8 = db8 + _fold8(d)
            dw8 = [acc + _fold8(d * u) for acc, u in zip(dw8, us)]
        dbias_ref[...] = jnp.sum(db8, axis=0, keepdims=True)
        for k in range(SSM_K):
            dw_ref[k:k + 1, :] = jnp.sum(dw8[k], axis=0, keepdims=True)
        for r0 in range(0, T, CONV_ROWS):
            du_ref[r0:r0 + CONV_ROWS, :] = _conv_back_tile(scr, taps, r0).astype(BF16)

    wspec = pl.BlockSpec((SSM_K, CB_W), lambda j: (0, j))
    bspec = pl.BlockSpec((1, CB_W), lambda j: (0, j))
    outs, carried = _call(
        body, grid=(N // CB_W,),
        in_specs=[_col(T), wspec, bspec,
                  pl.BlockSpec((T, CB_W), lambda j: (0, jnp.minimum(j, n_x - 1))),
                  pl.BlockSpec((T, CB_W), lambda j: (0, jnp.clip(j - n_x, 0, n_b - 1))),
                  pl.BlockSpec((T, CB_W), lambda j: (0, jnp.clip(j - n_x - n_b, 0, n_b - 1)))],
        out_specs=[_col(T), wspec, bspec],
        out_shape=[_sds((T, N), BF16), _sds((SSM_K, N), F32), _sds((1, N), F32)],
        scratch=[pltpu.VMEM((T + CONV_PAD, CB_W), F32)],
        args=[u, w, b, dxs, dB, dC], name=name, sem=("parallel",), comm=comm)
    return outs if comm is None else (outs, carried)


def _split3(v):
    hi = v.astype(BF16)
    r = v - hi.astype(F32)
    mid = r.astype(BF16)
    lo = (r - mid.astype(F32)).astype(BF16)
    return hi, mid, lo


def _head_expand(n_lanes):
    h = lax.broadcasted_iota(jnp.int32, (LANES, n_lanes), 0)
    l = lax.broadcasted_iota(jnp.int32, (LANES, n_lanes), 1)
    return (jnp.right_shift(l, HEADDIM.bit_length() - 1) == h).astype(BF16)


def _softplus(v):
    return jnp.maximum(v, 0.0) + jnp.log1p(jnp.exp(-jnp.abs(v)))


def _ssd_prep(dt_raw, dt_bias, a_log, n_inner, name):
    T = dt_raw.shape[0]

    def body(r_ref, b_ref, al_ref, dt_ref, cs_ref):
        dt = _softplus(r_ref[...] + b_ref[...])
        a = dt * (-jnp.exp(al_ref[...]))
        i = lax.broadcasted_iota(jnp.int32, (CHUNK, CHUNK), 0)
        j = lax.broadcasted_iota(jnp.int32, (CHUNK, CHUNK), 1)
        tri = (j <= i).astype(BF16)
        cs = sum(_dot(tri, p) for p in _split3(a))
        ex = _head_expand(n_inner)
        dt_ref[...] = sum(_dot(p, ex) for p in _split3(dt))
        cs_ref[...] = sum(_dot(p, ex) for p in _split3(cs))

    blk = pl.BlockSpec((CHUNK, LANES), lambda c: (c, 0))
    out = pl.BlockSpec((CHUNK, n_inner), lambda c: (c, 0))
    return pl.pallas_call(body, grid=(T // CHUNK,), in_specs=[blk, _vec(LANES), _vec(LANES)], out_specs=[out, out],
                          out_shape=[_sds((T, n_inner), F32)] * 2, name=name, compiler_params=_params("parallel"))(dt_raw, dt_bias, a_log)


def _pair_terms(cs_p):
    lane = lax.broadcasted_iota(jnp.int32, (CHUNK, CHUNK), 1)
    sub = lax.broadcasted_iota(jnp.int32, (CHUNK, CHUNK), 0)
    csT = cs_p.T
    Ls = []
    for k in range(2):
        col = jnp.sum(jnp.where(lane == k * HEADDIM, cs_p, 0.0), axis=1, keepdims=True)
        rowv = csT[k * HEADDIM:k * HEADDIM + 1, :]
        Ls.append(jnp.exp(jnp.where(sub >= lane, col - rowv, -jnp.inf)))
    return Ls, jnp.exp(csT[:, CHUNK - 1:CHUNK])


def _block_diag(xp):
    lane = lax.broadcasted_iota(jnp.int32, xp.shape, 1)
    return jnp.concatenate([jnp.where(lane < HEADDIM, xp, 0.0), jnp.where(lane >= HEADDIM, xp, 0.0)], axis=0)


SSD_GROUPS_PER_STEP = 8


def _ssd_specs(T, n_inner):
    nc, gs = T // CHUNK, SSD_GROUPS_PER_STEP
    bo, co = n_inner // (gs * NSTATE), (n_inner + NGROUPS * NSTATE) // (gs * NSTATE)
    assert NGROUPS % gs == 0 and n_inner % (gs * NSTATE) == 0 and (NGROUPS * NSTATE) % (gs * NSTATE) == 0
    g_blk = lambda f: pl.BlockSpec((CHUNK, gs * GROUP_W), lambda c, s: (f(c), s))
    b_blk = lambda f: pl.BlockSpec((CHUNK, gs * NSTATE), lambda c, s: (f(c), bo + s))
    c_blk = lambda f: pl.BlockSpec((CHUNK, gs * NSTATE), lambda c, s: (f(c), co + s))
    return nc, g_blk, b_blk, c_blk


def _ssd_fwd(xbc, dt_e, cs_e, d_e, name, comm=None):
    T = xbc.shape[0]
    n_inner = dt_e.shape[1]
    nc, g_blk, b_blk, c_blk = _ssd_specs(T, n_inner)
    ident = lambda c: c

    gs = SSD_GROUPS_PER_STEP

    def body(xs_ref, b_ref, c_ref, dt_ref, cs_ref, d_ref, y_ref, p_ref, st):
        c, s = pl.program_id(0), pl.program_id(1)

        @pl.when(c == 0)
        def _():
            for gi in range(gs):
                st[s * gs + gi] = jnp.zeros((GROUP_W, NSTATE), F32)

        for gi in range(gs):
            g = s * gs + gi
            gw, gn = slice(gi * GROUP_W, (gi + 1) * GROUP_W), slice(gi * NSTATE, (gi + 1) * NSTATE)
            P = st[g]
            p_ref[0, gi] = P
            xs, dt, cs = xs_ref[:, gw], dt_ref[:, gw], cs_ref[:, gw]
            Bf, Cf = b_ref[:, gn], c_ref[:, gn]
            CBm = _dot3(Cf, Bf, NT)
            X = xs * dt
            decay = jnp.exp(cs[CHUNK - 1:CHUNK, :] - cs)
            y_off = _dot3(Cf, P, NT) * jnp.exp(cs)
            ys, ecl = [], []
            for pr in range(2):
                sl = slice(pr * LANES, (pr + 1) * LANES)
                Ls, e_last = _pair_terms(cs[:, sl])
                ecl.append(e_last)
                Mcat = jnp.concatenate([CBm * L for L in Ls], axis=1)
                ys.append(_dot3(Mcat, _block_diag(X[:, sl])))
            y_ref[:, gw] = jnp.concatenate(ys, axis=1) + y_off + xs * d_ref[:, gw]
            S = _dot3(X * decay, Bf, TN)
            st[g] = P * jnp.concatenate(ecl, axis=0) + S

    p_blk = pl.BlockSpec((1, gs, GROUP_W, NSTATE), lambda c, s: (c, s, 0, 0))
    outs, carried = _call(
        body, grid=(nc, NGROUPS // gs),
        in_specs=[g_blk(ident), b_blk(ident), c_blk(ident), g_blk(ident), g_blk(ident), pl.BlockSpec((1, gs * GROUP_W), lambda c, s: (0, s))],
        out_specs=[g_blk(ident), p_blk],
        out_shape=[_sds((T, n_inner), F32), _sds((nc, NGROUPS, GROUP_W, NSTATE), F32)],
        scratch=[pltpu.VMEM((NGROUPS, GROUP_W, NSTATE), F32)],
        args=[xbc, xbc, xbc, dt_e, cs_e, d_e], name=name, sem=("arbitrary", "arbitrary"), comm=comm)
    return outs if comm is None else (outs, carried)


def _ssd_bwd(xbc, dt_e, cs_e, d_e, states, dy, name, comm=None):
    T = xbc.shape[0]
    n_inner = dt_e.shape[1]
    nc, g_blk, b_blk, c_blk = _ssd_specs(T, n_inner)
    rev = lambda c: nc - 1 - c

    gs = SSD_GROUPS_PER_STEP

    def body(xs_ref, b_ref, c_ref, dt_ref, cs_ref, d_ref, p_ref, pn_ref, dy_ref,
             dxs_ref, db_ref, dc_ref, ddt_ref, dcs_ref, dd_ref, dst):
        cc, s = pl.program_id(0), pl.program_id(1)

        @pl.when(cc == 0)
        def _():
            for gi in range(gs):
                dst[s * gs + gi] = jnp.zeros((GROUP_W, NSTATE), F32)

        for gi in range(gs):
            one_group(s * gs + gi, gi, xs_ref, b_ref, c_ref, dt_ref, cs_ref, d_ref, p_ref, pn_ref, dy_ref,
                      dxs_ref, db_ref, dc_ref, ddt_ref, dcs_ref, dd_ref, dst)

    def one_group(g, gi, xs_ref, b_ref, c_ref, dt_ref, cs_ref, d_ref, p_ref, pn_ref, dy_ref,
                  dxs_ref, db_ref, dc_ref, ddt_ref, dcs_ref, dd_ref, dst):
        gw, gn = slice(gi * GROUP_W, (gi + 1) * GROUP_W), slice(gi * NSTATE, (gi + 1) * NSTATE)
        dS = dst[g]
        P, Pn = p_ref[0, gi], pn_ref[0, gi]
        xs, dt, cs, dY = xs_ref[:, gw], dt_ref[:, gw], cs_ref[:, gw], dy_ref[:, gw]
        Bf, Cf = b_ref[:, gn], c_ref[:, gn]
        Bb, Cb = Bf.astype(BF16), Cf.astype(BF16)
        X = xs * dt
        ecs = jnp.exp(cs)
        decay = jnp.exp(cs[CHUNK - 1:CHUNK, :] - cs)
        CBm = _dot3(Cf, Bf, NT)
        dYe = dY * ecs
        dP_off = _dot3(dYe, Cf, TN)
        dC = _dot(dYe.astype(BF16), P.astype(BF16))
        dcs = dYe * _dot3(Cf, P, NT)
        Xd = X * decay
        dB = _dot(Xd.astype(BF16), dS.astype(BF16))
        E = _dot3(Bf, dS, NT)
        dX = E * decay
        dcs = dcs - E * Xd
        R = _dot3(jnp.ones((8, NSTATE), F32), dS * Pn, NT)
        sub_g = lax.broadcasted_iota(jnp.int32, (CHUNK, GROUP_W), 0)
        dcs = dcs + jnp.where(sub_g == CHUNK - 1, R[0:1, :], 0.0)
        lane = lax.broadcasted_iota(jnp.int32, (CHUNK, CHUNK), 1)
        sub = lax.broadcasted_iota(jnp.int32, (CHUNK, CHUNK), 0)
        dCB = jnp.zeros((CHUNK, CHUNK), F32)
        dXs, dcss, ecl = [], [], []
        for pr in range(2):
            sl = slice(pr * LANES, (pr + 1) * LANES)
            Ls, e_last = _pair_terms(cs[:, sl])
            ecl.append(e_last)
            dYp = dY[:, sl]
            dMcat = _dot3(dYp, _block_diag(X[:, sl]), NT)
            Mcat = jnp.concatenate([CBm * L for L in Ls], axis=1)
            dXt = _dot3(Mcat, dYp, TN)
            dXs.append(jnp.where(lane < HEADDIM, dXt[:CHUNK], dXt[CHUNK:]))
            colacc = jnp.zeros((CHUNK, CHUNK), F32)
            rowacc = jnp.zeros((CHUNK, CHUNK), F32)
            for k in range(2):
                dG = dMcat[:, k * CHUNK:(k + 1) * CHUNK] * Ls[k]
                dCB = dCB + dG
                Q = dG * CBm
                colacc = colacc + jnp.where(lane == k * HEADDIM, jnp.sum(Q, axis=1, keepdims=True), 0.0)
                rowacc = rowacc + jnp.where(sub == k * HEADDIM, jnp.sum(Q, axis=0, keepdims=True), 0.0)
            dcss.append(colacc - rowacc.T)
        dX = dX + jnp.concatenate(dXs, axis=1)
        dcs = dcs + jnp.concatenate(dcss, axis=1)
        dCBb = dCB.astype(BF16)
        dc_ref[:, gn] = dC + _dot(dCBb, Bb)
        db_ref[:, gn] = dB + _dot(dCBb, Cb, TN)
        dxs_ref[:, gw] = dX * dt + dY * d_ref[:, gw]
        ddt_ref[:, gw] = dX * xs
        dcs_ref[:, gw] = dcs
        dd_ref[0, :, gw] = jnp.sum(dY * xs, axis=0, keepdims=True)
        dst[g] = dS * jnp.concatenate(ecl, axis=0) + dP_off

    p_blk = pl.BlockSpec((1, gs, GROUP_W, NSTATE), lambda c, s: (nc - 1 - c, s, 0, 0))
    pn_blk = pl.BlockSpec((1, gs, GROUP_W, NSTATE), lambda c, s: (jnp.minimum(nc - c, nc - 1), s, 0, 0))
    st_blk = pl.BlockSpec((CHUNK, gs * NSTATE), lambda c, s: (nc - 1 - c, s))
    outs, carried = _call(
        body, grid=(nc, NGROUPS // gs),
        in_specs=[g_blk(rev), b_blk(rev), c_blk(rev), g_blk(rev), g_blk(rev), pl.BlockSpec((1, gs * GROUP_W), lambda c, s: (0, s)),
                  p_blk, pn_blk, g_blk(rev)],
        out_specs=[g_blk(rev), st_blk, st_blk, g_blk(rev), g_blk(rev), pl.BlockSpec((1, 1, gs * GROUP_W), lambda c, s: (nc - 1 - c, 0, s))],
        out_shape=[_sds((T, n_inner), F32), _sds((T, NGROUPS * NSTATE), F32), _sds((T, NGROUPS * NSTATE), F32),
                   _sds((T, n_inner), F32), _sds((T, n_inner), F32), _sds((nc, 1, n_inner), F32)],
        scratch=[pltpu.VMEM((NGROUPS, GROUP_W, NSTATE), F32)],
        args=[xbc, xbc, xbc, dt_e, cs_e, d_e, states, states, dy], name=name, sem=("arbitrary", "arbitrary"), comm=comm)
    return outs if comm is None else (outs, carried)


def _ssd_post(ddt_e, dcs_e, dd_p, dt_raw, dt_bias, a_log, n_heads, name):
    T, n_inner = ddt_e.shape

    def body(ddt_ref, dcs_ref, dd_ref, r_ref, b_ref, al_ref, draw_ref, dbias_ref, dal_ref, ddsk_ref):
        @pl.when(pl.program_id(0) == 0)
        def _():
            dbias_ref[...] = jnp.zeros_like(dbias_ref)
            dal_ref[...] = jnp.zeros_like(dal_ref)
            ddsk_ref[...] = jnp.zeros_like(ddsk_ref)

        ex = _head_expand(n_inner)
        red = lambda v: sum(_dot(p, ex, NT) for p in _split3(v))
        raw = r_ref[...] + b_ref[...]
        dt = _softplus(raw)
        A = -jnp.exp(al_ref[...])
        i = lax.broadcasted_iota(jnp.int32, (CHUNK, CHUNK), 0)
        j = lax.broadcasted_iota(jnp.int32, (CHUNK, CHUNK), 1)
        upper = (j >= i).astype(BF16)
        da = sum(_dot(upper, p) for p in _split3(red(dcs_ref[...])))
        ddt = red(ddt_ref[...]) + da * A
        lane = lax.broadcasted_iota(jnp.int32, (CHUNK, LANES), 1)
        draw = jnp.where(lane < n_heads, ddt * jax.nn.sigmoid(raw), 0.0)
        draw_ref[...] = draw.astype(BF16)
        dbias_ref[...] += jnp.sum(draw, axis=0, keepdims=True)
        dal_ref[...] += jnp.sum(da * dt, axis=0, keepdims=True) * A
        ddsk_ref[...] += red(jnp.broadcast_to(dd_ref[0], (8, n_inner)))[0:1, :]

    wide = pl.BlockSpec((CHUNK, n_inner), lambda c: (c, 0))
    blk = pl.BlockSpec((CHUNK, LANES), lambda c: (c, 0))
    return pl.pallas_call(
        body, grid=(T // CHUNK,),
        in_specs=[wide, wide, pl.BlockSpec((1, 1, n_inner), lambda c: (c, 0, 0)), blk, _vec(LANES), _vec(LANES)],
        out_specs=[blk, _vec(LANES), _vec(LANES), _vec(LANES)],
        out_shape=[_sds((T, LANES), BF16)] + [_sds((1, LANES), F32)] * 3,
        name=name, compiler_params=_params("arbitrary"))(ddt_e, dcs_e, dd_p, dt_raw, dt_bias, a_log)


def _row2(v):
    return v.reshape(1, -1).astype(F32)


def _pad_lanes(v):
    return jnp.pad(_row2(v), ((0, 0), (0, LANES - v.shape[-1])))


class _NoExchange:
    def __init__(self, W):
        self.W, self.grads = W, {}

    def weight(self, k):
        return self.W[k]

    def carry(self, name):
        return None

    def carried(self, name, outs):
        pass

    def grad(self, k, g):
        self.grads[k] = g

    def tok(self):
        return jnp.zeros((), F32)

    def point(self, name, value):
        pass


def _local_step(x, tgt, S, small):
    T, D = x.shape

    def mm(a, b, *, name, **kw):
        comm = S.carry(name)
        if comm is None:
            return _mm(a, b, name=name, **kw)
        res, outs = _mm(a, b, name=name, comm=comm, **kw)
        S.carried(name, outs)
        return res

    def carrying(fn, *args, name):
        comm = S.carry(name)
        if comm is None:
            return fn(*args, name)
        res, outs = fn(*args, name, comm=comm)
        S.carried(name, outs)
        return res

    n_inner = 2 * D
    n_heads = n_inner // HEADDIM
    norm_mix, norm_mlp, norm_final = _row2(small["norm_mix"]), _row2(small["norm_mlp"]), _row2(small["norm_final"])
    b_gate, ssm_b, ssm_norm_w = _row2(small["b_gate"]), _row2(small["ssm_conv_b"]), _row2(small["ssm_norm_w"])
    dt_bias, a_log = _pad_lanes(small["dt_bias"]), _pad_lanes(small["A_log"])
    d_e = jnp.repeat(small["D_skip"].astype(F32), HEADDIM).reshape(1, n_inner)
    sc_w, ssm_w = small["sc_conv_w"], small["ssm_conv_w"]

    hb = _rms_fwd(x, norm_mix + S.tok(), "rms_mix")
    p_xbc = mm(hb, S.weight("xbc"), mode="nn", name="proj_xbc")
    p_dt = mm(hb, S.weight("dt"), mode="nn", name="proj_dt")
    p_z = mm(hb, S.weight("z"), mode="nn", name="proj_z")
    p_sc = mm(hb, S.weight("sc"), mode="nn", name="proj_sc")
    p_gate = mm(hb, S.weight("gate"), mode="nn", name="proj_gate")
    xbc = carrying(_ssm_conv_fwd, p_xbc, ssm_w, ssm_b, name="ssm_conv_fwd")
    dt_e, cs_e = _ssd_prep(p_dt, dt_bias, a_log, n_inner, "ssd_prep")
    y, states = carrying(_ssd_fwd, xbc, dt_e, cs_e, d_e, name="ssd_fwd")
    S.point("ssd_fwd_done", y)
    yb = carrying(_gnorm_fwd, y, p_z, ssm_norm_w, name="gnorm_fwd")
    ya = _sc_fwd(p_sc, sc_w, "sc_fwd")
    br_a = mm(ya, S.weight("bsc"), mode="nn", name="branch_sc")
    br_b = mm(yb, S.weight("bssm"), mode="nn", name="branch_ssm")
    merged = _merge_fwd(p_gate, b_gate, br_a, br_b, "merge_fwd")
    x1 = mm(merged, S.weight("out"), mode="nn", name="out_proj", extras=(x,), epi=_epi_add)
    h2 = _rms_fwd(x1, norm_mlp, "rms_mlp")
    r_act = mm(h2, S.weight("w1"), mode="nn", name="mlp_up", epi=_epi_relu2, out_dtypes=(BF16,))
    x2 = mm(r_act, S.weight("w2"), mode="nn", name="mlp_down", extras=(x1,), epi=_epi_add)
    dx2, dx2b, g_norm_final, loss_row = _final(x2, norm_final, tgt, "final")

    S.grad("w2", mm(r_act, dx2b, mode="tn", name="mlp_down_dw", out_dtypes=(BF16,)))
    da = mm(dx2b, S.weight("w2"), mode="nt", name="mlp_down_dx", extras=(r_act,), epi=_epi_relu2_bwd, out_dtypes=(BF16,))
    S.grad("w1", mm(h2, da, mode="tn", name="mlp_up_dw", out_dtypes=(BF16,)))
    dh2 = mm(da, S.weight("w1"), mode="nt", name="mlp_up_dx")
    dx1, dx1b, g_norm_mlp = _rms_bwd(x1, norm_mlp + S.tok(), dh2, dx2, "rms_mlp_bwd")
    S.grad("out", mm(merged, dx1b, mode="tn", name="out_proj_dw", out_dtypes=(BF16,)))
    dmerged = mm(dx1b, S.weight("out"), mode="nt", name="out_proj_dx")
    dbr_a, dbr_b, d_gate, g_b_gate = _merge_bwd(dmerged, p_gate, b_gate, br_a, br_b, "merge_bwd")
    S.grad("bssm", mm(yb, dbr_b, mode="tn", name="branch_ssm_dw", out_dtypes=(BF16,)))
    S.grad("bsc", mm(ya, dbr_a, mode="tn", name="branch_sc_dw", out_dtypes=(BF16,)))
    dyb = mm(dbr_b, S.weight("bssm"), mode="nt", name="branch_ssm_dx")
    dya = mm(dbr_a, S.weight("bsc"), mode="nt", name="branch_sc_dx")
    dy, d_z, g_ssm_norm_w = _gnorm_bwd(y, p_z, ssm_norm_w + S.tok(), dyb, "gnorm_bwd")
    dxs, dB, dC, ddt_e, dcs_e, dd_p = carrying(_ssd_bwd, xbc, dt_e, cs_e, d_e, states, dy, name="ssd_bwd")
    d_dt, g_dt_bias, g_a_log, g_d_skip = _ssd_post(ddt_e, dcs_e, dd_p, p_dt, dt_bias, a_log, n_heads, "ssd_post")
    d_xbc, g_ssm_w, g_ssm_b = carrying(_ssm_conv_bwd, p_xbc, ssm_w, ssm_b, dxs, dB, dC, name="ssm_conv_bwd")
    d_scB, d_scC, d_scX, g_sc_w = _sc_bwd(p_sc, sc_w, dya, "sc_bwd")
    d_sc = jnp.concatenate([d_scB, d_scC, d_scX], axis=1)
    pieces = [("sc", d_sc), ("z", d_z), ("xbc", d_xbc), ("dt", d_dt), ("gate", d_gate)]
    S.grad("win", {k: mm(hb, d, mode="tn", name="proj_dw_" + k, out_dtypes=(BF16,)) for k, d in pieces})
    pieces = [(k, d + S.tok().astype(d.dtype) if k == "dt" else d) for k, d in pieces]
    dh = mm([d for _, d in pieces], [S.weight(k) for k, _ in pieces], mode="nt", name="proj_dx")
    grad_x, _, g_norm_mix = _rms_bwd(x, norm_mix, dh, dx1, "rms_mix_bwd")

    g_small = dict(norm_mix=g_norm_mix, b_gate=g_b_gate, sc_conv_w=g_sc_w, ssm_conv_w=g_ssm_w, ssm_conv_b=g_ssm_b,
                   dt_bias=g_dt_bias, A_log=g_a_log, D_skip=g_d_skip, ssm_norm_w=g_ssm_norm_w, norm_mlp=g_norm_mlp,
                   norm_final=g_norm_final, loss=loss_row)
    return grad_x, g_small


class _Place:
    def __init__(self, k=0):
        x, y, c = lax.axis_index("x"), lax.axis_index("y"), lax.axis_index("c")
        self.x = 1 - x if k & 4 else x
        self.y = 1 - y if k & 2 else y
        self.c = 1 - c if k & 1 else c
        self.chip = 2 * self.x + self.y
        self.id = 2 * self.chip + self.c


ICI_PEERS = (2, 4, 6)
SIBLING = (1,)
ALL_PEERS = (1, 2, 3, 4, 5, 6, 7)


class _Comm:
    def __init__(self, arrs, out_shape, ks, src, dst, own=None, aliases=None):
        self.arrs, self.out_shape, self.ks = list(arrs), list(out_shape), tuple(ks)
        self.n = len(self.arrs)
        self.src, self.dst, self.own = src, dst, own
        self.aliases = aliases or {}
        dma = pltpu.SemaphoreType.DMA
        self.scratch = [dma((self.n, len(self.ks))), dma((self.n, len(self.ks))), dma((self.n,))]

    def _copies(self, ins, outs, sems, with_recvs):
        send_sems, recv_sems, local_sems = sems
        me = _Place()
        owns, sends, recvs = [], [], []
        for a in range(self.n):
            if self.own is not None:
                s, d = self.own(a, ins[a], outs[a], me)
                owns.append(pltpu.make_async_copy(s, d, local_sems.at[a]))
            for i, k in enumerate(self.ks):
                peer = _Place(k)
                for sender, lst in ((me, sends), (peer, recvs)) if with_recvs else ((me, sends),):
                    lst.append(pltpu.make_async_remote_copy(
                        src_ref=self.src(a, ins[a], me, peer), dst_ref=self.dst(a, outs[a], sender),
                        send_sem=send_sems.at[a, i], recv_sem=recv_sems.at[a, i],
                        device_id=(peer.x, peer.y, peer.c), device_id_type=MESH))
        return owns, sends, recvs

    def start(self, ins, outs, sems):
        owns, sends, _ = self._copies(ins, outs, sems, False)
        for cp in owns + sends:
            cp.start()

    def finish(self, ins, outs, sems):
        owns, sends, recvs = self._copies(ins, outs, sems, True)
        for cp in recvs:
            cp.wait_recv()
        for cp in sends:
            cp.wait_send()
        for cp in owns:
            cp.wait()


def _run_comm(comm, name, after=()):
    n, n_after = comm.n, len(after)

    def body(*refs):
        ins, outs, sems = refs[:n], refs[n + n_after:2 * n + n_after], refs[2 * n + n_after:]
        comm.start(ins, outs, sems)
        comm.finish(ins, outs, sems)

    return list(pl.pallas_call(body, in_specs=[ANY] * (n + n_after), out_specs=[ANY] * n, out_shape=comm.out_shape,
                               scratch_shapes=comm.scratch, input_output_aliases=dict(comm.aliases), name=name)(*comm.arrs, *after))


def _gather_ici(shards):
    return _Comm(shards, [_sds((4, 2) + s.shape, s.dtype) for s in shards], ICI_PEERS,
                 src=lambda a, i, me, p: i, dst=lambda a, o, s: o.at[s.chip, s.c], own=lambda a, i, o, me: (i, o.at[me.chip, me.c]))


def _gather_sibling(bufs):
    return _Comm(bufs, [_sds(b.shape, b.dtype) for b in bufs], SIBLING,
                 src=lambda a, i, me, p: i.at[:, me.c], dst=lambda a, o, s: o.at[:, s.c], aliases={a: a for a in range(len(bufs))})


def _scatter_sibling(parts):
    return _Comm(parts, [_sds((4,) + p.shape[2:], p.dtype) for p in parts], SIBLING,
                 src=lambda a, i, me, p: i.at[:, p.c], dst=lambda a, o, s: o)


def _scatter_ici(parts):
    return _Comm(parts, [_sds(p.shape, p.dtype) for p in parts], ICI_PEERS,
                 src=lambda a, i, me, p: i.at[p.chip], dst=lambda a, o, s: o.at[s.chip], own=lambda a, i, o, me: (i.at[me.chip], o.at[me.chip]))


HBM_SPEC = pl.BlockSpec(memory_space=pltpu.HBM)
SEM_SPEC = pl.BlockSpec(memory_space=pltpu.SEMAPHORE)
DATAFLOW = pltpu.SideEffectType.DATAFLOW_SIDE_EFFECTING


def _own_part(parts, name):
    n, R, C = parts.shape
    tr = R if R <= 256 else 256
    chip = (2 * lax.axis_index("x") + lax.axis_index("y")).astype(jnp.int32).reshape(1)

    def body(q_ref, p_ref, o_ref):
        o_ref[...] = p_ref[...]

    blk = pl.BlockSpec((1, tr, C), lambda i, q_ref: (q_ref[0], i, 0))
    spec = pltpu.PrefetchScalarGridSpec(num_scalar_prefetch=1, grid=(R // tr,), in_specs=[blk], out_specs=blk)
    return pl.pallas_call(body, grid_spec=spec, out_shape=_sds((n, R, C), parts.dtype), name=name,
                          compiler_params=_params("parallel"))(chip, parts)


def _ici_copy(gather, a, srcs, lands, send_sems, recv_sems, i, me, peer, sender):
    src = lands[a].at[me.chip, me.c] if gather else srcs[a].at[peer.chip]
    dst = lands[a].at[sender.chip, sender.c] if gather else lands[a].at[sender.chip]
    j = a * len(ICI_PEERS) + i
    return pltpu.make_async_remote_copy(src_ref=src, dst_ref=dst, send_sem=send_sems.at[j], recv_sem=recv_sems.at[j],
                                        device_id=(peer.x, peer.y, peer.c), device_id_type=MESH)


def _ici_start(srcs, lands, gather, name):
    n, n_s = len(lands), len(srcs)
    bufs = list(srcs) + list(lands)

    def body(*refs):
        src_refs, land_refs = refs[:n_s], refs[n_s:n_s + n]
        send_sems, recv_sems = refs[n_s + n], refs[n_s + n + 1]
        token = refs[-1]
        me = _Place()
        for a in range(n):
            for i, k in enumerate(ICI_PEERS):
                _ici_copy(gather, a, src_refs, land_refs, send_sems, recv_sems, i, me, _Place(k), me).start()
        token[...] = jnp.zeros_like(token)

    dma = pltpu.SemaphoreType.DMA((n * len(ICI_PEERS),))
    outs = pl.pallas_call(
        body, name=name, out_shape=(dma, dma, *[pltpu.HBM(v.shape, v.dtype) for v in bufs], _sds((8, LANES), F32)),
        in_specs=(HBM_SPEC,) * len(bufs),
        out_specs=(SEM_SPEC, SEM_SPEC) + (HBM_SPEC,) * len(bufs) + (pl.BlockSpec(memory_space=pltpu.VMEM),),
        input_output_aliases={j: 2 + j for j in range(len(bufs))}, compiler_params=pltpu.CompilerParams(has_side_effects=DATAFLOW),
    )(*[pltpu.with_memory_space_constraint(v, pltpu.HBM) for v in bufs])
    return outs[0], outs[1], list(outs[2:2 + n_s]), list(outs[2 + n_s:2 + n_s + n]), outs[-1]


def _ici_wait(flight, after, gather, name):
    send_sems, recv_sems, srcs, lands, _ = flight
    n, n_s = len(lands), len(srcs)
    bufs = srcs + lands

    def body(*refs):
        src_refs, land_refs = refs[:n_s], refs[n_s:n_s + n]
        s_sems, r_sems = refs[n_s + n], refs[n_s + n + 1]
        me = _Place()
        for a in range(n):
            for i, k in enumerate(ICI_PEERS):
                peer = _Place(k)
                cp = _ici_copy(gather, a, src_refs, land_refs, s_sems, r_sems, i, me, peer, peer)
                cp.wait_send()
                cp.wait_recv()

    outs = pl.pallas_call(
        body, name=name, out_shape=tuple(pltpu.HBM(v.shape, v.dtype) for v in bufs),
        in_specs=(HBM_SPEC,) * len(bufs) + (SEM_SPEC, SEM_SPEC) + (ANY,) * len(after), out_specs=(HBM_SPEC,) * len(bufs),
        input_output_aliases={j: j for j in range(len(bufs))}, compiler_params=pltpu.CompilerParams(has_side_effects=DATAFLOW),
    )(*bufs, send_sems, recv_sems, *after)
    return list(outs[n_s:])


def _own_shard(shard, name):
    R, C = shard.shape
    tr = R if R <= 256 else 256
    place = jnp.stack([2 * lax.axis_index("x") + lax.axis_index("y"), lax.axis_index("c")]).astype(jnp.int32)

    def body(q_ref, s_ref, o_ref):
        o_ref[0, 0] = s_ref[...].astype(o_ref.dtype)

    spec = pltpu.PrefetchScalarGridSpec(
        num_scalar_prefetch=1, grid=(R // tr,), in_specs=[pl.BlockSpec((tr, C), lambda i, q_ref: (i, 0))],
        out_specs=pl.BlockSpec((1, 1, tr, C), lambda i, q_ref: (q_ref[0], q_ref[1], i, 0)))
    return pl.pallas_call(body, grid_spec=spec, out_shape=_sds((4, 2, R, C), BF16), name=name,
                          compiler_params=_params("parallel"))(place, shard)


def _col_pieces(widths):
    out, c = [], 0
    for k, w in widths:
        out.append((k, c, w))
        c += w
    return out


def _split_range(c0, n, bounds):
    parts, c = [], c0
    while c < c0 + n:
        r = max(i for i in range(len(bounds) - 1) if bounds[i] <= c)
        w = min(c0 + n, bounds[r + 1]) - c
        parts.append((r, c - bounds[r], w))
        c += w
    return parts


def _win_unpack(g, widths, name):
    n, R, C = g.shape
    tr = min(256, R)
    pieces = _col_pieces(widths)
    padded = [-(-w // LANES) * LANES for _, _, w in pieces]
    shard_bounds = [s * C for s in range(n + 1)]

    def body(g_ref, *o_refs):
        for (k, c0, w), o_ref in zip(pieces, o_refs):
            for t in range(0, o_ref.shape[1], LANES):
                valid = max(0, min(LANES, w - t))
                cols = [g_ref[s, :, o:o + ww] for s, o, ww in _split_range(c0 + t, valid, shard_bounds)] if valid else []
                if valid < LANES:
                    cols.append(jnp.zeros((tr, LANES - valid), g_ref.dtype))
                o_ref[:, t:t + LANES] = cols[0] if len(cols) == 1 else jnp.concatenate(cols, axis=1)

    return pl.pallas_call(
        body, grid=(R // tr,), in_specs=[pl.BlockSpec((n, tr, C), lambda i: (0, i, 0))],
        out_specs=[pl.BlockSpec((tr, p), lambda i: (i, 0)) for p in padded],
        out_shape=[_sds((R, p), g.dtype) for p in padded], name=name, compiler_params=_params("parallel"))(g)


def _win_pack(grads, widths, n, name):
    R = grads[0].shape[0]
    tr = min(256, R)
    pieces = _col_pieces(widths)
    total = pieces[-1][1] + pieces[-1][2]
    C = total // n
    bounds = [c0 for _, c0, _ in pieces] + [total]

    def body(*refs):
        g_refs, o_ref = refs[:-1], refs[-1]
        for s in range(n):
            for t in range(0, C, LANES):
                w = min(LANES, C - t)
                cols = [g_refs[r][:, o:o + ww] for r, o, ww in _split_range(s * C + t, w, bounds)]
                o_ref[s, :, t:t + w] = cols[0] if len(cols) == 1 else jnp.concatenate(cols, axis=1)

    return pl.pallas_call(
        body, grid=(R // tr,), in_specs=[pl.BlockSpec((tr, gr.shape[1]), lambda i: (i, 0)) for gr in grads],
        out_specs=pl.BlockSpec((n, tr, C), lambda i: (0, i, 0)), out_shape=_sds((n, R, C), grads[0].dtype),
        name=name, compiler_params=_params("parallel"))(*grads)


def _gather_all(arrs):
    return _Comm(arrs, [_sds((N_DEV,) + a.shape, a.dtype) for a in arrs], ALL_PEERS,
                 src=lambda a, i, me, p: i, dst=lambda a, o, s: o.at[s.id], own=lambda a, i, o, me: (i, o.at[me.id]))


def _add_halves(parts, got, name):
    n, _, R, C = parts.shape
    tr = R if R <= 256 else 256
    assert R % tr == 0
    core = lax.axis_index("c").astype(jnp.int32).reshape(1)

    def body(c_ref, p_ref, g_ref, o_ref):
        o_ref[0] = (p_ref[0, 0].astype(F32) + g_ref[0].astype(F32)).astype(o_ref.dtype)

    spec = pltpu.PrefetchScalarGridSpec(
        num_scalar_prefetch=1, grid=(n, R // tr),
        in_specs=[pl.BlockSpec((1, 1, tr, C), lambda q, i, c_ref: (q, c_ref[0], i, 0)), pl.BlockSpec((1, tr, C), lambda q, i, c_ref: (q, i, 0))],
        out_specs=pl.BlockSpec((1, tr, C), lambda q, i, c_ref: (q, i, 0)))
    return pl.pallas_call(body, grid_spec=spec, out_shape=_sds((n, R, C), parts.dtype), name=name,
                          compiler_params=_params("parallel", "parallel"))(core, parts, got)


def _adam(w, m, v, gparts, name):
    R, C = w.shape
    n = gparts.shape[0]
    tr = R if R <= 256 else 128
    assert R % tr == 0
    c1 = 1.0 / (1.0 - ADAM_B1 ** ADAM_STEP)
    c2 = 1.0 / (1.0 - ADAM_B2 ** ADAM_STEP)

    def body(w_ref, m_ref, v_ref, g_ref, go_ref, d_ref, mo_ref, vo_ref):
        g = g_ref[0].astype(F32)
        for s in range(1, n):
            g = g + g_ref[s].astype(F32)
        mn = ADAM_B1 * m_ref[...] + (1.0 - ADAM_B1) * g
        vn = ADAM_B2 * v_ref[...] + (1.0 - ADAM_B2) * (g * g)
        go_ref[...] = g
        mo_ref[...] = mn
        vo_ref[...] = vn
        d_ref[...] = -ADAM_LR * ((mn * c1) / (jnp.sqrt(vn * c2) + ADAM_EPS) + ADAM_WD * w_ref[...])

    blk = pl.BlockSpec((tr, C), lambda i: (i, 0))
    return pl.pallas_call(
        body, grid=(R // tr,), in_specs=[blk, blk, blk, pl.BlockSpec((n, tr, C), lambda i: (0, i, 0))],
        out_specs=[blk] * 4, out_shape=[_sds((R, C), F32)] * 4, name=name, compiler_params=_params("parallel"))(w, m, v, gparts)


_SMALL_ORDER = ("norm_mix", "b_gate", "sc_conv_w", "ssm_conv_w", "ssm_conv_b", "dt_bias", "A_log", "D_skip", "ssm_norm_w",
                "norm_mlp", "norm_final", "loss")
_REPLICATED = ("norm_mix", "b_gate", "ssm_conv_b", "dt_bias", "A_log", "D_skip", "ssm_norm_w", "norm_mlp", "norm_final")


def _cols_to_slots(g, n):
    R = g.shape[0]
    return jnp.transpose(g.reshape(R, n, g.shape[1] // n), (1, 0, 2))


def _slots_to_cols(g):
    n, R, C = g.shape
    return jnp.transpose(g, (1, 0, 2)).reshape(R, n * C)


def kernel(x, norm_mix, w_in, b_gate, sc_conv_w, ssm_conv_w, ssm_conv_b, dt_bias, A_log, D_skip, ssm_norm_w, w_branch_sc, w_branch_ssm, w_out, norm_mlp, w_mlp1, w_mlp2, norm_final, loss_target, m_norm_mix, m_w_in, m_b_gate, m_sc_conv_w, m_ssm_conv_w, m_ssm_conv_b, m_dt_bias, m_A_log, m_D_skip, m_ssm_norm_w, m_w_branch_sc, m_w_branch_ssm, m_w_out, m_norm_mlp, m_w_mlp1, m_w_mlp2, m_norm_final, v_norm_mix, v_w_in, v_b_gate, v_sc_conv_w, v_ssm_conv_w, v_ssm_conv_b, v_dt_bias, v_A_log, v_D_skip, v_ssm_norm_w, v_w_branch_sc, v_w_branch_ssm, v_w_out, v_norm_mlp, v_w_mlp1, v_w_mlp2, v_norm_final):
    T, D = x.shape[1], x.shape[2]
    n_inner = 2 * D
    n_heads = n_inner // HEADDIM
    n_xbc = n_inner + 2 * NGROUPS * NSTATE
    me = 4 * lax.axis_index("x") + 2 * lax.axis_index("y") + lax.axis_index("c")

    in_cols = [("sc", 3 * D), ("z", n_inner), ("xbc", n_xbc), ("dt", n_heads), ("gate", 2 * D)]
    by_owner = lambda b: b.reshape((N_DEV,) + b.shape[2:])
    to_owner = lambda g: g.reshape((4, 2) + g.shape[1:])
    rows_of = lambda g: to_owner(g.reshape((N_DEV, g.shape[0] // N_DEV) + g.shape[1:]))
    cols_of = lambda g: to_owner(_cols_to_slots(g, N_DEV))

    class Schedule(_NoExchange):
        late = ("bssm", "bsc", "out", "w1", "w2")
        gather_sib = dict(gnorm_fwd=("bsc", "bssm", "out"), branch_ssm=("w1", "w2"))
        scatter_sib = dict(mlp_up_dx=("w2", "w1"), branch_ssm_dx=("out", "bssm", "bsc"))
        shards = dict(bsc=w_branch_sc, bssm=w_branch_ssm, out=w_out, w1=w_mlp1, w2=w_mlp2)

        def __init__(self):
            bufs = _run_comm(_gather_ici([w_in.astype(BF16), sc_conv_w, ssm_conv_w]), "gather_in_ici")
            bufs = _run_comm(_gather_sibling(bufs), "gather_in_sibling")
            self.W = dict(zip([k for k, _ in in_cols], _win_unpack(by_owner(bufs[0]), in_cols, "win_unpack")))
            self.taps = dict(sc_conv_w=_slots_to_cols(by_owner(bufs[1])), ssm_conv_w=_slots_to_cols(by_owner(bufs[2])))
            self.staged, self.grads, self.summed, self.scatters = {}, {}, {}, []
            lands = [_own_shard(self.shards[k], "own_shard_" + k) for k in self.late]
            self.gather_flight = _ici_start([], lands, True, "gather_late_start")
            self.token = self.gather_flight[4][0, 0]

        def tok(self):
            return self.token

        def point(self, name, value):
            if name == "ssd_fwd_done":
                lands = _ici_wait(self.gather_flight, [value], True, "gather_late_wait")
                self.staged.update(zip(self.late, lands))

        def carry(self, name):
            if name in self.gather_sib:
                return _gather_sibling([self.staged.pop(k) for k in self.gather_sib[name]])
            if name in self.scatter_sib:
                return _scatter_sibling([self.grads[k] for k in self.scatter_sib[name]])
            return None

        def start_scatter(self, keys, halves):
            lands = [_own_part(h, "own_part_" + k) for k, h in zip(keys, halves)]
            flight = _ici_start(halves, lands, False, "scatter_%s_start" % keys[0])
            self.scatters.append((keys, flight))
            self.token = flight[4][0, 0]

        def carried(self, name, outs):
            if name in self.gather_sib:
                for k, b in zip(self.gather_sib[name], outs):
                    full = by_owner(b)
                    self.W[k] = _slots_to_cols(full) if k == "w1" else full.reshape(-1, D)
            else:
                keys = self.scatter_sib[name]
                self.start_scatter(keys, [_add_halves(self.grads[k], b, "add_halves_" + k) for k, b in zip(keys, outs)])

        def grad(self, k, g):
            if k == "win":
                g = to_owner(_win_pack([g[k] for k, _ in in_cols], in_cols, N_DEV, "win_pack"))
                got = _run_comm(_scatter_sibling([g]), "scatter_sibling_win")[0]
                self.start_scatter(("win",), [_add_halves(g, got, "add_halves_win")])
            else:
                self.grads[k] = cols_of(g) if k == "w1" else rows_of(g)

        def finish_scatter(self, after):
            keys, flight = self.scatters.pop(0)
            return dict(zip(keys, _ici_wait(flight, after, False, "scatter_%s_wait" % keys[0])))

    S = Schedule()
    small = dict(norm_mix=norm_mix, b_gate=b_gate, ssm_conv_b=ssm_conv_b, dt_bias=dt_bias, A_log=A_log, D_skip=D_skip,
                 ssm_norm_w=ssm_norm_w, norm_mlp=norm_mlp, norm_final=norm_final, **S.taps)
    grad_x, g_small = _local_step(x.reshape(T, D), loss_target.reshape(T, D), S, small)

    small_flat = jnp.concatenate([g_small[k].reshape(-1) for k in _SMALL_ORDER])
    n_small = small_flat.shape[0]
    rows = -(-n_small // (8 * LANES)) * 8
    small_pack = jnp.pad(small_flat, (0, rows * LANES - n_small)).reshape(rows, LANES)

    res = {}
    big = [("w_in", "win", w_in, m_w_in, v_w_in), ("w_branch_sc", "bsc", w_branch_sc, m_w_branch_sc, v_w_branch_sc),
           ("w_branch_ssm", "bssm", w_branch_ssm, m_w_branch_ssm, v_w_branch_ssm), ("w_out", "out", w_out, m_w_out, v_w_out),
           ("w_mlp1", "w1", w_mlp1, m_w_mlp1, v_w_mlp1), ("w_mlp2", "w2", w_mlp2, m_w_mlp2, v_w_mlp2)]
    by_grad = {gk: (k, w, m, v) for k, gk, w, m, v in big}
    after = [grad_x]
    while S.scatters:
        for gk, parts in S.finish_scatter(after).items():
            k, w, m, v = by_grad[gk]
            res[k] = _adam(w, m, v, parts, "adam_" + k)
            after = [res[k][1]]
    small_parts = _run_comm(_gather_all([small_pack]), "gather_small", after=[res[k][1]])[0]

    sizes = {k: g_small[k].size for k in _SMALL_ORDER}
    offs, o = {}, 0
    for k in _SMALL_ORDER:
        offs[k] = o
        o += sizes[k]
    rep_w = dict(norm_mix=norm_mix, b_gate=b_gate, ssm_conv_b=ssm_conv_b, dt_bias=dt_bias, A_log=A_log, D_skip=D_skip,
                 ssm_norm_w=ssm_norm_w, norm_mlp=norm_mlp, norm_final=norm_final)
    rep_m = dict(norm_mix=m_norm_mix, b_gate=m_b_gate, ssm_conv_b=m_ssm_conv_b, dt_bias=m_dt_bias, A_log=m_A_log, D_skip=m_D_skip,
                 ssm_norm_w=m_ssm_norm_w, norm_mlp=m_norm_mlp, norm_final=m_norm_final)
    rep_v = dict(norm_mix=v_norm_mix, b_gate=v_b_gate, ssm_conv_b=v_ssm_conv_b, dt_bias=v_dt_bias, A_log=v_A_log, D_skip=v_D_skip,
                 ssm_norm_w=v_ssm_norm_w, norm_mlp=v_norm_mlp, norm_final=v_norm_final)

    def pack(d):
        segs = [jnp.pad(d[k].astype(F32).reshape(-1), (0, sizes[k] - d[k].size)) if k in d else jnp.zeros((sizes[k],), F32)
                for k in _SMALL_ORDER]
        return jnp.pad(jnp.concatenate(segs), (0, rows * LANES - n_small)).reshape(rows, LANES)

    sm = _adam(pack(rep_w), pack(rep_m), pack(rep_v), small_parts, "adam_small")
    sm = [s.reshape(-1) for s in sm]
    for k in _REPLICATED:
        n_k = rep_w[k].shape[0]
        res[k] = tuple(s[offs[k]:offs[k] + n_k] for s in sm)
    loss = sm[0][offs["loss"]]
    for k, w, m, v, K, full in (("sc_conv_w", sc_conv_w, m_sc_conv_w, v_sc_conv_w, SC_K, D),
                                ("ssm_conv_w", ssm_conv_w, m_ssm_conv_w, v_ssm_conv_w, SSM_K, n_xbc)):
        g_full = sm[0][offs[k]:offs[k] + K * full].reshape(K, full)
        cw = full // N_DEV
        g_mine = lax.dynamic_slice_in_dim(g_full, me * cw, cw, axis=1)
        res[k] = _adam(w, m, v, g_mine[None], "adam_" + k)

    order = ("norm_mix", "w_in", "b_gate", "sc_conv_w", "ssm_conv_w", "ssm_conv_b", "dt_bias", "A_log", "D_skip", "ssm_norm_w",
             "w_branch_sc", "w_branch_ssm", "w_out", "norm_mlp", "w_mlp1", "w_mlp2", "norm_final")
    outs = [loss, grad_x.reshape(1, T, D)]
    for j in range(4):
        outs += [res[k][j] for k in order]
    return tuple(outs)
```

```python
import functools

import jax
import jax.numpy as jnp
from jax import lax
from jax.experimental import pallas as pl
from jax.experimental.pallas import tpu as pltpu

F32 = jnp.float32
BF16 = jnp.bfloat16

EPS = 1e-6
N_DEV = 8
HEADDIM = 64
NSTATE = 128
CHUNK = 128
NGROUPS = 8
GROUP_W = 256
SC_K = 3
SSM_K = 4
LANES = 128

ADAM_LR = 0.001
ADAM_B1 = 0.9
ADAM_B2 = 0.999
ADAM_EPS = 1e-08
ADAM_WD = 0.01
ADAM_STEP = 10

NN = (((1,), (0,)), ((), ()))
NT = (((1,), (1,)), ((), ()))
TN = (((0,), (0,)), ((), ()))
_DIMS = {"nn": NN, "nt": NT, "tn": TN}

ANY = pl.BlockSpec(memory_space=pl.ANY)
MESH = pl.DeviceIdType.MESH


def _sds(shape, dtype):
    return jax.ShapeDtypeStruct(tuple(shape), dtype)


def _dot(a, b, dims=NN):
    return lax.dot_general(a, b, dims, preferred_element_type=F32)


def _dot3(a, b, dims=NN):
    return lax.dot_general(a, b, dims, preferred_element_type=F32, precision=lax.Precision.HIGH)


def _params(*sem):
    return pltpu.CompilerParams(dimension_semantics=tuple(sem))


def _call(body, *, grid, in_specs, out_specs, out_shape, args, name, sem, scratch=(), comm=None):
    if comm is None:
        outs = pl.pallas_call(body, grid=grid, in_specs=list(in_specs), out_specs=list(out_specs), out_shape=list(out_shape),
                              scratch_shapes=list(scratch), name=name, compiler_params=_params(*sem))(*args)
        return list(outs), None
    n, n_in, n_out, n_scr = comm.n, len(in_specs), len(out_shape), len(scratch)

    def wrapped(*refs):
        ins, c_in = refs[:n_in], refs[n_in:n_in + n]
        outs, c_out = refs[n_in + n:n_in + n + n_out], refs[n_in + n + n_out:n_in + 2 * n + n_out]
        rest = refs[n_in + 2 * n + n_out:]
        scr, sems = rest[:n_scr], rest[n_scr:]
        first, last = None, None
        for d, g in enumerate(grid):
            f, l = pl.program_id(d) == 0, pl.program_id(d) == g - 1
            first, last = (f, l) if first is None else (first & f, last & l)

        @pl.when(first)
        def _():
            comm.start(c_in, c_out, sems)

        body(*ins, *outs, *scr)

        @pl.when(last)
        def _():
            comm.finish(c_in, c_out, sems)

    outs = pl.pallas_call(
        wrapped, grid=grid, in_specs=list(in_specs) + [ANY] * n, out_specs=list(out_specs) + [ANY] * n,
        out_shape=list(out_shape) + comm.out_shape, scratch_shapes=list(scratch) + comm.scratch,
        input_output_aliases={n_in + i: n_out + o for i, o in comm.aliases.items()},
        name=name, compiler_params=_params(*["arbitrary"] * len(grid)))(*args, *comm.arrs)
    return list(outs[:n_out]), list(outs[n_out:])


MM_VMEM_BUDGET = 44 * 2 ** 20


def _mm_tiles(M, N, k_bytes, mn_bytes):
    best = None
    for tm in (2048, 1024, 512, 256, 128):
        for tn in (1024, 512, 256, 128):
            if M % tm or N % tn:
                continue
            need = 2 * ((tm + tn) * k_bytes + tm * tn * mn_bytes) + 4 * tm * tn * 4
            if need <= MM_VMEM_BUDGET and (best is None or (tm * tn, tm) > (best[0] * best[1], best[0])):
                best = (tm, tn)
    assert best is not None, (M, N, k_bytes, mn_bytes)
    return best


def _mm(a, b, *, mode, name, extras=(), epi=None, out_dtypes=(F32,), comm=None):
    a_list = list(a) if isinstance(a, (list, tuple)) else [a]
    b_list = list(b) if isinstance(b, (list, tuple)) else [b]
    if mode == "nn":
        M, N = a_list[0].shape[0], b_list[0].shape[1]
    elif mode == "nt":
        M, N = a_list[0].shape[0], b_list[0].shape[0]
    else:
        M, N = a_list[0].shape[1], b_list[0].shape[1]
    k_bytes = sum((av.shape[0] if mode == "tn" else av.shape[1]) * av.dtype.itemsize for av in a_list)
    mn_bytes = sum(e.dtype.itemsize for e in extras) + sum(jnp.dtype(d).itemsize for d in out_dtypes)
    tm, tn = _mm_tiles(min(M, 2048), min(N, 1024), k_bytes, mn_bytes) if M % 128 == 0 and N % 128 == 0 else (M, N)
    assert M % tm == 0 and N % tn == 0
    a_specs, b_specs = [], []
    for av, bv in zip(a_list, b_list):
        K = av.shape[0] if mode == "tn" else av.shape[1]
        a_specs.append(pl.BlockSpec((K, tm), lambda i, j: (0, i)) if mode == "tn" else pl.BlockSpec((tm, K), lambda i, j: (i, 0)))
        b_specs.append(pl.BlockSpec((tn, K), lambda i, j: (j, 0)) if mode == "nt" else pl.BlockSpec((K, tn), lambda i, j: (0, j)))
    mn_spec = pl.BlockSpec((tm, tn), lambda i, j: (i, j))
    n_p, n_ex = len(a_list), len(extras)
    dims = _DIMS[mode]

    def body(*refs):
        acc = _dot(refs[0][...], refs[n_p][...], dims)
        for p in range(1, n_p):
            acc = acc + _dot(refs[p][...], refs[n_p + p][...], dims)
        rest = refs[2 * n_p:]
        res = (acc,) if epi is None else epi(acc, *[r[...] for r in rest[:n_ex]])
        for o_ref, r in zip(rest[n_ex:], res):
            o_ref[...] = r.astype(o_ref.dtype)

    outs, carried = _call(
        body, grid=(M // tm, N // tn), in_specs=a_specs + b_specs + [mn_spec] * n_ex,
        out_specs=[mn_spec] * len(out_dtypes), out_shape=[_sds((M, N), d) for d in out_dtypes],
        args=a_list + b_list + list(extras), name=name, sem=("parallel", "parallel"), comm=comm)
    res = outs[0] if len(outs) == 1 else outs
    return res if comm is None else (res, carried)


def _epi_add(acc, r):
    return (acc + r,)


def _epi_add2(acc, r):
    s = acc + r
    return (s, s)


def _epi_relu2(acc):
    p = jnp.maximum(acc, 0.0)
    return (p * p,)


def _epi_relu2_bwd(acc, r):
    return (acc * (2.0 * jnp.sqrt(r.astype(F32))),)


def _row(tr, n):
    return pl.BlockSpec((tr, n), lambda i: (i, 0))


def _vec(n):
    return pl.BlockSpec((1, n), lambda i: (0, 0))


def _rms_fwd(x, w, name, comm=None):
    T, D = x.shape
    tr = min(256, T)

    def body(x_ref, w_ref, o_ref):
        xv = x_ref[...]
        r = lax.rsqrt(jnp.mean(xv * xv, axis=-1, keepdims=True) + EPS)
        o_ref[...] = (xv * r * w_ref[...]).astype(BF16)

    outs, carried = _call(body, grid=(T // tr,), in_specs=[_row(tr, D), _vec(D)], out_specs=[_row(tr, D)],
                          out_shape=[_sds((T, D), BF16)], args=[x, w], name=name, sem=("parallel",), comm=comm)
    return outs[0] if comm is None else (outs[0], carried)


def _rms_bwd(x, w, dh, dres, name):
    T, D = x.shape
    tr = min(256, T)

    def body(x_ref, w_ref, dh_ref, dres_ref, dx_ref, dxb_ref, dw_ref):
        @pl.when(pl.program_id(0) == 0)
        def _():
            dw_ref[...] = jnp.zeros_like(dw_ref)

        xv = x_ref[...]
        r = lax.rsqrt(jnp.mean(xv * xv, axis=-1, keepdims=True) + EPS)
        xh = xv * r
        dh_v = dh_ref[...]
        dw_ref[...] += jnp.sum(dh_v * xh, axis=0, keepdims=True)
        dxh = dh_v * w_ref[...]
        dx = r * (dxh - xh * jnp.mean(dxh * xh, axis=-1, keepdims=True)) + dres_ref[...]
        dx_ref[...] = dx
        dxb_ref[...] = dx.astype(BF16)

    return pl.pallas_call(
        body, grid=(T // tr,), in_specs=[_row(tr, D), _vec(D), _row(tr, D), _row(tr, D)],
        out_specs=[_row(tr, D), _row(tr, D), _vec(D)],
        out_shape=[_sds((T, D), F32), _sds((T, D), BF16), _sds((1, D), F32)],
        name=name, compiler_params=_params("arbitrary"))(x, w, dh, dres)


def _final(x2, w, tgt, name):
    T, D = x2.shape
    tr = min(256, T)

    def body(x_ref, w_ref, t_ref, dx_ref, dxb_ref, dw_ref, loss_ref):
        @pl.when(pl.program_id(0) == 0)
        def _():
            dw_ref[...] = jnp.zeros_like(dw_ref)
            loss_ref[...] = jnp.zeros_like(loss_ref)

        xv = x_ref[...]
        wv = w_ref[...]
        r = lax.rsqrt(jnp.mean(xv * xv, axis=-1, keepdims=True) + EPS)
        xh = xv * r
        err = xh * wv - t_ref[...]
        part = jnp.sum(jnp.sum(err * err, axis=1, keepdims=True), axis=0, keepdims=True) * (0.5 / D)
        loss_ref[...] += jnp.broadcast_to(part, loss_ref.shape)
        dy = err * (1.0 / D)
        dw_ref[...] += jnp.sum(dy * xh, axis=0, keepdims=True)
        dxh = dy * wv
        dx = r * (dxh - xh * jnp.mean(dxh * xh, axis=-1, keepdims=True))
        dx_ref[...] = dx
        dxb_ref[...] = dx.astype(BF16)

    return pl.pallas_call(
        body, grid=(T // tr,), in_specs=[_row(tr, D), _vec(D), _row(tr, D)],
        out_specs=[_row(tr, D), _row(tr, D), _vec(D), _vec(LANES)],
        out_shape=[_sds((T, D), F32), _sds((T, D), BF16), _sds((1, D), F32), _sds((1, LANES), F32)],
        name=name, compiler_params=_params("arbitrary"))(x2, w, tgt)


def _silu_parts(z):
    s = jax.nn.sigmoid(z)
    return z * s, s * (1.0 + z * (1.0 - s))


def _gnorm_fwd(y, z, w, name, comm=None):
    T, N = y.shape
    tr = min(256, T)

    def body(y_ref, z_ref, w_ref, o_ref):
        for g in range(N // GROUP_W):
            sl = slice(g * GROUP_W, (g + 1) * GROUP_W)
            silu, _ = _silu_parts(z_ref[:, sl])
            yz = y_ref[:, sl] * silu
            r = lax.rsqrt(jnp.mean(yz * yz, axis=-1, keepdims=True) + EPS)
            o_ref[:, sl] = (yz * r * w_ref[:, sl]).astype(BF16)

    outs, carried = _call(body, grid=(T // tr,), in_specs=[_row(tr, N), _row(tr, N), _vec(N)], out_specs=[_row(tr, N)],
                          out_shape=[_sds((T, N), BF16)], args=[y, z, w], name=name, sem=("parallel",), comm=comm)
    return outs[0] if comm is None else (outs[0], carried)


def _gnorm_bwd(y, z, w, dyb, name):
    T, N = y.shape
    tr = min(256, T)

    def body(y_ref, z_ref, w_ref, d_ref, dy_ref, dz_ref, dw_ref):
        @pl.when(pl.program_id(0) == 0)
        def _():
            dw_ref[...] = jnp.zeros_like(dw_ref)

        for g in range(N // GROUP_W):
            sl = slice(g * GROUP_W, (g + 1) * GROUP_W)
            yv = y_ref[:, sl]
            silu, dsilu = _silu_parts(z_ref[:, sl])
            yz = yv * silu
            r = lax.rsqrt(jnp.mean(yz * yz, axis=-1, keepdims=True) + EPS)
            yzh = yz * r
            d = d_ref[:, sl]
            dw_ref[:, sl] += jnp.sum(d * yzh, axis=0, keepdims=True)
            dyzh = d * w_ref[:, sl]
            dyz = r * (dyzh - yzh * jnp.mean(dyzh * yzh, axis=-1, keepdims=True))
            dy_ref[:, sl] = dyz * silu
            dz_ref[:, sl] = (dyz * yv * dsilu).astype(BF16)

    return pl.pallas_call(
        body, grid=(T // tr,), in_specs=[_row(tr, N), _row(tr, N), _vec(N), _row(tr, N)],
        out_specs=[_row(tr, N), _row(tr, N), _vec(N)],
        out_shape=[_sds((T, N), F32), _sds((T, N), BF16), _sds((1, N), F32)],
        name=name, compiler_params=_params("arbitrary"))(y, z, w, dyb)


def _merge_fwd(gate_raw, b_gate, br_a, br_b, name):
    T, D = br_a.shape
    tr = min(256, T)

    def body(g_ref, bg_ref, a_ref, b_ref, o_ref):
        g = jax.nn.sigmoid(g_ref[...] + bg_ref[...])
        o_ref[...] = (g[:, :D] * a_ref[...] + g[:, D:] * b_ref[...]).astype(BF16)

    return pl.pallas_call(body, grid=(T // tr,), in_specs=[_row(tr, 2 * D), _vec(2 * D), _row(tr, D), _row(tr, D)],
                          out_specs=_row(tr, D), out_shape=_sds((T, D), BF16), name=name,
                          compiler_params=_params("parallel"))(gate_raw, b_gate, br_a, br_b)


def _merge_bwd(dmerged, gate_raw, b_gate, br_a, br_b, name):
    T, D = br_a.shape
    tr = min(256, T)

    def body(d_ref, g_ref, bg_ref, a_ref, b_ref, da_ref, db_ref, dg_ref, dbg_ref):
        @pl.when(pl.program_id(0) == 0)
        def _():
            dbg_ref[...] = jnp.zeros_like(dbg_ref)

        g = jax.nn.sigmoid(g_ref[...] + bg_ref[...])
        d = d_ref[...]
        da_ref[...] = (d * g[:, :D]).astype(BF16)
        db_ref[...] = (d * g[:, D:]).astype(BF16)
        dg = jnp.concatenate([d * a_ref[...], d * b_ref[...]], axis=1) * g * (1.0 - g)
        dg_ref[...] = dg.astype(BF16)
        dbg_ref[...] += jnp.sum(dg, axis=0, keepdims=True)

    return pl.pallas_call(
        body, grid=(T // tr,), in_specs=[_row(tr, D), _row(tr, 2 * D), _vec(2 * D), _row(tr, D), _row(tr, D)],
        out_specs=[_row(tr, D), _row(tr, D), _row(tr, 2 * D), _vec(2 * D)],
        out_shape=[_sds((T, D), BF16), _sds((T, D), BF16), _sds((T, 2 * D), BF16), _sds((1, 2 * D), F32)],
        name=name, compiler_params=_params("arbitrary"))(dmerged, gate_raw, b_gate, br_a, br_b)


CB_W = 256
CONV_ROWS = 32
CONV_PAD = 8


def _rows_down(load, r0, s):
    if s == 0:
        return load(r0, r0 + CONV_ROWS)
    if r0 == 0:
        row = lax.broadcasted_iota(jnp.int32, (CONV_ROWS, CB_W), 0)
        return jnp.where(row >= s, pltpu.roll(load(0, CONV_ROWS), s, 0), 0.0)
    return load(r0 - s, r0 - s + CONV_ROWS)


def _conv_tile(load, taps, r0):
    K = len(taps)
    us = [_rows_down(load, r0, K - 1 - k) for k in range(K)]
    acc = us[K - 1] * taps[K - 1]
    for k in range(K - 1):
        acc = acc + us[k] * taps[k]
    return acc, us


def _conv_back_tile(scr, taps, r0):
    K = len(taps)
    du = scr[r0:r0 + CONV_ROWS, :] * taps[K - 1]
    for k in range(K - 1):
        s = K - 1 - k
        du = du + scr[r0 + s:r0 + s + CONV_ROWS, :] * taps[k]
    return du


def _fold8(v):
    return jnp.sum(v.reshape(CONV_ROWS // 8, 8, v.shape[1]), axis=0)


def _col(T, j0=0):
    return pl.BlockSpec((T, CB_W), lambda j: (0, j + j0))


def _sc_fwd(psc, w, name):
    T, D = psc.shape[0], psc.shape[1] // 3
    nb = D // CB_W

    def body(b_ref, c_ref, x_ref, w_ref, o_ref):
        taps = [w_ref[k:k + 1, :] for k in range(SC_K)]
        load = lambda a, b: c_ref[a:b, :] * x_ref[a:b, :]
        for r0 in range(0, T, CONV_ROWS):
            cu, _ = _conv_tile(load, taps, r0)
            o_ref[r0:r0 + CONV_ROWS, :] = (b_ref[r0:r0 + CONV_ROWS, :] * cu).astype(BF16)

    return pl.pallas_call(
        body, grid=(nb,), in_specs=[_col(T), _col(T, nb), _col(T, 2 * nb), pl.BlockSpec((SC_K, CB_W), lambda j: (0, j))],
        out_specs=_col(T), out_shape=_sds((T, D), BF16), name=name, compiler_params=_params("parallel"))(psc, psc, psc, w)


def _sc_bwd(psc, w, dya, name):
    T, D = psc.shape[0], psc.shape[1] // 3
    nb = D // CB_W

    def body(b_ref, c_ref, x_ref, w_ref, d_ref, db_ref, dc_ref, dx_ref, dw_ref, scr):
        taps = [w_ref[k:k + 1, :] for k in range(SC_K)]
        load = lambda a, b: c_ref[a:b, :] * x_ref[a:b, :]
        scr[T:T + CONV_PAD, :] = jnp.zeros((CONV_PAD, CB_W), F32)
        dw8 = [jnp.zeros((8, CB_W), F32)] * SC_K
        for r0 in range(0, T, CONV_ROWS):
            rows = slice(r0, r0 + CONV_ROWS)
            cu, us = _conv_tile(load, taps, r0)
            d = d_ref[rows, :]
            db_ref[rows, :] = (d * cu).astype(BF16)
            dcu = d * b_ref[rows, :]
            scr[rows, :] = dcu
            dw8 = [acc + _fold8(dcu * u) for acc, u in zip(dw8, us)]
        for k in range(SC_K):
            dw_ref[k:k + 1, :] = jnp.sum(dw8[k], axis=0, keepdims=True)
        for r0 in range(0, T, CONV_ROWS):
            rows = slice(r0, r0 + CONV_ROWS)
            du = _conv_back_tile(scr, taps, r0)
            dc_ref[rows, :] = (du * x_ref[rows, :]).astype(BF16)
            dx_ref[rows, :] = (du * c_ref[rows, :]).astype(BF16)

    wspec = pl.BlockSpec((SC_K, CB_W), lambda j: (0, j))
    return pl.pallas_call(
        body, grid=(nb,), in_specs=[_col(T), _col(T, nb), _col(T, 2 * nb), wspec, _col(T)],
        out_specs=[_col(T), _col(T), _col(T), wspec],
        out_shape=[_sds((T, D), BF16)] * 3 + [_sds((SC_K, D), F32)],
        scratch_shapes=[pltpu.VMEM((T + CONV_PAD, CB_W), F32)],
        name=name, compiler_params=_params("parallel"))(psc, psc, psc, w, dya)


def _ssm_conv_fwd(u, w, b, name, comm=None):
    T, N = u.shape

    def body(u_ref, w_ref, b_ref, o_ref):
        taps = [w_ref[k:k + 1, :] for k in range(SSM_K)]
        bias = b_ref[...]
        for r0 in range(0, T, CONV_ROWS):
            c, _ = _conv_tile(lambda a, b: u_ref[a:b, :], taps, r0)
            c = c + bias
            o_ref[r0:r0 + CONV_ROWS, :] = c * jax.nn.sigmoid(c)

    outs, carried = _call(
        body, grid=(N // CB_W,), in_specs=[_col(T), pl.BlockSpec((SSM_K, CB_W), lambda j: (0, j)), pl.BlockSpec((1, CB_W), lambda j: (0, j))],
        out_specs=[_col(T)], out_shape=[_sds((T, N), F32)], args=[u, w, b], name=name, sem=("parallel",), comm=comm)
    return outs[0] if comm is None else (outs[0], carried)


def _ssm_conv_bwd(u, w, b, dxs, dB, dC, name, comm=None):
    T, N = u.shape
    n_x, n_b = dxs.shape[1] // CB_W, dB.shape[1] // CB_W

    def body(u_ref, w_ref, b_ref, dx_ref, db_ref, dc_ref, du_ref, dw_ref, dbias_ref, scr):
        j = pl.program_id(0)
        taps = [w_ref[k:k + 1, :] for k in range(SSM_K)]
        bias = b_ref[...]
        scr[T:T + CONV_PAD, :] = jnp.zeros((CONV_PAD, CB_W), F32)
        dw8 = [jnp.zeros((8, CB_W), F32)] * SSM_K
        db8 = jnp.zeros((8, CB_W), F32)
        for r0 in range(0, T, CONV_ROWS):
            rows = slice(r0, r0 + CONV_ROWS)
            c, us = _conv_tile(lambda a, b: u_ref[a:b, :], taps, r0)
            _, dsilu = _silu_parts(c + bias)
            d = jnp.where(j < n_x, dx_ref[rows, :], jnp.where(j < n_x + n_b, db_ref[rows, :], dc_ref[rows, :])) * dsilu
            scr[rows, :] = d
            db8 = db8 + _fold8(d)
            dw8 = [acc + _fold8(d * u) for acc, u in zip(dw8, us)]
        dbias_ref[...] = jnp.sum(db8, axis=0, keepdims=True)
        for k in range(SSM_K):
            dw_ref[k:k + 1, :] = jnp.sum(dw8[k], axis=0, keepdims=True)
        for r0 in range(0, T, CONV_ROWS):
            du_ref[r0:r0 + CONV_ROWS, :] = _conv_back_tile(scr, taps, r0).astype(BF16)

    wspec = pl.BlockSpec((SSM_K, CB_W), lambda j: (0, j))
    bspec = pl.BlockSpec((1, CB_W), lambda j: (0, j))
    outs, carried = _call(
        body, grid=(N // CB_W,),
        in_specs=[_col(T), wspec, bspec,
                  pl.BlockSpec((T, CB_W), lambda j: (0, jnp.minimum(j, n_x - 1))),
                  pl.BlockSpec((T, CB_W), lambda j: (0, jnp.clip(j - n_x, 0, n_b - 1))),
                  pl.BlockSpec((T, CB_W), lambda j: (0, jnp.clip(j - n_x - n_b, 0, n_b - 1)))],
        out_specs=[_col(T), wspec, bspec],
        out_shape=[_sds((T, N), BF16), _sds((SSM_K, N), F32), _sds((1, N), F32)],
        scratch=[pltpu.VMEM((T + CONV_PAD, CB_W), F32)],
        args=[u, w, b, dxs, dB, dC], name=name, sem=("parallel",), comm=comm)
    return outs if comm is None else (outs, carried)


def _split3(v):
    hi = v.astype(BF16)
    r = v - hi.astype(F32)
    mid = r.astype(BF16)
    lo = (r - mid.astype(F32)).astype(BF16)
    return hi, mid, lo


def _head_expand(n_lanes):
    h = lax.broadcasted_iota(jnp.int32, (LANES, n_lanes), 0)
    l = lax.broadcasted_iota(jnp.int32, (LANES, n_lanes), 1)
    return (jnp.right_shift(l, HEADDIM.bit_length() - 1) == h).astype(BF16)


def _softplus(v):
    return jnp.maximum(v, 0.0) + jnp.log1p(jnp.exp(-jnp.abs(v)))


def _ssd_prep(dt_raw, dt_bias, a_log, n_inner, name):
    T = dt_raw.shape[0]

    def body(r_ref, b_ref, al_ref, dt_ref, cs_ref):
        dt = _softplus(r_ref[...] + b_ref[...])
        a = dt * (-jnp.exp(al_ref[...]))
        i = lax.broadcasted_iota(jnp.int32, (CHUNK, CHUNK), 0)
        j = lax.broadcasted_iota(jnp.int32, (CHUNK, CHUNK), 1)
        tri = (j <= i).astype(BF16)
        cs = sum(_dot(tri, p) for p in _split3(a))
        ex = _head_expand(n_inner)
        dt_ref[...] = sum(_dot(p, ex) for p in _split3(dt))
        cs_ref[...] = sum(_dot(p, ex) for p in _split3(cs))

    blk = pl.BlockSpec((CHUNK, LANES), lambda c: (c, 0))
    out = pl.BlockSpec((CHUNK, n_inner), lambda c: (c, 0))
    return pl.pallas_call(body, grid=(T // CHUNK,), in_specs=[blk, _vec(LANES), _vec(LANES)], out_specs=[out, out],
                          out_shape=[_sds((T, n_inner), F32)] * 2, name=name, compiler_params=_params("parallel"))(dt_raw, dt_bias, a_log)


def _pair_terms(cs_p):
    lane = lax.broadcasted_iota(jnp.int32, (CHUNK, CHUNK), 1)
    sub = lax.broadcasted_iota(jnp.int32, (CHUNK, CHUNK), 0)
    csT = cs_p.T
    Ls = []
    for k in range(2):
        col = jnp.sum(jnp.where(lane == k * HEADDIM, cs_p, 0.0), axis=1, keepdims=True)
        rowv = csT[k * HEADDIM:k * HEADDIM + 1, :]
        Ls.append(jnp.exp(jnp.where(sub >= lane, col - rowv, -jnp.inf)))
    return Ls, jnp.exp(csT[:, CHUNK - 1:CHUNK])


def _block_diag(xp):
    lane = lax.broadcasted_iota(jnp.int32, xp.shape, 1)
    return jnp.concatenate([jnp.where(lane < HEADDIM, xp, 0.0), jnp.where(lane >= HEADDIM, xp, 0.0)], axis=0)


SSD_GROUPS_PER_STEP = 8


def _ssd_specs(T, n_inner):
    nc, gs = T // CHUNK, SSD_GROUPS_PER_STEP
    bo, co = n_inner // (gs * NSTATE), (n_inner + NGROUPS * NSTATE) // (gs * NSTATE)
    assert NGROUPS % gs == 0 and n_inner % (gs * NSTATE) == 0 and (NGROUPS * NSTATE) % (gs * NSTATE) == 0
    g_blk = lambda f: pl.BlockSpec((CHUNK, gs * GROUP_W), lambda c, s: (f(c), s))
    b_blk = lambda f: pl.BlockSpec((CHUNK, gs * NSTATE), lambda c, s: (f(c), bo + s))
    c_blk = lambda f: pl.BlockSpec((CHUNK, gs * NSTATE), lambda c, s: (f(c), co + s))
    return nc, g_blk, b_blk, c_blk


def _ssd_fwd(xbc, dt_e, cs_e, d_e, name, comm=None):
    T = xbc.shape[0]
    n_inner = dt_e.shape[1]
    nc, g_blk, b_blk, c_blk = _ssd_specs(T, n_inner)
    ident = lambda c: c

    gs = SSD_GROUPS_PER_STEP

    def body(xs_ref, b_ref, c_ref, dt_ref, cs_ref, d_ref, y_ref, p_ref, st):
        c, s = pl.program_id(0), pl.program_id(1)

        @pl.when(c == 0)
        def _():
            for gi in range(gs):
                st[s * gs + gi] = jnp.zeros((GROUP_W, NSTATE), F32)

        for gi in range(gs):
            g = s * gs + gi
            gw, gn = slice(gi * GROUP_W, (gi + 1) * GROUP_W), slice(gi * NSTATE, (gi + 1) * NSTATE)
            P = st[g]
            p_ref[0, gi] = P
            xs, dt, cs = xs_ref[:, gw], dt_ref[:, gw], cs_ref[:, gw]
            Bf, Cf = b_ref[:, gn], c_ref[:, gn]
            CBm = _dot3(Cf, Bf, NT)
            X = xs * dt
            decay = jnp.exp(cs[CHUNK - 1:CHUNK, :] - cs)
            y_off = _dot3(Cf, P, NT) * jnp.exp(cs)
            ys, ecl = [], []
            for pr in range(2):
                sl = slice(pr * LANES, (pr + 1) * LANES)
                Ls, e_last = _pair_terms(cs[:, sl])
                ecl.append(e_last)
                Mcat = jnp.concatenate([CBm * L for L in Ls], axis=1)
                ys.append(_dot3(Mcat, _block_diag(X[:, sl])))
            y_ref[:, gw] = jnp.concatenate(ys, axis=1) + y_off + xs * d_ref[:, gw]
            S = _dot3(X * decay, Bf, TN)
            st[g] = P * jnp.concatenate(ecl, axis=0) + S

    p_blk = pl.BlockSpec((1, gs, GROUP_W, NSTATE), lambda c, s: (c, s, 0, 0))
    outs, carried = _call(
        body, grid=(nc, NGROUPS // gs),
        in_specs=[g_blk(ident), b_blk(ident), c_blk(ident), g_blk(ident), g_blk(ident), pl.BlockSpec((1, gs * GROUP_W), lambda c, s: (0, s))],
        out_specs=[g_blk(ident), p_blk],
        out_shape=[_sds((T, n_inner), F32), _sds((nc, NGROUPS, GROUP_W, NSTATE), F32)],
        scratch=[pltpu.VMEM((NGROUPS, GROUP_W, NSTATE), F32)],
        args=[xbc, xbc, xbc, dt_e, cs_e, d_e], name=name, sem=("arbitrary", "arbitrary"), comm=comm)
    return outs if comm is None else (outs, carried)


def _ssd_bwd(xbc, dt_e, cs_e, d_e, states, dy, name, comm=None):
    T = xbc.shape[0]
    n_inner = dt_e.shape[1]
    nc, g_blk, b_blk, c_blk = _ssd_specs(T, n_inner)
    rev = lambda c: nc - 1 - c

    gs = SSD_GROUPS_PER_STEP

    def body(xs_ref, b_ref, c_ref, dt_ref, cs_ref, d_ref, p_ref, pn_ref, dy_ref,
             dxs_ref, db_ref, dc_ref, ddt_ref, dcs_ref, dd_ref, dst):
        cc, s = pl.program_id(0), pl.program_id(1)

        @pl.when(cc == 0)
        def _():
            for gi in range(gs):
                dst[s * gs + gi] = jnp.zeros((GROUP_W, NSTATE), F32)

        for gi in range(gs):
            one_group(s * gs + gi, gi, xs_ref, b_ref, c_ref, dt_ref, cs_ref, d_ref, p_ref, pn_ref, dy_ref,
                      dxs_ref, db_ref, dc_ref, ddt_ref, dcs_ref, dd_ref, dst)

    def one_group(g, gi, xs_ref, b_ref, c_ref, dt_ref, cs_ref, d_ref, p_ref, pn_ref, dy_ref,
                  dxs_ref, db_ref, dc_ref, ddt_ref, dcs_ref, dd_ref, dst):
        gw, gn = slice(gi * GROUP_W, (gi + 1) * GROUP_W), slice(gi * NSTATE, (gi + 1) * NSTATE)
        dS = dst[g]
        P, Pn = p_ref[0, gi], pn_ref[0, gi]
        xs, dt, cs, dY = xs_ref[:, gw], dt_ref[:, gw], cs_ref[:, gw], dy_ref[:, gw]
        Bf, Cf = b_ref[:, gn], c_ref[:, gn]
        Bb, Cb = Bf.astype(BF16), Cf.astype(BF16)
        X = xs * dt
        ecs = jnp.exp(cs)
        decay = jnp.exp(cs[CHUNK - 1:CHUNK, :] - cs)
        CBm = _dot3(Cf, Bf, NT)
        dYe = dY * ecs
        dP_off = _dot3(dYe, Cf, TN)
        dC = _dot(dYe.astype(BF16), P.astype(BF16))
        dcs = dYe * _dot3(Cf, P, NT)
        Xd = X * decay
        dB = _dot(Xd.astype(BF16), dS.astype(BF16))
        E = _dot3(Bf, dS, NT)
        dX = E * decay
        dcs = dcs - E * Xd
        R = _dot3(jnp.ones((8, NSTATE), F32), dS * Pn, NT)
        sub_g = lax.broadcasted_iota(jnp.int32, (CHUNK, GROUP_W), 0)
        dcs = dcs + jnp.where(sub_g == CHUNK - 1, R[0:1, :], 0.0)
        lane = lax.broadcasted_iota(jnp.int32, (CHUNK, CHUNK), 1)
        sub = lax.broadcasted_iota(jnp.int32, (CHUNK, CHUNK), 0)
        dCB = jnp.zeros((CHUNK, CHUNK), F32)
        dXs, dcss, ecl = [], [], []
        for pr in range(2):
            sl = slice(pr * LANES, (pr + 1) * LANES)
            Ls, e_last = _pair_terms(cs[:, sl])
            ecl.append(e_last)
            dYp = dY[:, sl]
            dMcat = _dot3(dYp, _block_diag(X[:, sl]), NT)
            Mcat = jnp.concatenate([CBm * L for L in Ls], axis=1)
            dXt = _dot3(Mcat, dYp, TN)
            dXs.append(jnp.where(lane < HEADDIM, dXt[:CHUNK], dXt[CHUNK:]))
            colacc = jnp.zeros((CHUNK, CHUNK), F32)
            rowacc = jnp.zeros((CHUNK, CHUNK), F32)
            for k in range(2):
                dG = dMcat[:, k * CHUNK:(k + 1) * CHUNK] * Ls[k]
                dCB = dCB + dG
                Q = dG * CBm
                colacc = colacc + jnp.where(lane == k * HEADDIM, jnp.sum(Q, axis=1, keepdims=True), 0.0)
                rowacc = rowacc + jnp.where(sub == k * HEADDIM, jnp.sum(Q, axis=0, keepdims=True), 0.0)
            dcss.append(colacc - rowacc.T)
        dX = dX + jnp.concatenate(dXs, axis=1)
        dcs = dcs + jnp.concatenate(dcss, axis=1)
        dCBb = dCB.astype(BF16)
        dc_ref[:, gn] = dC + _dot(dCBb, Bb)
        db_ref[:, gn] = dB + _dot(dCBb, Cb, TN)
        dxs_ref[:, gw] = dX * dt + dY * d_ref[:, gw]
        ddt_ref[:, gw] = dX * xs
        dcs_ref[:, gw] = dcs
        dd_ref[0, :, gw] = jnp.sum(dY * xs, axis=0, keepdims=True)
        dst[g] = dS * jnp.concatenate(ecl, axis=0) + dP_off

    p_blk = pl.BlockSpec((1, gs, GROUP_W, NSTATE), lambda c, s: (nc - 1 - c, s, 0, 0))
    pn_blk = pl.BlockSpec((1, gs, GROUP_W, NSTATE), lambda c, s: (jnp.minimum(nc - c, nc - 1), s, 0, 0))
    st_blk = pl.BlockSpec((CHUNK, gs * NSTATE), lambda c, s: (nc - 1 - c, s))
    outs, carried = _call(
        body, grid=(nc, NGROUPS // gs),
        in_specs=[g_blk(rev), b_blk(rev), c_blk(rev), g_blk(rev), g_blk(rev), pl.BlockSpec((1, gs * GROUP_W), lambda c, s: (0, s)),
                  p_blk, pn_blk, g_blk(rev)],
        out_specs=[g_blk(rev), st_blk, st_blk, g_blk(rev), g_blk(rev), pl.BlockSpec((1, 1, gs * GROUP_W), lambda c, s: (nc - 1 - c, 0, s))],
        out_shape=[_sds((T, n_inner), F32), _sds((T, NGROUPS * NSTATE), F32), _sds((T, NGROUPS * NSTATE), F32),
                   _sds((T, n_inner), F32), _sds((T, n_inner), F32), _sds((nc, 1, n_inner), F32)],
        scratch=[pltpu.VMEM((NGROUPS, GROUP_W, NSTATE), F32)],
        args=[xbc, xbc, xbc, dt_e, cs_e, d_e, states, states, dy], name=name, sem=("arbitrary", "arbitrary"), comm=comm)
    return outs if comm is None else (outs, carried)


def _ssd_post(ddt_e, dcs_e, dd_p, dt_raw, dt_bias, a_log, n_heads, name):
    T, n_inner = ddt_e.shape

    def body(ddt_ref, dcs_ref, dd_ref, r_ref, b_ref, al_ref, draw_ref, dbias_ref, dal_ref, ddsk_ref):
        @pl.when(pl.program_id(0) == 0)
        def _():
            dbias_ref[...] = jnp.zeros_like(dbias_ref)
            dal_ref[...] = jnp.zeros_like(dal_ref)
            ddsk_ref[...] = jnp.zeros_like(ddsk_ref)

        ex = _head_expand(n_inner)
        red = lambda v: sum(_dot(p, ex, NT) for p in _split3(v))
        raw = r_ref[...] + b_ref[...]
        dt = _softplus(raw)
        A = -jnp.exp(al_ref[...])
        i = lax.broadcasted_iota(jnp.int32, (CHUNK, CHUNK), 0)
        j = lax.broadcasted_iota(jnp.int32, (CHUNK, CHUNK), 1)
        upper = (j >= i).astype(BF16)
        da = sum(_dot(upper, p) for p in _split3(red(dcs_ref[...])))
        ddt = red(ddt_ref[...]) + da * A
        lane = lax.broadcasted_iota(jnp.int32, (CHUNK, LANES), 1)
        draw = jnp.where(lane < n_heads, ddt * jax.nn.sigmoid(raw), 0.0)
        draw_ref[...] = draw.astype(BF16)
        dbias_ref[...] += jnp.sum(draw, axis=0, keepdims=True)
        dal_ref[...] += jnp.sum(da * dt, axis=0, keepdims=True) * A
        ddsk_ref[...] += red(jnp.broadcast_to(dd_ref[0], (8, n_inner)))[0:1, :]

    wide = pl.BlockSpec((CHUNK, n_inner), lambda c: (c, 0))
    blk = pl.BlockSpec((CHUNK, LANES), lambda c: (c, 0))
    return pl.pallas_call(
        body, grid=(T // CHUNK,),
        in_specs=[wide, wide, pl.BlockSpec((1, 1, n_inner), lambda c: (c, 0, 0)), blk, _vec(LANES), _vec(LANES)],
        out_specs=[blk, _vec(LANES), _vec(LANES), _vec(LANES)],
        out_shape=[_sds((T, LANES), BF16)] + [_sds((1, LANES), F32)] * 3,
        name=name, compiler_params=_params("arbitrary"))(ddt_e, dcs_e, dd_p, dt_raw, dt_bias, a_log)


def _row2(v):
    return v.reshape(1, -1).astype(F32)


def _pad_lanes(v):
    return jnp.pad(_row2(v), ((0, 0), (0, LANES - v.shape[-1])))


class _NoExchange:
    def __init__(self, W):
        self.W, self.grads = W, {}

    def weight(self, k):
        return self.W[k]

    def carry(self, name):
        return None

    def carried(self, name, outs):
        pass

    def grad(self, k, g):
        self.grads[k] = g

    def tok(self):
        return jnp.zeros((), F32)

    def point(self, name, value):
        pass


def _local_step(x, tgt, S, small):
    T, D = x.shape

    def mm(a, b, *, name, **kw):
        comm = S.carry(name)
        if comm is None:
            return _mm(a, b, name=name, **kw)
        res, outs = _mm(a, b, name=name, comm=comm, **kw)
        S.carried(name, outs)
        return res

    def carrying(fn, *args, name):
        comm = S.carry(name)
        if comm is None:
            return fn(*args, name)
        res, outs = fn(*args, name, comm=comm)
        S.carried(name, outs)
        return res

    n_inner = 2 * D
    n_heads = n_inner // HEADDIM
    norm_mix, norm_mlp, norm_final = _row2(small["norm_mix"]), _row2(small["norm_mlp"]), _row2(small["norm_final"])
    b_gate, ssm_b, ssm_norm_w = _row2(small["b_gate"]), _row2(small["ssm_conv_b"]), _row2(small["ssm_norm_w"])
    dt_bias, a_log = _pad_lanes(small["dt_bias"]), _pad_lanes(small["A_log"])
    d_e = jnp.repeat(small["D_skip"].astype(F32), HEADDIM).reshape(1, n_inner)

    hb = carrying(_rms_fwd, x, norm_mix, name="rms_mix")
    sc_w, ssm_w = S.weight("sc_conv_w"), S.weight("ssm_conv_w")
    p_xbc = mm(hb, S.weight("xbc"), mode="nn", name="proj_xbc")
    p_dt = mm(hb, S.weight("dt"), mode="nn", name="proj_dt")
    p_z = mm(hb, S.weight("z"), mode="nn", name="proj_z")
    p_sc = mm(hb, S.weight("sc"), mode="nn", name="proj_sc")
    p_gate = mm(hb, S.weight("gate"), mode="nn", name="proj_gate")
    xbc = carrying(_ssm_conv_fwd, p_xbc, ssm_w, ssm_b, name="ssm_conv_fwd")
    dt_e, cs_e = _ssd_prep(p_dt, dt_bias, a_log, n_inner, "ssd_prep")
    y, states = carrying(_ssd_fwd, xbc, dt_e, cs_e, d_e, name="ssd_fwd")
    S.point("ssd_fwd_done", y)
    yb = carrying(_gnorm_fwd, y, p_z, ssm_norm_w, name="gnorm_fwd")
    ya = _sc_fwd(p_sc, sc_w, "sc_fwd")
    br_a = mm(ya, S.weight("bsc"), mode="nn", name="branch_sc")
    br_b = mm(yb, S.weight("bssm"), mode="nn", name="branch_ssm")
    merged = _merge_fwd(p_gate, b_gate, br_a, br_b, "merge_fwd")
    x1 = mm(merged, S.weight("out"), mode="nn", name="out_proj", extras=(x,), epi=_epi_add)
    h2 = _rms_fwd(x1, norm_mlp, "rms_mlp")
    r_act = mm(h2, S.weight("w1"), mode="nn", name="mlp_up", epi=_epi_relu2, out_dtypes=(BF16,))
    x2 = mm(r_act, S.weight("w2"), mode="nn", name="mlp_down", extras=(x1,), epi=_epi_add)
    dx2, dx2b, g_norm_final, loss_row = _final(x2, norm_final, tgt, "final")

    S.grad("w2", mm(r_act, dx2b, mode="tn", name="mlp_down_dw", out_dtypes=(BF16,)))
    da = mm(dx2b, S.weight("w2"), mode="nt", name="mlp_down_dx", extras=(r_act,), epi=_epi_relu2_bwd, out_dtypes=(BF16,))
    S.grad("w1", mm(h2, da, mode="tn", name="mlp_up_dw", out_dtypes=(BF16,)))
    dh2 = mm(da, S.weight("w1"), mode="nt", name="mlp_up_dx")
    dx1, dx1b, g_norm_mlp = _rms_bwd(x1, norm_mlp + S.tok(), dh2, dx2, "rms_mlp_bwd")
    S.grad("out", mm(merged, dx1b, mode="tn", name="out_proj_dw", out_dtypes=(BF16,)))
    dmerged = mm(dx1b, S.weight("out"), mode="nt", name="out_proj_dx")
    dbr_a, dbr_b, d_gate, g_b_gate = _merge_bwd(dmerged, p_gate, b_gate, br_a, br_b, "merge_bwd")
    S.grad("bssm", mm(yb, dbr_b, mode="tn", name="branch_ssm_dw", out_dtypes=(BF16,)))
    S.grad("bsc", mm(ya, dbr_a, mode="tn", name="branch_sc_dw", out_dtypes=(BF16,)))
    dyb = mm(dbr_b, S.weight("bssm"), mode="nt", name="branch_ssm_dx")
    dya = mm(dbr_a, S.weight("bsc"), mode="nt", name="branch_sc_dx")
    dy, d_z, g_ssm_norm_w = _gnorm_bwd(y, p_z, ssm_norm_w + S.tok(), dyb, "gnorm_bwd")
    dxs, dB, dC, ddt_e, dcs_e, dd_p = carrying(_ssd_bwd, xbc, dt_e, cs_e, d_e, states, dy, name="ssd_bwd")
    d_dt, g_dt_bias, g_a_log, g_d_skip = _ssd_post(ddt_e, dcs_e, dd_p, p_dt, dt_bias, a_log, n_heads, "ssd_post")
    d_xbc, g_ssm_w, g_ssm_b = carrying(_ssm_conv_bwd, p_xbc, ssm_w, ssm_b, dxs, dB, dC, name="ssm_conv_bwd")
    d_scB, d_scC, d_scX, g_sc_w = _sc_bwd(p_sc, sc_w, dya, "sc_bwd")
    d_sc = jnp.concatenate([d_scB, d_scC, d_scX], axis=1)
    pieces = [("sc", d_sc), ("z", d_z), ("xbc", d_xbc), ("dt", d_dt), ("gate", d_gate)]
    S.grad("win", {k: mm(hb, d, mode="tn", name="proj_dw_" + k, out_dtypes=(BF16,)) for k, d in pieces})
    pieces = [(k, d + S.tok().astype(d.dtype) if k == "dt" else d) for k, d in pieces]
    dh = mm([d for _, d in pieces], [S.weight(k) for k, _ in pieces], mode="nt", name="proj_dx")
    grad_x, _, g_norm_mix = _rms_bwd(x, norm_mix, dh, dx1, "rms_mix_bwd")

    g_small = dict(norm_mix=g_norm_mix, b_gate=g_b_gate, sc_conv_w=g_sc_w, ssm_conv_w=g_ssm_w, ssm_conv_b=g_ssm_b,
                   dt_bias=g_dt_bias, A_log=g_a_log, D_skip=g_d_skip, ssm_norm_w=g_ssm_norm_w, norm_mlp=g_norm_mlp,
                   norm_final=g_norm_final, loss=loss_row)
    return grad_x, g_small


class _Place:
    def __init__(self, k=0):
        x, y, c = lax.axis_index("x"), lax.axis_index("y"), lax.axis_index("c")
        self.x = 1 - x if k & 4 else x
        self.y = 1 - y if k & 2 else y
        self.c = 1 - c if k & 1 else c
        self.chip = 2 * self.x + self.y
        self.id = 2 * self.chip + self.c


ICI_PEERS = (2, 4, 6)
SIBLING = (1,)
ALL_PEERS = (1, 2, 3, 4, 5, 6, 7)


class _Comm:
    def __init__(self, arrs, out_shape, ks, src, dst, own=None, aliases=None):
        self.arrs, self.out_shape, self.ks = list(arrs), list(out_shape), tuple(ks)
        self.n = len(self.arrs)
        self.src, self.dst, self.own = src, dst, own
        self.aliases = aliases or {}
        dma = pltpu.SemaphoreType.DMA
        self.scratch = [dma((self.n, len(self.ks))), dma((self.n, len(self.ks))), dma((self.n,))]

    def _copies(self, ins, outs, sems, with_recvs):
        send_sems, recv_sems, local_sems = sems
        me = _Place()
        owns, sends, recvs = [], [], []
        for a in range(self.n):
            if self.own is not None:
                s, d = self.own(a, ins[a], outs[a], me)
                owns.append(pltpu.make_async_copy(s, d, local_sems.at[a]))
            for i, k in enumerate(self.ks):
                peer = _Place(k)
                for sender, lst in ((me, sends), (peer, recvs)) if with_recvs else ((me, sends),):
                    lst.append(pltpu.make_async_remote_copy(
                        src_ref=self.src(a, ins[a], me, peer), dst_ref=self.dst(a, outs[a], sender),
                        send_sem=send_sems.at[a, i], recv_sem=recv_sems.at[a, i],
                        device_id=(peer.x, peer.y, peer.c), device_id_type=MESH))
        return owns, sends, recvs

    def start(self, ins, outs, sems):
        owns, sends, _ = self._copies(ins, outs, sems, False)
        for cp in owns + sends:
            cp.start()

    def finish(self, ins, outs, sems):
        owns, sends, recvs = self._copies(ins, outs, sems, True)
        for cp in recvs:
            cp.wait_recv()
        for cp in sends:
            cp.wait_send()
        for cp in owns:
            cp.wait()


class _GatherBoth:
    def __init__(self, shards):
        self.arrs, self.n, self.aliases = list(shards), len(shards), {}
        self.out_shape = [_sds((4, 2) + s.shape, s.dtype) for s in shards]
        dma = pltpu.SemaphoreType.DMA
        self.scratch = [dma((self.n, 7)), dma((self.n, 7)), dma((self.n,))]

    def _copy(self, a, j, src, slot, to, outs, sems):
        return pltpu.make_async_remote_copy(src_ref=src, dst_ref=outs[a].at[slot.chip, slot.c], send_sem=sems[0].at[a, j],
                                            recv_sem=sems[1].at[a, j], device_id=(to.x, to.y, to.c), device_id_type=MESH)

    def start(self, ins, outs, sems):
        me, sib = _Place(), _Place(1)
        for a in range(self.n):
            pltpu.make_async_copy(ins[a], outs[a].at[me.chip, me.c], sems[2].at[a]).start()
            self._copy(a, 0, ins[a], me, sib, outs, sems).start()
            for i, k in enumerate(ICI_PEERS):
                self._copy(a, 1 + i, ins[a], me, _Place(k), outs, sems).start()

    def finish(self, ins, outs, sems):
        me, sib = _Place(), _Place(1)
        passed = []
        for i, k in enumerate(ICI_PEERS):
            peer = _Place(k)
            for a in range(self.n):
                self._copy(a, 1 + i, ins[a], peer, peer, outs, sems).wait_recv()
                cp = self._copy(a, 4 + i, outs[a].at[peer.chip, peer.c], peer, sib, outs, sems)
                cp.start()
                passed.append(cp)
        for a in range(self.n):
            self._copy(a, 0, ins[a], sib, sib, outs, sems).wait_recv()
            for i, k in enumerate(ICI_PEERS):
                far = _Place(k | 1)
                self._copy(a, 4 + i, outs[a].at[far.chip, far.c], far, sib, outs, sems).wait_recv()
        for a in range(self.n):
            self._copy(a, 0, ins[a], me, sib, outs, sems).wait_send()
            for i, k in enumerate(ICI_PEERS):
                self._copy(a, 1 + i, ins[a], me, _Place(k), outs, sems).wait_send()
            pltpu.make_async_copy(ins[a], outs[a].at[me.chip, me.c], sems[2].at[a]).wait()
        for cp in passed:
            cp.wait_send()


def _run_comm(comm, name, after=()):
    n, n_after = comm.n, len(after)

    def body(*refs):
        ins, outs, sems = refs[:n], refs[n + n_after:2 * n + n_after], refs[2 * n + n_after:]
        comm.start(ins, outs, sems)
        comm.finish(ins, outs, sems)

    return list(pl.pallas_call(body, in_specs=[ANY] * (n + n_after), out_specs=[ANY] * n, out_shape=comm.out_shape,
                               scratch_shapes=comm.scratch, input_output_aliases=dict(comm.aliases), name=name)(*comm.arrs, *after))


def _gather_ici(shards):
    return _Comm(shards, [_sds((4, 2) + s.shape, s.dtype) for s in shards], ICI_PEERS,
                 src=lambda a, i, me, p: i, dst=lambda a, o, s: o.at[s.chip, s.c], own=lambda a, i, o, me: (i, o.at[me.chip, me.c]))


def _gather_sibling(bufs):
    return _Comm(bufs, [_sds(b.shape, b.dtype) for b in bufs], SIBLING,
                 src=lambda a, i, me, p: i.at[:, me.c], dst=lambda a, o, s: o.at[:, s.c], aliases={a: a for a in range(len(bufs))})


def _scatter_sibling(parts):
    return _Comm(parts, [_sds((4,) + p.shape[2:], p.dtype) for p in parts], SIBLING,
                 src=lambda a, i, me, p: i.at[:, p.c], dst=lambda a, o, s: o)


def _scatter_ici(parts):
    return _Comm(parts, [_sds(p.shape, p.dtype) for p in parts], ICI_PEERS,
                 src=lambda a, i, me, p: i.at[p.chip], dst=lambda a, o, s: o.at[s.chip], own=lambda a, i, o, me: (i.at[me.chip], o.at[me.chip]))


HBM_SPEC = pl.BlockSpec(memory_space=pltpu.HBM)
SEM_SPEC = pl.BlockSpec(memory_space=pltpu.SEMAPHORE)
DATAFLOW = pltpu.SideEffectType.DATAFLOW_SIDE_EFFECTING


def _own_part(parts, name):
    n, R, C = parts.shape
    tr = R if R <= 256 else 256
    chip = (2 * lax.axis_index("x") + lax.axis_index("y")).astype(jnp.int32).reshape(1)

    def body(q_ref, p_ref, o_ref):
        o_ref[...] = p_ref[...]

    blk = pl.BlockSpec((1, tr, C), lambda i, q_ref: (q_ref[0], i, 0))
    spec = pltpu.PrefetchScalarGridSpec(num_scalar_prefetch=1, grid=(R // tr,), in_specs=[blk], out_specs=blk)
    return pl.pallas_call(body, grid_spec=spec, out_shape=_sds((n, R, C), parts.dtype), name=name,
                          compiler_params=_params("parallel"))(chip, parts)


def _ici_copy(gather, a, srcs, lands, send_sems, recv_sems, i, me, peer, sender):
    src = lands[a].at[me.chip, me.c] if gather else srcs[a].at[peer.chip]
    dst = lands[a].at[sender.chip, sender.c] if gather else lands[a].at[sender.chip]
    j = a * len(ICI_PEERS) + i
    return pltpu.make_async_remote_copy(src_ref=src, dst_ref=dst, send_sem=send_sems.at[j], recv_sem=recv_sems.at[j],
                                        device_id=(peer.x, peer.y, peer.c), device_id_type=MESH)


def _ici_start(srcs, lands, gather, name):
    n, n_s = len(lands), len(srcs)
    bufs = list(srcs) + list(lands)

    def body(*refs):
        src_refs, land_refs = refs[:n_s], refs[n_s:n_s + n]
        send_sems, recv_sems = refs[n_s + n], refs[n_s + n + 1]
        token = refs[-1]
        me = _Place()
        for a in range(n):
            for i, k in enumerate(ICI_PEERS):
                _ici_copy(gather, a, src_refs, land_refs, send_sems, recv_sems, i, me, _Place(k), me).start()
        token[...] = jnp.zeros_like(token)

    dma = pltpu.SemaphoreType.DMA((n * len(ICI_PEERS),))
    outs = pl.pallas_call(
        body, name=name, out_shape=(dma, dma, *[pltpu.HBM(v.shape, v.dtype) for v in bufs], _sds((8, LANES), F32)),
        in_specs=(HBM_SPEC,) * len(bufs),
        out_specs=(SEM_SPEC, SEM_SPEC) + (HBM_SPEC,) * len(bufs) + (pl.BlockSpec(memory_space=pltpu.VMEM),),
        input_output_aliases={j: 2 + j for j in range(len(bufs))}, compiler_params=pltpu.CompilerParams(has_side_effects=DATAFLOW),
    )(*[pltpu.with_memory_space_constraint(v, pltpu.HBM) for v in bufs])
    return outs[0], outs[1], list(outs[2:2 + n_s]), list(outs[2 + n_s:2 + n_s + n]), outs[-1]


def _ici_wait(flight, after, gather, name):
    send_sems, recv_sems, srcs, lands, _ = flight
    n, n_s = len(lands), len(srcs)
    bufs = srcs + lands

    def body(*refs):
        src_refs, land_refs = refs[:n_s], refs[n_s:n_s + n]
        s_sems, r_sems = refs[n_s + n], refs[n_s + n + 1]
        me = _Place()
        for a in range(n):
            for i, k in enumerate(ICI_PEERS):
                peer = _Place(k)
                cp = _ici_copy(gather, a, src_refs, land_refs, s_sems, r_sems, i, me, peer, peer)
                cp.wait_send()
                cp.wait_recv()

    outs = pl.pallas_call(
        body, name=name, out_shape=tuple(pltpu.HBM(v.shape, v.dtype) for v in bufs),
        in_specs=(HBM_SPEC,) * len(bufs) + (SEM_SPEC, SEM_SPEC) + (ANY,) * len(after), out_specs=(HBM_SPEC,) * len(bufs),
        input_output_aliases={j: j for j in range(len(bufs))}, compiler_params=pltpu.CompilerParams(has_side_effects=DATAFLOW),
    )(*bufs, send_sems, recv_sems, *after)
    return list(outs[n_s:])


def _own_shard(shard, name):
    R, C = shard.shape
    tr = R if R <= 256 else 256
    place = jnp.stack([2 * lax.axis_index("x") + lax.axis_index("y"), lax.axis_index("c")]).astype(jnp.int32)

    def body(q_ref, s_ref, o_ref):
        o_ref[0, 0] = s_ref[...].astype(o_ref.dtype)

    spec = pltpu.PrefetchScalarGridSpec(
        num_scalar_prefetch=1, grid=(R // tr,), in_specs=[pl.BlockSpec((tr, C), lambda i, q_ref: (i, 0))],
        out_specs=pl.BlockSpec((1, 1, tr, C), lambda i, q_ref: (q_ref[0], q_ref[1], i, 0)))
    return pl.pallas_call(body, grid_spec=spec, out_shape=_sds((4, 2, R, C), BF16), name=name,
                          compiler_params=_params("parallel"))(place, shard)


def _col_pieces(widths):
    out, c = [], 0
    for k, w in widths:
        out.append((k, c, w))
        c += w
    return out


def _split_range(c0, n, bounds):
    parts, c = [], c0
    while c < c0 + n:
        r = max(i for i in range(len(bounds) - 1) if bounds[i] <= c)
        w = min(c0 + n, bounds[r + 1]) - c
        parts.append((r, c - bounds[r], w))
        c += w
    return parts


def _win_unpack(g, widths, name):
    n, R, C = g.shape
    tr = min(256, R)
    pieces = _col_pieces(widths)
    padded = [-(-w // LANES) * LANES for _, _, w in pieces]
    shard_bounds = [s * C for s in range(n + 1)]

    def body(g_ref, *o_refs):
        for (k, c0, w), o_ref in zip(pieces, o_refs):
            for t in range(0, o_ref.shape[1], LANES):
                valid = max(0, min(LANES, w - t))
                cols = [g_ref[s, :, o:o + ww] for s, o, ww in _split_range(c0 + t, valid, shard_bounds)] if valid else []
                if valid < LANES:
                    cols.append(jnp.zeros((tr, LANES - valid), g_ref.dtype))
                o_ref[:, t:t + LANES] = cols[0] if len(cols) == 1 else jnp.concatenate(cols, axis=1)

    return pl.pallas_call(
        body, grid=(R // tr,), in_specs=[pl.BlockSpec((n, tr, C), lambda i: (0, i, 0))],
        out_specs=[pl.BlockSpec((tr, p), lambda i: (i, 0)) for p in padded],
        out_shape=[_sds((R, p), g.dtype) for p in padded], name=name, compiler_params=_params("parallel"))(g)


def _win_pack(grads, widths, n, name):
    R = grads[0].shape[0]
    tr = min(256, R)
    pieces = _col_pieces(widths)
    total = pieces[-1][1] + pieces[-1][2]
    C = total // n
    bounds = [c0 for _, c0, _ in pieces] + [total]

    def body(*refs):
        g_refs, o_ref = refs[:-1], refs[-1]
        for s in range(n):
            for t in range(0, C, LANES):
                w = min(LANES, C - t)
                cols = [g_refs[r][:, o:o + ww] for r, o, ww in _split_range(s * C + t, w, bounds)]
                o_ref[s, :, t:t + w] = cols[0] if len(cols) == 1 else jnp.concatenate(cols, axis=1)

    return pl.pallas_call(
        body, grid=(R // tr,), in_specs=[pl.BlockSpec((tr, gr.shape[1]), lambda i: (i, 0)) for gr in grads],
        out_specs=pl.BlockSpec((n, tr, C), lambda i: (0, i, 0)), out_shape=_sds((n, R, C), grads[0].dtype),
        name=name, compiler_params=_params("parallel"))(*grads)


def _gather_all(arrs):
    return _Comm(arrs, [_sds((N_DEV,) + a.shape, a.dtype) for a in arrs], ALL_PEERS,
                 src=lambda a, i, me, p: i, dst=lambda a, o, s: o.at[s.id], own=lambda a, i, o, me: (i, o.at[me.id]))


def _add_halves(parts, got, name):
    n, _, R, C = parts.shape
    tr = R if R <= 256 else 256
    assert R % tr == 0
    core = lax.axis_index("c").astype(jnp.int32).reshape(1)

    def body(c_ref, p_ref, g_ref, o_ref):
        o_ref[0] = (p_ref[0, 0].astype(F32) + g_ref[0].astype(F32)).astype(o_ref.dtype)

    spec = pltpu.PrefetchScalarGridSpec(
        num_scalar_prefetch=1, grid=(n, R // tr),
        in_specs=[pl.BlockSpec((1, 1, tr, C), lambda q, i, c_ref: (q, c_ref[0], i, 0)), pl.BlockSpec((1, tr, C), lambda q, i, c_ref: (q, i, 0))],
        out_specs=pl.BlockSpec((1, tr, C), lambda q, i, c_ref: (q, i, 0)))
    return pl.pallas_call(body, grid_spec=spec, out_shape=_sds((n, R, C), parts.dtype), name=name,
                          compiler_params=_params("parallel", "parallel"))(core, parts, got)


def _adam(w, m, v, gparts, name):
    R, C = w.shape
    n = gparts.shape[0]
    tr = R if R <= 256 else 128
    assert R % tr == 0
    c1 = 1.0 / (1.0 - ADAM_B1 ** ADAM_STEP)
    c2 = 1.0 / (1.0 - ADAM_B2 ** ADAM_STEP)

    def body(w_ref, m_ref, v_ref, g_ref, go_ref, d_ref, mo_ref, vo_ref):
        g = g_ref[0].astype(F32)
        for s in range(1, n):
            g = g + g_ref[s].astype(F32)
        mn = ADAM_B1 * m_ref[...] + (1.0 - ADAM_B1) * g
        vn = ADAM_B2 * v_ref[...] + (1.0 - ADAM_B2) * (g * g)
        go_ref[...] = g
        mo_ref[...] = mn
        vo_ref[...] = vn
        d_ref[...] = -ADAM_LR * ((mn * c1) / (jnp.sqrt(vn * c2) + ADAM_EPS) + ADAM_WD * w_ref[...])

    blk = pl.BlockSpec((tr, C), lambda i: (i, 0))
    return pl.pallas_call(
        body, grid=(R // tr,), in_specs=[blk, blk, blk, pl.BlockSpec((n, tr, C), lambda i: (0, i, 0))],
        out_specs=[blk] * 4, out_shape=[_sds((R, C), F32)] * 4, name=name, compiler_params=_params("parallel"))(w, m, v, gparts)


_SMALL_ORDER = ("norm_mix", "b_gate", "sc_conv_w", "ssm_conv_w", "ssm_conv_b", "dt_bias", "A_log", "D_skip", "ssm_norm_w",
                "norm_mlp", "norm_final", "loss")
_REPLICATED = ("norm_mix", "b_gate", "ssm_conv_b", "dt_bias", "A_log", "D_skip", "ssm_norm_w", "norm_mlp", "norm_final")


def _cols_to_slots(g, n):
    R = g.shape[0]
    return jnp.transpose(g.reshape(R, n, g.shape[1] // n), (1, 0, 2))


def _slots_to_cols(g):
    n, R, C = g.shape
    return jnp.transpose(g, (1, 0, 2)).reshape(R, n * C)


def kernel(x, norm_mix, w_in, b_gate, sc_conv_w, ssm_conv_w, ssm_conv_b, dt_bias, A_log, D_skip, ssm_norm_w, w_branch_sc, w_branch_ssm, w_out, norm_mlp, w_mlp1, w_mlp2, norm_final, loss_target, m_norm_mix, m_w_in, m_b_gate, m_sc_conv_w, m_ssm_conv_w, m_ssm_conv_b, m_dt_bias, m_A_log, m_D_skip, m_ssm_norm_w, m_w_branch_sc, m_w_branch_ssm, m_w_out, m_norm_mlp, m_w_mlp1, m_w_mlp2, m_norm_final, v_norm_mix, v_w_in, v_b_gate, v_sc_conv_w, v_ssm_conv_w, v_ssm_conv_b, v_dt_bias, v_A_log, v_D_skip, v_ssm_norm_w, v_w_branch_sc, v_w_branch_ssm, v_w_out, v_norm_mlp, v_w_mlp1, v_w_mlp2, v_norm_final):
    T, D = x.shape[1], x.shape[2]
    n_inner = 2 * D
    n_heads = n_inner // HEADDIM
    n_xbc = n_inner + 2 * NGROUPS * NSTATE
    me = 4 * lax.axis_index("x") + 2 * lax.axis_index("y") + lax.axis_index("c")

    in_cols = [("sc", 3 * D), ("z", n_inner), ("xbc", n_xbc), ("dt", n_heads), ("gate", 2 * D)]
    by_owner = lambda b: b.reshape((N_DEV,) + b.shape[2:])
    to_owner = lambda g: g.reshape((4, 2) + g.shape[1:])
    rows_of = lambda g: to_owner(g.reshape((N_DEV, g.shape[0] // N_DEV) + g.shape[1:]))
    cols_of = lambda g: to_owner(_cols_to_slots(g, N_DEV))

    class Schedule(_NoExchange):
        late = ("bssm", "bsc", "out", "w1", "w2")
        gather_sib = dict(gnorm_fwd=("bsc", "bssm", "out"), branch_ssm=("w1", "w2"))
        scatter_sib = dict(mlp_up_dx=("w2", "w1"), branch_ssm_dx=("out", "bssm", "bsc"))
        shards = dict(bsc=w_branch_sc, bssm=w_branch_ssm, out=w_out, w1=w_mlp1, w2=w_mlp2)

        def __init__(self):
            self.W, self.staged, self.grads, self.summed, self.scatters = {}, {}, {}, {}, []
            self.token = jnp.zeros((), F32)

        def first_weights(self, bufs):
            self.W.update(zip([k for k, _ in in_cols], _win_unpack(by_owner(bufs[0]), in_cols, "win_unpack")))
            self.W.update(sc_conv_w=_slots_to_cols(by_owner(bufs[1])), ssm_conv_w=_slots_to_cols(by_owner(bufs[2])))
            lands = [_own_shard(self.shards[k], "own_shard_" + k) for k in self.late]
            self.gather_flight = _ici_start([], lands, True, "gather_late_start")
            self.token = self.gather_flight[4][0, 0]
            self.W["dt"] = self.W["dt"] + self.token.astype(BF16)

        def tok(self):
            return self.token

        def point(self, name, value):
            if name == "ssd_fwd_done":
                lands = _ici_wait(self.gather_flight, [value], True, "gather_late_wait")
                self.staged.update(zip(self.late, lands))

        def carry(self, name):
            if name == "rms_mix":
                return _GatherBoth([w_in.astype(BF16), sc_conv_w, ssm_conv_w])
            if name in self.gather_sib:
                return _gather_sibling([self.staged.pop(k) for k in self.gather_sib[name]])
            if name in self.scatter_sib:
                return _scatter_sibling([self.grads[k] for k in self.scatter_sib[name]])
            return None

        def start_scatter(self, keys, halves):
            lands = [_own_part(h, "own_part_" + k) for k, h in zip(keys, halves)]
            flight = _ici_start(halves, lands, False, "scatter_%s_start" % keys[0])
            self.scatters.append((keys, flight))
            self.token = flight[4][0, 0]

        def carried(self, name, outs):
            if name == "rms_mix":
                self.first_weights(outs)
            elif name in self.gather_sib:
                for k, b in zip(self.gather_sib[name], outs):
                    full = by_owner(b)
                    self.W[k] = _slots_to_cols(full) if k == "w1" else full.reshape(-1, D)
            else:
                keys = self.scatter_sib[name]
                self.start_scatter(keys, [_add_halves(self.grads[k], b, "add_halves_" + k) for k, b in zip(keys, outs)])

        def grad(self, k, g):
            if k == "win":
                g = to_owner(_win_pack([g[k] for k, _ in in_cols], in_cols, N_DEV, "win_pack"))
                got = _run_comm(_scatter_sibling([g]), "scatter_sibling_win")[0]
                self.start_scatter(("win",), [_add_halves(g, got, "add_halves_win")])
            else:
                self.grads[k] = cols_of(g) if k == "w1" else rows_of(g)

        def finish_scatter(self, after):
            keys, flight = self.scatters.pop(0)
            return dict(zip(keys, _ici_wait(flight, after, False, "scatter_%s_wait" % keys[0])))

    S = Schedule()
    small = dict(norm_mix=norm_mix, b_gate=b_gate, ssm_conv_b=ssm_conv_b, dt_bias=dt_bias, A_log=A_log, D_skip=D_skip,
                 ssm_norm_w=ssm_norm_w, norm_mlp=norm_mlp, norm_final=norm_final)
    grad_x, g_small = _local_step(x.reshape(T, D), loss_target.reshape(T, D), S, small)

    small_flat = jnp.concatenate([g_small[k].reshape(-1) for k in _SMALL_ORDER])
    n_small = small_flat.shape[0]
    rows = -(-n_small // (8 * LANES)) * 8
    small_pack = jnp.pad(small_flat, (0, rows * LANES - n_small)).reshape(rows, LANES)

    res = {}
    big = [("w_in", "win", w_in, m_w_in, v_w_in), ("w_branch_sc", "bsc", w_branch_sc, m_w_branch_sc, v_w_branch_sc),
           ("w_branch_ssm", "bssm", w_branch_ssm, m_w_branch_ssm, v_w_branch_ssm), ("w_out", "out", w_out, m_w_out, v_w_out),
           ("w_mlp1", "w1", w_mlp1, m_w_mlp1, v_w_mlp1), ("w_mlp2", "w2", w_mlp2, m_w_mlp2, v_w_mlp2)]
    by_grad = {gk: (k, w, m, v) for k, gk, w, m, v in big}
    after = [grad_x]
    while S.scatters:
        for gk, parts in S.finish_scatter(after).items():
            k, w, m, v = by_grad[gk]
            res[k] = _adam(w, m, v, parts, "adam_" + k)
            after = [res[k][1]]
    small_parts = _run_comm(_gather_all([small_pack]), "gather_small", after=[res[k][1]])[0]

    sizes = {k: g_small[k].size for k in _SMALL_ORDER}
    offs, o = {}, 0
    for k in _SMALL_ORDER:
        offs[k] = o
        o += sizes[k]
    rep_w = dict(norm_mix=norm_mix, b_gate=b_gate, ssm_conv_b=ssm_conv_b, dt_bias=dt_bias, A_log=A_log, D_skip=D_skip,
                 ssm_norm_w=ssm_norm_w, norm_mlp=norm_mlp, norm_final=norm_final)
    rep_m = dict(norm_mix=m_norm_mix, b_gate=m_b_gate, ssm_conv_b=m_ssm_conv_b, dt_bias=m_dt_bias, A_log=m_A_log, D_skip=m_D_skip,
                 ssm_norm_w=m_ssm_norm_w, norm_mlp=m_norm_mlp, norm_final=m_norm_final)
    rep_v = dict(norm_mix=v_norm_mix, b_gate=v_b_gate, ssm_conv_b=v_ssm_conv_b, dt_bias=v_dt_bias, A_log=v_A_log, D_skip=v_D_skip,
                 ssm_norm_w=v_ssm_norm_w, norm_mlp=v_norm_mlp, norm_final=v_norm_final)

    def pack(d):
        segs = [jnp.pad(d[k].astype(F32).reshape(-1), (0, sizes[k] - d[k].size)) if k in d else jnp.zeros((sizes[k],), F32)
                for k in _SMALL_ORDER]
        return jnp.pad(jnp.concatenate(segs), (0, rows * LANES - n_small)).reshape(rows, LANES)

    sm = _adam(pack(rep_w), pack(rep_m), pack(rep_v), small_parts, "adam_small")
    sm = [s.reshape(-1) for s in sm]
    for k in _REPLICATED:
        n_k = rep_w[k].shape[0]
        res[k] = tuple(s[offs[k]:offs[k] + n_k] for s in sm)
    loss = sm[0][offs["loss"]]
    for k, w, m, v, K, full in (("sc_conv_w", sc_conv_w, m_sc_conv_w, v_sc_conv_w, SC_K, D),
                                ("ssm_conv_w", ssm_conv_w, m_ssm_conv_w, v_ssm_conv_w, SSM_K, n_xbc)):
        g_full = sm[0][offs[k]:offs[k] + K * full].reshape(K, full)
        cw = full // N_DEV
        g_mine = lax.dynamic_slice_in_dim(g_full, me * cw, cw, axis=1)
        res[k] = _adam(w, m, v, g_mine[None], "adam_" + k)

    order = ("norm_mix", "w_in", "b_gate", "sc_conv_w", "ssm_conv_w", "ssm_conv_b", "dt_bias", "A_log", "D_skip", "ssm_norm_w",
             "w_branch_sc", "w_branch_ssm", "w_out", "norm_mlp", "w_mlp1", "w_mlp2", "norm_final")
    outs = [loss, grad_x.reshape(1, T, D)]
    for j in range(4):
        outs += [res[k][j] for k in order]
    return tuple(outs)
```

```python
import functools

import jax
import jax.numpy as jnp
from jax import lax
from jax.experimental import pallas as pl
from jax.experimental.pallas import tpu as pltpu

F32 = jnp.float32
BF16 = jnp.bfloat16

EPS = 1e-6
N_DEV = 8
HEADDIM = 64
NSTATE = 128
CHUNK = 128
NGROUPS = 8
GROUP_W = 256
SC_K = 3
SSM_K = 4
LANES = 128

ADAM_LR = 0.001
ADAM_B1 = 0.9
ADAM_B2 = 0.999
ADAM_EPS = 1e-08
ADAM_WD = 0.01
ADAM_STEP = 10

NN = (((1,), (0,)), ((), ()))
NT = (((1,), (1,)), ((), ()))
TN = (((0,), (0,)), ((), ()))
_DIMS = {"nn": NN, "nt": NT, "tn": TN}

ANY = pl.BlockSpec(memory_space=pl.ANY)
MESH = pl.DeviceIdType.MESH


def _sds(shape, dtype):
    return jax.ShapeDtypeStruct(tuple(shape), dtype)


def _dot(a, b, dims=NN):
    return lax.dot_general(a, b, dims, preferred_element_type=F32)


def _dot3(a, b, dims=NN):
    return lax.dot_general(a, b, dims, preferred_element_type=F32, precision=lax.Precision.HIGH)


def _params(*sem):
    return pltpu.CompilerParams(dimension_semantics=tuple(sem))


def _call(body, *, grid, in_specs, out_specs, out_shape, args, name, sem, scratch=(), comm=None):
    if comm is None:
        outs = pl.pallas_call(body, grid=grid, in_specs=list(in_specs), out_specs=list(out_specs), out_shape=list(out_shape),
                              scratch_shapes=list(scratch), name=name, compiler_params=_params(*sem))(*args)
        return list(outs), None
    n, n_in, n_out, n_scr = comm.n, len(in_specs), len(out_shape), len(scratch)

    def wrapped(*refs):
        ins, c_in = refs[:n_in], refs[n_in:n_in + n]
        outs, c_out = refs[n_in + n:n_in + n + n_out], refs[n_in + n + n_out:n_in + 2 * n + n_out]
        rest = refs[n_in + 2 * n + n_out:]
        scr, sems = rest[:n_scr], rest[n_scr:]
        first, last = None, None
        for d, g in enumerate(grid):
            f, l = pl.program_id(d) == 0, pl.program_id(d) == g - 1
            first, last = (f, l) if first is None else (first & f, last & l)

        @pl.when(first)
        def _():
            comm.start(c_in, c_out, sems)

        body(*ins, *outs, *scr)

        @pl.when(last)
        def _():
            comm.finish(c_in, c_out, sems)

    outs = pl.pallas_call(
        wrapped, grid=grid, in_specs=list(in_specs) + [ANY] * n, out_specs=list(out_specs) + [ANY] * n,
        out_shape=list(out_shape) + comm.out_shape, scratch_shapes=list(scratch) + comm.scratch,
        input_output_aliases={n_in + i: n_out + o for i, o in comm.aliases.items()},
        name=name, compiler_params=_params(*["arbitrary"] * len(grid)))(*args, *comm.arrs)
    return list(outs[:n_out]), list(outs[n_out:])


MM_VMEM_BUDGET = 44 * 2 ** 20


def _mm_tiles(M, N, k_bytes, mn_bytes):
    best = None
    for tm in (2048, 1024, 512, 256, 128):
        for tn in (1024, 512, 256, 128):
            if M % tm or N % tn:
                continue
            need = 2 * ((tm + tn) * k_bytes + tm * tn * mn_bytes) + 4 * tm * tn * 4
            if need <= MM_VMEM_BUDGET and (best is None or (tm * tn, tm) > (best[0] * best[1], best[0])):
                best = (tm, tn)
    assert best is not None, (M, N, k_bytes, mn_bytes)
    return best


def _mm(a, b, *, mode, name, extras=(), epi=None, out_dtypes=(F32,), comm=None):
    a_list = list(a) if isinstance(a, (list, tuple)) else [a]
    b_list = list(b) if isinstance(b, (list, tuple)) else [b]
    if mode == "nn":
        M, N = a_list[0].shape[0], b_list[0].shape[1]
    elif mode == "nt":
        M, N = a_list[0].shape[0], b_list[0].shape[0]
    else:
        M, N = a_list[0].shape[1], b_list[0].shape[1]
    k_bytes = sum((av.shape[0] if mode == "tn" else av.shape[1]) * av.dtype.itemsize for av in a_list)
    mn_bytes = sum(e.dtype.itemsize for e in extras) + sum(jnp.dtype(d).itemsize for d in out_dtypes)
    tm, tn = _mm_tiles(min(M, 2048), min(N, 1024), k_bytes, mn_bytes) if M % 128 == 0 and N % 128 == 0 else (M, N)
    assert M % tm == 0 and N % tn == 0
    a_specs, b_specs = [], []
    for av, bv in zip(a_list, b_list):
        K = av.shape[0] if mode == "tn" else av.shape[1]
        a_specs.append(pl.BlockSpec((K, tm), lambda i, j: (0, i)) if mode == "tn" else pl.BlockSpec((tm, K), lambda i, j: (i, 0)))
        b_specs.append(pl.BlockSpec((tn, K), lambda i, j: (j, 0)) if mode == "nt" else pl.BlockSpec((K, tn), lambda i, j: (0, j)))
    mn_spec = pl.BlockSpec((tm, tn), lambda i, j: (i, j))
    n_p, n_ex = len(a_list), len(extras)
    dims = _DIMS[mode]

    def body(*refs):
        acc = _dot(refs[0][...], refs[n_p][...], dims)
        for p in range(1, n_p):
            acc = acc + _dot(refs[p][...], refs[n_p + p][...], dims)
        rest = refs[2 * n_p:]
        res = (acc,) if epi is None else epi(acc, *[r[...] for r in rest[:n_ex]])
        for o_ref, r in zip(rest[n_ex:], res):
            o_ref[...] = r.astype(o_ref.dtype)

    outs, carried = _call(
        body, grid=(M // tm, N // tn), in_specs=a_specs + b_specs + [mn_spec] * n_ex,
        out_specs=[mn_spec] * len(out_dtypes), out_shape=[_sds((M, N), d) for d in out_dtypes],
        args=a_list + b_list + list(extras), name=name, sem=("parallel", "parallel"), comm=comm)
    res = outs[0] if len(outs) == 1 else outs
    return res if comm is None else (res, carried)


def _epi_add(acc, r):
    return (acc + r,)


def _epi_add2(acc, r):
    s = acc + r
    return (s, s)


def _epi_relu2(acc):
    p = jnp.maximum(acc, 0.0)
    return (p * p,)


def _epi_relu2_bwd(acc, r):
    return (acc * (2.0 * jnp.sqrt(r.astype(F32))),)


def _row(tr, n):
    return pl.BlockSpec((tr, n), lambda i: (i, 0))


def _vec(n):
    return pl.BlockSpec((1, n), lambda i: (0, 0))


def _rms_fwd(x, w, name, comm=None):
    T, D = x.shape
    tr = min(256, T)

    def body(x_ref, w_ref, o_ref):
        xv = x_ref[...]
        r = lax.rsqrt(jnp.mean(xv * xv, axis=-1, keepdims=True) + EPS)
        o_ref[...] = (xv * r * w_ref[...]).astype(BF16)

    outs, carried = _call(body, grid=(T // tr,), in_specs=[_row(tr, D), _vec(D)], out_specs=[_row(tr, D)],
                          out_shape=[_sds((T, D), BF16)], args=[x, w], name=name, sem=("parallel",), comm=comm)
    return outs[0] if comm is None else (outs[0], carried)


def _rms_bwd(x, w, dh, dres, name):
    T, D = x.shape
    tr = min(256, T)

    def body(x_ref, w_ref, dh_ref, dres_ref, dx_ref, dxb_ref, dw_ref):
        @pl.when(pl.program_id(0) == 0)
        def _():
            dw_ref[...] = jnp.zeros_like(dw_ref)

        xv = x_ref[...]
        r = lax.rsqrt(jnp.mean(xv * xv, axis=-1, keepdims=True) + EPS)
        xh = xv * r
        dh_v = dh_ref[...]
        dw_ref[...] += jnp.sum(dh_v * xh, axis=0, keepdims=True)
        dxh = dh_v * w_ref[...]
        dx = r * (dxh - xh * jnp.mean(dxh * xh, axis=-1, keepdims=True)) + dres_ref[...]
        dx_ref[...] = dx
        dxb_ref[...] = dx.astype(BF16)

    return pl.pallas_call(
        body, grid=(T // tr,), in_specs=[_row(tr, D), _vec(D), _row(tr, D), _row(tr, D)],
        out_specs=[_row(tr, D), _row(tr, D), _vec(D)],
        out_shape=[_sds((T, D), F32), _sds((T, D), BF16), _sds((1, D), F32)],
        name=name, compiler_params=_params("arbitrary"))(x, w, dh, dres)


def _final(x2, w, tgt, name):
    T, D = x2.shape
    tr = min(256, T)

    def body(x_ref, w_ref, t_ref, dx_ref, dxb_ref, dw_ref, loss_ref):
        @pl.when(pl.program_id(0) == 0)
        def _():
            dw_ref[...] = jnp.zeros_like(dw_ref)
            loss_ref[...] = jnp.zeros_like(loss_ref)

        xv = x_ref[...]
        wv = w_ref[...]
        r = lax.rsqrt(jnp.mean(xv * xv, axis=-1, keepdims=True) + EPS)
        xh = xv * r
        err = xh * wv - t_ref[...]
        part = jnp.sum(jnp.sum(err * err, axis=1, keepdims=True), axis=0, keepdims=True) * (0.5 / D)
        loss_ref[...] += jnp.broadcast_to(part, loss_ref.shape)
        dy = err * (1.0 / D)
        dw_ref[...] += jnp.sum(dy * xh, axis=0, keepdims=True)
        dxh = dy * wv
        dx = r * (dxh - xh * jnp.mean(dxh * xh, axis=-1, keepdims=True))
        dx_ref[...] = dx
        dxb_ref[...] = dx.astype(BF16)

    return pl.pallas_call(
        body, grid=(T // tr,), in_specs=[_row(tr, D), _vec(D), _row(tr, D)],
        out_specs=[_row(tr, D), _row(tr, D), _vec(D), _vec(LANES)],
        out_shape=[_sds((T, D), F32), _sds((T, D), BF16), _sds((1, D), F32), _sds((1, LANES), F32)],
        name=name, compiler_params=_params("arbitrary"))(x2, w, tgt)


def _silu_parts(z):
    s = jax.nn.sigmoid(z)
    return z * s, s * (1.0 + z * (1.0 - s))


def _gnorm_fwd(y, z, w, name, comm=None):
    T, N = y.shape
    tr = min(256, T)

    def body(y_ref, z_ref, w_ref, o_ref):
        for g in range(N // GROUP_W):
            sl = slice(g * GROUP_W, (g + 1) * GROUP_W)
            silu, _ = _silu_parts(z_ref[:, sl])
            yz = y_ref[:, sl] * silu
            r = lax.rsqrt(jnp.mean(yz * yz, axis=-1, keepdims=True) + EPS)
            o_ref[:, sl] = (yz * r * w_ref[:, sl]).astype(BF16)

    outs, carried = _call(body, grid=(T // tr,), in_specs=[_row(tr, N), _row(tr, N), _vec(N)], out_specs=[_row(tr, N)],
                          out_shape=[_sds((T, N), BF16)], args=[y, z, w], name=name, sem=("parallel",), comm=comm)
    return outs[0] if comm is None else (outs[0], carried)


def _gnorm_bwd(y, z, w, dyb, name):
    T, N = y.shape
    tr = min(256, T)

    def body(y_ref, z_ref, w_ref, d_ref, dy_ref, dz_ref, dw_ref):
        @pl.when(pl.program_id(0) == 0)
        def _():
            dw_ref[...] = jnp.zeros_like(dw_ref)

        for g in range(N // GROUP_W):
            sl = slice(g * GROUP_W, (g + 1) * GROUP_W)
            yv = y_ref[:, sl]
            silu, dsilu = _silu_parts(z_ref[:, sl])
            yz = yv * silu
            r = lax.rsqrt(jnp.mean(yz * yz, axis=-1, keepdims=True) + EPS)
            yzh = yz * r
            d = d_ref[:, sl]
            dw_ref[:, sl] += jnp.sum(d * yzh, axis=0, keepdims=True)
            dyzh = d * w_ref[:, sl]
            dyz = r * (dyzh - yzh * jnp.mean(dyzh * yzh, axis=-1, keepdims=True))
            dy_ref[:, sl] = dyz * silu
            dz_ref[:, sl] = (dyz * yv * dsilu).astype(BF16)

    return pl.pallas_call(
        body, grid=(T // tr,), in_specs=[_row(tr, N), _row(tr, N), _vec(N), _row(tr, N)],
        out_specs=[_row(tr, N), _row(tr, N), _vec(N)],
        out_shape=[_sds((T, N), F32), _sds((T, N), BF16), _sds((1, N), F32)],
        name=name, compiler_params=_params("arbitrary"))(y, z, w, dyb)


def _merge_fwd(gate_raw, b_gate, br_a, br_b, name):
    T, D = br_a.shape
    tr = min(256, T)

    def body(g_ref, bg_ref, a_ref, b_ref, o_ref):
        g = jax.nn.sigmoid(g_ref[...] + bg_ref[...])
        o_ref[...] = (g[:, :D] * a_ref[...] + g[:, D:] * b_ref[...]).astype(BF16)

    return pl.pallas_call(body, grid=(T // tr,), in_specs=[_row(tr, 2 * D), _vec(2 * D), _row(tr, D), _row(tr, D)],
                          out_specs=_row(tr, D), out_shape=_sds((T, D), BF16), name=name,
                          compiler_params=_params("parallel"))(gate_raw, b_gate, br_a, br_b)


def _merge_bwd(dmerged, gate_raw, b_gate, br_a, br_b, name):
    T, D = br_a.shape
    tr = min(256, T)

    def body(d_ref, g_ref, bg_ref, a_ref, b_ref, da_ref, db_ref, dg_ref, dbg_ref):
        @pl.when(pl.program_id(0) == 0)
        def _():
            dbg_ref[...] = jnp.zeros_like(dbg_ref)

        g = jax.nn.sigmoid(g_ref[...] + bg_ref[...])
        d = d_ref[...]
        da_ref[...] = (d * g[:, :D]).astype(BF16)
        db_ref[...] = (d * g[:, D:]).astype(BF16)
        dg = jnp.concatenate([d * a_ref[...], d * b_ref[...]], axis=1) * g * (1.0 - g)
        dg_ref[...] = dg.astype(BF16)
        dbg_ref[...] += jnp.sum(dg, axis=0, keepdims=True)

    return pl.pallas_call(
        body, grid=(T // tr,), in_specs=[_row(tr, D), _row(tr, 2 * D), _vec(2 * D), _row(tr, D), _row(tr, D)],
        out_specs=[_row(tr, D), _row(tr, D), _row(tr, 2 * D), _vec(2 * D)],
        out_shape=[_sds((T, D), BF16), _sds((T, D), BF16), _sds((T, 2 * D), BF16), _sds((1, 2 * D), F32)],
        name=name, compiler_params=_params("arbitrary"))(dmerged, gate_raw, b_gate, br_a, br_b)


CB_W = 256
CONV_ROWS = 32
CONV_PAD = 8


def _rows_down(load, r0, s):
    if s == 0:
        return load(r0, r0 + CONV_ROWS)
    if r0 == 0:
        row = lax.broadcasted_iota(jnp.int32, (CONV_ROWS, CB_W), 0)
        return jnp.where(row >= s, pltpu.roll(load(0, CONV_ROWS), s, 0), 0.0)
    return load(r0 - s, r0 - s + CONV_ROWS)


def _conv_tile(load, taps, r0):
    K = len(taps)
    us = [_rows_down(load, r0, K - 1 - k) for k in range(K)]
    acc = us[K - 1] * taps[K - 1]
    for k in range(K - 1):
        acc = acc + us[k] * taps[k]
    return acc, us


def _conv_back_tile(scr, taps, r0):
    K = len(taps)
    du = scr[r0:r0 + CONV_ROWS, :] * taps[K - 1]
    for k in range(K - 1):
        s = K - 1 - k
        du = du + scr[r0 + s:r0 + s + CONV_ROWS, :] * taps[k]
    return du


def _fold8(v):
    return jnp.sum(v.reshape(CONV_ROWS // 8, 8, v.shape[1]), axis=0)


def _col(T, j0=0):
    return pl.BlockSpec((T, CB_W), lambda j: (0, j + j0))


def _sc_fwd(psc, w, name):
    T, D = psc.shape[0], psc.shape[1] // 3
    nb = D // CB_W

    def body(b_ref, c_ref, x_ref, w_ref, o_ref):
        taps = [w_ref[k:k + 1, :] for k in range(SC_K)]
        load = lambda a, b: c_ref[a:b, :] * x_ref[a:b, :]
        for r0 in range(0, T, CONV_ROWS):
            cu, _ = _conv_tile(load, taps, r0)
            o_ref[r0:r0 + CONV_ROWS, :] = (b_ref[r0:r0 + CONV_ROWS, :] * cu).astype(BF16)

    return pl.pallas_call(
        body, grid=(nb,), in_specs=[_col(T), _col(T, nb), _col(T, 2 * nb), pl.BlockSpec((SC_K, CB_W), lambda j: (0, j))],
        out_specs=_col(T), out_shape=_sds((T, D), BF16), name=name, compiler_params=_params("parallel"))(psc, psc, psc, w)


def _sc_bwd(psc, w, dya, name):
    T, D = psc.shape[0], psc.shape[1] // 3
    nb = D // CB_W

    def body(b_ref, c_ref, x_ref, w_ref, d_ref, db_ref, dc_ref, dx_ref, dw_ref, scr):
        taps = [w_ref[k:k + 1, :] for k in range(SC_K)]
        load = lambda a, b: c_ref[a:b, :] * x_ref[a:b, :]
        scr[T:T + CONV_PAD, :] = jnp.zeros((CONV_PAD, CB_W), F32)
        dw8 = [jnp.zeros((8, CB_W), F32)] * SC_K
        for r0 in range(0, T, CONV_ROWS):
            rows = slice(r0, r0 + CONV_ROWS)
            cu, us = _conv_tile(load, taps, r0)
            d = d_ref[rows, :]
            db_ref[rows, :] = (d * cu).astype(BF16)
            dcu = d * b_ref[rows, :]
            scr[rows, :] = dcu
            dw8 = [acc + _fold8(dcu * u) for acc, u in zip(dw8, us)]
        for k in range(SC_K):
            dw_ref[k:k + 1, :] = jnp.sum(dw8[k], axis=0, keepdims=True)
        for r0 in range(0, T, CONV_ROWS):
            rows = slice(r0, r0 + CONV_ROWS)
            du = _conv_back_tile(scr, taps, r0)
            dc_ref[rows, :] = (du * x_ref[rows, :]).astype(BF16)
            dx_ref[rows, :] = (du * c_ref[rows, :]).astype(BF16)

    wspec = pl.BlockSpec((SC_K, CB_W), lambda j: (0, j))
    return pl.pallas_call(
        body, grid=(nb,), in_specs=[_col(T), _col(T, nb), _col(T, 2 * nb), wspec, _col(T)],
        out_specs=[_col(T), _col(T), _col(T), wspec],
        out_shape=[_sds((T, D), BF16)] * 3 + [_sds((SC_K, D), F32)],
        scratch_shapes=[pltpu.VMEM((T + CONV_PAD, CB_W), F32)],
        name=name, compiler_params=_params("parallel"))(psc, psc, psc, w, dya)


def _ssm_conv_fwd(u, w, b, name, comm=None):
    T, N = u.shape

    def body(u_ref, w_ref, b_ref, o_ref):
        taps = [w_ref[k:k + 1, :] for k in range(SSM_K)]
        bias = b_ref[...]
        for r0 in range(0, T, CONV_ROWS):
            c, _ = _conv_tile(lambda a, b: u_ref[a:b, :], taps, r0)
            c = c + bias
            o_ref[r0:r0 + CONV_ROWS, :] = c * jax.nn.sigmoid(c)

    outs, carried = _call(
        body, grid=(N // CB_W,), in_specs=[_col(T), pl.BlockSpec((SSM_K, CB_W), lambda j: (0, j)), pl.BlockSpec((1, CB_W), lambda j: (0, j))],
        out_specs=[_col(T)], out_shape=[_sds((T, N), F32)], args=[u, w, b], name=name, sem=("parallel",), comm=comm)
    return outs[0] if comm is None else (outs[0], carried)


def _ssm_conv_bwd(u, w, b, dxs, dB, dC, name, comm=None):
    T, N = u.shape
    n_x, n_b = dxs.shape[1] // CB_W, dB.shape[1] // CB_W

    def body(u_ref, w_ref, b_ref, dx_ref, db_ref, dc_ref, du_ref, dw_ref, dbias_ref, scr):
        j = pl.program_id(0)
        taps = [w_ref[k:k + 1, :] for k in range(SSM_K)]
        bias = b_ref[...]
        scr[T:T + CONV_PAD, :] = jnp.zeros((CONV_PAD, CB_W), F32)
        dw8 = [jnp.zeros((8, CB_W), F32)] * SSM_K
        db8 = jnp.zeros((8, CB_W), F32)
        for r0 in range(0, T, CONV_ROWS):
            rows = slice(r0, r0 + CONV_ROWS)
            c, us = _conv_tile(lambda a, b: u_ref[a:b, :], taps, r0)
            _, dsilu = _silu_parts(c + bias)
            d = jnp.where(j < n_x, dx_ref[rows, :], jnp.where(j < n_x + n_b, db_ref[rows, :], dc_ref[rows, :])) * dsilu
            scr[rows, :] = d
            db8 = db8 + _fold8(d)
            dw8 = [acc + _fold8(d * u) for acc, u in zip(dw8, us)]
        dbias_ref[...] = jnp.sum(db8, axis=0, keepdims=True)
        for k in range(SSM_K):
            dw_ref[k:k + 1, :] = jnp.sum(dw8[k], axis=0, keepdims=True)
        for r0 in range(0, T, CONV_ROWS):
            du_ref[r0:r0 + CONV_ROWS, :] = _conv_back_tile(scr, taps, r0).astype(BF16)

    wspec = pl.BlockSpec((SSM_K, CB_W), lambda j: (0, j))
    bspec = pl.BlockSpec((1, CB_W), lambda j: (0, j))
    outs, carried = _call(
        body, grid=(N // CB_W,),
        in_specs=[_col(T), wspec, bspec,
                  pl.BlockSpec((T, CB_W), lambda j: (0, jnp.minimum(j, n_x - 1))),
                  pl.BlockSpec((T, CB_W), lambda j: (0, jnp.clip(j - n_x, 0, n_b - 1))),
                  pl.BlockSpec((T, CB_W), lambda j: (0, jnp.clip(j - n_x - n_b, 0, n_b - 1)))],
        out_specs=[_col(T), wspec, bspec],
        out_shape=[_sds((T, N), BF16), _sds((SSM_K, N), F32), _sds((1, N), F32)],
        scratch=[pltpu.VMEM((T + CONV_PAD, CB_W), F32)],
        args=[u, w, b, dxs, dB, dC], name=name, sem=("parallel",), comm=comm)
    return outs if comm is None else (outs, carried)


def _split3(v):
    hi = v.astype(BF16)
    r = v - hi.astype(F32)
    mid = r.astype(BF16)
    lo = (r - mid.astype(F32)).astype(BF16)
    return hi, mid, lo


def _head_expand(n_lanes):
    h = lax.broadcasted_iota(jnp.int32, (LANES, n_lanes), 0)
    l = lax.broadcasted_iota(jnp.int32, (LANES, n_lanes), 1)
    return (jnp.right_shift(l, HEADDIM.bit_length() - 1) == h).astype(BF16)


def _softplus(v):
    return jnp.maximum(v, 0.0) + jnp.log1p(jnp.exp(-jnp.abs(v)))


def _ssd_prep(dt_raw, dt_bias, a_log, n_inner, name):
    T = dt_raw.shape[0]

    def body(r_ref, b_ref, al_ref, dt_ref, cs_ref):
        dt = _softplus(r_ref[...] + b_ref[...])
        a = dt * (-jnp.exp(al_ref[...]))
        i = lax.broadcasted_iota(jnp.int32, (CHUNK, CHUNK), 0)
        j = lax.broadcasted_iota(jnp.int32, (CHUNK, CHUNK), 1)
        tri = (j <= i).astype(BF16)
        cs = sum(_dot(tri, p) for p in _split3(a))
        ex = _head_expand(n_inner)
        dt_ref[...] = sum(_dot(p, ex) for p in _split3(dt))
        cs_ref[...] = sum(_dot(p, ex) for p in _split3(cs))

    blk = pl.BlockSpec((CHUNK, LANES), lambda c: (c, 0))
    out = pl.BlockSpec((CHUNK, n_inner), lambda c: (c, 0))
    return pl.pallas_call(body, grid=(T // CHUNK,), in_specs=[blk, _vec(LANES), _vec(LANES)], out_specs=[out, out],
                          out_shape=[_sds((T, n_inner), F32)] * 2, name=name, compiler_params=_params("parallel"))(dt_raw, dt_bias, a_log)


def _pair_terms(cs_p):
    lane = lax.broadcasted_iota(jnp.int32, (CHUNK, CHUNK), 1)
    sub = lax.broadcasted_iota(jnp.int32, (CHUNK, CHUNK), 0)
    csT = cs_p.T
    Ls = []
    for k in range(2):
        col = jnp.sum(jnp.where(lane == k * HEADDIM, cs_p, 0.0), axis=1, keepdims=True)
        rowv = csT[k * HEADDIM:k * HEADDIM + 1, :]
        Ls.append(jnp.exp(jnp.where(sub >= lane, col - rowv, -jnp.inf)))
    return Ls, jnp.exp(csT[:, CHUNK - 1:CHUNK])


def _block_diag(xp):
    lane = lax.broadcasted_iota(jnp.int32, xp.shape, 1)
    return jnp.concatenate([jnp.where(lane < HEADDIM, xp, 0.0), jnp.where(lane >= HEADDIM, xp, 0.0)], axis=0)


SSD_GROUPS_PER_STEP = 8


def _ssd_specs(T, n_inner):
    nc, gs = T // CHUNK, SSD_GROUPS_PER_STEP
    bo, co = n_inner // (gs * NSTATE), (n_inner + NGROUPS * NSTATE) // (gs * NSTATE)
    assert NGROUPS % gs == 0 and n_inner % (gs * NSTATE) == 0 and (NGROUPS * NSTATE) % (gs * NSTATE) == 0
    g_blk = lambda f: pl.BlockSpec((CHUNK, gs * GROUP_W), lambda c, s: (f(c), s))
    b_blk = lambda f: pl.BlockSpec((CHUNK, gs * NSTATE), lambda c, s: (f(c), bo + s))
    c_blk = lambda f: pl.BlockSpec((CHUNK, gs * NSTATE), lambda c, s: (f(c), co + s))
    return nc, g_blk, b_blk, c_blk


def _ssd_fwd(xbc, dt_e, cs_e, d_e, name, comm=None):
    T = xbc.shape[0]
    n_inner = dt_e.shape[1]
    nc, g_blk, b_blk, c_blk = _ssd_specs(T, n_inner)
    ident = lambda c: c

    gs = SSD_GROUPS_PER_STEP

    def body(xs_ref, b_ref, c_ref, dt_ref, cs_ref, d_ref, y_ref, p_ref, st):
        c, s = pl.program_id(0), pl.program_id(1)

        @pl.when(c == 0)
        def _():
            for gi in range(gs):
                st[s * gs + gi] = jnp.zeros((GROUP_W, NSTATE), F32)

        for gi in range(gs):
            g = s * gs + gi
            gw, gn = slice(gi * GROUP_W, (gi + 1) * GROUP_W), slice(gi * NSTATE, (gi + 1) * NSTATE)
            P = st[g]
            p_ref[0, gi] = P
            xs, dt, cs = xs_ref[:, gw], dt_ref[:, gw], cs_ref[:, gw]
            Bf, Cf = b_ref[:, gn], c_ref[:, gn]
            CBm = _dot3(Cf, Bf, NT)
            X = xs * dt
            decay = jnp.exp(cs[CHUNK - 1:CHUNK, :] - cs)
            y_off = _dot3(Cf, P, NT) * jnp.exp(cs)
            ys, ecl = [], []
            for pr in range(2):
                sl = slice(pr * LANES, (pr + 1) * LANES)
                Ls, e_last = _pair_terms(cs[:, sl])
                ecl.append(e_last)
                Mcat = jnp.concatenate([CBm * L for L in Ls], axis=1)
                ys.append(_dot3(Mcat, _block_diag(X[:, sl])))
            y_ref[:, gw] = jnp.concatenate(ys, axis=1) + y_off + xs * d_ref[:, gw]
            S = _dot3(X * decay, Bf, TN)
            st[g] = P * jnp.concatenate(ecl, axis=0) + S

    p_blk = pl.BlockSpec((1, gs, GROUP_W, NSTATE), lambda c, s: (c, s, 0, 0))
    outs, carried = _call(
        body, grid=(nc, NGROUPS // gs),
        in_specs=[g_blk(ident), b_blk(ident), c_blk(ident), g_blk(ident), g_blk(ident), pl.BlockSpec((1, gs * GROUP_W), lambda c, s: (0, s))],
        out_specs=[g_blk(ident), p_blk],
        out_shape=[_sds((T, n_inner), F32), _sds((nc, NGROUPS, GROUP_W, NSTATE), F32)],
        scratch=[pltpu.VMEM((NGROUPS, GROUP_W, NSTATE), F32)],
        args=[xbc, xbc, xbc, dt_e, cs_e, d_e], name=name, sem=("arbitrary", "arbitrary"), comm=comm)
    return outs if comm is None else (outs, carried)


def _ssd_bwd(xbc, dt_e, cs_e, d_e, states, dy, name, comm=None):
    T = xbc.shape[0]
    n_inner = dt_e.shape[1]
    nc, g_blk, b_blk, c_blk = _ssd_specs(T, n_inner)
    rev = lambda c: nc - 1 - c

    gs = SSD_GROUPS_PER_STEP

    def body(xs_ref, b_ref, c_ref, dt_ref, cs_ref, d_ref, p_ref, pn_ref, dy_ref,
             dxs_ref, db_ref, dc_ref, ddt_ref, dcs_ref, dd_ref, dst):
        cc, s = pl.program_id(0), pl.program_id(1)

        @pl.when(cc == 0)
        def _():
            for gi in range(gs):
                dst[s * gs + gi] = jnp.zeros((GROUP_W, NSTATE), F32)

        for gi in range(gs):
            one_group(s * gs + gi, gi, xs_ref, b_ref, c_ref, dt_ref, cs_ref, d_ref, p_ref, pn_ref, dy_ref,
                      dxs_ref, db_ref, dc_ref, ddt_ref, dcs_ref, dd_ref, dst)

    def one_group(g, gi, xs_ref, b_ref, c_ref, dt_ref, cs_ref, d_ref, p_ref, pn_ref, dy_ref,
                  dxs_ref, db_ref, dc_ref, ddt_ref, dcs_ref, dd_ref, dst):
        gw, gn = slice(gi * GROUP_W, (gi + 1) * GROUP_W), slice(gi * NSTATE, (gi + 1) * NSTATE)
        dS = dst[g]
        P, Pn = p_ref[0, gi], pn_ref[0, gi]
        xs, dt, cs, dY = xs_ref[:, gw], dt_ref[:, gw], cs_ref[:, gw], dy_ref[:, gw]
        Bf, Cf = b_ref[:, gn], c_ref[:, gn]
        Bb, Cb = Bf.astype(BF16), Cf.astype(BF16)
        X = xs * dt
        ecs = jnp.exp(cs)
        decay = jnp.exp(cs[CHUNK - 1:CHUNK, :] - cs)
        CBm = _dot3(Cf, Bf, NT)
        dYe = dY * ecs
        dP_off = _dot3(dYe, Cf, TN)
        dC = _dot(dYe.astype(BF16), P.astype(BF16))
        dcs = dYe * _dot3(Cf, P, NT)
        Xd = X * decay
        dB = _dot(Xd.astype(BF16), dS.astype(BF16))
        E = _dot3(Bf, dS, NT)
        dX = E * decay
        dcs = dcs - E * Xd
        R = _dot3(jnp.ones((8, NSTATE), F32), dS * Pn, NT)
        sub_g = lax.broadcasted_iota(jnp.int32, (CHUNK, GROUP_W), 0)
        dcs = dcs + jnp.where(sub_g == CHUNK - 1, R[0:1, :], 0.0)
        lane = lax.broadcasted_iota(jnp.int32, (CHUNK, CHUNK), 1)
        sub = lax.broadcasted_iota(jnp.int32, (CHUNK, CHUNK), 0)
        dCB = jnp.zeros((CHUNK, CHUNK), F32)
        dXs, dcss, ecl = [], [], []
        for pr in range(2):
            sl = slice(pr * LANES, (pr + 1) * LANES)
            Ls, e_last = _pair_terms(cs[:, sl])
            ecl.append(e_last)
            dYp = dY[:, sl]
            dMcat = _dot3(dYp, _block_diag(X[:, sl]), NT)
            Mcat = jnp.concatenate([CBm * L for L in Ls], axis=1)
            dXt = _dot3(Mcat, dYp, TN)
            dXs.append(jnp.where(lane < HEADDIM, dXt[:CHUNK], dXt[CHUNK:]))
            colacc = jnp.zeros((CHUNK, CHUNK), F32)
            rowacc = jnp.zeros((CHUNK, CHUNK), F32)
            for k in range(2):
                dG = dMcat[:, k * CHUNK:(k + 1) * CHUNK] * Ls[k]
                dCB = dCB + dG
                Q = dG * CBm
                colacc = colacc + jnp.where(lane == k * HEADDIM, jnp.sum(Q, axis=1, keepdims=True), 0.0)
                rowacc = rowacc + jnp.where(sub == k * HEADDIM, jnp.sum(Q, axis=0, keepdims=True), 0.0)
            dcss.append(colacc - rowacc.T)
        dX = dX + jnp.concatenate(dXs, axis=1)
        dcs = dcs + jnp.concatenate(dcss, axis=1)
        dCBb = dCB.astype(BF16)
        dc_ref[:, gn] = dC + _dot(dCBb, Bb)
        db_ref[:, gn] = dB + _dot(dCBb, Cb, TN)
        dxs_ref[:, gw] = dX * dt + dY * d_ref[:, gw]
        ddt_ref[:, gw] = dX * xs
        dcs_ref[:, gw] = dcs
        dd_ref[0, :, gw] = jnp.sum(dY * xs, axis=0, keepdims=True)
        dst[g] = dS * jnp.concatenate(ecl, axis=0) + dP_off

    p_blk = pl.BlockSpec((1, gs, GROUP_W, NSTATE), lambda c, s: (nc - 1 - c, s, 0, 0))
    pn_blk = pl.BlockSpec((1, gs, GROUP_W, NSTATE), lambda c, s: (jnp.minimum(nc - c, nc - 1), s, 0, 0))
    st_blk = pl.BlockSpec((CHUNK, gs * NSTATE), lambda c, s: (nc - 1 - c, s))
    outs, carried = _call(
        body, grid=(nc, NGROUPS // gs),
        in_specs=[g_blk(rev), b_blk(rev), c_blk(rev), g_blk(rev), g_blk(rev), pl.BlockSpec((1, gs * GROUP_W), lambda c, s: (0, s)),
                  p_blk, pn_blk, g_blk(rev)],
        out_specs=[g_blk(rev), st_blk, st_blk, g_blk(rev), g_blk(rev), pl.BlockSpec((1, 1, gs * GROUP_W), lambda c, s: (nc - 1 - c, 0, s))],
        out_shape=[_sds((T, n_inner), F32), _sds((T, NGROUPS * NSTATE), F32), _sds((T, NGROUPS * NSTATE), F32),
                   _sds((T, n_inner), F32), _sds((T, n_inner), F32), _sds((nc, 1, n_inner), F32)],
        scratch=[pltpu.VMEM((NGROUPS, GROUP_W, NSTATE), F32)],
        args=[xbc, xbc, xbc, dt_e, cs_e, d_e, states, states, dy], name=name, sem=("arbitrary", "arbitrary"), comm=comm)
    return outs if comm is None else (outs, carried)


def _ssd_post(ddt_e, dcs_e, dd_p, dt_raw, dt_bias, a_log, n_heads, name):
    T, n_inner = ddt_e.shape

    def body(ddt_ref, dcs_ref, dd_ref, r_ref, b_ref, al_ref, draw_ref, dbias_ref, dal_ref, ddsk_ref):
        @pl.when(pl.program_id(0) == 0)
        def _():
            dbias_ref[...] = jnp.zeros_like(dbias_ref)
            dal_ref[...] = jnp.zeros_like(dal_ref)
            ddsk_ref[...] = jnp.zeros_like(ddsk_ref)

        ex = _head_expand(n_inner)
        red = lambda v: sum(_dot(p, ex, NT) for p in _split3(v))
        raw = r_ref[...] + b_ref[...]
        dt = _softplus(raw)
        A = -jnp.exp(al_ref[...])
        i = lax.broadcasted_iota(jnp.int32, (CHUNK, CHUNK), 0)
        j = lax.broadcasted_iota(jnp.int32, (CHUNK, CHUNK), 1)
        upper = (j >= i).astype(BF16)
        da = sum(_dot(upper, p) for p in _split3(red(dcs_ref[...])))
        ddt = red(ddt_ref[...]) + da * A
        lane = lax.broadcasted_iota(jnp.int32, (CHUNK, LANES), 1)
        draw = jnp.where(lane < n_heads, ddt * jax.nn.sigmoid(raw), 0.0)
        draw_ref[...] = draw.astype(BF16)
        dbias_ref[...] += jnp.sum(draw, axis=0, keepdims=True)
        dal_ref[...] += jnp.sum(da * dt, axis=0, keepdims=True) * A
        ddsk_ref[...] += red(jnp.broadcast_to(dd_ref[0], (8, n_inner)))[0:1, :]

    wide = pl.BlockSpec((CHUNK, n_inner), lambda c: (c, 0))
    blk = pl.BlockSpec((CHUNK, LANES), lambda c: (c, 0))
    return pl.pallas_call(
        body, grid=(T // CHUNK,),
        in_specs=[wide, wide, pl.BlockSpec((1, 1, n_inner), lambda c: (c, 0, 0)), blk, _vec(LANES), _vec(LANES)],
        out_specs=[blk, _vec(LANES), _vec(LANES), _vec(LANES)],
        out_shape=[_sds((T, LANES), BF16)] + [_sds((1, LANES), F32)] * 3,
        name=name, compiler_params=_params("arbitrary"))(ddt_e, dcs_e, dd_p, dt_raw, dt_bias, a_log)


def _row2(v):
    return v.reshape(1, -1).astype(F32)


def _pad_lanes(v):
    return jnp.pad(_row2(v), ((0, 0), (0, LANES - v.shape[-1])))


class _NoExchange:
    def __init__(self, W):
        self.W, self.grads = W, {}

    def weight(self, k):
        return self.W[k]

    def carry(self, name):
        return None

    def carried(self, name, outs):
        pass

    def grad(self, k, g):
        self.grads[k] = g

    def tok(self):
        return jnp.zeros((), F32)

    def point(self, name, value):
        pass


def _local_step(x, tgt, S, small):
    T, D = x.shape

    def mm(a, b, *, name, **kw):
        comm = S.carry(name)
        if comm is None:
            return _mm(a, b, name=name, **kw)
        res, outs = _mm(a, b, name=name, comm=comm, **kw)
        S.carried(name, outs)
        return res

    def carrying(fn, *args, name):
        comm = S.carry(name)
        if comm is None:
            return fn(*args, name)
        res, outs = fn(*args, name, comm=comm)
        S.carried(name, outs)
        return res

    n_inner = 2 * D
    n_heads = n_inner // HEADDIM
    norm_mix, norm_mlp, norm_final = _row2(small["norm_mix"]), _row2(small["norm_mlp"]), _row2(small["norm_final"])
    b_gate, ssm_b, ssm_norm_w = _row2(small["b_gate"]), _row2(small["ssm_conv_b"]), _row2(small["ssm_norm_w"])
    dt_bias, a_log = _pad_lanes(small["dt_bias"]), _pad_lanes(small["A_log"])
    d_e = jnp.repeat(small["D_skip"].astype(F32), HEADDIM).reshape(1, n_inner)

    hb = carrying(_rms_fwd, x, norm_mix, name="rms_mix")
    sc_w, ssm_w = S.weight("sc_conv_w"), S.weight("ssm_conv_w")
    p_xbc = mm(hb, S.weight("xbc"), mode="nn", name="proj_xbc")
    p_dt = mm(hb, S.weight("dt"), mode="nn", name="proj_dt")
    p_z = mm(hb, S.weight("z"), mode="nn", name="proj_z")
    p_sc = mm(hb, S.weight("sc"), mode="nn", name="proj_sc")
    p_gate = mm(hb, S.weight("gate"), mode="nn", name="proj_gate")
    xbc = carrying(_ssm_conv_fwd, p_xbc, ssm_w, ssm_b, name="ssm_conv_fwd")
    dt_e, cs_e = _ssd_prep(p_dt, dt_bias, a_log, n_inner, "ssd_prep")
    y, states = carrying(_ssd_fwd, xbc, dt_e, cs_e, d_e, name="ssd_fwd")
    S.point("ssd_fwd_done", y)
    yb = carrying(_gnorm_fwd, y, p_z, ssm_norm_w, name="gnorm_fwd")
    ya = _sc_fwd(p_sc, sc_w, "sc_fwd")
    br_a = mm(ya, S.weight("bsc"), mode="nn", name="branch_sc")
    br_b = mm(yb, S.weight("bssm"), mode="nn", name="branch_ssm")
    merged = _merge_fwd(p_gate, b_gate, br_a, br_b, "merge_fwd")
    x1 = mm(merged, S.weight("out"), mode="nn", name="out_proj", extras=(x,), epi=_epi_add)
    h2 = _rms_fwd(x1, norm_mlp, "rms_mlp")
    r_act = mm(h2, S.weight("w1"), mode="nn", name="mlp_up", epi=_epi_relu2, out_dtypes=(BF16,))
    x2 = mm(r_act, S.weight("w2"), mode="nn", name="mlp_down", extras=(x1,), epi=_epi_add)
    dx2, dx2b, g_norm_final, loss_row = _final(x2, norm_final, tgt, "final")

    S.grad("w2", mm(r_act, dx2b, mode="tn", name="mlp_down_dw", out_dtypes=(BF16,)))
    da = mm(dx2b, S.weight("w2"), mode="nt", name="mlp_down_dx", extras=(r_act,), epi=_epi_relu2_bwd, out_dtypes=(BF16,))
    S.grad("w1", mm(h2, da, mode="tn", name="mlp_up_dw", out_dtypes=(BF16,)))
    dh2 = mm(da, S.weight("w1"), mode="nt", name="mlp_up_dx")
    dx1, dx1b, g_norm_mlp = _rms_bwd(x1, norm_mlp + S.tok(), dh2, dx2, "rms_mlp_bwd")
    S.grad("out", mm(merged, dx1b, mode="tn", name="out_proj_dw", out_dtypes=(BF16,)))
    dmerged = mm(dx1b, S.weight("out"), mode="nt", name="out_proj_dx")
    dbr_a, dbr_b, d_gate, g_b_gate = _merge_bwd(dmerged, p_gate, b_gate, br_a, br_b, "merge_bwd")
    S.grad("bssm", mm(yb, dbr_b, mode="tn", name="branch_ssm_dw", out_dtypes=(BF16,)))
    S.grad("bsc", mm(ya, dbr_a, mode="tn", name="branch_sc_dw", out_dtypes=(BF16,)))
    dyb = mm(dbr_b, S.weight("bssm"), mode="nt", name="branch_ssm_dx")
    dya = mm(dbr_a, S.weight("bsc"), mode="nt", name="branch_sc_dx")
    dy, d_z, g_ssm_norm_w = _gnorm_bwd(y, p_z, ssm_norm_w + S.tok(), dyb, "gnorm_bwd")
    dxs, dB, dC, ddt_e, dcs_e, dd_p = carrying(_ssd_bwd, xbc, dt_e, cs_e, d_e, states, dy, name="ssd_bwd")
    d_dt, g_dt_bias, g_a_log, g_d_skip = _ssd_post(ddt_e, dcs_e, dd_p, p_dt, dt_bias, a_log, n_heads, "ssd_post")
    d_xbc, g_ssm_w, g_ssm_b = carrying(_ssm_conv_bwd, p_xbc, ssm_w, ssm_b, dxs, dB, dC, name="ssm_conv_bwd")
    d_scB, d_scC, d_scX, g_sc_w = _sc_bwd(p_sc, sc_w, dya, "sc_bwd")
    d_sc = jnp.concatenate([d_scB, d_scC, d_scX], axis=1)
    pieces = [("sc", d_sc), ("z", d_z), ("xbc", d_xbc), ("dt", d_dt), ("gate", d_gate)]
    S.grad("win", {k: mm(hb, d, mode="tn", name="proj_dw_" + k, out_dtypes=(BF16,)) for k, d in pieces})
    pieces = [(k, d + S.tok().astype(d.dtype) if k == "dt" else d) for k, d in pieces]
    dh = mm([d for _, d in pieces], [S.weight(k) for k, _ in pieces], mode="nt", name="proj_dx")
    grad_x, _, g_norm_mix = _rms_bwd(x, norm_mix, dh, dx1, "rms_mix_bwd")

    g_small = dict(norm_mix=g_norm_mix, b_gate=g_b_gate, sc_conv_w=g_sc_w, ssm_conv_w=g_ssm_w, ssm_conv_b=g_ssm_b,
                   dt_bias=g_dt_bias, A_log=g_a_log, D_skip=g_d_skip, ssm_norm_w=g_ssm_norm_w, norm_mlp=g_norm_mlp,
                   norm_final=g_norm_final, loss=loss_row)
    return grad_x, g_small


class _Place:
    def __init__(self, k=0):
        x, y, c = lax.axis_index("x"), lax.axis_index("y"), lax.axis_index("c")
        self.x = 1 - x if k & 4 else x
        self.y = 1 - y if k & 2 else y
        self.c = 1 - c if k & 1 else c
        self.chip = 2 * self.x + self.y
        self.id = 2 * self.chip + self.c


ICI_PEERS = (2, 4, 6)
SIBLING = (1,)
ALL_PEERS = (1, 2, 3, 4, 5, 6, 7)


class _Comm:
    def __init__(self, arrs, out_shape, ks, src, dst, own=None, aliases=None):
        self.arrs, self.out_shape, self.ks = list(arrs), list(out_shape), tuple(ks)
        self.n = len(self.arrs)
        self.src, self.dst, self.own = src, dst, own
        self.aliases = aliases or {}
        dma = pltpu.SemaphoreType.DMA
        self.scratch = [dma((self.n, len(self.ks))), dma((self.n, len(self.ks))), dma((self.n,))]

    def _copies(self, ins, outs, sems, with_recvs):
        send_sems, recv_sems, local_sems = sems
        me = _Place()
        owns, sends, recvs = [], [], []
        for a in range(self.n):
            if self.own is not None:
                s, d = self.own(a, ins[a], outs[a], me)
                owns.append(pltpu.make_async_copy(s, d, local_sems.at[a]))
            for i, k in enumerate(self.ks):
                peer = _Place(k)
                for sender, lst in ((me, sends), (peer, recvs)) if with_recvs else ((me, sends),):
                    lst.append(pltpu.make_async_remote_copy(
                        src_ref=self.src(a, ins[a], me, peer), dst_ref=self.dst(a, outs[a], sender),
                        send_sem=send_sems.at[a, i], recv_sem=recv_sems.at[a, i],
                        device_id=(peer.x, peer.y, peer.c), device_id_type=MESH))
        return owns, sends, recvs

    def start(self, ins, outs, sems):
        owns, sends, _ = self._copies(ins, outs, sems, False)
        for cp in owns + sends:
            cp.start()

    def finish(self, ins, outs, sems):
        owns, sends, recvs = self._copies(ins, outs, sems, True)
        for cp in recvs:
            cp.wait_recv()
        for cp in sends:
            cp.wait_send()
        for cp in owns:
            cp.wait()


class _GatherBoth:
    def __init__(self, shards):
        self.arrs, self.n, self.aliases = list(shards), len(shards), {}
        self.out_shape = [_sds((4, 2) + s.shape, s.dtype) for s in shards]
        dma = pltpu.SemaphoreType.DMA
        self.scratch = [dma((self.n, 7)), dma((self.n, 7)), dma((self.n,))]

    def _copy(self, a, j, src, slot, to, outs, sems):
        return pltpu.make_async_remote_copy(src_ref=src, dst_ref=outs[a].at[slot.chip, slot.c], send_sem=sems[0].at[a, j],
                                            recv_sem=sems[1].at[a, j], device_id=(to.x, to.y, to.c), device_id_type=MESH)

    def start(self, ins, outs, sems):
        me, sib = _Place(), _Place(1)
        for a in range(self.n):
            pltpu.make_async_copy(ins[a], outs[a].at[me.chip, me.c], sems[2].at[a]).start()
            self._copy(a, 0, ins[a], me, sib, outs, sems).start()
            for i, k in enumerate(ICI_PEERS):
                self._copy(a, 1 + i, ins[a], me, _Place(k), outs, sems).start()

    def finish(self, ins, outs, sems):
        me, sib = _Place(), _Place(1)
        passed = []
        for i, k in enumerate(ICI_PEERS):
            peer = _Place(k)
            for a in range(self.n):
                self._copy(a, 1 + i, ins[a], peer, peer, outs, sems).wait_recv()
                cp = self._copy(a, 4 + i, outs[a].at[peer.chip, peer.c], peer, sib, outs, sems)
                cp.start()
                passed.append(cp)
        for a in range(self.n):
            self._copy(a, 0, ins[a], sib, sib, outs, sems).wait_recv()
            for i, k in enumerate(ICI_PEERS):
                far = _Place(k | 1)
                self._copy(a, 4 + i, outs[a].at[far.chip, far.c], far, sib, outs, sems).wait_recv()
        for a in range(self.n):
            self._copy(a, 0, ins[a], me, sib, outs, sems).wait_send()
            for i, k in enumerate(ICI_PEERS):
                self._copy(a, 1 + i, ins[a], me, _Place(k), outs, sems).wait_send()
            pltpu.make_async_copy(ins[a], outs[a].at[me.chip, me.c], sems[2].at[a]).wait()
        for cp in passed:
            cp.wait_send()


def _run_comm(comm, name, after=()):
    n, n_after = comm.n, len(after)

    def body(*refs):
        ins, outs, sems = refs[:n], refs[n + n_after:2 * n + n_after], refs[2 * n + n_after:]
        comm.start(ins, outs, sems)
        comm.finish(ins, outs, sems)

    return list(pl.pallas_call(body, in_specs=[ANY] * (n + n_after), out_specs=[ANY] * n, out_shape=comm.out_shape,
                               scratch_shapes=comm.scratch, input_output_aliases=dict(comm.aliases), name=name)(*comm.arrs, *after))


def _gather_ici(shards):
    return _Comm(shards, [_sds((4, 2) + s.shape, s.dtype) for s in shards], ICI_PEERS,
                 src=lambda a, i, me, p: i, dst=lambda a, o, s: o.at[s.chip, s.c], own=lambda a, i, o, me: (i, o.at[me.chip, me.c]))


def _gather_sibling(bufs):
    return _Comm(bufs, [_sds(b.shape, b.dtype) for b in bufs], SIBLING,
                 src=lambda a, i, me, p: i.at[:, me.c], dst=lambda a, o, s: o.at[:, s.c], aliases={a: a for a in range(len(bufs))})


def _scatter_sibling(parts):
    return _Comm(parts, [_sds((4,) + p.shape[2:], p.dtype) for p in parts], SIBLING,
                 src=lambda a, i, me, p: i.at[:, p.c], dst=lambda a, o, s: o)


def _scatter_ici(parts):
    return _Comm(parts, [_sds(p.shape, p.dtype) for p in parts], ICI_PEERS,
                 src=lambda a, i, me, p: i.at[p.chip], dst=lambda a, o, s: o.at[s.chip], own=lambda a, i, o, me: (i.at[me.chip], o.at[me.chip]))


HBM_SPEC = pl.BlockSpec(memory_space=pltpu.HBM)
SEM_SPEC = pl.BlockSpec(memory_space=pltpu.SEMAPHORE)
DATAFLOW = pltpu.SideEffectType.DATAFLOW_SIDE_EFFECTING


def _own_part(parts, name):
    n, R, C = parts.shape
    tr = R if R <= 256 else 256
    chip = (2 * lax.axis_index("x") + lax.axis_index("y")).astype(jnp.int32).reshape(1)

    def body(q_ref, p_ref, o_ref):
        o_ref[...] = p_ref[...]

    blk = pl.BlockSpec((1, tr, C), lambda i, q_ref: (q_ref[0], i, 0))
    spec = pltpu.PrefetchScalarGridSpec(num_scalar_prefetch=1, grid=(R // tr,), in_specs=[blk], out_specs=blk)
    return pl.pallas_call(body, grid_spec=spec, out_shape=_sds((n, R, C), parts.dtype), name=name,
                          compiler_params=_params("parallel"))(chip, parts)


def _ici_copy(gather, a, srcs, lands, send_sems, recv_sems, i, me, peer, sender):
    src = lands[a].at[me.chip, me.c] if gather else srcs[a].at[peer.chip]
    dst = lands[a].at[sender.chip, sender.c] if gather else lands[a].at[sender.chip]
    j = a * len(ICI_PEERS) + i
    return pltpu.make_async_remote_copy(src_ref=src, dst_ref=dst, send_sem=send_sems.at[j], recv_sem=recv_sems.at[j],
                                        device_id=(peer.x, peer.y, peer.c), device_id_type=MESH)


def _ici_start(srcs, lands, gather, name):
    n, n_s = len(lands), len(srcs)
    bufs = list(srcs) + list(lands)

    def body(*refs):
        src_refs, land_refs = refs[:n_s], refs[n_s:n_s + n]
        send_sems, recv_sems = refs[n_s + n], refs[n_s + n + 1]
        token = refs[-1]
        me = _Place()
        for a in range(n):
            for i, k in enumerate(ICI_PEERS):
                _ici_copy(gather, a, src_refs, land_refs, send_sems, recv_sems, i, me, _Place(k), me).start()
        token[...] = jnp.zeros_like(token)

    dma = pltpu.SemaphoreType.DMA((n * len(ICI_PEERS),))
    outs = pl.pallas_call(
        body, name=name, out_shape=(dma, dma, *[pltpu.HBM(v.shape, v.dtype) for v in bufs], _sds((8, LANES), F32)),
        in_specs=(HBM_SPEC,) * len(bufs),
        out_specs=(SEM_SPEC, SEM_SPEC) + (HBM_SPEC,) * len(bufs) + (pl.BlockSpec(memory_space=pltpu.VMEM),),
        input_output_aliases={j: 2 + j for j in range(len(bufs))}, compiler_params=pltpu.CompilerParams(has_side_effects=DATAFLOW),
    )(*[pltpu.with_memory_space_constraint(v, pltpu.HBM) for v in bufs])
    return outs[0], outs[1], list(outs[2:2 + n_s]), list(outs[2 + n_s:2 + n_s + n]), outs[-1]


def _ici_wait(flight, after, gather, name):
    send_sems, recv_sems, srcs, lands, _ = flight
    n, n_s = len(lands), len(srcs)
    bufs = srcs + lands

    def body(*refs):
        src_refs, land_refs = refs[:n_s], refs[n_s:n_s + n]
        s_sems, r_sems = refs[n_s + n], refs[n_s + n + 1]
        me = _Place()
        for a in range(n):
            for i, k in enumerate(ICI_PEERS):
                peer = _Place(k)
                cp = _ici_copy(gather, a, src_refs, land_refs, s_sems, r_sems, i, me, peer, peer)
                cp.wait_send()
                cp.wait_recv()

    outs = pl.pallas_call(
        body, name=name, out_shape=tuple(pltpu.HBM(v.shape, v.dtype) for v in bufs),
        in_specs=(HBM_SPEC,) * len(bufs) + (SEM_SPEC, SEM_SPEC) + (ANY,) * len(after), out_specs=(HBM_SPEC,) * len(bufs),
        input_output_aliases={j: j for j in range(len(bufs))}, compiler_params=pltpu.CompilerParams(has_side_effects=DATAFLOW),
    )(*bufs, send_sems, recv_sems, *after)
    return list(outs[n_s:])


def _own_shard(shard, after, name):
    R, C = shard.shape
    tr = R if R <= 256 else 256
    place = jnp.stack([2 * lax.axis_index("x") + lax.axis_index("y"), lax.axis_index("c")]).astype(jnp.int32)

    def body(q_ref, s_ref, after_ref, o_ref):
        o_ref[0, 0] = s_ref[...].astype(o_ref.dtype)

    spec = pltpu.PrefetchScalarGridSpec(
        num_scalar_prefetch=1, grid=(R // tr,), in_specs=[pl.BlockSpec((tr, C), lambda i, q_ref: (i, 0)), ANY],
        out_specs=pl.BlockSpec((1, 1, tr, C), lambda i, q_ref: (q_ref[0], q_ref[1], i, 0)))
    return pl.pallas_call(body, grid_spec=spec, out_shape=_sds((4, 2, R, C), BF16), name=name,
                          compiler_params=_params("parallel"))(place, shard, after)


def _col_pieces(widths):
    out, c = [], 0
    for k, w in widths:
        out.append((k, c, w))
        c += w
    return out


def _split_range(c0, n, bounds):
    parts, c = [], c0
    while c < c0 + n:
        r = max(i for i in range(len(bounds) - 1) if bounds[i] <= c)
        w = min(c0 + n, bounds[r + 1]) - c
        parts.append((r, c - bounds[r], w))
        c += w
    return parts


def _win_unpack(g, widths, name):
    n, R, C = g.shape
    tr = min(256, R)
    pieces = _col_pieces(widths)
    padded = [-(-w // LANES) * LANES for _, _, w in pieces]
    shard_bounds = [s * C for s in range(n + 1)]

    def body(g_ref, *o_refs):
        for (k, c0, w), o_ref in zip(pieces, o_refs):
            for t in range(0, o_ref.shape[1], LANES):
                valid = max(0, min(LANES, w - t))
                cols = [g_ref[s, :, o:o + ww] for s, o, ww in _split_range(c0 + t, valid, shard_bounds)] if valid else []
                if valid < LANES:
                    cols.append(jnp.zeros((tr, LANES - valid), g_ref.dtype))
                o_ref[:, t:t + LANES] = cols[0] if len(cols) == 1 else jnp.concatenate(cols, axis=1)

    return pl.pallas_call(
        body, grid=(R // tr,), in_specs=[pl.BlockSpec((n, tr, C), lambda i: (0, i, 0))],
        out_specs=[pl.BlockSpec((tr, p), lambda i: (i, 0)) for p in padded],
        out_shape=[_sds((R, p), g.dtype) for p in padded], name=name, compiler_params=_params("parallel"))(g)


def _win_pack(grads, widths, n, name):
    R = grads[0].shape[0]
    tr = min(256, R)
    pieces = _col_pieces(widths)
    total = pieces[-1][1] + pieces[-1][2]
    C = total // n
    bounds = [c0 for _, c0, _ in pieces] + [total]

    def body(*refs):
        g_refs, o_ref = refs[:-1], refs[-1]
        for s in range(n):
            for t in range(0, C, LANES):
                w = min(LANES, C - t)
                cols = [g_refs[r][:, o:o + ww] for r, o, ww in _split_range(s * C + t, w, bounds)]
                o_ref[s, :, t:t + w] = cols[0] if len(cols) == 1 else jnp.concatenate(cols, axis=1)

    return pl.pallas_call(
        body, grid=(R // tr,), in_specs=[pl.BlockSpec((tr, gr.shape[1]), lambda i: (i, 0)) for gr in grads],
        out_specs=pl.BlockSpec((n, tr, C), lambda i: (0, i, 0)), out_shape=_sds((n, R, C), grads[0].dtype),
        name=name, compiler_params=_params("parallel"))(*grads)


def _gather_all(arrs):
    return _Comm(arrs, [_sds((N_DEV,) + a.shape, a.dtype) for a in arrs], ALL_PEERS,
                 src=lambda a, i, me, p: i, dst=lambda a, o, s: o.at[s.id], own=lambda a, i, o, me: (i, o.at[me.id]))


def _add_halves(parts, got, name):
    n, _, R, C = parts.shape
    tr = R if R <= 256 else 256
    assert R % tr == 0
    core = lax.axis_index("c").astype(jnp.int32).reshape(1)

    def body(c_ref, p_ref, g_ref, o_ref):
        o_ref[0] = (p_ref[0, 0].astype(F32) + g_ref[0].astype(F32)).astype(o_ref.dtype)

    spec = pltpu.PrefetchScalarGridSpec(
        num_scalar_prefetch=1, grid=(n, R // tr),
        in_specs=[pl.BlockSpec((1, 1, tr, C), lambda q, i, c_ref: (q, c_ref[0], i, 0)), pl.BlockSpec((1, tr, C), lambda q, i, c_ref: (q, i, 0))],
        out_specs=pl.BlockSpec((1, tr, C), lambda q, i, c_ref: (q, i, 0)))
    return pl.pallas_call(body, grid_spec=spec, out_shape=_sds((n, R, C), parts.dtype), name=name,
                          compiler_params=_params("parallel", "parallel"))(core, parts, got)


def _adam(w, m, v, gparts, name, comm=None):
    R, C = w.shape
    n = gparts.shape[0]
    tr = R if R <= 256 else 128
    assert R % tr == 0
    c1 = 1.0 / (1.0 - ADAM_B1 ** ADAM_STEP)
    c2 = 1.0 / (1.0 - ADAM_B2 ** ADAM_STEP)

    def body(w_ref, m_ref, v_ref, g_ref, go_ref, d_ref, mo_ref, vo_ref):
        g = g_ref[0].astype(F32)
        for s in range(1, n):
            g = g + g_ref[s].astype(F32)
        mn = ADAM_B1 * m_ref[...] + (1.0 - ADAM_B1) * g
        vn = ADAM_B2 * v_ref[...] + (1.0 - ADAM_B2) * (g * g)
        go_ref[...] = g
        mo_ref[...] = mn
        vo_ref[...] = vn
        d_ref[...] = -ADAM_LR * ((mn * c1) / (jnp.sqrt(vn * c2) + ADAM_EPS) + ADAM_WD * w_ref[...])

    blk = pl.BlockSpec((tr, C), lambda i: (i, 0))
    outs, carried = _call(
        body, grid=(R // tr,), in_specs=[blk, blk, blk, pl.BlockSpec((n, tr, C), lambda i: (0, i, 0))],
        out_specs=[blk] * 4, out_shape=[_sds((R, C), F32)] * 4, args=[w, m, v, gparts], name=name, sem=("parallel",), comm=comm)
    return outs if comm is None else (outs, carried)


_SMALL_ORDER = ("norm_mix", "b_gate", "sc_conv_w", "ssm_conv_w", "ssm_conv_b", "dt_bias", "A_log", "D_skip", "ssm_norm_w",
                "norm_mlp", "norm_final", "loss")
_REPLICATED = ("norm_mix", "b_gate", "ssm_conv_b", "dt_bias", "A_log", "D_skip", "ssm_norm_w", "norm_mlp", "norm_final")


def _cols_to_slots(g, n):
    R = g.shape[0]
    return jnp.transpose(g.reshape(R, n, g.shape[1] // n), (1, 0, 2))


def _slots_to_cols(g):
    n, R, C = g.shape
    return jnp.transpose(g, (1, 0, 2)).reshape(R, n * C)


def kernel(x, norm_mix, w_in, b_gate, sc_conv_w, ssm_conv_w, ssm_conv_b, dt_bias, A_log, D_skip, ssm_norm_w, w_branch_sc, w_branch_ssm, w_out, norm_mlp, w_mlp1, w_mlp2, norm_final, loss_target, m_norm_mix, m_w_in, m_b_gate, m_sc_conv_w, m_ssm_conv_w, m_ssm_conv_b, m_dt_bias, m_A_log, m_D_skip, m_ssm_norm_w, m_w_branch_sc, m_w_branch_ssm, m_w_out, m_norm_mlp, m_w_mlp1, m_w_mlp2, m_norm_final, v_norm_mix, v_w_in, v_b_gate, v_sc_conv_w, v_ssm_conv_w, v_ssm_conv_b, v_dt_bias, v_A_log, v_D_skip, v_ssm_norm_w, v_w_branch_sc, v_w_branch_ssm, v_w_out, v_norm_mlp, v_w_mlp1, v_w_mlp2, v_norm_final):
    T, D = x.shape[1], x.shape[2]
    n_inner = 2 * D
    n_heads = n_inner // HEADDIM
    n_xbc = n_inner + 2 * NGROUPS * NSTATE
    me = 4 * lax.axis_index("x") + 2 * lax.axis_index("y") + lax.axis_index("c")

    in_cols = [("sc", 3 * D), ("z", n_inner), ("xbc", n_xbc), ("dt", n_heads), ("gate", 2 * D)]
    by_owner = lambda b: b.reshape((N_DEV,) + b.shape[2:])
    to_owner = lambda g: g.reshape((4, 2) + g.shape[1:])
    rows_of = lambda g: to_owner(g.reshape((N_DEV, g.shape[0] // N_DEV) + g.shape[1:]))
    cols_of = lambda g: to_owner(_cols_to_slots(g, N_DEV))

    class Schedule(_NoExchange):
        late = ("bssm", "bsc", "out", "w1", "w2")
        gather_sib = dict(gnorm_fwd=("bsc", "bssm", "out"), branch_ssm=("w1", "w2"))
        scatter_sib = dict(mlp_up_dx=("w2", "w1"), branch_ssm_dx=("out", "bssm", "bsc"))
        shards = dict(bsc=w_branch_sc, bssm=w_branch_ssm, out=w_out, w1=w_mlp1, w2=w_mlp2)

        def __init__(self):
            self.W, self.staged, self.grads, self.summed, self.scatters = {}, {}, {}, {}, []
            self.token = jnp.zeros((), F32)

        def first_weights(self, bufs):
            self.W.update(zip([k for k, _ in in_cols], _win_unpack(by_owner(bufs[0]), in_cols, "win_unpack")))
            self.W.update(sc_conv_w=_slots_to_cols(by_owner(bufs[1])), ssm_conv_w=_slots_to_cols(by_owner(bufs[2])))
            lands = [_own_shard(self.shards[k], bufs[1], "own_shard_" + k) for k in self.late]
            self.gather_flight = _ici_start([], lands, True, "gather_late_start")
            self.token = self.gather_flight[4][0, 0]
            self.W["dt"] = self.W["dt"] + self.token.astype(BF16)

        def tok(self):
            return self.token

        def point(self, name, value):
            if name == "ssd_fwd_done":
                lands = _ici_wait(self.gather_flight, [value], True, "gather_late_wait")
                self.staged.update(zip(self.late, lands))

        def carry(self, name):
            if name == "rms_mix":
                return _GatherBoth([w_in.astype(BF16), sc_conv_w, ssm_conv_w])
            if name in self.gather_sib:
                return _gather_sibling([self.staged.pop(k) for k in self.gather_sib[name]])
            if name in self.scatter_sib:
                return _scatter_sibling([self.grads[k] for k in self.scatter_sib[name]])
            return None

        def start_scatter(self, keys, halves):
            lands = [_own_part(h, "own_part_" + k) for k, h in zip(keys, halves)]
            flight = _ici_start(halves, lands, False, "scatter_%s_start" % keys[0])
            self.scatters.append((keys, flight))
            self.token = flight[4][0, 0]

        def carried(self, name, outs):
            if name == "rms_mix":
                self.first_weights(outs)
            elif name in self.gather_sib:
                for k, b in zip(self.gather_sib[name], outs):
                    full = by_owner(b)
                    self.W[k] = _slots_to_cols(full) if k == "w1" else full.reshape(-1, D)
            else:
                keys = self.scatter_sib[name]
                self.start_scatter(keys, [_add_halves(self.grads[k], b, "add_halves_" + k) for k, b in zip(keys, outs)])

        def grad(self, k, g):
            if k == "win":
                g = to_owner(_win_pack([g[k] for k, _ in in_cols], in_cols, N_DEV, "win_pack"))
                got = _run_comm(_scatter_sibling([g]), "scatter_sibling_win")[0]
                self.start_scatter(("win",), [_add_halves(g, got, "add_halves_win")])
            else:
                self.grads[k] = cols_of(g) if k == "w1" else rows_of(g)

        def finish_scatter(self, after):
            keys, flight = self.scatters.pop(0)
            return dict(zip(keys, _ici_wait(flight, after, False, "scatter_%s_wait" % keys[0])))

    S = Schedule()
    small = dict(norm_mix=norm_mix, b_gate=b_gate, ssm_conv_b=ssm_conv_b, dt_bias=dt_bias, A_log=A_log, D_skip=D_skip,
                 ssm_norm_w=ssm_norm_w, norm_mlp=norm_mlp, norm_final=norm_final)
    grad_x, g_small = _local_step(x.reshape(T, D), loss_target.reshape(T, D), S, small)

    small_flat = jnp.concatenate([g_small[k].reshape(-1) for k in _SMALL_ORDER])
    n_small = small_flat.shape[0]
    rows = -(-n_small // (8 * LANES)) * 8
    small_pack = jnp.pad(small_flat, (0, rows * LANES - n_small)).reshape(rows, LANES)

    res = {}
    big = [("w_in", "win", w_in, m_w_in, v_w_in), ("w_branch_sc", "bsc", w_branch_sc, m_w_branch_sc, v_w_branch_sc),
           ("w_branch_ssm", "bssm", w_branch_ssm, m_w_branch_ssm, v_w_branch_ssm), ("w_out", "out", w_out, m_w_out, v_w_out),
           ("w_mlp1", "w1", w_mlp1, m_w_mlp1, v_w_mlp1), ("w_mlp2", "w2", w_mlp2, m_w_mlp2, v_w_mlp2)]
    by_grad = {gk: (k, w, m, v) for k, gk, w, m, v in big}
    after = [grad_x]
    while S.scatters:
        for gk, parts in S.finish_scatter(after).items():
            k, w, m, v = by_grad[gk]
            if gk == "win":
                res[k], (small_parts,) = _adam(w, m, v, parts, "adam_" + k, comm=_gather_all([small_pack]))
            else:
                res[k] = _adam(w, m, v, parts, "adam_" + k)
            after = after + [res[k][1]]

    sizes = {k: g_small[k].size for k in _SMALL_ORDER}
    offs, o = {}, 0
    for k in _SMALL_ORDER:
        offs[k] = o
        o += sizes[k]
    rep_w = dict(norm_mix=norm_mix, b_gate=b_gate, ssm_conv_b=ssm_conv_b, dt_bias=dt_bias, A_log=A_log, D_skip=D_skip,
                 ssm_norm_w=ssm_norm_w, norm_mlp=norm_mlp, norm_final=norm_final)
    rep_m = dict(norm_mix=m_norm_mix, b_gate=m_b_gate, ssm_conv_b=m_ssm_conv_b, dt_bias=m_dt_bias, A_log=m_A_log, D_skip=m_D_skip,
                 ssm_norm_w=m_ssm_norm_w, norm_mlp=m_norm_mlp, norm_final=m_norm_final)
    rep_v = dict(norm_mix=v_norm_mix, b_gate=v_b_gate, ssm_conv_b=v_ssm_conv_b, dt_bias=v_dt_bias, A_log=v_A_log, D_skip=v_D_skip,
                 ssm_norm_w=v_ssm_norm_w, norm_mlp=v_norm_mlp, norm_final=v_norm_final)

    def pack(d):
        segs = [jnp.pad(d[k].astype(F32).reshape(-1), (0, sizes[k] - d[k].size)) if k in d else jnp.zeros((sizes[k],), F32)
                for k in _SMALL_ORDER]
        return jnp.pad(jnp.concatenate(segs), (0, rows * LANES - n_small)).reshape(rows, LANES)

    sm = _adam(pack(rep_w), pack(rep_m), pack(rep_v), small_parts, "adam_small")
    sm = [s.reshape(-1) for s in sm]
    for k in _REPLICATED:
        n_k = rep_w[k].shape[0]
        res[k] = tuple(s[offs[k]:offs[k] + n_k] for s in sm)
    loss = sm[0][offs["loss"]]
    for k, w, m, v, K, full in (("sc_conv_w", sc_conv_w, m_sc_conv_w, v_sc_conv_w, SC_K, D),
                                ("ssm_conv_w", ssm_conv_w, m_ssm_conv_w, v_ssm_conv_w, SSM_K, n_xbc)):
        g_full = sm[0][offs[k]:offs[k] + K * full].reshape(K, full)
        cw = full // N_DEV
        g_mine = lax.dynamic_slice_in_dim(g_full, me * cw, cw, axis=1)
        res[k] = _adam(w, m, v, g_mine[None], "adam_" + k)

    order = ("norm_mix", "w_in", "b_gate", "sc_conv_w", "ssm_conv_w", "ssm_conv_b", "dt_bias", "A_log", "D_skip", "ssm_norm_w",
             "w_branch_sc", "w_branch_ssm", "w_out", "norm_mlp", "w_mlp1", "w_mlp2", "norm_final")
    outs = [loss, grad_x.reshape(1, T, D)]
    for j in range(4):
        outs += [res[k][j] for k in order]
    return tuple(outs)
```

```python
import functools

import jax
import jax.numpy as jnp
from jax import lax
from jax.experimental import pallas as pl
from jax.experimental.pallas import tpu as pltpu

F32 = jnp.float32
BF16 = jnp.bfloat16

EPS = 1e-6
N_DEV = 8
HEADDIM = 64
NSTATE = 128
CHUNK = 128
NGROUPS = 8
GROUP_W = 256
SC_K = 3
SSM_K = 4
LANES = 128

ADAM_LR = 0.001
ADAM_B1 = 0.9
ADAM_B2 = 0.999
ADAM_EPS = 1e-08
ADAM_WD = 0.01
ADAM_STEP = 10

NN = (((1,), (0,)), ((), ()))
NT = (((1,), (1,)), ((), ()))
TN = (((0,), (0,)), ((), ()))
_DIMS = {"nn": NN, "nt": NT, "tn": TN}

ANY = pl.BlockSpec(memory_space=pl.ANY)
MESH = pl.DeviceIdType.MESH


def _sds(shape, dtype):
    return jax.ShapeDtypeStruct(tuple(shape), dtype)


def _dot(a, b, dims=NN):
    return lax.dot_general(a, b, dims, preferred_element_type=F32)


def _dot3(a, b, dims=NN):
    return lax.dot_general(a, b, dims, preferred_element_type=F32, precision=lax.Precision.HIGH)


def _params(*sem):
    return pltpu.CompilerParams(dimension_semantics=tuple(sem))


def _call(body, *, grid, in_specs, out_specs, out_shape, args, name, sem, scratch=(), comm=None):
    if comm is None:
        outs = pl.pallas_call(body, grid=grid, in_specs=list(in_specs), out_specs=list(out_specs), out_shape=list(out_shape),
                              scratch_shapes=list(scratch), name=name, compiler_params=_params(*sem))(*args)
        return list(outs), None
    n, n_in, n_out, n_scr = comm.n, len(in_specs), len(out_shape), len(scratch)

    def wrapped(*refs):
        ins, c_in = refs[:n_in], refs[n_in:n_in + n]
        outs, c_out = refs[n_in + n:n_in + n + n_out], refs[n_in + n + n_out:n_in + 2 * n + n_out]
        rest = refs[n_in + 2 * n + n_out:]
        scr, sems = rest[:n_scr], rest[n_scr:]
        first, last = None, None
        for d, g in enumerate(grid):
            f, l = pl.program_id(d) == 0, pl.program_id(d) == g - 1
            first, last = (f, l) if first is None else (first & f, last & l)

        @pl.when(first)
        def _():
            comm.start(c_in, c_out, sems)

        body(*ins, *outs, *scr)

        @pl.when(last)
        def _():
            comm.finish(c_in, c_out, sems)

    outs = pl.pallas_call(
        wrapped, grid=grid, in_specs=list(in_specs) + [ANY] * n, out_specs=list(out_specs) + [ANY] * n,
        out_shape=list(out_shape) + comm.out_shape, scratch_shapes=list(scratch) + comm.scratch,
        input_output_aliases={n_in + i: n_out + o for i, o in comm.aliases.items()},
        name=name, compiler_params=_params(*["arbitrary"] * len(grid)))(*args, *comm.arrs)
    return list(outs[:n_out]), list(outs[n_out:])


MM_VMEM_BUDGET = 44 * 2 ** 20


def _mm_tiles(M, N, k_bytes, mn_bytes):
    best = None
    for tm in (2048, 1024, 512, 256, 128):
        for tn in (1024, 512, 256, 128):
            if M % tm or N % tn:
                continue
            need = 2 * ((tm + tn) * k_bytes + tm * tn * mn_bytes) + 4 * tm * tn * 4
            if need <= MM_VMEM_BUDGET and (best is None or (tm * tn, tm) > (best[0] * best[1], best[0])):
                best = (tm, tn)
    assert best is not None, (M, N, k_bytes, mn_bytes)
    return best


def _mm(a, b, *, mode, name, extras=(), epi=None, out_dtypes=(F32,), comm=None):
    a_list = list(a) if isinstance(a, (list, tuple)) else [a]
    b_list = list(b) if isinstance(b, (list, tuple)) else [b]
    if mode == "nn":
        M, N = a_list[0].shape[0], b_list[0].shape[1]
    elif mode == "nt":
        M, N = a_list[0].shape[0], b_list[0].shape[0]
    else:
        M, N = a_list[0].shape[1], b_list[0].shape[1]
    k_bytes = sum((av.shape[0] if mode == "tn" else av.shape[1]) * av.dtype.itemsize for av in a_list)
    mn_bytes = sum(e.dtype.itemsize for e in extras) + sum(jnp.dtype(d).itemsize for d in out_dtypes)
    tm, tn = _mm_tiles(min(M, 2048), min(N, 1024), k_bytes, mn_bytes) if M % 128 == 0 and N % 128 == 0 else (M, N)
    assert M % tm == 0 and N % tn == 0
    a_specs, b_specs = [], []
    for av, bv in zip(a_list, b_list):
        K = av.shape[0] if mode == "tn" else av.shape[1]
        a_specs.append(pl.BlockSpec((K, tm), lambda i, j: (0, i)) if mode == "tn" else pl.BlockSpec((tm, K), lambda i, j: (i, 0)))
        b_specs.append(pl.BlockSpec((tn, K), lambda i, j: (j, 0)) if mode == "nt" else pl.BlockSpec((K, tn), lambda i, j: (0, j)))
    mn_spec = pl.BlockSpec((tm, tn), lambda i, j: (i, j))
    n_p, n_ex = len(a_list), len(extras)
    dims = _DIMS[mode]

    def body(*refs):
        acc = _dot(refs[0][...], refs[n_p][...], dims)
        for p in range(1, n_p):
            acc = acc + _dot(refs[p][...], refs[n_p + p][...], dims)
        rest = refs[2 * n_p:]
        res = (acc,) if epi is None else epi(acc, *[r[...] for r in rest[:n_ex]])
        for o_ref, r in zip(rest[n_ex:], res):
            o_ref[...] = r.astype(o_ref.dtype)

    outs, carried = _call(
        body, grid=(M // tm, N // tn), in_specs=a_specs + b_specs + [mn_spec] * n_ex,
        out_specs=[mn_spec] * len(out_dtypes), out_shape=[_sds((M, N), d) for d in out_dtypes],
        args=a_list + b_list + list(extras), name=name, sem=("parallel", "parallel"), comm=comm)
    res = outs[0] if len(outs) == 1 else outs
    return res if comm is None else (res, carried)


def _epi_add(acc, r):
    return (acc + r,)


def _epi_add2(acc, r):
    s = acc + r
    return (s, s)


def _epi_relu2(acc):
    p = jnp.maximum(acc, 0.0)
    return (p * p,)


def _epi_relu2_bwd(acc, r):
    return (acc * (2.0 * jnp.sqrt(r.astype(F32))),)


def _row(tr, n):
    return pl.BlockSpec((tr, n), lambda i: (i, 0))


def _vec(n):
    return pl.BlockSpec((1, n), lambda i: (0, 0))


def _rms_fwd(x, w, name, comm=None):
    T, D = x.shape
    tr = min(256, T)

    def body(x_ref, w_ref, o_ref):
        xv = x_ref[...]
        r = lax.rsqrt(jnp.mean(xv * xv, axis=-1, keepdims=True) + EPS)
        o_ref[...] = (xv * r * w_ref[...]).astype(BF16)

    outs, carried = _call(body, grid=(T // tr,), in_specs=[_row(tr, D), _vec(D)], out_specs=[_row(tr, D)],
                          out_shape=[_sds((T, D), BF16)], args=[x, w], name=name, sem=("parallel",), comm=comm)
    return outs[0] if comm is None else (outs[0], carried)


def _rms_bwd(x, w, dh, dres, name):
    T, D = x.shape
    tr = min(256, T)

    def body(x_ref, w_ref, dh_ref, dres_ref, dx_ref, dxb_ref, dw_ref):
        @pl.when(pl.program_id(0) == 0)
        def _():
            dw_ref[...] = jnp.zeros_like(dw_ref)

        xv = x_ref[...]
        r = lax.rsqrt(jnp.mean(xv * xv, axis=-1, keepdims=True) + EPS)
        xh = xv * r
        dh_v = dh_ref[...]
        dw_ref[...] += jnp.sum(dh_v * xh, axis=0, keepdims=True)
        dxh = dh_v * w_ref[...]
        dx = r * (dxh - xh * jnp.mean(dxh * xh, axis=-1, keepdims=True)) + dres_ref[...]
        dx_ref[...] = dx
        dxb_ref[...] = dx.astype(BF16)

    return pl.pallas_call(
        body, grid=(T // tr,), in_specs=[_row(tr, D), _vec(D), _row(tr, D), _row(tr, D)],
        out_specs=[_row(tr, D), _row(tr, D), _vec(D)],
        out_shape=[_sds((T, D), F32), _sds((T, D), BF16), _sds((1, D), F32)],
        name=name, compiler_params=_params("arbitrary"))(x, w, dh, dres)


def _final(x2, w, tgt, name):
    T, D = x2.shape
    tr = min(256, T)

    def body(x_ref, w_ref, t_ref, dx_ref, dxb_ref, dw_ref, loss_ref):
        @pl.when(pl.program_id(0) == 0)
        def _():
            dw_ref[...] = jnp.zeros_like(dw_ref)
            loss_ref[...] = jnp.zeros_like(loss_ref)

        xv = x_ref[...]
        wv = w_ref[...]
        r = lax.rsqrt(jnp.mean(xv * xv, axis=-1, keepdims=True) + EPS)
        xh = xv * r
        err = xh * wv - t_ref[...]
        part = jnp.sum(jnp.sum(err * err, axis=1, keepdims=True), axis=0, keepdims=True) * (0.5 / D)
        loss_ref[...] += jnp.broadcast_to(part, loss_ref.shape)
        dy = err * (1.0 / D)
        dw_ref[...] += jnp.sum(dy * xh, axis=0, keepdims=True)
        dxh = dy * wv
        dx = r * (dxh - xh * jnp.mean(dxh * xh, axis=-1, keepdims=True))
        dx_ref[...] = dx
        dxb_ref[...] = dx.astype(BF16)

    return pl.pallas_call(
        body, grid=(T // tr,), in_specs=[_row(tr, D), _vec(D), _row(tr, D)],
        out_specs=[_row(tr, D), _row(tr, D), _vec(D), _vec(LANES)],
        out_shape=[_sds((T, D), F32), _sds((T, D), BF16), _sds((1, D), F32), _sds((1, LANES), F32)],
        name=name, compiler_params=_params("arbitrary"))(x2, w, tgt)


def _silu_parts(z):
    s = jax.nn.sigmoid(z)
    return z * s, s * (1.0 + z * (1.0 - s))


def _gnorm_fwd(y, z, w, name, comm=None):
    T, N = y.shape
    tr = min(256, T)

    def body(y_ref, z_ref, w_ref, o_ref):
        for g in range(N // GROUP_W):
            sl = slice(g * GROUP_W, (g + 1) * GROUP_W)
            silu, _ = _silu_parts(z_ref[:, sl])
            yz = y_ref[:, sl] * silu
            r = lax.rsqrt(jnp.mean(yz * yz, axis=-1, keepdims=True) + EPS)
            o_ref[:, sl] = (yz * r * w_ref[:, sl]).astype(BF16)

    outs, carried = _call(body, grid=(T // tr,), in_specs=[_row(tr, N), _row(tr, N), _vec(N)], out_specs=[_row(tr, N)],
                          out_shape=[_sds((T, N), BF16)], args=[y, z, w], name=name, sem=("parallel",), comm=comm)
    return outs[0] if comm is None else (outs[0], carried)


def _gnorm_bwd(y, z, w, dyb, name):
    T, N = y.shape
    tr = min(256, T)

    def body(y_ref, z_ref, w_ref, d_ref, dy_ref, dz_ref, dw_ref):
        @pl.when(pl.program_id(0) == 0)
        def _():
            dw_ref[...] = jnp.zeros_like(dw_ref)

        for g in range(N // GROUP_W):
            sl = slice(g * GROUP_W, (g + 1) * GROUP_W)
            yv = y_ref[:, sl]
            silu, dsilu = _silu_parts(z_ref[:, sl])
            yz = yv * silu
            r = lax.rsqrt(jnp.mean(yz * yz, axis=-1, keepdims=True) + EPS)
            yzh = yz * r
            d = d_ref[:, sl]
            dw_ref[:, sl] += jnp.sum(d * yzh, axis=0, keepdims=True)
            dyzh = d * w_ref[:, sl]
            dyz = r * (dyzh - yzh * jnp.mean(dyzh * yzh, axis=-1, keepdims=True))
            dy_ref[:, sl] = dyz * silu
            dz_ref[:, sl] = (dyz * yv * dsilu).astype(BF16)

    return pl.pallas_call(
        body, grid=(T // tr,), in_specs=[_row(tr, N), _row(tr, N), _vec(N), _row(tr, N)],
        out_specs=[_row(tr, N), _row(tr, N), _vec(N)],
        out_shape=[_sds((T, N), F32), _sds((T, N), BF16), _sds((1, N), F32)],
        name=name, compiler_params=_params("arbitrary"))(y, z, w, dyb)


def _merge_fwd(gate_raw, b_gate, br_a, br_b, name):
    T, D = br_a.shape
    tr = min(256, T)

    def body(g_ref, bg_ref, a_ref, b_ref, o_ref):
        g = jax.nn.sigmoid(g_ref[...] + bg_ref[...])
        o_ref[...] = (g[:, :D] * a_ref[...] + g[:, D:] * b_ref[...]).astype(BF16)

    return pl.pallas_call(body, grid=(T // tr,), in_specs=[_row(tr, 2 * D), _vec(2 * D), _row(tr, D), _row(tr, D)],
                          out_specs=_row(tr, D), out_shape=_sds((T, D), BF16), name=name,
                          compiler_params=_params("parallel"))(gate_raw, b_gate, br_a, br_b)


def _merge_bwd(dmerged, gate_raw, b_gate, br_a, br_b, name):
    T, D = br_a.shape
    tr = min(256, T)

    def body(d_ref, g_ref, bg_ref, a_ref, b_ref, da_ref, db_ref, dg_ref, dbg_ref):
        @pl.when(pl.program_id(0) == 0)
        def _():
            dbg_ref[...] = jnp.zeros_like(dbg_ref)

        g = jax.nn.sigmoid(g_ref[...] + bg_ref[...])
        d = d_ref[...]
        da_ref[...] = (d * g[:, :D]).astype(BF16)
        db_ref[...] = (d * g[:, D:]).astype(BF16)
        dg = jnp.concatenate([d * a_ref[...], d * b_ref[...]], axis=1) * g * (1.0 - g)
        dg_ref[...] = dg.astype(BF16)
        dbg_ref[...] += jnp.sum(dg, axis=0, keepdims=True)

    return pl.pallas_call(
        body, grid=(T // tr,), in_specs=[_row(tr, D), _row(tr, 2 * D), _vec(2 * D), _row(tr, D), _row(tr, D)],
        out_specs=[_row(tr, D), _row(tr, D), _row(tr, 2 * D), _vec(2 * D)],
        out_shape=[_sds((T, D), BF16), _sds((T, D), BF16), _sds((T, 2 * D), BF16), _sds((1, 2 * D), F32)],
        name=name, compiler_params=_params("arbitrary"))(dmerged, gate_raw, b_gate, br_a, br_b)


CB_W = 256
CONV_ROWS = 32
CONV_PAD = 8


def _rows_down(load, r0, s):
    if s == 0:
        return load(r0, r0 + CONV_ROWS)
    if r0 == 0:
        row = lax.broadcasted_iota(jnp.int32, (CONV_ROWS, CB_W), 0)
        return jnp.where(row >= s, pltpu.roll(load(0, CONV_ROWS), s, 0), 0.0)
    return load(r0 - s, r0 - s + CONV_ROWS)


def _conv_tile(load, taps, r0):
    K = len(taps)
    us = [_rows_down(load, r0, K - 1 - k) for k in range(K)]
    acc = us[K - 1] * taps[K - 1]
    for k in range(K - 1):
        acc = acc + us[k] * taps[k]
    return acc, us


def _conv_back_tile(scr, taps, r0):
    K = len(taps)
    du = scr[r0:r0 + CONV_ROWS, :] * taps[K - 1]
    for k in range(K - 1):
        s = K - 1 - k
        du = du + scr[r0 + s:r0 + s + CONV_ROWS, :] * taps[k]
    return du


def _fold8(v):
    return jnp.sum(v.reshape(CONV_ROWS // 8, 8, v.shape[1]), axis=0)


def _col(T, j0=0):
    return pl.BlockSpec((T, CB_W), lambda j: (0, j + j0))


def _sc_fwd(psc, w, name):
    T, D = psc.shape[0], psc.shape[1] // 3
    nb = D // CB_W

    def body(b_ref, c_ref, x_ref, w_ref, o_ref):
        taps = [w_ref[k:k + 1, :] for k in range(SC_K)]
        load = lambda a, b: c_ref[a:b, :] * x_ref[a:b, :]
        for r0 in range(0, T, CONV_ROWS):
            cu, _ = _conv_tile(load, taps, r0)
            o_ref[r0:r0 + CONV_ROWS, :] = (b_ref[r0:r0 + CONV_ROWS, :] * cu).astype(BF16)

    return pl.pallas_call(
        body, grid=(nb,), in_specs=[_col(T), _col(T, nb), _col(T, 2 * nb), pl.BlockSpec((SC_K, CB_W), lambda j: (0, j))],
        out_specs=_col(T), out_shape=_sds((T, D), BF16), name=name, compiler_params=_params("parallel"))(psc, psc, psc, w)


def _sc_bwd(psc, w, dya, name):
    T, D = psc.shape[0], psc.shape[1] // 3
    nb = D // CB_W

    def body(b_ref, c_ref, x_ref, w_ref, d_ref, db_ref, dc_ref, dx_ref, dw_ref, scr):
        taps = [w_ref[k:k + 1, :] for k in range(SC_K)]
        load = lambda a, b: c_ref[a:b, :] * x_ref[a:b, :]
        scr[T:T + CONV_PAD, :] = jnp.zeros((CONV_PAD, CB_W), F32)
        dw8 = [jnp.zeros((8, CB_W), F32)] * SC_K
        for r0 in range(0, T, CONV_ROWS):
            rows = slice(r0, r0 + CONV_ROWS)
            cu, us = _conv_tile(load, taps, r0)
            d = d_ref[rows, :]
            db_ref[rows, :] = (d * cu).astype(BF16)
            dcu = d * b_ref[rows, :]
            scr[rows, :] = dcu
            dw8 = [acc + _fold8(dcu * u) for acc, u in zip(dw8, us)]
        for k in range(SC_K):
            dw_ref[k:k + 1, :] = jnp.sum(dw8[k], axis=0, keepdims=True)
        for r0 in range(0, T, CONV_ROWS):
            rows = slice(r0, r0 + CONV_ROWS)
            du = _conv_back_tile(scr, taps, r0)
            dc_ref[rows, :] = (du * x_ref[rows, :]).astype(BF16)
            dx_ref[rows, :] = (du * c_ref[rows, :]).astype(BF16)

    wspec = pl.BlockSpec((SC_K, CB_W), lambda j: (0, j))
    return pl.pallas_call(
        body, grid=(nb,), in_specs=[_col(T), _col(T, nb), _col(T, 2 * nb), wspec, _col(T)],
        out_specs=[_col(T), _col(T), _col(T), wspec],
        out_shape=[_sds((T, D), BF16)] * 3 + [_sds((SC_K, D), F32)],
        scratch_shapes=[pltpu.VMEM((T + CONV_PAD, CB_W), F32)],
        name=name, compiler_params=_params("parallel"))(psc, psc, psc, w, dya)


def _ssm_conv_fwd(u, w, b, name, comm=None):
    T, N = u.shape

    def body(u_ref, w_ref, b_ref, o_ref):
        taps = [w_ref[k:k + 1, :] for k in range(SSM_K)]
        bias = b_ref[...]
        for r0 in range(0, T, CONV_ROWS):
            c, _ = _conv_tile(lambda a, b: u_ref[a:b, :], taps, r0)
            c = c + bias
            o_ref[r0:r0 + CONV_ROWS, :] = c * jax.nn.sigmoid(c)

    outs, carried = _call(
        body, grid=(N // CB_W,), in_specs=[_col(T), pl.BlockSpec((SSM_K, CB_W), lambda j: (0, j)), pl.BlockSpec((1, CB_W), lambda j: (0, j))],
        out_specs=[_col(T)], out_shape=[_sds((T, N), F32)], args=[u, w, b], name=name, sem=("parallel",), comm=comm)
    return outs[0] if comm is None else (outs[0], carried)


def _ssm_conv_bwd(u, w, b, dxs, dB, dC, name, comm=None):
    T, N = u.shape
    n_x, n_b = dxs.shape[1] // CB_W, dB.shape[1] // CB_W

    def body(u_ref, w_ref, b_ref, dx_ref, db_ref, dc_ref, du_ref, dw_ref, dbias_ref, scr):
        j = pl.program_id(0)
        taps = [w_ref[k:k + 1, :] for k in range(SSM_K)]
        bias = b_ref[...]
        scr[T:T + CONV_PAD, :] = jnp.zeros((CONV_PAD, CB_W), F32)
        dw8 = [jnp.zeros((8, CB_W), F32)] * SSM_K
        db8 = jnp.zeros((8, CB_W), F32)
        for r0 in range(0, T, CONV_ROWS):
            rows = slice(r0, r0 + CONV_ROWS)
            c, us = _conv_tile(lambda a, b: u_ref[a:b, :], taps, r0)
            _, dsilu = _silu_parts(c + bias)
            d = jnp.where(j < n_x, dx_ref[rows, :], jnp.where(j < n_x + n_b, db_ref[rows, :], dc_ref[rows, :])) * dsilu
            scr[rows, :] = d
            db8 = db8 + _fold8(d)
            dw8 = [acc + _fold8(d * u) for acc, u in zip(dw8, us)]
        dbias_ref[...] = jnp.sum(db8, axis=0, keepdims=True)
        for k in range(SSM_K):
            dw_ref[k:k + 1, :] = jnp.sum(dw8[k], axis=0, keepdims=True)
        for r0 in range(0, T, CONV_ROWS):
            du_ref[r0:r0 + CONV_ROWS, :] = _conv_back_tile(scr, taps, r0).astype(BF16)

    wspec = pl.BlockSpec((SSM_K, CB_W), lambda j: (0, j))
    bspec = pl.BlockSpec((1, CB_W), lambda j: (0, j))
    outs, carried = _call(
        body, grid=(N // CB_W,),
        in_specs=[_col(T), wspec, bspec,
                  pl.BlockSpec((T, CB_W), lambda j: (0, jnp.minimum(j, n_x - 1))),
                  pl.BlockSpec((T, CB_W), lambda j: (0, jnp.clip(j - n_x, 0, n_b - 1))),
                  pl.BlockSpec((T, CB_W), lambda j: (0, jnp.clip(j - n_x - n_b, 0, n_b - 1)))],
        out_specs=[_col(T), wspec, bspec],
        out_shape=[_sds((T, N), BF16), _sds((SSM_K, N), F32), _sds((1, N), F32)],
        scratch=[pltpu.VMEM((T + CONV_PAD, CB_W), F32)],
        args=[u, w, b, dxs, dB, dC], name=name, sem=("parallel",), comm=comm)
    return outs if comm is None else (outs, carried)


def _split3(v):
    hi = v.astype(BF16)
    r = v - hi.astype(F32)
    mid = r.astype(BF16)
    lo = (r - mid.astype(F32)).astype(BF16)
    return hi, mid, lo


def _head_expand(n_lanes):
    h = lax.broadcasted_iota(jnp.int32, (LANES, n_lanes), 0)
    l = lax.broadcasted_iota(jnp.int32, (LANES, n_lanes), 1)
    return (jnp.right_shift(l, HEADDIM.bit_length() - 1) == h).astype(BF16)


def _softplus(v):
    return jnp.maximum(v, 0.0) + jnp.log1p(jnp.exp(-jnp.abs(v)))


def _ssd_prep(dt_raw, dt_bias, a_log, n_inner, name):
    T = dt_raw.shape[0]

    def body(r_ref, b_ref, al_ref, dt_ref, cs_ref):
        dt = _softplus(r_ref[...] + b_ref[...])
        a = dt * (-jnp.exp(al_ref[...]))
        i = lax.broadcasted_iota(jnp.int32, (CHUNK, CHUNK), 0)
        j = lax.broadcasted_iota(jnp.int32, (CHUNK, CHUNK), 1)
        tri = (j <= i).astype(BF16)
        cs = sum(_dot(tri, p) for p in _split3(a))
        ex = _head_expand(n_inner)
        dt_ref[...] = sum(_dot(p, ex) for p in _split3(dt))
        cs_ref[...] = sum(_dot(p, ex) for p in _split3(cs))

    blk = pl.BlockSpec((CHUNK, LANES), lambda c: (c, 0))
    out = pl.BlockSpec((CHUNK, n_inner), lambda c: (c, 0))
    return pl.pallas_call(body, grid=(T // CHUNK,), in_specs=[blk, _vec(LANES), _vec(LANES)], out_specs=[out, out],
                          out_shape=[_sds((T, n_inner), F32)] * 2, name=name, compiler_params=_params("parallel"))(dt_raw, dt_bias, a_log)


def _pair_terms(cs_p):
    lane = lax.broadcasted_iota(jnp.int32, (CHUNK, CHUNK), 1)
    sub = lax.broadcasted_iota(jnp.int32, (CHUNK, CHUNK), 0)
    csT = cs_p.T
    Ls = []
    for k in range(2):
        col = jnp.sum(jnp.where(lane == k * HEADDIM, cs_p, 0.0), axis=1, keepdims=True)
        rowv = csT[k * HEADDIM:k * HEADDIM + 1, :]
        Ls.append(jnp.exp(jnp.where(sub >= lane, col - rowv, -jnp.inf)))
    return Ls, jnp.exp(csT[:, CHUNK - 1:CHUNK])


def _block_diag(xp):
    lane = lax.broadcasted_iota(jnp.int32, xp.shape, 1)
    return jnp.concatenate([jnp.where(lane < HEADDIM, xp, 0.0), jnp.where(lane >= HEADDIM, xp, 0.0)], axis=0)


SSD_GROUPS_PER_STEP = 8


def _ssd_specs(T, n_inner):
    nc, gs = T // CHUNK, SSD_GROUPS_PER_STEP
    bo, co = n_inner // (gs * NSTATE), (n_inner + NGROUPS * NSTATE) // (gs * NSTATE)
    assert NGROUPS % gs == 0 and n_inner % (gs * NSTATE) == 0 and (NGROUPS * NSTATE) % (gs * NSTATE) == 0
    g_blk = lambda f: pl.BlockSpec((CHUNK, gs * GROUP_W), lambda c, s: (f(c), s))
    b_blk = lambda f: pl.BlockSpec((CHUNK, gs * NSTATE), lambda c, s: (f(c), bo + s))
    c_blk = lambda f: pl.BlockSpec((CHUNK, gs * NSTATE), lambda c, s: (f(c), co + s))
    return nc, g_blk, b_blk, c_blk


def _ssd_fwd(xbc, dt_e, cs_e, d_e, name, comm=None):
    T = xbc.shape[0]
    n_inner = dt_e.shape[1]
    nc, g_blk, b_blk, c_blk = _ssd_specs(T, n_inner)
    ident = lambda c: c

    gs = SSD_GROUPS_PER_STEP

    def body(xs_ref, b_ref, c_ref, dt_ref, cs_ref, d_ref, y_ref, p_ref, st):
        c, s = pl.program_id(0), pl.program_id(1)

        @pl.when(c == 0)
        def _():
            for gi in range(gs):
                st[s * gs + gi] = jnp.zeros((GROUP_W, NSTATE), F32)

        for gi in range(gs):
            g = s * gs + gi
            gw, gn = slice(gi * GROUP_W, (gi + 1) * GROUP_W), slice(gi * NSTATE, (gi + 1) * NSTATE)
            P = st[g]
            p_ref[0, gi] = P
            xs, dt, cs = xs_ref[:, gw], dt_ref[:, gw], cs_ref[:, gw]
            Bf, Cf = b_ref[:, gn], c_ref[:, gn]
            CBm = _dot3(Cf, Bf, NT)
            X = xs * dt
            decay = jnp.exp(cs[CHUNK - 1:CHUNK, :] - cs)
            y_off = _dot3(Cf, P, NT) * jnp.exp(cs)
            ys, ecl = [], []
            for pr in range(2):
                sl = slice(pr * LANES, (pr + 1) * LANES)
                Ls, e_last = _pair_terms(cs[:, sl])
                ecl.append(e_last)
                Mcat = jnp.concatenate([CBm * L for L in Ls], axis=1)
                ys.append(_dot3(Mcat, _block_diag(X[:, sl])))
            y_ref[:, gw] = jnp.concatenate(ys, axis=1) + y_off + xs * d_ref[:, gw]
            S = _dot3(X * decay, Bf, TN)
            st[g] = P * jnp.concatenate(ecl, axis=0) + S

    p_blk = pl.BlockSpec((1, gs, GROUP_W, NSTATE), lambda c, s: (c, s, 0, 0))
    outs, carried = _call(
        body, grid=(nc, NGROUPS // gs),
        in_specs=[g_blk(ident), b_blk(ident), c_blk(ident), g_blk(ident), g_blk(ident), pl.BlockSpec((1, gs * GROUP_W), lambda c, s: (0, s))],
        out_specs=[g_blk(ident), p_blk],
        out_shape=[_sds((T, n_inner), F32), _sds((nc, NGROUPS, GROUP_W, NSTATE), F32)],
        scratch=[pltpu.VMEM((NGROUPS, GROUP_W, NSTATE), F32)],
        args=[xbc, xbc, xbc, dt_e, cs_e, d_e], name=name, sem=("arbitrary", "arbitrary"), comm=comm)
    return outs if comm is None else (outs, carried)


def _ssd_bwd(xbc, dt_e, cs_e, d_e, states, dy, name, comm=None):
    T = xbc.shape[0]
    n_inner = dt_e.shape[1]
    nc, g_blk, b_blk, c_blk = _ssd_specs(T, n_inner)
    rev = lambda c: nc - 1 - c

    gs = SSD_GROUPS_PER_STEP

    def body(xs_ref, b_ref, c_ref, dt_ref, cs_ref, d_ref, p_ref, pn_ref, dy_ref,
             dxs_ref, db_ref, dc_ref, ddt_ref, dcs_ref, dd_ref, dst):
        cc, s = pl.program_id(0), pl.program_id(1)

        @pl.when(cc == 0)
        def _():
            for gi in range(gs):
                dst[s * gs + gi] = jnp.zeros((GROUP_W, NSTATE), F32)

        for gi in range(gs):
            one_group(s * gs + gi, gi, xs_ref, b_ref, c_ref, dt_ref, cs_ref, d_ref, p_ref, pn_ref, dy_ref,
                      dxs_ref, db_ref, dc_ref, ddt_ref, dcs_ref, dd_ref, dst)

    def one_group(g, gi, xs_ref, b_ref, c_ref, dt_ref, cs_ref, d_ref, p_ref, pn_ref, dy_ref,
                  dxs_ref, db_ref, dc_ref, ddt_ref, dcs_ref, dd_ref, dst):
        gw, gn = slice(gi * GROUP_W, (gi + 1) * GROUP_W), slice(gi * NSTATE, (gi + 1) * NSTATE)
        dS = dst[g]
        P, Pn = p_ref[0, gi], pn_ref[0, gi]
        xs, dt, cs, dY = xs_ref[:, gw], dt_ref[:, gw], cs_ref[:, gw], dy_ref[:, gw]
        Bf, Cf = b_ref[:, gn], c_ref[:, gn]
        Bb, Cb = Bf.astype(BF16), Cf.astype(BF16)
        X = xs * dt
        ecs = jnp.exp(cs)
        decay = jnp.exp(cs[CHUNK - 1:CHUNK, :] - cs)
        CBm = _dot3(Cf, Bf, NT)
        dYe = dY * ecs
        dP_off = _dot3(dYe, Cf, TN)
        dC = _dot(dYe.astype(BF16), P.astype(BF16))
        dcs = dYe * _dot3(Cf, P, NT)
        Xd = X * decay
        dB = _dot(Xd.astype(BF16), dS.astype(BF16))
        E = _dot3(Bf, dS, NT)
        dX = E * decay
        dcs = dcs - E * Xd
        R = _dot3(jnp.ones((8, NSTATE), F32), dS * Pn, NT)
        sub_g = lax.broadcasted_iota(jnp.int32, (CHUNK, GROUP_W), 0)
        dcs = dcs + jnp.where(sub_g == CHUNK - 1, R[0:1, :], 0.0)
        lane = lax.broadcasted_iota(jnp.int32, (CHUNK, CHUNK), 1)
        sub = lax.broadcasted_iota(jnp.int32, (CHUNK, CHUNK), 0)
        dCB = jnp.zeros((CHUNK, CHUNK), F32)
        dXs, dcss, ecl = [], [], []
        for pr in range(2):
            sl = slice(pr * LANES, (pr + 1) * LANES)
            Ls, e_last = _pair_terms(cs[:, sl])
            ecl.append(e_last)
            dYp = dY[:, sl]
            dMcat = _dot3(dYp, _block_diag(X[:, sl]), NT)
            Mcat = jnp.concatenate([CBm * L for L in Ls], axis=1)
            dXt = _dot3(Mcat, dYp, TN)
            dXs.append(jnp.where(lane < HEADDIM, dXt[:CHUNK], dXt[CHUNK:]))
            colacc = jnp.zeros((CHUNK, CHUNK), F32)
            rowacc = jnp.zeros((CHUNK, CHUNK), F32)
            for k in range(2):
                dG = dMcat[:, k * CHUNK:(k + 1) * CHUNK] * Ls[k]
                dCB = dCB + dG
                Q = dG * CBm
                colacc = colacc + jnp.where(lane == k * HEADDIM, jnp.sum(Q, axis=1, keepdims=True), 0.0)
                rowacc = rowacc + jnp.where(sub == k * HEADDIM, jnp.sum(Q, axis=0, keepdims=True), 0.0)
            dcss.append(colacc - rowacc.T)
        dX = dX + jnp.concatenate(dXs, axis=1)
        dcs = dcs + jnp.concatenate(dcss, axis=1)
        dCBb = dCB.astype(BF16)
        dc_ref[:, gn] = dC + _dot(dCBb, Bb)
        db_ref[:, gn] = dB + _dot(dCBb, Cb, TN)
        dxs_ref[:, gw] = dX * dt + dY * d_ref[:, gw]
        ddt_ref[:, gw] = dX * xs
        dcs_ref[:, gw] = dcs
        dd_ref[0, :, gw] = jnp.sum(dY * xs, axis=0, keepdims=True)
        dst[g] = dS * jnp.concatenate(ecl, axis=0) + dP_off

    p_blk = pl.BlockSpec((1, gs, GROUP_W, NSTATE), lambda c, s: (nc - 1 - c, s, 0, 0))
    pn_blk = pl.BlockSpec((1, gs, GROUP_W, NSTATE), lambda c, s: (jnp.minimum(nc - c, nc - 1), s, 0, 0))
    st_blk = pl.BlockSpec((CHUNK, gs * NSTATE), lambda c, s: (nc - 1 - c, s))
    outs, carried = _call(
        body, grid=(nc, NGROUPS // gs),
        in_specs=[g_blk(rev), b_blk(rev), c_blk(rev), g_blk(rev), g_blk(rev), pl.BlockSpec((1, gs * GROUP_W), lambda c, s: (0, s)),
                  p_blk, pn_blk, g_blk(rev)],
        out_specs=[g_blk(rev), st_blk, st_blk, g_blk(rev), g_blk(rev), pl.BlockSpec((1, 1, gs * GROUP_W), lambda c, s: (nc - 1 - c, 0, s))],
        out_shape=[_sds((T, n_inner), F32), _sds((T, NGROUPS * NSTATE), F32), _sds((T, NGROUPS * NSTATE), F32),
                   _sds((T, n_inner), F32), _sds((T, n_inner), F32), _sds((nc, 1, n_inner), F32)],
        scratch=[pltpu.VMEM((NGROUPS, GROUP_W, NSTATE), F32)],
        args=[xbc, xbc, xbc, dt_e, cs_e, d_e, states, states, dy], name=name, sem=("arbitrary", "arbitrary"), comm=comm)
    return outs if comm is None else (outs, carried)


def _ssd_post(ddt_e, dcs_e, dd_p, dt_raw, dt_bias, a_log, n_heads, name):
    T, n_inner = ddt_e.shape

    def body(ddt_ref, dcs_ref, dd_ref, r_ref, b_ref, al_ref, draw_ref, dbias_ref, dal_ref, ddsk_ref):
        @pl.when(pl.program_id(0) == 0)
        def _():
            dbias_ref[...] = jnp.zeros_like(dbias_ref)
            dal_ref[...] = jnp.zeros_like(dal_ref)
            ddsk_ref[...] = jnp.zeros_like(ddsk_ref)

        ex = _head_expand(n_inner)
        red = lambda v: sum(_dot(p, ex, NT) for p in _split3(v))
        raw = r_ref[...] + b_ref[...]
        dt = _softplus(raw)
        A = -jnp.exp(al_ref[...])
        i = lax.broadcasted_iota(jnp.int32, (CHUNK, CHUNK), 0)
        j = lax.broadcasted_iota(jnp.int32, (CHUNK, CHUNK), 1)
        upper = (j >= i).astype(BF16)
        da = sum(_dot(upper, p) for p in _split3(red(dcs_ref[...])))
        ddt = red(ddt_ref[...]) + da * A
        lane = lax.broadcasted_iota(jnp.int32, (CHUNK, LANES), 1)
        draw = jnp.where(lane < n_heads, ddt * jax.nn.sigmoid(raw), 0.0)
        draw_ref[...] = draw.astype(BF16)
        dbias_ref[...] += jnp.sum(draw, axis=0, keepdims=True)
        dal_ref[...] += jnp.sum(da * dt, axis=0, keepdims=True) * A
        ddsk_ref[...] += red(jnp.broadcast_to(dd_ref[0], (8, n_inner)))[0:1, :]

    wide = pl.BlockSpec((CHUNK, n_inner), lambda c: (c, 0))
    blk = pl.BlockSpec((CHUNK, LANES), lambda c: (c, 0))
    return pl.pallas_call(
        body, grid=(T // CHUNK,),
        in_specs=[wide, wide, pl.BlockSpec((1, 1, n_inner), lambda c: (c, 0, 0)), blk, _vec(LANES), _vec(LANES)],
        out_specs=[blk, _vec(LANES), _vec(LANES), _vec(LANES)],
        out_shape=[_sds((T, LANES), BF16)] + [_sds((1, LANES), F32)] * 3,
        name=name, compiler_params=_params("arbitrary"))(ddt_e, dcs_e, dd_p, dt_raw, dt_bias, a_log)


def _row2(v):
    return v.reshape(1, -1).astype(F32)


def _pad_lanes(v):
    return jnp.pad(_row2(v), ((0, 0), (0, LANES - v.shape[-1])))


class _NoExchange:
    def __init__(self, W):
        self.W, self.grads = W, {}

    def weight(self, k):
        return self.W[k]

    def carry(self, name):
        return None

    def carried(self, name, outs):
        pass

    def grad(self, k, g):
        self.grads[k] = g

    def tok(self):
        return jnp.zeros((), F32)

    def point(self, name, value):
        pass


def _local_step(x, tgt, S, small):
    T, D = x.shape

    def mm(a, b, *, name, **kw):
        comm = S.carry(name)
        if comm is None:
            return _mm(a, b, name=name, **kw)
        res, outs = _mm(a, b, name=name, comm=comm, **kw)
        S.carried(name, outs)
        return res

    def carrying(fn, *args, name):
        comm = S.carry(name)
        if comm is None:
            return fn(*args, name)
        res, outs = fn(*args, name, comm=comm)
        S.carried(name, outs)
        return res

    n_inner = 2 * D
    n_heads = n_inner // HEADDIM
    norm_mix, norm_mlp, norm_final = _row2(small["norm_mix"]), _row2(small["norm_mlp"]), _row2(small["norm_final"])
    b_gate, ssm_b, ssm_norm_w = _row2(small["b_gate"]), _row2(small["ssm_conv_b"]), _row2(small["ssm_norm_w"])
    dt_bias, a_log = _pad_lanes(small["dt_bias"]), _pad_lanes(small["A_log"])
    d_e = jnp.repeat(small["D_skip"].astype(F32), HEADDIM).reshape(1, n_inner)

    hb = carrying(_rms_fwd, x, norm_mix, name="rms_mix")
    sc_w, ssm_w = S.weight("sc_conv_w"), S.weight("ssm_conv_w")
    p_xbc = mm(hb, S.weight("xbc"), mode="nn", name="proj_xbc")
    S.point("first_projection_done", [p_xbc])
    p_dt = mm(hb, S.weight("dt"), mode="nn", name="proj_dt")
    p_z = mm(hb, S.weight("z"), mode="nn", name="proj_z")
    p_sc = mm(hb, S.weight("sc"), mode="nn", name="proj_sc")
    p_gate = mm(hb, S.weight("gate"), mode="nn", name="proj_gate")
    xbc = carrying(_ssm_conv_fwd, p_xbc, ssm_w, ssm_b, name="ssm_conv_fwd")
    dt_e, cs_e = _ssd_prep(p_dt, dt_bias, a_log, n_inner, "ssd_prep")
    ya = _sc_fwd(p_sc, sc_w, "sc_fwd")
    y, states = carrying(_ssd_fwd, xbc, dt_e, cs_e, d_e, name="ssd_fwd")
    S.point("mixers_done", [y, ya, p_gate])
    yb = carrying(_gnorm_fwd, y, p_z, ssm_norm_w, name="gnorm_fwd")
    br_a = mm(ya, S.weight("bsc"), mode="nn", name="branch_sc")
    br_b = mm(yb, S.weight("bssm"), mode="nn", name="branch_ssm")
    merged = _merge_fwd(p_gate, b_gate, br_a, br_b, "merge_fwd")
    x1 = mm(merged, S.weight("out"), mode="nn", name="out_proj", extras=(x,), epi=_epi_add)
    h2 = _rms_fwd(x1, norm_mlp, "rms_mlp")
    r_act = mm(h2, S.weight("w1"), mode="nn", name="mlp_up", epi=_epi_relu2, out_dtypes=(BF16,))
    x2 = mm(r_act, S.weight("w2"), mode="nn", name="mlp_down", extras=(x1,), epi=_epi_add)
    dx2, dx2b, g_norm_final, loss_row = _final(x2, norm_final, tgt, "final")

    S.grad("w2", mm(r_act, dx2b, mode="tn", name="mlp_down_dw", out_dtypes=(BF16,)))
    da = mm(dx2b, S.weight("w2"), mode="nt", name="mlp_down_dx", extras=(r_act,), epi=_epi_relu2_bwd, out_dtypes=(BF16,))
    S.grad("w1", mm(h2, da, mode="tn", name="mlp_up_dw", out_dtypes=(BF16,)))
    dh2 = mm(da, S.weight("w1"), mode="nt", name="mlp_up_dx")
    dx1, dx1b, g_norm_mlp = _rms_bwd(x1, norm_mlp + S.tok(), dh2, dx2, "rms_mlp_bwd")
    S.grad("out", mm(merged, dx1b, mode="tn", name="out_proj_dw", out_dtypes=(BF16,)))
    dmerged = mm(dx1b, S.weight("out"), mode="nt", name="out_proj_dx")
    dbr_a, dbr_b, d_gate, g_b_gate = _merge_bwd(dmerged, p_gate, b_gate, br_a, br_b, "merge_bwd")
    S.grad("bssm", mm(yb, dbr_b, mode="tn", name="branch_ssm_dw", out_dtypes=(BF16,)))
    S.grad("bsc", mm(ya, dbr_a, mode="tn", name="branch_sc_dw", out_dtypes=(BF16,)))
    dyb = mm(dbr_b, S.weight("bssm"), mode="nt", name="branch_ssm_dx")
    dya = mm(dbr_a, S.weight("bsc"), mode="nt", name="branch_sc_dx")
    dy, d_z, g_ssm_norm_w = _gnorm_bwd(y, p_z, ssm_norm_w + S.tok(), dyb, "gnorm_bwd")
    dxs, dB, dC, ddt_e, dcs_e, dd_p = carrying(_ssd_bwd, xbc, dt_e, cs_e, d_e, states, dy, name="ssd_bwd")
    d_dt, g_dt_bias, g_a_log, g_d_skip = _ssd_post(ddt_e, dcs_e, dd_p, p_dt, dt_bias, a_log, n_heads, "ssd_post")
    d_xbc, g_ssm_w, g_ssm_b = carrying(_ssm_conv_bwd, p_xbc, ssm_w, ssm_b, dxs, dB, dC, name="ssm_conv_bwd")
    d_scB, d_scC, d_scX, g_sc_w = _sc_bwd(p_sc, sc_w, dya, "sc_bwd")
    d_sc = jnp.concatenate([d_scB, d_scC, d_scX], axis=1)
    pieces = [("sc", d_sc), ("z", d_z), ("xbc", d_xbc), ("dt", d_dt), ("gate", d_gate)]
    S.grad("win", {k: mm(hb, d, mode="tn", name="proj_dw_" + k, out_dtypes=(BF16,)) for k, d in pieces})
    pieces = [(k, d + S.tok().astype(d.dtype) if k == "dt" else d) for k, d in pieces]
    dh = mm([d for _, d in pieces], [S.weight(k) for k, _ in pieces], mode="nt", name="proj_dx")
    grad_x, _, g_norm_mix = _rms_bwd(x, norm_mix, dh, dx1, "rms_mix_bwd")

    g_small = dict(norm_mix=g_norm_mix, b_gate=g_b_gate, sc_conv_w=g_sc_w, ssm_conv_w=g_ssm_w, ssm_conv_b=g_ssm_b,
                   dt_bias=g_dt_bias, A_log=g_a_log, D_skip=g_d_skip, ssm_norm_w=g_ssm_norm_w, norm_mlp=g_norm_mlp,
                   norm_final=g_norm_final, loss=loss_row)
    return grad_x, g_small


class _Place:
    def __init__(self, k=0):
        x, y, c = lax.axis_index("x"), lax.axis_index("y"), lax.axis_index("c")
        self.x = 1 - x if k & 4 else x
        self.y = 1 - y if k & 2 else y
        self.c = 1 - c if k & 1 else c
        self.chip = 2 * self.x + self.y
        self.id = 2 * self.chip + self.c


ICI_PEERS = (2, 4, 6)
SIBLING = (1,)
ALL_PEERS = (1, 2, 3, 4, 5, 6, 7)


class _Comm:
    def __init__(self, arrs, out_shape, ks, src, dst, own=None, aliases=None):
        self.arrs, self.out_shape, self.ks = list(arrs), list(out_shape), tuple(ks)
        self.n = len(self.arrs)
        self.src, self.dst, self.own = src, dst, own
        self.aliases = aliases or {}
        dma = pltpu.SemaphoreType.DMA
        self.scratch = [dma((self.n, len(self.ks))), dma((self.n, len(self.ks))), dma((self.n,))]

    def _copies(self, ins, outs, sems, with_recvs):
        send_sems, recv_sems, local_sems = sems
        me = _Place()
        owns, sends, recvs = [], [], []
        for a in range(self.n):
            if self.own is not None:
                s, d = self.own(a, ins[a], outs[a], me)
                owns.append(pltpu.make_async_copy(s, d, local_sems.at[a]))
            for i, k in enumerate(self.ks):
                peer = _Place(k)
                for sender, lst in ((me, sends), (peer, recvs)) if with_recvs else ((me, sends),):
                    lst.append(pltpu.make_async_remote_copy(
                        src_ref=self.src(a, ins[a], me, peer), dst_ref=self.dst(a, outs[a], sender),
                        send_sem=send_sems.at[a, i], recv_sem=recv_sems.at[a, i],
                        device_id=(peer.x, peer.y, peer.c), device_id_type=MESH))
        return owns, sends, recvs

    def start(self, ins, outs, sems):
        owns, sends, _ = self._copies(ins, outs, sems, False)
        for cp in owns + sends:
            cp.start()

    def finish(self, ins, outs, sems):
        owns, sends, recvs = self._copies(ins, outs, sems, True)
        for cp in recvs:
            cp.wait_recv()
        for cp in sends:
            cp.wait_send()
        for cp in owns:
            cp.wait()


class _GatherBoth:
    def __init__(self, shards):
        self.arrs, self.n, self.aliases = list(shards), len(shards), {}
        self.out_shape = [_sds((4, 2) + s.shape, s.dtype) for s in shards]
        dma = pltpu.SemaphoreType.DMA
        self.scratch = [dma((self.n, 7)), dma((self.n, 7)), dma((self.n,))]

    def _copy(self, a, j, src, slot, to, outs, sems):
        return pltpu.make_async_remote_copy(src_ref=src, dst_ref=outs[a].at[slot.chip, slot.c], send_sem=sems[0].at[a, j],
                                            recv_sem=sems[1].at[a, j], device_id=(to.x, to.y, to.c), device_id_type=MESH)

    def start(self, ins, outs, sems):
        me, sib = _Place(), _Place(1)
        for a in range(self.n):
            pltpu.make_async_copy(ins[a], outs[a].at[me.chip, me.c], sems[2].at[a]).start()
            self._copy(a, 0, ins[a], me, sib, outs, sems).start()
            for i, k in enumerate(ICI_PEERS):
                self._copy(a, 1 + i, ins[a], me, _Place(k), outs, sems).start()

    def finish(self, ins, outs, sems):
        me, sib = _Place(), _Place(1)
        passed = []
        for i, k in enumerate(ICI_PEERS):
            peer = _Place(k)
            for a in range(self.n):
                self._copy(a, 1 + i, ins[a], peer, peer, outs, sems).wait_recv()
                cp = self._copy(a, 4 + i, outs[a].at[peer.chip, peer.c], peer, sib, outs, sems)
                cp.start()
                passed.append(cp)
        for a in range(self.n):
            self._copy(a, 0, ins[a], sib, sib, outs, sems).wait_recv()
            for i, k in enumerate(ICI_PEERS):
                far = _Place(k | 1)
                self._copy(a, 4 + i, outs[a].at[far.chip, far.c], far, sib, outs, sems).wait_recv()
        for a in range(self.n):
            self._copy(a, 0, ins[a], me, sib, outs, sems).wait_send()
            for i, k in enumerate(ICI_PEERS):
                self._copy(a, 1 + i, ins[a], me, _Place(k), outs, sems).wait_send()
            pltpu.make_async_copy(ins[a], outs[a].at[me.chip, me.c], sems[2].at[a]).wait()
        for cp in passed:
            cp.wait_send()


def _run_comm(comm, name, after=()):
    n, n_after = comm.n, len(after)

    def body(*refs):
        ins, outs, sems = refs[:n], refs[n + n_after:2 * n + n_after], refs[2 * n + n_after:]
        comm.start(ins, outs, sems)
        comm.finish(ins, outs, sems)

    return list(pl.pallas_call(body, in_specs=[ANY] * (n + n_after), out_specs=[ANY] * n, out_shape=comm.out_shape,
                               scratch_shapes=comm.scratch, input_output_aliases=dict(comm.aliases), name=name)(*comm.arrs, *after))


def _gather_ici(shards):
    return _Comm(shards, [_sds((4, 2) + s.shape, s.dtype) for s in shards], ICI_PEERS,
                 src=lambda a, i, me, p: i, dst=lambda a, o, s: o.at[s.chip, s.c], own=lambda a, i, o, me: (i, o.at[me.chip, me.c]))


def _gather_sibling(bufs):
    return _Comm(bufs, [_sds(b.shape, b.dtype) for b in bufs], SIBLING,
                 src=lambda a, i, me, p: i.at[:, me.c], dst=lambda a, o, s: o.at[:, s.c], aliases={a: a for a in range(len(bufs))})


def _scatter_sibling(parts):
    return _Comm(parts, [_sds((4,) + p.shape[2:], p.dtype) for p in parts], SIBLING,
                 src=lambda a, i, me, p: i.at[:, p.c], dst=lambda a, o, s: o)


def _scatter_ici(parts):
    return _Comm(parts, [_sds(p.shape, p.dtype) for p in parts], ICI_PEERS,
                 src=lambda a, i, me, p: i.at[p.chip], dst=lambda a, o, s: o.at[s.chip], own=lambda a, i, o, me: (i.at[me.chip], o.at[me.chip]))


HBM_SPEC = pl.BlockSpec(memory_space=pltpu.HBM)
SEM_SPEC = pl.BlockSpec(memory_space=pltpu.SEMAPHORE)
DATAFLOW = pltpu.SideEffectType.DATAFLOW_SIDE_EFFECTING


def _tiles_2d(R, C, max_rows=256):
    if R % max_rows == 0:
        return max_rows, C, R // max_rows, lambda i: (i, 0)
    if R <= 2 * max_rows or C % 256:
        return R, C, 1, lambda i: (0, 0)
    return R, 256, C // 256, lambda i: (0, i)


def _own_part(parts, name):
    n, R, C = parts.shape
    br, bc, nb, at = _tiles_2d(R, C)
    chip = (2 * lax.axis_index("x") + lax.axis_index("y")).astype(jnp.int32).reshape(1)

    def body(q_ref, p_ref, o_ref):
        o_ref[...] = p_ref[...]

    blk = pl.BlockSpec((1, br, bc), lambda i, q_ref: (q_ref[0],) + at(i))
    spec = pltpu.PrefetchScalarGridSpec(num_scalar_prefetch=1, grid=(nb,), in_specs=[blk], out_specs=blk)
    return pl.pallas_call(body, grid_spec=spec, out_shape=_sds((n, R, C), parts.dtype), name=name,
                          compiler_params=_params("parallel"))(chip, parts)


def _ici_copy(gather, a, srcs, lands, send_sems, recv_sems, i, me, peer, sender):
    src = lands[a].at[me.chip, me.c] if gather else srcs[a].at[peer.chip]
    dst = lands[a].at[sender.chip, sender.c] if gather else lands[a].at[sender.chip]
    j = a * len(ICI_PEERS) + i
    return pltpu.make_async_remote_copy(src_ref=src, dst_ref=dst, send_sem=send_sems.at[j], recv_sem=recv_sems.at[j],
                                        device_id=(peer.x, peer.y, peer.c), device_id_type=MESH)


def _ici_start(srcs, lands, gather, name):
    n, n_s = len(lands), len(srcs)
    bufs = list(srcs) + list(lands)

    def body(*refs):
        src_refs, land_refs = refs[:n_s], refs[n_s:n_s + n]
        send_sems, recv_sems = refs[n_s + n], refs[n_s + n + 1]
        token = refs[-1]
        me = _Place()
        for a in range(n):
            for i, k in enumerate(ICI_PEERS):
                _ici_copy(gather, a, src_refs, land_refs, send_sems, recv_sems, i, me, _Place(k), me).start()
        token[...] = jnp.zeros_like(token)

    dma = pltpu.SemaphoreType.DMA((n * len(ICI_PEERS),))
    outs = pl.pallas_call(
        body, name=name, out_shape=(dma, dma, *[pltpu.HBM(v.shape, v.dtype) for v in bufs], _sds((8, LANES), F32)),
        in_specs=(HBM_SPEC,) * len(bufs),
        out_specs=(SEM_SPEC, SEM_SPEC) + (HBM_SPEC,) * len(bufs) + (pl.BlockSpec(memory_space=pltpu.VMEM),),
        input_output_aliases={j: 2 + j for j in range(len(bufs))}, compiler_params=pltpu.CompilerParams(has_side_effects=DATAFLOW),
    )(*[pltpu.with_memory_space_constraint(v, pltpu.HBM) for v in bufs])
    return outs[0], outs[1], list(outs[2:2 + n_s]), list(outs[2 + n_s:2 + n_s + n]), outs[-1]


def _ici_wait(flight, after, gather, name):
    send_sems, recv_sems, srcs, lands, _ = flight
    n, n_s = len(lands), len(srcs)
    bufs = srcs + lands

    def body(*refs):
        src_refs, land_refs = refs[:n_s], refs[n_s:n_s + n]
        s_sems, r_sems = refs[n_s + n], refs[n_s + n + 1]
        me = _Place()
        for a in range(n):
            for i, k in enumerate(ICI_PEERS):
                peer = _Place(k)
                cp = _ici_copy(gather, a, src_refs, land_refs, s_sems, r_sems, i, me, peer, peer)
                cp.wait_send()
                cp.wait_recv()

    outs = pl.pallas_call(
        body, name=name, out_shape=tuple(pltpu.HBM(v.shape, v.dtype) for v in bufs),
        in_specs=(HBM_SPEC,) * len(bufs) + (SEM_SPEC, SEM_SPEC) + (ANY,) * len(after), out_specs=(HBM_SPEC,) * len(bufs),
        input_output_aliases={j: j for j in range(len(bufs))}, compiler_params=pltpu.CompilerParams(has_side_effects=DATAFLOW),
    )(*bufs, send_sems, recv_sems, *after)
    return list(outs[n_s:])


def _own_shard(shard, after, name):
    R, C = shard.shape
    tr = R if R <= 256 else 256
    place = jnp.stack([2 * lax.axis_index("x") + lax.axis_index("y"), lax.axis_index("c")]).astype(jnp.int32)

    def body(q_ref, s_ref, after_ref, o_ref):
        o_ref[0, 0] = s_ref[...].astype(o_ref.dtype)

    spec = pltpu.PrefetchScalarGridSpec(
        num_scalar_prefetch=1, grid=(R // tr,), in_specs=[pl.BlockSpec((tr, C), lambda i, q_ref: (i, 0)), ANY],
        out_specs=pl.BlockSpec((1, 1, tr, C), lambda i, q_ref: (q_ref[0], q_ref[1], i, 0)))
    return pl.pallas_call(body, grid_spec=spec, out_shape=_sds((4, 2, R, C), BF16), name=name,
                          compiler_params=_params("parallel"))(place, shard, after)


def _col_pieces(widths):
    out, c = [], 0
    for k, w in widths:
        out.append((k, c, w))
        c += w
    return out


def _split_range(c0, n, bounds):
    parts, c = [], c0
    while c < c0 + n:
        r = max(i for i in range(len(bounds) - 1) if bounds[i] <= c)
        w = min(c0 + n, bounds[r + 1]) - c
        parts.append((r, c - bounds[r], w))
        c += w
    return parts


def _win_unpack(g, widths, name):
    n, R, C = g.shape
    tr = min(256, R)
    pieces = _col_pieces(widths)
    padded = [-(-w // LANES) * LANES for _, _, w in pieces]
    shard_bounds = [s * C for s in range(n + 1)]

    def body(g_ref, *o_refs):
        for (k, c0, w), o_ref in zip(pieces, o_refs):
            for t in range(0, o_ref.shape[1], LANES):
                valid = max(0, min(LANES, w - t))
                cols = [g_ref[s, :, o:o + ww] for s, o, ww in _split_range(c0 + t, valid, shard_bounds)] if valid else []
                if valid < LANES:
                    cols.append(jnp.zeros((tr, LANES - valid), g_ref.dtype))
                o_ref[:, t:t + LANES] = cols[0] if len(cols) == 1 else jnp.concatenate(cols, axis=1)

    return pl.pallas_call(
        body, grid=(R // tr,), in_specs=[pl.BlockSpec((n, tr, C), lambda i: (0, i, 0))],
        out_specs=[pl.BlockSpec((tr, p), lambda i: (i, 0)) for p in padded],
        out_shape=[_sds((R, p), g.dtype) for p in padded], name=name, compiler_params=_params("parallel"))(g)


def _win_pack(grads, widths, n, name):
    R = grads[0].shape[0]
    tr = min(256, R)
    pieces = _col_pieces(widths)
    total = pieces[-1][1] + pieces[-1][2]
    C = total // n
    bounds = [c0 for _, c0, _ in pieces] + [total]

    def body(*refs):
        g_refs, o_ref = refs[:-1], refs[-1]

        def tile_t(c0):
            cols = [g_refs[r][:, o:o + ww] for r, o, ww in _split_range(c0, LANES, bounds)]
            tile = cols[0] if len(cols) == 1 else jnp.concatenate(cols, axis=1)
            return tile.astype(F32).T

        for s in range(n):
            full = C // LANES * LANES
            for t in range(0, full, LANES):
                o_ref[s, t:t + LANES, :] = tile_t(s * C + t).astype(o_ref.dtype)
            if full < C:
                o_ref[s, full:C, :] = tile_t(s * C + C - LANES)[LANES - (C - full):, :].astype(o_ref.dtype)

    return pl.pallas_call(
        body, grid=(R // tr,), in_specs=[pl.BlockSpec((tr, gr.shape[1]), lambda i: (i, 0)) for gr in grads],
        out_specs=pl.BlockSpec((n, C, tr), lambda i: (0, 0, i)), out_shape=_sds((n, C, R), grads[0].dtype),
        name=name, compiler_params=_params("parallel"))(*grads)


def _gather_all(arrs):
    return _Comm(arrs, [_sds((N_DEV,) + a.shape, a.dtype) for a in arrs], ALL_PEERS,
                 src=lambda a, i, me, p: i, dst=lambda a, o, s: o.at[s.id], own=lambda a, i, o, me: (i, o.at[me.id]))


def _add_halves(parts, got, name):
    n, _, R, C = parts.shape
    br, bc, nb, at = _tiles_2d(R, C)
    core = lax.axis_index("c").astype(jnp.int32).reshape(1)

    def body(c_ref, p_ref, g_ref, o_ref):
        o_ref[0] = (p_ref[0, 0].astype(F32) + g_ref[0].astype(F32)).astype(o_ref.dtype)

    spec = pltpu.PrefetchScalarGridSpec(
        num_scalar_prefetch=1, grid=(n, nb),
        in_specs=[pl.BlockSpec((1, 1, br, bc), lambda q, i, c_ref: (q, c_ref[0]) + at(i)), pl.BlockSpec((1, br, bc), lambda q, i, c_ref: (q,) + at(i))],
        out_specs=pl.BlockSpec((1, br, bc), lambda q, i, c_ref: (q,) + at(i)))
    return pl.pallas_call(body, grid_spec=spec, out_shape=_sds((n, R, C), parts.dtype), name=name,
                          compiler_params=_params("parallel", "parallel"))(core, parts, got)


def _adam(w, m, v, gparts, name, comm=None):
    R, C = w.shape
    n = gparts.shape[0]
    br, bc, nb, at = _tiles_2d(R, C, max_rows=128)
    c1 = 1.0 / (1.0 - ADAM_B1 ** ADAM_STEP)
    c2 = 1.0 / (1.0 - ADAM_B2 ** ADAM_STEP)

    def body(w_ref, m_ref, v_ref, g_ref, go_ref, d_ref, mo_ref, vo_ref):
        g = g_ref[0].astype(F32)
        for s in range(1, n):
            g = g + g_ref[s].astype(F32)
        mn = ADAM_B1 * m_ref[...] + (1.0 - ADAM_B1) * g
        vn = ADAM_B2 * v_ref[...] + (1.0 - ADAM_B2) * (g * g)
        go_ref[...] = g
        mo_ref[...] = mn
        vo_ref[...] = vn
        d_ref[...] = -ADAM_LR * ((mn * c1) / (jnp.sqrt(vn * c2) + ADAM_EPS) + ADAM_WD * w_ref[...])

    blk = pl.BlockSpec((br, bc), at)
    outs, carried = _call(
        body, grid=(nb,), in_specs=[blk, blk, blk, pl.BlockSpec((n, br, bc), lambda i: (0,) + at(i))],
        out_specs=[blk] * 4, out_shape=[_sds((R, C), F32)] * 4, args=[w, m, v, gparts], name=name, sem=("parallel",), comm=comm)
    return outs if comm is None else (outs, carried)


_SMALL_ORDER = ("norm_mix", "b_gate", "sc_conv_w", "ssm_conv_w", "ssm_conv_b", "dt_bias", "A_log", "D_skip", "ssm_norm_w",
                "norm_mlp", "norm_final", "loss")
_REPLICATED = ("norm_mix", "b_gate", "ssm_conv_b", "dt_bias", "A_log", "D_skip", "ssm_norm_w", "norm_mlp", "norm_final")


def _cols_to_slots(g, n):
    R = g.shape[0]
    return jnp.transpose(g.reshape(R, n, g.shape[1] // n), (1, 0, 2))


def _slots_to_cols(g):
    n, R, C = g.shape
    return jnp.transpose(g, (1, 0, 2)).reshape(R, n * C)


def kernel(x, norm_mix, w_in, b_gate, sc_conv_w, ssm_conv_w, ssm_conv_b, dt_bias, A_log, D_skip, ssm_norm_w, w_branch_sc, w_branch_ssm, w_out, norm_mlp, w_mlp1, w_mlp2, norm_final, loss_target, m_norm_mix, m_w_in, m_b_gate, m_sc_conv_w, m_ssm_conv_w, m_ssm_conv_b, m_dt_bias, m_A_log, m_D_skip, m_ssm_norm_w, m_w_branch_sc, m_w_branch_ssm, m_w_out, m_norm_mlp, m_w_mlp1, m_w_mlp2, m_norm_final, v_norm_mix, v_w_in, v_b_gate, v_sc_conv_w, v_ssm_conv_w, v_ssm_conv_b, v_dt_bias, v_A_log, v_D_skip, v_ssm_norm_w, v_w_branch_sc, v_w_branch_ssm, v_w_out, v_norm_mlp, v_w_mlp1, v_w_mlp2, v_norm_final):
    T, D = x.shape[1], x.shape[2]
    n_inner = 2 * D
    n_heads = n_inner // HEADDIM
    n_xbc = n_inner + 2 * NGROUPS * NSTATE
    me = 4 * lax.axis_index("x") + 2 * lax.axis_index("y") + lax.axis_index("c")

    in_cols = [("sc", 3 * D), ("z", n_inner), ("xbc", n_xbc), ("dt", n_heads), ("gate", 2 * D)]
    by_owner = lambda b: b.reshape((N_DEV,) + b.shape[2:])
    to_owner = lambda g: g.reshape((4, 2) + g.shape[1:])
    rows_of = lambda g: to_owner(g.reshape((N_DEV, g.shape[0] // N_DEV) + g.shape[1:]))
    cols_of = lambda g: to_owner(_cols_to_slots(g, N_DEV))

    class Schedule(_NoExchange):
        late = ("bssm", "bsc", "out", "w1", "w2")
        gather_sib = dict(gnorm_fwd=("bsc", "bssm", "out"), branch_ssm=("w1", "w2"))
        scatter_sib = dict(mlp_up_dx=("w2", "w1"), branch_ssm_dx=("out", "bssm", "bsc"))
        shards = dict(bsc=w_branch_sc, bssm=w_branch_ssm, out=w_out, w1=w_mlp1, w2=w_mlp2)

        def __init__(self):
            self.W, self.staged, self.grads, self.summed, self.scatters = {}, {}, {}, {}, []
            self.token = jnp.zeros((), F32)

        def first_weights(self, bufs):
            self.W.update(zip([k for k, _ in in_cols], _win_unpack(by_owner(bufs[0]), in_cols, "win_unpack")))
            self.W.update(sc_conv_w=_slots_to_cols(by_owner(bufs[1])), ssm_conv_w=_slots_to_cols(by_owner(bufs[2])))

        def tok(self):
            return self.token

        def point(self, name, values):
            if name == "first_projection_done":
                lands = [_own_shard(self.shards[k], values[0], "own_shard_" + k) for k in self.late]
                self.gather_flight = _ici_start([], lands, True, "gather_late_start")
                self.token = self.gather_flight[4][0, 0]
                self.W["dt"] = self.W["dt"] + self.token.astype(BF16)
            elif name == "mixers_done":
                lands = _ici_wait(self.gather_flight, values, True, "gather_late_wait")
                self.staged.update(zip(self.late, lands))

        def carry(self, name):
            if name == "rms_mix":
                return _GatherBoth([w_in.astype(BF16), sc_conv_w, ssm_conv_w])
            if name in self.gather_sib:
                return _gather_sibling([self.staged.pop(k) for k in self.gather_sib[name]])
            if name in self.scatter_sib:
                return _scatter_sibling([self.grads[k] for k in self.scatter_sib[name]])
            return None

        def start_scatter(self, keys, halves):
            lands = [_own_part(h, "own_part_" + k) for k, h in zip(keys, halves)]
            flight = _ici_start(halves, lands, False, "scatter_%s_start" % keys[0])
            self.scatters.append((keys, flight))
            self.token = flight[4][0, 0]

        def carried(self, name, outs):
            if name == "rms_mix":
                self.first_weights(outs)
            elif name in self.gather_sib:
                for k, b in zip(self.gather_sib[name], outs):
                    full = by_owner(b)
                    self.W[k] = _slots_to_cols(full) if k == "w1" else full.reshape(-1, D)
            else:
                keys = self.scatter_sib[name]
                self.start_scatter(keys, [_add_halves(self.grads[k], b, "add_halves_" + k) for k, b in zip(keys, outs)])

        def grad(self, k, g):
            if k == "win":
                g = to_owner(_win_pack([g[k] for k, _ in in_cols], in_cols, N_DEV, "win_pack"))
                got = _run_comm(_scatter_sibling([g]), "scatter_sibling_win")[0]
                self.start_scatter(("win",), [_add_halves(g, got, "add_halves_win")])
            else:
                self.grads[k] = cols_of(g) if k == "w1" else rows_of(g)

        def finish_scatter(self, after):
            keys, flight = self.scatters.pop(0)
            return dict(zip(keys, _ici_wait(flight, after, False, "scatter_%s_wait" % keys[0])))

    S = Schedule()
    small = dict(norm_mix=norm_mix, b_gate=b_gate, ssm_conv_b=ssm_conv_b, dt_bias=dt_bias, A_log=A_log, D_skip=D_skip,
                 ssm_norm_w=ssm_norm_w, norm_mlp=norm_mlp, norm_final=norm_final)
    grad_x, g_small = _local_step(x.reshape(T, D), loss_target.reshape(T, D), S, small)

    small_flat = jnp.concatenate([g_small[k].reshape(-1) for k in _SMALL_ORDER])
    n_small = small_flat.shape[0]
    rows = -(-n_small // (8 * LANES)) * 8
    small_pack = jnp.pad(small_flat, (0, rows * LANES - n_small)).reshape(rows, LANES)

    res = {}
    big = [("w_in", "win", w_in, m_w_in, v_w_in), ("w_branch_sc", "bsc", w_branch_sc, m_w_branch_sc, v_w_branch_sc),
           ("w_branch_ssm", "bssm", w_branch_ssm, m_w_branch_ssm, v_w_branch_ssm), ("w_out", "out", w_out, m_w_out, v_w_out),
           ("w_mlp1", "w1", w_mlp1, m_w_mlp1, v_w_mlp1), ("w_mlp2", "w2", w_mlp2, m_w_mlp2, v_w_mlp2)]
    by_grad = {gk: (k, w, m, v) for k, gk, w, m, v in big}
    after = [grad_x]
    while S.scatters:
        for gk, parts in S.finish_scatter(after).items():
            k, w, m, v = by_grad[gk]
            if gk == "win":
                res_t, (small_parts,) = _adam(w.T, m.T, v.T, parts, "adam_" + k, comm=_gather_all([small_pack]))
                res[k] = [r.T for r in res_t]
            else:
                res[k] = _adam(w, m, v, parts, "adam_" + k)
            after = after + [res[k][1]]

    sizes = {k: g_small[k].size for k in _SMALL_ORDER}
    offs, o = {}, 0
    for k in _SMALL_ORDER:
        offs[k] = o
        o += sizes[k]
    rep_w = dict(norm_mix=norm_mix, b_gate=b_gate, ssm_conv_b=ssm_conv_b, dt_bias=dt_bias, A_log=A_log, D_skip=D_skip,
                 ssm_norm_w=ssm_norm_w, norm_mlp=norm_mlp, norm_final=norm_final)
    rep_m = dict(norm_mix=m_norm_mix, b_gate=m_b_gate, ssm_conv_b=m_ssm_conv_b, dt_bias=m_dt_bias, A_log=m_A_log, D_skip=m_D_skip,
                 ssm_norm_w=m_ssm_norm_w, norm_mlp=m_norm_mlp, norm_final=m_norm_final)
    rep_v = dict(norm_mix=v_norm_mix, b_gate=v_b_gate, ssm_conv_b=v_ssm_conv_b, dt_bias=v_dt_bias, A_log=v_A_log, D_skip=v_D_skip,
                 ssm_norm_w=v_ssm_norm_w, norm_mlp=v_norm_mlp, norm_final=v_norm_final)

    def pack(d):
        segs = [jnp.pad(d[k].astype(F32).reshape(-1), (0, sizes[k] - d[k].size)) if k in d else jnp.zeros((sizes[k],), F32)
                for k in _SMALL_ORDER]
        return jnp.pad(jnp.concatenate(segs), (0, rows * LANES - n_small)).reshape(rows, LANES)

    sm = _adam(pack(rep_w), pack(rep_m), pack(rep_v), small_parts, "adam_small")
    sm = [s.reshape(-1) for s in sm]
    for k in _REPLICATED:
        n_k = rep_w[k].shape[0]
        res[k] = tuple(s[offs[k]:offs[k] + n_k] for s in sm)
    loss = sm[0][offs["loss"]]
    for k, w, m, v, K, full in (("sc_conv_w", sc_conv_w, m_sc_conv_w, v_sc_conv_w, SC_K, D),
                                ("ssm_conv_w", ssm_conv_w, m_ssm_conv_w, v_ssm_conv_w, SSM_K, n_xbc)):
        g_full = sm[0][offs[k]:offs[k] + K * full].reshape(K, full)
        cw = full // N_DEV
        g_mine = lax.dynamic_slice_in_dim(g_full, me * cw, cw, axis=1)
        res[k] = _adam(w, m, v, g_mine[None], "adam_" + k)

    order = ("norm_mix", "w_in", "b_gate", "sc_conv_w", "ssm_conv_w", "ssm_conv_b", "dt_bias", "A_log", "D_skip", "ssm_norm_w",
             "w_branch_sc", "w_branch_ssm", "w_out", "norm_mlp", "w_mlp1", "w_mlp2", "norm_final")
    outs = [loss, grad_x.reshape(1, T, D)]
    for j in range(4):
        outs += [res[k][j] for k in order]
    return tuple(outs)
```

```python
import functools

import jax
import jax.numpy as jnp
from jax import lax
from jax.experimental import pallas as pl
from jax.experimental.pallas import tpu as pltpu

F32 = jnp.float32
BF16 = jnp.bfloat16

EPS = 1e-6
N_DEV = 8
HEADDIM = 64
NSTATE = 128
CHUNK = 128
NGROUPS = 8
GROUP_W = 256
SC_K = 3
SSM_K = 4
LANES = 128

ADAM_LR = 0.001
ADAM_B1 = 0.9
ADAM_B2 = 0.999
ADAM_EPS = 1e-08
ADAM_WD = 0.01
ADAM_STEP = 10

NN = (((1,), (0,)), ((), ()))
NT = (((1,), (1,)), ((), ()))
TN = (((0,), (0,)), ((), ()))
_DIMS = {"nn": NN, "nt": NT, "tn": TN}

ANY = pl.BlockSpec(memory_space=pl.ANY)
MESH = pl.DeviceIdType.MESH


def _sds(shape, dtype):
    return jax.ShapeDtypeStruct(tuple(shape), dtype)


def _dot(a, b, dims=NN):
    return lax.dot_general(a, b, dims, preferred_element_type=F32)


def _dot3(a, b, dims=NN):
    return lax.dot_general(a, b, dims, preferred_element_type=F32, precision=lax.Precision.HIGH)


def _params(*sem):
    return pltpu.CompilerParams(dimension_semantics=tuple(sem))


def _call(body, *, grid, in_specs, out_specs, out_shape, args, name, sem, scratch=(), comm=None):
    if comm is None:
        outs = pl.pallas_call(body, grid=grid, in_specs=list(in_specs), out_specs=list(out_specs), out_shape=list(out_shape),
                              scratch_shapes=list(scratch), name=name, compiler_params=_params(*sem))(*args)
        return list(outs), None
    n, n_in, n_out, n_scr = comm.n, len(in_specs), len(out_shape), len(scratch)

    def wrapped(*refs):
        ins, c_in = refs[:n_in], refs[n_in:n_in + n]
        outs, c_out = refs[n_in + n:n_in + n + n_out], refs[n_in + n + n_out:n_in + 2 * n + n_out]
        rest = refs[n_in + 2 * n + n_out:]
        scr, sems = rest[:n_scr], rest[n_scr:]
        first, last = None, None
        for d, g in enumerate(grid):
            f, l = pl.program_id(d) == 0, pl.program_id(d) == g - 1
            first, last = (f, l) if first is None else (first & f, last & l)

        @pl.when(first)
        def _():
            comm.start(c_in, c_out, sems)

        body(*ins, *outs, *scr)

        @pl.when(last)
        def _():
            comm.finish(c_in, c_out, sems)

    outs = pl.pallas_call(
        wrapped, grid=grid, in_specs=list(in_specs) + [ANY] * n, out_specs=list(out_specs) + [ANY] * n,
        out_shape=list(out_shape) + comm.out_shape, scratch_shapes=list(scratch) + comm.scratch,
        input_output_aliases={n_in + i: n_out + o for i, o in comm.aliases.items()},
        name=name, compiler_params=_params(*["arbitrary"] * len(grid)))(*args, *comm.arrs)
    return list(outs[:n_out]), list(outs[n_out:])


MM_VMEM_BUDGET = 44 * 2 ** 20


def _mm_tiles(M, N, k_bytes, mn_bytes):
    best = None
    for tm in (2048, 1024, 512, 256, 128):
        for tn in (1024, 512, 256, 128):
            if M % tm or N % tn:
                continue
            need = 2 * ((tm + tn) * k_bytes + tm * tn * mn_bytes) + 4 * tm * tn * 4
            if need <= MM_VMEM_BUDGET and (best is None or (tm * tn, tm) > (best[0] * best[1], best[0])):
                best = (tm, tn)
    assert best is not None, (M, N, k_bytes, mn_bytes)
    return best


def _mm(a, b, *, mode, name, extras=(), epi=None, out_dtypes=(F32,), comm=None):
    a_list = list(a) if isinstance(a, (list, tuple)) else [a]
    b_list = list(b) if isinstance(b, (list, tuple)) else [b]
    if mode == "nn":
        M, N = a_list[0].shape[0], b_list[0].shape[1]
    elif mode == "nt":
        M, N = a_list[0].shape[0], b_list[0].shape[0]
    else:
        M, N = a_list[0].shape[1], b_list[0].shape[1]
    k_bytes = sum((av.shape[0] if mode == "tn" else av.shape[1]) * av.dtype.itemsize for av in a_list)
    mn_bytes = sum(e.dtype.itemsize for e in extras) + sum(jnp.dtype(d).itemsize for d in out_dtypes)
    tm, tn = _mm_tiles(min(M, 2048), min(N, 1024), k_bytes, mn_bytes) if M % 128 == 0 and N % 128 == 0 else (M, N)
    assert M % tm == 0 and N % tn == 0
    a_specs, b_specs = [], []
    for av, bv in zip(a_list, b_list):
        K = av.shape[0] if mode == "tn" else av.shape[1]
        a_specs.append(pl.BlockSpec((K, tm), lambda i, j: (0, i)) if mode == "tn" else pl.BlockSpec((tm, K), lambda i, j: (i, 0)))
        b_specs.append(pl.BlockSpec((tn, K), lambda i, j: (j, 0)) if mode == "nt" else pl.BlockSpec((K, tn), lambda i, j: (0, j)))
    mn_spec = pl.BlockSpec((tm, tn), lambda i, j: (i, j))
    n_p, n_ex = len(a_list), len(extras)
    dims = _DIMS[mode]

    def body(*refs):
        acc = _dot(refs[0][...], refs[n_p][...], dims)
        for p in range(1, n_p):
            acc = acc + _dot(refs[p][...], refs[n_p + p][...], dims)
        rest = refs[2 * n_p:]
        res = (acc,) if epi is None else epi(acc, *[r[...] for r in rest[:n_ex]])
        for o_ref, r in zip(rest[n_ex:], res):
            o_ref[...] = r.astype(o_ref.dtype)

    outs, carried = _call(
        body, grid=(M // tm, N // tn), in_specs=a_specs + b_specs + [mn_spec] * n_ex,
        out_specs=[mn_spec] * len(out_dtypes), out_shape=[_sds((M, N), d) for d in out_dtypes],
        args=a_list + b_list + list(extras), name=name, sem=("parallel", "parallel"), comm=comm)
    res = outs[0] if len(outs) == 1 else outs
    return res if comm is None else (res, carried)


def _epi_add(acc, r):
    return (acc + r,)


def _epi_add2(acc, r):
    s = acc + r
    return (s, s)


def _epi_relu2(acc):
    p = jnp.maximum(acc, 0.0)
    return (p * p,)


def _epi_relu2_bwd(acc, r):
    return (acc * (2.0 * jnp.sqrt(r.astype(F32))),)


def _row(tr, n):
    return pl.BlockSpec((tr, n), lambda i: (i, 0))


def _vec(n):
    return pl.BlockSpec((1, n), lambda i: (0, 0))


def _rms_fwd(x, w, name, comm=None):
    T, D = x.shape
    tr = min(256, T)

    def body(x_ref, w_ref, o_ref):
        xv = x_ref[...]
        r = lax.rsqrt(jnp.mean(xv * xv, axis=-1, keepdims=True) + EPS)
        o_ref[...] = (xv * r * w_ref[...]).astype(BF16)

    outs, carried = _call(body, grid=(T // tr,), in_specs=[_row(tr, D), _vec(D)], out_specs=[_row(tr, D)],
                          out_shape=[_sds((T, D), BF16)], args=[x, w], name=name, sem=("parallel",), comm=comm)
    return outs[0] if comm is None else (outs[0], carried)


def _rms_bwd(x, w, dh, dres, name):
    T, D = x.shape
    tr = min(256, T)

    def body(x_ref, w_ref, dh_ref, dres_ref, dx_ref, dxb_ref, dw_ref):
        @pl.when(pl.program_id(0) == 0)
        def _():
            dw_ref[...] = jnp.zeros_like(dw_ref)

        xv = x_ref[...]
        r = lax.rsqrt(jnp.mean(xv * xv, axis=-1, keepdims=True) + EPS)
        xh = xv * r
        dh_v = dh_ref[...]
        dw_ref[...] += jnp.sum(dh_v * xh, axis=0, keepdims=True)
        dxh = dh_v * w_ref[...]
        dx = r * (dxh - xh * jnp.mean(dxh * xh, axis=-1, keepdims=True)) + dres_ref[...]
        dx_ref[...] = dx
        dxb_ref[...] = dx.astype(BF16)

    return pl.pallas_call(
        body, grid=(T // tr,), in_specs=[_row(tr, D), _vec(D), _row(tr, D), _row(tr, D)],
        out_specs=[_row(tr, D), _row(tr, D), _vec(D)],
        out_shape=[_sds((T, D), F32), _sds((T, D), BF16), _sds((1, D), F32)],
        name=name, compiler_params=_params("arbitrary"))(x, w, dh, dres)


def _final(x2, w, tgt, name):
    T, D = x2.shape
    tr = min(256, T)

    def body(x_ref, w_ref, t_ref, dx_ref, dxb_ref, dw_ref, loss_ref):
        @pl.when(pl.program_id(0) == 0)
        def _():
            dw_ref[...] = jnp.zeros_like(dw_ref)
            loss_ref[...] = jnp.zeros_like(loss_ref)

        xv = x_ref[...]
        wv = w_ref[...]
        r = lax.rsqrt(jnp.mean(xv * xv, axis=-1, keepdims=True) + EPS)
        xh = xv * r
        err = xh * wv - t_ref[...]
        part = jnp.sum(jnp.sum(err * err, axis=1, keepdims=True), axis=0, keepdims=True) * (0.5 / D)
        loss_ref[...] += jnp.broadcast_to(part, loss_ref.shape)
        dy = err * (1.0 / D)
        dw_ref[...] += jnp.sum(dy * xh, axis=0, keepdims=True)
        dxh = dy * wv
        dx = r * (dxh - xh * jnp.mean(dxh * xh, axis=-1, keepdims=True))
        dx_ref[...] = dx
        dxb_ref[...] = dx.astype(BF16)

    return pl.pallas_call(
        body, grid=(T // tr,), in_specs=[_row(tr, D), _vec(D), _row(tr, D)],
        out_specs=[_row(tr, D), _row(tr, D), _vec(D), _vec(LANES)],
        out_shape=[_sds((T, D), F32), _sds((T, D), BF16), _sds((1, D), F32), _sds((1, LANES), F32)],
        name=name, compiler_params=_params("arbitrary"))(x2, w, tgt)


def _silu_parts(z):
    s = jax.nn.sigmoid(z)
    return z * s, s * (1.0 + z * (1.0 - s))


def _gnorm_fwd(y, z, w, name, comm=None):
    T, N = y.shape
    tr = min(256, T)

    def body(y_ref, z_ref, w_ref, o_ref):
        for g in range(N // GROUP_W):
            sl = slice(g * GROUP_W, (g + 1) * GROUP_W)
            silu, _ = _silu_parts(z_ref[:, sl])
            yz = y_ref[:, sl] * silu
            r = lax.rsqrt(jnp.mean(yz * yz, axis=-1, keepdims=True) + EPS)
            o_ref[:, sl] = (yz * r * w_ref[:, sl]).astype(BF16)

    outs, carried = _call(body, grid=(T // tr,), in_specs=[_row(tr, N), _row(tr, N), _vec(N)], out_specs=[_row(tr, N)],
                          out_shape=[_sds((T, N), BF16)], args=[y, z, w], name=name, sem=("parallel",), comm=comm)
    return outs[0] if comm is None else (outs[0], carried)


def _gnorm_bwd(y, z, w, dyb, name):
    T, N = y.shape
    tr = min(256, T)

    def body(y_ref, z_ref, w_ref, d_ref, dy_ref, dz_ref, dw_ref):
        @pl.when(pl.program_id(0) == 0)
        def _():
            dw_ref[...] = jnp.zeros_like(dw_ref)

        for g in range(N // GROUP_W):
            sl = slice(g * GROUP_W, (g + 1) * GROUP_W)
            yv = y_ref[:, sl]
            silu, dsilu = _silu_parts(z_ref[:, sl])
            yz = yv * silu
            r = lax.rsqrt(jnp.mean(yz * yz, axis=-1, keepdims=True) + EPS)
            yzh = yz * r
            d = d_ref[:, sl]
            dw_ref[:, sl] += jnp.sum(d * yzh, axis=0, keepdims=True)
            dyzh = d * w_ref[:, sl]
            dyz = r * (dyzh - yzh * jnp.mean(dyzh * yzh, axis=-1, keepdims=True))
            dy_ref[:, sl] = dyz * silu
            dz_ref[:, sl] = (dyz * yv * dsilu).astype(BF16)

    return pl.pallas_call(
        body, grid=(T // tr,), in_specs=[_row(tr, N), _row(tr, N), _vec(N), _row(tr, N)],
        out_specs=[_row(tr, N), _row(tr, N), _vec(N)],
        out_shape=[_sds((T, N), F32), _sds((T, N), BF16), _sds((1, N), F32)],
        name=name, compiler_params=_params("arbitrary"))(y, z, w, dyb)


def _merge_fwd(gate_raw, b_gate, br_a, br_b, name):
    T, D = br_a.shape
    tr = min(256, T)

    def body(g_ref, bg_ref, a_ref, b_ref, o_ref):
        g = jax.nn.sigmoid(g_ref[...] + bg_ref[...])
        o_ref[...] = (g[:, :D] * a_ref[...] + g[:, D:] * b_ref[...]).astype(BF16)

    return pl.pallas_call(body, grid=(T // tr,), in_specs=[_row(tr, 2 * D), _vec(2 * D), _row(tr, D), _row(tr, D)],
                          out_specs=_row(tr, D), out_shape=_sds((T, D), BF16), name=name,
                          compiler_params=_params("parallel"))(gate_raw, b_gate, br_a, br_b)


def _merge_bwd(dmerged, gate_raw, b_gate, br_a, br_b, name):
    T, D = br_a.shape
    tr = min(256, T)

    def body(d_ref, g_ref, bg_ref, a_ref, b_ref, da_ref, db_ref, dg_ref, dbg_ref):
        @pl.when(pl.program_id(0) == 0)
        def _():
            dbg_ref[...] = jnp.zeros_like(dbg_ref)

        g = jax.nn.sigmoid(g_ref[...] + bg_ref[...])
        d = d_ref[...]
        da_ref[...] = (d * g[:, :D]).astype(BF16)
        db_ref[...] = (d * g[:, D:]).astype(BF16)
        dg = jnp.concatenate([d * a_ref[...], d * b_ref[...]], axis=1) * g * (1.0 - g)
        dg_ref[...] = dg.astype(BF16)
        dbg_ref[...] += jnp.sum(dg, axis=0, keepdims=True)

    return pl.pallas_call(
        body, grid=(T // tr,), in_specs=[_row(tr, D), _row(tr, 2 * D), _vec(2 * D), _row(tr, D), _row(tr, D)],
        out_specs=[_row(tr, D), _row(tr, D), _row(tr, 2 * D), _vec(2 * D)],
        out_shape=[_sds((T, D), BF16), _sds((T, D), BF16), _sds((T, 2 * D), BF16), _sds((1, 2 * D), F32)],
        name=name, compiler_params=_params("arbitrary"))(dmerged, gate_raw, b_gate, br_a, br_b)


CB_W = 256
CONV_ROWS = 32
CONV_PAD = 8


def _rows_down(load, r0, s):
    if s == 0:
        return load(r0, r0 + CONV_ROWS)
    if r0 == 0:
        row = lax.broadcasted_iota(jnp.int32, (CONV_ROWS, CB_W), 0)
        return jnp.where(row >= s, pltpu.roll(load(0, CONV_ROWS), s, 0), 0.0)
    return load(r0 - s, r0 - s + CONV_ROWS)


def _conv_tile(load, taps, r0):
    K = len(taps)
    us = [_rows_down(load, r0, K - 1 - k) for k in range(K)]
    acc = us[K - 1] * taps[K - 1]
    for k in range(K - 1):
        acc = acc + us[k] * taps[k]
    return acc, us


def _conv_back_tile(scr, taps, r0):
    K = len(taps)
    du = scr[r0:r0 + CONV_ROWS, :] * taps[K - 1]
    for k in range(K - 1):
        s = K - 1 - k
        du = du + scr[r0 + s:r0 + s + CONV_ROWS, :] * taps[k]
    return du


def _fold8(v):
    return jnp.sum(v.reshape(CONV_ROWS // 8, 8, v.shape[1]), axis=0)


def _col(T, j0=0):
    return pl.BlockSpec((T, CB_W), lambda j: (0, j + j0))


def _sc_fwd(psc, w, name):
    T, D = psc.shape[0], psc.shape[1] // 3
    nb = D // CB_W

    def body(b_ref, c_ref, x_ref, w_ref, o_ref):
        taps = [w_ref[k:k + 1, :] for k in range(SC_K)]
        load = lambda a, b: c_ref[a:b, :] * x_ref[a:b, :]
        for r0 in range(0, T, CONV_ROWS):
            cu, _ = _conv_tile(load, taps, r0)
            o_ref[r0:r0 + CONV_ROWS, :] = (b_ref[r0:r0 + CONV_ROWS, :] * cu).astype(BF16)

    return pl.pallas_call(
        body, grid=(nb,), in_specs=[_col(T), _col(T, nb), _col(T, 2 * nb), pl.BlockSpec((SC_K, CB_W), lambda j: (0, j))],
        out_specs=_col(T), out_shape=_sds((T, D), BF16), name=name, compiler_params=_params("parallel"))(psc, psc, psc, w)


def _sc_bwd(psc, w, dya, name):
    T, D = psc.shape[0], psc.shape[1] // 3
    nb = D // CB_W

    def body(b_ref, c_ref, x_ref, w_ref, d_ref, db_ref, dc_ref, dx_ref, dw_ref, scr):
        taps = [w_ref[k:k + 1, :] for k in range(SC_K)]
        load = lambda a, b: c_ref[a:b, :] * x_ref[a:b, :]
        scr[T:T + CONV_PAD, :] = jnp.zeros((CONV_PAD, CB_W), F32)
        dw8 = [jnp.zeros((8, CB_W), F32)] * SC_K
        for r0 in range(0, T, CONV_ROWS):
            rows = slice(r0, r0 + CONV_ROWS)
            cu, us = _conv_tile(load, taps, r0)
            d = d_ref[rows, :]
            db_ref[rows, :] = (d * cu).astype(BF16)
            dcu = d * b_ref[rows, :]
            scr[rows, :] = dcu
            dw8 = [acc + _fold8(dcu * u) for acc, u in zip(dw8, us)]
        for k in range(SC_K):
            dw_ref[k:k + 1, :] = jnp.sum(dw8[k], axis=0, keepdims=True)
        for r0 in range(0, T, CONV_ROWS):
            rows = slice(r0, r0 + CONV_ROWS)
            du = _conv_back_tile(scr, taps, r0)
            dc_ref[rows, :] = (du * x_ref[rows, :]).astype(BF16)
            dx_ref[rows, :] = (du * c_ref[rows, :]).astype(BF16)

    wspec = pl.BlockSpec((SC_K, CB_W), lambda j: (0, j))
    return pl.pallas_call(
        body, grid=(nb,), in_specs=[_col(T), _col(T, nb), _col(T, 2 * nb), wspec, _col(T)],
        out_specs=[_col(T), _col(T), _col(T), wspec],
        out_shape=[_sds((T, D), BF16)] * 3 + [_sds((SC_K, D), F32)],
        scratch_shapes=[pltpu.VMEM((T + CONV_PAD, CB_W), F32)],
        name=name, compiler_params=_params("parallel"))(psc, psc, psc, w, dya)


def _ssm_conv_fwd(u, w, b, name, comm=None):
    T, N = u.shape

    def body(u_ref, w_ref, b_ref, o_ref):
        taps = [w_ref[k:k + 1, :] for k in range(SSM_K)]
        bias = b_ref[...]
        for r0 in range(0, T, CONV_ROWS):
            c, _ = _conv_tile(lambda a, b: u_ref[a:b, :], taps, r0)
            c = c + bias
            o_ref[r0:r0 + CONV_ROWS, :] = c * jax.nn.sigmoid(c)

    outs, carried = _call(
        body, grid=(N // CB_W,), in_specs=[_col(T), pl.BlockSpec((SSM_K, CB_W), lambda j: (0, j)), pl.BlockSpec((1, CB_W), lambda j: (0, j))],
        out_specs=[_col(T)], out_shape=[_sds((T, N), F32)], args=[u, w, b], name=name, sem=("parallel",), comm=comm)
    return outs[0] if comm is None else (outs[0], carried)


def _ssm_conv_bwd(u, w, b, dxs, dB, dC, name, comm=None):
    T, N = u.shape
    n_x, n_b = dxs.shape[1] // CB_W, dB.shape[1] // CB_W

    def body(u_ref, w_ref, b_ref, dx_ref, db_ref, dc_ref, du_ref, dw_ref, dbias_ref, scr):
        j = pl.program_id(0)
        taps = [w_ref[k:k + 1, :] for k in range(SSM_K)]
        bias = b_ref[...]
        scr[T:T + CONV_PAD, :] = jnp.zeros((CONV_PAD, CB_W), F32)
        dw8 = [jnp.zeros((8, CB_W), F32)] * SSM_K
        db8 = jnp.zeros((8, CB_W), F32)
        for r0 in range(0, T, CONV_ROWS):
            rows = slice(r0, r0 + CONV_ROWS)
            c, us = _conv_tile(lambda a, b: u_ref[a:b, :], taps, r0)
            _, dsilu = _silu_parts(c + bias)
            d = jnp.where(j < n_x, dx_ref[rows, :], jnp.where(j < n_x + n_b, db_ref[rows, :], dc_ref[rows, :])) * dsilu
            scr[rows, :] = d
            db8 = db8 + _fold8(d)
            dw8 = [acc + _fold8(d * u) for acc, u in zip(dw8, us)]
        dbias_ref[...] = jnp.sum(db8, axis=0, keepdims=True)
        for k in range(SSM_K):
            dw_ref[k:k + 1, :] = jnp.sum(dw8[k], axis=0, keepdims=True)
        for r0 in range(0, T, CONV_ROWS):
            du_ref[r0:r0 + CONV_ROWS, :] = _conv_back_tile(scr, taps, r0).astype(BF16)

    wspec = pl.BlockSpec((SSM_K, CB_W), lambda j: (0, j))
    bspec = pl.BlockSpec((1, CB_W), lambda j: (0, j))
    outs, carried = _call(
        body, grid=(N // CB_W,),
        in_specs=[_col(T), wspec, bspec,
                  pl.BlockSpec((T, CB_W), lambda j: (0, jnp.minimum(j, n_x - 1))),
                  pl.BlockSpec((T, CB_W), lambda j: (0, jnp.clip(j - n_x, 0, n_b - 1))),
                  pl.BlockSpec((T, CB_W), lambda j: (0, jnp.clip(j - n_x - n_b, 0, n_b - 1)))],
        out_specs=[_col(T), wspec, bspec],
        out_shape=[_sds((T, N), BF16), _sds((SSM_K, N), F32), _sds((1, N), F32)],
        scratch=[pltpu.VMEM((T + CONV_PAD, CB_W), F32)],
        args=[u, w, b, dxs, dB, dC], name=name, sem=("parallel",), comm=comm)
    return outs if comm is None else (outs, carried)


def _split3(v):
    hi = v.astype(BF16)
    r = v - hi.astype(F32)
    mid = r.astype(BF16)
    lo = (r - mid.astype(F32)).astype(BF16)
    return hi, mid, lo


def _head_expand(n_lanes):
    h = lax.broadcasted_iota(jnp.int32, (LANES, n_lanes), 0)
    l = lax.broadcasted_iota(jnp.int32, (LANES, n_lanes), 1)
    return (jnp.right_shift(l, HEADDIM.bit_length() - 1) == h).astype(BF16)


def _softplus(v):
    return jnp.maximum(v, 0.0) + jnp.log1p(jnp.exp(-jnp.abs(v)))


def _ssd_prep(dt_raw, dt_bias, a_log, n_inner, name):
    T = dt_raw.shape[0]

    def body(r_ref, b_ref, al_ref, dt_ref, cs_ref):
        dt = _softplus(r_ref[...] + b_ref[...])
        a = dt * (-jnp.exp(al_ref[...]))
        i = lax.broadcasted_iota(jnp.int32, (CHUNK, CHUNK), 0)
        j = lax.broadcasted_iota(jnp.int32, (CHUNK, CHUNK), 1)
        tri = (j <= i).astype(BF16)
        cs = sum(_dot(tri, p) for p in _split3(a))
        ex = _head_expand(n_inner)
        dt_ref[...] = sum(_dot(p, ex) for p in _split3(dt))
        cs_ref[...] = sum(_dot(p, ex) for p in _split3(cs))

    blk = pl.BlockSpec((CHUNK, LANES), lambda c: (c, 0))
    out = pl.BlockSpec((CHUNK, n_inner), lambda c: (c, 0))
    return pl.pallas_call(body, grid=(T // CHUNK,), in_specs=[blk, _vec(LANES), _vec(LANES)], out_specs=[out, out],
                          out_shape=[_sds((T, n_inner), F32)] * 2, name=name, compiler_params=_params("parallel"))(dt_raw, dt_bias, a_log)


def _pair_terms(cs_p):
    lane = lax.broadcasted_iota(jnp.int32, (CHUNK, CHUNK), 1)
    sub = lax.broadcasted_iota(jnp.int32, (CHUNK, CHUNK), 0)
    csT = cs_p.T
    Ls = []
    for k in range(2):
        col = jnp.sum(jnp.where(lane == k * HEADDIM, cs_p, 0.0), axis=1, keepdims=True)
        rowv = csT[k * HEADDIM:k * HEADDIM + 1, :]
        Ls.append(jnp.exp(jnp.where(sub >= lane, col - rowv, -jnp.inf)))
    return Ls, jnp.exp(csT[:, CHUNK - 1:CHUNK])


def _block_diag(xp):
    lane = lax.broadcasted_iota(jnp.int32, xp.shape, 1)
    return jnp.concatenate([jnp.where(lane < HEADDIM, xp, 0.0), jnp.where(lane >= HEADDIM, xp, 0.0)], axis=0)


SSD_GROUPS_PER_STEP = 8


def _ssd_specs(T, n_inner):
    nc, gs = T // CHUNK, SSD_GROUPS_PER_STEP
    bo, co = n_inner // (gs * NSTATE), (n_inner + NGROUPS * NSTATE) // (gs * NSTATE)
    assert NGROUPS % gs == 0 and n_inner % (gs * NSTATE) == 0 and (NGROUPS * NSTATE) % (gs * NSTATE) == 0
    g_blk = lambda f: pl.BlockSpec((CHUNK, gs * GROUP_W), lambda c, s: (f(c), s))
    b_blk = lambda f: pl.BlockSpec((CHUNK, gs * NSTATE), lambda c, s: (f(c), bo + s))
    c_blk = lambda f: pl.BlockSpec((CHUNK, gs * NSTATE), lambda c, s: (f(c), co + s))
    return nc, g_blk, b_blk, c_blk


def _ssd_fwd(xbc, dt_e, cs_e, d_e, name, comm=None):
    T = xbc.shape[0]
    n_inner = dt_e.shape[1]
    nc, g_blk, b_blk, c_blk = _ssd_specs(T, n_inner)
    ident = lambda c: c

    gs = SSD_GROUPS_PER_STEP

    def body(xs_ref, b_ref, c_ref, dt_ref, cs_ref, d_ref, y_ref, p_ref, st):
        c, s = pl.program_id(0), pl.program_id(1)

        @pl.when(c == 0)
        def _():
            for gi in range(gs):
                st[s * gs + gi] = jnp.zeros((GROUP_W, NSTATE), F32)

        for gi in range(gs):
            g = s * gs + gi
            gw, gn = slice(gi * GROUP_W, (gi + 1) * GROUP_W), slice(gi * NSTATE, (gi + 1) * NSTATE)
            P = st[g]
            p_ref[0, gi] = P
            xs, dt, cs = xs_ref[:, gw], dt_ref[:, gw], cs_ref[:, gw]
            Bf, Cf = b_ref[:, gn], c_ref[:, gn]
            Cb = Cf.astype(BF16)
            CBm = _dot(Cb, Bf.astype(BF16), NT)
            X = xs * dt
            decay = jnp.exp(cs[CHUNK - 1:CHUNK, :] - cs)
            y_off = _dot(Cb, P.astype(BF16), NT) * jnp.exp(cs)
            ys, ecl = [], []
            for pr in range(2):
                sl = slice(pr * LANES, (pr + 1) * LANES)
                Ls, e_last = _pair_terms(cs[:, sl])
                ecl.append(e_last)
                Mcat = jnp.concatenate([(CBm * L).astype(BF16) for L in Ls], axis=1)
                ys.append(_dot(Mcat, _block_diag(X[:, sl]).astype(BF16)))
            y_ref[:, gw] = jnp.concatenate(ys, axis=1) + y_off + xs * d_ref[:, gw]
            S = _dot3(X * decay, Bf, TN)
            st[g] = P * jnp.concatenate(ecl, axis=0) + S

    p_blk = pl.BlockSpec((1, gs, GROUP_W, NSTATE), lambda c, s: (c, s, 0, 0))
    outs, carried = _call(
        body, grid=(nc, NGROUPS // gs),
        in_specs=[g_blk(ident), b_blk(ident), c_blk(ident), g_blk(ident), g_blk(ident), pl.BlockSpec((1, gs * GROUP_W), lambda c, s: (0, s))],
        out_specs=[g_blk(ident), p_blk],
        out_shape=[_sds((T, n_inner), F32), _sds((nc, NGROUPS, GROUP_W, NSTATE), F32)],
        scratch=[pltpu.VMEM((NGROUPS, GROUP_W, NSTATE), F32)],
        args=[xbc, xbc, xbc, dt_e, cs_e, d_e], name=name, sem=("arbitrary", "arbitrary"), comm=comm)
    return outs if comm is None else (outs, carried)


def _ssd_bwd(xbc, dt_e, cs_e, d_e, states, dy, name, comm=None):
    T = xbc.shape[0]
    n_inner = dt_e.shape[1]
    nc, g_blk, b_blk, c_blk = _ssd_specs(T, n_inner)
    rev = lambda c: nc - 1 - c

    gs = SSD_GROUPS_PER_STEP

    def body(xs_ref, b_ref, c_ref, dt_ref, cs_ref, d_ref, p_ref, pn_ref, dy_ref,
             dxs_ref, db_ref, dc_ref, ddt_ref, dcs_ref, dd_ref, dst):
        cc, s = pl.program_id(0), pl.program_id(1)

        @pl.when(cc == 0)
        def _():
            for gi in range(gs):
                dst[s * gs + gi] = jnp.zeros((GROUP_W, NSTATE), F32)

        for gi in range(gs):
            one_group(s * gs + gi, gi, xs_ref, b_ref, c_ref, dt_ref, cs_ref, d_ref, p_ref, pn_ref, dy_ref,
                      dxs_ref, db_ref, dc_ref, ddt_ref, dcs_ref, dd_ref, dst)

    def one_group(g, gi, xs_ref, b_ref, c_ref, dt_ref, cs_ref, d_ref, p_ref, pn_ref, dy_ref,
                  dxs_ref, db_ref, dc_ref, ddt_ref, dcs_ref, dd_ref, dst):
        gw, gn = slice(gi * GROUP_W, (gi + 1) * GROUP_W), slice(gi * NSTATE, (gi + 1) * NSTATE)
        dS = dst[g]
        P, Pn = p_ref[0, gi], pn_ref[0, gi]
        xs, dt, cs, dY = xs_ref[:, gw], dt_ref[:, gw], cs_ref[:, gw], dy_ref[:, gw]
        Bf, Cf = b_ref[:, gn], c_ref[:, gn]
        Bb, Cb = Bf.astype(BF16), Cf.astype(BF16)
        X = xs * dt
        ecs = jnp.exp(cs)
        decay = jnp.exp(cs[CHUNK - 1:CHUNK, :] - cs)
        CBm = _dot(Cb, Bb, NT)
        dYe = dY * ecs
        dYeb, Pb = dYe.astype(BF16), P.astype(BF16)
        dP_off = _dot(dYeb, Cb, TN)
        dC = _dot(dYeb, Pb)
        dcs = dYe * _dot(Cb, Pb, NT)
        Xd = X * decay
        dB = _dot(Xd.astype(BF16), dS.astype(BF16))
        E = _dot3(Bf, dS, NT)
        dX = E * decay
        dcs = dcs - E * Xd
        R = _dot3(jnp.ones((8, NSTATE), F32), dS * Pn, NT)
        sub_g = lax.broadcasted_iota(jnp.int32, (CHUNK, GROUP_W), 0)
        dcs = dcs + jnp.where(sub_g == CHUNK - 1, R[0:1, :], 0.0)
        lane = lax.broadcasted_iota(jnp.int32, (CHUNK, CHUNK), 1)
        sub = lax.broadcasted_iota(jnp.int32, (CHUNK, CHUNK), 0)
        dCB = jnp.zeros((CHUNK, CHUNK), F32)
        dXs, dcss, ecl = [], [], []
        for pr in range(2):
            sl = slice(pr * LANES, (pr + 1) * LANES)
            Ls, e_last = _pair_terms(cs[:, sl])
            ecl.append(e_last)
            dYpb = dY[:, sl].astype(BF16)
            dMcat = _dot(dYpb, _block_diag(X[:, sl]).astype(BF16), NT)
            Mcat = jnp.concatenate([(CBm * L).astype(BF16) for L in Ls], axis=1)
            dXt = _dot(Mcat, dYpb, TN)
            dXs.append(jnp.where(lane < HEADDIM, dXt[:CHUNK], dXt[CHUNK:]))
            colacc = jnp.zeros((CHUNK, CHUNK), F32)
            rowacc = jnp.zeros((CHUNK, CHUNK), F32)
            for k in range(2):
                dG = dMcat[:, k * CHUNK:(k + 1) * CHUNK] * Ls[k]
                dCB = dCB + dG
                Q = dG * CBm
                colacc = colacc + jnp.where(lane == k * HEADDIM, jnp.sum(Q, axis=1, keepdims=True), 0.0)
                rowacc = rowacc + jnp.where(sub == k * HEADDIM, jnp.sum(Q, axis=0, keepdims=True), 0.0)
            dcss.append(colacc - rowacc.T)
        dX = dX + jnp.concatenate(dXs, axis=1)
        dcs = dcs + jnp.concatenate(dcss, axis=1)
        dCBb = dCB.astype(BF16)
        dc_ref[:, gn] = dC + _dot(dCBb, Bb)
        db_ref[:, gn] = dB + _dot(dCBb, Cb, TN)
        dxs_ref[:, gw] = dX * dt + dY * d_ref[:, gw]
        ddt_ref[:, gw] = dX * xs
        dcs_ref[:, gw] = dcs
        dd_ref[0, :, gw] = jnp.sum(dY * xs, axis=0, keepdims=True)
        dst[g] = dS * jnp.concatenate(ecl, axis=0) + dP_off

    p_blk = pl.BlockSpec((1, gs, GROUP_W, NSTATE), lambda c, s: (nc - 1 - c, s, 0, 0))
    pn_blk = pl.BlockSpec((1, gs, GROUP_W, NSTATE), lambda c, s: (jnp.minimum(nc - c, nc - 1), s, 0, 0))
    st_blk = pl.BlockSpec((CHUNK, gs * NSTATE), lambda c, s: (nc - 1 - c, s))
    outs, carried = _call(
        body, grid=(nc, NGROUPS // gs),
        in_specs=[g_blk(rev), b_blk(rev), c_blk(rev), g_blk(rev), g_blk(rev), pl.BlockSpec((1, gs * GROUP_W), lambda c, s: (0, s)),
                  p_blk, pn_blk, g_blk(rev)],
        out_specs=[g_blk(rev), st_blk, st_blk, g_blk(rev), g_blk(rev), pl.BlockSpec((1, 1, gs * GROUP_W), lambda c, s: (nc - 1 - c, 0, s))],
        out_shape=[_sds((T, n_inner), F32), _sds((T, NGROUPS * NSTATE), F32), _sds((T, NGROUPS * NSTATE), F32),
                   _sds((T, n_inner), F32), _sds((T, n_inner), F32), _sds((nc, 1, n_inner), F32)],
        scratch=[pltpu.VMEM((NGROUPS, GROUP_W, NSTATE), F32)],
        args=[xbc, xbc, xbc, dt_e, cs_e, d_e, states, states, dy], name=name, sem=("arbitrary", "arbitrary"), comm=comm)
    return outs if comm is None else (outs, carried)


def _ssd_post(ddt_e, dcs_e, dd_p, dt_raw, dt_bias, a_log, n_heads, name):
    T, n_inner = ddt_e.shape

    def body(ddt_ref, dcs_ref, dd_ref, r_ref, b_ref, al_ref, draw_ref, dbias_ref, dal_ref, ddsk_ref):
        @pl.when(pl.program_id(0) == 0)
        def _():
            dbias_ref[...] = jnp.zeros_like(dbias_ref)
            dal_ref[...] = jnp.zeros_like(dal_ref)
            ddsk_ref[...] = jnp.zeros_like(ddsk_ref)

        ex = _head_expand(n_inner)
        red = lambda v: sum(_dot(p, ex, NT) for p in _split3(v))
        raw = r_ref[...] + b_ref[...]
        dt = _softplus(raw)
        A = -jnp.exp(al_ref[...])
        i = lax.broadcasted_iota(jnp.int32, (CHUNK, CHUNK), 0)
        j = lax.broadcasted_iota(jnp.int32, (CHUNK, CHUNK), 1)
        upper = (j >= i).astype(BF16)
        da = sum(_dot(upper, p) for p in _split3(red(dcs_ref[...])))
        ddt = red(ddt_ref[...]) + da * A
        lane = lax.broadcasted_iota(jnp.int32, (CHUNK, LANES), 1)
        draw = jnp.where(lane < n_heads, ddt * jax.nn.sigmoid(raw), 0.0)
        draw_ref[...] = draw.astype(BF16)
        dbias_ref[...] += jnp.sum(draw, axis=0, keepdims=True)
        dal_ref[...] += jnp.sum(da * dt, axis=0, keepdims=True) * A
        ddsk_ref[...] += red(jnp.broadcast_to(dd_ref[0], (8, n_inner)))[0:1, :]

    wide = pl.BlockSpec((CHUNK, n_inner), lambda c: (c, 0))
    blk = pl.BlockSpec((CHUNK, LANES), lambda c: (c, 0))
    return pl.pallas_call(
        body, grid=(T // CHUNK,),
        in_specs=[wide, wide, pl.BlockSpec((1, 1, n_inner), lambda c: (c, 0, 0)), blk, _vec(LANES), _vec(LANES)],
        out_specs=[blk, _vec(LANES), _vec(LANES), _vec(LANES)],
        out_shape=[_sds((T, LANES), BF16)] + [_sds((1, LANES), F32)] * 3,
        name=name, compiler_params=_params("arbitrary"))(ddt_e, dcs_e, dd_p, dt_raw, dt_bias, a_log)


def _row2(v):
    return v.reshape(1, -1).astype(F32)


def _pad_lanes(v):
    return jnp.pad(_row2(v), ((0, 0), (0, LANES - v.shape[-1])))


class _NoExchange:
    def __init__(self, W):
        self.W, self.grads = W, {}

    def weight(self, k):
        return self.W[k]

    def carry(self, name):
        return None

    def carried(self, name, outs):
        pass

    def grad(self, k, g):
        self.grads[k] = g

    def tok(self):
        return jnp.zeros((), F32)

    def point(self, name, value):
        pass


def _local_step(x, tgt, S, small):
    T, D = x.shape

    def mm(a, b, *, name, **kw):
        comm = S.carry(name)
        if comm is None:
            return _mm(a, b, name=name, **kw)
        res, outs = _mm(a, b, name=name, comm=comm, **kw)
        S.carried(name, outs)
        return res

    def carrying(fn, *args, name):
        comm = S.carry(name)
        if comm is None:
            return fn(*args, name)
        res, outs = fn(*args, name, comm=comm)
        S.carried(name, outs)
        return res

    n_inner = 2 * D
    n_heads = n_inner // HEADDIM
    norm_mix, norm_mlp, norm_final = _row2(small["norm_mix"]), _row2(small["norm_mlp"]), _row2(small["norm_final"])
    b_gate, ssm_b, ssm_norm_w = _row2(small["b_gate"]), _row2(small["ssm_conv_b"]), _row2(small["ssm_norm_w"])
    dt_bias, a_log = _pad_lanes(small["dt_bias"]), _pad_lanes(small["A_log"])
    d_e = jnp.repeat(small["D_skip"].astype(F32), HEADDIM).reshape(1, n_inner)

    hb = carrying(_rms_fwd, x, norm_mix, name="rms_mix")
    sc_w, ssm_w = S.weight("sc_conv_w"), S.weight("ssm_conv_w")
    p_xbc = mm(hb, S.weight("xbc"), mode="nn", name="proj_xbc")
    S.point("first_projection_done", [p_xbc])
    p_dt = mm(hb, S.weight("dt"), mode="nn", name="proj_dt")
    p_z = mm(hb, S.weight("z"), mode="nn", name="proj_z")
    p_sc = mm(hb, S.weight("sc"), mode="nn", name="proj_sc")
    p_gate = mm(hb, S.weight("gate"), mode="nn", name="proj_gate")
    xbc = carrying(_ssm_conv_fwd, p_xbc, ssm_w, ssm_b, name="ssm_conv_fwd")
    dt_e, cs_e = _ssd_prep(p_dt, dt_bias, a_log, n_inner, "ssd_prep")
    ya = _sc_fwd(p_sc, sc_w, "sc_fwd")
    y, states = carrying(_ssd_fwd, xbc, dt_e, cs_e, d_e, name="ssd_fwd")
    S.point("mixers_done", [y, ya, p_gate])
    yb = carrying(_gnorm_fwd, y, p_z, ssm_norm_w, name="gnorm_fwd")
    br_a = mm(ya, S.weight("bsc"), mode="nn", name="branch_sc")
    br_b = mm(yb, S.weight("bssm"), mode="nn", name="branch_ssm")
    merged = _merge_fwd(p_gate, b_gate, br_a, br_b, "merge_fwd")
    x1 = mm(merged, S.weight("out"), mode="nn", name="out_proj", extras=(x,), epi=_epi_add)
    h2 = _rms_fwd(x1, norm_mlp, "rms_mlp")
    r_act = mm(h2, S.weight("w1"), mode="nn", name="mlp_up", epi=_epi_relu2, out_dtypes=(BF16,))
    x2 = mm(r_act, S.weight("w2"), mode="nn", name="mlp_down", extras=(x1,), epi=_epi_add)
    dx2, dx2b, g_norm_final, loss_row = _final(x2, norm_final, tgt, "final")

    S.grad("w2", mm(r_act, dx2b, mode="tn", name="mlp_down_dw", out_dtypes=(BF16,)))
    da = mm(dx2b, S.weight("w2"), mode="nt", name="mlp_down_dx", extras=(r_act,), epi=_epi_relu2_bwd, out_dtypes=(BF16,))
    S.grad("w1", mm(h2, da, mode="tn", name="mlp_up_dw", out_dtypes=(BF16,)))
    dh2 = mm(da, S.weight("w1"), mode="nt", name="mlp_up_dx")
    dx1, dx1b, g_norm_mlp = _rms_bwd(x1, norm_mlp + S.tok(), dh2, dx2, "rms_mlp_bwd")
    S.grad("out", mm(merged, dx1b, mode="tn", name="out_proj_dw", out_dtypes=(BF16,)))
    dmerged = mm(dx1b, S.weight("out"), mode="nt", name="out_proj_dx")
    dbr_a, dbr_b, d_gate, g_b_gate = _merge_bwd(dmerged, p_gate, b_gate, br_a, br_b, "merge_bwd")
    S.grad("bssm", mm(yb, dbr_b, mode="tn", name="branch_ssm_dw", out_dtypes=(BF16,)))
    S.grad("bsc", mm(ya, dbr_a, mode="tn", name="branch_sc_dw", out_dtypes=(BF16,)))
    dyb = mm(dbr_b, S.weight("bssm"), mode="nt", name="branch_ssm_dx")
    dya = mm(dbr_a, S.weight("bsc"), mode="nt", name="branch_sc_dx")
    dy, d_z, g_ssm_norm_w = _gnorm_bwd(y, p_z, ssm_norm_w + S.tok(), dyb, "gnorm_bwd")
    dxs, dB, dC, ddt_e, dcs_e, dd_p = carrying(_ssd_bwd, xbc, dt_e, cs_e, d_e, states, dy, name="ssd_bwd")
    d_dt, g_dt_bias, g_a_log, g_d_skip = _ssd_post(ddt_e, dcs_e, dd_p, p_dt, dt_bias, a_log, n_heads, "ssd_post")
    d_xbc, g_ssm_w, g_ssm_b = carrying(_ssm_conv_bwd, p_xbc, ssm_w, ssm_b, dxs, dB, dC, name="ssm_conv_bwd")
    d_scB, d_scC, d_scX, g_sc_w = _sc_bwd(p_sc, sc_w, dya, "sc_bwd")
    d_sc = jnp.concatenate([d_scB, d_scC, d_scX], axis=1)
    pieces = [("sc", d_sc), ("z", d_z), ("xbc", d_xbc), ("dt", d_dt), ("gate", d_gate)]
    S.grad("win", {k: mm(hb, d, mode="tn", name="proj_dw_" + k, out_dtypes=(BF16,)) for k, d in pieces})
    pieces = [(k, d + S.tok().astype(d.dtype) if k == "dt" else d) for k, d in pieces]
    dh = mm([d for _, d in pieces], [S.weight(k) for k, _ in pieces], mode="nt", name="proj_dx")
    grad_x, _, g_norm_mix = _rms_bwd(x, norm_mix, dh, dx1, "rms_mix_bwd")

    g_small = dict(norm_mix=g_norm_mix, b_gate=g_b_gate, sc_conv_w=g_sc_w, ssm_conv_w=g_ssm_w, ssm_conv_b=g_ssm_b,
                   dt_bias=g_dt_bias, A_log=g_a_log, D_skip=g_d_skip, ssm_norm_w=g_ssm_norm_w, norm_mlp=g_norm_mlp,
                   norm_final=g_norm_final, loss=loss_row)
    return grad_x, g_small


class _Place:
    def __init__(self, k=0):
        x, y, c = lax.axis_index("x"), lax.axis_index("y"), lax.axis_index("c")
        self.x = 1 - x if k & 4 else x
        self.y = 1 - y if k & 2 else y
        self.c = 1 - c if k & 1 else c
        self.chip = 2 * self.x + self.y
        self.id = 2 * self.chip + self.c


ICI_PEERS = (2, 4, 6)
SIBLING = (1,)
ALL_PEERS = (1, 2, 3, 4, 5, 6, 7)


class _Comm:
    def __init__(self, arrs, out_shape, ks, src, dst, own=None, aliases=None):
        self.arrs, self.out_shape, self.ks = list(arrs), list(out_shape), tuple(ks)
        self.n = len(self.arrs)
        self.src, self.dst, self.own = src, dst, own
        self.aliases = aliases or {}
        dma = pltpu.SemaphoreType.DMA
        self.scratch = [dma((self.n, len(self.ks))), dma((self.n, len(self.ks))), dma((self.n,))]

    def _copies(self, ins, outs, sems, with_recvs):
        send_sems, recv_sems, local_sems = sems
        me = _Place()
        owns, sends, recvs = [], [], []
        for a in range(self.n):
            if self.own is not None:
                s, d = self.own(a, ins[a], outs[a], me)
                owns.append(pltpu.make_async_copy(s, d, local_sems.at[a]))
            for i, k in enumerate(self.ks):
                peer = _Place(k)
                for sender, lst in ((me, sends), (peer, recvs)) if with_recvs else ((me, sends),):
                    lst.append(pltpu.make_async_remote_copy(
                        src_ref=self.src(a, ins[a], me, peer), dst_ref=self.dst(a, outs[a], sender),
                        send_sem=send_sems.at[a, i], recv_sem=recv_sems.at[a, i],
                        device_id=(peer.x, peer.y, peer.c), device_id_type=MESH))
        return owns, sends, recvs

    def start(self, ins, outs, sems):
        owns, sends, _ = self._copies(ins, outs, sems, False)
        for cp in owns + sends:
            cp.start()

    def finish(self, ins, outs, sems):
        owns, sends, recvs = self._copies(ins, outs, sems, True)
        for cp in recvs:
            cp.wait_recv()
        for cp in sends:
            cp.wait_send()
        for cp in owns:
            cp.wait()


class _GatherBoth:
    def __init__(self, shards):
        self.arrs, self.n, self.aliases = list(shards), len(shards), {}
        self.out_shape = [_sds((4, 2) + s.shape, s.dtype) for s in shards]
        dma = pltpu.SemaphoreType.DMA
        self.scratch = [dma((self.n, 7)), dma((self.n, 7)), dma((self.n,))]

    def _copy(self, a, j, src, slot, to, outs, sems):
        return pltpu.make_async_remote_copy(src_ref=src, dst_ref=outs[a].at[slot.chip, slot.c], send_sem=sems[0].at[a, j],
                                            recv_sem=sems[1].at[a, j], device_id=(to.x, to.y, to.c), device_id_type=MESH)

    def start(self, ins, outs, sems):
        me, sib = _Place(), _Place(1)
        for a in range(self.n):
            pltpu.make_async_copy(ins[a], outs[a].at[me.chip, me.c], sems[2].at[a]).start()
            self._copy(a, 0, ins[a], me, sib, outs, sems).start()
            for i, k in enumerate(ICI_PEERS):
                self._copy(a, 1 + i, ins[a], me, _Place(k), outs, sems).start()

    def finish(self, ins, outs, sems):
        me, sib = _Place(), _Place(1)
        passed = []
        for i, k in enumerate(ICI_PEERS):
            peer = _Place(k)
            for a in range(self.n):
                self._copy(a, 1 + i, ins[a], peer, peer, outs, sems).wait_recv()
                cp = self._copy(a, 4 + i, outs[a].at[peer.chip, peer.c], peer, sib, outs, sems)
                cp.start()
                passed.append(cp)
        for a in range(self.n):
            self._copy(a, 0, ins[a], sib, sib, outs, sems).wait_recv()
            for i, k in enumerate(ICI_PEERS):
                far = _Place(k | 1)
                self._copy(a, 4 + i, outs[a].at[far.chip, far.c], far, sib, outs, sems).wait_recv()
        for a in range(self.n):
            self._copy(a, 0, ins[a], me, sib, outs, sems).wait_send()
            for i, k in enumerate(ICI_PEERS):
                self._copy(a, 1 + i, ins[a], me, _Place(k), outs, sems).wait_send()
            pltpu.make_async_copy(ins[a], outs[a].at[me.chip, me.c], sems[2].at[a]).wait()
        for cp in passed:
            cp.wait_send()


def _run_comm(comm, name, after=()):
    n, n_after = comm.n, len(after)

    def body(*refs):
        ins, outs, sems = refs[:n], refs[n + n_after:2 * n + n_after], refs[2 * n + n_after:]
        comm.start(ins, outs, sems)
        comm.finish(ins, outs, sems)

    return list(pl.pallas_call(body, in_specs=[ANY] * (n + n_after), out_specs=[ANY] * n, out_shape=comm.out_shape,
                               scratch_shapes=comm.scratch, input_output_aliases=dict(comm.aliases), name=name)(*comm.arrs, *after))


def _gather_ici(shards):
    return _Comm(shards, [_sds((4, 2) + s.shape, s.dtype) for s in shards], ICI_PEERS,
                 src=lambda a, i, me, p: i, dst=lambda a, o, s: o.at[s.chip, s.c], own=lambda a, i, o, me: (i, o.at[me.chip, me.c]))


def _gather_sibling(bufs):
    return _Comm(bufs, [_sds(b.shape, b.dtype) for b in bufs], SIBLING,
                 src=lambda a, i, me, p: i.at[:, me.c], dst=lambda a, o, s: o.at[:, s.c], aliases={a: a for a in range(len(bufs))})


def _scatter_sibling(parts):
    return _Comm(parts, [_sds((4,) + p.shape[2:], p.dtype) for p in parts], SIBLING,
                 src=lambda a, i, me, p: i.at[:, p.c], dst=lambda a, o, s: o)


def _scatter_ici(parts):
    return _Comm(parts, [_sds(p.shape, p.dtype) for p in parts], ICI_PEERS,
                 src=lambda a, i, me, p: i.at[p.chip], dst=lambda a, o, s: o.at[s.chip], own=lambda a, i, o, me: (i.at[me.chip], o.at[me.chip]))


HBM_SPEC = pl.BlockSpec(memory_space=pltpu.HBM)
SEM_SPEC = pl.BlockSpec(memory_space=pltpu.SEMAPHORE)
DATAFLOW = pltpu.SideEffectType.DATAFLOW_SIDE_EFFECTING


def _tiles_2d(R, C, max_rows=256):
    if R % max_rows == 0:
        return max_rows, C, R // max_rows, lambda i: (i, 0)
    if R <= 2 * max_rows or C % 256:
        return R, C, 1, lambda i: (0, 0)
    return R, 256, C // 256, lambda i: (0, i)


def _own_part(parts, name):
    n, R, C = parts.shape
    br, bc, nb, at = _tiles_2d(R, C)
    chip = (2 * lax.axis_index("x") + lax.axis_index("y")).astype(jnp.int32).reshape(1)

    def body(q_ref, p_ref, o_ref):
        o_ref[...] = p_ref[...]

    blk = pl.BlockSpec((1, br, bc), lambda i, q_ref: (q_ref[0],) + at(i))
    spec = pltpu.PrefetchScalarGridSpec(num_scalar_prefetch=1, grid=(nb,), in_specs=[blk], out_specs=blk)
    return pl.pallas_call(body, grid_spec=spec, out_shape=_sds((n, R, C), parts.dtype), name=name,
                          compiler_params=_params("parallel"))(chip, parts)


def _ici_copy(gather, a, srcs, lands, send_sems, recv_sems, i, me, peer, sender):
    src = lands[a].at[me.chip, me.c] if gather else srcs[a].at[peer.chip]
    dst = lands[a].at[sender.chip, sender.c] if gather else lands[a].at[sender.chip]
    j = a * len(ICI_PEERS) + i
    return pltpu.make_async_remote_copy(src_ref=src, dst_ref=dst, send_sem=send_sems.at[j], recv_sem=recv_sems.at[j],
                                        device_id=(peer.x, peer.y, peer.c), device_id_type=MESH)


def _ici_start(srcs, lands, gather, name):
    n, n_s = len(lands), len(srcs)
    bufs = list(srcs) + list(lands)

    def body(*refs):
        src_refs, land_refs = refs[:n_s], refs[n_s:n_s + n]
        send_sems, recv_sems = refs[n_s + n], refs[n_s + n + 1]
        token = refs[-1]
        me = _Place()
        for a in range(n):
            for i, k in enumerate(ICI_PEERS):
                _ici_copy(gather, a, src_refs, land_refs, send_sems, recv_sems, i, me, _Place(k), me).start()
        token[...] = jnp.zeros_like(token)

    dma = pltpu.SemaphoreType.DMA((n * len(ICI_PEERS),))
    outs = pl.pallas_call(
        body, name=name, out_shape=(dma, dma, *[pltpu.HBM(v.shape, v.dtype) for v in bufs], _sds((8, LANES), F32)),
        in_specs=(HBM_SPEC,) * len(bufs),
        out_specs=(SEM_SPEC, SEM_SPEC) + (HBM_SPEC,) * len(bufs) + (pl.BlockSpec(memory_space=pltpu.VMEM),),
        input_output_aliases={j: 2 + j for j in range(len(bufs))}, compiler_params=pltpu.CompilerParams(has_side_effects=DATAFLOW),
    )(*[pltpu.with_memory_space_constraint(v, pltpu.HBM) for v in bufs])
    return outs[0], outs[1], list(outs[2:2 + n_s]), list(outs[2 + n_s:2 + n_s + n]), outs[-1]


def _ici_wait(flight, after, gather, name):
    send_sems, recv_sems, srcs, lands, _ = flight
    n, n_s = len(lands), len(srcs)
    bufs = srcs + lands

    def body(*refs):
        src_refs, land_refs = refs[:n_s], refs[n_s:n_s + n]
        s_sems, r_sems = refs[n_s + n], refs[n_s + n + 1]
        me = _Place()
        for a in range(n):
            for i, k in enumerate(ICI_PEERS):
                peer = _Place(k)
                cp = _ici_copy(gather, a, src_refs, land_refs, s_sems, r_sems, i, me, peer, peer)
                cp.wait_send()
                cp.wait_recv()

    outs = pl.pallas_call(
        body, name=name, out_shape=tuple(pltpu.HBM(v.shape, v.dtype) for v in bufs),
        in_specs=(HBM_SPEC,) * len(bufs) + (SEM_SPEC, SEM_SPEC) + (ANY,) * len(after), out_specs=(HBM_SPEC,) * len(bufs),
        input_output_aliases={j: j for j in range(len(bufs))}, compiler_params=pltpu.CompilerParams(has_side_effects=DATAFLOW),
    )(*bufs, send_sems, recv_sems, *after)
    return list(outs[n_s:])


def _own_shard(shard, after, name):
    R, C = shard.shape
    tr = R if R <= 256 else 256
    place = jnp.stack([2 * lax.axis_index("x") + lax.axis_index("y"), lax.axis_index("c")]).astype(jnp.int32)

    def body(q_ref, s_ref, after_ref, o_ref):
        o_ref[0, 0] = s_ref[...].astype(o_ref.dtype)

    spec = pltpu.PrefetchScalarGridSpec(
        num_scalar_prefetch=1, grid=(R // tr,), in_specs=[pl.BlockSpec((tr, C), lambda i, q_ref: (i, 0)), ANY],
        out_specs=pl.BlockSpec((1, 1, tr, C), lambda i, q_ref: (q_ref[0], q_ref[1], i, 0)))
    return pl.pallas_call(body, grid_spec=spec, out_shape=_sds((4, 2, R, C), BF16), name=name,
                          compiler_params=_params("parallel"))(place, shard, after)


def _col_pieces(widths):
    out, c = [], 0
    for k, w in widths:
        out.append((k, c, w))
        c += w
    return out


def _split_range(c0, n, bounds):
    parts, c = [], c0
    while c < c0 + n:
        r = max(i for i in range(len(bounds) - 1) if bounds[i] <= c)
        w = min(c0 + n, bounds[r + 1]) - c
        parts.append((r, c - bounds[r], w))
        c += w
    return parts


def _win_unpack(g, widths, name):
    n, R, C = g.shape
    tr = min(256, R)
    pieces = _col_pieces(widths)
    padded = [-(-w // LANES) * LANES for _, _, w in pieces]
    shard_bounds = [s * C for s in range(n + 1)]

    def body(g_ref, *o_refs):
        for (k, c0, w), o_ref in zip(pieces, o_refs):
            for t in range(0, o_ref.shape[1], LANES):
                valid = max(0, min(LANES, w - t))
                cols = [g_ref[s, :, o:o + ww] for s, o, ww in _split_range(c0 + t, valid, shard_bounds)] if valid else []
                if valid < LANES:
                    cols.append(jnp.zeros((tr, LANES - valid), g_ref.dtype))
                o_ref[:, t:t + LANES] = cols[0] if len(cols) == 1 else jnp.concatenate(cols, axis=1)

    return pl.pallas_call(
        body, grid=(R // tr,), in_specs=[pl.BlockSpec((n, tr, C), lambda i: (0, i, 0))],
        out_specs=[pl.BlockSpec((tr, p), lambda i: (i, 0)) for p in padded],
        out_shape=[_sds((R, p), g.dtype) for p in padded], name=name, compiler_params=_params("parallel"))(g)


def _win_pack(grads, widths, n, name):
    R = grads[0].shape[0]
    tr = min(256, R)
    pieces = _col_pieces(widths)
    total = pieces[-1][1] + pieces[-1][2]
    C = total // n
    bounds = [c0 for _, c0, _ in pieces] + [total]

    def body(*refs):
        g_refs, o_ref = refs[:-1], refs[-1]

        def tile_t(c0):
            cols = [g_refs[r][:, o:o + ww] for r, o, ww in _split_range(c0, LANES, bounds)]
            tile = cols[0] if len(cols) == 1 else jnp.concatenate(cols, axis=1)
            return tile.astype(F32).T

        for s in range(n):
            full = C // LANES * LANES
            for t in range(0, full, LANES):
                o_ref[s, t:t + LANES, :] = tile_t(s * C + t).astype(o_ref.dtype)
            if full < C:
                o_ref[s, full:C, :] = tile_t(s * C + C - LANES)[LANES - (C - full):, :].astype(o_ref.dtype)

    return pl.pallas_call(
        body, grid=(R // tr,), in_specs=[pl.BlockSpec((tr, gr.shape[1]), lambda i: (i, 0)) for gr in grads],
        out_specs=pl.BlockSpec((n, C, tr), lambda i: (0, 0, i)), out_shape=_sds((n, C, R), grads[0].dtype),
        name=name, compiler_params=_params("parallel"))(*grads)


def _gather_all(arrs):
    return _Comm(arrs, [_sds((N_DEV,) + a.shape, a.dtype) for a in arrs], ALL_PEERS,
                 src=lambda a, i, me, p: i, dst=lambda a, o, s: o.at[s.id], own=lambda a, i, o, me: (i, o.at[me.id]))


def _add_halves(parts, got, name):
    n, _, R, C = parts.shape
    br, bc, nb, at = _tiles_2d(R, C)
    core = lax.axis_index("c").astype(jnp.int32).reshape(1)

    def body(c_ref, p_ref, g_ref, o_ref):
        o_ref[0] = (p_ref[0, 0].astype(F32) + g_ref[0].astype(F32)).astype(o_ref.dtype)

    spec = pltpu.PrefetchScalarGridSpec(
        num_scalar_prefetch=1, grid=(n, nb),
        in_specs=[pl.BlockSpec((1, 1, br, bc), lambda q, i, c_ref: (q, c_ref[0]) + at(i)), pl.BlockSpec((1, br, bc), lambda q, i, c_ref: (q,) + at(i))],
        out_specs=pl.BlockSpec((1, br, bc), lambda q, i, c_ref: (q,) + at(i)))
    return pl.pallas_call(body, grid_spec=spec, out_shape=_sds((n, R, C), parts.dtype), name=name,
                          compiler_params=_params("parallel", "parallel"))(core, parts, got)


def _adam(w, m, v, gparts, name, comm=None):
    R, C = w.shape
    n = gparts.shape[0]
    br, bc, nb, at = _tiles_2d(R, C, max_rows=128)
    c1 = 1.0 / (1.0 - ADAM_B1 ** ADAM_STEP)
    c2 = 1.0 / (1.0 - ADAM_B2 ** ADAM_STEP)

    def body(w_ref, m_ref, v_ref, g_ref, go_ref, d_ref, mo_ref, vo_ref):
        g = g_ref[0].astype(F32)
        for s in range(1, n):
            g = g + g_ref[s].astype(F32)
        mn = ADAM_B1 * m_ref[...] + (1.0 - ADAM_B1) * g
        vn = ADAM_B2 * v_ref[...] + (1.0 - ADAM_B2) * (g * g)
        go_ref[...] = g
        mo_ref[...] = mn
        vo_ref[...] = vn
        d_ref[...] = -ADAM_LR * ((mn * c1) / (jnp.sqrt(vn * c2) + ADAM_EPS) + ADAM_WD * w_ref[...])

    blk = pl.BlockSpec((br, bc), at)
    outs, carried = _call(
        body, grid=(nb,), in_specs=[blk, blk, blk, pl.BlockSpec((n, br, bc), lambda i: (0,) + at(i))],
        out_specs=[blk] * 4, out_shape=[_sds((R, C), F32)] * 4, args=[w, m, v, gparts], name=name, sem=("parallel",), comm=comm)
    return outs if comm is None else (outs, carried)


_SMALL_ORDER = ("norm_mix", "b_gate", "sc_conv_w", "ssm_conv_w", "ssm_conv_b", "dt_bias", "A_log", "D_skip", "ssm_norm_w",
                "norm_mlp", "norm_final", "loss")
_REPLICATED = ("norm_mix", "b_gate", "ssm_conv_b", "dt_bias", "A_log", "D_skip", "ssm_norm_w", "norm_mlp", "norm_final")


def _cols_to_slots(g, n):
    R = g.shape[0]
    return jnp.transpose(g.reshape(R, n, g.shape[1] // n), (1, 0, 2))


def _slots_to_cols(g):
    n, R, C = g.shape
    return jnp.transpose(g, (1, 0, 2)).reshape(R, n * C)


def kernel(x, norm_mix, w_in, b_gate, sc_conv_w, ssm_conv_w, ssm_conv_b, dt_bias, A_log, D_skip, ssm_norm_w, w_branch_sc, w_branch_ssm, w_out, norm_mlp, w_mlp1, w_mlp2, norm_final, loss_target, m_norm_mix, m_w_in, m_b_gate, m_sc_conv_w, m_ssm_conv_w, m_ssm_conv_b, m_dt_bias, m_A_log, m_D_skip, m_ssm_norm_w, m_w_branch_sc, m_w_branch_ssm, m_w_out, m_norm_mlp, m_w_mlp1, m_w_mlp2, m_norm_final, v_norm_mix, v_w_in, v_b_gate, v_sc_conv_w, v_ssm_conv_w, v_ssm_conv_b, v_dt_bias, v_A_log, v_D_skip, v_ssm_norm_w, v_w_branch_sc, v_w_branch_ssm, v_w_out, v_norm_mlp, v_w_mlp1, v_w_mlp2, v_norm_final):
    T, D = x.shape[1], x.shape[2]
    n_inner = 2 * D
    n_heads = n_inner // HEADDIM
    n_xbc = n_inner + 2 * NGROUPS * NSTATE
    me = 4 * lax.axis_index("x") + 2 * lax.axis_index("y") + lax.axis_index("c")

    in_cols = [("sc", 3 * D), ("z", n_inner), ("xbc", n_xbc), ("dt", n_heads), ("gate", 2 * D)]
    by_owner = lambda b: b.reshape((N_DEV,) + b.shape[2:])
    to_owner = lambda g: g.reshape((4, 2) + g.shape[1:])
    rows_of = lambda g: to_owner(g.reshape((N_DEV, g.shape[0] // N_DEV) + g.shape[1:]))
    cols_of = lambda g: to_owner(_cols_to_slots(g, N_DEV))

    class Schedule(_NoExchange):
        late = ("bssm", "bsc", "out", "w1", "w2")
        gather_sib = dict(gnorm_fwd=("bsc", "bssm", "out"), branch_ssm=("w1", "w2"))
        scatter_sib = dict(mlp_up_dx=("w2", "w1"), branch_ssm_dx=("out", "bssm", "bsc"))
        shards = dict(bsc=w_branch_sc, bssm=w_branch_ssm, out=w_out, w1=w_mlp1, w2=w_mlp2)

        def __init__(self):
            self.W, self.staged, self.grads, self.summed, self.scatters = {}, {}, {}, {}, []
            self.token = jnp.zeros((), F32)

        def first_weights(self, bufs):
            self.W.update(zip([k for k, _ in in_cols], _win_unpack(by_owner(bufs[0]), in_cols, "win_unpack")))
            self.W.update(sc_conv_w=_slots_to_cols(by_owner(bufs[1])), ssm_conv_w=_slots_to_cols(by_owner(bufs[2])))

        def tok(self):
            return self.token

        def point(self, name, values):
            if name == "first_projection_done":
                lands = [_own_shard(self.shards[k], values[0], "own_shard_" + k) for k in self.late]
                self.gather_flight = _ici_start([], lands, True, "gather_late_start")
                self.token = self.gather_flight[4][0, 0]
                self.W["dt"] = self.W["dt"] + self.token.astype(BF16)
            elif name == "mixers_done":
                lands = _ici_wait(self.gather_flight, values, True, "gather_late_wait")
                self.staged.update(zip(self.late, lands))

        def carry(self, name):
            if name == "rms_mix":
                return _GatherBoth([w_in.astype(BF16), sc_conv_w, ssm_conv_w])
            if name in self.gather_sib:
                return _gather_sibling([self.staged.pop(k) for k in self.gather_sib[name]])
            if name in self.scatter_sib:
                return _scatter_sibling([self.grads[k] for k in self.scatter_sib[name]])
            return None

        def start_scatter(self, keys, halves):
            lands = [_own_part(h, "own_part_" + k) for k, h in zip(keys, halves)]
            flight = _ici_start(halves, lands, False, "scatter_%s_start" % keys[0])
            self.scatters.append((keys, flight))
            self.token = flight[4][0, 0]

        def carried(self, name, outs):
            if name == "rms_mix":
                self.first_weights(outs)
            elif name in self.gather_sib:
                for k, b in zip(self.gather_sib[name], outs):
                    full = by_owner(b)
                    self.W[k] = _slots_to_cols(full) if k == "w1" else full.reshape(-1, D)
            else:
                keys = self.scatter_sib[name]
                self.start_scatter(keys, [_add_halves(self.grads[k], b, "add_halves_" + k) for k, b in zip(keys, outs)])

        def grad(self, k, g):
            if k == "win":
                g = to_owner(_win_pack([g[k] for k, _ in in_cols], in_cols, N_DEV, "win_pack"))
                got = _run_comm(_scatter_sibling([g]), "scatter_sibling_win")[0]
                self.start_scatter(("win",), [_add_halves(g, got, "add_halves_win")])
            else:
                self.grads[k] = cols_of(g) if k == "w1" else rows_of(g)

        def finish_scatter(self, after):
            keys, flight = self.scatters.pop(0)
            return dict(zip(keys, _ici_wait(flight, after, False, "scatter_%s_wait" % keys[0])))

    S = Schedule()
    small = dict(norm_mix=norm_mix, b_gate=b_gate, ssm_conv_b=ssm_conv_b, dt_bias=dt_bias, A_log=A_log, D_skip=D_skip,
                 ssm_norm_w=ssm_norm_w, norm_mlp=norm_mlp, norm_final=norm_final)
    grad_x, g_small = _local_step(x.reshape(T, D), loss_target.reshape(T, D), S, small)

    small_flat = jnp.concatenate([g_small[k].reshape(-1) for k in _SMALL_ORDER])
    n_small = small_flat.shape[0]
    rows = -(-n_small // (8 * LANES)) * 8
    small_pack = jnp.pad(small_flat, (0, rows * LANES - n_small)).reshape(rows, LANES)

    res = {}
    big = [("w_in", "win", w_in, m_w_in, v_w_in), ("w_branch_sc", "bsc", w_branch_sc, m_w_branch_sc, v_w_branch_sc),
           ("w_branch_ssm", "bssm", w_branch_ssm, m_w_branch_ssm, v_w_branch_ssm), ("w_out", "out", w_out, m_w_out, v_w_out),
           ("w_mlp1", "w1", w_mlp1, m_w_mlp1, v_w_mlp1), ("w_mlp2", "w2", w_mlp2, m_w_mlp2, v_w_mlp2)]
    by_grad = {gk: (k, w, m, v) for k, gk, w, m, v in big}
    after = [grad_x]
    while S.scatters:
        for gk, parts in S.finish_scatter(after).items():
            k, w, m, v = by_grad[gk]
            if gk == "win":
                res_t, (small_parts,) = _adam(w.T, m.T, v.T, parts, "adam_" + k, comm=_gather_all([small_pack]))
                res[k] = [r.T for r in res_t]
            else:
                res[k] = _adam(w, m, v, parts, "adam_" + k)
            after = after + [res[k][1]]

    sizes = {k: g_small[k].size for k in _SMALL_ORDER}
    offs, o = {}, 0
    for k in _SMALL_ORDER:
        offs[k] = o
        o += sizes[k]
    rep_w = dict(norm_mix=norm_mix, b_gate=b_gate, ssm_conv_b=ssm_conv_b, dt_bias=dt_bias, A_log=A_log, D_skip=D_skip,
                 ssm_norm_w=ssm_norm_w, norm_mlp=norm_mlp, norm_final=norm_final)
    rep_m = dict(norm_mix=m_norm_mix, b_gate=m_b_gate, ssm_conv_b=m_ssm_conv_b, dt_bias=m_dt_bias, A_log=m_A_log, D_skip=m_D_skip,
                 ssm_norm_w=m_ssm_norm_w, norm_mlp=m_norm_mlp, norm_final=m_norm_final)
    rep_v = dict(norm_mix=v_norm_mix, b_gate=v_b_gate, ssm_conv_b=v_ssm_conv_b, dt_bias=v_dt_bias, A_log=v_A_log, D_skip=v_D_skip,
                 ssm_norm_w=v_ssm_norm_w, norm_mlp=v_norm_mlp, norm_final=v_norm_final)

    def pack(d):
        segs = [jnp.pad(d[k].astype(F32).reshape(-1), (0, sizes[k] - d[k].size)) if k in d else jnp.zeros((sizes[k],), F32)
                for k in _SMALL_ORDER]
        return jnp.pad(jnp.concatenate(segs), (0, rows * LANES - n_small)).reshape(rows, LANES)

    sm = _adam(pack(rep_w), pack(rep_m), pack(rep_v), small_parts, "adam_small")
    sm = [s.reshape(-1) for s in sm]
    for k in _REPLICATED:
        n_k = rep_w[k].shape[0]
        res[k] = tuple(s[offs[k]:offs[k] + n_k] for s in sm)
    loss = sm[0][offs["loss"]]
    for k, w, m, v, K, full in (("sc_conv_w", sc_conv_w, m_sc_conv_w, v_sc_conv_w, SC_K, D),
                                ("ssm_conv_w", ssm_conv_w, m_ssm_conv_w, v_ssm_conv_w, SSM_K, n_xbc)):
        g_full = sm[0][offs[k]:offs[k] + K * full].reshape(K, full)
        cw = full // N_DEV
        g_mine = lax.dynamic_slice_in_dim(g_full, me * cw, cw, axis=1)
        res[k] = _adam(w, m, v, g_mine[None], "adam_" + k)

    order = ("norm_mix", "w_in", "b_gate", "sc_conv_w", "ssm_conv_w", "ssm_conv_b", "dt_bias", "A_log", "D_skip", "ssm_norm_w",
             "w_branch_sc", "w_branch_ssm", "w_out", "norm_mlp", "w_mlp1", "w_mlp2", "norm_final")
    outs = [loss, grad_x.reshape(1, T, D)]
    for j in range(4):
        outs += [res[k][j] for k in order]
    return tuple(outs)
```

```python
import functools

import jax
import jax.numpy as jnp
from jax import lax
from jax.experimental import pallas as pl
from jax.experimental.pallas import tpu as pltpu

F32 = jnp.float32
BF16 = jnp.bfloat16

EPS = 1e-6
N_DEV = 8
HEADDIM = 64
NSTATE = 128
CHUNK = 128
NGROUPS = 8
GROUP_W = 256
SC_K = 3
SSM_K = 4
LANES = 128

ADAM_LR = 0.001
ADAM_B1 = 0.9
ADAM_B2 = 0.999
ADAM_EPS = 1e-08
ADAM_WD = 0.01
ADAM_STEP = 10

NN = (((1,), (0,)), ((), ()))
NT = (((1,), (1,)), ((), ()))
TN = (((0,), (0,)), ((), ()))
_DIMS = {"nn": NN, "nt": NT, "tn": TN}

ANY = pl.BlockSpec(memory_space=pl.ANY)
MESH = pl.DeviceIdType.MESH


def _sds(shape, dtype):
    return jax.ShapeDtypeStruct(tuple(shape), dtype)


def _dot(a, b, dims=NN):
    return lax.dot_general(a, b, dims, preferred_element_type=F32)


def _dot3(a, b, dims=NN):
    return lax.dot_general(a, b, dims, preferred_element_type=F32, precision=lax.Precision.HIGH)


def _params(*sem):
    return pltpu.CompilerParams(dimension_semantics=tuple(sem))


def _call(body, *, grid, in_specs, out_specs, out_shape, args, name, sem, scratch=(), comm=None):
    if comm is None:
        outs = pl.pallas_call(body, grid=grid, in_specs=list(in_specs), out_specs=list(out_specs), out_shape=list(out_shape),
                              scratch_shapes=list(scratch), name=name, compiler_params=_params(*sem))(*args)
        return list(outs), None
    n, n_in, n_out, n_scr = comm.n, len(in_specs), len(out_shape), len(scratch)

    def wrapped(*refs):
        ins, c_in = refs[:n_in], refs[n_in:n_in + n]
        outs, c_out = refs[n_in + n:n_in + n + n_out], refs[n_in + n + n_out:n_in + 2 * n + n_out]
        rest = refs[n_in + 2 * n + n_out:]
        scr, sems = rest[:n_scr], rest[n_scr:]
        first, last = None, None
        for d, g in enumerate(grid):
            f, l = pl.program_id(d) == 0, pl.program_id(d) == g - 1
            first, last = (f, l) if first is None else (first & f, last & l)

        @pl.when(first)
        def _():
            comm.start(c_in, c_out, sems)

        body(*ins, *outs, *scr)

        @pl.when(last)
        def _():
            comm.finish(c_in, c_out, sems)

    outs = pl.pallas_call(
        wrapped, grid=grid, in_specs=list(in_specs) + [ANY] * n, out_specs=list(out_specs) + [ANY] * n,
        out_shape=list(out_shape) + comm.out_shape, scratch_shapes=list(scratch) + comm.scratch,
        input_output_aliases={n_in + i: n_out + o for i, o in comm.aliases.items()},
        name=name, compiler_params=_params(*["arbitrary"] * len(grid)))(*args, *comm.arrs)
    return list(outs[:n_out]), list(outs[n_out:])


MM_VMEM_BUDGET = 44 * 2 ** 20


def _mm_tiles(M, N, k_bytes, mn_bytes):
    best = None
    for tm in (2048, 1024, 512, 256, 128):
        for tn in (1024, 512, 256, 128):
            if M % tm or N % tn:
                continue
            need = 2 * ((tm + tn) * k_bytes + tm * tn * mn_bytes) + 4 * tm * tn * 4
            if need <= MM_VMEM_BUDGET and (best is None or (tm * tn, tm) > (best[0] * best[1], best[0])):
                best = (tm, tn)
    assert best is not None, (M, N, k_bytes, mn_bytes)
    return best


def _mm(a, b, *, mode, name, extras=(), epi=None, out_dtypes=(F32,), comm=None):
    a_list = list(a) if isinstance(a, (list, tuple)) else [a]
    b_list = list(b) if isinstance(b, (list, tuple)) else [b]
    if mode == "nn":
        M, N = a_list[0].shape[0], b_list[0].shape[1]
    elif mode == "nt":
        M, N = a_list[0].shape[0], b_list[0].shape[0]
    else:
        M, N = a_list[0].shape[1], b_list[0].shape[1]
    k_bytes = sum((av.shape[0] if mode == "tn" else av.shape[1]) * av.dtype.itemsize for av in a_list)
    mn_bytes = sum(e.dtype.itemsize for e in extras) + sum(jnp.dtype(d).itemsize for d in out_dtypes)
    tm, tn = _mm_tiles(min(M, 2048), min(N, 1024), k_bytes, mn_bytes) if M % 128 == 0 and N % 128 == 0 else (M, N)
    assert M % tm == 0 and N % tn == 0
    a_specs, b_specs = [], []
    for av, bv in zip(a_list, b_list):
        K = av.shape[0] if mode == "tn" else av.shape[1]
        a_specs.append(pl.BlockSpec((K, tm), lambda i, j: (0, i)) if mode == "tn" else pl.BlockSpec((tm, K), lambda i, j: (i, 0)))
        b_specs.append(pl.BlockSpec((tn, K), lambda i, j: (j, 0)) if mode == "nt" else pl.BlockSpec((K, tn), lambda i, j: (0, j)))
    mn_spec = pl.BlockSpec((tm, tn), lambda i, j: (i, j))
    n_p, n_ex = len(a_list), len(extras)
    dims = _DIMS[mode]

    def body(*refs):
        acc = _dot(refs[0][...], refs[n_p][...], dims)
        for p in range(1, n_p):
            acc = acc + _dot(refs[p][...], refs[n_p + p][...], dims)
        rest = refs[2 * n_p:]
        res = (acc,) if epi is None else epi(acc, *[r[...] for r in rest[:n_ex]])
        for o_ref, r in zip(rest[n_ex:], res):
            o_ref[...] = r.astype(o_ref.dtype)

    outs, carried = _call(
        body, grid=(M // tm, N // tn), in_specs=a_specs + b_specs + [mn_spec] * n_ex,
        out_specs=[mn_spec] * len(out_dtypes), out_shape=[_sds((M, N), d) for d in out_dtypes],
        args=a_list + b_list + list(extras), name=name, sem=("parallel", "parallel"), comm=comm)
    res = outs[0] if len(outs) == 1 else outs
    return res if comm is None else (res, carried)


def _epi_add(acc, r):
    return (acc + r,)


def _epi_add2(acc, r):
    s = acc + r
    return (s, s)


def _epi_relu2(acc):
    p = jnp.maximum(acc, 0.0)
    return (p * p,)


def _epi_relu2_bwd(acc, r):
    return (acc * (2.0 * jnp.sqrt(r.astype(F32))),)


def _row(tr, n):
    return pl.BlockSpec((tr, n), lambda i: (i, 0))


def _vec(n):
    return pl.BlockSpec((1, n), lambda i: (0, 0))


def _rms_fwd(x, w, name, comm=None):
    T, D = x.shape
    tr = min(256, T)

    def body(x_ref, w_ref, o_ref):
        xv = x_ref[...]
        r = lax.rsqrt(jnp.mean(xv * xv, axis=-1, keepdims=True) + EPS)
        o_ref[...] = (xv * r * w_ref[...]).astype(BF16)

    outs, carried = _call(body, grid=(T // tr,), in_specs=[_row(tr, D), _vec(D)], out_specs=[_row(tr, D)],
                          out_shape=[_sds((T, D), BF16)], args=[x, w], name=name, sem=("parallel",), comm=comm)
    return outs[0] if comm is None else (outs[0], carried)


def _rms_bwd(x, w, dh, dres, name):
    T, D = x.shape
    tr = min(256, T)

    def body(x_ref, w_ref, dh_ref, dres_ref, dx_ref, dxb_ref, dw_ref):
        @pl.when(pl.program_id(0) == 0)
        def _():
            dw_ref[...] = jnp.zeros_like(dw_ref)

        xv = x_ref[...]
        r = lax.rsqrt(jnp.mean(xv * xv, axis=-1, keepdims=True) + EPS)
        xh = xv * r
        dh_v = dh_ref[...]
        dw_ref[...] += jnp.sum(dh_v * xh, axis=0, keepdims=True)
        dxh = dh_v * w_ref[...]
        dx = r * (dxh - xh * jnp.mean(dxh * xh, axis=-1, keepdims=True)) + dres_ref[...]
        dx_ref[...] = dx
        dxb_ref[...] = dx.astype(BF16)

    return pl.pallas_call(
        body, grid=(T // tr,), in_specs=[_row(tr, D), _vec(D), _row(tr, D), _row(tr, D)],
        out_specs=[_row(tr, D), _row(tr, D), _vec(D)],
        out_shape=[_sds((T, D), F32), _sds((T, D), BF16), _sds((1, D), F32)],
        name=name, compiler_params=_params("arbitrary"))(x, w, dh, dres)


def _final(x2, w, tgt, name):
    T, D = x2.shape
    tr = min(256, T)

    def body(x_ref, w_ref, t_ref, dx_ref, dxb_ref, dw_ref, loss_ref):
        @pl.when(pl.program_id(0) == 0)
        def _():
            dw_ref[...] = jnp.zeros_like(dw_ref)
            loss_ref[...] = jnp.zeros_like(loss_ref)

        xv = x_ref[...]
        wv = w_ref[...]
        r = lax.rsqrt(jnp.mean(xv * xv, axis=-1, keepdims=True) + EPS)
        xh = xv * r
        err = xh * wv - t_ref[...]
        part = jnp.sum(jnp.sum(err * err, axis=1, keepdims=True), axis=0, keepdims=True) * (0.5 / D)
        loss_ref[...] += jnp.broadcast_to(part, loss_ref.shape)
        dy = err * (1.0 / D)
        dw_ref[...] += jnp.sum(dy * xh, axis=0, keepdims=True)
        dxh = dy * wv
        dx = r * (dxh - xh * jnp.mean(dxh * xh, axis=-1, keepdims=True))
        dx_ref[...] = dx
        dxb_ref[...] = dx.astype(BF16)

    return pl.pallas_call(
        body, grid=(T // tr,), in_specs=[_row(tr, D), _vec(D), _row(tr, D)],
        out_specs=[_row(tr, D), _row(tr, D), _vec(D), _vec(LANES)],
        out_shape=[_sds((T, D), F32), _sds((T, D), BF16), _sds((1, D), F32), _sds((1, LANES), F32)],
        name=name, compiler_params=_params("arbitrary"))(x2, w, tgt)


def _silu_parts(z):
    s = jax.nn.sigmoid(z)
    return z * s, s * (1.0 + z * (1.0 - s))


def _gnorm_fwd(y, z, w, name, comm=None):
    T, N = y.shape
    tr = min(256, T)

    def body(y_ref, z_ref, w_ref, o_ref):
        for g in range(N // GROUP_W):
            sl = slice(g * GROUP_W, (g + 1) * GROUP_W)
            silu, _ = _silu_parts(z_ref[:, sl])
            yz = y_ref[:, sl] * silu
            r = lax.rsqrt(jnp.mean(yz * yz, axis=-1, keepdims=True) + EPS)
            o_ref[:, sl] = (yz * r * w_ref[:, sl]).astype(BF16)

    outs, carried = _call(body, grid=(T // tr,), in_specs=[_row(tr, N), _row(tr, N), _vec(N)], out_specs=[_row(tr, N)],
                          out_shape=[_sds((T, N), BF16)], args=[y, z, w], name=name, sem=("parallel",), comm=comm)
    return outs[0] if comm is None else (outs[0], carried)


def _gnorm_bwd(y, z, w, dyb, name):
    T, N = y.shape
    tr = min(256, T)

    def body(y_ref, z_ref, w_ref, d_ref, dy_ref, dz_ref, dw_ref):
        @pl.when(pl.program_id(0) == 0)
        def _():
            dw_ref[...] = jnp.zeros_like(dw_ref)

        for g in range(N // GROUP_W):
            sl = slice(g * GROUP_W, (g + 1) * GROUP_W)
            yv = y_ref[:, sl]
            silu, dsilu = _silu_parts(z_ref[:, sl])
            yz = yv * silu
            r = lax.rsqrt(jnp.mean(yz * yz, axis=-1, keepdims=True) + EPS)
            yzh = yz * r
            d = d_ref[:, sl]
            dw_ref[:, sl] += jnp.sum(d * yzh, axis=0, keepdims=True)
            dyzh = d * w_ref[:, sl]
            dyz = r * (dyzh - yzh * jnp.mean(dyzh * yzh, axis=-1, keepdims=True))
            dy_ref[:, sl] = dyz * silu
            dz_ref[:, sl] = (dyz * yv * dsilu).astype(BF16)

    return pl.pallas_call(
        body, grid=(T // tr,), in_specs=[_row(tr, N), _row(tr, N), _vec(N), _row(tr, N)],
        out_specs=[_row(tr, N), _row(tr, N), _vec(N)],
        out_shape=[_sds((T, N), F32), _sds((T, N), BF16), _sds((1, N), F32)],
        name=name, compiler_params=_params("arbitrary"))(y, z, w, dyb)


def _merge_fwd(gate_raw, b_gate, br_a, br_b, name):
    T, D = br_a.shape
    tr = min(256, T)

    def body(g_ref, bg_ref, a_ref, b_ref, o_ref):
        g = jax.nn.sigmoid(g_ref[...] + bg_ref[...])
        o_ref[...] = (g[:, :D] * a_ref[...] + g[:, D:] * b_ref[...]).astype(BF16)

    return pl.pallas_call(body, grid=(T // tr,), in_specs=[_row(tr, 2 * D), _vec(2 * D), _row(tr, D), _row(tr, D)],
                          out_specs=_row(tr, D), out_shape=_sds((T, D), BF16), name=name,
                          compiler_params=_params("parallel"))(gate_raw, b_gate, br_a, br_b)


def _merge_bwd(dmerged, gate_raw, b_gate, br_a, br_b, name):
    T, D = br_a.shape
    tr = min(256, T)

    def body(d_ref, g_ref, bg_ref, a_ref, b_ref, da_ref, db_ref, dg_ref, dbg_ref):
        @pl.when(pl.program_id(0) == 0)
        def _():
            dbg_ref[...] = jnp.zeros_like(dbg_ref)

        g = jax.nn.sigmoid(g_ref[...] + bg_ref[...])
        d = d_ref[...]
        da_ref[...] = (d * g[:, :D]).astype(BF16)
        db_ref[...] = (d * g[:, D:]).astype(BF16)
        dg = jnp.concatenate([d * a_ref[...], d * b_ref[...]], axis=1) * g * (1.0 - g)
        dg_ref[...] = dg.astype(BF16)
        dbg_ref[...] += jnp.sum(dg, axis=0, keepdims=True)

    return pl.pallas_call(
        body, grid=(T // tr,), in_specs=[_row(tr, D), _row(tr, 2 * D), _vec(2 * D), _row(tr, D), _row(tr, D)],
        out_specs=[_row(tr, D), _row(tr, D), _row(tr, 2 * D), _vec(2 * D)],
        out_shape=[_sds((T, D), BF16), _sds((T, D), BF16), _sds((T, 2 * D), BF16), _sds((1, 2 * D), F32)],
        name=name, compiler_params=_params("arbitrary"))(dmerged, gate_raw, b_gate, br_a, br_b)


CB_W = 256
CONV_ROWS = 32
CONV_PAD = 8


def _rows_down(load, r0, s):
    if s == 0:
        return load(r0, r0 + CONV_ROWS)
    if r0 == 0:
        row = lax.broadcasted_iota(jnp.int32, (CONV_ROWS, CB_W), 0)
        return jnp.where(row >= s, pltpu.roll(load(0, CONV_ROWS), s, 0), 0.0)
    return load(r0 - s, r0 - s + CONV_ROWS)


def _conv_tile(load, taps, r0):
    K = len(taps)
    us = [_rows_down(load, r0, K - 1 - k) for k in range(K)]
    acc = us[K - 1] * taps[K - 1]
    for k in range(K - 1):
        acc = acc + us[k] * taps[k]
    return acc, us


def _conv_back_tile(scr, taps, r0):
    K = len(taps)
    du = scr[r0:r0 + CONV_ROWS, :] * taps[K - 1]
    for k in range(K - 1):
        s = K - 1 - k
        du = du + scr[r0 + s:r0 + s + CONV_ROWS, :] * taps[k]
    return du


def _fold8(v):
    return jnp.sum(v.reshape(CONV_ROWS // 8, 8, v.shape[1]), axis=0)


def _col(T, j0=0):
    return pl.BlockSpec((T, CB_W), lambda j: (0, j + j0))


def _sc_fwd(psc, w, name):
    T, D = psc.shape[0], psc.shape[1] // 3
    nb = D // CB_W

    def body(b_ref, c_ref, x_ref, w_ref, o_ref):
        taps = [w_ref[k:k + 1, :] for k in range(SC_K)]
        load = lambda a, b: c_ref[a:b, :] * x_ref[a:b, :]
        for r0 in range(0, T, CONV_ROWS):
            cu, _ = _conv_tile(load, taps, r0)
            o_ref[r0:r0 + CONV_ROWS, :] = (b_ref[r0:r0 + CONV_ROWS, :] * cu).astype(BF16)

    return pl.pallas_call(
        body, grid=(nb,), in_specs=[_col(T), _col(T, nb), _col(T, 2 * nb), pl.BlockSpec((SC_K, CB_W), lambda j: (0, j))],
        out_specs=_col(T), out_shape=_sds((T, D), BF16), name=name, compiler_params=_params("parallel"))(psc, psc, psc, w)


def _sc_bwd(psc, w, dya, name):
    T, D = psc.shape[0], psc.shape[1] // 3
    nb = D // CB_W

    def body(b_ref, c_ref, x_ref, w_ref, d_ref, db_ref, dc_ref, dx_ref, dw_ref, scr):
        taps = [w_ref[k:k + 1, :] for k in range(SC_K)]
        load = lambda a, b: c_ref[a:b, :] * x_ref[a:b, :]
        scr[T:T + CONV_PAD, :] = jnp.zeros((CONV_PAD, CB_W), F32)
        dw8 = [jnp.zeros((8, CB_W), F32)] * SC_K
        for r0 in range(0, T, CONV_ROWS):
            rows = slice(r0, r0 + CONV_ROWS)
            cu, us = _conv_tile(load, taps, r0)
            d = d_ref[rows, :]
            db_ref[rows, :] = (d * cu).astype(BF16)
            dcu = d * b_ref[rows, :]
            scr[rows, :] = dcu
            dw8 = [acc + _fold8(dcu * u) for acc, u in zip(dw8, us)]
        for k in range(SC_K):
            dw_ref[k:k + 1, :] = jnp.sum(dw8[k], axis=0, keepdims=True)
        for r0 in range(0, T, CONV_ROWS):
            rows = slice(r0, r0 + CONV_ROWS)
            du = _conv_back_tile(scr, taps, r0)
            dc_ref[rows, :] = (du * x_ref[rows, :]).astype(BF16)
            dx_ref[rows, :] = (du * c_ref[rows, :]).astype(BF16)

    wspec = pl.BlockSpec((SC_K, CB_W), lambda j: (0, j))
    return pl.pallas_call(
        body, grid=(nb,), in_specs=[_col(T), _col(T, nb), _col(T, 2 * nb), wspec, _col(T)],
        out_specs=[_col(T), _col(T), _col(T), wspec],
        out_shape=[_sds((T, D), BF16)] * 3 + [_sds((SC_K, D), F32)],
        scratch_shapes=[pltpu.VMEM((T + CONV_PAD, CB_W), F32)],
        name=name, compiler_params=_params("parallel"))(psc, psc, psc, w, dya)


def _ssm_conv_fwd(u, w, b, name, comm=None):
    T, N = u.shape

    def body(u_ref, w_ref, b_ref, o_ref):
        taps = [w_ref[k:k + 1, :] for k in range(SSM_K)]
        bias = b_ref[...]
        for r0 in range(0, T, CONV_ROWS):
            c, _ = _conv_tile(lambda a, b: u_ref[a:b, :], taps, r0)
            c = c + bias
            o_ref[r0:r0 + CONV_ROWS, :] = c * jax.nn.sigmoid(c)

    outs, carried = _call(
        body, grid=(N // CB_W,), in_specs=[_col(T), pl.BlockSpec((SSM_K, CB_W), lambda j: (0, j)), pl.BlockSpec((1, CB_W), lambda j: (0, j))],
        out_specs=[_col(T)], out_shape=[_sds((T, N), F32)], args=[u, w, b], name=name, sem=("parallel",), comm=comm)
    return outs[0] if comm is None else (outs[0], carried)


def _ssm_conv_bwd(u, w, b, dxs, dB, dC, name, comm=None):
    T, N = u.shape
    n_x, n_b = dxs.shape[1] // CB_W, dB.shape[1] // CB_W

    def body(u_ref, w_ref, b_ref, dx_ref, db_ref, dc_ref, du_ref, dw_ref, dbias_ref, scr):
        j = pl.program_id(0)
        taps = [w_ref[k:k + 1, :] for k in range(SSM_K)]
        bias = b_ref[...]
        scr[T:T + CONV_PAD, :] = jnp.zeros((CONV_PAD, CB_W), F32)
        dw8 = [jnp.zeros((8, CB_W), F32)] * SSM_K
        db8 = jnp.zeros((8, CB_W), F32)
        for r0 in range(0, T, CONV_ROWS):
            rows = slice(r0, r0 + CONV_ROWS)
            c, us = _conv_tile(lambda a, b: u_ref[a:b, :], taps, r0)
            _, dsilu = _silu_parts(c + bias)
            d = jnp.where(j < n_x, dx_ref[rows, :], jnp.where(j < n_x + n_b, db_ref[rows, :], dc_ref[rows, :])) * dsilu
            scr[rows, :] = d
            db8 = db8 + _fold8(d)
            dw8 = [acc + _fold8(d * u) for acc, u in zip(dw8, us)]
        dbias_ref[...] = jnp.sum(db8, axis=0, keepdims=True)
        for k in range(SSM_K):
            dw_ref[k:k + 1, :] = jnp.sum(dw8[k], axis=0, keepdims=True)
        for r0 in range(0, T, CONV_ROWS):
            du_ref[r0:r0 + CONV_ROWS, :] = _conv_back_tile(scr, taps, r0).astype(BF16)

    wspec = pl.BlockSpec((SSM_K, CB_W), lambda j: (0, j))
    bspec = pl.BlockSpec((1, CB_W), lambda j: (0, j))
    outs, carried = _call(
        body, grid=(N // CB_W,),
        in_specs=[_col(T), wspec, bspec,
                  pl.BlockSpec((T, CB_W), lambda j: (0, jnp.minimum(j, n_x - 1))),
                  pl.BlockSpec((T, CB_W), lambda j: (0, jnp.clip(j - n_x, 0, n_b - 1))),
                  pl.BlockSpec((T, CB_W), lambda j: (0, jnp.clip(j - n_x - n_b, 0, n_b - 1)))],
        out_specs=[_col(T), wspec, bspec],
        out_shape=[_sds((T, N), BF16), _sds((SSM_K, N), F32), _sds((1, N), F32)],
        scratch=[pltpu.VMEM((T + CONV_PAD, CB_W), F32)],
        args=[u, w, b, dxs, dB, dC], name=name, sem=("parallel",), comm=comm)
    return outs if comm is None else (outs, carried)


def _split3(v):
    hi = v.astype(BF16)
    r = v - hi.astype(F32)
    mid = r.astype(BF16)
    lo = (r - mid.astype(F32)).astype(BF16)
    return hi, mid, lo


def _head_expand(n_lanes):
    h = lax.broadcasted_iota(jnp.int32, (LANES, n_lanes), 0)
    l = lax.broadcasted_iota(jnp.int32, (LANES, n_lanes), 1)
    return (jnp.right_shift(l, HEADDIM.bit_length() - 1) == h).astype(BF16)


def _softplus(v):
    return jnp.maximum(v, 0.0) + jnp.log1p(jnp.exp(-jnp.abs(v)))


def _ssd_prep(dt_raw, dt_bias, a_log, n_inner, name):
    T = dt_raw.shape[0]

    def body(r_ref, b_ref, al_ref, dt_ref, cs_ref):
        dt = _softplus(r_ref[...] + b_ref[...])
        a = dt * (-jnp.exp(al_ref[...]))
        i = lax.broadcasted_iota(jnp.int32, (CHUNK, CHUNK), 0)
        j = lax.broadcasted_iota(jnp.int32, (CHUNK, CHUNK), 1)
        tri = (j <= i).astype(BF16)
        cs = sum(_dot(tri, p) for p in _split3(a))
        ex = _head_expand(n_inner)
        dt_ref[...] = sum(_dot(p, ex) for p in _split3(dt))
        cs_ref[...] = sum(_dot(p, ex) for p in _split3(cs))

    blk = pl.BlockSpec((CHUNK, LANES), lambda c: (c, 0))
    out = pl.BlockSpec((CHUNK, n_inner), lambda c: (c, 0))
    return pl.pallas_call(body, grid=(T // CHUNK,), in_specs=[blk, _vec(LANES), _vec(LANES)], out_specs=[out, out],
                          out_shape=[_sds((T, n_inner), F32)] * 2, name=name, compiler_params=_params("parallel"))(dt_raw, dt_bias, a_log)


def _pair_terms(cs_p):
    lane = lax.broadcasted_iota(jnp.int32, (CHUNK, CHUNK), 1)
    sub = lax.broadcasted_iota(jnp.int32, (CHUNK, CHUNK), 0)
    csT = cs_p.T
    Ls = []
    for k in range(2):
        col = jnp.sum(jnp.where(lane == k * HEADDIM, cs_p, 0.0), axis=1, keepdims=True)
        rowv = csT[k * HEADDIM:k * HEADDIM + 1, :]
        Ls.append(jnp.exp(jnp.where(sub >= lane, col - rowv, -jnp.inf)))
    return Ls, jnp.exp(csT[:, CHUNK - 1:CHUNK])


def _block_diag(xp):
    lane = lax.broadcasted_iota(jnp.int32, xp.shape, 1)
    return jnp.concatenate([jnp.where(lane < HEADDIM, xp, 0.0), jnp.where(lane >= HEADDIM, xp, 0.0)], axis=0)


SSD_GROUPS_PER_STEP = 8


def _ssd_specs(T, n_inner):
    nc, gs = T // CHUNK, SSD_GROUPS_PER_STEP
    bo, co = n_inner // (gs * NSTATE), (n_inner + NGROUPS * NSTATE) // (gs * NSTATE)
    assert NGROUPS % gs == 0 and n_inner % (gs * NSTATE) == 0 and (NGROUPS * NSTATE) % (gs * NSTATE) == 0
    g_blk = lambda f: pl.BlockSpec((CHUNK, gs * GROUP_W), lambda c, s: (f(c), s))
    b_blk = lambda f: pl.BlockSpec((CHUNK, gs * NSTATE), lambda c, s: (f(c), bo + s))
    c_blk = lambda f: pl.BlockSpec((CHUNK, gs * NSTATE), lambda c, s: (f(c), co + s))
    return nc, g_blk, b_blk, c_blk


def _ssd_fwd(xbc, dt_e, cs_e, d_e, name, comm=None):
    T = xbc.shape[0]
    n_inner = dt_e.shape[1]
    nc, g_blk, b_blk, c_blk = _ssd_specs(T, n_inner)
    ident = lambda c: c

    gs = SSD_GROUPS_PER_STEP

    def body(xs_ref, b_ref, c_ref, dt_ref, cs_ref, d_ref, y_ref, p_ref, st):
        c, s = pl.program_id(0), pl.program_id(1)

        @pl.when(c == 0)
        def _():
            for gi in range(gs):
                st[s * gs + gi] = jnp.zeros((GROUP_W, NSTATE), F32)

        for gi in range(gs):
            g = s * gs + gi
            gw, gn = slice(gi * GROUP_W, (gi + 1) * GROUP_W), slice(gi * NSTATE, (gi + 1) * NSTATE)
            P = st[g]
            p_ref[0, gi] = P
            xs, dt, cs = xs_ref[:, gw], dt_ref[:, gw], cs_ref[:, gw]
            Bf, Cf = b_ref[:, gn], c_ref[:, gn]
            Cb = Cf.astype(BF16)
            CBm = _dot(Cb, Bf.astype(BF16), NT)
            X = xs * dt
            decay = jnp.exp(cs[CHUNK - 1:CHUNK, :] - cs)
            y_off = _dot(Cb, P.astype(BF16), NT) * jnp.exp(cs)
            ys, ecl = [], []
            for pr in range(2):
                sl = slice(pr * LANES, (pr + 1) * LANES)
                Ls, e_last = _pair_terms(cs[:, sl])
                ecl.append(e_last)
                Mcat = jnp.concatenate([(CBm * L).astype(BF16) for L in Ls], axis=1)
                ys.append(_dot(Mcat, _block_diag(X[:, sl]).astype(BF16)))
            y_ref[:, gw] = jnp.concatenate(ys, axis=1) + y_off + xs * d_ref[:, gw]
            S = _dot3(X * decay, Bf, TN)
            st[g] = P * jnp.concatenate(ecl, axis=0) + S

    p_blk = pl.BlockSpec((1, gs, GROUP_W, NSTATE), lambda c, s: (c, s, 0, 0))
    outs, carried = _call(
        body, grid=(nc, NGROUPS // gs),
        in_specs=[g_blk(ident), b_blk(ident), c_blk(ident), g_blk(ident), g_blk(ident), pl.BlockSpec((1, gs * GROUP_W), lambda c, s: (0, s))],
        out_specs=[g_blk(ident), p_blk],
        out_shape=[_sds((T, n_inner), F32), _sds((nc, NGROUPS, GROUP_W, NSTATE), F32)],
        scratch=[pltpu.VMEM((NGROUPS, GROUP_W, NSTATE), F32)],
        args=[xbc, xbc, xbc, dt_e, cs_e, d_e], name=name, sem=("arbitrary", "arbitrary"), comm=comm)
    return outs if comm is None else (outs, carried)


def _ssd_bwd(xbc, dt_e, cs_e, d_e, states, dy, name, comm=None):
    T = xbc.shape[0]
    n_inner = dt_e.shape[1]
    nc, g_blk, b_blk, c_blk = _ssd_specs(T, n_inner)
    rev = lambda c: nc - 1 - c

    gs = SSD_GROUPS_PER_STEP

    def body(xs_ref, b_ref, c_ref, dt_ref, cs_ref, d_ref, p_ref, pn_ref, dy_ref,
             dxs_ref, db_ref, dc_ref, ddt_ref, dcs_ref, dd_ref, dst):
        cc, s = pl.program_id(0), pl.program_id(1)

        @pl.when(cc == 0)
        def _():
            for gi in range(gs):
                dst[s * gs + gi] = jnp.zeros((GROUP_W, NSTATE), F32)

        for gi in range(gs):
            one_group(s * gs + gi, gi, xs_ref, b_ref, c_ref, dt_ref, cs_ref, d_ref, p_ref, pn_ref, dy_ref,
                      dxs_ref, db_ref, dc_ref, ddt_ref, dcs_ref, dd_ref, dst)

    def one_group(g, gi, xs_ref, b_ref, c_ref, dt_ref, cs_ref, d_ref, p_ref, pn_ref, dy_ref,
                  dxs_ref, db_ref, dc_ref, ddt_ref, dcs_ref, dd_ref, dst):
        gw, gn = slice(gi * GROUP_W, (gi + 1) * GROUP_W), slice(gi * NSTATE, (gi + 1) * NSTATE)
        dS = dst[g]
        P, Pn = p_ref[0, gi], pn_ref[0, gi]
        xs, dt, cs, dY = xs_ref[:, gw], dt_ref[:, gw], cs_ref[:, gw], dy_ref[:, gw]
        Bf, Cf = b_ref[:, gn], c_ref[:, gn]
        Bb, Cb = Bf.astype(BF16), Cf.astype(BF16)
        X = xs * dt
        ecs = jnp.exp(cs)
        decay = jnp.exp(cs[CHUNK - 1:CHUNK, :] - cs)
        CBm = _dot3(Cf, Bf, NT)
        dYe = dY * ecs
        dP_off = _dot3(dYe, Cf, TN)
        dC = _dot(dYe.astype(BF16), P.astype(BF16))
        dcs = dYe * _dot3(Cf, P, NT)
        Xd = X * decay
        dB = _dot(Xd.astype(BF16), dS.astype(BF16))
        E = _dot3(Bf, dS, NT)
        dX = E * decay
        dcs = dcs - E * Xd
        R = _dot3(jnp.ones((8, NSTATE), F32), dS * Pn, NT)
        sub_g = lax.broadcasted_iota(jnp.int32, (CHUNK, GROUP_W), 0)
        dcs = dcs + jnp.where(sub_g == CHUNK - 1, R[0:1, :], 0.0)
        lane = lax.broadcasted_iota(jnp.int32, (CHUNK, CHUNK), 1)
        sub = lax.broadcasted_iota(jnp.int32, (CHUNK, CHUNK), 0)
        dCB = jnp.zeros((CHUNK, CHUNK), F32)
        dXs, dcss, ecl = [], [], []
        for pr in range(2):
            sl = slice(pr * LANES, (pr + 1) * LANES)
            Ls, e_last = _pair_terms(cs[:, sl])
            ecl.append(e_last)
            dYp = dY[:, sl]
            dMcat = _dot3(dYp, _block_diag(X[:, sl]), NT)
            Mcat = jnp.concatenate([CBm * L for L in Ls], axis=1)
            dXt = _dot3(Mcat, dYp, TN)
            dXs.append(jnp.where(lane < HEADDIM, dXt[:CHUNK], dXt[CHUNK:]))
            colacc = jnp.zeros((CHUNK, CHUNK), F32)
            rowacc = jnp.zeros((CHUNK, CHUNK), F32)
            for k in range(2):
                dG = dMcat[:, k * CHUNK:(k + 1) * CHUNK] * Ls[k]
                dCB = dCB + dG
                Q = dG * CBm
                colacc = colacc + jnp.where(lane == k * HEADDIM, jnp.sum(Q, axis=1, keepdims=True), 0.0)
                rowacc = rowacc + jnp.where(sub == k * HEADDIM, jnp.sum(Q, axis=0, keepdims=True), 0.0)
            dcss.append(colacc - rowacc.T)
        dX = dX + jnp.concatenate(dXs, axis=1)
        dcs = dcs + jnp.concatenate(dcss, axis=1)
        dCBb = dCB.astype(BF16)
        dc_ref[:, gn] = dC + _dot(dCBb, Bb)
        db_ref[:, gn] = dB + _dot(dCBb, Cb, TN)
        dxs_ref[:, gw] = dX * dt + dY * d_ref[:, gw]
        ddt_ref[:, gw] = dX * xs
        dcs_ref[:, gw] = dcs
        dd_ref[0, :, gw] = jnp.sum(dY * xs, axis=0, keepdims=True)
        dst[g] = dS * jnp.concatenate(ecl, axis=0) + dP_off

    p_blk = pl.BlockSpec((1, gs, GROUP_W, NSTATE), lambda c, s: (nc - 1 - c, s, 0, 0))
    pn_blk = pl.BlockSpec((1, gs, GROUP_W, NSTATE), lambda c, s: (jnp.minimum(nc - c, nc - 1), s, 0, 0))
    st_blk = pl.BlockSpec((CHUNK, gs * NSTATE), lambda c, s: (nc - 1 - c, s))
    outs, carried = _call(
        body, grid=(nc, NGROUPS // gs),
        in_specs=[g_blk(rev), b_blk(rev), c_blk(rev), g_blk(rev), g_blk(rev), pl.BlockSpec((1, gs * GROUP_W), lambda c, s: (0, s)),
                  p_blk, pn_blk, g_blk(rev)],
        out_specs=[g_blk(rev), st_blk, st_blk, g_blk(rev), g_blk(rev), pl.BlockSpec((1, 1, gs * GROUP_W), lambda c, s: (nc - 1 - c, 0, s))],
        out_shape=[_sds((T, n_inner), F32), _sds((T, NGROUPS * NSTATE), F32), _sds((T, NGROUPS * NSTATE), F32),
                   _sds((T, n_inner), F32), _sds((T, n_inner), F32), _sds((nc, 1, n_inner), F32)],
        scratch=[pltpu.VMEM((NGROUPS, GROUP_W, NSTATE), F32)],
        args=[xbc, xbc, xbc, dt_e, cs_e, d_e, states, states, dy], name=name, sem=("arbitrary", "arbitrary"), comm=comm)
    return outs if comm is None else (outs, carried)


def _ssd_post(ddt_e, dcs_e, dd_p, dt_raw, dt_bias, a_log, n_heads, name):
    T, n_inner = ddt_e.shape

    def body(ddt_ref, dcs_ref, dd_ref, r_ref, b_ref, al_ref, draw_ref, dbias_ref, dal_ref, ddsk_ref):
        @pl.when(pl.program_id(0) == 0)
        def _():
            dbias_ref[...] = jnp.zeros_like(dbias_ref)
            dal_ref[...] = jnp.zeros_like(dal_ref)
            ddsk_ref[...] = jnp.zeros_like(ddsk_ref)

        ex = _head_expand(n_inner)
        red = lambda v: sum(_dot(p, ex, NT) for p in _split3(v))
        raw = r_ref[...] + b_ref[...]
        dt = _softplus(raw)
        A = -jnp.exp(al_ref[...])
        i = lax.broadcasted_iota(jnp.int32, (CHUNK, CHUNK), 0)
        j = lax.broadcasted_iota(jnp.int32, (CHUNK, CHUNK), 1)
        upper = (j >= i).astype(BF16)
        da = sum(_dot(upper, p) for p in _split3(red(dcs_ref[...])))
        ddt = red(ddt_ref[...]) + da * A
        lane = lax.broadcasted_iota(jnp.int32, (CHUNK, LANES), 1)
        draw = jnp.where(lane < n_heads, ddt * jax.nn.sigmoid(raw), 0.0)
        draw_ref[...] = draw.astype(BF16)
        dbias_ref[...] += jnp.sum(draw, axis=0, keepdims=True)
        dal_ref[...] += jnp.sum(da * dt, axis=0, keepdims=True) * A
        ddsk_ref[...] += red(jnp.broadcast_to(dd_ref[0], (8, n_inner)))[0:1, :]

    wide = pl.BlockSpec((CHUNK, n_inner), lambda c: (c, 0))
    blk = pl.BlockSpec((CHUNK, LANES), lambda c: (c, 0))
    return pl.pallas_call(
        body, grid=(T // CHUNK,),
        in_specs=[wide, wide, pl.BlockSpec((1, 1, n_inner), lambda c: (c, 0, 0)), blk, _vec(LANES), _vec(LANES)],
        out_specs=[blk, _vec(LANES), _vec(LANES), _vec(LANES)],
        out_shape=[_sds((T, LANES), BF16)] + [_sds((1, LANES), F32)] * 3,
        name=name, compiler_params=_params("arbitrary"))(ddt_e, dcs_e, dd_p, dt_raw, dt_bias, a_log)


def _row2(v):
    return v.reshape(1, -1).astype(F32)


def _pad_lanes(v):
    return jnp.pad(_row2(v), ((0, 0), (0, LANES - v.shape[-1])))


class _NoExchange:
    def __init__(self, W):
        self.W, self.grads = W, {}

    def weight(self, k):
        return self.W[k]

    def carry(self, name):
        return None

    def carried(self, name, outs):
        pass

    def grad(self, k, g):
        self.grads[k] = g

    def tok(self):
        return jnp.zeros((), F32)

    def point(self, name, value):
        pass


def _local_step(x, tgt, S, small):
    T, D = x.shape

    def mm(a, b, *, name, **kw):
        comm = S.carry(name)
        if comm is None:
            return _mm(a, b, name=name, **kw)
        res, outs = _mm(a, b, name=name, comm=comm, **kw)
        S.carried(name, outs)
        return res

    def carrying(fn, *args, name):
        comm = S.carry(name)
        if comm is None:
            return fn(*args, name)
        res, outs = fn(*args, name, comm=comm)
        S.carried(name, outs)
        return res

    n_inner = 2 * D
    n_heads = n_inner // HEADDIM
    norm_mix, norm_mlp, norm_final = _row2(small["norm_mix"]), _row2(small["norm_mlp"]), _row2(small["norm_final"])
    b_gate, ssm_b, ssm_norm_w = _row2(small["b_gate"]), _row2(small["ssm_conv_b"]), _row2(small["ssm_norm_w"])
    dt_bias, a_log = _pad_lanes(small["dt_bias"]), _pad_lanes(small["A_log"])
    d_e = jnp.repeat(small["D_skip"].astype(F32), HEADDIM).reshape(1, n_inner)

    hb = carrying(_rms_fwd, x, norm_mix, name="rms_mix")
    sc_w, ssm_w = S.weight("sc_conv_w"), S.weight("ssm_conv_w")
    p_xbc = mm(hb, S.weight("xbc"), mode="nn", name="proj_xbc")
    p_dt = mm(hb, S.weight("dt"), mode="nn", name="proj_dt")
    p_z = mm(hb, S.weight("z"), mode="nn", name="proj_z")
    p_sc = mm(hb, S.weight("sc"), mode="nn", name="proj_sc")
    p_gate = mm(hb, S.weight("gate"), mode="nn", name="proj_gate")
    xbc = carrying(_ssm_conv_fwd, p_xbc, ssm_w, ssm_b, name="ssm_conv_fwd")
    dt_e, cs_e = _ssd_prep(p_dt, dt_bias, a_log, n_inner, "ssd_prep")
    ya = _sc_fwd(p_sc, sc_w, "sc_fwd")
    y, states = carrying(_ssd_fwd, xbc, dt_e, cs_e, d_e, name="ssd_fwd")
    S.point("mixers_done", [y, ya, p_gate])
    yb = carrying(_gnorm_fwd, y, p_z, ssm_norm_w, name="gnorm_fwd")
    br_a = mm(ya, S.weight("bsc"), mode="nn", name="branch_sc")
    br_b = mm(yb, S.weight("bssm"), mode="nn", name="branch_ssm")
    merged = _merge_fwd(p_gate, b_gate, br_a, br_b, "merge_fwd")
    x1 = mm(merged, S.weight("out"), mode="nn", name="out_proj", extras=(x,), epi=_epi_add)
    h2 = _rms_fwd(x1, norm_mlp, "rms_mlp")
    r_act = mm(h2, S.weight("w1"), mode="nn", name="mlp_up", epi=_epi_relu2, out_dtypes=(BF16,))
    x2 = mm(r_act, S.weight("w2"), mode="nn", name="mlp_down", extras=(x1,), epi=_epi_add)
    dx2, dx2b, g_norm_final, loss_row = _final(x2, norm_final, tgt, "final")

    S.grad("w2", mm(r_act, dx2b, mode="tn", name="mlp_down_dw", out_dtypes=(BF16,)))
    da = mm(dx2b, S.weight("w2"), mode="nt", name="mlp_down_dx", extras=(r_act,), epi=_epi_relu2_bwd, out_dtypes=(BF16,))
    S.grad("w1", mm(h2, da, mode="tn", name="mlp_up_dw", out_dtypes=(BF16,)))
    dh2 = mm(da, S.weight("w1"), mode="nt", name="mlp_up_dx")
    dx1, dx1b, g_norm_mlp = _rms_bwd(x1, norm_mlp + S.tok(), dh2, dx2, "rms_mlp_bwd")
    S.grad("out", mm(merged, dx1b, mode="tn", name="out_proj_dw", out_dtypes=(BF16,)))
    dmerged = mm(dx1b, S.weight("out"), mode="nt", name="out_proj_dx")
    dbr_a, dbr_b, d_gate, g_b_gate = _merge_bwd(dmerged, p_gate, b_gate, br_a, br_b, "merge_bwd")
    S.grad("bssm", mm(yb, dbr_b, mode="tn", name="branch_ssm_dw", out_dtypes=(BF16,)))
    S.grad("bsc", mm(ya, dbr_a, mode="tn", name="branch_sc_dw", out_dtypes=(BF16,)))
    dyb = mm(dbr_b, S.weight("bssm"), mode="nt", name="branch_ssm_dx")
    dya = mm(dbr_a, S.weight("bsc"), mode="nt", name="branch_sc_dx")
    dy, d_z, g_ssm_norm_w = _gnorm_bwd(y, p_z, ssm_norm_w + S.tok(), dyb, "gnorm_bwd")
    dxs, dB, dC, ddt_e, dcs_e, dd_p = carrying(_ssd_bwd, xbc, dt_e, cs_e, d_e, states, dy, name="ssd_bwd")
    d_dt, g_dt_bias, g_a_log, g_d_skip = _ssd_post(ddt_e, dcs_e, dd_p, p_dt, dt_bias, a_log, n_heads, "ssd_post")
    d_xbc, g_ssm_w, g_ssm_b = carrying(_ssm_conv_bwd, p_xbc, ssm_w, ssm_b, dxs, dB, dC, name="ssm_conv_bwd")
    d_scB, d_scC, d_scX, g_sc_w = _sc_bwd(p_sc, sc_w, dya, "sc_bwd")
    d_sc = jnp.concatenate([d_scB, d_scC, d_scX], axis=1)
    pieces = [("sc", d_sc), ("z", d_z), ("xbc", d_xbc), ("dt", d_dt), ("gate", d_gate)]
    S.grad("win", {k: mm(hb, d, mode="tn", name="proj_dw_" + k, out_dtypes=(BF16,)) for k, d in pieces})
    pieces = [(k, d + S.tok().astype(d.dtype) if k == "dt" else d) for k, d in pieces]
    dh = mm([d for _, d in pieces], [S.weight(k) for k, _ in pieces], mode="nt", name="proj_dx")
    grad_x, _, g_norm_mix = _rms_bwd(x, norm_mix, dh, dx1, "rms_mix_bwd")

    g_small = dict(norm_mix=g_norm_mix, b_gate=g_b_gate, sc_conv_w=g_sc_w, ssm_conv_w=g_ssm_w, ssm_conv_b=g_ssm_b,
                   dt_bias=g_dt_bias, A_log=g_a_log, D_skip=g_d_skip, ssm_norm_w=g_ssm_norm_w, norm_mlp=g_norm_mlp,
                   norm_final=g_norm_final, loss=loss_row)
    return grad_x, g_small


class _Place:
    def __init__(self, k=0):
        x, y, c = lax.axis_index("x"), lax.axis_index("y"), lax.axis_index("c")
        self.x = 1 - x if k & 4 else x
        self.y = 1 - y if k & 2 else y
        self.c = 1 - c if k & 1 else c
        self.chip = 2 * self.x + self.y
        self.id = 2 * self.chip + self.c


ICI_PEERS = (2, 4, 6)
SIBLING = (1,)
ALL_PEERS = (1, 2, 3, 4, 5, 6, 7)


class _Comm:
    def __init__(self, arrs, out_shape, ks, src, dst, own=None, aliases=None):
        self.arrs, self.out_shape, self.ks = list(arrs), list(out_shape), tuple(ks)
        self.n = len(self.arrs)
        self.src, self.dst, self.own = src, dst, own
        self.aliases = aliases or {}
        dma = pltpu.SemaphoreType.DMA
        self.scratch = [dma((self.n, len(self.ks))), dma((self.n, len(self.ks))), dma((self.n,))]

    def _copies(self, ins, outs, sems, with_recvs):
        send_sems, recv_sems, local_sems = sems
        me = _Place()
        owns, sends, recvs = [], [], []
        for a in range(self.n):
            if self.own is not None:
                s, d = self.own(a, ins[a], outs[a], me)
                owns.append(pltpu.make_async_copy(s, d, local_sems.at[a]))
            for i, k in enumerate(self.ks):
                peer = _Place(k)
                for sender, lst in ((me, sends), (peer, recvs)) if with_recvs else ((me, sends),):
                    lst.append(pltpu.make_async_remote_copy(
                        src_ref=self.src(a, ins[a], me, peer), dst_ref=self.dst(a, outs[a], sender),
                        send_sem=send_sems.at[a, i], recv_sem=recv_sems.at[a, i],
                        device_id=(peer.x, peer.y, peer.c), device_id_type=MESH))
        return owns, sends, recvs

    def start(self, ins, outs, sems):
        owns, sends, _ = self._copies(ins, outs, sems, False)
        for cp in owns + sends:
            cp.start()

    def finish(self, ins, outs, sems):
        owns, sends, recvs = self._copies(ins, outs, sems, True)
        for cp in recvs:
            cp.wait_recv()
        for cp in sends:
            cp.wait_send()
        for cp in owns:
            cp.wait()


class _GatherBoth:
    def __init__(self, shards):
        self.arrs, self.n, self.aliases = list(shards), len(shards), {}
        self.out_shape = [_sds((4, 2) + s.shape, s.dtype) for s in shards]
        dma = pltpu.SemaphoreType.DMA
        self.scratch = [dma((self.n, 7)), dma((self.n, 7)), dma((self.n,))]

    def _copy(self, a, j, src, slot, to, outs, sems):
        return pltpu.make_async_remote_copy(src_ref=src, dst_ref=outs[a].at[slot.chip, slot.c], send_sem=sems[0].at[a, j],
                                            recv_sem=sems[1].at[a, j], device_id=(to.x, to.y, to.c), device_id_type=MESH)

    def start(self, ins, outs, sems):
        me, sib = _Place(), _Place(1)
        for a in range(self.n):
            pltpu.make_async_copy(ins[a], outs[a].at[me.chip, me.c], sems[2].at[a]).start()
            self._copy(a, 0, ins[a], me, sib, outs, sems).start()
            for i, k in enumerate(ICI_PEERS):
                self._copy(a, 1 + i, ins[a], me, _Place(k), outs, sems).start()

    def finish(self, ins, outs, sems):
        me, sib = _Place(), _Place(1)
        passed = []
        for i, k in enumerate(ICI_PEERS):
            peer = _Place(k)
            for a in range(self.n):
                self._copy(a, 1 + i, ins[a], peer, peer, outs, sems).wait_recv()
                cp = self._copy(a, 4 + i, outs[a].at[peer.chip, peer.c], peer, sib, outs, sems)
                cp.start()
                passed.append(cp)
        for a in range(self.n):
            self._copy(a, 0, ins[a], sib, sib, outs, sems).wait_recv()
            for i, k in enumerate(ICI_PEERS):
                far = _Place(k | 1)
                self._copy(a, 4 + i, outs[a].at[far.chip, far.c], far, sib, outs, sems).wait_recv()
        for a in range(self.n):
            self._copy(a, 0, ins[a], me, sib, outs, sems).wait_send()
            for i, k in enumerate(ICI_PEERS):
                self._copy(a, 1 + i, ins[a], me, _Place(k), outs, sems).wait_send()
            pltpu.make_async_copy(ins[a], outs[a].at[me.chip, me.c], sems[2].at[a]).wait()
        for cp in passed:
            cp.wait_send()


def _run_comm(comm, name, after=()):
    n, n_after = comm.n, len(after)

    def body(*refs):
        ins, outs, sems = refs[:n], refs[n + n_after:2 * n + n_after], refs[2 * n + n_after:]
        comm.start(ins, outs, sems)
        comm.finish(ins, outs, sems)

    return list(pl.pallas_call(body, in_specs=[ANY] * (n + n_after), out_specs=[ANY] * n, out_shape=comm.out_shape,
                               scratch_shapes=comm.scratch, input_output_aliases=dict(comm.aliases), name=name)(*comm.arrs, *after))


def _gather_ici(shards):
    return _Comm(shards, [_sds((4, 2) + s.shape, s.dtype) for s in shards], ICI_PEERS,
                 src=lambda a, i, me, p: i, dst=lambda a, o, s: o.at[s.chip, s.c], own=lambda a, i, o, me: (i, o.at[me.chip, me.c]))


def _gather_sibling(bufs):
    return _Comm(bufs, [_sds(b.shape, b.dtype) for b in bufs], SIBLING,
                 src=lambda a, i, me, p: i.at[:, me.c], dst=lambda a, o, s: o.at[:, s.c], aliases={a: a for a in range(len(bufs))})


def _scatter_sibling(parts):
    return _Comm(parts, [_sds((4,) + p.shape[2:], p.dtype) for p in parts], SIBLING,
                 src=lambda a, i, me, p: i.at[:, p.c], dst=lambda a, o, s: o)


def _scatter_ici(parts):
    return _Comm(parts, [_sds(p.shape, p.dtype) for p in parts], ICI_PEERS,
                 src=lambda a, i, me, p: i.at[p.chip], dst=lambda a, o, s: o.at[s.chip], own=lambda a, i, o, me: (i.at[me.chip], o.at[me.chip]))


HBM_SPEC = pl.BlockSpec(memory_space=pltpu.HBM)
SEM_SPEC = pl.BlockSpec(memory_space=pltpu.SEMAPHORE)
DATAFLOW = pltpu.SideEffectType.DATAFLOW_SIDE_EFFECTING


def _tiles_2d(R, C, max_rows=256):
    if R % max_rows == 0:
        return max_rows, C, R // max_rows, lambda i: (i, 0)
    if R <= 2 * max_rows or C % 256:
        return R, C, 1, lambda i: (0, 0)
    return R, 256, C // 256, lambda i: (0, i)


def _own_part(parts, name):
    n, R, C = parts.shape
    br, bc, nb, at = _tiles_2d(R, C)
    chip = (2 * lax.axis_index("x") + lax.axis_index("y")).astype(jnp.int32).reshape(1)

    def body(q_ref, p_ref, o_ref):
        o_ref[...] = p_ref[...]

    blk = pl.BlockSpec((1, br, bc), lambda i, q_ref: (q_ref[0],) + at(i))
    spec = pltpu.PrefetchScalarGridSpec(num_scalar_prefetch=1, grid=(nb,), in_specs=[blk], out_specs=blk)
    return pl.pallas_call(body, grid_spec=spec, out_shape=_sds((n, R, C), parts.dtype), name=name,
                          compiler_params=_params("parallel"))(chip, parts)


def _ici_copy(gather, a, srcs, lands, send_sems, recv_sems, i, me, peer, sender):
    src = lands[a].at[me.chip, me.c] if gather else srcs[a].at[peer.chip]
    dst = lands[a].at[sender.chip, sender.c] if gather else lands[a].at[sender.chip]
    j = a * len(ICI_PEERS) + i
    return pltpu.make_async_remote_copy(src_ref=src, dst_ref=dst, send_sem=send_sems.at[j], recv_sem=recv_sems.at[j],
                                        device_id=(peer.x, peer.y, peer.c), device_id_type=MESH)


def _ici_start(srcs, lands, gather, name):
    n, n_s = len(lands), len(srcs)
    bufs = list(srcs) + list(lands)

    def body(*refs):
        src_refs, land_refs = refs[:n_s], refs[n_s:n_s + n]
        send_sems, recv_sems = refs[n_s + n], refs[n_s + n + 1]
        token = refs[-1]
        me = _Place()
        for a in range(n):
            for i, k in enumerate(ICI_PEERS):
                _ici_copy(gather, a, src_refs, land_refs, send_sems, recv_sems, i, me, _Place(k), me).start()
        token[...] = jnp.zeros_like(token)

    dma = pltpu.SemaphoreType.DMA((n * len(ICI_PEERS),))
    outs = pl.pallas_call(
        body, name=name, out_shape=(dma, dma, *[pltpu.HBM(v.shape, v.dtype) for v in bufs], _sds((8, LANES), F32)),
        in_specs=(HBM_SPEC,) * len(bufs),
        out_specs=(SEM_SPEC, SEM_SPEC) + (HBM_SPEC,) * len(bufs) + (pl.BlockSpec(memory_space=pltpu.VMEM),),
        input_output_aliases={j: 2 + j for j in range(len(bufs))}, compiler_params=pltpu.CompilerParams(has_side_effects=DATAFLOW),
    )(*[pltpu.with_memory_space_constraint(v, pltpu.HBM) for v in bufs])
    return outs[0], outs[1], list(outs[2:2 + n_s]), list(outs[2 + n_s:2 + n_s + n]), outs[-1]


def _ici_wait(flight, after, gather, name):
    send_sems, recv_sems, srcs, lands, _ = flight
    n, n_s = len(lands), len(srcs)
    bufs = srcs + lands

    def body(*refs):
        src_refs, land_refs = refs[:n_s], refs[n_s:n_s + n]
        s_sems, r_sems = refs[n_s + n], refs[n_s + n + 1]
        me = _Place()
        for a in range(n):
            for i, k in enumerate(ICI_PEERS):
                peer = _Place(k)
                cp = _ici_copy(gather, a, src_refs, land_refs, s_sems, r_sems, i, me, peer, peer)
                cp.wait_send()
                cp.wait_recv()

    outs = pl.pallas_call(
        body, name=name, out_shape=tuple(pltpu.HBM(v.shape, v.dtype) for v in bufs),
        in_specs=(HBM_SPEC,) * len(bufs) + (SEM_SPEC, SEM_SPEC) + (ANY,) * len(after), out_specs=(HBM_SPEC,) * len(bufs),
        input_output_aliases={j: j for j in range(len(bufs))}, compiler_params=pltpu.CompilerParams(has_side_effects=DATAFLOW),
    )(*bufs, send_sems, recv_sems, *after)
    return list(outs[n_s:])


def _own_shard(shard, after, name):
    R, C = shard.shape
    tr = R if R <= 256 else 256
    place = jnp.stack([2 * lax.axis_index("x") + lax.axis_index("y"), lax.axis_index("c")]).astype(jnp.int32)

    def body(q_ref, s_ref, after_ref, o_ref):
        o_ref[0, 0] = s_ref[...].astype(o_ref.dtype)

    spec = pltpu.PrefetchScalarGridSpec(
        num_scalar_prefetch=1, grid=(R // tr,), in_specs=[pl.BlockSpec((tr, C), lambda i, q_ref: (i, 0)), ANY],
        out_specs=pl.BlockSpec((1, 1, tr, C), lambda i, q_ref: (q_ref[0], q_ref[1], i, 0)))
    return pl.pallas_call(body, grid_spec=spec, out_shape=_sds((4, 2, R, C), BF16), name=name,
                          compiler_params=_params("parallel"))(place, shard, after)


def _col_pieces(widths):
    out, c = [], 0
    for k, w in widths:
        out.append((k, c, w))
        c += w
    return out


def _split_range(c0, n, bounds):
    parts, c = [], c0
    while c < c0 + n:
        r = max(i for i in range(len(bounds) - 1) if bounds[i] <= c)
        w = min(c0 + n, bounds[r + 1]) - c
        parts.append((r, c - bounds[r], w))
        c += w
    return parts


def _win_unpack(g, widths, name):
    n, R, C = g.shape
    tr = min(256, R)
    pieces = _col_pieces(widths)
    padded = [-(-w // LANES) * LANES for _, _, w in pieces]
    shard_bounds = [s * C for s in range(n + 1)]

    def body(g_ref, *o_refs):
        for (k, c0, w), o_ref in zip(pieces, o_refs):
            for t in range(0, o_ref.shape[1], LANES):
                valid = max(0, min(LANES, w - t))
                cols = [g_ref[s, :, o:o + ww] for s, o, ww in _split_range(c0 + t, valid, shard_bounds)] if valid else []
                if valid < LANES:
                    cols.append(jnp.zeros((tr, LANES - valid), g_ref.dtype))
                o_ref[:, t:t + LANES] = cols[0] if len(cols) == 1 else jnp.concatenate(cols, axis=1)

    return pl.pallas_call(
        body, grid=(R // tr,), in_specs=[pl.BlockSpec((n, tr, C), lambda i: (0, i, 0))],
        out_specs=[pl.BlockSpec((tr, p), lambda i: (i, 0)) for p in padded],
        out_shape=[_sds((R, p), g.dtype) for p in padded], name=name, compiler_params=_params("parallel"))(g)


def _win_pack(grads, widths, n, name):
    R = grads[0].shape[0]
    tr = min(256, R)
    pieces = _col_pieces(widths)
    total = pieces[-1][1] + pieces[-1][2]
    C = total // n
    bounds = [c0 for _, c0, _ in pieces] + [total]

    def body(*refs):
        g_refs, o_ref = refs[:-1], refs[-1]

        def tile_t(c0):
            cols = [g_refs[r][:, o:o + ww] for r, o, ww in _split_range(c0, LANES, bounds)]
            tile = cols[0] if len(cols) == 1 else jnp.concatenate(cols, axis=1)
            return tile.astype(F32).T

        for s in range(n):
            full = C // LANES * LANES
            for t in range(0, full, LANES):
                o_ref[s, t:t + LANES, :] = tile_t(s * C + t).astype(o_ref.dtype)
            if full < C:
                o_ref[s, full:C, :] = tile_t(s * C + C - LANES)[LANES - (C - full):, :].astype(o_ref.dtype)

    return pl.pallas_call(
        body, grid=(R // tr,), in_specs=[pl.BlockSpec((tr, gr.shape[1]), lambda i: (i, 0)) for gr in grads],
        out_specs=pl.BlockSpec((n, C, tr), lambda i: (0, 0, i)), out_shape=_sds((n, C, R), grads[0].dtype),
        name=name, compiler_params=_params("parallel"))(*grads)


def _gather_all(arrs):
    return _Comm(arrs, [_sds((N_DEV,) + a.shape, a.dtype) for a in arrs], ALL_PEERS,
                 src=lambda a, i, me, p: i, dst=lambda a, o, s: o.at[s.id], own=lambda a, i, o, me: (i, o.at[me.id]))


def _add_halves(parts, got, name):
    n, _, R, C = parts.shape
    br, bc, nb, at = _tiles_2d(R, C)
    core = lax.axis_index("c").astype(jnp.int32).reshape(1)

    def body(c_ref, p_ref, g_ref, o_ref):
        o_ref[0] = (p_ref[0, 0].astype(F32) + g_ref[0].astype(F32)).astype(o_ref.dtype)

    spec = pltpu.PrefetchScalarGridSpec(
        num_scalar_prefetch=1, grid=(n, nb),
        in_specs=[pl.BlockSpec((1, 1, br, bc), lambda q, i, c_ref: (q, c_ref[0]) + at(i)), pl.BlockSpec((1, br, bc), lambda q, i, c_ref: (q,) + at(i))],
        out_specs=pl.BlockSpec((1, br, bc), lambda q, i, c_ref: (q,) + at(i)))
    return pl.pallas_call(body, grid_spec=spec, out_shape=_sds((n, R, C), parts.dtype), name=name,
                          compiler_params=_params("parallel", "parallel"))(core, parts, got)


def _adam(w, m, v, gparts, name, comm=None):
    R, C = w.shape
    n = gparts.shape[0]
    br, bc, nb, at = _tiles_2d(R, C, max_rows=128)
    c1 = 1.0 / (1.0 - ADAM_B1 ** ADAM_STEP)
    c2 = 1.0 / (1.0 - ADAM_B2 ** ADAM_STEP)

    def body(w_ref, m_ref, v_ref, g_ref, go_ref, d_ref, mo_ref, vo_ref):
        g = g_ref[0].astype(F32)
        for s in range(1, n):
            g = g + g_ref[s].astype(F32)
        mn = ADAM_B1 * m_ref[...] + (1.0 - ADAM_B1) * g
        vn = ADAM_B2 * v_ref[...] + (1.0 - ADAM_B2) * (g * g)
        go_ref[...] = g
        mo_ref[...] = mn
        vo_ref[...] = vn
        d_ref[...] = -ADAM_LR * ((mn * c1) / (jnp.sqrt(vn * c2) + ADAM_EPS) + ADAM_WD * w_ref[...])

    blk = pl.BlockSpec((br, bc), at)
    outs, carried = _call(
        body, grid=(nb,), in_specs=[blk, blk, blk, pl.BlockSpec((n, br, bc), lambda i: (0,) + at(i))],
        out_specs=[blk] * 4, out_shape=[_sds((R, C), F32)] * 4, args=[w, m, v, gparts], name=name, sem=("parallel",), comm=comm)
    return outs if comm is None else (outs, carried)


_SMALL_ORDER = ("norm_mix", "b_gate", "sc_conv_w", "ssm_conv_w", "ssm_conv_b", "dt_bias", "A_log", "D_skip", "ssm_norm_w",
                "norm_mlp", "norm_final", "loss")
_REPLICATED = ("norm_mix", "b_gate", "ssm_conv_b", "dt_bias", "A_log", "D_skip", "ssm_norm_w", "norm_mlp", "norm_final")


def _cols_to_slots(g, n):
    R = g.shape[0]
    return jnp.transpose(g.reshape(R, n, g.shape[1] // n), (1, 0, 2))


def _slots_to_cols(g):
    n, R, C = g.shape
    return jnp.transpose(g, (1, 0, 2)).reshape(R, n * C)


def kernel(x, norm_mix, w_in, b_gate, sc_conv_w, ssm_conv_w, ssm_conv_b, dt_bias, A_log, D_skip, ssm_norm_w, w_branch_sc, w_branch_ssm, w_out, norm_mlp, w_mlp1, w_mlp2, norm_final, loss_target, m_norm_mix, m_w_in, m_b_gate, m_sc_conv_w, m_ssm_conv_w, m_ssm_conv_b, m_dt_bias, m_A_log, m_D_skip, m_ssm_norm_w, m_w_branch_sc, m_w_branch_ssm, m_w_out, m_norm_mlp, m_w_mlp1, m_w_mlp2, m_norm_final, v_norm_mix, v_w_in, v_b_gate, v_sc_conv_w, v_ssm_conv_w, v_ssm_conv_b, v_dt_bias, v_A_log, v_D_skip, v_ssm_norm_w, v_w_branch_sc, v_w_branch_ssm, v_w_out, v_norm_mlp, v_w_mlp1, v_w_mlp2, v_norm_final):
    T, D = x.shape[1], x.shape[2]
    n_inner = 2 * D
    n_heads = n_inner // HEADDIM
    n_xbc = n_inner + 2 * NGROUPS * NSTATE
    me = 4 * lax.axis_index("x") + 2 * lax.axis_index("y") + lax.axis_index("c")

    in_cols = [("sc", 3 * D), ("z", n_inner), ("xbc", n_xbc), ("dt", n_heads), ("gate", 2 * D)]
    by_owner = lambda b: b.reshape((N_DEV,) + b.shape[2:])
    to_owner = lambda g: g.reshape((4, 2) + g.shape[1:])
    rows_of = lambda g: to_owner(g.reshape((N_DEV, g.shape[0] // N_DEV) + g.shape[1:]))
    cols_of = lambda g: to_owner(_cols_to_slots(g, N_DEV))

    class Schedule(_NoExchange):
        late = ("bssm", "bsc", "out", "w1", "w2")
        gather_sib = dict(gnorm_fwd=("bsc", "bssm", "out"), branch_ssm=("w1", "w2"))
        scatter_sib = dict(mlp_up_dx=("w2", "w1"), branch_ssm_dx=("out", "bssm", "bsc"))
        shards = dict(bsc=w_branch_sc, bssm=w_branch_ssm, out=w_out, w1=w_mlp1, w2=w_mlp2)

        def __init__(self):
            self.W, self.staged, self.grads, self.summed, self.scatters = {}, {}, {}, {}, []
            self.token = jnp.zeros((), F32)

        def first_weights(self, bufs):
            self.W.update(zip([k for k, _ in in_cols], _win_unpack(by_owner(bufs[0]), in_cols, "win_unpack")))
            self.W.update(sc_conv_w=_slots_to_cols(by_owner(bufs[1])), ssm_conv_w=_slots_to_cols(by_owner(bufs[2])))
            lands = [_own_shard(self.shards[k], bufs[1], "own_shard_" + k) for k in self.late]
            self.gather_flight = _ici_start([], lands, True, "gather_late_start")
            self.token = self.gather_flight[4][0, 0]
            self.W["dt"] = self.W["dt"] + self.token.astype(BF16)

        def tok(self):
            return self.token

        def point(self, name, values):
            if name == "mixers_done":
                lands = _ici_wait(self.gather_flight, values, True, "gather_late_wait")
                self.staged.update(zip(self.late, lands))

        def carry(self, name):
            if name == "rms_mix":
                return _GatherBoth([w_in.astype(BF16), sc_conv_w, ssm_conv_w])
            if name in self.gather_sib:
                return _gather_sibling([self.staged.pop(k) for k in self.gather_sib[name]])
            if name in self.scatter_sib:
                return _scatter_sibling([self.grads[k] for k in self.scatter_sib[name]])
            return None

        def start_scatter(self, keys, halves):
            lands = [_own_part(h, "own_part_" + k) for k, h in zip(keys, halves)]
            flight = _ici_start(halves, lands, False, "scatter_%s_start" % keys[0])
            self.scatters.append((keys, flight))
            self.token = flight[4][0, 0]

        def carried(self, name, outs):
            if name == "rms_mix":
                self.first_weights(outs)
            elif name in self.gather_sib:
                for k, b in zip(self.gather_sib[name], outs):
                    full = by_owner(b)
                    self.W[k] = _slots_to_cols(full) if k == "w1" else full.reshape(-1, D)
            else:
                keys = self.scatter_sib[name]
                self.start_scatter(keys, [_add_halves(self.grads[k], b, "add_halves_" + k) for k, b in zip(keys, outs)])

        def grad(self, k, g):
            if k == "win":
                g = to_owner(_win_pack([g[k] for k, _ in in_cols], in_cols, N_DEV, "win_pack"))
                got = _run_comm(_scatter_sibling([g]), "scatter_sibling_win")[0]
                self.start_scatter(("win",), [_add_halves(g, got, "add_halves_win")])
            else:
                self.grads[k] = cols_of(g) if k == "w1" else rows_of(g)

        def finish_scatter(self, after):
            keys, flight = self.scatters.pop(0)
            return dict(zip(keys, _ici_wait(flight, after, False, "scatter_%s_wait" % keys[0])))

    S = Schedule()
    small = dict(norm_mix=norm_mix, b_gate=b_gate, ssm_conv_b=ssm_conv_b, dt_bias=dt_bias, A_log=A_log, D_skip=D_skip,
                 ssm_norm_w=ssm_norm_w, norm_mlp=norm_mlp, norm_final=norm_final)
    grad_x, g_small = _local_step(x.reshape(T, D), loss_target.reshape(T, D), S, small)

    small_flat = jnp.concatenate([g_small[k].reshape(-1) for k in _SMALL_ORDER])
    n_small = small_flat.shape[0]
    rows = -(-n_small // (8 * LANES)) * 8
    small_pack = jnp.pad(small_flat, (0, rows * LANES - n_small)).reshape(rows, LANES)

    res = {}
    big = [("w_in", "win", w_in, m_w_in, v_w_in), ("w_branch_sc", "bsc", w_branch_sc, m_w_branch_sc, v_w_branch_sc),
           ("w_branch_ssm", "bssm", w_branch_ssm, m_w_branch_ssm, v_w_branch_ssm), ("w_out", "out", w_out, m_w_out, v_w_out),
           ("w_mlp1", "w1", w_mlp1, m_w_mlp1, v_w_mlp1), ("w_mlp2", "w2", w_mlp2, m_w_mlp2, v_w_mlp2)]
    by_grad = {gk: (k, w, m, v) for k, gk, w, m, v in big}
    after = [grad_x]
    while S.scatters:
        for gk, parts in S.finish_scatter(after).items():
            k, w, m, v = by_grad[gk]
            if gk == "win":
                res_t, (small_parts,) = _adam(w.T, m.T, v.T, parts, "adam_" + k, comm=_gather_all([small_pack]))
                res[k] = [r.T for r in res_t]
            else:
                res[k] = _adam(w, m, v, parts, "adam_" + k)
            after = after + [res[k][1]]

    sizes = {k: g_small[k].size for k in _SMALL_ORDER}
    offs, o = {}, 0
    for k in _SMALL_ORDER:
        offs[k] = o
        o += sizes[k]
    rep_w = dict(norm_mix=norm_mix, b_gate=b_gate, ssm_conv_b=ssm_conv_b, dt_bias=dt_bias, A_log=A_log, D_skip=D_skip,
                 ssm_norm_w=ssm_norm_w, norm_mlp=norm_mlp, norm_final=norm_final)
    rep_m = dict(norm_mix=m_norm_mix, b_gate=m_b_gate, ssm_conv_b=m_ssm_conv_b, dt_bias=m_dt_bias, A_log=m_A_log, D_skip=m_D_skip,
                 ssm_norm_w=m_ssm_norm_w, norm_mlp=m_norm_mlp, norm_final=m_norm_final)
    rep_v = dict(norm_mix=v_norm_mix, b_gate=v_b_gate, ssm_conv_b=v_ssm_conv_b, dt_bias=v_dt_bias, A_log=v_A_log, D_skip=v_D_skip,
                 ssm_norm_w=v_ssm_norm_w, norm_mlp=v_norm_mlp, norm_final=v_norm_final)

    def pack(d):
        segs = [jnp.pad(d[k].astype(F32).reshape(-1), (0, sizes[k] - d[k].size)) if k in d else jnp.zeros((sizes[k],), F32)
                for k in _SMALL_ORDER]
        return jnp.pad(jnp.concatenate(segs), (0, rows * LANES - n_small)).reshape(rows, LANES)

    sm = _adam(pack(rep_w), pack(rep_m), pack(rep_v), small_parts, "adam_small")
    sm = [s.reshape(-1) for s in sm]
    for k in _REPLICATED:
        n_k = rep_w[k].shape[0]
        res[k] = tuple(s[offs[k]:offs[k] + n_k] for s in sm)
    loss = sm[0][offs["loss"]]
    for k, w, m, v, K, full in (("sc_conv_w", sc_conv_w, m_sc_conv_w, v_sc_conv_w, SC_K, D),
                                ("ssm_conv_w", ssm_conv_w, m_ssm_conv_w, v_ssm_conv_w, SSM_K, n_xbc)):
        g_full = sm[0][offs[k]:offs[k] + K * full].reshape(K, full)
        cw = full // N_DEV
        g_mine = lax.dynamic_slice_in_dim(g_full, me * cw, cw, axis=1)
        res[k] = _adam(w, m, v, g_mine[None], "adam_" + k)

    order = ("norm_mix", "w_in", "b_gate", "sc_conv_w", "ssm_conv_w", "ssm_conv_b", "dt_bias", "A_log", "D_skip", "ssm_norm_w",
             "w_branch_sc", "w_branch_ssm", "w_out", "norm_mlp", "w_mlp1", "w_mlp2", "norm_final")
    outs = [loss, grad_x.reshape(1, T, D)]
    for j in range(4):
        outs += [res[k][j] for k in order]
    return tuple(outs)
```

```python
import functools

import jax
import jax.numpy as jnp
from jax import lax
from jax.experimental import pallas as pl
from jax.experimental.pallas import tpu as pltpu

F32 = jnp.float32
BF16 = jnp.bfloat16

EPS = 1e-6
N_DEV = 8
HEADDIM = 64
NSTATE = 128
CHUNK = 128
NGROUPS = 8
GROUP_W = 256
SC_K = 3
SSM_K = 4
LANES = 128

ADAM_LR = 0.001
ADAM_B1 = 0.9
ADAM_B2 = 0.999
ADAM_EPS = 1e-08
ADAM_WD = 0.01
ADAM_STEP = 10

NN = (((1,), (0,)), ((), ()))
NT = (((1,), (1,)), ((), ()))
TN = (((0,), (0,)), ((), ()))
_DIMS = {"nn": NN, "nt": NT, "tn": TN}

ANY = pl.BlockSpec(memory_space=pl.ANY)
MESH = pl.DeviceIdType.MESH


def _sds(shape, dtype):
    return jax.ShapeDtypeStruct(tuple(shape), dtype)


def _dot(a, b, dims=NN):
    return lax.dot_general(a, b, dims, preferred_element_type=F32)


def _dot3(a, b, dims=NN):
    return lax.dot_general(a, b, dims, preferred_element_type=F32, precision=lax.Precision.HIGH)


def _params(*sem):
    return pltpu.CompilerParams(dimension_semantics=tuple(sem))


def _call(body, *, grid, in_specs, out_specs, out_shape, args, name, sem, scratch=(), comm=None):
    if comm is None:
        outs = pl.pallas_call(body, grid=grid, in_specs=list(in_specs), out_specs=list(out_specs), out_shape=list(out_shape),
                              scratch_shapes=list(scratch), name=name, compiler_params=_params(*sem))(*args)
        return list(outs), None
    n, n_in, n_out, n_scr = comm.n, len(in_specs), len(out_shape), len(scratch)

    def wrapped(*refs):
        ins, c_in = refs[:n_in], refs[n_in:n_in + n]
        outs, c_out = refs[n_in + n:n_in + n + n_out], refs[n_in + n + n_out:n_in + 2 * n + n_out]
        rest = refs[n_in + 2 * n + n_out:]
        scr, sems = rest[:n_scr], rest[n_scr:]
        first, last = None, None
        for d, g in enumerate(grid):
            f, l = pl.program_id(d) == 0, pl.program_id(d) == g - 1
            first, last = (f, l) if first is None else (first & f, last & l)

        @pl.when(first)
        def _():
            comm.start(c_in, c_out, sems)

        body(*ins, *outs, *scr)

        @pl.when(last)
        def _():
            comm.finish(c_in, c_out, sems)

    outs = pl.pallas_call(
        wrapped, grid=grid, in_specs=list(in_specs) + [ANY] * n, out_specs=list(out_specs) + [ANY] * n,
        out_shape=list(out_shape) + comm.out_shape, scratch_shapes=list(scratch) + comm.scratch,
        input_output_aliases={n_in + i: n_out + o for i, o in comm.aliases.items()},
        name=name, compiler_params=_params(*["arbitrary"] * len(grid)))(*args, *comm.arrs)
    return list(outs[:n_out]), list(outs[n_out:])


MM_VMEM_BUDGET = 44 * 2 ** 20


def _mm_tiles(M, N, k_bytes, mn_bytes):
    best = None
    for tm in (2048, 1024, 512, 256, 128):
        for tn in (1024, 512, 256, 128):
            if M % tm or N % tn:
                continue
            need = 2 * ((tm + tn) * k_bytes + tm * tn * mn_bytes) + 4 * tm * tn * 4
            if need <= MM_VMEM_BUDGET and (best is None or (tm * tn, tm) > (best[0] * best[1], best[0])):
                best = (tm, tn)
    assert best is not None, (M, N, k_bytes, mn_bytes)
    return best


def _mm(a, b, *, mode, name, extras=(), epi=None, out_dtypes=(F32,), comm=None):
    a_list = list(a) if isinstance(a, (list, tuple)) else [a]
    b_list = list(b) if isinstance(b, (list, tuple)) else [b]
    if mode == "nn":
        M, N = a_list[0].shape[0], b_list[0].shape[1]
    elif mode == "nt":
        M, N = a_list[0].shape[0], b_list[0].shape[0]
    else:
        M, N = a_list[0].shape[1], b_list[0].shape[1]
    k_bytes = sum((av.shape[0] if mode == "tn" else av.shape[1]) * av.dtype.itemsize for av in a_list)
    mn_bytes = sum(e.dtype.itemsize for e in extras) + sum(jnp.dtype(d).itemsize for d in out_dtypes)
    tm, tn = _mm_tiles(min(M, 2048), min(N, 1024), k_bytes, mn_bytes) if M % 128 == 0 and N % 128 == 0 else (M, N)
    assert M % tm == 0 and N % tn == 0
    a_specs, b_specs = [], []
    for av, bv in zip(a_list, b_list):
        K = av.shape[0] if mode == "tn" else av.shape[1]
        a_specs.append(pl.BlockSpec((K, tm), lambda i, j: (0, i)) if mode == "tn" else pl.BlockSpec((tm, K), lambda i, j: (i, 0)))
        b_specs.append(pl.BlockSpec((tn, K), lambda i, j: (j, 0)) if mode == "nt" else pl.BlockSpec((K, tn), lambda i, j: (0, j)))
    mn_spec = pl.BlockSpec((tm, tn), lambda i, j: (i, j))
    n_p, n_ex = len(a_list), len(extras)
    dims = _DIMS[mode]

    def body(*refs):
        acc = _dot(refs[0][...], refs[n_p][...], dims)
        for p in range(1, n_p):
            acc = acc + _dot(refs[p][...], refs[n_p + p][...], dims)
        rest = refs[2 * n_p:]
        res = (acc,) if epi is None else epi(acc, *[r[...] for r in rest[:n_ex]])
        for o_ref, r in zip(rest[n_ex:], res):
            o_ref[...] = r.astype(o_ref.dtype)

    outs, carried = _call(
        body, grid=(M // tm, N // tn), in_specs=a_specs + b_specs + [mn_spec] * n_ex,
        out_specs=[mn_spec] * len(out_dtypes), out_shape=[_sds((M, N), d) for d in out_dtypes],
        args=a_list + b_list + list(extras), name=name, sem=("parallel", "parallel"), comm=comm)
    res = outs[0] if len(outs) == 1 else outs
    return res if comm is None else (res, carried)


def _epi_add(acc, r):
    return (acc + r,)


def _epi_add2(acc, r):
    s = acc + r
    return (s, s)


def _epi_relu2(acc):
    p = jnp.maximum(acc, 0.0)
    return (p * p,)


def _epi_relu2_bwd(acc, r):
    return (acc * (2.0 * jnp.sqrt(r.astype(F32))),)


def _row(tr, n):
    return pl.BlockSpec((tr, n), lambda i: (i, 0))


def _vec(n):
    return pl.BlockSpec((1, n), lambda i: (0, 0))


def _rms_fwd(x, w, name, comm=None):
    T, D = x.shape
    tr = min(256, T)

    def body(x_ref, w_ref, o_ref):
        xv = x_ref[...]
        r = lax.rsqrt(jnp.mean(xv * xv, axis=-1, keepdims=True) + EPS)
        o_ref[...] = (xv * r * w_ref[...]).astype(BF16)

    outs, carried = _call(body, grid=(T // tr,), in_specs=[_row(tr, D), _vec(D)], out_specs=[_row(tr, D)],
                          out_shape=[_sds((T, D), BF16)], args=[x, w], name=name, sem=("parallel",), comm=comm)
    return outs[0] if comm is None else (outs[0], carried)


def _rms_bwd(x, w, dh, dres, name):
    T, D = x.shape
    tr = min(256, T)

    def body(x_ref, w_ref, dh_ref, dres_ref, dx_ref, dxb_ref, dw_ref):
        @pl.when(pl.program_id(0) == 0)
        def _():
            dw_ref[...] = jnp.zeros_like(dw_ref)

        xv = x_ref[...]
        r = lax.rsqrt(jnp.mean(xv * xv, axis=-1, keepdims=True) + EPS)
        xh = xv * r
        dh_v = dh_ref[...]
        dw_ref[...] += jnp.sum(dh_v * xh, axis=0, keepdims=True)
        dxh = dh_v * w_ref[...]
        dx = r * (dxh - xh * jnp.mean(dxh * xh, axis=-1, keepdims=True)) + dres_ref[...]
        dx_ref[...] = dx
        dxb_ref[...] = dx.astype(BF16)

    return pl.pallas_call(
        body, grid=(T // tr,), in_specs=[_row(tr, D), _vec(D), _row(tr, D), _row(tr, D)],
        out_specs=[_row(tr, D), _row(tr, D), _vec(D)],
        out_shape=[_sds((T, D), F32), _sds((T, D), BF16), _sds((1, D), F32)],
        name=name, compiler_params=_params("arbitrary"))(x, w, dh, dres)


def _final(x2, w, tgt, name):
    T, D = x2.shape
    tr = min(256, T)

    def body(x_ref, w_ref, t_ref, dx_ref, dxb_ref, dw_ref, loss_ref):
        @pl.when(pl.program_id(0) == 0)
        def _():
            dw_ref[...] = jnp.zeros_like(dw_ref)
            loss_ref[...] = jnp.zeros_like(loss_ref)

        xv = x_ref[...]
        wv = w_ref[...]
        r = lax.rsqrt(jnp.mean(xv * xv, axis=-1, keepdims=True) + EPS)
        xh = xv * r
        err = xh * wv - t_ref[...]
        part = jnp.sum(jnp.sum(err * err, axis=1, keepdims=True), axis=0, keepdims=True) * (0.5 / D)
        loss_ref[...] += jnp.broadcast_to(part, loss_ref.shape)
        dy = err * (1.0 / D)
        dw_ref[...] += jnp.sum(dy * xh, axis=0, keepdims=True)
        dxh = dy * wv
        dx = r * (dxh - xh * jnp.mean(dxh * xh, axis=-1, keepdims=True))
        dx_ref[...] = dx
        dxb_ref[...] = dx.astype(BF16)

    return pl.pallas_call(
        body, grid=(T // tr,), in_specs=[_row(tr, D), _vec(D), _row(tr, D)],
        out_specs=[_row(tr, D), _row(tr, D), _vec(D), _vec(LANES)],
        out_shape=[_sds((T, D), F32), _sds((T, D), BF16), _sds((1, D), F32), _sds((1, LANES), F32)],
        name=name, compiler_params=_params("arbitrary"))(x2, w, tgt)


def _silu_parts(z):
    s = jax.nn.sigmoid(z)
    return z * s, s * (1.0 + z * (1.0 - s))


def _gnorm_fwd(y, z, w, name, comm=None):
    T, N = y.shape
    tr = min(256, T)

    def body(y_ref, z_ref, w_ref, o_ref):
        for g in range(N // GROUP_W):
            sl = slice(g * GROUP_W, (g + 1) * GROUP_W)
            silu, _ = _silu_parts(z_ref[:, sl])
            yz = y_ref[:, sl] * silu
            r = lax.rsqrt(jnp.mean(yz * yz, axis=-1, keepdims=True) + EPS)
            o_ref[:, sl] = (yz * r * w_ref[:, sl]).astype(BF16)

    outs, carried = _call(body, grid=(T // tr,), in_specs=[_row(tr, N), _row(tr, N), _vec(N)], out_specs=[_row(tr, N)],
                          out_shape=[_sds((T, N), BF16)], args=[y, z, w], name=name, sem=("parallel",), comm=comm)
    return outs[0] if comm is None else (outs[0], carried)


def _gnorm_bwd(y, z, w, dyb, name):
    T, N = y.shape
    tr = min(256, T)

    def body(y_ref, z_ref, w_ref, d_ref, dy_ref, dz_ref, dw_ref):
        @pl.when(pl.program_id(0) == 0)
        def _():
            dw_ref[...] = jnp.zeros_like(dw_ref)

        for g in range(N // GROUP_W):
            sl = slice(g * GROUP_W, (g + 1) * GROUP_W)
            yv = y_ref[:, sl]
            silu, dsilu = _silu_parts(z_ref[:, sl])
            yz = yv * silu
            r = lax.rsqrt(jnp.mean(yz * yz, axis=-1, keepdims=True) + EPS)
            yzh = yz * r
            d = d_ref[:, sl]
            dw_ref[:, sl] += jnp.sum(d * yzh, axis=0, keepdims=True)
            dyzh = d * w_ref[:, sl]
            dyz = r * (dyzh - yzh * jnp.mean(dyzh * yzh, axis=-1, keepdims=True))
            dy_ref[:, sl] = dyz * silu
            dz_ref[:, sl] = (dyz * yv * dsilu).astype(BF16)

    return pl.pallas_call(
        body, grid=(T // tr,), in_specs=[_row(tr, N), _row(tr, N), _vec(N), _row(tr, N)],
        out_specs=[_row(tr, N), _row(tr, N), _vec(N)],
        out_shape=[_sds((T, N), F32), _sds((T, N), BF16), _sds((1, N), F32)],
        name=name, compiler_params=_params("arbitrary"))(y, z, w, dyb)


def _merge_fwd(gate_raw, b_gate, br_a, br_b, name):
    T, D = br_a.shape
    tr = min(256, T)

    def body(g_ref, bg_ref, a_ref, b_ref, o_ref):
        g = jax.nn.sigmoid(g_ref[...] + bg_ref[...])
        o_ref[...] = (g[:, :D] * a_ref[...] + g[:, D:] * b_ref[...]).astype(BF16)

    return pl.pallas_call(body, grid=(T // tr,), in_specs=[_row(tr, 2 * D), _vec(2 * D), _row(tr, D), _row(tr, D)],
                          out_specs=_row(tr, D), out_shape=_sds((T, D), BF16), name=name,
                          compiler_params=_params("parallel"))(gate_raw, b_gate, br_a, br_b)


def _merge_bwd(dmerged, gate_raw, b_gate, br_a, br_b, name):
    T, D = br_a.shape
    tr = min(256, T)

    def body(d_ref, g_ref, bg_ref, a_ref, b_ref, da_ref, db_ref, dg_ref, dbg_ref):
        @pl.when(pl.program_id(0) == 0)
        def _():
            dbg_ref[...] = jnp.zeros_like(dbg_ref)

        g = jax.nn.sigmoid(g_ref[...] + bg_ref[...])
        d = d_ref[...]
        da_ref[...] = (d * g[:, :D]).astype(BF16)
        db_ref[...] = (d * g[:, D:]).astype(BF16)
        dg = jnp.concatenate([d * a_ref[...], d * b_ref[...]], axis=1) * g * (1.0 - g)
        dg_ref[...] = dg.astype(BF16)
        dbg_ref[...] += jnp.sum(dg, axis=0, keepdims=True)

    return pl.pallas_call(
        body, grid=(T // tr,), in_specs=[_row(tr, D), _row(tr, 2 * D), _vec(2 * D), _row(tr, D), _row(tr, D)],
        out_specs=[_row(tr, D), _row(tr, D), _row(tr, 2 * D), _vec(2 * D)],
        out_shape=[_sds((T, D), BF16), _sds((T, D), BF16), _sds((T, 2 * D), BF16), _sds((1, 2 * D), F32)],
        name=name, compiler_params=_params("arbitrary"))(dmerged, gate_raw, b_gate, br_a, br_b)


CB_W = 256
CONV_ROWS = 32
CONV_PAD = 8


def _rows_down(load, r0, s):
    if s == 0:
        return load(r0, r0 + CONV_ROWS)
    if r0 == 0:
        row = lax.broadcasted_iota(jnp.int32, (CONV_ROWS, CB_W), 0)
        return jnp.where(row >= s, pltpu.roll(load(0, CONV_ROWS), s, 0), 0.0)
    return load(r0 - s, r0 - s + CONV_ROWS)


def _conv_tile(load, taps, r0):
    K = len(taps)
    us = [_rows_down(load, r0, K - 1 - k) for k in range(K)]
    acc = us[K - 1] * taps[K - 1]
    for k in range(K - 1):
        acc = acc + us[k] * taps[k]
    return acc, us


def _conv_back_tile(scr, taps, r0):
    K = len(taps)
    du = scr[r0:r0 + CONV_ROWS, :] * taps[K - 1]
    for k in range(K - 1):
        s = K - 1 - k
        du = du + scr[r0 + s:r0 + s + CONV_ROWS, :] * taps[k]
    return du


def _fold8(v):
    return jnp.sum(v.reshape(CONV_ROWS // 8, 8, v.shape[1]), axis=0)


def _col(T, j0=0):
    return pl.BlockSpec((T, CB_W), lambda j: (0, j + j0))


def _sc_fwd(psc, w, name):
    T, D = psc.shape[0], psc.shape[1] // 3
    nb = D // CB_W

    def body(b_ref, c_ref, x_ref, w_ref, o_ref):
        taps = [w_ref[k:k + 1, :] for k in range(SC_K)]
        load = lambda a, b: c_ref[a:b, :] * x_ref[a:b, :]
        for r0 in range(0, T, CONV_ROWS):
            cu, _ = _conv_tile(load, taps, r0)
            o_ref[r0:r0 + CONV_ROWS, :] = (b_ref[r0:r0 + CONV_ROWS, :] * cu).astype(BF16)

    return pl.pallas_call(
        body, grid=(nb,), in_specs=[_col(T), _col(T, nb), _col(T, 2 * nb), pl.BlockSpec((SC_K, CB_W), lambda j: (0, j))],
        out_specs=_col(T), out_shape=_sds((T, D), BF16), name=name, compiler_params=_params("parallel"))(psc, psc, psc, w)


def _sc_bwd(psc, w, dya, name):
    T, D = psc.shape[0], psc.shape[1] // 3
    nb = D // CB_W

    def body(b_ref, c_ref, x_ref, w_ref, d_ref, db_ref, dc_ref, dx_ref, dw_ref, scr):
        taps = [w_ref[k:k + 1, :] for k in range(SC_K)]
        load = lambda a, b: c_ref[a:b, :] * x_ref[a:b, :]
        scr[T:T + CONV_PAD, :] = jnp.zeros((CONV_PAD, CB_W), F32)
        dw8 = [jnp.zeros((8, CB_W), F32)] * SC_K
        for r0 in range(0, T, CONV_ROWS):
            rows = slice(r0, r0 + CONV_ROWS)
            cu, us = _conv_tile(load, taps, r0)
            d = d_ref[rows, :]
            db_ref[rows, :] = (d * cu).astype(BF16)
            dcu = d * b_ref[rows, :]
            scr[rows, :] = dcu
            dw8 = [acc + _fold8(dcu * u) for acc, u in zip(dw8, us)]
        for k in range(SC_K):
            dw_ref[k:k + 1, :] = jnp.sum(dw8[k], axis=0, keepdims=True)
        for r0 in range(0, T, CONV_ROWS):
            rows = slice(r0, r0 + CONV_ROWS)
            du = _conv_back_tile(scr, taps, r0)
            dc_ref[rows, :] = (du * x_ref[rows, :]).astype(BF16)
            dx_ref[rows, :] = (du * c_ref[rows, :]).astype(BF16)

    wspec = pl.BlockSpec((SC_K, CB_W), lambda j: (0, j))
    return pl.pallas_call(
        body, grid=(nb,), in_specs=[_col(T), _col(T, nb), _col(T, 2 * nb), wspec, _col(T)],
        out_specs=[_col(T), _col(T), _col(T), wspec],
        out_shape=[_sds((T, D), BF16)] * 3 + [_sds((SC_K, D), F32)],
        scratch_shapes=[pltpu.VMEM((T + CONV_PAD, CB_W), F32)],
        name=name, compiler_params=_params("parallel"))(psc, psc, psc, w, dya)


def _ssm_conv_fwd(u, w, b, name, comm=None):
    T, N = u.shape

    def body(u_ref, w_ref, b_ref, o_ref):
        taps = [w_ref[k:k + 1, :] for k in range(SSM_K)]
        bias = b_ref[...]
        for r0 in range(0, T, CONV_ROWS):
            c, _ = _conv_tile(lambda a, b: u_ref[a:b, :], taps, r0)
            c = c + bias
            o_ref[r0:r0 + CONV_ROWS, :] = c * jax.nn.sigmoid(c)

    outs, carried = _call(
        body, grid=(N // CB_W,), in_specs=[_col(T), pl.BlockSpec((SSM_K, CB_W), lambda j: (0, j)), pl.BlockSpec((1, CB_W), lambda j: (0, j))],
        out_specs=[_col(T)], out_shape=[_sds((T, N), F32)], args=[u, w, b], name=name, sem=("parallel",), comm=comm)
    return outs[0] if comm is None else (outs[0], carried)


def _ssm_conv_bwd(u, w, b, dxs, dB, dC, name, comm=None):
    T, N = u.shape
    n_x, n_b = dxs.shape[1] // CB_W, dB.shape[1] // CB_W

    def body(u_ref, w_ref, b_ref, dx_ref, db_ref, dc_ref, du_ref, dw_ref, dbias_ref, scr):
        j = pl.program_id(0)
        taps = [w_ref[k:k + 1, :] for k in range(SSM_K)]
        bias = b_ref[...]
        scr[T:T + CONV_PAD, :] = jnp.zeros((CONV_PAD, CB_W), F32)
        dw8 = [jnp.zeros((8, CB_W), F32)] * SSM_K
        db8 = jnp.zeros((8, CB_W), F32)
        for r0 in range(0, T, CONV_ROWS):
            rows = slice(r0, r0 + CONV_ROWS)
            c, us = _conv_tile(lambda a, b: u_ref[a:b, :], taps, r0)
            _, dsilu = _silu_parts(c + bias)
            d = jnp.where(j < n_x, dx_ref[rows, :], jnp.where(j < n_x + n_b, db_ref[rows, :], dc_ref[rows, :])) * dsilu
            scr[rows, :] = d
            db8 = db8 + _fold8(d)
            dw8 = [acc + _fold8(d * u) for acc, u in zip(dw8, us)]
        dbias_ref[...] = jnp.sum(db8, axis=0, keepdims=True)
        for k in range(SSM_K):
            dw_ref[k:k + 1, :] = jnp.sum(dw8[k], axis=0, keepdims=True)
        for r0 in range(0, T, CONV_ROWS):
            du_ref[r0:r0 + CONV_ROWS, :] = _conv_back_tile(scr, taps, r0).astype(BF16)

    wspec = pl.BlockSpec((SSM_K, CB_W), lambda j: (0, j))
    bspec = pl.BlockSpec((1, CB_W), lambda j: (0, j))
    outs, carried = _call(
        body, grid=(N // CB_W,),
        in_specs=[_col(T), wspec, bspec,
                  pl.BlockSpec((T, CB_W), lambda j: (0, jnp.minimum(j, n_x - 1))),
                  pl.BlockSpec((T, CB_W), lambda j: (0, jnp.clip(j - n_x, 0, n_b - 1))),
                  pl.BlockSpec((T, CB_W), lambda j: (0, jnp.clip(j - n_x - n_b, 0, n_b - 1)))],
        out_specs=[_col(T), wspec, bspec],
        out_shape=[_sds((T, N), BF16), _sds((SSM_K, N), F32), _sds((1, N), F32)],
        scratch=[pltpu.VMEM((T + CONV_PAD, CB_W), F32)],
        args=[u, w, b, dxs, dB, dC], name=name, sem=("parallel",), comm=comm)
    return outs if comm is None else (outs, carried)


def _split3(v):
    hi = v.astype(BF16)
    r = v - hi.astype(F32)
    mid = r.astype(BF16)
    lo = (r - mid.astype(F32)).astype(BF16)
    return hi, mid, lo


def _head_expand(n_lanes):
    h = lax.broadcasted_iota(jnp.int32, (LANES, n_lanes), 0)
    l = lax.broadcasted_iota(jnp.int32, (LANES, n_lanes), 1)
    return (jnp.right_shift(l, HEADDIM.bit_length() - 1) == h).astype(BF16)


def _softplus(v):
    return jnp.maximum(v, 0.0) + jnp.log1p(jnp.exp(-jnp.abs(v)))


def _ssd_prep(dt_raw, dt_bias, a_log, n_inner, name):
    T = dt_raw.shape[0]

    def body(r_ref, b_ref, al_ref, ex_ref, dt_ref, cs_ref):
        dt = _softplus(r_ref[...] + b_ref[...])
        a = dt * (-jnp.exp(al_ref[...]))
        i = lax.broadcasted_iota(jnp.int32, (CHUNK, CHUNK), 0)
        j = lax.broadcasted_iota(jnp.int32, (CHUNK, CHUNK), 1)
        tri = (j <= i).astype(BF16)
        cs = sum(_dot(tri, p) for p in _split3(a))
        ex = ex_ref[...]
        dt_ref[...] = sum(_dot(p, ex) for p in _split3(dt))
        cs_ref[...] = sum(_dot(p, ex) for p in _split3(cs))

    blk = pl.BlockSpec((CHUNK, LANES), lambda c: (c, 0))
    out = pl.BlockSpec((CHUNK, n_inner), lambda c: (c, 0))
    ex_spec = pl.BlockSpec((LANES, n_inner), lambda c: (0, 0))
    return pl.pallas_call(body, grid=(T // CHUNK,), in_specs=[blk, _vec(LANES), _vec(LANES), ex_spec], out_specs=[out, out],
                          out_shape=[_sds((T, n_inner), F32)] * 2, name=name,
                          compiler_params=_params("parallel"))(dt_raw, dt_bias, a_log, _head_expand(n_inner))


def _pair_terms(cs_p):
    lane = lax.broadcasted_iota(jnp.int32, (CHUNK, CHUNK), 1)
    sub = lax.broadcasted_iota(jnp.int32, (CHUNK, CHUNK), 0)
    csT = cs_p.T
    Ls = []
    for k in range(2):
        col = jnp.sum(jnp.where(lane == k * HEADDIM, cs_p, 0.0), axis=1, keepdims=True)
        rowv = csT[k * HEADDIM:k * HEADDIM + 1, :]
        Ls.append(jnp.exp(jnp.where(sub >= lane, col - rowv, -jnp.inf)))
    return Ls, jnp.exp(csT[:, CHUNK - 1:CHUNK])


def _block_diag(xp):
    lane = lax.broadcasted_iota(jnp.int32, xp.shape, 1)
    return jnp.concatenate([jnp.where(lane < HEADDIM, xp, 0.0), jnp.where(lane >= HEADDIM, xp, 0.0)], axis=0)


SSD_GROUPS_PER_STEP = 8


def _ssd_specs(T, n_inner):
    nc, gs = T // CHUNK, SSD_GROUPS_PER_STEP
    bo, co = n_inner // (gs * NSTATE), (n_inner + NGROUPS * NSTATE) // (gs * NSTATE)
    assert NGROUPS % gs == 0 and n_inner % (gs * NSTATE) == 0 and (NGROUPS * NSTATE) % (gs * NSTATE) == 0
    g_blk = lambda f: pl.BlockSpec((CHUNK, gs * GROUP_W), lambda c, s: (f(c), s))
    b_blk = lambda f: pl.BlockSpec((CHUNK, gs * NSTATE), lambda c, s: (f(c), bo + s))
    c_blk = lambda f: pl.BlockSpec((CHUNK, gs * NSTATE), lambda c, s: (f(c), co + s))
    return nc, g_blk, b_blk, c_blk


def _ssd_fwd(xbc, dt_e, cs_e, d_e, name, comm=None):
    T = xbc.shape[0]
    n_inner = dt_e.shape[1]
    nc, g_blk, b_blk, c_blk = _ssd_specs(T, n_inner)
    ident = lambda c: c

    gs = SSD_GROUPS_PER_STEP

    def body(xs_ref, b_ref, c_ref, dt_ref, cs_ref, d_ref, y_ref, p_ref, st):
        c, s = pl.program_id(0), pl.program_id(1)

        @pl.when(c == 0)
        def _():
            for gi in range(gs):
                st[s * gs + gi] = jnp.zeros((GROUP_W, NSTATE), F32)

        for gi in range(gs):
            g = s * gs + gi
            gw, gn = slice(gi * GROUP_W, (gi + 1) * GROUP_W), slice(gi * NSTATE, (gi + 1) * NSTATE)
            P = st[g]
            p_ref[0, gi] = P
            xs, dt, cs = xs_ref[:, gw], dt_ref[:, gw], cs_ref[:, gw]
            Bf, Cf = b_ref[:, gn], c_ref[:, gn]
            Cb = Cf.astype(BF16)
            CBm = _dot(Cb, Bf.astype(BF16), NT)
            X = xs * dt
            decay = jnp.exp(cs[CHUNK - 1:CHUNK, :] - cs)
            y_off = _dot(Cb, P.astype(BF16), NT) * jnp.exp(cs)
            ys, ecl = [], []
            for pr in range(2):
                sl = slice(pr * LANES, (pr + 1) * LANES)
                Ls, e_last = _pair_terms(cs[:, sl])
                ecl.append(e_last)
                Mcat = jnp.concatenate([(CBm * L).astype(BF16) for L in Ls], axis=1)
                ys.append(_dot(Mcat, _block_diag(X[:, sl]).astype(BF16)))
            y_ref[:, gw] = jnp.concatenate(ys, axis=1) + y_off + xs * d_ref[:, gw]
            S = _dot3(X * decay, Bf, TN)
            st[g] = P * jnp.concatenate(ecl, axis=0) + S

    p_blk = pl.BlockSpec((1, gs, GROUP_W, NSTATE), lambda c, s: (c, s, 0, 0))
    outs, carried = _call(
        body, grid=(nc, NGROUPS // gs),
        in_specs=[g_blk(ident), b_blk(ident), c_blk(ident), g_blk(ident), g_blk(ident), pl.BlockSpec((1, gs * GROUP_W), lambda c, s: (0, s))],
        out_specs=[g_blk(ident), p_blk],
        out_shape=[_sds((T, n_inner), F32), _sds((nc, NGROUPS, GROUP_W, NSTATE), F32)],
        scratch=[pltpu.VMEM((NGROUPS, GROUP_W, NSTATE), F32)],
        args=[xbc, xbc, xbc, dt_e, cs_e, d_e], name=name, sem=("arbitrary", "arbitrary"), comm=comm)
    return outs if comm is None else (outs, carried)


def _ssd_bwd(xbc, dt_e, cs_e, d_e, states, dy, name, comm=None):
    T = xbc.shape[0]
    n_inner = dt_e.shape[1]
    nc, g_blk, b_blk, c_blk = _ssd_specs(T, n_inner)
    rev = lambda c: nc - 1 - c

    gs = SSD_GROUPS_PER_STEP

    def body(xs_ref, b_ref, c_ref, dt_ref, cs_ref, d_ref, p_ref, pn_ref, dy_ref,
             dxs_ref, db_ref, dc_ref, ddt_ref, dcs_ref, dd_ref, dst):
        cc, s = pl.program_id(0), pl.program_id(1)

        @pl.when(cc == 0)
        def _():
            for gi in range(gs):
                dst[s * gs + gi] = jnp.zeros((GROUP_W, NSTATE), F32)

        for gi in range(gs):
            one_group(s * gs + gi, gi, xs_ref, b_ref, c_ref, dt_ref, cs_ref, d_ref, p_ref, pn_ref, dy_ref,
                      dxs_ref, db_ref, dc_ref, ddt_ref, dcs_ref, dd_ref, dst)

    def one_group(g, gi, xs_ref, b_ref, c_ref, dt_ref, cs_ref, d_ref, p_ref, pn_ref, dy_ref,
                  dxs_ref, db_ref, dc_ref, ddt_ref, dcs_ref, dd_ref, dst):
        gw, gn = slice(gi * GROUP_W, (gi + 1) * GROUP_W), slice(gi * NSTATE, (gi + 1) * NSTATE)
        dS = dst[g]
        P, Pn = p_ref[0, gi], pn_ref[0, gi]
        xs, dt, cs, dY = xs_ref[:, gw], dt_ref[:, gw], cs_ref[:, gw], dy_ref[:, gw]
        Bf, Cf = b_ref[:, gn], c_ref[:, gn]
        Bb, Cb = Bf.astype(BF16), Cf.astype(BF16)
        X = xs * dt
        ecs = jnp.exp(cs)
        decay = jnp.exp(cs[CHUNK - 1:CHUNK, :] - cs)
        CBm = _dot3(Cf, Bf, NT)
        dYe = dY * ecs
        dP_off = _dot3(dYe, Cf, TN)
        dC = _dot(dYe.astype(BF16), P.astype(BF16))
        dcs = dYe * _dot3(Cf, P, NT)
        Xd = X * decay
        dB = _dot(Xd.astype(BF16), dS.astype(BF16))
        E = _dot3(Bf, dS, NT)
        dX = E * decay
        dcs = dcs - E * Xd
        R = _dot3(jnp.ones((8, NSTATE), F32), dS * Pn, NT)
        sub_g = lax.broadcasted_iota(jnp.int32, (CHUNK, GROUP_W), 0)
        dcs = dcs + jnp.where(sub_g == CHUNK - 1, R[0:1, :], 0.0)
        lane = lax.broadcasted_iota(jnp.int32, (CHUNK, CHUNK), 1)
        sub = lax.broadcasted_iota(jnp.int32, (CHUNK, CHUNK), 0)
        dCB = jnp.zeros((CHUNK, CHUNK), F32)
        dXs, dcss, ecl = [], [], []
        for pr in range(2):
            sl = slice(pr * LANES, (pr + 1) * LANES)
            Ls, e_last = _pair_terms(cs[:, sl])
            ecl.append(e_last)
            dYp = dY[:, sl]
            dMcat = _dot3(dYp, _block_diag(X[:, sl]), NT)
            Mcat = jnp.concatenate([CBm * L for L in Ls], axis=1)
            dXt = _dot3(Mcat, dYp, TN)
            dXs.append(jnp.where(lane < HEADDIM, dXt[:CHUNK], dXt[CHUNK:]))
            colacc = jnp.zeros((CHUNK, CHUNK), F32)
            rowacc = jnp.zeros((CHUNK, CHUNK), F32)
            for k in range(2):
                dG = dMcat[:, k * CHUNK:(k + 1) * CHUNK] * Ls[k]
                dCB = dCB + dG
                Q = dG * CBm
                colacc = colacc + jnp.where(lane == k * HEADDIM, jnp.sum(Q, axis=1, keepdims=True), 0.0)
                rowacc = rowacc + jnp.where(sub == k * HEADDIM, jnp.sum(Q, axis=0, keepdims=True), 0.0)
            dcss.append(colacc - rowacc.T)
        dX = dX + jnp.concatenate(dXs, axis=1)
        dcs = dcs + jnp.concatenate(dcss, axis=1)
        dCBb = dCB.astype(BF16)
        dc_ref[:, gn] = dC + _dot(dCBb, Bb)
        db_ref[:, gn] = dB + _dot(dCBb, Cb, TN)
        dxs_ref[:, gw] = dX * dt + dY * d_ref[:, gw]
        ddt_ref[:, gw] = dX * xs
        dcs_ref[:, gw] = dcs
        dd_ref[0, :, gw] = jnp.sum(dY * xs, axis=0, keepdims=True)
        dst[g] = dS * jnp.concatenate(ecl, axis=0) + dP_off

    p_blk = pl.BlockSpec((1, gs, GROUP_W, NSTATE), lambda c, s: (nc - 1 - c, s, 0, 0))
    pn_blk = pl.BlockSpec((1, gs, GROUP_W, NSTATE), lambda c, s: (jnp.minimum(nc - c, nc - 1), s, 0, 0))
    st_blk = pl.BlockSpec((CHUNK, gs * NSTATE), lambda c, s: (nc - 1 - c, s))
    outs, carried = _call(
        body, grid=(nc, NGROUPS // gs),
        in_specs=[g_blk(rev), b_blk(rev), c_blk(rev), g_blk(rev), g_blk(rev), pl.BlockSpec((1, gs * GROUP_W), lambda c, s: (0, s)),
                  p_blk, pn_blk, g_blk(rev)],
        out_specs=[g_blk(rev), st_blk, st_blk, g_blk(rev), g_blk(rev), pl.BlockSpec((1, 1, gs * GROUP_W), lambda c, s: (nc - 1 - c, 0, s))],
        out_shape=[_sds((T, n_inner), F32), _sds((T, NGROUPS * NSTATE), F32), _sds((T, NGROUPS * NSTATE), F32),
                   _sds((T, n_inner), F32), _sds((T, n_inner), F32), _sds((nc, 1, n_inner), F32)],
        scratch=[pltpu.VMEM((NGROUPS, GROUP_W, NSTATE), F32)],
        args=[xbc, xbc, xbc, dt_e, cs_e, d_e, states, states, dy], name=name, sem=("arbitrary", "arbitrary"), comm=comm)
    return outs if comm is None else (outs, carried)


def _ssd_post(ddt_e, dcs_e, dd_p, dt_raw, dt_bias, a_log, n_heads, name):
    T, n_inner = ddt_e.shape

    def body(ddt_ref, dcs_ref, dd_ref, r_ref, b_ref, al_ref, ex_ref, draw_ref, dbias_ref, dal_ref, ddsk_ref):
        @pl.when(pl.program_id(0) == 0)
        def _():
            dbias_ref[...] = jnp.zeros_like(dbias_ref)
            dal_ref[...] = jnp.zeros_like(dal_ref)
            ddsk_ref[...] = jnp.zeros_like(ddsk_ref)

        spread = [ddt_ref[...], dcs_ref[...], jnp.broadcast_to(dd_ref[0], (8, n_inner))]
        stacked = _dot(jnp.concatenate([p for v in spread for p in _split3(v)], axis=0), ex_ref[...], NT)
        sums, r0 = [], 0
        for v in spread:
            n = v.shape[0]
            sums.append(stacked[r0:r0 + n] + stacked[r0 + n:r0 + 2 * n] + stacked[r0 + 2 * n:r0 + 3 * n])
            r0 += 3 * n
        ddt_h, dcs_h, dd_h = sums
        raw = r_ref[...] + b_ref[...]
        dt = _softplus(raw)
        A = -jnp.exp(al_ref[...])
        i = lax.broadcasted_iota(jnp.int32, (CHUNK, CHUNK), 0)
        j = lax.broadcasted_iota(jnp.int32, (CHUNK, CHUNK), 1)
        upper = (j >= i).astype(BF16)
        da = sum(_dot(upper, p) for p in _split3(dcs_h))
        ddt = ddt_h + da * A
        lane = lax.broadcasted_iota(jnp.int32, (CHUNK, LANES), 1)
        draw = jnp.where(lane < n_heads, ddt * jax.nn.sigmoid(raw), 0.0)
        draw_ref[...] = draw.astype(BF16)
        dbias_ref[...] += jnp.sum(draw, axis=0, keepdims=True)
        dal_ref[...] += jnp.sum(da * dt, axis=0, keepdims=True) * A
        ddsk_ref[...] += dd_h[0:1, :]

    wide = pl.BlockSpec((CHUNK, n_inner), lambda c: (c, 0))
    blk = pl.BlockSpec((CHUNK, LANES), lambda c: (c, 0))
    return pl.pallas_call(
        body, grid=(T // CHUNK,),
        in_specs=[wide, wide, pl.BlockSpec((1, 1, n_inner), lambda c: (c, 0, 0)), blk, _vec(LANES), _vec(LANES),
                  pl.BlockSpec((LANES, n_inner), lambda c: (0, 0))],
        out_specs=[blk, _vec(LANES), _vec(LANES), _vec(LANES)],
        out_shape=[_sds((T, LANES), BF16)] + [_sds((1, LANES), F32)] * 3,
        name=name, compiler_params=_params("arbitrary"))(ddt_e, dcs_e, dd_p, dt_raw, dt_bias, a_log, _head_expand(n_inner))


def _row2(v):
    return v.reshape(1, -1).astype(F32)


def _pad_lanes(v):
    return jnp.pad(_row2(v), ((0, 0), (0, LANES - v.shape[-1])))


class _NoExchange:
    def __init__(self, W):
        self.W, self.grads = W, {}

    def weight(self, k):
        return self.W[k]

    def carry(self, name):
        return None

    def carried(self, name, outs):
        pass

    def grad(self, k, g):
        self.grads[k] = g

    def tok(self):
        return jnp.zeros((), F32)

    def point(self, name, value):
        pass


def _local_step(x, tgt, S, small):
    T, D = x.shape

    def mm(a, b, *, name, **kw):
        comm = S.carry(name)
        if comm is None:
            return _mm(a, b, name=name, **kw)
        res, outs = _mm(a, b, name=name, comm=comm, **kw)
        S.carried(name, outs)
        return res

    def carrying(fn, *args, name):
        comm = S.carry(name)
        if comm is None:
            return fn(*args, name)
        res, outs = fn(*args, name, comm=comm)
        S.carried(name, outs)
        return res

    n_inner = 2 * D
    n_heads = n_inner // HEADDIM
    norm_mix, norm_mlp, norm_final = _row2(small["norm_mix"]), _row2(small["norm_mlp"]), _row2(small["norm_final"])
    b_gate, ssm_b, ssm_norm_w = _row2(small["b_gate"]), _row2(small["ssm_conv_b"]), _row2(small["ssm_norm_w"])
    dt_bias, a_log = _pad_lanes(small["dt_bias"]), _pad_lanes(small["A_log"])
    d_e = jnp.repeat(small["D_skip"].astype(F32), HEADDIM).reshape(1, n_inner)

    hb = carrying(_rms_fwd, x, norm_mix, name="rms_mix")
    sc_w, ssm_w = S.weight("sc_conv_w"), S.weight("ssm_conv_w")
    p_xbc = mm(hb, S.weight("xbc"), mode="nn", name="proj_xbc")
    p_dt = mm(hb, S.weight("dt"), mode="nn", name="proj_dt")
    p_z = mm(hb, S.weight("z"), mode="nn", name="proj_z")
    p_sc = mm(hb, S.weight("sc"), mode="nn", name="proj_sc")
    p_gate = mm(hb, S.weight("gate"), mode="nn", name="proj_gate")
    xbc = carrying(_ssm_conv_fwd, p_xbc, ssm_w, ssm_b, name="ssm_conv_fwd")
    dt_e, cs_e = _ssd_prep(p_dt, dt_bias, a_log, n_inner, "ssd_prep")
    ya = _sc_fwd(p_sc, sc_w, "sc_fwd")
    y, states = carrying(_ssd_fwd, xbc, dt_e, cs_e, d_e, name="ssd_fwd")
    S.point("mixers_done", [y, ya, p_gate])
    yb = carrying(_gnorm_fwd, y, p_z, ssm_norm_w, name="gnorm_fwd")
    br_a = mm(ya, S.weight("bsc"), mode="nn", name="branch_sc")
    br_b = mm(yb, S.weight("bssm"), mode="nn", name="branch_ssm")
    merged = _merge_fwd(p_gate, b_gate, br_a, br_b, "merge_fwd")
    x1 = mm(merged, S.weight("out"), mode="nn", name="out_proj", extras=(x,), epi=_epi_add)
    h2 = _rms_fwd(x1, norm_mlp, "rms_mlp")
    r_act = mm(h2, S.weight("w1"), mode="nn", name="mlp_up", epi=_epi_relu2, out_dtypes=(BF16,))
    x2 = mm(r_act, S.weight("w2"), mode="nn", name="mlp_down", extras=(x1,), epi=_epi_add)
    dx2, dx2b, g_norm_final, loss_row = _final(x2, norm_final, tgt, "final")

    S.grad("w2", mm(r_act, dx2b, mode="tn", name="mlp_down_dw", out_dtypes=(BF16,)))
    da = mm(dx2b, S.weight("w2"), mode="nt", name="mlp_down_dx", extras=(r_act,), epi=_epi_relu2_bwd, out_dtypes=(BF16,))
    S.grad("w1", mm(h2, da, mode="tn", name="mlp_up_dw", out_dtypes=(BF16,)))
    dh2 = mm(da, S.weight("w1"), mode="nt", name="mlp_up_dx")
    dx1, dx1b, g_norm_mlp = _rms_bwd(x1, norm_mlp + S.tok(), dh2, dx2, "rms_mlp_bwd")
    S.grad("out", mm(merged, dx1b, mode="tn", name="out_proj_dw", out_dtypes=(BF16,)))
    dmerged = mm(dx1b, S.weight("out"), mode="nt", name="out_proj_dx")
    dbr_a, dbr_b, d_gate, g_b_gate = _merge_bwd(dmerged, p_gate, b_gate, br_a, br_b, "merge_bwd")
    S.grad("bssm", mm(yb, dbr_b, mode="tn", name="branch_ssm_dw", out_dtypes=(BF16,)))
    S.grad("bsc", mm(ya, dbr_a, mode="tn", name="branch_sc_dw", out_dtypes=(BF16,)))
    dyb = mm(dbr_b, S.weight("bssm"), mode="nt", name="branch_ssm_dx")
    dya = mm(dbr_a, S.weight("bsc"), mode="nt", name="branch_sc_dx")
    dy, d_z, g_ssm_norm_w = _gnorm_bwd(y, p_z, ssm_norm_w + S.tok(), dyb, "gnorm_bwd")
    dxs, dB, dC, ddt_e, dcs_e, dd_p = carrying(_ssd_bwd, xbc, dt_e, cs_e, d_e, states, dy, name="ssd_bwd")
    d_dt, g_dt_bias, g_a_log, g_d_skip = _ssd_post(ddt_e, dcs_e, dd_p, p_dt, dt_bias, a_log, n_heads, "ssd_post")
    d_xbc, g_ssm_w, g_ssm_b = carrying(_ssm_conv_bwd, p_xbc, ssm_w, ssm_b, dxs, dB, dC, name="ssm_conv_bwd")
    d_scB, d_scC, d_scX, g_sc_w = _sc_bwd(p_sc, sc_w, dya, "sc_bwd")
    d_sc = jnp.concatenate([d_scB, d_scC, d_scX], axis=1)
    pieces = [("sc", d_sc), ("z", d_z), ("xbc", d_xbc), ("dt", d_dt), ("gate", d_gate)]
    S.grad("win", {k: mm(hb, d, mode="tn", name="proj_dw_" + k, out_dtypes=(BF16,)) for k, d in pieces})
    pieces = [(k, d + S.tok().astype(d.dtype) if k == "dt" else d) for k, d in pieces]
    dh = mm([d for _, d in pieces], [S.weight(k) for k, _ in pieces], mode="nt", name="proj_dx")
    grad_x, _, g_norm_mix = _rms_bwd(x, norm_mix, dh, dx1, "rms_mix_bwd")

    g_small = dict(norm_mix=g_norm_mix, b_gate=g_b_gate, sc_conv_w=g_sc_w, ssm_conv_w=g_ssm_w, ssm_conv_b=g_ssm_b,
                   dt_bias=g_dt_bias, A_log=g_a_log, D_skip=g_d_skip, ssm_norm_w=g_ssm_norm_w, norm_mlp=g_norm_mlp,
                   norm_final=g_norm_final, loss=loss_row)
    return grad_x, g_small


class _Place:
    def __init__(self, k=0):
        x, y, c = lax.axis_index("x"), lax.axis_index("y"), lax.axis_index("c")
        self.x = 1 - x if k & 4 else x
        self.y = 1 - y if k & 2 else y
        self.c = 1 - c if k & 1 else c
        self.chip = 2 * self.x + self.y
        self.id = 2 * self.chip + self.c


ICI_PEERS = (2, 4, 6)
SIBLING = (1,)
ALL_PEERS = (1, 2, 3, 4, 5, 6, 7)


class _Comm:
    def __init__(self, arrs, out_shape, ks, src, dst, own=None, aliases=None):
        self.arrs, self.out_shape, self.ks = list(arrs), list(out_shape), tuple(ks)
        self.n = len(self.arrs)
        self.src, self.dst, self.own = src, dst, own
        self.aliases = aliases or {}
        dma = pltpu.SemaphoreType.DMA
        self.scratch = [dma((self.n, len(self.ks))), dma((self.n, len(self.ks))), dma((self.n,))]

    def _copies(self, ins, outs, sems, with_recvs):
        send_sems, recv_sems, local_sems = sems
        me = _Place()
        owns, sends, recvs = [], [], []
        for a in range(self.n):
            if self.own is not None:
                s, d = self.own(a, ins[a], outs[a], me)
                owns.append(pltpu.make_async_copy(s, d, local_sems.at[a]))
            for i, k in enumerate(self.ks):
                peer = _Place(k)
                for sender, lst in ((me, sends), (peer, recvs)) if with_recvs else ((me, sends),):
                    lst.append(pltpu.make_async_remote_copy(
                        src_ref=self.src(a, ins[a], me, peer), dst_ref=self.dst(a, outs[a], sender),
                        send_sem=send_sems.at[a, i], recv_sem=recv_sems.at[a, i],
                        device_id=(peer.x, peer.y, peer.c), device_id_type=MESH))
        return owns, sends, recvs

    def start(self, ins, outs, sems):
        owns, sends, _ = self._copies(ins, outs, sems, False)
        for cp in owns + sends:
            cp.start()

    def finish(self, ins, outs, sems):
        owns, sends, recvs = self._copies(ins, outs, sems, True)
        for cp in recvs:
            cp.wait_recv()
        for cp in sends:
            cp.wait_send()
        for cp in owns:
            cp.wait()


class _GatherBoth:
    def __init__(self, shards):
        self.arrs, self.n, self.aliases = list(shards), len(shards), {}
        self.out_shape = [_sds((4, 2) + s.shape, s.dtype) for s in shards]
        dma = pltpu.SemaphoreType.DMA
        self.scratch = [dma((self.n, 7)), dma((self.n, 7)), dma((self.n,))]

    def _copy(self, a, j, src, slot, to, outs, sems):
        return pltpu.make_async_remote_copy(src_ref=src, dst_ref=outs[a].at[slot.chip, slot.c], send_sem=sems[0].at[a, j],
                                            recv_sem=sems[1].at[a, j], device_id=(to.x, to.y, to.c), device_id_type=MESH)

    def start(self, ins, outs, sems):
        me, sib = _Place(), _Place(1)
        for a in range(self.n):
            pltpu.make_async_copy(ins[a], outs[a].at[me.chip, me.c], sems[2].at[a]).start()
            self._copy(a, 0, ins[a], me, sib, outs, sems).start()
            for i, k in enumerate(ICI_PEERS):
                self._copy(a, 1 + i, ins[a], me, _Place(k), outs, sems).start()

    def finish(self, ins, outs, sems):
        me, sib = _Place(), _Place(1)
        passed = []
        for i, k in enumerate(ICI_PEERS):
            peer = _Place(k)
            for a in range(self.n):
                self._copy(a, 1 + i, ins[a], peer, peer, outs, sems).wait_recv()
                cp = self._copy(a, 4 + i, outs[a].at[peer.chip, peer.c], peer, sib, outs, sems)
                cp.start()
                passed.append(cp)
        for a in range(self.n):
            self._copy(a, 0, ins[a], sib, sib, outs, sems).wait_recv()
            for i, k in enumerate(ICI_PEERS):
                far = _Place(k | 1)
                self._copy(a, 4 + i, outs[a].at[far.chip, far.c], far, sib, outs, sems).wait_recv()
        for a in range(self.n):
            self._copy(a, 0, ins[a], me, sib, outs, sems).wait_send()
            for i, k in enumerate(ICI_PEERS):
                self._copy(a, 1 + i, ins[a], me, _Place(k), outs, sems).wait_send()
            pltpu.make_async_copy(ins[a], outs[a].at[me.chip, me.c], sems[2].at[a]).wait()
        for cp in passed:
            cp.wait_send()


def _run_comm(comm, name, after=()):
    n, n_after = comm.n, len(after)

    def body(*refs):
        ins, outs, sems = refs[:n], refs[n + n_after:2 * n + n_after], refs[2 * n + n_after:]
        comm.start(ins, outs, sems)
        comm.finish(ins, outs, sems)

    return list(pl.pallas_call(body, in_specs=[ANY] * (n + n_after), out_specs=[ANY] * n, out_shape=comm.out_shape,
                               scratch_shapes=comm.scratch, input_output_aliases=dict(comm.aliases), name=name)(*comm.arrs, *after))


def _gather_ici(shards):
    return _Comm(shards, [_sds((4, 2) + s.shape, s.dtype) for s in shards], ICI_PEERS,
                 src=lambda a, i, me, p: i, dst=lambda a, o, s: o.at[s.chip, s.c], own=lambda a, i, o, me: (i, o.at[me.chip, me.c]))


def _gather_sibling(bufs):
    return _Comm(bufs, [_sds(b.shape, b.dtype) for b in bufs], SIBLING,
                 src=lambda a, i, me, p: i.at[:, me.c], dst=lambda a, o, s: o.at[:, s.c], aliases={a: a for a in range(len(bufs))})


def _scatter_sibling(parts):
    return _Comm(parts, [_sds((4,) + p.shape[2:], p.dtype) for p in parts], SIBLING,
                 src=lambda a, i, me, p: i.at[:, p.c], dst=lambda a, o, s: o)


def _scatter_ici(parts):
    return _Comm(parts, [_sds(p.shape, p.dtype) for p in parts], ICI_PEERS,
                 src=lambda a, i, me, p: i.at[p.chip], dst=lambda a, o, s: o.at[s.chip], own=lambda a, i, o, me: (i.at[me.chip], o.at[me.chip]))


HBM_SPEC = pl.BlockSpec(memory_space=pltpu.HBM)
SEM_SPEC = pl.BlockSpec(memory_space=pltpu.SEMAPHORE)
DATAFLOW = pltpu.SideEffectType.DATAFLOW_SIDE_EFFECTING


def _tiles_2d(R, C, max_rows=256):
    if R % max_rows == 0:
        return max_rows, C, R // max_rows, lambda i: (i, 0)
    if R <= 2 * max_rows or C % 256:
        return R, C, 1, lambda i: (0, 0)
    return R, 256, C // 256, lambda i: (0, i)


def _ici_copy(gather, a, srcs, lands, send_sems, recv_sems, i, me, peer, sender):
    src = lands[a].at[me.chip, me.c] if gather else srcs[a].at[peer.chip]
    dst = lands[a].at[sender.chip, sender.c] if gather else lands[a].at[sender.chip]
    j = a * len(ICI_PEERS) + i
    return pltpu.make_async_remote_copy(src_ref=src, dst_ref=dst, send_sem=send_sems.at[j], recv_sem=recv_sems.at[j],
                                        device_id=(peer.x, peer.y, peer.c), device_id_type=MESH)


def _ici_start(srcs, lands, gather, name):
    n, n_s = len(lands), len(srcs)
    bufs = list(srcs) + list(lands)

    def body(*refs):
        src_refs, land_refs = refs[:n_s], refs[n_s:n_s + n]
        send_sems, recv_sems = refs[n_s + n], refs[n_s + n + 1]
        token = refs[-1]
        me = _Place()
        for a in range(n):
            for i, k in enumerate(ICI_PEERS):
                _ici_copy(gather, a, src_refs, land_refs, send_sems, recv_sems, i, me, _Place(k), me).start()
        token[...] = jnp.zeros_like(token)

    dma = pltpu.SemaphoreType.DMA((n * len(ICI_PEERS),))
    outs = pl.pallas_call(
        body, name=name, out_shape=(dma, dma, *[pltpu.HBM(v.shape, v.dtype) for v in bufs], _sds((8, LANES), F32)),
        in_specs=(HBM_SPEC,) * len(bufs),
        out_specs=(SEM_SPEC, SEM_SPEC) + (HBM_SPEC,) * len(bufs) + (pl.BlockSpec(memory_space=pltpu.VMEM),),
        input_output_aliases={j: 2 + j for j in range(len(bufs))}, compiler_params=pltpu.CompilerParams(has_side_effects=DATAFLOW),
    )(*[pltpu.with_memory_space_constraint(v, pltpu.HBM) for v in bufs])
    return outs[0], outs[1], list(outs[2:2 + n_s]), list(outs[2 + n_s:2 + n_s + n]), outs[-1]


def _ici_wait(flight, after, gather, name):
    send_sems, recv_sems, srcs, lands, _ = flight
    n, n_s = len(lands), len(srcs)
    bufs = srcs + lands

    def body(*refs):
        src_refs, land_refs = refs[:n_s], refs[n_s:n_s + n]
        s_sems, r_sems = refs[n_s + n], refs[n_s + n + 1]
        me = _Place()
        for a in range(n):
            for i, k in enumerate(ICI_PEERS):
                peer = _Place(k)
                cp = _ici_copy(gather, a, src_refs, land_refs, s_sems, r_sems, i, me, peer, peer)
                cp.wait_send()
                cp.wait_recv()

    outs = pl.pallas_call(
        body, name=name, out_shape=tuple(pltpu.HBM(v.shape, v.dtype) for v in bufs),
        in_specs=(HBM_SPEC,) * len(bufs) + (SEM_SPEC, SEM_SPEC) + (ANY,) * len(after), out_specs=(HBM_SPEC,) * len(bufs),
        input_output_aliases={j: j for j in range(len(bufs))}, compiler_params=pltpu.CompilerParams(has_side_effects=DATAFLOW),
    )(*bufs, send_sems, recv_sems, *after)
    return list(outs[n_s:])


def _own_shard(shard, after, name):
    R, C = shard.shape
    tr = R if R <= 256 else 256
    place = jnp.stack([2 * lax.axis_index("x") + lax.axis_index("y"), lax.axis_index("c")]).astype(jnp.int32)

    def body(q_ref, s_ref, after_ref, o_ref):
        o_ref[0, 0] = s_ref[...].astype(o_ref.dtype)

    spec = pltpu.PrefetchScalarGridSpec(
        num_scalar_prefetch=1, grid=(R // tr,), in_specs=[pl.BlockSpec((tr, C), lambda i, q_ref: (i, 0)), ANY],
        out_specs=pl.BlockSpec((1, 1, tr, C), lambda i, q_ref: (q_ref[0], q_ref[1], i, 0)))
    return pl.pallas_call(body, grid_spec=spec, out_shape=_sds((4, 2, R, C), BF16), name=name,
                          compiler_params=_params("parallel"))(place, shard, after)


def _col_pieces(widths):
    out, c = [], 0
    for k, w in widths:
        out.append((k, c, w))
        c += w
    return out


def _split_range(c0, n, bounds):
    parts, c = [], c0
    while c < c0 + n:
        r = max(i for i in range(len(bounds) - 1) if bounds[i] <= c)
        w = min(c0 + n, bounds[r + 1]) - c
        parts.append((r, c - bounds[r], w))
        c += w
    return parts


def _win_unpack(g, widths, name):
    n, R, C = g.shape
    tr = min(256, R)
    pieces = _col_pieces(widths)
    padded = [-(-w // LANES) * LANES for _, _, w in pieces]
    shard_bounds = [s * C for s in range(n + 1)]

    def body(g_ref, *o_refs):
        for (k, c0, w), o_ref in zip(pieces, o_refs):
            for t in range(0, o_ref.shape[1], LANES):
                valid = max(0, min(LANES, w - t))
                cols = [g_ref[s, :, o:o + ww] for s, o, ww in _split_range(c0 + t, valid, shard_bounds)] if valid else []
                if valid < LANES:
                    cols.append(jnp.zeros((tr, LANES - valid), g_ref.dtype))
                o_ref[:, t:t + LANES] = cols[0] if len(cols) == 1 else jnp.concatenate(cols, axis=1)

    return pl.pallas_call(
        body, grid=(R // tr,), in_specs=[pl.BlockSpec((n, tr, C), lambda i: (0, i, 0))],
        out_specs=[pl.BlockSpec((tr, p), lambda i: (i, 0)) for p in padded],
        out_shape=[_sds((R, p), g.dtype) for p in padded], name=name, compiler_params=_params("parallel"))(g)


def _win_pack(grads, widths, n, name):
    R = grads[0].shape[0]
    tr = min(256, R)
    pieces = _col_pieces(widths)
    total = pieces[-1][1] + pieces[-1][2]
    C = total // n
    bounds = [c0 for _, c0, _ in pieces] + [total]

    def body(*refs):
        g_refs, o_ref = refs[:-1], refs[-1]

        def tile_t(c0):
            cols = [g_refs[r][:, o:o + ww] for r, o, ww in _split_range(c0, LANES, bounds)]
            tile = cols[0] if len(cols) == 1 else jnp.concatenate(cols, axis=1)
            return tile.astype(F32).T

        for s in range(n):
            full = C // LANES * LANES
            for t in range(0, full, LANES):
                o_ref[s, t:t + LANES, :] = tile_t(s * C + t).astype(o_ref.dtype)
            if full < C:
                o_ref[s, full:C, :] = tile_t(s * C + C - LANES)[LANES - (C - full):, :].astype(o_ref.dtype)

    return pl.pallas_call(
        body, grid=(R // tr,), in_specs=[pl.BlockSpec((tr, gr.shape[1]), lambda i: (i, 0)) for gr in grads],
        out_specs=pl.BlockSpec((n, C, tr), lambda i: (0, 0, i)), out_shape=_sds((n, C, R), grads[0].dtype),
        name=name, compiler_params=_params("parallel"))(*grads)


def _gather_all(arrs):
    return _Comm(arrs, [_sds((N_DEV,) + a.shape, a.dtype) for a in arrs], ALL_PEERS,
                 src=lambda a, i, me, p: i, dst=lambda a, o, s: o.at[s.id], own=lambda a, i, o, me: (i, o.at[me.id]))


def _add_halves(parts, got, name):
    n, _, R, C = parts.shape
    br, bc, nb, at = _tiles_2d(R, C)
    place = jnp.stack([lax.axis_index("c"), 2 * lax.axis_index("x") + lax.axis_index("y")]).astype(jnp.int32)

    def body(q_ref, p_ref, g_ref, o_ref, land_ref):
        s = (p_ref[0, 0].astype(F32) + g_ref[0].astype(F32)).astype(o_ref.dtype)
        o_ref[0] = s

        @pl.when(pl.program_id(1) == q_ref[1])
        def _():
            land_ref[0] = s

    spec = pltpu.PrefetchScalarGridSpec(
        num_scalar_prefetch=1, grid=(nb, n),
        in_specs=[pl.BlockSpec((1, 1, br, bc), lambda i, q, q_ref: (q, q_ref[0]) + at(i)), pl.BlockSpec((1, br, bc), lambda i, q, q_ref: (q,) + at(i))],
        out_specs=[pl.BlockSpec((1, br, bc), lambda i, q, q_ref: (q,) + at(i)), pl.BlockSpec((1, br, bc), lambda i, q, q_ref: (q_ref[1],) + at(i))])
    return pl.pallas_call(body, grid_spec=spec, out_shape=[_sds((n, R, C), parts.dtype)] * 2, name=name,
                          compiler_params=_params("parallel", "arbitrary"))(place, parts, got)


def _adam(w, m, v, gparts, name, comm=None):
    R, C = w.shape
    n = gparts.shape[0]
    br, bc, nb, at = _tiles_2d(R, C, max_rows=128)
    c1 = 1.0 / (1.0 - ADAM_B1 ** ADAM_STEP)
    c2 = 1.0 / (1.0 - ADAM_B2 ** ADAM_STEP)

    def body(w_ref, m_ref, v_ref, g_ref, go_ref, d_ref, mo_ref, vo_ref):
        g = g_ref[0].astype(F32)
        for s in range(1, n):
            g = g + g_ref[s].astype(F32)
        mn = ADAM_B1 * m_ref[...] + (1.0 - ADAM_B1) * g
        vn = ADAM_B2 * v_ref[...] + (1.0 - ADAM_B2) * (g * g)
        go_ref[...] = g
        mo_ref[...] = mn
        vo_ref[...] = vn
        d_ref[...] = -ADAM_LR * ((mn * c1) / (jnp.sqrt(vn * c2) + ADAM_EPS) + ADAM_WD * w_ref[...])

    blk = pl.BlockSpec((br, bc), at)
    outs, carried = _call(
        body, grid=(nb,), in_specs=[blk, blk, blk, pl.BlockSpec((n, br, bc), lambda i: (0,) + at(i))],
        out_specs=[blk] * 4, out_shape=[_sds((R, C), F32)] * 4, args=[w, m, v, gparts], name=name, sem=("parallel",), comm=comm)
    return outs if comm is None else (outs, carried)


_SMALL_ORDER = ("norm_mix", "b_gate", "sc_conv_w", "ssm_conv_w", "ssm_conv_b", "dt_bias", "A_log", "D_skip", "ssm_norm_w",
                "norm_mlp", "norm_final", "loss")
_REPLICATED = ("norm_mix", "b_gate", "ssm_conv_b", "dt_bias", "A_log", "D_skip", "ssm_norm_w", "norm_mlp", "norm_final")


def _cols_to_slots(g, n):
    R = g.shape[0]
    return jnp.transpose(g.reshape(R, n, g.shape[1] // n), (1, 0, 2))


def _slots_to_cols(g):
    n, R, C = g.shape
    return jnp.transpose(g, (1, 0, 2)).reshape(R, n * C)


def kernel(x, norm_mix, w_in, b_gate, sc_conv_w, ssm_conv_w, ssm_conv_b, dt_bias, A_log, D_skip, ssm_norm_w, w_branch_sc, w_branch_ssm, w_out, norm_mlp, w_mlp1, w_mlp2, norm_final, loss_target, m_norm_mix, m_w_in, m_b_gate, m_sc_conv_w, m_ssm_conv_w, m_ssm_conv_b, m_dt_bias, m_A_log, m_D_skip, m_ssm_norm_w, m_w_branch_sc, m_w_branch_ssm, m_w_out, m_norm_mlp, m_w_mlp1, m_w_mlp2, m_norm_final, v_norm_mix, v_w_in, v_b_gate, v_sc_conv_w, v_ssm_conv_w, v_ssm_conv_b, v_dt_bias, v_A_log, v_D_skip, v_ssm_norm_w, v_w_branch_sc, v_w_branch_ssm, v_w_out, v_norm_mlp, v_w_mlp1, v_w_mlp2, v_norm_final):
    T, D = x.shape[1], x.shape[2]
    n_inner = 2 * D
    n_heads = n_inner // HEADDIM
    n_xbc = n_inner + 2 * NGROUPS * NSTATE
    me = 4 * lax.axis_index("x") + 2 * lax.axis_index("y") + lax.axis_index("c")

    in_cols = [("sc", 3 * D), ("z", n_inner), ("xbc", n_xbc), ("dt", n_heads), ("gate", 2 * D)]
    by_owner = lambda b: b.reshape((N_DEV,) + b.shape[2:])
    to_owner = lambda g: g.reshape((4, 2) + g.shape[1:])
    rows_of = lambda g: to_owner(g.reshape((N_DEV, g.shape[0] // N_DEV) + g.shape[1:]))
    cols_of = lambda g: to_owner(_cols_to_slots(g, N_DEV))

    class Schedule(_NoExchange):
        late = ("bssm", "bsc", "out", "w1", "w2")
        gather_sib = dict(gnorm_fwd=("bsc", "bssm", "out"), branch_ssm=("w1", "w2"))
        scatter_sib = dict(mlp_up_dx=("w2", "w1"), branch_ssm_dx=("out", "bssm", "bsc"))
        shards = dict(bsc=w_branch_sc, bssm=w_branch_ssm, out=w_out, w1=w_mlp1, w2=w_mlp2)

        def __init__(self):
            self.W, self.staged, self.grads, self.summed, self.scatters = {}, {}, {}, {}, []
            self.token = jnp.zeros((), F32)

        def first_weights(self, bufs):
            self.W.update(zip([k for k, _ in in_cols], _win_unpack(by_owner(bufs[0]), in_cols, "win_unpack")))
            self.W.update(sc_conv_w=_slots_to_cols(by_owner(bufs[1])), ssm_conv_w=_slots_to_cols(by_owner(bufs[2])))
            lands = [_own_shard(self.shards[k], bufs[1], "own_shard_" + k) for k in self.late]
            self.gather_flight = _ici_start([], lands, True, "gather_late_start")
            self.token = self.gather_flight[4][0, 0]
            self.W["dt"] = self.W["dt"] + self.token.astype(BF16)

        def tok(self):
            return self.token

        def point(self, name, values):
            if name == "mixers_done":
                lands = _ici_wait(self.gather_flight, values, True, "gather_late_wait")
                self.staged.update(zip(self.late, lands))

        def carry(self, name):
            if name == "rms_mix":
                return _GatherBoth([w_in.astype(BF16), sc_conv_w, ssm_conv_w])
            if name in self.gather_sib:
                return _gather_sibling([self.staged.pop(k) for k in self.gather_sib[name]])
            if name in self.scatter_sib:
                return _scatter_sibling([self.grads[k] for k in self.scatter_sib[name]])
            return None

        def start_scatter(self, keys, halves_and_lands):
            halves, lands = [h for h, _ in halves_and_lands], [l for _, l in halves_and_lands]
            flight = _ici_start(halves, lands, False, "scatter_%s_start" % keys[0])
            self.scatters.append((keys, flight))
            self.token = flight[4][0, 0]

        def carried(self, name, outs):
            if name == "rms_mix":
                self.first_weights(outs)
            elif name in self.gather_sib:
                for k, b in zip(self.gather_sib[name], outs):
                    full = by_owner(b)
                    self.W[k] = _slots_to_cols(full) if k == "w1" else full.reshape(-1, D)
            else:
                keys = self.scatter_sib[name]
                self.start_scatter(keys, [_add_halves(self.grads[k], b, "add_halves_" + k) for k, b in zip(keys, outs)])

        def grad(self, k, g):
            if k == "win":
                g = to_owner(_win_pack([g[k] for k, _ in in_cols], in_cols, N_DEV, "win_pack"))
                got = _run_comm(_scatter_sibling([g]), "scatter_sibling_win")[0]
                self.start_scatter(("win",), [_add_halves(g, got, "add_halves_win")])
            else:
                self.grads[k] = cols_of(g) if k == "w1" else rows_of(g)

        def finish_scatter(self, after):
            keys, flight = self.scatters.pop(0)
            return dict(zip(keys, _ici_wait(flight, after, False, "scatter_%s_wait" % keys[0])))

    S = Schedule()
    small = dict(norm_mix=norm_mix, b_gate=b_gate, ssm_conv_b=ssm_conv_b, dt_bias=dt_bias, A_log=A_log, D_skip=D_skip,
                 ssm_norm_w=ssm_norm_w, norm_mlp=norm_mlp, norm_final=norm_final)
    grad_x, g_small = _local_step(x.reshape(T, D), loss_target.reshape(T, D), S, small)

    small_flat = jnp.concatenate([g_small[k].reshape(-1) for k in _SMALL_ORDER])
    n_small = small_flat.shape[0]
    rows = -(-n_small // (8 * LANES)) * 8
    small_pack = jnp.pad(small_flat, (0, rows * LANES - n_small)).reshape(rows, LANES)

    res = {}
    big = [("w_in", "win", w_in, m_w_in, v_w_in), ("w_branch_sc", "bsc", w_branch_sc, m_w_branch_sc, v_w_branch_sc),
           ("w_branch_ssm", "bssm", w_branch_ssm, m_w_branch_ssm, v_w_branch_ssm), ("w_out", "out", w_out, m_w_out, v_w_out),
           ("w_mlp1", "w1", w_mlp1, m_w_mlp1, v_w_mlp1), ("w_mlp2", "w2", w_mlp2, m_w_mlp2, v_w_mlp2)]
    by_grad = {gk: (k, w, m, v) for k, gk, w, m, v in big}
    after = [grad_x]
    while S.scatters:
        for gk, parts in S.finish_scatter(after).items():
            k, w, m, v = by_grad[gk]
            if gk == "win":
                res_t, (small_parts,) = _adam(w.T, m.T, v.T, parts, "adam_" + k, comm=_gather_all([small_pack]))
                res[k] = [r.T for r in res_t]
            else:
                res[k] = _adam(w, m, v, parts, "adam_" + k)
            after = after + [res[k][1]]

    sizes = {k: g_small[k].size for k in _SMALL_ORDER}
    offs, o = {}, 0
    for k in _SMALL_ORDER:
        offs[k] = o
        o += sizes[k]
    rep_w = dict(norm_mix=norm_mix, b_gate=b_gate, ssm_conv_b=ssm_conv_b, dt_bias=dt_bias, A_log=A_log, D_skip=D_skip,
                 ssm_norm_w=ssm_norm_w, norm_mlp=norm_mlp, norm_final=norm_final)
    rep_m = dict(norm_mix=m_norm_mix, b_gate=m_b_gate, ssm_conv_b=m_ssm_conv_b, dt_bias=m_dt_bias, A_log=m_A_log, D_skip=m_D_skip,
                 ssm_norm_w=m_ssm_norm_w, norm_mlp=m_norm_mlp, norm_final=m_norm_final)
    rep_v = dict(norm_mix=v_norm_mix, b_gate=v_b_gate, ssm_conv_b=v_ssm_conv_b, dt_bias=v_dt_bias, A_log=v_A_log, D_skip=v_D_skip,
                 ssm_norm_w=v_ssm_norm_w, norm_mlp=v_norm_mlp, norm_final=v_norm_final)

    def pack(d):
        segs = [jnp.pad(d[k].astype(F32).reshape(-1), (0, sizes[k] - d[k].size)) if k in d else jnp.zeros((sizes[k],), F32)
                for k in _SMALL_ORDER]
        return jnp.pad(jnp.concatenate(segs), (0, rows * LANES - n_small)).reshape(rows, LANES)

    sm = _adam(pack(rep_w), pack(rep_m), pack(rep_v), small_parts, "adam_small")
    sm = [s.reshape(-1) for s in sm]
    for k in _REPLICATED:
        n_k = rep_w[k].shape[0]
        res[k] = tuple(s[offs[k]:offs[k] + n_k] for s in sm)
    loss = sm[0][offs["loss"]]
    for k, w, m, v, K, full in (("sc_conv_w", sc_conv_w, m_sc_conv_w, v_sc_conv_w, SC_K, D),
                                ("ssm_conv_w", ssm_conv_w, m_ssm_conv_w, v_ssm_conv_w, SSM_K, n_xbc)):
        g_full = sm[0][offs[k]:offs[k] + K * full].reshape(K, full)
        cw = full // N_DEV
        g_mine = lax.dynamic_slice_in_dim(g_full, me * cw, cw, axis=1)
        res[k] = _adam(w, m, v, g_mine[None], "adam_" + k)

    order = ("norm_mix", "w_in", "b_gate", "sc_conv_w", "ssm_conv_w", "ssm_conv_b", "dt_bias", "A_log", "D_skip", "ssm_norm_w",
             "w_branch_sc", "w_branch_ssm", "w_out", "norm_mlp", "w_mlp1", "w_mlp2", "norm_final")
    outs = [loss, grad_x.reshape(1, T, D)]
    for j in range(4):
        outs += [res[k][j] for k in order]
    return tuple(outs)
```

```python
import jax
import jax.numpy as jnp
from jax import lax
from jax.experimental import pallas as pl
from jax.experimental.pallas import tpu as pltpu

F32 = jnp.float32
BF16 = jnp.bfloat16

EPS = 1e-6
N_DEV = 8
HEADDIM = 64
NSTATE = 128
CHUNK = 128
NGROUPS = 8
GROUP_W = 256
SC_K = 3
SSM_K = 4
LANES = 128

ADAM_LR = 0.001
ADAM_B1 = 0.9
ADAM_B2 = 0.999
ADAM_EPS = 1e-08
ADAM_WD = 0.01
ADAM_STEP = 10

NN = (((1,), (0,)), ((), ()))
NT = (((1,), (1,)), ((), ()))
TN = (((0,), (0,)), ((), ()))
_DIMS = {"nn": NN, "nt": NT, "tn": TN}

ANY = pl.BlockSpec(memory_space=pl.ANY)
MESH = pl.DeviceIdType.MESH


def _sds(shape, dtype):
    return jax.ShapeDtypeStruct(tuple(shape), dtype)


def _dot(a, b, dims=NN):
    return lax.dot_general(a, b, dims, preferred_element_type=F32)


def _dot3(a, b, dims=NN):
    return lax.dot_general(a, b, dims, preferred_element_type=F32, precision=lax.Precision.HIGH)


def _params(*sem):
    return pltpu.CompilerParams(dimension_semantics=tuple(sem))


def _call(body, *, grid, in_specs, out_specs, out_shape, args, name, sem, scratch=(), comm=None):
    if comm is None:
        outs = pl.pallas_call(body, grid=grid, in_specs=list(in_specs), out_specs=list(out_specs), out_shape=list(out_shape),
                              scratch_shapes=list(scratch), name=name, compiler_params=_params(*sem))(*args)
        return list(outs), None
    n, n_in, n_out, n_scr = comm.n, len(in_specs), len(out_shape), len(scratch)

    def wrapped(*refs):
        ins, c_in = refs[:n_in], refs[n_in:n_in + n]
        outs, c_out = refs[n_in + n:n_in + n + n_out], refs[n_in + n + n_out:n_in + 2 * n + n_out]
        rest = refs[n_in + 2 * n + n_out:]
        scr, sems = rest[:n_scr], rest[n_scr:]
        first, last = None, None
        for d, g in enumerate(grid):
            f, l = pl.program_id(d) == 0, pl.program_id(d) == g - 1
            first, last = (f, l) if first is None else (first & f, last & l)

        @pl.when(first)
        def _():
            comm.start(c_in, c_out, sems)

        body(*ins, *outs, *scr)

        @pl.when(last)
        def _():
            comm.finish(c_in, c_out, sems)

    outs = pl.pallas_call(
        wrapped, grid=grid, in_specs=list(in_specs) + [ANY] * n, out_specs=list(out_specs) + [ANY] * n,
        out_shape=list(out_shape) + comm.out_shape, scratch_shapes=list(scratch) + comm.scratch,
        input_output_aliases={n_in + i: n_out + o for i, o in comm.aliases.items()},
        name=name, compiler_params=_params(*["arbitrary"] * len(grid)))(*args, *comm.arrs)
    return list(outs[:n_out]), list(outs[n_out:])


MM_VMEM_BUDGET = 44 * 2 ** 20


def _mm_tiles(M, N, k_bytes, mn_bytes):
    best = None
    for tm in (2048, 1024, 512, 256, 128):
        for tn in (1024, 512, 256, 128):
            if M % tm or N % tn:
                continue
            need = 2 * ((tm + tn) * k_bytes + tm * tn * mn_bytes) + 4 * tm * tn * 4
            if need <= MM_VMEM_BUDGET and (best is None or (tm * tn, tm) > (best[0] * best[1], best[0])):
                best = (tm, tn)
    assert best is not None, (M, N, k_bytes, mn_bytes)
    return best


def _mm(a, b, *, mode, name, extras=(), epi=None, out_dtypes=(F32,), comm=None):
    a_list = list(a) if isinstance(a, (list, tuple)) else [a]
    b_list = list(b) if isinstance(b, (list, tuple)) else [b]
    if mode == "nn":
        M, N = a_list[0].shape[0], b_list[0].shape[1]
    elif mode == "nt":
        M, N = a_list[0].shape[0], b_list[0].shape[0]
    else:
        M, N = a_list[0].shape[1], b_list[0].shape[1]
    k_bytes = sum((av.shape[0] if mode == "tn" else av.shape[1]) * av.dtype.itemsize for av in a_list)
    mn_bytes = sum(e.dtype.itemsize for e in extras) + sum(jnp.dtype(d).itemsize for d in out_dtypes)
    tm, tn = _mm_tiles(min(M, 2048), min(N, 1024), k_bytes, mn_bytes) if M % 128 == 0 and N % 128 == 0 else (M, N)
    assert M % tm == 0 and N % tn == 0
    a_specs, b_specs = [], []
    for av, bv in zip(a_list, b_list):
        K = av.shape[0] if mode == "tn" else av.shape[1]
        a_specs.append(pl.BlockSpec((K, tm), lambda i, j: (0, i)) if mode == "tn" else pl.BlockSpec((tm, K), lambda i, j: (i, 0)))
        b_specs.append(pl.BlockSpec((tn, K), lambda i, j: (j, 0)) if mode == "nt" else pl.BlockSpec((K, tn), lambda i, j: (0, j)))
    mn_spec = pl.BlockSpec((tm, tn), lambda i, j: (i, j))
    n_p, n_ex = len(a_list), len(extras)
    dims = _DIMS[mode]

    def body(*refs):
        acc = _dot(refs[0][...], refs[n_p][...], dims)
        for p in range(1, n_p):
            acc = acc + _dot(refs[p][...], refs[n_p + p][...], dims)
        rest = refs[2 * n_p:]
        res = (acc,) if epi is None else epi(acc, *[r[...] for r in rest[:n_ex]])
        for o_ref, r in zip(rest[n_ex:], res):
            o_ref[...] = r.astype(o_ref.dtype)

    outs, carried = _call(
        body, grid=(M // tm, N // tn), in_specs=a_specs + b_specs + [mn_spec] * n_ex,
        out_specs=[mn_spec] * len(out_dtypes), out_shape=[_sds((M, N), d) for d in out_dtypes],
        args=a_list + b_list + list(extras), name=name, sem=("parallel", "parallel"), comm=comm)
    res = outs[0] if len(outs) == 1 else outs
    return res if comm is None else (res, carried)


def _epi_add(acc, r):
    return (acc + r,)


def _epi_relu2(acc):
    p = jnp.maximum(acc, 0.0)
    return (p * p,)


def _epi_relu2_bwd(acc, r):
    return (acc * (2.0 * jnp.sqrt(r.astype(F32))),)


def _row(tr, n):
    return pl.BlockSpec((tr, n), lambda i: (i, 0))


def _vec(n):
    return pl.BlockSpec((1, n), lambda i: (0, 0))


def _rms_fwd(x, w, name, comm=None):
    T, D = x.shape
    tr = min(256, T)

    def body(x_ref, w_ref, o_ref):
        xv = x_ref[...]
        r = lax.rsqrt(jnp.mean(xv * xv, axis=-1, keepdims=True) + EPS)
        o_ref[...] = (xv * r * w_ref[...]).astype(BF16)

    outs, carried = _call(body, grid=(T // tr,), in_specs=[_row(tr, D), _vec(D)], out_specs=[_row(tr, D)],
                          out_shape=[_sds((T, D), BF16)], args=[x, w], name=name, sem=("parallel",), comm=comm)
    return outs[0] if comm is None else (outs[0], carried)


def _rms_bwd(x, w, dh, dres, name):
    T, D = x.shape
    tr = min(256, T)

    def body(x_ref, w_ref, dh_ref, dres_ref, dx_ref, dxb_ref, dw_ref):
        @pl.when(pl.program_id(0) == 0)
        def _():
            dw_ref[...] = jnp.zeros_like(dw_ref)

        xv = x_ref[...]
        r = lax.rsqrt(jnp.mean(xv * xv, axis=-1, keepdims=True) + EPS)
        xh = xv * r
        dh_v = dh_ref[...]
        dw_ref[...] += jnp.sum(dh_v * xh, axis=0, keepdims=True)
        dxh = dh_v * w_ref[...]
        dx = r * (dxh - xh * jnp.mean(dxh * xh, axis=-1, keepdims=True)) + dres_ref[...]
        dx_ref[...] = dx
        dxb_ref[...] = dx.astype(BF16)

    return pl.pallas_call(
        body, grid=(T // tr,), in_specs=[_row(tr, D), _vec(D), _row(tr, D), _row(tr, D)],
        out_specs=[_row(tr, D), _row(tr, D), _vec(D)],
        out_shape=[_sds((T, D), F32), _sds((T, D), BF16), _sds((1, D), F32)],
        name=name, compiler_params=_params("arbitrary"))(x, w, dh, dres)


def _final(x2, w, tgt, name):
    T, D = x2.shape
    tr = min(256, T)

    def body(x_ref, w_ref, t_ref, dx_ref, dxb_ref, dw_ref, loss_ref):
        @pl.when(pl.program_id(0) == 0)
        def _():
            dw_ref[...] = jnp.zeros_like(dw_ref)
            loss_ref[...] = jnp.zeros_like(loss_ref)

        xv = x_ref[...]
        wv = w_ref[...]
        r = lax.rsqrt(jnp.mean(xv * xv, axis=-1, keepdims=True) + EPS)
        xh = xv * r
        err = xh * wv - t_ref[...]
        part = jnp.sum(jnp.sum(err * err, axis=1, keepdims=True), axis=0, keepdims=True) * (0.5 / D)
        loss_ref[...] += jnp.broadcast_to(part, loss_ref.shape)
        dy = err * (1.0 / D)
        dw_ref[...] += jnp.sum(dy * xh, axis=0, keepdims=True)
        dxh = dy * wv
        dx = r * (dxh - xh * jnp.mean(dxh * xh, axis=-1, keepdims=True))
        dx_ref[...] = dx
        dxb_ref[...] = dx.astype(BF16)

    return pl.pallas_call(
        body, grid=(T // tr,), in_specs=[_row(tr, D), _vec(D), _row(tr, D)],
        out_specs=[_row(tr, D), _row(tr, D), _vec(D), _vec(LANES)],
        out_shape=[_sds((T, D), F32), _sds((T, D), BF16), _sds((1, D), F32), _sds((1, LANES), F32)],
        name=name, compiler_params=_params("arbitrary"))(x2, w, tgt)


def _silu_parts(z):
    s = jax.nn.sigmoid(z)
    return z * s, s * (1.0 + z * (1.0 - s))


def _gnorm_fwd(y, z, w, name, comm=None):
    T, N = y.shape
    tr = min(256, T)

    def body(y_ref, z_ref, w_ref, o_ref):
        for g in range(N // GROUP_W):
            sl = slice(g * GROUP_W, (g + 1) * GROUP_W)
            silu, _ = _silu_parts(z_ref[:, sl])
            yz = y_ref[:, sl] * silu
            r = lax.rsqrt(jnp.mean(yz * yz, axis=-1, keepdims=True) + EPS)
            o_ref[:, sl] = (yz * r * w_ref[:, sl]).astype(BF16)

    outs, carried = _call(body, grid=(T // tr,), in_specs=[_row(tr, N), _row(tr, N), _vec(N)], out_specs=[_row(tr, N)],
                          out_shape=[_sds((T, N), BF16)], args=[y, z, w], name=name, sem=("parallel",), comm=comm)
    return outs[0] if comm is None else (outs[0], carried)


def _gnorm_bwd(y, z, w, dyb, name):
    T, N = y.shape
    tr = min(256, T)

    def body(y_ref, z_ref, w_ref, d_ref, dy_ref, dz_ref, dw_ref):
        @pl.when(pl.program_id(0) == 0)
        def _():
            dw_ref[...] = jnp.zeros_like(dw_ref)

        for g in range(N // GROUP_W):
            sl = slice(g * GROUP_W, (g + 1) * GROUP_W)
            yv = y_ref[:, sl]
            silu, dsilu = _silu_parts(z_ref[:, sl])
            yz = yv * silu
            r = lax.rsqrt(jnp.mean(yz * yz, axis=-1, keepdims=True) + EPS)
            yzh = yz * r
            d = d_ref[:, sl]
            dw_ref[:, sl] += jnp.sum(d * yzh, axis=0, keepdims=True)
            dyzh = d * w_ref[:, sl]
            dyz = r * (dyzh - yzh * jnp.mean(dyzh * yzh, axis=-1, keepdims=True))
            dy_ref[:, sl] = dyz * silu
            dz_ref[:, sl] = (dyz * yv * dsilu).astype(BF16)

    return pl.pallas_call(
        body, grid=(T // tr,), in_specs=[_row(tr, N), _row(tr, N), _vec(N), _row(tr, N)],
        out_specs=[_row(tr, N), _row(tr, N), _vec(N)],
        out_shape=[_sds((T, N), F32), _sds((T, N), BF16), _sds((1, N), F32)],
        name=name, compiler_params=_params("arbitrary"))(y, z, w, dyb)


def _merge_fwd(gate_raw, b_gate, br_a, br_b, name):
    T, D = br_a.shape
    tr = min(256, T)

    def body(g_ref, bg_ref, a_ref, b_ref, o_ref):
        g = jax.nn.sigmoid(g_ref[...] + bg_ref[...])
        o_ref[...] = (g[:, :D] * a_ref[...] + g[:, D:] * b_ref[...]).astype(BF16)

    return pl.pallas_call(body, grid=(T // tr,), in_specs=[_row(tr, 2 * D), _vec(2 * D), _row(tr, D), _row(tr, D)],
                          out_specs=_row(tr, D), out_shape=_sds((T, D), BF16), name=name,
                          compiler_params=_params("parallel"))(gate_raw, b_gate, br_a, br_b)


def _merge_bwd(dmerged, gate_raw, b_gate, br_a, br_b, name):
    T, D = br_a.shape
    tr = min(256, T)

    def body(d_ref, g_ref, bg_ref, a_ref, b_ref, da_ref, db_ref, dg_ref, dbg_ref):
        @pl.when(pl.program_id(0) == 0)
        def _():
            dbg_ref[...] = jnp.zeros_like(dbg_ref)

        g = jax.nn.sigmoid(g_ref[...] + bg_ref[...])
        d = d_ref[...]
        da_ref[...] = (d * g[:, :D]).astype(BF16)
        db_ref[...] = (d * g[:, D:]).astype(BF16)
        dg = jnp.concatenate([d * a_ref[...], d * b_ref[...]], axis=1) * g * (1.0 - g)
        dg_ref[...] = dg.astype(BF16)
        dbg_ref[...] += jnp.sum(dg, axis=0, keepdims=True)

    return pl.pallas_call(
        body, grid=(T // tr,), in_specs=[_row(tr, D), _row(tr, 2 * D), _vec(2 * D), _row(tr, D), _row(tr, D)],
        out_specs=[_row(tr, D), _row(tr, D), _row(tr, 2 * D), _vec(2 * D)],
        out_shape=[_sds((T, D), BF16), _sds((T, D), BF16), _sds((T, 2 * D), BF16), _sds((1, 2 * D), F32)],
        name=name, compiler_params=_params("arbitrary"))(dmerged, gate_raw, b_gate, br_a, br_b)


CB_W = 256
CONV_ROWS = 32
CONV_PAD = 8


def _rows_down(load, r0, s):
    if s == 0:
        return load(r0, r0 + CONV_ROWS)
    if r0 == 0:
        row = lax.broadcasted_iota(jnp.int32, (CONV_ROWS, CB_W), 0)
        return jnp.where(row >= s, pltpu.roll(load(0, CONV_ROWS), s, 0), 0.0)
    return load(r0 - s, r0 - s + CONV_ROWS)


def _conv_tile(load, taps, r0):
    K = len(taps)
    us = [_rows_down(load, r0, K - 1 - k) for k in range(K)]
    acc = us[K - 1] * taps[K - 1]
    for k in range(K - 1):
        acc = acc + us[k] * taps[k]
    return acc, us


def _conv_back_tile(scr, taps, r0):
    K = len(taps)
    du = scr[r0:r0 + CONV_ROWS, :] * taps[K - 1]
    for k in range(K - 1):
        s = K - 1 - k
        du = du + scr[r0 + s:r0 + s + CONV_ROWS, :] * taps[k]
    return du


def _fold8(v):
    return jnp.sum(v.reshape(CONV_ROWS // 8, 8, v.shape[1]), axis=0)


def _col(T, j0=0):
    return pl.BlockSpec((T, CB_W), lambda j: (0, j + j0))


def _sc_fwd(psc, w, name):
    T, D = psc.shape[0], psc.shape[1] // 3
    nb = D // CB_W

    def body(b_ref, c_ref, x_ref, w_ref, o_ref):
        taps = [w_ref[k:k + 1, :] for k in range(SC_K)]
        load = lambda a, b: c_ref[a:b, :] * x_ref[a:b, :]
        for r0 in range(0, T, CONV_ROWS):
            cu, _ = _conv_tile(load, taps, r0)
            o_ref[r0:r0 + CONV_ROWS, :] = (b_ref[r0:r0 + CONV_ROWS, :] * cu).astype(BF16)

    return pl.pallas_call(
        body, grid=(nb,), in_specs=[_col(T), _col(T, nb), _col(T, 2 * nb), pl.BlockSpec((SC_K, CB_W), lambda j: (0, j))],
        out_specs=_col(T), out_shape=_sds((T, D), BF16), name=name, compiler_params=_params("parallel"))(psc, psc, psc, w)


def _sc_bwd(psc, w, dya, name):
    T, D = psc.shape[0], psc.shape[1] // 3
    nb = D // CB_W

    def body(b_ref, c_ref, x_ref, w_ref, d_ref, db_ref, dc_ref, dx_ref, dw_ref, scr):
        taps = [w_ref[k:k + 1, :] for k in range(SC_K)]
        load = lambda a, b: c_ref[a:b, :] * x_ref[a:b, :]
        scr[T:T + CONV_PAD, :] = jnp.zeros((CONV_PAD, CB_W), F32)
        dw8 = [jnp.zeros((8, CB_W), F32)] * SC_K
        for r0 in range(0, T, CONV_ROWS):
            rows = slice(r0, r0 + CONV_ROWS)
            cu, us = _conv_tile(load, taps, r0)
            d = d_ref[rows, :]
            db_ref[rows, :] = (d * cu).astype(BF16)
            dcu = d * b_ref[rows, :]
            scr[rows, :] = dcu
            dw8 = [acc + _fold8(dcu * u) for acc, u in zip(dw8, us)]
        for k in range(SC_K):
            dw_ref[k:k + 1, :] = jnp.sum(dw8[k], axis=0, keepdims=True)
        for r0 in range(0, T, CONV_ROWS):
            rows = slice(r0, r0 + CONV_ROWS)
            du = _conv_back_tile(scr, taps, r0)
            dc_ref[rows, :] = (du * x_ref[rows, :]).astype(BF16)
            dx_ref[rows, :] = (du * c_ref[rows, :]).astype(BF16)

    wspec = pl.BlockSpec((SC_K, CB_W), lambda j: (0, j))
    return pl.pallas_call(
        body, grid=(nb,), in_specs=[_col(T), _col(T, nb), _col(T, 2 * nb), wspec, _col(T)],
        out_specs=[_col(T), _col(T), _col(T), wspec],
        out_shape=[_sds((T, D), BF16)] * 3 + [_sds((SC_K, D), F32)],
        scratch_shapes=[pltpu.VMEM((T + CONV_PAD, CB_W), F32)],
        name=name, compiler_params=_params("parallel"))(psc, psc, psc, w, dya)


def _ssm_conv_fwd(u, w, b, name, comm=None):
    T, N = u.shape

    def body(u_ref, w_ref, b_ref, o_ref):
        taps = [w_ref[k:k + 1, :] for k in range(SSM_K)]
        bias = b_ref[...]
        for r0 in range(0, T, CONV_ROWS):
            c, _ = _conv_tile(lambda a, b: u_ref[a:b, :], taps, r0)
            c = c + bias
            o_ref[r0:r0 + CONV_ROWS, :] = c * jax.nn.sigmoid(c)

    outs, carried = _call(
        body, grid=(N // CB_W,), in_specs=[_col(T), pl.BlockSpec((SSM_K, CB_W), lambda j: (0, j)), pl.BlockSpec((1, CB_W), lambda j: (0, j))],
        out_specs=[_col(T)], out_shape=[_sds((T, N), F32)], args=[u, w, b], name=name, sem=("parallel",), comm=comm)
    return outs[0] if comm is None else (outs[0], carried)


def _ssm_conv_bwd(u, w, b, dxs, dB, dC, name, comm=None):
    T, N = u.shape
    n_x, n_b = dxs.shape[1] // CB_W, dB.shape[1] // CB_W

    def body(u_ref, w_ref, b_ref, dx_ref, db_ref, dc_ref, du_ref, dw_ref, dbias_ref, scr):
        j = pl.program_id(0)
        taps = [w_ref[k:k + 1, :] for k in range(SSM_K)]
        bias = b_ref[...]
        scr[T:T + CONV_PAD, :] = jnp.zeros((CONV_PAD, CB_W), F32)
        dw8 = [jnp.zeros((8, CB_W), F32)] * SSM_K
        db8 = jnp.zeros((8, CB_W), F32)
        for r0 in range(0, T, CONV_ROWS):
            rows = slice(r0, r0 + CONV_ROWS)
            c, us = _conv_tile(lambda a, b: u_ref[a:b, :], taps, r0)
            _, dsilu = _silu_parts(c + bias)
            d = jnp.where(j < n_x, dx_ref[rows, :], jnp.where(j < n_x + n_b, db_ref[rows, :], dc_ref[rows, :])) * dsilu
            scr[rows, :] = d
            db8 = db8 + _fold8(d)
            dw8 = [acc + _fold8(d * u) for acc, u in zip(dw8, us)]
        dbias_ref[...] = jnp.sum(db8, axis=0, keepdims=True)
        for k in range(SSM_K):
            dw_ref[k:k + 1, :] = jnp.sum(dw8[k], axis=0, keepdims=True)
        for r0 in range(0, T, CONV_ROWS):
            du_ref[r0:r0 + CONV_ROWS, :] = _conv_back_tile(scr, taps, r0).astype(BF16)

    wspec = pl.BlockSpec((SSM_K, CB_W), lambda j: (0, j))
    bspec = pl.BlockSpec((1, CB_W), lambda j: (0, j))
    outs, carried = _call(
        body, grid=(N // CB_W,),
        in_specs=[_col(T), wspec, bspec,
                  pl.BlockSpec((T, CB_W), lambda j: (0, jnp.minimum(j, n_x - 1))),
                  pl.BlockSpec((T, CB_W), lambda j: (0, jnp.clip(j - n_x, 0, n_b - 1))),
                  pl.BlockSpec((T, CB_W), lambda j: (0, jnp.clip(j - n_x - n_b, 0, n_b - 1)))],
        out_specs=[_col(T), wspec, bspec],
        out_shape=[_sds((T, N), BF16), _sds((SSM_K, N), F32), _sds((1, N), F32)],
        scratch=[pltpu.VMEM((T + CONV_PAD, CB_W), F32)],
        args=[u, w, b, dxs, dB, dC], name=name, sem=("parallel",), comm=comm)
    return outs if comm is None else (outs, carried)


def _split3(v):
    hi = v.astype(BF16)
    r = v - hi.astype(F32)
    mid = r.astype(BF16)
    lo = (r - mid.astype(F32)).astype(BF16)
    return hi, mid, lo


def _head_expand(n_lanes):
    h = lax.broadcasted_iota(jnp.int32, (LANES, n_lanes), 0)
    l = lax.broadcasted_iota(jnp.int32, (LANES, n_lanes), 1)
    return (jnp.right_shift(l, HEADDIM.bit_length() - 1) == h).astype(BF16)


def _softplus(v):
    return jnp.maximum(v, 0.0) + jnp.log1p(jnp.exp(-jnp.abs(v)))


def _ssd_prep(dt_raw, dt_bias, a_log, n_inner, name):
    T = dt_raw.shape[0]

    def body(r_ref, b_ref, al_ref, ex_ref, dt_ref, cs_ref):
        dt = _softplus(r_ref[...] + b_ref[...])
        a = dt * (-jnp.exp(al_ref[...]))
        i = lax.broadcasted_iota(jnp.int32, (CHUNK, CHUNK), 0)
        j = lax.broadcasted_iota(jnp.int32, (CHUNK, CHUNK), 1)
        tri = (j <= i).astype(BF16)
        cs = sum(_dot(tri, p) for p in _split3(a))
        ex = ex_ref[...]
        dt_ref[...] = sum(_dot(p, ex) for p in _split3(dt))
        cs_ref[...] = sum(_dot(p, ex) for p in _split3(cs))

    blk = pl.BlockSpec((CHUNK, LANES), lambda c: (c, 0))
    out = pl.BlockSpec((CHUNK, n_inner), lambda c: (c, 0))
    ex_spec = pl.BlockSpec((LANES, n_inner), lambda c: (0, 0))
    return pl.pallas_call(body, grid=(T // CHUNK,), in_specs=[blk, _vec(LANES), _vec(LANES), ex_spec], out_specs=[out, out],
                          out_shape=[_sds((T, n_inner), F32)] * 2, name=name,
                          compiler_params=_params("parallel"))(dt_raw, dt_bias, a_log, _head_expand(n_inner))


def _pair_terms(cs_p):
    lane = lax.broadcasted_iota(jnp.int32, (CHUNK, CHUNK), 1)
    sub = lax.broadcasted_iota(jnp.int32, (CHUNK, CHUNK), 0)
    csT = cs_p.T
    Ls = []
    for k in range(2):
        col = jnp.sum(jnp.where(lane == k * HEADDIM, cs_p, 0.0), axis=1, keepdims=True)
        rowv = csT[k * HEADDIM:k * HEADDIM + 1, :]
        Ls.append(jnp.exp(jnp.where(sub >= lane, col - rowv, -jnp.inf)))
    return Ls, jnp.exp(csT[:, CHUNK - 1:CHUNK])


def _block_diag(xp):
    lane = lax.broadcasted_iota(jnp.int32, xp.shape, 1)
    return jnp.concatenate([jnp.where(lane < HEADDIM, xp, 0.0), jnp.where(lane >= HEADDIM, xp, 0.0)], axis=0)


SSD_GROUPS_PER_STEP = 8


def _ssd_specs(T, n_inner):
    nc, gs = T // CHUNK, SSD_GROUPS_PER_STEP
    bo, co = n_inner // (gs * NSTATE), (n_inner + NGROUPS * NSTATE) // (gs * NSTATE)
    assert NGROUPS % gs == 0 and n_inner % (gs * NSTATE) == 0 and (NGROUPS * NSTATE) % (gs * NSTATE) == 0
    g_blk = lambda f: pl.BlockSpec((CHUNK, gs * GROUP_W), lambda c, s: (f(c), s))
    b_blk = lambda f: pl.BlockSpec((CHUNK, gs * NSTATE), lambda c, s: (f(c), bo + s))
    c_blk = lambda f: pl.BlockSpec((CHUNK, gs * NSTATE), lambda c, s: (f(c), co + s))
    return nc, g_blk, b_blk, c_blk


def _ssd_fwd(xbc, dt_e, cs_e, d_e, name, comm=None):
    T = xbc.shape[0]
    n_inner = dt_e.shape[1]
    nc, g_blk, b_blk, c_blk = _ssd_specs(T, n_inner)
    ident = lambda c: c

    gs = SSD_GROUPS_PER_STEP

    def body(xs_ref, b_ref, c_ref, dt_ref, cs_ref, d_ref, y_ref, p_ref, st):
        c, s = pl.program_id(0), pl.program_id(1)

        @pl.when(c == 0)
        def _():
            for gi in range(gs):
                st[s * gs + gi] = jnp.zeros((GROUP_W, NSTATE), F32)

        for gi in range(gs):
            g = s * gs + gi
            gw, gn = slice(gi * GROUP_W, (gi + 1) * GROUP_W), slice(gi * NSTATE, (gi + 1) * NSTATE)
            P = st[g]
            p_ref[0, gi] = P
            xs, dt, cs = xs_ref[:, gw], dt_ref[:, gw], cs_ref[:, gw]
            Bf, Cf = b_ref[:, gn], c_ref[:, gn]
            Cb = Cf.astype(BF16)
            CBm = _dot(Cb, Bf.astype(BF16), NT)
            X = xs * dt
            decay = jnp.exp(cs[CHUNK - 1:CHUNK, :] - cs)
            y_off = _dot(Cb, P.astype(BF16), NT) * jnp.exp(cs)
            ys, ecl = [], []
            for pr in range(2):
                sl = slice(pr * LANES, (pr + 1) * LANES)
                Ls, e_last = _pair_terms(cs[:, sl])
                ecl.append(e_last)
                Mcat = jnp.concatenate([(CBm * L).astype(BF16) for L in Ls], axis=1)
                ys.append(_dot(Mcat, _block_diag(X[:, sl]).astype(BF16)))
            y_ref[:, gw] = jnp.concatenate(ys, axis=1) + y_off + xs * d_ref[:, gw]
            S = _dot3(X * decay, Bf, TN)
            st[g] = P * jnp.concatenate(ecl, axis=0) + S

    p_blk = pl.BlockSpec((1, gs, GROUP_W, NSTATE), lambda c, s: (c, s, 0, 0))
    outs, carried = _call(
        body, grid=(nc, NGROUPS // gs),
        in_specs=[g_blk(ident), b_blk(ident), c_blk(ident), g_blk(ident), g_blk(ident), pl.BlockSpec((1, gs * GROUP_W), lambda c, s: (0, s))],
        out_specs=[g_blk(ident), p_blk],
        out_shape=[_sds((T, n_inner), F32), _sds((nc, NGROUPS, GROUP_W, NSTATE), F32)],
        scratch=[pltpu.VMEM((NGROUPS, GROUP_W, NSTATE), F32)],
        args=[xbc, xbc, xbc, dt_e, cs_e, d_e], name=name, sem=("arbitrary", "arbitrary"), comm=comm)
    return outs if comm is None else (outs, carried)


def _ssd_bwd(xbc, dt_e, cs_e, d_e, states, dy, name, comm=None):
    T = xbc.shape[0]
    n_inner = dt_e.shape[1]
    nc, g_blk, b_blk, c_blk = _ssd_specs(T, n_inner)
    rev = lambda c: nc - 1 - c

    gs = SSD_GROUPS_PER_STEP

    def body(xs_ref, b_ref, c_ref, dt_ref, cs_ref, d_ref, p_ref, pn_ref, dy_ref,
             dxs_ref, db_ref, dc_ref, ddt_ref, dcs_ref, dd_ref, dst):
        cc, s = pl.program_id(0), pl.program_id(1)

        @pl.when(cc == 0)
        def _():
            for gi in range(gs):
                dst[s * gs + gi] = jnp.zeros((GROUP_W, NSTATE), F32)

        for gi in range(gs):
            one_group(s * gs + gi, gi, xs_ref, b_ref, c_ref, dt_ref, cs_ref, d_ref, p_ref, pn_ref, dy_ref,
                      dxs_ref, db_ref, dc_ref, ddt_ref, dcs_ref, dd_ref, dst)

    def one_group(g, gi, xs_ref, b_ref, c_ref, dt_ref, cs_ref, d_ref, p_ref, pn_ref, dy_ref,
                  dxs_ref, db_ref, dc_ref, ddt_ref, dcs_ref, dd_ref, dst):
        gw, gn = slice(gi * GROUP_W, (gi + 1) * GROUP_W), slice(gi * NSTATE, (gi + 1) * NSTATE)
        dS = dst[g]
        P, Pn = p_ref[0, gi], pn_ref[0, gi]
        xs, dt, cs, dY = xs_ref[:, gw], dt_ref[:, gw], cs_ref[:, gw], dy_ref[:, gw]
        Bf, Cf = b_ref[:, gn], c_ref[:, gn]
        Bb, Cb = Bf.astype(BF16), Cf.astype(BF16)
        X = xs * dt
        ecs = jnp.exp(cs)
        decay = jnp.exp(cs[CHUNK - 1:CHUNK, :] - cs)
        CBm = _dot3(Cf, Bf, NT)
        dYe = dY * ecs
        dP_off = _dot3(dYe, Cf, TN)
        dC = _dot(dYe.astype(BF16), P.astype(BF16))
        dcs = dYe * _dot3(Cf, P, NT)
        Xd = X * decay
        dB = _dot(Xd.astype(BF16), dS.astype(BF16))
        E = _dot3(Bf, dS, NT)
        dX = E * decay
        dcs = dcs - E * Xd
        R = _dot3(jnp.ones((8, NSTATE), F32), dS * Pn, NT)
        sub_g = lax.broadcasted_iota(jnp.int32, (CHUNK, GROUP_W), 0)
        dcs = dcs + jnp.where(sub_g == CHUNK - 1, R[0:1, :], 0.0)
        lane = lax.broadcasted_iota(jnp.int32, (CHUNK, CHUNK), 1)
        sub = lax.broadcasted_iota(jnp.int32, (CHUNK, CHUNK), 0)
        dCB = jnp.zeros((CHUNK, CHUNK), F32)
        dXs, dcss, ecl = [], [], []
        for pr in range(2):
            sl = slice(pr * LANES, (pr + 1) * LANES)
            Ls, e_last = _pair_terms(cs[:, sl])
            ecl.append(e_last)
            dYp = dY[:, sl]
            dMcat = _dot3(dYp, _block_diag(X[:, sl]), NT)
            Mcat = jnp.concatenate([CBm * L for L in Ls], axis=1)
            dXt = _dot3(Mcat, dYp, TN)
            dXs.append(jnp.where(lane < HEADDIM, dXt[:CHUNK], dXt[CHUNK:]))
            colacc = jnp.zeros((CHUNK, CHUNK), F32)
            rowacc = jnp.zeros((CHUNK, CHUNK), F32)
            for k in range(2):
                dG = dMcat[:, k * CHUNK:(k + 1) * CHUNK] * Ls[k]
                dCB = dCB + dG
                Q = dG * CBm
                colacc = colacc + jnp.where(lane == k * HEADDIM, jnp.sum(Q, axis=1, keepdims=True), 0.0)
                rowacc = rowacc + jnp.where(sub == k * HEADDIM, jnp.sum(Q, axis=0, keepdims=True), 0.0)
            dcss.append(colacc - rowacc.T)
        dX = dX + jnp.concatenate(dXs, axis=1)
        dcs = dcs + jnp.concatenate(dcss, axis=1)
        dCBb = dCB.astype(BF16)
        dc_ref[:, gn] = dC + _dot(dCBb, Bb)
        db_ref[:, gn] = dB + _dot(dCBb, Cb, TN)
        dxs_ref[:, gw] = dX * dt + dY * d_ref[:, gw]
        ddt_ref[:, gw] = dX * xs
        dcs_ref[:, gw] = dcs
        dd_ref[0, :, gw] = jnp.sum(dY * xs, axis=0, keepdims=True)
        dst[g] = dS * jnp.concatenate(ecl, axis=0) + dP_off

    p_blk = pl.BlockSpec((1, gs, GROUP_W, NSTATE), lambda c, s: (nc - 1 - c, s, 0, 0))
    pn_blk = pl.BlockSpec((1, gs, GROUP_W, NSTATE), lambda c, s: (jnp.minimum(nc - c, nc - 1), s, 0, 0))
    st_blk = pl.BlockSpec((CHUNK, gs * NSTATE), lambda c, s: (nc - 1 - c, s))
    outs, carried = _call(
        body, grid=(nc, NGROUPS // gs),
        in_specs=[g_blk(rev), b_blk(rev), c_blk(rev), g_blk(rev), g_blk(rev), pl.BlockSpec((1, gs * GROUP_W), lambda c, s: (0, s)),
                  p_blk, pn_blk, g_blk(rev)],
        out_specs=[g_blk(rev), st_blk, st_blk, g_blk(rev), g_blk(rev), pl.BlockSpec((1, 1, gs * GROUP_W), lambda c, s: (nc - 1 - c, 0, s))],
        out_shape=[_sds((T, n_inner), F32), _sds((T, NGROUPS * NSTATE), F32), _sds((T, NGROUPS * NSTATE), F32),
                   _sds((T, n_inner), F32), _sds((T, n_inner), F32), _sds((nc, 1, n_inner), F32)],
        scratch=[pltpu.VMEM((NGROUPS, GROUP_W, NSTATE), F32)],
        args=[xbc, xbc, xbc, dt_e, cs_e, d_e, states, states, dy], name=name, sem=("arbitrary", "arbitrary"), comm=comm)
    return outs if comm is None else (outs, carried)


def _ssd_post(ddt_e, dcs_e, dd_p, dt_raw, dt_bias, a_log, n_heads, name):
    T, n_inner = ddt_e.shape

    def body(ddt_ref, dcs_ref, dd_ref, r_ref, b_ref, al_ref, ex_ref, draw_ref, dbias_ref, dal_ref, ddsk_ref):
        @pl.when(pl.program_id(0) == 0)
        def _():
            dbias_ref[...] = jnp.zeros_like(dbias_ref)
            dal_ref[...] = jnp.zeros_like(dal_ref)
            ddsk_ref[...] = jnp.zeros_like(ddsk_ref)

        spread = [ddt_ref[...], dcs_ref[...], jnp.broadcast_to(dd_ref[0], (8, n_inner))]
        stacked = _dot(jnp.concatenate([p for v in spread for p in _split3(v)], axis=0), ex_ref[...], NT)
        sums, r0 = [], 0
        for v in spread:
            n = v.shape[0]
            sums.append(stacked[r0:r0 + n] + stacked[r0 + n:r0 + 2 * n] + stacked[r0 + 2 * n:r0 + 3 * n])
            r0 += 3 * n
        ddt_h, dcs_h, dd_h = sums
        raw = r_ref[...] + b_ref[...]
        dt = _softplus(raw)
        A = -jnp.exp(al_ref[...])
        i = lax.broadcasted_iota(jnp.int32, (CHUNK, CHUNK), 0)
        j = lax.broadcasted_iota(jnp.int32, (CHUNK, CHUNK), 1)
        upper = (j >= i).astype(BF16)
        da = sum(_dot(upper, p) for p in _split3(dcs_h))
        ddt = ddt_h + da * A
        lane = lax.broadcasted_iota(jnp.int32, (CHUNK, LANES), 1)
        draw = jnp.where(lane < n_heads, ddt * jax.nn.sigmoid(raw), 0.0)
        draw_ref[...] = draw.astype(BF16)
        dbias_ref[...] += jnp.sum(draw, axis=0, keepdims=True)
        dal_ref[...] += jnp.sum(da * dt, axis=0, keepdims=True) * A
        ddsk_ref[...] += dd_h[0:1, :]

    wide = pl.BlockSpec((CHUNK, n_inner), lambda c: (c, 0))
    blk = pl.BlockSpec((CHUNK, LANES), lambda c: (c, 0))
    return pl.pallas_call(
        body, grid=(T // CHUNK,),
        in_specs=[wide, wide, pl.BlockSpec((1, 1, n_inner), lambda c: (c, 0, 0)), blk, _vec(LANES), _vec(LANES),
                  pl.BlockSpec((LANES, n_inner), lambda c: (0, 0))],
        out_specs=[blk, _vec(LANES), _vec(LANES), _vec(LANES)],
        out_shape=[_sds((T, LANES), BF16)] + [_sds((1, LANES), F32)] * 3,
        name=name, compiler_params=_params("arbitrary"))(ddt_e, dcs_e, dd_p, dt_raw, dt_bias, a_log, _head_expand(n_inner))


def _row2(v):
    return v.reshape(1, -1).astype(F32)


def _pad_lanes(v):
    return jnp.pad(_row2(v), ((0, 0), (0, LANES - v.shape[-1])))


class _NoExchange:
    def __init__(self, W):
        self.W, self.grads = W, {}

    def weight(self, k):
        return self.W[k]

    def carry(self, name):
        return None

    def carried(self, name, outs):
        pass

    def grad(self, k, g):
        self.grads[k] = g

    def tok(self):
        return jnp.zeros((), F32)

    def point(self, name, value):
        pass


def _local_step(x, tgt, S, small):
    T, D = x.shape

    def mm(a, b, *, name, **kw):
        comm = S.carry(name)
        if comm is None:
            return _mm(a, b, name=name, **kw)
        res, outs = _mm(a, b, name=name, comm=comm, **kw)
        S.carried(name, outs)
        return res

    def carrying(fn, *args, name):
        comm = S.carry(name)
        if comm is None:
            return fn(*args, name)
        res, outs = fn(*args, name, comm=comm)
        S.carried(name, outs)
        return res

    n_inner = 2 * D
    n_heads = n_inner // HEADDIM
    norm_mix, norm_mlp, norm_final = _row2(small["norm_mix"]), _row2(small["norm_mlp"]), _row2(small["norm_final"])
    b_gate, ssm_b, ssm_norm_w = _row2(small["b_gate"]), _row2(small["ssm_conv_b"]), _row2(small["ssm_norm_w"])
    dt_bias, a_log = _pad_lanes(small["dt_bias"]), _pad_lanes(small["A_log"])
    d_e = jnp.repeat(small["D_skip"].astype(F32), HEADDIM).reshape(1, n_inner)

    hb = carrying(_rms_fwd, x, norm_mix, name="rms_mix")
    sc_w, ssm_w = S.weight("sc_conv_w"), S.weight("ssm_conv_w")
    p_xbc = mm(hb, S.weight("xbc"), mode="nn", name="proj_xbc")
    p_dt = mm(hb, S.weight("dt"), mode="nn", name="proj_dt")
    p_z = mm(hb, S.weight("z"), mode="nn", name="proj_z")
    p_sc = mm(hb, S.weight("sc"), mode="nn", name="proj_sc")
    p_gate = mm(hb, S.weight("gate"), mode="nn", name="proj_gate")
    xbc = carrying(_ssm_conv_fwd, p_xbc, ssm_w, ssm_b, name="ssm_conv_fwd")
    dt_e, cs_e = _ssd_prep(p_dt, dt_bias, a_log, n_inner, "ssd_prep")
    ya = _sc_fwd(p_sc, sc_w, "sc_fwd")
    y, states = carrying(_ssd_fwd, xbc, dt_e, cs_e, d_e, name="ssd_fwd")
    S.point("mixers_done", [y, ya, p_gate])
    yb = carrying(_gnorm_fwd, y, p_z, ssm_norm_w, name="gnorm_fwd")
    br_a = mm(ya, S.weight("bsc"), mode="nn", name="branch_sc")
    br_b = mm(yb, S.weight("bssm"), mode="nn", name="branch_ssm")
    merged = _merge_fwd(p_gate, b_gate, br_a, br_b, "merge_fwd")
    x1 = mm(merged, S.weight("out"), mode="nn", name="out_proj", extras=(x,), epi=_epi_add)
    h2 = _rms_fwd(x1, norm_mlp, "rms_mlp")
    r_act = mm(h2, S.weight("w1"), mode="nn", name="mlp_up", epi=_epi_relu2, out_dtypes=(BF16,))
    x2 = mm(r_act, S.weight("w2"), mode="nn", name="mlp_down", extras=(x1,), epi=_epi_add)
    dx2, dx2b, g_norm_final, loss_row = _final(x2, norm_final, tgt, "final")

    S.grad("w2", mm(r_act, dx2b, mode="tn", name="mlp_down_dw", out_dtypes=(BF16,)))
    da = mm(dx2b, S.weight("w2"), mode="nt", name="mlp_down_dx", extras=(r_act,), epi=_epi_relu2_bwd, out_dtypes=(BF16,))
    S.grad("w1", mm(h2, da, mode="tn", name="mlp_up_dw", out_dtypes=(BF16,)))
    dh2 = mm(da, S.weight("w1"), mode="nt", name="mlp_up_dx")
    dx1, dx1b, g_norm_mlp = _rms_bwd(x1, norm_mlp + S.tok(), dh2, dx2, "rms_mlp_bwd")
    S.grad("out", mm(merged, dx1b, mode="tn", name="out_proj_dw", out_dtypes=(BF16,)))
    dmerged = mm(dx1b, S.weight("out"), mode="nt", name="out_proj_dx")
    dbr_a, dbr_b, d_gate, g_b_gate = _merge_bwd(dmerged, p_gate, b_gate, br_a, br_b, "merge_bwd")
    S.grad("bssm", mm(yb, dbr_b, mode="tn", name="branch_ssm_dw", out_dtypes=(BF16,)))
    S.grad("bsc", mm(ya, dbr_a, mode="tn", name="branch_sc_dw", out_dtypes=(BF16,)))
    dyb = mm(dbr_b, S.weight("bssm"), mode="nt", name="branch_ssm_dx")
    dya = mm(dbr_a, S.weight("bsc"), mode="nt", name="branch_sc_dx")
    dy, d_z, g_ssm_norm_w = _gnorm_bwd(y, p_z, ssm_norm_w + S.tok(), dyb, "gnorm_bwd")
    dxs, dB, dC, ddt_e, dcs_e, dd_p = carrying(_ssd_bwd, xbc, dt_e, cs_e, d_e, states, dy, name="ssd_bwd")
    d_dt, g_dt_bias, g_a_log, g_d_skip = _ssd_post(ddt_e, dcs_e, dd_p, p_dt, dt_bias, a_log, n_heads, "ssd_post")
    d_xbc, g_ssm_w, g_ssm_b = carrying(_ssm_conv_bwd, p_xbc, ssm_w, ssm_b, dxs, dB, dC, name="ssm_conv_bwd")
    d_scB, d_scC, d_scX, g_sc_w = _sc_bwd(p_sc, sc_w, dya, "sc_bwd")
    d_sc = jnp.concatenate([d_scB, d_scC, d_scX], axis=1)
    pieces = [("sc", d_sc), ("z", d_z), ("xbc", d_xbc), ("dt", d_dt), ("gate", d_gate)]
    S.grad("win", {k: mm(hb, d, mode="tn", name="proj_dw_" + k, out_dtypes=(BF16,)) for k, d in pieces})
    pieces = [(k, d + S.tok().astype(d.dtype) if k == "dt" else d) for k, d in pieces]
    dh = mm([d for _, d in pieces], [S.weight(k) for k, _ in pieces], mode="nt", name="proj_dx")
    grad_x, _, g_norm_mix = _rms_bwd(x, norm_mix, dh, dx1, "rms_mix_bwd")

    g_small = dict(norm_mix=g_norm_mix, b_gate=g_b_gate, sc_conv_w=g_sc_w, ssm_conv_w=g_ssm_w, ssm_conv_b=g_ssm_b,
                   dt_bias=g_dt_bias, A_log=g_a_log, D_skip=g_d_skip, ssm_norm_w=g_ssm_norm_w, norm_mlp=g_norm_mlp,
                   norm_final=g_norm_final, loss=loss_row)
    return grad_x, g_small


class _Place:
    def __init__(self, k=0):
        x, y, c = lax.axis_index("x"), lax.axis_index("y"), lax.axis_index("c")
        self.x = 1 - x if k & 4 else x
        self.y = 1 - y if k & 2 else y
        self.c = 1 - c if k & 1 else c
        self.chip = 2 * self.x + self.y
        self.id = 2 * self.chip + self.c


ICI_PEERS = (2, 4, 6)
SIBLING = (1,)
ALL_PEERS = (1, 2, 3, 4, 5, 6, 7)


class _Comm:
    def __init__(self, arrs, out_shape, ks, src, dst, own=None, aliases=None):
        self.arrs, self.out_shape, self.ks = list(arrs), list(out_shape), tuple(ks)
        self.n = len(self.arrs)
        self.src, self.dst, self.own = src, dst, own
        self.aliases = aliases or {}
        dma = pltpu.SemaphoreType.DMA
        self.scratch = [dma((self.n, len(self.ks))), dma((self.n, len(self.ks))), dma((self.n,))]

    def _copies(self, ins, outs, sems, with_recvs):
        send_sems, recv_sems, local_sems = sems
        me = _Place()
        owns, sends, recvs = [], [], []
        for a in range(self.n):
            if self.own is not None:
                s, d = self.own(a, ins[a], outs[a], me)
                owns.append(pltpu.make_async_copy(s, d, local_sems.at[a]))
            for i, k in enumerate(self.ks):
                peer = _Place(k)
                for sender, lst in ((me, sends), (peer, recvs)) if with_recvs else ((me, sends),):
                    lst.append(pltpu.make_async_remote_copy(
                        src_ref=self.src(a, ins[a], me, peer), dst_ref=self.dst(a, outs[a], sender),
                        send_sem=send_sems.at[a, i], recv_sem=recv_sems.at[a, i],
                        device_id=(peer.x, peer.y, peer.c), device_id_type=MESH))
        return owns, sends, recvs

    def start(self, ins, outs, sems):
        owns, sends, _ = self._copies(ins, outs, sems, False)
        for cp in owns + sends:
            cp.start()

    def finish(self, ins, outs, sems):
        owns, sends, recvs = self._copies(ins, outs, sems, True)
        for cp in recvs:
            cp.wait_recv()
        for cp in sends:
            cp.wait_send()
        for cp in owns:
            cp.wait()


class _GatherBoth:
    def __init__(self, shards):
        self.arrs, self.n, self.aliases = list(shards), len(shards), {}
        self.out_shape = [_sds((4, 2) + s.shape, s.dtype) for s in shards]
        dma = pltpu.SemaphoreType.DMA
        self.scratch = [dma((self.n, 7)), dma((self.n, 7)), dma((self.n,))]

    def _copy(self, a, j, src, slot, to, outs, sems):
        return pltpu.make_async_remote_copy(src_ref=src, dst_ref=outs[a].at[slot.chip, slot.c], send_sem=sems[0].at[a, j],
                                            recv_sem=sems[1].at[a, j], device_id=(to.x, to.y, to.c), device_id_type=MESH)

    def start(self, ins, outs, sems):
        me, sib = _Place(), _Place(1)
        for a in range(self.n):
            pltpu.make_async_copy(ins[a], outs[a].at[me.chip, me.c], sems[2].at[a]).start()
            self._copy(a, 0, ins[a], me, sib, outs, sems).start()
            for i, k in enumerate(ICI_PEERS):
                self._copy(a, 1 + i, ins[a], me, _Place(k), outs, sems).start()

    def finish(self, ins, outs, sems):
        me, sib = _Place(), _Place(1)
        passed = []
        for i, k in enumerate(ICI_PEERS):
            peer = _Place(k)
            for a in range(self.n):
                self._copy(a, 1 + i, ins[a], peer, peer, outs, sems).wait_recv()
                cp = self._copy(a, 4 + i, outs[a].at[peer.chip, peer.c], peer, sib, outs, sems)
                cp.start()
                passed.append(cp)
        for a in range(self.n):
            self._copy(a, 0, ins[a], sib, sib, outs, sems).wait_recv()
            for i, k in enumerate(ICI_PEERS):
                far = _Place(k | 1)
                self._copy(a, 4 + i, outs[a].at[far.chip, far.c], far, sib, outs, sems).wait_recv()
        for a in range(self.n):
            self._copy(a, 0, ins[a], me, sib, outs, sems).wait_send()
            for i, k in enumerate(ICI_PEERS):
                self._copy(a, 1 + i, ins[a], me, _Place(k), outs, sems).wait_send()
            pltpu.make_async_copy(ins[a], outs[a].at[me.chip, me.c], sems[2].at[a]).wait()
        for cp in passed:
            cp.wait_send()


def _run_comm(comm, name, after=()):
    n, n_after = comm.n, len(after)

    def body(*refs):
        ins, outs, sems = refs[:n], refs[n + n_after:2 * n + n_after], refs[2 * n + n_after:]
        comm.start(ins, outs, sems)
        comm.finish(ins, outs, sems)

    return list(pl.pallas_call(body, in_specs=[ANY] * (n + n_after), out_specs=[ANY] * n, out_shape=comm.out_shape,
                               scratch_shapes=comm.scratch, input_output_aliases=dict(comm.aliases), name=name)(*comm.arrs, *after))


def _gather_sibling(bufs):
    return _Comm(bufs, [_sds(b.shape, b.dtype) for b in bufs], SIBLING,
                 src=lambda a, i, me, p: i.at[:, me.c], dst=lambda a, o, s: o.at[:, s.c], aliases={a: a for a in range(len(bufs))})


def _scatter_sibling(parts):
    return _Comm(parts, [_sds((4,) + p.shape[2:], p.dtype) for p in parts], SIBLING,
                 src=lambda a, i, me, p: i.at[:, p.c], dst=lambda a, o, s: o)


HBM_SPEC = pl.BlockSpec(memory_space=pltpu.HBM)
SEM_SPEC = pl.BlockSpec(memory_space=pltpu.SEMAPHORE)
DATAFLOW = pltpu.SideEffectType.DATAFLOW_SIDE_EFFECTING


def _tiles_2d(R, C, max_rows=256):
    if R % max_rows == 0:
        return max_rows, C, R // max_rows, lambda i: (i, 0)
    if R <= 2 * max_rows or C % 256:
        return R, C, 1, lambda i: (0, 0)
    return R, 256, C // 256, lambda i: (0, i)


def _ici_copy(gather, a, srcs, lands, send_sems, recv_sems, i, me, peer, sender):
    src = lands[a].at[me.chip, me.c] if gather else srcs[a].at[peer.chip]
    dst = lands[a].at[sender.chip, sender.c] if gather else lands[a].at[sender.chip]
    j = a * len(ICI_PEERS) + i
    return pltpu.make_async_remote_copy(src_ref=src, dst_ref=dst, send_sem=send_sems.at[j], recv_sem=recv_sems.at[j],
                                        device_id=(peer.x, peer.y, peer.c), device_id_type=MESH)


def _ici_start(srcs, lands, gather, name):
    n, n_s = len(lands), len(srcs)
    bufs = list(srcs) + list(lands)

    def body(*refs):
        src_refs, land_refs = refs[:n_s], refs[n_s:n_s + n]
        send_sems, recv_sems = refs[n_s + n], refs[n_s + n + 1]
        token = refs[-1]
        me = _Place()
        for a in range(n):
            for i, k in enumerate(ICI_PEERS):
                _ici_copy(gather, a, src_refs, land_refs, send_sems, recv_sems, i, me, _Place(k), me).start()
        token[...] = jnp.zeros_like(token)

    dma = pltpu.SemaphoreType.DMA((n * len(ICI_PEERS),))
    outs = pl.pallas_call(
        body, name=name, out_shape=(dma, dma, *[pltpu.HBM(v.shape, v.dtype) for v in bufs], _sds((8, LANES), F32)),
        in_specs=(HBM_SPEC,) * len(bufs),
        out_specs=(SEM_SPEC, SEM_SPEC) + (HBM_SPEC,) * len(bufs) + (pl.BlockSpec(memory_space=pltpu.VMEM),),
        input_output_aliases={j: 2 + j for j in range(len(bufs))}, compiler_params=pltpu.CompilerParams(has_side_effects=DATAFLOW),
    )(*[pltpu.with_memory_space_constraint(v, pltpu.HBM) for v in bufs])
    return outs[0], outs[1], list(outs[2:2 + n_s]), list(outs[2 + n_s:2 + n_s + n]), outs[-1]


def _ici_wait(flight, after, gather, name):
    send_sems, recv_sems, srcs, lands, _ = flight
    n, n_s = len(lands), len(srcs)
    bufs = srcs + lands

    def body(*refs):
        src_refs, land_refs = refs[:n_s], refs[n_s:n_s + n]
        s_sems, r_sems = refs[n_s + n], refs[n_s + n + 1]
        me = _Place()
        for a in range(n):
            for i, k in enumerate(ICI_PEERS):
                peer = _Place(k)
                cp = _ici_copy(gather, a, src_refs, land_refs, s_sems, r_sems, i, me, peer, peer)
                cp.wait_send()
                cp.wait_recv()

    outs = pl.pallas_call(
        body, name=name, out_shape=tuple(pltpu.HBM(v.shape, v.dtype) for v in bufs),
        in_specs=(HBM_SPEC,) * len(bufs) + (SEM_SPEC, SEM_SPEC) + (ANY,) * len(after), out_specs=(HBM_SPEC,) * len(bufs),
        input_output_aliases={j: j for j in range(len(bufs))}, compiler_params=pltpu.CompilerParams(has_side_effects=DATAFLOW),
    )(*bufs, send_sems, recv_sems, *after)
    return list(outs[n_s:])


def _own_shards(shards, after, name):
    n = len(shards)
    vmem = pl.BlockSpec(memory_space=pltpu.VMEM)

    def body(*refs):
        ins, outs, cast, sems = refs[:n], refs[n + 1:2 * n + 1], refs[2 * n + 1:3 * n + 1], refs[3 * n + 1]
        me = _Place()
        copies = []
        for a in range(n):
            cast[a][...] = ins[a][...].astype(BF16)
            copies.append(pltpu.make_async_copy(cast[a], outs[a].at[me.chip, me.c], sems.at[a]))
            copies[-1].start()
        for cp in copies:
            cp.wait()

    return list(pl.pallas_call(
        body, in_specs=[vmem] * n + [ANY], out_specs=[ANY] * n, out_shape=[_sds((4, 2) + s.shape, BF16) for s in shards],
        scratch_shapes=[pltpu.VMEM(s.shape, BF16) for s in shards] + [pltpu.SemaphoreType.DMA((n,))], name=name)(*shards, after))


def _col_pieces(widths):
    out, c = [], 0
    for k, w in widths:
        out.append((k, c, w))
        c += w
    return out


def _split_range(c0, n, bounds):
    parts, c = [], c0
    while c < c0 + n:
        r = max(i for i in range(len(bounds) - 1) if bounds[i] <= c)
        w = min(c0 + n, bounds[r + 1]) - c
        parts.append((r, c - bounds[r], w))
        c += w
    return parts


def _win_unpack(g, widths, name):
    n, R, C = g.shape
    tr = min(256, R)
    pieces = _col_pieces(widths)
    padded = [-(-w // LANES) * LANES for _, _, w in pieces]
    shard_bounds = [s * C for s in range(n + 1)]

    def body(g_ref, *o_refs):
        for (k, c0, w), o_ref in zip(pieces, o_refs):
            for t in range(0, o_ref.shape[1], LANES):
                valid = max(0, min(LANES, w - t))
                cols = [g_ref[s, :, o:o + ww] for s, o, ww in _split_range(c0 + t, valid, shard_bounds)] if valid else []
                if valid < LANES:
                    cols.append(jnp.zeros((tr, LANES - valid), g_ref.dtype))
                o_ref[:, t:t + LANES] = cols[0] if len(cols) == 1 else jnp.concatenate(cols, axis=1)

    return pl.pallas_call(
        body, grid=(R // tr,), in_specs=[pl.BlockSpec((n, tr, C), lambda i: (0, i, 0))],
        out_specs=[pl.BlockSpec((tr, p), lambda i: (i, 0)) for p in padded],
        out_shape=[_sds((R, p), g.dtype) for p in padded], name=name, compiler_params=_params("parallel"))(g)


def _win_pack(grads, widths, n, name):
    R = grads[0].shape[0]
    tr = min(256, R)
    pieces = _col_pieces(widths)
    total = pieces[-1][1] + pieces[-1][2]
    C = total // n
    bounds = [c0 for _, c0, _ in pieces] + [total]

    def body(*refs):
        g_refs, o_ref = refs[:-1], refs[-1]

        def tile_t(c0):
            cols = [g_refs[r][:, o:o + ww] for r, o, ww in _split_range(c0, LANES, bounds)]
            tile = cols[0] if len(cols) == 1 else jnp.concatenate(cols, axis=1)
            return tile.astype(F32).T

        for s in range(n):
            full = C // LANES * LANES
            for t in range(0, full, LANES):
                o_ref[s, t:t + LANES, :] = tile_t(s * C + t).astype(o_ref.dtype)
            if full < C:
                o_ref[s, full:C, :] = tile_t(s * C + C - LANES)[LANES - (C - full):, :].astype(o_ref.dtype)

    return pl.pallas_call(
        body, grid=(R // tr,), in_specs=[pl.BlockSpec((tr, gr.shape[1]), lambda i: (i, 0)) for gr in grads],
        out_specs=pl.BlockSpec((n, C, tr), lambda i: (0, 0, i)), out_shape=_sds((n, C, R), grads[0].dtype),
        name=name, compiler_params=_params("parallel"))(*grads)


def _gather_all(arrs):
    return _Comm(arrs, [_sds((N_DEV,) + a.shape, a.dtype) for a in arrs], ALL_PEERS,
                 src=lambda a, i, me, p: i, dst=lambda a, o, s: o.at[s.id], own=lambda a, i, o, me: (i, o.at[me.id]))


def _add_halves(parts, got, name):
    n, _, R, C = parts.shape
    br, bc, nb, at = _tiles_2d(R, C)
    place = jnp.stack([lax.axis_index("c"), 2 * lax.axis_index("x") + lax.axis_index("y")]).astype(jnp.int32)

    def body(q_ref, p_ref, g_ref, o_ref, land_ref):
        s = (p_ref[0, 0].astype(F32) + g_ref[0].astype(F32)).astype(o_ref.dtype)
        o_ref[0] = s

        @pl.when(pl.program_id(1) == q_ref[1])
        def _():
            land_ref[0] = s

    spec = pltpu.PrefetchScalarGridSpec(
        num_scalar_prefetch=1, grid=(nb, n),
        in_specs=[pl.BlockSpec((1, 1, br, bc), lambda i, q, q_ref: (q, q_ref[0]) + at(i)), pl.BlockSpec((1, br, bc), lambda i, q, q_ref: (q,) + at(i))],
        out_specs=[pl.BlockSpec((1, br, bc), lambda i, q, q_ref: (q,) + at(i)), pl.BlockSpec((1, br, bc), lambda i, q, q_ref: (q_ref[1],) + at(i))])
    return pl.pallas_call(body, grid_spec=spec, out_shape=[_sds((n, R, C), parts.dtype)] * 2, name=name,
                          compiler_params=_params("parallel", "arbitrary"))(place, parts, got)


def _adam(w, m, v, gparts, name, comm=None):
    R, C = w.shape
    n = gparts.shape[0]
    br, bc, nb, at = _tiles_2d(R, C, max_rows=128)
    c1 = 1.0 / (1.0 - ADAM_B1 ** ADAM_STEP)
    c2 = 1.0 / (1.0 - ADAM_B2 ** ADAM_STEP)

    def body(w_ref, m_ref, v_ref, g_ref, go_ref, d_ref, mo_ref, vo_ref):
        g = g_ref[0].astype(F32)
        for s in range(1, n):
            g = g + g_ref[s].astype(F32)
        mn = ADAM_B1 * m_ref[...] + (1.0 - ADAM_B1) * g
        vn = ADAM_B2 * v_ref[...] + (1.0 - ADAM_B2) * (g * g)
        go_ref[...] = g
        mo_ref[...] = mn
        vo_ref[...] = vn
        d_ref[...] = -ADAM_LR * ((mn * c1) / (jnp.sqrt(vn * c2) + ADAM_EPS) + ADAM_WD * w_ref[...])

    blk = pl.BlockSpec((br, bc), at)
    outs, carried = _call(
        body, grid=(nb,), in_specs=[blk, blk, blk, pl.BlockSpec((n, br, bc), lambda i: (0,) + at(i))],
        out_specs=[blk] * 4, out_shape=[_sds((R, C), F32)] * 4, args=[w, m, v, gparts], name=name, sem=("parallel",), comm=comm)
    return outs if comm is None else (outs, carried)


_SMALL_ORDER = ("norm_mix", "b_gate", "sc_conv_w", "ssm_conv_w", "ssm_conv_b", "dt_bias", "A_log", "D_skip", "ssm_norm_w",
                "norm_mlp", "norm_final", "loss")
_REPLICATED = ("norm_mix", "b_gate", "ssm_conv_b", "dt_bias", "A_log", "D_skip", "ssm_norm_w", "norm_mlp", "norm_final")


def _cols_to_slots(g, n):
    R = g.shape[0]
    return jnp.transpose(g.reshape(R, n, g.shape[1] // n), (1, 0, 2))


def _slots_to_cols(g):
    n, R, C = g.shape
    return jnp.transpose(g, (1, 0, 2)).reshape(R, n * C)


def kernel(x, norm_mix, w_in, b_gate, sc_conv_w, ssm_conv_w, ssm_conv_b, dt_bias, A_log, D_skip, ssm_norm_w, w_branch_sc, w_branch_ssm, w_out, norm_mlp, w_mlp1, w_mlp2, norm_final, loss_target, m_norm_mix, m_w_in, m_b_gate, m_sc_conv_w, m_ssm_conv_w, m_ssm_conv_b, m_dt_bias, m_A_log, m_D_skip, m_ssm_norm_w, m_w_branch_sc, m_w_branch_ssm, m_w_out, m_norm_mlp, m_w_mlp1, m_w_mlp2, m_norm_final, v_norm_mix, v_w_in, v_b_gate, v_sc_conv_w, v_ssm_conv_w, v_ssm_conv_b, v_dt_bias, v_A_log, v_D_skip, v_ssm_norm_w, v_w_branch_sc, v_w_branch_ssm, v_w_out, v_norm_mlp, v_w_mlp1, v_w_mlp2, v_norm_final):
    T, D = x.shape[1], x.shape[2]
    n_inner = 2 * D
    n_heads = n_inner // HEADDIM
    n_xbc = n_inner + 2 * NGROUPS * NSTATE
    me = 4 * lax.axis_index("x") + 2 * lax.axis_index("y") + lax.axis_index("c")

    in_cols = [("sc", 3 * D), ("z", n_inner), ("xbc", n_xbc), ("dt", n_heads), ("gate", 2 * D)]
    by_owner = lambda b: b.reshape((N_DEV,) + b.shape[2:])
    to_owner = lambda g: g.reshape((4, 2) + g.shape[1:])
    rows_of = lambda g: to_owner(g.reshape((N_DEV, g.shape[0] // N_DEV) + g.shape[1:]))
    cols_of = lambda g: to_owner(_cols_to_slots(g, N_DEV))

    class Schedule(_NoExchange):
        late = ("bssm", "bsc", "out", "w1", "w2")
        gather_sib = dict(gnorm_fwd=("bsc", "bssm", "out"), branch_ssm=("w1", "w2"))
        scatter_sib = dict(mlp_up_dx=("w2", "w1"), branch_ssm_dx=("out", "bssm", "bsc"))
        shards = dict(bsc=w_branch_sc, bssm=w_branch_ssm, out=w_out, w1=w_mlp1, w2=w_mlp2)

        def __init__(self):
            self.W, self.staged, self.grads, self.summed, self.scatters = {}, {}, {}, {}, []
            self.token = jnp.zeros((), F32)

        def first_weights(self, bufs):
            self.W.update(zip([k for k, _ in in_cols], _win_unpack(by_owner(bufs[0]), in_cols, "win_unpack")))
            self.W.update(sc_conv_w=_slots_to_cols(by_owner(bufs[1])), ssm_conv_w=_slots_to_cols(by_owner(bufs[2])))
            lands = _own_shards([self.shards[k] for k in self.late], bufs[1], "own_shards")
            self.gather_flight = _ici_start([], lands, True, "gather_late_start")
            self.token = self.gather_flight[4][0, 0]
            self.W["dt"] = self.W["dt"] + self.token.astype(BF16)

        def tok(self):
            return self.token

        def point(self, name, values):
            if name == "mixers_done":
                lands = _ici_wait(self.gather_flight, values, True, "gather_late_wait")
                self.staged.update(zip(self.late, lands))

        def carry(self, name):
            if name == "rms_mix":
                return _GatherBoth([w_in.astype(BF16), sc_conv_w, ssm_conv_w])
            if name in self.gather_sib:
                return _gather_sibling([self.staged.pop(k) for k in self.gather_sib[name]])
            if name in self.scatter_sib:
                return _scatter_sibling([self.grads[k] for k in self.scatter_sib[name]])
            return None

        def start_scatter(self, keys, halves_and_lands):
            halves, lands = [h for h, _ in halves_and_lands], [l for _, l in halves_and_lands]
            flight = _ici_start(halves, lands, False, "scatter_%s_start" % keys[0])
            self.scatters.append((keys, flight))
            self.token = flight[4][0, 0]

        def carried(self, name, outs):
            if name == "rms_mix":
                self.first_weights(outs)
            elif name in self.gather_sib:
                for k, b in zip(self.gather_sib[name], outs):
                    full = by_owner(b)
                    self.W[k] = _slots_to_cols(full) if k == "w1" else full.reshape(-1, D)
            else:
                keys = self.scatter_sib[name]
                self.start_scatter(keys, [_add_halves(self.grads[k], b, "add_halves_" + k) for k, b in zip(keys, outs)])

        def grad(self, k, g):
            if k == "win":
                g = to_owner(_win_pack([g[k] for k, _ in in_cols], in_cols, N_DEV, "win_pack"))
                got = _run_comm(_scatter_sibling([g]), "scatter_sibling_win")[0]
                self.start_scatter(("win",), [_add_halves(g, got, "add_halves_win")])
            else:
                self.grads[k] = cols_of(g) if k == "w1" else rows_of(g)

        def finish_scatter(self, after):
            keys, flight = self.scatters.pop(0)
            return dict(zip(keys, _ici_wait(flight, after, False, "scatter_%s_wait" % keys[0])))

    S = Schedule()
    small = dict(norm_mix=norm_mix, b_gate=b_gate, ssm_conv_b=ssm_conv_b, dt_bias=dt_bias, A_log=A_log, D_skip=D_skip,
                 ssm_norm_w=ssm_norm_w, norm_mlp=norm_mlp, norm_final=norm_final)
    grad_x, g_small = _local_step(x.reshape(T, D), loss_target.reshape(T, D), S, small)

    small_flat = jnp.concatenate([g_small[k].reshape(-1) for k in _SMALL_ORDER])
    n_small = small_flat.shape[0]
    rows = -(-n_small // (8 * LANES)) * 8
    small_pack = jnp.pad(small_flat, (0, rows * LANES - n_small)).reshape(rows, LANES)

    res = {}
    big = [("w_in", "win", w_in, m_w_in, v_w_in), ("w_branch_sc", "bsc", w_branch_sc, m_w_branch_sc, v_w_branch_sc),
           ("w_branch_ssm", "bssm", w_branch_ssm, m_w_branch_ssm, v_w_branch_ssm), ("w_out", "out", w_out, m_w_out, v_w_out),
           ("w_mlp1", "w1", w_mlp1, m_w_mlp1, v_w_mlp1), ("w_mlp2", "w2", w_mlp2, m_w_mlp2, v_w_mlp2)]
    by_grad = {gk: (k, w, m, v) for k, gk, w, m, v in big}
    after = [grad_x]
    while S.scatters:
        for gk, parts in S.finish_scatter(after).items():
            k, w, m, v = by_grad[gk]
            if gk == "win":
                res_t, (small_parts,) = _adam(w.T, m.T, v.T, parts, "adam_" + k, comm=_gather_all([small_pack]))
                res[k] = [r.T for r in res_t]
            else:
                res[k] = _adam(w, m, v, parts, "adam_" + k)
            after = after + [res[k][1]]

    sizes = {k: g_small[k].size for k in _SMALL_ORDER}
    offs, o = {}, 0
    for k in _SMALL_ORDER:
        offs[k] = o
        o += sizes[k]
    rep_w = dict(norm_mix=norm_mix, b_gate=b_gate, ssm_conv_b=ssm_conv_b, dt_bias=dt_bias, A_log=A_log, D_skip=D_skip,
                 ssm_norm_w=ssm_norm_w, norm_mlp=norm_mlp, norm_final=norm_final)
    rep_m = dict(norm_mix=m_norm_mix, b_gate=m_b_gate, ssm_conv_b=m_ssm_conv_b, dt_bias=m_dt_bias, A_log=m_A_log, D_skip=m_D_skip,
                 ssm_norm_w=m_ssm_norm_w, norm_mlp=m_norm_mlp, norm_final=m_norm_final)
    rep_v = dict(norm_mix=v_norm_mix, b_gate=v_b_gate, ssm_conv_b=v_ssm_conv_b, dt_bias=v_dt_bias, A_log=v_A_log, D_skip=v_D_skip,
                 ssm_norm_w=v_ssm_norm_w, norm_mlp=v_norm_mlp, norm_final=v_norm_final)

    def pack(d):
        segs = [jnp.pad(d[k].astype(F32).reshape(-1), (0, sizes[k] - d[k].size)) if k in d else jnp.zeros((sizes[k],), F32)
                for k in _SMALL_ORDER]
        return jnp.pad(jnp.concatenate(segs), (0, rows * LANES - n_small)).reshape(rows, LANES)

    sm = _adam(pack(rep_w), pack(rep_m), pack(rep_v), small_parts, "adam_small")
    sm = [s.reshape(-1) for s in sm]
    for k in _REPLICATED:
        n_k = rep_w[k].shape[0]
        res[k] = tuple(s[offs[k]:offs[k] + n_k] for s in sm)
    loss = sm[0][offs["loss"]]
    for k, w, m, v, K, full in (("sc_conv_w", sc_conv_w, m_sc_conv_w, v_sc_conv_w, SC_K, D),
                                ("ssm_conv_w", ssm_conv_w, m_ssm_conv_w, v_ssm_conv_w, SSM_K, n_xbc)):
        g_full = sm[0][offs[k]:offs[k] + K * full].reshape(K, full)
        cw = full // N_DEV
        g_mine = lax.dynamic_slice_in_dim(g_full, me * cw, cw, axis=1)
        res[k] = _adam(w, m, v, g_mine[None], "adam_" + k)

    order = ("norm_mix", "w_in", "b_gate", "sc_conv_w", "ssm_conv_w", "ssm_conv_b", "dt_bias", "A_log", "D_skip", "ssm_norm_w",
             "w_branch_sc", "w_branch_ssm", "w_out", "norm_mlp", "w_mlp1", "w_mlp2", "norm_final")
    outs = [loss, grad_x.reshape(1, T, D)]
    for j in range(4):
        outs += [res[k][j] for k in order]
    return tuple(outs)
```

```python
import jax
import jax.numpy as jnp
from jax import lax
from jax.experimental import pallas as pl
from jax.experimental.pallas import tpu as pltpu

F32 = jnp.float32
BF16 = jnp.bfloat16

EPS = 1e-6
N_DEV = 8
HEADDIM = 64
NSTATE = 128
CHUNK = 128
NGROUPS = 8
GROUP_W = 256
SC_K = 3
SSM_K = 4
LANES = 128

ADAM_LR = 0.001
ADAM_B1 = 0.9
ADAM_B2 = 0.999
ADAM_EPS = 1e-08
ADAM_WD = 0.01
ADAM_STEP = 10

NN = (((1,), (0,)), ((), ()))
NT = (((1,), (1,)), ((), ()))
TN = (((0,), (0,)), ((), ()))
_DIMS = {"nn": NN, "nt": NT, "tn": TN}

ANY = pl.BlockSpec(memory_space=pl.ANY)
MESH = pl.DeviceIdType.MESH


def _sds(shape, dtype):
    return jax.ShapeDtypeStruct(tuple(shape), dtype)


def _dot(a, b, dims=NN):
    return lax.dot_general(a, b, dims, preferred_element_type=F32)


def _dot3(a, b, dims=NN):
    return lax.dot_general(a, b, dims, preferred_element_type=F32, precision=lax.Precision.HIGH)


def _params(*sem):
    return pltpu.CompilerParams(dimension_semantics=tuple(sem))


def _call(body, *, grid, in_specs, out_specs, out_shape, args, name, sem, scratch=(), comm=None):
    if comm is None:
        outs = pl.pallas_call(body, grid=grid, in_specs=list(in_specs), out_specs=list(out_specs), out_shape=list(out_shape),
                              scratch_shapes=list(scratch), name=name, compiler_params=_params(*sem))(*args)
        return list(outs), None
    n, n_in, n_out, n_scr = comm.n, len(in_specs), len(out_shape), len(scratch)

    def wrapped(*refs):
        ins, c_in = refs[:n_in], refs[n_in:n_in + n]
        outs, c_out = refs[n_in + n:n_in + n + n_out], refs[n_in + n + n_out:n_in + 2 * n + n_out]
        rest = refs[n_in + 2 * n + n_out:]
        scr, sems = rest[:n_scr], rest[n_scr:]
        first, last = None, None
        for d, g in enumerate(grid):
            f, l = pl.program_id(d) == 0, pl.program_id(d) == g - 1
            first, last = (f, l) if first is None else (first & f, last & l)

        @pl.when(first)
        def _():
            comm.start(c_in, c_out, sems)

        body(*ins, *outs, *scr)

        @pl.when(last)
        def _():
            comm.finish(c_in, c_out, sems)

    outs = pl.pallas_call(
        wrapped, grid=grid, in_specs=list(in_specs) + [ANY] * n, out_specs=list(out_specs) + [ANY] * n,
        out_shape=list(out_shape) + comm.out_shape, scratch_shapes=list(scratch) + comm.scratch,
        input_output_aliases={n_in + i: n_out + o for i, o in comm.aliases.items()},
        name=name, compiler_params=_params(*["arbitrary"] * len(grid)))(*args, *comm.arrs)
    return list(outs[:n_out]), list(outs[n_out:])


MM_VMEM_BUDGET = 44 * 2 ** 20


def _mm_tiles(M, N, k_bytes, mn_bytes):
    best = None
    for tm in (2048, 1024, 512, 256, 128):
        for tn in (1024, 512, 256, 128):
            if M % tm or N % tn:
                continue
            need = 2 * ((tm + tn) * k_bytes + tm * tn * mn_bytes) + 4 * tm * tn * 4
            if need <= MM_VMEM_BUDGET and (best is None or (tm * tn, tm) > (best[0] * best[1], best[0])):
                best = (tm, tn)
    assert best is not None, (M, N, k_bytes, mn_bytes)
    return best


def _mm(a, b, *, mode, name, extras=(), epi=None, out_dtypes=(F32,), comm=None):
    a_list = list(a) if isinstance(a, (list, tuple)) else [a]
    b_list = list(b) if isinstance(b, (list, tuple)) else [b]
    if mode == "nn":
        M, N = a_list[0].shape[0], b_list[0].shape[1]
    elif mode == "nt":
        M, N = a_list[0].shape[0], b_list[0].shape[0]
    else:
        M, N = a_list[0].shape[1], b_list[0].shape[1]
    k_bytes = sum((av.shape[0] if mode == "tn" else av.shape[1]) * av.dtype.itemsize for av in a_list)
    mn_bytes = sum(e.dtype.itemsize for e in extras) + sum(jnp.dtype(d).itemsize for d in out_dtypes)
    tm, tn = _mm_tiles(min(M, 2048), min(N, 1024), k_bytes, mn_bytes) if M % 128 == 0 and N % 128 == 0 else (M, N)
    assert M % tm == 0 and N % tn == 0
    a_specs, b_specs = [], []
    for av, bv in zip(a_list, b_list):
        K = av.shape[0] if mode == "tn" else av.shape[1]
        a_specs.append(pl.BlockSpec((K, tm), lambda i, j: (0, i)) if mode == "tn" else pl.BlockSpec((tm, K), lambda i, j: (i, 0)))
        b_specs.append(pl.BlockSpec((tn, K), lambda i, j: (j, 0)) if mode == "nt" else pl.BlockSpec((K, tn), lambda i, j: (0, j)))
    mn_spec = pl.BlockSpec((tm, tn), lambda i, j: (i, j))
    n_p, n_ex = len(a_list), len(extras)
    dims = _DIMS[mode]

    def body(*refs):
        acc = _dot(refs[0][...], refs[n_p][...], dims)
        for p in range(1, n_p):
            acc = acc + _dot(refs[p][...], refs[n_p + p][...], dims)
        rest = refs[2 * n_p:]
        res = (acc,) if epi is None else epi(acc, *[r[...] for r in rest[:n_ex]])
        for o_ref, r in zip(rest[n_ex:], res):
            o_ref[...] = r.astype(o_ref.dtype)

    outs, carried = _call(
        body, grid=(M // tm, N // tn), in_specs=a_specs + b_specs + [mn_spec] * n_ex,
        out_specs=[mn_spec] * len(out_dtypes), out_shape=[_sds((M, N), d) for d in out_dtypes],
        args=a_list + b_list + list(extras), name=name, sem=("parallel", "parallel"), comm=comm)
    res = outs[0] if len(outs) == 1 else outs
    return res if comm is None else (res, carried)


def _epi_add(acc, r):
    return (acc + r,)


def _epi_relu2(acc):
    p = jnp.maximum(acc, 0.0)
    return (p * p,)


def _epi_relu2_bwd(acc, r):
    return (acc * (2.0 * jnp.sqrt(r.astype(F32))),)


def _row(tr, n):
    return pl.BlockSpec((tr, n), lambda i: (i, 0))


def _vec(n):
    return pl.BlockSpec((1, n), lambda i: (0, 0))


def _rms_fwd(x, w, name, comm=None):
    T, D = x.shape
    tr = min(256, T)

    def body(x_ref, w_ref, o_ref):
        xv = x_ref[...]
        r = lax.rsqrt(jnp.mean(xv * xv, axis=-1, keepdims=True) + EPS)
        o_ref[...] = (xv * r * w_ref[...]).astype(BF16)

    outs, carried = _call(body, grid=(T // tr,), in_specs=[_row(tr, D), _vec(D)], out_specs=[_row(tr, D)],
                          out_shape=[_sds((T, D), BF16)], args=[x, w], name=name, sem=("parallel",), comm=comm)
    return outs[0] if comm is None else (outs[0], carried)


def _rms_bwd(x, w, dh, dres, name):
    T, D = x.shape
    tr = min(256, T)

    def body(x_ref, w_ref, dh_ref, dres_ref, dx_ref, dxb_ref, dw_ref):
        @pl.when(pl.program_id(0) == 0)
        def _():
            dw_ref[...] = jnp.zeros_like(dw_ref)

        xv = x_ref[...]
        r = lax.rsqrt(jnp.mean(xv * xv, axis=-1, keepdims=True) + EPS)
        xh = xv * r
        dh_v = dh_ref[...]
        dw_ref[...] += jnp.sum(dh_v * xh, axis=0, keepdims=True)
        dxh = dh_v * w_ref[...]
        dx = r * (dxh - xh * jnp.mean(dxh * xh, axis=-1, keepdims=True)) + dres_ref[...]
        dx_ref[...] = dx
        dxb_ref[...] = dx.astype(BF16)

    return pl.pallas_call(
        body, grid=(T // tr,), in_specs=[_row(tr, D), _vec(D), _row(tr, D), _row(tr, D)],
        out_specs=[_row(tr, D), _row(tr, D), _vec(D)],
        out_shape=[_sds((T, D), F32), _sds((T, D), BF16), _sds((1, D), F32)],
        name=name, compiler_params=_params("arbitrary"))(x, w, dh, dres)


def _final(x2, w, tgt, name):
    T, D = x2.shape
    tr = min(256, T)

    def body(x_ref, w_ref, t_ref, dx_ref, dxb_ref, dw_ref, loss_ref):
        @pl.when(pl.program_id(0) == 0)
        def _():
            dw_ref[...] = jnp.zeros_like(dw_ref)
            loss_ref[...] = jnp.zeros_like(loss_ref)

        xv = x_ref[...]
        wv = w_ref[...]
        r = lax.rsqrt(jnp.mean(xv * xv, axis=-1, keepdims=True) + EPS)
        xh = xv * r
        err = xh * wv - t_ref[...]
        part = jnp.sum(jnp.sum(err * err, axis=1, keepdims=True), axis=0, keepdims=True) * (0.5 / D)
        loss_ref[...] += jnp.broadcast_to(part, loss_ref.shape)
        dy = err * (1.0 / D)
        dw_ref[...] += jnp.sum(dy * xh, axis=0, keepdims=True)
        dxh = dy * wv
        dx = r * (dxh - xh * jnp.mean(dxh * xh, axis=-1, keepdims=True))
        dx_ref[...] = dx
        dxb_ref[...] = dx.astype(BF16)

    return pl.pallas_call(
        body, grid=(T // tr,), in_specs=[_row(tr, D), _vec(D), _row(tr, D)],
        out_specs=[_row(tr, D), _row(tr, D), _vec(D), _vec(LANES)],
        out_shape=[_sds((T, D), F32), _sds((T, D), BF16), _sds((1, D), F32), _sds((1, LANES), F32)],
        name=name, compiler_params=_params("arbitrary"))(x2, w, tgt)


def _silu_parts(z):
    s = jax.nn.sigmoid(z)
    return z * s, s * (1.0 + z * (1.0 - s))


def _gnorm_fwd(y, z, w, name, comm=None):
    T, N = y.shape
    tr = min(256, T)

    def body(y_ref, z_ref, w_ref, o_ref):
        for g in range(N // GROUP_W):
            sl = slice(g * GROUP_W, (g + 1) * GROUP_W)
            silu, _ = _silu_parts(z_ref[:, sl])
            yz = y_ref[:, sl] * silu
            r = lax.rsqrt(jnp.mean(yz * yz, axis=-1, keepdims=True) + EPS)
            o_ref[:, sl] = (yz * r * w_ref[:, sl]).astype(BF16)

    outs, carried = _call(body, grid=(T // tr,), in_specs=[_row(tr, N), _row(tr, N), _vec(N)], out_specs=[_row(tr, N)],
                          out_shape=[_sds((T, N), BF16)], args=[y, z, w], name=name, sem=("parallel",), comm=comm)
    return outs[0] if comm is None else (outs[0], carried)


def _gnorm_bwd(y, z, w, dyb, name):
    T, N = y.shape
    tr = min(256, T)

    def body(y_ref, z_ref, w_ref, d_ref, dy_ref, dz_ref, dw_ref):
        @pl.when(pl.program_id(0) == 0)
        def _():
            dw_ref[...] = jnp.zeros_like(dw_ref)

        for g in range(N // GROUP_W):
            sl = slice(g * GROUP_W, (g + 1) * GROUP_W)
            yv = y_ref[:, sl]
            silu, dsilu = _silu_parts(z_ref[:, sl])
            yz = yv * silu
            r = lax.rsqrt(jnp.mean(yz * yz, axis=-1, keepdims=True) + EPS)
            yzh = yz * r
            d = d_ref[:, sl]
            dw_ref[:, sl] += jnp.sum(d * yzh, axis=0, keepdims=True)
            dyzh = d * w_ref[:, sl]
            dyz = r * (dyzh - yzh * jnp.mean(dyzh * yzh, axis=-1, keepdims=True))
            dy_ref[:, sl] = dyz * silu
            dz_ref[:, sl] = (dyz * yv * dsilu).astype(BF16)

    return pl.pallas_call(
        body, grid=(T // tr,), in_specs=[_row(tr, N), _row(tr, N), _vec(N), _row(tr, N)],
        out_specs=[_row(tr, N), _row(tr, N), _vec(N)],
        out_shape=[_sds((T, N), F32), _sds((T, N), BF16), _sds((1, N), F32)],
        name=name, compiler_params=_params("arbitrary"))(y, z, w, dyb)


def _merge_fwd(gate_raw, b_gate, br_a, br_b, name):
    T, D = br_a.shape
    tr = min(256, T)

    def body(g_ref, bg_ref, a_ref, b_ref, o_ref):
        g = jax.nn.sigmoid(g_ref[...] + bg_ref[...])
        o_ref[...] = (g[:, :D] * a_ref[...] + g[:, D:] * b_ref[...]).astype(BF16)

    return pl.pallas_call(body, grid=(T // tr,), in_specs=[_row(tr, 2 * D), _vec(2 * D), _row(tr, D), _row(tr, D)],
                          out_specs=_row(tr, D), out_shape=_sds((T, D), BF16), name=name,
                          compiler_params=_params("parallel"))(gate_raw, b_gate, br_a, br_b)


def _merge_bwd(dmerged, gate_raw, b_gate, br_a, br_b, name):
    T, D = br_a.shape
    tr = min(256, T)

    def body(d_ref, g_ref, bg_ref, a_ref, b_ref, da_ref, db_ref, dg_ref, dbg_ref):
        @pl.when(pl.program_id(0) == 0)
        def _():
            dbg_ref[...] = jnp.zeros_like(dbg_ref)

        g = jax.nn.sigmoid(g_ref[...] + bg_ref[...])
        d = d_ref[...]
        da_ref[...] = (d * g[:, :D]).astype(BF16)
        db_ref[...] = (d * g[:, D:]).astype(BF16)
        dg = jnp.concatenate([d * a_ref[...], d * b_ref[...]], axis=1) * g * (1.0 - g)
        dg_ref[...] = dg.astype(BF16)
        dbg_ref[...] += jnp.sum(dg, axis=0, keepdims=True)

    return pl.pallas_call(
        body, grid=(T // tr,), in_specs=[_row(tr, D), _row(tr, 2 * D), _vec(2 * D), _row(tr, D), _row(tr, D)],
        out_specs=[_row(tr, D), _row(tr, D), _row(tr, 2 * D), _vec(2 * D)],
        out_shape=[_sds((T, D), BF16), _sds((T, D), BF16), _sds((T, 2 * D), BF16), _sds((1, 2 * D), F32)],
        name=name, compiler_params=_params("arbitrary"))(dmerged, gate_raw, b_gate, br_a, br_b)


CB_W = 256
CONV_ROWS = 32
CONV_PAD = 8


def _rows_down(load, r0, s):
    if s == 0:
        return load(r0, r0 + CONV_ROWS)
    if r0 == 0:
        row = lax.broadcasted_iota(jnp.int32, (CONV_ROWS, CB_W), 0)
        return jnp.where(row >= s, pltpu.roll(load(0, CONV_ROWS), s, 0), 0.0)
    return load(r0 - s, r0 - s + CONV_ROWS)


def _conv_tile(load, taps, r0):
    K = len(taps)
    us = [_rows_down(load, r0, K - 1 - k) for k in range(K)]
    acc = us[K - 1] * taps[K - 1]
    for k in range(K - 1):
        acc = acc + us[k] * taps[k]
    return acc, us


def _conv_back_tile(scr, taps, r0):
    K = len(taps)
    du = scr[r0:r0 + CONV_ROWS, :] * taps[K - 1]
    for k in range(K - 1):
        s = K - 1 - k
        du = du + scr[r0 + s:r0 + s + CONV_ROWS, :] * taps[k]
    return du


def _fold8(v):
    return jnp.sum(v.reshape(CONV_ROWS // 8, 8, v.shape[1]), axis=0)


def _col(T, j0=0):
    return pl.BlockSpec((T, CB_W), lambda j: (0, j + j0))


def _sc_fwd(psc, w, name):
    T, D = psc.shape[0], psc.shape[1] // 3
    nb = D // CB_W

    def body(b_ref, c_ref, x_ref, w_ref, o_ref):
        taps = [w_ref[k:k + 1, :] for k in range(SC_K)]
        load = lambda a, b: c_ref[a:b, :] * x_ref[a:b, :]
        for r0 in range(0, T, CONV_ROWS):
            cu, _ = _conv_tile(load, taps, r0)
            o_ref[r0:r0 + CONV_ROWS, :] = (b_ref[r0:r0 + CONV_ROWS, :] * cu).astype(BF16)

    return pl.pallas_call(
        body, grid=(nb,), in_specs=[_col(T), _col(T, nb), _col(T, 2 * nb), pl.BlockSpec((SC_K, CB_W), lambda j: (0, j))],
        out_specs=_col(T), out_shape=_sds((T, D), BF16), name=name, compiler_params=_params("parallel"))(psc, psc, psc, w)


def _sc_bwd(psc, w, dya, name):
    T, D = psc.shape[0], psc.shape[1] // 3
    nb = D // CB_W

    def body(b_ref, c_ref, x_ref, w_ref, d_ref, db_ref, dc_ref, dx_ref, dw_ref, scr):
        taps = [w_ref[k:k + 1, :] for k in range(SC_K)]
        load = lambda a, b: c_ref[a:b, :] * x_ref[a:b, :]
        scr[T:T + CONV_PAD, :] = jnp.zeros((CONV_PAD, CB_W), F32)
        dw8 = [jnp.zeros((8, CB_W), F32)] * SC_K
        for r0 in range(0, T, CONV_ROWS):
            rows = slice(r0, r0 + CONV_ROWS)
            cu, us = _conv_tile(load, taps, r0)
            d = d_ref[rows, :]
            db_ref[rows, :] = (d * cu).astype(BF16)
            dcu = d * b_ref[rows, :]
            scr[rows, :] = dcu
            dw8 = [acc + _fold8(dcu * u) for acc, u in zip(dw8, us)]
        for k in range(SC_K):
            dw_ref[k:k + 1, :] = jnp.sum(dw8[k], axis=0, keepdims=True)
        for r0 in range(0, T, CONV_ROWS):
            rows = slice(r0, r0 + CONV_ROWS)
            du = _conv_back_tile(scr, taps, r0)
            dc_ref[rows, :] = (du * x_ref[rows, :]).astype(BF16)
            dx_ref[rows, :] = (du * c_ref[rows, :]).astype(BF16)

    wspec = pl.BlockSpec((SC_K, CB_W), lambda j: (0, j))
    return pl.pallas_call(
        body, grid=(nb,), in_specs=[_col(T), _col(T, nb), _col(T, 2 * nb), wspec, _col(T)],
        out_specs=[_col(T), _col(T), _col(T), wspec],
        out_shape=[_sds((T, D), BF16)] * 3 + [_sds((SC_K, D), F32)],
        scratch_shapes=[pltpu.VMEM((T + CONV_PAD, CB_W), F32)],
        name=name, compiler_params=_params("parallel"))(psc, psc, psc, w, dya)


def _ssm_conv_fwd(u, w, b, name, comm=None):
    T, N = u.shape

    def body(u_ref, w_ref, b_ref, o_ref):
        taps = [w_ref[k:k + 1, :] for k in range(SSM_K)]
        bias = b_ref[...]
        for r0 in range(0, T, CONV_ROWS):
            c, _ = _conv_tile(lambda a, b: u_ref[a:b, :], taps, r0)
            c = c + bias
            o_ref[r0:r0 + CONV_ROWS, :] = c * jax.nn.sigmoid(c)

    outs, carried = _call(
        body, grid=(N // CB_W,), in_specs=[_col(T), pl.BlockSpec((SSM_K, CB_W), lambda j: (0, j)), pl.BlockSpec((1, CB_W), lambda j: (0, j))],
        out_specs=[_col(T)], out_shape=[_sds((T, N), F32)], args=[u, w, b], name=name, sem=("parallel",), comm=comm)
    return outs[0] if comm is None else (outs[0], carried)


def _ssm_conv_bwd(u, w, b, dxs, dB, dC, name, comm=None):
    T, N = u.shape
    n_x, n_b = dxs.shape[1] // CB_W, dB.shape[1] // CB_W

    def body(u_ref, w_ref, b_ref, dx_ref, db_ref, dc_ref, du_ref, dw_ref, dbias_ref, scr):
        j = pl.program_id(0)
        taps = [w_ref[k:k + 1, :] for k in range(SSM_K)]
        bias = b_ref[...]
        scr[T:T + CONV_PAD, :] = jnp.zeros((CONV_PAD, CB_W), F32)
        dw8 = [jnp.zeros((8, CB_W), F32)] * SSM_K
        db8 = jnp.zeros((8, CB_W), F32)
        for r0 in range(0, T, CONV_ROWS):
            rows = slice(r0, r0 + CONV_ROWS)
            c, us = _conv_tile(lambda a, b: u_ref[a:b, :], taps, r0)
            _, dsilu = _silu_parts(c + bias)
            d = jnp.where(j < n_x, dx_ref[rows, :], jnp.where(j < n_x + n_b, db_ref[rows, :], dc_ref[rows, :])) * dsilu
            scr[rows, :] = d
            db8 = db8 + _fold8(d)
            dw8 = [acc + _fold8(d * u) for acc, u in zip(dw8, us)]
        dbias_ref[...] = jnp.sum(db8, axis=0, keepdims=True)
        for k in range(SSM_K):
            dw_ref[k:k + 1, :] = jnp.sum(dw8[k], axis=0, keepdims=True)
        for r0 in range(0, T, CONV_ROWS):
            du_ref[r0:r0 + CONV_ROWS, :] = _conv_back_tile(scr, taps, r0).astype(BF16)

    wspec = pl.BlockSpec((SSM_K, CB_W), lambda j: (0, j))
    bspec = pl.BlockSpec((1, CB_W), lambda j: (0, j))
    outs, carried = _call(
        body, grid=(N // CB_W,),
        in_specs=[_col(T), wspec, bspec,
                  pl.BlockSpec((T, CB_W), lambda j: (0, jnp.minimum(j, n_x - 1))),
                  pl.BlockSpec((T, CB_W), lambda j: (0, jnp.clip(j - n_x, 0, n_b - 1))),
                  pl.BlockSpec((T, CB_W), lambda j: (0, jnp.clip(j - n_x - n_b, 0, n_b - 1)))],
        out_specs=[_col(T), wspec, bspec],
        out_shape=[_sds((T, N), BF16), _sds((SSM_K, N), F32), _sds((1, N), F32)],
        scratch=[pltpu.VMEM((T + CONV_PAD, CB_W), F32)],
        args=[u, w, b, dxs, dB, dC], name=name, sem=("parallel",), comm=comm)
    return outs if comm is None else (outs, carried)


def _split3(v):
    hi = v.astype(BF16)
    r = v - hi.astype(F32)
    mid = r.astype(BF16)
    lo = (r - mid.astype(F32)).astype(BF16)
    return hi, mid, lo


def _head_expand(n_lanes):
    h = lax.broadcasted_iota(jnp.int32, (LANES, n_lanes), 0)
    l = lax.broadcasted_iota(jnp.int32, (LANES, n_lanes), 1)
    return (jnp.right_shift(l, HEADDIM.bit_length() - 1) == h).astype(BF16)


def _softplus(v):
    return jnp.maximum(v, 0.0) + jnp.log1p(jnp.exp(-jnp.abs(v)))


def _ssd_prep(dt_raw, dt_bias, a_log, n_inner, name):
    T = dt_raw.shape[0]

    def body(r_ref, b_ref, al_ref, ex_ref, dt_ref, cs_ref):
        dt = _softplus(r_ref[...] + b_ref[...])
        a = dt * (-jnp.exp(al_ref[...]))
        i = lax.broadcasted_iota(jnp.int32, (CHUNK, CHUNK), 0)
        j = lax.broadcasted_iota(jnp.int32, (CHUNK, CHUNK), 1)
        tri = (j <= i).astype(BF16)
        cs = sum(_dot(tri, p) for p in _split3(a))
        ex = ex_ref[...]
        dt_ref[...] = sum(_dot(p, ex) for p in _split3(dt))
        cs_ref[...] = sum(_dot(p, ex) for p in _split3(cs))

    blk = pl.BlockSpec((CHUNK, LANES), lambda c: (c, 0))
    out = pl.BlockSpec((CHUNK, n_inner), lambda c: (c, 0))
    ex_spec = pl.BlockSpec((LANES, n_inner), lambda c: (0, 0))
    return pl.pallas_call(body, grid=(T // CHUNK,), in_specs=[blk, _vec(LANES), _vec(LANES), ex_spec], out_specs=[out, out],
                          out_shape=[_sds((T, n_inner), F32)] * 2, name=name,
                          compiler_params=_params("parallel"))(dt_raw, dt_bias, a_log, _head_expand(n_inner))


def _pair_terms(cs_p):
    lane = lax.broadcasted_iota(jnp.int32, (CHUNK, CHUNK), 1)
    sub = lax.broadcasted_iota(jnp.int32, (CHUNK, CHUNK), 0)
    csT = cs_p.T
    Ls = []
    for k in range(2):
        col = jnp.sum(jnp.where(lane == k * HEADDIM, cs_p, 0.0), axis=1, keepdims=True)
        rowv = csT[k * HEADDIM:k * HEADDIM + 1, :]
        Ls.append(jnp.exp(jnp.where(sub >= lane, col - rowv, -jnp.inf)))
    return Ls, jnp.exp(csT[:, CHUNK - 1:CHUNK])


def _block_diag(xp):
    lane = lax.broadcasted_iota(jnp.int32, xp.shape, 1)
    return jnp.concatenate([jnp.where(lane < HEADDIM, xp, 0.0), jnp.where(lane >= HEADDIM, xp, 0.0)], axis=0)


SSD_GROUPS_PER_STEP = 8


def _ssd_specs(T, n_inner):
    nc, gs = T // CHUNK, SSD_GROUPS_PER_STEP
    bo, co = n_inner // (gs * NSTATE), (n_inner + NGROUPS * NSTATE) // (gs * NSTATE)
    assert NGROUPS % gs == 0 and n_inner % (gs * NSTATE) == 0 and (NGROUPS * NSTATE) % (gs * NSTATE) == 0
    g_blk = lambda f: pl.BlockSpec((CHUNK, gs * GROUP_W), lambda c, s: (f(c), s))
    b_blk = lambda f: pl.BlockSpec((CHUNK, gs * NSTATE), lambda c, s: (f(c), bo + s))
    c_blk = lambda f: pl.BlockSpec((CHUNK, gs * NSTATE), lambda c, s: (f(c), co + s))
    return nc, g_blk, b_blk, c_blk


def _ssd_fwd(xbc, dt_e, cs_e, d_e, name, comm=None):
    T = xbc.shape[0]
    n_inner = dt_e.shape[1]
    nc, g_blk, b_blk, c_blk = _ssd_specs(T, n_inner)
    ident = lambda c: c

    gs = SSD_GROUPS_PER_STEP

    def body(xs_ref, b_ref, c_ref, dt_ref, cs_ref, d_ref, y_ref, p_ref, st):
        c, s = pl.program_id(0), pl.program_id(1)

        @pl.when(c == 0)
        def _():
            for gi in range(gs):
                st[s * gs + gi] = jnp.zeros((GROUP_W, NSTATE), F32)

        for gi in range(gs):
            g = s * gs + gi
            gw, gn = slice(gi * GROUP_W, (gi + 1) * GROUP_W), slice(gi * NSTATE, (gi + 1) * NSTATE)
            P = st[g]
            p_ref[0, gi] = P
            xs, dt, cs = xs_ref[:, gw], dt_ref[:, gw], cs_ref[:, gw]
            Bf, Cf = b_ref[:, gn], c_ref[:, gn]
            Cb = Cf.astype(BF16)
            CBm = _dot(Cb, Bf.astype(BF16), NT)
            X = xs * dt
            decay = jnp.exp(cs[CHUNK - 1:CHUNK, :] - cs)
            y_off = _dot(Cb, P.astype(BF16), NT) * jnp.exp(cs)
            ys, ecl = [], []
            for pr in range(2):
                sl = slice(pr * LANES, (pr + 1) * LANES)
                Ls, e_last = _pair_terms(cs[:, sl])
                ecl.append(e_last)
                Mcat = jnp.concatenate([(CBm * L).astype(BF16) for L in Ls], axis=1)
                ys.append(_dot(Mcat, _block_diag(X[:, sl]).astype(BF16)))
            y_ref[:, gw] = jnp.concatenate(ys, axis=1) + y_off + xs * d_ref[:, gw]
            S = _dot3(X * decay, Bf, TN)
            st[g] = P * jnp.concatenate(ecl, axis=0) + S

    p_blk = pl.BlockSpec((1, gs, GROUP_W, NSTATE), lambda c, s: (c, s, 0, 0))
    outs, carried = _call(
        body, grid=(nc, NGROUPS // gs),
        in_specs=[g_blk(ident), b_blk(ident), c_blk(ident), g_blk(ident), g_blk(ident), pl.BlockSpec((1, gs * GROUP_W), lambda c, s: (0, s))],
        out_specs=[g_blk(ident), p_blk],
        out_shape=[_sds((T, n_inner), F32), _sds((nc, NGROUPS, GROUP_W, NSTATE), F32)],
        scratch=[pltpu.VMEM((NGROUPS, GROUP_W, NSTATE), F32)],
        args=[xbc, xbc, xbc, dt_e, cs_e, d_e], name=name, sem=("arbitrary", "arbitrary"), comm=comm)
    return outs if comm is None else (outs, carried)


def _ssd_bwd(xbc, dt_e, cs_e, d_e, states, dy, name, comm=None):
    T = xbc.shape[0]
    n_inner = dt_e.shape[1]
    nc, g_blk, b_blk, c_blk = _ssd_specs(T, n_inner)
    rev = lambda c: nc - 1 - c

    gs = SSD_GROUPS_PER_STEP

    def body(xs_ref, b_ref, c_ref, dt_ref, cs_ref, d_ref, p_ref, pn_ref, dy_ref,
             dxs_ref, db_ref, dc_ref, ddt_ref, dcs_ref, dd_ref, dst):
        cc, s = pl.program_id(0), pl.program_id(1)

        @pl.when(cc == 0)
        def _():
            for gi in range(gs):
                dst[s * gs + gi] = jnp.zeros((GROUP_W, NSTATE), F32)

        for gi in range(gs):
            one_group(s * gs + gi, gi, xs_ref, b_ref, c_ref, dt_ref, cs_ref, d_ref, p_ref, pn_ref, dy_ref,
                      dxs_ref, db_ref, dc_ref, ddt_ref, dcs_ref, dd_ref, dst)

    def one_group(g, gi, xs_ref, b_ref, c_ref, dt_ref, cs_ref, d_ref, p_ref, pn_ref, dy_ref,
                  dxs_ref, db_ref, dc_ref, ddt_ref, dcs_ref, dd_ref, dst):
        gw, gn = slice(gi * GROUP_W, (gi + 1) * GROUP_W), slice(gi * NSTATE, (gi + 1) * NSTATE)
        dS = dst[g]
        P, Pn = p_ref[0, gi], pn_ref[0, gi]
        xs, dt, cs, dY = xs_ref[:, gw], dt_ref[:, gw], cs_ref[:, gw], dy_ref[:, gw]
        Bf, Cf = b_ref[:, gn], c_ref[:, gn]
        Bb, Cb = Bf.astype(BF16), Cf.astype(BF16)
        X = xs * dt
        ecs = jnp.exp(cs)
        decay = jnp.exp(cs[CHUNK - 1:CHUNK, :] - cs)
        CBm = _dot3(Cf, Bf, NT)
        dYe = dY * ecs
        dP_off = _dot3(dYe, Cf, TN)
        dC = _dot(dYe.astype(BF16), P.astype(BF16))
        dcs = dYe * _dot3(Cf, P, NT)
        Xd = X * decay
        dB = _dot(Xd.astype(BF16), dS.astype(BF16))
        E = _dot3(Bf, dS, NT)
        dX = E * decay
        dcs = dcs - E * Xd
        R = _dot3(jnp.ones((8, NSTATE), F32), dS * Pn, NT)
        sub_g = lax.broadcasted_iota(jnp.int32, (CHUNK, GROUP_W), 0)
        dcs = dcs + jnp.where(sub_g == CHUNK - 1, R[0:1, :], 0.0)
        lane = lax.broadcasted_iota(jnp.int32, (CHUNK, CHUNK), 1)
        sub = lax.broadcasted_iota(jnp.int32, (CHUNK, CHUNK), 0)
        dCB = jnp.zeros((CHUNK, CHUNK), F32)
        dXs, dcss, ecl = [], [], []
        for pr in range(2):
            sl = slice(pr * LANES, (pr + 1) * LANES)
            Ls, e_last = _pair_terms(cs[:, sl])
            ecl.append(e_last)
            dYpb = dY[:, sl].astype(BF16)
            dMcat = _dot(dYpb, _block_diag(X[:, sl]).astype(BF16), NT)
            Mcat = jnp.concatenate([(CBm * L).astype(BF16) for L in Ls], axis=1)
            dXt = _dot(Mcat, dYpb, TN)
            dXs.append(jnp.where(lane < HEADDIM, dXt[:CHUNK], dXt[CHUNK:]))
            colacc = jnp.zeros((CHUNK, CHUNK), F32)
            rowacc = jnp.zeros((CHUNK, CHUNK), F32)
            for k in range(2):
                dG = dMcat[:, k * CHUNK:(k + 1) * CHUNK] * Ls[k]
                dCB = dCB + dG
                Q = dG * CBm
                colacc = colacc + jnp.where(lane == k * HEADDIM, jnp.sum(Q, axis=1, keepdims=True), 0.0)
                rowacc = rowacc + jnp.where(sub == k * HEADDIM, jnp.sum(Q, axis=0, keepdims=True), 0.0)
            dcss.append(colacc - rowacc.T)
        dX = dX + jnp.concatenate(dXs, axis=1)
        dcs = dcs + jnp.concatenate(dcss, axis=1)
        dCBb = dCB.astype(BF16)
        dc_ref[:, gn] = dC + _dot(dCBb, Bb)
        db_ref[:, gn] = dB + _dot(dCBb, Cb, TN)
        dxs_ref[:, gw] = dX * dt + dY * d_ref[:, gw]
        ddt_ref[:, gw] = dX * xs
        dcs_ref[:, gw] = dcs
        dd_ref[0, :, gw] = jnp.sum(dY * xs, axis=0, keepdims=True)
        dst[g] = dS * jnp.concatenate(ecl, axis=0) + dP_off

    p_blk = pl.BlockSpec((1, gs, GROUP_W, NSTATE), lambda c, s: (nc - 1 - c, s, 0, 0))
    pn_blk = pl.BlockSpec((1, gs, GROUP_W, NSTATE), lambda c, s: (jnp.minimum(nc - c, nc - 1), s, 0, 0))
    st_blk = pl.BlockSpec((CHUNK, gs * NSTATE), lambda c, s: (nc - 1 - c, s))
    outs, carried = _call(
        body, grid=(nc, NGROUPS // gs),
        in_specs=[g_blk(rev), b_blk(rev), c_blk(rev), g_blk(rev), g_blk(rev), pl.BlockSpec((1, gs * GROUP_W), lambda c, s: (0, s)),
                  p_blk, pn_blk, g_blk(rev)],
        out_specs=[g_blk(rev), st_blk, st_blk, g_blk(rev), g_blk(rev), pl.BlockSpec((1, 1, gs * GROUP_W), lambda c, s: (nc - 1 - c, 0, s))],
        out_shape=[_sds((T, n_inner), F32), _sds((T, NGROUPS * NSTATE), F32), _sds((T, NGROUPS * NSTATE), F32),
                   _sds((T, n_inner), F32), _sds((T, n_inner), F32), _sds((nc, 1, n_inner), F32)],
        scratch=[pltpu.VMEM((NGROUPS, GROUP_W, NSTATE), F32)],
        args=[xbc, xbc, xbc, dt_e, cs_e, d_e, states, states, dy], name=name, sem=("arbitrary", "arbitrary"), comm=comm)
    return outs if comm is None else (outs, carried)


def _ssd_post(ddt_e, dcs_e, dd_p, dt_raw, dt_bias, a_log, n_heads, name):
    T, n_inner = ddt_e.shape

    def body(ddt_ref, dcs_ref, dd_ref, r_ref, b_ref, al_ref, ex_ref, draw_ref, dbias_ref, dal_ref, ddsk_ref):
        @pl.when(pl.program_id(0) == 0)
        def _():
            dbias_ref[...] = jnp.zeros_like(dbias_ref)
            dal_ref[...] = jnp.zeros_like(dal_ref)
            ddsk_ref[...] = jnp.zeros_like(ddsk_ref)

        spread = [ddt_ref[...], dcs_ref[...], jnp.broadcast_to(dd_ref[0], (8, n_inner))]
        stacked = _dot(jnp.concatenate([p for v in spread for p in _split3(v)], axis=0), ex_ref[...], NT)
        sums, r0 = [], 0
        for v in spread:
            n = v.shape[0]
            sums.append(stacked[r0:r0 + n] + stacked[r0 + n:r0 + 2 * n] + stacked[r0 + 2 * n:r0 + 3 * n])
            r0 += 3 * n
        ddt_h, dcs_h, dd_h = sums
        raw = r_ref[...] + b_ref[...]
        dt = _softplus(raw)
        A = -jnp.exp(al_ref[...])
        i = lax.broadcasted_iota(jnp.int32, (CHUNK, CHUNK), 0)
        j = lax.broadcasted_iota(jnp.int32, (CHUNK, CHUNK), 1)
        upper = (j >= i).astype(BF16)
        da = sum(_dot(upper, p) for p in _split3(dcs_h))
        ddt = ddt_h + da * A
        lane = lax.broadcasted_iota(jnp.int32, (CHUNK, LANES), 1)
        draw = jnp.where(lane < n_heads, ddt * jax.nn.sigmoid(raw), 0.0)
        draw_ref[...] = draw.astype(BF16)
        dbias_ref[...] += jnp.sum(draw, axis=0, keepdims=True)
        dal_ref[...] += jnp.sum(da * dt, axis=0, keepdims=True) * A
        ddsk_ref[...] += dd_h[0:1, :]

    wide = pl.BlockSpec((CHUNK, n_inner), lambda c: (c, 0))
    blk = pl.BlockSpec((CHUNK, LANES), lambda c: (c, 0))
    return pl.pallas_call(
        body, grid=(T // CHUNK,),
        in_specs=[wide, wide, pl.BlockSpec((1, 1, n_inner), lambda c: (c, 0, 0)), blk, _vec(LANES), _vec(LANES),
                  pl.BlockSpec((LANES, n_inner), lambda c: (0, 0))],
        out_specs=[blk, _vec(LANES), _vec(LANES), _vec(LANES)],
        out_shape=[_sds((T, LANES), BF16)] + [_sds((1, LANES), F32)] * 3,
        name=name, compiler_params=_params("arbitrary"))(ddt_e, dcs_e, dd_p, dt_raw, dt_bias, a_log, _head_expand(n_inner))


def _row2(v):
    return v.reshape(1, -1).astype(F32)


def _pad_lanes(v):
    return jnp.pad(_row2(v), ((0, 0), (0, LANES - v.shape[-1])))


class _NoExchange:
    def __init__(self, W):
        self.W, self.grads = W, {}

    def weight(self, k):
        return self.W[k]

    def carry(self, name):
        return None

    def carried(self, name, outs):
        pass

    def grad(self, k, g):
        self.grads[k] = g

    def tok(self):
        return jnp.zeros((), F32)

    def point(self, name, value):
        pass


def _local_step(x, tgt, S, small):
    T, D = x.shape

    def mm(a, b, *, name, **kw):
        comm = S.carry(name)
        if comm is None:
            return _mm(a, b, name=name, **kw)
        res, outs = _mm(a, b, name=name, comm=comm, **kw)
        S.carried(name, outs)
        return res

    def carrying(fn, *args, name):
        comm = S.carry(name)
        if comm is None:
            return fn(*args, name)
        res, outs = fn(*args, name, comm=comm)
        S.carried(name, outs)
        return res

    n_inner = 2 * D
    n_heads = n_inner // HEADDIM
    norm_mix, norm_mlp, norm_final = _row2(small["norm_mix"]), _row2(small["norm_mlp"]), _row2(small["norm_final"])
    b_gate, ssm_b, ssm_norm_w = _row2(small["b_gate"]), _row2(small["ssm_conv_b"]), _row2(small["ssm_norm_w"])
    dt_bias, a_log = _pad_lanes(small["dt_bias"]), _pad_lanes(small["A_log"])
    d_e = jnp.repeat(small["D_skip"].astype(F32), HEADDIM).reshape(1, n_inner)

    hb = carrying(_rms_fwd, x, norm_mix, name="rms_mix")
    sc_w, ssm_w = S.weight("sc_conv_w"), S.weight("ssm_conv_w")
    p_xbc = mm(hb, S.weight("xbc"), mode="nn", name="proj_xbc")
    p_dt = mm(hb, S.weight("dt"), mode="nn", name="proj_dt")
    p_z = mm(hb, S.weight("z"), mode="nn", name="proj_z")
    p_sc = mm(hb, S.weight("sc"), mode="nn", name="proj_sc")
    p_gate = mm(hb, S.weight("gate"), mode="nn", name="proj_gate")
    xbc = carrying(_ssm_conv_fwd, p_xbc, ssm_w, ssm_b, name="ssm_conv_fwd")
    dt_e, cs_e = _ssd_prep(p_dt, dt_bias, a_log, n_inner, "ssd_prep")
    ya = _sc_fwd(p_sc, sc_w, "sc_fwd")
    y, states = carrying(_ssd_fwd, xbc, dt_e, cs_e, d_e, name="ssd_fwd")
    S.point("mixers_done", [y, ya, p_gate])
    yb = carrying(_gnorm_fwd, y, p_z, ssm_norm_w, name="gnorm_fwd")
    br_a = mm(ya, S.weight("bsc"), mode="nn", name="branch_sc")
    br_b = mm(yb, S.weight("bssm"), mode="nn", name="branch_ssm")
    merged = _merge_fwd(p_gate, b_gate, br_a, br_b, "merge_fwd")
    x1 = mm(merged, S.weight("out"), mode="nn", name="out_proj", extras=(x,), epi=_epi_add)
    h2 = _rms_fwd(x1, norm_mlp, "rms_mlp")
    r_act = mm(h2, S.weight("w1"), mode="nn", name="mlp_up", epi=_epi_relu2, out_dtypes=(BF16,))
    x2 = mm(r_act, S.weight("w2"), mode="nn", name="mlp_down", extras=(x1,), epi=_epi_add)
    dx2, dx2b, g_norm_final, loss_row = _final(x2, norm_final, tgt, "final")

    S.grad("w2", mm(r_act, dx2b, mode="tn", name="mlp_down_dw", out_dtypes=(BF16,)))
    da = mm(dx2b, S.weight("w2"), mode="nt", name="mlp_down_dx", extras=(r_act,), epi=_epi_relu2_bwd, out_dtypes=(BF16,))
    S.grad("w1", mm(h2, da, mode="tn", name="mlp_up_dw", out_dtypes=(BF16,)))
    dh2 = mm(da, S.weight("w1"), mode="nt", name="mlp_up_dx")
    dx1, dx1b, g_norm_mlp = _rms_bwd(x1, norm_mlp + S.tok(), dh2, dx2, "rms_mlp_bwd")
    S.grad("out", mm(merged, dx1b, mode="tn", name="out_proj_dw", out_dtypes=(BF16,)))
    dmerged = mm(dx1b, S.weight("out"), mode="nt", name="out_proj_dx")
    dbr_a, dbr_b, d_gate, g_b_gate = _merge_bwd(dmerged, p_gate, b_gate, br_a, br_b, "merge_bwd")
    S.grad("bssm", mm(yb, dbr_b, mode="tn", name="branch_ssm_dw", out_dtypes=(BF16,)))
    S.grad("bsc", mm(ya, dbr_a, mode="tn", name="branch_sc_dw", out_dtypes=(BF16,)))
    dyb = mm(dbr_b, S.weight("bssm"), mode="nt", name="branch_ssm_dx")
    dya = mm(dbr_a, S.weight("bsc"), mode="nt", name="branch_sc_dx")
    dy, d_z, g_ssm_norm_w = _gnorm_bwd(y, p_z, ssm_norm_w + S.tok(), dyb, "gnorm_bwd")
    dxs, dB, dC, ddt_e, dcs_e, dd_p = carrying(_ssd_bwd, xbc, dt_e, cs_e, d_e, states, dy, name="ssd_bwd")
    d_dt, g_dt_bias, g_a_log, g_d_skip = _ssd_post(ddt_e, dcs_e, dd_p, p_dt, dt_bias, a_log, n_heads, "ssd_post")
    d_xbc, g_ssm_w, g_ssm_b = carrying(_ssm_conv_bwd, p_xbc, ssm_w, ssm_b, dxs, dB, dC, name="ssm_conv_bwd")
    d_scB, d_scC, d_scX, g_sc_w = _sc_bwd(p_sc, sc_w, dya, "sc_bwd")
    d_sc = jnp.concatenate([d_scB, d_scC, d_scX], axis=1)
    pieces = [("sc", d_sc), ("z", d_z), ("xbc", d_xbc), ("dt", d_dt), ("gate", d_gate)]
    S.grad("win", {k: mm(hb, d, mode="tn", name="proj_dw_" + k, out_dtypes=(BF16,)) for k, d in pieces})
    pieces = [(k, d + S.tok().astype(d.dtype) if k == "dt" else d) for k, d in pieces]
    dh = mm([d for _, d in pieces], [S.weight(k) for k, _ in pieces], mode="nt", name="proj_dx")
    grad_x, _, g_norm_mix = _rms_bwd(x, norm_mix, dh, dx1, "rms_mix_bwd")

    g_small = dict(norm_mix=g_norm_mix, b_gate=g_b_gate, sc_conv_w=g_sc_w, ssm_conv_w=g_ssm_w, ssm_conv_b=g_ssm_b,
                   dt_bias=g_dt_bias, A_log=g_a_log, D_skip=g_d_skip, ssm_norm_w=g_ssm_norm_w, norm_mlp=g_norm_mlp,
                   norm_final=g_norm_final, loss=loss_row)
    return grad_x, g_small


class _Place:
    def __init__(self, k=0):
        x, y, c = lax.axis_index("x"), lax.axis_index("y"), lax.axis_index("c")
        self.x = 1 - x if k & 4 else x
        self.y = 1 - y if k & 2 else y
        self.c = 1 - c if k & 1 else c
        self.chip = 2 * self.x + self.y
        self.id = 2 * self.chip + self.c


ICI_PEERS = (2, 4, 6)
SIBLING = (1,)
ALL_PEERS = (1, 2, 3, 4, 5, 6, 7)


class _Comm:
    def __init__(self, arrs, out_shape, ks, src, dst, own=None, aliases=None):
        self.arrs, self.out_shape, self.ks = list(arrs), list(out_shape), tuple(ks)
        self.n = len(self.arrs)
        self.src, self.dst, self.own = src, dst, own
        self.aliases = aliases or {}
        dma = pltpu.SemaphoreType.DMA
        self.scratch = [dma((self.n, len(self.ks))), dma((self.n, len(self.ks))), dma((self.n,))]

    def _copies(self, ins, outs, sems, with_recvs):
        send_sems, recv_sems, local_sems = sems
        me = _Place()
        owns, sends, recvs = [], [], []
        for a in range(self.n):
            if self.own is not None:
                s, d = self.own(a, ins[a], outs[a], me)
                owns.append(pltpu.make_async_copy(s, d, local_sems.at[a]))
            for i, k in enumerate(self.ks):
                peer = _Place(k)
                for sender, lst in ((me, sends), (peer, recvs)) if with_recvs else ((me, sends),):
                    lst.append(pltpu.make_async_remote_copy(
                        src_ref=self.src(a, ins[a], me, peer), dst_ref=self.dst(a, outs[a], sender),
                        send_sem=send_sems.at[a, i], recv_sem=recv_sems.at[a, i],
                        device_id=(peer.x, peer.y, peer.c), device_id_type=MESH))
        return owns, sends, recvs

    def start(self, ins, outs, sems):
        owns, sends, _ = self._copies(ins, outs, sems, False)
        for cp in owns + sends:
            cp.start()

    def finish(self, ins, outs, sems):
        owns, sends, recvs = self._copies(ins, outs, sems, True)
        for cp in recvs:
            cp.wait_recv()
        for cp in sends:
            cp.wait_send()
        for cp in owns:
            cp.wait()


class _GatherBoth:
    def __init__(self, shards):
        self.arrs, self.n, self.aliases = list(shards), len(shards), {}
        self.out_shape = [_sds((4, 2) + s.shape, s.dtype) for s in shards]
        dma = pltpu.SemaphoreType.DMA
        self.scratch = [dma((self.n, 7)), dma((self.n, 7)), dma((self.n,))]

    def _copy(self, a, j, src, slot, to, outs, sems):
        return pltpu.make_async_remote_copy(src_ref=src, dst_ref=outs[a].at[slot.chip, slot.c], send_sem=sems[0].at[a, j],
                                            recv_sem=sems[1].at[a, j], device_id=(to.x, to.y, to.c), device_id_type=MESH)

    def start(self, ins, outs, sems):
        me, sib = _Place(), _Place(1)
        for a in range(self.n):
            pltpu.make_async_copy(ins[a], outs[a].at[me.chip, me.c], sems[2].at[a]).start()
            self._copy(a, 0, ins[a], me, sib, outs, sems).start()
            for i, k in enumerate(ICI_PEERS):
                self._copy(a, 1 + i, ins[a], me, _Place(k), outs, sems).start()

    def finish(self, ins, outs, sems):
        me, sib = _Place(), _Place(1)
        passed = []
        for i, k in enumerate(ICI_PEERS):
            peer = _Place(k)
            for a in range(self.n):
                self._copy(a, 1 + i, ins[a], peer, peer, outs, sems).wait_recv()
                cp = self._copy(a, 4 + i, outs[a].at[peer.chip, peer.c], peer, sib, outs, sems)
                cp.start()
                passed.append(cp)
        for a in range(self.n):
            self._copy(a, 0, ins[a], sib, sib, outs, sems).wait_recv()
            for i, k in enumerate(ICI_PEERS):
                far = _Place(k | 1)
                self._copy(a, 4 + i, outs[a].at[far.chip, far.c], far, sib, outs, sems).wait_recv()
        for a in range(self.n):
            self._copy(a, 0, ins[a], me, sib, outs, sems).wait_send()
            for i, k in enumerate(ICI_PEERS):
                self._copy(a, 1 + i, ins[a], me, _Place(k), outs, sems).wait_send()
            pltpu.make_async_copy(ins[a], outs[a].at[me.chip, me.c], sems[2].at[a]).wait()
        for cp in passed:
            cp.wait_send()


def _run_comm(comm, name, after=()):
    n, n_after = comm.n, len(after)

    def body(*refs):
        ins, outs, sems = refs[:n], refs[n + n_after:2 * n + n_after], refs[2 * n + n_after:]
        comm.start(ins, outs, sems)
        comm.finish(ins, outs, sems)

    return list(pl.pallas_call(body, in_specs=[ANY] * (n + n_after), out_specs=[ANY] * n, out_shape=comm.out_shape,
                               scratch_shapes=comm.scratch, input_output_aliases=dict(comm.aliases), name=name)(*comm.arrs, *after))


def _gather_sibling(bufs):
    return _Comm(bufs, [_sds(b.shape, b.dtype) for b in bufs], SIBLING,
                 src=lambda a, i, me, p: i.at[:, me.c], dst=lambda a, o, s: o.at[:, s.c], aliases={a: a for a in range(len(bufs))})


def _scatter_sibling(parts):
    return _Comm(parts, [_sds((4,) + p.shape[2:], p.dtype) for p in parts], SIBLING,
                 src=lambda a, i, me, p: i.at[:, p.c], dst=lambda a, o, s: o)


HBM_SPEC = pl.BlockSpec(memory_space=pltpu.HBM)
SEM_SPEC = pl.BlockSpec(memory_space=pltpu.SEMAPHORE)
DATAFLOW = pltpu.SideEffectType.DATAFLOW_SIDE_EFFECTING


def _tiles_2d(R, C, max_rows=256):
    if R % max_rows == 0:
        return max_rows, C, R // max_rows, lambda i: (i, 0)
    if R <= 2 * max_rows or C % 256:
        return R, C, 1, lambda i: (0, 0)
    return R, 256, C // 256, lambda i: (0, i)


def _ici_copy(gather, a, srcs, lands, send_sems, recv_sems, i, me, peer, sender):
    src = lands[a].at[me.chip, me.c] if gather else srcs[a].at[peer.chip]
    dst = lands[a].at[sender.chip, sender.c] if gather else lands[a].at[sender.chip]
    j = a * len(ICI_PEERS) + i
    return pltpu.make_async_remote_copy(src_ref=src, dst_ref=dst, send_sem=send_sems.at[j], recv_sem=recv_sems.at[j],
                                        device_id=(peer.x, peer.y, peer.c), device_id_type=MESH)


def _ici_start(srcs, lands, gather, name):
    n, n_s = len(lands), len(srcs)
    bufs = list(srcs) + list(lands)

    def body(*refs):
        src_refs, land_refs = refs[:n_s], refs[n_s:n_s + n]
        send_sems, recv_sems = refs[n_s + n], refs[n_s + n + 1]
        token = refs[-1]
        me = _Place()
        for a in range(n):
            for i, k in enumerate(ICI_PEERS):
                _ici_copy(gather, a, src_refs, land_refs, send_sems, recv_sems, i, me, _Place(k), me).start()
        token[...] = jnp.zeros_like(token)

    dma = pltpu.SemaphoreType.DMA((n * len(ICI_PEERS),))
    outs = pl.pallas_call(
        body, name=name, out_shape=(dma, dma, *[pltpu.HBM(v.shape, v.dtype) for v in bufs], _sds((8, LANES), F32)),
        in_specs=(HBM_SPEC,) * len(bufs),
        out_specs=(SEM_SPEC, SEM_SPEC) + (HBM_SPEC,) * len(bufs) + (pl.BlockSpec(memory_space=pltpu.VMEM),),
        input_output_aliases={j: 2 + j for j in range(len(bufs))}, compiler_params=pltpu.CompilerParams(has_side_effects=DATAFLOW),
    )(*[pltpu.with_memory_space_constraint(v, pltpu.HBM) for v in bufs])
    return outs[0], outs[1], list(outs[2:2 + n_s]), list(outs[2 + n_s:2 + n_s + n]), outs[-1]


def _ici_wait(flight, after, gather, name):
    send_sems, recv_sems, srcs, lands, _ = flight
    n, n_s = len(lands), len(srcs)
    bufs = srcs + lands

    def body(*refs):
        src_refs, land_refs = refs[:n_s], refs[n_s:n_s + n]
        s_sems, r_sems = refs[n_s + n], refs[n_s + n + 1]
        me = _Place()
        for a in range(n):
            for i, k in enumerate(ICI_PEERS):
                peer = _Place(k)
                cp = _ici_copy(gather, a, src_refs, land_refs, s_sems, r_sems, i, me, peer, peer)
                cp.wait_send()
                cp.wait_recv()

    outs = pl.pallas_call(
        body, name=name, out_shape=tuple(pltpu.HBM(v.shape, v.dtype) for v in bufs),
        in_specs=(HBM_SPEC,) * len(bufs) + (SEM_SPEC, SEM_SPEC) + (ANY,) * len(after), out_specs=(HBM_SPEC,) * len(bufs),
        input_output_aliases={j: j for j in range(len(bufs))}, compiler_params=pltpu.CompilerParams(has_side_effects=DATAFLOW),
    )(*bufs, send_sems, recv_sems, *after)
    return list(outs[n_s:])


def _own_shards(shards, after, name):
    n = len(shards)
    vmem = pl.BlockSpec(memory_space=pltpu.VMEM)

    def body(*refs):
        ins, outs, cast, sems = refs[:n], refs[n + 1:2 * n + 1], refs[2 * n + 1:3 * n + 1], refs[3 * n + 1]
        me = _Place()
        copies = []
        for a in range(n):
            cast[a][...] = ins[a][...].astype(BF16)
            copies.append(pltpu.make_async_copy(cast[a], outs[a].at[me.chip, me.c], sems.at[a]))
            copies[-1].start()
        for cp in copies:
            cp.wait()

    return list(pl.pallas_call(
        body, in_specs=[vmem] * n + [ANY], out_specs=[ANY] * n, out_shape=[_sds((4, 2) + s.shape, BF16) for s in shards],
        scratch_shapes=[pltpu.VMEM(s.shape, BF16) for s in shards] + [pltpu.SemaphoreType.DMA((n,))], name=name)(*shards, after))


def _col_pieces(widths):
    out, c = [], 0
    for k, w in widths:
        out.append((k, c, w))
        c += w
    return out


def _split_range(c0, n, bounds):
    parts, c = [], c0
    while c < c0 + n:
        r = max(i for i in range(len(bounds) - 1) if bounds[i] <= c)
        w = min(c0 + n, bounds[r + 1]) - c
        parts.append((r, c - bounds[r], w))
        c += w
    return parts


def _win_unpack(g, widths, name):
    n, R, C = g.shape
    tr = min(256, R)
    pieces = _col_pieces(widths)
    padded = [-(-w // LANES) * LANES for _, _, w in pieces]
    shard_bounds = [s * C for s in range(n + 1)]

    def body(g_ref, *o_refs):
        for (k, c0, w), o_ref in zip(pieces, o_refs):
            for t in range(0, o_ref.shape[1], LANES):
                valid = max(0, min(LANES, w - t))
                cols = [g_ref[s, :, o:o + ww] for s, o, ww in _split_range(c0 + t, valid, shard_bounds)] if valid else []
                if valid < LANES:
                    cols.append(jnp.zeros((tr, LANES - valid), g_ref.dtype))
                o_ref[:, t:t + LANES] = cols[0] if len(cols) == 1 else jnp.concatenate(cols, axis=1)

    return pl.pallas_call(
        body, grid=(R // tr,), in_specs=[pl.BlockSpec((n, tr, C), lambda i: (0, i, 0))],
        out_specs=[pl.BlockSpec((tr, p), lambda i: (i, 0)) for p in padded],
        out_shape=[_sds((R, p), g.dtype) for p in padded], name=name, compiler_params=_params("parallel"))(g)


def _win_pack(grads, widths, n, name):
    R = grads[0].shape[0]
    tr = min(256, R)
    pieces = _col_pieces(widths)
    total = pieces[-1][1] + pieces[-1][2]
    C = total // n
    bounds = [c0 for _, c0, _ in pieces] + [total]

    def body(*refs):
        g_refs, o_ref = refs[:-1], refs[-1]

        def tile_t(c0):
            cols = [g_refs[r][:, o:o + ww] for r, o, ww in _split_range(c0, LANES, bounds)]
            tile = cols[0] if len(cols) == 1 else jnp.concatenate(cols, axis=1)
            return tile.astype(F32).T

        for s in range(n):
            full = C // LANES * LANES
            for t in range(0, full, LANES):
                o_ref[s, t:t + LANES, :] = tile_t(s * C + t).astype(o_ref.dtype)
            if full < C:
                o_ref[s, full:C, :] = tile_t(s * C + C - LANES)[LANES - (C - full):, :].astype(o_ref.dtype)

    return pl.pallas_call(
        body, grid=(R // tr,), in_specs=[pl.BlockSpec((tr, gr.shape[1]), lambda i: (i, 0)) for gr in grads],
        out_specs=pl.BlockSpec((n, C, tr), lambda i: (0, 0, i)), out_shape=_sds((n, C, R), grads[0].dtype),
        name=name, compiler_params=_params("parallel"))(*grads)


def _gather_all(arrs):
    return _Comm(arrs, [_sds((N_DEV,) + a.shape, a.dtype) for a in arrs], ALL_PEERS,
                 src=lambda a, i, me, p: i, dst=lambda a, o, s: o.at[s.id], own=lambda a, i, o, me: (i, o.at[me.id]))


def _add_halves(parts, got, name):
    n, _, R, C = parts.shape
    br, bc, nb, at = _tiles_2d(R, C)
    place = jnp.stack([lax.axis_index("c"), 2 * lax.axis_index("x") + lax.axis_index("y")]).astype(jnp.int32)

    def body(q_ref, p_ref, g_ref, o_ref, land_ref):
        s = (p_ref[0, 0].astype(F32) + g_ref[0].astype(F32)).astype(o_ref.dtype)
        o_ref[0] = s

        @pl.when(pl.program_id(1) == q_ref[1])
        def _():
            land_ref[0] = s

    spec = pltpu.PrefetchScalarGridSpec(
        num_scalar_prefetch=1, grid=(nb, n),
        in_specs=[pl.BlockSpec((1, 1, br, bc), lambda i, q, q_ref: (q, q_ref[0]) + at(i)), pl.BlockSpec((1, br, bc), lambda i, q, q_ref: (q,) + at(i))],
        out_specs=[pl.BlockSpec((1, br, bc), lambda i, q, q_ref: (q,) + at(i)), pl.BlockSpec((1, br, bc), lambda i, q, q_ref: (q_ref[1],) + at(i))])
    return pl.pallas_call(body, grid_spec=spec, out_shape=[_sds((n, R, C), parts.dtype)] * 2, name=name,
                          compiler_params=_params("parallel", "arbitrary"))(place, parts, got)


def _adam(w, m, v, gparts, name, comm=None):
    R, C = w.shape
    n = gparts.shape[0]
    br, bc, nb, at = _tiles_2d(R, C, max_rows=128)
    c1 = 1.0 / (1.0 - ADAM_B1 ** ADAM_STEP)
    c2 = 1.0 / (1.0 - ADAM_B2 ** ADAM_STEP)

    def body(w_ref, m_ref, v_ref, g_ref, go_ref, d_ref, mo_ref, vo_ref):
        g = g_ref[0].astype(F32)
        for s in range(1, n):
            g = g + g_ref[s].astype(F32)
        mn = ADAM_B1 * m_ref[...] + (1.0 - ADAM_B1) * g
        vn = ADAM_B2 * v_ref[...] + (1.0 - ADAM_B2) * (g * g)
        go_ref[...] = g
        mo_ref[...] = mn
        vo_ref[...] = vn
        d_ref[...] = -ADAM_LR * ((mn * c1) / (jnp.sqrt(vn * c2) + ADAM_EPS) + ADAM_WD * w_ref[...])

    blk = pl.BlockSpec((br, bc), at)
    outs, carried = _call(
        body, grid=(nb,), in_specs=[blk, blk, blk, pl.BlockSpec((n, br, bc), lambda i: (0,) + at(i))],
        out_specs=[blk] * 4, out_shape=[_sds((R, C), F32)] * 4, args=[w, m, v, gparts], name=name, sem=("parallel",), comm=comm)
    return outs if comm is None else (outs, carried)


_SMALL_ORDER = ("norm_mix", "b_gate", "sc_conv_w", "ssm_conv_w", "ssm_conv_b", "dt_bias", "A_log", "D_skip", "ssm_norm_w",
                "norm_mlp", "norm_final", "loss")
_REPLICATED = ("norm_mix", "b_gate", "ssm_conv_b", "dt_bias", "A_log", "D_skip", "ssm_norm_w", "norm_mlp", "norm_final")


def _cols_to_slots(g, n):
    R = g.shape[0]
    return jnp.transpose(g.reshape(R, n, g.shape[1] // n), (1, 0, 2))


def _slots_to_cols(g):
    n, R, C = g.shape
    return jnp.transpose(g, (1, 0, 2)).reshape(R, n * C)


def kernel(x, norm_mix, w_in, b_gate, sc_conv_w, ssm_conv_w, ssm_conv_b, dt_bias, A_log, D_skip, ssm_norm_w, w_branch_sc, w_branch_ssm, w_out, norm_mlp, w_mlp1, w_mlp2, norm_final, loss_target, m_norm_mix, m_w_in, m_b_gate, m_sc_conv_w, m_ssm_conv_w, m_ssm_conv_b, m_dt_bias, m_A_log, m_D_skip, m_ssm_norm_w, m_w_branch_sc, m_w_branch_ssm, m_w_out, m_norm_mlp, m_w_mlp1, m_w_mlp2, m_norm_final, v_norm_mix, v_w_in, v_b_gate, v_sc_conv_w, v_ssm_conv_w, v_ssm_conv_b, v_dt_bias, v_A_log, v_D_skip, v_ssm_norm_w, v_w_branch_sc, v_w_branch_ssm, v_w_out, v_norm_mlp, v_w_mlp1, v_w_mlp2, v_norm_final):
    T, D = x.shape[1], x.shape[2]
    n_inner = 2 * D
    n_heads = n_inner // HEADDIM
    n_xbc = n_inner + 2 * NGROUPS * NSTATE
    me = 4 * lax.axis_index("x") + 2 * lax.axis_index("y") + lax.axis_index("c")

    in_cols = [("sc", 3 * D), ("z", n_inner), ("xbc", n_xbc), ("dt", n_heads), ("gate", 2 * D)]
    by_owner = lambda b: b.reshape((N_DEV,) + b.shape[2:])
    to_owner = lambda g: g.reshape((4, 2) + g.shape[1:])
    rows_of = lambda g: to_owner(g.reshape((N_DEV, g.shape[0] // N_DEV) + g.shape[1:]))
    cols_of = lambda g: to_owner(_cols_to_slots(g, N_DEV))

    class Schedule(_NoExchange):
        late = ("bssm", "bsc", "out", "w1", "w2")
        gather_sib = dict(gnorm_fwd=("bsc", "bssm", "out"), branch_ssm=("w1", "w2"))
        scatter_sib = dict(mlp_up_dx=("w2", "w1"), branch_ssm_dx=("out", "bssm", "bsc"))
        shards = dict(bsc=w_branch_sc, bssm=w_branch_ssm, out=w_out, w1=w_mlp1, w2=w_mlp2)

        def __init__(self):
            self.W, self.staged, self.grads, self.summed, self.scatters = {}, {}, {}, {}, []
            self.token = jnp.zeros((), F32)

        def first_weights(self, bufs):
            self.W.update(zip([k for k, _ in in_cols], _win_unpack(by_owner(bufs[0]), in_cols, "win_unpack")))
            self.W.update(sc_conv_w=_slots_to_cols(by_owner(bufs[1])), ssm_conv_w=_slots_to_cols(by_owner(bufs[2])))
            lands = _own_shards([self.shards[k] for k in self.late], bufs[1], "own_shards")
            self.gather_flight = _ici_start([], lands, True, "gather_late_start")
            self.token = self.gather_flight[4][0, 0]
            self.W["dt"] = self.W["dt"] + self.token.astype(BF16)

        def tok(self):
            return self.token

        def point(self, name, values):
            if name == "mixers_done":
                lands = _ici_wait(self.gather_flight, values, True, "gather_late_wait")
                self.staged.update(zip(self.late, lands))

        def carry(self, name):
            if name == "rms_mix":
                return _GatherBoth([w_in.astype(BF16), sc_conv_w, ssm_conv_w])
            if name in self.gather_sib:
                return _gather_sibling([self.staged.pop(k) for k in self.gather_sib[name]])
            if name in self.scatter_sib:
                return _scatter_sibling([self.grads[k] for k in self.scatter_sib[name]])
            return None

        def start_scatter(self, keys, halves_and_lands):
            halves, lands = [h for h, _ in halves_and_lands], [l for _, l in halves_and_lands]
            flight = _ici_start(halves, lands, False, "scatter_%s_start" % keys[0])
            self.scatters.append((keys, flight))
            self.token = flight[4][0, 0]

        def carried(self, name, outs):
            if name == "rms_mix":
                self.first_weights(outs)
            elif name in self.gather_sib:
                for k, b in zip(self.gather_sib[name], outs):
                    full = by_owner(b)
                    self.W[k] = _slots_to_cols(full) if k == "w1" else full.reshape(-1, D)
            else:
                keys = self.scatter_sib[name]
                self.start_scatter(keys, [_add_halves(self.grads[k], b, "add_halves_" + k) for k, b in zip(keys, outs)])

        def grad(self, k, g):
            if k == "win":
                g = to_owner(_win_pack([g[k] for k, _ in in_cols], in_cols, N_DEV, "win_pack"))
                got = _run_comm(_scatter_sibling([g]), "scatter_sibling_win")[0]
                self.start_scatter(("win",), [_add_halves(g, got, "add_halves_win")])
            else:
                self.grads[k] = cols_of(g) if k == "w1" else rows_of(g)

        def finish_scatter(self, after):
            keys, flight = self.scatters.pop(0)
            return dict(zip(keys, _ici_wait(flight, after, False, "scatter_%s_wait" % keys[0])))

    S = Schedule()
    small = dict(norm_mix=norm_mix, b_gate=b_gate, ssm_conv_b=ssm_conv_b, dt_bias=dt_bias, A_log=A_log, D_skip=D_skip,
                 ssm_norm_w=ssm_norm_w, norm_mlp=norm_mlp, norm_final=norm_final)
    grad_x, g_small = _local_step(x.reshape(T, D), loss_target.reshape(T, D), S, small)

    small_flat = jnp.concatenate([g_small[k].reshape(-1) for k in _SMALL_ORDER])
    n_small = small_flat.shape[0]
    rows = -(-n_small // (8 * LANES)) * 8
    small_pack = jnp.pad(small_flat, (0, rows * LANES - n_small)).reshape(rows, LANES)

    res = {}
    big = [("w_in", "win", w_in, m_w_in, v_w_in), ("w_branch_sc", "bsc", w_branch_sc, m_w_branch_sc, v_w_branch_sc),
           ("w_branch_ssm", "bssm", w_branch_ssm, m_w_branch_ssm, v_w_branch_ssm), ("w_out", "out", w_out, m_w_out, v_w_out),
           ("w_mlp1", "w1", w_mlp1, m_w_mlp1, v_w_mlp1), ("w_mlp2", "w2", w_mlp2, m_w_mlp2, v_w_mlp2)]
    by_grad = {gk: (k, w, m, v) for k, gk, w, m, v in big}
    after = [grad_x]
    while S.scatters:
        for gk, parts in S.finish_scatter(after).items():
            k, w, m, v = by_grad[gk]
            if gk == "win":
                res_t, (small_parts,) = _adam(w.T, m.T, v.T, parts, "adam_" + k, comm=_gather_all([small_pack]))
                res[k] = [r.T for r in res_t]
            else:
                res[k] = _adam(w, m, v, parts, "adam_" + k)
            after = after + [res[k][1]]

    sizes = {k: g_small[k].size for k in _SMALL_ORDER}
    offs, o = {}, 0
    for k in _SMALL_ORDER:
        offs[k] = o
        o += sizes[k]
    rep_w = dict(norm_mix=norm_mix, b_gate=b_gate, ssm_conv_b=ssm_conv_b, dt_bias=dt_bias, A_log=A_log, D_skip=D_skip,
                 ssm_norm_w=ssm_norm_w, norm_mlp=norm_mlp, norm_final=norm_final)
    rep_m = dict(norm_mix=m_norm_mix, b_gate=m_b_gate, ssm_conv_b=m_ssm_conv_b, dt_bias=m_dt_bias, A_log=m_A_log, D_skip=m_D_skip,
                 ssm_norm_w=m_ssm_norm_w, norm_mlp=m_norm_mlp, norm_final=m_norm_final)
    rep_v = dict(norm_mix=v_norm_mix, b_gate=v_b_gate, ssm_conv_b=v_ssm_conv_b, dt_bias=v_dt_bias, A_log=v_A_log, D_skip=v_D_skip,
                 ssm_norm_w=v_ssm_norm_w, norm_mlp=v_norm_mlp, norm_final=v_norm_final)

    def pack(d):
        segs = [jnp.pad(d[k].astype(F32).reshape(-1), (0, sizes[k] - d[k].size)) if k in d else jnp.zeros((sizes[k],), F32)
                for k in _SMALL_ORDER]
        return jnp.pad(jnp.concatenate(segs), (0, rows * LANES - n_small)).reshape(rows, LANES)

    sm = _adam(pack(rep_w), pack(rep_m), pack(rep_v), small_parts, "adam_small")
    sm = [s.reshape(-1) for s in sm]
    for k in _REPLICATED:
        n_k = rep_w[k].shape[0]
        res[k] = tuple(s[offs[k]:offs[k] + n_k] for s in sm)
    loss = sm[0][offs["loss"]]
    for k, w, m, v, K, full in (("sc_conv_w", sc_conv_w, m_sc_conv_w, v_sc_conv_w, SC_K, D),
                                ("ssm_conv_w", ssm_conv_w, m_ssm_conv_w, v_ssm_conv_w, SSM_K, n_xbc)):
        g_full = sm[0][offs[k]:offs[k] + K * full].reshape(K, full)
        cw = full // N_DEV
        g_mine = lax.dynamic_slice_in_dim(g_full, me * cw, cw, axis=1)
        res[k] = _adam(w, m, v, g_mine[None], "adam_" + k)

    order = ("norm_mix", "w_in", "b_gate", "sc_conv_w", "ssm_conv_w", "ssm_conv_b", "dt_bias", "A_log", "D_skip", "ssm_norm_w",
             "w_branch_sc", "w_branch_ssm", "w_out", "norm_mlp", "w_mlp1", "w_mlp2", "norm_final")
    outs = [loss, grad_x.reshape(1, T, D)]
    for j in range(4):
        outs += [res[k][j] for k in order]
    return tuple(outs)
```

```python
import jax
import jax.numpy as jnp
from jax import lax
from jax.experimental import pallas as pl
from jax.experimental.pallas import tpu as pltpu

F32 = jnp.float32
BF16 = jnp.bfloat16

EPS = 1e-6
N_DEV = 8
HEADDIM = 64
NSTATE = 128
CHUNK = 128
NGROUPS = 8
GROUP_W = 256
SC_K = 3
SSM_K = 4
LANES = 128

ADAM_LR = 0.001
ADAM_B1 = 0.9
ADAM_B2 = 0.999
ADAM_EPS = 1e-08
ADAM_WD = 0.01
ADAM_STEP = 10

NN = (((1,), (0,)), ((), ()))
NT = (((1,), (1,)), ((), ()))
TN = (((0,), (0,)), ((), ()))
_DIMS = {"nn": NN, "nt": NT, "tn": TN}

ANY = pl.BlockSpec(memory_space=pl.ANY)
MESH = pl.DeviceIdType.MESH


def _sds(shape, dtype):
    return jax.ShapeDtypeStruct(tuple(shape), dtype)


def _dot(a, b, dims=NN):
    return lax.dot_general(a, b, dims, preferred_element_type=F32)


def _dot3(a, b, dims=NN):
    return lax.dot_general(a, b, dims, preferred_element_type=F32, precision=lax.Precision.HIGH)


def _params(*sem):
    return pltpu.CompilerParams(dimension_semantics=tuple(sem))


def _call(body, *, grid, in_specs, out_specs, out_shape, args, name, sem, scratch=(), comm=None):
    if comm is None:
        outs = pl.pallas_call(body, grid=grid, in_specs=list(in_specs), out_specs=list(out_specs), out_shape=list(out_shape),
                              scratch_shapes=list(scratch), name=name, compiler_params=_params(*sem))(*args)
        return list(outs), None
    n, n_in, n_out, n_scr = comm.n, len(in_specs), len(out_shape), len(scratch)

    def wrapped(*refs):
        ins, c_in = refs[:n_in], refs[n_in:n_in + n]
        outs, c_out = refs[n_in + n:n_in + n + n_out], refs[n_in + n + n_out:n_in + 2 * n + n_out]
        rest = refs[n_in + 2 * n + n_out:]
        scr, sems = rest[:n_scr], rest[n_scr:]
        first, last = None, None
        for d, g in enumerate(grid):
            f, l = pl.program_id(d) == 0, pl.program_id(d) == g - 1
            first, last = (f, l) if first is None else (first & f, last & l)

        @pl.when(first)
        def _():
            comm.start(c_in, c_out, sems)

        body(*ins, *outs, *scr)

        @pl.when(last)
        def _():
            comm.finish(c_in, c_out, sems)

    outs = pl.pallas_call(
        wrapped, grid=grid, in_specs=list(in_specs) + [ANY] * n, out_specs=list(out_specs) + [ANY] * n,
        out_shape=list(out_shape) + comm.out_shape, scratch_shapes=list(scratch) + comm.scratch,
        input_output_aliases={n_in + i: n_out + o for i, o in comm.aliases.items()},
        name=name, compiler_params=_params(*["arbitrary"] * len(grid)))(*args, *comm.arrs)
    return list(outs[:n_out]), list(outs[n_out:])


MM_VMEM_BUDGET = 44 * 2 ** 20


def _mm_tiles(M, N, k_bytes, mn_bytes):
    best = None
    for tm in (2048, 1024, 512, 256, 128):
        for tn in (1024, 512, 256, 128):
            if M % tm or N % tn:
                continue
            need = 2 * ((tm + tn) * k_bytes + tm * tn * mn_bytes) + 4 * tm * tn * 4
            if need <= MM_VMEM_BUDGET and (best is None or (tm * tn, tm) > (best[0] * best[1], best[0])):
                best = (tm, tn)
    assert best is not None, (M, N, k_bytes, mn_bytes)
    return best


def _mm(a, b, *, mode, name, extras=(), epi=None, out_dtypes=(F32,), comm=None):
    a_list = list(a) if isinstance(a, (list, tuple)) else [a]
    b_list = list(b) if isinstance(b, (list, tuple)) else [b]
    if mode == "nn":
        M, N = a_list[0].shape[0], b_list[0].shape[1]
    elif mode == "nt":
        M, N = a_list[0].shape[0], b_list[0].shape[0]
    else:
        M, N = a_list[0].shape[1], b_list[0].shape[1]
    k_bytes = sum((av.shape[0] if mode == "tn" else av.shape[1]) * av.dtype.itemsize for av in a_list)
    mn_bytes = sum(e.dtype.itemsize for e in extras) + sum(jnp.dtype(d).itemsize for d in out_dtypes)
    tm, tn = _mm_tiles(min(M, 2048), min(N, 1024), k_bytes, mn_bytes) if M % 128 == 0 and N % 128 == 0 else (M, N)
    assert M % tm == 0 and N % tn == 0
    a_specs, b_specs = [], []
    for av, bv in zip(a_list, b_list):
        K = av.shape[0] if mode == "tn" else av.shape[1]
        a_specs.append(pl.BlockSpec((K, tm), lambda i, j: (0, i)) if mode == "tn" else pl.BlockSpec((tm, K), lambda i, j: (i, 0)))
        b_specs.append(pl.BlockSpec((tn, K), lambda i, j: (j, 0)) if mode == "nt" else pl.BlockSpec((K, tn), lambda i, j: (0, j)))
    mn_spec = pl.BlockSpec((tm, tn), lambda i, j: (i, j))
    n_p, n_ex = len(a_list), len(extras)
    dims = _DIMS[mode]

    def body(*refs):
        acc = _dot(refs[0][...], refs[n_p][...], dims)
        for p in range(1, n_p):
            acc = acc + _dot(refs[p][...], refs[n_p + p][...], dims)
        rest = refs[2 * n_p:]
        res = (acc,) if epi is None else epi(acc, *[r[...] for r in rest[:n_ex]])
        for o_ref, r in zip(rest[n_ex:], res):
            o_ref[...] = r.astype(o_ref.dtype)

    outs, carried = _call(
        body, grid=(M // tm, N // tn), in_specs=a_specs + b_specs + [mn_spec] * n_ex,
        out_specs=[mn_spec] * len(out_dtypes), out_shape=[_sds((M, N), d) for d in out_dtypes],
        args=a_list + b_list + list(extras), name=name, sem=("parallel", "parallel"), comm=comm)
    res = outs[0] if len(outs) == 1 else outs
    return res if comm is None else (res, carried)


def _epi_add(acc, r):
    return (acc + r,)


def _epi_relu2(acc):
    p = jnp.maximum(acc, 0.0)
    return (p * p,)


def _epi_relu2_bwd(acc, r):
    return (acc * (2.0 * jnp.sqrt(r.astype(F32))),)


def _row(tr, n):
    return pl.BlockSpec((tr, n), lambda i: (i, 0))


def _vec(n):
    return pl.BlockSpec((1, n), lambda i: (0, 0))


def _rms_fwd(x, w, name, comm=None):
    T, D = x.shape
    tr = min(256, T)

    def body(x_ref, w_ref, o_ref):
        xv = x_ref[...]
        r = lax.rsqrt(jnp.mean(xv * xv, axis=-1, keepdims=True) + EPS)
        o_ref[...] = (xv * r * w_ref[...]).astype(BF16)

    outs, carried = _call(body, grid=(T // tr,), in_specs=[_row(tr, D), _vec(D)], out_specs=[_row(tr, D)],
                          out_shape=[_sds((T, D), BF16)], args=[x, w], name=name, sem=("parallel",), comm=comm)
    return outs[0] if comm is None else (outs[0], carried)


def _rms_bwd(x, w, dh, dres, name):
    T, D = x.shape
    tr = min(256, T)

    def body(x_ref, w_ref, dh_ref, dres_ref, dx_ref, dxb_ref, dw_ref):
        @pl.when(pl.program_id(0) == 0)
        def _():
            dw_ref[...] = jnp.zeros_like(dw_ref)

        xv = x_ref[...]
        r = lax.rsqrt(jnp.mean(xv * xv, axis=-1, keepdims=True) + EPS)
        xh = xv * r
        dh_v = dh_ref[...]
        dw_ref[...] += jnp.sum(dh_v * xh, axis=0, keepdims=True)
        dxh = dh_v * w_ref[...]
        dx = r * (dxh - xh * jnp.mean(dxh * xh, axis=-1, keepdims=True)) + dres_ref[...]
        dx_ref[...] = dx
        dxb_ref[...] = dx.astype(BF16)

    return pl.pallas_call(
        body, grid=(T // tr,), in_specs=[_row(tr, D), _vec(D), _row(tr, D), _row(tr, D)],
        out_specs=[_row(tr, D), _row(tr, D), _vec(D)],
        out_shape=[_sds((T, D), F32), _sds((T, D), BF16), _sds((1, D), F32)],
        name=name, compiler_params=_params("arbitrary"))(x, w, dh, dres)


def _final(x2, w, tgt, name):
    T, D = x2.shape
    tr = min(256, T)

    def body(x_ref, w_ref, t_ref, dx_ref, dxb_ref, dw_ref, loss_ref):
        @pl.when(pl.program_id(0) == 0)
        def _():
            dw_ref[...] = jnp.zeros_like(dw_ref)
            loss_ref[...] = jnp.zeros_like(loss_ref)

        xv = x_ref[...]
        wv = w_ref[...]
        r = lax.rsqrt(jnp.mean(xv * xv, axis=-1, keepdims=True) + EPS)
        xh = xv * r
        err = xh * wv - t_ref[...]
        part = jnp.sum(jnp.sum(err * err, axis=1, keepdims=True), axis=0, keepdims=True) * (0.5 / D)
        loss_ref[...] += jnp.broadcast_to(part, loss_ref.shape)
        dy = err * (1.0 / D)
        dw_ref[...] += jnp.sum(dy * xh, axis=0, keepdims=True)
        dxh = dy * wv
        dx = r * (dxh - xh * jnp.mean(dxh * xh, axis=-1, keepdims=True))
        dx_ref[...] = dx
        dxb_ref[...] = dx.astype(BF16)

    return pl.pallas_call(
        body, grid=(T // tr,), in_specs=[_row(tr, D), _vec(D), _row(tr, D)],
        out_specs=[_row(tr, D), _row(tr, D), _vec(D), _vec(LANES)],
        out_shape=[_sds((T, D), F32), _sds((T, D), BF16), _sds((1, D), F32), _sds((1, LANES), F32)],
        name=name, compiler_params=_params("arbitrary"))(x2, w, tgt)


def _silu_parts(z):
    s = jax.nn.sigmoid(z)
    return z * s, s * (1.0 + z * (1.0 - s))


def _gnorm_fwd(y, z, w, name, comm=None):
    T, N = y.shape
    tr = min(256, T)

    def body(y_ref, z_ref, w_ref, o_ref):
        for g in range(N // GROUP_W):
            sl = slice(g * GROUP_W, (g + 1) * GROUP_W)
            silu, _ = _silu_parts(z_ref[:, sl])
            yz = y_ref[:, sl] * silu
            r = lax.rsqrt(jnp.mean(yz * yz, axis=-1, keepdims=True) + EPS)
            o_ref[:, sl] = (yz * r * w_ref[:, sl]).astype(BF16)

    outs, carried = _call(body, grid=(T // tr,), in_specs=[_row(tr, N), _row(tr, N), _vec(N)], out_specs=[_row(tr, N)],
                          out_shape=[_sds((T, N), BF16)], args=[y, z, w], name=name, sem=("parallel",), comm=comm)
    return outs[0] if comm is None else (outs[0], carried)


def _gnorm_bwd(y, z, w, dyb, name):
    T, N = y.shape
    tr = min(256, T)

    def body(y_ref, z_ref, w_ref, d_ref, dy_ref, dz_ref, dw_ref):
        @pl.when(pl.program_id(0) == 0)
        def _():
            dw_ref[...] = jnp.zeros_like(dw_ref)

        for g in range(N // GROUP_W):
            sl = slice(g * GROUP_W, (g + 1) * GROUP_W)
            yv = y_ref[:, sl]
            silu, dsilu = _silu_parts(z_ref[:, sl])
            yz = yv * silu
            r = lax.rsqrt(jnp.mean(yz * yz, axis=-1, keepdims=True) + EPS)
            yzh = yz * r
            d = d_ref[:, sl]
            dw_ref[:, sl] += jnp.sum(d * yzh, axis=0, keepdims=True)
            dyzh = d * w_ref[:, sl]
            dyz = r * (dyzh - yzh * jnp.mean(dyzh * yzh, axis=-1, keepdims=True))
            dy_ref[:, sl] = dyz * silu
            dz_ref[:, sl] = (dyz * yv * dsilu).astype(BF16)

    return pl.pallas_call(
        body, grid=(T // tr,), in_specs=[_row(tr, N), _row(tr, N), _vec(N), _row(tr, N)],
        out_specs=[_row(tr, N), _row(tr, N), _vec(N)],
        out_shape=[_sds((T, N), F32), _sds((T, N), BF16), _sds((1, N), F32)],
        name=name, compiler_params=_params("arbitrary"))(y, z, w, dyb)


def _merge_fwd(gate_raw, b_gate, br_a, br_b, name):
    T, D = br_a.shape
    tr = min(256, T)

    def body(g_ref, bg_ref, a_ref, b_ref, o_ref):
        g = jax.nn.sigmoid(g_ref[...] + bg_ref[...])
        o_ref[...] = (g[:, :D] * a_ref[...] + g[:, D:] * b_ref[...]).astype(BF16)

    return pl.pallas_call(body, grid=(T // tr,), in_specs=[_row(tr, 2 * D), _vec(2 * D), _row(tr, D), _row(tr, D)],
                          out_specs=_row(tr, D), out_shape=_sds((T, D), BF16), name=name,
                          compiler_params=_params("parallel"))(gate_raw, b_gate, br_a, br_b)


def _merge_bwd(dmerged, gate_raw, b_gate, br_a, br_b, name):
    T, D = br_a.shape
    tr = min(256, T)

    def body(d_ref, g_ref, bg_ref, a_ref, b_ref, da_ref, db_ref, dg_ref, dbg_ref):
        @pl.when(pl.program_id(0) == 0)
        def _():
            dbg_ref[...] = jnp.zeros_like(dbg_ref)

        g = jax.nn.sigmoid(g_ref[...] + bg_ref[...])
        d = d_ref[...]
        da_ref[...] = (d * g[:, :D]).astype(BF16)
        db_ref[...] = (d * g[:, D:]).astype(BF16)
        dg = jnp.concatenate([d * a_ref[...], d * b_ref[...]], axis=1) * g * (1.0 - g)
        dg_ref[...] = dg.astype(BF16)
        dbg_ref[...] += jnp.sum(dg, axis=0, keepdims=True)

    return pl.pallas_call(
        body, grid=(T // tr,), in_specs=[_row(tr, D), _row(tr, 2 * D), _vec(2 * D), _row(tr, D), _row(tr, D)],
        out_specs=[_row(tr, D), _row(tr, D), _row(tr, 2 * D), _vec(2 * D)],
        out_shape=[_sds((T, D), BF16), _sds((T, D), BF16), _sds((T, 2 * D), BF16), _sds((1, 2 * D), F32)],
        name=name, compiler_params=_params("arbitrary"))(dmerged, gate_raw, b_gate, br_a, br_b)


CB_W = 256
CONV_ROWS = 32
CONV_PAD = 8


def _rows_down(load, r0, s):
    if s == 0:
        return load(r0, r0 + CONV_ROWS)
    if r0 == 0:
        row = lax.broadcasted_iota(jnp.int32, (CONV_ROWS, CB_W), 0)
        return jnp.where(row >= s, pltpu.roll(load(0, CONV_ROWS), s, 0), 0.0)
    return load(r0 - s, r0 - s + CONV_ROWS)


def _conv_tile(load, taps, r0):
    K = len(taps)
    us = [_rows_down(load, r0, K - 1 - k) for k in range(K)]
    acc = us[K - 1] * taps[K - 1]
    for k in range(K - 1):
        acc = acc + us[k] * taps[k]
    return acc, us


def _conv_back_tile(scr, taps, r0):
    K = len(taps)
    du = scr[r0:r0 + CONV_ROWS, :] * taps[K - 1]
    for k in range(K - 1):
        s = K - 1 - k
        du = du + scr[r0 + s:r0 + s + CONV_ROWS, :] * taps[k]
    return du


def _fold8(v):
    return jnp.sum(v.reshape(CONV_ROWS // 8, 8, v.shape[1]), axis=0)


def _col(T, j0=0):
    return pl.BlockSpec((T, CB_W), lambda j: (0, j + j0))


def _sc_fwd(psc, w, name):
    T, D = psc.shape[0], psc.shape[1] // 3
    nb = D // CB_W

    def body(b_ref, c_ref, x_ref, w_ref, o_ref):
        taps = [w_ref[k:k + 1, :] for k in range(SC_K)]
        load = lambda a, b: c_ref[a:b, :] * x_ref[a:b, :]
        for r0 in range(0, T, CONV_ROWS):
            cu, _ = _conv_tile(load, taps, r0)
            o_ref[r0:r0 + CONV_ROWS, :] = (b_ref[r0:r0 + CONV_ROWS, :] * cu).astype(BF16)

    return pl.pallas_call(
        body, grid=(nb,), in_specs=[_col(T), _col(T, nb), _col(T, 2 * nb), pl.BlockSpec((SC_K, CB_W), lambda j: (0, j))],
        out_specs=_col(T), out_shape=_sds((T, D), BF16), name=name, compiler_params=_params("parallel"))(psc, psc, psc, w)


def _sc_bwd(psc, w, dya, name):
    T, D = psc.shape[0], psc.shape[1] // 3
    nb = D // CB_W

    def body(b_ref, c_ref, x_ref, w_ref, d_ref, db_ref, dc_ref, dx_ref, dw_ref, scr):
        taps = [w_ref[k:k + 1, :] for k in range(SC_K)]
        load = lambda a, b: c_ref[a:b, :] * x_ref[a:b, :]
        scr[T:T + CONV_PAD, :] = jnp.zeros((CONV_PAD, CB_W), F32)
        dw8 = [jnp.zeros((8, CB_W), F32)] * SC_K
        for r0 in range(0, T, CONV_ROWS):
            rows = slice(r0, r0 + CONV_ROWS)
            cu, us = _conv_tile(load, taps, r0)
            d = d_ref[rows, :]
            db_ref[rows, :] = (d * cu).astype(BF16)
            dcu = d * b_ref[rows, :]
            scr[rows, :] = dcu
            dw8 = [acc + _fold8(dcu * u) for acc, u in zip(dw8, us)]
        for k in range(SC_K):
            dw_ref[k:k + 1, :] = jnp.sum(dw8[k], axis=0, keepdims=True)
        for r0 in range(0, T, CONV_ROWS):
            rows = slice(r0, r0 + CONV_ROWS)
            du = _conv_back_tile(scr, taps, r0)
            dc_ref[rows, :] = (du * x_ref[rows, :]).astype(BF16)
            dx_ref[rows, :] = (du * c_ref[rows, :]).astype(BF16)

    wspec = pl.BlockSpec((SC_K, CB_W), lambda j: (0, j))
    return pl.pallas_call(
        body, grid=(nb,), in_specs=[_col(T), _col(T, nb), _col(T, 2 * nb), wspec, _col(T)],
        out_specs=[_col(T), _col(T), _col(T), wspec],
        out_shape=[_sds((T, D), BF16)] * 3 + [_sds((SC_K, D), F32)],
        scratch_shapes=[pltpu.VMEM((T + CONV_PAD, CB_W), F32)],
        name=name, compiler_params=_params("parallel"))(psc, psc, psc, w, dya)


def _ssm_conv_fwd(u, w, b, name, comm=None):
    T, N = u.shape

    def body(u_ref, w_ref, b_ref, o_ref):
        taps = [w_ref[k:k + 1, :] for k in range(SSM_K)]
        bias = b_ref[...]
        for r0 in range(0, T, CONV_ROWS):
            c, _ = _conv_tile(lambda a, b: u_ref[a:b, :], taps, r0)
            c = c + bias
            o_ref[r0:r0 + CONV_ROWS, :] = c * jax.nn.sigmoid(c)

    outs, carried = _call(
        body, grid=(N // CB_W,), in_specs=[_col(T), pl.BlockSpec((SSM_K, CB_W), lambda j: (0, j)), pl.BlockSpec((1, CB_W), lambda j: (0, j))],
        out_specs=[_col(T)], out_shape=[_sds((T, N), F32)], args=[u, w, b], name=name, sem=("parallel",), comm=comm)
    return outs[0] if comm is None else (outs[0], carried)


def _ssm_conv_bwd(u, w, b, dxs, dB, dC, name, comm=None):
    T, N = u.shape
    n_x, n_b = dxs.shape[1] // CB_W, dB.shape[1] // CB_W

    def body(u_ref, w_ref, b_ref, dx_ref, db_ref, dc_ref, du_ref, dw_ref, dbias_ref, scr):
        j = pl.program_id(0)
        taps = [w_ref[k:k + 1, :] for k in range(SSM_K)]
        bias = b_ref[...]
        scr[T:T + CONV_PAD, :] = jnp.zeros((CONV_PAD, CB_W), F32)
        dw8 = [jnp.zeros((8, CB_W), F32)] * SSM_K
        db8 = jnp.zeros((8, CB_W), F32)
        for r0 in range(0, T, CONV_ROWS):
            rows = slice(r0, r0 + CONV_ROWS)
            c, us = _conv_tile(lambda a, b: u_ref[a:b, :], taps, r0)
            _, dsilu = _silu_parts(c + bias)
            d = jnp.where(j < n_x, dx_ref[rows, :], jnp.where(j < n_x + n_b, db_ref[rows, :], dc_ref[rows, :])) * dsilu
            scr[rows, :] = d
            db8 = db8 + _fold8(d)
            dw8 = [acc + _fold8(d * u) for acc, u in zip(dw8, us)]
        dbias_ref[...] = jnp.sum(db8, axis=0, keepdims=True)
        for k in range(SSM_K):
            dw_ref[k:k + 1, :] = jnp.sum(dw8[k], axis=0, keepdims=True)
        for r0 in range(0, T, CONV_ROWS):
            du_ref[r0:r0 + CONV_ROWS, :] = _conv_back_tile(scr, taps, r0).astype(BF16)

    wspec = pl.BlockSpec((SSM_K, CB_W), lambda j: (0, j))
    bspec = pl.BlockSpec((1, CB_W), lambda j: (0, j))
    outs, carried = _call(
        body, grid=(N // CB_W,),
        in_specs=[_col(T), wspec, bspec,
                  pl.BlockSpec((T, CB_W), lambda j: (0, jnp.minimum(j, n_x - 1))),
                  pl.BlockSpec((T, CB_W), lambda j: (0, jnp.clip(j - n_x, 0, n_b - 1))),
                  pl.BlockSpec((T, CB_W), lambda j: (0, jnp.clip(j - n_x - n_b, 0, n_b - 1)))],
        out_specs=[_col(T), wspec, bspec],
        out_shape=[_sds((T, N), BF16), _sds((SSM_K, N), F32), _sds((1, N), F32)],
        scratch=[pltpu.VMEM((T + CONV_PAD, CB_W), F32)],
        args=[u, w, b, dxs, dB, dC], name=name, sem=("parallel",), comm=comm)
    return outs if comm is None else (outs, carried)


def _split3(v):
    hi = v.astype(BF16)
    r = v - hi.astype(F32)
    mid = r.astype(BF16)
    lo = (r - mid.astype(F32)).astype(BF16)
    return hi, mid, lo


def _head_expand(n_lanes):
    h = lax.broadcasted_iota(jnp.int32, (LANES, n_lanes), 0)
    l = lax.broadcasted_iota(jnp.int32, (LANES, n_lanes), 1)
    return (jnp.right_shift(l, HEADDIM.bit_length() - 1) == h).astype(BF16)


def _softplus(v):
    return jnp.maximum(v, 0.0) + jnp.log1p(jnp.exp(-jnp.abs(v)))


def _ssd_prep(dt_raw, dt_bias, a_log, n_inner, name):
    T = dt_raw.shape[0]

    def body(r_ref, b_ref, al_ref, ex_ref, dt_ref, cs_ref):
        dt = _softplus(r_ref[...] + b_ref[...])
        a = dt * (-jnp.exp(al_ref[...]))
        i = lax.broadcasted_iota(jnp.int32, (CHUNK, CHUNK), 0)
        j = lax.broadcasted_iota(jnp.int32, (CHUNK, CHUNK), 1)
        tri = (j <= i).astype(BF16)
        cs = sum(_dot(tri, p) for p in _split3(a))
        ex = ex_ref[...]
        dt_ref[...] = sum(_dot(p, ex) for p in _split3(dt))
        cs_ref[...] = sum(_dot(p, ex) for p in _split3(cs))

    blk = pl.BlockSpec((CHUNK, LANES), lambda c: (c, 0))
    out = pl.BlockSpec((CHUNK, n_inner), lambda c: (c, 0))
    ex_spec = pl.BlockSpec((LANES, n_inner), lambda c: (0, 0))
    return pl.pallas_call(body, grid=(T // CHUNK,), in_specs=[blk, _vec(LANES), _vec(LANES), ex_spec], out_specs=[out, out],
                          out_shape=[_sds((T, n_inner), F32)] * 2, name=name,
                          compiler_params=_params("parallel"))(dt_raw, dt_bias, a_log, _head_expand(n_inner))


def _pair_terms(cs_p):
    lane = lax.broadcasted_iota(jnp.int32, (CHUNK, CHUNK), 1)
    sub = lax.broadcasted_iota(jnp.int32, (CHUNK, CHUNK), 0)
    csT = cs_p.T
    Ls = []
    for k in range(2):
        col = jnp.sum(jnp.where(lane == k * HEADDIM, cs_p, 0.0), axis=1, keepdims=True)
        rowv = csT[k * HEADDIM:k * HEADDIM + 1, :]
        Ls.append(jnp.exp(jnp.where(sub >= lane, col - rowv, -jnp.inf)))
    return Ls, jnp.exp(csT[:, CHUNK - 1:CHUNK])


def _block_diag(xp):
    lane = lax.broadcasted_iota(jnp.int32, xp.shape, 1)
    return jnp.concatenate([jnp.where(lane < HEADDIM, xp, 0.0), jnp.where(lane >= HEADDIM, xp, 0.0)], axis=0)


SSD_GROUPS_PER_STEP = 8


def _ssd_specs(T, n_inner):
    nc, gs = T // CHUNK, SSD_GROUPS_PER_STEP
    bo, co = n_inner // (gs * NSTATE), (n_inner + NGROUPS * NSTATE) // (gs * NSTATE)
    assert NGROUPS % gs == 0 and n_inner % (gs * NSTATE) == 0 and (NGROUPS * NSTATE) % (gs * NSTATE) == 0
    g_blk = lambda f: pl.BlockSpec((CHUNK, gs * GROUP_W), lambda c, s: (f(c), s))
    b_blk = lambda f: pl.BlockSpec((CHUNK, gs * NSTATE), lambda c, s: (f(c), bo + s))
    c_blk = lambda f: pl.BlockSpec((CHUNK, gs * NSTATE), lambda c, s: (f(c), co + s))
    return nc, g_blk, b_blk, c_blk


def _ssd_fwd(xbc, dt_e, cs_e, d_e, name, comm=None):
    T = xbc.shape[0]
    n_inner = dt_e.shape[1]
    nc, g_blk, b_blk, c_blk = _ssd_specs(T, n_inner)
    ident = lambda c: c

    gs = SSD_GROUPS_PER_STEP

    def body(xs_ref, b_ref, c_ref, dt_ref, cs_ref, d_ref, y_ref, p_ref, st):
        c, s = pl.program_id(0), pl.program_id(1)

        @pl.when(c == 0)
        def _():
            for gi in range(gs):
                st[s * gs + gi] = jnp.zeros((GROUP_W, NSTATE), F32)

        for gi in range(gs):
            g = s * gs + gi
            gw, gn = slice(gi * GROUP_W, (gi + 1) * GROUP_W), slice(gi * NSTATE, (gi + 1) * NSTATE)
            P = st[g]
            p_ref[0, gi] = P
            xs, dt, cs = xs_ref[:, gw], dt_ref[:, gw], cs_ref[:, gw]
            Bf, Cf = b_ref[:, gn], c_ref[:, gn]
            Cb = Cf.astype(BF16)
            CBm = _dot(Cb, Bf.astype(BF16), NT)
            X = xs * dt
            decay = jnp.exp(cs[CHUNK - 1:CHUNK, :] - cs)
            y_off = _dot(Cb, P.astype(BF16), NT) * jnp.exp(cs)
            ys, ecl = [], []
            for pr in range(2):
                sl = slice(pr * LANES, (pr + 1) * LANES)
                Ls, e_last = _pair_terms(cs[:, sl])
                ecl.append(e_last)
                Mcat = jnp.concatenate([(CBm * L).astype(BF16) for L in Ls], axis=1)
                ys.append(_dot(Mcat, _block_diag(X[:, sl]).astype(BF16)))
            y_ref[:, gw] = jnp.concatenate(ys, axis=1) + y_off + xs * d_ref[:, gw]
            S = _dot3(X * decay, Bf, TN)
            st[g] = P * jnp.concatenate(ecl, axis=0) + S

    p_blk = pl.BlockSpec((1, gs, GROUP_W, NSTATE), lambda c, s: (c, s, 0, 0))
    outs, carried = _call(
        body, grid=(nc, NGROUPS // gs),
        in_specs=[g_blk(ident), b_blk(ident), c_blk(ident), g_blk(ident), g_blk(ident), pl.BlockSpec((1, gs * GROUP_W), lambda c, s: (0, s))],
        out_specs=[g_blk(ident), p_blk],
        out_shape=[_sds((T, n_inner), F32), _sds((nc, NGROUPS, GROUP_W, NSTATE), F32)],
        scratch=[pltpu.VMEM((NGROUPS, GROUP_W, NSTATE), F32)],
        args=[xbc, xbc, xbc, dt_e, cs_e, d_e], name=name, sem=("arbitrary", "arbitrary"), comm=comm)
    return outs if comm is None else (outs, carried)


def _ssd_bwd(xbc, dt_e, cs_e, d_e, states, dy, name, comm=None):
    T = xbc.shape[0]
    n_inner = dt_e.shape[1]
    nc, g_blk, b_blk, c_blk = _ssd_specs(T, n_inner)
    rev = lambda c: nc - 1 - c

    gs = SSD_GROUPS_PER_STEP

    def body(xs_ref, b_ref, c_ref, dt_ref, cs_ref, d_ref, p_ref, pn_ref, dy_ref,
             dxs_ref, db_ref, dc_ref, ddt_ref, dcs_ref, dd_ref, dst):
        cc, s = pl.program_id(0), pl.program_id(1)

        @pl.when(cc == 0)
        def _():
            for gi in range(gs):
                dst[s * gs + gi] = jnp.zeros((GROUP_W, NSTATE), F32)

        for gi in range(gs):
            one_group(s * gs + gi, gi, xs_ref, b_ref, c_ref, dt_ref, cs_ref, d_ref, p_ref, pn_ref, dy_ref,
                      dxs_ref, db_ref, dc_ref, ddt_ref, dcs_ref, dd_ref, dst)

    def one_group(g, gi, xs_ref, b_ref, c_ref, dt_ref, cs_ref, d_ref, p_ref, pn_ref, dy_ref,
                  dxs_ref, db_ref, dc_ref, ddt_ref, dcs_ref, dd_ref, dst):
        gw, gn = slice(gi * GROUP_W, (gi + 1) * GROUP_W), slice(gi * NSTATE, (gi + 1) * NSTATE)
        dS = dst[g]
        P, Pn = p_ref[0, gi], pn_ref[0, gi]
        xs, dt, cs, dY = xs_ref[:, gw], dt_ref[:, gw], cs_ref[:, gw], dy_ref[:, gw]
        Bf, Cf = b_ref[:, gn], c_ref[:, gn]
        Bb, Cb = Bf.astype(BF16), Cf.astype(BF16)
        X = xs * dt
        ecs = jnp.exp(cs)
        decay = jnp.exp(cs[CHUNK - 1:CHUNK, :] - cs)
        CBm = _dot3(Cf, Bf, NT)
        dYe = dY * ecs
        dP_off = _dot3(dYe, Cf, TN)
        dC = _dot(dYe.astype(BF16), P.astype(BF16))
        dcs = dYe * _dot3(Cf, P, NT)
        Xd = X * decay
        dB = _dot(Xd.astype(BF16), dS.astype(BF16))
        E = _dot3(Bf, dS, NT)
        dX = E * decay
        dcs = dcs - E * Xd
        R = _dot3(jnp.ones((8, NSTATE), F32), dS * Pn, NT)
        sub_g = lax.broadcasted_iota(jnp.int32, (CHUNK, GROUP_W), 0)
        dcs = dcs + jnp.where(sub_g == CHUNK - 1, R[0:1, :], 0.0)
        lane = lax.broadcasted_iota(jnp.int32, (CHUNK, CHUNK), 1)
        sub = lax.broadcasted_iota(jnp.int32, (CHUNK, CHUNK), 0)
        dCB = jnp.zeros((CHUNK, CHUNK), F32)
        dXs, dcss, ecl = [], [], []
        for pr in range(2):
            sl = slice(pr * LANES, (pr + 1) * LANES)
            Ls, e_last = _pair_terms(cs[:, sl])
            ecl.append(e_last)
            dYpb = dY[:, sl].astype(BF16)
            dMcat = _dot(dYpb, _block_diag(X[:, sl]).astype(BF16), NT)
            Mcat = jnp.concatenate([(CBm * L).astype(BF16) for L in Ls], axis=1)
            dXt = _dot(Mcat, dYpb, TN)
            dXs.append(jnp.where(lane < HEADDIM, dXt[:CHUNK], dXt[CHUNK:]))
            colacc = jnp.zeros((CHUNK, CHUNK), F32)
            rowacc = jnp.zeros((CHUNK, CHUNK), F32)
            for k in range(2):
                dG = dMcat[:, k * CHUNK:(k + 1) * CHUNK] * Ls[k]
                dCB = dCB + dG
                Q = dG * CBm
                colacc = colacc + jnp.where(lane == k * HEADDIM, jnp.sum(Q, axis=1, keepdims=True), 0.0)
                rowacc = rowacc + jnp.where(sub == k * HEADDIM, jnp.sum(Q, axis=0, keepdims=True), 0.0)
            dcss.append(colacc - rowacc.T)
        dX = dX + jnp.concatenate(dXs, axis=1)
        dcs = dcs + jnp.concatenate(dcss, axis=1)
        dCBb = dCB.astype(BF16)
        dc_ref[:, gn] = dC + _dot(dCBb, Bb)
        db_ref[:, gn] = dB + _dot(dCBb, Cb, TN)
        dxs_ref[:, gw] = dX * dt + dY * d_ref[:, gw]
        ddt_ref[:, gw] = dX * xs
        dcs_ref[:, gw] = dcs
        dd_ref[0, :, gw] = jnp.sum(dY * xs, axis=0, keepdims=True)
        dst[g] = dS * jnp.concatenate(ecl, axis=0) + dP_off

    p_blk = pl.BlockSpec((1, gs, GROUP_W, NSTATE), lambda c, s: (nc - 1 - c, s, 0, 0))
    pn_blk = pl.BlockSpec((1, gs, GROUP_W, NSTATE), lambda c, s: (jnp.minimum(nc - c, nc - 1), s, 0, 0))
    st_blk = pl.BlockSpec((CHUNK, gs * NSTATE), lambda c, s: (nc - 1 - c, s))
    outs, carried = _call(
        body, grid=(nc, NGROUPS // gs),
        in_specs=[g_blk(rev), b_blk(rev), c_blk(rev), g_blk(rev), g_blk(rev), pl.BlockSpec((1, gs * GROUP_W), lambda c, s: (0, s)),
                  p_blk, pn_blk, g_blk(rev)],
        out_specs=[g_blk(rev), st_blk, st_blk, g_blk(rev), g_blk(rev), pl.BlockSpec((1, 1, gs * GROUP_W), lambda c, s: (nc - 1 - c, 0, s))],
        out_shape=[_sds((T, n_inner), F32), _sds((T, NGROUPS * NSTATE), F32), _sds((T, NGROUPS * NSTATE), F32),
                   _sds((T, n_inner), F32), _sds((T, n_inner), F32), _sds((nc, 1, n_inner), F32)],
        scratch=[pltpu.VMEM((NGROUPS, GROUP_W, NSTATE), F32)],
        args=[xbc, xbc, xbc, dt_e, cs_e, d_e, states, states, dy], name=name, sem=("arbitrary", "arbitrary"), comm=comm)
    return outs if comm is None else (outs, carried)


def _ssd_post(ddt_e, dcs_e, dd_p, dt_raw, dt_bias, a_log, n_heads, name):
    T, n_inner = ddt_e.shape

    def body(ddt_ref, dcs_ref, dd_ref, r_ref, b_ref, al_ref, ex_ref, draw_ref, dbias_ref, dal_ref, ddsk_ref):
        @pl.when(pl.program_id(0) == 0)
        def _():
            dbias_ref[...] = jnp.zeros_like(dbias_ref)
            dal_ref[...] = jnp.zeros_like(dal_ref)
            ddsk_ref[...] = jnp.zeros_like(ddsk_ref)

        spread = [ddt_ref[...], dcs_ref[...], jnp.broadcast_to(dd_ref[0], (8, n_inner))]
        stacked = _dot(jnp.concatenate([p for v in spread for p in _split3(v)], axis=0), ex_ref[...], NT)
        sums, r0 = [], 0
        for v in spread:
            n = v.shape[0]
            sums.append(stacked[r0:r0 + n] + stacked[r0 + n:r0 + 2 * n] + stacked[r0 + 2 * n:r0 + 3 * n])
            r0 += 3 * n
        ddt_h, dcs_h, dd_h = sums
        raw = r_ref[...] + b_ref[...]
        dt = _softplus(raw)
        A = -jnp.exp(al_ref[...])
        i = lax.broadcasted_iota(jnp.int32, (CHUNK, CHUNK), 0)
        j = lax.broadcasted_iota(jnp.int32, (CHUNK, CHUNK), 1)
        upper = (j >= i).astype(BF16)
        da = sum(_dot(upper, p) for p in _split3(dcs_h))
        ddt = ddt_h + da * A
        lane = lax.broadcasted_iota(jnp.int32, (CHUNK, LANES), 1)
        draw = jnp.where(lane < n_heads, ddt * jax.nn.sigmoid(raw), 0.0)
        draw_ref[...] = draw.astype(BF16)
        dbias_ref[...] += jnp.sum(draw, axis=0, keepdims=True)
        dal_ref[...] += jnp.sum(da * dt, axis=0, keepdims=True) * A
        ddsk_ref[...] += dd_h[0:1, :]

    wide = pl.BlockSpec((CHUNK, n_inner), lambda c: (c, 0))
    blk = pl.BlockSpec((CHUNK, LANES), lambda c: (c, 0))
    return pl.pallas_call(
        body, grid=(T // CHUNK,),
        in_specs=[wide, wide, pl.BlockSpec((1, 1, n_inner), lambda c: (c, 0, 0)), blk, _vec(LANES), _vec(LANES),
                  pl.BlockSpec((LANES, n_inner), lambda c: (0, 0))],
        out_specs=[blk, _vec(LANES), _vec(LANES), _vec(LANES)],
        out_shape=[_sds((T, LANES), BF16)] + [_sds((1, LANES), F32)] * 3,
        name=name, compiler_params=_params("arbitrary"))(ddt_e, dcs_e, dd_p, dt_raw, dt_bias, a_log, _head_expand(n_inner))


def _row2(v):
    return v.reshape(1, -1).astype(F32)


def _pad_lanes(v):
    return jnp.pad(_row2(v), ((0, 0), (0, LANES - v.shape[-1])))


class _NoExchange:
    def __init__(self, W):
        self.W, self.grads = W, {}

    def weight(self, k):
        return self.W[k]

    def carry(self, name):
        return None

    def carried(self, name, outs):
        pass

    def grad(self, k, g):
        self.grads[k] = g

    def tok(self):
        return jnp.zeros((), F32)

    def point(self, name, value):
        pass


def _local_step(x, tgt, S, small):
    T, D = x.shape

    def mm(a, b, *, name, **kw):
        comm = S.carry(name)
        if comm is None:
            return _mm(a, b, name=name, **kw)
        res, outs = _mm(a, b, name=name, comm=comm, **kw)
        S.carried(name, outs)
        return res

    def carrying(fn, *args, name):
        comm = S.carry(name)
        if comm is None:
            return fn(*args, name)
        res, outs = fn(*args, name, comm=comm)
        S.carried(name, outs)
        return res

    n_inner = 2 * D
    n_heads = n_inner // HEADDIM
    norm_mix, norm_mlp, norm_final = _row2(small["norm_mix"]), _row2(small["norm_mlp"]), _row2(small["norm_final"])
    b_gate, ssm_b, ssm_norm_w = _row2(small["b_gate"]), _row2(small["ssm_conv_b"]), _row2(small["ssm_norm_w"])
    dt_bias, a_log = _pad_lanes(small["dt_bias"]), _pad_lanes(small["A_log"])
    d_e = jnp.repeat(small["D_skip"].astype(F32), HEADDIM).reshape(1, n_inner)

    hb = carrying(_rms_fwd, x, norm_mix, name="rms_mix")
    sc_w, ssm_w = S.weight("sc_conv_w"), S.weight("ssm_conv_w")
    p_xbc = mm(hb, S.weight("xbc"), mode="nn", name="proj_xbc")
    p_dt = mm(hb, S.weight("dt"), mode="nn", name="proj_dt")
    p_z = mm(hb, S.weight("z"), mode="nn", name="proj_z")
    p_sc = mm(hb, S.weight("sc"), mode="nn", name="proj_sc")
    p_gate = mm(hb, S.weight("gate"), mode="nn", name="proj_gate")
    xbc = carrying(_ssm_conv_fwd, p_xbc, ssm_w, ssm_b, name="ssm_conv_fwd")
    dt_e, cs_e = _ssd_prep(p_dt, dt_bias, a_log, n_inner, "ssd_prep")
    ya = _sc_fwd(p_sc, sc_w, "sc_fwd")
    y, states = carrying(_ssd_fwd, xbc, dt_e, cs_e, d_e, name="ssd_fwd")
    S.point("mixers_done", [y, ya, p_gate])
    yb = carrying(_gnorm_fwd, y, p_z, ssm_norm_w, name="gnorm_fwd")
    br_a = mm(ya, S.weight("bsc"), mode="nn", name="branch_sc")
    br_b = mm(yb, S.weight("bssm"), mode="nn", name="branch_ssm")
    merged = _merge_fwd(p_gate, b_gate, br_a, br_b, "merge_fwd")
    x1 = mm(merged, S.weight("out"), mode="nn", name="out_proj", extras=(x,), epi=_epi_add)
    h2 = _rms_fwd(x1, norm_mlp, "rms_mlp")
    r_act = mm(h2, S.weight("w1"), mode="nn", name="mlp_up", epi=_epi_relu2, out_dtypes=(BF16,))
    x2 = mm(r_act, S.weight("w2"), mode="nn", name="mlp_down", extras=(x1,), epi=_epi_add)
    dx2, dx2b, g_norm_final, loss_row = _final(x2, norm_final, tgt, "final")

    S.grad("w2", mm(r_act, dx2b, mode="tn", name="mlp_down_dw", out_dtypes=(BF16,)))
    da = mm(dx2b, S.weight("w2"), mode="nt", name="mlp_down_dx", extras=(r_act,), epi=_epi_relu2_bwd, out_dtypes=(BF16,))
    S.grad("w1", mm(h2, da, mode="tn", name="mlp_up_dw", out_dtypes=(BF16,)))
    dh2 = mm(da, S.weight("w1"), mode="nt", name="mlp_up_dx")
    dx1, dx1b, g_norm_mlp = _rms_bwd(x1, norm_mlp + S.tok(), dh2, dx2, "rms_mlp_bwd")
    S.grad("out", mm(merged, dx1b, mode="tn", name="out_proj_dw", out_dtypes=(BF16,)))
    dmerged = mm(dx1b, S.weight("out"), mode="nt", name="out_proj_dx")
    dbr_a, dbr_b, d_gate, g_b_gate = _merge_bwd(dmerged, p_gate, b_gate, br_a, br_b, "merge_bwd")
    S.grad("bssm", mm(yb, dbr_b, mode="tn", name="branch_ssm_dw", out_dtypes=(BF16,)))
    S.grad("bsc", mm(ya, dbr_a, mode="tn", name="branch_sc_dw", out_dtypes=(BF16,)))
    dyb = mm(dbr_b, S.weight("bssm"), mode="nt", name="branch_ssm_dx")
    dya = mm(dbr_a, S.weight("bsc"), mode="nt", name="branch_sc_dx")
    dy, d_z, g_ssm_norm_w = _gnorm_bwd(y, p_z, ssm_norm_w + S.tok(), dyb, "gnorm_bwd")
    dxs, dB, dC, ddt_e, dcs_e, dd_p = carrying(_ssd_bwd, xbc, dt_e, cs_e, d_e, states, dy, name="ssd_bwd")
    d_dt, g_dt_bias, g_a_log, g_d_skip = _ssd_post(ddt_e, dcs_e, dd_p, p_dt, dt_bias, a_log, n_heads, "ssd_post")
    d_xbc, g_ssm_w, g_ssm_b = carrying(_ssm_conv_bwd, p_xbc, ssm_w, ssm_b, dxs, dB, dC, name="ssm_conv_bwd")
    d_scB, d_scC, d_scX, g_sc_w = _sc_bwd(p_sc, sc_w, dya, "sc_bwd")
    d_sc = jnp.concatenate([d_scB, d_scC, d_scX], axis=1)
    pieces = [("sc", d_sc), ("z", d_z), ("xbc", d_xbc), ("dt", d_dt), ("gate", d_gate)]
    S.grad("win", {k: mm(hb, d, mode="tn", name="proj_dw_" + k, out_dtypes=(BF16,)) for k, d in pieces})
    pieces = [(k, d + S.tok().astype(d.dtype) if k == "dt" else d) for k, d in pieces]
    dh = mm([d for _, d in pieces], [S.weight(k) for k, _ in pieces], mode="nt", name="proj_dx")
    grad_x, _, g_norm_mix = _rms_bwd(x, norm_mix, dh, dx1, "rms_mix_bwd")

    g_small = dict(norm_mix=g_norm_mix, b_gate=g_b_gate, sc_conv_w=g_sc_w, ssm_conv_w=g_ssm_w, ssm_conv_b=g_ssm_b,
                   dt_bias=g_dt_bias, A_log=g_a_log, D_skip=g_d_skip, ssm_norm_w=g_ssm_norm_w, norm_mlp=g_norm_mlp,
                   norm_final=g_norm_final, loss=loss_row)
    return grad_x, g_small


class _Place:
    def __init__(self, k=0):
        x, y, c = lax.axis_index("x"), lax.axis_index("y"), lax.axis_index("c")
        self.x = 1 - x if k & 4 else x
        self.y = 1 - y if k & 2 else y
        self.c = 1 - c if k & 1 else c
        self.chip = 2 * self.x + self.y
        self.id = 2 * self.chip + self.c


ICI_PEERS = (2, 4, 6)
SIBLING = (1,)
ALL_PEERS = (1, 2, 3, 4, 5, 6, 7)


class _Comm:
    def __init__(self, arrs, out_shape, ks, src, dst, own=None, aliases=None):
        self.arrs, self.out_shape, self.ks = list(arrs), list(out_shape), tuple(ks)
        self.n = len(self.arrs)
        self.src, self.dst, self.own = src, dst, own
        self.aliases = aliases or {}
        dma = pltpu.SemaphoreType.DMA
        self.scratch = [dma((self.n, len(self.ks))), dma((self.n, len(self.ks))), dma((self.n,))]

    def _copies(self, ins, outs, sems, with_recvs):
        send_sems, recv_sems, local_sems = sems
        me = _Place()
        owns, sends, recvs = [], [], []
        for a in range(self.n):
            if self.own is not None:
                s, d = self.own(a, ins[a], outs[a], me)
                owns.append(pltpu.make_async_copy(s, d, local_sems.at[a]))
            for i, k in enumerate(self.ks):
                peer = _Place(k)
                for sender, lst in ((me, sends), (peer, recvs)) if with_recvs else ((me, sends),):
                    lst.append(pltpu.make_async_remote_copy(
                        src_ref=self.src(a, ins[a], me, peer), dst_ref=self.dst(a, outs[a], sender),
                        send_sem=send_sems.at[a, i], recv_sem=recv_sems.at[a, i],
                        device_id=(peer.x, peer.y, peer.c), device_id_type=MESH))
        return owns, sends, recvs

    def start(self, ins, outs, sems):
        owns, sends, _ = self._copies(ins, outs, sems, False)
        for cp in owns + sends:
            cp.start()

    def finish(self, ins, outs, sems):
        owns, sends, recvs = self._copies(ins, outs, sems, True)
        for cp in recvs:
            cp.wait_recv()
        for cp in sends:
            cp.wait_send()
        for cp in owns:
            cp.wait()


class _GatherBoth:
    def __init__(self, shards):
        self.arrs, self.n, self.aliases = list(shards), len(shards), {}
        self.out_shape = [_sds((4, 2) + s.shape, s.dtype) for s in shards]
        dma = pltpu.SemaphoreType.DMA
        self.scratch = [dma((self.n, 7)), dma((self.n, 7)), dma((self.n,))]

    def _copy(self, a, j, src, slot, to, outs, sems):
        return pltpu.make_async_remote_copy(src_ref=src, dst_ref=outs[a].at[slot.chip, slot.c], send_sem=sems[0].at[a, j],
                                            recv_sem=sems[1].at[a, j], device_id=(to.x, to.y, to.c), device_id_type=MESH)

    def start(self, ins, outs, sems):
        me, sib = _Place(), _Place(1)
        for a in range(self.n):
            pltpu.make_async_copy(ins[a], outs[a].at[me.chip, me.c], sems[2].at[a]).start()
            self._copy(a, 0, ins[a], me, sib, outs, sems).start()
            for i, k in enumerate(ICI_PEERS):
                self._copy(a, 1 + i, ins[a], me, _Place(k), outs, sems).start()

    def finish(self, ins, outs, sems):
        me, sib = _Place(), _Place(1)
        passed = []
        for i, k in enumerate(ICI_PEERS):
            peer = _Place(k)
            for a in range(self.n):
                self._copy(a, 1 + i, ins[a], peer, peer, outs, sems).wait_recv()
                cp = self._copy(a, 4 + i, outs[a].at[peer.chip, peer.c], peer, sib, outs, sems)
                cp.start()
                passed.append(cp)
        for a in range(self.n):
            self._copy(a, 0, ins[a], sib, sib, outs, sems).wait_recv()
            for i, k in enumerate(ICI_PEERS):
                far = _Place(k | 1)
                self._copy(a, 4 + i, outs[a].at[far.chip, far.c], far, sib, outs, sems).wait_recv()
        for a in range(self.n):
            self._copy(a, 0, ins[a], me, sib, outs, sems).wait_send()
            for i, k in enumerate(ICI_PEERS):
                self._copy(a, 1 + i, ins[a], me, _Place(k), outs, sems).wait_send()
            pltpu.make_async_copy(ins[a], outs[a].at[me.chip, me.c], sems[2].at[a]).wait()
        for cp in passed:
            cp.wait_send()


def _run_comm(comm, name, after=()):
    n, n_after = comm.n, len(after)

    def body(*refs):
        ins, outs, sems = refs[:n], refs[n + n_after:2 * n + n_after], refs[2 * n + n_after:]
        comm.start(ins, outs, sems)
        comm.finish(ins, outs, sems)

    return list(pl.pallas_call(body, in_specs=[ANY] * (n + n_after), out_specs=[ANY] * n, out_shape=comm.out_shape,
                               scratch_shapes=comm.scratch, input_output_aliases=dict(comm.aliases), name=name)(*comm.arrs, *after))


def _gather_sibling(bufs):
    return _Comm(bufs, [_sds(b.shape, b.dtype) for b in bufs], SIBLING,
                 src=lambda a, i, me, p: i.at[:, me.c], dst=lambda a, o, s: o.at[:, s.c], aliases={a: a for a in range(len(bufs))})


def _scatter_sibling(parts):
    return _Comm(parts, [_sds((4,) + p.shape[2:], p.dtype) for p in parts], SIBLING,
                 src=lambda a, i, me, p: i.at[:, p.c], dst=lambda a, o, s: o)


HBM_SPEC = pl.BlockSpec(memory_space=pltpu.HBM)
SEM_SPEC = pl.BlockSpec(memory_space=pltpu.SEMAPHORE)
DATAFLOW = pltpu.SideEffectType.DATAFLOW_SIDE_EFFECTING


def _tiles_2d(R, C, max_rows=256):
    if R % max_rows == 0:
        return max_rows, C, R // max_rows, lambda i: (i, 0)
    if R <= 2 * max_rows or C % 256:
        return R, C, 1, lambda i: (0, 0)
    return R, 256, C // 256, lambda i: (0, i)


def _ici_copy(gather, a, srcs, lands, send_sems, recv_sems, i, me, peer, sender):
    src = lands[a].at[me.chip, me.c] if gather else srcs[a].at[peer.chip]
    dst = lands[a].at[sender.chip, sender.c] if gather else lands[a].at[sender.chip]
    j = a * len(ICI_PEERS) + i
    return pltpu.make_async_remote_copy(src_ref=src, dst_ref=dst, send_sem=send_sems.at[j], recv_sem=recv_sems.at[j],
                                        device_id=(peer.x, peer.y, peer.c), device_id_type=MESH)


def _ici_start(srcs, lands, gather, name):
    n, n_s = len(lands), len(srcs)
    bufs = list(srcs) + list(lands)

    def body(*refs):
        src_refs, land_refs = refs[:n_s], refs[n_s:n_s + n]
        send_sems, recv_sems = refs[n_s + n], refs[n_s + n + 1]
        token = refs[-1]
        me = _Place()
        for a in range(n):
            for i, k in enumerate(ICI_PEERS):
                _ici_copy(gather, a, src_refs, land_refs, send_sems, recv_sems, i, me, _Place(k), me).start()
        token[...] = jnp.zeros_like(token)

    dma = pltpu.SemaphoreType.DMA((n * len(ICI_PEERS),))
    outs = pl.pallas_call(
        body, name=name, out_shape=(dma, dma, *[pltpu.HBM(v.shape, v.dtype) for v in bufs], _sds((8, LANES), F32)),
        in_specs=(HBM_SPEC,) * len(bufs),
        out_specs=(SEM_SPEC, SEM_SPEC) + (HBM_SPEC,) * len(bufs) + (pl.BlockSpec(memory_space=pltpu.VMEM),),
        input_output_aliases={j: 2 + j for j in range(len(bufs))}, compiler_params=pltpu.CompilerParams(has_side_effects=DATAFLOW),
    )(*[pltpu.with_memory_space_constraint(v, pltpu.HBM) for v in bufs])
    return outs[0], outs[1], list(outs[2:2 + n_s]), list(outs[2 + n_s:2 + n_s + n]), outs[-1]


def _ici_wait(flight, after, gather, name):
    send_sems, recv_sems, srcs, lands, _ = flight
    n, n_s = len(lands), len(srcs)
    bufs = srcs + lands

    def body(*refs):
        src_refs, land_refs = refs[:n_s], refs[n_s:n_s + n]
        s_sems, r_sems = refs[n_s + n], refs[n_s + n + 1]
        me = _Place()
        for a in range(n):
            for i, k in enumerate(ICI_PEERS):
                peer = _Place(k)
                cp = _ici_copy(gather, a, src_refs, land_refs, s_sems, r_sems, i, me, peer, peer)
                cp.wait_send()
                cp.wait_recv()

    outs = pl.pallas_call(
        body, name=name, out_shape=tuple(pltpu.HBM(v.shape, v.dtype) for v in bufs),
        in_specs=(HBM_SPEC,) * len(bufs) + (SEM_SPEC, SEM_SPEC) + (ANY,) * len(after), out_specs=(HBM_SPEC,) * len(bufs),
        input_output_aliases={j: j for j in range(len(bufs))}, compiler_params=pltpu.CompilerParams(has_side_effects=DATAFLOW),
    )(*bufs, send_sems, recv_sems, *after)
    return list(outs[n_s:])


def _own_shards(shards, after, name):
    n = len(shards)
    vmem = pl.BlockSpec(memory_space=pltpu.VMEM)

    def body(*refs):
        ins, outs, cast, sems = refs[:n], refs[n + 1:2 * n + 1], refs[2 * n + 1:3 * n + 1], refs[3 * n + 1]
        me = _Place()
        copies = []
        for a in range(n):
            cast[a][...] = ins[a][...].astype(BF16)
            copies.append(pltpu.make_async_copy(cast[a], outs[a].at[me.chip, me.c], sems.at[a]))
            copies[-1].start()
        for cp in copies:
            cp.wait()

    return list(pl.pallas_call(
        body, in_specs=[vmem] * n + [ANY], out_specs=[ANY] * n, out_shape=[_sds((4, 2) + s.shape, BF16) for s in shards],
        scratch_shapes=[pltpu.VMEM(s.shape, BF16) for s in shards] + [pltpu.SemaphoreType.DMA((n,))], name=name)(*shards, after))


def _col_pieces(widths):
    out, c = [], 0
    for k, w in widths:
        out.append((k, c, w))
        c += w
    return out


def _split_range(c0, n, bounds):
    parts, c = [], c0
    while c < c0 + n:
        r = max(i for i in range(len(bounds) - 1) if bounds[i] <= c)
        w = min(c0 + n, bounds[r + 1]) - c
        parts.append((r, c - bounds[r], w))
        c += w
    return parts


def _win_unpack(g, widths, name):
    n, R, C = g.shape
    tr = min(256, R)
    pieces = _col_pieces(widths)
    padded = [-(-w // LANES) * LANES for _, _, w in pieces]
    shard_bounds = [s * C for s in range(n + 1)]

    def body(g_ref, *o_refs):
        for (k, c0, w), o_ref in zip(pieces, o_refs):
            for t in range(0, o_ref.shape[1], LANES):
                valid = max(0, min(LANES, w - t))
                cols = [g_ref[s, :, o:o + ww] for s, o, ww in _split_range(c0 + t, valid, shard_bounds)] if valid else []
                if valid < LANES:
                    cols.append(jnp.zeros((tr, LANES - valid), g_ref.dtype))
                o_ref[:, t:t + LANES] = cols[0] if len(cols) == 1 else jnp.concatenate(cols, axis=1)

    return pl.pallas_call(
        body, grid=(R // tr,), in_specs=[pl.BlockSpec((n, tr, C), lambda i: (0, i, 0))],
        out_specs=[pl.BlockSpec((tr, p), lambda i: (i, 0)) for p in padded],
        out_shape=[_sds((R, p), g.dtype) for p in padded], name=name, compiler_params=_params("parallel"))(g)


def _win_pack(grads, widths, n, name):
    R = grads[0].shape[0]
    tr = min(256, R)
    pieces = _col_pieces(widths)
    total = pieces[-1][1] + pieces[-1][2]
    C = total // n
    bounds = [c0 for _, c0, _ in pieces] + [total]

    def body(*refs):
        g_refs, o_ref = refs[:-1], refs[-1]

        def tile_t(c0):
            cols = [g_refs[r][:, o:o + ww] for r, o, ww in _split_range(c0, LANES, bounds)]
            tile = cols[0] if len(cols) == 1 else jnp.concatenate(cols, axis=1)
            return tile.astype(F32).T

        for s in range(n):
            full = C // LANES * LANES
            for t in range(0, full, LANES):
                o_ref[s, t:t + LANES, :] = tile_t(s * C + t).astype(o_ref.dtype)
            if full < C:
                o_ref[s, full:C, :] = tile_t(s * C + C - LANES)[LANES - (C - full):, :].astype(o_ref.dtype)

    return pl.pallas_call(
        body, grid=(R // tr,), in_specs=[pl.BlockSpec((tr, gr.shape[1]), lambda i: (i, 0)) for gr in grads],
        out_specs=pl.BlockSpec((n, C, tr), lambda i: (0, 0, i)), out_shape=_sds((n, C, R), grads[0].dtype),
        name=name, compiler_params=_params("parallel"))(*grads)


def _gather_all(arrs):
    return _Comm(arrs, [_sds((N_DEV,) + a.shape, a.dtype) for a in arrs], ALL_PEERS,
                 src=lambda a, i, me, p: i, dst=lambda a, o, s: o.at[s.id], own=lambda a, i, o, me: (i, o.at[me.id]))


def _add_halves(parts, got, name):
    n, _, R, C = parts.shape
    br, bc, nb, at = _tiles_2d(R, C, max_rows=1024)
    place = jnp.stack([lax.axis_index("c"), 2 * lax.axis_index("x") + lax.axis_index("y")]).astype(jnp.int32)

    def body(q_ref, p_ref, g_ref, o_ref, land_ref):
        s = (p_ref[0, 0].astype(F32) + g_ref[0].astype(F32)).astype(o_ref.dtype)
        o_ref[0] = s

        @pl.when(pl.program_id(1) == q_ref[1])
        def _():
            land_ref[0] = s

    spec = pltpu.PrefetchScalarGridSpec(
        num_scalar_prefetch=1, grid=(nb, n),
        in_specs=[pl.BlockSpec((1, 1, br, bc), lambda i, q, q_ref: (q, q_ref[0]) + at(i)), pl.BlockSpec((1, br, bc), lambda i, q, q_ref: (q,) + at(i))],
        out_specs=[pl.BlockSpec((1, br, bc), lambda i, q, q_ref: (q,) + at(i)), pl.BlockSpec((1, br, bc), lambda i, q, q_ref: (q_ref[1],) + at(i))])
    return pl.pallas_call(body, grid_spec=spec, out_shape=[_sds((n, R, C), parts.dtype)] * 2, name=name,
                          compiler_params=_params("parallel", "arbitrary"))(place, parts, got)


def _adam(w, m, v, gparts, name, comm=None):
    R, C = w.shape
    n = gparts.shape[0]
    br, bc, nb, at = _tiles_2d(R, C, max_rows=512)
    c1 = 1.0 / (1.0 - ADAM_B1 ** ADAM_STEP)
    c2 = 1.0 / (1.0 - ADAM_B2 ** ADAM_STEP)

    def body(w_ref, m_ref, v_ref, g_ref, go_ref, d_ref, mo_ref, vo_ref):
        g = g_ref[0].astype(F32)
        for s in range(1, n):
            g = g + g_ref[s].astype(F32)
        mn = ADAM_B1 * m_ref[...] + (1.0 - ADAM_B1) * g
        vn = ADAM_B2 * v_ref[...] + (1.0 - ADAM_B2) * (g * g)
        go_ref[...] = g
        mo_ref[...] = mn
        vo_ref[...] = vn
        d_ref[...] = -ADAM_LR * ((mn * c1) / (jnp.sqrt(vn * c2) + ADAM_EPS) + ADAM_WD * w_ref[...])

    blk = pl.BlockSpec((br, bc), at)
    outs, carried = _call(
        body, grid=(nb,), in_specs=[blk, blk, blk, pl.BlockSpec((n, br, bc), lambda i: (0,) + at(i))],
        out_specs=[blk] * 4, out_shape=[_sds((R, C), F32)] * 4, args=[w, m, v, gparts], name=name, sem=("parallel",), comm=comm)
    return outs if comm is None else (outs, carried)


_SMALL_ORDER = ("norm_mix", "b_gate", "sc_conv_w", "ssm_conv_w", "ssm_conv_b", "dt_bias", "A_log", "D_skip", "ssm_norm_w",
                "norm_mlp", "norm_final", "loss")
_REPLICATED = ("norm_mix", "b_gate", "ssm_conv_b", "dt_bias", "A_log", "D_skip", "ssm_norm_w", "norm_mlp", "norm_final")


def _cols_to_slots(g, n):
    R = g.shape[0]
    return jnp.transpose(g.reshape(R, n, g.shape[1] // n), (1, 0, 2))


def _slots_to_cols(g):
    n, R, C = g.shape
    return jnp.transpose(g, (1, 0, 2)).reshape(R, n * C)


def kernel(x, norm_mix, w_in, b_gate, sc_conv_w, ssm_conv_w, ssm_conv_b, dt_bias, A_log, D_skip, ssm_norm_w, w_branch_sc, w_branch_ssm, w_out, norm_mlp, w_mlp1, w_mlp2, norm_final, loss_target, m_norm_mix, m_w_in, m_b_gate, m_sc_conv_w, m_ssm_conv_w, m_ssm_conv_b, m_dt_bias, m_A_log, m_D_skip, m_ssm_norm_w, m_w_branch_sc, m_w_branch_ssm, m_w_out, m_norm_mlp, m_w_mlp1, m_w_mlp2, m_norm_final, v_norm_mix, v_w_in, v_b_gate, v_sc_conv_w, v_ssm_conv_w, v_ssm_conv_b, v_dt_bias, v_A_log, v_D_skip, v_ssm_norm_w, v_w_branch_sc, v_w_branch_ssm, v_w_out, v_norm_mlp, v_w_mlp1, v_w_mlp2, v_norm_final):
    T, D = x.shape[1], x.shape[2]
    n_inner = 2 * D
    n_heads = n_inner // HEADDIM
    n_xbc = n_inner + 2 * NGROUPS * NSTATE
    me = 4 * lax.axis_index("x") + 2 * lax.axis_index("y") + lax.axis_index("c")

    in_cols = [("sc", 3 * D), ("z", n_inner), ("xbc", n_xbc), ("dt", n_heads), ("gate", 2 * D)]
    by_owner = lambda b: b.reshape((N_DEV,) + b.shape[2:])
    to_owner = lambda g: g.reshape((4, 2) + g.shape[1:])
    rows_of = lambda g: to_owner(g.reshape((N_DEV, g.shape[0] // N_DEV) + g.shape[1:]))
    cols_of = lambda g: to_owner(_cols_to_slots(g, N_DEV))

    class Schedule(_NoExchange):
        late = ("bssm", "bsc", "out", "w1", "w2")
        gather_sib = dict(gnorm_fwd=("bsc", "bssm", "out"), branch_ssm=("w1", "w2"))
        scatter_sib = dict(mlp_up_dx=("w2", "w1"), branch_ssm_dx=("out", "bssm", "bsc"))
        shards = dict(bsc=w_branch_sc, bssm=w_branch_ssm, out=w_out, w1=w_mlp1, w2=w_mlp2)

        def __init__(self):
            self.W, self.staged, self.grads, self.summed, self.scatters = {}, {}, {}, {}, []
            self.token = jnp.zeros((), F32)

        def first_weights(self, bufs):
            self.W.update(zip([k for k, _ in in_cols], _win_unpack(by_owner(bufs[0]), in_cols, "win_unpack")))
            self.W.update(sc_conv_w=_slots_to_cols(by_owner(bufs[1])), ssm_conv_w=_slots_to_cols(by_owner(bufs[2])))
            lands = _own_shards([self.shards[k] for k in self.late], bufs[1], "own_shards")
            self.gather_flight = _ici_start([], lands, True, "gather_late_start")
            self.token = self.gather_flight[4][0, 0]
            self.W["dt"] = self.W["dt"] + self.token.astype(BF16)

        def tok(self):
            return self.token

        def point(self, name, values):
            if name == "mixers_done":
                lands = _ici_wait(self.gather_flight, values, True, "gather_late_wait")
                self.staged.update(zip(self.late, lands))

        def carry(self, name):
            if name == "rms_mix":
                return _GatherBoth([w_in.astype(BF16), sc_conv_w, ssm_conv_w])
            if name in self.gather_sib:
                return _gather_sibling([self.staged.pop(k) for k in self.gather_sib[name]])
            if name in self.scatter_sib:
                return _scatter_sibling([self.grads[k] for k in self.scatter_sib[name]])
            return None

        def start_scatter(self, keys, halves_and_lands):
            halves, lands = [h for h, _ in halves_and_lands], [l for _, l in halves_and_lands]
            flight = _ici_start(halves, lands, False, "scatter_%s_start" % keys[0])
            self.scatters.append((keys, flight))
            self.token = flight[4][0, 0]

        def carried(self, name, outs):
            if name == "rms_mix":
                self.first_weights(outs)
            elif name in self.gather_sib:
                for k, b in zip(self.gather_sib[name], outs):
                    full = by_owner(b)
                    self.W[k] = _slots_to_cols(full) if k == "w1" else full.reshape(-1, D)
            else:
                keys = self.scatter_sib[name]
                self.start_scatter(keys, [_add_halves(self.grads[k], b, "add_halves_" + k) for k, b in zip(keys, outs)])

        def grad(self, k, g):
            if k == "win":
                g = to_owner(_win_pack([g[k] for k, _ in in_cols], in_cols, N_DEV, "win_pack"))
                got = _run_comm(_scatter_sibling([g]), "scatter_sibling_win")[0]
                self.start_scatter(("win",), [_add_halves(g, got, "add_halves_win")])
            else:
                self.grads[k] = cols_of(g) if k == "w1" else rows_of(g)

        def finish_scatter(self, after):
            keys, flight = self.scatters.pop(0)
            return dict(zip(keys, _ici_wait(flight, after, False, "scatter_%s_wait" % keys[0])))

    S = Schedule()
    small = dict(norm_mix=norm_mix, b_gate=b_gate, ssm_conv_b=ssm_conv_b, dt_bias=dt_bias, A_log=A_log, D_skip=D_skip,
                 ssm_norm_w=ssm_norm_w, norm_mlp=norm_mlp, norm_final=norm_final)
    grad_x, g_small = _local_step(x.reshape(T, D), loss_target.reshape(T, D), S, small)

    small_flat = jnp.concatenate([g_small[k].reshape(-1) for k in _SMALL_ORDER])
    n_small = small_flat.shape[0]
    rows = -(-n_small // (8 * LANES)) * 8
    small_pack = jnp.pad(small_flat, (0, rows * LANES - n_small)).reshape(rows, LANES)

    res = {}
    big = [("w_in", "win", w_in, m_w_in, v_w_in), ("w_branch_sc", "bsc", w_branch_sc, m_w_branch_sc, v_w_branch_sc),
           ("w_branch_ssm", "bssm", w_branch_ssm, m_w_branch_ssm, v_w_branch_ssm), ("w_out", "out", w_out, m_w_out, v_w_out),
           ("w_mlp1", "w1", w_mlp1, m_w_mlp1, v_w_mlp1), ("w_mlp2", "w2", w_mlp2, m_w_mlp2, v_w_mlp2)]
    by_grad = {gk: (k, w, m, v) for k, gk, w, m, v in big}
    after = [grad_x]
    while S.scatters:
        for gk, parts in S.finish_scatter(after).items():
            k, w, m, v = by_grad[gk]
            if gk == "win":
                res_t, (small_parts,) = _adam(w.T, m.T, v.T, parts, "adam_" + k, comm=_gather_all([small_pack]))
                res[k] = [r.T for r in res_t]
            else:
                res[k] = _adam(w, m, v, parts, "adam_" + k)
            after = after + [res[k][1]]

    sizes = {k: g_small[k].size for k in _SMALL_ORDER}
    offs, o = {}, 0
    for k in _SMALL_ORDER:
        offs[k] = o
        o += sizes[k]
    rep_w = dict(norm_mix=norm_mix, b_gate=b_gate, ssm_conv_b=ssm_conv_b, dt_bias=dt_bias, A_log=A_log, D_skip=D_skip,
                 ssm_norm_w=ssm_norm_w, norm_mlp=norm_mlp, norm_final=norm_final)
    rep_m = dict(norm_mix=m_norm_mix, b_gate=m_b_gate, ssm_conv_b=m_ssm_conv_b, dt_bias=m_dt_bias, A_log=m_A_log, D_skip=m_D_skip,
                 ssm_norm_w=m_ssm_norm_w, norm_mlp=m_norm_mlp, norm_final=m_norm_final)
    rep_v = dict(norm_mix=v_norm_mix, b_gate=v_b_gate, ssm_conv_b=v_ssm_conv_b, dt_bias=v_dt_bias, A_log=v_A_log, D_skip=v_D_skip,
                 ssm_norm_w=v_ssm_norm_w, norm_mlp=v_norm_mlp, norm_final=v_norm_final)

    def pack(d):
        segs = [jnp.pad(d[k].astype(F32).reshape(-1), (0, sizes[k] - d[k].size)) if k in d else jnp.zeros((sizes[k],), F32)
                for k in _SMALL_ORDER]
        return jnp.pad(jnp.concatenate(segs), (0, rows * LANES - n_small)).reshape(rows, LANES)

    sm = _adam(pack(rep_w), pack(rep_m), pack(rep_v), small_parts, "adam_small")
    sm = [s.reshape(-1) for s in sm]
    for k in _REPLICATED:
        n_k = rep_w[k].shape[0]
        res[k] = tuple(s[offs[k]:offs[k] + n_k] for s in sm)
    loss = sm[0][offs["loss"]]
    for k, w, m, v, K, full in (("sc_conv_w", sc_conv_w, m_sc_conv_w, v_sc_conv_w, SC_K, D),
                                ("ssm_conv_w", ssm_conv_w, m_ssm_conv_w, v_ssm_conv_w, SSM_K, n_xbc)):
        g_full = sm[0][offs[k]:offs[k] + K * full].reshape(K, full)
        cw = full // N_DEV
        g_mine = lax.dynamic_slice_in_dim(g_full, me * cw, cw, axis=1)
        res[k] = _adam(w, m, v, g_mine[None], "adam_" + k)

    order = ("norm_mix", "w_in", "b_gate", "sc_conv_w", "ssm_conv_w", "ssm_conv_b", "dt_bias", "A_log", "D_skip", "ssm_norm_w",
             "w_branch_sc", "w_branch_ssm", "w_out", "norm_mlp", "w_mlp1", "w_mlp2", "norm_final")
    outs = [loss, grad_x.reshape(1, T, D)]
    for j in range(4):
        outs += [res[k][j] for k in order]
    return tuple(outs)
```

```python
import jax
import jax.numpy as jnp
from jax import lax
from jax.experimental import pallas as pl
from jax.experimental.pallas import tpu as pltpu

F32 = jnp.float32
BF16 = jnp.bfloat16

EPS = 1e-6
N_DEV = 8
HEADDIM = 64
NSTATE = 128
CHUNK = 128
NGROUPS = 8
GROUP_W = 256
SC_K = 3
SSM_K = 4
LANES = 128

ADAM_LR = 0.001
ADAM_B1 = 0.9
ADAM_B2 = 0.999
ADAM_EPS = 1e-08
ADAM_WD = 0.01
ADAM_STEP = 10

NN = (((1,), (0,)), ((), ()))
NT = (((1,), (1,)), ((), ()))
TN = (((0,), (0,)), ((), ()))
_DIMS = {"nn": NN, "nt": NT, "tn": TN}

ANY = pl.BlockSpec(memory_space=pl.ANY)
MESH = pl.DeviceIdType.MESH


def _sds(shape, dtype):
    return jax.ShapeDtypeStruct(tuple(shape), dtype)


def _dot(a, b, dims=NN):
    return lax.dot_general(a, b, dims, preferred_element_type=F32)


def _dot3(a, b, dims=NN):
    return lax.dot_general(a, b, dims, preferred_element_type=F32, precision=lax.Precision.HIGH)


def _params(*sem):
    return pltpu.CompilerParams(dimension_semantics=tuple(sem))


def _call(body, *, grid, in_specs, out_specs, out_shape, args, name, sem, scratch=(), comm=None):
    if comm is None:
        outs = pl.pallas_call(body, grid=grid, in_specs=list(in_specs), out_specs=list(out_specs), out_shape=list(out_shape),
                              scratch_shapes=list(scratch), name=name, compiler_params=_params(*sem))(*args)
        return list(outs), None
    n, n_in, n_out, n_scr = comm.n, len(in_specs), len(out_shape), len(scratch)

    def wrapped(*refs):
        ins, c_in = refs[:n_in], refs[n_in:n_in + n]
        outs, c_out = refs[n_in + n:n_in + n + n_out], refs[n_in + n + n_out:n_in + 2 * n + n_out]
        rest = refs[n_in + 2 * n + n_out:]
        scr, sems = rest[:n_scr], rest[n_scr:]
        first, last = None, None
        for d, g in enumerate(grid):
            f, l = pl.program_id(d) == 0, pl.program_id(d) == g - 1
            first, last = (f, l) if first is None else (first & f, last & l)

        @pl.when(first)
        def _():
            comm.start(c_in, c_out, sems)

        body(*ins, *outs, *scr)

        @pl.when(last)
        def _():
            comm.finish(c_in, c_out, sems)

    outs = pl.pallas_call(
        wrapped, grid=grid, in_specs=list(in_specs) + [ANY] * n, out_specs=list(out_specs) + [ANY] * n,
        out_shape=list(out_shape) + comm.out_shape, scratch_shapes=list(scratch) + comm.scratch,
        input_output_aliases={n_in + i: n_out + o for i, o in comm.aliases.items()},
        name=name, compiler_params=_params(*["arbitrary"] * len(grid)))(*args, *comm.arrs)
    return list(outs[:n_out]), list(outs[n_out:])


MM_VMEM_BUDGET = 44 * 2 ** 20


def _mm_tiles(M, N, k_bytes, mn_bytes):
    best = None
    for tm in (2048, 1024, 512, 256, 128):
        for tn in (1024, 512, 256, 128):
            if M % tm or N % tn:
                continue
            need = 2 * ((tm + tn) * k_bytes + tm * tn * mn_bytes) + 4 * tm * tn * 4
            if need <= MM_VMEM_BUDGET and (best is None or (tm * tn, tm) > (best[0] * best[1], best[0])):
                best = (tm, tn)
    assert best is not None, (M, N, k_bytes, mn_bytes)
    return best


def _mm(a, b, *, mode, name, extras=(), epi=None, out_dtypes=(F32,), comm=None):
    a_list = list(a) if isinstance(a, (list, tuple)) else [a]
    b_list = list(b) if isinstance(b, (list, tuple)) else [b]
    if mode == "nn":
        M, N = a_list[0].shape[0], b_list[0].shape[1]
    elif mode == "nt":
        M, N = a_list[0].shape[0], b_list[0].shape[0]
    else:
        M, N = a_list[0].shape[1], b_list[0].shape[1]
    k_bytes = sum((av.shape[0] if mode == "tn" else av.shape[1]) * av.dtype.itemsize for av in a_list)
    mn_bytes = sum(e.dtype.itemsize for e in extras) + sum(jnp.dtype(d).itemsize for d in out_dtypes)
    tm, tn = _mm_tiles(min(M, 2048), min(N, 1024), k_bytes, mn_bytes) if M % 128 == 0 and N % 128 == 0 else (M, N)
    assert M % tm == 0 and N % tn == 0
    a_specs, b_specs = [], []
    for av, bv in zip(a_list, b_list):
        K = av.shape[0] if mode == "tn" else av.shape[1]
        a_specs.append(pl.BlockSpec((K, tm), lambda i, j: (0, i)) if mode == "tn" else pl.BlockSpec((tm, K), lambda i, j: (i, 0)))
        b_specs.append(pl.BlockSpec((tn, K), lambda i, j: (j, 0)) if mode == "nt" else pl.BlockSpec((K, tn), lambda i, j: (0, j)))
    mn_spec = pl.BlockSpec((tm, tn), lambda i, j: (i, j))
    n_p, n_ex = len(a_list), len(extras)
    dims = _DIMS[mode]

    def body(*refs):
        acc = _dot(refs[0][...], refs[n_p][...], dims)
        for p in range(1, n_p):
            acc = acc + _dot(refs[p][...], refs[n_p + p][...], dims)
        rest = refs[2 * n_p:]
        res = (acc,) if epi is None else epi(acc, *[r[...] for r in rest[:n_ex]])
        for o_ref, r in zip(rest[n_ex:], res):
            o_ref[...] = r.astype(o_ref.dtype)

    outs, carried = _call(
        body, grid=(M // tm, N // tn), in_specs=a_specs + b_specs + [mn_spec] * n_ex,
        out_specs=[mn_spec] * len(out_dtypes), out_shape=[_sds((M, N), d) for d in out_dtypes],
        args=a_list + b_list + list(extras), name=name, sem=("parallel", "parallel"), comm=comm)
    res = outs[0] if len(outs) == 1 else outs
    return res if comm is None else (res, carried)


def _epi_add(acc, r):
    return (acc + r,)


def _epi_relu2(acc):
    p = jnp.maximum(acc, 0.0)
    return (p * p,)


def _epi_relu2_bwd(acc, r):
    return (acc * (2.0 * jnp.sqrt(r.astype(F32))),)


ROW_TILE = 512


def _row(tr, n):
    return pl.BlockSpec((tr, n), lambda i: (i, 0))


def _vec(n):
    return pl.BlockSpec((1, n), lambda i: (0, 0))


def _rms_fwd(x, w, name, comm=None):
    T, D = x.shape
    tr = min(ROW_TILE, T)

    def body(x_ref, w_ref, o_ref):
        xv = x_ref[...]
        r = lax.rsqrt(jnp.mean(xv * xv, axis=-1, keepdims=True) + EPS)
        o_ref[...] = (xv * r * w_ref[...]).astype(BF16)

    outs, carried = _call(body, grid=(T // tr,), in_specs=[_row(tr, D), _vec(D)], out_specs=[_row(tr, D)],
                          out_shape=[_sds((T, D), BF16)], args=[x, w], name=name, sem=("parallel",), comm=comm)
    return outs[0] if comm is None else (outs[0], carried)


def _rms_bwd(x, w, dh, dres, name):
    T, D = x.shape
    tr = min(ROW_TILE, T)

    def body(x_ref, w_ref, dh_ref, dres_ref, dx_ref, dxb_ref, dw_ref):
        @pl.when(pl.program_id(0) == 0)
        def _():
            dw_ref[...] = jnp.zeros_like(dw_ref)

        xv = x_ref[...]
        r = lax.rsqrt(jnp.mean(xv * xv, axis=-1, keepdims=True) + EPS)
        xh = xv * r
        dh_v = dh_ref[...]
        dw_ref[...] += jnp.sum(dh_v * xh, axis=0, keepdims=True)
        dxh = dh_v * w_ref[...]
        dx = r * (dxh - xh * jnp.mean(dxh * xh, axis=-1, keepdims=True)) + dres_ref[...]
        dx_ref[...] = dx
        dxb_ref[...] = dx.astype(BF16)

    return pl.pallas_call(
        body, grid=(T // tr,), in_specs=[_row(tr, D), _vec(D), _row(tr, D), _row(tr, D)],
        out_specs=[_row(tr, D), _row(tr, D), _vec(D)],
        out_shape=[_sds((T, D), F32), _sds((T, D), BF16), _sds((1, D), F32)],
        name=name, compiler_params=_params("arbitrary"))(x, w, dh, dres)


def _final(x2, w, tgt, name):
    T, D = x2.shape
    tr = min(ROW_TILE, T)

    def body(x_ref, w_ref, t_ref, dx_ref, dxb_ref, dw_ref, loss_ref):
        @pl.when(pl.program_id(0) == 0)
        def _():
            dw_ref[...] = jnp.zeros_like(dw_ref)
            loss_ref[...] = jnp.zeros_like(loss_ref)

        xv = x_ref[...]
        wv = w_ref[...]
        r = lax.rsqrt(jnp.mean(xv * xv, axis=-1, keepdims=True) + EPS)
        xh = xv * r
        err = xh * wv - t_ref[...]
        part = jnp.sum(jnp.sum(err * err, axis=1, keepdims=True), axis=0, keepdims=True) * (0.5 / D)
        loss_ref[...] += jnp.broadcast_to(part, loss_ref.shape)
        dy = err * (1.0 / D)
        dw_ref[...] += jnp.sum(dy * xh, axis=0, keepdims=True)
        dxh = dy * wv
        dx = r * (dxh - xh * jnp.mean(dxh * xh, axis=-1, keepdims=True))
        dx_ref[...] = dx
        dxb_ref[...] = dx.astype(BF16)

    return pl.pallas_call(
        body, grid=(T // tr,), in_specs=[_row(tr, D), _vec(D), _row(tr, D)],
        out_specs=[_row(tr, D), _row(tr, D), _vec(D), _vec(LANES)],
        out_shape=[_sds((T, D), F32), _sds((T, D), BF16), _sds((1, D), F32), _sds((1, LANES), F32)],
        name=name, compiler_params=_params("arbitrary"))(x2, w, tgt)


def _silu_parts(z):
    s = jax.nn.sigmoid(z)
    return z * s, s * (1.0 + z * (1.0 - s))


def _gnorm_fwd(y, z, w, name, comm=None):
    T, N = y.shape
    tr = min(ROW_TILE, T)

    def body(y_ref, z_ref, w_ref, o_ref):
        for g in range(N // GROUP_W):
            sl = slice(g * GROUP_W, (g + 1) * GROUP_W)
            silu, _ = _silu_parts(z_ref[:, sl])
            yz = y_ref[:, sl] * silu
            r = lax.rsqrt(jnp.mean(yz * yz, axis=-1, keepdims=True) + EPS)
            o_ref[:, sl] = (yz * r * w_ref[:, sl]).astype(BF16)

    outs, carried = _call(body, grid=(T // tr,), in_specs=[_row(tr, N), _row(tr, N), _vec(N)], out_specs=[_row(tr, N)],
                          out_shape=[_sds((T, N), BF16)], args=[y, z, w], name=name, sem=("parallel",), comm=comm)
    return outs[0] if comm is None else (outs[0], carried)


def _gnorm_bwd(y, z, w, dyb, name):
    T, N = y.shape
    tr = min(ROW_TILE, T)

    def body(y_ref, z_ref, w_ref, d_ref, dy_ref, dz_ref, dw_ref):
        @pl.when(pl.program_id(0) == 0)
        def _():
            dw_ref[...] = jnp.zeros_like(dw_ref)

        for g in range(N // GROUP_W):
            sl = slice(g * GROUP_W, (g + 1) * GROUP_W)
            yv = y_ref[:, sl]
            silu, dsilu = _silu_parts(z_ref[:, sl])
            yz = yv * silu
            r = lax.rsqrt(jnp.mean(yz * yz, axis=-1, keepdims=True) + EPS)
            yzh = yz * r
            d = d_ref[:, sl]
            dw_ref[:, sl] += jnp.sum(d * yzh, axis=0, keepdims=True)
            dyzh = d * w_ref[:, sl]
            dyz = r * (dyzh - yzh * jnp.mean(dyzh * yzh, axis=-1, keepdims=True))
            dy_ref[:, sl] = dyz * silu
            dz_ref[:, sl] = (dyz * yv * dsilu).astype(BF16)

    return pl.pallas_call(
        body, grid=(T // tr,), in_specs=[_row(tr, N), _row(tr, N), _vec(N), _row(tr, N)],
        out_specs=[_row(tr, N), _row(tr, N), _vec(N)],
        out_shape=[_sds((T, N), F32), _sds((T, N), BF16), _sds((1, N), F32)],
        name=name, compiler_params=_params("arbitrary"))(y, z, w, dyb)


def _merge_fwd(gate_raw, b_gate, br_a, br_b, name):
    T, D = br_a.shape
    tr = min(ROW_TILE, T)

    def body(g_ref, bg_ref, a_ref, b_ref, o_ref):
        g = jax.nn.sigmoid(g_ref[...] + bg_ref[...])
        o_ref[...] = (g[:, :D] * a_ref[...] + g[:, D:] * b_ref[...]).astype(BF16)

    return pl.pallas_call(body, grid=(T // tr,), in_specs=[_row(tr, 2 * D), _vec(2 * D), _row(tr, D), _row(tr, D)],
                          out_specs=_row(tr, D), out_shape=_sds((T, D), BF16), name=name,
                          compiler_params=_params("parallel"))(gate_raw, b_gate, br_a, br_b)


def _merge_bwd(dmerged, gate_raw, b_gate, br_a, br_b, name):
    T, D = br_a.shape
    tr = min(ROW_TILE, T)

    def body(d_ref, g_ref, bg_ref, a_ref, b_ref, da_ref, db_ref, dg_ref, dbg_ref):
        @pl.when(pl.program_id(0) == 0)
        def _():
            dbg_ref[...] = jnp.zeros_like(dbg_ref)

        g = jax.nn.sigmoid(g_ref[...] + bg_ref[...])
        d = d_ref[...]
        da_ref[...] = (d * g[:, :D]).astype(BF16)
        db_ref[...] = (d * g[:, D:]).astype(BF16)
        dg = jnp.concatenate([d * a_ref[...], d * b_ref[...]], axis=1) * g * (1.0 - g)
        dg_ref[...] = dg.astype(BF16)
        dbg_ref[...] += jnp.sum(dg, axis=0, keepdims=True)

    return pl.pallas_call(
        body, grid=(T // tr,), in_specs=[_row(tr, D), _row(tr, 2 * D), _vec(2 * D), _row(tr, D), _row(tr, D)],
        out_specs=[_row(tr, D), _row(tr, D), _row(tr, 2 * D), _vec(2 * D)],
        out_shape=[_sds((T, D), BF16), _sds((T, D), BF16), _sds((T, 2 * D), BF16), _sds((1, 2 * D), F32)],
        name=name, compiler_params=_params("arbitrary"))(dmerged, gate_raw, b_gate, br_a, br_b)


CB_W = 256
CONV_ROWS = 32
CONV_PAD = 8


def _rows_down(load, r0, s):
    if s == 0:
        return load(r0, r0 + CONV_ROWS)
    if r0 == 0:
        row = lax.broadcasted_iota(jnp.int32, (CONV_ROWS, CB_W), 0)
        return jnp.where(row >= s, pltpu.roll(load(0, CONV_ROWS), s, 0), 0.0)
    return load(r0 - s, r0 - s + CONV_ROWS)


def _conv_tile(load, taps, r0):
    K = len(taps)
    us = [_rows_down(load, r0, K - 1 - k) for k in range(K)]
    acc = us[K - 1] * taps[K - 1]
    for k in range(K - 1):
        acc = acc + us[k] * taps[k]
    return acc, us


def _conv_back_tile(scr, taps, r0):
    K = len(taps)
    du = scr[r0:r0 + CONV_ROWS, :] * taps[K - 1]
    for k in range(K - 1):
        s = K - 1 - k
        du = du + scr[r0 + s:r0 + s + CONV_ROWS, :] * taps[k]
    return du


def _fold8(v):
    return jnp.sum(v.reshape(CONV_ROWS // 8, 8, v.shape[1]), axis=0)


def _col(T, j0=0):
    return pl.BlockSpec((T, CB_W), lambda j: (0, j + j0))


def _sc_fwd(psc, w, name):
    T, D = psc.shape[0], psc.shape[1] // 3
    nb = D // CB_W

    def body(b_ref, c_ref, x_ref, w_ref, o_ref):
        taps = [w_ref[k:k + 1, :] for k in range(SC_K)]
        load = lambda a, b: c_ref[a:b, :] * x_ref[a:b, :]
        for r0 in range(0, T, CONV_ROWS):
            cu, _ = _conv_tile(load, taps, r0)
            o_ref[r0:r0 + CONV_ROWS, :] = (b_ref[r0:r0 + CONV_ROWS, :] * cu).astype(BF16)

    return pl.pallas_call(
        body, grid=(nb,), in_specs=[_col(T), _col(T, nb), _col(T, 2 * nb), pl.BlockSpec((SC_K, CB_W), lambda j: (0, j))],
        out_specs=_col(T), out_shape=_sds((T, D), BF16), name=name, compiler_params=_params("parallel"))(psc, psc, psc, w)


def _sc_bwd(psc, w, dya, name):
    T, D = psc.shape[0], psc.shape[1] // 3
    nb = D // CB_W

    def body(b_ref, c_ref, x_ref, w_ref, d_ref, db_ref, dc_ref, dx_ref, dw_ref, scr):
        taps = [w_ref[k:k + 1, :] for k in range(SC_K)]
        load = lambda a, b: c_ref[a:b, :] * x_ref[a:b, :]
        scr[T:T + CONV_PAD, :] = jnp.zeros((CONV_PAD, CB_W), F32)
        dw8 = [jnp.zeros((8, CB_W), F32)] * SC_K
        for r0 in range(0, T, CONV_ROWS):
            rows = slice(r0, r0 + CONV_ROWS)
            cu, us = _conv_tile(load, taps, r0)
            d = d_ref[rows, :]
            db_ref[rows, :] = (d * cu).astype(BF16)
            dcu = d * b_ref[rows, :]
            scr[rows, :] = dcu
            dw8 = [acc + _fold8(dcu * u) for acc, u in zip(dw8, us)]
        for k in range(SC_K):
            dw_ref[k:k + 1, :] = jnp.sum(dw8[k], axis=0, keepdims=True)
        for r0 in range(0, T, CONV_ROWS):
            rows = slice(r0, r0 + CONV_ROWS)
            du = _conv_back_tile(scr, taps, r0)
            dc_ref[rows, :] = (du * x_ref[rows, :]).astype(BF16)
            dx_ref[rows, :] = (du * c_ref[rows, :]).astype(BF16)

    wspec = pl.BlockSpec((SC_K, CB_W), lambda j: (0, j))
    return pl.pallas_call(
        body, grid=(nb,), in_specs=[_col(T), _col(T, nb), _col(T, 2 * nb), wspec, _col(T)],
        out_specs=[_col(T), _col(T), _col(T), wspec],
        out_shape=[_sds((T, D), BF16)] * 3 + [_sds((SC_K, D), F32)],
        scratch_shapes=[pltpu.VMEM((T + CONV_PAD, CB_W), F32)],
        name=name, compiler_params=_params("parallel"))(psc, psc, psc, w, dya)


def _ssm_conv_fwd(u, w, b, name, comm=None):
    T, N = u.shape

    def body(u_ref, w_ref, b_ref, o_ref):
        taps = [w_ref[k:k + 1, :] for k in range(SSM_K)]
        bias = b_ref[...]
        for r0 in range(0, T, CONV_ROWS):
            c, _ = _conv_tile(lambda a, b: u_ref[a:b, :], taps, r0)
            c = c + bias
            o_ref[r0:r0 + CONV_ROWS, :] = c * jax.nn.sigmoid(c)

    outs, carried = _call(
        body, grid=(N // CB_W,), in_specs=[_col(T), pl.BlockSpec((SSM_K, CB_W), lambda j: (0, j)), pl.BlockSpec((1, CB_W), lambda j: (0, j))],
        out_specs=[_col(T)], out_shape=[_sds((T, N), F32)], args=[u, w, b], name=name, sem=("parallel",), comm=comm)
    return outs[0] if comm is None else (outs[0], carried)


def _ssm_conv_bwd(u, w, b, dxs, dB, dC, name, comm=None):
    T, N = u.shape
    n_x, n_b = dxs.shape[1] // CB_W, dB.shape[1] // CB_W

    def body(u_ref, w_ref, b_ref, dx_ref, db_ref, dc_ref, du_ref, dw_ref, dbias_ref, scr):
        j = pl.program_id(0)
        taps = [w_ref[k:k + 1, :] for k in range(SSM_K)]
        bias = b_ref[...]
        scr[T:T + CONV_PAD, :] = jnp.zeros((CONV_PAD, CB_W), F32)
        dw8 = [jnp.zeros((8, CB_W), F32)] * SSM_K
        db8 = jnp.zeros((8, CB_W), F32)
        for r0 in range(0, T, CONV_ROWS):
            rows = slice(r0, r0 + CONV_ROWS)
            c, us = _conv_tile(lambda a, b: u_ref[a:b, :], taps, r0)
            _, dsilu = _silu_parts(c + bias)
            d = jnp.where(j < n_x, dx_ref[rows, :], jnp.where(j < n_x + n_b, db_ref[rows, :], dc_ref[rows, :])) * dsilu
            scr[rows, :] = d
            db8 = db8 + _fold8(d)
            dw8 = [acc + _fold8(d * u) for acc, u in zip(dw8, us)]
        dbias_ref[...] = jnp.sum(db8, axis=0, keepdims=True)
        for k in range(SSM_K):
            dw_ref[k:k + 1, :] = jnp.sum(dw8[k], axis=0, keepdims=True)
        for r0 in range(0, T, CONV_ROWS):
            du_ref[r0:r0 + CONV_ROWS, :] = _conv_back_tile(scr, taps, r0).astype(BF16)

    wspec = pl.BlockSpec((SSM_K, CB_W), lambda j: (0, j))
    bspec = pl.BlockSpec((1, CB_W), lambda j: (0, j))
    outs, carried = _call(
        body, grid=(N // CB_W,),
        in_specs=[_col(T), wspec, bspec,
                  pl.BlockSpec((T, CB_W), lambda j: (0, jnp.minimum(j, n_x - 1))),
                  pl.BlockSpec((T, CB_W), lambda j: (0, jnp.clip(j - n_x, 0, n_b - 1))),
                  pl.BlockSpec((T, CB_W), lambda j: (0, jnp.clip(j - n_x - n_b, 0, n_b - 1)))],
        out_specs=[_col(T), wspec, bspec],
        out_shape=[_sds((T, N), BF16), _sds((SSM_K, N), F32), _sds((1, N), F32)],
        scratch=[pltpu.VMEM((T + CONV_PAD, CB_W), F32)],
        args=[u, w, b, dxs, dB, dC], name=name, sem=("parallel",), comm=comm)
    return outs if comm is None else (outs, carried)


def _split3(v):
    hi = v.astype(BF16)
    r = v - hi.astype(F32)
    mid = r.astype(BF16)
    lo = (r - mid.astype(F32)).astype(BF16)
    return hi, mid, lo


def _head_expand(n_lanes):
    h = lax.broadcasted_iota(jnp.int32, (LANES, n_lanes), 0)
    l = lax.broadcasted_iota(jnp.int32, (LANES, n_lanes), 1)
    return (jnp.right_shift(l, HEADDIM.bit_length() - 1) == h).astype(BF16)


def _softplus(v):
    return jnp.maximum(v, 0.0) + jnp.log1p(jnp.exp(-jnp.abs(v)))


def _ssd_prep(dt_raw, dt_bias, a_log, n_inner, name):
    T = dt_raw.shape[0]

    def body(r_ref, b_ref, al_ref, ex_ref, dt_ref, cs_ref):
        dt = _softplus(r_ref[...] + b_ref[...])
        a = dt * (-jnp.exp(al_ref[...]))
        i = lax.broadcasted_iota(jnp.int32, (CHUNK, CHUNK), 0)
        j = lax.broadcasted_iota(jnp.int32, (CHUNK, CHUNK), 1)
        tri = (j <= i).astype(BF16)
        cs = sum(_dot(tri, p) for p in _split3(a))
        ex = ex_ref[...]
        dt_ref[...] = sum(_dot(p, ex) for p in _split3(dt))
        cs_ref[...] = sum(_dot(p, ex) for p in _split3(cs))

    blk = pl.BlockSpec((CHUNK, LANES), lambda c: (c, 0))
    out = pl.BlockSpec((CHUNK, n_inner), lambda c: (c, 0))
    ex_spec = pl.BlockSpec((LANES, n_inner), lambda c: (0, 0))
    return pl.pallas_call(body, grid=(T // CHUNK,), in_specs=[blk, _vec(LANES), _vec(LANES), ex_spec], out_specs=[out, out],
                          out_shape=[_sds((T, n_inner), F32)] * 2, name=name,
                          compiler_params=_params("parallel"))(dt_raw, dt_bias, a_log, _head_expand(n_inner))


def _pair_terms(cs_p):
    lane = lax.broadcasted_iota(jnp.int32, (CHUNK, CHUNK), 1)
    sub = lax.broadcasted_iota(jnp.int32, (CHUNK, CHUNK), 0)
    csT = cs_p.T
    Ls = []
    for k in range(2):
        col = jnp.sum(jnp.where(lane == k * HEADDIM, cs_p, 0.0), axis=1, keepdims=True)
        rowv = csT[k * HEADDIM:k * HEADDIM + 1, :]
        Ls.append(jnp.exp(jnp.where(sub >= lane, col - rowv, -jnp.inf)))
    return Ls, jnp.exp(csT[:, CHUNK - 1:CHUNK])


def _block_diag(xp):
    lane = lax.broadcasted_iota(jnp.int32, xp.shape, 1)
    return jnp.concatenate([jnp.where(lane < HEADDIM, xp, 0.0), jnp.where(lane >= HEADDIM, xp, 0.0)], axis=0)


SSD_GROUPS_PER_STEP = 8


def _ssd_specs(T, n_inner):
    nc, gs = T // CHUNK, SSD_GROUPS_PER_STEP
    bo, co = n_inner // (gs * NSTATE), (n_inner + NGROUPS * NSTATE) // (gs * NSTATE)
    assert NGROUPS % gs == 0 and n_inner % (gs * NSTATE) == 0 and (NGROUPS * NSTATE) % (gs * NSTATE) == 0
    g_blk = lambda f: pl.BlockSpec((CHUNK, gs * GROUP_W), lambda c, s: (f(c), s))
    b_blk = lambda f: pl.BlockSpec((CHUNK, gs * NSTATE), lambda c, s: (f(c), bo + s))
    c_blk = lambda f: pl.BlockSpec((CHUNK, gs * NSTATE), lambda c, s: (f(c), co + s))
    return nc, g_blk, b_blk, c_blk


def _ssd_fwd(xbc, dt_e, cs_e, d_e, name, comm=None):
    T = xbc.shape[0]
    n_inner = dt_e.shape[1]
    nc, g_blk, b_blk, c_blk = _ssd_specs(T, n_inner)
    ident = lambda c: c

    gs = SSD_GROUPS_PER_STEP

    def body(xs_ref, b_ref, c_ref, dt_ref, cs_ref, d_ref, y_ref, p_ref, st):
        c, s = pl.program_id(0), pl.program_id(1)

        @pl.when(c == 0)
        def _():
            for gi in range(gs):
                st[s * gs + gi] = jnp.zeros((GROUP_W, NSTATE), F32)

        for gi in range(gs):
            g = s * gs + gi
            gw, gn = slice(gi * GROUP_W, (gi + 1) * GROUP_W), slice(gi * NSTATE, (gi + 1) * NSTATE)
            P = st[g]
            p_ref[0, gi] = P
            xs, dt, cs = xs_ref[:, gw], dt_ref[:, gw], cs_ref[:, gw]
            Bf, Cf = b_ref[:, gn], c_ref[:, gn]
            Cb = Cf.astype(BF16)
            CBm = _dot(Cb, Bf.astype(BF16), NT)
            X = xs * dt
            decay = jnp.exp(cs[CHUNK - 1:CHUNK, :] - cs)
            y_off = _dot(Cb, P.astype(BF16), NT) * jnp.exp(cs)
            ys, ecl = [], []
            for pr in range(2):
                sl = slice(pr * LANES, (pr + 1) * LANES)
                Ls, e_last = _pair_terms(cs[:, sl])
                ecl.append(e_last)
                Mcat = jnp.concatenate([(CBm * L).astype(BF16) for L in Ls], axis=1)
                ys.append(_dot(Mcat, _block_diag(X[:, sl]).astype(BF16)))
            y_ref[:, gw] = jnp.concatenate(ys, axis=1) + y_off + xs * d_ref[:, gw]
            S = _dot3(X * decay, Bf, TN)
            st[g] = P * jnp.concatenate(ecl, axis=0) + S

    p_blk = pl.BlockSpec((1, gs, GROUP_W, NSTATE), lambda c, s: (c, s, 0, 0))
    outs, carried = _call(
        body, grid=(nc, NGROUPS // gs),
        in_specs=[g_blk(ident), b_blk(ident), c_blk(ident), g_blk(ident), g_blk(ident), pl.BlockSpec((1, gs * GROUP_W), lambda c, s: (0, s))],
        out_specs=[g_blk(ident), p_blk],
        out_shape=[_sds((T, n_inner), F32), _sds((nc, NGROUPS, GROUP_W, NSTATE), F32)],
        scratch=[pltpu.VMEM((NGROUPS, GROUP_W, NSTATE), F32)],
        args=[xbc, xbc, xbc, dt_e, cs_e, d_e], name=name, sem=("arbitrary", "arbitrary"), comm=comm)
    return outs if comm is None else (outs, carried)


def _ssd_bwd(xbc, dt_e, cs_e, d_e, states, dy, name, comm=None):
    T = xbc.shape[0]
    n_inner = dt_e.shape[1]
    nc, g_blk, b_blk, c_blk = _ssd_specs(T, n_inner)
    rev = lambda c: nc - 1 - c

    gs = SSD_GROUPS_PER_STEP

    def body(xs_ref, b_ref, c_ref, dt_ref, cs_ref, d_ref, p_ref, pn_ref, dy_ref,
             dxs_ref, db_ref, dc_ref, ddt_ref, dcs_ref, dd_ref, dst):
        cc, s = pl.program_id(0), pl.program_id(1)

        @pl.when(cc == 0)
        def _():
            for gi in range(gs):
                dst[s * gs + gi] = jnp.zeros((GROUP_W, NSTATE), F32)

        for gi in range(gs):
            one_group(s * gs + gi, gi, xs_ref, b_ref, c_ref, dt_ref, cs_ref, d_ref, p_ref, pn_ref, dy_ref,
                      dxs_ref, db_ref, dc_ref, ddt_ref, dcs_ref, dd_ref, dst)

    def one_group(g, gi, xs_ref, b_ref, c_ref, dt_ref, cs_ref, d_ref, p_ref, pn_ref, dy_ref,
                  dxs_ref, db_ref, dc_ref, ddt_ref, dcs_ref, dd_ref, dst):
        gw, gn = slice(gi * GROUP_W, (gi + 1) * GROUP_W), slice(gi * NSTATE, (gi + 1) * NSTATE)
        dS = dst[g]
        P, Pn = p_ref[0, gi], pn_ref[0, gi]
        xs, dt, cs, dY = xs_ref[:, gw], dt_ref[:, gw], cs_ref[:, gw], dy_ref[:, gw]
        Bf, Cf = b_ref[:, gn], c_ref[:, gn]
        Bb, Cb = Bf.astype(BF16), Cf.astype(BF16)
        X = xs * dt
        ecs = jnp.exp(cs)
        decay = jnp.exp(cs[CHUNK - 1:CHUNK, :] - cs)
        CBm = _dot3(Cf, Bf, NT)
        dYe = dY * ecs
        dP_off = _dot3(dYe, Cf, TN)
        dC = _dot(dYe.astype(BF16), P.astype(BF16))
        dcs = dYe * _dot3(Cf, P, NT)
        Xd = X * decay
        dB = _dot(Xd.astype(BF16), dS.astype(BF16))
        E = _dot3(Bf, dS, NT)
        dX = E * decay
        dcs = dcs - E * Xd
        R = _dot3(jnp.ones((8, NSTATE), F32), dS * Pn, NT)
        sub_g = lax.broadcasted_iota(jnp.int32, (CHUNK, GROUP_W), 0)
        dcs = dcs + jnp.where(sub_g == CHUNK - 1, R[0:1, :], 0.0)
        lane = lax.broadcasted_iota(jnp.int32, (CHUNK, CHUNK), 1)
        sub = lax.broadcasted_iota(jnp.int32, (CHUNK, CHUNK), 0)
        dCB = jnp.zeros((CHUNK, CHUNK), F32)
        dXs, dcss, ecl = [], [], []
        for pr in range(2):
            sl = slice(pr * LANES, (pr + 1) * LANES)
            Ls, e_last = _pair_terms(cs[:, sl])
            ecl.append(e_last)
            dYpb = dY[:, sl].astype(BF16)
            dMcat = _dot(dYpb, _block_diag(X[:, sl]).astype(BF16), NT)
            Mcat = jnp.concatenate([(CBm * L).astype(BF16) for L in Ls], axis=1)
            dXt = _dot(Mcat, dYpb, TN)
            dXs.append(jnp.where(lane < HEADDIM, dXt[:CHUNK], dXt[CHUNK:]))
            colacc = jnp.zeros((CHUNK, CHUNK), F32)
            rowacc = jnp.zeros((CHUNK, CHUNK), F32)
            for k in range(2):
                dG = dMcat[:, k * CHUNK:(k + 1) * CHUNK] * Ls[k]
                dCB = dCB + dG
                Q = dG * CBm
                colacc = colacc + jnp.where(lane == k * HEADDIM, jnp.sum(Q, axis=1, keepdims=True), 0.0)
                rowacc = rowacc + jnp.where(sub == k * HEADDIM, jnp.sum(Q, axis=0, keepdims=True), 0.0)
            dcss.append(colacc - rowacc.T)
        dX = dX + jnp.concatenate(dXs, axis=1)
        dcs = dcs + jnp.concatenate(dcss, axis=1)
        dCBb = dCB.astype(BF16)
        dc_ref[:, gn] = dC + _dot(dCBb, Bb)
        db_ref[:, gn] = dB + _dot(dCBb, Cb, TN)
        dxs_ref[:, gw] = dX * dt + dY * d_ref[:, gw]
        ddt_ref[:, gw] = dX * xs
        dcs_ref[:, gw] = dcs
        dd_ref[0, :, gw] = jnp.sum(dY * xs, axis=0, keepdims=True)
        dst[g] = dS * jnp.concatenate(ecl, axis=0) + dP_off

    p_blk = pl.BlockSpec((1, gs, GROUP_W, NSTATE), lambda c, s: (nc - 1 - c, s, 0, 0))
    pn_blk = pl.BlockSpec((1, gs, GROUP_W, NSTATE), lambda c, s: (jnp.minimum(nc - c, nc - 1), s, 0, 0))
    st_blk = pl.BlockSpec((CHUNK, gs * NSTATE), lambda c, s: (nc - 1 - c, s))
    outs, carried = _call(
        body, grid=(nc, NGROUPS // gs),
        in_specs=[g_blk(rev), b_blk(rev), c_blk(rev), g_blk(rev), g_blk(rev), pl.BlockSpec((1, gs * GROUP_W), lambda c, s: (0, s)),
                  p_blk, pn_blk, g_blk(rev)],
        out_specs=[g_blk(rev), st_blk, st_blk, g_blk(rev), g_blk(rev), pl.BlockSpec((1, 1, gs * GROUP_W), lambda c, s: (nc - 1 - c, 0, s))],
        out_shape=[_sds((T, n_inner), F32), _sds((T, NGROUPS * NSTATE), F32), _sds((T, NGROUPS * NSTATE), F32),
                   _sds((T, n_inner), F32), _sds((T, n_inner), F32), _sds((nc, 1, n_inner), F32)],
        scratch=[pltpu.VMEM((NGROUPS, GROUP_W, NSTATE), F32)],
        args=[xbc, xbc, xbc, dt_e, cs_e, d_e, states, states, dy], name=name, sem=("arbitrary", "arbitrary"), comm=comm)
    return outs if comm is None else (outs, carried)


def _ssd_post(ddt_e, dcs_e, dd_p, dt_raw, dt_bias, a_log, n_heads, name):
    T, n_inner = ddt_e.shape

    def body(ddt_ref, dcs_ref, dd_ref, r_ref, b_ref, al_ref, ex_ref, draw_ref, dbias_ref, dal_ref, ddsk_ref):
        @pl.when(pl.program_id(0) == 0)
        def _():
            dbias_ref[...] = jnp.zeros_like(dbias_ref)
            dal_ref[...] = jnp.zeros_like(dal_ref)
            ddsk_ref[...] = jnp.zeros_like(ddsk_ref)

        spread = [ddt_ref[...], dcs_ref[...], jnp.broadcast_to(dd_ref[0], (8, n_inner))]
        stacked = _dot(jnp.concatenate([p for v in spread for p in _split3(v)], axis=0), ex_ref[...], NT)
        sums, r0 = [], 0
        for v in spread:
            n = v.shape[0]
            sums.append(stacked[r0:r0 + n] + stacked[r0 + n:r0 + 2 * n] + stacked[r0 + 2 * n:r0 + 3 * n])
            r0 += 3 * n
        ddt_h, dcs_h, dd_h = sums
        raw = r_ref[...] + b_ref[...]
        dt = _softplus(raw)
        A = -jnp.exp(al_ref[...])
        i = lax.broadcasted_iota(jnp.int32, (CHUNK, CHUNK), 0)
        j = lax.broadcasted_iota(jnp.int32, (CHUNK, CHUNK), 1)
        upper = (j >= i).astype(BF16)
        da = sum(_dot(upper, p) for p in _split3(dcs_h))
        ddt = ddt_h + da * A
        lane = lax.broadcasted_iota(jnp.int32, (CHUNK, LANES), 1)
        draw = jnp.where(lane < n_heads, ddt * jax.nn.sigmoid(raw), 0.0)
        draw_ref[...] = draw.astype(BF16)
        dbias_ref[...] += jnp.sum(draw, axis=0, keepdims=True)
        dal_ref[...] += jnp.sum(da * dt, axis=0, keepdims=True) * A
        ddsk_ref[...] += dd_h[0:1, :]

    wide = pl.BlockSpec((CHUNK, n_inner), lambda c: (c, 0))
    blk = pl.BlockSpec((CHUNK, LANES), lambda c: (c, 0))
    return pl.pallas_call(
        body, grid=(T // CHUNK,),
        in_specs=[wide, wide, pl.BlockSpec((1, 1, n_inner), lambda c: (c, 0, 0)), blk, _vec(LANES), _vec(LANES),
                  pl.BlockSpec((LANES, n_inner), lambda c: (0, 0))],
        out_specs=[blk, _vec(LANES), _vec(LANES), _vec(LANES)],
        out_shape=[_sds((T, LANES), BF16)] + [_sds((1, LANES), F32)] * 3,
        name=name, compiler_params=_params("arbitrary"))(ddt_e, dcs_e, dd_p, dt_raw, dt_bias, a_log, _head_expand(n_inner))


def _row2(v):
    return v.reshape(1, -1).astype(F32)


def _pad_lanes(v):
    return jnp.pad(_row2(v), ((0, 0), (0, LANES - v.shape[-1])))


class _NoExchange:
    def __init__(self, W):
        self.W, self.grads = W, {}

    def weight(self, k):
        return self.W[k]

    def carry(self, name):
        return None

    def carried(self, name, outs):
        pass

    def grad(self, k, g):
        self.grads[k] = g

    def tok(self):
        return jnp.zeros((), F32)

    def point(self, name, value):
        pass


def _local_step(x, tgt, S, small):
    T, D = x.shape

    def mm(a, b, *, name, **kw):
        comm = S.carry(name)
        if comm is None:
            return _mm(a, b, name=name, **kw)
        res, outs = _mm(a, b, name=name, comm=comm, **kw)
        S.carried(name, outs)
        return res

    def carrying(fn, *args, name):
        comm = S.carry(name)
        if comm is None:
            return fn(*args, name)
        res, outs = fn(*args, name, comm=comm)
        S.carried(name, outs)
        return res

    n_inner = 2 * D
    n_heads = n_inner // HEADDIM
    norm_mix, norm_mlp, norm_final = _row2(small["norm_mix"]), _row2(small["norm_mlp"]), _row2(small["norm_final"])
    b_gate, ssm_b, ssm_norm_w = _row2(small["b_gate"]), _row2(small["ssm_conv_b"]), _row2(small["ssm_norm_w"])
    dt_bias, a_log = _pad_lanes(small["dt_bias"]), _pad_lanes(small["A_log"])
    d_e = jnp.repeat(small["D_skip"].astype(F32), HEADDIM).reshape(1, n_inner)

    hb = carrying(_rms_fwd, x, norm_mix, name="rms_mix")
    sc_w, ssm_w = S.weight("sc_conv_w"), S.weight("ssm_conv_w")
    p_xbc = mm(hb, S.weight("xbc"), mode="nn", name="proj_xbc")
    p_dt = mm(hb, S.weight("dt"), mode="nn", name="proj_dt")
    p_z = mm(hb, S.weight("z"), mode="nn", name="proj_z")
    p_sc = mm(hb, S.weight("sc"), mode="nn", name="proj_sc")
    p_gate = mm(hb, S.weight("gate"), mode="nn", name="proj_gate")
    xbc = carrying(_ssm_conv_fwd, p_xbc, ssm_w, ssm_b, name="ssm_conv_fwd")
    dt_e, cs_e = _ssd_prep(p_dt, dt_bias, a_log, n_inner, "ssd_prep")
    ya = _sc_fwd(p_sc, sc_w, "sc_fwd")
    y, states = carrying(_ssd_fwd, xbc, dt_e, cs_e, d_e, name="ssd_fwd")
    S.point("mixers_done", [y, ya, p_gate])
    yb = carrying(_gnorm_fwd, y, p_z, ssm_norm_w, name="gnorm_fwd")
    br_a = mm(ya, S.weight("bsc"), mode="nn", name="branch_sc")
    br_b = mm(yb, S.weight("bssm"), mode="nn", name="branch_ssm")
    merged = _merge_fwd(p_gate, b_gate, br_a, br_b, "merge_fwd")
    x1 = mm(merged, S.weight("out"), mode="nn", name="out_proj", extras=(x,), epi=_epi_add)
    h2 = _rms_fwd(x1, norm_mlp, "rms_mlp")
    r_act = mm(h2, S.weight("w1"), mode="nn", name="mlp_up", epi=_epi_relu2, out_dtypes=(BF16,))
    x2 = mm(r_act, S.weight("w2"), mode="nn", name="mlp_down", extras=(x1,), epi=_epi_add)
    dx2, dx2b, g_norm_final, loss_row = _final(x2, norm_final, tgt, "final")

    S.grad("w2", mm(r_act, dx2b, mode="tn", name="mlp_down_dw", out_dtypes=(BF16,)))
    da = mm(dx2b, S.weight("w2"), mode="nt", name="mlp_down_dx", extras=(r_act,), epi=_epi_relu2_bwd, out_dtypes=(BF16,))
    S.grad("w1", mm(h2, da, mode="tn", name="mlp_up_dw", out_dtypes=(BF16,)))
    dh2 = mm(da, S.weight("w1"), mode="nt", name="mlp_up_dx")
    dx1, dx1b, g_norm_mlp = _rms_bwd(x1, norm_mlp + S.tok(), dh2, dx2, "rms_mlp_bwd")
    S.grad("out", mm(merged, dx1b, mode="tn", name="out_proj_dw", out_dtypes=(BF16,)))
    dmerged = mm(dx1b, S.weight("out"), mode="nt", name="out_proj_dx")
    dbr_a, dbr_b, d_gate, g_b_gate = _merge_bwd(dmerged, p_gate, b_gate, br_a, br_b, "merge_bwd")
    S.grad("bssm", mm(yb, dbr_b, mode="tn", name="branch_ssm_dw", out_dtypes=(BF16,)))
    S.grad("bsc", mm(ya, dbr_a, mode="tn", name="branch_sc_dw", out_dtypes=(BF16,)))
    dyb = mm(dbr_b, S.weight("bssm"), mode="nt", name="branch_ssm_dx")
    dya = mm(dbr_a, S.weight("bsc"), mode="nt", name="branch_sc_dx")
    dy, d_z, g_ssm_norm_w = _gnorm_bwd(y, p_z, ssm_norm_w + S.tok(), dyb, "gnorm_bwd")
    dxs, dB, dC, ddt_e, dcs_e, dd_p = carrying(_ssd_bwd, xbc, dt_e, cs_e, d_e, states, dy, name="ssd_bwd")
    d_dt, g_dt_bias, g_a_log, g_d_skip = _ssd_post(ddt_e, dcs_e, dd_p, p_dt, dt_bias, a_log, n_heads, "ssd_post")
    d_xbc, g_ssm_w, g_ssm_b = carrying(_ssm_conv_bwd, p_xbc, ssm_w, ssm_b, dxs, dB, dC, name="ssm_conv_bwd")
    d_scB, d_scC, d_scX, g_sc_w = _sc_bwd(p_sc, sc_w, dya, "sc_bwd")
    d_sc = jnp.concatenate([d_scB, d_scC, d_scX], axis=1)
    pieces = [("sc", d_sc), ("z", d_z), ("xbc", d_xbc), ("dt", d_dt), ("gate", d_gate)]
    S.grad("win", {k: mm(hb, d, mode="tn", name="proj_dw_" + k, out_dtypes=(BF16,)) for k, d in pieces})
    pieces = [(k, d + S.tok().astype(d.dtype) if k == "dt" else d) for k, d in pieces]
    dh = mm([d for _, d in pieces], [S.weight(k) for k, _ in pieces], mode="nt", name="proj_dx")
    grad_x, _, g_norm_mix = _rms_bwd(x, norm_mix, dh, dx1, "rms_mix_bwd")

    g_small = dict(norm_mix=g_norm_mix, b_gate=g_b_gate, sc_conv_w=g_sc_w, ssm_conv_w=g_ssm_w, ssm_conv_b=g_ssm_b,
                   dt_bias=g_dt_bias, A_log=g_a_log, D_skip=g_d_skip, ssm_norm_w=g_ssm_norm_w, norm_mlp=g_norm_mlp,
                   norm_final=g_norm_final, loss=loss_row)
    return grad_x, g_small


class _Place:
    def __init__(self, k=0):
        x, y, c = lax.axis_index("x"), lax.axis_index("y"), lax.axis_index("c")
        self.x = 1 - x if k & 4 else x
        self.y = 1 - y if k & 2 else y
        self.c = 1 - c if k & 1 else c
        self.chip = 2 * self.x + self.y
        self.id = 2 * self.chip + self.c


ICI_PEERS = (2, 4, 6)
SIBLING = (1,)
ALL_PEERS = (1, 2, 3, 4, 5, 6, 7)


class _Comm:
    def __init__(self, arrs, out_shape, ks, src, dst, own=None, aliases=None):
        self.arrs, self.out_shape, self.ks = list(arrs), list(out_shape), tuple(ks)
        self.n = len(self.arrs)
        self.src, self.dst, self.own = src, dst, own
        self.aliases = aliases or {}
        dma = pltpu.SemaphoreType.DMA
        self.scratch = [dma((self.n, len(self.ks))), dma((self.n, len(self.ks))), dma((self.n,))]

    def _copies(self, ins, outs, sems, with_recvs):
        send_sems, recv_sems, local_sems = sems
        me = _Place()
        owns, sends, recvs = [], [], []
        for a in range(self.n):
            if self.own is not None:
                s, d = self.own(a, ins[a], outs[a], me)
                owns.append(pltpu.make_async_copy(s, d, local_sems.at[a]))
            for i, k in enumerate(self.ks):
                peer = _Place(k)
                for sender, lst in ((me, sends), (peer, recvs)) if with_recvs else ((me, sends),):
                    lst.append(pltpu.make_async_remote_copy(
                        src_ref=self.src(a, ins[a], me, peer), dst_ref=self.dst(a, outs[a], sender),
                        send_sem=send_sems.at[a, i], recv_sem=recv_sems.at[a, i],
                        device_id=(peer.x, peer.y, peer.c), device_id_type=MESH))
        return owns, sends, recvs

    def start(self, ins, outs, sems):
        owns, sends, _ = self._copies(ins, outs, sems, False)
        for cp in owns + sends:
            cp.start()

    def finish(self, ins, outs, sems):
        owns, sends, recvs = self._copies(ins, outs, sems, True)
        for cp in recvs:
            cp.wait_recv()
        for cp in sends:
            cp.wait_send()
        for cp in owns:
            cp.wait()


class _GatherBoth:
    def __init__(self, shards):
        self.arrs, self.n, self.aliases = list(shards), len(shards), {}
        self.out_shape = [_sds((4, 2) + s.shape, s.dtype) for s in shards]
        dma = pltpu.SemaphoreType.DMA
        self.scratch = [dma((self.n, 7)), dma((self.n, 7)), dma((self.n,))]

    def _copy(self, a, j, src, slot, to, outs, sems):
        return pltpu.make_async_remote_copy(src_ref=src, dst_ref=outs[a].at[slot.chip, slot.c], send_sem=sems[0].at[a, j],
                                            recv_sem=sems[1].at[a, j], device_id=(to.x, to.y, to.c), device_id_type=MESH)

    def start(self, ins, outs, sems):
        me, sib = _Place(), _Place(1)
        for a in range(self.n):
            pltpu.make_async_copy(ins[a], outs[a].at[me.chip, me.c], sems[2].at[a]).start()
            self._copy(a, 0, ins[a], me, sib, outs, sems).start()
            for i, k in enumerate(ICI_PEERS):
                self._copy(a, 1 + i, ins[a], me, _Place(k), outs, sems).start()

    def finish(self, ins, outs, sems):
        me, sib = _Place(), _Place(1)
        passed = []
        for i, k in enumerate(ICI_PEERS):
            peer = _Place(k)
            for a in range(self.n):
                self._copy(a, 1 + i, ins[a], peer, peer, outs, sems).wait_recv()
                cp = self._copy(a, 4 + i, outs[a].at[peer.chip, peer.c], peer, sib, outs, sems)
                cp.start()
                passed.append(cp)
        for a in range(self.n):
            self._copy(a, 0, ins[a], sib, sib, outs, sems).wait_recv()
            for i, k in enumerate(ICI_PEERS):
                far = _Place(k | 1)
                self._copy(a, 4 + i, outs[a].at[far.chip, far.c], far, sib, outs, sems).wait_recv()
        for a in range(self.n):
            self._copy(a, 0, ins[a], me, sib, outs, sems).wait_send()
            for i, k in enumerate(ICI_PEERS):
                self._copy(a, 1 + i, ins[a], me, _Place(k), outs, sems).wait_send()
            pltpu.make_async_copy(ins[a], outs[a].at[me.chip, me.c], sems[2].at[a]).wait()
        for cp in passed:
            cp.wait_send()


def _run_comm(comm, name, after=()):
    n, n_after = comm.n, len(after)

    def body(*refs):
        ins, outs, sems = refs[:n], refs[n + n_after:2 * n + n_after], refs[2 * n + n_after:]
        comm.start(ins, outs, sems)
        comm.finish(ins, outs, sems)

    return list(pl.pallas_call(body, in_specs=[ANY] * (n + n_after), out_specs=[ANY] * n, out_shape=comm.out_shape,
                               scratch_shapes=comm.scratch, input_output_aliases=dict(comm.aliases), name=name)(*comm.arrs, *after))


def _gather_sibling(bufs):
    return _Comm(bufs, [_sds(b.shape, b.dtype) for b in bufs], SIBLING,
                 src=lambda a, i, me, p: i.at[:, me.c], dst=lambda a, o, s: o.at[:, s.c], aliases={a: a for a in range(len(bufs))})


def _scatter_sibling(parts):
    return _Comm(parts, [_sds((4,) + p.shape[2:], p.dtype) for p in parts], SIBLING,
                 src=lambda a, i, me, p: i.at[:, p.c], dst=lambda a, o, s: o)


HBM_SPEC = pl.BlockSpec(memory_space=pltpu.HBM)
SEM_SPEC = pl.BlockSpec(memory_space=pltpu.SEMAPHORE)
DATAFLOW = pltpu.SideEffectType.DATAFLOW_SIDE_EFFECTING


def _tiles_2d(R, C, max_rows=256):
    if R % max_rows == 0:
        return max_rows, C, R // max_rows, lambda i: (i, 0)
    if R <= 2 * max_rows or C % 256:
        return R, C, 1, lambda i: (0, 0)
    return R, 256, C // 256, lambda i: (0, i)


def _ici_copy(gather, a, srcs, lands, send_sems, recv_sems, i, me, peer, sender):
    src = lands[a].at[me.chip, me.c] if gather else srcs[a].at[peer.chip]
    dst = lands[a].at[sender.chip, sender.c] if gather else lands[a].at[sender.chip]
    j = a * len(ICI_PEERS) + i
    return pltpu.make_async_remote_copy(src_ref=src, dst_ref=dst, send_sem=send_sems.at[j], recv_sem=recv_sems.at[j],
                                        device_id=(peer.x, peer.y, peer.c), device_id_type=MESH)


def _ici_start(srcs, lands, gather, name):
    n, n_s = len(lands), len(srcs)
    bufs = list(srcs) + list(lands)

    def body(*refs):
        src_refs, land_refs = refs[:n_s], refs[n_s:n_s + n]
        send_sems, recv_sems = refs[n_s + n], refs[n_s + n + 1]
        token = refs[-1]
        me = _Place()
        for a in range(n):
            for i, k in enumerate(ICI_PEERS):
                _ici_copy(gather, a, src_refs, land_refs, send_sems, recv_sems, i, me, _Place(k), me).start()
        token[...] = jnp.zeros_like(token)

    dma = pltpu.SemaphoreType.DMA((n * len(ICI_PEERS),))
    outs = pl.pallas_call(
        body, name=name, out_shape=(dma, dma, *[pltpu.HBM(v.shape, v.dtype) for v in bufs], _sds((8, LANES), F32)),
        in_specs=(HBM_SPEC,) * len(bufs),
        out_specs=(SEM_SPEC, SEM_SPEC) + (HBM_SPEC,) * len(bufs) + (pl.BlockSpec(memory_space=pltpu.VMEM),),
        input_output_aliases={j: 2 + j for j in range(len(bufs))}, compiler_params=pltpu.CompilerParams(has_side_effects=DATAFLOW),
    )(*[pltpu.with_memory_space_constraint(v, pltpu.HBM) for v in bufs])
    return outs[0], outs[1], list(outs[2:2 + n_s]), list(outs[2 + n_s:2 + n_s + n]), outs[-1]


def _ici_wait(flight, after, gather, name):
    send_sems, recv_sems, srcs, lands, _ = flight
    n, n_s = len(lands), len(srcs)
    bufs = srcs + lands

    def body(*refs):
        src_refs, land_refs = refs[:n_s], refs[n_s:n_s + n]
        s_sems, r_sems = refs[n_s + n], refs[n_s + n + 1]
        me = _Place()
        for a in range(n):
            for i, k in enumerate(ICI_PEERS):
                peer = _Place(k)
                cp = _ici_copy(gather, a, src_refs, land_refs, s_sems, r_sems, i, me, peer, peer)
                cp.wait_send()
                cp.wait_recv()

    outs = pl.pallas_call(
        body, name=name, out_shape=tuple(pltpu.HBM(v.shape, v.dtype) for v in bufs),
        in_specs=(HBM_SPEC,) * len(bufs) + (SEM_SPEC, SEM_SPEC) + (ANY,) * len(after), out_specs=(HBM_SPEC,) * len(bufs),
        input_output_aliases={j: j for j in range(len(bufs))}, compiler_params=pltpu.CompilerParams(has_side_effects=DATAFLOW),
    )(*bufs, send_sems, recv_sems, *after)
    return list(outs[n_s:])


def _own_shards(shards, after, name):
    n = len(shards)
    vmem = pl.BlockSpec(memory_space=pltpu.VMEM)

    def body(*refs):
        ins, outs, cast, sems = refs[:n], refs[n + 1:2 * n + 1], refs[2 * n + 1:3 * n + 1], refs[3 * n + 1]
        me = _Place()
        copies = []
        for a in range(n):
            cast[a][...] = ins[a][...].astype(BF16)
            copies.append(pltpu.make_async_copy(cast[a], outs[a].at[me.chip, me.c], sems.at[a]))
            copies[-1].start()
        for cp in copies:
            cp.wait()

    return list(pl.pallas_call(
        body, in_specs=[vmem] * n + [ANY], out_specs=[ANY] * n, out_shape=[_sds((4, 2) + s.shape, BF16) for s in shards],
        scratch_shapes=[pltpu.VMEM(s.shape, BF16) for s in shards] + [pltpu.SemaphoreType.DMA((n,))], name=name)(*shards, after))


def _col_pieces(widths):
    out, c = [], 0
    for k, w in widths:
        out.append((k, c, w))
        c += w
    return out


def _split_range(c0, n, bounds):
    parts, c = [], c0
    while c < c0 + n:
        r = max(i for i in range(len(bounds) - 1) if bounds[i] <= c)
        w = min(c0 + n, bounds[r + 1]) - c
        parts.append((r, c - bounds[r], w))
        c += w
    return parts


def _win_unpack(g, widths, name):
    n, R, C = g.shape
    tr = min(256, R)
    pieces = _col_pieces(widths)
    padded = [-(-w // LANES) * LANES for _, _, w in pieces]
    shard_bounds = [s * C for s in range(n + 1)]

    def body(g_ref, *o_refs):
        for (k, c0, w), o_ref in zip(pieces, o_refs):
            for t in range(0, o_ref.shape[1], LANES):
                valid = max(0, min(LANES, w - t))
                cols = [g_ref[s, :, o:o + ww] for s, o, ww in _split_range(c0 + t, valid, shard_bounds)] if valid else []
                if valid < LANES:
                    cols.append(jnp.zeros((tr, LANES - valid), g_ref.dtype))
                o_ref[:, t:t + LANES] = cols[0] if len(cols) == 1 else jnp.concatenate(cols, axis=1)

    return pl.pallas_call(
        body, grid=(R // tr,), in_specs=[pl.BlockSpec((n, tr, C), lambda i: (0, i, 0))],
        out_specs=[pl.BlockSpec((tr, p), lambda i: (i, 0)) for p in padded],
        out_shape=[_sds((R, p), g.dtype) for p in padded], name=name, compiler_params=_params("parallel"))(g)


def _win_pack(grads, widths, n, name):
    R = grads[0].shape[0]
    tr = min(256, R)
    pieces = _col_pieces(widths)
    total = pieces[-1][1] + pieces[-1][2]
    C = total // n
    bounds = [c0 for _, c0, _ in pieces] + [total]

    def body(*refs):
        g_refs, o_ref = refs[:-1], refs[-1]

        def tile_t(c0):
            cols = [g_refs[r][:, o:o + ww] for r, o, ww in _split_range(c0, LANES, bounds)]
            tile = cols[0] if len(cols) == 1 else jnp.concatenate(cols, axis=1)
            return tile.astype(F32).T

        for s in range(n):
            full = C // LANES * LANES
            for t in range(0, full, LANES):
                o_ref[s, t:t + LANES, :] = tile_t(s * C + t).astype(o_ref.dtype)
            if full < C:
                o_ref[s, full:C, :] = tile_t(s * C + C - LANES)[LANES - (C - full):, :].astype(o_ref.dtype)

    return pl.pallas_call(
        body, grid=(R // tr,), in_specs=[pl.BlockSpec((tr, gr.shape[1]), lambda i: (i, 0)) for gr in grads],
        out_specs=pl.BlockSpec((n, C, tr), lambda i: (0, 0, i)), out_shape=_sds((n, C, R), grads[0].dtype),
        name=name, compiler_params=_params("parallel"))(*grads)


def _gather_all(arrs):
    return _Comm(arrs, [_sds((N_DEV,) + a.shape, a.dtype) for a in arrs], ALL_PEERS,
                 src=lambda a, i, me, p: i, dst=lambda a, o, s: o.at[s.id], own=lambda a, i, o, me: (i, o.at[me.id]))


def _add_halves(parts, got, name):
    n, _, R, C = parts.shape
    br, bc, nb, at = _tiles_2d(R, C, max_rows=1024)
    place = jnp.stack([lax.axis_index("c"), 2 * lax.axis_index("x") + lax.axis_index("y")]).astype(jnp.int32)

    def body(q_ref, p_ref, g_ref, o_ref, land_ref):
        s = (p_ref[0, 0].astype(F32) + g_ref[0].astype(F32)).astype(o_ref.dtype)
        o_ref[0] = s

        @pl.when(pl.program_id(1) == q_ref[1])
        def _():
            land_ref[0] = s

    spec = pltpu.PrefetchScalarGridSpec(
        num_scalar_prefetch=1, grid=(nb, n),
        in_specs=[pl.BlockSpec((1, 1, br, bc), lambda i, q, q_ref: (q, q_ref[0]) + at(i)), pl.BlockSpec((1, br, bc), lambda i, q, q_ref: (q,) + at(i))],
        out_specs=[pl.BlockSpec((1, br, bc), lambda i, q, q_ref: (q,) + at(i)), pl.BlockSpec((1, br, bc), lambda i, q, q_ref: (q_ref[1],) + at(i))])
    return pl.pallas_call(body, grid_spec=spec, out_shape=[_sds((n, R, C), parts.dtype)] * 2, name=name,
                          compiler_params=_params("parallel", "arbitrary"))(place, parts, got)


def _adam(w, m, v, gparts, name, comm=None):
    R, C = w.shape
    n = gparts.shape[0]
    br, bc, nb, at = _tiles_2d(R, C, max_rows=512)
    c1 = 1.0 / (1.0 - ADAM_B1 ** ADAM_STEP)
    c2 = 1.0 / (1.0 - ADAM_B2 ** ADAM_STEP)

    def body(w_ref, m_ref, v_ref, g_ref, go_ref, d_ref, mo_ref, vo_ref):
        g = g_ref[0].astype(F32)
        for s in range(1, n):
            g = g + g_ref[s].astype(F32)
        mn = ADAM_B1 * m_ref[...] + (1.0 - ADAM_B1) * g
        vn = ADAM_B2 * v_ref[...] + (1.0 - ADAM_B2) * (g * g)
        go_ref[...] = g
        mo_ref[...] = mn
        vo_ref[...] = vn
        d_ref[...] = -ADAM_LR * ((mn * c1) / (jnp.sqrt(vn * c2) + ADAM_EPS) + ADAM_WD * w_ref[...])

    blk = pl.BlockSpec((br, bc), at)
    outs, carried = _call(
        body, grid=(nb,), in_specs=[blk, blk, blk, pl.BlockSpec((n, br, bc), lambda i: (0,) + at(i))],
        out_specs=[blk] * 4, out_shape=[_sds((R, C), F32)] * 4, args=[w, m, v, gparts], name=name, sem=("parallel",), comm=comm)
    return outs if comm is None else (outs, carried)


_SMALL_ORDER = ("norm_mix", "b_gate", "sc_conv_w", "ssm_conv_w", "ssm_conv_b", "dt_bias", "A_log", "D_skip", "ssm_norm_w",
                "norm_mlp", "norm_final", "loss")
_REPLICATED = ("norm_mix", "b_gate", "ssm_conv_b", "dt_bias", "A_log", "D_skip", "ssm_norm_w", "norm_mlp", "norm_final")


def _cols_to_slots(g, n):
    R = g.shape[0]
    return jnp.transpose(g.reshape(R, n, g.shape[1] // n), (1, 0, 2))


def _slots_to_cols(g):
    n, R, C = g.shape
    return jnp.transpose(g, (1, 0, 2)).reshape(R, n * C)


def kernel(x, norm_mix, w_in, b_gate, sc_conv_w, ssm_conv_w, ssm_conv_b, dt_bias, A_log, D_skip, ssm_norm_w, w_branch_sc, w_branch_ssm, w_out, norm_mlp, w_mlp1, w_mlp2, norm_final, loss_target, m_norm_mix, m_w_in, m_b_gate, m_sc_conv_w, m_ssm_conv_w, m_ssm_conv_b, m_dt_bias, m_A_log, m_D_skip, m_ssm_norm_w, m_w_branch_sc, m_w_branch_ssm, m_w_out, m_norm_mlp, m_w_mlp1, m_w_mlp2, m_norm_final, v_norm_mix, v_w_in, v_b_gate, v_sc_conv_w, v_ssm_conv_w, v_ssm_conv_b, v_dt_bias, v_A_log, v_D_skip, v_ssm_norm_w, v_w_branch_sc, v_w_branch_ssm, v_w_out, v_norm_mlp, v_w_mlp1, v_w_mlp2, v_norm_final):
    T, D = x.shape[1], x.shape[2]
    n_inner = 2 * D
    n_heads = n_inner // HEADDIM
    n_xbc = n_inner + 2 * NGROUPS * NSTATE
    me = 4 * lax.axis_index("x") + 2 * lax.axis_index("y") + lax.axis_index("c")

    in_cols = [("sc", 3 * D), ("z", n_inner), ("xbc", n_xbc), ("dt", n_heads), ("gate", 2 * D)]
    by_owner = lambda b: b.reshape((N_DEV,) + b.shape[2:])
    to_owner = lambda g: g.reshape((4, 2) + g.shape[1:])
    rows_of = lambda g: to_owner(g.reshape((N_DEV, g.shape[0] // N_DEV) + g.shape[1:]))
    cols_of = lambda g: to_owner(_cols_to_slots(g, N_DEV))

    class Schedule(_NoExchange):
        late = ("bssm", "bsc", "out", "w1", "w2")
        gather_sib = dict(gnorm_fwd=("bsc", "bssm", "out"), branch_ssm=("w1", "w2"))
        scatter_sib = dict(mlp_up_dx=("w2", "w1"), branch_ssm_dx=("out", "bssm", "bsc"))
        shards = dict(bsc=w_branch_sc, bssm=w_branch_ssm, out=w_out, w1=w_mlp1, w2=w_mlp2)

        def __init__(self):
            self.W, self.staged, self.grads, self.summed, self.scatters = {}, {}, {}, {}, []
            self.token = jnp.zeros((), F32)

        def first_weights(self, bufs):
            self.W.update(zip([k for k, _ in in_cols], _win_unpack(by_owner(bufs[0]), in_cols, "win_unpack")))
            self.W.update(sc_conv_w=_slots_to_cols(by_owner(bufs[1])), ssm_conv_w=_slots_to_cols(by_owner(bufs[2])))
            lands = _own_shards([self.shards[k] for k in self.late], bufs[1], "own_shards")
            self.gather_flight = _ici_start([], lands, True, "gather_late_start")
            self.token = self.gather_flight[4][0, 0]
            self.W["dt"] = self.W["dt"] + self.token.astype(BF16)

        def tok(self):
            return self.token

        def point(self, name, values):
            if name == "mixers_done":
                lands = _ici_wait(self.gather_flight, values, True, "gather_late_wait")
                self.staged.update(zip(self.late, lands))

        def carry(self, name):
            if name == "rms_mix":
                return _GatherBoth([w_in.astype(BF16), sc_conv_w, ssm_conv_w])
            if name in self.gather_sib:
                return _gather_sibling([self.staged.pop(k) for k in self.gather_sib[name]])
            if name in self.scatter_sib:
                return _scatter_sibling([self.grads[k] for k in self.scatter_sib[name]])
            return None

        def start_scatter(self, keys, halves_and_lands):
            halves, lands = [h for h, _ in halves_and_lands], [l for _, l in halves_and_lands]
            flight = _ici_start(halves, lands, False, "scatter_%s_start" % keys[0])
            self.scatters.append((keys, flight))
            self.token = flight[4][0, 0]

        def carried(self, name, outs):
            if name == "rms_mix":
                self.first_weights(outs)
            elif name in self.gather_sib:
                for k, b in zip(self.gather_sib[name], outs):
                    full = by_owner(b)
                    self.W[k] = _slots_to_cols(full) if k == "w1" else full.reshape(-1, D)
            else:
                keys = self.scatter_sib[name]
                self.start_scatter(keys, [_add_halves(self.grads[k], b, "add_halves_" + k) for k, b in zip(keys, outs)])

        def grad(self, k, g):
            if k == "win":
                g = to_owner(_win_pack([g[k] for k, _ in in_cols], in_cols, N_DEV, "win_pack"))
                got = _run_comm(_scatter_sibling([g]), "scatter_sibling_win")[0]
                self.start_scatter(("win",), [_add_halves(g, got, "add_halves_win")])
            else:
                self.grads[k] = cols_of(g) if k == "w1" else rows_of(g)

        def finish_scatter(self, after):
            keys, flight = self.scatters.pop(0)
            return dict(zip(keys, _ici_wait(flight, after, False, "scatter_%s_wait" % keys[0])))

    S = Schedule()
    small = dict(norm_mix=norm_mix, b_gate=b_gate, ssm_conv_b=ssm_conv_b, dt_bias=dt_bias, A_log=A_log, D_skip=D_skip,
                 ssm_norm_w=ssm_norm_w, norm_mlp=norm_mlp, norm_final=norm_final)
    grad_x, g_small = _local_step(x.reshape(T, D), loss_target.reshape(T, D), S, small)

    small_flat = jnp.concatenate([g_small[k].reshape(-1) for k in _SMALL_ORDER])
    n_small = small_flat.shape[0]
    rows = -(-n_small // (8 * LANES)) * 8
    small_pack = jnp.pad(small_flat, (0, rows * LANES - n_small)).reshape(rows, LANES)

    res = {}
    big = [("w_in", "win", w_in, m_w_in, v_w_in), ("w_branch_sc", "bsc", w_branch_sc, m_w_branch_sc, v_w_branch_sc),
           ("w_branch_ssm", "bssm", w_branch_ssm, m_w_branch_ssm, v_w_branch_ssm), ("w_out", "out", w_out, m_w_out, v_w_out),
           ("w_mlp1", "w1", w_mlp1, m_w_mlp1, v_w_mlp1), ("w_mlp2", "w2", w_mlp2, m_w_mlp2, v_w_mlp2)]
    by_grad = {gk: (k, w, m, v) for k, gk, w, m, v in big}
    after = [grad_x]
    while S.scatters:
        for gk, parts in S.finish_scatter(after).items():
            k, w, m, v = by_grad[gk]
            if gk == "win":
                res_t, (small_parts,) = _adam(w.T, m.T, v.T, parts, "adam_" + k, comm=_gather_all([small_pack]))
                res[k] = [r.T for r in res_t]
            else:
                res[k] = _adam(w, m, v, parts, "adam_" + k)
            after = after + [res[k][1]]

    sizes = {k: g_small[k].size for k in _SMALL_ORDER}
    offs, o = {}, 0
    for k in _SMALL_ORDER:
        offs[k] = o
        o += sizes[k]
    rep_w = dict(norm_mix=norm_mix, b_gate=b_gate, ssm_conv_b=ssm_conv_b, dt_bias=dt_bias, A_log=A_log, D_skip=D_skip,
                 ssm_norm_w=ssm_norm_w, norm_mlp=norm_mlp, norm_final=norm_final)
    rep_m = dict(norm_mix=m_norm_mix, b_gate=m_b_gate, ssm_conv_b=m_ssm_conv_b, dt_bias=m_dt_bias, A_log=m_A_log, D_skip=m_D_skip,
                 ssm_norm_w=m_ssm_norm_w, norm_mlp=m_norm_mlp, norm_final=m_norm_final)
    rep_v = dict(norm_mix=v_norm_mix, b_gate=v_b_gate, ssm_conv_b=v_ssm_conv_b, dt_bias=v_dt_bias, A_log=v_A_log, D_skip=v_D_skip,
                 ssm_norm_w=v_ssm_norm_w, norm_mlp=v_norm_mlp, norm_final=v_norm_final)

    def pack(d):
        segs = [jnp.pad(d[k].astype(F32).reshape(-1), (0, sizes[k] - d[k].size)) if k in d else jnp.zeros((sizes[k],), F32)
                for k in _SMALL_ORDER]
        return jnp.pad(jnp.concatenate(segs), (0, rows * LANES - n_small)).reshape(rows, LANES)

    sm = _adam(pack(rep_w), pack(rep_m), pack(rep_v), small_parts, "adam_small")
    sm = [s.reshape(-1) for s in sm]
    for k in _REPLICATED:
        n_k = rep_w[k].shape[0]
        res[k] = tuple(s[offs[k]:offs[k] + n_k] for s in sm)
    loss = sm[0][offs["loss"]]
    for k, w, m, v, K, full in (("sc_conv_w", sc_conv_w, m_sc_conv_w, v_sc_conv_w, SC_K, D),
                                ("ssm_conv_w", ssm_conv_w, m_ssm_conv_w, v_ssm_conv_w, SSM_K, n_xbc)):
        g_full = sm[0][offs[k]:offs[k] + K * full].reshape(K, full)
        cw = full // N_DEV
        g_mine = lax.dynamic_slice_in_dim(g_full, me * cw, cw, axis=1)
        res[k] = _adam(w, m, v, g_mine[None], "adam_" + k)

    order = ("norm_mix", "w_in", "b_gate", "sc_conv_w", "ssm_conv_w", "ssm_conv_b", "dt_bias", "A_log", "D_skip", "ssm_norm_w",
             "w_branch_sc", "w_branch_ssm", "w_out", "norm_mlp", "w_mlp1", "w_mlp2", "norm_final")
    outs = [loss, grad_x.reshape(1, T, D)]
    for j in range(4):
        outs += [res[k][j] for k in order]
    return tuple(outs)
```

```python
import jax
import jax.numpy as jnp
from jax import lax
from jax.experimental import pallas as pl
from jax.experimental.pallas import tpu as pltpu

F32 = jnp.float32
BF16 = jnp.bfloat16

EPS = 1e-6
N_DEV = 8
HEADDIM = 64
NSTATE = 128
CHUNK = 128
NGROUPS = 8
GROUP_W = 256
SC_K = 3
SSM_K = 4
LANES = 128

ADAM_LR = 0.001
ADAM_B1 = 0.9
ADAM_B2 = 0.999
ADAM_EPS = 1e-08
ADAM_WD = 0.01
ADAM_STEP = 10

NN = (((1,), (0,)), ((), ()))
NT = (((1,), (1,)), ((), ()))
TN = (((0,), (0,)), ((), ()))
_DIMS = {"nn": NN, "nt": NT, "tn": TN}

ANY = pl.BlockSpec(memory_space=pl.ANY)
MESH = pl.DeviceIdType.MESH


def _sds(shape, dtype):
    return jax.ShapeDtypeStruct(tuple(shape), dtype)


def _dot(a, b, dims=NN):
    return lax.dot_general(a, b, dims, preferred_element_type=F32)


def _dot3(a, b, dims=NN):
    return lax.dot_general(a, b, dims, preferred_element_type=F32, precision=lax.Precision.HIGH)


def _params(*sem):
    return pltpu.CompilerParams(dimension_semantics=tuple(sem))


def _call(body, *, grid, in_specs, out_specs, out_shape, args, name, sem, scratch=(), comm=None):
    if comm is None:
        outs = pl.pallas_call(body, grid=grid, in_specs=list(in_specs), out_specs=list(out_specs), out_shape=list(out_shape),
                              scratch_shapes=list(scratch), name=name, compiler_params=_params(*sem))(*args)
        return list(outs), None
    n, n_in, n_out, n_scr = comm.n, len(in_specs), len(out_shape), len(scratch)

    def wrapped(*refs):
        ins, c_in = refs[:n_in], refs[n_in:n_in + n]
        outs, c_out = refs[n_in + n:n_in + n + n_out], refs[n_in + n + n_out:n_in + 2 * n + n_out]
        rest = refs[n_in + 2 * n + n_out:]
        scr, sems = rest[:n_scr], rest[n_scr:]
        first, last = None, None
        for d, g in enumerate(grid):
            f, l = pl.program_id(d) == 0, pl.program_id(d) == g - 1
            first, last = (f, l) if first is None else (first & f, last & l)

        @pl.when(first)
        def _():
            comm.start(c_in, c_out, sems)

        body(*ins, *outs, *scr)

        @pl.when(last)
        def _():
            comm.finish(c_in, c_out, sems)

    outs = pl.pallas_call(
        wrapped, grid=grid, in_specs=list(in_specs) + [ANY] * n, out_specs=list(out_specs) + [ANY] * n,
        out_shape=list(out_shape) + comm.out_shape, scratch_shapes=list(scratch) + comm.scratch,
        input_output_aliases={n_in + i: n_out + o for i, o in comm.aliases.items()},
        name=name, compiler_params=_params(*["arbitrary"] * len(grid)))(*args, *comm.arrs)
    return list(outs[:n_out]), list(outs[n_out:])


MM_VMEM_BUDGET = 44 * 2 ** 20
MM_MIN_STEPS = 4


def _mm_tiles(M, N, k_bytes, mn_bytes):
    best = None
    for tm in (2048, 1024, 512, 256, 128):
        for tn in (1024, 512, 256, 128):
            if M % tm or N % tn:
                continue
            need = 2 * ((tm + tn) * k_bytes + tm * tn * mn_bytes) + 4 * tm * tn * 4
            if need <= MM_VMEM_BUDGET and (best is None or (tm * tn, tm) > (best[0] * best[1], best[0])):
                best = (tm, tn)
    assert best is not None, (M, N, k_bytes, mn_bytes)
    tm, tn = best
    while (M // tm) * (N // tn) < MM_MIN_STEPS and tm > 256:
        tm //= 2
    return tm, tn


def _mm(a, b, *, mode, name, extras=(), epi=None, out_dtypes=(F32,), comm=None):
    a_list = list(a) if isinstance(a, (list, tuple)) else [a]
    b_list = list(b) if isinstance(b, (list, tuple)) else [b]
    if mode == "nn":
        M, N = a_list[0].shape[0], b_list[0].shape[1]
    elif mode == "nt":
        M, N = a_list[0].shape[0], b_list[0].shape[0]
    else:
        M, N = a_list[0].shape[1], b_list[0].shape[1]
    k_bytes = sum((av.shape[0] if mode == "tn" else av.shape[1]) * av.dtype.itemsize for av in a_list)
    mn_bytes = sum(e.dtype.itemsize for e in extras) + sum(jnp.dtype(d).itemsize for d in out_dtypes)
    tm, tn = _mm_tiles(M, N, k_bytes, mn_bytes) if M % 128 == 0 and N % 128 == 0 else (M, N)
    assert M % tm == 0 and N % tn == 0
    a_specs, b_specs = [], []
    for av, bv in zip(a_list, b_list):
        K = av.shape[0] if mode == "tn" else av.shape[1]
        a_specs.append(pl.BlockSpec((K, tm), lambda i, j: (0, i)) if mode == "tn" else pl.BlockSpec((tm, K), lambda i, j: (i, 0)))
        b_specs.append(pl.BlockSpec((tn, K), lambda i, j: (j, 0)) if mode == "nt" else pl.BlockSpec((K, tn), lambda i, j: (0, j)))
    mn_spec = pl.BlockSpec((tm, tn), lambda i, j: (i, j))
    n_p, n_ex = len(a_list), len(extras)
    dims = _DIMS[mode]

    def body(*refs):
        acc = _dot(refs[0][...], refs[n_p][...], dims)
        for p in range(1, n_p):
            acc = acc + _dot(refs[p][...], refs[n_p + p][...], dims)
        rest = refs[2 * n_p:]
        res = (acc,) if epi is None else epi(acc, *[r[...] for r in rest[:n_ex]])
        for o_ref, r in zip(rest[n_ex:], res):
            o_ref[...] = r.astype(o_ref.dtype)

    outs, carried = _call(
        body, grid=(M // tm, N // tn), in_specs=a_specs + b_specs + [mn_spec] * n_ex,
        out_specs=[mn_spec] * len(out_dtypes), out_shape=[_sds((M, N), d) for d in out_dtypes],
        args=a_list + b_list + list(extras), name=name, sem=("parallel", "parallel"), comm=comm)
    res = outs[0] if len(outs) == 1 else outs
    return res if comm is None else (res, carried)


def _epi_add(acc, r):
    return (acc + r,)


def _epi_relu2(acc):
    p = jnp.maximum(acc, 0.0)
    return (p * p,)


def _epi_relu2_bwd(acc, r):
    return (acc * (2.0 * jnp.sqrt(r.astype(F32))),)


ROW_TILE = 512


def _row(tr, n):
    return pl.BlockSpec((tr, n), lambda i: (i, 0))


def _vec(n):
    return pl.BlockSpec((1, n), lambda i: (0, 0))


def _rms_fwd(x, w, name, comm=None):
    T, D = x.shape
    tr = min(ROW_TILE, T)

    def body(x_ref, w_ref, o_ref):
        xv = x_ref[...]
        r = lax.rsqrt(jnp.mean(xv * xv, axis=-1, keepdims=True) + EPS)
        o_ref[...] = (xv * r * w_ref[...]).astype(BF16)

    outs, carried = _call(body, grid=(T // tr,), in_specs=[_row(tr, D), _vec(D)], out_specs=[_row(tr, D)],
                          out_shape=[_sds((T, D), BF16)], args=[x, w], name=name, sem=("parallel",), comm=comm)
    return outs[0] if comm is None else (outs[0], carried)


def _rms_bwd(x, w, dh, dres, name):
    T, D = x.shape
    tr = min(ROW_TILE, T)

    def body(x_ref, w_ref, dh_ref, dres_ref, dx_ref, dxb_ref, dw_ref):
        @pl.when(pl.program_id(0) == 0)
        def _():
            dw_ref[...] = jnp.zeros_like(dw_ref)

        xv = x_ref[...]
        r = lax.rsqrt(jnp.mean(xv * xv, axis=-1, keepdims=True) + EPS)
        xh = xv * r
        dh_v = dh_ref[...]
        dw_ref[...] += jnp.sum(dh_v * xh, axis=0, keepdims=True)
        dxh = dh_v * w_ref[...]
        dx = r * (dxh - xh * jnp.mean(dxh * xh, axis=-1, keepdims=True)) + dres_ref[...]
        dx_ref[...] = dx
        dxb_ref[...] = dx.astype(BF16)

    return pl.pallas_call(
        body, grid=(T // tr,), in_specs=[_row(tr, D), _vec(D), _row(tr, D), _row(tr, D)],
        out_specs=[_row(tr, D), _row(tr, D), _vec(D)],
        out_shape=[_sds((T, D), F32), _sds((T, D), BF16), _sds((1, D), F32)],
        name=name, compiler_params=_params("arbitrary"))(x, w, dh, dres)


def _final(x2, w, tgt, name):
    T, D = x2.shape
    tr = min(ROW_TILE, T)

    def body(x_ref, w_ref, t_ref, dx_ref, dxb_ref, dw_ref, loss_ref):
        @pl.when(pl.program_id(0) == 0)
        def _():
            dw_ref[...] = jnp.zeros_like(dw_ref)
            loss_ref[...] = jnp.zeros_like(loss_ref)

        xv = x_ref[...]
        wv = w_ref[...]
        r = lax.rsqrt(jnp.mean(xv * xv, axis=-1, keepdims=True) + EPS)
        xh = xv * r
        err = xh * wv - t_ref[...]
        part = jnp.sum(jnp.sum(err * err, axis=1, keepdims=True), axis=0, keepdims=True) * (0.5 / D)
        loss_ref[...] += jnp.broadcast_to(part, loss_ref.shape)
        dy = err * (1.0 / D)
        dw_ref[...] += jnp.sum(dy * xh, axis=0, keepdims=True)
        dxh = dy * wv
        dx = r * (dxh - xh * jnp.mean(dxh * xh, axis=-1, keepdims=True))
        dx_ref[...] = dx
        dxb_ref[...] = dx.astype(BF16)

    return pl.pallas_call(
        body, grid=(T // tr,), in_specs=[_row(tr, D), _vec(D), _row(tr, D)],
        out_specs=[_row(tr, D), _row(tr, D), _vec(D), _vec(LANES)],
        out_shape=[_sds((T, D), F32), _sds((T, D), BF16), _sds((1, D), F32), _sds((1, LANES), F32)],
        name=name, compiler_params=_params("arbitrary"))(x2, w, tgt)


def _silu_parts(z):
    s = jax.nn.sigmoid(z)
    return z * s, s * (1.0 + z * (1.0 - s))


def _gnorm_fwd(y, z, w, name, comm=None):
    T, N = y.shape
    tr = min(ROW_TILE, T)

    def body(y_ref, z_ref, w_ref, o_ref):
        for g in range(N // GROUP_W):
            sl = slice(g * GROUP_W, (g + 1) * GROUP_W)
            silu, _ = _silu_parts(z_ref[:, sl])
            yz = y_ref[:, sl] * silu
            r = lax.rsqrt(jnp.mean(yz * yz, axis=-1, keepdims=True) + EPS)
            o_ref[:, sl] = (yz * r * w_ref[:, sl]).astype(BF16)

    outs, carried = _call(body, grid=(T // tr,), in_specs=[_row(tr, N), _row(tr, N), _vec(N)], out_specs=[_row(tr, N)],
                          out_shape=[_sds((T, N), BF16)], args=[y, z, w], name=name, sem=("parallel",), comm=comm)
    return outs[0] if comm is None else (outs[0], carried)


def _gnorm_bwd(y, z, w, dyb, name):
    T, N = y.shape
    tr = min(ROW_TILE, T)

    def body(y_ref, z_ref, w_ref, d_ref, dy_ref, dz_ref, dw_ref):
        @pl.when(pl.program_id(0) == 0)
        def _():
            dw_ref[...] = jnp.zeros_like(dw_ref)

        for g in range(N // GROUP_W):
            sl = slice(g * GROUP_W, (g + 1) * GROUP_W)
            yv = y_ref[:, sl]
            silu, dsilu = _silu_parts(z_ref[:, sl])
            yz = yv * silu
            r = lax.rsqrt(jnp.mean(yz * yz, axis=-1, keepdims=True) + EPS)
            yzh = yz * r
            d = d_ref[:, sl]
            dw_ref[:, sl] += jnp.sum(d * yzh, axis=0, keepdims=True)
            dyzh = d * w_ref[:, sl]
            dyz = r * (dyzh - yzh * jnp.mean(dyzh * yzh, axis=-1, keepdims=True))
            dy_ref[:, sl] = dyz * silu
            dz_ref[:, sl] = (dyz * yv * dsilu).astype(BF16)

    return pl.pallas_call(
        body, grid=(T // tr,), in_specs=[_row(tr, N), _row(tr, N), _vec(N), _row(tr, N)],
        out_specs=[_row(tr, N), _row(tr, N), _vec(N)],
        out_shape=[_sds((T, N), F32), _sds((T, N), BF16), _sds((1, N), F32)],
        name=name, compiler_params=_params("arbitrary"))(y, z, w, dyb)


def _merge_fwd(gate_raw, b_gate, br_a, br_b, name):
    T, D = br_a.shape
    tr = min(ROW_TILE, T)

    def body(g_ref, bg_ref, a_ref, b_ref, o_ref):
        g = jax.nn.sigmoid(g_ref[...] + bg_ref[...])
        o_ref[...] = (g[:, :D] * a_ref[...] + g[:, D:] * b_ref[...]).astype(BF16)

    return pl.pallas_call(body, grid=(T // tr,), in_specs=[_row(tr, 2 * D), _vec(2 * D), _row(tr, D), _row(tr, D)],
                          out_specs=_row(tr, D), out_shape=_sds((T, D), BF16), name=name,
                          compiler_params=_params("parallel"))(gate_raw, b_gate, br_a, br_b)


def _merge_bwd(dmerged, gate_raw, b_gate, br_a, br_b, name):
    T, D = br_a.shape
    tr = min(ROW_TILE, T)

    def body(d_ref, g_ref, bg_ref, a_ref, b_ref, da_ref, db_ref, dg_ref, dbg_ref):
        @pl.when(pl.program_id(0) == 0)
        def _():
            dbg_ref[...] = jnp.zeros_like(dbg_ref)

        g = jax.nn.sigmoid(g_ref[...] + bg_ref[...])
        d = d_ref[...]
        da_ref[...] = (d * g[:, :D]).astype(BF16)
        db_ref[...] = (d * g[:, D:]).astype(BF16)
        dg = jnp.concatenate([d * a_ref[...], d * b_ref[...]], axis=1) * g * (1.0 - g)
        dg_ref[...] = dg.astype(BF16)
        dbg_ref[...] += jnp.sum(dg, axis=0, keepdims=True)

    return pl.pallas_call(
        body, grid=(T // tr,), in_specs=[_row(tr, D), _row(tr, 2 * D), _vec(2 * D), _row(tr, D), _row(tr, D)],
        out_specs=[_row(tr, D), _row(tr, D), _row(tr, 2 * D), _vec(2 * D)],
        out_shape=[_sds((T, D), BF16), _sds((T, D), BF16), _sds((T, 2 * D), BF16), _sds((1, 2 * D), F32)],
        name=name, compiler_params=_params("arbitrary"))(dmerged, gate_raw, b_gate, br_a, br_b)


CB_W = 256
CONV_ROWS = 32
CONV_PAD = 8


def _rows_down(load, r0, s):
    if s == 0:
        return load(r0, r0 + CONV_ROWS)
    if r0 == 0:
        row = lax.broadcasted_iota(jnp.int32, (CONV_ROWS, CB_W), 0)
        return jnp.where(row >= s, pltpu.roll(load(0, CONV_ROWS), s, 0), 0.0)
    return load(r0 - s, r0 - s + CONV_ROWS)


def _conv_tile(load, taps, r0):
    K = len(taps)
    us = [_rows_down(load, r0, K - 1 - k) for k in range(K)]
    acc = us[K - 1] * taps[K - 1]
    for k in range(K - 1):
        acc = acc + us[k] * taps[k]
    return acc, us


def _conv_back_tile(scr, taps, r0):
    K = len(taps)
    du = scr[r0:r0 + CONV_ROWS, :] * taps[K - 1]
    for k in range(K - 1):
        s = K - 1 - k
        du = du + scr[r0 + s:r0 + s + CONV_ROWS, :] * taps[k]
    return du


def _fold8(v):
    return jnp.sum(v.reshape(CONV_ROWS // 8, 8, v.shape[1]), axis=0)


def _col(T, j0=0):
    return pl.BlockSpec((T, CB_W), lambda j: (0, j + j0))


def _sc_fwd(psc, w, name):
    T, D = psc.shape[0], psc.shape[1] // 3
    nb = D // CB_W

    def body(b_ref, c_ref, x_ref, w_ref, o_ref):
        taps = [w_ref[k:k + 1, :] for k in range(SC_K)]
        load = lambda a, b: c_ref[a:b, :] * x_ref[a:b, :]
        for r0 in range(0, T, CONV_ROWS):
            cu, _ = _conv_tile(load, taps, r0)
            o_ref[r0:r0 + CONV_ROWS, :] = (b_ref[r0:r0 + CONV_ROWS, :] * cu).astype(BF16)

    return pl.pallas_call(
        body, grid=(nb,), in_specs=[_col(T), _col(T, nb), _col(T, 2 * nb), pl.BlockSpec((SC_K, CB_W), lambda j: (0, j))],
        out_specs=_col(T), out_shape=_sds((T, D), BF16), name=name, compiler_params=_params("parallel"))(psc, psc, psc, w)


def _sc_bwd(psc, w, dya, name):
    T, D = psc.shape[0], psc.shape[1] // 3
    nb = D // CB_W

    def body(b_ref, c_ref, x_ref, w_ref, d_ref, db_ref, dc_ref, dx_ref, dw_ref, scr):
        taps = [w_ref[k:k + 1, :] for k in range(SC_K)]
        load = lambda a, b: c_ref[a:b, :] * x_ref[a:b, :]
        scr[T:T + CONV_PAD, :] = jnp.zeros((CONV_PAD, CB_W), F32)
        dw8 = [jnp.zeros((8, CB_W), F32)] * SC_K
        for r0 in range(0, T, CONV_ROWS):
            rows = slice(r0, r0 + CONV_ROWS)
            cu, us = _conv_tile(load, taps, r0)
            d = d_ref[rows, :]
            db_ref[rows, :] = (d * cu).astype(BF16)
            dcu = d * b_ref[rows, :]
            scr[rows, :] = dcu
            dw8 = [acc + _fold8(dcu * u) for acc, u in zip(dw8, us)]
        for k in range(SC_K):
            dw_ref[k:k + 1, :] = jnp.sum(dw8[k], axis=0, keepdims=True)
        for r0 in range(0, T, CONV_ROWS):
            rows = slice(r0, r0 + CONV_ROWS)
            du = _conv_back_tile(scr, taps, r0)
            dc_ref[rows, :] = (du * x_ref[rows, :]).astype(BF16)
            dx_ref[rows, :] = (du * c_ref[rows, :]).astype(BF16)

    wspec = pl.BlockSpec((SC_K, CB_W), lambda j: (0, j))
    return pl.pallas_call(
        body, grid=(nb,), in_specs=[_col(T), _col(T, nb), _col(T, 2 * nb), wspec, _col(T)],
        out_specs=[_col(T), _col(T), _col(T), wspec],
        out_shape=[_sds((T, D), BF16)] * 3 + [_sds((SC_K, D), F32)],
        scratch_shapes=[pltpu.VMEM((T + CONV_PAD, CB_W), F32)],
        name=name, compiler_params=_params("parallel"))(psc, psc, psc, w, dya)


def _ssm_conv_fwd(u, w, b, name, comm=None):
    T, N = u.shape

    def body(u_ref, w_ref, b_ref, o_ref):
        taps = [w_ref[k:k + 1, :] for k in range(SSM_K)]
        bias = b_ref[...]
        for r0 in range(0, T, CONV_ROWS):
            c, _ = _conv_tile(lambda a, b: u_ref[a:b, :], taps, r0)
            c = c + bias
            o_ref[r0:r0 + CONV_ROWS, :] = c * jax.nn.sigmoid(c)

    outs, carried = _call(
        body, grid=(N // CB_W,), in_specs=[_col(T), pl.BlockSpec((SSM_K, CB_W), lambda j: (0, j)), pl.BlockSpec((1, CB_W), lambda j: (0, j))],
        out_specs=[_col(T)], out_shape=[_sds((T, N), F32)], args=[u, w, b], name=name, sem=("parallel",), comm=comm)
    return outs[0] if comm is None else (outs[0], carried)


def _ssm_conv_bwd(u, w, b, dxs, dB, dC, name, comm=None):
    T, N = u.shape
    n_x, n_b = dxs.shape[1] // CB_W, dB.shape[1] // CB_W

    def body(u_ref, w_ref, b_ref, dx_ref, db_ref, dc_ref, du_ref, dw_ref, dbias_ref, scr):
        j = pl.program_id(0)
        taps = [w_ref[k:k + 1, :] for k in range(SSM_K)]
        bias = b_ref[...]
        scr[T:T + CONV_PAD, :] = jnp.zeros((CONV_PAD, CB_W), F32)
        dw8 = [jnp.zeros((8, CB_W), F32)] * SSM_K
        db8 = jnp.zeros((8, CB_W), F32)
        for r0 in range(0, T, CONV_ROWS):
            rows = slice(r0, r0 + CONV_ROWS)
            c, us = _conv_tile(lambda a, b: u_ref[a:b, :], taps, r0)
            _, dsilu = _silu_parts(c + bias)
            d = jnp.where(j < n_x, dx_ref[rows, :], jnp.where(j < n_x + n_b, db_ref[rows, :], dc_ref[rows, :])) * dsilu
            scr[rows, :] = d
            db8 = db8 + _fold8(d)
            dw8 = [acc + _fold8(d * u) for acc, u in zip(dw8, us)]
        dbias_ref[...] = jnp.sum(db8, axis=0, keepdims=True)
        for k in range(SSM_K):
            dw_ref[k:k + 1, :] = jnp.sum(dw8[k], axis=0, keepdims=True)
        for r0 in range(0, T, CONV_ROWS):
            du_ref[r0:r0 + CONV_ROWS, :] = _conv_back_tile(scr, taps, r0).astype(BF16)

    wspec = pl.BlockSpec((SSM_K, CB_W), lambda j: (0, j))
    bspec = pl.BlockSpec((1, CB_W), lambda j: (0, j))
    outs, carried = _call(
        body, grid=(N // CB_W,),
        in_specs=[_col(T), wspec, bspec,
                  pl.BlockSpec((T, CB_W), lambda j: (0, jnp.minimum(j, n_x - 1))),
                  pl.BlockSpec((T, CB_W), lambda j: (0, jnp.clip(j - n_x, 0, n_b - 1))),
                  pl.BlockSpec((T, CB_W), lambda j: (0, jnp.clip(j - n_x - n_b, 0, n_b - 1)))],
        out_specs=[_col(T), wspec, bspec],
        out_shape=[_sds((T, N), BF16), _sds((SSM_K, N), F32), _sds((1, N), F32)],
        scratch=[pltpu.VMEM((T + CONV_PAD, CB_W), F32)],
        args=[u, w, b, dxs, dB, dC], name=name, sem=("parallel",), comm=comm)
    return outs if comm is None else (outs, carried)


def _split3(v):
    hi = v.astype(BF16)
    r = v - hi.astype(F32)
    mid = r.astype(BF16)
    lo = (r - mid.astype(F32)).astype(BF16)
    return hi, mid, lo


def _head_expand(n_lanes):
    h = lax.broadcasted_iota(jnp.int32, (LANES, n_lanes), 0)
    l = lax.broadcasted_iota(jnp.int32, (LANES, n_lanes), 1)
    return (jnp.right_shift(l, HEADDIM.bit_length() - 1) == h).astype(BF16)


def _softplus(v):
    return jnp.maximum(v, 0.0) + jnp.log1p(jnp.exp(-jnp.abs(v)))


def _ssd_prep(dt_raw, dt_bias, a_log, n_inner, name):
    T = dt_raw.shape[0]

    def body(r_ref, b_ref, al_ref, ex_ref, dt_ref, cs_ref):
        dt = _softplus(r_ref[...] + b_ref[...])
        a = dt * (-jnp.exp(al_ref[...]))
        i = lax.broadcasted_iota(jnp.int32, (CHUNK, CHUNK), 0)
        j = lax.broadcasted_iota(jnp.int32, (CHUNK, CHUNK), 1)
        tri = (j <= i).astype(BF16)
        cs = sum(_dot(tri, p) for p in _split3(a))
        ex = ex_ref[...]
        dt_ref[...] = sum(_dot(p, ex) for p in _split3(dt))
        cs_ref[...] = sum(_dot(p, ex) for p in _split3(cs))

    blk = pl.BlockSpec((CHUNK, LANES), lambda c: (c, 0))
    out = pl.BlockSpec((CHUNK, n_inner), lambda c: (c, 0))
    ex_spec = pl.BlockSpec((LANES, n_inner), lambda c: (0, 0))
    return pl.pallas_call(body, grid=(T // CHUNK,), in_specs=[blk, _vec(LANES), _vec(LANES), ex_spec], out_specs=[out, out],
                          out_shape=[_sds((T, n_inner), F32)] * 2, name=name,
                          compiler_params=_params("parallel"))(dt_raw, dt_bias, a_log, _head_expand(n_inner))


def _pair_terms(cs_p):
    lane = lax.broadcasted_iota(jnp.int32, (CHUNK, CHUNK), 1)
    sub = lax.broadcasted_iota(jnp.int32, (CHUNK, CHUNK), 0)
    csT = cs_p.T
    Ls = []
    for k in range(2):
        col = jnp.sum(jnp.where(lane == k * HEADDIM, cs_p, 0.0), axis=1, keepdims=True)
        rowv = csT[k * HEADDIM:k * HEADDIM + 1, :]
        Ls.append(jnp.exp(jnp.where(sub >= lane, col - rowv, -jnp.inf)))
    return Ls, jnp.exp(csT[:, CHUNK - 1:CHUNK])


def _block_diag(xp):
    lane = lax.broadcasted_iota(jnp.int32, xp.shape, 1)
    return jnp.concatenate([jnp.where(lane < HEADDIM, xp, 0.0), jnp.where(lane >= HEADDIM, xp, 0.0)], axis=0)


SSD_GROUPS_PER_STEP = 8


def _ssd_specs(T, n_inner):
    nc, gs = T // CHUNK, SSD_GROUPS_PER_STEP
    bo, co = n_inner // (gs * NSTATE), (n_inner + NGROUPS * NSTATE) // (gs * NSTATE)
    assert NGROUPS % gs == 0 and n_inner % (gs * NSTATE) == 0 and (NGROUPS * NSTATE) % (gs * NSTATE) == 0
    g_blk = lambda f: pl.BlockSpec((CHUNK, gs * GROUP_W), lambda c, s: (f(c), s))
    b_blk = lambda f: pl.BlockSpec((CHUNK, gs * NSTATE), lambda c, s: (f(c), bo + s))
    c_blk = lambda f: pl.BlockSpec((CHUNK, gs * NSTATE), lambda c, s: (f(c), co + s))
    return nc, g_blk, b_blk, c_blk


def _ssd_fwd(xbc, dt_e, cs_e, d_e, name, comm=None):
    T = xbc.shape[0]
    n_inner = dt_e.shape[1]
    nc, g_blk, b_blk, c_blk = _ssd_specs(T, n_inner)
    ident = lambda c: c

    gs = SSD_GROUPS_PER_STEP

    def body(xs_ref, b_ref, c_ref, dt_ref, cs_ref, d_ref, y_ref, p_ref, st):
        c, s = pl.program_id(0), pl.program_id(1)

        @pl.when(c == 0)
        def _():
            for gi in range(gs):
                st[s * gs + gi] = jnp.zeros((GROUP_W, NSTATE), F32)

        for gi in range(gs):
            g = s * gs + gi
            gw, gn = slice(gi * GROUP_W, (gi + 1) * GROUP_W), slice(gi * NSTATE, (gi + 1) * NSTATE)
            P = st[g]
            p_ref[0, gi] = P
            xs, dt, cs = xs_ref[:, gw], dt_ref[:, gw], cs_ref[:, gw]
            Bf, Cf = b_ref[:, gn], c_ref[:, gn]
            Cb = Cf.astype(BF16)
            CBm = _dot(Cb, Bf.astype(BF16), NT)
            X = xs * dt
            decay = jnp.exp(cs[CHUNK - 1:CHUNK, :] - cs)
            y_off = _dot(Cb, P.astype(BF16), NT) * jnp.exp(cs)
            ys, ecl = [], []
            for pr in range(2):
                sl = slice(pr * LANES, (pr + 1) * LANES)
                Ls, e_last = _pair_terms(cs[:, sl])
                ecl.append(e_last)
                Mcat = jnp.concatenate([(CBm * L).astype(BF16) for L in Ls], axis=1)
                ys.append(_dot(Mcat, _block_diag(X[:, sl]).astype(BF16)))
            y_ref[:, gw] = jnp.concatenate(ys, axis=1) + y_off + xs * d_ref[:, gw]
            S = _dot3(X * decay, Bf, TN)
            st[g] = P * jnp.concatenate(ecl, axis=0) + S

    p_blk = pl.BlockSpec((1, gs, GROUP_W, NSTATE), lambda c, s: (c, s, 0, 0))
    outs, carried = _call(
        body, grid=(nc, NGROUPS // gs),
        in_specs=[g_blk(ident), b_blk(ident), c_blk(ident), g_blk(ident), g_blk(ident), pl.BlockSpec((1, gs * GROUP_W), lambda c, s: (0, s))],
        out_specs=[g_blk(ident), p_blk],
        out_shape=[_sds((T, n_inner), F32), _sds((nc, NGROUPS, GROUP_W, NSTATE), F32)],
        scratch=[pltpu.VMEM((NGROUPS, GROUP_W, NSTATE), F32)],
        args=[xbc, xbc, xbc, dt_e, cs_e, d_e], name=name, sem=("arbitrary", "arbitrary"), comm=comm)
    return outs if comm is None else (outs, carried)


def _ssd_bwd(xbc, dt_e, cs_e, d_e, states, dy, name, comm=None):
    T = xbc.shape[0]
    n_inner = dt_e.shape[1]
    nc, g_blk, b_blk, c_blk = _ssd_specs(T, n_inner)
    rev = lambda c: nc - 1 - c

    gs = SSD_GROUPS_PER_STEP

    def body(xs_ref, b_ref, c_ref, dt_ref, cs_ref, d_ref, p_ref, pn_ref, dy_ref,
             dxs_ref, db_ref, dc_ref, ddt_ref, dcs_ref, dd_ref, dst):
        cc, s = pl.program_id(0), pl.program_id(1)

        @pl.when(cc == 0)
        def _():
            for gi in range(gs):
                dst[s * gs + gi] = jnp.zeros((GROUP_W, NSTATE), F32)

        for gi in range(gs):
            one_group(s * gs + gi, gi, xs_ref, b_ref, c_ref, dt_ref, cs_ref, d_ref, p_ref, pn_ref, dy_ref,
                      dxs_ref, db_ref, dc_ref, ddt_ref, dcs_ref, dd_ref, dst)

    def one_group(g, gi, xs_ref, b_ref, c_ref, dt_ref, cs_ref, d_ref, p_ref, pn_ref, dy_ref,
                  dxs_ref, db_ref, dc_ref, ddt_ref, dcs_ref, dd_ref, dst):
        gw, gn = slice(gi * GROUP_W, (gi + 1) * GROUP_W), slice(gi * NSTATE, (gi + 1) * NSTATE)
        dS = dst[g]
        P, Pn = p_ref[0, gi], pn_ref[0, gi]
        xs, dt, cs, dY = xs_ref[:, gw], dt_ref[:, gw], cs_ref[:, gw], dy_ref[:, gw]
        Bf, Cf = b_ref[:, gn], c_ref[:, gn]
        Bb, Cb = Bf.astype(BF16), Cf.astype(BF16)
        X = xs * dt
        ecs = jnp.exp(cs)
        decay = jnp.exp(cs[CHUNK - 1:CHUNK, :] - cs)
        CBm = _dot3(Cf, Bf, NT)
        dYe = dY * ecs
        dP_off = _dot3(dYe, Cf, TN)
        dC = _dot(dYe.astype(BF16), P.astype(BF16))
        dcs = dYe * _dot3(Cf, P, NT)
        Xd = X * decay
        dB = _dot(Xd.astype(BF16), dS.astype(BF16))
        E = _dot3(Bf, dS, NT)
        dX = E * decay
        dcs = dcs - E * Xd
        R = _dot3(jnp.ones((8, NSTATE), F32), dS * Pn, NT)
        sub_g = lax.broadcasted_iota(jnp.int32, (CHUNK, GROUP_W), 0)
        dcs = dcs + jnp.where(sub_g == CHUNK - 1, R[0:1, :], 0.0)
        lane = lax.broadcasted_iota(jnp.int32, (CHUNK, CHUNK), 1)
        sub = lax.broadcasted_iota(jnp.int32, (CHUNK, CHUNK), 0)
        dCB = jnp.zeros((CHUNK, CHUNK), F32)
        dXs, dcss, ecl = [], [], []
        for pr in range(2):
            sl = slice(pr * LANES, (pr + 1) * LANES)
            Ls, e_last = _pair_terms(cs[:, sl])
            ecl.append(e_last)
            dYpb = dY[:, sl].astype(BF16)
            dMcat = _dot(dYpb, _block_diag(X[:, sl]).astype(BF16), NT)
            Mcat = jnp.concatenate([(CBm * L).astype(BF16) for L in Ls], axis=1)
            dXt = _dot(Mcat, dYpb, TN)
            dXs.append(jnp.where(lane < HEADDIM, dXt[:CHUNK], dXt[CHUNK:]))
            colacc = jnp.zeros((CHUNK, CHUNK), F32)
            rowacc = jnp.zeros((CHUNK, CHUNK), F32)
            for k in range(2):
                dG = dMcat[:, k * CHUNK:(k + 1) * CHUNK] * Ls[k]
                dCB = dCB + dG
                Q = dG * CBm
                colacc = colacc + jnp.where(lane == k * HEADDIM, jnp.sum(Q, axis=1, keepdims=True), 0.0)
                rowacc = rowacc + jnp.where(sub == k * HEADDIM, jnp.sum(Q, axis=0, keepdims=True), 0.0)
            dcss.append(colacc - rowacc.T)
        dX = dX + jnp.concatenate(dXs, axis=1)
        dcs = dcs + jnp.concatenate(dcss, axis=1)
        dCBb = dCB.astype(BF16)
        dc_ref[:, gn] = dC + _dot(dCBb, Bb)
        db_ref[:, gn] = dB + _dot(dCBb, Cb, TN)
        dxs_ref[:, gw] = dX * dt + dY * d_ref[:, gw]
        ddt_ref[:, gw] = dX * xs
        dcs_ref[:, gw] = dcs
        dd_ref[0, :, gw] = jnp.sum(dY * xs, axis=0, keepdims=True)
        dst[g] = dS * jnp.concatenate(ecl, axis=0) + dP_off

    p_blk = pl.BlockSpec((1, gs, GROUP_W, NSTATE), lambda c, s: (nc - 1 - c, s, 0, 0))
    pn_blk = pl.BlockSpec((1, gs, GROUP_W, NSTATE), lambda c, s: (jnp.minimum(nc - c, nc - 1), s, 0, 0))
    st_blk = pl.BlockSpec((CHUNK, gs * NSTATE), lambda c, s: (nc - 1 - c, s))
    outs, carried = _call(
        body, grid=(nc, NGROUPS // gs),
        in_specs=[g_blk(rev), b_blk(rev), c_blk(rev), g_blk(rev), g_blk(rev), pl.BlockSpec((1, gs * GROUP_W), lambda c, s: (0, s)),
                  p_blk, pn_blk, g_blk(rev)],
        out_specs=[g_blk(rev), st_blk, st_blk, g_blk(rev), g_blk(rev), pl.BlockSpec((1, 1, gs * GROUP_W), lambda c, s: (nc - 1 - c, 0, s))],
        out_shape=[_sds((T, n_inner), F32), _sds((T, NGROUPS * NSTATE), F32), _sds((T, NGROUPS * NSTATE), F32),
                   _sds((T, n_inner), F32), _sds((T, n_inner), F32), _sds((nc, 1, n_inner), F32)],
        scratch=[pltpu.VMEM((NGROUPS, GROUP_W, NSTATE), F32)],
        args=[xbc, xbc, xbc, dt_e, cs_e, d_e, states, states, dy], name=name, sem=("arbitrary", "arbitrary"), comm=comm)
    return outs if comm is None else (outs, carried)


def _ssd_post(ddt_e, dcs_e, dd_p, dt_raw, dt_bias, a_log, n_heads, name):
    T, n_inner = ddt_e.shape

    def body(ddt_ref, dcs_ref, dd_ref, r_ref, b_ref, al_ref, ex_ref, draw_ref, dbias_ref, dal_ref, ddsk_ref):
        @pl.when(pl.program_id(0) == 0)
        def _():
            dbias_ref[...] = jnp.zeros_like(dbias_ref)
            dal_ref[...] = jnp.zeros_like(dal_ref)
            ddsk_ref[...] = jnp.zeros_like(ddsk_ref)

        spread = [ddt_ref[...], dcs_ref[...], jnp.broadcast_to(dd_ref[0], (8, n_inner))]
        stacked = _dot(jnp.concatenate([p for v in spread for p in _split3(v)], axis=0), ex_ref[...], NT)
        sums, r0 = [], 0
        for v in spread:
            n = v.shape[0]
            sums.append(stacked[r0:r0 + n] + stacked[r0 + n:r0 + 2 * n] + stacked[r0 + 2 * n:r0 + 3 * n])
            r0 += 3 * n
        ddt_h, dcs_h, dd_h = sums
        raw = r_ref[...] + b_ref[...]
        dt = _softplus(raw)
        A = -jnp.exp(al_ref[...])
        i = lax.broadcasted_iota(jnp.int32, (CHUNK, CHUNK), 0)
        j = lax.broadcasted_iota(jnp.int32, (CHUNK, CHUNK), 1)
        upper = (j >= i).astype(BF16)
        da = sum(_dot(upper, p) for p in _split3(dcs_h))
        ddt = ddt_h + da * A
        lane = lax.broadcasted_iota(jnp.int32, (CHUNK, LANES), 1)
        draw = jnp.where(lane < n_heads, ddt * jax.nn.sigmoid(raw), 0.0)
        draw_ref[...] = draw.astype(BF16)
        dbias_ref[...] += jnp.sum(draw, axis=0, keepdims=True)
        dal_ref[...] += jnp.sum(da * dt, axis=0, keepdims=True) * A
        ddsk_ref[...] += dd_h[0:1, :]

    wide = pl.BlockSpec((CHUNK, n_inner), lambda c: (c, 0))
    blk = pl.BlockSpec((CHUNK, LANES), lambda c: (c, 0))
    return pl.pallas_call(
        body, grid=(T // CHUNK,),
        in_specs=[wide, wide, pl.BlockSpec((1, 1, n_inner), lambda c: (c, 0, 0)), blk, _vec(LANES), _vec(LANES),
                  pl.BlockSpec((LANES, n_inner), lambda c: (0, 0))],
        out_specs=[blk, _vec(LANES), _vec(LANES), _vec(LANES)],
        out_shape=[_sds((T, LANES), BF16)] + [_sds((1, LANES), F32)] * 3,
        name=name, compiler_params=_params("arbitrary"))(ddt_e, dcs_e, dd_p, dt_raw, dt_bias, a_log, _head_expand(n_inner))


def _row2(v):
    return v.reshape(1, -1).astype(F32)


def _pad_lanes(v):
    return jnp.pad(_row2(v), ((0, 0), (0, LANES - v.shape[-1])))


class _NoExchange:
    def __init__(self, W):
        self.W, self.grads = W, {}

    def weight(self, k):
        return self.W[k]

    def carry(self, name):
        return None

    def carried(self, name, outs):
        pass

    def grad(self, k, g):
        self.grads[k] = g

    def tok(self):
        return jnp.zeros((), F32)

    def point(self, name, value):
        pass


def _local_step(x, tgt, S, small):
    T, D = x.shape

    def mm(a, b, *, name, **kw):
        comm = S.carry(name)
        if comm is None:
            return _mm(a, b, name=name, **kw)
        res, outs = _mm(a, b, name=name, comm=comm, **kw)
        S.carried(name, outs)
        return res

    def carrying(fn, *args, name):
        comm = S.carry(name)
        if comm is None:
            return fn(*args, name)
        res, outs = fn(*args, name, comm=comm)
        S.carried(name, outs)
        return res

    n_inner = 2 * D
    n_heads = n_inner // HEADDIM
    norm_mix, norm_mlp, norm_final = _row2(small["norm_mix"]), _row2(small["norm_mlp"]), _row2(small["norm_final"])
    b_gate, ssm_b, ssm_norm_w = _row2(small["b_gate"]), _row2(small["ssm_conv_b"]), _row2(small["ssm_norm_w"])
    dt_bias, a_log = _pad_lanes(small["dt_bias"]), _pad_lanes(small["A_log"])
    d_e = jnp.repeat(small["D_skip"].astype(F32), HEADDIM).reshape(1, n_inner)

    hb = carrying(_rms_fwd, x, norm_mix, name="rms_mix")
    sc_w, ssm_w = S.weight("sc_conv_w"), S.weight("ssm_conv_w")
    p_xbc = mm(hb, S.weight("xbc"), mode="nn", name="proj_xbc")
    p_dt = mm(hb, S.weight("dt"), mode="nn", name="proj_dt")
    p_z = mm(hb, S.weight("z"), mode="nn", name="proj_z")
    p_sc = mm(hb, S.weight("sc"), mode="nn", name="proj_sc")
    p_gate = mm(hb, S.weight("gate"), mode="nn", name="proj_gate")
    xbc = carrying(_ssm_conv_fwd, p_xbc, ssm_w, ssm_b, name="ssm_conv_fwd")
    dt_e, cs_e = _ssd_prep(p_dt, dt_bias, a_log, n_inner, "ssd_prep")
    ya = _sc_fwd(p_sc, sc_w, "sc_fwd")
    y, states = carrying(_ssd_fwd, xbc, dt_e, cs_e, d_e, name="ssd_fwd")
    S.point("mixers_done", [y, ya, p_gate])
    yb = carrying(_gnorm_fwd, y, p_z, ssm_norm_w, name="gnorm_fwd")
    br_a = mm(ya, S.weight("bsc"), mode="nn", name="branch_sc")
    br_b = mm(yb, S.weight("bssm"), mode="nn", name="branch_ssm")
    merged = _merge_fwd(p_gate, b_gate, br_a, br_b, "merge_fwd")
    x1 = mm(merged, S.weight("out"), mode="nn", name="out_proj", extras=(x,), epi=_epi_add)
    h2 = _rms_fwd(x1, norm_mlp, "rms_mlp")
    r_act = mm(h2, S.weight("w1"), mode="nn", name="mlp_up", epi=_epi_relu2, out_dtypes=(BF16,))
    x2 = mm(r_act, S.weight("w2"), mode="nn", name="mlp_down", extras=(x1,), epi=_epi_add)
    dx2, dx2b, g_norm_final, loss_row = _final(x2, norm_final, tgt, "final")

    S.grad("w2", mm(r_act, dx2b, mode="tn", name="mlp_down_dw", out_dtypes=(BF16,)))
    da = mm(dx2b, S.weight("w2"), mode="nt", name="mlp_down_dx", extras=(r_act,), epi=_epi_relu2_bwd, out_dtypes=(BF16,))
    S.grad("w1", mm(h2, da, mode="tn", name="mlp_up_dw", out_dtypes=(BF16,)))
    dh2 = mm(da, S.weight("w1"), mode="nt", name="mlp_up_dx")
    dx1, dx1b, g_norm_mlp = _rms_bwd(x1, norm_mlp + S.tok(), dh2, dx2, "rms_mlp_bwd")
    S.grad("out", mm(merged, dx1b, mode="tn", name="out_proj_dw", out_dtypes=(BF16,)))
    dmerged = mm(dx1b, S.weight("out"), mode="nt", name="out_proj_dx")
    dbr_a, dbr_b, d_gate, g_b_gate = _merge_bwd(dmerged, p_gate, b_gate, br_a, br_b, "merge_bwd")
    S.grad("bssm", mm(yb, dbr_b, mode="tn", name="branch_ssm_dw", out_dtypes=(BF16,)))
    S.grad("bsc", mm(ya, dbr_a, mode="tn", name="branch_sc_dw", out_dtypes=(BF16,)))
    dyb = mm(dbr_b, S.weight("bssm"), mode="nt", name="branch_ssm_dx")
    dya = mm(dbr_a, S.weight("bsc"), mode="nt", name="branch_sc_dx")
    dy, d_z, g_ssm_norm_w = _gnorm_bwd(y, p_z, ssm_norm_w + S.tok(), dyb, "gnorm_bwd")
    dxs, dB, dC, ddt_e, dcs_e, dd_p = carrying(_ssd_bwd, xbc, dt_e, cs_e, d_e, states, dy, name="ssd_bwd")
    d_dt, g_dt_bias, g_a_log, g_d_skip = _ssd_post(ddt_e, dcs_e, dd_p, p_dt, dt_bias, a_log, n_heads, "ssd_post")
    d_xbc, g_ssm_w, g_ssm_b = carrying(_ssm_conv_bwd, p_xbc, ssm_w, ssm_b, dxs, dB, dC, name="ssm_conv_bwd")
    d_scB, d_scC, d_scX, g_sc_w = _sc_bwd(p_sc, sc_w, dya, "sc_bwd")
    d_sc = jnp.concatenate([d_scB, d_scC, d_scX], axis=1)
    pieces = [("sc", d_sc), ("z", d_z), ("xbc", d_xbc), ("dt", d_dt), ("gate", d_gate)]
    S.grad("win", {k: mm(hb, d, mode="tn", name="proj_dw_" + k, out_dtypes=(BF16,)) for k, d in pieces})
    pieces = [(k, d + S.tok().astype(d.dtype) if k == "dt" else d) for k, d in pieces]
    dh = mm([d for _, d in pieces], [S.weight(k) for k, _ in pieces], mode="nt", name="proj_dx")
    grad_x, _, g_norm_mix = _rms_bwd(x, norm_mix, dh, dx1, "rms_mix_bwd")

    g_small = dict(norm_mix=g_norm_mix, b_gate=g_b_gate, sc_conv_w=g_sc_w, ssm_conv_w=g_ssm_w, ssm_conv_b=g_ssm_b,
                   dt_bias=g_dt_bias, A_log=g_a_log, D_skip=g_d_skip, ssm_norm_w=g_ssm_norm_w, norm_mlp=g_norm_mlp,
                   norm_final=g_norm_final, loss=loss_row)
    return grad_x, g_small


class _Place:
    def __init__(self, k=0):
        x, y, c = lax.axis_index("x"), lax.axis_index("y"), lax.axis_index("c")
        self.x = 1 - x if k & 4 else x
        self.y = 1 - y if k & 2 else y
        self.c = 1 - c if k & 1 else c
        self.chip = 2 * self.x + self.y
        self.id = 2 * self.chip + self.c


ICI_PEERS = (2, 4, 6)
SIBLING = (1,)
ALL_PEERS = (1, 2, 3, 4, 5, 6, 7)


class _Comm:
    def __init__(self, arrs, out_shape, ks, src, dst, own=None, aliases=None):
        self.arrs, self.out_shape, self.ks = list(arrs), list(out_shape), tuple(ks)
        self.n = len(self.arrs)
        self.src, self.dst, self.own = src, dst, own
        self.aliases = aliases or {}
        dma = pltpu.SemaphoreType.DMA
        self.scratch = [dma((self.n, len(self.ks))), dma((self.n, len(self.ks))), dma((self.n,))]

    def _copies(self, ins, outs, sems, with_recvs):
        send_sems, recv_sems, local_sems = sems
        me = _Place()
        owns, sends, recvs = [], [], []
        for a in range(self.n):
            if self.own is not None:
                s, d = self.own(a, ins[a], outs[a], me)
                owns.append(pltpu.make_async_copy(s, d, local_sems.at[a]))
            for i, k in enumerate(self.ks):
                peer = _Place(k)
                for sender, lst in ((me, sends), (peer, recvs)) if with_recvs else ((me, sends),):
                    lst.append(pltpu.make_async_remote_copy(
                        src_ref=self.src(a, ins[a], me, peer), dst_ref=self.dst(a, outs[a], sender),
                        send_sem=send_sems.at[a, i], recv_sem=recv_sems.at[a, i],
                        device_id=(peer.x, peer.y, peer.c), device_id_type=MESH))
        return owns, sends, recvs

    def start(self, ins, outs, sems):
        owns, sends, _ = self._copies(ins, outs, sems, False)
        for cp in owns + sends:
            cp.start()

    def finish(self, ins, outs, sems):
        owns, sends, recvs = self._copies(ins, outs, sems, True)
        for cp in recvs:
            cp.wait_recv()
        for cp in sends:
            cp.wait_send()
        for cp in owns:
            cp.wait()


class _GatherBoth:
    def __init__(self, shards):
        self.arrs, self.n, self.aliases = list(shards), len(shards), {}
        self.out_shape = [_sds((4, 2) + s.shape, s.dtype) for s in shards]
        dma = pltpu.SemaphoreType.DMA
        self.scratch = [dma((self.n, 7)), dma((self.n, 7)), dma((self.n,))]

    def _copy(self, a, j, src, slot, to, outs, sems):
        return pltpu.make_async_remote_copy(src_ref=src, dst_ref=outs[a].at[slot.chip, slot.c], send_sem=sems[0].at[a, j],
                                            recv_sem=sems[1].at[a, j], device_id=(to.x, to.y, to.c), device_id_type=MESH)

    def start(self, ins, outs, sems):
        me, sib = _Place(), _Place(1)
        for a in range(self.n):
            pltpu.make_async_copy(ins[a], outs[a].at[me.chip, me.c], sems[2].at[a]).start()
            self._copy(a, 0, ins[a], me, sib, outs, sems).start()
            for i, k in enumerate(ICI_PEERS):
                self._copy(a, 1 + i, ins[a], me, _Place(k), outs, sems).start()

    def finish(self, ins, outs, sems):
        me, sib = _Place(), _Place(1)
        passed = []
        for i, k in enumerate(ICI_PEERS):
            peer = _Place(k)
            for a in range(self.n):
                self._copy(a, 1 + i, ins[a], peer, peer, outs, sems).wait_recv()
                cp = self._copy(a, 4 + i, outs[a].at[peer.chip, peer.c], peer, sib, outs, sems)
                cp.start()
                passed.append(cp)
        for a in range(self.n):
            self._copy(a, 0, ins[a], sib, sib, outs, sems).wait_recv()
            for i, k in enumerate(ICI_PEERS):
                far = _Place(k | 1)
                self._copy(a, 4 + i, outs[a].at[far.chip, far.c], far, sib, outs, sems).wait_recv()
        for a in range(self.n):
            self._copy(a, 0, ins[a], me, sib, outs, sems).wait_send()
            for i, k in enumerate(ICI_PEERS):
                self._copy(a, 1 + i, ins[a], me, _Place(k), outs, sems).wait_send()
            pltpu.make_async_copy(ins[a], outs[a].at[me.chip, me.c], sems[2].at[a]).wait()
        for cp in passed:
            cp.wait_send()


def _run_comm(comm, name, after=()):
    n, n_after = comm.n, len(after)

    def body(*refs):
        ins, outs, sems = refs[:n], refs[n + n_after:2 * n + n_after], refs[2 * n + n_after:]
        comm.start(ins, outs, sems)
        comm.finish(ins, outs, sems)

    return list(pl.pallas_call(body, in_specs=[ANY] * (n + n_after), out_specs=[ANY] * n, out_shape=comm.out_shape,
                               scratch_shapes=comm.scratch, input_output_aliases=dict(comm.aliases), name=name)(*comm.arrs, *after))


def _gather_sibling(bufs):
    return _Comm(bufs, [_sds(b.shape, b.dtype) for b in bufs], SIBLING,
                 src=lambda a, i, me, p: i.at[:, me.c], dst=lambda a, o, s: o.at[:, s.c], aliases={a: a for a in range(len(bufs))})


def _scatter_sibling(parts):
    return _Comm(parts, [_sds((4,) + p.shape[2:], p.dtype) for p in parts], SIBLING,
                 src=lambda a, i, me, p: i.at[:, p.c], dst=lambda a, o, s: o)


HBM_SPEC = pl.BlockSpec(memory_space=pltpu.HBM)
SEM_SPEC = pl.BlockSpec(memory_space=pltpu.SEMAPHORE)
DATAFLOW = pltpu.SideEffectType.DATAFLOW_SIDE_EFFECTING


def _tiles_2d(R, C, max_rows=256):
    if R % max_rows == 0:
        return max_rows, C, R // max_rows, lambda i: (i, 0)
    if R <= 2 * max_rows or C % 256:
        return R, C, 1, lambda i: (0, 0)
    return R, 256, C // 256, lambda i: (0, i)


def _ici_copy(gather, a, srcs, lands, send_sems, recv_sems, i, me, peer, sender):
    src = lands[a].at[me.chip, me.c] if gather else srcs[a].at[peer.chip]
    dst = lands[a].at[sender.chip, sender.c] if gather else lands[a].at[sender.chip]
    j = a * len(ICI_PEERS) + i
    return pltpu.make_async_remote_copy(src_ref=src, dst_ref=dst, send_sem=send_sems.at[j], recv_sem=recv_sems.at[j],
                                        device_id=(peer.x, peer.y, peer.c), device_id_type=MESH)


def _ici_start(srcs, lands, gather, name):
    n, n_s = len(lands), len(srcs)
    bufs = list(srcs) + list(lands)

    def body(*refs):
        src_refs, land_refs = refs[:n_s], refs[n_s:n_s + n]
        send_sems, recv_sems = refs[n_s + n], refs[n_s + n + 1]
        token = refs[-1]
        me = _Place()
        for a in range(n):
            for i, k in enumerate(ICI_PEERS):
                _ici_copy(gather, a, src_refs, land_refs, send_sems, recv_sems, i, me, _Place(k), me).start()
        token[...] = jnp.zeros_like(token)

    dma = pltpu.SemaphoreType.DMA((n * len(ICI_PEERS),))
    outs = pl.pallas_call(
        body, name=name, out_shape=(dma, dma, *[pltpu.HBM(v.shape, v.dtype) for v in bufs], _sds((8, LANES), F32)),
        in_specs=(HBM_SPEC,) * len(bufs),
        out_specs=(SEM_SPEC, SEM_SPEC) + (HBM_SPEC,) * len(bufs) + (pl.BlockSpec(memory_space=pltpu.VMEM),),
        input_output_aliases={j: 2 + j for j in range(len(bufs))}, compiler_params=pltpu.CompilerParams(has_side_effects=DATAFLOW),
    )(*[pltpu.with_memory_space_constraint(v, pltpu.HBM) for v in bufs])
    return outs[0], outs[1], list(outs[2:2 + n_s]), list(outs[2 + n_s:2 + n_s + n]), outs[-1]


def _ici_wait(flight, after, gather, name):
    send_sems, recv_sems, srcs, lands, _ = flight
    n, n_s = len(lands), len(srcs)
    bufs = srcs + lands

    def body(*refs):
        src_refs, land_refs = refs[:n_s], refs[n_s:n_s + n]
        s_sems, r_sems = refs[n_s + n], refs[n_s + n + 1]
        me = _Place()
        for a in range(n):
            for i, k in enumerate(ICI_PEERS):
                peer = _Place(k)
                cp = _ici_copy(gather, a, src_refs, land_refs, s_sems, r_sems, i, me, peer, peer)
                cp.wait_send()
                cp.wait_recv()

    outs = pl.pallas_call(
        body, name=name, out_shape=tuple(pltpu.HBM(v.shape, v.dtype) for v in bufs),
        in_specs=(HBM_SPEC,) * len(bufs) + (SEM_SPEC, SEM_SPEC) + (ANY,) * len(after), out_specs=(HBM_SPEC,) * len(bufs),
        input_output_aliases={j: j for j in range(len(bufs))}, compiler_params=pltpu.CompilerParams(has_side_effects=DATAFLOW),
    )(*bufs, send_sems, recv_sems, *after)
    return list(outs[n_s:])


def _own_shards(shards, after, name):
    n = len(shards)
    vmem = pl.BlockSpec(memory_space=pltpu.VMEM)

    def body(*refs):
        ins, outs, cast, sems = refs[:n], refs[n + 1:2 * n + 1], refs[2 * n + 1:3 * n + 1], refs[3 * n + 1]
        me = _Place()
        copies = []
        for a in range(n):
            cast[a][...] = ins[a][...].astype(BF16)
            copies.append(pltpu.make_async_copy(cast[a], outs[a].at[me.chip, me.c], sems.at[a]))
            copies[-1].start()
        for cp in copies:
            cp.wait()

    return list(pl.pallas_call(
        body, in_specs=[vmem] * n + [ANY], out_specs=[ANY] * n, out_shape=[_sds((4, 2) + s.shape, BF16) for s in shards],
        scratch_shapes=[pltpu.VMEM(s.shape, BF16) for s in shards] + [pltpu.SemaphoreType.DMA((n,))], name=name)(*shards, after))


def _col_pieces(widths):
    out, c = [], 0
    for k, w in widths:
        out.append((k, c, w))
        c += w
    return out


def _split_range(c0, n, bounds):
    parts, c = [], c0
    while c < c0 + n:
        r = max(i for i in range(len(bounds) - 1) if bounds[i] <= c)
        w = min(c0 + n, bounds[r + 1]) - c
        parts.append((r, c - bounds[r], w))
        c += w
    return parts


def _win_unpack(g, widths, name):
    n, R, C = g.shape
    tr = min(256, R)
    pieces = _col_pieces(widths)
    padded = [-(-w // LANES) * LANES for _, _, w in pieces]
    shard_bounds = [s * C for s in range(n + 1)]

    def body(g_ref, *o_refs):
        for (k, c0, w), o_ref in zip(pieces, o_refs):
            for t in range(0, o_ref.shape[1], LANES):
                valid = max(0, min(LANES, w - t))
                cols = [g_ref[s, :, o:o + ww] for s, o, ww in _split_range(c0 + t, valid, shard_bounds)] if valid else []
                if valid < LANES:
                    cols.append(jnp.zeros((tr, LANES - valid), g_ref.dtype))
                o_ref[:, t:t + LANES] = cols[0] if len(cols) == 1 else jnp.concatenate(cols, axis=1)

    return pl.pallas_call(
        body, grid=(R // tr,), in_specs=[pl.BlockSpec((n, tr, C), lambda i: (0, i, 0))],
        out_specs=[pl.BlockSpec((tr, p), lambda i: (i, 0)) for p in padded],
        out_shape=[_sds((R, p), g.dtype) for p in padded], name=name, compiler_params=_params("parallel"))(g)


def _win_pack(grads, widths, n, name):
    R = grads[0].shape[0]
    tr = min(256, R)
    pieces = _col_pieces(widths)
    total = pieces[-1][1] + pieces[-1][2]
    C = total // n
    bounds = [c0 for _, c0, _ in pieces] + [total]

    def body(*refs):
        g_refs, o_ref = refs[:-1], refs[-1]

        def tile_t(c0):
            cols = [g_refs[r][:, o:o + ww] for r, o, ww in _split_range(c0, LANES, bounds)]
            tile = cols[0] if len(cols) == 1 else jnp.concatenate(cols, axis=1)
            return tile.astype(F32).T

        for s in range(n):
            full = C // LANES * LANES
            for t in range(0, full, LANES):
                o_ref[s, t:t + LANES, :] = tile_t(s * C + t).astype(o_ref.dtype)
            if full < C:
                o_ref[s, full:C, :] = tile_t(s * C + C - LANES)[LANES - (C - full):, :].astype(o_ref.dtype)

    return pl.pallas_call(
        body, grid=(R // tr,), in_specs=[pl.BlockSpec((tr, gr.shape[1]), lambda i: (i, 0)) for gr in grads],
        out_specs=pl.BlockSpec((n, C, tr), lambda i: (0, 0, i)), out_shape=_sds((n, C, R), grads[0].dtype),
        name=name, compiler_params=_params("parallel"))(*grads)


def _gather_all(arrs):
    return _Comm(arrs, [_sds((N_DEV,) + a.shape, a.dtype) for a in arrs], ALL_PEERS,
                 src=lambda a, i, me, p: i, dst=lambda a, o, s: o.at[s.id], own=lambda a, i, o, me: (i, o.at[me.id]))


def _add_halves(parts, got, name):
    n, _, R, C = parts.shape
    br, bc, nb, at = _tiles_2d(R, C, max_rows=1024)
    place = jnp.stack([lax.axis_index("c"), 2 * lax.axis_index("x") + lax.axis_index("y")]).astype(jnp.int32)

    def body(q_ref, p_ref, g_ref, o_ref, land_ref):
        s = (p_ref[0, 0].astype(F32) + g_ref[0].astype(F32)).astype(o_ref.dtype)
        o_ref[0] = s

        @pl.when(pl.program_id(1) == q_ref[1])
        def _():
            land_ref[0] = s

    spec = pltpu.PrefetchScalarGridSpec(
        num_scalar_prefetch=1, grid=(nb, n),
        in_specs=[pl.BlockSpec((1, 1, br, bc), lambda i, q, q_ref: (q, q_ref[0]) + at(i)), pl.BlockSpec((1, br, bc), lambda i, q, q_ref: (q,) + at(i))],
        out_specs=[pl.BlockSpec((1, br, bc), lambda i, q, q_ref: (q,) + at(i)), pl.BlockSpec((1, br, bc), lambda i, q, q_ref: (q_ref[1],) + at(i))])
    return pl.pallas_call(body, grid_spec=spec, out_shape=[_sds((n, R, C), parts.dtype)] * 2, name=name,
                          compiler_params=_params("parallel", "arbitrary"))(place, parts, got)


def _adam(w, m, v, gparts, name, comm=None):
    R, C = w.shape
    n = gparts.shape[0]
    br, bc, nb, at = _tiles_2d(R, C, max_rows=512)
    c1 = 1.0 / (1.0 - ADAM_B1 ** ADAM_STEP)
    c2 = 1.0 / (1.0 - ADAM_B2 ** ADAM_STEP)

    def body(w_ref, m_ref, v_ref, g_ref, go_ref, d_ref, mo_ref, vo_ref):
        g = g_ref[0].astype(F32)
        for s in range(1, n):
            g = g + g_ref[s].astype(F32)
        mn = ADAM_B1 * m_ref[...] + (1.0 - ADAM_B1) * g
        vn = ADAM_B2 * v_ref[...] + (1.0 - ADAM_B2) * (g * g)
        go_ref[...] = g
        mo_ref[...] = mn
        vo_ref[...] = vn
        d_ref[...] = -ADAM_LR * ((mn * c1) / (jnp.sqrt(vn * c2) + ADAM_EPS) + ADAM_WD * w_ref[...])

    blk = pl.BlockSpec((br, bc), at)
    outs, carried = _call(
        body, grid=(nb,), in_specs=[blk, blk, blk, pl.BlockSpec((n, br, bc), lambda i: (0,) + at(i))],
        out_specs=[blk] * 4, out_shape=[_sds((R, C), F32)] * 4, args=[w, m, v, gparts], name=name, sem=("parallel",), comm=comm)
    return outs if comm is None else (outs, carried)


_SMALL_ORDER = ("norm_mix", "b_gate", "sc_conv_w", "ssm_conv_w", "ssm_conv_b", "dt_bias", "A_log", "D_skip", "ssm_norm_w",
                "norm_mlp", "norm_final", "loss")
_REPLICATED = ("norm_mix", "b_gate", "ssm_conv_b", "dt_bias", "A_log", "D_skip", "ssm_norm_w", "norm_mlp", "norm_final")


def _cols_to_slots(g, n):
    R = g.shape[0]
    return jnp.transpose(g.reshape(R, n, g.shape[1] // n), (1, 0, 2))


def _slots_to_cols(g):
    n, R, C = g.shape
    return jnp.transpose(g, (1, 0, 2)).reshape(R, n * C)


def kernel(x, norm_mix, w_in, b_gate, sc_conv_w, ssm_conv_w, ssm_conv_b, dt_bias, A_log, D_skip, ssm_norm_w, w_branch_sc, w_branch_ssm, w_out, norm_mlp, w_mlp1, w_mlp2, norm_final, loss_target, m_norm_mix, m_w_in, m_b_gate, m_sc_conv_w, m_ssm_conv_w, m_ssm_conv_b, m_dt_bias, m_A_log, m_D_skip, m_ssm_norm_w, m_w_branch_sc, m_w_branch_ssm, m_w_out, m_norm_mlp, m_w_mlp1, m_w_mlp2, m_norm_final, v_norm_mix, v_w_in, v_b_gate, v_sc_conv_w, v_ssm_conv_w, v_ssm_conv_b, v_dt_bias, v_A_log, v_D_skip, v_ssm_norm_w, v_w_branch_sc, v_w_branch_ssm, v_w_out, v_norm_mlp, v_w_mlp1, v_w_mlp2, v_norm_final):
    T, D = x.shape[1], x.shape[2]
    n_inner = 2 * D
    n_heads = n_inner // HEADDIM
    n_xbc = n_inner + 2 * NGROUPS * NSTATE
    me = 4 * lax.axis_index("x") + 2 * lax.axis_index("y") + lax.axis_index("c")

    in_cols = [("sc", 3 * D), ("z", n_inner), ("xbc", n_xbc), ("dt", n_heads), ("gate", 2 * D)]
    by_owner = lambda b: b.reshape((N_DEV,) + b.shape[2:])
    to_owner = lambda g: g.reshape((4, 2) + g.shape[1:])
    rows_of = lambda g: to_owner(g.reshape((N_DEV, g.shape[0] // N_DEV) + g.shape[1:]))
    cols_of = lambda g: to_owner(_cols_to_slots(g, N_DEV))

    class Schedule(_NoExchange):
        late = ("bssm", "bsc", "out", "w1", "w2")
        gather_sib = dict(gnorm_fwd=("bsc", "bssm", "out"), branch_ssm=("w1", "w2"))
        scatter_sib = dict(mlp_up_dx=("w2", "w1"), branch_ssm_dx=("out", "bssm", "bsc"))
        shards = dict(bsc=w_branch_sc, bssm=w_branch_ssm, out=w_out, w1=w_mlp1, w2=w_mlp2)

        def __init__(self):
            self.W, self.staged, self.grads, self.summed, self.scatters = {}, {}, {}, {}, []
            self.token = jnp.zeros((), F32)

        def first_weights(self, bufs):
            self.W.update(zip([k for k, _ in in_cols], _win_unpack(by_owner(bufs[0]), in_cols, "win_unpack")))
            self.W.update(sc_conv_w=_slots_to_cols(by_owner(bufs[1])), ssm_conv_w=_slots_to_cols(by_owner(bufs[2])))
            lands = _own_shards([self.shards[k] for k in self.late], bufs[1], "own_shards")
            self.gather_flight = _ici_start([], lands, True, "gather_late_start")
            self.token = self.gather_flight[4][0, 0]
            self.W["dt"] = self.W["dt"] + self.token.astype(BF16)

        def tok(self):
            return self.token

        def point(self, name, values):
            if name == "mixers_done":
                lands = _ici_wait(self.gather_flight, values, True, "gather_late_wait")
                self.staged.update(zip(self.late, lands))

        def carry(self, name):
            if name == "rms_mix":
                return _GatherBoth([w_in.astype(BF16), sc_conv_w, ssm_conv_w])
            if name in self.gather_sib:
                return _gather_sibling([self.staged.pop(k) for k in self.gather_sib[name]])
            if name in self.scatter_sib:
                return _scatter_sibling([self.grads[k] for k in self.scatter_sib[name]])
            return None

        def start_scatter(self, keys, halves_and_lands):
            halves, lands = [h for h, _ in halves_and_lands], [l for _, l in halves_and_lands]
            flight = _ici_start(halves, lands, False, "scatter_%s_start" % keys[0])
            self.scatters.append((keys, flight))
            self.token = flight[4][0, 0]

        def carried(self, name, outs):
            if name == "rms_mix":
                self.first_weights(outs)
            elif name in self.gather_sib:
                for k, b in zip(self.gather_sib[name], outs):
                    full = by_owner(b)
                    self.W[k] = _slots_to_cols(full) if k == "w1" else full.reshape(-1, D)
            else:
                keys = self.scatter_sib[name]
                self.start_scatter(keys, [_add_halves(self.grads[k], b, "add_halves_" + k) for k, b in zip(keys, outs)])

        def grad(self, k, g):
            if k == "win":
                g = to_owner(_win_pack([g[k] for k, _ in in_cols], in_cols, N_DEV, "win_pack"))
                got = _run_comm(_scatter_sibling([g]), "scatter_sibling_win")[0]
                self.start_scatter(("win",), [_add_halves(g, got, "add_halves_win")])
            else:
                self.grads[k] = cols_of(g) if k == "w1" else rows_of(g)

        def finish_scatter(self, after):
            keys, flight = self.scatters.pop(0)
            return dict(zip(keys, _ici_wait(flight, after, False, "scatter_%s_wait" % keys[0])))

    S = Schedule()
    small = dict(norm_mix=norm_mix, b_gate=b_gate, ssm_conv_b=ssm_conv_b, dt_bias=dt_bias, A_log=A_log, D_skip=D_skip,
                 ssm_norm_w=ssm_norm_w, norm_mlp=norm_mlp, norm_final=norm_final)
    grad_x, g_small = _local_step(x.reshape(T, D), loss_target.reshape(T, D), S, small)

    small_flat = jnp.concatenate([g_small[k].reshape(-1) for k in _SMALL_ORDER])
    n_small = small_flat.shape[0]
    rows = -(-n_small // (8 * LANES)) * 8
    small_pack = jnp.pad(small_flat, (0, rows * LANES - n_small)).reshape(rows, LANES)

    res = {}
    big = [("w_in", "win", w_in, m_w_in, v_w_in), ("w_branch_sc", "bsc", w_branch_sc, m_w_branch_sc, v_w_branch_sc),
           ("w_branch_ssm", "bssm", w_branch_ssm, m_w_branch_ssm, v_w_branch_ssm), ("w_out", "out", w_out, m_w_out, v_w_out),
           ("w_mlp1", "w1", w_mlp1, m_w_mlp1, v_w_mlp1), ("w_mlp2", "w2", w_mlp2, m_w_mlp2, v_w_mlp2)]
    by_grad = {gk: (k, w, m, v) for k, gk, w, m, v in big}
    after = [grad_x]
    while S.scatters:
        for gk, parts in S.finish_scatter(after).items():
            k, w, m, v = by_grad[gk]
            if gk == "win":
                res_t, (small_parts,) = _adam(w.T, m.T, v.T, parts, "adam_" + k, comm=_gather_all([small_pack]))
                res[k] = [r.T for r in res_t]
            else:
                res[k] = _adam(w, m, v, parts, "adam_" + k)
            after = after + [res[k][1]]

    sizes = {k: g_small[k].size for k in _SMALL_ORDER}
    offs, o = {}, 0
    for k in _SMALL_ORDER:
        offs[k] = o
        o += sizes[k]
    rep_w = dict(norm_mix=norm_mix, b_gate=b_gate, ssm_conv_b=ssm_conv_b, dt_bias=dt_bias, A_log=A_log, D_skip=D_skip,
                 ssm_norm_w=ssm_norm_w, norm_mlp=norm_mlp, norm_final=norm_final)
    rep_m = dict(norm_mix=m_norm_mix, b_gate=m_b_gate, ssm_conv_b=m_ssm_conv_b, dt_bias=m_dt_bias, A_log=m_A_log, D_skip=m_D_skip,
                 ssm_norm_w=m_ssm_norm_w, norm_mlp=m_norm_mlp, norm_final=m_norm_final)
    rep_v = dict(norm_mix=v_norm_mix, b_gate=v_b_gate, ssm_conv_b=v_ssm_conv_b, dt_bias=v_dt_bias, A_log=v_A_log, D_skip=v_D_skip,
                 ssm_norm_w=v_ssm_norm_w, norm_mlp=v_norm_mlp, norm_final=v_norm_final)

    def pack(d):
        segs = [jnp.pad(d[k].astype(F32).reshape(-1), (0, sizes[k] - d[k].size)) if k in d else jnp.zeros((sizes[k],), F32)
                for k in _SMALL_ORDER]
        return jnp.pad(jnp.concatenate(segs), (0, rows * LANES - n_small)).reshape(rows, LANES)

    sm = _adam(pack(rep_w), pack(rep_m), pack(rep_v), small_parts, "adam_small")
    sm = [s.reshape(-1) for s in sm]
    for k in _REPLICATED:
        n_k = rep_w[k].shape[0]
        res[k] = tuple(s[offs[k]:offs[k] + n_k] for s in sm)
    loss = sm[0][offs["loss"]]
    for k, w, m, v, K, full in (("sc_conv_w", sc_conv_w, m_sc_conv_w, v_sc_conv_w, SC_K, D),
                                ("ssm_conv_w", ssm_conv_w, m_ssm_conv_w, v_ssm_conv_w, SSM_K, n_xbc)):
        g_full = sm[0][offs[k]:offs[k] + K * full].reshape(K, full)
        cw = full // N_DEV
        g_mine = lax.dynamic_slice_in_dim(g_full, me * cw, cw, axis=1)
        res[k] = _adam(w, m, v, g_mine[None], "adam_" + k)

    order = ("norm_mix", "w_in", "b_gate", "sc_conv_w", "ssm_conv_w", "ssm_conv_b", "dt_bias", "A_log", "D_skip", "ssm_norm_w",
             "w_branch_sc", "w_branch_ssm", "w_out", "norm_mlp", "w_mlp1", "w_mlp2", "norm_final")
    outs = [loss, grad_x.reshape(1, T, D)]
    for j in range(4):
        outs += [res[k][j] for k in order]
    return tuple(outs)
```

```python
import jax
import jax.numpy as jnp
from jax import lax
from jax.experimental import pallas as pl
from jax.experimental.pallas import tpu as pltpu

F32 = jnp.float32
BF16 = jnp.bfloat16

EPS = 1e-6
N_DEV = 8
HEADDIM = 64
NSTATE = 128
CHUNK = 128
NGROUPS = 8
GROUP_W = 256
SC_K = 3
SSM_K = 4
LANES = 128

ADAM_LR = 0.001
ADAM_B1 = 0.9
ADAM_B2 = 0.999
ADAM_EPS = 1e-08
ADAM_WD = 0.01
ADAM_STEP = 10

NN = (((1,), (0,)), ((), ()))
NT = (((1,), (1,)), ((), ()))
TN = (((0,), (0,)), ((), ()))
_DIMS = {"nn": NN, "nt": NT, "tn": TN}

ANY = pl.BlockSpec(memory_space=pl.ANY)
MESH = pl.DeviceIdType.MESH


def _sds(shape, dtype):
    return jax.ShapeDtypeStruct(tuple(shape), dtype)


def _dot(a, b, dims=NN):
    return lax.dot_general(a, b, dims, preferred_element_type=F32)


def _dot3(a, b, dims=NN):
    return lax.dot_general(a, b, dims, preferred_element_type=F32, precision=lax.Precision.HIGH)


def _params(*sem):
    return pltpu.CompilerParams(dimension_semantics=tuple(sem))


def _call(body, *, grid, in_specs, out_specs, out_shape, args, name, sem, scratch=(), comm=None):
    if comm is None:
        outs = pl.pallas_call(body, grid=grid, in_specs=list(in_specs), out_specs=list(out_specs), out_shape=list(out_shape),
                              scratch_shapes=list(scratch), name=name, compiler_params=_params(*sem))(*args)
        return list(outs), None
    n, n_in, n_out, n_scr = comm.n, len(in_specs), len(out_shape), len(scratch)

    def wrapped(*refs):
        ins, c_in = refs[:n_in], refs[n_in:n_in + n]
        outs, c_out = refs[n_in + n:n_in + n + n_out], refs[n_in + n + n_out:n_in + 2 * n + n_out]
        rest = refs[n_in + 2 * n + n_out:]
        scr, sems = rest[:n_scr], rest[n_scr:]
        first, last = None, None
        for d, g in enumerate(grid):
            f, l = pl.program_id(d) == 0, pl.program_id(d) == g - 1
            first, last = (f, l) if first is None else (first & f, last & l)

        @pl.when(first)
        def _():
            comm.start(c_in, c_out, sems)

        body(*ins, *outs, *scr)

        @pl.when(last)
        def _():
            comm.finish(c_in, c_out, sems)

    outs = pl.pallas_call(
        wrapped, grid=grid, in_specs=list(in_specs) + [ANY] * n, out_specs=list(out_specs) + [ANY] * n,
        out_shape=list(out_shape) + comm.out_shape, scratch_shapes=list(scratch) + comm.scratch,
        input_output_aliases={n_in + i: n_out + o for i, o in comm.aliases.items()},
        name=name, compiler_params=_params(*["arbitrary"] * len(grid)))(*args, *comm.arrs)
    return list(outs[:n_out]), list(outs[n_out:])


MM_VMEM_BUDGET = 44 * 2 ** 20


def _mm_tiles(M, N, k_bytes, mn_bytes):
    best = None
    for tm in (2048, 1024, 512, 256, 128):
        for tn in (1024, 512, 256, 128):
            if M % tm or N % tn:
                continue
            need = 2 * ((tm + tn) * k_bytes + tm * tn * mn_bytes) + 4 * tm * tn * 4
            if need <= MM_VMEM_BUDGET and (best is None or (tm * tn, tm) > (best[0] * best[1], best[0])):
                best = (tm, tn)
    assert best is not None, (M, N, k_bytes, mn_bytes)
    return best


def _mm(a, b, *, mode, name, extras=(), epi=None, out_dtypes=(F32,), comm=None):
    a_list = list(a) if isinstance(a, (list, tuple)) else [a]
    b_list = list(b) if isinstance(b, (list, tuple)) else [b]
    if mode == "nn":
        M, N = a_list[0].shape[0], b_list[0].shape[1]
    elif mode == "nt":
        M, N = a_list[0].shape[0], b_list[0].shape[0]
    else:
        M, N = a_list[0].shape[1], b_list[0].shape[1]
    k_bytes = sum((av.shape[0] if mode == "tn" else av.shape[1]) * av.dtype.itemsize for av in a_list)
    mn_bytes = sum(e.dtype.itemsize for e in extras) + sum(jnp.dtype(d).itemsize for d in out_dtypes)
    tm, tn = _mm_tiles(min(M, 2048), min(N, 1024), k_bytes, mn_bytes) if M % 128 == 0 and N % 128 == 0 else (M, N)
    assert M % tm == 0 and N % tn == 0
    a_specs, b_specs = [], []
    for av, bv in zip(a_list, b_list):
        K = av.shape[0] if mode == "tn" else av.shape[1]
        a_specs.append(pl.BlockSpec((K, tm), lambda i, j: (0, i)) if mode == "tn" else pl.BlockSpec((tm, K), lambda i, j: (i, 0)))
        b_specs.append(pl.BlockSpec((tn, K), lambda i, j: (j, 0)) if mode == "nt" else pl.BlockSpec((K, tn), lambda i, j: (0, j)))
    mn_spec = pl.BlockSpec((tm, tn), lambda i, j: (i, j))
    n_p, n_ex = len(a_list), len(extras)
    dims = _DIMS[mode]

    def body(*refs):
        acc = _dot(refs[0][...], refs[n_p][...], dims)
        for p in range(1, n_p):
            acc = acc + _dot(refs[p][...], refs[n_p + p][...], dims)
        rest = refs[2 * n_p:]
        res = (acc,) if epi is None else epi(acc, *[r[...] for r in rest[:n_ex]])
        for o_ref, r in zip(rest[n_ex:], res):
            o_ref[...] = r.astype(o_ref.dtype)

    outs, carried = _call(
        body, grid=(M // tm, N // tn), in_specs=a_specs + b_specs + [mn_spec] * n_ex,
        out_specs=[mn_spec] * len(out_dtypes), out_shape=[_sds((M, N), d) for d in out_dtypes],
        args=a_list + b_list + list(extras), name=name, sem=("parallel", "parallel"), comm=comm)
    res = outs[0] if len(outs) == 1 else outs
    return res if comm is None else (res, carried)


def _epi_add(acc, r):
    return (acc + r,)


def _epi_relu2(acc):
    p = jnp.maximum(acc, 0.0)
    return (p * p,)


def _epi_relu2_bwd(acc, r):
    return (acc * (2.0 * jnp.sqrt(r.astype(F32))),)


ROW_TILE = 512


def _row(tr, n):
    return pl.BlockSpec((tr, n), lambda i: (i, 0))


def _vec(n):
    return pl.BlockSpec((1, n), lambda i: (0, 0))


def _rms_fwd(x, w, name, comm=None):
    T, D = x.shape
    tr = min(ROW_TILE, T)

    def body(x_ref, w_ref, o_ref):
        xv = x_ref[...]
        r = lax.rsqrt(jnp.mean(xv * xv, axis=-1, keepdims=True) + EPS)
        o_ref[...] = (xv * r * w_ref[...]).astype(BF16)

    outs, carried = _call(body, grid=(T // tr,), in_specs=[_row(tr, D), _vec(D)], out_specs=[_row(tr, D)],
                          out_shape=[_sds((T, D), BF16)], args=[x, w], name=name, sem=("parallel",), comm=comm)
    return outs[0] if comm is None else (outs[0], carried)


def _rms_bwd(x, w, dh, dres, name):
    T, D = x.shape
    tr = min(ROW_TILE, T)

    def body(x_ref, w_ref, dh_ref, dres_ref, dx_ref, dxb_ref, dw_ref):
        @pl.when(pl.program_id(0) == 0)
        def _():
            dw_ref[...] = jnp.zeros_like(dw_ref)

        xv = x_ref[...]
        r = lax.rsqrt(jnp.mean(xv * xv, axis=-1, keepdims=True) + EPS)
        xh = xv * r
        dh_v = dh_ref[...]
        dw_ref[...] += jnp.sum(dh_v * xh, axis=0, keepdims=True)
        dxh = dh_v * w_ref[...]
        dx = r * (dxh - xh * jnp.mean(dxh * xh, axis=-1, keepdims=True)) + dres_ref[...]
        dx_ref[...] = dx
        dxb_ref[...] = dx.astype(BF16)

    return pl.pallas_call(
        body, grid=(T // tr,), in_specs=[_row(tr, D), _vec(D), _row(tr, D), _row(tr, D)],
        out_specs=[_row(tr, D), _row(tr, D), _vec(D)],
        out_shape=[_sds((T, D), F32), _sds((T, D), BF16), _sds((1, D), F32)],
        name=name, compiler_params=_params("arbitrary"))(x, w, dh, dres)


def _final(x2, w, tgt, name):
    T, D = x2.shape
    tr = min(ROW_TILE, T)

    def body(x_ref, w_ref, t_ref, dx_ref, dxb_ref, dw_ref, loss_ref):
        @pl.when(pl.program_id(0) == 0)
        def _():
            dw_ref[...] = jnp.zeros_like(dw_ref)
            loss_ref[...] = jnp.zeros_like(loss_ref)

        xv = x_ref[...]
        wv = w_ref[...]
        r = lax.rsqrt(jnp.mean(xv * xv, axis=-1, keepdims=True) + EPS)
        xh = xv * r
        err = xh * wv - t_ref[...]
        part = jnp.sum(jnp.sum(err * err, axis=1, keepdims=True), axis=0, keepdims=True) * (0.5 / D)
        loss_ref[...] += jnp.broadcast_to(part, loss_ref.shape)
        dy = err * (1.0 / D)
        dw_ref[...] += jnp.sum(dy * xh, axis=0, keepdims=True)
        dxh = dy * wv
        dx = r * (dxh - xh * jnp.mean(dxh * xh, axis=-1, keepdims=True))
        dx_ref[...] = dx
        dxb_ref[...] = dx.astype(BF16)

    return pl.pallas_call(
        body, grid=(T // tr,), in_specs=[_row(tr, D), _vec(D), _row(tr, D)],
        out_specs=[_row(tr, D), _row(tr, D), _vec(D), _vec(LANES)],
        out_shape=[_sds((T, D), F32), _sds((T, D), BF16), _sds((1, D), F32), _sds((1, LANES), F32)],
        name=name, compiler_params=_params("arbitrary"))(x2, w, tgt)


def _silu_parts(z):
    s = jax.nn.sigmoid(z)
    return z * s, s * (1.0 + z * (1.0 - s))


def _gnorm_fwd(y, z, w, name, comm=None):
    T, N = y.shape
    tr = min(ROW_TILE, T)

    def body(y_ref, z_ref, w_ref, o_ref):
        for g in range(N // GROUP_W):
            sl = slice(g * GROUP_W, (g + 1) * GROUP_W)
            silu, _ = _silu_parts(z_ref[:, sl])
            yz = y_ref[:, sl] * silu
            r = lax.rsqrt(jnp.mean(yz * yz, axis=-1, keepdims=True) + EPS)
            o_ref[:, sl] = (yz * r * w_ref[:, sl]).astype(BF16)

    outs, carried = _call(body, grid=(T // tr,), in_specs=[_row(tr, N), _row(tr, N), _vec(N)], out_specs=[_row(tr, N)],
                          out_shape=[_sds((T, N), BF16)], args=[y, z, w], name=name, sem=("parallel",), comm=comm)
    return outs[0] if comm is None else (outs[0], carried)


def _gnorm_bwd(y, z, w, dyb, name):
    T, N = y.shape
    tr = min(ROW_TILE, T)

    def body(y_ref, z_ref, w_ref, d_ref, dy_ref, dz_ref, dw_ref):
        @pl.when(pl.program_id(0) == 0)
        def _():
            dw_ref[...] = jnp.zeros_like(dw_ref)

        for g in range(N // GROUP_W):
            sl = slice(g * GROUP_W, (g + 1) * GROUP_W)
            yv = y_ref[:, sl]
            silu, dsilu = _silu_parts(z_ref[:, sl])
            yz = yv * silu
            r = lax.rsqrt(jnp.mean(yz * yz, axis=-1, keepdims=True) + EPS)
            yzh = yz * r
            d = d_ref[:, sl]
            dw_ref[:, sl] += jnp.sum(d * yzh, axis=0, keepdims=True)
            dyzh = d * w_ref[:, sl]
            dyz = r * (dyzh - yzh * jnp.mean(dyzh * yzh, axis=-1, keepdims=True))
            dy_ref[:, sl] = dyz * silu
            dz_ref[:, sl] = (dyz * yv * dsilu).astype(BF16)

    return pl.pallas_call(
        body, grid=(T // tr,), in_specs=[_row(tr, N), _row(tr, N), _vec(N), _row(tr, N)],
        out_specs=[_row(tr, N), _row(tr, N), _vec(N)],
        out_shape=[_sds((T, N), F32), _sds((T, N), BF16), _sds((1, N), F32)],
        name=name, compiler_params=_params("arbitrary"))(y, z, w, dyb)


def _merge_fwd(gate_raw, b_gate, br_a, br_b, name):
    T, D = br_a.shape
    tr = min(ROW_TILE, T)

    def body(g_ref, bg_ref, a_ref, b_ref, o_ref):
        g = jax.nn.sigmoid(g_ref[...] + bg_ref[...])
        o_ref[...] = (g[:, :D] * a_ref[...] + g[:, D:] * b_ref[...]).astype(BF16)

    return pl.pallas_call(body, grid=(T // tr,), in_specs=[_row(tr, 2 * D), _vec(2 * D), _row(tr, D), _row(tr, D)],
                          out_specs=_row(tr, D), out_shape=_sds((T, D), BF16), name=name,
                          compiler_params=_params("parallel"))(gate_raw, b_gate, br_a, br_b)


def _merge_bwd(dmerged, gate_raw, b_gate, br_a, br_b, name):
    T, D = br_a.shape
    tr = min(ROW_TILE, T)

    def body(d_ref, g_ref, bg_ref, a_ref, b_ref, da_ref, db_ref, dg_ref, dbg_ref):
        @pl.when(pl.program_id(0) == 0)
        def _():
            dbg_ref[...] = jnp.zeros_like(dbg_ref)

        g = jax.nn.sigmoid(g_ref[...] + bg_ref[...])
        d = d_ref[...]
        da_ref[...] = (d * g[:, :D]).astype(BF16)
        db_ref[...] = (d * g[:, D:]).astype(BF16)
        dg = jnp.concatenate([d * a_ref[...], d * b_ref[...]], axis=1) * g * (1.0 - g)
        dg_ref[...] = dg.astype(BF16)
        dbg_ref[...] += jnp.sum(dg, axis=0, keepdims=True)

    return pl.pallas_call(
        body, grid=(T // tr,), in_specs=[_row(tr, D), _row(tr, 2 * D), _vec(2 * D), _row(tr, D), _row(tr, D)],
        out_specs=[_row(tr, D), _row(tr, D), _row(tr, 2 * D), _vec(2 * D)],
        out_shape=[_sds((T, D), BF16), _sds((T, D), BF16), _sds((T, 2 * D), BF16), _sds((1, 2 * D), F32)],
        name=name, compiler_params=_params("arbitrary"))(dmerged, gate_raw, b_gate, br_a, br_b)


CB_W = 256
CONV_ROWS = 32
CONV_PAD = 8


def _rows_down(load, r0, s):
    if s == 0:
        return load(r0, r0 + CONV_ROWS)
    if r0 == 0:
        row = lax.broadcasted_iota(jnp.int32, (CONV_ROWS, CB_W), 0)
        return jnp.where(row >= s, pltpu.roll(load(0, CONV_ROWS), s, 0), 0.0)
    return load(r0 - s, r0 - s + CONV_ROWS)


def _conv_tile(load, taps, r0):
    K = len(taps)
    us = [_rows_down(load, r0, K - 1 - k) for k in range(K)]
    acc = us[K - 1] * taps[K - 1]
    for k in range(K - 1):
        acc = acc + us[k] * taps[k]
    return acc, us


def _conv_back_tile(scr, taps, r0):
    K = len(taps)
    du = scr[r0:r0 + CONV_ROWS, :] * taps[K - 1]
    for k in range(K - 1):
        s = K - 1 - k
        du = du + scr[r0 + s:r0 + s + CONV_ROWS, :] * taps[k]
    return du


def _fold8(v):
    return jnp.sum(v.reshape(CONV_ROWS // 8, 8, v.shape[1]), axis=0)


def _col(T, j0=0):
    return pl.BlockSpec((T, CB_W), lambda j: (0, j + j0))


def _sc_fwd(psc, w, name):
    T, D = psc.shape[0], psc.shape[1] // 3
    nb = D // CB_W

    def body(b_ref, c_ref, x_ref, w_ref, o_ref):
        taps = [w_ref[k:k + 1, :] for k in range(SC_K)]
        load = lambda a, b: c_ref[a:b, :] * x_ref[a:b, :]
        for r0 in range(0, T, CONV_ROWS):
            cu, _ = _conv_tile(load, taps, r0)
            o_ref[r0:r0 + CONV_ROWS, :] = (b_ref[r0:r0 + CONV_ROWS, :] * cu).astype(BF16)

    return pl.pallas_call(
        body, grid=(nb,), in_specs=[_col(T), _col(T, nb), _col(T, 2 * nb), pl.BlockSpec((SC_K, CB_W), lambda j: (0, j))],
        out_specs=_col(T), out_shape=_sds((T, D), BF16), name=name, compiler_params=_params("parallel"))(psc, psc, psc, w)


def _sc_bwd(psc, w, dya, name):
    T, D = psc.shape[0], psc.shape[1] // 3
    nb = D // CB_W

    def body(b_ref, c_ref, x_ref, w_ref, d_ref, db_ref, dc_ref, dx_ref, dw_ref, scr):
        taps = [w_ref[k:k + 1, :] for k in range(SC_K)]
        load = lambda a, b: c_ref[a:b, :] * x_ref[a:b, :]
        scr[T:T + CONV_PAD, :] = jnp.zeros((CONV_PAD, CB_W), F32)
        dw8 = [jnp.zeros((8, CB_W), F32)] * SC_K
        for r0 in range(0, T, CONV_ROWS):
            rows = slice(r0, r0 + CONV_ROWS)
            cu, us = _conv_tile(load, taps, r0)
            d = d_ref[rows, :]
            db_ref[rows, :] = (d * cu).astype(BF16)
            dcu = d * b_ref[rows, :]
            scr[rows, :] = dcu
            dw8 = [acc + _fold8(dcu * u) for acc, u in zip(dw8, us)]
        for k in range(SC_K):
            dw_ref[k:k + 1, :] = jnp.sum(dw8[k], axis=0, keepdims=True)
        for r0 in range(0, T, CONV_ROWS):
            rows = slice(r0, r0 + CONV_ROWS)
            du = _conv_back_tile(scr, taps, r0)
            dc_ref[rows, :] = (du * x_ref[rows, :]).astype(BF16)
            dx_ref[rows, :] = (du * c_ref[rows, :]).astype(BF16)

    wspec = pl.BlockSpec((SC_K, CB_W), lambda j: (0, j))
    return pl.pallas_call(
        body, grid=(nb,), in_specs=[_col(T), _col(T, nb), _col(T, 2 * nb), wspec, _col(T)],
        out_specs=[_col(T), _col(T), _col(T), wspec],
        out_shape=[_sds((T, D), BF16)] * 3 + [_sds((SC_K, D), F32)],
        scratch_shapes=[pltpu.VMEM((T + CONV_PAD, CB_W), F32)],
        name=name, compiler_params=_params("parallel"))(psc, psc, psc, w, dya)


def _ssm_conv_fwd(u, w, b, name, comm=None):
    T, N = u.shape

    def body(u_ref, w_ref, b_ref, o_ref):
        taps = [w_ref[k:k + 1, :] for k in range(SSM_K)]
        bias = b_ref[...]
        for r0 in range(0, T, CONV_ROWS):
            c, _ = _conv_tile(lambda a, b: u_ref[a:b, :], taps, r0)
            c = c + bias
            o_ref[r0:r0 + CONV_ROWS, :] = c * jax.nn.sigmoid(c)

    outs, carried = _call(
        body, grid=(N // CB_W,), in_specs=[_col(T), pl.BlockSpec((SSM_K, CB_W), lambda j: (0, j)), pl.BlockSpec((1, CB_W), lambda j: (0, j))],
        out_specs=[_col(T)], out_shape=[_sds((T, N), F32)], args=[u, w, b], name=name, sem=("parallel",), comm=comm)
    return outs[0] if comm is None else (outs[0], carried)


def _ssm_conv_bwd(u, w, b, dxs, dB, dC, name, comm=None):
    T, N = u.shape
    n_x, n_b = dxs.shape[1] // CB_W, dB.shape[1] // CB_W

    def body(u_ref, w_ref, b_ref, dx_ref, db_ref, dc_ref, du_ref, dw_ref, dbias_ref, scr):
        j = pl.program_id(0)
        taps = [w_ref[k:k + 1, :] for k in range(SSM_K)]
        bias = b_ref[...]
        scr[T:T + CONV_PAD, :] = jnp.zeros((CONV_PAD, CB_W), F32)
        dw8 = [jnp.zeros((8, CB_W), F32)] * SSM_K
        db8 = jnp.zeros((8, CB_W), F32)
        for r0 in range(0, T, CONV_ROWS):
            rows = slice(r0, r0 + CONV_ROWS)
            c, us = _conv_tile(lambda a, b: u_ref[a:b, :], taps, r0)
            _, dsilu = _silu_parts(c + bias)
            d = jnp.where(j < n_x, dx_ref[rows, :], jnp.where(j < n_x + n_b, db_ref[rows, :], dc_ref[rows, :])) * dsilu
            scr[rows, :] = d
            db8 = db8 + _fold8(d)
            dw8 = [acc + _fold8(d * u) for acc, u in zip(dw8, us)]
        dbias_ref[...] = jnp.sum(db8, axis=0, keepdims=True)
        for k in range(SSM_K):
            dw_ref[k:k + 1, :] = jnp.sum(dw8[k], axis=0, keepdims=True)
        for r0 in range(0, T, CONV_ROWS):
            du_ref[r0:r0 + CONV_ROWS, :] = _conv_back_tile(scr, taps, r0).astype(BF16)

    wspec = pl.BlockSpec((SSM_K, CB_W), lambda j: (0, j))
    bspec = pl.BlockSpec((1, CB_W), lambda j: (0, j))
    outs, carried = _call(
        body, grid=(N // CB_W,),
        in_specs=[_col(T), wspec, bspec,
                  pl.BlockSpec((T, CB_W), lambda j: (0, jnp.minimum(j, n_x - 1))),
                  pl.BlockSpec((T, CB_W), lambda j: (0, jnp.clip(j - n_x, 0, n_b - 1))),
                  pl.BlockSpec((T, CB_W), lambda j: (0, jnp.clip(j - n_x - n_b, 0, n_b - 1)))],
        out_specs=[_col(T), wspec, bspec],
        out_shape=[_sds((T, N), BF16), _sds((SSM_K, N), F32), _sds((1, N), F32)],
        scratch=[pltpu.VMEM((T + CONV_PAD, CB_W), F32)],
        args=[u, w, b, dxs, dB, dC], name=name, sem=("parallel",), comm=comm)
    return outs if comm is None else (outs, carried)


def _split3(v):
    hi = v.astype(BF16)
    r = v - hi.astype(F32)
    mid = r.astype(BF16)
    lo = (r - mid.astype(F32)).astype(BF16)
    return hi, mid, lo


def _head_expand(n_lanes):
    h = lax.broadcasted_iota(jnp.int32, (LANES, n_lanes), 0)
    l = lax.broadcasted_iota(jnp.int32, (LANES, n_lanes), 1)
    return (jnp.right_shift(l, HEADDIM.bit_length() - 1) == h).astype(BF16)


def _softplus(v):
    return jnp.maximum(v, 0.0) + jnp.log1p(jnp.exp(-jnp.abs(v)))


def _ssd_prep(dt_raw, dt_bias, a_log, n_inner, name):
    T = dt_raw.shape[0]

    def body(r_ref, b_ref, al_ref, ex_ref, dt_ref, cs_ref):
        dt = _softplus(r_ref[...] + b_ref[...])
        a = dt * (-jnp.exp(al_ref[...]))
        i = lax.broadcasted_iota(jnp.int32, (CHUNK, CHUNK), 0)
        j = lax.broadcasted_iota(jnp.int32, (CHUNK, CHUNK), 1)
        tri = (j <= i).astype(BF16)
        cs = sum(_dot(tri, p) for p in _split3(a))
        ex = ex_ref[...]
        dt_ref[...] = sum(_dot(p, ex) for p in _split3(dt))
        cs_ref[...] = sum(_dot(p, ex) for p in _split3(cs))

    blk = pl.BlockSpec((CHUNK, LANES), lambda c: (c, 0))
    out = pl.BlockSpec((CHUNK, n_inner), lambda c: (c, 0))
    ex_spec = pl.BlockSpec((LANES, n_inner), lambda c: (0, 0))
    return pl.pallas_call(body, grid=(T // CHUNK,), in_specs=[blk, _vec(LANES), _vec(LANES), ex_spec], out_specs=[out, out],
                          out_shape=[_sds((T, n_inner), F32)] * 2, name=name,
                          compiler_params=_params("parallel"))(dt_raw, dt_bias, a_log, _head_expand(n_inner))


def _pair_terms(cs_p):
    lane = lax.broadcasted_iota(jnp.int32, (CHUNK, CHUNK), 1)
    sub = lax.broadcasted_iota(jnp.int32, (CHUNK, CHUNK), 0)
    csT = cs_p.T
    Ls = []
    for k in range(2):
        col = jnp.sum(jnp.where(lane == k * HEADDIM, cs_p, 0.0), axis=1, keepdims=True)
        rowv = csT[k * HEADDIM:k * HEADDIM + 1, :]
        Ls.append(jnp.exp(jnp.where(sub >= lane, col - rowv, -jnp.inf)))
    return Ls, jnp.exp(csT[:, CHUNK - 1:CHUNK])


def _block_diag(xp):
    lane = lax.broadcasted_iota(jnp.int32, xp.shape, 1)
    return jnp.concatenate([jnp.where(lane < HEADDIM, xp, 0.0), jnp.where(lane >= HEADDIM, xp, 0.0)], axis=0)


SSD_GROUPS_PER_STEP = 8


def _ssd_specs(T, n_inner):
    nc, gs = T // CHUNK, SSD_GROUPS_PER_STEP
    bo, co = n_inner // (gs * NSTATE), (n_inner + NGROUPS * NSTATE) // (gs * NSTATE)
    assert NGROUPS % gs == 0 and n_inner % (gs * NSTATE) == 0 and (NGROUPS * NSTATE) % (gs * NSTATE) == 0
    g_blk = lambda f: pl.BlockSpec((CHUNK, gs * GROUP_W), lambda c, s: (f(c), s))
    b_blk = lambda f: pl.BlockSpec((CHUNK, gs * NSTATE), lambda c, s: (f(c), bo + s))
    c_blk = lambda f: pl.BlockSpec((CHUNK, gs * NSTATE), lambda c, s: (f(c), co + s))
    return nc, g_blk, b_blk, c_blk


def _ssd_fwd(xbc, dt_e, cs_e, d_e, name, comm=None):
    T = xbc.shape[0]
    n_inner = dt_e.shape[1]
    nc, g_blk, b_blk, c_blk = _ssd_specs(T, n_inner)
    ident = lambda c: c

    gs = SSD_GROUPS_PER_STEP

    def body(xs_ref, b_ref, c_ref, dt_ref, cs_ref, d_ref, y_ref, p_ref, st):
        c, s = pl.program_id(0), pl.program_id(1)

        @pl.when(c == 0)
        def _():
            for gi in range(gs):
                st[s * gs + gi] = jnp.zeros((GROUP_W, NSTATE), F32)

        for gi in range(gs):
            g = s * gs + gi
            gw, gn = slice(gi * GROUP_W, (gi + 1) * GROUP_W), slice(gi * NSTATE, (gi + 1) * NSTATE)
            P = st[g]
            p_ref[0, gi] = P
            xs, dt, cs = xs_ref[:, gw], dt_ref[:, gw], cs_ref[:, gw]
            Bf, Cf = b_ref[:, gn], c_ref[:, gn]
            Cb = Cf.astype(BF16)
            CBm = _dot(Cb, Bf.astype(BF16), NT)
            X = xs * dt
            decay = jnp.exp(cs[CHUNK - 1:CHUNK, :] - cs)
            y_off = _dot(Cb, P.astype(BF16), NT) * jnp.exp(cs)
            ys, ecl = [], []
            for pr in range(2):
                sl = slice(pr * LANES, (pr + 1) * LANES)
                Ls, e_last = _pair_terms(cs[:, sl])
                ecl.append(e_last)
                Mcat = jnp.concatenate([(CBm * L).astype(BF16) for L in Ls], axis=1)
                ys.append(_dot(Mcat, _block_diag(X[:, sl]).astype(BF16)))
            y_ref[:, gw] = jnp.concatenate(ys, axis=1) + y_off + xs * d_ref[:, gw]
            S = _dot3(X * decay, Bf, TN)
            st[g] = P * jnp.concatenate(ecl, axis=0) + S

    p_blk = pl.BlockSpec((1, gs, GROUP_W, NSTATE), lambda c, s: (c, s, 0, 0))
    outs, carried = _call(
        body, grid=(nc, NGROUPS // gs),
        in_specs=[g_blk(ident), b_blk(ident), c_blk(ident), g_blk(ident), g_blk(ident), pl.BlockSpec((1, gs * GROUP_W), lambda c, s: (0, s))],
        out_specs=[g_blk(ident), p_blk],
        out_shape=[_sds((T, n_inner), F32), _sds((nc, NGROUPS, GROUP_W, NSTATE), F32)],
        scratch=[pltpu.VMEM((NGROUPS, GROUP_W, NSTATE), F32)],
        args=[xbc, xbc, xbc, dt_e, cs_e, d_e], name=name, sem=("arbitrary", "arbitrary"), comm=comm)
    return outs if comm is None else (outs, carried)


def _ssd_bwd(xbc, dt_e, cs_e, d_e, states, dy, name, comm=None):
    T = xbc.shape[0]
    n_inner = dt_e.shape[1]
    nc, g_blk, b_blk, c_blk = _ssd_specs(T, n_inner)
    rev = lambda c: nc - 1 - c

    gs = SSD_GROUPS_PER_STEP

    def body(xs_ref, b_ref, c_ref, dt_ref, cs_ref, d_ref, p_ref, pn_ref, dy_ref,
             dxs_ref, db_ref, dc_ref, ddt_ref, dcs_ref, dd_ref, dst):
        cc, s = pl.program_id(0), pl.program_id(1)

        @pl.when(cc == 0)
        def _():
            for gi in range(gs):
                dst[s * gs + gi] = jnp.zeros((GROUP_W, NSTATE), F32)

        for gi in range(gs):
            one_group(s * gs + gi, gi, xs_ref, b_ref, c_ref, dt_ref, cs_ref, d_ref, p_ref, pn_ref, dy_ref,
                      dxs_ref, db_ref, dc_ref, ddt_ref, dcs_ref, dd_ref, dst)

    def one_group(g, gi, xs_ref, b_ref, c_ref, dt_ref, cs_ref, d_ref, p_ref, pn_ref, dy_ref,
                  dxs_ref, db_ref, dc_ref, ddt_ref, dcs_ref, dd_ref, dst):
        gw, gn = slice(gi * GROUP_W, (gi + 1) * GROUP_W), slice(gi * NSTATE, (gi + 1) * NSTATE)
        dS = dst[g]
        P, Pn = p_ref[0, gi], pn_ref[0, gi]
        xs, dt, cs, dY = xs_ref[:, gw], dt_ref[:, gw], cs_ref[:, gw], dy_ref[:, gw]
        Bf, Cf = b_ref[:, gn], c_ref[:, gn]
        Bb, Cb = Bf.astype(BF16), Cf.astype(BF16)
        X = xs * dt
        ecs = jnp.exp(cs)
        decay = jnp.exp(cs[CHUNK - 1:CHUNK, :] - cs)
        CBm = _dot3(Cf, Bf, NT)
        dYe = dY * ecs
        dP_off = _dot3(dYe, Cf, TN)
        dC = _dot(dYe.astype(BF16), P.astype(BF16))
        dcs = dYe * _dot3(Cf, P, NT)
        Xd = X * decay
        dB = _dot(Xd.astype(BF16), dS.astype(BF16))
        E = _dot3(Bf, dS, NT)
        dX = E * decay
        dcs = dcs - E * Xd
        R = _dot3(jnp.ones((8, NSTATE), F32), dS * Pn, NT)
        sub_g = lax.broadcasted_iota(jnp.int32, (CHUNK, GROUP_W), 0)
        dcs = dcs + jnp.where(sub_g == CHUNK - 1, R[0:1, :], 0.0)
        lane = lax.broadcasted_iota(jnp.int32, (CHUNK, CHUNK), 1)
        sub = lax.broadcasted_iota(jnp.int32, (CHUNK, CHUNK), 0)
        dCB = jnp.zeros((CHUNK, CHUNK), F32)
        dXs, dcss, ecl = [], [], []
        for pr in range(2):
            sl = slice(pr * LANES, (pr + 1) * LANES)
            Ls, e_last = _pair_terms(cs[:, sl])
            ecl.append(e_last)
            dYpb = dY[:, sl].astype(BF16)
            dMcat = _dot(dYpb, _block_diag(X[:, sl]).astype(BF16), NT)
            Mcat = jnp.concatenate([(CBm * L).astype(BF16) for L in Ls], axis=1)
            dXt = _dot(Mcat, dYpb, TN)
            dXs.append(jnp.where(lane < HEADDIM, dXt[:CHUNK], dXt[CHUNK:]))
            colacc = jnp.zeros((CHUNK, CHUNK), F32)
            rowacc = jnp.zeros((CHUNK, CHUNK), F32)
            for k in range(2):
                dG = dMcat[:, k * CHUNK:(k + 1) * CHUNK] * Ls[k]
                dCB = dCB + dG
                Q = dG * CBm
                colacc = colacc + jnp.where(lane == k * HEADDIM, jnp.sum(Q, axis=1, keepdims=True), 0.0)
                rowacc = rowacc + jnp.where(sub == k * HEADDIM, jnp.sum(Q, axis=0, keepdims=True), 0.0)
            dcss.append(colacc - rowacc.T)
        dX = dX + jnp.concatenate(dXs, axis=1)
        dcs = dcs + jnp.concatenate(dcss, axis=1)
        dCBb = dCB.astype(BF16)
        dc_ref[:, gn] = dC + _dot(dCBb, Bb)
        db_ref[:, gn] = dB + _dot(dCBb, Cb, TN)
        dxs_ref[:, gw] = dX * dt + dY * d_ref[:, gw]
        ddt_ref[:, gw] = dX * xs
        dcs_ref[:, gw] = dcs
        dd_ref[0, :, gw] = jnp.sum(dY * xs, axis=0, keepdims=True)
        dst[g] = dS * jnp.concatenate(ecl, axis=0) + dP_off

    p_blk = pl.BlockSpec((1, gs, GROUP_W, NSTATE), lambda c, s: (nc - 1 - c, s, 0, 0))
    pn_blk = pl.BlockSpec((1, gs, GROUP_W, NSTATE), lambda c, s: (jnp.minimum(nc - c, nc - 1), s, 0, 0))
    st_blk = pl.BlockSpec((CHUNK, gs * NSTATE), lambda c, s: (nc - 1 - c, s))
    outs, carried = _call(
        body, grid=(nc, NGROUPS // gs),
        in_specs=[g_blk(rev), b_blk(rev), c_blk(rev), g_blk(rev), g_blk(rev), pl.BlockSpec((1, gs * GROUP_W), lambda c, s: (0, s)),
                  p_blk, pn_blk, g_blk(rev)],
        out_specs=[g_blk(rev), st_blk, st_blk, g_blk(rev), g_blk(rev), pl.BlockSpec((1, 1, gs * GROUP_W), lambda c, s: (nc - 1 - c, 0, s))],
        out_shape=[_sds((T, n_inner), F32), _sds((T, NGROUPS * NSTATE), F32), _sds((T, NGROUPS * NSTATE), F32),
                   _sds((T, n_inner), F32), _sds((T, n_inner), F32), _sds((nc, 1, n_inner), F32)],
        scratch=[pltpu.VMEM((NGROUPS, GROUP_W, NSTATE), F32)],
        args=[xbc, xbc, xbc, dt_e, cs_e, d_e, states, states, dy], name=name, sem=("arbitrary", "arbitrary"), comm=comm)
    return outs if comm is None else (outs, carried)


def _ssd_post(ddt_e, dcs_e, dd_p, dt_raw, dt_bias, a_log, n_heads, name):
    T, n_inner = ddt_e.shape

    def body(ddt_ref, dcs_ref, dd_ref, r_ref, b_ref, al_ref, ex_ref, draw_ref, dbias_ref, dal_ref, ddsk_ref):
        @pl.when(pl.program_id(0) == 0)
        def _():
            dbias_ref[...] = jnp.zeros_like(dbias_ref)
            dal_ref[...] = jnp.zeros_like(dal_ref)
            ddsk_ref[...] = jnp.zeros_like(ddsk_ref)

        spread = [ddt_ref[...], dcs_ref[...], jnp.broadcast_to(dd_ref[0], (8, n_inner))]
        stacked = _dot(jnp.concatenate([p for v in spread for p in _split3(v)], axis=0), ex_ref[...], NT)
        sums, r0 = [], 0
        for v in spread:
            n = v.shape[0]
            sums.append(stacked[r0:r0 + n] + stacked[r0 + n:r0 + 2 * n] + stacked[r0 + 2 * n:r0 + 3 * n])
            r0 += 3 * n
        ddt_h, dcs_h, dd_h = sums
        raw = r_ref[...] + b_ref[...]
        dt = _softplus(raw)
        A = -jnp.exp(al_ref[...])
        i = lax.broadcasted_iota(jnp.int32, (CHUNK, CHUNK), 0)
        j = lax.broadcasted_iota(jnp.int32, (CHUNK, CHUNK), 1)
        upper = (j >= i).astype(BF16)
        da = sum(_dot(upper, p) for p in _split3(dcs_h))
        ddt = ddt_h + da * A
        lane = lax.broadcasted_iota(jnp.int32, (CHUNK, LANES), 1)
        draw = jnp.where(lane < n_heads, ddt * jax.nn.sigmoid(raw), 0.0)
        draw_ref[...] = draw.astype(BF16)
        dbias_ref[...] += jnp.sum(draw, axis=0, keepdims=True)
        dal_ref[...] += jnp.sum(da * dt, axis=0, keepdims=True) * A
        ddsk_ref[...] += dd_h[0:1, :]

    wide = pl.BlockSpec((CHUNK, n_inner), lambda c: (c, 0))
    blk = pl.BlockSpec((CHUNK, LANES), lambda c: (c, 0))
    return pl.pallas_call(
        body, grid=(T // CHUNK,),
        in_specs=[wide, wide, pl.BlockSpec((1, 1, n_inner), lambda c: (c, 0, 0)), blk, _vec(LANES), _vec(LANES),
                  pl.BlockSpec((LANES, n_inner), lambda c: (0, 0))],
        out_specs=[blk, _vec(LANES), _vec(LANES), _vec(LANES)],
        out_shape=[_sds((T, LANES), BF16)] + [_sds((1, LANES), F32)] * 3,
        name=name, compiler_params=_params("arbitrary"))(ddt_e, dcs_e, dd_p, dt_raw, dt_bias, a_log, _head_expand(n_inner))


def _row2(v):
    return v.reshape(1, -1).astype(F32)


def _pad_lanes(v):
    return jnp.pad(_row2(v), ((0, 0), (0, LANES - v.shape[-1])))


class _NoExchange:
    def __init__(self, W):
        self.W, self.grads = W, {}

    def weight(self, k):
        return self.W[k]

    def carry(self, name):
        return None

    def carried(self, name, outs):
        pass

    def grad(self, k, g):
        self.grads[k] = g

    def tok(self):
        return jnp.zeros((), F32)

    def point(self, name, value):
        pass


def _local_step(x, tgt, S, small):
    T, D = x.shape

    def mm(a, b, *, name, **kw):
        comm = S.carry(name)
        if comm is None:
            return _mm(a, b, name=name, **kw)
        res, outs = _mm(a, b, name=name, comm=comm, **kw)
        S.carried(name, outs)
        return res

    def carrying(fn, *args, name):
        comm = S.carry(name)
        if comm is None:
            return fn(*args, name)
        res, outs = fn(*args, name, comm=comm)
        S.carried(name, outs)
        return res

    n_inner = 2 * D
    n_heads = n_inner // HEADDIM
    norm_mix, norm_mlp, norm_final = _row2(small["norm_mix"]), _row2(small["norm_mlp"]), _row2(small["norm_final"])
    b_gate, ssm_b, ssm_norm_w = _row2(small["b_gate"]), _row2(small["ssm_conv_b"]), _row2(small["ssm_norm_w"])
    dt_bias, a_log = _pad_lanes(small["dt_bias"]), _pad_lanes(small["A_log"])
    d_e = jnp.repeat(small["D_skip"].astype(F32), HEADDIM).reshape(1, n_inner)

    hb = carrying(_rms_fwd, x, norm_mix, name="rms_mix")
    sc_w, ssm_w = S.weight("sc_conv_w"), S.weight("ssm_conv_w")
    p_xbc = mm(hb, S.weight("xbc"), mode="nn", name="proj_xbc")
    p_dt = mm(hb, S.weight("dt"), mode="nn", name="proj_dt")
    p_z = mm(hb, S.weight("z"), mode="nn", name="proj_z")
    p_sc = mm(hb, S.weight("sc"), mode="nn", name="proj_sc")
    p_gate = mm(hb, S.weight("gate"), mode="nn", name="proj_gate")
    xbc = carrying(_ssm_conv_fwd, p_xbc, ssm_w, ssm_b, name="ssm_conv_fwd")
    dt_e, cs_e = _ssd_prep(p_dt, dt_bias, a_log, n_inner, "ssd_prep")
    ya = _sc_fwd(p_sc, sc_w, "sc_fwd")
    y, states = carrying(_ssd_fwd, xbc, dt_e, cs_e, d_e, name="ssd_fwd")
    S.point("mixers_done", [y, ya, p_gate])
    yb = carrying(_gnorm_fwd, y, p_z, ssm_norm_w, name="gnorm_fwd")
    br_a = mm(ya, S.weight("bsc"), mode="nn", name="branch_sc")
    br_b = mm(yb, S.weight("bssm"), mode="nn", name="branch_ssm")
    merged = _merge_fwd(p_gate, b_gate, br_a, br_b, "merge_fwd")
    x1 = mm(merged, S.weight("out"), mode="nn", name="out_proj", extras=(x,), epi=_epi_add)
    h2 = _rms_fwd(x1, norm_mlp, "rms_mlp")
    r_act = mm(h2, S.weight("w1"), mode="nn", name="mlp_up", epi=_epi_relu2, out_dtypes=(BF16,))
    x2 = mm(r_act, S.weight("w2"), mode="nn", name="mlp_down", extras=(x1,), epi=_epi_add)
    dx2, dx2b, g_norm_final, loss_row = _final(x2, norm_final, tgt, "final")

    S.grad("w2", mm(r_act, dx2b, mode="tn", name="mlp_down_dw", out_dtypes=(BF16,)))
    da = mm(dx2b, S.weight("w2"), mode="nt", name="mlp_down_dx", extras=(r_act,), epi=_epi_relu2_bwd, out_dtypes=(BF16,))
    S.grad("w1", mm(h2, da, mode="tn", name="mlp_up_dw", out_dtypes=(BF16,)))
    dh2 = mm(da, S.weight("w1"), mode="nt", name="mlp_up_dx")
    dx1, dx1b, g_norm_mlp = _rms_bwd(x1, norm_mlp + S.tok(), dh2, dx2, "rms_mlp_bwd")
    S.grad("out", mm(merged, dx1b, mode="tn", name="out_proj_dw", out_dtypes=(BF16,)))
    dmerged = mm(dx1b, S.weight("out"), mode="nt", name="out_proj_dx")
    dbr_a, dbr_b, d_gate, g_b_gate = _merge_bwd(dmerged, p_gate, b_gate, br_a, br_b, "merge_bwd")
    S.grad("bssm", mm(yb, dbr_b, mode="tn", name="branch_ssm_dw", out_dtypes=(BF16,)))
    S.grad("bsc", mm(ya, dbr_a, mode="tn", name="branch_sc_dw", out_dtypes=(BF16,)))
    dyb = mm(dbr_b, S.weight("bssm"), mode="nt", name="branch_ssm_dx")
    dya = mm(dbr_a, S.weight("bsc"), mode="nt", name="branch_sc_dx")
    dy, d_z, g_ssm_norm_w = _gnorm_bwd(y, p_z, ssm_norm_w + S.tok(), dyb, "gnorm_bwd")
    dxs, dB, dC, ddt_e, dcs_e, dd_p = carrying(_ssd_bwd, xbc, dt_e, cs_e, d_e, states, dy, name="ssd_bwd")
    d_dt, g_dt_bias, g_a_log, g_d_skip = _ssd_post(ddt_e, dcs_e, dd_p, p_dt, dt_bias, a_log, n_heads, "ssd_post")
    d_xbc, g_ssm_w, g_ssm_b = carrying(_ssm_conv_bwd, p_xbc, ssm_w, ssm_b, dxs, dB, dC, name="ssm_conv_bwd")
    d_scB, d_scC, d_scX, g_sc_w = _sc_bwd(p_sc, sc_w, dya, "sc_bwd")
    d_sc = jnp.concatenate([d_scB, d_scC, d_scX], axis=1)
    pieces = [("sc", d_sc), ("z", d_z), ("xbc", d_xbc), ("dt", d_dt), ("gate", d_gate)]
    S.grad("win", {k: mm(hb, d, mode="tn", name="proj_dw_" + k, out_dtypes=(BF16,)) for k, d in pieces})
    pieces = [(k, d + S.tok().astype(d.dtype) if k == "dt" else d) for k, d in pieces]
    dh = mm([d for _, d in pieces], [S.weight(k) for k, _ in pieces], mode="nt", name="proj_dx")
    grad_x, _, g_norm_mix = _rms_bwd(x, norm_mix, dh, dx1, "rms_mix_bwd")

    g_small = dict(norm_mix=g_norm_mix, b_gate=g_b_gate, sc_conv_w=g_sc_w, ssm_conv_w=g_ssm_w, ssm_conv_b=g_ssm_b,
                   dt_bias=g_dt_bias, A_log=g_a_log, D_skip=g_d_skip, ssm_norm_w=g_ssm_norm_w, norm_mlp=g_norm_mlp,
                   norm_final=g_norm_final, loss=loss_row)
    return grad_x, g_small


class _Place:
    def __init__(self, k=0):
        x, y, c = lax.axis_index("x"), lax.axis_index("y"), lax.axis_index("c")
        self.x = 1 - x if k & 4 else x
        self.y = 1 - y if k & 2 else y
        self.c = 1 - c if k & 1 else c
        self.chip = 2 * self.x + self.y
        self.id = 2 * self.chip + self.c


ICI_PEERS = (2, 4, 6)
SIBLING = (1,)
ALL_PEERS = (1, 2, 3, 4, 5, 6, 7)


class _Comm:
    def __init__(self, arrs, out_shape, ks, src, dst, own=None, aliases=None):
        self.arrs, self.out_shape, self.ks = list(arrs), list(out_shape), tuple(ks)
        self.n = len(self.arrs)
        self.src, self.dst, self.own = src, dst, own
        self.aliases = aliases or {}
        dma = pltpu.SemaphoreType.DMA
        self.scratch = [dma((self.n, len(self.ks))), dma((self.n, len(self.ks))), dma((self.n,))]

    def _copies(self, ins, outs, sems, with_recvs):
        send_sems, recv_sems, local_sems = sems
        me = _Place()
        owns, sends, recvs = [], [], []
        for a in range(self.n):
            if self.own is not None:
                s, d = self.own(a, ins[a], outs[a], me)
                owns.append(pltpu.make_async_copy(s, d, local_sems.at[a]))
            for i, k in enumerate(self.ks):
                peer = _Place(k)
                for sender, lst in ((me, sends), (peer, recvs)) if with_recvs else ((me, sends),):
                    lst.append(pltpu.make_async_remote_copy(
                        src_ref=self.src(a, ins[a], me, peer), dst_ref=self.dst(a, outs[a], sender),
                        send_sem=send_sems.at[a, i], recv_sem=recv_sems.at[a, i],
                        device_id=(peer.x, peer.y, peer.c), device_id_type=MESH))
        return owns, sends, recvs

    def start(self, ins, outs, sems):
        owns, sends, _ = self._copies(ins, outs, sems, False)
        for cp in owns + sends:
            cp.start()

    def finish(self, ins, outs, sems):
        owns, sends, recvs = self._copies(ins, outs, sems, True)
        for cp in recvs:
            cp.wait_recv()
        for cp in sends:
            cp.wait_send()
        for cp in owns:
            cp.wait()


class _GatherBoth:
    def __init__(self, shards):
        self.arrs, self.n, self.aliases = list(shards), len(shards), {}
        self.out_shape = [_sds((4, 2) + s.shape, s.dtype) for s in shards]
        dma = pltpu.SemaphoreType.DMA
        self.scratch = [dma((self.n, 7)), dma((self.n, 7)), dma((self.n,))]

    def _copy(self, a, j, src, slot, to, outs, sems):
        return pltpu.make_async_remote_copy(src_ref=src, dst_ref=outs[a].at[slot.chip, slot.c], send_sem=sems[0].at[a, j],
                                            recv_sem=sems[1].at[a, j], device_id=(to.x, to.y, to.c), device_id_type=MESH)

    def start(self, ins, outs, sems):
        me, sib = _Place(), _Place(1)
        for a in range(self.n):
            pltpu.make_async_copy(ins[a], outs[a].at[me.chip, me.c], sems[2].at[a]).start()
            self._copy(a, 0, ins[a], me, sib, outs, sems).start()
            for i, k in enumerate(ICI_PEERS):
                self._copy(a, 1 + i, ins[a], me, _Place(k), outs, sems).start()

    def finish(self, ins, outs, sems):
        me, sib = _Place(), _Place(1)
        passed = []
        for i, k in enumerate(ICI_PEERS):
            peer = _Place(k)
            for a in range(self.n):
                self._copy(a, 1 + i, ins[a], peer, peer, outs, sems).wait_recv()
                cp = self._copy(a, 4 + i, outs[a].at[peer.chip, peer.c], peer, sib, outs, sems)
                cp.start()
                passed.append(cp)
        for a in range(self.n):
            self._copy(a, 0, ins[a], sib, sib, outs, sems).wait_recv()
            for i, k in enumerate(ICI_PEERS):
                far = _Place(k | 1)
                self._copy(a, 4 + i, outs[a].at[far.chip, far.c], far, sib, outs, sems).wait_recv()
        for a in range(self.n):
            self._copy(a, 0, ins[a], me, sib, outs, sems).wait_send()
            for i, k in enumerate(ICI_PEERS):
                self._copy(a, 1 + i, ins[a], me, _Place(k), outs, sems).wait_send()
            pltpu.make_async_copy(ins[a], outs[a].at[me.chip, me.c], sems[2].at[a]).wait()
        for cp in passed:
            cp.wait_send()


def _run_comm(comm, name, after=()):
    n, n_after = comm.n, len(after)

    def body(*refs):
        ins, outs, sems = refs[:n], refs[n + n_after:2 * n + n_after], refs[2 * n + n_after:]
        comm.start(ins, outs, sems)
        comm.finish(ins, outs, sems)

    return list(pl.pallas_call(body, in_specs=[ANY] * (n + n_after), out_specs=[ANY] * n, out_shape=comm.out_shape,
                               scratch_shapes=comm.scratch, input_output_aliases=dict(comm.aliases), name=name)(*comm.arrs, *after))


def _gather_sibling(bufs):
    return _Comm(bufs, [_sds(b.shape, b.dtype) for b in bufs], SIBLING,
                 src=lambda a, i, me, p: i.at[:, me.c], dst=lambda a, o, s: o.at[:, s.c], aliases={a: a for a in range(len(bufs))})


def _scatter_sibling(parts):
    return _Comm(parts, [_sds((4,) + p.shape[2:], p.dtype) for p in parts], SIBLING,
                 src=lambda a, i, me, p: i.at[:, p.c], dst=lambda a, o, s: o)


HBM_SPEC = pl.BlockSpec(memory_space=pltpu.HBM)
SEM_SPEC = pl.BlockSpec(memory_space=pltpu.SEMAPHORE)
DATAFLOW = pltpu.SideEffectType.DATAFLOW_SIDE_EFFECTING


def _tiles_2d(R, C, max_rows=256):
    if R % max_rows == 0:
        return max_rows, C, R // max_rows, lambda i: (i, 0)
    if R <= 2 * max_rows or C % 256:
        return R, C, 1, lambda i: (0, 0)
    return R, 256, C // 256, lambda i: (0, i)


def _ici_copy(gather, a, srcs, lands, send_sems, recv_sems, i, me, peer, sender):
    src = lands[a].at[me.chip, me.c] if gather else srcs[a].at[peer.chip]
    dst = lands[a].at[sender.chip, sender.c] if gather else lands[a].at[sender.chip]
    j = a * len(ICI_PEERS) + i
    return pltpu.make_async_remote_copy(src_ref=src, dst_ref=dst, send_sem=send_sems.at[j], recv_sem=recv_sems.at[j],
                                        device_id=(peer.x, peer.y, peer.c), device_id_type=MESH)


def _ici_start(srcs, lands, gather, name):
    n, n_s = len(lands), len(srcs)
    bufs = list(srcs) + list(lands)

    def body(*refs):
        src_refs, land_refs = refs[:n_s], refs[n_s:n_s + n]
        send_sems, recv_sems = refs[n_s + n], refs[n_s + n + 1]
        token = refs[-1]
        me = _Place()
        for a in range(n):
            for i, k in enumerate(ICI_PEERS):
                _ici_copy(gather, a, src_refs, land_refs, send_sems, recv_sems, i, me, _Place(k), me).start()
        token[...] = jnp.zeros_like(token)

    dma = pltpu.SemaphoreType.DMA((n * len(ICI_PEERS),))
    outs = pl.pallas_call(
        body, name=name, out_shape=(dma, dma, *[pltpu.HBM(v.shape, v.dtype) for v in bufs], _sds((8, LANES), F32)),
        in_specs=(HBM_SPEC,) * len(bufs),
        out_specs=(SEM_SPEC, SEM_SPEC) + (HBM_SPEC,) * len(bufs) + (pl.BlockSpec(memory_space=pltpu.VMEM),),
        input_output_aliases={j: 2 + j for j in range(len(bufs))}, compiler_params=pltpu.CompilerParams(has_side_effects=DATAFLOW),
    )(*[pltpu.with_memory_space_constraint(v, pltpu.HBM) for v in bufs])
    return outs[0], outs[1], list(outs[2:2 + n_s]), list(outs[2 + n_s:2 + n_s + n]), outs[-1]


def _ici_wait(flight, after, gather, name):
    send_sems, recv_sems, srcs, lands, _ = flight
    n, n_s = len(lands), len(srcs)
    bufs = srcs + lands

    def body(*refs):
        src_refs, land_refs = refs[:n_s], refs[n_s:n_s + n]
        s_sems, r_sems = refs[n_s + n], refs[n_s + n + 1]
        me = _Place()
        for a in range(n):
            for i, k in enumerate(ICI_PEERS):
                peer = _Place(k)
                cp = _ici_copy(gather, a, src_refs, land_refs, s_sems, r_sems, i, me, peer, peer)
                cp.wait_send()
                cp.wait_recv()

    outs = pl.pallas_call(
        body, name=name, out_shape=tuple(pltpu.HBM(v.shape, v.dtype) for v in bufs),
        in_specs=(HBM_SPEC,) * len(bufs) + (SEM_SPEC, SEM_SPEC) + (ANY,) * len(after), out_specs=(HBM_SPEC,) * len(bufs),
        input_output_aliases={j: j for j in range(len(bufs))}, compiler_params=pltpu.CompilerParams(has_side_effects=DATAFLOW),
    )(*bufs, send_sems, recv_sems, *after)
    return list(outs[n_s:])


def _own_shards(shards, after, name):
    n = len(shards)
    vmem = pl.BlockSpec(memory_space=pltpu.VMEM)

    def body(*refs):
        ins, outs, cast, sems = refs[:n], refs[n + 1:2 * n + 1], refs[2 * n + 1:3 * n + 1], refs[3 * n + 1]
        me = _Place()
        copies = []
        for a in range(n):
            cast[a][...] = ins[a][...].astype(BF16)
            copies.append(pltpu.make_async_copy(cast[a], outs[a].at[me.chip, me.c], sems.at[a]))
            copies[-1].start()
        for cp in copies:
            cp.wait()

    return list(pl.pallas_call(
        body, in_specs=[vmem] * n + [ANY], out_specs=[ANY] * n, out_shape=[_sds((4, 2) + s.shape, BF16) for s in shards],
        scratch_shapes=[pltpu.VMEM(s.shape, BF16) for s in shards] + [pltpu.SemaphoreType.DMA((n,))], name=name)(*shards, after))


def _col_pieces(widths):
    out, c = [], 0
    for k, w in widths:
        out.append((k, c, w))
        c += w
    return out


def _split_range(c0, n, bounds):
    parts, c = [], c0
    while c < c0 + n:
        r = max(i for i in range(len(bounds) - 1) if bounds[i] <= c)
        w = min(c0 + n, bounds[r + 1]) - c
        parts.append((r, c - bounds[r], w))
        c += w
    return parts


def _win_unpack(g, widths, name):
    n, R, C = g.shape
    tr = min(256, R)
    pieces = _col_pieces(widths)
    padded = [-(-w // LANES) * LANES for _, _, w in pieces]
    shard_bounds = [s * C for s in range(n + 1)]

    def body(g_ref, *o_refs):
        for (k, c0, w), o_ref in zip(pieces, o_refs):
            for t in range(0, o_ref.shape[1], LANES):
                valid = max(0, min(LANES, w - t))
                cols = [g_ref[s, :, o:o + ww] for s, o, ww in _split_range(c0 + t, valid, shard_bounds)] if valid else []
                if valid < LANES:
                    cols.append(jnp.zeros((tr, LANES - valid), g_ref.dtype))
                o_ref[:, t:t + LANES] = cols[0] if len(cols) == 1 else jnp.concatenate(cols, axis=1)

    return pl.pallas_call(
        body, grid=(R // tr,), in_specs=[pl.BlockSpec((n, tr, C), lambda i: (0, i, 0))],
        out_specs=[pl.BlockSpec((tr, p), lambda i: (i, 0)) for p in padded],
        out_shape=[_sds((R, p), g.dtype) for p in padded], name=name, compiler_params=_params("parallel"))(g)


def _win_pack(grads, widths, n, mine, name, comm=None):
    R = grads[0].shape[0]
    tr = min(256, R)
    pieces = _col_pieces(widths)
    total = pieces[-1][1] + pieces[-1][2]
    C = total // n
    bounds = [c0 for _, c0, _ in pieces] + [total]
    n_g = len(grads)

    def body(*refs):
        g_refs, o_ref = refs[:n_g], refs[n_g]

        def tile_t(c0):
            cols = [g_refs[r][:, o:o + ww] for r, o, ww in _split_range(c0, LANES, bounds)]
            tile = cols[0] if len(cols) == 1 else jnp.concatenate(cols, axis=1)
            return tile.astype(F32).T

        def slots_of(core):
            for q in range(n // 2):
                s = 2 * q + core
                full = C // LANES * LANES
                for t in range(0, full, LANES):
                    o_ref[q, t:t + LANES, :] = tile_t(s * C + t).astype(o_ref.dtype)
                if full < C:
                    o_ref[q, full:C, :] = tile_t(s * C + C - LANES)[LANES - (C - full):, :].astype(o_ref.dtype)

        my_core = lax.axis_index("c")
        want = my_core if mine else 1 - my_core
        for core in range(2):
            @pl.when(want == core)
            def _(core=core):
                slots_of(core)

    outs, carried = _call(
        body, grid=(R // tr,), in_specs=[pl.BlockSpec((tr, gr.shape[1]), lambda i: (i, 0)) for gr in grads],
        out_specs=[pl.BlockSpec((n // 2, C, tr), lambda i: (0, 0, i))], out_shape=[_sds((n // 2, C, R), grads[0].dtype)],
        args=list(grads), name=name, sem=("parallel",), comm=comm)
    return outs[0] if comm is None else (outs[0], carried)


def _gather_all(arrs):
    return _Comm(arrs, [_sds((N_DEV,) + a.shape, a.dtype) for a in arrs], ALL_PEERS,
                 src=lambda a, i, me, p: i, dst=lambda a, o, s: o.at[s.id], own=lambda a, i, o, me: (i, o.at[me.id]))


def _add_halves(parts, got, name):
    by_core = parts.ndim == 4
    n, (R, C) = parts.shape[0], parts.shape[-2:]
    br, bc, nb, at = _tiles_2d(R, C, max_rows=1024)
    place = jnp.stack([lax.axis_index("c"), 2 * lax.axis_index("x") + lax.axis_index("y")]).astype(jnp.int32)

    def body(q_ref, p_ref, g_ref, o_ref, land_ref):
        mine = p_ref[0, 0] if by_core else p_ref[0]
        s = (mine.astype(F32) + g_ref[0].astype(F32)).astype(o_ref.dtype)
        o_ref[0] = s

        @pl.when(pl.program_id(1) == q_ref[1])
        def _():
            land_ref[0] = s

    spec = pltpu.PrefetchScalarGridSpec(
        num_scalar_prefetch=1, grid=(nb, n),
        in_specs=[pl.BlockSpec((1, 1, br, bc), lambda i, q, q_ref: (q, q_ref[0]) + at(i)) if by_core else
                  pl.BlockSpec((1, br, bc), lambda i, q, q_ref: (q,) + at(i)),
                  pl.BlockSpec((1, br, bc), lambda i, q, q_ref: (q,) + at(i))],
        out_specs=[pl.BlockSpec((1, br, bc), lambda i, q, q_ref: (q,) + at(i)), pl.BlockSpec((1, br, bc), lambda i, q, q_ref: (q_ref[1],) + at(i))])
    return pl.pallas_call(body, grid_spec=spec, out_shape=[_sds((n, R, C), parts.dtype)] * 2, name=name,
                          compiler_params=_params("parallel", "arbitrary"))(place, parts, got)


def _adam(w, m, v, gparts, name, comm=None):
    R, C = w.shape
    n = gparts.shape[0]
    br, bc, nb, at = _tiles_2d(R, C, max_rows=512)
    c1 = 1.0 / (1.0 - ADAM_B1 ** ADAM_STEP)
    c2 = 1.0 / (1.0 - ADAM_B2 ** ADAM_STEP)

    def body(w_ref, m_ref, v_ref, g_ref, go_ref, d_ref, mo_ref, vo_ref):
        g = g_ref[0].astype(F32)
        for s in range(1, n):
            g = g + g_ref[s].astype(F32)
        mn = ADAM_B1 * m_ref[...] + (1.0 - ADAM_B1) * g
        vn = ADAM_B2 * v_ref[...] + (1.0 - ADAM_B2) * (g * g)
        go_ref[...] = g
        mo_ref[...] = mn
        vo_ref[...] = vn
        d_ref[...] = -ADAM_LR * ((mn * c1) / (jnp.sqrt(vn * c2) + ADAM_EPS) + ADAM_WD * w_ref[...])

    blk = pl.BlockSpec((br, bc), at)
    outs, carried = _call(
        body, grid=(nb,), in_specs=[blk, blk, blk, pl.BlockSpec((n, br, bc), lambda i: (0,) + at(i))],
        out_specs=[blk] * 4, out_shape=[_sds((R, C), F32)] * 4, args=[w, m, v, gparts], name=name, sem=("parallel",), comm=comm)
    return outs if comm is None else (outs, carried)


_SMALL_ORDER = ("norm_mix", "b_gate", "sc_conv_w", "ssm_conv_w", "ssm_conv_b", "dt_bias", "A_log", "D_skip", "ssm_norm_w",
                "norm_mlp", "norm_final", "loss")
_REPLICATED = ("norm_mix", "b_gate", "ssm_conv_b", "dt_bias", "A_log", "D_skip", "ssm_norm_w", "norm_mlp", "norm_final")


def _cols_to_slots(g, n):
    R = g.shape[0]
    return jnp.transpose(g.reshape(R, n, g.shape[1] // n), (1, 0, 2))


def _slots_to_cols(g):
    n, R, C = g.shape
    return jnp.transpose(g, (1, 0, 2)).reshape(R, n * C)


def kernel(x, norm_mix, w_in, b_gate, sc_conv_w, ssm_conv_w, ssm_conv_b, dt_bias, A_log, D_skip, ssm_norm_w, w_branch_sc, w_branch_ssm, w_out, norm_mlp, w_mlp1, w_mlp2, norm_final, loss_target, m_norm_mix, m_w_in, m_b_gate, m_sc_conv_w, m_ssm_conv_w, m_ssm_conv_b, m_dt_bias, m_A_log, m_D_skip, m_ssm_norm_w, m_w_branch_sc, m_w_branch_ssm, m_w_out, m_norm_mlp, m_w_mlp1, m_w_mlp2, m_norm_final, v_norm_mix, v_w_in, v_b_gate, v_sc_conv_w, v_ssm_conv_w, v_ssm_conv_b, v_dt_bias, v_A_log, v_D_skip, v_ssm_norm_w, v_w_branch_sc, v_w_branch_ssm, v_w_out, v_norm_mlp, v_w_mlp1, v_w_mlp2, v_norm_final):
    T, D = x.shape[1], x.shape[2]
    n_inner = 2 * D
    n_heads = n_inner // HEADDIM
    n_xbc = n_inner + 2 * NGROUPS * NSTATE
    me = 4 * lax.axis_index("x") + 2 * lax.axis_index("y") + lax.axis_index("c")

    in_cols = [("sc", 3 * D), ("z", n_inner), ("xbc", n_xbc), ("dt", n_heads), ("gate", 2 * D)]
    by_owner = lambda b: b.reshape((N_DEV,) + b.shape[2:])
    to_owner = lambda g: g.reshape((4, 2) + g.shape[1:])
    rows_of = lambda g: to_owner(g.reshape((N_DEV, g.shape[0] // N_DEV) + g.shape[1:]))
    cols_of = lambda g: to_owner(_cols_to_slots(g, N_DEV))

    class Schedule(_NoExchange):
        late = ("bssm", "bsc", "out", "w1", "w2")
        gather_sib = dict(gnorm_fwd=("bsc", "bssm", "out"), branch_ssm=("w1", "w2"))
        scatter_sib = dict(mlp_up_dx=("w2", "w1"), branch_ssm_dx=("out", "bssm", "bsc"))
        shards = dict(bsc=w_branch_sc, bssm=w_branch_ssm, out=w_out, w1=w_mlp1, w2=w_mlp2)

        def __init__(self):
            self.W, self.staged, self.grads, self.summed, self.scatters = {}, {}, {}, {}, []
            self.token = jnp.zeros((), F32)

        def first_weights(self, bufs):
            self.W.update(zip([k for k, _ in in_cols], _win_unpack(by_owner(bufs[0]), in_cols, "win_unpack")))
            self.W.update(sc_conv_w=_slots_to_cols(by_owner(bufs[1])), ssm_conv_w=_slots_to_cols(by_owner(bufs[2])))
            lands = _own_shards([self.shards[k] for k in self.late], bufs[1], "own_shards")
            self.gather_flight = _ici_start([], lands, True, "gather_late_start")
            self.token = self.gather_flight[4][0, 0]
            self.W["dt"] = self.W["dt"] + self.token.astype(BF16)

        def tok(self):
            return self.token

        def point(self, name, values):
            if name == "mixers_done":
                lands = _ici_wait(self.gather_flight, values, True, "gather_late_wait")
                self.staged.update(zip(self.late, lands))

        def carry(self, name):
            if name == "rms_mix":
                return _GatherBoth([w_in.astype(BF16), sc_conv_w, ssm_conv_w])
            if name in self.gather_sib:
                return _gather_sibling([self.staged.pop(k) for k in self.gather_sib[name]])
            if name in self.scatter_sib:
                return _scatter_sibling([self.grads[k] for k in self.scatter_sib[name]])
            return None

        def start_scatter(self, keys, halves_and_lands):
            halves, lands = [h for h, _ in halves_and_lands], [l for _, l in halves_and_lands]
            flight = _ici_start(halves, lands, False, "scatter_%s_start" % keys[0])
            self.scatters.append((keys, flight))
            self.token = flight[4][0, 0]

        def carried(self, name, outs):
            if name == "rms_mix":
                self.first_weights(outs)
            elif name in self.gather_sib:
                for k, b in zip(self.gather_sib[name], outs):
                    full = by_owner(b)
                    self.W[k] = _slots_to_cols(full) if k == "w1" else full.reshape(-1, D)
            else:
                keys = self.scatter_sib[name]
                self.start_scatter(keys, [_add_halves(self.grads[k], b, "add_halves_" + k) for k, b in zip(keys, outs)])

        def grad(self, k, g):
            if k == "win":
                pieces = [g[k] for k, _ in in_cols]
                theirs = _win_pack(pieces, in_cols, N_DEV, False, "win_pack_theirs")
                to_sibling = _Comm([theirs], [_sds(theirs.shape, theirs.dtype)], SIBLING, src=lambda a, i, me, p: i, dst=lambda a, o, s: o)
                mine, (got,) = _win_pack(pieces, in_cols, N_DEV, True, "win_pack_mine", comm=to_sibling)
                self.start_scatter(("win",), [_add_halves(mine, got, "add_halves_win")])
            else:
                self.grads[k] = cols_of(g) if k == "w1" else rows_of(g)

        def finish_scatter(self, after):
            keys, flight = self.scatters.pop(0)
            return dict(zip(keys, _ici_wait(flight, after, False, "scatter_%s_wait" % keys[0])))

    S = Schedule()
    small = dict(norm_mix=norm_mix, b_gate=b_gate, ssm_conv_b=ssm_conv_b, dt_bias=dt_bias, A_log=A_log, D_skip=D_skip,
                 ssm_norm_w=ssm_norm_w, norm_mlp=norm_mlp, norm_final=norm_final)
    grad_x, g_small = _local_step(x.reshape(T, D), loss_target.reshape(T, D), S, small)

    small_flat = jnp.concatenate([g_small[k].reshape(-1) for k in _SMALL_ORDER])
    n_small = small_flat.shape[0]
    rows = -(-n_small // (8 * LANES)) * 8
    small_pack = jnp.pad(small_flat, (0, rows * LANES - n_small)).reshape(rows, LANES)

    res = {}
    big = [("w_in", "win", w_in, m_w_in, v_w_in), ("w_branch_sc", "bsc", w_branch_sc, m_w_branch_sc, v_w_branch_sc),
           ("w_branch_ssm", "bssm", w_branch_ssm, m_w_branch_ssm, v_w_branch_ssm), ("w_out", "out", w_out, m_w_out, v_w_out),
           ("w_mlp1", "w1", w_mlp1, m_w_mlp1, v_w_mlp1), ("w_mlp2", "w2", w_mlp2, m_w_mlp2, v_w_mlp2)]
    by_grad = {gk: (k, w, m, v) for k, gk, w, m, v in big}
    after = [grad_x]
    while S.scatters:
        for gk, parts in S.finish_scatter(after).items():
            k, w, m, v = by_grad[gk]
            if gk == "win":
                res_t, (small_parts,) = _adam(w.T, m.T, v.T, parts, "adam_" + k, comm=_gather_all([small_pack]))
                res[k] = [r.T for r in res_t]
            else:
                res[k] = _adam(w, m, v, parts, "adam_" + k)
            after = after + [res[k][1]]

    sizes = {k: g_small[k].size for k in _SMALL_ORDER}
    offs, o = {}, 0
    for k in _SMALL_ORDER:
        offs[k] = o
        o += sizes[k]
    rep_w = dict(norm_mix=norm_mix, b_gate=b_gate, ssm_conv_b=ssm_conv_b, dt_bias=dt_bias, A_log=A_log, D_skip=D_skip,
                 ssm_norm_w=ssm_norm_w, norm_mlp=norm_mlp, norm_final=norm_final)
    rep_m = dict(norm_mix=m_norm_mix, b_gate=m_b_gate, ssm_conv_b=m_ssm_conv_b, dt_bias=m_dt_bias, A_log=m_A_log, D_skip=m_D_skip,
                 ssm_norm_w=m_ssm_norm_w, norm_mlp=m_norm_mlp, norm_final=m_norm_final)
    rep_v = dict(norm_mix=v_norm_mix, b_gate=v_b_gate, ssm_conv_b=v_ssm_conv_b, dt_bias=v_dt_bias, A_log=v_A_log, D_skip=v_D_skip,
                 ssm_norm_w=v_ssm_norm_w, norm_mlp=v_norm_mlp, norm_final=v_norm_final)

    def pack(d):
        segs = [jnp.pad(d[k].astype(F32).reshape(-1), (0, sizes[k] - d[k].size)) if k in d else jnp.zeros((sizes[k],), F32)
                for k in _SMALL_ORDER]
        return jnp.pad(jnp.concatenate(segs), (0, rows * LANES - n_small)).reshape(rows, LANES)

    sm = _adam(pack(rep_w), pack(rep_m), pack(rep_v), small_parts, "adam_small")
    sm = [s.reshape(-1) for s in sm]
    for k in _REPLICATED:
        n_k = rep_w[k].shape[0]
        res[k] = tuple(s[offs[k]:offs[k] + n_k] for s in sm)
    loss = sm[0][offs["loss"]]
    for k, w, m, v, K, full in (("sc_conv_w", sc_conv_w, m_sc_conv_w, v_sc_conv_w, SC_K, D),
                                ("ssm_conv_w", ssm_conv_w, m_ssm_conv_w, v_ssm_conv_w, SSM_K, n_xbc)):
        g_full = sm[0][offs[k]:offs[k] + K * full].reshape(K, full)
        cw = full // N_DEV
        g_mine = lax.dynamic_slice_in_dim(g_full, me * cw, cw, axis=1)
        res[k] = _adam(w, m, v, g_mine[None], "adam_" + k)

    order = ("norm_mix", "w_in", "b_gate", "sc_conv_w", "ssm_conv_w", "ssm_conv_b", "dt_bias", "A_log", "D_skip", "ssm_norm_w",
             "w_branch_sc", "w_branch_ssm", "w_out", "norm_mlp", "w_mlp1", "w_mlp2", "norm_final")
    outs = [loss, grad_x.reshape(1, T, D)]
    for j in range(4):
        outs += [res[k][j] for k in order]
    return tuple(outs)
```

```python
import jax
import jax.numpy as jnp
from jax import lax
from jax.experimental import pallas as pl
from jax.experimental.pallas import tpu as pltpu

F32 = jnp.float32
BF16 = jnp.bfloat16

EPS = 1e-6
N_DEV = 8
HEADDIM = 64
NSTATE = 128
CHUNK = 128
NGROUPS = 8
GROUP_W = 256
SC_K = 3
SSM_K = 4
LANES = 128

ADAM_LR = 0.001
ADAM_B1 = 0.9
ADAM_B2 = 0.999
ADAM_EPS = 1e-08
ADAM_WD = 0.01
ADAM_STEP = 10

NN = (((1,), (0,)), ((), ()))
NT = (((1,), (1,)), ((), ()))
TN = (((0,), (0,)), ((), ()))
_DIMS = {"nn": NN, "nt": NT, "tn": TN}

ANY = pl.BlockSpec(memory_space=pl.ANY)
MESH = pl.DeviceIdType.MESH


def _sds(shape, dtype):
    return jax.ShapeDtypeStruct(tuple(shape), dtype)


def _dot(a, b, dims=NN):
    return lax.dot_general(a, b, dims, preferred_element_type=F32)


def _dot3(a, b, dims=NN):
    return lax.dot_general(a, b, dims, preferred_element_type=F32, precision=lax.Precision.HIGH)


def _params(*sem):
    return pltpu.CompilerParams(dimension_semantics=tuple(sem))


def _call(body, *, grid, in_specs, out_specs, out_shape, args, name, sem, scratch=(), comm=None):
    if comm is None:
        outs = pl.pallas_call(body, grid=grid, in_specs=list(in_specs), out_specs=list(out_specs), out_shape=list(out_shape),
                              scratch_shapes=list(scratch), name=name, compiler_params=_params(*sem))(*args)
        return list(outs), None
    n, n_in, n_out, n_scr = comm.n, len(in_specs), len(out_shape), len(scratch)

    def wrapped(*refs):
        ins, c_in = refs[:n_in], refs[n_in:n_in + n]
        outs, c_out = refs[n_in + n:n_in + n + n_out], refs[n_in + n + n_out:n_in + 2 * n + n_out]
        rest = refs[n_in + 2 * n + n_out:]
        scr, sems = rest[:n_scr], rest[n_scr:]
        first, last = None, None
        for d, g in enumerate(grid):
            f, l = pl.program_id(d) == 0, pl.program_id(d) == g - 1
            first, last = (f, l) if first is None else (first & f, last & l)

        @pl.when(first)
        def _():
            comm.start(c_in, c_out, sems)

        body(*ins, *outs, *scr)

        @pl.when(last)
        def _():
            comm.finish(c_in, c_out, sems)

    outs = pl.pallas_call(
        wrapped, grid=grid, in_specs=list(in_specs) + [ANY] * n, out_specs=list(out_specs) + [ANY] * n,
        out_shape=list(out_shape) + comm.out_shape, scratch_shapes=list(scratch) + comm.scratch,
        input_output_aliases={n_in + i: n_out + o for i, o in comm.aliases.items()},
        name=name, compiler_params=_params(*["arbitrary"] * len(grid)))(*args, *comm.arrs)
    return list(outs[:n_out]), list(outs[n_out:])


MM_VMEM_BUDGET = 44 * 2 ** 20


def _mm_tiles(M, N, k_bytes, mn_bytes):
    best = None
    for tm in (2048, 1024, 512, 256, 128):
        for tn in (1024, 512, 256, 128):
            if M % tm or N % tn:
                continue
            need = 2 * ((tm + tn) * k_bytes + tm * tn * mn_bytes) + 4 * tm * tn * 4
            if need <= MM_VMEM_BUDGET and (best is None or (tm * tn, tm) > (best[0] * best[1], best[0])):
                best = (tm, tn)
    assert best is not None, (M, N, k_bytes, mn_bytes)
    return best


def _mm(a, b, *, mode, name, extras=(), epi=None, out_dtypes=(F32,), comm=None):
    a_list = list(a) if isinstance(a, (list, tuple)) else [a]
    b_list = list(b) if isinstance(b, (list, tuple)) else [b]
    if mode == "nn":
        M, N = a_list[0].shape[0], b_list[0].shape[1]
    elif mode == "nt":
        M, N = a_list[0].shape[0], b_list[0].shape[0]
    else:
        M, N = a_list[0].shape[1], b_list[0].shape[1]
    k_bytes = sum((av.shape[0] if mode == "tn" else av.shape[1]) * av.dtype.itemsize for av in a_list)
    mn_bytes = sum(e.dtype.itemsize for e in extras) + sum(jnp.dtype(d).itemsize for d in out_dtypes)
    tm, tn = _mm_tiles(min(M, 2048), min(N, 1024), k_bytes, mn_bytes) if M % 128 == 0 and N % 128 == 0 else (M, N)
    assert M % tm == 0 and N % tn == 0
    a_specs, b_specs = [], []
    for av, bv in zip(a_list, b_list):
        K = av.shape[0] if mode == "tn" else av.shape[1]
        a_specs.append(pl.BlockSpec((K, tm), lambda i, j: (0, i)) if mode == "tn" else pl.BlockSpec((tm, K), lambda i, j: (i, 0)))
        b_specs.append(pl.BlockSpec((tn, K), lambda i, j: (j, 0)) if mode == "nt" else pl.BlockSpec((K, tn), lambda i, j: (0, j)))
    mn_spec = pl.BlockSpec((tm, tn), lambda i, j: (i, j))
    n_p, n_ex = len(a_list), len(extras)
    dims = _DIMS[mode]

    def body(*refs):
        acc = _dot(refs[0][...], refs[n_p][...], dims)
        for p in range(1, n_p):
            acc = acc + _dot(refs[p][...], refs[n_p + p][...], dims)
        rest = refs[2 * n_p:]
        res = (acc,) if epi is None else epi(acc, *[r[...] for r in rest[:n_ex]])
        for o_ref, r in zip(rest[n_ex:], res):
            o_ref[...] = r.astype(o_ref.dtype)

    outs, carried = _call(
        body, grid=(M // tm, N // tn), in_specs=a_specs + b_specs + [mn_spec] * n_ex,
        out_specs=[mn_spec] * len(out_dtypes), out_shape=[_sds((M, N), d) for d in out_dtypes],
        args=a_list + b_list + list(extras), name=name, sem=("parallel", "parallel"), comm=comm)
    res = outs[0] if len(outs) == 1 else outs
    return res if comm is None else (res, carried)


def _epi_add(acc, r):
    return (acc + r,)


def _epi_relu2(acc):
    p = jnp.maximum(acc, 0.0)
    return (p * p,)


def _epi_relu2_bwd(acc, r):
    return (acc * (2.0 * jnp.sqrt(r.astype(F32))),)


ROW_TILE = 512


def _row(tr, n):
    return pl.BlockSpec((tr, n), lambda i: (i, 0))


def _vec(n):
    return pl.BlockSpec((1, n), lambda i: (0, 0))


def _rms_fwd(x, w, name, comm=None):
    T, D = x.shape
    tr = min(ROW_TILE, T)

    def body(x_ref, w_ref, o_ref):
        xv = x_ref[...]
        r = lax.rsqrt(jnp.mean(xv * xv, axis=-1, keepdims=True) + EPS)
        o_ref[...] = (xv * r * w_ref[...]).astype(BF16)

    outs, carried = _call(body, grid=(T // tr,), in_specs=[_row(tr, D), _vec(D)], out_specs=[_row(tr, D)],
                          out_shape=[_sds((T, D), BF16)], args=[x, w], name=name, sem=("parallel",), comm=comm)
    return outs[0] if comm is None else (outs[0], carried)


def _rms_bwd(x, w, dh, dres, name):
    T, D = x.shape
    tr = min(ROW_TILE, T)

    def body(x_ref, w_ref, dh_ref, dres_ref, dx_ref, dxb_ref, dw_ref):
        @pl.when(pl.program_id(0) == 0)
        def _():
            dw_ref[...] = jnp.zeros_like(dw_ref)

        xv = x_ref[...]
        r = lax.rsqrt(jnp.mean(xv * xv, axis=-1, keepdims=True) + EPS)
        xh = xv * r
        dh_v = dh_ref[...]
        dw_ref[...] += jnp.sum(dh_v * xh, axis=0, keepdims=True)
        dxh = dh_v * w_ref[...]
        dx = r * (dxh - xh * jnp.mean(dxh * xh, axis=-1, keepdims=True)) + dres_ref[...]
        dx_ref[...] = dx
        dxb_ref[...] = dx.astype(BF16)

    return pl.pallas_call(
        body, grid=(T // tr,), in_specs=[_row(tr, D), _vec(D), _row(tr, D), _row(tr, D)],
        out_specs=[_row(tr, D), _row(tr, D), _vec(D)],
        out_shape=[_sds((T, D), F32), _sds((T, D), BF16), _sds((1, D), F32)],
        name=name, compiler_params=_params("arbitrary"))(x, w, dh, dres)


def _final(x2, w, tgt, name):
    T, D = x2.shape
    tr = min(ROW_TILE, T)

    def body(x_ref, w_ref, t_ref, dx_ref, dxb_ref, dw_ref, loss_ref):
        @pl.when(pl.program_id(0) == 0)
        def _():
            dw_ref[...] = jnp.zeros_like(dw_ref)
            loss_ref[...] = jnp.zeros_like(loss_ref)

        xv = x_ref[...]
        wv = w_ref[...]
        r = lax.rsqrt(jnp.mean(xv * xv, axis=-1, keepdims=True) + EPS)
        xh = xv * r
        err = xh * wv - t_ref[...]
        part = jnp.sum(jnp.sum(err * err, axis=1, keepdims=True), axis=0, keepdims=True) * (0.5 / D)
        loss_ref[...] += jnp.broadcast_to(part, loss_ref.shape)
        dy = err * (1.0 / D)
        dw_ref[...] += jnp.sum(dy * xh, axis=0, keepdims=True)
        dxh = dy * wv
        dx = r * (dxh - xh * jnp.mean(dxh * xh, axis=-1, keepdims=True))
        dx_ref[...] = dx
        dxb_ref[...] = dx.astype(BF16)

    return pl.pallas_call(
        body, grid=(T // tr,), in_specs=[_row(tr, D), _vec(D), _row(tr, D)],
        out_specs=[_row(tr, D), _row(tr, D), _vec(D), _vec(LANES)],
        out_shape=[_sds((T, D), F32), _sds((T, D), BF16), _sds((1, D), F32), _sds((1, LANES), F32)],
        name=name, compiler_params=_params("arbitrary"))(x2, w, tgt)


def _silu_parts(z):
    s = jax.nn.sigmoid(z)
    return z * s, s * (1.0 + z * (1.0 - s))


def _gnorm_fwd(y, z, w, name, comm=None):
    T, N = y.shape
    tr = min(ROW_TILE, T)

    def body(y_ref, z_ref, w_ref, o_ref):
        for g in range(N // GROUP_W):
            sl = slice(g * GROUP_W, (g + 1) * GROUP_W)
            silu, _ = _silu_parts(z_ref[:, sl])
            yz = y_ref[:, sl] * silu
            r = lax.rsqrt(jnp.mean(yz * yz, axis=-1, keepdims=True) + EPS)
            o_ref[:, sl] = (yz * r * w_ref[:, sl]).astype(BF16)

    outs, carried = _call(body, grid=(T // tr,), in_specs=[_row(tr, N), _row(tr, N), _vec(N)], out_specs=[_row(tr, N)],
                          out_shape=[_sds((T, N), BF16)], args=[y, z, w], name=name, sem=("parallel",), comm=comm)
    return outs[0] if comm is None else (outs[0], carried)


def _gnorm_bwd(y, z, w, dyb, name):
    T, N = y.shape
    tr = min(ROW_TILE, T)

    def body(y_ref, z_ref, w_ref, d_ref, dy_ref, dz_ref, dw_ref):
        @pl.when(pl.program_id(0) == 0)
        def _():
            dw_ref[...] = jnp.zeros_like(dw_ref)

        for g in range(N // GROUP_W):
            sl = slice(g * GROUP_W, (g + 1) * GROUP_W)
            yv = y_ref[:, sl]
            silu, dsilu = _silu_parts(z_ref[:, sl])
            yz = yv * silu
            r = lax.rsqrt(jnp.mean(yz * yz, axis=-1, keepdims=True) + EPS)
            yzh = yz * r
            d = d_ref[:, sl]
            dw_ref[:, sl] += jnp.sum(d * yzh, axis=0, keepdims=True)
            dyzh = d * w_ref[:, sl]
            dyz = r * (dyzh - yzh * jnp.mean(dyzh * yzh, axis=-1, keepdims=True))
            dy_ref[:, sl] = dyz * silu
            dz_ref[:, sl] = (dyz * yv * dsilu).astype(BF16)

    return pl.pallas_call(
        body, grid=(T // tr,), in_specs=[_row(tr, N), _row(tr, N), _vec(N), _row(tr, N)],
        out_specs=[_row(tr, N), _row(tr, N), _vec(N)],
        out_shape=[_sds((T, N), F32), _sds((T, N), BF16), _sds((1, N), F32)],
        name=name, compiler_params=_params("arbitrary"))(y, z, w, dyb)


def _merge_fwd(gate_raw, b_gate, br_a, br_b, name):
    T, D = br_a.shape
    tr = min(ROW_TILE, T)

    def body(g_ref, bg_ref, a_ref, b_ref, o_ref):
        g = jax.nn.sigmoid(g_ref[...] + bg_ref[...])
        o_ref[...] = (g[:, :D] * a_ref[...] + g[:, D:] * b_ref[...]).astype(BF16)

    return pl.pallas_call(body, grid=(T // tr,), in_specs=[_row(tr, 2 * D), _vec(2 * D), _row(tr, D), _row(tr, D)],
                          out_specs=_row(tr, D), out_shape=_sds((T, D), BF16), name=name,
                          compiler_params=_params("parallel"))(gate_raw, b_gate, br_a, br_b)


def _merge_bwd(dmerged, gate_raw, b_gate, br_a, br_b, name):
    T, D = br_a.shape
    tr = min(ROW_TILE, T)

    def body(d_ref, g_ref, bg_ref, a_ref, b_ref, da_ref, db_ref, dg_ref, dbg_ref):
        @pl.when(pl.program_id(0) == 0)
        def _():
            dbg_ref[...] = jnp.zeros_like(dbg_ref)

        g = jax.nn.sigmoid(g_ref[...] + bg_ref[...])
        d = d_ref[...]
        da_ref[...] = (d * g[:, :D]).astype(BF16)
        db_ref[...] = (d * g[:, D:]).astype(BF16)
        dg = jnp.concatenate([d * a_ref[...], d * b_ref[...]], axis=1) * g * (1.0 - g)
        dg_ref[...] = dg.astype(BF16)
        dbg_ref[...] += jnp.sum(dg, axis=0, keepdims=True)

    return pl.pallas_call(
        body, grid=(T // tr,), in_specs=[_row(tr, D), _row(tr, 2 * D), _vec(2 * D), _row(tr, D), _row(tr, D)],
        out_specs=[_row(tr, D), _row(tr, D), _row(tr, 2 * D), _vec(2 * D)],
        out_shape=[_sds((T, D), BF16), _sds((T, D), BF16), _sds((T, 2 * D), BF16), _sds((1, 2 * D), F32)],
        name=name, compiler_params=_params("arbitrary"))(dmerged, gate_raw, b_gate, br_a, br_b)


CB_W = 256
CONV_ROWS = 32
CONV_PAD = 8


def _rows_down(load, r0, s):
    if s == 0:
        return load(r0, r0 + CONV_ROWS)
    if r0 == 0:
        row = lax.broadcasted_iota(jnp.int32, (CONV_ROWS, CB_W), 0)
        return jnp.where(row >= s, pltpu.roll(load(0, CONV_ROWS), s, 0), 0.0)
    return load(r0 - s, r0 - s + CONV_ROWS)


def _conv_tile(load, taps, r0):
    K = len(taps)
    us = [_rows_down(load, r0, K - 1 - k) for k in range(K)]
    acc = us[K - 1] * taps[K - 1]
    for k in range(K - 1):
        acc = acc + us[k] * taps[k]
    return acc, us


def _conv_back_tile(scr, taps, r0):
    K = len(taps)
    du = scr[r0:r0 + CONV_ROWS, :] * taps[K - 1]
    for k in range(K - 1):
        s = K - 1 - k
        du = du + scr[r0 + s:r0 + s + CONV_ROWS, :] * taps[k]
    return du


def _fold8(v):
    return jnp.sum(v.reshape(CONV_ROWS // 8, 8, v.shape[1]), axis=0)


def _col(T, j0=0):
    return pl.BlockSpec((T, CB_W), lambda j: (0, j + j0))


def _sc_fwd(psc, w, name):
    T, D = psc.shape[0], psc.shape[1] // 3
    nb = D // CB_W

    def body(b_ref, c_ref, x_ref, w_ref, o_ref):
        taps = [w_ref[k:k + 1, :] for k in range(SC_K)]
        load = lambda a, b: c_ref[a:b, :] * x_ref[a:b, :]
        for r0 in range(0, T, CONV_ROWS):
            cu, _ = _conv_tile(load, taps, r0)
            o_ref[r0:r0 + CONV_ROWS, :] = (b_ref[r0:r0 + CONV_ROWS, :] * cu).astype(BF16)

    return pl.pallas_call(
        body, grid=(nb,), in_specs=[_col(T), _col(T, nb), _col(T, 2 * nb), pl.BlockSpec((SC_K, CB_W), lambda j: (0, j))],
        out_specs=_col(T), out_shape=_sds((T, D), BF16), name=name, compiler_params=_params("parallel"))(psc, psc, psc, w)


def _sc_bwd(psc, w, dya, name):
    T, D = psc.shape[0], psc.shape[1] // 3
    nb = D // CB_W

    def body(b_ref, c_ref, x_ref, w_ref, d_ref, db_ref, dc_ref, dx_ref, dw_ref, scr):
        taps = [w_ref[k:k + 1, :] for k in range(SC_K)]
        load = lambda a, b: c_ref[a:b, :] * x_ref[a:b, :]
        scr[T:T + CONV_PAD, :] = jnp.zeros((CONV_PAD, CB_W), F32)
        dw8 = [jnp.zeros((8, CB_W), F32)] * SC_K
        for r0 in range(0, T, CONV_ROWS):
            rows = slice(r0, r0 + CONV_ROWS)
            cu, us = _conv_tile(load, taps, r0)
            d = d_ref[rows, :]
            db_ref[rows, :] = (d * cu).astype(BF16)
            dcu = d * b_ref[rows, :]
            scr[rows, :] = dcu
            dw8 = [acc + _fold8(dcu * u) for acc, u in zip(dw8, us)]
        for k in range(SC_K):
            dw_ref[k:k + 1, :] = jnp.sum(dw8[k], axis=0, keepdims=True)
        for r0 in range(0, T, CONV_ROWS):
            rows = slice(r0, r0 + CONV_ROWS)
            du = _conv_back_tile(scr, taps, r0)
            dc_ref[rows, :] = (du * x_ref[rows, :]).astype(BF16)
            dx_ref[rows, :] = (du * c_ref[rows, :]).astype(BF16)

    wspec = pl.BlockSpec((SC_K, CB_W), lambda j: (0, j))
    return pl.pallas_call(
        body, grid=(nb,), in_specs=[_col(T), _col(T, nb), _col(T, 2 * nb), wspec, _col(T)],
        out_specs=[_col(T), _col(T), _col(T), wspec],
        out_shape=[_sds((T, D), BF16)] * 3 + [_sds((SC_K, D), F32)],
        scratch_shapes=[pltpu.VMEM((T + CONV_PAD, CB_W), F32)],
        name=name, compiler_params=_params("parallel"))(psc, psc, psc, w, dya)


def _ssm_conv_fwd(u, w, b, name, comm=None):
    T, N = u.shape

    def body(u_ref, w_ref, b_ref, o_ref):
        taps = [w_ref[k:k + 1, :] for k in range(SSM_K)]
        bias = b_ref[...]
        for r0 in range(0, T, CONV_ROWS):
            c, _ = _conv_tile(lambda a, b: u_ref[a:b, :], taps, r0)
            c = c + bias
            o_ref[r0:r0 + CONV_ROWS, :] = c * jax.nn.sigmoid(c)

    outs, carried = _call(
        body, grid=(N // CB_W,), in_specs=[_col(T), pl.BlockSpec((SSM_K, CB_W), lambda j: (0, j)), pl.BlockSpec((1, CB_W), lambda j: (0, j))],
        out_specs=[_col(T)], out_shape=[_sds((T, N), F32)], args=[u, w, b], name=name, sem=("parallel",), comm=comm)
    return outs[0] if comm is None else (outs[0], carried)


def _ssm_conv_bwd(u, w, b, dxs, dB, dC, name, comm=None):
    T, N = u.shape
    n_x, n_b = dxs.shape[1] // CB_W, dB.shape[1] // CB_W

    def body(u_ref, w_ref, b_ref, dx_ref, db_ref, dc_ref, du_ref, dw_ref, dbias_ref, scr):
        j = pl.program_id(0)
        taps = [w_ref[k:k + 1, :] for k in range(SSM_K)]
        bias = b_ref[...]
        scr[T:T + CONV_PAD, :] = jnp.zeros((CONV_PAD, CB_W), F32)
        dw8 = [jnp.zeros((8, CB_W), F32)] * SSM_K
        db8 = jnp.zeros((8, CB_W), F32)
        for r0 in range(0, T, CONV_ROWS):
            rows = slice(r0, r0 + CONV_ROWS)
            c, us = _conv_tile(lambda a, b: u_ref[a:b, :], taps, r0)
            _, dsilu = _silu_parts(c + bias)
            d = jnp.where(j < n_x, dx_ref[rows, :], jnp.where(j < n_x + n_b, db_ref[rows, :], dc_ref[rows, :])) * dsilu
            scr[rows, :] = d
            db8 = db8 + _fold8(d)
            dw8 = [acc + _fold8(d * u) for acc, u in zip(dw8, us)]
        dbias_ref[...] = jnp.sum(db8, axis=0, keepdims=True)
        for k in range(SSM_K):
            dw_ref[k:k + 1, :] = jnp.sum(dw8[k], axis=0, keepdims=True)
        for r0 in range(0, T, CONV_ROWS):
            du_ref[r0:r0 + CONV_ROWS, :] = _conv_back_tile(scr, taps, r0).astype(BF16)

    wspec = pl.BlockSpec((SSM_K, CB_W), lambda j: (0, j))
    bspec = pl.BlockSpec((1, CB_W), lambda j: (0, j))
    outs, carried = _call(
        body, grid=(N // CB_W,),
        in_specs=[_col(T), wspec, bspec,
                  pl.BlockSpec((T, CB_W), lambda j: (0, jnp.minimum(j, n_x - 1))),
                  pl.BlockSpec((T, CB_W), lambda j: (0, jnp.clip(j - n_x, 0, n_b - 1))),
                  pl.BlockSpec((T, CB_W), lambda j: (0, jnp.clip(j - n_x - n_b, 0, n_b - 1)))],
        out_specs=[_col(T), wspec, bspec],
        out_shape=[_sds((T, N), BF16), _sds((SSM_K, N), F32), _sds((1, N), F32)],
        scratch=[pltpu.VMEM((T + CONV_PAD, CB_W), F32)],
        args=[u, w, b, dxs, dB, dC], name=name, sem=("parallel",), comm=comm)
    return outs if comm is None else (outs, carried)


def _split3(v):
    hi = v.astype(BF16)
    r = v - hi.astype(F32)
    mid = r.astype(BF16)
    lo = (r - mid.astype(F32)).astype(BF16)
    return hi, mid, lo


def _head_expand(n_lanes):
    h = lax.broadcasted_iota(jnp.int32, (LANES, n_lanes), 0)
    l = lax.broadcasted_iota(jnp.int32, (LANES, n_lanes), 1)
    return (jnp.right_shift(l, HEADDIM.bit_length() - 1) == h).astype(BF16)


def _softplus(v):
    return jnp.maximum(v, 0.0) + jnp.log1p(jnp.exp(-jnp.abs(v)))


PREP_CHUNKS = 4


def _ssd_prep(dt_raw, dt_bias, a_log, n_inner, name):
    T = dt_raw.shape[0]
    rows = PREP_CHUNKS * CHUNK if T % (PREP_CHUNKS * CHUNK) == 0 else CHUNK

    def body(r_ref, b_ref, al_ref, ex_ref, dt_ref, cs_ref):
        i = lax.broadcasted_iota(jnp.int32, (CHUNK, CHUNK), 0)
        j = lax.broadcasted_iota(jnp.int32, (CHUNK, CHUNK), 1)
        tri = (j <= i).astype(BF16)
        ex = ex_ref[...]
        for r0 in range(0, rows, CHUNK):
            dt = _softplus(r_ref[r0:r0 + CHUNK, :] + b_ref[...])
            a = dt * (-jnp.exp(al_ref[...]))
            cs = sum(_dot(tri, p) for p in _split3(a))
            dt_ref[r0:r0 + CHUNK, :] = sum(_dot(p, ex) for p in _split3(dt))
            cs_ref[r0:r0 + CHUNK, :] = sum(_dot(p, ex) for p in _split3(cs))

    blk = pl.BlockSpec((rows, LANES), lambda c: (c, 0))
    out = pl.BlockSpec((rows, n_inner), lambda c: (c, 0))
    ex_spec = pl.BlockSpec((LANES, n_inner), lambda c: (0, 0))
    return pl.pallas_call(body, grid=(T // rows,), in_specs=[blk, _vec(LANES), _vec(LANES), ex_spec], out_specs=[out, out],
                          out_shape=[_sds((T, n_inner), F32)] * 2, name=name,
                          compiler_params=_params("parallel"))(dt_raw, dt_bias, a_log, _head_expand(n_inner))


def _pair_terms(cs_p):
    lane = lax.broadcasted_iota(jnp.int32, (CHUNK, CHUNK), 1)
    sub = lax.broadcasted_iota(jnp.int32, (CHUNK, CHUNK), 0)
    csT = cs_p.T
    Ls = []
    for k in range(2):
        col = jnp.sum(jnp.where(lane == k * HEADDIM, cs_p, 0.0), axis=1, keepdims=True)
        rowv = csT[k * HEADDIM:k * HEADDIM + 1, :]
        Ls.append(jnp.exp(jnp.where(sub >= lane, col - rowv, -jnp.inf)))
    return Ls, jnp.exp(csT[:, CHUNK - 1:CHUNK])


def _block_diag(xp):
    lane = lax.broadcasted_iota(jnp.int32, xp.shape, 1)
    return jnp.concatenate([jnp.where(lane < HEADDIM, xp, 0.0), jnp.where(lane >= HEADDIM, xp, 0.0)], axis=0)


SSD_GROUPS_PER_STEP = 8


def _ssd_specs(T, n_inner):
    nc, gs = T // CHUNK, SSD_GROUPS_PER_STEP
    bo, co = n_inner // (gs * NSTATE), (n_inner + NGROUPS * NSTATE) // (gs * NSTATE)
    assert NGROUPS % gs == 0 and n_inner % (gs * NSTATE) == 0 and (NGROUPS * NSTATE) % (gs * NSTATE) == 0
    g_blk = lambda f: pl.BlockSpec((CHUNK, gs * GROUP_W), lambda c, s: (f(c), s))
    b_blk = lambda f: pl.BlockSpec((CHUNK, gs * NSTATE), lambda c, s: (f(c), bo + s))
    c_blk = lambda f: pl.BlockSpec((CHUNK, gs * NSTATE), lambda c, s: (f(c), co + s))
    return nc, g_blk, b_blk, c_blk


def _ssd_fwd(xbc, dt_e, cs_e, d_e, name, comm=None):
    T = xbc.shape[0]
    n_inner = dt_e.shape[1]
    nc, g_blk, b_blk, c_blk = _ssd_specs(T, n_inner)
    ident = lambda c: c

    gs = SSD_GROUPS_PER_STEP

    def body(xs_ref, b_ref, c_ref, dt_ref, cs_ref, d_ref, y_ref, p_ref, st):
        c, s = pl.program_id(0), pl.program_id(1)

        @pl.when(c == 0)
        def _():
            for gi in range(gs):
                st[s * gs + gi] = jnp.zeros((GROUP_W, NSTATE), F32)

        for gi in range(gs):
            g = s * gs + gi
            gw, gn = slice(gi * GROUP_W, (gi + 1) * GROUP_W), slice(gi * NSTATE, (gi + 1) * NSTATE)
            P = st[g]
            p_ref[0, gi] = P
            xs, dt, cs = xs_ref[:, gw], dt_ref[:, gw], cs_ref[:, gw]
            Bf, Cf = b_ref[:, gn], c_ref[:, gn]
            Cb = Cf.astype(BF16)
            CBm = _dot(Cb, Bf.astype(BF16), NT)
            X = xs * dt
            decay = jnp.exp(cs[CHUNK - 1:CHUNK, :] - cs)
            y_off = _dot(Cb, P.astype(BF16), NT) * jnp.exp(cs)
            ys, ecl = [], []
            for pr in range(2):
                sl = slice(pr * LANES, (pr + 1) * LANES)
                Ls, e_last = _pair_terms(cs[:, sl])
                ecl.append(e_last)
                Mcat = jnp.concatenate([(CBm * L).astype(BF16) for L in Ls], axis=1)
                ys.append(_dot(Mcat, _block_diag(X[:, sl]).astype(BF16)))
            y_ref[:, gw] = jnp.concatenate(ys, axis=1) + y_off + xs * d_ref[:, gw]
            S = _dot3(X * decay, Bf, TN)
            st[g] = P * jnp.concatenate(ecl, axis=0) + S

    p_blk = pl.BlockSpec((1, gs, GROUP_W, NSTATE), lambda c, s: (c, s, 0, 0))
    outs, carried = _call(
        body, grid=(nc, NGROUPS // gs),
        in_specs=[g_blk(ident), b_blk(ident), c_blk(ident), g_blk(ident), g_blk(ident), pl.BlockSpec((1, gs * GROUP_W), lambda c, s: (0, s))],
        out_specs=[g_blk(ident), p_blk],
        out_shape=[_sds((T, n_inner), F32), _sds((nc, NGROUPS, GROUP_W, NSTATE), F32)],
        scratch=[pltpu.VMEM((NGROUPS, GROUP_W, NSTATE), F32)],
        args=[xbc, xbc, xbc, dt_e, cs_e, d_e], name=name, sem=("arbitrary", "arbitrary"), comm=comm)
    return outs if comm is None else (outs, carried)


def _ssd_bwd(xbc, dt_e, cs_e, d_e, states, dy, name, comm=None):
    T = xbc.shape[0]
    n_inner = dt_e.shape[1]
    nc, g_blk, b_blk, c_blk = _ssd_specs(T, n_inner)
    rev = lambda c: nc - 1 - c

    gs = SSD_GROUPS_PER_STEP

    def body(xs_ref, b_ref, c_ref, dt_ref, cs_ref, d_ref, p_ref, pn_ref, dy_ref,
             dxs_ref, db_ref, dc_ref, ddt_ref, dcs_ref, dd_ref, dst):
        cc, s = pl.program_id(0), pl.program_id(1)

        @pl.when(cc == 0)
        def _():
            for gi in range(gs):
                dst[s * gs + gi] = jnp.zeros((GROUP_W, NSTATE), F32)

        for gi in range(gs):
            one_group(s * gs + gi, gi, xs_ref, b_ref, c_ref, dt_ref, cs_ref, d_ref, p_ref, pn_ref, dy_ref,
                      dxs_ref, db_ref, dc_ref, ddt_ref, dcs_ref, dd_ref, dst)

    def one_group(g, gi, xs_ref, b_ref, c_ref, dt_ref, cs_ref, d_ref, p_ref, pn_ref, dy_ref,
                  dxs_ref, db_ref, dc_ref, ddt_ref, dcs_ref, dd_ref, dst):
        gw, gn = slice(gi * GROUP_W, (gi + 1) * GROUP_W), slice(gi * NSTATE, (gi + 1) * NSTATE)
        dS = dst[g]
        P, Pn = p_ref[0, gi], pn_ref[0, gi]
        xs, dt, cs, dY = xs_ref[:, gw], dt_ref[:, gw], cs_ref[:, gw], dy_ref[:, gw]
        Bf, Cf = b_ref[:, gn], c_ref[:, gn]
        Bb, Cb = Bf.astype(BF16), Cf.astype(BF16)
        X = xs * dt
        ecs = jnp.exp(cs)
        decay = jnp.exp(cs[CHUNK - 1:CHUNK, :] - cs)
        CBm = _dot3(Cf, Bf, NT)
        dYe = dY * ecs
        dP_off = _dot3(dYe, Cf, TN)
        dC = _dot(dYe.astype(BF16), P.astype(BF16))
        dcs = dYe * _dot3(Cf, P, NT)
        Xd = X * decay
        dB = _dot(Xd.astype(BF16), dS.astype(BF16))
        E = _dot3(Bf, dS, NT)
        dX = E * decay
        dcs = dcs - E * Xd
        R = _dot3(jnp.ones((8, NSTATE), F32), dS * Pn, NT)
        sub_g = lax.broadcasted_iota(jnp.int32, (CHUNK, GROUP_W), 0)
        dcs = dcs + jnp.where(sub_g == CHUNK - 1, R[0:1, :], 0.0)
        lane = lax.broadcasted_iota(jnp.int32, (CHUNK, CHUNK), 1)
        sub = lax.broadcasted_iota(jnp.int32, (CHUNK, CHUNK), 0)
        dCB = jnp.zeros((CHUNK, CHUNK), F32)
        dXs, dcss, ecl = [], [], []
        for pr in range(2):
            sl = slice(pr * LANES, (pr + 1) * LANES)
            Ls, e_last = _pair_terms(cs[:, sl])
            ecl.append(e_last)
            dYpb = dY[:, sl].astype(BF16)
            dMcat = _dot(dYpb, _block_diag(X[:, sl]).astype(BF16), NT)
            Mcat = jnp.concatenate([(CBm * L).astype(BF16) for L in Ls], axis=1)
            dXt = _dot(Mcat, dYpb, TN)
            dXs.append(jnp.where(lane < HEADDIM, dXt[:CHUNK], dXt[CHUNK:]))
            colacc = jnp.zeros((CHUNK, CHUNK), F32)
            rowacc = jnp.zeros((CHUNK, CHUNK), F32)
            for k in range(2):
                dG = dMcat[:, k * CHUNK:(k + 1) * CHUNK] * Ls[k]
                dCB = dCB + dG
                Q = dG * CBm
                colacc = colacc + jnp.where(lane == k * HEADDIM, jnp.sum(Q, axis=1, keepdims=True), 0.0)
                rowacc = rowacc + jnp.where(sub == k * HEADDIM, jnp.sum(Q, axis=0, keepdims=True), 0.0)
            dcss.append(colacc - rowacc.T)
        dX = dX + jnp.concatenate(dXs, axis=1)
        dcs = dcs + jnp.concatenate(dcss, axis=1)
        dCBb = dCB.astype(BF16)
        dc_ref[:, gn] = dC + _dot(dCBb, Bb)
        db_ref[:, gn] = dB + _dot(dCBb, Cb, TN)
        dxs_ref[:, gw] = dX * dt + dY * d_ref[:, gw]
        ddt_ref[:, gw] = dX * xs
        dcs_ref[:, gw] = dcs
        dd_ref[0, :, gw] = jnp.sum(dY * xs, axis=0, keepdims=True)
        dst[g] = dS * jnp.concatenate(ecl, axis=0) + dP_off

    p_blk = pl.BlockSpec((1, gs, GROUP_W, NSTATE), lambda c, s: (nc - 1 - c, s, 0, 0))
    pn_blk = pl.BlockSpec((1, gs, GROUP_W, NSTATE), lambda c, s: (jnp.minimum(nc - c, nc - 1), s, 0, 0))
    st_blk = pl.BlockSpec((CHUNK, gs * NSTATE), lambda c, s: (nc - 1 - c, s))
    outs, carried = _call(
        body, grid=(nc, NGROUPS // gs),
        in_specs=[g_blk(rev), b_blk(rev), c_blk(rev), g_blk(rev), g_blk(rev), pl.BlockSpec((1, gs * GROUP_W), lambda c, s: (0, s)),
                  p_blk, pn_blk, g_blk(rev)],
        out_specs=[g_blk(rev), st_blk, st_blk, g_blk(rev), g_blk(rev), pl.BlockSpec((1, 1, gs * GROUP_W), lambda c, s: (nc - 1 - c, 0, s))],
        out_shape=[_sds((T, n_inner), F32), _sds((T, NGROUPS * NSTATE), F32), _sds((T, NGROUPS * NSTATE), F32),
                   _sds((T, n_inner), F32), _sds((T, n_inner), F32), _sds((nc, 1, n_inner), F32)],
        scratch=[pltpu.VMEM((NGROUPS, GROUP_W, NSTATE), F32)],
        args=[xbc, xbc, xbc, dt_e, cs_e, d_e, states, states, dy], name=name, sem=("arbitrary", "arbitrary"), comm=comm)
    return outs if comm is None else (outs, carried)


def _ssd_post(ddt_e, dcs_e, dd_p, dt_raw, dt_bias, a_log, n_heads, name):
    T, n_inner = ddt_e.shape
    per_step = 1
    rows = per_step * CHUNK

    def body(ddt_ref, dcs_ref, dd_ref, r_ref, b_ref, al_ref, ex_ref, draw_ref, dbias_ref, dal_ref, ddsk_ref):
        @pl.when(pl.program_id(0) == 0)
        def _():
            dbias_ref[...] = jnp.zeros_like(dbias_ref)
            dal_ref[...] = jnp.zeros_like(dal_ref)
            ddsk_ref[...] = jnp.zeros_like(ddsk_ref)

        A = -jnp.exp(al_ref[...])
        i = lax.broadcasted_iota(jnp.int32, (CHUNK, CHUNK), 0)
        j = lax.broadcasted_iota(jnp.int32, (CHUNK, CHUNK), 1)
        upper = (j >= i).astype(BF16)
        lane = lax.broadcasted_iota(jnp.int32, (CHUNK, LANES), 1)
        dbias, dal, ddsk = [jnp.zeros((1, LANES), F32)] * 3
        for k in range(per_step):
            rk = slice(k * CHUNK, (k + 1) * CHUNK)
            spread = [ddt_ref[rk, :], dcs_ref[rk, :], jnp.broadcast_to(dd_ref[k], (8, n_inner))]
            stacked = _dot(jnp.concatenate([p for v in spread for p in _split3(v)], axis=0), ex_ref[...], NT)
            sums, r0 = [], 0
            for v in spread:
                n = v.shape[0]
                sums.append(stacked[r0:r0 + n] + stacked[r0 + n:r0 + 2 * n] + stacked[r0 + 2 * n:r0 + 3 * n])
                r0 += 3 * n
            ddt_h, dcs_h, dd_h = sums
            raw = r_ref[rk, :] + b_ref[...]
            dt = _softplus(raw)
            da = sum(_dot(upper, p) for p in _split3(dcs_h))
            ddt = ddt_h + da * A
            draw = jnp.where(lane < n_heads, ddt * jax.nn.sigmoid(raw), 0.0)
            draw_ref[rk, :] = draw.astype(BF16)
            dbias = dbias + jnp.sum(draw, axis=0, keepdims=True)
            dal = dal + jnp.sum(da * dt, axis=0, keepdims=True) * A
            ddsk = ddsk + dd_h[0:1, :]
        dbias_ref[...] += dbias
        dal_ref[...] += dal
        ddsk_ref[...] += ddsk

    wide = pl.BlockSpec((rows, n_inner), lambda c: (c, 0))
    blk = pl.BlockSpec((rows, LANES), lambda c: (c, 0))
    return pl.pallas_call(
        body, grid=(T // rows,),
        in_specs=[wide, wide, pl.BlockSpec((per_step, 1, n_inner), lambda c: (c, 0, 0)), blk, _vec(LANES), _vec(LANES),
                  pl.BlockSpec((LANES, n_inner), lambda c: (0, 0))],
        out_specs=[blk, _vec(LANES), _vec(LANES), _vec(LANES)],
        out_shape=[_sds((T, LANES), BF16)] + [_sds((1, LANES), F32)] * 3,
        name=name, compiler_params=_params("arbitrary"))(ddt_e, dcs_e, dd_p, dt_raw, dt_bias, a_log, _head_expand(n_inner))


def _row2(v):
    return v.reshape(1, -1).astype(F32)


def _pad_lanes(v):
    return jnp.pad(_row2(v), ((0, 0), (0, LANES - v.shape[-1])))


class _NoExchange:
    def __init__(self, W):
        self.W, self.grads = W, {}

    def weight(self, k):
        return self.W[k]

    def carry(self, name):
        return None

    def carried(self, name, outs):
        pass

    def grad(self, k, g):
        self.grads[k] = g

    def tok(self):
        return jnp.zeros((), F32)

    def point(self, name, value):
        pass


def _local_step(x, tgt, S, small):
    T, D = x.shape

    def mm(a, b, *, name, **kw):
        comm = S.carry(name)
        if comm is None:
            return _mm(a, b, name=name, **kw)
        res, outs = _mm(a, b, name=name, comm=comm, **kw)
        S.carried(name, outs)
        return res

    def carrying(fn, *args, name):
        comm = S.carry(name)
        if comm is None:
            return fn(*args, name)
        res, outs = fn(*args, name, comm=comm)
        S.carried(name, outs)
        return res

    n_inner = 2 * D
    n_heads = n_inner // HEADDIM
    norm_mix, norm_mlp, norm_final = _row2(small["norm_mix"]), _row2(small["norm_mlp"]), _row2(small["norm_final"])
    b_gate, ssm_b, ssm_norm_w = _row2(small["b_gate"]), _row2(small["ssm_conv_b"]), _row2(small["ssm_norm_w"])
    dt_bias, a_log = _pad_lanes(small["dt_bias"]), _pad_lanes(small["A_log"])
    d_e = jnp.repeat(small["D_skip"].astype(F32), HEADDIM).reshape(1, n_inner)

    hb = carrying(_rms_fwd, x, norm_mix, name="rms_mix")
    sc_w, ssm_w = S.weight("sc_conv_w"), S.weight("ssm_conv_w")
    p_xbc = mm(hb, S.weight("xbc"), mode="nn", name="proj_xbc")
    p_dt = mm(hb, S.weight("dt"), mode="nn", name="proj_dt")
    p_z = mm(hb, S.weight("z"), mode="nn", name="proj_z")
    p_sc = mm(hb, S.weight("sc"), mode="nn", name="proj_sc")
    p_gate = mm(hb, S.weight("gate"), mode="nn", name="proj_gate")
    xbc = carrying(_ssm_conv_fwd, p_xbc, ssm_w, ssm_b, name="ssm_conv_fwd")
    dt_e, cs_e = _ssd_prep(p_dt, dt_bias, a_log, n_inner, "ssd_prep")
    ya = _sc_fwd(p_sc, sc_w, "sc_fwd")
    y, states = carrying(_ssd_fwd, xbc, dt_e, cs_e, d_e, name="ssd_fwd")
    S.point("mixers_done", [y, ya, p_gate])
    yb = carrying(_gnorm_fwd, y, p_z, ssm_norm_w, name="gnorm_fwd")
    br_a = mm(ya, S.weight("bsc"), mode="nn", name="branch_sc")
    br_b = mm(yb, S.weight("bssm"), mode="nn", name="branch_ssm")
    merged = _merge_fwd(p_gate, b_gate, br_a, br_b, "merge_fwd")
    x1 = mm(merged, S.weight("out"), mode="nn", name="out_proj", extras=(x,), epi=_epi_add)
    h2 = _rms_fwd(x1, norm_mlp, "rms_mlp")
    r_act = mm(h2, S.weight("w1"), mode="nn", name="mlp_up", epi=_epi_relu2, out_dtypes=(BF16,))
    x2 = mm(r_act, S.weight("w2"), mode="nn", name="mlp_down", extras=(x1,), epi=_epi_add)
    dx2, dx2b, g_norm_final, loss_row = _final(x2, norm_final, tgt, "final")

    S.grad("w2", mm(r_act, dx2b, mode="tn", name="mlp_down_dw", out_dtypes=(BF16,)))
    da = mm(dx2b, S.weight("w2"), mode="nt", name="mlp_down_dx", extras=(r_act,), epi=_epi_relu2_bwd, out_dtypes=(BF16,))
    S.grad("w1", mm(h2, da, mode="tn", name="mlp_up_dw", out_dtypes=(BF16,)))
    dh2 = mm(da, S.weight("w1"), mode="nt", name="mlp_up_dx")
    dx1, dx1b, g_norm_mlp = _rms_bwd(x1, norm_mlp + S.tok(), dh2, dx2, "rms_mlp_bwd")
    S.grad("out", mm(merged, dx1b, mode="tn", name="out_proj_dw", out_dtypes=(BF16,)))
    dmerged = mm(dx1b, S.weight("out"), mode="nt", name="out_proj_dx")
    dbr_a, dbr_b, d_gate, g_b_gate = _merge_bwd(dmerged, p_gate, b_gate, br_a, br_b, "merge_bwd")
    S.grad("bssm", mm(yb, dbr_b, mode="tn", name="branch_ssm_dw", out_dtypes=(BF16,)))
    S.grad("bsc", mm(ya, dbr_a, mode="tn", name="branch_sc_dw", out_dtypes=(BF16,)))
    dyb = mm(dbr_b, S.weight("bssm"), mode="nt", name="branch_ssm_dx")
    dya = mm(dbr_a, S.weight("bsc"), mode="nt", name="branch_sc_dx")
    dy, d_z, g_ssm_norm_w = _gnorm_bwd(y, p_z, ssm_norm_w + S.tok(), dyb, "gnorm_bwd")
    dxs, dB, dC, ddt_e, dcs_e, dd_p = carrying(_ssd_bwd, xbc, dt_e, cs_e, d_e, states, dy, name="ssd_bwd")
    d_dt, g_dt_bias, g_a_log, g_d_skip = _ssd_post(ddt_e, dcs_e, dd_p, p_dt, dt_bias, a_log, n_heads, "ssd_post")
    d_xbc, g_ssm_w, g_ssm_b = carrying(_ssm_conv_bwd, p_xbc, ssm_w, ssm_b, dxs, dB, dC, name="ssm_conv_bwd")
    d_scB, d_scC, d_scX, g_sc_w = _sc_bwd(p_sc, sc_w, dya, "sc_bwd")
    d_sc = jnp.concatenate([d_scB, d_scC, d_scX], axis=1)
    pieces = [("sc", d_sc), ("z", d_z), ("xbc", d_xbc), ("dt", d_dt), ("gate", d_gate)]
    S.grad("win", {k: mm(hb, d, mode="tn", name="proj_dw_" + k, out_dtypes=(BF16,)) for k, d in pieces})
    pieces = [(k, d + S.tok().astype(d.dtype) if k == "dt" else d) for k, d in pieces]
    dh = mm([d for _, d in pieces], [S.weight(k) for k, _ in pieces], mode="nt", name="proj_dx")
    grad_x, _, g_norm_mix = _rms_bwd(x, norm_mix, dh, dx1, "rms_mix_bwd")

    g_small = dict(norm_mix=g_norm_mix, b_gate=g_b_gate, sc_conv_w=g_sc_w, ssm_conv_w=g_ssm_w, ssm_conv_b=g_ssm_b,
                   dt_bias=g_dt_bias, A_log=g_a_log, D_skip=g_d_skip, ssm_norm_w=g_ssm_norm_w, norm_mlp=g_norm_mlp,
                   norm_final=g_norm_final, loss=loss_row)
    return grad_x, g_small


class _Place:
    def __init__(self, k=0):
        x, y, c = lax.axis_index("x"), lax.axis_index("y"), lax.axis_index("c")
        self.x = 1 - x if k & 4 else x
        self.y = 1 - y if k & 2 else y
        self.c = 1 - c if k & 1 else c
        self.chip = 2 * self.x + self.y
        self.id = 2 * self.chip + self.c


ICI_PEERS = (2, 4, 6)
SIBLING = (1,)
ALL_PEERS = (1, 2, 3, 4, 5, 6, 7)


class _Comm:
    def __init__(self, arrs, out_shape, ks, src, dst, own=None, aliases=None):
        self.arrs, self.out_shape, self.ks = list(arrs), list(out_shape), tuple(ks)
        self.n = len(self.arrs)
        self.src, self.dst, self.own = src, dst, own
        self.aliases = aliases or {}
        dma = pltpu.SemaphoreType.DMA
        self.scratch = [dma((self.n, len(self.ks))), dma((self.n, len(self.ks))), dma((self.n,))]

    def _copies(self, ins, outs, sems, with_recvs):
        send_sems, recv_sems, local_sems = sems
        me = _Place()
        owns, sends, recvs = [], [], []
        for a in range(self.n):
            if self.own is not None:
                s, d = self.own(a, ins[a], outs[a], me)
                owns.append(pltpu.make_async_copy(s, d, local_sems.at[a]))
            for i, k in enumerate(self.ks):
                peer = _Place(k)
                for sender, lst in ((me, sends), (peer, recvs)) if with_recvs else ((me, sends),):
                    lst.append(pltpu.make_async_remote_copy(
                        src_ref=self.src(a, ins[a], me, peer), dst_ref=self.dst(a, outs[a], sender),
                        send_sem=send_sems.at[a, i], recv_sem=recv_sems.at[a, i],
                        device_id=(peer.x, peer.y, peer.c), device_id_type=MESH))
        return owns, sends, recvs

    def start(self, ins, outs, sems):
        owns, sends, _ = self._copies(ins, outs, sems, False)
        for cp in owns + sends:
            cp.start()

    def finish(self, ins, outs, sems):
        owns, sends, recvs = self._copies(ins, outs, sems, True)
        for cp in recvs:
            cp.wait_recv()
        for cp in sends:
            cp.wait_send()
        for cp in owns:
            cp.wait()


class _GatherBoth:
    def __init__(self, shards):
        self.arrs, self.n, self.aliases = list(shards), len(shards), {}
        self.out_shape = [_sds((4, 2) + s.shape, s.dtype) for s in shards]
        dma = pltpu.SemaphoreType.DMA
        self.scratch = [dma((self.n, 7)), dma((self.n, 7)), dma((self.n,))]

    def _copy(self, a, j, src, slot, to, outs, sems):
        return pltpu.make_async_remote_copy(src_ref=src, dst_ref=outs[a].at[slot.chip, slot.c], send_sem=sems[0].at[a, j],
                                            recv_sem=sems[1].at[a, j], device_id=(to.x, to.y, to.c), device_id_type=MESH)

    def start(self, ins, outs, sems):
        me, sib = _Place(), _Place(1)
        for a in range(self.n):
            pltpu.make_async_copy(ins[a], outs[a].at[me.chip, me.c], sems[2].at[a]).start()
            self._copy(a, 0, ins[a], me, sib, outs, sems).start()
            for i, k in enumerate(ICI_PEERS):
                self._copy(a, 1 + i, ins[a], me, _Place(k), outs, sems).start()

    def finish(self, ins, outs, sems):
        me, sib = _Place(), _Place(1)
        passed = []
        for i, k in enumerate(ICI_PEERS):
            peer = _Place(k)
            for a in range(self.n):
                self._copy(a, 1 + i, ins[a], peer, peer, outs, sems).wait_recv()
                cp = self._copy(a, 4 + i, outs[a].at[peer.chip, peer.c], peer, sib, outs, sems)
                cp.start()
                passed.append(cp)
        for a in range(self.n):
            self._copy(a, 0, ins[a], sib, sib, outs, sems).wait_recv()
            for i, k in enumerate(ICI_PEERS):
                far = _Place(k | 1)
                self._copy(a, 4 + i, outs[a].at[far.chip, far.c], far, sib, outs, sems).wait_recv()
        for a in range(self.n):
            self._copy(a, 0, ins[a], me, sib, outs, sems).wait_send()
            for i, k in enumerate(ICI_PEERS):
                self._copy(a, 1 + i, ins[a], me, _Place(k), outs, sems).wait_send()
            pltpu.make_async_copy(ins[a], outs[a].at[me.chip, me.c], sems[2].at[a]).wait()
        for cp in passed:
            cp.wait_send()


def _run_comm(comm, name, after=()):
    n, n_after = comm.n, len(after)

    def body(*refs):
        ins, outs, sems = refs[:n], refs[n + n_after:2 * n + n_after], refs[2 * n + n_after:]
        comm.start(ins, outs, sems)
        comm.finish(ins, outs, sems)

    return list(pl.pallas_call(body, in_specs=[ANY] * (n + n_after), out_specs=[ANY] * n, out_shape=comm.out_shape,
                               scratch_shapes=comm.scratch, input_output_aliases=dict(comm.aliases), name=name)(*comm.arrs, *after))


def _gather_sibling(bufs):
    return _Comm(bufs, [_sds(b.shape, b.dtype) for b in bufs], SIBLING,
                 src=lambda a, i, me, p: i.at[:, me.c], dst=lambda a, o, s: o.at[:, s.c], aliases={a: a for a in range(len(bufs))})


def _scatter_sibling(parts):
    return _Comm(parts, [_sds((4,) + p.shape[2:], p.dtype) for p in parts], SIBLING,
                 src=lambda a, i, me, p: i.at[:, p.c], dst=lambda a, o, s: o)


HBM_SPEC = pl.BlockSpec(memory_space=pltpu.HBM)
SEM_SPEC = pl.BlockSpec(memory_space=pltpu.SEMAPHORE)
DATAFLOW = pltpu.SideEffectType.DATAFLOW_SIDE_EFFECTING


def _tiles_2d(R, C, max_rows=256):
    if R % max_rows == 0:
        return max_rows, C, R // max_rows, lambda i: (i, 0)
    if R <= 2 * max_rows or C % 256:
        return R, C, 1, lambda i: (0, 0)
    return R, 256, C // 256, lambda i: (0, i)


def _ici_copy(gather, a, srcs, lands, send_sems, recv_sems, i, me, peer, sender):
    src = lands[a].at[me.chip, me.c] if gather else srcs[a].at[peer.chip]
    dst = lands[a].at[sender.chip, sender.c] if gather else lands[a].at[sender.chip]
    j = a * len(ICI_PEERS) + i
    return pltpu.make_async_remote_copy(src_ref=src, dst_ref=dst, send_sem=send_sems.at[j], recv_sem=recv_sems.at[j],
                                        device_id=(peer.x, peer.y, peer.c), device_id_type=MESH)


def _ici_start(srcs, lands, gather, name):
    n, n_s = len(lands), len(srcs)
    bufs = list(srcs) + list(lands)

    def body(*refs):
        src_refs, land_refs = refs[:n_s], refs[n_s:n_s + n]
        send_sems, recv_sems = refs[n_s + n], refs[n_s + n + 1]
        token = refs[-1]
        me = _Place()
        for a in range(n):
            for i, k in enumerate(ICI_PEERS):
                _ici_copy(gather, a, src_refs, land_refs, send_sems, recv_sems, i, me, _Place(k), me).start()
        token[...] = jnp.zeros_like(token)

    dma = pltpu.SemaphoreType.DMA((n * len(ICI_PEERS),))
    outs = pl.pallas_call(
        body, name=name, out_shape=(dma, dma, *[pltpu.HBM(v.shape, v.dtype) for v in bufs], _sds((8, LANES), F32)),
        in_specs=(HBM_SPEC,) * len(bufs),
        out_specs=(SEM_SPEC, SEM_SPEC) + (HBM_SPEC,) * len(bufs) + (pl.BlockSpec(memory_space=pltpu.VMEM),),
        input_output_aliases={j: 2 + j for j in range(len(bufs))}, compiler_params=pltpu.CompilerParams(has_side_effects=DATAFLOW),
    )(*[pltpu.with_memory_space_constraint(v, pltpu.HBM) for v in bufs])
    return outs[0], outs[1], list(outs[2:2 + n_s]), list(outs[2 + n_s:2 + n_s + n]), outs[-1]


def _ici_wait(flight, after, gather, name):
    send_sems, recv_sems, srcs, lands, _ = flight
    n, n_s = len(lands), len(srcs)
    bufs = srcs + lands

    def body(*refs):
        src_refs, land_refs = refs[:n_s], refs[n_s:n_s + n]
        s_sems, r_sems = refs[n_s + n], refs[n_s + n + 1]
        me = _Place()
        for a in range(n):
            for i, k in enumerate(ICI_PEERS):
                peer = _Place(k)
                cp = _ici_copy(gather, a, src_refs, land_refs, s_sems, r_sems, i, me, peer, peer)
                cp.wait_send()
                cp.wait_recv()

    outs = pl.pallas_call(
        body, name=name, out_shape=tuple(pltpu.HBM(v.shape, v.dtype) for v in bufs),
        in_specs=(HBM_SPEC,) * len(bufs) + (SEM_SPEC, SEM_SPEC) + (ANY,) * len(after), out_specs=(HBM_SPEC,) * len(bufs),
        input_output_aliases={j: j for j in range(len(bufs))}, compiler_params=pltpu.CompilerParams(has_side_effects=DATAFLOW),
    )(*bufs, send_sems, recv_sems, *after)
    return list(outs[n_s:])


def _own_shards(shards, after, name):
    n = len(shards)
    vmem = pl.BlockSpec(memory_space=pltpu.VMEM)

    def body(*refs):
        ins, outs, cast, sems = refs[:n], refs[n + 1:2 * n + 1], refs[2 * n + 1:3 * n + 1], refs[3 * n + 1]
        me = _Place()
        copies = []
        for a in range(n):
            cast[a][...] = ins[a][...].astype(BF16)
            copies.append(pltpu.make_async_copy(cast[a], outs[a].at[me.chip, me.c], sems.at[a]))
            copies[-1].start()
        for cp in copies:
            cp.wait()

    return list(pl.pallas_call(
        body, in_specs=[vmem] * n + [ANY], out_specs=[ANY] * n, out_shape=[_sds((4, 2) + s.shape, BF16) for s in shards],
        scratch_shapes=[pltpu.VMEM(s.shape, BF16) for s in shards] + [pltpu.SemaphoreType.DMA((n,))], name=name)(*shards, after))


def _col_pieces(widths):
    out, c = [], 0
    for k, w in widths:
        out.append((k, c, w))
        c += w
    return out


def _split_range(c0, n, bounds):
    parts, c = [], c0
    while c < c0 + n:
        r = max(i for i in range(len(bounds) - 1) if bounds[i] <= c)
        w = min(c0 + n, bounds[r + 1]) - c
        parts.append((r, c - bounds[r], w))
        c += w
    return parts


def _win_unpack(g, widths, name):
    n, R, C = g.shape
    tr = min(256, R)
    pieces = _col_pieces(widths)
    padded = [-(-w // LANES) * LANES for _, _, w in pieces]
    shard_bounds = [s * C for s in range(n + 1)]

    def body(g_ref, *o_refs):
        for (k, c0, w), o_ref in zip(pieces, o_refs):
            for t in range(0, o_ref.shape[1], LANES):
                valid = max(0, min(LANES, w - t))
                cols = [g_ref[s, :, o:o + ww] for s, o, ww in _split_range(c0 + t, valid, shard_bounds)] if valid else []
                if valid < LANES:
                    cols.append(jnp.zeros((tr, LANES - valid), g_ref.dtype))
                o_ref[:, t:t + LANES] = cols[0] if len(cols) == 1 else jnp.concatenate(cols, axis=1)

    return pl.pallas_call(
        body, grid=(R // tr,), in_specs=[pl.BlockSpec((n, tr, C), lambda i: (0, i, 0))],
        out_specs=[pl.BlockSpec((tr, p), lambda i: (i, 0)) for p in padded],
        out_shape=[_sds((R, p), g.dtype) for p in padded], name=name, compiler_params=_params("parallel"))(g)


def _win_pack(grads, widths, n, name):
    R = grads[0].shape[0]
    tr = min(256, R)
    pieces = _col_pieces(widths)
    total = pieces[-1][1] + pieces[-1][2]
    C = total // n
    bounds = [c0 for _, c0, _ in pieces] + [total]

    def body(*refs):
        g_refs, o_ref = refs[:-1], refs[-1]

        def tile_t(c0):
            cols = [g_refs[r][:, o:o + ww] for r, o, ww in _split_range(c0, LANES, bounds)]
            tile = cols[0] if len(cols) == 1 else jnp.concatenate(cols, axis=1)
            return tile.astype(F32).T

        for s in range(n):
            full = C // LANES * LANES
            for t in range(0, full, LANES):
                o_ref[s, t:t + LANES, :] = tile_t(s * C + t).astype(o_ref.dtype)
            if full < C:
                o_ref[s, full:C, :] = tile_t(s * C + C - LANES)[LANES - (C - full):, :].astype(o_ref.dtype)

    return pl.pallas_call(
        body, grid=(R // tr,), in_specs=[pl.BlockSpec((tr, gr.shape[1]), lambda i: (i, 0)) for gr in grads],
        out_specs=pl.BlockSpec((n, C, tr), lambda i: (0, 0, i)), out_shape=_sds((n, C, R), grads[0].dtype),
        name=name, compiler_params=_params("parallel"))(*grads)


def _gather_all(arrs):
    return _Comm(arrs, [_sds((N_DEV,) + a.shape, a.dtype) for a in arrs], ALL_PEERS,
                 src=lambda a, i, me, p: i, dst=lambda a, o, s: o.at[s.id], own=lambda a, i, o, me: (i, o.at[me.id]))


def _add_halves(parts, got, name):
    n, _, R, C = parts.shape
    br, bc, nb, at = _tiles_2d(R, C, max_rows=1024)
    place = jnp.stack([lax.axis_index("c"), 2 * lax.axis_index("x") + lax.axis_index("y")]).astype(jnp.int32)

    def body(q_ref, p_ref, g_ref, o_ref, land_ref):
        s = (p_ref[0, 0].astype(F32) + g_ref[0].astype(F32)).astype(o_ref.dtype)
        o_ref[0] = s

        @pl.when(pl.program_id(1) == q_ref[1])
        def _():
            land_ref[0] = s

    spec = pltpu.PrefetchScalarGridSpec(
        num_scalar_prefetch=1, grid=(nb, n),
        in_specs=[pl.BlockSpec((1, 1, br, bc), lambda i, q, q_ref: (q, q_ref[0]) + at(i)), pl.BlockSpec((1, br, bc), lambda i, q, q_ref: (q,) + at(i))],
        out_specs=[pl.BlockSpec((1, br, bc), lambda i, q, q_ref: (q,) + at(i)), pl.BlockSpec((1, br, bc), lambda i, q, q_ref: (q_ref[1],) + at(i))])
    return pl.pallas_call(body, grid_spec=spec, out_shape=[_sds((n, R, C), parts.dtype)] * 2, name=name,
                          compiler_params=_params("parallel", "arbitrary"))(place, parts, got)


def _adam(w, m, v, gparts, name, comm=None):
    R, C = w.shape
    n = gparts.shape[0]
    br, bc, nb, at = _tiles_2d(R, C, max_rows=512)
    c1 = 1.0 / (1.0 - ADAM_B1 ** ADAM_STEP)
    c2 = 1.0 / (1.0 - ADAM_B2 ** ADAM_STEP)

    def body(w_ref, m_ref, v_ref, g_ref, go_ref, d_ref, mo_ref, vo_ref):
        g = g_ref[0].astype(F32)
        for s in range(1, n):
            g = g + g_ref[s].astype(F32)
        mn = ADAM_B1 * m_ref[...] + (1.0 - ADAM_B1) * g
        vn = ADAM_B2 * v_ref[...] + (1.0 - ADAM_B2) * (g * g)
        go_ref[...] = g
        mo_ref[...] = mn
        vo_ref[...] = vn
        d_ref[...] = -ADAM_LR * ((mn * c1) / (jnp.sqrt(vn * c2) + ADAM_EPS) + ADAM_WD * w_ref[...])

    blk = pl.BlockSpec((br, bc), at)
    outs, carried = _call(
        body, grid=(nb,), in_specs=[blk, blk, blk, pl.BlockSpec((n, br, bc), lambda i: (0,) + at(i))],
        out_specs=[blk] * 4, out_shape=[_sds((R, C), F32)] * 4, args=[w, m, v, gparts], name=name, sem=("parallel",), comm=comm)
    return outs if comm is None else (outs, carried)


_SMALL_ORDER = ("norm_mix", "b_gate", "sc_conv_w", "ssm_conv_w", "ssm_conv_b", "dt_bias", "A_log", "D_skip", "ssm_norm_w",
                "norm_mlp", "norm_final", "loss")
_REPLICATED = ("norm_mix", "b_gate", "ssm_conv_b", "dt_bias", "A_log", "D_skip", "ssm_norm_w", "norm_mlp", "norm_final")


def _cols_to_slots(g, n):
    R = g.shape[0]
    return jnp.transpose(g.reshape(R, n, g.shape[1] // n), (1, 0, 2))


def _slots_to_cols(g):
    n, R, C = g.shape
    return jnp.transpose(g, (1, 0, 2)).reshape(R, n * C)


def kernel(x, norm_mix, w_in, b_gate, sc_conv_w, ssm_conv_w, ssm_conv_b, dt_bias, A_log, D_skip, ssm_norm_w, w_branch_sc, w_branch_ssm, w_out, norm_mlp, w_mlp1, w_mlp2, norm_final, loss_target, m_norm_mix, m_w_in, m_b_gate, m_sc_conv_w, m_ssm_conv_w, m_ssm_conv_b, m_dt_bias, m_A_log, m_D_skip, m_ssm_norm_w, m_w_branch_sc, m_w_branch_ssm, m_w_out, m_norm_mlp, m_w_mlp1, m_w_mlp2, m_norm_final, v_norm_mix, v_w_in, v_b_gate, v_sc_conv_w, v_ssm_conv_w, v_ssm_conv_b, v_dt_bias, v_A_log, v_D_skip, v_ssm_norm_w, v_w_branch_sc, v_w_branch_ssm, v_w_out, v_norm_mlp, v_w_mlp1, v_w_mlp2, v_norm_final):
    T, D = x.shape[1], x.shape[2]
    n_inner = 2 * D
    n_heads = n_inner // HEADDIM
    n_xbc = n_inner + 2 * NGROUPS * NSTATE
    me = 4 * lax.axis_index("x") + 2 * lax.axis_index("y") + lax.axis_index("c")

    in_cols = [("sc", 3 * D), ("z", n_inner), ("xbc", n_xbc), ("dt", n_heads), ("gate", 2 * D)]
    by_owner = lambda b: b.reshape((N_DEV,) + b.shape[2:])
    to_owner = lambda g: g.reshape((4, 2) + g.shape[1:])
    rows_of = lambda g: to_owner(g.reshape((N_DEV, g.shape[0] // N_DEV) + g.shape[1:]))
    cols_of = lambda g: to_owner(_cols_to_slots(g, N_DEV))

    class Schedule(_NoExchange):
        late = ("bssm", "bsc", "out", "w1", "w2")
        gather_sib = dict(gnorm_fwd=("bsc", "bssm", "out"), branch_ssm=("w1", "w2"))
        scatter_sib = dict(mlp_up_dx=("w2", "w1"), branch_ssm_dx=("out", "bssm", "bsc"))
        shards = dict(bsc=w_branch_sc, bssm=w_branch_ssm, out=w_out, w1=w_mlp1, w2=w_mlp2)

        def __init__(self):
            self.W, self.staged, self.grads, self.summed, self.scatters = {}, {}, {}, {}, []
            self.token = jnp.zeros((), F32)

        def first_weights(self, bufs):
            self.W.update(zip([k for k, _ in in_cols], _win_unpack(by_owner(bufs[0]), in_cols, "win_unpack")))
            self.W.update(sc_conv_w=_slots_to_cols(by_owner(bufs[1])), ssm_conv_w=_slots_to_cols(by_owner(bufs[2])))
            lands = _own_shards([self.shards[k] for k in self.late], bufs[1], "own_shards")
            self.gather_flight = _ici_start([], lands, True, "gather_late_start")
            self.token = self.gather_flight[4][0, 0]
            self.W["dt"] = self.W["dt"] + self.token.astype(BF16)

        def tok(self):
            return self.token

        def point(self, name, values):
            if name == "mixers_done":
                lands = _ici_wait(self.gather_flight, values, True, "gather_late_wait")
                self.staged.update(zip(self.late, lands))

        def carry(self, name):
            if name == "rms_mix":
                return _GatherBoth([w_in.astype(BF16), sc_conv_w, ssm_conv_w])
            if name in self.gather_sib:
                return _gather_sibling([self.staged.pop(k) for k in self.gather_sib[name]])
            if name in self.scatter_sib:
                return _scatter_sibling([self.grads[k] for k in self.scatter_sib[name]])
            return None

        def start_scatter(self, keys, halves_and_lands):
            halves, lands = [h for h, _ in halves_and_lands], [l for _, l in halves_and_lands]
            flight = _ici_start(halves, lands, False, "scatter_%s_start" % keys[0])
            self.scatters.append((keys, flight))
            self.token = flight[4][0, 0]

        def carried(self, name, outs):
            if name == "rms_mix":
                self.first_weights(outs)
            elif name in self.gather_sib:
                for k, b in zip(self.gather_sib[name], outs):
                    full = by_owner(b)
                    self.W[k] = _slots_to_cols(full) if k == "w1" else full.reshape(-1, D)
            else:
                keys = self.scatter_sib[name]
                self.start_scatter(keys, [_add_halves(self.grads[k], b, "add_halves_" + k) for k, b in zip(keys, outs)])

        def grad(self, k, g):
            if k == "win":
                g = to_owner(_win_pack([g[k] for k, _ in in_cols], in_cols, N_DEV, "win_pack"))
                got = _run_comm(_scatter_sibling([g]), "scatter_sibling_win")[0]
                self.start_scatter(("win",), [_add_halves(g, got, "add_halves_win")])
            else:
                self.grads[k] = cols_of(g) if k == "w1" else rows_of(g)

        def finish_scatter(self, after):
            keys, flight = self.scatters.pop(0)
            return dict(zip(keys, _ici_wait(flight, after, False, "scatter_%s_wait" % keys[0])))

    S = Schedule()
    small = dict(norm_mix=norm_mix, b_gate=b_gate, ssm_conv_b=ssm_conv_b, dt_bias=dt_bias, A_log=A_log, D_skip=D_skip,
                 ssm_norm_w=ssm_norm_w, norm_mlp=norm_mlp, norm_final=norm_final)
    grad_x, g_small = _local_step(x.reshape(T, D), loss_target.reshape(T, D), S, small)

    small_flat = jnp.concatenate([g_small[k].reshape(-1) for k in _SMALL_ORDER])
    n_small = small_flat.shape[0]
    rows = -(-n_small // (8 * LANES)) * 8
    small_pack = jnp.pad(small_flat, (0, rows * LANES - n_small)).reshape(rows, LANES)

    res = {}
    big = [("w_in", "win", w_in, m_w_in, v_w_in), ("w_branch_sc", "bsc", w_branch_sc, m_w_branch_sc, v_w_branch_sc),
           ("w_branch_ssm", "bssm", w_branch_ssm, m_w_branch_ssm, v_w_branch_ssm), ("w_out", "out", w_out, m_w_out, v_w_out),
           ("w_mlp1", "w1", w_mlp1, m_w_mlp1, v_w_mlp1), ("w_mlp2", "w2", w_mlp2, m_w_mlp2, v_w_mlp2)]
    by_grad = {gk: (k, w, m, v) for k, gk, w, m, v in big}
    after = [grad_x]
    while S.scatters:
        for gk, parts in S.finish_scatter(after).items():
            k, w, m, v = by_grad[gk]
            if gk == "win":
                res_t, (small_parts,) = _adam(w.T, m.T, v.T, parts, "adam_" + k, comm=_gather_all([small_pack]))
                res[k] = [r.T for r in res_t]
            else:
                res[k] = _adam(w, m, v, parts, "adam_" + k)
            after = after + [res[k][1]]

    sizes = {k: g_small[k].size for k in _SMALL_ORDER}
    offs, o = {}, 0
    for k in _SMALL_ORDER:
        offs[k] = o
        o += sizes[k]
    rep_w = dict(norm_mix=norm_mix, b_gate=b_gate, ssm_conv_b=ssm_conv_b, dt_bias=dt_bias, A_log=A_log, D_skip=D_skip,
                 ssm_norm_w=ssm_norm_w, norm_mlp=norm_mlp, norm_final=norm_final)
    rep_m = dict(norm_mix=m_norm_mix, b_gate=m_b_gate, ssm_conv_b=m_ssm_conv_b, dt_bias=m_dt_bias, A_log=m_A_log, D_skip=m_D_skip,
                 ssm_norm_w=m_ssm_norm_w, norm_mlp=m_norm_mlp, norm_final=m_norm_final)
    rep_v = dict(norm_mix=v_norm_mix, b_gate=v_b_gate, ssm_conv_b=v_ssm_conv_b, dt_bias=v_dt_bias, A_log=v_A_log, D_skip=v_D_skip,
                 ssm_norm_w=v_ssm_norm_w, norm_mlp=v_norm_mlp, norm_final=v_norm_final)

    def pack(d):
        segs = [jnp.pad(d[k].astype(F32).reshape(-1), (0, sizes[k] - d[k].size)) if k in d else jnp.zeros((sizes[k],), F32)
                for k in _SMALL_ORDER]
        return jnp.pad(jnp.concatenate(segs), (0, rows * LANES - n_small)).reshape(rows, LANES)

    sm = _adam(pack(rep_w), pack(rep_m), pack(rep_v), small_parts, "adam_small")
    sm = [s.reshape(-1) for s in sm]
    for k in _REPLICATED:
        n_k = rep_w[k].shape[0]
        res[k] = tuple(s[offs[k]:offs[k] + n_k] for s in sm)
    loss = sm[0][offs["loss"]]
    for k, w, m, v, K, full in (("sc_conv_w", sc_conv_w, m_sc_conv_w, v_sc_conv_w, SC_K, D),
                                ("ssm_conv_w", ssm_conv_w, m_ssm_conv_w, v_ssm_conv_w, SSM_K, n_xbc)):
        g_full = sm[0][offs[k]:offs[k] + K * full].reshape(K, full)
        cw = full // N_DEV
        g_mine = lax.dynamic_slice_in_dim(g_full, me * cw, cw, axis=1)
        res[k] = _adam(w, m, v, g_mine[None], "adam_" + k)

    order = ("norm_mix", "w_in", "b_gate", "sc_conv_w", "ssm_conv_w", "ssm_conv_b", "dt_bias", "A_log", "D_skip", "ssm_norm_w",
             "w_branch_sc", "w_branch_ssm", "w_out", "norm_mlp", "w_mlp1", "w_mlp2", "norm_final")
    outs = [loss, grad_x.reshape(1, T, D)]
    for j in range(4):
        outs += [res[k][j] for k in order]
    return tuple(outs)
```

```python
import jax
import jax.numpy as jnp
from jax import lax
from jax.experimental import pallas as pl
from jax.experimental.pallas import tpu as pltpu

F32 = jnp.float32
BF16 = jnp.bfloat16

EPS = 1e-6
N_DEV = 8
HEADDIM = 64
NSTATE = 128
CHUNK = 128
NGROUPS = 8
GROUP_W = 256
SC_K = 3
SSM_K = 4
LANES = 128

ADAM_LR = 0.001
ADAM_B1 = 0.9
ADAM_B2 = 0.999
ADAM_EPS = 1e-08
ADAM_WD = 0.01
ADAM_STEP = 10

NN = (((1,), (0,)), ((), ()))
NT = (((1,), (1,)), ((), ()))
TN = (((0,), (0,)), ((), ()))
_DIMS = {"nn": NN, "nt": NT, "tn": TN}

ANY = pl.BlockSpec(memory_space=pl.ANY)
MESH = pl.DeviceIdType.MESH


def _sds(shape, dtype):
    return jax.ShapeDtypeStruct(tuple(shape), dtype)


def _dot(a, b, dims=NN):
    return lax.dot_general(a, b, dims, preferred_element_type=F32)


def _dot3(a, b, dims=NN):
    return lax.dot_general(a, b, dims, preferred_element_type=F32, precision=lax.Precision.HIGH)


def _params(*sem):
    return pltpu.CompilerParams(dimension_semantics=tuple(sem))


def _call(body, *, grid, in_specs, out_specs, out_shape, args, name, sem, scratch=(), comm=None):
    if comm is None:
        outs = pl.pallas_call(body, grid=grid, in_specs=list(in_specs), out_specs=list(out_specs), out_shape=list(out_shape),
                              scratch_shapes=list(scratch), name=name, compiler_params=_params(*sem))(*args)
        return list(outs), None
    n, n_in, n_out, n_scr = comm.n, len(in_specs), len(out_shape), len(scratch)

    def wrapped(*refs):
        ins, c_in = refs[:n_in], refs[n_in:n_in + n]
        outs, c_out = refs[n_in + n:n_in + n + n_out], refs[n_in + n + n_out:n_in + 2 * n + n_out]
        rest = refs[n_in + 2 * n + n_out:]
        scr, sems = rest[:n_scr], rest[n_scr:]
        first, last = None, None
        for d, g in enumerate(grid):
            f, l = pl.program_id(d) == 0, pl.program_id(d) == g - 1
            first, last = (f, l) if first is None else (first & f, last & l)

        @pl.when(first)
        def _():
            comm.start(c_in, c_out, sems)

        body(*ins, *outs, *scr)

        @pl.when(last)
        def _():
            comm.finish(c_in, c_out, sems)

    outs = pl.pallas_call(
        wrapped, grid=grid, in_specs=list(in_specs) + [ANY] * n, out_specs=list(out_specs) + [ANY] * n,
        out_shape=list(out_shape) + comm.out_shape, scratch_shapes=list(scratch) + comm.scratch,
        input_output_aliases={n_in + i: n_out + o for i, o in comm.aliases.items()},
        name=name, compiler_params=_params(*["arbitrary"] * len(grid)))(*args, *comm.arrs)
    return list(outs[:n_out]), list(outs[n_out:])


MM_VMEM_BUDGET = 44 * 2 ** 20


def _mm_tiles(M, N, k_bytes, mn_bytes):
    best = None
    for tm in (2048, 1024, 512, 256, 128):
        for tn in (1024, 512, 256, 128):
            if M % tm or N % tn:
                continue
            need = 2 * ((tm + tn) * k_bytes + tm * tn * mn_bytes) + 4 * tm * tn * 4
            if need <= MM_VMEM_BUDGET and (best is None or (tm * tn, tm) > (best[0] * best[1], best[0])):
                best = (tm, tn)
    assert best is not None, (M, N, k_bytes, mn_bytes)
    return best


def _mm(a, b, *, mode, name, extras=(), epi=None, out_dtypes=(F32,), comm=None):
    a_list = list(a) if isinstance(a, (list, tuple)) else [a]
    b_list = list(b) if isinstance(b, (list, tuple)) else [b]
    if mode == "nn":
        M, N = a_list[0].shape[0], b_list[0].shape[1]
    elif mode == "nt":
        M, N = a_list[0].shape[0], b_list[0].shape[0]
    else:
        M, N = a_list[0].shape[1], b_list[0].shape[1]
    k_bytes = sum((av.shape[0] if mode == "tn" else av.shape[1]) * av.dtype.itemsize for av in a_list)
    mn_bytes = sum(e.dtype.itemsize for e in extras) + sum(jnp.dtype(d).itemsize for d in out_dtypes)
    tm, tn = _mm_tiles(min(M, 2048), min(N, 1024), k_bytes, mn_bytes) if M % 128 == 0 and N % 128 == 0 else (M, N)
    assert M % tm == 0 and N % tn == 0
    a_specs, b_specs = [], []
    for av, bv in zip(a_list, b_list):
        K = av.shape[0] if mode == "tn" else av.shape[1]
        a_specs.append(pl.BlockSpec((K, tm), lambda i, j: (0, i)) if mode == "tn" else pl.BlockSpec((tm, K), lambda i, j: (i, 0)))
        b_specs.append(pl.BlockSpec((tn, K), lambda i, j: (j, 0)) if mode == "nt" else pl.BlockSpec((K, tn), lambda i, j: (0, j)))
    mn_spec = pl.BlockSpec((tm, tn), lambda i, j: (i, j))
    n_p, n_ex = len(a_list), len(extras)
    dims = _DIMS[mode]

    def body(*refs):
        acc = _dot(refs[0][...], refs[n_p][...], dims)
        for p in range(1, n_p):
            acc = acc + _dot(refs[p][...], refs[n_p + p][...], dims)
        rest = refs[2 * n_p:]
        res = (acc,) if epi is None else epi(acc, *[r[...] for r in rest[:n_ex]])
        for o_ref, r in zip(rest[n_ex:], res):
            o_ref[...] = r.astype(o_ref.dtype)

    outs, carried = _call(
        body, grid=(M // tm, N // tn), in_specs=a_specs + b_specs + [mn_spec] * n_ex,
        out_specs=[mn_spec] * len(out_dtypes), out_shape=[_sds((M, N), d) for d in out_dtypes],
        args=a_list + b_list + list(extras), name=name, sem=("parallel", "parallel"), comm=comm)
    res = outs[0] if len(outs) == 1 else outs
    return res if comm is None else (res, carried)


def _epi_add(acc, r):
    return (acc + r,)


def _epi_relu2(acc):
    p = jnp.maximum(acc, 0.0)
    return (p * p,)


def _epi_relu2_bwd(acc, r):
    return (acc * (2.0 * jnp.sqrt(r.astype(F32))),)


ROW_TILE = 512


def _row(tr, n):
    return pl.BlockSpec((tr, n), lambda i: (i, 0))


def _vec(n):
    return pl.BlockSpec((1, n), lambda i: (0, 0))


def _rms_fwd(x, w, name, comm=None):
    T, D = x.shape
    tr = min(ROW_TILE, T)

    def body(x_ref, w_ref, o_ref):
        xv = x_ref[...]
        r = lax.rsqrt(jnp.mean(xv * xv, axis=-1, keepdims=True) + EPS)
        o_ref[...] = (xv * r * w_ref[...]).astype(BF16)

    outs, carried = _call(body, grid=(T // tr,), in_specs=[_row(tr, D), _vec(D)], out_specs=[_row(tr, D)],
                          out_shape=[_sds((T, D), BF16)], args=[x, w], name=name, sem=("parallel",), comm=comm)
    return outs[0] if comm is None else (outs[0], carried)


def _rms_bwd(x, w, dh, dres, name):
    T, D = x.shape
    tr = min(ROW_TILE, T)

    def body(x_ref, w_ref, dh_ref, dres_ref, dx_ref, dxb_ref, dw_ref):
        @pl.when(pl.program_id(0) == 0)
        def _():
            dw_ref[...] = jnp.zeros_like(dw_ref)

        xv = x_ref[...]
        r = lax.rsqrt(jnp.mean(xv * xv, axis=-1, keepdims=True) + EPS)
        xh = xv * r
        dh_v = dh_ref[...]
        dw_ref[...] += jnp.sum(dh_v * xh, axis=0, keepdims=True)
        dxh = dh_v * w_ref[...]
        dx = r * (dxh - xh * jnp.mean(dxh * xh, axis=-1, keepdims=True)) + dres_ref[...]
        dx_ref[...] = dx
        dxb_ref[...] = dx.astype(BF16)

    return pl.pallas_call(
        body, grid=(T // tr,), in_specs=[_row(tr, D), _vec(D), _row(tr, D), _row(tr, D)],
        out_specs=[_row(tr, D), _row(tr, D), _vec(D)],
        out_shape=[_sds((T, D), F32), _sds((T, D), BF16), _sds((1, D), F32)],
        name=name, compiler_params=_params("arbitrary"))(x, w, dh, dres)


def _final(x2, w, tgt, name):
    T, D = x2.shape
    tr = min(ROW_TILE, T)

    def body(x_ref, w_ref, t_ref, dx_ref, dxb_ref, dw_ref, loss_ref):
        @pl.when(pl.program_id(0) == 0)
        def _():
            dw_ref[...] = jnp.zeros_like(dw_ref)
            loss_ref[...] = jnp.zeros_like(loss_ref)

        xv = x_ref[...]
        wv = w_ref[...]
        r = lax.rsqrt(jnp.mean(xv * xv, axis=-1, keepdims=True) + EPS)
        xh = xv * r
        err = xh * wv - t_ref[...]
        part = jnp.sum(jnp.sum(err * err, axis=1, keepdims=True), axis=0, keepdims=True) * (0.5 / D)
        loss_ref[...] += jnp.broadcast_to(part, loss_ref.shape)
        dy = err * (1.0 / D)
        dw_ref[...] += jnp.sum(dy * xh, axis=0, keepdims=True)
        dxh = dy * wv
        dx = r * (dxh - xh * jnp.mean(dxh * xh, axis=-1, keepdims=True))
        dx_ref[...] = dx
        dxb_ref[...] = dx.astype(BF16)

    return pl.pallas_call(
        body, grid=(T // tr,), in_specs=[_row(tr, D), _vec(D), _row(tr, D)],
        out_specs=[_row(tr, D), _row(tr, D), _vec(D), _vec(LANES)],
        out_shape=[_sds((T, D), F32), _sds((T, D), BF16), _sds((1, D), F32), _sds((1, LANES), F32)],
        name=name, compiler_params=_params("arbitrary"))(x2, w, tgt)


def _silu_parts(z):
    s = jax.nn.sigmoid(z)
    return z * s, s * (1.0 + z * (1.0 - s))


def _gnorm_fwd(y, z, w, name, comm=None):
    T, N = y.shape
    tr = min(ROW_TILE, T)

    def body(y_ref, z_ref, w_ref, o_ref):
        for g in range(N // GROUP_W):
            sl = slice(g * GROUP_W, (g + 1) * GROUP_W)
            silu, _ = _silu_parts(z_ref[:, sl])
            yz = y_ref[:, sl] * silu
            r = lax.rsqrt(jnp.mean(yz * yz, axis=-1, keepdims=True) + EPS)
            o_ref[:, sl] = (yz * r * w_ref[:, sl]).astype(BF16)

    outs, carried = _call(body, grid=(T // tr,), in_specs=[_row(tr, N), _row(tr, N), _vec(N)], out_specs=[_row(tr, N)],
                          out_shape=[_sds((T, N), BF16)], args=[y, z, w], name=name, sem=("parallel",), comm=comm)
    return outs[0] if comm is None else (outs[0], carried)


def _gnorm_bwd(y, z, w, dyb, name):
    T, N = y.shape
    tr = min(ROW_TILE, T)

    def body(y_ref, z_ref, w_ref, d_ref, dy_ref, dz_ref, dw_ref):
        @pl.when(pl.program_id(0) == 0)
        def _():
            dw_ref[...] = jnp.zeros_like(dw_ref)

        for g in range(N // GROUP_W):
            sl = slice(g * GROUP_W, (g + 1) * GROUP_W)
            yv = y_ref[:, sl]
            silu, dsilu = _silu_parts(z_ref[:, sl])
            yz = yv * silu
            r = lax.rsqrt(jnp.mean(yz * yz, axis=-1, keepdims=True) + EPS)
            yzh = yz * r
            d = d_ref[:, sl]
            dw_ref[:, sl] += jnp.sum(d * yzh, axis=0, keepdims=True)
            dyzh = d * w_ref[:, sl]
            dyz = r * (dyzh - yzh * jnp.mean(dyzh * yzh, axis=-1, keepdims=True))
            dy_ref[:, sl] = dyz * silu
            dz_ref[:, sl] = (dyz * yv * dsilu).astype(BF16)

    return pl.pallas_call(
        body, grid=(T // tr,), in_specs=[_row(tr, N), _row(tr, N), _vec(N), _row(tr, N)],
        out_specs=[_row(tr, N), _row(tr, N), _vec(N)],
        out_shape=[_sds((T, N), F32), _sds((T, N), BF16), _sds((1, N), F32)],
        name=name, compiler_params=_params("arbitrary"))(y, z, w, dyb)


def _merge_fwd(gate_raw, b_gate, br_a, br_b, name):
    T, D = br_a.shape
    tr = min(ROW_TILE, T)

    def body(g_ref, bg_ref, a_ref, b_ref, o_ref):
        g = jax.nn.sigmoid(g_ref[...] + bg_ref[...])
        o_ref[...] = (g[:, :D] * a_ref[...] + g[:, D:] * b_ref[...]).astype(BF16)

    return pl.pallas_call(body, grid=(T // tr,), in_specs=[_row(tr, 2 * D), _vec(2 * D), _row(tr, D), _row(tr, D)],
                          out_specs=_row(tr, D), out_shape=_sds((T, D), BF16), name=name,
                          compiler_params=_params("parallel"))(gate_raw, b_gate, br_a, br_b)


def _merge_bwd(dmerged, gate_raw, b_gate, br_a, br_b, name):
    T, D = br_a.shape
    tr = min(ROW_TILE, T)

    def body(d_ref, g_ref, bg_ref, a_ref, b_ref, da_ref, db_ref, dg_ref, dbg_ref):
        @pl.when(pl.program_id(0) == 0)
        def _():
            dbg_ref[...] = jnp.zeros_like(dbg_ref)

        g = jax.nn.sigmoid(g_ref[...] + bg_ref[...])
        d = d_ref[...]
        da_ref[...] = (d * g[:, :D]).astype(BF16)
        db_ref[...] = (d * g[:, D:]).astype(BF16)
        dg = jnp.concatenate([d * a_ref[...], d * b_ref[...]], axis=1) * g * (1.0 - g)
        dg_ref[...] = dg.astype(BF16)
        dbg_ref[...] += jnp.sum(dg, axis=0, keepdims=True)

    return pl.pallas_call(
        body, grid=(T // tr,), in_specs=[_row(tr, D), _row(tr, 2 * D), _vec(2 * D), _row(tr, D), _row(tr, D)],
        out_specs=[_row(tr, D), _row(tr, D), _row(tr, 2 * D), _vec(2 * D)],
        out_shape=[_sds((T, D), BF16), _sds((T, D), BF16), _sds((T, 2 * D), BF16), _sds((1, 2 * D), F32)],
        name=name, compiler_params=_params("arbitrary"))(dmerged, gate_raw, b_gate, br_a, br_b)


CB_W = 256
CONV_ROWS = 32
CONV_PAD = 8


def _rows_down(load, r0, s):
    if s == 0:
        return load(r0, r0 + CONV_ROWS)
    if r0 == 0:
        row = lax.broadcasted_iota(jnp.int32, (CONV_ROWS, CB_W), 0)
        return jnp.where(row >= s, pltpu.roll(load(0, CONV_ROWS), s, 0), 0.0)
    return load(r0 - s, r0 - s + CONV_ROWS)


def _conv_tile(load, taps, r0):
    K = len(taps)
    us = [_rows_down(load, r0, K - 1 - k) for k in range(K)]
    acc = us[K - 1] * taps[K - 1]
    for k in range(K - 1):
        acc = acc + us[k] * taps[k]
    return acc, us


def _conv_back_tile(scr, taps, r0):
    K = len(taps)
    du = scr[r0:r0 + CONV_ROWS, :] * taps[K - 1]
    for k in range(K - 1):
        s = K - 1 - k
        du = du + scr[r0 + s:r0 + s + CONV_ROWS, :] * taps[k]
    return du


def _fold8(v):
    return jnp.sum(v.reshape(CONV_ROWS // 8, 8, v.shape[1]), axis=0)


def _col(T, j0=0):
    return pl.BlockSpec((T, CB_W), lambda j: (0, j + j0))


def _sc_fwd(psc, w, name):
    T, D = psc.shape[0], psc.shape[1] // 3
    nb = D // CB_W

    def body(b_ref, c_ref, x_ref, w_ref, o_ref):
        taps = [w_ref[k:k + 1, :] for k in range(SC_K)]
        load = lambda a, b: c_ref[a:b, :] * x_ref[a:b, :]
        for r0 in range(0, T, CONV_ROWS):
            cu, _ = _conv_tile(load, taps, r0)
            o_ref[r0:r0 + CONV_ROWS, :] = (b_ref[r0:r0 + CONV_ROWS, :] * cu).astype(BF16)

    return pl.pallas_call(
        body, grid=(nb,), in_specs=[_col(T), _col(T, nb), _col(T, 2 * nb), pl.BlockSpec((SC_K, CB_W), lambda j: (0, j))],
        out_specs=_col(T), out_shape=_sds((T, D), BF16), name=name, compiler_params=_params("parallel"))(psc, psc, psc, w)


def _sc_bwd(psc, w, dya, name):
    T, D = psc.shape[0], psc.shape[1] // 3
    nb = D // CB_W

    def body(b_ref, c_ref, x_ref, w_ref, d_ref, db_ref, dc_ref, dx_ref, dw_ref, scr):
        taps = [w_ref[k:k + 1, :] for k in range(SC_K)]
        load = lambda a, b: c_ref[a:b, :] * x_ref[a:b, :]
        scr[T:T + CONV_PAD, :] = jnp.zeros((CONV_PAD, CB_W), F32)
        dw8 = [jnp.zeros((8, CB_W), F32)] * SC_K
        for r0 in range(0, T, CONV_ROWS):
            rows = slice(r0, r0 + CONV_ROWS)
            cu, us = _conv_tile(load, taps, r0)
            d = d_ref[rows, :]
            db_ref[rows, :] = (d * cu).astype(BF16)
            dcu = d * b_ref[rows, :]
            scr[rows, :] = dcu
            dw8 = [acc + _fold8(dcu * u) for acc, u in zip(dw8, us)]
        for k in range(SC_K):
            dw_ref[k:k + 1, :] = jnp.sum(dw8[k], axis=0, keepdims=True)
        for r0 in range(0, T, CONV_ROWS):
            rows = slice(r0, r0 + CONV_ROWS)
            du = _conv_back_tile(scr, taps, r0)
            dc_ref[rows, :] = (du * x_ref[rows, :]).astype(BF16)
            dx_ref[rows, :] = (du * c_ref[rows, :]).astype(BF16)

    wspec = pl.BlockSpec((SC_K, CB_W), lambda j: (0, j))
    return pl.pallas_call(
        body, grid=(nb,), in_specs=[_col(T), _col(T, nb), _col(T, 2 * nb), wspec, _col(T)],
        out_specs=[_col(T), _col(T), _col(T), wspec],
        out_shape=[_sds((T, D), BF16)] * 3 + [_sds((SC_K, D), F32)],
        scratch_shapes=[pltpu.VMEM((T + CONV_PAD, CB_W), F32)],
        name=name, compiler_params=_params("parallel"))(psc, psc, psc, w, dya)


def _ssm_conv_fwd(u, w, b, name, comm=None):
    T, N = u.shape

    def body(u_ref, w_ref, b_ref, o_ref):
        taps = [w_ref[k:k + 1, :] for k in range(SSM_K)]
        bias = b_ref[...]
        for r0 in range(0, T, CONV_ROWS):
            c, _ = _conv_tile(lambda a, b: u_ref[a:b, :], taps, r0)
            c = c + bias
            o_ref[r0:r0 + CONV_ROWS, :] = c * jax.nn.sigmoid(c)

    outs, carried = _call(
        body, grid=(N // CB_W,), in_specs=[_col(T), pl.BlockSpec((SSM_K, CB_W), lambda j: (0, j)), pl.BlockSpec((1, CB_W), lambda j: (0, j))],
        out_specs=[_col(T)], out_shape=[_sds((T, N), F32)], args=[u, w, b], name=name, sem=("parallel",), comm=comm)
    return outs[0] if comm is None else (outs[0], carried)


def _ssm_conv_bwd(u, w, b, dxs, dB, dC, name, comm=None):
    T, N = u.shape
    n_x, n_b = dxs.shape[1] // CB_W, dB.shape[1] // CB_W

    def body(u_ref, w_ref, b_ref, dx_ref, db_ref, dc_ref, du_ref, dw_ref, dbias_ref, scr):
        j = pl.program_id(0)
        taps = [w_ref[k:k + 1, :] for k in range(SSM_K)]
        bias = b_ref[...]
        scr[T:T + CONV_PAD, :] = jnp.zeros((CONV_PAD, CB_W), F32)
        dw8 = [jnp.zeros((8, CB_W), F32)] * SSM_K
        db8 = jnp.zeros((8, CB_W), F32)
        for r0 in range(0, T, CONV_ROWS):
            rows = slice(r0, r0 + CONV_ROWS)
            c, us = _conv_tile(lambda a, b: u_ref[a:b, :], taps, r0)
            _, dsilu = _silu_parts(c + bias)
            d = jnp.where(j < n_x, dx_ref[rows, :], jnp.where(j < n_x + n_b, db_ref[rows, :], dc_ref[rows, :])) * dsilu
            scr[rows, :] = d
            db8 = db8 + _fold8(d)
            dw8 = [acc + _fold8(d * u) for acc, u in zip(dw8, us)]
        dbias_ref[...] = jnp.sum(db8, axis=0, keepdims=True)
        for k in range(SSM_K):
            dw_ref[k:k + 1, :] = jnp.sum(dw8[k], axis=0, keepdims=True)
        for r0 in range(0, T, CONV_ROWS):
            du_ref[r0:r0 + CONV_ROWS, :] = _conv_back_tile(scr, taps, r0).astype(BF16)

    wspec = pl.BlockSpec((SSM_K, CB_W), lambda j: (0, j))
    bspec = pl.BlockSpec((1, CB_W), lambda j: (0, j))
    outs, carried = _call(
        body, grid=(N // CB_W,),
        in_specs=[_col(T), wspec, bspec,
                  pl.BlockSpec((T, CB_W), lambda j: (0, jnp.minimum(j, n_x - 1))),
                  pl.BlockSpec((T, CB_W), lambda j: (0, jnp.clip(j - n_x, 0, n_b - 1))),
                  pl.BlockSpec((T, CB_W), lambda j: (0, jnp.clip(j - n_x - n_b, 0, n_b - 1)))],
        out_specs=[_col(T), wspec, bspec],
        out_shape=[_sds((T, N), BF16), _sds((SSM_K, N), F32), _sds((1, N), F32)],
        scratch=[pltpu.VMEM((T + CONV_PAD, CB_W), F32)],
        args=[u, w, b, dxs, dB, dC], name=name, sem=("parallel",), comm=comm)
    return outs if comm is None else (outs, carried)


def _split3(v):
    hi = v.astype(BF16)
    r = v - hi.astype(F32)
    mid = r.astype(BF16)
    lo = (r - mid.astype(F32)).astype(BF16)
    return hi, mid, lo


def _head_expand(n_lanes):
    h = lax.broadcasted_iota(jnp.int32, (LANES, n_lanes), 0)
    l = lax.broadcasted_iota(jnp.int32, (LANES, n_lanes), 1)
    return (jnp.right_shift(l, HEADDIM.bit_length() - 1) == h).astype(BF16)


def _softplus(v):
    return jnp.maximum(v, 0.0) + jnp.log1p(jnp.exp(-jnp.abs(v)))


PREP_CHUNKS = 4


def _ssd_prep(dt_raw, dt_bias, a_log, n_inner, name):
    T = dt_raw.shape[0]
    rows = PREP_CHUNKS * CHUNK if T % (PREP_CHUNKS * CHUNK) == 0 else CHUNK

    def body(r_ref, b_ref, al_ref, ex_ref, dt_ref, cs_ref):
        i = lax.broadcasted_iota(jnp.int32, (CHUNK, CHUNK), 0)
        j = lax.broadcasted_iota(jnp.int32, (CHUNK, CHUNK), 1)
        tri = (j <= i).astype(BF16)
        ex = ex_ref[...]
        for r0 in range(0, rows, CHUNK):
            dt = _softplus(r_ref[r0:r0 + CHUNK, :] + b_ref[...])
            a = dt * (-jnp.exp(al_ref[...]))
            cs = sum(_dot(tri, p) for p in _split3(a))
            dt_ref[r0:r0 + CHUNK, :] = sum(_dot(p, ex) for p in _split3(dt))
            cs_ref[r0:r0 + CHUNK, :] = sum(_dot(p, ex) for p in _split3(cs))

    blk = pl.BlockSpec((rows, LANES), lambda c: (c, 0))
    out = pl.BlockSpec((rows, n_inner), lambda c: (c, 0))
    ex_spec = pl.BlockSpec((LANES, n_inner), lambda c: (0, 0))
    return pl.pallas_call(body, grid=(T // rows,), in_specs=[blk, _vec(LANES), _vec(LANES), ex_spec], out_specs=[out, out],
                          out_shape=[_sds((T, n_inner), F32)] * 2, name=name,
                          compiler_params=_params("parallel"))(dt_raw, dt_bias, a_log, _head_expand(n_inner))


def _pair_terms(cs_p):
    lane = lax.broadcasted_iota(jnp.int32, (CHUNK, CHUNK), 1)
    sub = lax.broadcasted_iota(jnp.int32, (CHUNK, CHUNK), 0)
    csT = cs_p.T
    Ls = []
    for k in range(2):
        col = jnp.sum(jnp.where(lane == k * HEADDIM, cs_p, 0.0), axis=1, keepdims=True)
        rowv = csT[k * HEADDIM:k * HEADDIM + 1, :]
        Ls.append(jnp.exp(jnp.where(sub >= lane, col - rowv, -jnp.inf)))
    return Ls, jnp.exp(csT[:, CHUNK - 1:CHUNK])


def _block_diag(xp):
    lane = lax.broadcasted_iota(jnp.int32, xp.shape, 1)
    return jnp.concatenate([jnp.where(lane < HEADDIM, xp, 0.0), jnp.where(lane >= HEADDIM, xp, 0.0)], axis=0)


SSD_GROUPS_PER_STEP = 8


def _ssd_specs(T, n_inner):
    nc, gs = T // CHUNK, SSD_GROUPS_PER_STEP
    bo, co = n_inner // (gs * NSTATE), (n_inner + NGROUPS * NSTATE) // (gs * NSTATE)
    assert NGROUPS % gs == 0 and n_inner % (gs * NSTATE) == 0 and (NGROUPS * NSTATE) % (gs * NSTATE) == 0
    g_blk = lambda f: pl.BlockSpec((CHUNK, gs * GROUP_W), lambda c, s: (f(c), s))
    b_blk = lambda f: pl.BlockSpec((CHUNK, gs * NSTATE), lambda c, s: (f(c), bo + s))
    c_blk = lambda f: pl.BlockSpec((CHUNK, gs * NSTATE), lambda c, s: (f(c), co + s))
    return nc, g_blk, b_blk, c_blk


def _ssd_fwd(xbc, dt_e, cs_e, d_e, name, comm=None):
    T = xbc.shape[0]
    n_inner = dt_e.shape[1]
    nc, g_blk, b_blk, c_blk = _ssd_specs(T, n_inner)
    ident = lambda c: c

    gs = SSD_GROUPS_PER_STEP

    def body(xs_ref, b_ref, c_ref, dt_ref, cs_ref, d_ref, y_ref, p_ref, st):
        c, s = pl.program_id(0), pl.program_id(1)

        @pl.when(c == 0)
        def _():
            for gi in range(gs):
                st[s * gs + gi] = jnp.zeros((GROUP_W, NSTATE), F32)

        for gi in range(gs):
            g = s * gs + gi
            gw, gn = slice(gi * GROUP_W, (gi + 1) * GROUP_W), slice(gi * NSTATE, (gi + 1) * NSTATE)
            P = st[g]
            p_ref[0, gi] = P
            xs, dt, cs = xs_ref[:, gw], dt_ref[:, gw], cs_ref[:, gw]
            Bf, Cf = b_ref[:, gn], c_ref[:, gn]
            Cb = Cf.astype(BF16)
            CBm = _dot(Cb, Bf.astype(BF16), NT)
            X = xs * dt
            decay = jnp.exp(cs[CHUNK - 1:CHUNK, :] - cs)
            y_off = _dot(Cb, P.astype(BF16), NT) * jnp.exp(cs)
            ys, ecl = [], []
            for pr in range(2):
                sl = slice(pr * LANES, (pr + 1) * LANES)
                Ls, e_last = _pair_terms(cs[:, sl])
                ecl.append(e_last)
                Mcat = jnp.concatenate([(CBm * L).astype(BF16) for L in Ls], axis=1)
                ys.append(_dot(Mcat, _block_diag(X[:, sl]).astype(BF16)))
            y_ref[:, gw] = jnp.concatenate(ys, axis=1) + y_off + xs * d_ref[:, gw]
            S = _dot3(X * decay, Bf, TN)
            st[g] = P * jnp.concatenate(ecl, axis=0) + S

    p_blk = pl.BlockSpec((1, gs, GROUP_W, NSTATE), lambda c, s: (c, s, 0, 0))
    outs, carried = _call(
        body, grid=(nc, NGROUPS // gs),
        in_specs=[g_blk(ident), b_blk(ident), c_blk(ident), g_blk(ident), g_blk(ident), pl.BlockSpec((1, gs * GROUP_W), lambda c, s: (0, s))],
        out_specs=[g_blk(ident), p_blk],
        out_shape=[_sds((T, n_inner), F32), _sds((nc, NGROUPS, GROUP_W, NSTATE), F32)],
        scratch=[pltpu.VMEM((NGROUPS, GROUP_W, NSTATE), F32)],
        args=[xbc, xbc, xbc, dt_e, cs_e, d_e], name=name, sem=("arbitrary", "arbitrary"), comm=comm)
    return outs if comm is None else (outs, carried)


def _ssd_bwd(xbc, dt_e, cs_e, d_e, states, dy, name, comm=None):
    T = xbc.shape[0]
    n_inner = dt_e.shape[1]
    nc, g_blk, b_blk, c_blk = _ssd_specs(T, n_inner)
    rev = lambda c: nc - 1 - c

    gs = SSD_GROUPS_PER_STEP

    def body(xs_ref, b_ref, c_ref, dt_ref, cs_ref, d_ref, p_ref, pn_ref, dy_ref,
             dxs_ref, db_ref, dc_ref, ddt_ref, dcs_ref, dd_ref, dst):
        cc, s = pl.program_id(0), pl.program_id(1)

        @pl.when(cc == 0)
        def _():
            for gi in range(gs):
                dst[s * gs + gi] = jnp.zeros((GROUP_W, NSTATE), F32)

        for gi in range(gs):
            one_group(s * gs + gi, gi, xs_ref, b_ref, c_ref, dt_ref, cs_ref, d_ref, p_ref, pn_ref, dy_ref,
                      dxs_ref, db_ref, dc_ref, ddt_ref, dcs_ref, dd_ref, dst)

    def one_group(g, gi, xs_ref, b_ref, c_ref, dt_ref, cs_ref, d_ref, p_ref, pn_ref, dy_ref,
                  dxs_ref, db_ref, dc_ref, ddt_ref, dcs_ref, dd_ref, dst):
        gw, gn = slice(gi * GROUP_W, (gi + 1) * GROUP_W), slice(gi * NSTATE, (gi + 1) * NSTATE)
        dS = dst[g]
        P, Pn = p_ref[0, gi], pn_ref[0, gi]
        xs, dt, cs, dY = xs_ref[:, gw], dt_ref[:, gw], cs_ref[:, gw], dy_ref[:, gw]
        Bf, Cf = b_ref[:, gn], c_ref[:, gn]
        Bb, Cb = Bf.astype(BF16), Cf.astype(BF16)
        X = xs * dt
        ecs = jnp.exp(cs)
        decay = jnp.exp(cs[CHUNK - 1:CHUNK, :] - cs)
        CBm = _dot3(Cf, Bf, NT)
        dYe = dY * ecs
        dP_off = _dot3(dYe, Cf, TN)
        dC = _dot(dYe.astype(BF16), P.astype(BF16))
        dcs = dYe * _dot3(Cf, P, NT)
        Xd = X * decay
        dB = _dot(Xd.astype(BF16), dS.astype(BF16))
        E = _dot3(Bf, dS, NT)
        dX = E * decay
        dcs = dcs - E * Xd
        R = _dot3(jnp.ones((8, NSTATE), F32), dS * Pn, NT)
        sub_g = lax.broadcasted_iota(jnp.int32, (CHUNK, GROUP_W), 0)
        dcs = dcs + jnp.where(sub_g == CHUNK - 1, R[0:1, :], 0.0)
        lane = lax.broadcasted_iota(jnp.int32, (CHUNK, CHUNK), 1)
        sub = lax.broadcasted_iota(jnp.int32, (CHUNK, CHUNK), 0)
        dCB = jnp.zeros((CHUNK, CHUNK), F32)
        dXs, dcss, ecl = [], [], []
        for pr in range(2):
            sl = slice(pr * LANES, (pr + 1) * LANES)
            Ls, e_last = _pair_terms(cs[:, sl])
            ecl.append(e_last)
            dYpb = dY[:, sl].astype(BF16)
            dMcat = _dot(dYpb, _block_diag(X[:, sl]).astype(BF16), NT)
            Mcat = jnp.concatenate([(CBm * L).astype(BF16) for L in Ls], axis=1)
            dXt = _dot(Mcat, dYpb, TN)
            dXs.append(jnp.where(lane < HEADDIM, dXt[:CHUNK], dXt[CHUNK:]))
            colacc = jnp.zeros((CHUNK, CHUNK), F32)
            rowacc = jnp.zeros((CHUNK, CHUNK), F32)
            for k in range(2):
                dG = dMcat[:, k * CHUNK:(k + 1) * CHUNK] * Ls[k]
                dCB = dCB + dG
                Q = dG * CBm
                colacc = colacc + jnp.where(lane == k * HEADDIM, jnp.sum(Q, axis=1, keepdims=True), 0.0)
                rowacc = rowacc + jnp.where(sub == k * HEADDIM, jnp.sum(Q, axis=0, keepdims=True), 0.0)
            dcss.append(colacc - rowacc.T)
        dX = dX + jnp.concatenate(dXs, axis=1)
        dcs = dcs + jnp.concatenate(dcss, axis=1)
        dCBb = dCB.astype(BF16)
        dc_ref[:, gn] = dC + _dot(dCBb, Bb)
        db_ref[:, gn] = dB + _dot(dCBb, Cb, TN)
        dxs_ref[:, gw] = dX * dt + dY * d_ref[:, gw]
        ddt_ref[:, gw] = dX * xs
        dcs_ref[:, gw] = dcs
        dd_ref[0, :, gw] = jnp.sum(dY * xs, axis=0, keepdims=True)
        dst[g] = dS * jnp.concatenate(ecl, axis=0) + dP_off

    p_blk = pl.BlockSpec((1, gs, GROUP_W, NSTATE), lambda c, s: (nc - 1 - c, s, 0, 0))
    pn_blk = pl.BlockSpec((1, gs, GROUP_W, NSTATE), lambda c, s: (jnp.minimum(nc - c, nc - 1), s, 0, 0))
    st_blk = pl.BlockSpec((CHUNK, gs * NSTATE), lambda c, s: (nc - 1 - c, s))
    outs, carried = _call(
        body, grid=(nc, NGROUPS // gs),
        in_specs=[g_blk(rev), b_blk(rev), c_blk(rev), g_blk(rev), g_blk(rev), pl.BlockSpec((1, gs * GROUP_W), lambda c, s: (0, s)),
                  p_blk, pn_blk, g_blk(rev)],
        out_specs=[g_blk(rev), st_blk, st_blk, g_blk(rev), g_blk(rev), pl.BlockSpec((1, 1, gs * GROUP_W), lambda c, s: (nc - 1 - c, 0, s))],
        out_shape=[_sds((T, n_inner), F32), _sds((T, NGROUPS * NSTATE), F32), _sds((T, NGROUPS * NSTATE), F32),
                   _sds((T, n_inner), F32), _sds((T, n_inner), F32), _sds((nc, 1, n_inner), F32)],
        scratch=[pltpu.VMEM((NGROUPS, GROUP_W, NSTATE), F32)],
        args=[xbc, xbc, xbc, dt_e, cs_e, d_e, states, states, dy], name=name, sem=("arbitrary", "arbitrary"), comm=comm)
    return outs if comm is None else (outs, carried)


def _ssd_post(ddt_e, dcs_e, dd_p, dt_raw, dt_bias, a_log, n_heads, name):
    T, n_inner = ddt_e.shape

    def body(ddt_ref, dcs_ref, dd_ref, r_ref, b_ref, al_ref, ex_ref, draw_ref, dbias_ref, dal_ref, ddsk_ref):
        @pl.when(pl.program_id(0) == 0)
        def _():
            dbias_ref[...] = jnp.zeros_like(dbias_ref)
            dal_ref[...] = jnp.zeros_like(dal_ref)
            ddsk_ref[...] = jnp.zeros_like(ddsk_ref)

        spread = [ddt_ref[...], dcs_ref[...], jnp.broadcast_to(dd_ref[0], (8, n_inner))]
        stacked = _dot(jnp.concatenate([p for v in spread for p in _split3(v)], axis=0), ex_ref[...], NT)
        sums, r0 = [], 0
        for v in spread:
            n = v.shape[0]
            sums.append(stacked[r0:r0 + n] + stacked[r0 + n:r0 + 2 * n] + stacked[r0 + 2 * n:r0 + 3 * n])
            r0 += 3 * n
        ddt_h, dcs_h, dd_h = sums
        raw = r_ref[...] + b_ref[...]
        dt = _softplus(raw)
        A = -jnp.exp(al_ref[...])
        i = lax.broadcasted_iota(jnp.int32, (CHUNK, CHUNK), 0)
        j = lax.broadcasted_iota(jnp.int32, (CHUNK, CHUNK), 1)
        upper = (j >= i).astype(BF16)
        da = sum(_dot(upper, p) for p in _split3(dcs_h))
        ddt = ddt_h + da * A
        lane = lax.broadcasted_iota(jnp.int32, (CHUNK, LANES), 1)
        draw = jnp.where(lane < n_heads, ddt * jax.nn.sigmoid(raw), 0.0)
        draw_ref[...] = draw.astype(BF16)
        dbias_ref[...] += jnp.sum(draw, axis=0, keepdims=True)
        dal_ref[...] += jnp.sum(da * dt, axis=0, keepdims=True) * A
        ddsk_ref[...] += dd_h[0:1, :]

    wide = pl.BlockSpec((CHUNK, n_inner), lambda c: (c, 0))
    blk = pl.BlockSpec((CHUNK, LANES), lambda c: (c, 0))
    return pl.pallas_call(
        body, grid=(T // CHUNK,),
        in_specs=[wide, wide, pl.BlockSpec((1, 1, n_inner), lambda c: (c, 0, 0)), blk, _vec(LANES), _vec(LANES),
                  pl.BlockSpec((LANES, n_inner), lambda c: (0, 0))],
        out_specs=[blk, _vec(LANES), _vec(LANES), _vec(LANES)],
        out_shape=[_sds((T, LANES), BF16)] + [_sds((1, LANES), F32)] * 3,
        name=name, compiler_params=_params("arbitrary"))(ddt_e, dcs_e, dd_p, dt_raw, dt_bias, a_log, _head_expand(n_inner))


def _row2(v):
    return v.reshape(1, -1).astype(F32)


def _pad_lanes(v):
    return jnp.pad(_row2(v), ((0, 0), (0, LANES - v.shape[-1])))


class _NoExchange:
    def __init__(self, W):
        self.W, self.grads = W, {}

    def weight(self, k):
        return self.W[k]

    def carry(self, name):
        return None

    def carried(self, name, outs):
        pass

    def grad(self, k, g):
        self.grads[k] = g

    def tok(self):
        return jnp.zeros((), F32)

    def point(self, name, value):
        pass


def _local_step(x, tgt, S, small):
    T, D = x.shape

    def mm(a, b, *, name, **kw):
        comm = S.carry(name)
        if comm is None:
            return _mm(a, b, name=name, **kw)
        res, outs = _mm(a, b, name=name, comm=comm, **kw)
        S.carried(name, outs)
        return res

    def carrying(fn, *args, name):
        comm = S.carry(name)
        if comm is None:
            return fn(*args, name)
        res, outs = fn(*args, name, comm=comm)
        S.carried(name, outs)
        return res

    n_inner = 2 * D
    n_heads = n_inner // HEADDIM
    norm_mix, norm_mlp, norm_final = _row2(small["norm_mix"]), _row2(small["norm_mlp"]), _row2(small["norm_final"])
    b_gate, ssm_b, ssm_norm_w = _row2(small["b_gate"]), _row2(small["ssm_conv_b"]), _row2(small["ssm_norm_w"])
    dt_bias, a_log = _pad_lanes(small["dt_bias"]), _pad_lanes(small["A_log"])
    d_e = jnp.repeat(small["D_skip"].astype(F32), HEADDIM).reshape(1, n_inner)

    hb = carrying(_rms_fwd, x, norm_mix, name="rms_mix")
    sc_w, ssm_w = S.weight("sc_conv_w"), S.weight("ssm_conv_w")
    p_xbc = mm(hb, S.weight("xbc"), mode="nn", name="proj_xbc")
    p_dt = mm(hb, S.weight("dt"), mode="nn", name="proj_dt")
    p_z = mm(hb, S.weight("z"), mode="nn", name="proj_z")
    p_sc = mm(hb, S.weight("sc"), mode="nn", name="proj_sc")
    p_gate = mm(hb, S.weight("gate"), mode="nn", name="proj_gate")
    xbc = carrying(_ssm_conv_fwd, p_xbc, ssm_w, ssm_b, name="ssm_conv_fwd")
    dt_e, cs_e = _ssd_prep(p_dt, dt_bias, a_log, n_inner, "ssd_prep")
    ya = _sc_fwd(p_sc, sc_w, "sc_fwd")
    y, states = carrying(_ssd_fwd, xbc, dt_e, cs_e, d_e, name="ssd_fwd")
    S.point("mixers_done", [y, ya, p_gate])
    yb = carrying(_gnorm_fwd, y, p_z, ssm_norm_w, name="gnorm_fwd")
    br_a = mm(ya, S.weight("bsc"), mode="nn", name="branch_sc")
    br_b = mm(yb, S.weight("bssm"), mode="nn", name="branch_ssm")
    merged = _merge_fwd(p_gate, b_gate, br_a, br_b, "merge_fwd")
    x1 = mm(merged, S.weight("out"), mode="nn", name="out_proj", extras=(x,), epi=_epi_add)
    h2 = _rms_fwd(x1, norm_mlp, "rms_mlp")
    r_act = mm(h2, S.weight("w1"), mode="nn", name="mlp_up", epi=_epi_relu2, out_dtypes=(BF16,))
    x2 = mm(r_act, S.weight("w2"), mode="nn", name="mlp_down", extras=(x1,), epi=_epi_add)
    dx2, dx2b, g_norm_final, loss_row = _final(x2, norm_final, tgt, "final")

    S.grad("w2", mm(r_act, dx2b, mode="tn", name="mlp_down_dw", out_dtypes=(BF16,)))
    da = mm(dx2b, S.weight("w2"), mode="nt", name="mlp_down_dx", extras=(r_act,), epi=_epi_relu2_bwd, out_dtypes=(BF16,))
    S.grad("w1", mm(h2, da, mode="tn", name="mlp_up_dw", out_dtypes=(BF16,)))
    dh2 = mm(da, S.weight("w1"), mode="nt", name="mlp_up_dx")
    dx1, dx1b, g_norm_mlp = _rms_bwd(x1, norm_mlp + S.tok(), dh2, dx2, "rms_mlp_bwd")
    S.grad("out", mm(merged, dx1b, mode="tn", name="out_proj_dw", out_dtypes=(BF16,)))
    dmerged = mm(dx1b, S.weight("out"), mode="nt", name="out_proj_dx")
    dbr_a, dbr_b, d_gate, g_b_gate = _merge_bwd(dmerged, p_gate, b_gate, br_a, br_b, "merge_bwd")
    S.grad("bssm", mm(yb, dbr_b, mode="tn", name="branch_ssm_dw", out_dtypes=(BF16,)))
    S.grad("bsc", mm(ya, dbr_a, mode="tn", name="branch_sc_dw", out_dtypes=(BF16,)))
    dyb = mm(dbr_b, S.weight("bssm"), mode="nt", name="branch_ssm_dx")
    dya = mm(dbr_a, S.weight("bsc"), mode="nt", name="branch_sc_dx")
    dy, d_z, g_ssm_norm_w = _gnorm_bwd(y, p_z, ssm_norm_w + S.tok(), dyb, "gnorm_bwd")
    dxs, dB, dC, ddt_e, dcs_e, dd_p = carrying(_ssd_bwd, xbc, dt_e, cs_e, d_e, states, dy, name="ssd_bwd")
    d_dt, g_dt_bias, g_a_log, g_d_skip = _ssd_post(ddt_e, dcs_e, dd_p, p_dt, dt_bias, a_log, n_heads, "ssd_post")
    d_xbc, g_ssm_w, g_ssm_b = carrying(_ssm_conv_bwd, p_xbc, ssm_w, ssm_b, dxs, dB, dC, name="ssm_conv_bwd")
    d_scB, d_scC, d_scX, g_sc_w = _sc_bwd(p_sc, sc_w, dya, "sc_bwd")
    d_sc = jnp.concatenate([d_scB, d_scC, d_scX], axis=1)
    pieces = [("sc", d_sc), ("z", d_z), ("xbc", d_xbc), ("dt", d_dt), ("gate", d_gate)]
    S.grad("win", {k: mm(hb, d, mode="tn", name="proj_dw_" + k, out_dtypes=(BF16,)) for k, d in pieces})
    pieces = [(k, d + S.tok().astype(d.dtype) if k == "dt" else d) for k, d in pieces]
    dh = mm([d for _, d in pieces], [S.weight(k) for k, _ in pieces], mode="nt", name="proj_dx")
    grad_x, _, g_norm_mix = _rms_bwd(x, norm_mix, dh, dx1, "rms_mix_bwd")

    g_small = dict(norm_mix=g_norm_mix, b_gate=g_b_gate, sc_conv_w=g_sc_w, ssm_conv_w=g_ssm_w, ssm_conv_b=g_ssm_b,
                   dt_bias=g_dt_bias, A_log=g_a_log, D_skip=g_d_skip, ssm_norm_w=g_ssm_norm_w, norm_mlp=g_norm_mlp,
                   norm_final=g_norm_final, loss=loss_row)
    return grad_x, g_small


class _Place:
    def __init__(self, k=0):
        x, y, c = lax.axis_index("x"), lax.axis_index("y"), lax.axis_index("c")
        self.x = 1 - x if k & 4 else x
        self.y = 1 - y if k & 2 else y
        self.c = 1 - c if k & 1 else c
        self.chip = 2 * self.x + self.y
        self.id = 2 * self.chip + self.c


ICI_PEERS = (2, 4, 6)
SIBLING = (1,)
ALL_PEERS = (1, 2, 3, 4, 5, 6, 7)


class _Comm:
    def __init__(self, arrs, out_shape, ks, src, dst, own=None, aliases=None):
        self.arrs, self.out_shape, self.ks = list(arrs), list(out_shape), tuple(ks)
        self.n = len(self.arrs)
        self.src, self.dst, self.own = src, dst, own
        self.aliases = aliases or {}
        dma = pltpu.SemaphoreType.DMA
        self.scratch = [dma((self.n, len(self.ks))), dma((self.n, len(self.ks))), dma((self.n,))]

    def _copies(self, ins, outs, sems, with_recvs):
        send_sems, recv_sems, local_sems = sems
        me = _Place()
        owns, sends, recvs = [], [], []
        for a in range(self.n):
            if self.own is not None:
                s, d = self.own(a, ins[a], outs[a], me)
                owns.append(pltpu.make_async_copy(s, d, local_sems.at[a]))
            for i, k in enumerate(self.ks):
                peer = _Place(k)
                for sender, lst in ((me, sends), (peer, recvs)) if with_recvs else ((me, sends),):
                    lst.append(pltpu.make_async_remote_copy(
                        src_ref=self.src(a, ins[a], me, peer), dst_ref=self.dst(a, outs[a], sender),
                        send_sem=send_sems.at[a, i], recv_sem=recv_sems.at[a, i],
                        device_id=(peer.x, peer.y, peer.c), device_id_type=MESH))
        return owns, sends, recvs

    def start(self, ins, outs, sems):
        owns, sends, _ = self._copies(ins, outs, sems, False)
        for cp in owns + sends:
            cp.start()

    def finish(self, ins, outs, sems):
        owns, sends, recvs = self._copies(ins, outs, sems, True)
        for cp in recvs:
            cp.wait_recv()
        for cp in sends:
            cp.wait_send()
        for cp in owns:
            cp.wait()


class _GatherBoth:
    def __init__(self, shards):
        self.arrs, self.n, self.aliases = list(shards), len(shards), {}
        self.out_shape = [_sds((4, 2) + s.shape, s.dtype) for s in shards]
        dma = pltpu.SemaphoreType.DMA
        self.scratch = [dma((self.n, 7)), dma((self.n, 7)), dma((self.n,))]

    def _copy(self, a, j, src, slot, to, outs, sems):
        return pltpu.make_async_remote_copy(src_ref=src, dst_ref=outs[a].at[slot.chip, slot.c], send_sem=sems[0].at[a, j],
                                            recv_sem=sems[1].at[a, j], device_id=(to.x, to.y, to.c), device_id_type=MESH)

    def start(self, ins, outs, sems):
        me, sib = _Place(), _Place(1)
        for a in range(self.n):
            pltpu.make_async_copy(ins[a], outs[a].at[me.chip, me.c], sems[2].at[a]).start()
            self._copy(a, 0, ins[a], me, sib, outs, sems).start()
            for i, k in enumerate(ICI_PEERS):
                self._copy(a, 1 + i, ins[a], me, _Place(k), outs, sems).start()

    def finish(self, ins, outs, sems):
        me, sib = _Place(), _Place(1)
        passed = []
        for i, k in enumerate(ICI_PEERS):
            peer = _Place(k)
            for a in range(self.n):
                self._copy(a, 1 + i, ins[a], peer, peer, outs, sems).wait_recv()
                cp = self._copy(a, 4 + i, outs[a].at[peer.chip, peer.c], peer, sib, outs, sems)
                cp.start()
                passed.append(cp)
        for a in range(self.n):
            self._copy(a, 0, ins[a], sib, sib, outs, sems).wait_recv()
            for i, k in enumerate(ICI_PEERS):
                far = _Place(k | 1)
                self._copy(a, 4 + i, outs[a].at[far.chip, far.c], far, sib, outs, sems).wait_recv()
        for a in range(self.n):
            self._copy(a, 0, ins[a], me, sib, outs, sems).wait_send()
            for i, k in enumerate(ICI_PEERS):
                self._copy(a, 1 + i, ins[a], me, _Place(k), outs, sems).wait_send()
            pltpu.make_async_copy(ins[a], outs[a].at[me.chip, me.c], sems[2].at[a]).wait()
        for cp in passed:
            cp.wait_send()


def _run_comm(comm, name, after=()):
    n, n_after = comm.n, len(after)

    def body(*refs):
        ins, outs, sems = refs[:n], refs[n + n_after:2 * n + n_after], refs[2 * n + n_after:]
        comm.start(ins, outs, sems)
        comm.finish(ins, outs, sems)

    return list(pl.pallas_call(body, in_specs=[ANY] * (n + n_after), out_specs=[ANY] * n, out_shape=comm.out_shape,
                               scratch_shapes=comm.scratch, input_output_aliases=dict(comm.aliases), name=name)(*comm.arrs, *after))


def _gather_sibling(bufs):
    return _Comm(bufs, [_sds(b.shape, b.dtype) for b in bufs], SIBLING,
                 src=lambda a, i, me, p: i.at[:, me.c], dst=lambda a, o, s: o.at[:, s.c], aliases={a: a for a in range(len(bufs))})


def _scatter_sibling(parts):
    return _Comm(parts, [_sds((4,) + p.shape[2:], p.dtype) for p in parts], SIBLING,
                 src=lambda a, i, me, p: i.at[:, p.c], dst=lambda a, o, s: o)


HBM_SPEC = pl.BlockSpec(memory_space=pltpu.HBM)
SEM_SPEC = pl.BlockSpec(memory_space=pltpu.SEMAPHORE)
DATAFLOW = pltpu.SideEffectType.DATAFLOW_SIDE_EFFECTING


def _tiles_2d(R, C, max_rows=256):
    if R % max_rows == 0:
        return max_rows, C, R // max_rows, lambda i: (i, 0)
    if R <= 2 * max_rows or C % 256:
        return R, C, 1, lambda i: (0, 0)
    return R, 256, C // 256, lambda i: (0, i)


def _ici_copy(gather, a, srcs, lands, send_sems, recv_sems, i, me, peer, sender):
    src = lands[a].at[me.chip, me.c] if gather else srcs[a].at[peer.chip]
    dst = lands[a].at[sender.chip, sender.c] if gather else lands[a].at[sender.chip]
    j = a * len(ICI_PEERS) + i
    return pltpu.make_async_remote_copy(src_ref=src, dst_ref=dst, send_sem=send_sems.at[j], recv_sem=recv_sems.at[j],
                                        device_id=(peer.x, peer.y, peer.c), device_id_type=MESH)


def _ici_start(srcs, lands, gather, name):
    n, n_s = len(lands), len(srcs)
    bufs = list(srcs) + list(lands)

    def body(*refs):
        src_refs, land_refs = refs[:n_s], refs[n_s:n_s + n]
        send_sems, recv_sems = refs[n_s + n], refs[n_s + n + 1]
        token = refs[-1]
        me = _Place()
        for a in range(n):
            for i, k in enumerate(ICI_PEERS):
                _ici_copy(gather, a, src_refs, land_refs, send_sems, recv_sems, i, me, _Place(k), me).start()
        token[...] = jnp.zeros_like(token)

    dma = pltpu.SemaphoreType.DMA((n * len(ICI_PEERS),))
    outs = pl.pallas_call(
        body, name=name, out_shape=(dma, dma, *[pltpu.HBM(v.shape, v.dtype) for v in bufs], _sds((8, LANES), F32)),
        in_specs=(HBM_SPEC,) * len(bufs),
        out_specs=(SEM_SPEC, SEM_SPEC) + (HBM_SPEC,) * len(bufs) + (pl.BlockSpec(memory_space=pltpu.VMEM),),
        input_output_aliases={j: 2 + j for j in range(len(bufs))}, compiler_params=pltpu.CompilerParams(has_side_effects=DATAFLOW),
    )(*[pltpu.with_memory_space_constraint(v, pltpu.HBM) for v in bufs])
    return outs[0], outs[1], list(outs[2:2 + n_s]), list(outs[2 + n_s:2 + n_s + n]), outs[-1]


def _ici_wait(flight, after, gather, name):
    send_sems, recv_sems, srcs, lands, _ = flight
    n, n_s = len(lands), len(srcs)
    bufs = srcs + lands

    def body(*refs):
        src_refs, land_refs = refs[:n_s], refs[n_s:n_s + n]
        s_sems, r_sems = refs[n_s + n], refs[n_s + n + 1]
        me = _Place()
        for a in range(n):
            for i, k in enumerate(ICI_PEERS):
                peer = _Place(k)
                cp = _ici_copy(gather, a, src_refs, land_refs, s_sems, r_sems, i, me, peer, peer)
                cp.wait_send()
                cp.wait_recv()

    outs = pl.pallas_call(
        body, name=name, out_shape=tuple(pltpu.HBM(v.shape, v.dtype) for v in bufs),
        in_specs=(HBM_SPEC,) * len(bufs) + (SEM_SPEC, SEM_SPEC) + (ANY,) * len(after), out_specs=(HBM_SPEC,) * len(bufs),
        input_output_aliases={j: j for j in range(len(bufs))}, compiler_params=pltpu.CompilerParams(has_side_effects=DATAFLOW),
    )(*bufs, send_sems, recv_sems, *after)
    return list(outs[n_s:])


def _own_shards(shards, after, name):
    n = len(shards)
    vmem = pl.BlockSpec(memory_space=pltpu.VMEM)

    def body(*refs):
        ins, outs, cast, sems = refs[:n], refs[n + 1:2 * n + 1], refs[2 * n + 1:3 * n + 1], refs[3 * n + 1]
        me = _Place()
        copies = []
        for a in range(n):
            cast[a][...] = ins[a][...].astype(BF16)
            copies.append(pltpu.make_async_copy(cast[a], outs[a].at[me.chip, me.c], sems.at[a]))
            copies[-1].start()
        for cp in copies:
            cp.wait()

    return list(pl.pallas_call(
        body, in_specs=[vmem] * n + [ANY], out_specs=[ANY] * n, out_shape=[_sds((4, 2) + s.shape, BF16) for s in shards],
        scratch_shapes=[pltpu.VMEM(s.shape, BF16) for s in shards] + [pltpu.SemaphoreType.DMA((n,))], name=name)(*shards, after))


def _col_pieces(widths):
    out, c = [], 0
    for k, w in widths:
        out.append((k, c, w))
        c += w
    return out


def _split_range(c0, n, bounds):
    parts, c = [], c0
    while c < c0 + n:
        r = max(i for i in range(len(bounds) - 1) if bounds[i] <= c)
        w = min(c0 + n, bounds[r + 1]) - c
        parts.append((r, c - bounds[r], w))
        c += w
    return parts


def _win_unpack(g, widths, name):
    n, R, C = g.shape
    tr = min(256, R)
    pieces = _col_pieces(widths)
    padded = [-(-w // LANES) * LANES for _, _, w in pieces]
    shard_bounds = [s * C for s in range(n + 1)]

    def body(g_ref, *o_refs):
        for (k, c0, w), o_ref in zip(pieces, o_refs):
            for t in range(0, o_ref.shape[1], LANES):
                valid = max(0, min(LANES, w - t))
                cols = [g_ref[s, :, o:o + ww] for s, o, ww in _split_range(c0 + t, valid, shard_bounds)] if valid else []
                if valid < LANES:
                    cols.append(jnp.zeros((tr, LANES - valid), g_ref.dtype))
                o_ref[:, t:t + LANES] = cols[0] if len(cols) == 1 else jnp.concatenate(cols, axis=1)

    return pl.pallas_call(
        body, grid=(R // tr,), in_specs=[pl.BlockSpec((n, tr, C), lambda i: (0, i, 0))],
        out_specs=[pl.BlockSpec((tr, p), lambda i: (i, 0)) for p in padded],
        out_shape=[_sds((R, p), g.dtype) for p in padded], name=name, compiler_params=_params("parallel"))(g)


def _win_pack(grads, widths, n, name):
    R = grads[0].shape[0]
    tr = min(256, R)
    pieces = _col_pieces(widths)
    total = pieces[-1][1] + pieces[-1][2]
    C = total // n
    bounds = [c0 for _, c0, _ in pieces] + [total]

    def body(*refs):
        g_refs, o_ref = refs[:-1], refs[-1]

        def tile_t(c0):
            cols = [g_refs[r][:, o:o + ww] for r, o, ww in _split_range(c0, LANES, bounds)]
            tile = cols[0] if len(cols) == 1 else jnp.concatenate(cols, axis=1)
            return tile.astype(F32).T

        for s in range(n):
            full = C // LANES * LANES
            for t in range(0, full, LANES):
                o_ref[s, t:t + LANES, :] = tile_t(s * C + t).astype(o_ref.dtype)
            if full < C:
                o_ref[s, full:C, :] = tile_t(s * C + C - LANES)[LANES - (C - full):, :].astype(o_ref.dtype)

    return pl.pallas_call(
        body, grid=(R // tr,), in_specs=[pl.BlockSpec((tr, gr.shape[1]), lambda i: (i, 0)) for gr in grads],
        out_specs=pl.BlockSpec((n, C, tr), lambda i: (0, 0, i)), out_shape=_sds((n, C, R), grads[0].dtype),
        name=name, compiler_params=_params("parallel"))(*grads)


def _gather_all(arrs):
    return _Comm(arrs, [_sds((N_DEV,) + a.shape, a.dtype) for a in arrs], ALL_PEERS,
                 src=lambda a, i, me, p: i, dst=lambda a, o, s: o.at[s.id], own=lambda a, i, o, me: (i, o.at[me.id]))


def _add_halves(parts, got, name):
    n, _, R, C = parts.shape
    br, bc, nb, at = _tiles_2d(R, C, max_rows=1024)
    place = jnp.stack([lax.axis_index("c"), 2 * lax.axis_index("x") + lax.axis_index("y")]).astype(jnp.int32)

    def body(q_ref, p_ref, g_ref, o_ref, land_ref):
        s = (p_ref[0, 0].astype(F32) + g_ref[0].astype(F32)).astype(o_ref.dtype)
        o_ref[0] = s

        @pl.when(pl.program_id(1) == q_ref[1])
        def _():
            land_ref[0] = s

    spec = pltpu.PrefetchScalarGridSpec(
        num_scalar_prefetch=1, grid=(nb, n),
        in_specs=[pl.BlockSpec((1, 1, br, bc), lambda i, q, q_ref: (q, q_ref[0]) + at(i)), pl.BlockSpec((1, br, bc), lambda i, q, q_ref: (q,) + at(i))],
        out_specs=[pl.BlockSpec((1, br, bc), lambda i, q, q_ref: (q,) + at(i)), pl.BlockSpec((1, br, bc), lambda i, q, q_ref: (q_ref[1],) + at(i))])
    return pl.pallas_call(body, grid_spec=spec, out_shape=[_sds((n, R, C), parts.dtype)] * 2, name=name,
                          compiler_params=_params("parallel", "arbitrary"))(place, parts, got)


def _adam(w, m, v, gparts, name, comm=None):
    R, C = w.shape
    n = gparts.shape[0]
    br, bc, nb, at = _tiles_2d(R, C, max_rows=512)
    c1 = 1.0 / (1.0 - ADAM_B1 ** ADAM_STEP)
    c2 = 1.0 / (1.0 - ADAM_B2 ** ADAM_STEP)

    def body(w_ref, m_ref, v_ref, g_ref, go_ref, d_ref, mo_ref, vo_ref):
        g = g_ref[0].astype(F32)
        for s in range(1, n):
            g = g + g_ref[s].astype(F32)
        mn = ADAM_B1 * m_ref[...] + (1.0 - ADAM_B1) * g
        vn = ADAM_B2 * v_ref[...] + (1.0 - ADAM_B2) * (g * g)
        go_ref[...] = g
        mo_ref[...] = mn
        vo_ref[...] = vn
        d_ref[...] = -ADAM_LR * ((mn * c1) / (jnp.sqrt(vn * c2) + ADAM_EPS) + ADAM_WD * w_ref[...])

    blk = pl.BlockSpec((br, bc), at)
    outs, carried = _call(
        body, grid=(nb,), in_specs=[blk, blk, blk, pl.BlockSpec((n, br, bc), lambda i: (0,) + at(i))],
        out_specs=[blk] * 4, out_shape=[_sds((R, C), F32)] * 4, args=[w, m, v, gparts], name=name, sem=("parallel",), comm=comm)
    return outs if comm is None else (outs, carried)


_SMALL_ORDER = ("norm_mix", "b_gate", "sc_conv_w", "ssm_conv_w", "ssm_conv_b", "dt_bias", "A_log", "D_skip", "ssm_norm_w",
                "norm_mlp", "norm_final", "loss")
_REPLICATED = ("norm_mix", "b_gate", "ssm_conv_b", "dt_bias", "A_log", "D_skip", "ssm_norm_w", "norm_mlp", "norm_final")


def _cols_to_slots(g, n):
    R = g.shape[0]
    return jnp.transpose(g.reshape(R, n, g.shape[1] // n), (1, 0, 2))


def _slots_to_cols(g):
    n, R, C = g.shape
    return jnp.transpose(g, (1, 0, 2)).reshape(R, n * C)


def kernel(x, norm_mix, w_in, b_gate, sc_conv_w, ssm_conv_w, ssm_conv_b, dt_bias, A_log, D_skip, ssm_norm_w, w_branch_sc, w_branch_ssm, w_out, norm_mlp, w_mlp1, w_mlp2, norm_final, loss_target, m_norm_mix, m_w_in, m_b_gate, m_sc_conv_w, m_ssm_conv_w, m_ssm_conv_b, m_dt_bias, m_A_log, m_D_skip, m_ssm_norm_w, m_w_branch_sc, m_w_branch_ssm, m_w_out, m_norm_mlp, m_w_mlp1, m_w_mlp2, m_norm_final, v_norm_mix, v_w_in, v_b_gate, v_sc_conv_w, v_ssm_conv_w, v_ssm_conv_b, v_dt_bias, v_A_log, v_D_skip, v_ssm_norm_w, v_w_branch_sc, v_w_branch_ssm, v_w_out, v_norm_mlp, v_w_mlp1, v_w_mlp2, v_norm_final):
    T, D = x.shape[1], x.shape[2]
    n_inner = 2 * D
    n_heads = n_inner // HEADDIM
    n_xbc = n_inner + 2 * NGROUPS * NSTATE
    me = 4 * lax.axis_index("x") + 2 * lax.axis_index("y") + lax.axis_index("c")

    in_cols = [("sc", 3 * D), ("z", n_inner), ("xbc", n_xbc), ("dt", n_heads), ("gate", 2 * D)]
    by_owner = lambda b: b.reshape((N_DEV,) + b.shape[2:])
    to_owner = lambda g: g.reshape((4, 2) + g.shape[1:])
    rows_of = lambda g: to_owner(g.reshape((N_DEV, g.shape[0] // N_DEV) + g.shape[1:]))
    cols_of = lambda g: to_owner(_cols_to_slots(g, N_DEV))

    class Schedule(_NoExchange):
        late = ("bssm", "bsc", "out", "w1", "w2")
        gather_sib = dict(gnorm_fwd=("bsc", "bssm", "out"), branch_ssm=("w1", "w2"))
        scatter_sib = dict(mlp_up_dx=("w2", "w1"), branch_ssm_dx=("out", "bssm", "bsc"))
        shards = dict(bsc=w_branch_sc, bssm=w_branch_ssm, out=w_out, w1=w_mlp1, w2=w_mlp2)

        def __init__(self):
            self.W, self.staged, self.grads, self.summed, self.scatters = {}, {}, {}, {}, []
            self.token = jnp.zeros((), F32)

        def first_weights(self, bufs):
            self.W.update(zip([k for k, _ in in_cols], _win_unpack(by_owner(bufs[0]), in_cols, "win_unpack")))
            self.W.update(sc_conv_w=_slots_to_cols(by_owner(bufs[1])), ssm_conv_w=_slots_to_cols(by_owner(bufs[2])))
            lands = _own_shards([self.shards[k] for k in self.late], bufs[1], "own_shards")
            self.gather_flight = _ici_start([], lands, True, "gather_late_start")
            self.token = self.gather_flight[4][0, 0]
            self.W["dt"] = self.W["dt"] + self.token.astype(BF16)

        def tok(self):
            return self.token

        def point(self, name, values):
            if name == "mixers_done":
                lands = _ici_wait(self.gather_flight, values, True, "gather_late_wait")
                self.staged.update(zip(self.late, lands))

        def carry(self, name):
            if name == "rms_mix":
                return _GatherBoth([w_in.astype(BF16), sc_conv_w, ssm_conv_w])
            if name in self.gather_sib:
                return _gather_sibling([self.staged.pop(k) for k in self.gather_sib[name]])
            if name in self.scatter_sib:
                return _scatter_sibling([self.grads[k] for k in self.scatter_sib[name]])
            return None

        def start_scatter(self, keys, halves_and_lands):
            halves, lands = [h for h, _ in halves_and_lands], [l for _, l in halves_and_lands]
            flight = _ici_start(halves, lands, False, "scatter_%s_start" % keys[0])
            self.scatters.append((keys, flight))
            self.token = flight[4][0, 0]

        def carried(self, name, outs):
            if name == "rms_mix":
                self.first_weights(outs)
            elif name in self.gather_sib:
                for k, b in zip(self.gather_sib[name], outs):
                    full = by_owner(b)
                    self.W[k] = _slots_to_cols(full) if k == "w1" else full.reshape(-1, D)
            else:
                keys = self.scatter_sib[name]
                self.start_scatter(keys, [_add_halves(self.grads[k], b, "add_halves_" + k) for k, b in zip(keys, outs)])

        def grad(self, k, g):
            if k == "win":
                g = to_owner(_win_pack([g[k] for k, _ in in_cols], in_cols, N_DEV, "win_pack"))
                got = _run_comm(_scatter_sibling([g]), "scatter_sibling_win")[0]
                self.start_scatter(("win",), [_add_halves(g, got, "add_halves_win")])
            else:
                self.grads[k] = cols_of(g) if k == "w1" else rows_of(g)

        def finish_scatter(self, after):
            keys, flight = self.scatters.pop(0)
            return dict(zip(keys, _ici_wait(flight, after, False, "scatter_%s_wait" % keys[0])))

    S = Schedule()
    small = dict(norm_mix=norm_mix, b_gate=b_gate, ssm_conv_b=ssm_conv_b, dt_bias=dt_bias, A_log=A_log, D_skip=D_skip,
                 ssm_norm_w=ssm_norm_w, norm_mlp=norm_mlp, norm_final=norm_final)
    grad_x, g_small = _local_step(x.reshape(T, D), loss_target.reshape(T, D), S, small)

    small_flat = jnp.concatenate([g_small[k].reshape(-1) for k in _SMALL_ORDER])
    n_small = small_flat.shape[0]
    rows = -(-n_small // (8 * LANES)) * 8
    small_pack = jnp.pad(small_flat, (0, rows * LANES - n_small)).reshape(rows, LANES)

    res = {}
    big = [("w_in", "win", w_in, m_w_in, v_w_in), ("w_branch_sc", "bsc", w_branch_sc, m_w_branch_sc, v_w_branch_sc),
           ("w_branch_ssm", "bssm", w_branch_ssm, m_w_branch_ssm, v_w_branch_ssm), ("w_out", "out", w_out, m_w_out, v_w_out),
           ("w_mlp1", "w1", w_mlp1, m_w_mlp1, v_w_mlp1), ("w_mlp2", "w2", w_mlp2, m_w_mlp2, v_w_mlp2)]
    by_grad = {gk: (k, w, m, v) for k, gk, w, m, v in big}
    after = [grad_x]
    while S.scatters:
        for gk, parts in S.finish_scatter(after).items():
            k, w, m, v = by_grad[gk]
            if gk == "win":
                res_t, (small_parts,) = _adam(w.T, m.T, v.T, parts, "adam_" + k, comm=_gather_all([small_pack]))
                res[k] = [r.T for r in res_t]
            else:
                res[k] = _adam(w, m, v, parts, "adam_" + k)
            after = after + [res[k][1]]

    sizes = {k: g_small[k].size for k in _SMALL_ORDER}
    offs, o = {}, 0
    for k in _SMALL_ORDER:
        offs[k] = o
        o += sizes[k]
    rep_w = dict(norm_mix=norm_mix, b_gate=b_gate, ssm_conv_b=ssm_conv_b, dt_bias=dt_bias, A_log=A_log, D_skip=D_skip,
                 ssm_norm_w=ssm_norm_w, norm_mlp=norm_mlp, norm_final=norm_final)
    rep_m = dict(norm_mix=m_norm_mix, b_gate=m_b_gate, ssm_conv_b=m_ssm_conv_b, dt_bias=m_dt_bias, A_log=m_A_log, D_skip=m_D_skip,
                 ssm_norm_w=m_ssm_norm_w, norm_mlp=m_norm_mlp, norm_final=m_norm_final)
    rep_v = dict(norm_mix=v_norm_mix, b_gate=v_b_gate, ssm_conv_b=v_ssm_conv_b, dt_bias=v_dt_bias, A_log=v_A_log, D_skip=v_D_skip,
                 ssm_norm_w=v_ssm_norm_w, norm_mlp=v_norm_mlp, norm_final=v_norm_final)

    def pack(d):
        segs = [jnp.pad(d[k].astype(F32).reshape(-1), (0, sizes[k] - d[k].size)) if k in d else jnp.zeros((sizes[k],), F32)
                for k in _SMALL_ORDER]
        return jnp.pad(jnp.concatenate(segs), (0, rows * LANES - n_small)).reshape(rows, LANES)

    sm = _adam(pack(rep_w), pack(rep_m), pack(rep_v), small_parts, "adam_small")
    sm = [s.reshape(-1) for s in sm]
    for k in _REPLICATED:
        n_k = rep_w[k].shape[0]
        res[k] = tuple(s[offs[k]:offs[k] + n_k] for s in sm)
    loss = sm[0][offs["loss"]]
    for k, w, m, v, K, full in (("sc_conv_w", sc_conv_w, m_sc_conv_w, v_sc_conv_w, SC_K, D),
                                ("ssm_conv_w", ssm_conv_w, m_ssm_conv_w, v_ssm_conv_w, SSM_K, n_xbc)):
        g_full = sm[0][offs[k]:offs[k] + K * full].reshape(K, full)
        cw = full // N_DEV
        g_mine = lax.dynamic_slice_in_dim(g_full, me * cw, cw, axis=1)
        res[k] = _adam(w, m, v, g_mine[None], "adam_" + k)

    order = ("norm_mix", "w_in", "b_gate", "sc_conv_w", "ssm_conv_w", "ssm_conv_b", "dt_bias", "A_log", "D_skip", "ssm_norm_w",
             "w_branch_sc", "w_branch_ssm", "w_out", "norm_mlp", "w_mlp1", "w_mlp2", "norm_final")
    outs = [loss, grad_x.reshape(1, T, D)]
    for j in range(4):
        outs += [res[k][j] for k in order]
    return tuple(outs)
```

```python
import jax
import jax.numpy as jnp
from jax import lax
from jax.experimental import pallas as pl
from jax.experimental.pallas import tpu as pltpu

F32 = jnp.float32
BF16 = jnp.bfloat16

EPS = 1e-6
N_DEV = 8
HEADDIM = 64
NSTATE = 128
CHUNK = 128
NGROUPS = 8
GROUP_W = 256
SC_K = 3
SSM_K = 4
LANES = 128

ADAM_LR = 0.001
ADAM_B1 = 0.9
ADAM_B2 = 0.999
ADAM_EPS = 1e-08
ADAM_WD = 0.01
ADAM_STEP = 10

NN = (((1,), (0,)), ((), ()))
NT = (((1,), (1,)), ((), ()))
TN = (((0,), (0,)), ((), ()))
_DIMS = {"nn": NN, "nt": NT, "tn": TN}

ANY = pl.BlockSpec(memory_space=pl.ANY)
MESH = pl.DeviceIdType.MESH


def _sds(shape, dtype):
    return jax.ShapeDtypeStruct(tuple(shape), dtype)


def _dot(a, b, dims=NN):
    return lax.dot_general(a, b, dims, preferred_element_type=F32)


def _dot3(a, b, dims=NN):
    return lax.dot_general(a, b, dims, preferred_element_type=F32, precision=lax.Precision.HIGH)


def _params(*sem):
    return pltpu.CompilerParams(dimension_semantics=tuple(sem))


def _call(body, *, grid, in_specs, out_specs, out_shape, args, name, sem, scratch=(), comm=None):
    if comm is None:
        outs = pl.pallas_call(body, grid=grid, in_specs=list(in_specs), out_specs=list(out_specs), out_shape=list(out_shape),
                              scratch_shapes=list(scratch), name=name, compiler_params=_params(*sem))(*args)
        return list(outs), None
    n, n_in, n_out, n_scr = comm.n, len(in_specs), len(out_shape), len(scratch)

    def wrapped(*refs):
        ins, c_in = refs[:n_in], refs[n_in:n_in + n]
        outs, c_out = refs[n_in + n:n_in + n + n_out], refs[n_in + n + n_out:n_in + 2 * n + n_out]
        rest = refs[n_in + 2 * n + n_out:]
        scr, sems = rest[:n_scr], rest[n_scr:]
        first, last = None, None
        for d, g in enumerate(grid):
            f, l = pl.program_id(d) == 0, pl.program_id(d) == g - 1
            first, last = (f, l) if first is None else (first & f, last & l)

        @pl.when(first)
        def _():
            comm.start(c_in, c_out, sems)

        body(*ins, *outs, *scr)

        @pl.when(last)
        def _():
            comm.finish(c_in, c_out, sems)

    outs = pl.pallas_call(
        wrapped, grid=grid, in_specs=list(in_specs) + [ANY] * n, out_specs=list(out_specs) + [ANY] * n,
        out_shape=list(out_shape) + comm.out_shape, scratch_shapes=list(scratch) + comm.scratch,
        input_output_aliases={n_in + i: n_out + o for i, o in comm.aliases.items()},
        name=name, compiler_params=_params(*["arbitrary"] * len(grid)))(*args, *comm.arrs)
    return list(outs[:n_out]), list(outs[n_out:])


MM_VMEM_BUDGET = 44 * 2 ** 20


def _mm_tiles(M, N, k_bytes, mn_bytes):
    best = None
    for tm in (2048, 1024, 512, 256, 128):
        for tn in (1024, 512, 256, 128):
            if M % tm or N % tn:
                continue
            need = 2 * ((tm + tn) * k_bytes + tm * tn * mn_bytes) + 4 * tm * tn * 4
            if need <= MM_VMEM_BUDGET and (best is None or (tm * tn, tm) > (best[0] * best[1], best[0])):
                best = (tm, tn)
    assert best is not None, (M, N, k_bytes, mn_bytes)
    return best


def _mm(a, b, *, mode, name, extras=(), epi=None, out_dtypes=(F32,), comm=None):
    a_list = list(a) if isinstance(a, (list, tuple)) else [a]
    b_list = list(b) if isinstance(b, (list, tuple)) else [b]
    if mode == "nn":
        M, N = a_list[0].shape[0], b_list[0].shape[1]
    elif mode == "nt":
        M, N = a_list[0].shape[0], b_list[0].shape[0]
    else:
        M, N = a_list[0].shape[1], b_list[0].shape[1]
    k_bytes = sum((av.shape[0] if mode == "tn" else av.shape[1]) * av.dtype.itemsize for av in a_list)
    mn_bytes = sum(e.dtype.itemsize for e in extras) + sum(jnp.dtype(d).itemsize for d in out_dtypes)
    tm, tn = _mm_tiles(min(M, 2048), min(N, 1024), k_bytes, mn_bytes) if M % 128 == 0 and N % 128 == 0 else (M, N)
    assert M % tm == 0 and N % tn == 0
    a_specs, b_specs = [], []
    for av, bv in zip(a_list, b_list):
        K = av.shape[0] if mode == "tn" else av.shape[1]
        a_specs.append(pl.BlockSpec((K, tm), lambda i, j: (0, i)) if mode == "tn" else pl.BlockSpec((tm, K), lambda i, j: (i, 0)))
        b_specs.append(pl.BlockSpec((tn, K), lambda i, j: (j, 0)) if mode == "nt" else pl.BlockSpec((K, tn), lambda i, j: (0, j)))
    mn_spec = pl.BlockSpec((tm, tn), lambda i, j: (i, j))
    n_p, n_ex = len(a_list), len(extras)
    dims = _DIMS[mode]

    def body(*refs):
        acc = _dot(refs[0][...], refs[n_p][...], dims)
        for p in range(1, n_p):
            acc = acc + _dot(refs[p][...], refs[n_p + p][...], dims)
        rest = refs[2 * n_p:]
        res = (acc,) if epi is None else epi(acc, *[r[...] for r in rest[:n_ex]])
        for o_ref, r in zip(rest[n_ex:], res):
            o_ref[...] = r.astype(o_ref.dtype)

    outs, carried = _call(
        body, grid=(M // tm, N // tn), in_specs=a_specs + b_specs + [mn_spec] * n_ex,
        out_specs=[mn_spec] * len(out_dtypes), out_shape=[_sds((M, N), d) for d in out_dtypes],
        args=a_list + b_list + list(extras), name=name, sem=("parallel", "parallel"), comm=comm)
    res = outs[0] if len(outs) == 1 else outs
    return res if comm is None else (res, carried)


def _epi_add(acc, r):
    return (acc + r,)


def _epi_relu2(acc):
    p = jnp.maximum(acc, 0.0)
    return (p * p,)


def _epi_relu2_bwd(acc, r):
    return (acc * (2.0 * jnp.sqrt(r.astype(F32))),)


ROW_TILE = 512


def _row(tr, n):
    return pl.BlockSpec((tr, n), lambda i: (i, 0))


def _vec(n):
    return pl.BlockSpec((1, n), lambda i: (0, 0))


def _rms_fwd(x, w, name, comm=None):
    T, D = x.shape
    tr = min(ROW_TILE, T)

    def body(x_ref, w_ref, o_ref):
        xv = x_ref[...]
        r = lax.rsqrt(jnp.mean(xv * xv, axis=-1, keepdims=True) + EPS)
        o_ref[...] = (xv * r * w_ref[...]).astype(BF16)

    outs, carried = _call(body, grid=(T // tr,), in_specs=[_row(tr, D), _vec(D)], out_specs=[_row(tr, D)],
                          out_shape=[_sds((T, D), BF16)], args=[x, w], name=name, sem=("parallel",), comm=comm)
    return outs[0] if comm is None else (outs[0], carried)


def _rms_bwd(x, w, dh, dres, name):
    T, D = x.shape
    tr = min(ROW_TILE, T)

    def body(x_ref, w_ref, dh_ref, dres_ref, dx_ref, dxb_ref, dw_ref):
        @pl.when(pl.program_id(0) == 0)
        def _():
            dw_ref[...] = jnp.zeros_like(dw_ref)

        xv = x_ref[...]
        r = lax.rsqrt(jnp.mean(xv * xv, axis=-1, keepdims=True) + EPS)
        xh = xv * r
        dh_v = dh_ref[...]
        dw_ref[...] += jnp.sum(dh_v * xh, axis=0, keepdims=True)
        dxh = dh_v * w_ref[...]
        dx = r * (dxh - xh * jnp.mean(dxh * xh, axis=-1, keepdims=True)) + dres_ref[...]
        dx_ref[...] = dx
        dxb_ref[...] = dx.astype(BF16)

    return pl.pallas_call(
        body, grid=(T // tr,), in_specs=[_row(tr, D), _vec(D), _row(tr, D), _row(tr, D)],
        out_specs=[_row(tr, D), _row(tr, D), _vec(D)],
        out_shape=[_sds((T, D), F32), _sds((T, D), BF16), _sds((1, D), F32)],
        name=name, compiler_params=_params("arbitrary"))(x, w, dh, dres)


def _final(x2, w, tgt, name):
    T, D = x2.shape
    tr = min(ROW_TILE, T)

    def body(x_ref, w_ref, t_ref, dx_ref, dxb_ref, dw_ref, loss_ref):
        @pl.when(pl.program_id(0) == 0)
        def _():
            dw_ref[...] = jnp.zeros_like(dw_ref)
            loss_ref[...] = jnp.zeros_like(loss_ref)

        xv = x_ref[...]
        wv = w_ref[...]
        r = lax.rsqrt(jnp.mean(xv * xv, axis=-1, keepdims=True) + EPS)
        xh = xv * r
        err = xh * wv - t_ref[...]
        part = jnp.sum(jnp.sum(err * err, axis=1, keepdims=True), axis=0, keepdims=True) * (0.5 / D)
        loss_ref[...] += jnp.broadcast_to(part, loss_ref.shape)
        dy = err * (1.0 / D)
        dw_ref[...] += jnp.sum(dy * xh, axis=0, keepdims=True)
        dxh = dy * wv
        dx = r * (dxh - xh * jnp.mean(dxh * xh, axis=-1, keepdims=True))
        dx_ref[...] = dx
        dxb_ref[...] = dx.astype(BF16)

    return pl.pallas_call(
        body, grid=(T // tr,), in_specs=[_row(tr, D), _vec(D), _row(tr, D)],
        out_specs=[_row(tr, D), _row(tr, D), _vec(D), _vec(LANES)],
        out_shape=[_sds((T, D), F32), _sds((T, D), BF16), _sds((1, D), F32), _sds((1, LANES), F32)],
        name=name, compiler_params=_params("arbitrary"))(x2, w, tgt)


def _silu_parts(z):
    s = jax.nn.sigmoid(z)
    return z * s, s * (1.0 + z * (1.0 - s))


def _gnorm_fwd(y, z, w, name, comm=None):
    T, N = y.shape
    tr = min(ROW_TILE, T)

    def body(y_ref, z_ref, w_ref, o_ref):
        for g in range(N // GROUP_W):
            sl = slice(g * GROUP_W, (g + 1) * GROUP_W)
            silu, _ = _silu_parts(z_ref[:, sl].astype(F32))
            yz = y_ref[:, sl] * silu
            r = lax.rsqrt(jnp.mean(yz * yz, axis=-1, keepdims=True) + EPS)
            o_ref[:, sl] = (yz * r * w_ref[:, sl]).astype(BF16)

    outs, carried = _call(body, grid=(T // tr,), in_specs=[_row(tr, N), _row(tr, N), _vec(N)], out_specs=[_row(tr, N)],
                          out_shape=[_sds((T, N), BF16)], args=[y, z, w], name=name, sem=("parallel",), comm=comm)
    return outs[0] if comm is None else (outs[0], carried)


def _gnorm_bwd(y, z, w, dyb, name):
    T, N = y.shape
    tr = min(ROW_TILE, T)

    def body(y_ref, z_ref, w_ref, d_ref, dy_ref, dz_ref, dw_ref):
        @pl.when(pl.program_id(0) == 0)
        def _():
            dw_ref[...] = jnp.zeros_like(dw_ref)

        for g in range(N // GROUP_W):
            sl = slice(g * GROUP_W, (g + 1) * GROUP_W)
            yv = y_ref[:, sl]
            silu, dsilu = _silu_parts(z_ref[:, sl].astype(F32))
            yz = yv * silu
            r = lax.rsqrt(jnp.mean(yz * yz, axis=-1, keepdims=True) + EPS)
            yzh = yz * r
            d = d_ref[:, sl].astype(F32)
            dw_ref[:, sl] += jnp.sum(d * yzh, axis=0, keepdims=True)
            dyzh = d * w_ref[:, sl]
            dyz = r * (dyzh - yzh * jnp.mean(dyzh * yzh, axis=-1, keepdims=True))
            dy_ref[:, sl] = dyz * silu
            dz_ref[:, sl] = (dyz * yv * dsilu).astype(BF16)

    return pl.pallas_call(
        body, grid=(T // tr,), in_specs=[_row(tr, N), _row(tr, N), _vec(N), _row(tr, N)],
        out_specs=[_row(tr, N), _row(tr, N), _vec(N)],
        out_shape=[_sds((T, N), F32), _sds((T, N), BF16), _sds((1, N), F32)],
        name=name, compiler_params=_params("arbitrary"))(y, z, w, dyb)


def _merge_fwd(gate_raw, b_gate, br_a, br_b, name):
    T, D = br_a.shape
    tr = min(ROW_TILE, T)

    def body(g_ref, bg_ref, a_ref, b_ref, o_ref):
        g = jax.nn.sigmoid(g_ref[...].astype(F32) + bg_ref[...])
        o_ref[...] = (g[:, :D] * a_ref[...] + g[:, D:] * b_ref[...]).astype(BF16)

    return pl.pallas_call(body, grid=(T // tr,), in_specs=[_row(tr, 2 * D), _vec(2 * D), _row(tr, D), _row(tr, D)],
                          out_specs=_row(tr, D), out_shape=_sds((T, D), BF16), name=name,
                          compiler_params=_params("parallel"))(gate_raw, b_gate, br_a, br_b)


def _merge_bwd(dmerged, gate_raw, b_gate, br_a, br_b, name):
    T, D = br_a.shape
    tr = min(ROW_TILE, T)

    def body(d_ref, g_ref, bg_ref, a_ref, b_ref, da_ref, db_ref, dg_ref, dbg_ref):
        @pl.when(pl.program_id(0) == 0)
        def _():
            dbg_ref[...] = jnp.zeros_like(dbg_ref)

        g = jax.nn.sigmoid(g_ref[...].astype(F32) + bg_ref[...])
        d = d_ref[...].astype(F32)
        da_ref[...] = (d * g[:, :D]).astype(BF16)
        db_ref[...] = (d * g[:, D:]).astype(BF16)
        dg = jnp.concatenate([d * a_ref[...], d * b_ref[...]], axis=1) * g * (1.0 - g)
        dg_ref[...] = dg.astype(BF16)
        dbg_ref[...] += jnp.sum(dg, axis=0, keepdims=True)

    return pl.pallas_call(
        body, grid=(T // tr,), in_specs=[_row(tr, D), _row(tr, 2 * D), _vec(2 * D), _row(tr, D), _row(tr, D)],
        out_specs=[_row(tr, D), _row(tr, D), _row(tr, 2 * D), _vec(2 * D)],
        out_shape=[_sds((T, D), BF16), _sds((T, D), BF16), _sds((T, 2 * D), BF16), _sds((1, 2 * D), F32)],
        name=name, compiler_params=_params("arbitrary"))(dmerged, gate_raw, b_gate, br_a, br_b)


CB_W = 256
CONV_ROWS = 32
CONV_PAD = 8


def _rows_down(load, r0, s):
    if s == 0:
        return load(r0, r0 + CONV_ROWS)
    if r0 == 0:
        row = lax.broadcasted_iota(jnp.int32, (CONV_ROWS, CB_W), 0)
        return jnp.where(row >= s, pltpu.roll(load(0, CONV_ROWS), s, 0), 0.0)
    return load(r0 - s, r0 - s + CONV_ROWS)


def _conv_tile(load, taps, r0):
    K = len(taps)
    us = [_rows_down(load, r0, K - 1 - k) for k in range(K)]
    acc = us[K - 1] * taps[K - 1]
    for k in range(K - 1):
        acc = acc + us[k] * taps[k]
    return acc, us


def _conv_back_tile(scr, taps, r0):
    K = len(taps)
    du = scr[r0:r0 + CONV_ROWS, :] * taps[K - 1]
    for k in range(K - 1):
        s = K - 1 - k
        du = du + scr[r0 + s:r0 + s + CONV_ROWS, :] * taps[k]
    return du


def _fold8(v):
    return jnp.sum(v.reshape(CONV_ROWS // 8, 8, v.shape[1]), axis=0)


def _col(T, j0=0):
    return pl.BlockSpec((T, CB_W), lambda j: (0, j + j0))


def _sc_fwd(psc, w, name):
    T, D = psc.shape[0], psc.shape[1] // 3
    nb = D // CB_W

    def body(b_ref, c_ref, x_ref, w_ref, o_ref):
        taps = [w_ref[k:k + 1, :] for k in range(SC_K)]
        load = lambda a, b: c_ref[a:b, :] * x_ref[a:b, :]
        for r0 in range(0, T, CONV_ROWS):
            cu, _ = _conv_tile(load, taps, r0)
            o_ref[r0:r0 + CONV_ROWS, :] = (b_ref[r0:r0 + CONV_ROWS, :] * cu).astype(BF16)

    return pl.pallas_call(
        body, grid=(nb,), in_specs=[_col(T), _col(T, nb), _col(T, 2 * nb), pl.BlockSpec((SC_K, CB_W), lambda j: (0, j))],
        out_specs=_col(T), out_shape=_sds((T, D), BF16), name=name, compiler_params=_params("parallel"))(psc, psc, psc, w)


def _sc_bwd(psc, w, dya, name):
    T, D = psc.shape[0], psc.shape[1] // 3
    nb = D // CB_W

    def body(b_ref, c_ref, x_ref, w_ref, d_ref, db_ref, dc_ref, dx_ref, dw_ref, scr):
        taps = [w_ref[k:k + 1, :] for k in range(SC_K)]
        load = lambda a, b: c_ref[a:b, :] * x_ref[a:b, :]
        scr[T:T + CONV_PAD, :] = jnp.zeros((CONV_PAD, CB_W), F32)
        dw8 = [jnp.zeros((8, CB_W), F32)] * SC_K
        for r0 in range(0, T, CONV_ROWS):
            rows = slice(r0, r0 + CONV_ROWS)
            cu, us = _conv_tile(load, taps, r0)
            d = d_ref[rows, :].astype(F32)
            db_ref[rows, :] = (d * cu).astype(BF16)
            dcu = d * b_ref[rows, :]
            scr[rows, :] = dcu
            dw8 = [acc + _fold8(dcu * u) for acc, u in zip(dw8, us)]
        for k in range(SC_K):
            dw_ref[k:k + 1, :] = jnp.sum(dw8[k], axis=0, keepdims=True)
        for r0 in range(0, T, CONV_ROWS):
            rows = slice(r0, r0 + CONV_ROWS)
            du = _conv_back_tile(scr, taps, r0)
            dc_ref[rows, :] = (du * x_ref[rows, :]).astype(BF16)
            dx_ref[rows, :] = (du * c_ref[rows, :]).astype(BF16)

    wspec = pl.BlockSpec((SC_K, CB_W), lambda j: (0, j))
    return pl.pallas_call(
        body, grid=(nb,), in_specs=[_col(T), _col(T, nb), _col(T, 2 * nb), wspec, _col(T)],
        out_specs=[_col(T), _col(T), _col(T), wspec],
        out_shape=[_sds((T, D), BF16)] * 3 + [_sds((SC_K, D), F32)],
        scratch_shapes=[pltpu.VMEM((T + CONV_PAD, CB_W), F32)],
        name=name, compiler_params=_params("parallel"))(psc, psc, psc, w, dya)


def _ssm_conv_fwd(u, w, b, name, comm=None):
    T, N = u.shape

    def body(u_ref, w_ref, b_ref, o_ref):
        taps = [w_ref[k:k + 1, :] for k in range(SSM_K)]
        bias = b_ref[...]
        for r0 in range(0, T, CONV_ROWS):
            c, _ = _conv_tile(lambda a, b: u_ref[a:b, :], taps, r0)
            c = c + bias
            o_ref[r0:r0 + CONV_ROWS, :] = c * jax.nn.sigmoid(c)

    outs, carried = _call(
        body, grid=(N // CB_W,), in_specs=[_col(T), pl.BlockSpec((SSM_K, CB_W), lambda j: (0, j)), pl.BlockSpec((1, CB_W), lambda j: (0, j))],
        out_specs=[_col(T)], out_shape=[_sds((T, N), F32)], args=[u, w, b], name=name, sem=("parallel",), comm=comm)
    return outs[0] if comm is None else (outs[0], carried)


def _ssm_conv_bwd(u, w, b, dxs, dB, dC, name, comm=None):
    T, N = u.shape
    n_x, n_b = dxs.shape[1] // CB_W, dB.shape[1] // CB_W

    def body(u_ref, w_ref, b_ref, dx_ref, db_ref, dc_ref, du_ref, dw_ref, dbias_ref, scr):
        j = pl.program_id(0)
        taps = [w_ref[k:k + 1, :] for k in range(SSM_K)]
        bias = b_ref[...]
        scr[T:T + CONV_PAD, :] = jnp.zeros((CONV_PAD, CB_W), F32)
        dw8 = [jnp.zeros((8, CB_W), F32)] * SSM_K
        db8 = jnp.zeros((8, CB_W), F32)
        for r0 in range(0, T, CONV_ROWS):
            rows = slice(r0, r0 + CONV_ROWS)
            c, us = _conv_tile(lambda a, b: u_ref[a:b, :], taps, r0)
            _, dsilu = _silu_parts(c + bias)
            d = jnp.where(j < n_x, dx_ref[rows, :], jnp.where(j < n_x + n_b, db_ref[rows, :], dc_ref[rows, :])) * dsilu
            scr[rows, :] = d
            db8 = db8 + _fold8(d)
            dw8 = [acc + _fold8(d * u) for acc, u in zip(dw8, us)]
        dbias_ref[...] = jnp.sum(db8, axis=0, keepdims=True)
        for k in range(SSM_K):
            dw_ref[k:k + 1, :] = jnp.sum(dw8[k], axis=0, keepdims=True)
        for r0 in range(0, T, CONV_ROWS):
            du_ref[r0:r0 + CONV_ROWS, :] = _conv_back_tile(scr, taps, r0).astype(BF16)

    wspec = pl.BlockSpec((SSM_K, CB_W), lambda j: (0, j))
    bspec = pl.BlockSpec((1, CB_W), lambda j: (0, j))
    outs, carried = _call(
        body, grid=(N // CB_W,),
        in_specs=[_col(T), wspec, bspec,
                  pl.BlockSpec((T, CB_W), lambda j: (0, jnp.minimum(j, n_x - 1))),
                  pl.BlockSpec((T, CB_W), lambda j: (0, jnp.clip(j - n_x, 0, n_b - 1))),
                  pl.BlockSpec((T, CB_W), lambda j: (0, jnp.clip(j - n_x - n_b, 0, n_b - 1)))],
        out_specs=[_col(T), wspec, bspec],
        out_shape=[_sds((T, N), BF16), _sds((SSM_K, N), F32), _sds((1, N), F32)],
        scratch=[pltpu.VMEM((T + CONV_PAD, CB_W), F32)],
        args=[u, w, b, dxs, dB, dC], name=name, sem=("parallel",), comm=comm)
    return outs if comm is None else (outs, carried)


def _split3(v):
    hi = v.astype(BF16)
    r = v - hi.astype(F32)
    mid = r.astype(BF16)
    lo = (r - mid.astype(F32)).astype(BF16)
    return hi, mid, lo


def _head_expand(n_lanes):
    h = lax.broadcasted_iota(jnp.int32, (LANES, n_lanes), 0)
    l = lax.broadcasted_iota(jnp.int32, (LANES, n_lanes), 1)
    return (jnp.right_shift(l, HEADDIM.bit_length() - 1) == h).astype(BF16)


def _softplus(v):
    return jnp.maximum(v, 0.0) + jnp.log1p(jnp.exp(-jnp.abs(v)))


PREP_CHUNKS = 4


def _ssd_prep(dt_raw, dt_bias, a_log, n_inner, name):
    T = dt_raw.shape[0]
    rows = PREP_CHUNKS * CHUNK if T % (PREP_CHUNKS * CHUNK) == 0 else CHUNK

    def body(r_ref, b_ref, al_ref, ex_ref, dt_ref, cs_ref):
        i = lax.broadcasted_iota(jnp.int32, (CHUNK, CHUNK), 0)
        j = lax.broadcasted_iota(jnp.int32, (CHUNK, CHUNK), 1)
        tri = (j <= i).astype(BF16)
        ex = ex_ref[...]
        for r0 in range(0, rows, CHUNK):
            dt = _softplus(r_ref[r0:r0 + CHUNK, :] + b_ref[...])
            a = dt * (-jnp.exp(al_ref[...]))
            cs = sum(_dot(tri, p) for p in _split3(a))
            dt_ref[r0:r0 + CHUNK, :] = sum(_dot(p, ex) for p in _split3(dt))
            cs_ref[r0:r0 + CHUNK, :] = sum(_dot(p, ex) for p in _split3(cs))

    blk = pl.BlockSpec((rows, LANES), lambda c: (c, 0))
    out = pl.BlockSpec((rows, n_inner), lambda c: (c, 0))
    ex_spec = pl.BlockSpec((LANES, n_inner), lambda c: (0, 0))
    return pl.pallas_call(body, grid=(T // rows,), in_specs=[blk, _vec(LANES), _vec(LANES), ex_spec], out_specs=[out, out],
                          out_shape=[_sds((T, n_inner), F32)] * 2, name=name,
                          compiler_params=_params("parallel"))(dt_raw, dt_bias, a_log, _head_expand(n_inner))


def _pair_terms(cs_p):
    lane = lax.broadcasted_iota(jnp.int32, (CHUNK, CHUNK), 1)
    sub = lax.broadcasted_iota(jnp.int32, (CHUNK, CHUNK), 0)
    csT = cs_p.T
    Ls = []
    for k in range(2):
        col = jnp.sum(jnp.where(lane == k * HEADDIM, cs_p, 0.0), axis=1, keepdims=True)
        rowv = csT[k * HEADDIM:k * HEADDIM + 1, :]
        Ls.append(jnp.exp(jnp.where(sub >= lane, col - rowv, -jnp.inf)))
    return Ls, jnp.exp(csT[:, CHUNK - 1:CHUNK])


def _block_diag(xp):
    lane = lax.broadcasted_iota(jnp.int32, xp.shape, 1)
    return jnp.concatenate([jnp.where(lane < HEADDIM, xp, 0.0), jnp.where(lane >= HEADDIM, xp, 0.0)], axis=0)


SSD_GROUPS_PER_STEP = 8


def _ssd_specs(T, n_inner):
    nc, gs = T // CHUNK, SSD_GROUPS_PER_STEP
    bo, co = n_inner // (gs * NSTATE), (n_inner + NGROUPS * NSTATE) // (gs * NSTATE)
    assert NGROUPS % gs == 0 and n_inner % (gs * NSTATE) == 0 and (NGROUPS * NSTATE) % (gs * NSTATE) == 0
    g_blk = lambda f: pl.BlockSpec((CHUNK, gs * GROUP_W), lambda c, s: (f(c), s))
    b_blk = lambda f: pl.BlockSpec((CHUNK, gs * NSTATE), lambda c, s: (f(c), bo + s))
    c_blk = lambda f: pl.BlockSpec((CHUNK, gs * NSTATE), lambda c, s: (f(c), co + s))
    return nc, g_blk, b_blk, c_blk


def _ssd_fwd(xbc, dt_e, cs_e, d_e, name, comm=None):
    T = xbc.shape[0]
    n_inner = dt_e.shape[1]
    nc, g_blk, b_blk, c_blk = _ssd_specs(T, n_inner)
    ident = lambda c: c

    gs = SSD_GROUPS_PER_STEP

    def body(xs_ref, b_ref, c_ref, dt_ref, cs_ref, d_ref, y_ref, p_ref, st):
        c, s = pl.program_id(0), pl.program_id(1)

        @pl.when(c == 0)
        def _():
            for gi in range(gs):
                st[s * gs + gi] = jnp.zeros((GROUP_W, NSTATE), F32)

        for gi in range(gs):
            g = s * gs + gi
            gw, gn = slice(gi * GROUP_W, (gi + 1) * GROUP_W), slice(gi * NSTATE, (gi + 1) * NSTATE)
            P = st[g]
            p_ref[0, gi] = P
            xs, dt, cs = xs_ref[:, gw], dt_ref[:, gw], cs_ref[:, gw]
            Bf, Cf = b_ref[:, gn], c_ref[:, gn]
            Cb = Cf.astype(BF16)
            CBm = _dot(Cb, Bf.astype(BF16), NT)
            X = xs * dt
            decay = jnp.exp(cs[CHUNK - 1:CHUNK, :] - cs)
            y_off = _dot(Cb, P.astype(BF16), NT) * jnp.exp(cs)
            ys, ecl = [], []
            for pr in range(2):
                sl = slice(pr * LANES, (pr + 1) * LANES)
                Ls, e_last = _pair_terms(cs[:, sl])
                ecl.append(e_last)
                Mcat = jnp.concatenate([(CBm * L).astype(BF16) for L in Ls], axis=1)
                ys.append(_dot(Mcat, _block_diag(X[:, sl]).astype(BF16)))
            y_ref[:, gw] = jnp.concatenate(ys, axis=1) + y_off + xs * d_ref[:, gw]
            S = _dot3(X * decay, Bf, TN)
            st[g] = P * jnp.concatenate(ecl, axis=0) + S

    p_blk = pl.BlockSpec((1, gs, GROUP_W, NSTATE), lambda c, s: (c, s, 0, 0))
    outs, carried = _call(
        body, grid=(nc, NGROUPS // gs),
        in_specs=[g_blk(ident), b_blk(ident), c_blk(ident), g_blk(ident), g_blk(ident), pl.BlockSpec((1, gs * GROUP_W), lambda c, s: (0, s))],
        out_specs=[g_blk(ident), p_blk],
        out_shape=[_sds((T, n_inner), F32), _sds((nc, NGROUPS, GROUP_W, NSTATE), F32)],
        scratch=[pltpu.VMEM((NGROUPS, GROUP_W, NSTATE), F32)],
        args=[xbc, xbc, xbc, dt_e, cs_e, d_e], name=name, sem=("arbitrary", "arbitrary"), comm=comm)
    return outs if comm is None else (outs, carried)


def _ssd_bwd(xbc, dt_e, cs_e, d_e, states, dy, name, comm=None):
    T = xbc.shape[0]
    n_inner = dt_e.shape[1]
    nc, g_blk, b_blk, c_blk = _ssd_specs(T, n_inner)
    rev = lambda c: nc - 1 - c

    gs = SSD_GROUPS_PER_STEP

    def body(xs_ref, b_ref, c_ref, dt_ref, cs_ref, d_ref, p_ref, pn_ref, dy_ref,
             dxs_ref, db_ref, dc_ref, ddt_ref, dcs_ref, dd_ref, dst):
        cc, s = pl.program_id(0), pl.program_id(1)

        @pl.when(cc == 0)
        def _():
            for gi in range(gs):
                dst[s * gs + gi] = jnp.zeros((GROUP_W, NSTATE), F32)

        for gi in range(gs):
            one_group(s * gs + gi, gi, xs_ref, b_ref, c_ref, dt_ref, cs_ref, d_ref, p_ref, pn_ref, dy_ref,
                      dxs_ref, db_ref, dc_ref, ddt_ref, dcs_ref, dd_ref, dst)

    def one_group(g, gi, xs_ref, b_ref, c_ref, dt_ref, cs_ref, d_ref, p_ref, pn_ref, dy_ref,
                  dxs_ref, db_ref, dc_ref, ddt_ref, dcs_ref, dd_ref, dst):
        gw, gn = slice(gi * GROUP_W, (gi + 1) * GROUP_W), slice(gi * NSTATE, (gi + 1) * NSTATE)
        dS = dst[g]
        P, Pn = p_ref[0, gi], pn_ref[0, gi]
        xs, dt, cs, dY = xs_ref[:, gw], dt_ref[:, gw], cs_ref[:, gw], dy_ref[:, gw]
        Bf, Cf = b_ref[:, gn], c_ref[:, gn]
        Bb, Cb = Bf.astype(BF16), Cf.astype(BF16)
        X = xs * dt
        ecs = jnp.exp(cs)
        decay = jnp.exp(cs[CHUNK - 1:CHUNK, :] - cs)
        CBm = _dot3(Cf, Bf, NT)
        dYe = dY * ecs
        dP_off = _dot3(dYe, Cf, TN)
        dC = _dot(dYe.astype(BF16), P.astype(BF16))
        dcs = dYe * _dot3(Cf, P, NT)
        Xd = X * decay
        dB = _dot(Xd.astype(BF16), dS.astype(BF16))
        E = _dot3(Bf, dS, NT)
        dX = E * decay
        dcs = dcs - E * Xd
        R = _dot3(jnp.ones((8, NSTATE), F32), dS * Pn, NT)
        sub_g = lax.broadcasted_iota(jnp.int32, (CHUNK, GROUP_W), 0)
        dcs = dcs + jnp.where(sub_g == CHUNK - 1, R[0:1, :], 0.0)
        lane = lax.broadcasted_iota(jnp.int32, (CHUNK, CHUNK), 1)
        sub = lax.broadcasted_iota(jnp.int32, (CHUNK, CHUNK), 0)
        dCB = jnp.zeros((CHUNK, CHUNK), F32)
        dXs, dcss, ecl = [], [], []
        for pr in range(2):
            sl = slice(pr * LANES, (pr + 1) * LANES)
            Ls, e_last = _pair_terms(cs[:, sl])
            ecl.append(e_last)
            dYpb = dY[:, sl].astype(BF16)
            dMcat = _dot(dYpb, _block_diag(X[:, sl]).astype(BF16), NT)
            Mcat = jnp.concatenate([(CBm * L).astype(BF16) for L in Ls], axis=1)
            dXt = _dot(Mcat, dYpb, TN)
            dXs.append(jnp.where(lane < HEADDIM, dXt[:CHUNK], dXt[CHUNK:]))
            colacc = jnp.zeros((CHUNK, CHUNK), F32)
            rowacc = jnp.zeros((CHUNK, CHUNK), F32)
            for k in range(2):
                dG = dMcat[:, k * CHUNK:(k + 1) * CHUNK] * Ls[k]
                dCB = dCB + dG
                Q = dG * CBm
                colacc = colacc + jnp.where(lane == k * HEADDIM, jnp.sum(Q, axis=1, keepdims=True), 0.0)
                rowacc = rowacc + jnp.where(sub == k * HEADDIM, jnp.sum(Q, axis=0, keepdims=True), 0.0)
            dcss.append(colacc - rowacc.T)
        dX = dX + jnp.concatenate(dXs, axis=1)
        dcs = dcs + jnp.concatenate(dcss, axis=1)
        dCBb = dCB.astype(BF16)
        dc_ref[:, gn] = dC + _dot(dCBb, Bb)
        db_ref[:, gn] = dB + _dot(dCBb, Cb, TN)
        dxs_ref[:, gw] = dX * dt + dY * d_ref[:, gw]
        ddt_ref[:, gw] = dX * xs
        dcs_ref[:, gw] = dcs
        dd_ref[0, :, gw] = jnp.sum(dY * xs, axis=0, keepdims=True)
        dst[g] = dS * jnp.concatenate(ecl, axis=0) + dP_off

    p_blk = pl.BlockSpec((1, gs, GROUP_W, NSTATE), lambda c, s: (nc - 1 - c, s, 0, 0))
    pn_blk = pl.BlockSpec((1, gs, GROUP_W, NSTATE), lambda c, s: (jnp.minimum(nc - c, nc - 1), s, 0, 0))
    st_blk = pl.BlockSpec((CHUNK, gs * NSTATE), lambda c, s: (nc - 1 - c, s))
    outs, carried = _call(
        body, grid=(nc, NGROUPS // gs),
        in_specs=[g_blk(rev), b_blk(rev), c_blk(rev), g_blk(rev), g_blk(rev), pl.BlockSpec((1, gs * GROUP_W), lambda c, s: (0, s)),
                  p_blk, pn_blk, g_blk(rev)],
        out_specs=[g_blk(rev), st_blk, st_blk, g_blk(rev), g_blk(rev), pl.BlockSpec((1, 1, gs * GROUP_W), lambda c, s: (nc - 1 - c, 0, s))],
        out_shape=[_sds((T, n_inner), F32), _sds((T, NGROUPS * NSTATE), F32), _sds((T, NGROUPS * NSTATE), F32),
                   _sds((T, n_inner), F32), _sds((T, n_inner), F32), _sds((nc, 1, n_inner), F32)],
        scratch=[pltpu.VMEM((NGROUPS, GROUP_W, NSTATE), F32)],
        args=[xbc, xbc, xbc, dt_e, cs_e, d_e, states, states, dy], name=name, sem=("arbitrary", "arbitrary"), comm=comm)
    return outs if comm is None else (outs, carried)


def _ssd_post(ddt_e, dcs_e, dd_p, dt_raw, dt_bias, a_log, n_heads, name):
    T, n_inner = ddt_e.shape

    def body(ddt_ref, dcs_ref, dd_ref, r_ref, b_ref, al_ref, ex_ref, draw_ref, dbias_ref, dal_ref, ddsk_ref):
        @pl.when(pl.program_id(0) == 0)
        def _():
            dbias_ref[...] = jnp.zeros_like(dbias_ref)
            dal_ref[...] = jnp.zeros_like(dal_ref)
            ddsk_ref[...] = jnp.zeros_like(ddsk_ref)

        spread = [ddt_ref[...], dcs_ref[...], jnp.broadcast_to(dd_ref[0], (8, n_inner))]
        stacked = _dot(jnp.concatenate([p for v in spread for p in _split3(v)], axis=0), ex_ref[...], NT)
        sums, r0 = [], 0
        for v in spread:
            n = v.shape[0]
            sums.append(stacked[r0:r0 + n] + stacked[r0 + n:r0 + 2 * n] + stacked[r0 + 2 * n:r0 + 3 * n])
            r0 += 3 * n
        ddt_h, dcs_h, dd_h = sums
        raw = r_ref[...] + b_ref[...]
        dt = _softplus(raw)
        A = -jnp.exp(al_ref[...])
        i = lax.broadcasted_iota(jnp.int32, (CHUNK, CHUNK), 0)
        j = lax.broadcasted_iota(jnp.int32, (CHUNK, CHUNK), 1)
        upper = (j >= i).astype(BF16)
        da = sum(_dot(upper, p) for p in _split3(dcs_h))
        ddt = ddt_h + da * A
        lane = lax.broadcasted_iota(jnp.int32, (CHUNK, LANES), 1)
        draw = jnp.where(lane < n_heads, ddt * jax.nn.sigmoid(raw), 0.0)
        draw_ref[...] = draw.astype(BF16)
        dbias_ref[...] += jnp.sum(draw, axis=0, keepdims=True)
        dal_ref[...] += jnp.sum(da * dt, axis=0, keepdims=True) * A
        ddsk_ref[...] += dd_h[0:1, :]

    wide = pl.BlockSpec((CHUNK, n_inner), lambda c: (c, 0))
    blk = pl.BlockSpec((CHUNK, LANES), lambda c: (c, 0))
    return pl.pallas_call(
        body, grid=(T // CHUNK,),
        in_specs=[wide, wide, pl.BlockSpec((1, 1, n_inner), lambda c: (c, 0, 0)), blk, _vec(LANES), _vec(LANES),
                  pl.BlockSpec((LANES, n_inner), lambda c: (0, 0))],
        out_specs=[blk, _vec(LANES), _vec(LANES), _vec(LANES)],
        out_shape=[_sds((T, LANES), BF16)] + [_sds((1, LANES), F32)] * 3,
        name=name, compiler_params=_params("arbitrary"))(ddt_e, dcs_e, dd_p, dt_raw, dt_bias, a_log, _head_expand(n_inner))


def _row2(v):
    return v.reshape(1, -1).astype(F32)


def _pad_lanes(v):
    return jnp.pad(_row2(v), ((0, 0), (0, LANES - v.shape[-1])))


class _NoExchange:
    def __init__(self, W):
        self.W, self.grads = W, {}

    def weight(self, k):
        return self.W[k]

    def carry(self, name):
        return None

    def carried(self, name, outs):
        pass

    def grad(self, k, g):
        self.grads[k] = g

    def tok(self):
        return jnp.zeros((), F32)

    def point(self, name, value):
        pass


def _local_step(x, tgt, S, small):
    T, D = x.shape

    def mm(a, b, *, name, **kw):
        comm = S.carry(name)
        if comm is None:
            return _mm(a, b, name=name, **kw)
        res, outs = _mm(a, b, name=name, comm=comm, **kw)
        S.carried(name, outs)
        return res

    def carrying(fn, *args, name):
        comm = S.carry(name)
        if comm is None:
            return fn(*args, name)
        res, outs = fn(*args, name, comm=comm)
        S.carried(name, outs)
        return res

    n_inner = 2 * D
    n_heads = n_inner // HEADDIM
    norm_mix, norm_mlp, norm_final = _row2(small["norm_mix"]), _row2(small["norm_mlp"]), _row2(small["norm_final"])
    b_gate, ssm_b, ssm_norm_w = _row2(small["b_gate"]), _row2(small["ssm_conv_b"]), _row2(small["ssm_norm_w"])
    dt_bias, a_log = _pad_lanes(small["dt_bias"]), _pad_lanes(small["A_log"])
    d_e = jnp.repeat(small["D_skip"].astype(F32), HEADDIM).reshape(1, n_inner)

    hb = carrying(_rms_fwd, x, norm_mix, name="rms_mix")
    sc_w, ssm_w = S.weight("sc_conv_w"), S.weight("ssm_conv_w")
    p_xbc = mm(hb, S.weight("xbc"), mode="nn", name="proj_xbc")
    p_dt = mm(hb, S.weight("dt"), mode="nn", name="proj_dt")
    p_z = mm(hb, S.weight("z"), mode="nn", name="proj_z", out_dtypes=(BF16,))
    p_sc = mm(hb, S.weight("sc"), mode="nn", name="proj_sc")
    p_gate = mm(hb, S.weight("gate"), mode="nn", name="proj_gate", out_dtypes=(BF16,))
    xbc = carrying(_ssm_conv_fwd, p_xbc, ssm_w, ssm_b, name="ssm_conv_fwd")
    dt_e, cs_e = _ssd_prep(p_dt, dt_bias, a_log, n_inner, "ssd_prep")
    ya = _sc_fwd(p_sc, sc_w, "sc_fwd")
    y, states = carrying(_ssd_fwd, xbc, dt_e, cs_e, d_e, name="ssd_fwd")
    S.point("mixers_done", [y, ya, p_gate])
    yb = carrying(_gnorm_fwd, y, p_z, ssm_norm_w, name="gnorm_fwd")
    br_a = mm(ya, S.weight("bsc"), mode="nn", name="branch_sc")
    br_b = mm(yb, S.weight("bssm"), mode="nn", name="branch_ssm")
    merged = _merge_fwd(p_gate, b_gate, br_a, br_b, "merge_fwd")
    x1 = mm(merged, S.weight("out"), mode="nn", name="out_proj", extras=(x,), epi=_epi_add)
    h2 = _rms_fwd(x1, norm_mlp, "rms_mlp")
    r_act = mm(h2, S.weight("w1"), mode="nn", name="mlp_up", epi=_epi_relu2, out_dtypes=(BF16,))
    x2 = mm(r_act, S.weight("w2"), mode="nn", name="mlp_down", extras=(x1,), epi=_epi_add)
    dx2, dx2b, g_norm_final, loss_row = _final(x2, norm_final, tgt, "final")

    S.grad("w2", mm(r_act, dx2b, mode="tn", name="mlp_down_dw", out_dtypes=(BF16,)))
    da = mm(dx2b, S.weight("w2"), mode="nt", name="mlp_down_dx", extras=(r_act,), epi=_epi_relu2_bwd, out_dtypes=(BF16,))
    S.grad("w1", mm(h2, da, mode="tn", name="mlp_up_dw", out_dtypes=(BF16,)))
    dh2 = mm(da, S.weight("w1"), mode="nt", name="mlp_up_dx")
    dx1, dx1b, g_norm_mlp = _rms_bwd(x1, norm_mlp + S.tok(), dh2, dx2, "rms_mlp_bwd")
    S.grad("out", mm(merged, dx1b, mode="tn", name="out_proj_dw", out_dtypes=(BF16,)))
    dmerged = mm(dx1b, S.weight("out"), mode="nt", name="out_proj_dx", out_dtypes=(BF16,))
    dbr_a, dbr_b, d_gate, g_b_gate = _merge_bwd(dmerged, p_gate, b_gate, br_a, br_b, "merge_bwd")
    S.grad("bssm", mm(yb, dbr_b, mode="tn", name="branch_ssm_dw", out_dtypes=(BF16,)))
    S.grad("bsc", mm(ya, dbr_a, mode="tn", name="branch_sc_dw", out_dtypes=(BF16,)))
    dyb = mm(dbr_b, S.weight("bssm"), mode="nt", name="branch_ssm_dx", out_dtypes=(BF16,))
    dya = mm(dbr_a, S.weight("bsc"), mode="nt", name="branch_sc_dx", out_dtypes=(BF16,))
    dy, d_z, g_ssm_norm_w = _gnorm_bwd(y, p_z, ssm_norm_w + S.tok(), dyb, "gnorm_bwd")
    dxs, dB, dC, ddt_e, dcs_e, dd_p = carrying(_ssd_bwd, xbc, dt_e, cs_e, d_e, states, dy, name="ssd_bwd")
    d_dt, g_dt_bias, g_a_log, g_d_skip = _ssd_post(ddt_e, dcs_e, dd_p, p_dt, dt_bias, a_log, n_heads, "ssd_post")
    d_xbc, g_ssm_w, g_ssm_b = carrying(_ssm_conv_bwd, p_xbc, ssm_w, ssm_b, dxs, dB, dC, name="ssm_conv_bwd")
    d_scB, d_scC, d_scX, g_sc_w = _sc_bwd(p_sc, sc_w, dya, "sc_bwd")
    d_sc = jnp.concatenate([d_scB, d_scC, d_scX], axis=1)
    pieces = [("sc", d_sc), ("z", d_z), ("xbc", d_xbc), ("dt", d_dt), ("gate", d_gate)]
    S.grad("win", {k: mm(hb, d, mode="tn", name="proj_dw_" + k, out_dtypes=(BF16,)) for k, d in pieces})
    pieces = [(k, d + S.tok().astype(d.dtype) if k == "dt" else d) for k, d in pieces]
    dh = mm([d for _, d in pieces], [S.weight(k) for k, _ in pieces], mode="nt", name="proj_dx")
    grad_x, _, g_norm_mix = _rms_bwd(x, norm_mix, dh, dx1, "rms_mix_bwd")

    g_small = dict(norm_mix=g_norm_mix, b_gate=g_b_gate, sc_conv_w=g_sc_w, ssm_conv_w=g_ssm_w, ssm_conv_b=g_ssm_b,
                   dt_bias=g_dt_bias, A_log=g_a_log, D_skip=g_d_skip, ssm_norm_w=g_ssm_norm_w, norm_mlp=g_norm_mlp,
                   norm_final=g_norm_final, loss=loss_row)
    return grad_x, g_small


class _Place:
    def __init__(self, k=0):
        x, y, c = lax.axis_index("x"), lax.axis_index("y"), lax.axis_index("c")
        self.x = 1 - x if k & 4 else x
        self.y = 1 - y if k & 2 else y
        self.c = 1 - c if k & 1 else c
        self.chip = 2 * self.x + self.y
        self.id = 2 * self.chip + self.c


ICI_PEERS = (2, 4, 6)
SIBLING = (1,)
ALL_PEERS = (1, 2, 3, 4, 5, 6, 7)


class _Comm:
    def __init__(self, arrs, out_shape, ks, src, dst, own=None, aliases=None):
        self.arrs, self.out_shape, self.ks = list(arrs), list(out_shape), tuple(ks)
        self.n = len(self.arrs)
        self.src, self.dst, self.own = src, dst, own
        self.aliases = aliases or {}
        dma = pltpu.SemaphoreType.DMA
        self.scratch = [dma((self.n, len(self.ks))), dma((self.n, len(self.ks))), dma((self.n,))]

    def _copies(self, ins, outs, sems, with_recvs):
        send_sems, recv_sems, local_sems = sems
        me = _Place()
        owns, sends, recvs = [], [], []
        for a in range(self.n):
            if self.own is not None:
                s, d = self.own(a, ins[a], outs[a], me)
                owns.append(pltpu.make_async_copy(s, d, local_sems.at[a]))
            for i, k in enumerate(self.ks):
                peer = _Place(k)
                for sender, lst in ((me, sends), (peer, recvs)) if with_recvs else ((me, sends),):
                    lst.append(pltpu.make_async_remote_copy(
                        src_ref=self.src(a, ins[a], me, peer), dst_ref=self.dst(a, outs[a], sender),
                        send_sem=send_sems.at[a, i], recv_sem=recv_sems.at[a, i],
                        device_id=(peer.x, peer.y, peer.c), device_id_type=MESH))
        return owns, sends, recvs

    def start(self, ins, outs, sems):
        owns, sends, _ = self._copies(ins, outs, sems, False)
        for cp in owns + sends:
            cp.start()

    def finish(self, ins, outs, sems):
        owns, sends, recvs = self._copies(ins, outs, sems, True)
        for cp in recvs:
            cp.wait_recv()
        for cp in sends:
            cp.wait_send()
        for cp in owns:
            cp.wait()


class _GatherBoth:
    def __init__(self, shards):
        self.arrs, self.n, self.aliases = list(shards), len(shards), {}
        self.out_shape = [_sds((4, 2) + s.shape, s.dtype) for s in shards]
        dma = pltpu.SemaphoreType.DMA
        self.scratch = [dma((self.n, 7)), dma((self.n, 7)), dma((self.n,))]

    def _copy(self, a, j, src, slot, to, outs, sems):
        return pltpu.make_async_remote_copy(src_ref=src, dst_ref=outs[a].at[slot.chip, slot.c], send_sem=sems[0].at[a, j],
                                            recv_sem=sems[1].at[a, j], device_id=(to.x, to.y, to.c), device_id_type=MESH)

    def start(self, ins, outs, sems):
        me, sib = _Place(), _Place(1)
        for a in range(self.n):
            pltpu.make_async_copy(ins[a], outs[a].at[me.chip, me.c], sems[2].at[a]).start()
            self._copy(a, 0, ins[a], me, sib, outs, sems).start()
            for i, k in enumerate(ICI_PEERS):
                self._copy(a, 1 + i, ins[a], me, _Place(k), outs, sems).start()

    def finish(self, ins, outs, sems):
        me, sib = _Place(), _Place(1)
        passed = []
        for i, k in enumerate(ICI_PEERS):
            peer = _Place(k)
            for a in range(self.n):
                self._copy(a, 1 + i, ins[a], peer, peer, outs, sems).wait_recv()
                cp = self._copy(a, 4 + i, outs[a].at[peer.chip, peer.c], peer, sib, outs, sems)
                cp.start()
                passed.append(cp)
        for a in range(self.n):
            self._copy(a, 0, ins[a], sib, sib, outs, sems).wait_recv()
            for i, k in enumerate(ICI_PEERS):
                far = _Place(k | 1)
                self._copy(a, 4 + i, outs[a].at[far.chip, far.c], far, sib, outs, sems).wait_recv()
        for a in range(self.n):
            self._copy(a, 0, ins[a], me, sib, outs, sems).wait_send()
            for i, k in enumerate(ICI_PEERS):
                self._copy(a, 1 + i, ins[a], me, _Place(k), outs, sems).wait_send()
            pltpu.make_async_copy(ins[a], outs[a].at[me.chip, me.c], sems[2].at[a]).wait()
        for cp in passed:
            cp.wait_send()


def _run_comm(comm, name, after=()):
    n, n_after = comm.n, len(after)

    def body(*refs):
        ins, outs, sems = refs[:n], refs[n + n_after:2 * n + n_after], refs[2 * n + n_after:]
        comm.start(ins, outs, sems)
        comm.finish(ins, outs, sems)

    return list(pl.pallas_call(body, in_specs=[ANY] * (n + n_after), out_specs=[ANY] * n, out_shape=comm.out_shape,
                               scratch_shapes=comm.scratch, input_output_aliases=dict(comm.aliases), name=name)(*comm.arrs, *after))


def _gather_sibling(bufs):
    return _Comm(bufs, [_sds(b.shape, b.dtype) for b in bufs], SIBLING,
                 src=lambda a, i, me, p: i.at[:, me.c], dst=lambda a, o, s: o.at[:, s.c], aliases={a: a for a in range(len(bufs))})


def _scatter_sibling(parts):
    return _Comm(parts, [_sds((4,) + p.shape[2:], p.dtype) for p in parts], SIBLING,
                 src=lambda a, i, me, p: i.at[:, p.c], dst=lambda a, o, s: o)


HBM_SPEC = pl.BlockSpec(memory_space=pltpu.HBM)
SEM_SPEC = pl.BlockSpec(memory_space=pltpu.SEMAPHORE)
DATAFLOW = pltpu.SideEffectType.DATAFLOW_SIDE_EFFECTING


def _tiles_2d(R, C, max_rows=256):
    if R % max_rows == 0:
        return max_rows, C, R // max_rows, lambda i: (i, 0)
    if R <= 2 * max_rows or C % 256:
        return R, C, 1, lambda i: (0, 0)
    return R, 256, C // 256, lambda i: (0, i)


def _ici_copy(gather, a, srcs, lands, send_sems, recv_sems, i, me, peer, sender):
    src = lands[a].at[me.chip, me.c] if gather else srcs[a].at[peer.chip]
    dst = lands[a].at[sender.chip, sender.c] if gather else lands[a].at[sender.chip]
    j = a * len(ICI_PEERS) + i
    return pltpu.make_async_remote_copy(src_ref=src, dst_ref=dst, send_sem=send_sems.at[j], recv_sem=recv_sems.at[j],
                                        device_id=(peer.x, peer.y, peer.c), device_id_type=MESH)


def _ici_start(srcs, lands, gather, name):
    n, n_s = len(lands), len(srcs)
    bufs = list(srcs) + list(lands)

    def body(*refs):
        src_refs, land_refs = refs[:n_s], refs[n_s:n_s + n]
        send_sems, recv_sems = refs[n_s + n], refs[n_s + n + 1]
        token = refs[-1]
        me = _Place()
        for a in range(n):
            for i, k in enumerate(ICI_PEERS):
                _ici_copy(gather, a, src_refs, land_refs, send_sems, recv_sems, i, me, _Place(k), me).start()
        token[...] = jnp.zeros_like(token)

    dma = pltpu.SemaphoreType.DMA((n * len(ICI_PEERS),))
    outs = pl.pallas_call(
        body, name=name, out_shape=(dma, dma, *[pltpu.HBM(v.shape, v.dtype) for v in bufs], _sds((8, LANES), F32)),
        in_specs=(HBM_SPEC,) * len(bufs),
        out_specs=(SEM_SPEC, SEM_SPEC) + (HBM_SPEC,) * len(bufs) + (pl.BlockSpec(memory_space=pltpu.VMEM),),
        input_output_aliases={j: 2 + j for j in range(len(bufs))}, compiler_params=pltpu.CompilerParams(has_side_effects=DATAFLOW),
    )(*[pltpu.with_memory_space_constraint(v, pltpu.HBM) for v in bufs])
    return outs[0], outs[1], list(outs[2:2 + n_s]), list(outs[2 + n_s:2 + n_s + n]), outs[-1]


def _ici_wait(flight, after, gather, name):
    send_sems, recv_sems, srcs, lands, _ = flight
    n, n_s = len(lands), len(srcs)
    bufs = srcs + lands

    def body(*refs):
        src_refs, land_refs = refs[:n_s], refs[n_s:n_s + n]
        s_sems, r_sems = refs[n_s + n], refs[n_s + n + 1]
        me = _Place()
        for a in range(n):
            for i, k in enumerate(ICI_PEERS):
                peer = _Place(k)
                cp = _ici_copy(gather, a, src_refs, land_refs, s_sems, r_sems, i, me, peer, peer)
                cp.wait_send()
                cp.wait_recv()

    outs = pl.pallas_call(
        body, name=name, out_shape=tuple(pltpu.HBM(v.shape, v.dtype) for v in bufs),
        in_specs=(HBM_SPEC,) * len(bufs) + (SEM_SPEC, SEM_SPEC) + (ANY,) * len(after), out_specs=(HBM_SPEC,) * len(bufs),
        input_output_aliases={j: j for j in range(len(bufs))}, compiler_params=pltpu.CompilerParams(has_side_effects=DATAFLOW),
    )(*bufs, send_sems, recv_sems, *after)
    return list(outs[n_s:])


def _own_shards(shards, after, name):
    n = len(shards)
    vmem = pl.BlockSpec(memory_space=pltpu.VMEM)

    def body(*refs):
        ins, outs, cast, sems = refs[:n], refs[n + 1:2 * n + 1], refs[2 * n + 1:3 * n + 1], refs[3 * n + 1]
        me = _Place()
        copies = []
        for a in range(n):
            cast[a][...] = ins[a][...].astype(BF16)
            copies.append(pltpu.make_async_copy(cast[a], outs[a].at[me.chip, me.c], sems.at[a]))
            copies[-1].start()
        for cp in copies:
            cp.wait()

    return list(pl.pallas_call(
        body, in_specs=[vmem] * n + [ANY], out_specs=[ANY] * n, out_shape=[_sds((4, 2) + s.shape, BF16) for s in shards],
        scratch_shapes=[pltpu.VMEM(s.shape, BF16) for s in shards] + [pltpu.SemaphoreType.DMA((n,))], name=name)(*shards, after))


def _col_pieces(widths):
    out, c = [], 0
    for k, w in widths:
        out.append((k, c, w))
        c += w
    return out


def _split_range(c0, n, bounds):
    parts, c = [], c0
    while c < c0 + n:
        r = max(i for i in range(len(bounds) - 1) if bounds[i] <= c)
        w = min(c0 + n, bounds[r + 1]) - c
        parts.append((r, c - bounds[r], w))
        c += w
    return parts


def _win_unpack(g, widths, name):
    n, R, C = g.shape
    tr = min(256, R)
    pieces = _col_pieces(widths)
    padded = [-(-w // LANES) * LANES for _, _, w in pieces]
    shard_bounds = [s * C for s in range(n + 1)]

    def body(g_ref, *o_refs):
        for (k, c0, w), o_ref in zip(pieces, o_refs):
            for t in range(0, o_ref.shape[1], LANES):
                valid = max(0, min(LANES, w - t))
                cols = [g_ref[s, :, o:o + ww] for s, o, ww in _split_range(c0 + t, valid, shard_bounds)] if valid else []
                if valid < LANES:
                    cols.append(jnp.zeros((tr, LANES - valid), g_ref.dtype))
                o_ref[:, t:t + LANES] = cols[0] if len(cols) == 1 else jnp.concatenate(cols, axis=1)

    return pl.pallas_call(
        body, grid=(R // tr,), in_specs=[pl.BlockSpec((n, tr, C), lambda i: (0, i, 0))],
        out_specs=[pl.BlockSpec((tr, p), lambda i: (i, 0)) for p in padded],
        out_shape=[_sds((R, p), g.dtype) for p in padded], name=name, compiler_params=_params("parallel"))(g)


def _win_pack(grads, widths, n, name):
    R = grads[0].shape[0]
    tr = min(256, R)
    pieces = _col_pieces(widths)
    total = pieces[-1][1] + pieces[-1][2]
    C = total // n
    bounds = [c0 for _, c0, _ in pieces] + [total]

    def body(*refs):
        g_refs, o_ref = refs[:-1], refs[-1]

        def tile_t(c0):
            cols = [g_refs[r][:, o:o + ww] for r, o, ww in _split_range(c0, LANES, bounds)]
            tile = cols[0] if len(cols) == 1 else jnp.concatenate(cols, axis=1)
            return tile.astype(F32).T

        for s in range(n):
            full = C // LANES * LANES
            for t in range(0, full, LANES):
                o_ref[s, t:t + LANES, :] = tile_t(s * C + t).astype(o_ref.dtype)
            if full < C:
                o_ref[s, full:C, :] = tile_t(s * C + C - LANES)[LANES - (C - full):, :].astype(o_ref.dtype)

    return pl.pallas_call(
        body, grid=(R // tr,), in_specs=[pl.BlockSpec((tr, gr.shape[1]), lambda i: (i, 0)) for gr in grads],
        out_specs=pl.BlockSpec((n, C, tr), lambda i: (0, 0, i)), out_shape=_sds((n, C, R), grads[0].dtype),
        name=name, compiler_params=_params("parallel"))(*grads)


def _gather_all(arrs):
    return _Comm(arrs, [_sds((N_DEV,) + a.shape, a.dtype) for a in arrs], ALL_PEERS,
                 src=lambda a, i, me, p: i, dst=lambda a, o, s: o.at[s.id], own=lambda a, i, o, me: (i, o.at[me.id]))


def _add_halves(parts, got, name):
    n, _, R, C = parts.shape
    br, bc, nb, at = _tiles_2d(R, C, max_rows=1024)
    place = jnp.stack([lax.axis_index("c"), 2 * lax.axis_index("x") + lax.axis_index("y")]).astype(jnp.int32)

    def body(q_ref, p_ref, g_ref, o_ref, land_ref):
        s = (p_ref[0, 0].astype(F32) + g_ref[0].astype(F32)).astype(o_ref.dtype)
        o_ref[0] = s

        @pl.when(pl.program_id(1) == q_ref[1])
        def _():
            land_ref[0] = s

    spec = pltpu.PrefetchScalarGridSpec(
        num_scalar_prefetch=1, grid=(nb, n),
        in_specs=[pl.BlockSpec((1, 1, br, bc), lambda i, q, q_ref: (q, q_ref[0]) + at(i)), pl.BlockSpec((1, br, bc), lambda i, q, q_ref: (q,) + at(i))],
        out_specs=[pl.BlockSpec((1, br, bc), lambda i, q, q_ref: (q,) + at(i)), pl.BlockSpec((1, br, bc), lambda i, q, q_ref: (q_ref[1],) + at(i))])
    return pl.pallas_call(body, grid_spec=spec, out_shape=[_sds((n, R, C), parts.dtype)] * 2, name=name,
                          compiler_params=_params("parallel", "arbitrary"))(place, parts, got)


def _adam(w, m, v, gparts, name, comm=None):
    R, C = w.shape
    n = gparts.shape[0]
    br, bc, nb, at = _tiles_2d(R, C, max_rows=512)
    c1 = 1.0 / (1.0 - ADAM_B1 ** ADAM_STEP)
    c2 = 1.0 / (1.0 - ADAM_B2 ** ADAM_STEP)

    def body(w_ref, m_ref, v_ref, g_ref, go_ref, d_ref, mo_ref, vo_ref):
        g = g_ref[0].astype(F32)
        for s in range(1, n):
            g = g + g_ref[s].astype(F32)
        mn = ADAM_B1 * m_ref[...] + (1.0 - ADAM_B1) * g
        vn = ADAM_B2 * v_ref[...] + (1.0 - ADAM_B2) * (g * g)
        go_ref[...] = g
        mo_ref[...] = mn
        vo_ref[...] = vn
        d_ref[...] = -ADAM_LR * ((mn * c1) / (jnp.sqrt(vn * c2) + ADAM_EPS) + ADAM_WD * w_ref[...])

    blk = pl.BlockSpec((br, bc), at)
    outs, carried = _call(
        body, grid=(nb,), in_specs=[blk, blk, blk, pl.BlockSpec((n, br, bc), lambda i: (0,) + at(i))],
        out_specs=[blk] * 4, out_shape=[_sds((R, C), F32)] * 4, args=[w, m, v, gparts], name=name, sem=("parallel",), comm=comm)
    return outs if comm is None else (outs, carried)


_SMALL_ORDER = ("norm_mix", "b_gate", "sc_conv_w", "ssm_conv_w", "ssm_conv_b", "dt_bias", "A_log", "D_skip", "ssm_norm_w",
                "norm_mlp", "norm_final", "loss")
_REPLICATED = ("norm_mix", "b_gate", "ssm_conv_b", "dt_bias", "A_log", "D_skip", "ssm_norm_w", "norm_mlp", "norm_final")


def _cols_to_slots(g, n):
    R = g.shape[0]
    return jnp.transpose(g.reshape(R, n, g.shape[1] // n), (1, 0, 2))


def _slots_to_cols(g):
    n, R, C = g.shape
    return jnp.transpose(g, (1, 0, 2)).reshape(R, n * C)


def kernel(x, norm_mix, w_in, b_gate, sc_conv_w, ssm_conv_w, ssm_conv_b, dt_bias, A_log, D_skip, ssm_norm_w, w_branch_sc, w_branch_ssm, w_out, norm_mlp, w_mlp1, w_mlp2, norm_final, loss_target, m_norm_mix, m_w_in, m_b_gate, m_sc_conv_w, m_ssm_conv_w, m_ssm_conv_b, m_dt_bias, m_A_log, m_D_skip, m_ssm_norm_w, m_w_branch_sc, m_w_branch_ssm, m_w_out, m_norm_mlp, m_w_mlp1, m_w_mlp2, m_norm_final, v_norm_mix, v_w_in, v_b_gate, v_sc_conv_w, v_ssm_conv_w, v_ssm_conv_b, v_dt_bias, v_A_log, v_D_skip, v_ssm_norm_w, v_w_branch_sc, v_w_branch_ssm, v_w_out, v_norm_mlp, v_w_mlp1, v_w_mlp2, v_norm_final):
    T, D = x.shape[1], x.shape[2]
    n_inner = 2 * D
    n_heads = n_inner // HEADDIM
    n_xbc = n_inner + 2 * NGROUPS * NSTATE
    me = 4 * lax.axis_index("x") + 2 * lax.axis_index("y") + lax.axis_index("c")

    in_cols = [("sc", 3 * D), ("z", n_inner), ("xbc", n_xbc), ("dt", n_heads), ("gate", 2 * D)]
    by_owner = lambda b: b.reshape((N_DEV,) + b.shape[2:])
    to_owner = lambda g: g.reshape((4, 2) + g.shape[1:])
    rows_of = lambda g: to_owner(g.reshape((N_DEV, g.shape[0] // N_DEV) + g.shape[1:]))
    cols_of = lambda g: to_owner(_cols_to_slots(g, N_DEV))

    class Schedule(_NoExchange):
        late = ("bssm", "bsc", "out", "w1", "w2")
        gather_sib = dict(gnorm_fwd=("bsc", "bssm", "out"), branch_ssm=("w1", "w2"))
        scatter_sib = dict(mlp_up_dx=("w2", "w1"), branch_ssm_dx=("out", "bssm", "bsc"))
        shards = dict(bsc=w_branch_sc, bssm=w_branch_ssm, out=w_out, w1=w_mlp1, w2=w_mlp2)

        def __init__(self):
            self.W, self.staged, self.grads, self.summed, self.scatters = {}, {}, {}, {}, []
            self.token = jnp.zeros((), F32)

        def first_weights(self, bufs):
            self.W.update(zip([k for k, _ in in_cols], _win_unpack(by_owner(bufs[0]), in_cols, "win_unpack")))
            self.W.update(sc_conv_w=_slots_to_cols(by_owner(bufs[1])), ssm_conv_w=_slots_to_cols(by_owner(bufs[2])))
            lands = _own_shards([self.shards[k] for k in self.late], bufs[1], "own_shards")
            self.gather_flight = _ici_start([], lands, True, "gather_late_start")
            self.token = self.gather_flight[4][0, 0]
            self.W["dt"] = self.W["dt"] + self.token.astype(BF16)

        def tok(self):
            return self.token

        def point(self, name, values):
            if name == "mixers_done":
                lands = _ici_wait(self.gather_flight, values, True, "gather_late_wait")
                self.staged.update(zip(self.late, lands))

        def carry(self, name):
            if name == "rms_mix":
                return _GatherBoth([w_in.astype(BF16), sc_conv_w, ssm_conv_w])
            if name in self.gather_sib:
                return _gather_sibling([self.staged.pop(k) for k in self.gather_sib[name]])
            if name in self.scatter_sib:
                return _scatter_sibling([self.grads[k] for k in self.scatter_sib[name]])
            return None

        def start_scatter(self, keys, halves_and_lands):
            halves, lands = [h for h, _ in halves_and_lands], [l for _, l in halves_and_lands]
            flight = _ici_start(halves, lands, False, "scatter_%s_start" % keys[0])
            self.scatters.append((keys, flight))
            self.token = flight[4][0, 0]

        def carried(self, name, outs):
            if name == "rms_mix":
                self.first_weights(outs)
            elif name in self.gather_sib:
                for k, b in zip(self.gather_sib[name], outs):
                    full = by_owner(b)
                    self.W[k] = _slots_to_cols(full) if k == "w1" else full.reshape(-1, D)
            else:
                keys = self.scatter_sib[name]
                self.start_scatter(keys, [_add_halves(self.grads[k], b, "add_halves_" + k) for k, b in zip(keys, outs)])

        def grad(self, k, g):
            if k == "win":
                g = to_owner(_win_pack([g[k] for k, _ in in_cols], in_cols, N_DEV, "win_pack"))
                got = _run_comm(_scatter_sibling([g]), "scatter_sibling_win")[0]
                self.start_scatter(("win",), [_add_halves(g, got, "add_halves_win")])
            else:
                self.grads[k] = cols_of(g) if k == "w1" else rows_of(g)

        def finish_scatter(self, after):
            keys, flight = self.scatters.pop(0)
            return dict(zip(keys, _ici_wait(flight, after, False, "scatter_%s_wait" % keys[0])))

    S = Schedule()
    small = dict(norm_mix=norm_mix, b_gate=b_gate, ssm_conv_b=ssm_conv_b, dt_bias=dt_bias, A_log=A_log, D_skip=D_skip,
                 ssm_norm_w=ssm_norm_w, norm_mlp=norm_mlp, norm_final=norm_final)
    grad_x, g_small = _local_step(x.reshape(T, D), loss_target.reshape(T, D), S, small)

    small_flat = jnp.concatenate([g_small[k].reshape(-1) for k in _SMALL_ORDER])
    n_small = small_flat.shape[0]
    rows = -(-n_small // (8 * LANES)) * 8
    small_pack = jnp.pad(small_flat, (0, rows * LANES - n_small)).reshape(rows, LANES)

    res = {}
    big = [("w_in", "win", w_in, m_w_in, v_w_in), ("w_branch_sc", "bsc", w_branch_sc, m_w_branch_sc, v_w_branch_sc),
           ("w_branch_ssm", "bssm", w_branch_ssm, m_w_branch_ssm, v_w_branch_ssm), ("w_out", "out", w_out, m_w_out, v_w_out),
           ("w_mlp1", "w1", w_mlp1, m_w_mlp1, v_w_mlp1), ("w_mlp2", "w2", w_mlp2, m_w_mlp2, v_w_mlp2)]
    by_grad = {gk: (k, w, m, v) for k, gk, w, m, v in big}
    after = [grad_x]
    while S.scatters:
        for gk, parts in S.finish_scatter(after).items():
            k, w, m, v = by_grad[gk]
            if gk == "win":
                res_t, (small_parts,) = _adam(w.T, m.T, v.T, parts, "adam_" + k, comm=_gather_all([small_pack]))
                res[k] = [r.T for r in res_t]
            else:
                res[k] = _adam(w, m, v, parts, "adam_" + k)
            after = after + [res[k][1]]

    sizes = {k: g_small[k].size for k in _SMALL_ORDER}
    offs, o = {}, 0
    for k in _SMALL_ORDER:
        offs[k] = o
        o += sizes[k]
    rep_w = dict(norm_mix=norm_mix, b_gate=b_gate, ssm_conv_b=ssm_conv_b, dt_bias=dt_bias, A_log=A_log, D_skip=D_skip,
                 ssm_norm_w=ssm_norm_w, norm_mlp=norm_mlp, norm_final=norm_final)
    rep_m = dict(norm_mix=m_norm_mix, b_gate=m_b_gate, ssm_conv_b=m_ssm_conv_b, dt_bias=m_dt_bias, A_log=m_A_log, D_skip=m_D_skip,
                 ssm_norm_w=m_ssm_norm_w, norm_mlp=m_norm_mlp, norm_final=m_norm_final)
    rep_v = dict(norm_mix=v_norm_mix, b_gate=v_b_gate, ssm_conv_b=v_ssm_conv_b, dt_bias=v_dt_bias, A_log=v_A_log, D_skip=v_D_skip,
                 ssm_norm_w=v_ssm_norm_w, norm_mlp=v_norm_mlp, norm_final=v_norm_final)

    def pack(d):
        segs = [jnp.pad(d[k].astype(F32).reshape(-1), (0, sizes[k] - d[k].size)) if k in d else jnp.zeros((sizes[k],), F32)
                for k in _SMALL_ORDER]
        return jnp.pad(jnp.concatenate(segs), (0, rows * LANES - n_small)).reshape(rows, LANES)

    sm = _adam(pack(rep_w), pack(rep_m), pack(rep_v), small_parts, "adam_small")
    sm = [s.reshape(-1) for s in sm]
    for k in _REPLICATED:
        n_k = rep_w[k].shape[0]
        res[k] = tuple(s[offs[k]:offs[k] + n_k] for s in sm)
    loss = sm[0][offs["loss"]]
    for k, w, m, v, K, full in (("sc_conv_w", sc_conv_w, m_sc_conv_w, v_sc_conv_w, SC_K, D),
                                ("ssm_conv_w", ssm_conv_w, m_ssm_conv_w, v_ssm_conv_w, SSM_K, n_xbc)):
        g_full = sm[0][offs[k]:offs[k] + K * full].reshape(K, full)
        cw = full // N_DEV
        g_mine = lax.dynamic_slice_in_dim(g_full, me * cw, cw, axis=1)
        res[k] = _adam(w, m, v, g_mine[None], "adam_" + k)

    order = ("norm_mix", "w_in", "b_gate", "sc_conv_w", "ssm_conv_w", "ssm_conv_b", "dt_bias", "A_log", "D_skip", "ssm_norm_w",
             "w_branch_sc", "w_branch_ssm", "w_out", "norm_mlp", "w_mlp1", "w_mlp2", "norm_final")
    outs = [loss, grad_x.reshape(1, T, D)]
    for j in range(4):
        outs += [res[k][j] for k in order]
    return tuple(outs)
```

```python
import jax
import jax.numpy as jnp
from jax import lax
from jax.experimental import pallas as pl
from jax.experimental.pallas import tpu as pltpu

F32 = jnp.float32
BF16 = jnp.bfloat16

EPS = 1e-6
N_DEV = 8
HEADDIM = 64
NSTATE = 128
CHUNK = 128
NGROUPS = 8
GROUP_W = 256
SC_K = 3
SSM_K = 4
LANES = 128

ADAM_LR = 0.001
ADAM_B1 = 0.9
ADAM_B2 = 0.999
ADAM_EPS = 1e-08
ADAM_WD = 0.01
ADAM_STEP = 10

NN = (((1,), (0,)), ((), ()))
NT = (((1,), (1,)), ((), ()))
TN = (((0,), (0,)), ((), ()))
_DIMS = {"nn": NN, "nt": NT, "tn": TN}

ANY = pl.BlockSpec(memory_space=pl.ANY)
MESH = pl.DeviceIdType.MESH


def _sds(shape, dtype):
    return jax.ShapeDtypeStruct(tuple(shape), dtype)


def _dot(a, b, dims=NN):
    return lax.dot_general(a, b, dims, preferred_element_type=F32)


def _dot3(a, b, dims=NN):
    return lax.dot_general(a, b, dims, preferred_element_type=F32, precision=lax.Precision.HIGH)


def _params(*sem):
    return pltpu.CompilerParams(dimension_semantics=tuple(sem))


def _call(body, *, grid, in_specs, out_specs, out_shape, args, name, sem, scratch=(), comm=None):
    if comm is None:
        outs = pl.pallas_call(body, grid=grid, in_specs=list(in_specs), out_specs=list(out_specs), out_shape=list(out_shape),
                              scratch_shapes=list(scratch), name=name, compiler_params=_params(*sem))(*args)
        return list(outs), None
    n, n_in, n_out, n_scr = comm.n, len(in_specs), len(out_shape), len(scratch)

    def wrapped(*refs):
        ins, c_in = refs[:n_in], refs[n_in:n_in + n]
        outs, c_out = refs[n_in + n:n_in + n + n_out], refs[n_in + n + n_out:n_in + 2 * n + n_out]
        rest = refs[n_in + 2 * n + n_out:]
        scr, sems = rest[:n_scr], rest[n_scr:]
        first, last = None, None
        for d, g in enumerate(grid):
            f, l = pl.program_id(d) == 0, pl.program_id(d) == g - 1
            first, last = (f, l) if first is None else (first & f, last & l)

        @pl.when(first)
        def _():
            comm.start(c_in, c_out, sems)

        body(*ins, *outs, *scr)

        @pl.when(last)
        def _():
            comm.finish(c_in, c_out, sems)

    outs = pl.pallas_call(
        wrapped, grid=grid, in_specs=list(in_specs) + [ANY] * n, out_specs=list(out_specs) + [ANY] * n,
        out_shape=list(out_shape) + comm.out_shape, scratch_shapes=list(scratch) + comm.scratch,
        input_output_aliases={n_in + i: n_out + o for i, o in comm.aliases.items()},
        name=name, compiler_params=_params(*["arbitrary"] * len(grid)))(*args, *comm.arrs)
    return list(outs[:n_out]), list(outs[n_out:])


MM_VMEM_BUDGET = 44 * 2 ** 20


def _mm_tiles(M, N, k_bytes, mn_bytes):
    best = None
    for tm in (2048, 1024, 512, 256, 128):
        for tn in (1024, 512, 256, 128):
            if M % tm or N % tn:
                continue
            need = 2 * ((tm + tn) * k_bytes + tm * tn * mn_bytes) + 4 * tm * tn * 4
            if need <= MM_VMEM_BUDGET and (best is None or (tm * tn, tm) > (best[0] * best[1], best[0])):
                best = (tm, tn)
    assert best is not None, (M, N, k_bytes, mn_bytes)
    return best


def _mm(a, b, *, mode, name, extras=(), epi=None, out_dtypes=(F32,), comm=None):
    a_list = list(a) if isinstance(a, (list, tuple)) else [a]
    b_list = list(b) if isinstance(b, (list, tuple)) else [b]
    if mode == "nn":
        M, N = a_list[0].shape[0], b_list[0].shape[1]
    elif mode == "nt":
        M, N = a_list[0].shape[0], b_list[0].shape[0]
    else:
        M, N = a_list[0].shape[1], b_list[0].shape[1]
    k_bytes = sum((av.shape[0] if mode == "tn" else av.shape[1]) * av.dtype.itemsize for av in a_list)
    mn_bytes = sum(e.dtype.itemsize for e in extras) + sum(jnp.dtype(d).itemsize for d in out_dtypes)
    tm, tn = _mm_tiles(min(M, 2048), min(N, 1024), k_bytes, mn_bytes) if M % 128 == 0 and N % 128 == 0 else (M, N)
    assert M % tm == 0 and N % tn == 0
    a_specs, b_specs = [], []
    for av, bv in zip(a_list, b_list):
        K = av.shape[0] if mode == "tn" else av.shape[1]
        a_specs.append(pl.BlockSpec((K, tm), lambda i, j: (0, i)) if mode == "tn" else pl.BlockSpec((tm, K), lambda i, j: (i, 0)))
        b_specs.append(pl.BlockSpec((tn, K), lambda i, j: (j, 0)) if mode == "nt" else pl.BlockSpec((K, tn), lambda i, j: (0, j)))
    mn_spec = pl.BlockSpec((tm, tn), lambda i, j: (i, j))
    n_p, n_ex = len(a_list), len(extras)
    dims = _DIMS[mode]

    def body(*refs):
        acc = _dot(refs[0][...], refs[n_p][...], dims)
        for p in range(1, n_p):
            acc = acc + _dot(refs[p][...], refs[n_p + p][...], dims)
        rest = refs[2 * n_p:]
        res = (acc,) if epi is None else epi(acc, *[r[...] for r in rest[:n_ex]])
        for o_ref, r in zip(rest[n_ex:], res):
            o_ref[...] = r.astype(o_ref.dtype)

    outs, carried = _call(
        body, grid=(M // tm, N // tn), in_specs=a_specs + b_specs + [mn_spec] * n_ex,
        out_specs=[mn_spec] * len(out_dtypes), out_shape=[_sds((M, N), d) for d in out_dtypes],
        args=a_list + b_list + list(extras), name=name, sem=("parallel", "parallel"), comm=comm)
    res = outs[0] if len(outs) == 1 else outs
    return res if comm is None else (res, carried)


def _epi_add(acc, r):
    return (acc + r,)


def _epi_relu2(acc):
    p = jnp.maximum(acc, 0.0)
    return (p * p,)


def _epi_relu2_bwd(acc, r):
    return (acc * (2.0 * jnp.sqrt(r.astype(F32))),)


ROW_TILE = 512


def _row(tr, n):
    return pl.BlockSpec((tr, n), lambda i: (i, 0))


def _vec(n):
    return pl.BlockSpec((1, n), lambda i: (0, 0))


def _rms_fwd(x, w, name, comm=None):
    T, D = x.shape
    tr = min(ROW_TILE, T)

    def body(x_ref, w_ref, o_ref):
        xv = x_ref[...]
        r = lax.rsqrt(jnp.mean(xv * xv, axis=-1, keepdims=True) + EPS)
        o_ref[...] = (xv * r * w_ref[...]).astype(BF16)

    outs, carried = _call(body, grid=(T // tr,), in_specs=[_row(tr, D), _vec(D)], out_specs=[_row(tr, D)],
                          out_shape=[_sds((T, D), BF16)], args=[x, w], name=name, sem=("parallel",), comm=comm)
    return outs[0] if comm is None else (outs[0], carried)


def _rms_bwd(x, w, dh, dres, name):
    T, D = x.shape
    tr = min(ROW_TILE, T)

    def body(x_ref, w_ref, dh_ref, dres_ref, dx_ref, dxb_ref, dw_ref):
        @pl.when(pl.program_id(0) == 0)
        def _():
            dw_ref[...] = jnp.zeros_like(dw_ref)

        xv = x_ref[...]
        r = lax.rsqrt(jnp.mean(xv * xv, axis=-1, keepdims=True) + EPS)
        xh = xv * r
        dh_v = dh_ref[...]
        dw_ref[...] += jnp.sum(dh_v * xh, axis=0, keepdims=True)
        dxh = dh_v * w_ref[...]
        dx = r * (dxh - xh * jnp.mean(dxh * xh, axis=-1, keepdims=True)) + dres_ref[...]
        dx_ref[...] = dx
        dxb_ref[...] = dx.astype(BF16)

    return pl.pallas_call(
        body, grid=(T // tr,), in_specs=[_row(tr, D), _vec(D), _row(tr, D), _row(tr, D)],
        out_specs=[_row(tr, D), _row(tr, D), _vec(D)],
        out_shape=[_sds((T, D), F32), _sds((T, D), BF16), _sds((1, D), F32)],
        name=name, compiler_params=_params("arbitrary"))(x, w, dh, dres)


def _final(x2, w, tgt, name):
    T, D = x2.shape
    tr = min(ROW_TILE, T)

    def body(x_ref, w_ref, t_ref, dx_ref, dxb_ref, dw_ref, loss_ref):
        @pl.when(pl.program_id(0) == 0)
        def _():
            dw_ref[...] = jnp.zeros_like(dw_ref)
            loss_ref[...] = jnp.zeros_like(loss_ref)

        xv = x_ref[...]
        wv = w_ref[...]
        r = lax.rsqrt(jnp.mean(xv * xv, axis=-1, keepdims=True) + EPS)
        xh = xv * r
        err = xh * wv - t_ref[...]
        part = jnp.sum(jnp.sum(err * err, axis=1, keepdims=True), axis=0, keepdims=True) * (0.5 / D)
        loss_ref[...] += jnp.broadcast_to(part, loss_ref.shape)
        dy = err * (1.0 / D)
        dw_ref[...] += jnp.sum(dy * xh, axis=0, keepdims=True)
        dxh = dy * wv
        dx = r * (dxh - xh * jnp.mean(dxh * xh, axis=-1, keepdims=True))
        dx_ref[...] = dx
        dxb_ref[...] = dx.astype(BF16)

    return pl.pallas_call(
        body, grid=(T // tr,), in_specs=[_row(tr, D), _vec(D), _row(tr, D)],
        out_specs=[_row(tr, D), _row(tr, D), _vec(D), _vec(LANES)],
        out_shape=[_sds((T, D), F32), _sds((T, D), BF16), _sds((1, D), F32), _sds((1, LANES), F32)],
        name=name, compiler_params=_params("arbitrary"))(x2, w, tgt)


def _silu_parts(z):
    s = jax.nn.sigmoid(z)
    return z * s, s * (1.0 + z * (1.0 - s))


def _gnorm_fwd(y, z, w, name, comm=None):
    T, N = y.shape
    tr = min(ROW_TILE, T)

    def body(y_ref, z_ref, w_ref, o_ref):
        for g in range(N // GROUP_W):
            sl = slice(g * GROUP_W, (g + 1) * GROUP_W)
            silu, _ = _silu_parts(z_ref[:, sl].astype(F32))
            yz = y_ref[:, sl] * silu
            r = lax.rsqrt(jnp.mean(yz * yz, axis=-1, keepdims=True) + EPS)
            o_ref[:, sl] = (yz * r * w_ref[:, sl]).astype(BF16)

    outs, carried = _call(body, grid=(T // tr,), in_specs=[_row(tr, N), _row(tr, N), _vec(N)], out_specs=[_row(tr, N)],
                          out_shape=[_sds((T, N), BF16)], args=[y, z, w], name=name, sem=("parallel",), comm=comm)
    return outs[0] if comm is None else (outs[0], carried)


def _gnorm_bwd(y, z, w, dyb, name):
    T, N = y.shape
    tr = min(ROW_TILE, T)

    def body(y_ref, z_ref, w_ref, d_ref, dy_ref, dz_ref, dw_ref):
        @pl.when(pl.program_id(0) == 0)
        def _():
            dw_ref[...] = jnp.zeros_like(dw_ref)

        for g in range(N // GROUP_W):
            sl = slice(g * GROUP_W, (g + 1) * GROUP_W)
            yv = y_ref[:, sl]
            silu, dsilu = _silu_parts(z_ref[:, sl].astype(F32))
            yz = yv * silu
            r = lax.rsqrt(jnp.mean(yz * yz, axis=-1, keepdims=True) + EPS)
            yzh = yz * r
            d = d_ref[:, sl].astype(F32)
            dw_ref[:, sl] += jnp.sum(d * yzh, axis=0, keepdims=True)
            dyzh = d * w_ref[:, sl]
            dyz = r * (dyzh - yzh * jnp.mean(dyzh * yzh, axis=-1, keepdims=True))
            dy_ref[:, sl] = dyz * silu
            dz_ref[:, sl] = (dyz * yv * dsilu).astype(BF16)

    return pl.pallas_call(
        body, grid=(T // tr,), in_specs=[_row(tr, N), _row(tr, N), _vec(N), _row(tr, N)],
        out_specs=[_row(tr, N), _row(tr, N), _vec(N)],
        out_shape=[_sds((T, N), F32), _sds((T, N), BF16), _sds((1, N), F32)],
        name=name, compiler_params=_params("arbitrary"))(y, z, w, dyb)


def _merge_fwd(gate_raw, b_gate, br_a, br_b, name):
    T, D = br_a.shape
    tr = min(ROW_TILE, T)

    def body(g_ref, bg_ref, a_ref, b_ref, o_ref):
        g = jax.nn.sigmoid(g_ref[...].astype(F32) + bg_ref[...])
        o_ref[...] = (g[:, :D] * a_ref[...].astype(F32) + g[:, D:] * b_ref[...].astype(F32)).astype(BF16)

    return pl.pallas_call(body, grid=(T // tr,), in_specs=[_row(tr, 2 * D), _vec(2 * D), _row(tr, D), _row(tr, D)],
                          out_specs=_row(tr, D), out_shape=_sds((T, D), BF16), name=name,
                          compiler_params=_params("parallel"))(gate_raw, b_gate, br_a, br_b)


def _merge_bwd(dmerged, gate_raw, b_gate, br_a, br_b, name):
    T, D = br_a.shape
    tr = min(ROW_TILE, T)

    def body(d_ref, g_ref, bg_ref, a_ref, b_ref, da_ref, db_ref, dg_ref, dbg_ref):
        @pl.when(pl.program_id(0) == 0)
        def _():
            dbg_ref[...] = jnp.zeros_like(dbg_ref)

        g = jax.nn.sigmoid(g_ref[...].astype(F32) + bg_ref[...])
        d = d_ref[...].astype(F32)
        da_ref[...] = (d * g[:, :D]).astype(BF16)
        db_ref[...] = (d * g[:, D:]).astype(BF16)
        dg = jnp.concatenate([d * a_ref[...].astype(F32), d * b_ref[...].astype(F32)], axis=1) * g * (1.0 - g)
        dg_ref[...] = dg.astype(BF16)
        dbg_ref[...] += jnp.sum(dg, axis=0, keepdims=True)

    return pl.pallas_call(
        body, grid=(T // tr,), in_specs=[_row(tr, D), _row(tr, 2 * D), _vec(2 * D), _row(tr, D), _row(tr, D)],
        out_specs=[_row(tr, D), _row(tr, D), _row(tr, 2 * D), _vec(2 * D)],
        out_shape=[_sds((T, D), BF16), _sds((T, D), BF16), _sds((T, 2 * D), BF16), _sds((1, 2 * D), F32)],
        name=name, compiler_params=_params("arbitrary"))(dmerged, gate_raw, b_gate, br_a, br_b)


CB_W = 256
CONV_ROWS = 32
CONV_PAD = 8


def _rows_down(load, r0, s):
    if s == 0:
        return load(r0, r0 + CONV_ROWS)
    if r0 == 0:
        row = lax.broadcasted_iota(jnp.int32, (CONV_ROWS, CB_W), 0)
        return jnp.where(row >= s, pltpu.roll(load(0, CONV_ROWS), s, 0), 0.0)
    return load(r0 - s, r0 - s + CONV_ROWS)


def _conv_tile(load, taps, r0):
    K = len(taps)
    us = [_rows_down(load, r0, K - 1 - k) for k in range(K)]
    acc = us[K - 1] * taps[K - 1]
    for k in range(K - 1):
        acc = acc + us[k] * taps[k]
    return acc, us


def _conv_back_tile(scr, taps, r0):
    K = len(taps)
    du = scr[r0:r0 + CONV_ROWS, :] * taps[K - 1]
    for k in range(K - 1):
        s = K - 1 - k
        du = du + scr[r0 + s:r0 + s + CONV_ROWS, :] * taps[k]
    return du


def _fold8(v):
    return jnp.sum(v.reshape(CONV_ROWS // 8, 8, v.shape[1]), axis=0)


def _col(T, j0=0):
    return pl.BlockSpec((T, CB_W), lambda j: (0, j + j0))


def _sc_fwd(psc, w, name):
    T, D = psc.shape[0], psc.shape[1] // 3
    nb = D // CB_W

    def body(b_ref, c_ref, x_ref, w_ref, o_ref):
        taps = [w_ref[k:k + 1, :] for k in range(SC_K)]
        load = lambda a, b: c_ref[a:b, :] * x_ref[a:b, :]
        for r0 in range(0, T, CONV_ROWS):
            cu, _ = _conv_tile(load, taps, r0)
            o_ref[r0:r0 + CONV_ROWS, :] = (b_ref[r0:r0 + CONV_ROWS, :] * cu).astype(BF16)

    return pl.pallas_call(
        body, grid=(nb,), in_specs=[_col(T), _col(T, nb), _col(T, 2 * nb), pl.BlockSpec((SC_K, CB_W), lambda j: (0, j))],
        out_specs=_col(T), out_shape=_sds((T, D), BF16), name=name, compiler_params=_params("parallel"))(psc, psc, psc, w)


def _sc_bwd(psc, w, dya, name):
    T, D = psc.shape[0], psc.shape[1] // 3
    nb = D // CB_W

    def body(b_ref, c_ref, x_ref, w_ref, d_ref, db_ref, dc_ref, dx_ref, dw_ref, scr):
        taps = [w_ref[k:k + 1, :] for k in range(SC_K)]
        load = lambda a, b: c_ref[a:b, :] * x_ref[a:b, :]
        scr[T:T + CONV_PAD, :] = jnp.zeros((CONV_PAD, CB_W), F32)
        dw8 = [jnp.zeros((8, CB_W), F32)] * SC_K
        for r0 in range(0, T, CONV_ROWS):
            rows = slice(r0, r0 + CONV_ROWS)
            cu, us = _conv_tile(load, taps, r0)
            d = d_ref[rows, :].astype(F32)
            db_ref[rows, :] = (d * cu).astype(BF16)
            dcu = d * b_ref[rows, :]
            scr[rows, :] = dcu
            dw8 = [acc + _fold8(dcu * u) for acc, u in zip(dw8, us)]
        for k in range(SC_K):
            dw_ref[k:k + 1, :] = jnp.sum(dw8[k], axis=0, keepdims=True)
        for r0 in range(0, T, CONV_ROWS):
            rows = slice(r0, r0 + CONV_ROWS)
            du = _conv_back_tile(scr, taps, r0)
            dc_ref[rows, :] = (du * x_ref[rows, :]).astype(BF16)
            dx_ref[rows, :] = (du * c_ref[rows, :]).astype(BF16)

    wspec = pl.BlockSpec((SC_K, CB_W), lambda j: (0, j))
    return pl.pallas_call(
        body, grid=(nb,), in_specs=[_col(T), _col(T, nb), _col(T, 2 * nb), wspec, _col(T)],
        out_specs=[_col(T), _col(T), _col(T), wspec],
        out_shape=[_sds((T, D), BF16)] * 3 + [_sds((SC_K, D), F32)],
        scratch_shapes=[pltpu.VMEM((T + CONV_PAD, CB_W), F32)],
        name=name, compiler_params=_params("parallel"))(psc, psc, psc, w, dya)


def _ssm_conv_fwd(u, w, b, name, comm=None):
    T, N = u.shape

    def body(u_ref, w_ref, b_ref, o_ref):
        taps = [w_ref[k:k + 1, :] for k in range(SSM_K)]
        bias = b_ref[...]
        for r0 in range(0, T, CONV_ROWS):
            c, _ = _conv_tile(lambda a, b: u_ref[a:b, :], taps, r0)
            c = c + bias
            o_ref[r0:r0 + CONV_ROWS, :] = c * jax.nn.sigmoid(c)

    outs, carried = _call(
        body, grid=(N // CB_W,), in_specs=[_col(T), pl.BlockSpec((SSM_K, CB_W), lambda j: (0, j)), pl.BlockSpec((1, CB_W), lambda j: (0, j))],
        out_specs=[_col(T)], out_shape=[_sds((T, N), F32)], args=[u, w, b], name=name, sem=("parallel",), comm=comm)
    return outs[0] if comm is None else (outs[0], carried)


def _ssm_conv_bwd(u, w, b, dxs, dB, dC, name, comm=None):
    T, N = u.shape
    n_x, n_b = dxs.shape[1] // CB_W, dB.shape[1] // CB_W

    def body(u_ref, w_ref, b_ref, dx_ref, db_ref, dc_ref, du_ref, dw_ref, dbias_ref, scr):
        j = pl.program_id(0)
        taps = [w_ref[k:k + 1, :] for k in range(SSM_K)]
        bias = b_ref[...]
        scr[T:T + CONV_PAD, :] = jnp.zeros((CONV_PAD, CB_W), F32)
        dw8 = [jnp.zeros((8, CB_W), F32)] * SSM_K
        db8 = jnp.zeros((8, CB_W), F32)
        for r0 in range(0, T, CONV_ROWS):
            rows = slice(r0, r0 + CONV_ROWS)
            c, us = _conv_tile(lambda a, b: u_ref[a:b, :], taps, r0)
            _, dsilu = _silu_parts(c + bias)
            d = jnp.where(j < n_x, dx_ref[rows, :], jnp.where(j < n_x + n_b, db_ref[rows, :], dc_ref[rows, :])) * dsilu
            scr[rows, :] = d
            db8 = db8 + _fold8(d)
            dw8 = [acc + _fold8(d * u) for acc, u in zip(dw8, us)]
        dbias_ref[...] = jnp.sum(db8, axis=0, keepdims=True)
        for k in range(SSM_K):
            dw_ref[k:k + 1, :] = jnp.sum(dw8[k], axis=0, keepdims=True)
        for r0 in range(0, T, CONV_ROWS):
            du_ref[r0:r0 + CONV_ROWS, :] = _conv_back_tile(scr, taps, r0).astype(BF16)

    wspec = pl.BlockSpec((SSM_K, CB_W), lambda j: (0, j))
    bspec = pl.BlockSpec((1, CB_W), lambda j: (0, j))
    outs, carried = _call(
        body, grid=(N // CB_W,),
        in_specs=[_col(T), wspec, bspec,
                  pl.BlockSpec((T, CB_W), lambda j: (0, jnp.minimum(j, n_x - 1))),
                  pl.BlockSpec((T, CB_W), lambda j: (0, jnp.clip(j - n_x, 0, n_b - 1))),
                  pl.BlockSpec((T, CB_W), lambda j: (0, jnp.clip(j - n_x - n_b, 0, n_b - 1)))],
        out_specs=[_col(T), wspec, bspec],
        out_shape=[_sds((T, N), BF16), _sds((SSM_K, N), F32), _sds((1, N), F32)],
        scratch=[pltpu.VMEM((T + CONV_PAD, CB_W), F32)],
        args=[u, w, b, dxs, dB, dC], name=name, sem=("parallel",), comm=comm)
    return outs if comm is None else (outs, carried)


def _split3(v):
    hi = v.astype(BF16)
    r = v - hi.astype(F32)
    mid = r.astype(BF16)
    lo = (r - mid.astype(F32)).astype(BF16)
    return hi, mid, lo


def _head_expand(n_lanes):
    h = lax.broadcasted_iota(jnp.int32, (LANES, n_lanes), 0)
    l = lax.broadcasted_iota(jnp.int32, (LANES, n_lanes), 1)
    return (jnp.right_shift(l, HEADDIM.bit_length() - 1) == h).astype(BF16)


def _softplus(v):
    return jnp.maximum(v, 0.0) + jnp.log1p(jnp.exp(-jnp.abs(v)))


PREP_CHUNKS = 4


def _ssd_prep(dt_raw, dt_bias, a_log, n_inner, name):
    T = dt_raw.shape[0]
    rows = PREP_CHUNKS * CHUNK if T % (PREP_CHUNKS * CHUNK) == 0 else CHUNK

    def body(r_ref, b_ref, al_ref, ex_ref, dt_ref, cs_ref):
        i = lax.broadcasted_iota(jnp.int32, (CHUNK, CHUNK), 0)
        j = lax.broadcasted_iota(jnp.int32, (CHUNK, CHUNK), 1)
        tri = (j <= i).astype(BF16)
        ex = ex_ref[...]
        for r0 in range(0, rows, CHUNK):
            dt = _softplus(r_ref[r0:r0 + CHUNK, :] + b_ref[...])
            a = dt * (-jnp.exp(al_ref[...]))
            cs = sum(_dot(tri, p) for p in _split3(a))
            dt_ref[r0:r0 + CHUNK, :] = sum(_dot(p, ex) for p in _split3(dt))
            cs_ref[r0:r0 + CHUNK, :] = sum(_dot(p, ex) for p in _split3(cs))

    blk = pl.BlockSpec((rows, LANES), lambda c: (c, 0))
    out = pl.BlockSpec((rows, n_inner), lambda c: (c, 0))
    ex_spec = pl.BlockSpec((LANES, n_inner), lambda c: (0, 0))
    return pl.pallas_call(body, grid=(T // rows,), in_specs=[blk, _vec(LANES), _vec(LANES), ex_spec], out_specs=[out, out],
                          out_shape=[_sds((T, n_inner), F32)] * 2, name=name,
                          compiler_params=_params("parallel"))(dt_raw, dt_bias, a_log, _head_expand(n_inner))


def _pair_terms(cs_p):
    lane = lax.broadcasted_iota(jnp.int32, (CHUNK, CHUNK), 1)
    sub = lax.broadcasted_iota(jnp.int32, (CHUNK, CHUNK), 0)
    csT = cs_p.T
    Ls = []
    for k in range(2):
        col = jnp.sum(jnp.where(lane == k * HEADDIM, cs_p, 0.0), axis=1, keepdims=True)
        rowv = csT[k * HEADDIM:k * HEADDIM + 1, :]
        Ls.append(jnp.exp(jnp.where(sub >= lane, col - rowv, -jnp.inf)))
    return Ls, jnp.exp(csT[:, CHUNK - 1:CHUNK])


def _block_diag(xp):
    lane = lax.broadcasted_iota(jnp.int32, xp.shape, 1)
    return jnp.concatenate([jnp.where(lane < HEADDIM, xp, 0.0), jnp.where(lane >= HEADDIM, xp, 0.0)], axis=0)


SSD_GROUPS_PER_STEP = 8


def _ssd_specs(T, n_inner):
    nc, gs = T // CHUNK, SSD_GROUPS_PER_STEP
    bo, co = n_inner // (gs * NSTATE), (n_inner + NGROUPS * NSTATE) // (gs * NSTATE)
    assert NGROUPS % gs == 0 and n_inner % (gs * NSTATE) == 0 and (NGROUPS * NSTATE) % (gs * NSTATE) == 0
    g_blk = lambda f: pl.BlockSpec((CHUNK, gs * GROUP_W), lambda c, s: (f(c), s))
    b_blk = lambda f: pl.BlockSpec((CHUNK, gs * NSTATE), lambda c, s: (f(c), bo + s))
    c_blk = lambda f: pl.BlockSpec((CHUNK, gs * NSTATE), lambda c, s: (f(c), co + s))
    return nc, g_blk, b_blk, c_blk


def _ssd_fwd(xbc, dt_e, cs_e, d_e, name, comm=None):
    T = xbc.shape[0]
    n_inner = dt_e.shape[1]
    nc, g_blk, b_blk, c_blk = _ssd_specs(T, n_inner)
    ident = lambda c: c

    gs = SSD_GROUPS_PER_STEP

    def body(xs_ref, b_ref, c_ref, dt_ref, cs_ref, d_ref, y_ref, p_ref, st):
        c, s = pl.program_id(0), pl.program_id(1)

        @pl.when(c == 0)
        def _():
            for gi in range(gs):
                st[s * gs + gi] = jnp.zeros((GROUP_W, NSTATE), F32)

        for gi in range(gs):
            g = s * gs + gi
            gw, gn = slice(gi * GROUP_W, (gi + 1) * GROUP_W), slice(gi * NSTATE, (gi + 1) * NSTATE)
            P = st[g]
            p_ref[0, gi] = P
            xs, dt, cs = xs_ref[:, gw], dt_ref[:, gw], cs_ref[:, gw]
            Bf, Cf = b_ref[:, gn], c_ref[:, gn]
            Cb = Cf.astype(BF16)
            CBm = _dot(Cb, Bf.astype(BF16), NT)
            X = xs * dt
            decay = jnp.exp(cs[CHUNK - 1:CHUNK, :] - cs)
            y_off = _dot(Cb, P.astype(BF16), NT) * jnp.exp(cs)
            ys, ecl = [], []
            for pr in range(2):
                sl = slice(pr * LANES, (pr + 1) * LANES)
                Ls, e_last = _pair_terms(cs[:, sl])
                ecl.append(e_last)
                Mcat = jnp.concatenate([(CBm * L).astype(BF16) for L in Ls], axis=1)
                ys.append(_dot(Mcat, _block_diag(X[:, sl]).astype(BF16)))
            y_ref[:, gw] = jnp.concatenate(ys, axis=1) + y_off + xs * d_ref[:, gw]
            S = _dot3(X * decay, Bf, TN)
            st[g] = P * jnp.concatenate(ecl, axis=0) + S

    p_blk = pl.BlockSpec((1, gs, GROUP_W, NSTATE), lambda c, s: (c, s, 0, 0))
    outs, carried = _call(
        body, grid=(nc, NGROUPS // gs),
        in_specs=[g_blk(ident), b_blk(ident), c_blk(ident), g_blk(ident), g_blk(ident), pl.BlockSpec((1, gs * GROUP_W), lambda c, s: (0, s))],
        out_specs=[g_blk(ident), p_blk],
        out_shape=[_sds((T, n_inner), F32), _sds((nc, NGROUPS, GROUP_W, NSTATE), F32)],
        scratch=[pltpu.VMEM((NGROUPS, GROUP_W, NSTATE), F32)],
        args=[xbc, xbc, xbc, dt_e, cs_e, d_e], name=name, sem=("arbitrary", "arbitrary"), comm=comm)
    return outs if comm is None else (outs, carried)


def _ssd_bwd(xbc, dt_e, cs_e, d_e, states, dy, name, comm=None):
    T = xbc.shape[0]
    n_inner = dt_e.shape[1]
    nc, g_blk, b_blk, c_blk = _ssd_specs(T, n_inner)
    rev = lambda c: nc - 1 - c

    gs = SSD_GROUPS_PER_STEP

    def body(xs_ref, b_ref, c_ref, dt_ref, cs_ref, d_ref, p_ref, pn_ref, dy_ref,
             dxs_ref, db_ref, dc_ref, ddt_ref, dcs_ref, dd_ref, dst):
        cc, s = pl.program_id(0), pl.program_id(1)

        @pl.when(cc == 0)
        def _():
            for gi in range(gs):
                dst[s * gs + gi] = jnp.zeros((GROUP_W, NSTATE), F32)

        for gi in range(gs):
            one_group(s * gs + gi, gi, xs_ref, b_ref, c_ref, dt_ref, cs_ref, d_ref, p_ref, pn_ref, dy_ref,
                      dxs_ref, db_ref, dc_ref, ddt_ref, dcs_ref, dd_ref, dst)

    def one_group(g, gi, xs_ref, b_ref, c_ref, dt_ref, cs_ref, d_ref, p_ref, pn_ref, dy_ref,
                  dxs_ref, db_ref, dc_ref, ddt_ref, dcs_ref, dd_ref, dst):
        gw, gn = slice(gi * GROUP_W, (gi + 1) * GROUP_W), slice(gi * NSTATE, (gi + 1) * NSTATE)
        dS = dst[g]
        P, Pn = p_ref[0, gi], pn_ref[0, gi]
        xs, dt, cs, dY = xs_ref[:, gw], dt_ref[:, gw], cs_ref[:, gw], dy_ref[:, gw]
        Bf, Cf = b_ref[:, gn], c_ref[:, gn]
        Bb, Cb = Bf.astype(BF16), Cf.astype(BF16)
        X = xs * dt
        ecs = jnp.exp(cs)
        decay = jnp.exp(cs[CHUNK - 1:CHUNK, :] - cs)
        CBm = _dot3(Cf, Bf, NT)
        dYe = dY * ecs
        dP_off = _dot3(dYe, Cf, TN)
        dC = _dot(dYe.astype(BF16), P.astype(BF16))
        dcs = dYe * _dot3(Cf, P, NT)
        Xd = X * decay
        dB = _dot(Xd.astype(BF16), dS.astype(BF16))
        E = _dot3(Bf, dS, NT)
        dX = E * decay
        dcs = dcs - E * Xd
        R = _dot3(jnp.ones((8, NSTATE), F32), dS * Pn, NT)
        sub_g = lax.broadcasted_iota(jnp.int32, (CHUNK, GROUP_W), 0)
        dcs = dcs + jnp.where(sub_g == CHUNK - 1, R[0:1, :], 0.0)
        lane = lax.broadcasted_iota(jnp.int32, (CHUNK, CHUNK), 1)
        sub = lax.broadcasted_iota(jnp.int32, (CHUNK, CHUNK), 0)
        dCB = jnp.zeros((CHUNK, CHUNK), F32)
        dXs, dcss, ecl = [], [], []
        for pr in range(2):
            sl = slice(pr * LANES, (pr + 1) * LANES)
            Ls, e_last = _pair_terms(cs[:, sl])
            ecl.append(e_last)
            dYpb = dY[:, sl].astype(BF16)
            dMcat = _dot(dYpb, _block_diag(X[:, sl]).astype(BF16), NT)
            Mcat = jnp.concatenate([(CBm * L).astype(BF16) for L in Ls], axis=1)
            dXt = _dot(Mcat, dYpb, TN)
            dXs.append(jnp.where(lane < HEADDIM, dXt[:CHUNK], dXt[CHUNK:]))
            colacc = jnp.zeros((CHUNK, CHUNK), F32)
            rowacc = jnp.zeros((CHUNK, CHUNK), F32)
            for k in range(2):
                dG = dMcat[:, k * CHUNK:(k + 1) * CHUNK] * Ls[k]
                dCB = dCB + dG
                Q = dG * CBm
                colacc = colacc + jnp.where(lane == k * HEADDIM, jnp.sum(Q, axis=1, keepdims=True), 0.0)
                rowacc = rowacc + jnp.where(sub == k * HEADDIM, jnp.sum(Q, axis=0, keepdims=True), 0.0)
            dcss.append(colacc - rowacc.T)
        dX = dX + jnp.concatenate(dXs, axis=1)
        dcs = dcs + jnp.concatenate(dcss, axis=1)
        dCBb = dCB.astype(BF16)
        dc_ref[:, gn] = dC + _dot(dCBb, Bb)
        db_ref[:, gn] = dB + _dot(dCBb, Cb, TN)
        dxs_ref[:, gw] = dX * dt + dY * d_ref[:, gw]
        ddt_ref[:, gw] = dX * xs
        dcs_ref[:, gw] = dcs
        dd_ref[0, :, gw] = jnp.sum(dY * xs, axis=0, keepdims=True)
        dst[g] = dS * jnp.concatenate(ecl, axis=0) + dP_off

    p_blk = pl.BlockSpec((1, gs, GROUP_W, NSTATE), lambda c, s: (nc - 1 - c, s, 0, 0))
    pn_blk = pl.BlockSpec((1, gs, GROUP_W, NSTATE), lambda c, s: (jnp.minimum(nc - c, nc - 1), s, 0, 0))
    st_blk = pl.BlockSpec((CHUNK, gs * NSTATE), lambda c, s: (nc - 1 - c, s))
    outs, carried = _call(
        body, grid=(nc, NGROUPS // gs),
        in_specs=[g_blk(rev), b_blk(rev), c_blk(rev), g_blk(rev), g_blk(rev), pl.BlockSpec((1, gs * GROUP_W), lambda c, s: (0, s)),
                  p_blk, pn_blk, g_blk(rev)],
        out_specs=[g_blk(rev), st_blk, st_blk, g_blk(rev), g_blk(rev), pl.BlockSpec((1, 1, gs * GROUP_W), lambda c, s: (nc - 1 - c, 0, s))],
        out_shape=[_sds((T, n_inner), F32), _sds((T, NGROUPS * NSTATE), F32), _sds((T, NGROUPS * NSTATE), F32),
                   _sds((T, n_inner), F32), _sds((T, n_inner), F32), _sds((nc, 1, n_inner), F32)],
        scratch=[pltpu.VMEM((NGROUPS, GROUP_W, NSTATE), F32)],
        args=[xbc, xbc, xbc, dt_e, cs_e, d_e, states, states, dy], name=name, sem=("arbitrary", "arbitrary"), comm=comm)
    return outs if comm is None else (outs, carried)


def _ssd_post(ddt_e, dcs_e, dd_p, dt_raw, dt_bias, a_log, n_heads, name):
    T, n_inner = ddt_e.shape

    def body(ddt_ref, dcs_ref, dd_ref, r_ref, b_ref, al_ref, ex_ref, draw_ref, dbias_ref, dal_ref, ddsk_ref):
        @pl.when(pl.program_id(0) == 0)
        def _():
            dbias_ref[...] = jnp.zeros_like(dbias_ref)
            dal_ref[...] = jnp.zeros_like(dal_ref)
            ddsk_ref[...] = jnp.zeros_like(ddsk_ref)

        spread = [ddt_ref[...], dcs_ref[...], jnp.broadcast_to(dd_ref[0], (8, n_inner))]
        stacked = _dot(jnp.concatenate([p for v in spread for p in _split3(v)], axis=0), ex_ref[...], NT)
        sums, r0 = [], 0
        for v in spread:
            n = v.shape[0]
            sums.append(stacked[r0:r0 + n] + stacked[r0 + n:r0 + 2 * n] + stacked[r0 + 2 * n:r0 + 3 * n])
            r0 += 3 * n
        ddt_h, dcs_h, dd_h = sums
        raw = r_ref[...] + b_ref[...]
        dt = _softplus(raw)
        A = -jnp.exp(al_ref[...])
        i = lax.broadcasted_iota(jnp.int32, (CHUNK, CHUNK), 0)
        j = lax.broadcasted_iota(jnp.int32, (CHUNK, CHUNK), 1)
        upper = (j >= i).astype(BF16)
        da = sum(_dot(upper, p) for p in _split3(dcs_h))
        ddt = ddt_h + da * A
        lane = lax.broadcasted_iota(jnp.int32, (CHUNK, LANES), 1)
        draw = jnp.where(lane < n_heads, ddt * jax.nn.sigmoid(raw), 0.0)
        draw_ref[...] = draw.astype(BF16)
        dbias_ref[...] += jnp.sum(draw, axis=0, keepdims=True)
        dal_ref[...] += jnp.sum(da * dt, axis=0, keepdims=True) * A
        ddsk_ref[...] += dd_h[0:1, :]

    wide = pl.BlockSpec((CHUNK, n_inner), lambda c: (c, 0))
    blk = pl.BlockSpec((CHUNK, LANES), lambda c: (c, 0))
    return pl.pallas_call(
        body, grid=(T // CHUNK,),
        in_specs=[wide, wide, pl.BlockSpec((1, 1, n_inner), lambda c: (c, 0, 0)), blk, _vec(LANES), _vec(LANES),
                  pl.BlockSpec((LANES, n_inner), lambda c: (0, 0))],
        out_specs=[blk, _vec(LANES), _vec(LANES), _vec(LANES)],
        out_shape=[_sds((T, LANES), BF16)] + [_sds((1, LANES), F32)] * 3,
        name=name, compiler_params=_params("arbitrary"))(ddt_e, dcs_e, dd_p, dt_raw, dt_bias, a_log, _head_expand(n_inner))


def _row2(v):
    return v.reshape(1, -1).astype(F32)


def _pad_lanes(v):
    return jnp.pad(_row2(v), ((0, 0), (0, LANES - v.shape[-1])))


class _NoExchange:
    def __init__(self, W):
        self.W, self.grads = W, {}

    def weight(self, k):
        return self.W[k]

    def carry(self, name):
        return None

    def carried(self, name, outs):
        pass

    def grad(self, k, g):
        self.grads[k] = g

    def tok(self):
        return jnp.zeros((), F32)

    def point(self, name, value):
        pass


def _local_step(x, tgt, S, small):
    T, D = x.shape

    def mm(a, b, *, name, **kw):
        comm = S.carry(name)
        if comm is None:
            return _mm(a, b, name=name, **kw)
        res, outs = _mm(a, b, name=name, comm=comm, **kw)
        S.carried(name, outs)
        return res

    def carrying(fn, *args, name):
        comm = S.carry(name)
        if comm is None:
            return fn(*args, name)
        res, outs = fn(*args, name, comm=comm)
        S.carried(name, outs)
        return res

    n_inner = 2 * D
    n_heads = n_inner // HEADDIM
    norm_mix, norm_mlp, norm_final = _row2(small["norm_mix"]), _row2(small["norm_mlp"]), _row2(small["norm_final"])
    b_gate, ssm_b, ssm_norm_w = _row2(small["b_gate"]), _row2(small["ssm_conv_b"]), _row2(small["ssm_norm_w"])
    dt_bias, a_log = _pad_lanes(small["dt_bias"]), _pad_lanes(small["A_log"])
    d_e = jnp.repeat(small["D_skip"].astype(F32), HEADDIM).reshape(1, n_inner)

    hb = carrying(_rms_fwd, x, norm_mix, name="rms_mix")
    sc_w, ssm_w = S.weight("sc_conv_w"), S.weight("ssm_conv_w")
    p_xbc = mm(hb, S.weight("xbc"), mode="nn", name="proj_xbc")
    p_dt = mm(hb, S.weight("dt"), mode="nn", name="proj_dt")
    p_z = mm(hb, S.weight("z"), mode="nn", name="proj_z", out_dtypes=(BF16,))
    p_sc = mm(hb, S.weight("sc"), mode="nn", name="proj_sc")
    p_gate = mm(hb, S.weight("gate"), mode="nn", name="proj_gate", out_dtypes=(BF16,))
    xbc = carrying(_ssm_conv_fwd, p_xbc, ssm_w, ssm_b, name="ssm_conv_fwd")
    dt_e, cs_e = _ssd_prep(p_dt, dt_bias, a_log, n_inner, "ssd_prep")
    ya = _sc_fwd(p_sc, sc_w, "sc_fwd")
    y, states = carrying(_ssd_fwd, xbc, dt_e, cs_e, d_e, name="ssd_fwd")
    S.point("mixers_done", [y, ya, p_gate])
    yb = carrying(_gnorm_fwd, y, p_z, ssm_norm_w, name="gnorm_fwd")
    br_a = mm(ya, S.weight("bsc"), mode="nn", name="branch_sc", out_dtypes=(BF16,))
    br_b = mm(yb, S.weight("bssm"), mode="nn", name="branch_ssm", out_dtypes=(BF16,))
    merged = _merge_fwd(p_gate, b_gate, br_a, br_b, "merge_fwd")
    x1 = mm(merged, S.weight("out"), mode="nn", name="out_proj", extras=(x,), epi=_epi_add)
    h2 = _rms_fwd(x1, norm_mlp, "rms_mlp")
    r_act = mm(h2, S.weight("w1"), mode="nn", name="mlp_up", epi=_epi_relu2, out_dtypes=(BF16,))
    x2 = mm(r_act, S.weight("w2"), mode="nn", name="mlp_down", extras=(x1,), epi=_epi_add)
    dx2, dx2b, g_norm_final, loss_row = _final(x2, norm_final, tgt, "final")

    S.grad("w2", mm(r_act, dx2b, mode="tn", name="mlp_down_dw", out_dtypes=(BF16,)))
    da = mm(dx2b, S.weight("w2"), mode="nt", name="mlp_down_dx", extras=(r_act,), epi=_epi_relu2_bwd, out_dtypes=(BF16,))
    S.grad("w1", mm(h2, da, mode="tn", name="mlp_up_dw", out_dtypes=(BF16,)))
    dh2 = mm(da, S.weight("w1"), mode="nt", name="mlp_up_dx")
    dx1, dx1b, g_norm_mlp = _rms_bwd(x1, norm_mlp + S.tok(), dh2, dx2, "rms_mlp_bwd")
    S.grad("out", mm(merged, dx1b, mode="tn", name="out_proj_dw", out_dtypes=(BF16,)))
    dmerged = mm(dx1b, S.weight("out"), mode="nt", name="out_proj_dx", out_dtypes=(BF16,))
    dbr_a, dbr_b, d_gate, g_b_gate = _merge_bwd(dmerged, p_gate, b_gate, br_a, br_b, "merge_bwd")
    S.grad("bssm", mm(yb, dbr_b, mode="tn", name="branch_ssm_dw", out_dtypes=(BF16,)))
    S.grad("bsc", mm(ya, dbr_a, mode="tn", name="branch_sc_dw", out_dtypes=(BF16,)))
    dyb = mm(dbr_b, S.weight("bssm"), mode="nt", name="branch_ssm_dx", out_dtypes=(BF16,))
    dya = mm(dbr_a, S.weight("bsc"), mode="nt", name="branch_sc_dx", out_dtypes=(BF16,))
    dy, d_z, g_ssm_norm_w = _gnorm_bwd(y, p_z, ssm_norm_w + S.tok(), dyb, "gnorm_bwd")
    dxs, dB, dC, ddt_e, dcs_e, dd_p = carrying(_ssd_bwd, xbc, dt_e, cs_e, d_e, states, dy, name="ssd_bwd")
    d_dt, g_dt_bias, g_a_log, g_d_skip = _ssd_post(ddt_e, dcs_e, dd_p, p_dt, dt_bias, a_log, n_heads, "ssd_post")
    d_xbc, g_ssm_w, g_ssm_b = carrying(_ssm_conv_bwd, p_xbc, ssm_w, ssm_b, dxs, dB, dC, name="ssm_conv_bwd")
    d_scB, d_scC, d_scX, g_sc_w = _sc_bwd(p_sc, sc_w, dya, "sc_bwd")
    d_sc = jnp.concatenate([d_scB, d_scC, d_scX], axis=1)
    pieces = [("sc", d_sc), ("z", d_z), ("xbc", d_xbc), ("dt", d_dt), ("gate", d_gate)]
    S.grad("win", {k: mm(hb, d, mode="tn", name="proj_dw_" + k, out_dtypes=(BF16,)) for k, d in pieces})
    pieces = [(k, d + S.tok().astype(d.dtype) if k == "dt" else d) for k, d in pieces]
    dh = mm([d for _, d in pieces], [S.weight(k) for k, _ in pieces], mode="nt", name="proj_dx")
    grad_x, _, g_norm_mix = _rms_bwd(x, norm_mix, dh, dx1, "rms_mix_bwd")

    g_small = dict(norm_mix=g_norm_mix, b_gate=g_b_gate, sc_conv_w=g_sc_w, ssm_conv_w=g_ssm_w, ssm_conv_b=g_ssm_b,
                   dt_bias=g_dt_bias, A_log=g_a_log, D_skip=g_d_skip, ssm_norm_w=g_ssm_norm_w, norm_mlp=g_norm_mlp,
                   norm_final=g_norm_final, loss=loss_row)
    return grad_x, g_small


class _Place:
    def __init__(self, k=0):
        x, y, c = lax.axis_index("x"), lax.axis_index("y"), lax.axis_index("c")
        self.x = 1 - x if k & 4 else x
        self.y = 1 - y if k & 2 else y
        self.c = 1 - c if k & 1 else c
        self.chip = 2 * self.x + self.y
        self.id = 2 * self.chip + self.c


ICI_PEERS = (2, 4, 6)
SIBLING = (1,)
ALL_PEERS = (1, 2, 3, 4, 5, 6, 7)


class _Comm:
    def __init__(self, arrs, out_shape, ks, src, dst, own=None, aliases=None):
        self.arrs, self.out_shape, self.ks = list(arrs), list(out_shape), tuple(ks)
        self.n = len(self.arrs)
        self.src, self.dst, self.own = src, dst, own
        self.aliases = aliases or {}
        dma = pltpu.SemaphoreType.DMA
        self.scratch = [dma((self.n, len(self.ks))), dma((self.n, len(self.ks))), dma((self.n,))]

    def _copies(self, ins, outs, sems, with_recvs):
        send_sems, recv_sems, local_sems = sems
        me = _Place()
        owns, sends, recvs = [], [], []
        for a in range(self.n):
            if self.own is not None:
                s, d = self.own(a, ins[a], outs[a], me)
                owns.append(pltpu.make_async_copy(s, d, local_sems.at[a]))
            for i, k in enumerate(self.ks):
                peer = _Place(k)
                for sender, lst in ((me, sends), (peer, recvs)) if with_recvs else ((me, sends),):
                    lst.append(pltpu.make_async_remote_copy(
                        src_ref=self.src(a, ins[a], me, peer), dst_ref=self.dst(a, outs[a], sender),
                        send_sem=send_sems.at[a, i], recv_sem=recv_sems.at[a, i],
                        device_id=(peer.x, peer.y, peer.c), device_id_type=MESH))
        return owns, sends, recvs

    def start(self, ins, outs, sems):
        owns, sends, _ = self._copies(ins, outs, sems, False)
        for cp in owns + sends:
            cp.start()

    def finish(self, ins, outs, sems):
        owns, sends, recvs = self._copies(ins, outs, sems, True)
        for cp in recvs:
            cp.wait_recv()
        for cp in sends:
            cp.wait_send()
        for cp in owns:
            cp.wait()


class _GatherBoth:
    def __init__(self, shards):
        self.arrs, self.n, self.aliases = list(shards), len(shards), {}
        self.out_shape = [_sds((4, 2) + s.shape, s.dtype) for s in shards]
        dma = pltpu.SemaphoreType.DMA
        self.scratch = [dma((self.n, 7)), dma((self.n, 7)), dma((self.n,))]

    def _copy(self, a, j, src, slot, to, outs, sems):
        return pltpu.make_async_remote_copy(src_ref=src, dst_ref=outs[a].at[slot.chip, slot.c], send_sem=sems[0].at[a, j],
                                            recv_sem=sems[1].at[a, j], device_id=(to.x, to.y, to.c), device_id_type=MESH)

    def start(self, ins, outs, sems):
        me, sib = _Place(), _Place(1)
        for a in range(self.n):
            pltpu.make_async_copy(ins[a], outs[a].at[me.chip, me.c], sems[2].at[a]).start()
            self._copy(a, 0, ins[a], me, sib, outs, sems).start()
            for i, k in enumerate(ICI_PEERS):
                self._copy(a, 1 + i, ins[a], me, _Place(k), outs, sems).start()

    def finish(self, ins, outs, sems):
        me, sib = _Place(), _Place(1)
        passed = []
        for i, k in enumerate(ICI_PEERS):
            peer = _Place(k)
            for a in range(self.n):
                self._copy(a, 1 + i, ins[a], peer, peer, outs, sems).wait_recv()
                cp = self._copy(a, 4 + i, outs[a].at[peer.chip, peer.c], peer, sib, outs, sems)
                cp.start()
                passed.append(cp)
        for a in range(self.n):
            self._copy(a, 0, ins[a], sib, sib, outs, sems).wait_recv()
            for i, k in enumerate(ICI_PEERS):
                far = _Place(k | 1)
                self._copy(a, 4 + i, outs[a].at[far.chip, far.c], far, sib, outs, sems).wait_recv()
        for a in range(self.n):
            self._copy(a, 0, ins[a], me, sib, outs, sems).wait_send()
            for i, k in enumerate(ICI_PEERS):
                self._copy(a, 1 + i, ins[a], me, _Place(k), outs, sems).wait_send()
            pltpu.make_async_copy(ins[a], outs[a].at[me.chip, me.c], sems[2].at[a]).wait()
        for cp in passed:
            cp.wait_send()


def _run_comm(comm, name, after=()):
    n, n_after = comm.n, len(after)

    def body(*refs):
        ins, outs, sems = refs[:n], refs[n + n_after:2 * n + n_after], refs[2 * n + n_after:]
        comm.start(ins, outs, sems)
        comm.finish(ins, outs, sems)

    return list(pl.pallas_call(body, in_specs=[ANY] * (n + n_after), out_specs=[ANY] * n, out_shape=comm.out_shape,
                               scratch_shapes=comm.scratch, input_output_aliases=dict(comm.aliases), name=name)(*comm.arrs, *after))


def _gather_sibling(bufs):
    return _Comm(bufs, [_sds(b.shape, b.dtype) for b in bufs], SIBLING,
                 src=lambda a, i, me, p: i.at[:, me.c], dst=lambda a, o, s: o.at[:, s.c], aliases={a: a for a in range(len(bufs))})


def _scatter_sibling(parts):
    return _Comm(parts, [_sds((4,) + p.shape[2:], p.dtype) for p in parts], SIBLING,
                 src=lambda a, i, me, p: i.at[:, p.c], dst=lambda a, o, s: o)


HBM_SPEC = pl.BlockSpec(memory_space=pltpu.HBM)
SEM_SPEC = pl.BlockSpec(memory_space=pltpu.SEMAPHORE)
DATAFLOW = pltpu.SideEffectType.DATAFLOW_SIDE_EFFECTING


def _tiles_2d(R, C, max_rows=256):
    if R % max_rows == 0:
        return max_rows, C, R // max_rows, lambda i: (i, 0)
    if R <= 2 * max_rows or C % 256:
        return R, C, 1, lambda i: (0, 0)
    return R, 256, C // 256, lambda i: (0, i)


def _ici_copy(gather, a, srcs, lands, send_sems, recv_sems, i, me, peer, sender):
    src = lands[a].at[me.chip, me.c] if gather else srcs[a].at[peer.chip]
    dst = lands[a].at[sender.chip, sender.c] if gather else lands[a].at[sender.chip]
    j = a * len(ICI_PEERS) + i
    return pltpu.make_async_remote_copy(src_ref=src, dst_ref=dst, send_sem=send_sems.at[j], recv_sem=recv_sems.at[j],
                                        device_id=(peer.x, peer.y, peer.c), device_id_type=MESH)


def _ici_start(srcs, lands, gather, name):
    n, n_s = len(lands), len(srcs)
    bufs = list(srcs) + list(lands)

    def body(*refs):
        src_refs, land_refs = refs[:n_s], refs[n_s:n_s + n]
        send_sems, recv_sems = refs[n_s + n], refs[n_s + n + 1]
        token = refs[-1]
        me = _Place()
        for a in range(n):
            for i, k in enumerate(ICI_PEERS):
                _ici_copy(gather, a, src_refs, land_refs, send_sems, recv_sems, i, me, _Place(k), me).start()
        token[...] = jnp.zeros_like(token)

    dma = pltpu.SemaphoreType.DMA((n * len(ICI_PEERS),))
    outs = pl.pallas_call(
        body, name=name, out_shape=(dma, dma, *[pltpu.HBM(v.shape, v.dtype) for v in bufs], _sds((8, LANES), F32)),
        in_specs=(HBM_SPEC,) * len(bufs),
        out_specs=(SEM_SPEC, SEM_SPEC) + (HBM_SPEC,) * len(bufs) + (pl.BlockSpec(memory_space=pltpu.VMEM),),
        input_output_aliases={j: 2 + j for j in range(len(bufs))}, compiler_params=pltpu.CompilerParams(has_side_effects=DATAFLOW),
    )(*[pltpu.with_memory_space_constraint(v, pltpu.HBM) for v in bufs])
    return outs[0], outs[1], list(outs[2:2 + n_s]), list(outs[2 + n_s:2 + n_s + n]), outs[-1]


def _ici_wait(flight, after, gather, name):
    send_sems, recv_sems, srcs, lands, _ = flight
    n, n_s = len(lands), len(srcs)
    bufs = srcs + lands

    def body(*refs):
        src_refs, land_refs = refs[:n_s], refs[n_s:n_s + n]
        s_sems, r_sems = refs[n_s + n], refs[n_s + n + 1]
        me = _Place()
        for a in range(n):
            for i, k in enumerate(ICI_PEERS):
                peer = _Place(k)
                cp = _ici_copy(gather, a, src_refs, land_refs, s_sems, r_sems, i, me, peer, peer)
                cp.wait_send()
                cp.wait_recv()

    outs = pl.pallas_call(
        body, name=name, out_shape=tuple(pltpu.HBM(v.shape, v.dtype) for v in bufs),
        in_specs=(HBM_SPEC,) * len(bufs) + (SEM_SPEC, SEM_SPEC) + (ANY,) * len(after), out_specs=(HBM_SPEC,) * len(bufs),
        input_output_aliases={j: j for j in range(len(bufs))}, compiler_params=pltpu.CompilerParams(has_side_effects=DATAFLOW),
    )(*bufs, send_sems, recv_sems, *after)
    return list(outs[n_s:])


def _own_shards(shards, after, name):
    n = len(shards)
    vmem = pl.BlockSpec(memory_space=pltpu.VMEM)

    def body(*refs):
        ins, outs, cast, sems = refs[:n], refs[n + 1:2 * n + 1], refs[2 * n + 1:3 * n + 1], refs[3 * n + 1]
        me = _Place()
        copies = []
        for a in range(n):
            cast[a][...] = ins[a][...].astype(BF16)
            copies.append(pltpu.make_async_copy(cast[a], outs[a].at[me.chip, me.c], sems.at[a]))
            copies[-1].start()
        for cp in copies:
            cp.wait()

    return list(pl.pallas_call(
        body, in_specs=[vmem] * n + [ANY], out_specs=[ANY] * n, out_shape=[_sds((4, 2) + s.shape, BF16) for s in shards],
        scratch_shapes=[pltpu.VMEM(s.shape, BF16) for s in shards] + [pltpu.SemaphoreType.DMA((n,))], name=name)(*shards, after))


def _col_pieces(widths):
    out, c = [], 0
    for k, w in widths:
        out.append((k, c, w))
        c += w
    return out


def _split_range(c0, n, bounds):
    parts, c = [], c0
    while c < c0 + n:
        r = max(i for i in range(len(bounds) - 1) if bounds[i] <= c)
        w = min(c0 + n, bounds[r + 1]) - c
        parts.append((r, c - bounds[r], w))
        c += w
    return parts


def _win_unpack(g, widths, name):
    n, R, C = g.shape
    tr = min(256, R)
    pieces = _col_pieces(widths)
    padded = [-(-w // LANES) * LANES for _, _, w in pieces]
    shard_bounds = [s * C for s in range(n + 1)]

    def body(g_ref, *o_refs):
        for (k, c0, w), o_ref in zip(pieces, o_refs):
            for t in range(0, o_ref.shape[1], LANES):
                valid = max(0, min(LANES, w - t))
                cols = [g_ref[s, :, o:o + ww] for s, o, ww in _split_range(c0 + t, valid, shard_bounds)] if valid else []
                if valid < LANES:
                    cols.append(jnp.zeros((tr, LANES - valid), g_ref.dtype))
                o_ref[:, t:t + LANES] = cols[0] if len(cols) == 1 else jnp.concatenate(cols, axis=1)

    return pl.pallas_call(
        body, grid=(R // tr,), in_specs=[pl.BlockSpec((n, tr, C), lambda i: (0, i, 0))],
        out_specs=[pl.BlockSpec((tr, p), lambda i: (i, 0)) for p in padded],
        out_shape=[_sds((R, p), g.dtype) for p in padded], name=name, compiler_params=_params("parallel"))(g)


def _win_pack(grads, widths, n, name):
    R = grads[0].shape[0]
    tr = min(256, R)
    pieces = _col_pieces(widths)
    total = pieces[-1][1] + pieces[-1][2]
    C = total // n
    bounds = [c0 for _, c0, _ in pieces] + [total]

    def body(*refs):
        g_refs, o_ref = refs[:-1], refs[-1]

        def tile_t(c0):
            cols = [g_refs[r][:, o:o + ww] for r, o, ww in _split_range(c0, LANES, bounds)]
            tile = cols[0] if len(cols) == 1 else jnp.concatenate(cols, axis=1)
            return tile.astype(F32).T

        for s in range(n):
            full = C // LANES * LANES
            for t in range(0, full, LANES):
                o_ref[s, t:t + LANES, :] = tile_t(s * C + t).astype(o_ref.dtype)
            if full < C:
                o_ref[s, full:C, :] = tile_t(s * C + C - LANES)[LANES - (C - full):, :].astype(o_ref.dtype)

    return pl.pallas_call(
        body, grid=(R // tr,), in_specs=[pl.BlockSpec((tr, gr.shape[1]), lambda i: (i, 0)) for gr in grads],
        out_specs=pl.BlockSpec((n, C, tr), lambda i: (0, 0, i)), out_shape=_sds((n, C, R), grads[0].dtype),
        name=name, compiler_params=_params("parallel"))(*grads)


def _gather_all(arrs):
    return _Comm(arrs, [_sds((N_DEV,) + a.shape, a.dtype) for a in arrs], ALL_PEERS,
                 src=lambda a, i, me, p: i, dst=lambda a, o, s: o.at[s.id], own=lambda a, i, o, me: (i, o.at[me.id]))


def _add_halves(parts, got, name):
    n, _, R, C = parts.shape
    br, bc, nb, at = _tiles_2d(R, C, max_rows=1024)
    place = jnp.stack([lax.axis_index("c"), 2 * lax.axis_index("x") + lax.axis_index("y")]).astype(jnp.int32)

    def body(q_ref, p_ref, g_ref, o_ref, land_ref):
        s = (p_ref[0, 0].astype(F32) + g_ref[0].astype(F32)).astype(o_ref.dtype)
        o_ref[0] = s

        @pl.when(pl.program_id(1) == q_ref[1])
        def _():
            land_ref[0] = s

    spec = pltpu.PrefetchScalarGridSpec(
        num_scalar_prefetch=1, grid=(nb, n),
        in_specs=[pl.BlockSpec((1, 1, br, bc), lambda i, q, q_ref: (q, q_ref[0]) + at(i)), pl.BlockSpec((1, br, bc), lambda i, q, q_ref: (q,) + at(i))],
        out_specs=[pl.BlockSpec((1, br, bc), lambda i, q, q_ref: (q,) + at(i)), pl.BlockSpec((1, br, bc), lambda i, q, q_ref: (q_ref[1],) + at(i))])
    return pl.pallas_call(body, grid_spec=spec, out_shape=[_sds((n, R, C), parts.dtype)] * 2, name=name,
                          compiler_params=_params("parallel", "arbitrary"))(place, parts, got)


def _adam(w, m, v, gparts, name, comm=None):
    R, C = w.shape
    n = gparts.shape[0]
    br, bc, nb, at = _tiles_2d(R, C, max_rows=512)
    c1 = 1.0 / (1.0 - ADAM_B1 ** ADAM_STEP)
    c2 = 1.0 / (1.0 - ADAM_B2 ** ADAM_STEP)

    def body(w_ref, m_ref, v_ref, g_ref, go_ref, d_ref, mo_ref, vo_ref):
        g = g_ref[0].astype(F32)
        for s in range(1, n):
            g = g + g_ref[s].astype(F32)
        mn = ADAM_B1 * m_ref[...] + (1.0 - ADAM_B1) * g
        vn = ADAM_B2 * v_ref[...] + (1.0 - ADAM_B2) * (g * g)
        go_ref[...] = g
        mo_ref[...] = mn
        vo_ref[...] = vn
        d_ref[...] = -ADAM_LR * ((mn * c1) / (jnp.sqrt(vn * c2) + ADAM_EPS) + ADAM_WD * w_ref[...])

    blk = pl.BlockSpec((br, bc), at)
    outs, carried = _call(
        body, grid=(nb,), in_specs=[blk, blk, blk, pl.BlockSpec((n, br, bc), lambda i: (0,) + at(i))],
        out_specs=[blk] * 4, out_shape=[_sds((R, C), F32)] * 4, args=[w, m, v, gparts], name=name, sem=("parallel",), comm=comm)
    return outs if comm is None else (outs, carried)


_SMALL_ORDER = ("norm_mix", "b_gate", "sc_conv_w", "ssm_conv_w", "ssm_conv_b", "dt_bias", "A_log", "D_skip", "ssm_norm_w",
                "norm_mlp", "norm_final", "loss")
_REPLICATED = ("norm_mix", "b_gate", "ssm_conv_b", "dt_bias", "A_log", "D_skip", "ssm_norm_w", "norm_mlp", "norm_final")


def _cols_to_slots(g, n):
    R = g.shape[0]
    return jnp.transpose(g.reshape(R, n, g.shape[1] // n), (1, 0, 2))


def _slots_to_cols(g):
    n, R, C = g.shape
    return jnp.transpose(g, (1, 0, 2)).reshape(R, n * C)


def kernel(x, norm_mix, w_in, b_gate, sc_conv_w, ssm_conv_w, ssm_conv_b, dt_bias, A_log, D_skip, ssm_norm_w, w_branch_sc, w_branch_ssm, w_out, norm_mlp, w_mlp1, w_mlp2, norm_final, loss_target, m_norm_mix, m_w_in, m_b_gate, m_sc_conv_w, m_ssm_conv_w, m_ssm_conv_b, m_dt_bias, m_A_log, m_D_skip, m_ssm_norm_w, m_w_branch_sc, m_w_branch_ssm, m_w_out, m_norm_mlp, m_w_mlp1, m_w_mlp2, m_norm_final, v_norm_mix, v_w_in, v_b_gate, v_sc_conv_w, v_ssm_conv_w, v_ssm_conv_b, v_dt_bias, v_A_log, v_D_skip, v_ssm_norm_w, v_w_branch_sc, v_w_branch_ssm, v_w_out, v_norm_mlp, v_w_mlp1, v_w_mlp2, v_norm_final):
    T, D = x.shape[1], x.shape[2]
    n_inner = 2 * D
    n_heads = n_inner // HEADDIM
    n_xbc = n_inner + 2 * NGROUPS * NSTATE
    me = 4 * lax.axis_index("x") + 2 * lax.axis_index("y") + lax.axis_index("c")

    in_cols = [("sc", 3 * D), ("z", n_inner), ("xbc", n_xbc), ("dt", n_heads), ("gate", 2 * D)]
    by_owner = lambda b: b.reshape((N_DEV,) + b.shape[2:])
    to_owner = lambda g: g.reshape((4, 2) + g.shape[1:])
    rows_of = lambda g: to_owner(g.reshape((N_DEV, g.shape[0] // N_DEV) + g.shape[1:]))
    cols_of = lambda g: to_owner(_cols_to_slots(g, N_DEV))

    class Schedule(_NoExchange):
        late = ("bssm", "bsc", "out", "w1", "w2")
        gather_sib = dict(gnorm_fwd=("bsc", "bssm", "out"), branch_ssm=("w1", "w2"))
        scatter_sib = dict(mlp_up_dx=("w2", "w1"), branch_ssm_dx=("out", "bssm", "bsc"))
        shards = dict(bsc=w_branch_sc, bssm=w_branch_ssm, out=w_out, w1=w_mlp1, w2=w_mlp2)

        def __init__(self):
            self.W, self.staged, self.grads, self.summed, self.scatters = {}, {}, {}, {}, []
            self.token = jnp.zeros((), F32)

        def first_weights(self, bufs):
            self.W.update(zip([k for k, _ in in_cols], _win_unpack(by_owner(bufs[0]), in_cols, "win_unpack")))
            self.W.update(sc_conv_w=_slots_to_cols(by_owner(bufs[1])), ssm_conv_w=_slots_to_cols(by_owner(bufs[2])))
            lands = _own_shards([self.shards[k] for k in self.late], bufs[1], "own_shards")
            self.gather_flight = _ici_start([], lands, True, "gather_late_start")
            self.token = self.gather_flight[4][0, 0]
            self.W["dt"] = self.W["dt"] + self.token.astype(BF16)

        def tok(self):
            return self.token

        def point(self, name, values):
            if name == "mixers_done":
                lands = _ici_wait(self.gather_flight, values, True, "gather_late_wait")
                self.staged.update(zip(self.late, lands))

        def carry(self, name):
            if name == "rms_mix":
                return _GatherBoth([w_in.astype(BF16), sc_conv_w, ssm_conv_w])
            if name in self.gather_sib:
                return _gather_sibling([self.staged.pop(k) for k in self.gather_sib[name]])
            if name in self.scatter_sib:
                return _scatter_sibling([self.grads[k] for k in self.scatter_sib[name]])
            return None

        def start_scatter(self, keys, halves_and_lands):
            halves, lands = [h for h, _ in halves_and_lands], [l for _, l in halves_and_lands]
            flight = _ici_start(halves, lands, False, "scatter_%s_start" % keys[0])
            self.scatters.append((keys, flight))
            self.token = flight[4][0, 0]

        def carried(self, name, outs):
            if name == "rms_mix":
                self.first_weights(outs)
            elif name in self.gather_sib:
                for k, b in zip(self.gather_sib[name], outs):
                    full = by_owner(b)
                    self.W[k] = _slots_to_cols(full) if k == "w1" else full.reshape(-1, D)
            else:
                keys = self.scatter_sib[name]
                self.start_scatter(keys, [_add_halves(self.grads[k], b, "add_halves_" + k) for k, b in zip(keys, outs)])

        def grad(self, k, g):
            if k == "win":
                g = to_owner(_win_pack([g[k] for k, _ in in_cols], in_cols, N_DEV, "win_pack"))
                got = _run_comm(_scatter_sibling([g]), "scatter_sibling_win")[0]
                self.start_scatter(("win",), [_add_halves(g, got, "add_halves_win")])
            else:
                self.grads[k] = cols_of(g) if k == "w1" else rows_of(g)

        def finish_scatter(self, after):
            keys, flight = self.scatters.pop(0)
            return dict(zip(keys, _ici_wait(flight, after, False, "scatter_%s_wait" % keys[0])))

    S = Schedule()
    small = dict(norm_mix=norm_mix, b_gate=b_gate, ssm_conv_b=ssm_conv_b, dt_bias=dt_bias, A_log=A_log, D_skip=D_skip,
                 ssm_norm_w=ssm_norm_w, norm_mlp=norm_mlp, norm_final=norm_final)
    grad_x, g_small = _local_step(x.reshape(T, D), loss_target.reshape(T, D), S, small)

    small_flat = jnp.concatenate([g_small[k].reshape(-1) for k in _SMALL_ORDER])
    n_small = small_flat.shape[0]
    rows = -(-n_small // (8 * LANES)) * 8
    small_pack = jnp.pad(small_flat, (0, rows * LANES - n_small)).reshape(rows, LANES)

    res = {}
    big = [("w_in", "win", w_in, m_w_in, v_w_in), ("w_branch_sc", "bsc", w_branch_sc, m_w_branch_sc, v_w_branch_sc),
           ("w_branch_ssm", "bssm", w_branch_ssm, m_w_branch_ssm, v_w_branch_ssm), ("w_out", "out", w_out, m_w_out, v_w_out),
           ("w_mlp1", "w1", w_mlp1, m_w_mlp1, v_w_mlp1), ("w_mlp2", "w2", w_mlp2, m_w_mlp2, v_w_mlp2)]
    by_grad = {gk: (k, w, m, v) for k, gk, w, m, v in big}
    after = [grad_x]
    while S.scatters:
        for gk, parts in S.finish_scatter(after).items():
            k, w, m, v = by_grad[gk]
            if gk == "win":
                res_t, (small_parts,) = _adam(w.T, m.T, v.T, parts, "adam_" + k, comm=_gather_all([small_pack]))
                res[k] = [r.T for r in res_t]
            else:
                res[k] = _adam(w, m, v, parts, "adam_" + k)
            after = after + [res[k][1]]

    sizes = {k: g_small[k].size for k in _SMALL_ORDER}
    offs, o = {}, 0
    for k in _SMALL_ORDER:
        offs[k] = o
        o += sizes[k]
    rep_w = dict(norm_mix=norm_mix, b_gate=b_gate, ssm_conv_b=ssm_conv_b, dt_bias=dt_bias, A_log=A_log, D_skip=D_skip,
                 ssm_norm_w=ssm_norm_w, norm_mlp=norm_mlp, norm_final=norm_final)
    rep_m = dict(norm_mix=m_norm_mix, b_gate=m_b_gate, ssm_conv_b=m_ssm_conv_b, dt_bias=m_dt_bias, A_log=m_A_log, D_skip=m_D_skip,
                 ssm_norm_w=m_ssm_norm_w, norm_mlp=m_norm_mlp, norm_final=m_norm_final)
    rep_v = dict(norm_mix=v_norm_mix, b_gate=v_b_gate, ssm_conv_b=v_ssm_conv_b, dt_bias=v_dt_bias, A_log=v_A_log, D_skip=v_D_skip,
                 ssm_norm_w=v_ssm_norm_w, norm_mlp=v_norm_mlp, norm_final=v_norm_final)

    def pack(d):
        segs = [jnp.pad(d[k].astype(F32).reshape(-1), (0, sizes[k] - d[k].size)) if k in d else jnp.zeros((sizes[k],), F32)
                for k in _SMALL_ORDER]
        return jnp.pad(jnp.concatenate(segs), (0, rows * LANES - n_small)).reshape(rows, LANES)

    sm = _adam(pack(rep_w), pack(rep_m), pack(rep_v), small_parts, "adam_small")
    sm = [s.reshape(-1) for s in sm]
    for k in _REPLICATED:
        n_k = rep_w[k].shape[0]
        res[k] = tuple(s[offs[k]:offs[k] + n_k] for s in sm)
    loss = sm[0][offs["loss"]]
    for k, w, m, v, K, full in (("sc_conv_w", sc_conv_w, m_sc_conv_w, v_sc_conv_w, SC_K, D),
                                ("ssm_conv_w", ssm_conv_w, m_ssm_conv_w, v_ssm_conv_w, SSM_K, n_xbc)):
        g_full = sm[0][offs[k]:offs[k] + K * full].reshape(K, full)
        cw = full // N_DEV
        g_mine = lax.dynamic_slice_in_dim(g_full, me * cw, cw, axis=1)
        res[k] = _adam(w, m, v, g_mine[None], "adam_" + k)

    order = ("norm_mix", "w_in", "b_gate", "sc_conv_w", "ssm_conv_w", "ssm_conv_b", "dt_bias", "A_log", "D_skip", "ssm_norm_w",
             "w_branch_sc", "w_branch_ssm", "w_out", "norm_mlp", "w_mlp1", "w_mlp2", "norm_final")
    outs = [loss, grad_x.reshape(1, T, D)]
    for j in range(4):
        outs += [res[k][j] for k in order]
    return tuple(outs)
```
